```python
import jax, jax.numpy as jnp
from jax import lax
import numpy as np

D_MODEL = 1024
BATCH = 8
SEQ = 4096
DEPTH = 4

CHUNK = 64
N_PREV_CHUNKS = 8
BAND = N_PREV_CHUNKS + 1

D_MIX = D_MODEL
HEAD_DIM = 64
ATTN_WIDTH = D_MIX // 2
N_ATTN_HEADS = ATTN_WIDTH // HEAD_DIM
CONV_WIDTH = D_MIX // 4
CONV_K = 3
POOL_WIDTH = D_MIX - ATTN_WIDTH - CONV_WIDTH
POOL_WINDOWS = (2, 4, 8, 16)
N_POOL_GROUPS = len(POOL_WINDOWS)
POOL_GROUP = POOL_WIDTH // N_POOL_GROUPS
REL_CLIP = 128

D_IN = 3 * ATTN_WIDTH + 3 * CONV_WIDTH + POOL_WIDTH
D_FF = 4 * D_MODEL
EPS = 1e-6
NEG_INF = -1e30

kernel_name = "hybrid_chunked_attn_conv_pool_trunk"


def rms_norm(x, g):
    x32 = x.astype(jnp.float32)
    y = x32 * lax.rsqrt(jnp.mean(x32 * x32, axis=-1, keepdims=True) + EPS)
    return (y * g.astype(jnp.float32)).astype(x.dtype)


def chunked_band_attention(q, k, v, rel_bias):
    b, s, h, d = q.shape
    nc = s // CHUNK
    qc = q.reshape(b, nc, CHUNK, h, d)
    pad = ((0, 0), (N_PREV_CHUNKS, 0), (0, 0), (0, 0), (0, 0))
    kp = jnp.pad(k.reshape(b, nc, CHUNK, h, d), pad)
    vp = jnp.pad(v.reshape(b, nc, CHUNK, h, d), pad)
    band_idx = jnp.arange(nc)[:, None] + jnp.arange(BAND)[None, :]
    kb = kp[:, band_idx].reshape(b, nc, BAND * CHUNK, h, d)
    vb = vp[:, band_idx].reshape(b, nc, BAND * CHUNK, h, d)
    scores = jnp.einsum('bnqhd,bnkhd->bnhqk', qc, kb).astype(jnp.float32) * (d ** -0.5)
    qi = jnp.arange(CHUNK)[:, None]
    kj = jnp.arange(BAND * CHUNK)[None, :]
    dist = qi + N_PREV_CHUNKS * CHUNK - kj
    bias = rel_bias[:, jnp.clip(dist, -REL_CLIP, REL_CLIP) + REL_CLIP].astype(jnp.float32)
    valid = jnp.repeat(band_idx >= N_PREV_CHUNKS, CHUNK, axis=1)
    scores = jnp.where(valid[None, :, None, None, :], scores + bias[None, None], NEG_INF)
    p = jax.nn.softmax(scores, axis=-1).astype(v.dtype)
    out = jnp.einsum('bnhqk,bnkhd->bnqhd', p, vb)
    return out.reshape(b, s, h * d)


def gated_short_conv(gb, gc, hin, conv_w):
    z = gc * hin
    s = z.shape[1]
    zp = jnp.pad(z, ((0, 0), (CONV_K - 1, 0), (0, 0)))
    y = sum(conv_w[i] * zp[:, i:i + s] for i in range(CONV_K))
    return gb * y


def multiscale_pool(u, pool_w, pool_scale):
    b, s, c = u.shape
    u32 = u.astype(jnp.float32)
    cs = jnp.concatenate([jnp.zeros((b, 1, c), jnp.float32), jnp.cumsum(u32, axis=1)], axis=1)
    pos1 = jnp.arange(s) + 1
    outs = []
    for g, w in enumerate(POOL_WINDOWS):
        sl = slice(g * POOL_GROUP, (g + 1) * POOL_GROUP)
        csg = cs[:, :, sl]
        lag = jnp.pad(csg[:, :s + 1 - w], ((0, 0), (w - 1, 0), (0, 0)))
        cnt = jnp.minimum(pos1, w).astype(jnp.float32)[None, :, None]
        m = (csg[:, 1:] - lag) / cnt - u32[:, :, sl]
        outs.append(jnp.einsum('bsc,cd->bsd', m.astype(u.dtype), pool_w[g]))
    return jnp.concatenate(outs, axis=-1) * pool_scale


def _fwd_setup_inputs(seed: int = 0) -> dict:
    key = jax.random.key(seed)
    ks = jax.random.split(key, 13)
    f32 = jnp.float32
    L = DEPTH
    x = jax.random.normal(ks[0], (BATCH, SEQ, D_MODEL), f32)
    norm1_g = 1.0 + 0.05 * jax.random.normal(ks[1], (L, D_MODEL), f32)
    w_in = jax.random.normal(ks[2], (L, D_MODEL, D_IN), f32) * D_MODEL ** -0.5
    q_norm_g = 1.0 + 0.05 * jax.random.normal(ks[3], (L, HEAD_DIM), f32)
    k_norm_g = 1.0 + 0.05 * jax.random.normal(ks[4], (L, HEAD_DIM), f32)
    rel_bias = 0.1 * jax.random.normal(ks[5], (L, N_ATTN_HEADS, 2 * REL_CLIP + 1), f32)
    conv_w = jax.random.normal(ks[6], (L, CONV_K, CONV_WIDTH), f32) * CONV_K ** -0.5
    pool_w = jax.random.normal(ks[7], (L, N_POOL_GROUPS, POOL_GROUP, POOL_GROUP), f32) * POOL_GROUP ** -0.5
    pool_scale = 0.5 + 0.1 * jax.random.normal(ks[8], (L, POOL_WIDTH), f32)
    w_out = jax.random.normal(ks[9], (L, D_MIX, D_MODEL), f32) * D_MIX ** -0.5
    norm2_g = 1.0 + 0.05 * jax.random.normal(ks[10], (L, D_MODEL), f32)
    w_mlp1 = jax.random.normal(ks[11], (L, D_MODEL, D_FF), f32) * D_MODEL ** -0.5
    w_mlp2 = jax.random.normal(ks[12], (L, D_FF, D_MODEL), f32) * D_FF ** -0.5
    return {"x": x, "norm1_g": norm1_g, "w_in": w_in, "q_norm_g": q_norm_g,
            "k_norm_g": k_norm_g, "rel_bias": rel_bias, "conv_w": conv_w,
            "pool_w": pool_w, "pool_scale": pool_scale, "w_out": w_out,
            "norm2_g": norm2_g, "w_mlp1": w_mlp1, "w_mlp2": w_mlp2}


def _fwd_reference(x, norm1_g, w_in, q_norm_g, k_norm_g, rel_bias, conv_w, pool_w,
              pool_scale, w_out, norm2_g, w_mlp1, w_mlp2):
    b, s, _ = x.shape
    a = ATTN_WIDTH
    c = CONV_WIDTH
    for l in range(DEPTH):
        h = rms_norm(x, norm1_g[l])
        p = jnp.einsum('bsd,de->bse', h, w_in[l])
        q = p[..., 0:a].reshape(b, s, N_ATTN_HEADS, HEAD_DIM)
        k = p[..., a:2 * a].reshape(b, s, N_ATTN_HEADS, HEAD_DIM)
        v = p[..., 2 * a:3 * a].reshape(b, s, N_ATTN_HEADS, HEAD_DIM)
        o = 3 * a
        gb = p[..., o:o + c]
        gc = p[..., o + c:o + 2 * c]
        hin = p[..., o + 2 * c:o + 3 * c]
        u = p[..., o + 3 * c:]
        q = rms_norm(q, q_norm_g[l])
        k = rms_norm(k, k_norm_g[l])
        y_attn = chunked_band_attention(q, k, v, rel_bias[l])
        y_conv = gated_short_conv(gb, gc, hin, conv_w[l])
        y_pool = multiscale_pool(u, pool_w[l], pool_scale[l])
        mix = jnp.concatenate([y_attn, y_conv, y_pool], axis=-1)
        x = x + jnp.einsum('bse,ed->bsd', mix, w_out[l])
        h2 = rms_norm(x, norm2_g[l])
        f = jnp.square(jax.nn.relu(jnp.einsum('bsd,df->bsf', h2, w_mlp1[l])))
        x = x + jnp.einsum('bsf,fd->bsd', f, w_mlp2[l])
    return x


import jax as _jax
import jax.numpy as _jnp

TWIN_FORMAT = 'train_step'
FWD_PARAMS = ['x', 'norm1_g', 'w_in', 'q_norm_g', 'k_norm_g', 'rel_bias', 'conv_w', 'pool_w', 'pool_scale', 'w_out', 'norm2_g', 'w_mlp1', 'w_mlp2']
TWIN_WEIGHTS = ['norm1_g', 'w_in', 'q_norm_g', 'k_norm_g', 'rel_bias', 'conv_w', 'pool_w', 'pool_scale', 'w_out', 'norm2_g', 'w_mlp1', 'w_mlp2']
TWIN_DIFF_INPUT = 'x'
TWIN_INPUTS = ['x', 'norm1_g', 'w_in', 'q_norm_g', 'k_norm_g', 'rel_bias', 'conv_w', 'pool_w', 'pool_scale', 'w_out', 'norm2_g', 'w_mlp1', 'w_mlp2', 'loss_target', 'm_norm1_g', 'm_w_in', 'm_q_norm_g', 'm_k_norm_g', 'm_rel_bias', 'm_conv_w', 'm_pool_w', 'm_pool_scale', 'm_w_out', 'm_norm2_g', 'm_w_mlp1', 'm_w_mlp2', 'v_norm1_g', 'v_w_in', 'v_q_norm_g', 'v_k_norm_g', 'v_rel_bias', 'v_conv_w', 'v_pool_w', 'v_pool_scale', 'v_w_out', 'v_norm2_g', 'v_w_mlp1', 'v_w_mlp2']
TWIN_OUTPUTS = ['loss', 'grad_x', 'grad_norm1_g', 'grad_w_in', 'grad_q_norm_g', 'grad_k_norm_g', 'grad_rel_bias', 'grad_conv_w', 'grad_pool_w', 'grad_pool_scale', 'grad_w_out', 'grad_norm2_g', 'grad_w_mlp1', 'grad_w_mlp2', 'delta_norm1_g', 'delta_w_in', 'delta_q_norm_g', 'delta_k_norm_g', 'delta_rel_bias', 'delta_conv_w', 'delta_pool_w', 'delta_pool_scale', 'delta_w_out', 'delta_norm2_g', 'delta_w_mlp1', 'delta_w_mlp2', 'new_m_norm1_g', 'new_m_w_in', 'new_m_q_norm_g', 'new_m_k_norm_g', 'new_m_rel_bias', 'new_m_conv_w', 'new_m_pool_w', 'new_m_pool_scale', 'new_m_w_out', 'new_m_norm2_g', 'new_m_w_mlp1', 'new_m_w_mlp2', 'new_v_norm1_g', 'new_v_w_in', 'new_v_q_norm_g', 'new_v_k_norm_g', 'new_v_rel_bias', 'new_v_conv_w', 'new_v_pool_w', 'new_v_pool_scale', 'new_v_w_out', 'new_v_norm2_g', 'new_v_w_mlp1', 'new_v_w_mlp2']
TWIN_LEAF_KINDS = {'loss': 'loss', 'grad_x': 'grad_x', 'grad_norm1_g': 'grad_w', 'grad_w_in': 'grad_w', 'grad_q_norm_g': 'grad_w', 'grad_k_norm_g': 'grad_w', 'grad_rel_bias': 'grad_w', 'grad_conv_w': 'grad_w', 'grad_pool_w': 'grad_w', 'grad_pool_scale': 'grad_w', 'grad_w_out': 'grad_w', 'grad_norm2_g': 'grad_w', 'grad_w_mlp1': 'grad_w', 'grad_w_mlp2': 'grad_w', 'delta_norm1_g': 'delta_w', 'delta_w_in': 'delta_w', 'delta_q_norm_g': 'delta_w', 'delta_k_norm_g': 'delta_w', 'delta_rel_bias': 'delta_w', 'delta_conv_w': 'delta_w', 'delta_pool_w': 'delta_w', 'delta_pool_scale': 'delta_w', 'delta_w_out': 'delta_w', 'delta_norm2_g': 'delta_w', 'delta_w_mlp1': 'delta_w', 'delta_w_mlp2': 'delta_w', 'new_m_norm1_g': 'new_m', 'new_m_w_in': 'new_m', 'new_m_q_norm_g': 'new_m', 'new_m_k_norm_g': 'new_m', 'new_m_rel_bias': 'new_m', 'new_m_conv_w': 'new_m', 'new_m_pool_w': 'new_m', 'new_m_pool_scale': 'new_m', 'new_m_w_out': 'new_m', 'new_m_norm2_g': 'new_m', 'new_m_w_mlp1': 'new_m', 'new_m_w_mlp2': 'new_m', 'new_v_norm1_g': 'new_v', 'new_v_w_in': 'new_v', 'new_v_q_norm_g': 'new_v', 'new_v_k_norm_g': 'new_v', 'new_v_rel_bias': 'new_v', 'new_v_conv_w': 'new_v', 'new_v_pool_w': 'new_v', 'new_v_pool_scale': 'new_v', 'new_v_w_out': 'new_v', 'new_v_norm2_g': 'new_v', 'new_v_w_mlp1': 'new_v', 'new_v_w_mlp2': 'new_v'}


def _forward(args):
    return _fwd_reference(*[args[k] for k in FWD_PARAMS])


def _output_shape():
    out = _jax.eval_shape(lambda: _forward(_fwd_setup_inputs(0)))
    return out.shape, out.dtype

N_MICROBATCH = 1
ADAM_LR = 0.001
ADAM_B1 = 0.9
ADAM_B2 = 0.999
ADAM_EPS = 1e-08
ADAM_WD = 0.01
ADAM_STEP = 10
PER_EXAMPLE_BATCH_AXIS = {'x': 0, 'loss_target': 0}
SHARED_INPUTS = []
_WEIGHT_DTYPES = {'norm1_g': _jnp.float32, 'w_in': _jnp.float32, 'q_norm_g': _jnp.float32, 'k_norm_g': _jnp.float32, 'rel_bias': _jnp.float32, 'conv_w': _jnp.float32, 'pool_w': _jnp.float32, 'pool_scale': _jnp.float32, 'w_out': _jnp.float32, 'norm2_g': _jnp.float32, 'w_mlp1': _jnp.float32, 'w_mlp2': _jnp.float32}
MOMENT_SCALE = {'norm1_g': 2.882185e+01, 'w_in': 6.894094e+00, 'q_norm_g': 9.972599e-01, 'k_norm_g': 9.978936e-01, 'rel_bias': 1.043415e-01, 'conv_w': 1.754790e+01, 'pool_w': 1.223358e+00, 'pool_scale': 1.023570e+01, 'w_out': 1.075700e+01, 'norm2_g': 9.888200e+01, 'w_mlp1': 7.722020e+00, 'w_mlp2': 2.947624e+01}


def _to_microbatches(a, axis):
    t = _jnp.moveaxis(a, axis, 0)
    t = t.reshape((N_MICROBATCH, t.shape[0] // N_MICROBATCH) + t.shape[1:])
    return _jnp.moveaxis(t, 1, axis + 1)


def setup_inputs(seed: int = 0) -> dict:
    inp = _fwd_setup_inputs(seed)
    key = _jax.random.fold_in(_jax.random.key(seed), 7919)
    shape, _ = _output_shape()
    out = dict(inp)
    out["loss_target"] = _jax.random.normal(_jax.random.fold_in(key, 0), shape, _jnp.float32)
    for i, name in enumerate(TWIN_WEIGHTS):
        w = inp[name].astype(_jnp.float32)
        if MOMENT_SCALE is None:
            s = _jnp.sqrt(_jnp.mean(_jnp.square(w)) + 1e-30)
        else:
            s = MOMENT_SCALE[name]
        km, kv = _jax.random.split(_jax.random.fold_in(key, i + 1))
        out[name] = w
        out["m_" + name] = s * _jax.random.normal(km, w.shape, _jnp.float32)
        out["v_" + name] = (s * s) * _jax.random.uniform(kv, w.shape, _jnp.float32, 0.5, 1.5)
    if N_MICROBATCH > 1:
        for name, axis in PER_EXAMPLE_BATCH_AXIS.items():
            out[name] = _to_microbatches(out[name], axis)
    return {'x': out['x'], 'norm1_g': out['norm1_g'], 'w_in': out['w_in'], 'q_norm_g': out['q_norm_g'], 'k_norm_g': out['k_norm_g'], 'rel_bias': out['rel_bias'], 'conv_w': out['conv_w'], 'pool_w': out['pool_w'], 'pool_scale': out['pool_scale'], 'w_out': out['w_out'], 'norm2_g': out['norm2_g'], 'w_mlp1': out['w_mlp1'], 'w_mlp2': out['w_mlp2'], 'loss_target': out['loss_target'], 'm_norm1_g': out['m_norm1_g'], 'm_w_in': out['m_w_in'], 'm_q_norm_g': out['m_q_norm_g'], 'm_k_norm_g': out['m_k_norm_g'], 'm_rel_bias': out['m_rel_bias'], 'm_conv_w': out['m_conv_w'], 'm_pool_w': out['m_pool_w'], 'm_pool_scale': out['m_pool_scale'], 'm_w_out': out['m_w_out'], 'm_norm2_g': out['m_norm2_g'], 'm_w_mlp1': out['m_w_mlp1'], 'm_w_mlp2': out['m_w_mlp2'], 'v_norm1_g': out['v_norm1_g'], 'v_w_in': out['v_w_in'], 'v_q_norm_g': out['v_q_norm_g'], 'v_k_norm_g': out['v_k_norm_g'], 'v_rel_bias': out['v_rel_bias'], 'v_conv_w': out['v_conv_w'], 'v_pool_w': out['v_pool_w'], 'v_pool_scale': out['v_pool_scale'], 'v_w_out': out['v_w_out'], 'v_norm2_g': out['v_norm2_g'], 'v_w_mlp1': out['v_w_mlp1'], 'v_w_mlp2': out['v_w_mlp2']}


def _loss(weights, diff, rest, loss_target):
    with _jax.named_scope("forward"):
        args = {**rest, TWIN_DIFF_INPUT: diff, **{k: w.astype(_WEIGHT_DTYPES[k]) for k, w in weights.items()}}
        y = _forward(args)
    with _jax.named_scope("loss_head"):
        err = _jnp.square(y.astype(_jnp.float32) - loss_target)
        return 0.5 * _jnp.sum(_jnp.mean(err, axis=-1)) if err.ndim else 0.5 * err


def _adamw(w, g, m, v):
    m = ADAM_B1 * m + (1.0 - ADAM_B1) * g
    v = ADAM_B2 * v + (1.0 - ADAM_B2) * _jnp.square(g)
    m_hat = m / (1.0 - ADAM_B1 ** ADAM_STEP)
    v_hat = v / (1.0 - ADAM_B2 ** ADAM_STEP)
    delta = -ADAM_LR * (m_hat / (_jnp.sqrt(v_hat) + ADAM_EPS) + ADAM_WD * w)
    return delta, m, v


def reference(x, norm1_g, w_in, q_norm_g, k_norm_g, rel_bias, conv_w, pool_w, pool_scale, w_out, norm2_g, w_mlp1, w_mlp2, loss_target, m_norm1_g, m_w_in, m_q_norm_g, m_k_norm_g, m_rel_bias, m_conv_w, m_pool_w, m_pool_scale, m_w_out, m_norm2_g, m_w_mlp1, m_w_mlp2, v_norm1_g, v_w_in, v_q_norm_g, v_k_norm_g, v_rel_bias, v_conv_w, v_pool_w, v_pool_scale, v_w_out, v_norm2_g, v_w_mlp1, v_w_mlp2):
    given = dict(x=x, norm1_g=norm1_g, w_in=w_in, q_norm_g=q_norm_g, k_norm_g=k_norm_g, rel_bias=rel_bias, conv_w=conv_w, pool_w=pool_w, pool_scale=pool_scale, w_out=w_out, norm2_g=norm2_g, w_mlp1=w_mlp1, w_mlp2=w_mlp2, loss_target=loss_target, m_norm1_g=m_norm1_g, m_w_in=m_w_in, m_q_norm_g=m_q_norm_g, m_k_norm_g=m_k_norm_g, m_rel_bias=m_rel_bias, m_conv_w=m_conv_w, m_pool_w=m_pool_w, m_pool_scale=m_pool_scale, m_w_out=m_w_out, m_norm2_g=m_norm2_g, m_w_mlp1=m_w_mlp1, m_w_mlp2=m_w_mlp2, v_norm1_g=v_norm1_g, v_w_in=v_w_in, v_q_norm_g=v_q_norm_g, v_k_norm_g=v_k_norm_g, v_rel_bias=v_rel_bias, v_conv_w=v_conv_w, v_pool_w=v_pool_w, v_pool_scale=v_pool_scale, v_w_out=v_w_out, v_norm2_g=v_norm2_g, v_w_mlp1=v_w_mlp1, v_w_mlp2=v_w_mlp2)
    weights = {n: given[n] for n in TWIN_WEIGHTS}
    shared = {n: given[n] for n in SHARED_INPUTS}
    per_example = {n: given[n] for n in ['x']}
    grad_fn = _jax.value_and_grad(_loss, argnums=(0, 1))

    def one_microbatch(ex, loss_target):
        ex = dict(ex)
        diff = ex.pop(TWIN_DIFF_INPUT)
        return grad_fn(weights, diff, {**shared, **ex}, loss_target)

    if N_MICROBATCH == 1:
        loss, (grad_w, grad_x) = one_microbatch(per_example, given["loss_target"])
    else:
        def body(carry, xs):
            loss_sum, grad_sum = carry
            l_k, (gw_k, gx_k) = one_microbatch(xs[0], xs[1])
            with _jax.named_scope("update"):
                return (loss_sum + l_k, _jax.tree.map(_jnp.add, grad_sum, gw_k)), gx_k

        init = (_jnp.zeros((), _jnp.float32), _jax.tree.map(_jnp.zeros_like, weights))
        (loss, grad_w), grad_x = _jax.lax.scan(body, init, (per_example, given["loss_target"]))
    with _jax.named_scope("update"):
        delta_w, new_m, new_v = {}, {}, {}
        for n in TWIN_WEIGHTS:
            delta_w[n], new_m[n], new_v[n] = _adamw(weights[n], grad_w[n], given["m_" + n], given["v_" + n])
    return (loss, grad_x, *[grad_w[n] for n in TWIN_WEIGHTS], *[delta_w[n] for n in TWIN_WEIGHTS],
            *[new_m[n] for n in TWIN_WEIGHTS], *[new_v[n] for n in TWIN_WEIGHTS])
```

```python
import functools

import jax
import jax.numpy as jnp
from jax import lax
from jax.experimental import pallas as pl
from jax.experimental.pallas import tpu as pltpu

F32 = jnp.float32
BF16 = jnp.bfloat16

D = 1024
DEPTH = 4
CH = 64
NPREV = 8
KB = (NPREV + 1) * CH
PADR = NPREV * CH
HD = 64
AW = 512
CW = 256
PWD = 256
DIN = 3 * AW + 3 * CW + PWD
DFF = 4 * D
NIDX = 384
EPS = 1e-6
NEG_INF = -1e30
QB = 4

ADAM_LR = 0.001
ADAM_B1 = 0.9
ADAM_B2 = 0.999
ADAM_EPS = 1e-08
ADAM_WD = 0.01
ADAM_STEP = 10

VMEM_LIMIT = 52 * 1024 * 1024
MESH = pl.DeviceIdType.MESH
ANY = pl.BlockSpec(memory_space=pl.ANY)


def _cp(*sem):
    return pltpu.CompilerParams(dimension_semantics=sem, vmem_limit_bytes=VMEM_LIMIT)


def _inv_rms(x):
    return lax.rsqrt(jnp.mean(x * x, axis=-1, keepdims=True) + EPS)


def _head_mean_matrix():
    r = lax.broadcasted_iota(jnp.int32, (AW, AW), 0) // HD
    c = lax.broadcasted_iota(jnp.int32, (AW, AW), 1) // HD
    return jnp.where(r == c, 1.0 / HD, 0.0).astype(BF16)


def _head_mean(x, hm):
    hi = x.astype(BF16)
    lo = (x - hi.astype(F32)).astype(BF16)
    return (jnp.dot(hi, hm, preferred_element_type=F32)
            + jnp.dot(lo, hm, preferred_element_type=F32))


def _cast_bf16(name, x):
    r, c = x.shape
    tm = 512

    def body(x_ref, o_ref):
        o_ref[...] = x_ref[...].astype(BF16)

    return pl.pallas_call(
        body, name=name, grid=(r // tm,),
        in_specs=[pl.BlockSpec((tm, c), lambda i: (i, 0))],
        out_specs=pl.BlockSpec((tm, c), lambda i: (i, 0)),
        out_shape=jax.ShapeDtypeStruct((r, c), BF16),
        compiler_params=_cp("parallel"),
    )(x)


def _rmsnorm(name, x, g):
    s = x.shape[0]
    tm = 512

    def body(x_ref, g_ref, h_ref):
        xv = x_ref[...]
        h_ref[...] = (xv * _inv_rms(xv) * g_ref[...]).astype(BF16)

    return pl.pallas_call(
        body, name=name, grid=(s // tm,),
        in_specs=[pl.BlockSpec((tm, D), lambda i: (i, 0)), pl.BlockSpec((1, D), lambda i: (0, 0))],
        out_specs=pl.BlockSpec((tm, D), lambda i: (i, 0)),
        out_shape=jax.ShapeDtypeStruct((s, D), BF16),
        compiler_params=_cp("parallel"),
    )(x, g)


def _mm_nn(name, a, w, l, tm, tn, out_dtype):
    s, k = a.shape
    n = w.shape[2]

    def body(a_ref, w_ref, o_ref):
        o_ref[...] = jnp.dot(a_ref[...], w_ref[...], preferred_element_type=F32).astype(o_ref.dtype)

    return pl.pallas_call(
        body, name=name, grid=(s // tm, n // tn),
        in_specs=[pl.BlockSpec((tm, k), lambda i, j: (i, 0)),
                  pl.BlockSpec((None, k, tn), lambda i, j: (l, 0, j))],
        out_specs=pl.BlockSpec((tm, tn), lambda i, j: (i, j)),
        out_shape=jax.ShapeDtypeStruct((s, n), out_dtype),
        compiler_params=_cp("parallel", "parallel"),
    )(a, w)


def _mm_mlp1(name, h2, w, l):
    s, k = h2.shape
    n = w.shape[2]
    tm, tn = 512, 1024

    def body(a_ref, w_ref, o_ref, f_ref):
        acc = jnp.dot(a_ref[...], w_ref[...], preferred_element_type=F32)
        o_ref[...] = acc.astype(BF16)
        f_ref[...] = jnp.square(jnp.maximum(acc, 0.0)).astype(BF16)

    return pl.pallas_call(
        body, name=name, grid=(s // tm, n // tn),
        in_specs=[pl.BlockSpec((tm, k), lambda i, j: (i, 0)),
                  pl.BlockSpec((None, k, tn), lambda i, j: (l, 0, j))],
        out_specs=[pl.BlockSpec((tm, tn), lambda i, j: (i, j))] * 2,
        out_shape=[jax.ShapeDtypeStruct((s, n), BF16)] * 2,
        compiler_params=_cp("parallel", "parallel"),
    )(h2, w)


def _mm_res_norm(name, a, w, l, res, g):
    s, k = a.shape
    tm = 256

    def body(a_ref, w_ref, r_ref, g_ref, x_ref, h_ref):
        acc = r_ref[...] + jnp.dot(a_ref[...], w_ref[...], preferred_element_type=F32)
        x_ref[...] = acc
        h_ref[...] = (acc * _inv_rms(acc) * g_ref[...]).astype(BF16)

    return pl.pallas_call(
        body, name=name, grid=(s // tm,),
        in_specs=[pl.BlockSpec((tm, k), lambda i: (i, 0)),
                  pl.BlockSpec((None, k, D), lambda i: (l, 0, 0)),
                  pl.BlockSpec((tm, D), lambda i: (i, 0)),
                  pl.BlockSpec((1, D), lambda i: (0, 0))],
        out_specs=[pl.BlockSpec((tm, D), lambda i: (i, 0))] * 2,
        out_shape=[jax.ShapeDtypeStruct((s, D), F32), jax.ShapeDtypeStruct((s, D), BF16)],
        compiler_params=_cp("parallel"),
    )(a, w, res, g)


def _qkv(name, p, qg, kg):
    s = p.shape[0]
    tm = PADR
    nb = s // tm

    def body(pq_ref, pk_ref, pv_ref, qg_ref, kg_ref, q_ref, k_ref, v_ref):
        t = pl.program_id(0)
        hm = _head_mean_matrix()

        def nrm(x, g):
            return x * lax.rsqrt(_head_mean(x * x, hm) + EPS) * g

        q_ref[...] = (nrm(pq_ref[...], qg_ref[...]) * 0.125).astype(BF16)
        kk = nrm(pk_ref[...], kg_ref[...]).astype(BF16)
        vv = pv_ref[...].astype(BF16)
        first = t == 0
        k_ref[...] = jnp.where(first, jnp.zeros_like(kk), kk)
        v_ref[...] = jnp.where(first, jnp.zeros_like(vv), vv)

    def src(col):
        return pl.BlockSpec((tm, AW), lambda t: (jnp.maximum(t - 1, 0), col))

    gspec = pl.BlockSpec((1, AW), lambda t: (0, 0))
    return pl.pallas_call(
        body, name=name, grid=(nb + 1,),
        in_specs=[src(0), src(1), src(2), gspec, gspec],
        out_specs=[pl.BlockSpec((tm, AW), lambda t: (jnp.maximum(t - 1, 0), 0)),
                   pl.BlockSpec((tm, AW), lambda t: (t, 0)),
                   pl.BlockSpec((tm, AW), lambda t: (t, 0))],
        out_shape=[jax.ShapeDtypeStruct((s, AW), BF16),
                   jax.ShapeDtypeStruct((s + PADR, AW), BF16),
                   jax.ShapeDtypeStruct((s + PADR, AW), BF16)],
        compiler_params=_cp("arbitrary"),
    )(p, p, p, qg, kg)


def _bias_onehot(qi):
    idx = lax.broadcasted_iota(jnp.int32, (NIDX, KB), 0)
    kj = lax.broadcasted_iota(jnp.int32, (NIDX, KB), 1)
    rel = jnp.clip(qi + PADR - kj, -128, 128) + 128
    return jnp.where(rel == idx, 1.0, 0.0).astype(F32)


def _bias_expand(name, rb):
    def body(rb_ref, o_ref):
        def step(qi, carry):
            o_ref[qi] = jnp.dot(rb_ref[...], _bias_onehot(qi), preferred_element_type=F32,
                                precision=lax.Precision.HIGHEST)
            return carry

        lax.fori_loop(0, CH, step, 0)

    return pl.pallas_call(
        body, name=name,
        out_shape=jax.ShapeDtypeStruct((CH, 8, KB), F32),
    )(rb)


def _bias_reduce(name, db):
    def body(db_ref, o_ref):
        def step(qi, acc):
            return acc + lax.dot_general(db_ref[qi], _bias_onehot(qi), (((1,), (1,)), ((), ())),
                                         preferred_element_type=F32, precision=lax.Precision.HIGHEST)

        o_ref[...] = lax.fori_loop(0, CH, step, jnp.zeros((8, NIDX), F32))

    return pl.pallas_call(
        body, name=name,
        out_shape=jax.ShapeDtypeStruct((8, NIDX), F32),
    )(db)


def _softmax_band(qm, k, bias, valid):
    s = lax.dot_general(qm, k, (((1,), (1,)), ((), ())), preferred_element_type=F32)
    s = jnp.where(valid, s + bias, NEG_INF)
    e = jnp.exp(s - jnp.max(s, axis=-1, keepdims=True))
    return e / jnp.sum(e, axis=-1, keepdims=True)


def _attn_fwd(name, q, kp, vp, bias):
    s = q.shape[0]
    tq = QB * CH

    def body(q_ref, k_ref, v_ref, b_ref, o_ref):
        g = pl.program_id(1)
        lane = lax.broadcasted_iota(jnp.int32, (CH, 128), 1)
        kj = lax.broadcasted_iota(jnp.int32, (CH, KB), 1)
        for c in range(QB):
            n = g * QB + c
            start = pl.multiple_of(n * CH, CH)
            k = k_ref[pl.ds(start, KB), :]
            v = v_ref[pl.ds(start, KB), :]
            qc = q_ref[c * CH:(c + 1) * CH, :]
            valid = (kj + n * CH) >= PADR
            outs = []
            for hh in range(2):
                own = (lane < HD) if hh == 0 else (lane >= HD)
                pr = _softmax_band(jnp.where(own, qc, jnp.zeros_like(qc)), k, b_ref[hh], valid)
                outs.append(jnp.dot(pr.astype(BF16), v, preferred_element_type=F32))
            o_ref[c * CH:(c + 1) * CH, :] = jnp.where(lane < HD, outs[0], outs[1]).astype(BF16)

    return pl.pallas_call(
        body, name=name, grid=(AW // 128, s // tq),
        in_specs=[pl.BlockSpec((tq, 128), lambda h, g: (g, h)),
                  pl.BlockSpec((s + PADR, 128), lambda h, g: (0, h)),
                  pl.BlockSpec((s + PADR, 128), lambda h, g: (0, h)),
                  pl.BlockSpec((2, CH, KB), lambda h, g: (h, 0, 0))],
        out_specs=pl.BlockSpec((tq, 128), lambda h, g: (g, h)),
        out_shape=jax.ShapeDtypeStruct((s, AW), BF16),
        compiler_params=_cp("parallel", "arbitrary"),
    )(q, kp, vp, bias)


def _attn_bwd(name, q, kp, vp, bias, dmix):
    s = q.shape[0]
    tq = QB * CH

    def body(q_ref, k_ref, v_ref, b_ref, do_ref, dq_ref, dk_ref, dv_ref, db_ref):
        g = pl.program_id(1)

        @pl.when(g == 0)
        def _():
            dk_ref[...] = jnp.zeros_like(dk_ref)
            dv_ref[...] = jnp.zeros_like(dv_ref)
            db_ref[...] = jnp.zeros_like(db_ref)

        lane = lax.broadcasted_iota(jnp.int32, (CH, 128), 1)
        kj = lax.broadcasted_iota(jnp.int32, (CH, KB), 1)
        tn = (((0,), (0,)), ((), ()))
        for c in range(QB):
            n = g * QB + c
            start = pl.multiple_of(n * CH, CH)
            k = k_ref[pl.ds(start, KB), :]
            v = v_ref[pl.ds(start, KB), :]
            qc = q_ref[c * CH:(c + 1) * CH, :]
            doc = do_ref[c * CH:(c + 1) * CH, :].astype(BF16)
            valid = (kj + n * CH) >= PADR
            dqs = []
            dk_acc = jnp.zeros((KB, 128), F32)
            dv_acc = jnp.zeros((KB, 128), F32)
            for hh in range(2):
                own = (lane < HD) if hh == 0 else (lane >= HD)
                qm = jnp.where(own, qc, jnp.zeros_like(qc))
                dom = jnp.where(own, doc, jnp.zeros_like(doc))
                pr = _softmax_band(qm, k, b_ref[hh], valid)
                dpr = lax.dot_general(dom, v, (((1,), (1,)), ((), ())), preferred_element_type=F32)
                ds = pr * (dpr - jnp.sum(dpr * pr, axis=-1, keepdims=True))
                db_ref[hh] += ds
                dsb = ds.astype(BF16)
                dqs.append(jnp.dot(dsb, k, preferred_element_type=F32))
                dk_acc = dk_acc + lax.dot_general(dsb, qm, tn, preferred_element_type=F32)
                dv_acc = dv_acc + lax.dot_general(pr.astype(BF16), dom, tn, preferred_element_type=F32)
            dq_ref[c * CH:(c + 1) * CH, :] = jnp.where(lane < HD, dqs[0], dqs[1])
            dk_ref[pl.ds(start, KB), :] += dk_acc
            dv_ref[pl.ds(start, KB), :] += dv_acc

    return pl.pallas_call(
        body, name=name, grid=(AW // 128, s // tq),
        in_specs=[pl.BlockSpec((tq, 128), lambda h, g: (g, h)),
                  pl.BlockSpec((s + PADR, 128), lambda h, g: (0, h)),
                  pl.BlockSpec((s + PADR, 128), lambda h, g: (0, h)),
                  pl.BlockSpec((2, CH, KB), lambda h, g: (h, 0, 0)),
                  pl.BlockSpec((tq, 128), lambda h, g: (g, h))],
        out_specs=[pl.BlockSpec((tq, 128), lambda h, g: (g, h)),
                   pl.BlockSpec((s + PADR, 128), lambda h, g: (0, h)),
                   pl.BlockSpec((s + PADR, 128), lambda h, g: (0, h)),
                   pl.BlockSpec((2, CH, KB), lambda h, g: (h, 0, 0))],
        out_shape=[jax.ShapeDtypeStruct((s, AW), F32),
                   jax.ShapeDtypeStruct((s + PADR, AW), F32),
                   jax.ShapeDtypeStruct((s + PADR, AW), F32),
                   jax.ShapeDtypeStruct((8, CH, KB), F32)],
        compiler_params=_cp("parallel", "arbitrary"),
    )(q, kp, vp, bias, dmix)


def _rows_before(cur, prev, k):
    row = lax.broadcasted_iota(jnp.int32, cur.shape, 0)
    return jnp.where(row >= k, pltpu.roll(cur, k, 0), pltpu.roll(prev, k, 0))


def _rows_after(cur, nxt, k):
    n = cur.shape[0]
    row = lax.broadcasted_iota(jnp.int32, cur.shape, 0)
    return jnp.where(row < n - k, pltpu.roll(cur, n - k, 0), pltpu.roll(nxt, n - k, 0))


def _pool_window_lanes():
    lg = lax.broadcasted_iota(jnp.int32, (1, PWD), 1) // 64
    return lg, jnp.where(lg == 0, 2.0, jnp.where(lg == 1, 4.0, jnp.where(lg == 2, 8.0, 16.0))).astype(F32)


def _pool_mean_minus_token(u, up, row0):
    lg, wv = _pool_window_lanes()
    sums = []
    c, p = u, up
    for k in (1, 2, 4, 8):
        c2 = c + _rows_before(c, p, k)
        p = p + pltpu.roll(p, k, 0)
        c = c2
        sums.append(c)
    win = jnp.where(lg == 0, sums[0], jnp.where(lg == 1, sums[1], jnp.where(lg == 2, sums[2], sums[3])))
    pos1 = (row0 + lax.broadcasted_iota(jnp.int32, u.shape, 0) + 1).astype(F32)
    cnt = jnp.minimum(pos1, wv)
    return win / cnt - u, cnt


def _conv_taps(z, zp, w0, w1, w2):
    z1 = _rows_before(z, zp, 1)
    z2 = _rows_before(z, zp, 2)
    return (w0 * z2 + w1 * z1) + w2 * z, z1, z2


CP_TM = 512


def _convpool_fwd(name, p, o, cw, pwbd, ps):
    s = p.shape[0]
    tm = CP_TM
    nb = s // tm

    def body(gb_ref, gc_ref, hin_ref, u_ref, gcp_ref, hinp_ref, up_ref, o_ref, cw_ref, pw_ref, ps_ref, mix_ref):
        i = pl.program_id(0)
        has_prev = i > 0
        z = gc_ref[...] * hin_ref[...]
        zp = jnp.where(has_prev, gcp_ref[...] * hinp_ref[...], 0.0)
        y3, _, _ = _conv_taps(z, zp, cw_ref[0:1, :], cw_ref[1:2, :], cw_ref[2:3, :])
        m, _ = _pool_mean_minus_token(u_ref[...], jnp.where(has_prev, up_ref[...], 0.0), i * tm)
        yp = jnp.dot(m.astype(BF16), pw_ref[...].astype(BF16), preferred_element_type=F32) * ps_ref[...]
        mix_ref[:, 0:AW] = o_ref[...]
        mix_ref[:, AW:AW + CW] = (gb_ref[...] * y3).astype(BF16)
        mix_ref[:, AW + CW:D] = yp.astype(BF16)

    def cur(col):
        return pl.BlockSpec((tm, CW), lambda i: (i, col))

    def prev(col):
        return pl.BlockSpec((tm, CW), lambda i: (jnp.maximum(i - 1, 0), col))

    def whole(a):
        return pl.BlockSpec(a.shape, lambda i: (0,) * a.ndim)

    return pl.pallas_call(
        body, name=name, grid=(nb,),
        in_specs=[cur(6), cur(7), cur(8), cur(9), prev(7), prev(8), prev(9),
                  pl.BlockSpec((tm, AW), lambda i: (i, 0)), whole(cw), whole(pwbd), whole(ps)],
        out_specs=pl.BlockSpec((tm, D), lambda i: (i, 0)),
        out_shape=jax.ShapeDtypeStruct((s, D), BF16),
        compiler_params=_cp("parallel"),
    )(p, p, p, p, p, p, p, o, cw, pwbd, ps)


def _convpool_bwd(name, p, dmix, cw, pwbd, ps):
    s = p.shape[0]
    tm = CP_TM
    nb = s // tm

    def body(gb_ref, gc_ref, hin_ref, u_ref, gcp_ref, hinp_ref, up_ref, gbn_ref, dyc_ref, dyp_ref, dycn_ref, dypn_ref,
             cw_ref, pw_ref, ps_ref, dcp_ref, dw0_ref, dw1_ref, dw2_ref, dps_ref, dpw_ref):
        i = pl.program_id(0)
        has_prev = i > 0
        has_next = i < nb - 1
        w0, w1, w2 = cw_ref[0:1, :], cw_ref[1:2, :], cw_ref[2:3, :]
        gb, gc, hin = gb_ref[...], gc_ref[...], hin_ref[...]
        dyc = dyc_ref[...]
        z = gc * hin
        zp = jnp.where(has_prev, gcp_ref[...] * hinp_ref[...], 0.0)
        y3, z1, z2 = _conv_taps(z, zp, w0, w1, w2)
        dy3 = dyc * gb
        dy3n = jnp.where(has_next, dycn_ref[...] * gbn_ref[...], 0.0)
        dz = w2 * dy3 + w1 * _rows_after(dy3, dy3n, 1) + w0 * _rows_after(dy3, dy3n, 2)
        pw = pw_ref[...].astype(BF16)
        psv = ps_ref[...]
        m, cnt = _pool_mean_minus_token(u_ref[...], jnp.where(has_prev, up_ref[...], 0.0), i * tm)
        mb = m.astype(BF16)
        dyp = dyp_ref[...]
        dmp = (dyp * psv).astype(BF16)
        dmpn = jnp.where(has_next, dypn_ref[...] * psv, 0.0).astype(BF16)
        nt = (((1,), (1,)), ((), ()))
        dm = lax.dot_general(dmp, pw, nt, preferred_element_type=F32)
        dmn = lax.dot_general(dmpn, pw, nt, preferred_element_type=F32)
        lg, wv = _pool_window_lanes()
        cc, cn = dm / cnt, dmn / wv
        sums = []
        for k in (1, 2, 4, 8):
            c2 = cc + _rows_after(cc, cn, k)
            cn = cn + pltpu.roll(cn, tm - k, 0)
            cc = c2
            sums.append(cc)
        du = jnp.where(lg == 0, sums[0], jnp.where(lg == 1, sums[1], jnp.where(lg == 2, sums[2], sums[3]))) - dm
        dcp_ref[:, 0:CW] = (dyc * y3).astype(BF16)
        dcp_ref[:, CW:2 * CW] = (dz * hin).astype(BF16)
        dcp_ref[:, 2 * CW:3 * CW] = (dz * gc).astype(BF16)
        dcp_ref[:, 3 * CW:4 * CW] = du.astype(BF16)
        parts = (jnp.sum(dy3 * z2, axis=0, keepdims=True),
                 jnp.sum(dy3 * z1, axis=0, keepdims=True),
                 jnp.sum(dy3 * z, axis=0, keepdims=True),
                 jnp.sum(dyp * jnp.dot(mb, pw, preferred_element_type=F32), axis=0, keepdims=True),
                 lax.dot_general(mb, dmp, (((0,), (0,)), ((), ())), preferred_element_type=F32))
        accs = (dw0_ref, dw1_ref, dw2_ref, dps_ref, dpw_ref)

        @pl.when(i == 0)
        def _():
            for a, v in zip(accs, parts):
                a[...] = v

        @pl.when(i > 0)
        def _():
            for a, v in zip(accs, parts):
                a[...] += v

    def cur(col):
        return pl.BlockSpec((tm, CW), lambda i: (i, col))

    def prev(col):
        return pl.BlockSpec((tm, CW), lambda i: (jnp.maximum(i - 1, 0), col))

    def nxt(col):
        return pl.BlockSpec((tm, CW), lambda i: (jnp.minimum(i + 1, nb - 1), col))

    def whole(shape):
        return pl.BlockSpec(shape, lambda i: (0,) * len(shape))

    row = jax.ShapeDtypeStruct((1, CW), F32)
    return pl.pallas_call(
        body, name=name, grid=(nb,),
        in_specs=[cur(6), cur(7), cur(8), cur(9), prev(7), prev(8), prev(9), nxt(6),
                  cur(2), cur(3), nxt(2), nxt(3), whole(cw.shape), whole(pwbd.shape), whole(ps.shape)],
        out_specs=[pl.BlockSpec((tm, D), lambda i: (i, 0)), whole((1, CW)), whole((1, CW)), whole((1, CW)),
                   whole((1, PWD)), whole((PWD, PWD))],
        out_shape=[jax.ShapeDtypeStruct((s, D), BF16), row, row, row, row,
                   jax.ShapeDtypeStruct((PWD, PWD), F32)],
        compiler_params=_cp("arbitrary"),
    )(p, p, p, p, p, p, p, p, dmix, dmix, dmix, dmix, cw, pwbd, ps)


def _qkv_bwd(name, p, dq, dkp, dvp, dcp, qg, kg):
    s = p.shape[0]
    tm = 256
    off = PADR // tm

    def body(pq_ref, pk_ref, dq_ref, dk_ref, dv_ref, dcp_ref, qg_ref, kg_ref, dp_ref, dqg_ref, dkg_ref):
        i = pl.program_id(0)
        hm = _head_mean_matrix()

        def nrm_bwd(x, g, dy):
            r = lax.rsqrt(_head_mean(x * x, hm) + EPS)
            xn = x * r
            dxn = dy * g
            dx = r * (dxn - xn * _head_mean(dxn * xn, hm))
            dg = jnp.sum(dy * xn, axis=0, keepdims=True)
            dg = (dg[:, 0:128] + dg[:, 128:256]) + (dg[:, 256:384] + dg[:, 384:512])
            return dx, dg + pltpu.roll(dg, HD, 1)

        dxq, dgq = nrm_bwd(pq_ref[...], qg_ref[...], dq_ref[...] * 0.125)
        dxk, dgk = nrm_bwd(pk_ref[...], kg_ref[...], dk_ref[...])
        dp_ref[:, 0:AW] = dxq.astype(BF16)
        dp_ref[:, AW:2 * AW] = dxk.astype(BF16)
        dp_ref[:, 2 * AW:3 * AW] = dv_ref[...].astype(BF16)
        dp_ref[:, 3 * AW:DIN] = dcp_ref[...]

        @pl.when(i == 0)
        def _():
            dqg_ref[...] = dgq
            dkg_ref[...] = dgk

        @pl.when(i > 0)
        def _():
            dqg_ref[...] += dgq
            dkg_ref[...] += dgk

    gspec = pl.BlockSpec((1, AW), lambda i: (0, 0))
    gout = pl.BlockSpec((1, 128), lambda i: (0, 0))
    return pl.pallas_call(
        body, name=name, grid=(s // tm,),
        in_specs=[pl.BlockSpec((tm, AW), lambda i: (i, 0)), pl.BlockSpec((tm, AW), lambda i: (i, 1)),
                  pl.BlockSpec((tm, AW), lambda i: (i, 0)),
                  pl.BlockSpec((tm, AW), lambda i: (i + off, 0)),
                  pl.BlockSpec((tm, AW), lambda i: (i + off, 0)),
                  pl.BlockSpec((tm, D), lambda i: (i, 0)), gspec, gspec],
        out_specs=[pl.BlockSpec((tm, DIN), lambda i: (i, 0)), gout, gout],
        out_shape=[jax.ShapeDtypeStruct((s, DIN), BF16), jax.ShapeDtypeStruct((1, 128), F32),
                   jax.ShapeDtypeStruct((1, 128), F32)],
        compiler_params=_cp("arbitrary"),
    )(p, p, dq, dkp, dvp, dcp, qg, kg)


def _loss_grad(name, y, t):
    s = y.shape[0]
    tm = 512

    def body(y_ref, t_ref, dy_ref, dyb_ref, l_ref):
        i = pl.program_id(0)
        e = y_ref[...] - t_ref[...]
        dy = e * (1.0 / D)
        dy_ref[...] = dy
        dyb_ref[...] = dy.astype(BF16)
        part = 0.5 * jnp.sum(jnp.mean(e * e, axis=-1, keepdims=True), axis=0, keepdims=True)

        @pl.when(i == 0)
        def _():
            l_ref[...] = part

        @pl.when(i > 0)
        def _():
            l_ref[...] += part

    blk = pl.BlockSpec((tm, D), lambda i: (i, 0))
    return pl.pallas_call(
        body, name=name, grid=(s // tm,),
        in_specs=[blk, blk],
        out_specs=[blk, blk, pl.BlockSpec((1, 1), lambda i: (0, 0))],
        out_shape=[jax.ShapeDtypeStruct((s, D), F32), jax.ShapeDtypeStruct((s, D), BF16),
                   jax.ShapeDtypeStruct((1, 1), F32)],
        compiler_params=_cp("arbitrary"),
    )(y, t)


def _mm_nt_relu(name, dxb, w, l, a):
    s = dxb.shape[0]
    tm, tn = 512, 1024

    def body(d_ref, w_ref, a_ref, o_ref):
        df = lax.dot_general(d_ref[...], w_ref[...], (((1,), (1,)), ((), ())), preferred_element_type=F32)
        o_ref[...] = (df * (2.0 * jnp.maximum(a_ref[...].astype(F32), 0.0))).astype(BF16)

    return pl.pallas_call(
        body, name=name, grid=(s // tm, DFF // tn),
        in_specs=[pl.BlockSpec((tm, D), lambda i, j: (i, 0)),
                  pl.BlockSpec((None, tn, D), lambda i, j: (l, j, 0)),
                  pl.BlockSpec((tm, tn), lambda i, j: (i, j))],
        out_specs=pl.BlockSpec((tm, tn), lambda i, j: (i, j)),
        out_shape=jax.ShapeDtypeStruct((s, DFF), BF16),
        compiler_params=_cp("parallel", "parallel"),
    )(dxb, w, a)


def _mm_nt(name, a, w, l, out_dtype):
    s, k = a.shape
    n = w.shape[1]
    tm = 512

    def body(a_ref, w_ref, o_ref):
        o_ref[...] = lax.dot_general(a_ref[...], w_ref[...], (((1,), (1,)), ((), ())),
                                     preferred_element_type=F32).astype(o_ref.dtype)

    return pl.pallas_call(
        body, name=name, grid=(s // tm,),
        in_specs=[pl.BlockSpec((tm, k), lambda i: (i, 0)),
                  pl.BlockSpec((None, n, k), lambda i: (l, 0, 0))],
        out_specs=pl.BlockSpec((tm, n), lambda i: (i, 0)),
        out_shape=jax.ShapeDtypeStruct((s, n), out_dtype),
        compiler_params=_cp("parallel"),
    )(a, w)


def _mm_nt_normbwd(name, gy, w, l, x, g, dres):
    s, k = gy.shape
    tm = 256

    def body(gy_ref, w_ref, x_ref, g_ref, dr_ref, dx_ref, dxb_ref, dg_ref):
        i = pl.program_id(0)
        dh = lax.dot_general(gy_ref[...], w_ref[...], (((1,), (1,)), ((), ())), preferred_element_type=F32)
        xv = x_ref[...]
        r = _inv_rms(xv)
        xn = xv * r
        dxn = dh * g_ref[...]
        dx = r * (dxn - xn * jnp.mean(dxn * xn, axis=-1, keepdims=True)) + dr_ref[...]
        dx_ref[...] = dx
        dxb_ref[...] = dx.astype(BF16)
        part = jnp.sum(dh * xn, axis=0, keepdims=True)

        @pl.when(i == 0)
        def _():
            dg_ref[...] = part

        @pl.when(i > 0)
        def _():
            dg_ref[...] += part

    blk = pl.BlockSpec((tm, D), lambda i: (i, 0))
    vec = pl.BlockSpec((1, D), lambda i: (0, 0))
    return pl.pallas_call(
        body, name=name, grid=(s // tm,),
        in_specs=[pl.BlockSpec((tm, k), lambda i: (i, 0)),
                  pl.BlockSpec((None, D, k), lambda i: (l, 0, 0)), blk, vec, blk],
        out_specs=[blk, blk, vec],
        out_shape=[jax.ShapeDtypeStruct((s, D), F32), jax.ShapeDtypeStruct((s, D), BF16),
                   jax.ShapeDtypeStruct((1, D), F32)],
        compiler_params=_cp("arbitrary"),
    )(gy, w, x, g, dres)


def _mm_tn(name, a, b, acc, l, tma, tnb):
    s, m = a.shape
    n = b.shape[1]

    def body(a_ref, b_ref, acc_ref, o_ref):
        del acc_ref
        o_ref[...] = lax.dot_general(a_ref[...], b_ref[...], (((0,), (0,)), ((), ())),
                                     preferred_element_type=F32).astype(BF16)

    return pl.pallas_call(
        body, name=name, grid=(m // tma, n // tnb),
        in_specs=[pl.BlockSpec((s, tma), lambda i, j: (0, i)),
                  pl.BlockSpec((s, tnb), lambda i, j: (0, j)), ANY],
        out_specs=pl.BlockSpec((None, tma, tnb), lambda i, j: (l, i, j)),
        out_shape=jax.ShapeDtypeStruct(acc.shape, BF16),
        input_output_aliases={2: 0},
        compiler_params=_cp("parallel", "parallel"),
    )(a, b, acc)


def _adamw(name, g, w, m, v):
    r, c = g.shape
    tm = 256 if r % 256 == 0 else r

    def body(g_ref, w_ref, m_ref, v_ref, go_ref, d_ref, mo_ref, vo_ref):
        gv = g_ref[...]
        mn = ADAM_B1 * m_ref[...] + (1.0 - ADAM_B1) * gv
        vn = ADAM_B2 * v_ref[...] + (1.0 - ADAM_B2) * jnp.square(gv)
        m_hat = mn / (1.0 - ADAM_B1 ** ADAM_STEP)
        v_hat = vn / (1.0 - ADAM_B2 ** ADAM_STEP)
        go_ref[...] = gv
        d_ref[...] = -ADAM_LR * (m_hat / (jnp.sqrt(v_hat) + ADAM_EPS) + ADAM_WD * w_ref[...])
        mo_ref[...] = mn
        vo_ref[...] = vn

    blk = pl.BlockSpec((tm, c), lambda i: (i, 0))
    return pl.pallas_call(
        body, name=name, grid=(r // tm,),
        in_specs=[blk] * 4, out_specs=[blk] * 4,
        out_shape=[jax.ShapeDtypeStruct((r, c), F32)] * 4,
        compiler_params=_cp("parallel"),
    )(g, w, m, v)


def _place():
    x, y, c = lax.axis_index("x"), lax.axis_index("y"), lax.axis_index("c")
    chips = [(1 - x, y), (x, 1 - y), (1 - x, 1 - y)]
    return x, y, c, chips


BLOCK_AXIS = (2, 1, 2, 1)
LARGE_DIMS = ((D, DIN), (D, D), (D, DFF), (DFF, D))


def _block(ref, layers, t, b):
    width = LARGE_DIMS[t][BLOCK_AXIS[t] - 1] // 4
    if BLOCK_AXIS[t] == 1:
        return ref.at[layers, pl.ds(pl.multiple_of(b * width, 16), width), :]
    return ref.at[layers, :, pl.ds(pl.multiple_of(b * width, 128), width)]


def _full_shape(t, layers, dtype):
    r, c = LARGE_DIMS[t]
    return jax.ShapeDtypeStruct((layers, r, c), dtype)


def _block_shape(t, lead, dtype):
    r, c = LARGE_DIMS[t]
    if BLOCK_AXIS[t] == 1:
        return jax.ShapeDtypeStruct(lead + (r // 4, c), dtype)
    return jax.ShapeDtypeStruct(lead + (r, c // 4), dtype)


def _gather_weights(shards):
    def body(*refs):
        s_refs, f_refs = refs[0:4], refs[4:8]
        send_sems, recv_sems, loc_sems = refs[8:11]
        x, y, c, chips = _place()
        sib = (x, y, 1 - c)
        mine, theirs = pl.ds(2 * c, 2), pl.ds(2 * (1 - c), 2)
        every = pl.ds(0, DEPTH)
        me_b = 2 * x + y

        def rcopy(k, src, dst, to):
            return pltpu.make_async_remote_copy(src_ref=src, dst_ref=dst, send_sem=send_sems.at[k],
                                                recv_sem=recv_sems.at[k], device_id=to, device_id_type=MESH)

        locs = [pltpu.make_async_copy(s_refs[t], _block(f_refs[t], every, t, me_b), loc_sems.at[t]) for t in range(4)]
        for cp in locs:
            cp.start()
        sent = []
        for t in range(4):
            for j, (cx, cy) in enumerate(chips):
                sent.append(rcopy(3 * t + j, s_refs[t].at[mine], _block(f_refs[t], mine, t, me_b), (cx, cy, c)))
                sent[-1].start()
        for t in range(4):
            for j, (cx, cy) in enumerate(chips):
                k = 3 * t + j
                landed = _block(f_refs[t], mine, t, 2 * cx + cy)
                rcopy(k, landed, landed, (cx, cy, c)).wait_recv()
                sent.append(rcopy(12 + k, landed, landed, sib))
                sent[-1].start()
        for t in range(4):
            for j, (cx, cy) in enumerate(chips):
                passed = _block(f_refs[t], theirs, t, 2 * cx + cy)
                rcopy(12 + 3 * t + j, passed, passed, sib).wait_recv()
        for cp in sent:
            cp.wait_send()
        for cp in locs:
            cp.wait()

    return pl.pallas_call(
        body, name="gather_weights",
        in_specs=[ANY] * 4, out_specs=[ANY] * 4,
        out_shape=[_full_shape(t, DEPTH, BF16) for t in range(4)],
        scratch_shapes=[pltpu.SemaphoreType.DMA((24,)), pltpu.SemaphoreType.DMA((24,)),
                        pltpu.SemaphoreType.DMA((4,))],
    )(*shards)


def _swap_halves(grads):
    def body(*refs):
        d_refs, a_refs, send_sems, recv_sems = refs[0:4], refs[4:8], refs[8], refs[9]
        x, y, c, _ = _place()
        cps = [pltpu.make_async_remote_copy(src_ref=d_refs[t].at[pl.ds(2 * (1 - c), 2)], dst_ref=a_refs[t],
                                            send_sem=send_sems.at[t], recv_sem=recv_sems.at[t],
                                            device_id=(x, y, 1 - c), device_id_type=MESH) for t in range(4)]
        for cp in cps:
            cp.start()
        for cp in cps:
            cp.wait()

    return pl.pallas_call(
        body, name="swap_halves",
        in_specs=[ANY] * 4, out_specs=[ANY] * 4,
        out_shape=[_full_shape(t, 2, BF16) for t in range(4)],
        scratch_shapes=[pltpu.SemaphoreType.DMA((4,)), pltpu.SemaphoreType.DMA((4,))],
    )(*grads)


def _add_halves(name, grad, other, c1):
    _, r, c = grad.shape
    tm = min(512, r)

    def body(c_ref, g_ref, o_ref, s_ref):
        del c_ref
        s_ref[...] = (g_ref[...].astype(F32) + o_ref[...].astype(F32)).astype(BF16)

    return pl.pallas_call(
        body, name=name,
        grid_spec=pltpu.PrefetchScalarGridSpec(
            num_scalar_prefetch=1, grid=(2, r // tm),
            in_specs=[pl.BlockSpec((None, tm, c), lambda l, i, cr: (2 * cr[0] + l, i, 0)),
                      pl.BlockSpec((None, tm, c), lambda l, i, cr: (l, i, 0))],
            out_specs=pl.BlockSpec((None, tm, c), lambda l, i, cr: (l, i, 0))),
        out_shape=jax.ShapeDtypeStruct((2, r, c), BF16),
        compiler_params=_cp("parallel", "parallel"),
    )(c1, grad, other)


def _swap_blocks(sums):
    def body(*refs):
        s_refs, r_refs, send_sems, recv_sems = refs[0:4], refs[4:8], refs[8], refs[9]
        x, y, c, chips = _place()
        both = pl.ds(0, 2)
        cps = []
        for t in range(4):
            for j, (cx, cy) in enumerate(chips):
                cps.append(pltpu.make_async_remote_copy(
                    src_ref=_block(s_refs[t], both, t, 2 * cx + cy), dst_ref=r_refs[t].at[j],
                    send_sem=send_sems.at[3 * t + j], recv_sem=recv_sems.at[3 * t + j],
                    device_id=(cx, cy, c), device_id_type=MESH))
                cps[-1].start()
        for cp in cps:
            cp.wait()

    return pl.pallas_call(
        body, name="swap_blocks",
        in_specs=[ANY] * 4, out_specs=[ANY] * 4,
        out_shape=[_block_shape(t, (3, 2), BF16) for t in range(4)],
        scratch_shapes=[pltpu.SemaphoreType.DMA((12,)), pltpu.SemaphoreType.DMA((12,))],
    )(*sums)


def _add_blocks(name, t, own, others, b1):
    _, _, rb, cb = others.shape
    tm = min(256, rb)
    if BLOCK_AXIS[t] == 1:
        own_spec = pl.BlockSpec((None, tm, cb), lambda l, i, br: (l, br[0] * (rb // tm) + i, 0))
    else:
        own_spec = pl.BlockSpec((None, tm, cb), lambda l, i, br: (l, i, br[0]))

    def body(b_ref, o_ref, r0_ref, r1_ref, r2_ref, s_ref):
        del b_ref
        s_ref[...] = ((o_ref[...].astype(F32) + r0_ref[...].astype(F32))
                      + (r1_ref[...].astype(F32) + r2_ref[...].astype(F32)))

    def got(j):
        return pl.BlockSpec((None, None, tm, cb), lambda l, i, br: (j, l, i, 0))

    return pl.pallas_call(
        body, name=name,
        grid_spec=pltpu.PrefetchScalarGridSpec(
            num_scalar_prefetch=1, grid=(2, rb // tm),
            in_specs=[own_spec, got(0), got(1), got(2)],
            out_specs=pl.BlockSpec((None, tm, cb), lambda l, i, br: (l, i, 0))),
        out_shape=jax.ShapeDtypeStruct((2, rb, cb), F32),
        compiler_params=_cp("parallel", "parallel"),
    )(b1, own, others, others, others)


def _join_halves(totals):
    def body(*refs):
        t_refs, g_refs, send_sems, recv_sems, loc_sems = refs[0:4], refs[4:8], refs[8], refs[9], refs[10]
        x, y, c, _ = _place()
        mine = pl.ds(2 * c, 2)
        locs = [pltpu.make_async_copy(t_refs[t], g_refs[t].at[mine], loc_sems.at[t]) for t in range(4)]
        cps = [pltpu.make_async_remote_copy(src_ref=t_refs[t], dst_ref=g_refs[t].at[mine],
                                            send_sem=send_sems.at[t], recv_sem=recv_sems.at[t],
                                            device_id=(x, y, 1 - c), device_id_type=MESH) for t in range(4)]
        for cp in locs + cps:
            cp.start()
        for t in range(4):
            theirs = g_refs[t].at[pl.ds(2 * (1 - c), 2)]
            pltpu.make_async_remote_copy(src_ref=theirs, dst_ref=theirs, send_sem=send_sems.at[t],
                                         recv_sem=recv_sems.at[t], device_id=(x, y, 1 - c),
                                         device_id_type=MESH).wait_recv()
        for cp in cps:
            cp.wait_send()
        for cp in locs:
            cp.wait()

    return pl.pallas_call(
        body, name="join_halves",
        in_specs=[ANY] * 4, out_specs=[ANY] * 4,
        out_shape=[_block_shape(t, (DEPTH,), F32) for t in range(4)],
        scratch_shapes=[pltpu.SemaphoreType.DMA((4,)), pltpu.SemaphoreType.DMA((4,)),
                        pltpu.SemaphoreType.DMA((4,))],
    )(*totals)


def _all_gather8(name, v):
    m_per, n = v.shape

    def body(v_ref, out_ref, send_sems, recv_sems, local_sem):
        x, y, c, chips = _place()
        me, sib = (x, y, c), (x, y, 1 - c)

        def rows(px, py, pc):
            return out_ref.at[pl.ds((4 * px + 2 * py + pc) * m_per, m_per), :]

        def copy(k, block, to, src=None):
            return pltpu.make_async_remote_copy(
                src_ref=rows(*block) if src is None else src, dst_ref=rows(*block),
                send_sem=send_sems.at[k], recv_sem=recv_sems.at[k], device_id=to, device_id_type=MESH)

        mine = pltpu.make_async_copy(v_ref, rows(*me), local_sem)
        mine.start()
        first = [copy(0, me, sib, src=v_ref)]
        first += [copy(1 + j, me, (*chip, c), src=v_ref) for j, chip in enumerate(chips)]
        for cp in first:
            cp.start()
        passed = [copy(4 + j, (*chip, c), sib) for j, chip in enumerate(chips)]
        for j, chip in enumerate(chips):
            copy(1 + j, (*chip, c), me).wait_recv()
            passed[j].start()
        copy(0, sib, me).wait_recv()
        for j, chip in enumerate(chips):
            copy(4 + j, (*chip, 1 - c), me).wait_recv()
        for cp in first + passed:
            cp.wait_send()
        mine.wait()

    return pl.pallas_call(
        body, name=name,
        out_shape=jax.ShapeDtypeStruct((8 * m_per, n), v.dtype),
        in_specs=[pl.BlockSpec(memory_space=pltpu.VMEM)],
        out_specs=pl.BlockSpec(memory_space=pltpu.VMEM),
        scratch_shapes=[pltpu.SemaphoreType.DMA((7,)), pltpu.SemaphoreType.DMA((7,)), pltpu.SemaphoreType.DMA],
    )(v)


def _sum8(name, g):
    def body(g_ref, o_ref):
        acc = g_ref[0]
        for d in range(1, 8):
            acc = acc + g_ref[d]
        o_ref[...] = acc

    return pl.pallas_call(body, name=name, out_shape=jax.ShapeDtypeStruct(g.shape[1:], F32))(g)


def _pack(parts):
    flat = []
    for a in parts:
        a = a.reshape(-1)
        flat.append(jnp.pad(a, (0, (-a.shape[0]) % 128)))
    cat = jnp.concatenate(flat)
    cat = jnp.pad(cat, (0, (-cat.shape[0]) % 1024))
    return cat.reshape(-1, 128)


def _unpack(packed, shapes):
    flat = packed.reshape(-1)
    out, at = [], 0
    for shp in shapes:
        n = 1
        for d in shp:
            n *= d
        out.append(flat[at:at + n].reshape(shp))
        at += n + (-n) % 128
    return out


def _local_step(x, target, weights, small):
    w_in, w_out, w_1, w_2 = weights
    s = x.shape[0]
    saved = []
    xin = x
    h = _rmsnorm("norm_first", x, small["norm1_g"][0:1])
    for l in range(DEPTH):
        qg = jnp.tile(small["q_norm_g"][l], 8)[None]
        kg = jnp.tile(small["k_norm_g"][l], 8)[None]
        rb = jnp.pad(small["rel_bias"][l], ((0, 0), (0, NIDX - 257)))
        bias = jnp.transpose(_bias_expand(f"bias_expand_{l}", rb), (1, 0, 2))
        cw = small["conv_w"][l]
        pwbd = jax.scipy.linalg.block_diag(*[small["pool_w"][l, g] for g in range(4)])
        ps = small["pool_scale"][l][None]
        p = _mm_nn(f"proj_in_{l}", h, w_in, l, 512, 512, F32)
        q, kp, vp = _qkv(f"qkv_{l}", p, qg, kg)
        o = _attn_fwd(f"attn_fwd_{l}", q, kp, vp, bias)
        mix = _convpool_fwd(f"convpool_fwd_{l}", p, o, cw, pwbd, ps)
        x1, h2 = _mm_res_norm(f"proj_out_{l}", mix, w_out, l, xin, small["norm2_g"][l:l + 1])
        a, f = _mm_mlp1(f"mlp1_{l}", h2, w_1, l)
        gnext = small["norm1_g"][(l + 1) % DEPTH][None]
        x2, hnext = _mm_res_norm(f"mlp2_{l}", f, w_2, l, x1, gnext)
        saved.append(dict(xin=xin, h=h, p=p, q=q, kp=kp, vp=vp, bias=bias, mix=mix, x1=x1, h2=h2, a=a, f=f,
                          qg=qg, kg=kg, cw=cw, pwbd=pwbd, ps=ps))
        xin, h = x2, hnext

    dx, dxb, loss = _loss_grad("loss_grad", xin, target)
    g_in = lax.empty((DEPTH, D, DIN), BF16)
    g_out = lax.empty((DEPTH, D, D), BF16)
    g_1 = lax.empty((DEPTH, D, DFF), BF16)
    g_2 = lax.empty((DEPTH, DFF, D), BF16)
    gs = {k: [None] * DEPTH for k in ("norm1_g", "q_norm_g", "k_norm_g", "rel_bias", "conv_w", "pool_w",
                                      "pool_scale", "norm2_g")}
    for l in reversed(range(DEPTH)):
        sv = saved[l]
        da = _mm_nt_relu(f"mlp2_bwd_{l}", dxb, w_2, l, sv["a"])
        g_2 = _mm_tn(f"mlp2_wgrad_{l}", sv["f"], dxb, g_2, l, 512, 512)
        g_1 = _mm_tn(f"mlp1_wgrad_{l}", sv["h2"], da, g_1, l, 512, 512)
        dx1, dx1b, dg2 = _mm_nt_normbwd(f"mlp1_bwd_{l}", da, w_1, l, sv["x1"], small["norm2_g"][l:l + 1], dx)
        dmix = _mm_nt(f"proj_out_bwd_{l}", dx1b, w_out, l, F32)
        g_out = _mm_tn(f"proj_out_wgrad_{l}", sv["mix"], dx1b, g_out, l, 512, 512)
        dcp, dw0, dw1, dw2, dps, dpw = _convpool_bwd(f"convpool_bwd_{l}", sv["p"], dmix, sv["cw"], sv["pwbd"], sv["ps"])
        dq, dkp, dvp, db = _attn_bwd(f"attn_bwd_{l}", sv["q"], sv["kp"], sv["vp"], sv["bias"], dmix)
        drb = _bias_reduce(f"bias_reduce_{l}", jnp.transpose(db, (1, 0, 2)))
        dp, dqg, dkg = _qkv_bwd(f"qkv_bwd_{l}", sv["p"], dq, dkp, dvp, dcp, sv["qg"], sv["kg"])
        g_in = _mm_tn(f"proj_in_wgrad_{l}", sv["h"], dp, g_in, l, 512, 640)
        dx, dxb, dg1 = _mm_nt_normbwd(f"proj_in_bwd_{l}", dp, w_in, l, sv["xin"], small["norm1_g"][l:l + 1], dx1)
        gs["norm1_g"][l] = dg1[0]
        gs["q_norm_g"][l] = dqg[0, :HD]
        gs["k_norm_g"][l] = dkg[0, :HD]
        gs["rel_bias"][l] = drb[:, :257]
        gs["conv_w"][l] = jnp.concatenate([dw0, dw1, dw2], axis=0)
        gs["pool_w"][l] = jnp.stack([dpw[g * 64:(g + 1) * 64, g * 64:(g + 1) * 64] for g in range(4)])
        gs["pool_scale"][l] = dps[0]
        gs["norm2_g"][l] = dg2[0]
    gsmall = {k: jnp.stack(v) for k, v in gs.items()}
    return loss, dx, (g_in, g_out, g_1, g_2), gsmall


SMALL = ("norm1_g", "q_norm_g", "k_norm_g", "rel_bias", "conv_w", "pool_w", "pool_scale", "norm2_g")
LARGE = ("w_in", "w_out", "w_mlp1", "w_mlp2")


def kernel(x, norm1_g, w_in, q_norm_g, k_norm_g, rel_bias, conv_w, pool_w, pool_scale, w_out, norm2_g, w_mlp1, w_mlp2, loss_target, m_norm1_g, m_w_in, m_q_norm_g, m_k_norm_g, m_rel_bias, m_conv_w, m_pool_w, m_pool_scale, m_w_out, m_norm2_g, m_w_mlp1, m_w_mlp2, v_norm1_g, v_w_in, v_q_norm_g, v_k_norm_g, v_rel_bias, v_conv_w, v_pool_w, v_pool_scale, v_w_out, v_norm2_g, v_w_mlp1, v_w_mlp2):
    w = dict(norm1_g=norm1_g, w_in=w_in, q_norm_g=q_norm_g, k_norm_g=k_norm_g, rel_bias=rel_bias, conv_w=conv_w,
             pool_w=pool_w, pool_scale=pool_scale, w_out=w_out, norm2_g=norm2_g, w_mlp1=w_mlp1, w_mlp2=w_mlp2)
    m = dict(norm1_g=m_norm1_g, w_in=m_w_in, q_norm_g=m_q_norm_g, k_norm_g=m_k_norm_g, rel_bias=m_rel_bias,
             conv_w=m_conv_w, pool_w=m_pool_w, pool_scale=m_pool_scale, w_out=m_w_out, norm2_g=m_norm2_g,
             w_mlp1=m_w_mlp1, w_mlp2=m_w_mlp2)
    v = dict(norm1_g=v_norm1_g, w_in=v_w_in, q_norm_g=v_q_norm_g, k_norm_g=v_k_norm_g, rel_bias=v_rel_bias,
             conv_w=v_conv_w, pool_w=v_pool_w, pool_scale=v_pool_scale, w_out=v_w_out, norm2_g=v_norm2_g,
             w_mlp1=v_w_mlp1, w_mlp2=v_w_mlp2)
    ax, ay, ac = lax.axis_index("x"), lax.axis_index("y"), lax.axis_index("c")
    c1 = jnp.reshape(ac, (1,)).astype(jnp.int32)
    b1 = jnp.reshape(2 * ax + ay, (1,)).astype(jnp.int32)

    shards = [_cast_bf16(f"cast_{n}", w[n].reshape(-1, w[n].shape[-1])).reshape(w[n].shape) for n in LARGE]
    full = _gather_weights(shards)
    cw_rows = _all_gather8("gather_conv_w", jnp.pad(conv_w.reshape(DEPTH * 3, 64), ((0, 4), (0, 64))))
    cw_chips = [cw_rows[(4 * cx + 2 * cy) * 16:(4 * cx + 2 * cy) * 16 + 12, :64] for cx in range(2) for cy in range(2)]
    small = {n: w[n] for n in SMALL}
    small["conv_w"] = jnp.concatenate(cw_chips, axis=1).reshape(DEPTH, 3, CW)

    loss_part, grad_x, glarge, gsmall = _local_step(x[0], loss_target[0], full, small)
    loss = lax.psum(loss_part[0, 0], ("x", "y", "c"))

    other = _swap_halves(glarge)
    sums = [_add_halves(f"add_halves_{LARGE[t]}", glarge[t], other[t], c1) for t in range(4)]
    got = _swap_blocks(sums)
    totals = [_add_blocks(f"add_blocks_{LARGE[t]}", t, sums[t], got[t], b1) for t in range(4)]
    reduced = _join_halves(totals)
    out = {}
    for t, n in enumerate(LARGE):
        shp = w[n].shape
        two = lambda a: a.reshape(-1, shp[-1])
        res = _adamw(f"adamw_{n}", two(reduced[t]), two(w[n]), two(m[n]), two(v[n]))
        out[n] = [r.reshape(shp) for r in res]

    order = [n for n in SMALL]
    packed = _pack([gsmall[n] for n in order])
    rows = packed.shape[0]
    summed = _sum8("sum_small", _all_gather8("gather_small", packed).reshape(8, rows, 128))
    gfull = dict(zip(order, _unpack(summed, [gsmall[n].shape for n in order])))
    gfull["conv_w"] = lax.dynamic_slice_in_dim(gfull["conv_w"], (2 * ax + ay) * 64, 64, axis=2)
    res = _adamw("adamw_small", _pack([gfull[n] for n in order]), _pack([w[n] for n in order]),
                 _pack([m[n] for n in order]), _pack([v[n] for n in order]))
    for n, parts in zip(order, zip(*[_unpack(r, [w[k].shape for k in order]) for r in res])):
        out[n] = list(parts)

    names = ("norm1_g", "w_in", "q_norm_g", "k_norm_g", "rel_bias", "conv_w", "pool_w", "pool_scale", "w_out",
             "norm2_g", "w_mlp1", "w_mlp2")
    flat = [loss, grad_x[None]]
    for i in range(4):
        flat += [out[n][i] for n in names]
    return tuple(flat)
```

```python
import functools

import jax
import jax.numpy as jnp
from jax import lax
from jax.experimental import pallas as pl
from jax.experimental.pallas import tpu as pltpu

F32 = jnp.float32
BF16 = jnp.bfloat16

D = 1024
DEPTH = 4
CH = 64
NPREV = 8
KB = (NPREV + 1) * CH
PADR = NPREV * CH
HD = 64
AW = 512
CW = 256
PWD = 256
DIN = 3 * AW + 3 * CW + PWD
DFF = 4 * D
NIDX = 384
EPS = 1e-6
NEG_INF = -1e30
QB = 4

ADAM_LR = 0.001
ADAM_B1 = 0.9
ADAM_B2 = 0.999
ADAM_EPS = 1e-08
ADAM_WD = 0.01
ADAM_STEP = 10

VMEM_LIMIT = 52 * 1024 * 1024
MESH = pl.DeviceIdType.MESH
ANY = pl.BlockSpec(memory_space=pl.ANY)


def _cp(*sem):
    return pltpu.CompilerParams(dimension_semantics=sem, vmem_limit_bytes=VMEM_LIMIT)


def _inv_rms(x):
    return lax.rsqrt(jnp.mean(x * x, axis=-1, keepdims=True) + EPS)


def _head_mean_matrix():
    r = lax.broadcasted_iota(jnp.int32, (AW, AW), 0) // HD
    c = lax.broadcasted_iota(jnp.int32, (AW, AW), 1) // HD
    return jnp.where(r == c, 1.0 / HD, 0.0).astype(BF16)


def _head_mean(x, hm):
    hi = x.astype(BF16)
    lo = (x - hi.astype(F32)).astype(BF16)
    return (jnp.dot(hi, hm, preferred_element_type=F32)
            + jnp.dot(lo, hm, preferred_element_type=F32))


def _rmsnorm(name, x, g):
    s = x.shape[0]
    tm = 512

    def body(x_ref, g_ref, h_ref):
        xv = x_ref[...]
        h_ref[...] = (xv * _inv_rms(xv) * g_ref[...]).astype(BF16)

    return pl.pallas_call(
        body, name=name, grid=(s // tm,),
        in_specs=[pl.BlockSpec((tm, D), lambda i: (i, 0)), pl.BlockSpec((1, D), lambda i: (0, 0))],
        out_specs=pl.BlockSpec((tm, D), lambda i: (i, 0)),
        out_shape=jax.ShapeDtypeStruct((s, D), BF16),
        compiler_params=_cp("parallel"),
    )(x, g)


def _mm_nn(name, a, w, l, tm, tn, out_dtype):
    s, k = a.shape
    n = w.shape[2]

    def body(a_ref, w_ref, o_ref):
        o_ref[...] = jnp.dot(a_ref[...], w_ref[...], preferred_element_type=F32).astype(o_ref.dtype)

    return pl.pallas_call(
        body, name=name, grid=(s // tm, n // tn),
        in_specs=[pl.BlockSpec((tm, k), lambda i, j: (i, 0)),
                  pl.BlockSpec((None, k, tn), lambda i, j: (l, 0, j))],
        out_specs=pl.BlockSpec((tm, tn), lambda i, j: (i, j)),
        out_shape=jax.ShapeDtypeStruct((s, n), out_dtype),
        compiler_params=_cp("parallel", "parallel"),
    )(a, w)


def _mm_mlp1(name, h2, w, l):
    s, k = h2.shape
    n = w.shape[2]
    tm, tn = 512, 1024

    def body(a_ref, w_ref, o_ref, f_ref):
        acc = jnp.dot(a_ref[...], w_ref[...], preferred_element_type=F32)
        o_ref[...] = acc.astype(BF16)
        f_ref[...] = jnp.square(jnp.maximum(acc, 0.0)).astype(BF16)

    return pl.pallas_call(
        body, name=name, grid=(s // tm, n // tn),
        in_specs=[pl.BlockSpec((tm, k), lambda i, j: (i, 0)),
                  pl.BlockSpec((None, k, tn), lambda i, j: (l, 0, j))],
        out_specs=[pl.BlockSpec((tm, tn), lambda i, j: (i, j))] * 2,
        out_shape=[jax.ShapeDtypeStruct((s, n), BF16)] * 2,
        compiler_params=_cp("parallel", "parallel"),
    )(h2, w)


def _mm_res_norm(name, a, w, l, res, g):
    s, k = a.shape
    tm = 256

    def body(a_ref, w_ref, r_ref, g_ref, x_ref, h_ref):
        acc = r_ref[...] + jnp.dot(a_ref[...], w_ref[...], preferred_element_type=F32)
        x_ref[...] = acc
        h_ref[...] = (acc * _inv_rms(acc) * g_ref[...]).astype(BF16)

    return pl.pallas_call(
        body, name=name, grid=(s // tm,),
        in_specs=[pl.BlockSpec((tm, k), lambda i: (i, 0)),
                  pl.BlockSpec((None, k, D), lambda i: (l, 0, 0)),
                  pl.BlockSpec((tm, D), lambda i: (i, 0)),
                  pl.BlockSpec((1, D), lambda i: (0, 0))],
        out_specs=[pl.BlockSpec((tm, D), lambda i: (i, 0))] * 2,
        out_shape=[jax.ShapeDtypeStruct((s, D), F32), jax.ShapeDtypeStruct((s, D), BF16)],
        compiler_params=_cp("parallel"),
    )(a, w, res, g)


def _qkv(name, p, qg, kg):
    s = p.shape[0]
    tm = PADR
    nb = s // tm

    def body(pq_ref, pk_ref, pv_ref, qg_ref, kg_ref, q_ref, k_ref, v_ref):
        t = pl.program_id(0)
        hm = _head_mean_matrix()

        def nrm(x, g):
            return x * lax.rsqrt(_head_mean(x * x, hm) + EPS) * g

        q_ref[...] = (nrm(pq_ref[...], qg_ref[...]) * 0.125).astype(BF16)
        kk = nrm(pk_ref[...], kg_ref[...]).astype(BF16)
        vv = pv_ref[...].astype(BF16)
        first = t == 0
        k_ref[...] = jnp.where(first, jnp.zeros_like(kk), kk)
        v_ref[...] = jnp.where(first, jnp.zeros_like(vv), vv)

    def src(col):
        return pl.BlockSpec((tm, AW), lambda t: (jnp.maximum(t - 1, 0), col))

    gspec = pl.BlockSpec((1, AW), lambda t: (0, 0))
    return pl.pallas_call(
        body, name=name, grid=(nb + 1,),
        in_specs=[src(0), src(1), src(2), gspec, gspec],
        out_specs=[pl.BlockSpec((tm, AW), lambda t: (jnp.maximum(t - 1, 0), 0)),
                   pl.BlockSpec((tm, AW), lambda t: (t, 0)),
                   pl.BlockSpec((tm, AW), lambda t: (t, 0))],
        out_shape=[jax.ShapeDtypeStruct((s, AW), BF16),
                   jax.ShapeDtypeStruct((s + PADR, AW), BF16),
                   jax.ShapeDtypeStruct((s + PADR, AW), BF16)],
        compiler_params=_cp("arbitrary"),
    )(p, p, p, qg, kg)


def _bias_onehot(qi):
    idx = lax.broadcasted_iota(jnp.int32, (NIDX, KB), 0)
    kj = lax.broadcasted_iota(jnp.int32, (NIDX, KB), 1)
    rel = jnp.clip(qi + PADR - kj, -128, 128) + 128
    return jnp.where(rel == idx, 1.0, 0.0).astype(F32)


def _bias_expand(name, rb):
    def body(rb_ref, o_ref):
        def step(qi, carry):
            o_ref[qi] = jnp.dot(rb_ref[...], _bias_onehot(qi), preferred_element_type=F32,
                                precision=lax.Precision.HIGHEST)
            return carry

        lax.fori_loop(0, CH, step, 0)

    return pl.pallas_call(
        body, name=name,
        out_shape=jax.ShapeDtypeStruct((CH, 8, KB), F32),
    )(rb)


def _bias_reduce(name, db):
    def body(db_ref, o_ref):
        def step(qi, acc):
            return acc + lax.dot_general(db_ref[qi], _bias_onehot(qi), (((1,), (1,)), ((), ())),
                                         preferred_element_type=F32, precision=lax.Precision.HIGHEST)

        o_ref[...] = lax.fori_loop(0, CH, step, jnp.zeros((8, NIDX), F32))

    return pl.pallas_call(
        body, name=name,
        out_shape=jax.ShapeDtypeStruct((8, NIDX), F32),
    )(db)


def _softmax_band(qm, k, bias, valid):
    s = lax.dot_general(qm, k, (((1,), (1,)), ((), ())), preferred_element_type=F32)
    s = jnp.where(valid, s + bias, NEG_INF)
    e = jnp.exp(s - jnp.max(s, axis=-1, keepdims=True))
    return e / jnp.sum(e, axis=-1, keepdims=True)


def _attn_fwd(name, q, kp, vp, bias):
    s = q.shape[0]
    tq = QB * CH

    def body(q_ref, k_ref, v_ref, b_ref, o_ref):
        g = pl.program_id(1)
        lane = lax.broadcasted_iota(jnp.int32, (CH, 128), 1)
        kj = lax.broadcasted_iota(jnp.int32, (CH, KB), 1)
        for c in range(QB):
            n = g * QB + c
            start = pl.multiple_of(n * CH, CH)
            k = k_ref[pl.ds(start, KB), :]
            v = v_ref[pl.ds(start, KB), :]
            qc = q_ref[c * CH:(c + 1) * CH, :]
            valid = (kj + n * CH) >= PADR
            outs = []
            for hh in range(2):
                own = (lane < HD) if hh == 0 else (lane >= HD)
                pr = _softmax_band(jnp.where(own, qc, jnp.zeros_like(qc)), k, b_ref[hh], valid)
                outs.append(jnp.dot(pr.astype(BF16), v, preferred_element_type=F32))
            o_ref[c * CH:(c + 1) * CH, :] = jnp.where(lane < HD, outs[0], outs[1]).astype(BF16)

    return pl.pallas_call(
        body, name=name, grid=(AW // 128, s // tq),
        in_specs=[pl.BlockSpec((tq, 128), lambda h, g: (g, h)),
                  pl.BlockSpec((s + PADR, 128), lambda h, g: (0, h)),
                  pl.BlockSpec((s + PADR, 128), lambda h, g: (0, h)),
                  pl.BlockSpec((2, CH, KB), lambda h, g: (h, 0, 0))],
        out_specs=pl.BlockSpec((tq, 128), lambda h, g: (g, h)),
        out_shape=jax.ShapeDtypeStruct((s, AW), BF16),
        compiler_params=_cp("parallel", "arbitrary"),
    )(q, kp, vp, bias)


def _attn_bwd(name, q, kp, vp, bias, dmix):
    s = q.shape[0]
    tq = QB * CH

    def body(q_ref, k_ref, v_ref, b_ref, do_ref, dq_ref, dk_ref, dv_ref, db_ref):
        g = pl.program_id(1)

        @pl.when(g == 0)
        def _():
            dk_ref[...] = jnp.zeros_like(dk_ref)
            dv_ref[...] = jnp.zeros_like(dv_ref)
            db_ref[...] = jnp.zeros_like(db_ref)

        lane = lax.broadcasted_iota(jnp.int32, (CH, 128), 1)
        kj = lax.broadcasted_iota(jnp.int32, (CH, KB), 1)
        tn = (((0,), (0,)), ((), ()))
        for c in range(QB):
            n = g * QB + c
            start = pl.multiple_of(n * CH, CH)
            k = k_ref[pl.ds(start, KB), :]
            v = v_ref[pl.ds(start, KB), :]
            qc = q_ref[c * CH:(c + 1) * CH, :]
            doc = do_ref[c * CH:(c + 1) * CH, :].astype(BF16)
            valid = (kj + n * CH) >= PADR
            dqs = []
            dk_acc = jnp.zeros((KB, 128), F32)
            dv_acc = jnp.zeros((KB, 128), F32)
            for hh in range(2):
                own = (lane < HD) if hh == 0 else (lane >= HD)
                qm = jnp.where(own, qc, jnp.zeros_like(qc))
                dom = jnp.where(own, doc, jnp.zeros_like(doc))
                pr = _softmax_band(qm, k, b_ref[hh], valid)
                dpr = lax.dot_general(dom, v, (((1,), (1,)), ((), ())), preferred_element_type=F32)
                ds = pr * (dpr - jnp.sum(dpr * pr, axis=-1, keepdims=True))
                db_ref[hh] += ds
                dsb = ds.astype(BF16)
                dqs.append(jnp.dot(dsb, k, preferred_element_type=F32))
                dk_acc = dk_acc + lax.dot_general(dsb, qm, tn, preferred_element_type=F32)
                dv_acc = dv_acc + lax.dot_general(pr.astype(BF16), dom, tn, preferred_element_type=F32)
            dq_ref[c * CH:(c + 1) * CH, :] = jnp.where(lane < HD, dqs[0], dqs[1])
            dk_ref[pl.ds(start, KB), :] += dk_acc
            dv_ref[pl.ds(start, KB), :] += dv_acc

    return pl.pallas_call(
        body, name=name, grid=(AW // 128, s // tq),
        in_specs=[pl.BlockSpec((tq, 128), lambda h, g: (g, h)),
                  pl.BlockSpec((s + PADR, 128), lambda h, g: (0, h)),
                  pl.BlockSpec((s + PADR, 128), lambda h, g: (0, h)),
                  pl.BlockSpec((2, CH, KB), lambda h, g: (h, 0, 0)),
                  pl.BlockSpec((tq, 128), lambda h, g: (g, h))],
        out_specs=[pl.BlockSpec((tq, 128), lambda h, g: (g, h)),
                   pl.BlockSpec((s + PADR, 128), lambda h, g: (0, h)),
                   pl.BlockSpec((s + PADR, 128), lambda h, g: (0, h)),
                   pl.BlockSpec((2, CH, KB), lambda h, g: (h, 0, 0))],
        out_shape=[jax.ShapeDtypeStruct((s, AW), F32),
                   jax.ShapeDtypeStruct((s + PADR, AW), F32),
                   jax.ShapeDtypeStruct((s + PADR, AW), F32),
                   jax.ShapeDtypeStruct((8, CH, KB), F32)],
        compiler_params=_cp("parallel", "arbitrary"),
    )(q, kp, vp, bias, dmix)


def _rows_before(cur, prev, k):
    row = lax.broadcasted_iota(jnp.int32, cur.shape, 0)
    return jnp.where(row >= k, pltpu.roll(cur, k, 0), pltpu.roll(prev, k, 0))


def _rows_after(cur, nxt, k):
    n = cur.shape[0]
    row = lax.broadcasted_iota(jnp.int32, cur.shape, 0)
    return jnp.where(row < n - k, pltpu.roll(cur, n - k, 0), pltpu.roll(nxt, n - k, 0))


def _pool_window_lanes():
    lg = lax.broadcasted_iota(jnp.int32, (1, PWD), 1) // 64
    return lg, jnp.where(lg == 0, 2.0, jnp.where(lg == 1, 4.0, jnp.where(lg == 2, 8.0, 16.0))).astype(F32)


def _pool_mean_minus_token(u, up, row0):
    lg, wv = _pool_window_lanes()
    sums = []
    c, p = u, up
    for k in (1, 2, 4, 8):
        c2 = c + _rows_before(c, p, k)
        p = p + pltpu.roll(p, k, 0)
        c = c2
        sums.append(c)
    win = jnp.where(lg == 0, sums[0], jnp.where(lg == 1, sums[1], jnp.where(lg == 2, sums[2], sums[3])))
    pos1 = (row0 + lax.broadcasted_iota(jnp.int32, u.shape, 0) + 1).astype(F32)
    cnt = jnp.minimum(pos1, wv)
    return win / cnt - u, cnt


def _conv_taps(z, zp, w0, w1, w2):
    z1 = _rows_before(z, zp, 1)
    z2 = _rows_before(z, zp, 2)
    return (w0 * z2 + w1 * z1) + w2 * z, z1, z2


CP_TM = 512


def _convpool_fwd(name, p, o, cw, pwbd, ps):
    s = p.shape[0]
    tm = CP_TM
    nb = s // tm

    def body(gb_ref, gc_ref, hin_ref, u_ref, gcp_ref, hinp_ref, up_ref, o_ref, cw_ref, pw_ref, ps_ref, mix_ref):
        i = pl.program_id(0)
        has_prev = i > 0
        z = gc_ref[...] * hin_ref[...]
        zp = jnp.where(has_prev, gcp_ref[...] * hinp_ref[...], 0.0)
        y3, _, _ = _conv_taps(z, zp, cw_ref[0:1, :], cw_ref[1:2, :], cw_ref[2:3, :])
        m, _ = _pool_mean_minus_token(u_ref[...], jnp.where(has_prev, up_ref[...], 0.0), i * tm)
        yp = jnp.dot(m.astype(BF16), pw_ref[...].astype(BF16), preferred_element_type=F32) * ps_ref[...]
        mix_ref[:, 0:AW] = o_ref[...]
        mix_ref[:, AW:AW + CW] = (gb_ref[...] * y3).astype(BF16)
        mix_ref[:, AW + CW:D] = yp.astype(BF16)

    def cur(col):
        return pl.BlockSpec((tm, CW), lambda i: (i, col))

    def prev(col):
        return pl.BlockSpec((tm, CW), lambda i: (jnp.maximum(i - 1, 0), col))

    def whole(a):
        return pl.BlockSpec(a.shape, lambda i: (0,) * a.ndim)

    return pl.pallas_call(
        body, name=name, grid=(nb,),
        in_specs=[cur(6), cur(7), cur(8), cur(9), prev(7), prev(8), prev(9),
                  pl.BlockSpec((tm, AW), lambda i: (i, 0)), whole(cw), whole(pwbd), whole(ps)],
        out_specs=pl.BlockSpec((tm, D), lambda i: (i, 0)),
        out_shape=jax.ShapeDtypeStruct((s, D), BF16),
        compiler_params=_cp("parallel"),
    )(p, p, p, p, p, p, p, o, cw, pwbd, ps)


def _convpool_bwd(name, p, dmix, cw, pwbd, ps):
    s = p.shape[0]
    tm = CP_TM
    nb = s // tm

    def body(gb_ref, gc_ref, hin_ref, u_ref, gcp_ref, hinp_ref, up_ref, gbn_ref, dyc_ref, dyp_ref, dycn_ref, dypn_ref,
             cw_ref, pw_ref, ps_ref, dcp_ref, dw0_ref, dw1_ref, dw2_ref, dps_ref, dpw_ref):
        i = pl.program_id(0)
        has_prev = i > 0
        has_next = i < nb - 1
        w0, w1, w2 = cw_ref[0:1, :], cw_ref[1:2, :], cw_ref[2:3, :]
        gb, gc, hin = gb_ref[...], gc_ref[...], hin_ref[...]
        dyc = dyc_ref[...]
        z = gc * hin
        zp = jnp.where(has_prev, gcp_ref[...] * hinp_ref[...], 0.0)
        y3, z1, z2 = _conv_taps(z, zp, w0, w1, w2)
        dy3 = dyc * gb
        dy3n = jnp.where(has_next, dycn_ref[...] * gbn_ref[...], 0.0)
        dz = w2 * dy3 + w1 * _rows_after(dy3, dy3n, 1) + w0 * _rows_after(dy3, dy3n, 2)
        pw = pw_ref[...].astype(BF16)
        psv = ps_ref[...]
        m, cnt = _pool_mean_minus_token(u_ref[...], jnp.where(has_prev, up_ref[...], 0.0), i * tm)
        mb = m.astype(BF16)
        dyp = dyp_ref[...]
        dmp = (dyp * psv).astype(BF16)
        dmpn = jnp.where(has_next, dypn_ref[...] * psv, 0.0).astype(BF16)
        nt = (((1,), (1,)), ((), ()))
        dm = lax.dot_general(dmp, pw, nt, preferred_element_type=F32)
        dmn = lax.dot_general(dmpn, pw, nt, preferred_element_type=F32)
        lg, wv = _pool_window_lanes()
        cc, cn = dm / cnt, dmn / wv
        sums = []
        for k in (1, 2, 4, 8):
            c2 = cc + _rows_after(cc, cn, k)
            cn = cn + pltpu.roll(cn, tm - k, 0)
            cc = c2
            sums.append(cc)
        du = jnp.where(lg == 0, sums[0], jnp.where(lg == 1, sums[1], jnp.where(lg == 2, sums[2], sums[3]))) - dm
        dcp_ref[:, 0:CW] = (dyc * y3).astype(BF16)
        dcp_ref[:, CW:2 * CW] = (dz * hin).astype(BF16)
        dcp_ref[:, 2 * CW:3 * CW] = (dz * gc).astype(BF16)
        dcp_ref[:, 3 * CW:4 * CW] = du.astype(BF16)
        parts = (jnp.sum(dy3 * z2, axis=0, keepdims=True),
                 jnp.sum(dy3 * z1, axis=0, keepdims=True),
                 jnp.sum(dy3 * z, axis=0, keepdims=True),
                 jnp.sum(dyp * jnp.dot(mb, pw, preferred_element_type=F32), axis=0, keepdims=True),
                 lax.dot_general(mb, dmp, (((0,), (0,)), ((), ())), preferred_element_type=F32))
        accs = (dw0_ref, dw1_ref, dw2_ref, dps_ref, dpw_ref)

        @pl.when(i == 0)
        def _():
            for a, v in zip(accs, parts):
                a[...] = v

        @pl.when(i > 0)
        def _():
            for a, v in zip(accs, parts):
                a[...] += v

    def cur(col):
        return pl.BlockSpec((tm, CW), lambda i: (i, col))

    def prev(col):
        return pl.BlockSpec((tm, CW), lambda i: (jnp.maximum(i - 1, 0), col))

    def nxt(col):
        return pl.BlockSpec((tm, CW), lambda i: (jnp.minimum(i + 1, nb - 1), col))

    def whole(shape):
        return pl.BlockSpec(shape, lambda i: (0,) * len(shape))

    row = jax.ShapeDtypeStruct((1, CW), F32)
    return pl.pallas_call(
        body, name=name, grid=(nb,),
        in_specs=[cur(6), cur(7), cur(8), cur(9), prev(7), prev(8), prev(9), nxt(6),
                  cur(2), cur(3), nxt(2), nxt(3), whole(cw.shape), whole(pwbd.shape), whole(ps.shape)],
        out_specs=[pl.BlockSpec((tm, D), lambda i: (i, 0)), whole((1, CW)), whole((1, CW)), whole((1, CW)),
                   whole((1, PWD)), whole((PWD, PWD))],
        out_shape=[jax.ShapeDtypeStruct((s, D), BF16), row, row, row, row,
                   jax.ShapeDtypeStruct((PWD, PWD), F32)],
        compiler_params=_cp("arbitrary"),
    )(p, p, p, p, p, p, p, p, dmix, dmix, dmix, dmix, cw, pwbd, ps)


def _qkv_bwd(name, p, dq, dkp, dvp, dcp, qg, kg):
    s = p.shape[0]
    tm = 256
    off = PADR // tm

    def body(pq_ref, pk_ref, dq_ref, dk_ref, dv_ref, dcp_ref, qg_ref, kg_ref, dp_ref, dqg_ref, dkg_ref):
        i = pl.program_id(0)
        hm = _head_mean_matrix()

        def nrm_bwd(x, g, dy):
            r = lax.rsqrt(_head_mean(x * x, hm) + EPS)
            xn = x * r
            dxn = dy * g
            dx = r * (dxn - xn * _head_mean(dxn * xn, hm))
            dg = jnp.sum(dy * xn, axis=0, keepdims=True)
            dg = (dg[:, 0:128] + dg[:, 128:256]) + (dg[:, 256:384] + dg[:, 384:512])
            return dx, dg + pltpu.roll(dg, HD, 1)

        dxq, dgq = nrm_bwd(pq_ref[...], qg_ref[...], dq_ref[...] * 0.125)
        dxk, dgk = nrm_bwd(pk_ref[...], kg_ref[...], dk_ref[...])
        dp_ref[:, 0:AW] = dxq.astype(BF16)
        dp_ref[:, AW:2 * AW] = dxk.astype(BF16)
        dp_ref[:, 2 * AW:3 * AW] = dv_ref[...].astype(BF16)
        dp_ref[:, 3 * AW:DIN] = dcp_ref[...]

        @pl.when(i == 0)
        def _():
            dqg_ref[...] = dgq
            dkg_ref[...] = dgk

        @pl.when(i > 0)
        def _():
            dqg_ref[...] += dgq
            dkg_ref[...] += dgk

    gspec = pl.BlockSpec((1, AW), lambda i: (0, 0))
    gout = pl.BlockSpec((1, 128), lambda i: (0, 0))
    return pl.pallas_call(
        body, name=name, grid=(s // tm,),
        in_specs=[pl.BlockSpec((tm, AW), lambda i: (i, 0)), pl.BlockSpec((tm, AW), lambda i: (i, 1)),
                  pl.BlockSpec((tm, AW), lambda i: (i, 0)),
                  pl.BlockSpec((tm, AW), lambda i: (i + off, 0)),
                  pl.BlockSpec((tm, AW), lambda i: (i + off, 0)),
                  pl.BlockSpec((tm, D), lambda i: (i, 0)), gspec, gspec],
        out_specs=[pl.BlockSpec((tm, DIN), lambda i: (i, 0)), gout, gout],
        out_shape=[jax.ShapeDtypeStruct((s, DIN), BF16), jax.ShapeDtypeStruct((1, 128), F32),
                   jax.ShapeDtypeStruct((1, 128), F32)],
        compiler_params=_cp("arbitrary"),
    )(p, p, dq, dkp, dvp, dcp, qg, kg)


def _loss_grad(name, y, t):
    s = y.shape[0]
    tm = 512

    def body(y_ref, t_ref, dy_ref, dyb_ref, l_ref):
        i = pl.program_id(0)
        e = y_ref[...] - t_ref[...]
        dy = e * (1.0 / D)
        dy_ref[...] = dy
        dyb_ref[...] = dy.astype(BF16)
        part = 0.5 * jnp.sum(jnp.mean(e * e, axis=-1, keepdims=True), axis=0, keepdims=True)

        @pl.when(i == 0)
        def _():
            l_ref[...] = part

        @pl.when(i > 0)
        def _():
            l_ref[...] += part

    blk = pl.BlockSpec((tm, D), lambda i: (i, 0))
    return pl.pallas_call(
        body, name=name, grid=(s // tm,),
        in_specs=[blk, blk],
        out_specs=[blk, blk, pl.BlockSpec((1, 1), lambda i: (0, 0))],
        out_shape=[jax.ShapeDtypeStruct((s, D), F32), jax.ShapeDtypeStruct((s, D), BF16),
                   jax.ShapeDtypeStruct((1, 1), F32)],
        compiler_params=_cp("arbitrary"),
    )(y, t)


def _mm_nt_relu(name, dxb, w, l, a):
    s = dxb.shape[0]
    tm, tn = 512, 1024

    def body(d_ref, w_ref, a_ref, o_ref):
        df = lax.dot_general(d_ref[...], w_ref[...], (((1,), (1,)), ((), ())), preferred_element_type=F32)
        o_ref[...] = (df * (2.0 * jnp.maximum(a_ref[...].astype(F32), 0.0))).astype(BF16)

    return pl.pallas_call(
        body, name=name, grid=(s // tm, DFF // tn),
        in_specs=[pl.BlockSpec((tm, D), lambda i, j: (i, 0)),
                  pl.BlockSpec((None, tn, D), lambda i, j: (l, j, 0)),
                  pl.BlockSpec((tm, tn), lambda i, j: (i, j))],
        out_specs=pl.BlockSpec((tm, tn), lambda i, j: (i, j)),
        out_shape=jax.ShapeDtypeStruct((s, DFF), BF16),
        compiler_params=_cp("parallel", "parallel"),
    )(dxb, w, a)


def _mm_nt(name, a, w, l, out_dtype):
    s, k = a.shape
    n = w.shape[1]
    tm = 512

    def body(a_ref, w_ref, o_ref):
        o_ref[...] = lax.dot_general(a_ref[...], w_ref[...], (((1,), (1,)), ((), ())),
                                     preferred_element_type=F32).astype(o_ref.dtype)

    return pl.pallas_call(
        body, name=name, grid=(s // tm,),
        in_specs=[pl.BlockSpec((tm, k), lambda i: (i, 0)),
                  pl.BlockSpec((None, n, k), lambda i: (l, 0, 0))],
        out_specs=pl.BlockSpec((tm, n), lambda i: (i, 0)),
        out_shape=jax.ShapeDtypeStruct((s, n), out_dtype),
        compiler_params=_cp("parallel"),
    )(a, w)


def _mm_nt_normbwd(name, gy, w, l, x, g, dres):
    s, k = gy.shape
    tm = 256

    def body(gy_ref, w_ref, x_ref, g_ref, dr_ref, dx_ref, dxb_ref, dg_ref):
        i = pl.program_id(0)
        dh = lax.dot_general(gy_ref[...], w_ref[...], (((1,), (1,)), ((), ())), preferred_element_type=F32)
        xv = x_ref[...]
        r = _inv_rms(xv)
        xn = xv * r
        dxn = dh * g_ref[...]
        dx = r * (dxn - xn * jnp.mean(dxn * xn, axis=-1, keepdims=True)) + dr_ref[...]
        dx_ref[...] = dx
        dxb_ref[...] = dx.astype(BF16)
        part = jnp.sum(dh * xn, axis=0, keepdims=True)

        @pl.when(i == 0)
        def _():
            dg_ref[...] = part

        @pl.when(i > 0)
        def _():
            dg_ref[...] += part

    blk = pl.BlockSpec((tm, D), lambda i: (i, 0))
    vec = pl.BlockSpec((1, D), lambda i: (0, 0))
    return pl.pallas_call(
        body, name=name, grid=(s // tm,),
        in_specs=[pl.BlockSpec((tm, k), lambda i: (i, 0)),
                  pl.BlockSpec((None, D, k), lambda i: (l, 0, 0)), blk, vec, blk],
        out_specs=[blk, blk, vec],
        out_shape=[jax.ShapeDtypeStruct((s, D), F32), jax.ShapeDtypeStruct((s, D), BF16),
                   jax.ShapeDtypeStruct((1, D), F32)],
        compiler_params=_cp("arbitrary"),
    )(gy, w, x, g, dres)


def _mm_tn(name, a, b, acc, l, tma, tnb):
    s, m = a.shape
    n = b.shape[1]

    def body(a_ref, b_ref, acc_ref, o_ref):
        del acc_ref
        o_ref[...] = lax.dot_general(a_ref[...], b_ref[...], (((0,), (0,)), ((), ())),
                                     preferred_element_type=F32).astype(BF16)

    return pl.pallas_call(
        body, name=name, grid=(m // tma, n // tnb),
        in_specs=[pl.BlockSpec((s, tma), lambda i, j: (0, i)),
                  pl.BlockSpec((s, tnb), lambda i, j: (0, j)), ANY],
        out_specs=pl.BlockSpec((None, tma, tnb), lambda i, j: (l, i, j)),
        out_shape=jax.ShapeDtypeStruct(acc.shape, BF16),
        input_output_aliases={2: 0},
        compiler_params=_cp("parallel", "parallel"),
    )(a, b, acc)


def _adamw(name, g, w, m, v):
    r, c = g.shape
    tm = 256 if r % 256 == 0 else r

    def body(g_ref, w_ref, m_ref, v_ref, go_ref, d_ref, mo_ref, vo_ref):
        gv = g_ref[...]
        mn = ADAM_B1 * m_ref[...] + (1.0 - ADAM_B1) * gv
        vn = ADAM_B2 * v_ref[...] + (1.0 - ADAM_B2) * jnp.square(gv)
        m_hat = mn / (1.0 - ADAM_B1 ** ADAM_STEP)
        v_hat = vn / (1.0 - ADAM_B2 ** ADAM_STEP)
        go_ref[...] = gv
        d_ref[...] = -ADAM_LR * (m_hat / (jnp.sqrt(v_hat) + ADAM_EPS) + ADAM_WD * w_ref[...])
        mo_ref[...] = mn
        vo_ref[...] = vn

    blk = pl.BlockSpec((tm, c), lambda i: (i, 0))
    return pl.pallas_call(
        body, name=name, grid=(r // tm,),
        in_specs=[blk] * 4, out_specs=[blk] * 4,
        out_shape=[jax.ShapeDtypeStruct((r, c), F32)] * 4,
        compiler_params=_cp("parallel"),
    )(g, w, m, v)


def _place():
    x, y, c = lax.axis_index("x"), lax.axis_index("y"), lax.axis_index("c")
    chips = [(1 - x, y), (x, 1 - y), (1 - x, 1 - y)]
    return x, y, c, chips


BLOCK_AXIS = (2, 1, 2, 1)
LARGE_DIMS = ((D, DIN), (D, D), (D, DFF), (DFF, D))


def _block(ref, layers, t, b):
    width = LARGE_DIMS[t][BLOCK_AXIS[t] - 1] // 4
    if BLOCK_AXIS[t] == 1:
        return ref.at[layers, pl.ds(pl.multiple_of(b * width, 16), width), :]
    return ref.at[layers, :, pl.ds(pl.multiple_of(b * width, 128), width)]


def _full_shape(t, layers, dtype):
    r, c = LARGE_DIMS[t]
    return jax.ShapeDtypeStruct((layers, r, c), dtype)


def _block_shape(t, lead, dtype):
    r, c = LARGE_DIMS[t]
    if BLOCK_AXIS[t] == 1:
        return jax.ShapeDtypeStruct(lead + (r // 4, c), dtype)
    return jax.ShapeDtypeStruct(lead + (r, c // 4), dtype)


def _cast_into_full(name, t, shard, b1):
    _, r, c = shard.shape
    tm = min(256, r)
    if BLOCK_AXIS[t] == 1:
        out_spec = pl.BlockSpec((None, tm, c), lambda l, i, br: (l, br[0] * (r // tm) + i, 0))
    else:
        out_spec = pl.BlockSpec((None, tm, c), lambda l, i, br: (l, i, br[0]))

    def body(b_ref, x_ref, o_ref):
        del b_ref
        o_ref[...] = x_ref[...].astype(BF16)

    return pl.pallas_call(
        body, name=name,
        grid_spec=pltpu.PrefetchScalarGridSpec(
            num_scalar_prefetch=1, grid=(DEPTH, r // tm),
            in_specs=[pl.BlockSpec((None, tm, c), lambda l, i, br: (l, i, 0))],
            out_specs=out_spec),
        out_shape=_full_shape(t, DEPTH, BF16),
        compiler_params=_cp("parallel", "parallel"),
    )(b1, shard)


def _gather_weights(fulls):
    def body(*refs):
        f_refs = refs[4:8]
        send_sems, recv_sems = refs[8:10]
        x, y, c, chips = _place()
        sib = (x, y, 1 - c)
        mine, theirs = pl.ds(2 * c, 2), pl.ds(2 * (1 - c), 2)
        me_b = 2 * x + y

        def rcopy(k, src, dst, to):
            return pltpu.make_async_remote_copy(src_ref=src, dst_ref=dst, send_sem=send_sems.at[k],
                                                recv_sem=recv_sems.at[k], device_id=to, device_id_type=MESH)

        sent = []
        for t in range(4):
            own = _block(f_refs[t], mine, t, me_b)
            for j, (cx, cy) in enumerate(chips):
                sent.append(rcopy(3 * t + j, own, own, (cx, cy, c)))
                sent[-1].start()
        for t in range(4):
            for j, (cx, cy) in enumerate(chips):
                k = 3 * t + j
                landed = _block(f_refs[t], mine, t, 2 * cx + cy)
                rcopy(k, landed, landed, (cx, cy, c)).wait_recv()
                sent.append(rcopy(12 + k, landed, landed, sib))
                sent[-1].start()
        for t in range(4):
            for j, (cx, cy) in enumerate(chips):
                passed = _block(f_refs[t], theirs, t, 2 * cx + cy)
                rcopy(12 + 3 * t + j, passed, passed, sib).wait_recv()
        for cp in sent:
            cp.wait_send()

    return pl.pallas_call(
        body, name="gather_weights",
        in_specs=[ANY] * 4, out_specs=[ANY] * 4,
        out_shape=[_full_shape(t, DEPTH, BF16) for t in range(4)],
        input_output_aliases={t: t for t in range(4)},
        scratch_shapes=[pltpu.SemaphoreType.DMA((24,)), pltpu.SemaphoreType.DMA((24,))],
    )(*fulls)


def _swap_halves(grads):
    def body(*refs):
        d_refs, a_refs, send_sems, recv_sems = refs[0:4], refs[4:8], refs[8], refs[9]
        x, y, c, _ = _place()
        cps = [pltpu.make_async_remote_copy(src_ref=d_refs[t].at[pl.ds(2 * (1 - c), 2)], dst_ref=a_refs[t],
                                            send_sem=send_sems.at[t], recv_sem=recv_sems.at[t],
                                            device_id=(x, y, 1 - c), device_id_type=MESH) for t in range(4)]
        for cp in cps:
            cp.start()
        for cp in cps:
            cp.wait()

    return pl.pallas_call(
        body, name="swap_halves",
        in_specs=[ANY] * 4, out_specs=[ANY] * 4,
        out_shape=[_full_shape(t, 2, BF16) for t in range(4)],
        scratch_shapes=[pltpu.SemaphoreType.DMA((4,)), pltpu.SemaphoreType.DMA((4,))],
    )(*grads)


def _add_halves(name, grad, other, c1):
    _, r, c = grad.shape
    tm = min(512, r)

    def body(c_ref, g_ref, o_ref, s_ref):
        del c_ref
        s_ref[...] = (g_ref[...].astype(F32) + o_ref[...].astype(F32)).astype(BF16)

    return pl.pallas_call(
        body, name=name,
        grid_spec=pltpu.PrefetchScalarGridSpec(
            num_scalar_prefetch=1, grid=(2, r // tm),
            in_specs=[pl.BlockSpec((None, tm, c), lambda l, i, cr: (2 * cr[0] + l, i, 0)),
                      pl.BlockSpec((None, tm, c), lambda l, i, cr: (l, i, 0))],
            out_specs=pl.BlockSpec((None, tm, c), lambda l, i, cr: (l, i, 0))),
        out_shape=jax.ShapeDtypeStruct((2, r, c), BF16),
        compiler_params=_cp("parallel", "parallel"),
    )(c1, grad, other)


def _swap_blocks(sums):
    def body(*refs):
        s_refs, r_refs, send_sems, recv_sems = refs[0:4], refs[4:8], refs[8], refs[9]
        x, y, c, chips = _place()
        both = pl.ds(0, 2)
        cps = []
        for t in range(4):
            for j, (cx, cy) in enumerate(chips):
                cps.append(pltpu.make_async_remote_copy(
                    src_ref=_block(s_refs[t], both, t, 2 * cx + cy), dst_ref=r_refs[t].at[j],
                    send_sem=send_sems.at[3 * t + j], recv_sem=recv_sems.at[3 * t + j],
                    device_id=(cx, cy, c), device_id_type=MESH))
                cps[-1].start()
        for cp in cps:
            cp.wait()

    return pl.pallas_call(
        body, name="swap_blocks",
        in_specs=[ANY] * 4, out_specs=[ANY] * 4,
        out_shape=[_block_shape(t, (3, 2), BF16) for t in range(4)],
        scratch_shapes=[pltpu.SemaphoreType.DMA((12,)), pltpu.SemaphoreType.DMA((12,))],
    )(*sums)


def _add_blocks(name, t, own, others, bc):
    _, _, rb, cb = others.shape
    tm = min(256, rb)
    if BLOCK_AXIS[t] == 1:
        own_spec = pl.BlockSpec((None, tm, cb), lambda l, i, br: (l, br[0] * (rb // tm) + i, 0))
    else:
        own_spec = pl.BlockSpec((None, tm, cb), lambda l, i, br: (l, i, br[0]))

    def body(b_ref, o_ref, r0_ref, r1_ref, r2_ref, s_ref):
        del b_ref
        s_ref[...] = ((o_ref[...].astype(F32) + r0_ref[...].astype(F32))
                      + (r1_ref[...].astype(F32) + r2_ref[...].astype(F32)))

    def got(j):
        return pl.BlockSpec((None, None, tm, cb), lambda l, i, br: (j, l, i, 0))

    return pl.pallas_call(
        body, name=name,
        grid_spec=pltpu.PrefetchScalarGridSpec(
            num_scalar_prefetch=1, grid=(2, rb // tm),
            in_specs=[own_spec, got(0), got(1), got(2)],
            out_specs=pl.BlockSpec((None, tm, cb), lambda l, i, br: (2 * br[1] + l, i, 0))),
        out_shape=jax.ShapeDtypeStruct((DEPTH, rb, cb), F32),
        compiler_params=_cp("parallel", "parallel"),
    )(bc, own, others, others, others)


def _join_halves(totals):
    def body(*refs):
        g_refs, send_sems, recv_sems = refs[4:8], refs[8], refs[9]
        x, y, c, _ = _place()

        def layers(t, first):
            return g_refs[t].at[pl.ds(first, 2)]

        cps = [pltpu.make_async_remote_copy(src_ref=layers(t, 2 * c), dst_ref=layers(t, 2 * c),
                                            send_sem=send_sems.at[t], recv_sem=recv_sems.at[t],
                                            device_id=(x, y, 1 - c), device_id_type=MESH) for t in range(4)]
        for cp in cps:
            cp.start()
        for t in range(4):
            theirs = layers(t, 2 * (1 - c))
            pltpu.make_async_remote_copy(src_ref=theirs, dst_ref=theirs, send_sem=send_sems.at[t],
                                         recv_sem=recv_sems.at[t], device_id=(x, y, 1 - c),
                                         device_id_type=MESH).wait_recv()
        for cp in cps:
            cp.wait_send()

    return pl.pallas_call(
        body, name="join_halves",
        in_specs=[ANY] * 4, out_specs=[ANY] * 4,
        out_shape=[_block_shape(t, (DEPTH,), F32) for t in range(4)],
        input_output_aliases={t: t for t in range(4)},
        scratch_shapes=[pltpu.SemaphoreType.DMA((4,)), pltpu.SemaphoreType.DMA((4,))],
    )(*totals)


def _all_gather8(name, v):
    m_per, n = v.shape

    def body(v_ref, out_ref, send_sems, recv_sems, local_sem):
        x, y, c, chips = _place()
        me, sib = (x, y, c), (x, y, 1 - c)

        def rows(px, py, pc):
            return out_ref.at[pl.ds((4 * px + 2 * py + pc) * m_per, m_per), :]

        def copy(k, block, to, src=None):
            return pltpu.make_async_remote_copy(
                src_ref=rows(*block) if src is None else src, dst_ref=rows(*block),
                send_sem=send_sems.at[k], recv_sem=recv_sems.at[k], device_id=to, device_id_type=MESH)

        mine = pltpu.make_async_copy(v_ref, rows(*me), local_sem)
        mine.start()
        first = [copy(0, me, sib, src=v_ref)]
        first += [copy(1 + j, me, (*chip, c), src=v_ref) for j, chip in enumerate(chips)]
        for cp in first:
            cp.start()
        passed = [copy(4 + j, (*chip, c), sib) for j, chip in enumerate(chips)]
        for j, chip in enumerate(chips):
            copy(1 + j, (*chip, c), me).wait_recv()
            passed[j].start()
        copy(0, sib, me).wait_recv()
        for j, chip in enumerate(chips):
            copy(4 + j, (*chip, 1 - c), me).wait_recv()
        for cp in first + passed:
            cp.wait_send()
        mine.wait()

    return pl.pallas_call(
        body, name=name,
        out_shape=jax.ShapeDtypeStruct((8 * m_per, n), v.dtype),
        in_specs=[pl.BlockSpec(memory_space=pltpu.VMEM)],
        out_specs=pl.BlockSpec(memory_space=pltpu.VMEM),
        scratch_shapes=[pltpu.SemaphoreType.DMA((7,)), pltpu.SemaphoreType.DMA((7,)), pltpu.SemaphoreType.DMA],
    )(v)


def _sum8(name, g):
    def body(g_ref, o_ref):
        acc = g_ref[0]
        for d in range(1, 8):
            acc = acc + g_ref[d]
        o_ref[...] = acc

    return pl.pallas_call(body, name=name, out_shape=jax.ShapeDtypeStruct(g.shape[1:], F32))(g)


def _pack(parts):
    flat = []
    for a in parts:
        a = a.reshape(-1)
        flat.append(jnp.pad(a, (0, (-a.shape[0]) % 128)))
    cat = jnp.concatenate(flat)
    cat = jnp.pad(cat, (0, (-cat.shape[0]) % 1024))
    return cat.reshape(-1, 128)


def _unpack(packed, shapes):
    flat = packed.reshape(-1)
    out, at = [], 0
    for shp in shapes:
        n = 1
        for d in shp:
            n *= d
        out.append(flat[at:at + n].reshape(shp))
        at += n + (-n) % 128
    return out


def _local_step(x, target, weights, small):
    w_in, w_out, w_1, w_2 = weights
    s = x.shape[0]
    saved = []
    xin = x
    h = _rmsnorm("norm_first", x, small["norm1_g"][0:1])
    for l in range(DEPTH):
        qg = jnp.tile(small["q_norm_g"][l], 8)[None]
        kg = jnp.tile(small["k_norm_g"][l], 8)[None]
        rb = jnp.pad(small["rel_bias"][l], ((0, 0), (0, NIDX - 257)))
        bias = jnp.transpose(_bias_expand(f"bias_expand_{l}", rb), (1, 0, 2))
        cw = small["conv_w"][l]
        pwbd = jax.scipy.linalg.block_diag(*[small["pool_w"][l, g] for g in range(4)])
        ps = small["pool_scale"][l][None]
        p = _mm_nn(f"proj_in_{l}", h, w_in, l, 512, 512, F32)
        q, kp, vp = _qkv(f"qkv_{l}", p, qg, kg)
        o = _attn_fwd(f"attn_fwd_{l}", q, kp, vp, bias)
        mix = _convpool_fwd(f"convpool_fwd_{l}", p, o, cw, pwbd, ps)
        x1, h2 = _mm_res_norm(f"proj_out_{l}", mix, w_out, l, xin, small["norm2_g"][l:l + 1])
        a, f = _mm_mlp1(f"mlp1_{l}", h2, w_1, l)
        gnext = small["norm1_g"][(l + 1) % DEPTH][None]
        x2, hnext = _mm_res_norm(f"mlp2_{l}", f, w_2, l, x1, gnext)
        saved.append(dict(xin=xin, h=h, p=p, q=q, kp=kp, vp=vp, bias=bias, mix=mix, x1=x1, h2=h2, a=a, f=f,
                          qg=qg, kg=kg, cw=cw, pwbd=pwbd, ps=ps))
        xin, h = x2, hnext

    dx, dxb, loss = _loss_grad("loss_grad", xin, target)
    g_in = lax.empty((DEPTH, D, DIN), BF16)
    g_out = lax.empty((DEPTH, D, D), BF16)
    g_1 = lax.empty((DEPTH, D, DFF), BF16)
    g_2 = lax.empty((DEPTH, DFF, D), BF16)
    gs = {k: [None] * DEPTH for k in ("norm1_g", "q_norm_g", "k_norm_g", "rel_bias", "conv_w", "pool_w",
                                      "pool_scale", "norm2_g")}
    for l in reversed(range(DEPTH)):
        sv = saved[l]
        da = _mm_nt_relu(f"mlp2_bwd_{l}", dxb, w_2, l, sv["a"])
        g_2 = _mm_tn(f"mlp2_wgrad_{l}", sv["f"], dxb, g_2, l, 512, 512)
        g_1 = _mm_tn(f"mlp1_wgrad_{l}", sv["h2"], da, g_1, l, 512, 512)
        dx1, dx1b, dg2 = _mm_nt_normbwd(f"mlp1_bwd_{l}", da, w_1, l, sv["x1"], small["norm2_g"][l:l + 1], dx)
        dmix = _mm_nt(f"proj_out_bwd_{l}", dx1b, w_out, l, F32)
        g_out = _mm_tn(f"proj_out_wgrad_{l}", sv["mix"], dx1b, g_out, l, 512, 512)
        dcp, dw0, dw1, dw2, dps, dpw = _convpool_bwd(f"convpool_bwd_{l}", sv["p"], dmix, sv["cw"], sv["pwbd"], sv["ps"])
        dq, dkp, dvp, db = _attn_bwd(f"attn_bwd_{l}", sv["q"], sv["kp"], sv["vp"], sv["bias"], dmix)
        drb = _bias_reduce(f"bias_reduce_{l}", jnp.transpose(db, (1, 0, 2)))
        dp, dqg, dkg = _qkv_bwd(f"qkv_bwd_{l}", sv["p"], dq, dkp, dvp, dcp, sv["qg"], sv["kg"])
        g_in = _mm_tn(f"proj_in_wgrad_{l}", sv["h"], dp, g_in, l, 512, 640)
        dx, dxb, dg1 = _mm_nt_normbwd(f"proj_in_bwd_{l}", dp, w_in, l, sv["xin"], small["norm1_g"][l:l + 1], dx1)
        gs["norm1_g"][l] = dg1[0]
        gs["q_norm_g"][l] = dqg[0, :HD]
        gs["k_norm_g"][l] = dkg[0, :HD]
        gs["rel_bias"][l] = drb[:, :257]
        gs["conv_w"][l] = jnp.concatenate([dw0, dw1, dw2], axis=0)
        gs["pool_w"][l] = jnp.stack([dpw[g * 64:(g + 1) * 64, g * 64:(g + 1) * 64] for g in range(4)])
        gs["pool_scale"][l] = dps[0]
        gs["norm2_g"][l] = dg2[0]
    gsmall = {k: jnp.stack(v) for k, v in gs.items()}
    return loss, dx, (g_in, g_out, g_1, g_2), gsmall


SMALL = ("norm1_g", "q_norm_g", "k_norm_g", "rel_bias", "conv_w", "pool_w", "pool_scale", "norm2_g")
LARGE = ("w_in", "w_out", "w_mlp1", "w_mlp2")


def kernel(x, norm1_g, w_in, q_norm_g, k_norm_g, rel_bias, conv_w, pool_w, pool_scale, w_out, norm2_g, w_mlp1, w_mlp2, loss_target, m_norm1_g, m_w_in, m_q_norm_g, m_k_norm_g, m_rel_bias, m_conv_w, m_pool_w, m_pool_scale, m_w_out, m_norm2_g, m_w_mlp1, m_w_mlp2, v_norm1_g, v_w_in, v_q_norm_g, v_k_norm_g, v_rel_bias, v_conv_w, v_pool_w, v_pool_scale, v_w_out, v_norm2_g, v_w_mlp1, v_w_mlp2):
    w = dict(norm1_g=norm1_g, w_in=w_in, q_norm_g=q_norm_g, k_norm_g=k_norm_g, rel_bias=rel_bias, conv_w=conv_w,
             pool_w=pool_w, pool_scale=pool_scale, w_out=w_out, norm2_g=norm2_g, w_mlp1=w_mlp1, w_mlp2=w_mlp2)
    m = dict(norm1_g=m_norm1_g, w_in=m_w_in, q_norm_g=m_q_norm_g, k_norm_g=m_k_norm_g, rel_bias=m_rel_bias,
             conv_w=m_conv_w, pool_w=m_pool_w, pool_scale=m_pool_scale, w_out=m_w_out, norm2_g=m_norm2_g,
             w_mlp1=m_w_mlp1, w_mlp2=m_w_mlp2)
    v = dict(norm1_g=v_norm1_g, w_in=v_w_in, q_norm_g=v_q_norm_g, k_norm_g=v_k_norm_g, rel_bias=v_rel_bias,
             conv_w=v_conv_w, pool_w=v_pool_w, pool_scale=v_pool_scale, w_out=v_w_out, norm2_g=v_norm2_g,
             w_mlp1=v_w_mlp1, w_mlp2=v_w_mlp2)
    ax, ay, ac = lax.axis_index("x"), lax.axis_index("y"), lax.axis_index("c")
    c1 = jnp.reshape(ac, (1,)).astype(jnp.int32)
    b1 = jnp.reshape(2 * ax + ay, (1,)).astype(jnp.int32)

    full = _gather_weights([_cast_into_full(f"cast_{n}", t, w[n], b1) for t, n in enumerate(LARGE)])
    cw_rows = _all_gather8("gather_conv_w", jnp.pad(conv_w.reshape(DEPTH * 3, 64), ((0, 4), (0, 64))))
    cw_chips = [cw_rows[(4 * cx + 2 * cy) * 16:(4 * cx + 2 * cy) * 16 + 12, :64] for cx in range(2) for cy in range(2)]
    small = {n: w[n] for n in SMALL}
    small["conv_w"] = jnp.concatenate(cw_chips, axis=1).reshape(DEPTH, 3, CW)

    loss_part, grad_x, glarge, gsmall = _local_step(x[0], loss_target[0], full, small)
    loss = lax.psum(loss_part[0, 0], ("x", "y", "c"))

    other = _swap_halves(glarge)
    sums = [_add_halves(f"add_halves_{LARGE[t]}", glarge[t], other[t], c1) for t in range(4)]
    got = _swap_blocks(sums)
    bc = jnp.stack([2 * ax + ay, ac]).astype(jnp.int32)
    totals = [_add_blocks(f"add_blocks_{LARGE[t]}", t, sums[t], got[t], bc) for t in range(4)]
    reduced = _join_halves(totals)
    out = {}
    for t, n in enumerate(LARGE):
        shp = w[n].shape
        two = lambda a: a.reshape(-1, shp[-1])
        res = _adamw(f"adamw_{n}", two(reduced[t]), two(w[n]), two(m[n]), two(v[n]))
        out[n] = [r.reshape(shp) for r in res]

    order = [n for n in SMALL]
    packed = _pack([gsmall[n] for n in order])
    rows = packed.shape[0]
    summed = _sum8("sum_small", _all_gather8("gather_small", packed).reshape(8, rows, 128))
    gfull = dict(zip(order, _unpack(summed, [gsmall[n].shape for n in order])))
    gfull["conv_w"] = lax.dynamic_slice_in_dim(gfull["conv_w"], (2 * ax + ay) * 64, 64, axis=2)
    res = _adamw("adamw_small", _pack([gfull[n] for n in order]), _pack([w[n] for n in order]),
                 _pack([m[n] for n in order]), _pack([v[n] for n in order]))
    for n, parts in zip(order, zip(*[_unpack(r, [w[k].shape for k in order]) for r in res])):
        out[n] = list(parts)

    names = ("norm1_g", "w_in", "q_norm_g", "k_norm_g", "rel_bias", "conv_w", "pool_w", "pool_scale", "w_out",
             "norm2_g", "w_mlp1", "w_mlp2")
    flat = [loss, grad_x[None]]
    for i in range(4):
        flat += [out[n][i] for n in names]
    return tuple(flat)
```

```python
import functools

import jax
import jax.numpy as jnp
from jax import lax
from jax.experimental import pallas as pl
from jax.experimental.pallas import tpu as pltpu

F32 = jnp.float32
BF16 = jnp.bfloat16

D = 1024
DEPTH = 4
CH = 64
NPREV = 8
KB = (NPREV + 1) * CH
PADR = NPREV * CH
HD = 64
AW = 512
CW = 256
PWD = 256
DIN = 3 * AW + 3 * CW + PWD
DFF = 4 * D
NIDX = 384
EPS = 1e-6
NEG_INF = -1e30

ADAM_LR = 0.001
ADAM_B1 = 0.9
ADAM_B2 = 0.999
ADAM_EPS = 1e-08
ADAM_WD = 0.01
ADAM_STEP = 10

VMEM_LIMIT = 52 * 1024 * 1024
MESH = pl.DeviceIdType.MESH
ANY = pl.BlockSpec(memory_space=pl.ANY)


def _cp(*sem):
    return pltpu.CompilerParams(dimension_semantics=sem, vmem_limit_bytes=VMEM_LIMIT)


def _inv_rms(x):
    return lax.rsqrt(jnp.mean(x * x, axis=-1, keepdims=True) + EPS)


def _head_mean_matrix():
    r = lax.broadcasted_iota(jnp.int32, (AW, AW), 0) // HD
    c = lax.broadcasted_iota(jnp.int32, (AW, AW), 1) // HD
    return jnp.where(r == c, 1.0 / HD, 0.0).astype(BF16)


def _head_mean(x, hm):
    hi = x.astype(BF16)
    lo = (x - hi.astype(F32)).astype(BF16)
    return (jnp.dot(hi, hm, preferred_element_type=F32)
            + jnp.dot(lo, hm, preferred_element_type=F32))


def _rmsnorm(name, x, g):
    s = x.shape[0]
    tm = 512

    def body(x_ref, g_ref, h_ref):
        xv = x_ref[...]
        h_ref[...] = (xv * _inv_rms(xv) * g_ref[...]).astype(BF16)

    return pl.pallas_call(
        body, name=name, grid=(s // tm,),
        in_specs=[pl.BlockSpec((tm, D), lambda i: (i, 0)), pl.BlockSpec((1, D), lambda i: (0, 0))],
        out_specs=pl.BlockSpec((tm, D), lambda i: (i, 0)),
        out_shape=jax.ShapeDtypeStruct((s, D), BF16),
        compiler_params=_cp("parallel"),
    )(x, g)


def _mm_nn(name, a, w, l, tm, tn, out_dtype):
    s, k = a.shape
    n = w.shape[2]

    def body(a_ref, w_ref, o_ref):
        o_ref[...] = jnp.dot(a_ref[...], w_ref[...], preferred_element_type=F32).astype(o_ref.dtype)

    return pl.pallas_call(
        body, name=name, grid=(s // tm, n // tn),
        in_specs=[pl.BlockSpec((tm, k), lambda i, j: (i, 0)),
                  pl.BlockSpec((None, k, tn), lambda i, j: (l, 0, j))],
        out_specs=pl.BlockSpec((tm, tn), lambda i, j: (i, j)),
        out_shape=jax.ShapeDtypeStruct((s, n), out_dtype),
        compiler_params=_cp("parallel", "parallel"),
    )(a, w)


def _mm_mlp1(name, h2, w, l):
    s, k = h2.shape
    n = w.shape[2]
    tm, tn = 512, 1024

    def body(a_ref, w_ref, o_ref, f_ref):
        acc = jnp.dot(a_ref[...], w_ref[...], preferred_element_type=F32)
        o_ref[...] = acc.astype(BF16)
        f_ref[...] = jnp.square(jnp.maximum(acc, 0.0)).astype(BF16)

    return pl.pallas_call(
        body, name=name, grid=(s // tm, n // tn),
        in_specs=[pl.BlockSpec((tm, k), lambda i, j: (i, 0)),
                  pl.BlockSpec((None, k, tn), lambda i, j: (l, 0, j))],
        out_specs=[pl.BlockSpec((tm, tn), lambda i, j: (i, j))] * 2,
        out_shape=[jax.ShapeDtypeStruct((s, n), BF16)] * 2,
        compiler_params=_cp("parallel", "parallel"),
    )(h2, w)


def _mm_res_norm(name, a, w, l, res, g):
    s, k = a.shape
    tm = 256

    def body(a_ref, w_ref, r_ref, g_ref, x_ref, h_ref):
        acc = r_ref[...] + jnp.dot(a_ref[...], w_ref[...], preferred_element_type=F32)
        x_ref[...] = acc
        h_ref[...] = (acc * _inv_rms(acc) * g_ref[...]).astype(BF16)

    return pl.pallas_call(
        body, name=name, grid=(s // tm,),
        in_specs=[pl.BlockSpec((tm, k), lambda i: (i, 0)),
                  pl.BlockSpec((None, k, D), lambda i: (l, 0, 0)),
                  pl.BlockSpec((tm, D), lambda i: (i, 0)),
                  pl.BlockSpec((1, D), lambda i: (0, 0))],
        out_specs=[pl.BlockSpec((tm, D), lambda i: (i, 0))] * 2,
        out_shape=[jax.ShapeDtypeStruct((s, D), F32), jax.ShapeDtypeStruct((s, D), BF16)],
        compiler_params=_cp("parallel"),
    )(a, w, res, g)


def _qkv(name, p, qg, kg):
    s = p.shape[0]
    tm = PADR
    nb = s // tm

    def body(pq_ref, pk_ref, pv_ref, qg_ref, kg_ref, q_ref, qt_ref, k_ref, kt_ref, v_ref, vt_ref):
        t = pl.program_id(0)
        hm = _head_mean_matrix()

        def nrm(x, g):
            return x * lax.rsqrt(_head_mean(x * x, hm) + EPS) * g

        first = t == 0
        qq = nrm(pq_ref[...], qg_ref[...]) * 0.125
        kk = jnp.where(first, 0.0, nrm(pk_ref[...], kg_ref[...]))
        vv = jnp.where(first, 0.0, pv_ref[...])
        q_ref[...] = qq.astype(BF16)
        qt_ref[...] = qq.T.astype(BF16)
        k_ref[...] = kk.astype(BF16)
        kt_ref[...] = kk.T.astype(BF16)
        v_ref[...] = vv.astype(BF16)
        vt_ref[...] = vv.T.astype(BF16)

    def src(col):
        return pl.BlockSpec((tm, AW), lambda t: (jnp.maximum(t - 1, 0), col))

    gspec = pl.BlockSpec((1, AW), lambda t: (0, 0))
    rows = pl.BlockSpec((tm, AW), lambda t: (t, 0))
    cols = pl.BlockSpec((AW, tm), lambda t: (0, t))
    return pl.pallas_call(
        body, name=name, grid=(nb + 1,),
        in_specs=[src(0), src(1), src(2), gspec, gspec],
        out_specs=[pl.BlockSpec((tm, AW), lambda t: (jnp.maximum(t - 1, 0), 0)),
                   pl.BlockSpec((AW, tm), lambda t: (0, jnp.maximum(t - 1, 0))),
                   rows, cols, rows, cols],
        out_shape=[jax.ShapeDtypeStruct((s, AW), BF16), jax.ShapeDtypeStruct((AW, s), BF16),
                   jax.ShapeDtypeStruct((s + PADR, AW), BF16), jax.ShapeDtypeStruct((AW, s + PADR), BF16),
                   jax.ShapeDtypeStruct((s + PADR, AW), BF16), jax.ShapeDtypeStruct((AW, s + PADR), BF16)],
        compiler_params=_cp("arbitrary"),
    )(p, p, p, qg, kg)


NBAND = KB // CH
HIGHEST = lax.Precision.HIGHEST
NT_DIMS = (((1,), (1,)), ((), ()))


def _onehot_table(a):
    m = lax.broadcasted_iota(jnp.int32, (128, NIDX), 0)
    idx = lax.broadcasted_iota(jnp.int32, (128, NIDX), 1)
    rel = jnp.clip(KB - 1 - (CH * a + m), -128, 128) + 128
    return jnp.where(rel == idx, 1.0, 0.0).astype(F32)


def _onehot_diagonal():
    r = lax.broadcasted_iota(jnp.int32, (CH * CH, 128), 0)
    m = lax.broadcasted_iota(jnp.int32, (CH * CH, 128), 1)
    return jnp.where((r % CH) - (r // CH) + (CH - 1) == m, 1.0, 0.0).astype(F32)


def _bias_expand(name, rb):
    def body(rb_ref, o_ref):
        along = [lax.dot_general(rb_ref[...], _onehot_table(a), NT_DIMS, preferred_element_type=F32,
                                 precision=HIGHEST) for a in range(NBAND)]
        o_ref[...] = lax.dot_general(jnp.concatenate(along, axis=0), _onehot_diagonal(), NT_DIMS,
                                     preferred_element_type=F32, precision=HIGHEST)

    return pl.pallas_call(
        body, name=name,
        out_shape=jax.ShapeDtypeStruct((NBAND * 8, CH * CH), F32),
    )(rb)


def _bias_reduce(name, db):
    def body(db_ref, o_ref):
        along = jnp.dot(db_ref[...], _onehot_diagonal(), preferred_element_type=F32, precision=HIGHEST)
        acc = jnp.zeros((8, NIDX), F32)
        for a in range(NBAND):
            acc = acc + jnp.dot(along[8 * a:8 * a + 8, :], _onehot_table(a), preferred_element_type=F32,
                                precision=HIGHEST)
        o_ref[...] = acc

    return pl.pallas_call(
        body, name=name,
        out_shape=jax.ShapeDtypeStruct((8, NIDX), F32),
    )(db)


def _bias_layout(flat):
    b = flat.reshape(NBAND, 8, CH, CH).transpose(1, 0, 3, 2).reshape(4, 2, KB, CH)
    pair = b.transpose(0, 2, 1, 3).reshape(4, KB, 128)
    first = jnp.pad(pair, ((0, 0), (0, CH), (0, 0)), constant_values=NEG_INF)
    second = jnp.pad(pair, ((0, 0), (CH, 0), (0, 0)), constant_values=NEG_INF)
    return jnp.concatenate([first, second], axis=2)


def _bias_unlayout(dbt):
    b = dbt.reshape(4, NBAND, CH, 2, CH)
    return b.transpose(1, 0, 3, 4, 2).reshape(NBAND * 8, CH * CH)


UNIT = 2 * CH
BAND2 = KB + CH


def _pair_weights(xt):
    x = xt.astype(F32)
    row = lax.broadcasted_iota(jnp.int32, (128, UNIT), 0)
    low = lax.broadcasted_iota(jnp.int32, (128, UNIT), 1) < HD
    swapped = pltpu.roll(x, HD, 1)
    same = (row < HD) == low
    first = jnp.where(same, jnp.where(low, x, swapped), 0.0)
    second = jnp.where(same, jnp.where(low, swapped, x), 0.0)
    return jnp.concatenate([first, second], axis=1).astype(BF16)


def _pair_rows(x):
    low = lax.broadcasted_iota(jnp.int32, (CH, 128), 1) < HD
    zero = jnp.zeros((CH, 128), x.dtype)
    parts = []
    for c in range(2):
        xc = x[c * CH:(c + 1) * CH, :]
        parts += [jnp.where(low, xc, zero), jnp.where(low, zero, xc)]
    return jnp.concatenate(parts, axis=0)


def _unpair(raw):
    b0, b1 = raw[:, 0:128], raw[:, 128:256]
    row = lax.broadcasted_iota(jnp.int32, (128, 128), 0)
    low = lax.broadcasted_iota(jnp.int32, (128, 128), 1) < HD
    top = jnp.where(low, b0, pltpu.roll(b1, HD, 1))
    bottom = jnp.where(low, pltpu.roll(b0, HD, 1), b1)
    return jnp.where(row < HD, top, bottom).T


def _softmax_t(kb, qw, bias2, row0):
    s = jnp.dot(kb, qw, preferred_element_type=F32)
    valid = (row0 + lax.broadcasted_iota(jnp.int32, (BAND2, 256), 0)) >= PADR
    s = jnp.where(valid, s + bias2, NEG_INF)
    e = jnp.exp(s - jnp.max(s, axis=0, keepdims=True))
    return e * (1.0 / jnp.sum(e, axis=0, keepdims=True))


def _attn_fwd(name, kp, qt, vt, bias2):
    s = qt.shape[1]

    def body(k_ref, qt_ref, vt_ref, b_ref, o_ref):
        def unit(u, carry):
            r0 = pl.multiple_of(u * UNIT, UNIT)
            pt = _softmax_t(k_ref[pl.ds(r0, BAND2), :], _pair_weights(qt_ref[:, pl.ds(r0, UNIT)]), b_ref[...], r0)
            raw = jnp.dot(vt_ref[:, pl.ds(r0, BAND2)], pt.astype(BF16), preferred_element_type=F32)
            o_ref[pl.ds(r0, UNIT), :] = _unpair(raw).astype(BF16)
            return carry

        lax.fori_loop(0, s // UNIT, unit, 0)

    return pl.pallas_call(
        body, name=name, grid=(AW // 128,),
        in_specs=[pl.BlockSpec((s + PADR, 128), lambda h: (0, h)),
                  pl.BlockSpec((128, s), lambda h: (h, 0)),
                  pl.BlockSpec((128, s + PADR), lambda h: (h, 0)),
                  pl.BlockSpec((None, BAND2, 256), lambda h: (h, 0, 0))],
        out_specs=pl.BlockSpec((s, 128), lambda h: (0, h)),
        out_shape=jax.ShapeDtypeStruct((s, AW), BF16),
        compiler_params=_cp("parallel"),
    )(kp, qt, vt, bias2)


def _attn_bwd(name, q, qt, kp, kt, vp, bias2, do, dot):
    s = q.shape[0]

    def body(q_ref, qt_ref, k_ref, kt_ref, v_ref, b_ref, do_ref, dot_ref, dq_ref, dk_ref, dv_ref, db_ref):
        dk_ref[...] = jnp.zeros_like(dk_ref)
        dv_ref[...] = jnp.zeros_like(dv_ref)
        db_ref[...] = jnp.zeros_like(db_ref)

        def unit(u, carry):
            r0 = pl.multiple_of(u * UNIT, UNIT)
            rows, band = pl.ds(r0, UNIT), pl.ds(r0, BAND2)
            pt = _softmax_t(k_ref[band, :], _pair_weights(qt_ref[:, rows]), b_ref[...], r0)
            dpt = jnp.dot(v_ref[band, :], _pair_weights(dot_ref[:, rows]), preferred_element_type=F32)
            ds = pt * (dpt - jnp.sum(dpt * pt, axis=0, keepdims=True))
            db_ref[...] += ds[0:KB, 0:128] + ds[CH:BAND2, 128:256]
            dsb = ds.astype(BF16)
            dq_ref[rows, :] = _unpair(jnp.dot(kt_ref[:, band], dsb, preferred_element_type=F32))
            dk_ref[band, :] += jnp.dot(dsb, _pair_rows(q_ref[rows, :]), preferred_element_type=F32)
            dv_ref[band, :] += jnp.dot(pt.astype(BF16), _pair_rows(do_ref[rows, :]), preferred_element_type=F32)
            return carry

        lax.fori_loop(0, s // UNIT, unit, 0)

    row_q = pl.BlockSpec((s, 128), lambda h: (0, h))
    col_q = pl.BlockSpec((128, s), lambda h: (h, 0))
    row_k = pl.BlockSpec((s + PADR, 128), lambda h: (0, h))
    col_k = pl.BlockSpec((128, s + PADR), lambda h: (h, 0))
    return pl.pallas_call(
        body, name=name, grid=(AW // 128,),
        in_specs=[row_q, col_q, row_k, col_k, row_k,
                  pl.BlockSpec((None, BAND2, 256), lambda h: (h, 0, 0)), row_q, col_q],
        out_specs=[row_q, row_k, row_k, pl.BlockSpec((None, KB, 128), lambda h: (h, 0, 0))],
        out_shape=[jax.ShapeDtypeStruct((s, AW), F32),
                   jax.ShapeDtypeStruct((s + PADR, AW), F32),
                   jax.ShapeDtypeStruct((s + PADR, AW), F32),
                   jax.ShapeDtypeStruct((4, KB, 128), F32)],
        compiler_params=_cp("parallel"),
    )(q, qt, kp, kt, vp, bias2, do, dot)


def _rows_before(cur, prev, k):
    row = lax.broadcasted_iota(jnp.int32, cur.shape, 0)
    return jnp.where(row >= k, pltpu.roll(cur, k, 0), pltpu.roll(prev, k, 0))


def _rows_after(cur, nxt, k):
    n = cur.shape[0]
    row = lax.broadcasted_iota(jnp.int32, cur.shape, 0)
    return jnp.where(row < n - k, pltpu.roll(cur, n - k, 0), pltpu.roll(nxt, n - k, 0))


def _pool_window_lanes():
    lg = lax.broadcasted_iota(jnp.int32, (1, PWD), 1) // 64
    return lg, jnp.where(lg == 0, 2.0, jnp.where(lg == 1, 4.0, jnp.where(lg == 2, 8.0, 16.0))).astype(F32)


def _pool_mean_minus_token(u, up, row0):
    lg, wv = _pool_window_lanes()
    sums = []
    c, p = u, up
    for k in (1, 2, 4, 8):
        c2 = c + _rows_before(c, p, k)
        p = p + pltpu.roll(p, k, 0)
        c = c2
        sums.append(c)
    win = jnp.where(lg == 0, sums[0], jnp.where(lg == 1, sums[1], jnp.where(lg == 2, sums[2], sums[3])))
    pos1 = (row0 + lax.broadcasted_iota(jnp.int32, u.shape, 0) + 1).astype(F32)
    cnt = jnp.minimum(pos1, wv)
    return win / cnt - u, cnt


def _conv_taps(z, zp, w0, w1, w2):
    z1 = _rows_before(z, zp, 1)
    z2 = _rows_before(z, zp, 2)
    return (w0 * z2 + w1 * z1) + w2 * z, z1, z2


CP_TM = 512


def _convpool_fwd(name, p, o, cw, pwbd, ps):
    s = p.shape[0]
    tm = CP_TM
    nb = s // tm

    def body(gb_ref, gc_ref, hin_ref, u_ref, gcp_ref, hinp_ref, up_ref, o_ref, cw_ref, pw_ref, ps_ref, mix_ref):
        i = pl.program_id(0)
        has_prev = i > 0
        z = gc_ref[...] * hin_ref[...]
        zp = jnp.where(has_prev, gcp_ref[...] * hinp_ref[...], 0.0)
        y3, _, _ = _conv_taps(z, zp, cw_ref[0:1, :], cw_ref[1:2, :], cw_ref[2:3, :])
        m, _ = _pool_mean_minus_token(u_ref[...], jnp.where(has_prev, up_ref[...], 0.0), i * tm)
        yp = jnp.dot(m.astype(BF16), pw_ref[...].astype(BF16), preferred_element_type=F32) * ps_ref[...]
        mix_ref[:, 0:AW] = o_ref[...]
        mix_ref[:, AW:AW + CW] = (gb_ref[...] * y3).astype(BF16)
        mix_ref[:, AW + CW:D] = yp.astype(BF16)

    def cur(col):
        return pl.BlockSpec((tm, CW), lambda i: (i, col))

    def prev(col):
        return pl.BlockSpec((tm, CW), lambda i: (jnp.maximum(i - 1, 0), col))

    def whole(a):
        return pl.BlockSpec(a.shape, lambda i: (0,) * a.ndim)

    return pl.pallas_call(
        body, name=name, grid=(nb,),
        in_specs=[cur(6), cur(7), cur(8), cur(9), prev(7), prev(8), prev(9),
                  pl.BlockSpec((tm, AW), lambda i: (i, 0)), whole(cw), whole(pwbd), whole(ps)],
        out_specs=pl.BlockSpec((tm, D), lambda i: (i, 0)),
        out_shape=jax.ShapeDtypeStruct((s, D), BF16),
        compiler_params=_cp("parallel"),
    )(p, p, p, p, p, p, p, o, cw, pwbd, ps)


def _convpool_bwd(name, p, dmix, cw, pwbd, ps):
    s = p.shape[0]
    tm = CP_TM
    nb = s // tm

    def body(gb_ref, gc_ref, hin_ref, u_ref, gcp_ref, hinp_ref, up_ref, gbn_ref, dyc_ref, dyp_ref, dycn_ref, dypn_ref,
             cw_ref, pw_ref, ps_ref, dcp_ref, dw0_ref, dw1_ref, dw2_ref, dps_ref, dpw_ref):
        i = pl.program_id(0)
        has_prev = i > 0
        has_next = i < nb - 1
        w0, w1, w2 = cw_ref[0:1, :], cw_ref[1:2, :], cw_ref[2:3, :]
        gb, gc, hin = gb_ref[...], gc_ref[...], hin_ref[...]
        dyc = dyc_ref[...]
        z = gc * hin
        zp = jnp.where(has_prev, gcp_ref[...] * hinp_ref[...], 0.0)
        y3, z1, z2 = _conv_taps(z, zp, w0, w1, w2)
        dy3 = dyc * gb
        dy3n = jnp.where(has_next, dycn_ref[...] * gbn_ref[...], 0.0)
        dz = w2 * dy3 + w1 * _rows_after(dy3, dy3n, 1) + w0 * _rows_after(dy3, dy3n, 2)
        pw = pw_ref[...].astype(BF16)
        psv = ps_ref[...]
        m, cnt = _pool_mean_minus_token(u_ref[...], jnp.where(has_prev, up_ref[...], 0.0), i * tm)
        mb = m.astype(BF16)
        dyp = dyp_ref[...]
        dmp = (dyp * psv).astype(BF16)
        dmpn = jnp.where(has_next, dypn_ref[...] * psv, 0.0).astype(BF16)
        nt = (((1,), (1,)), ((), ()))
        dm = lax.dot_general(dmp, pw, nt, preferred_element_type=F32)
        dmn = lax.dot_general(dmpn, pw, nt, preferred_element_type=F32)
        lg, wv = _pool_window_lanes()
        cc, cn = dm / cnt, dmn / wv
        sums = []
        for k in (1, 2, 4, 8):
            c2 = cc + _rows_after(cc, cn, k)
            cn = cn + pltpu.roll(cn, tm - k, 0)
            cc = c2
            sums.append(cc)
        du = jnp.where(lg == 0, sums[0], jnp.where(lg == 1, sums[1], jnp.where(lg == 2, sums[2], sums[3]))) - dm
        dcp_ref[:, 0:CW] = (dyc * y3).astype(BF16)
        dcp_ref[:, CW:2 * CW] = (dz * hin).astype(BF16)
        dcp_ref[:, 2 * CW:3 * CW] = (dz * gc).astype(BF16)
        dcp_ref[:, 3 * CW:4 * CW] = du.astype(BF16)
        parts = (jnp.sum(dy3 * z2, axis=0, keepdims=True),
                 jnp.sum(dy3 * z1, axis=0, keepdims=True),
                 jnp.sum(dy3 * z, axis=0, keepdims=True),
                 jnp.sum(dyp * jnp.dot(mb, pw, preferred_element_type=F32), axis=0, keepdims=True),
                 lax.dot_general(mb, dmp, (((0,), (0,)), ((), ())), preferred_element_type=F32))
        accs = (dw0_ref, dw1_ref, dw2_ref, dps_ref, dpw_ref)

        @pl.when(i == 0)
        def _():
            for a, v in zip(accs, parts):
                a[...] = v

        @pl.when(i > 0)
        def _():
            for a, v in zip(accs, parts):
                a[...] += v

    def cur(col):
        return pl.BlockSpec((tm, CW), lambda i: (i, col))

    def prev(col):
        return pl.BlockSpec((tm, CW), lambda i: (jnp.maximum(i - 1, 0), col))

    def nxt(col):
        return pl.BlockSpec((tm, CW), lambda i: (jnp.minimum(i + 1, nb - 1), col))

    def whole(shape):
        return pl.BlockSpec(shape, lambda i: (0,) * len(shape))

    row = jax.ShapeDtypeStruct((1, CW), F32)
    return pl.pallas_call(
        body, name=name, grid=(nb,),
        in_specs=[cur(6), cur(7), cur(8), cur(9), prev(7), prev(8), prev(9), nxt(6),
                  cur(0), cur(1), nxt(0), nxt(1), whole(cw.shape), whole(pwbd.shape), whole(ps.shape)],
        out_specs=[pl.BlockSpec((tm, D), lambda i: (i, 0)), whole((1, CW)), whole((1, CW)), whole((1, CW)),
                   whole((1, PWD)), whole((PWD, PWD))],
        out_shape=[jax.ShapeDtypeStruct((s, D), BF16), row, row, row, row,
                   jax.ShapeDtypeStruct((PWD, PWD), F32)],
        compiler_params=_cp("arbitrary"),
    )(p, p, p, p, p, p, p, p, dmix, dmix, dmix, dmix, cw, pwbd, ps)


def _qkv_bwd(name, p, dq, dkp, dvp, dcp, qg, kg):
    s = p.shape[0]
    tm = 256
    off = PADR // tm

    def body(pq_ref, pk_ref, dq_ref, dk_ref, dv_ref, dcp_ref, qg_ref, kg_ref, dp_ref, dqg_ref, dkg_ref):
        i = pl.program_id(0)
        hm = _head_mean_matrix()

        def nrm_bwd(x, g, dy):
            r = lax.rsqrt(_head_mean(x * x, hm) + EPS)
            xn = x * r
            dxn = dy * g
            dx = r * (dxn - xn * _head_mean(dxn * xn, hm))
            dg = jnp.sum(dy * xn, axis=0, keepdims=True)
            dg = (dg[:, 0:128] + dg[:, 128:256]) + (dg[:, 256:384] + dg[:, 384:512])
            return dx, dg + pltpu.roll(dg, HD, 1)

        dxq, dgq = nrm_bwd(pq_ref[...], qg_ref[...], dq_ref[...] * 0.125)
        dxk, dgk = nrm_bwd(pk_ref[...], kg_ref[...], dk_ref[...])
        dp_ref[:, 0:AW] = dxq.astype(BF16)
        dp_ref[:, AW:2 * AW] = dxk.astype(BF16)
        dp_ref[:, 2 * AW:3 * AW] = dv_ref[...].astype(BF16)
        dp_ref[:, 3 * AW:DIN] = dcp_ref[...]

        @pl.when(i == 0)
        def _():
            dqg_ref[...] = dgq
            dkg_ref[...] = dgk

        @pl.when(i > 0)
        def _():
            dqg_ref[...] += dgq
            dkg_ref[...] += dgk

    gspec = pl.BlockSpec((1, AW), lambda i: (0, 0))
    gout = pl.BlockSpec((1, 128), lambda i: (0, 0))
    return pl.pallas_call(
        body, name=name, grid=(s // tm,),
        in_specs=[pl.BlockSpec((tm, AW), lambda i: (i, 0)), pl.BlockSpec((tm, AW), lambda i: (i, 1)),
                  pl.BlockSpec((tm, AW), lambda i: (i, 0)),
                  pl.BlockSpec((tm, AW), lambda i: (i + off, 0)),
                  pl.BlockSpec((tm, AW), lambda i: (i + off, 0)),
                  pl.BlockSpec((tm, D), lambda i: (i, 0)), gspec, gspec],
        out_specs=[pl.BlockSpec((tm, DIN), lambda i: (i, 0)), gout, gout],
        out_shape=[jax.ShapeDtypeStruct((s, DIN), BF16), jax.ShapeDtypeStruct((1, 128), F32),
                   jax.ShapeDtypeStruct((1, 128), F32)],
        compiler_params=_cp("arbitrary"),
    )(p, p, dq, dkp, dvp, dcp, qg, kg)


def _loss_grad(name, y, t):
    s = y.shape[0]
    tm = 512

    def body(y_ref, t_ref, dy_ref, dyb_ref, l_ref):
        i = pl.program_id(0)
        e = y_ref[...] - t_ref[...]
        dy = e * (1.0 / D)
        dy_ref[...] = dy
        dyb_ref[...] = dy.astype(BF16)
        part = 0.5 * jnp.sum(jnp.mean(e * e, axis=-1, keepdims=True), axis=0, keepdims=True)

        @pl.when(i == 0)
        def _():
            l_ref[...] = part

        @pl.when(i > 0)
        def _():
            l_ref[...] += part

    blk = pl.BlockSpec((tm, D), lambda i: (i, 0))
    return pl.pallas_call(
        body, name=name, grid=(s // tm,),
        in_specs=[blk, blk],
        out_specs=[blk, blk, pl.BlockSpec((1, 1), lambda i: (0, 0))],
        out_shape=[jax.ShapeDtypeStruct((s, D), F32), jax.ShapeDtypeStruct((s, D), BF16),
                   jax.ShapeDtypeStruct((1, 1), F32)],
        compiler_params=_cp("arbitrary"),
    )(y, t)


def _mm_nt_relu(name, dxb, w, l, a):
    s = dxb.shape[0]
    tm, tn = 512, 1024

    def body(d_ref, w_ref, a_ref, o_ref):
        df = lax.dot_general(d_ref[...], w_ref[...], (((1,), (1,)), ((), ())), preferred_element_type=F32)
        o_ref[...] = (df * (2.0 * jnp.maximum(a_ref[...].astype(F32), 0.0))).astype(BF16)

    return pl.pallas_call(
        body, name=name, grid=(s // tm, DFF // tn),
        in_specs=[pl.BlockSpec((tm, D), lambda i, j: (i, 0)),
                  pl.BlockSpec((None, tn, D), lambda i, j: (l, j, 0)),
                  pl.BlockSpec((tm, tn), lambda i, j: (i, j))],
        out_specs=pl.BlockSpec((tm, tn), lambda i, j: (i, j)),
        out_shape=jax.ShapeDtypeStruct((s, DFF), BF16),
        compiler_params=_cp("parallel", "parallel"),
    )(dxb, w, a)


def _proj_out_bwd(name, dxb, w, l):
    s = dxb.shape[0]
    tm = 512

    def body(d_ref, w_ref, do_ref, dot_ref, dcp_ref):
        d = d_ref[...]
        wa, wc = w_ref[0:AW, :], w_ref[AW:D, :]
        do_ref[...] = lax.dot_general(d, wa, NT_DIMS, preferred_element_type=F32).astype(BF16)
        dot_ref[...] = lax.dot_general(wa, d, NT_DIMS, preferred_element_type=F32).astype(BF16)
        dcp_ref[...] = lax.dot_general(d, wc, NT_DIMS, preferred_element_type=F32)

    return pl.pallas_call(
        body, name=name, grid=(s // tm,),
        in_specs=[pl.BlockSpec((tm, D), lambda i: (i, 0)),
                  pl.BlockSpec((None, D, D), lambda i: (l, 0, 0))],
        out_specs=[pl.BlockSpec((tm, AW), lambda i: (i, 0)), pl.BlockSpec((AW, tm), lambda i: (0, i)),
                   pl.BlockSpec((tm, D - AW), lambda i: (i, 0))],
        out_shape=[jax.ShapeDtypeStruct((s, AW), BF16), jax.ShapeDtypeStruct((AW, s), BF16),
                   jax.ShapeDtypeStruct((s, D - AW), F32)],
        compiler_params=_cp("parallel"),
    )(dxb, w)


def _mm_nt_normbwd(name, gy, w, l, x, g, dres):
    s, k = gy.shape
    tm = 256

    def body(gy_ref, w_ref, x_ref, g_ref, dr_ref, dx_ref, dxb_ref, dg_ref):
        i = pl.program_id(0)
        dh = lax.dot_general(gy_ref[...], w_ref[...], (((1,), (1,)), ((), ())), preferred_element_type=F32)
        xv = x_ref[...]
        r = _inv_rms(xv)
        xn = xv * r
        dxn = dh * g_ref[...]
        dx = r * (dxn - xn * jnp.mean(dxn * xn, axis=-1, keepdims=True)) + dr_ref[...]
        dx_ref[...] = dx
        dxb_ref[...] = dx.astype(BF16)
        part = jnp.sum(dh * xn, axis=0, keepdims=True)

        @pl.when(i == 0)
        def _():
            dg_ref[...] = part

        @pl.when(i > 0)
        def _():
            dg_ref[...] += part

    blk = pl.BlockSpec((tm, D), lambda i: (i, 0))
    vec = pl.BlockSpec((1, D), lambda i: (0, 0))
    return pl.pallas_call(
        body, name=name, grid=(s // tm,),
        in_specs=[pl.BlockSpec((tm, k), lambda i: (i, 0)),
                  pl.BlockSpec((None, D, k), lambda i: (l, 0, 0)), blk, vec, blk],
        out_specs=[blk, blk, vec],
        out_shape=[jax.ShapeDtypeStruct((s, D), F32), jax.ShapeDtypeStruct((s, D), BF16),
                   jax.ShapeDtypeStruct((1, D), F32)],
        compiler_params=_cp("arbitrary"),
    )(gy, w, x, g, dres)


def _mm_tn(name, a, b, acc, l, tma, tnb):
    s, m = a.shape
    n = b.shape[1]

    def body(a_ref, b_ref, acc_ref, o_ref):
        del acc_ref
        o_ref[...] = lax.dot_general(a_ref[...], b_ref[...], (((0,), (0,)), ((), ())),
                                     preferred_element_type=F32).astype(BF16)

    return pl.pallas_call(
        body, name=name, grid=(m // tma, n // tnb),
        in_specs=[pl.BlockSpec((s, tma), lambda i, j: (0, i)),
                  pl.BlockSpec((s, tnb), lambda i, j: (0, j)), ANY],
        out_specs=pl.BlockSpec((None, tma, tnb), lambda i, j: (l, i, j)),
        out_shape=jax.ShapeDtypeStruct(acc.shape, BF16),
        input_output_aliases={2: 0},
        compiler_params=_cp("parallel", "parallel"),
    )(a, b, acc)


def _adamw(name, g, w, m, v):
    r, c = g.shape
    tm = 256 if r % 256 == 0 else r

    def body(g_ref, w_ref, m_ref, v_ref, go_ref, d_ref, mo_ref, vo_ref):
        gv = g_ref[...]
        mn = ADAM_B1 * m_ref[...] + (1.0 - ADAM_B1) * gv
        vn = ADAM_B2 * v_ref[...] + (1.0 - ADAM_B2) * jnp.square(gv)
        m_hat = mn / (1.0 - ADAM_B1 ** ADAM_STEP)
        v_hat = vn / (1.0 - ADAM_B2 ** ADAM_STEP)
        go_ref[...] = gv
        d_ref[...] = -ADAM_LR * (m_hat / (jnp.sqrt(v_hat) + ADAM_EPS) + ADAM_WD * w_ref[...])
        mo_ref[...] = mn
        vo_ref[...] = vn

    blk = pl.BlockSpec((tm, c), lambda i: (i, 0))
    return pl.pallas_call(
        body, name=name, grid=(r // tm,),
        in_specs=[blk] * 4, out_specs=[blk] * 4,
        out_shape=[jax.ShapeDtypeStruct((r, c), F32)] * 4,
        compiler_params=_cp("parallel"),
    )(g, w, m, v)


def _place():
    x, y, c = lax.axis_index("x"), lax.axis_index("y"), lax.axis_index("c")
    chips = [(1 - x, y), (x, 1 - y), (1 - x, 1 - y)]
    return x, y, c, chips


BLOCK_AXIS = (2, 1, 2, 1)
LARGE_DIMS = ((D, DIN), (D, D), (D, DFF), (DFF, D))


def _block(ref, layers, t, b):
    width = LARGE_DIMS[t][BLOCK_AXIS[t] - 1] // 4
    if BLOCK_AXIS[t] == 1:
        return ref.at[layers, pl.ds(pl.multiple_of(b * width, 16), width), :]
    return ref.at[layers, :, pl.ds(pl.multiple_of(b * width, 128), width)]


def _full_shape(t, layers, dtype):
    r, c = LARGE_DIMS[t]
    return jax.ShapeDtypeStruct((layers, r, c), dtype)


def _block_shape(t, lead, dtype):
    r, c = LARGE_DIMS[t]
    if BLOCK_AXIS[t] == 1:
        return jax.ShapeDtypeStruct(lead + (r // 4, c), dtype)
    return jax.ShapeDtypeStruct(lead + (r, c // 4), dtype)


def _cast_into_full(name, t, shard, b1):
    _, r, c = shard.shape
    tm = min(256, r)
    if BLOCK_AXIS[t] == 1:
        out_spec = pl.BlockSpec((None, tm, c), lambda l, i, br: (l, br[0] * (r // tm) + i, 0))
    else:
        out_spec = pl.BlockSpec((None, tm, c), lambda l, i, br: (l, i, br[0]))

    def body(b_ref, x_ref, o_ref):
        del b_ref
        o_ref[...] = x_ref[...].astype(BF16)

    return pl.pallas_call(
        body, name=name,
        grid_spec=pltpu.PrefetchScalarGridSpec(
            num_scalar_prefetch=1, grid=(DEPTH, r // tm),
            in_specs=[pl.BlockSpec((None, tm, c), lambda l, i, br: (l, i, 0))],
            out_specs=out_spec),
        out_shape=_full_shape(t, DEPTH, BF16),
        compiler_params=_cp("parallel", "parallel"),
    )(b1, shard)


def _gather_weights(fulls):
    def body(*refs):
        f_refs = refs[4:8]
        send_sems, recv_sems = refs[8:10]
        x, y, c, chips = _place()
        sib = (x, y, 1 - c)
        mine, theirs = pl.ds(2 * c, 2), pl.ds(2 * (1 - c), 2)
        me_b = 2 * x + y

        def rcopy(k, src, dst, to):
            return pltpu.make_async_remote_copy(src_ref=src, dst_ref=dst, send_sem=send_sems.at[k],
                                                recv_sem=recv_sems.at[k], device_id=to, device_id_type=MESH)

        sent = []
        for t in range(4):
            own = _block(f_refs[t], mine, t, me_b)
            for j, (cx, cy) in enumerate(chips):
                sent.append(rcopy(3 * t + j, own, own, (cx, cy, c)))
                sent[-1].start()
        for t in range(4):
            for j, (cx, cy) in enumerate(chips):
                k = 3 * t + j
                landed = _block(f_refs[t], mine, t, 2 * cx + cy)
                rcopy(k, landed, landed, (cx, cy, c)).wait_recv()
                sent.append(rcopy(12 + k, landed, landed, sib))
                sent[-1].start()
        for t in range(4):
            for j, (cx, cy) in enumerate(chips):
                passed = _block(f_refs[t], theirs, t, 2 * cx + cy)
                rcopy(12 + 3 * t + j, passed, passed, sib).wait_recv()
        for cp in sent:
            cp.wait_send()

    return pl.pallas_call(
        body, name="gather_weights",
        in_specs=[ANY] * 4, out_specs=[ANY] * 4,
        out_shape=[_full_shape(t, DEPTH, BF16) for t in range(4)],
        input_output_aliases={t: t for t in range(4)},
        scratch_shapes=[pltpu.SemaphoreType.DMA((24,)), pltpu.SemaphoreType.DMA((24,))],
    )(*fulls)


def _swap_halves(grads):
    def body(*refs):
        d_refs, a_refs, send_sems, recv_sems = refs[0:4], refs[4:8], refs[8], refs[9]
        x, y, c, _ = _place()
        cps = [pltpu.make_async_remote_copy(src_ref=d_refs[t].at[pl.ds(2 * (1 - c), 2)], dst_ref=a_refs[t],
                                            send_sem=send_sems.at[t], recv_sem=recv_sems.at[t],
                                            device_id=(x, y, 1 - c), device_id_type=MESH) for t in range(4)]
        for cp in cps:
            cp.start()
        for cp in cps:
            cp.wait()

    return pl.pallas_call(
        body, name="swap_halves",
        in_specs=[ANY] * 4, out_specs=[ANY] * 4,
        out_shape=[_full_shape(t, 2, BF16) for t in range(4)],
        scratch_shapes=[pltpu.SemaphoreType.DMA((4,)), pltpu.SemaphoreType.DMA((4,))],
    )(*grads)


def _add_halves(name, grad, other, c1):
    _, r, c = grad.shape
    tm = min(512, r)

    def body(c_ref, g_ref, o_ref, s_ref):
        del c_ref
        s_ref[...] = (g_ref[...].astype(F32) + o_ref[...].astype(F32)).astype(BF16)

    return pl.pallas_call(
        body, name=name,
        grid_spec=pltpu.PrefetchScalarGridSpec(
            num_scalar_prefetch=1, grid=(2, r // tm),
            in_specs=[pl.BlockSpec((None, tm, c), lambda l, i, cr: (2 * cr[0] + l, i, 0)),
                      pl.BlockSpec((None, tm, c), lambda l, i, cr: (l, i, 0))],
            out_specs=pl.BlockSpec((None, tm, c), lambda l, i, cr: (l, i, 0))),
        out_shape=jax.ShapeDtypeStruct((2, r, c), BF16),
        compiler_params=_cp("parallel", "parallel"),
    )(c1, grad, other)


def _swap_blocks(sums):
    def body(*refs):
        s_refs, r_refs, send_sems, recv_sems = refs[0:4], refs[4:8], refs[8], refs[9]
        x, y, c, chips = _place()
        both = pl.ds(0, 2)
        cps = []
        for t in range(4):
            for j, (cx, cy) in enumerate(chips):
                cps.append(pltpu.make_async_remote_copy(
                    src_ref=_block(s_refs[t], both, t, 2 * cx + cy), dst_ref=r_refs[t].at[j],
                    send_sem=send_sems.at[3 * t + j], recv_sem=recv_sems.at[3 * t + j],
                    device_id=(cx, cy, c), device_id_type=MESH))
                cps[-1].start()
        for cp in cps:
            cp.wait()

    return pl.pallas_call(
        body, name="swap_blocks",
        in_specs=[ANY] * 4, out_specs=[ANY] * 4,
        out_shape=[_block_shape(t, (3, 2), BF16) for t in range(4)],
        scratch_shapes=[pltpu.SemaphoreType.DMA((12,)), pltpu.SemaphoreType.DMA((12,))],
    )(*sums)


def _add_blocks(name, t, own, others, bc):
    _, _, rb, cb = others.shape
    tm = min(256, rb)
    if BLOCK_AXIS[t] == 1:
        own_spec = pl.BlockSpec((None, tm, cb), lambda l, i, br: (l, br[0] * (rb // tm) + i, 0))
    else:
        own_spec = pl.BlockSpec((None, tm, cb), lambda l, i, br: (l, i, br[0]))

    def body(b_ref, o_ref, r0_ref, r1_ref, r2_ref, s_ref):
        del b_ref
        s_ref[...] = ((o_ref[...].astype(F32) + r0_ref[...].astype(F32))
                      + (r1_ref[...].astype(F32) + r2_ref[...].astype(F32)))

    def got(j):
        return pl.BlockSpec((None, None, tm, cb), lambda l, i, br: (j, l, i, 0))

    return pl.pallas_call(
        body, name=name,
        grid_spec=pltpu.PrefetchScalarGridSpec(
            num_scalar_prefetch=1, grid=(2, rb // tm),
            in_specs=[own_spec, got(0), got(1), got(2)],
            out_specs=pl.BlockSpec((None, tm, cb), lambda l, i, br: (2 * br[1] + l, i, 0))),
        out_shape=jax.ShapeDtypeStruct((DEPTH, rb, cb), F32),
        compiler_params=_cp("parallel", "parallel"),
    )(bc, own, others, others, others)


def _join_halves(totals):
    def body(*refs):
        g_refs, send_sems, recv_sems = refs[4:8], refs[8], refs[9]
        x, y, c, _ = _place()

        def layers(t, first):
            return g_refs[t].at[pl.ds(first, 2)]

        cps = [pltpu.make_async_remote_copy(src_ref=layers(t, 2 * c), dst_ref=layers(t, 2 * c),
                                            send_sem=send_sems.at[t], recv_sem=recv_sems.at[t],
                                            device_id=(x, y, 1 - c), device_id_type=MESH) for t in range(4)]
        for cp in cps:
            cp.start()
        for t in range(4):
            theirs = layers(t, 2 * (1 - c))
            pltpu.make_async_remote_copy(src_ref=theirs, dst_ref=theirs, send_sem=send_sems.at[t],
                                         recv_sem=recv_sems.at[t], device_id=(x, y, 1 - c),
                                         device_id_type=MESH).wait_recv()
        for cp in cps:
            cp.wait_send()

    return pl.pallas_call(
        body, name="join_halves",
        in_specs=[ANY] * 4, out_specs=[ANY] * 4,
        out_shape=[_block_shape(t, (DEPTH,), F32) for t in range(4)],
        input_output_aliases={t: t for t in range(4)},
        scratch_shapes=[pltpu.SemaphoreType.DMA((4,)), pltpu.SemaphoreType.DMA((4,))],
    )(*totals)


def _all_gather8(name, v):
    m_per, n = v.shape

    def body(v_ref, out_ref, send_sems, recv_sems, local_sem):
        x, y, c, chips = _place()
        me, sib = (x, y, c), (x, y, 1 - c)

        def rows(px, py, pc):
            return out_ref.at[pl.ds((4 * px + 2 * py + pc) * m_per, m_per), :]

        def copy(k, block, to, src=None):
            return pltpu.make_async_remote_copy(
                src_ref=rows(*block) if src is None else src, dst_ref=rows(*block),
                send_sem=send_sems.at[k], recv_sem=recv_sems.at[k], device_id=to, device_id_type=MESH)

        mine = pltpu.make_async_copy(v_ref, rows(*me), local_sem)
        mine.start()
        first = [copy(0, me, sib, src=v_ref)]
        first += [copy(1 + j, me, (*chip, c), src=v_ref) for j, chip in enumerate(chips)]
        for cp in first:
            cp.start()
        passed = [copy(4 + j, (*chip, c), sib) for j, chip in enumerate(chips)]
        for j, chip in enumerate(chips):
            copy(1 + j, (*chip, c), me).wait_recv()
            passed[j].start()
        copy(0, sib, me).wait_recv()
        for j, chip in enumerate(chips):
            copy(4 + j, (*chip, 1 - c), me).wait_recv()
        for cp in first + passed:
            cp.wait_send()
        mine.wait()

    return pl.pallas_call(
        body, name=name,
        out_shape=jax.ShapeDtypeStruct((8 * m_per, n), v.dtype),
        in_specs=[pl.BlockSpec(memory_space=pltpu.VMEM)],
        out_specs=pl.BlockSpec(memory_space=pltpu.VMEM),
        scratch_shapes=[pltpu.SemaphoreType.DMA((7,)), pltpu.SemaphoreType.DMA((7,)), pltpu.SemaphoreType.DMA],
    )(v)


def _sum8(name, g):
    def body(g_ref, o_ref):
        acc = g_ref[0]
        for d in range(1, 8):
            acc = acc + g_ref[d]
        o_ref[...] = acc

    return pl.pallas_call(body, name=name, out_shape=jax.ShapeDtypeStruct(g.shape[1:], F32))(g)


def _pack(parts):
    flat = []
    for a in parts:
        a = a.reshape(-1)
        flat.append(jnp.pad(a, (0, (-a.shape[0]) % 128)))
    cat = jnp.concatenate(flat)
    cat = jnp.pad(cat, (0, (-cat.shape[0]) % 1024))
    return cat.reshape(-1, 128)


def _unpack(packed, shapes):
    flat = packed.reshape(-1)
    out, at = [], 0
    for shp in shapes:
        n = 1
        for d in shp:
            n *= d
        out.append(flat[at:at + n].reshape(shp))
        at += n + (-n) % 128
    return out


def _local_step(x, target, weights, small):
    w_in, w_out, w_1, w_2 = weights
    s = x.shape[0]
    saved = []
    xin = x
    h = _rmsnorm("norm_first", x, small["norm1_g"][0:1])
    for l in range(DEPTH):
        qg = jnp.tile(small["q_norm_g"][l], 8)[None]
        kg = jnp.tile(small["k_norm_g"][l], 8)[None]
        rb = jnp.pad(small["rel_bias"][l], ((0, 0), (0, NIDX - 257)))
        bias = _bias_layout(_bias_expand(f"bias_expand_{l}", rb))
        cw = small["conv_w"][l]
        pwbd = jax.scipy.linalg.block_diag(*[small["pool_w"][l, g] for g in range(4)])
        ps = small["pool_scale"][l][None]
        p = _mm_nn(f"proj_in_{l}", h, w_in, l, 512, 512, F32)
        q, qt, kp, kt, vp, vt = _qkv(f"qkv_{l}", p, qg, kg)
        o = _attn_fwd(f"attn_fwd_{l}", kp, qt, vt, bias)
        mix = _convpool_fwd(f"convpool_fwd_{l}", p, o, cw, pwbd, ps)
        x1, h2 = _mm_res_norm(f"proj_out_{l}", mix, w_out, l, xin, small["norm2_g"][l:l + 1])
        a, f = _mm_mlp1(f"mlp1_{l}", h2, w_1, l)
        gnext = small["norm1_g"][(l + 1) % DEPTH][None]
        x2, hnext = _mm_res_norm(f"mlp2_{l}", f, w_2, l, x1, gnext)
        saved.append(dict(xin=xin, h=h, p=p, q=q, qt=qt, kp=kp, kt=kt, vp=vp, bias=bias, mix=mix, x1=x1, h2=h2, a=a, f=f,
                          qg=qg, kg=kg, cw=cw, pwbd=pwbd, ps=ps))
        xin, h = x2, hnext

    dx, dxb, loss = _loss_grad("loss_grad", xin, target)
    g_in = lax.empty((DEPTH, D, DIN), BF16)
    g_out = lax.empty((DEPTH, D, D), BF16)
    g_1 = lax.empty((DEPTH, D, DFF), BF16)
    g_2 = lax.empty((DEPTH, DFF, D), BF16)
    gs = {k: [None] * DEPTH for k in ("norm1_g", "q_norm_g", "k_norm_g", "rel_bias", "conv_w", "pool_w",
                                      "pool_scale", "norm2_g")}
    for l in reversed(range(DEPTH)):
        sv = saved[l]
        da = _mm_nt_relu(f"mlp2_bwd_{l}", dxb, w_2, l, sv["a"])
        g_2 = _mm_tn(f"mlp2_wgrad_{l}", sv["f"], dxb, g_2, l, 512, 512)
        g_1 = _mm_tn(f"mlp1_wgrad_{l}", sv["h2"], da, g_1, l, 512, 512)
        dx1, dx1b, dg2 = _mm_nt_normbwd(f"mlp1_bwd_{l}", da, w_1, l, sv["x1"], small["norm2_g"][l:l + 1], dx)
        do, dot, dmix = _proj_out_bwd(f"proj_out_bwd_{l}", dx1b, w_out, l)
        g_out = _mm_tn(f"proj_out_wgrad_{l}", sv["mix"], dx1b, g_out, l, 512, 512)
        dcp, dw0, dw1, dw2, dps, dpw = _convpool_bwd(f"convpool_bwd_{l}", sv["p"], dmix, sv["cw"], sv["pwbd"], sv["ps"])
        dq, dkp, dvp, db = _attn_bwd(f"attn_bwd_{l}", sv["q"], sv["qt"], sv["kp"], sv["kt"], sv["vp"], sv["bias"],
                                     do, dot)
        drb = _bias_reduce(f"bias_reduce_{l}", _bias_unlayout(db))
        dp, dqg, dkg = _qkv_bwd(f"qkv_bwd_{l}", sv["p"], dq, dkp, dvp, dcp, sv["qg"], sv["kg"])
        g_in = _mm_tn(f"proj_in_wgrad_{l}", sv["h"], dp, g_in, l, 512, 640)
        dx, dxb, dg1 = _mm_nt_normbwd(f"proj_in_bwd_{l}", dp, w_in, l, sv["xin"], small["norm1_g"][l:l + 1], dx1)
        gs["norm1_g"][l] = dg1[0]
        gs["q_norm_g"][l] = dqg[0, :HD]
        gs["k_norm_g"][l] = dkg[0, :HD]
        gs["rel_bias"][l] = drb[:, :257]
        gs["conv_w"][l] = jnp.concatenate([dw0, dw1, dw2], axis=0)
        gs["pool_w"][l] = jnp.stack([dpw[g * 64:(g + 1) * 64, g * 64:(g + 1) * 64] for g in range(4)])
        gs["pool_scale"][l] = dps[0]
        gs["norm2_g"][l] = dg2[0]
    gsmall = {k: jnp.stack(v) for k, v in gs.items()}
    return loss, dx, (g_in, g_out, g_1, g_2), gsmall


SMALL = ("norm1_g", "q_norm_g", "k_norm_g", "rel_bias", "conv_w", "pool_w", "pool_scale", "norm2_g")
LARGE = ("w_in", "w_out", "w_mlp1", "w_mlp2")


def kernel(x, norm1_g, w_in, q_norm_g, k_norm_g, rel_bias, conv_w, pool_w, pool_scale, w_out, norm2_g, w_mlp1, w_mlp2, loss_target, m_norm1_g, m_w_in, m_q_norm_g, m_k_norm_g, m_rel_bias, m_conv_w, m_pool_w, m_pool_scale, m_w_out, m_norm2_g, m_w_mlp1, m_w_mlp2, v_norm1_g, v_w_in, v_q_norm_g, v_k_norm_g, v_rel_bias, v_conv_w, v_pool_w, v_pool_scale, v_w_out, v_norm2_g, v_w_mlp1, v_w_mlp2):
    w = dict(norm1_g=norm1_g, w_in=w_in, q_norm_g=q_norm_g, k_norm_g=k_norm_g, rel_bias=rel_bias, conv_w=conv_w,
             pool_w=pool_w, pool_scale=pool_scale, w_out=w_out, norm2_g=norm2_g, w_mlp1=w_mlp1, w_mlp2=w_mlp2)
    m = dict(norm1_g=m_norm1_g, w_in=m_w_in, q_norm_g=m_q_norm_g, k_norm_g=m_k_norm_g, rel_bias=m_rel_bias,
             conv_w=m_conv_w, pool_w=m_pool_w, pool_scale=m_pool_scale, w_out=m_w_out, norm2_g=m_norm2_g,
             w_mlp1=m_w_mlp1, w_mlp2=m_w_mlp2)
    v = dict(norm1_g=v_norm1_g, w_in=v_w_in, q_norm_g=v_q_norm_g, k_norm_g=v_k_norm_g, rel_bias=v_rel_bias,
             conv_w=v_conv_w, pool_w=v_pool_w, pool_scale=v_pool_scale, w_out=v_w_out, norm2_g=v_norm2_g,
             w_mlp1=v_w_mlp1, w_mlp2=v_w_mlp2)
    ax, ay, ac = lax.axis_index("x"), lax.axis_index("y"), lax.axis_index("c")
    c1 = jnp.reshape(ac, (1,)).astype(jnp.int32)
    b1 = jnp.reshape(2 * ax + ay, (1,)).astype(jnp.int32)

    full = _gather_weights([_cast_into_full(f"cast_{n}", t, w[n], b1) for t, n in enumerate(LARGE)])
    cw_rows = _all_gather8("gather_conv_w", jnp.pad(conv_w.reshape(DEPTH * 3, 64), ((0, 4), (0, 64))))
    cw_chips = [cw_rows[(4 * cx + 2 * cy) * 16:(4 * cx + 2 * cy) * 16 + 12, :64] for cx in range(2) for cy in range(2)]
    small = {n: w[n] for n in SMALL}
    small["conv_w"] = jnp.concatenate(cw_chips, axis=1).reshape(DEPTH, 3, CW)

    loss_part, grad_x, glarge, gsmall = _local_step(x[0], loss_target[0], full, small)
    loss = lax.psum(loss_part[0, 0], ("x", "y", "c"))

    other = _swap_halves(glarge)
    sums = [_add_halves(f"add_halves_{LARGE[t]}", glarge[t], other[t], c1) for t in range(4)]
    got = _swap_blocks(sums)
    bc = jnp.stack([2 * ax + ay, ac]).astype(jnp.int32)
    totals = [_add_blocks(f"add_blocks_{LARGE[t]}", t, sums[t], got[t], bc) for t in range(4)]
    reduced = _join_halves(totals)
    out = {}
    for t, n in enumerate(LARGE):
        shp = w[n].shape
        two = lambda a: a.reshape(-1, shp[-1])
        res = _adamw(f"adamw_{n}", two(reduced[t]), two(w[n]), two(m[n]), two(v[n]))
        out[n] = [r.reshape(shp) for r in res]

    order = [n for n in SMALL]
    packed = _pack([gsmall[n] for n in order])
    rows = packed.shape[0]
    summed = _sum8("sum_small", _all_gather8("gather_small", packed).reshape(8, rows, 128))
    gfull = dict(zip(order, _unpack(summed, [gsmall[n].shape for n in order])))
    gfull["conv_w"] = lax.dynamic_slice_in_dim(gfull["conv_w"], (2 * ax + ay) * 64, 64, axis=2)
    res = _adamw("adamw_small", _pack([gfull[n] for n in order]), _pack([w[n] for n in order]),
                 _pack([m[n] for n in order]), _pack([v[n] for n in order]))
    for n, parts in zip(order, zip(*[_unpack(r, [w[k].shape for k in order]) for r in res])):
        out[n] = list(parts)

    names = ("norm1_g", "w_in", "q_norm_g", "k_norm_g", "rel_bias", "conv_w", "pool_w", "pool_scale", "w_out",
             "norm2_g", "w_mlp1", "w_mlp2")
    flat = [loss, grad_x[None]]
    for i in range(4):
        flat += [out[n][i] for n in names]
    return tuple(flat)
```

```python
import functools

import jax
import jax.numpy as jnp
from jax import lax
from jax.experimental import pallas as pl
from jax.experimental.pallas import tpu as pltpu

F32 = jnp.float32
BF16 = jnp.bfloat16

D = 1024
DEPTH = 4
CH = 64
NPREV = 8
KB = (NPREV + 1) * CH
PADR = NPREV * CH
HD = 64
AW = 512
CW = 256
PWD = 256
DIN = 3 * AW + 3 * CW + PWD
DFF = 4 * D
NIDX = 384
EPS = 1e-6
NEG_INF = -1e30

ADAM_LR = 0.001
ADAM_B1 = 0.9
ADAM_B2 = 0.999
ADAM_EPS = 1e-08
ADAM_WD = 0.01
ADAM_STEP = 10

VMEM_LIMIT = 52 * 1024 * 1024
MESH = pl.DeviceIdType.MESH
ANY = pl.BlockSpec(memory_space=pl.ANY)


def _cp(*sem):
    return pltpu.CompilerParams(dimension_semantics=sem, vmem_limit_bytes=VMEM_LIMIT)


def _inv_rms(x):
    return lax.rsqrt(jnp.mean(x * x, axis=-1, keepdims=True) + EPS)


def _head_mean_matrix():
    r = lax.broadcasted_iota(jnp.int32, (AW, AW), 0) // HD
    c = lax.broadcasted_iota(jnp.int32, (AW, AW), 1) // HD
    return jnp.where(r == c, 1.0 / HD, 0.0).astype(BF16)


def _head_mean(x, hm):
    hi = x.astype(BF16)
    lo = (x - hi.astype(F32)).astype(BF16)
    return (jnp.dot(hi, hm, preferred_element_type=F32)
            + jnp.dot(lo, hm, preferred_element_type=F32))


def _rmsnorm(name, x, g):
    s = x.shape[0]
    tm = 512

    def body(x_ref, g_ref, h_ref):
        xv = x_ref[...]
        h_ref[...] = (xv * _inv_rms(xv) * g_ref[...]).astype(BF16)

    return pl.pallas_call(
        body, name=name, grid=(s // tm,),
        in_specs=[pl.BlockSpec((tm, D), lambda i: (i, 0)), pl.BlockSpec((1, D), lambda i: (0, 0))],
        out_specs=pl.BlockSpec((tm, D), lambda i: (i, 0)),
        out_shape=jax.ShapeDtypeStruct((s, D), BF16),
        compiler_params=_cp("parallel"),
    )(x, g)


def _mm_nn(name, a, w, l, tm, tn, out_dtype):
    s, k = a.shape
    n = w.shape[2]

    def body(a_ref, w_ref, o_ref):
        o_ref[...] = jnp.dot(a_ref[...], w_ref[...], preferred_element_type=F32).astype(o_ref.dtype)

    return pl.pallas_call(
        body, name=name, grid=(s // tm, n // tn),
        in_specs=[pl.BlockSpec((tm, k), lambda i, j: (i, 0)),
                  pl.BlockSpec((None, k, tn), lambda i, j: (l, 0, j))],
        out_specs=pl.BlockSpec((tm, tn), lambda i, j: (i, j)),
        out_shape=jax.ShapeDtypeStruct((s, n), out_dtype),
        compiler_params=_cp("parallel", "parallel"),
    )(a, w)


def _mm_mlp1(name, h2, w, l):
    s, k = h2.shape
    n = w.shape[2]
    tm, tn = 512, 1024

    def body(a_ref, w_ref, o_ref, f_ref):
        acc = jnp.dot(a_ref[...], w_ref[...], preferred_element_type=F32)
        o_ref[...] = acc.astype(BF16)
        f_ref[...] = jnp.square(jnp.maximum(acc, 0.0)).astype(BF16)

    return pl.pallas_call(
        body, name=name, grid=(s // tm, n // tn),
        in_specs=[pl.BlockSpec((tm, k), lambda i, j: (i, 0)),
                  pl.BlockSpec((None, k, tn), lambda i, j: (l, 0, j))],
        out_specs=[pl.BlockSpec((tm, tn), lambda i, j: (i, j))] * 2,
        out_shape=[jax.ShapeDtypeStruct((s, n), BF16)] * 2,
        compiler_params=_cp("parallel", "parallel"),
    )(h2, w)


def _mm_res_norm(name, a, w, l, res, g):
    s, k = a.shape
    tm = 256

    def body(a_ref, w_ref, r_ref, g_ref, x_ref, h_ref):
        acc = r_ref[...] + jnp.dot(a_ref[...], w_ref[...], preferred_element_type=F32)
        x_ref[...] = acc
        h_ref[...] = (acc * _inv_rms(acc) * g_ref[...]).astype(BF16)

    return pl.pallas_call(
        body, name=name, grid=(s // tm,),
        in_specs=[pl.BlockSpec((tm, k), lambda i: (i, 0)),
                  pl.BlockSpec((None, k, D), lambda i: (l, 0, 0)),
                  pl.BlockSpec((tm, D), lambda i: (i, 0)),
                  pl.BlockSpec((1, D), lambda i: (0, 0))],
        out_specs=[pl.BlockSpec((tm, D), lambda i: (i, 0))] * 2,
        out_shape=[jax.ShapeDtypeStruct((s, D), F32), jax.ShapeDtypeStruct((s, D), BF16)],
        compiler_params=_cp("parallel"),
    )(a, w, res, g)


def _qkv(name, p, qg, kg):
    s = p.shape[0]
    tm = PADR
    nb = s // tm

    def body(pq_ref, pk_ref, pv_ref, qg_ref, kg_ref, q_ref, qt_ref, k_ref, kt_ref, v_ref, vt_ref):
        t = pl.program_id(0)
        hm = _head_mean_matrix()

        def nrm(x, g):
            return x * lax.rsqrt(_head_mean(x * x, hm) + EPS) * g

        first = t == 0
        qq = nrm(pq_ref[...], qg_ref[...]) * 0.125
        kk = jnp.where(first, 0.0, nrm(pk_ref[...], kg_ref[...]))
        vv = jnp.where(first, 0.0, pv_ref[...])
        q_ref[...] = qq.astype(BF16)
        qt_ref[...] = qq.T.astype(BF16)
        k_ref[...] = kk.astype(BF16)
        kt_ref[...] = kk.T.astype(BF16)
        v_ref[...] = vv.astype(BF16)
        vt_ref[...] = vv.T.astype(BF16)

    def src(col):
        return pl.BlockSpec((tm, AW), lambda t: (jnp.maximum(t - 1, 0), col))

    gspec = pl.BlockSpec((1, AW), lambda t: (0, 0))
    rows = pl.BlockSpec((tm, AW), lambda t: (t, 0))
    cols = pl.BlockSpec((AW, tm), lambda t: (0, t))
    return pl.pallas_call(
        body, name=name, grid=(nb + 1,),
        in_specs=[src(0), src(1), src(2), gspec, gspec],
        out_specs=[pl.BlockSpec((tm, AW), lambda t: (jnp.maximum(t - 1, 0), 0)),
                   pl.BlockSpec((AW, tm), lambda t: (0, jnp.maximum(t - 1, 0))),
                   rows, cols, rows, cols],
        out_shape=[jax.ShapeDtypeStruct((s, AW), BF16), jax.ShapeDtypeStruct((AW, s), BF16),
                   jax.ShapeDtypeStruct((s + PADR, AW), BF16), jax.ShapeDtypeStruct((AW, s + PADR), BF16),
                   jax.ShapeDtypeStruct((s + PADR, AW), BF16), jax.ShapeDtypeStruct((AW, s + PADR), BF16)],
        compiler_params=_cp("arbitrary"),
    )(p, p, p, qg, kg)


NBAND = KB // CH
HIGHEST = lax.Precision.HIGHEST
NT_DIMS = (((1,), (1,)), ((), ()))


def _onehot_table(a):
    m = lax.broadcasted_iota(jnp.int32, (128, NIDX), 0)
    idx = lax.broadcasted_iota(jnp.int32, (128, NIDX), 1)
    rel = jnp.clip(KB - 1 - (CH * a + m), -128, 128) + 128
    return jnp.where(rel == idx, 1.0, 0.0).astype(F32)


def _onehot_diagonal():
    r = lax.broadcasted_iota(jnp.int32, (CH * CH, 128), 0)
    m = lax.broadcasted_iota(jnp.int32, (CH * CH, 128), 1)
    return jnp.where((r % CH) - (r // CH) + (CH - 1) == m, 1.0, 0.0).astype(F32)


def _bias_expand(name, rb):
    def body(rb_ref, o_ref):
        along = [lax.dot_general(rb_ref[...], _onehot_table(a), NT_DIMS, preferred_element_type=F32,
                                 precision=HIGHEST) for a in range(NBAND)]
        o_ref[...] = lax.dot_general(jnp.concatenate(along, axis=0), _onehot_diagonal(), NT_DIMS,
                                     preferred_element_type=F32, precision=HIGHEST)

    return pl.pallas_call(
        body, name=name,
        out_shape=jax.ShapeDtypeStruct((NBAND * 8, CH * CH), F32),
    )(rb)


def _bias_reduce(name, db):
    def body(db_ref, o_ref):
        along = jnp.dot(db_ref[...], _onehot_diagonal(), preferred_element_type=F32, precision=HIGHEST)
        acc = jnp.zeros((8, NIDX), F32)
        for a in range(NBAND):
            acc = acc + jnp.dot(along[8 * a:8 * a + 8, :], _onehot_table(a), preferred_element_type=F32,
                                precision=HIGHEST)
        o_ref[...] = acc

    return pl.pallas_call(
        body, name=name,
        out_shape=jax.ShapeDtypeStruct((8, NIDX), F32),
    )(db)


def _bias_layout(flat):
    b = flat.reshape(NBAND, 8, CH, CH).transpose(1, 0, 3, 2).reshape(4, 2, KB, CH)
    pair = b.transpose(0, 2, 1, 3).reshape(4, KB, 128)
    first = jnp.pad(pair, ((0, 0), (0, CH), (0, 0)), constant_values=NEG_INF)
    second = jnp.pad(pair, ((0, 0), (CH, 0), (0, 0)), constant_values=NEG_INF)
    return jnp.concatenate([first, second], axis=2)


def _bias_unlayout(dbt):
    b = dbt.reshape(4, NBAND, CH, 2, CH)
    return b.transpose(1, 0, 3, 4, 2).reshape(NBAND * 8, CH * CH)


UNIT = 2 * CH
BAND2 = KB + CH


def _pair_weights(xt):
    x = xt.astype(F32)
    row = lax.broadcasted_iota(jnp.int32, (128, UNIT), 0)
    low = lax.broadcasted_iota(jnp.int32, (128, UNIT), 1) < HD
    swapped = pltpu.roll(x, HD, 1)
    same = (row < HD) == low
    first = jnp.where(same, jnp.where(low, x, swapped), 0.0)
    second = jnp.where(same, jnp.where(low, swapped, x), 0.0)
    return jnp.concatenate([first, second], axis=1).astype(BF16)


def _pair_rows(x):
    low = lax.broadcasted_iota(jnp.int32, (CH, 128), 1) < HD
    zero = jnp.zeros((CH, 128), x.dtype)
    parts = []
    for c in range(2):
        xc = x[c * CH:(c + 1) * CH, :]
        parts += [jnp.where(low, xc, zero), jnp.where(low, zero, xc)]
    return jnp.concatenate(parts, axis=0)


def _unpair(raw):
    b0, b1 = raw[:, 0:128], raw[:, 128:256]
    row = lax.broadcasted_iota(jnp.int32, (128, 128), 0)
    low = lax.broadcasted_iota(jnp.int32, (128, 128), 1) < HD
    top = jnp.where(low, b0, pltpu.roll(b1, HD, 1))
    bottom = jnp.where(low, pltpu.roll(b0, HD, 1), b1)
    return jnp.where(row < HD, top, bottom).T


def _softmax_t(kb, qw, bias2, row0):
    s = jnp.dot(kb, qw, preferred_element_type=F32)
    valid = (row0 + lax.broadcasted_iota(jnp.int32, (BAND2, 256), 0)) >= PADR
    s = jnp.where(valid, s + bias2, NEG_INF)
    e = jnp.exp(s - jnp.max(s, axis=0, keepdims=True))
    return e * (1.0 / jnp.sum(e, axis=0, keepdims=True))


def _attn_fwd(name, kp, qt, vt, bias2):
    s = qt.shape[1]

    def body(k_ref, qt_ref, vt_ref, b_ref, o_ref):
        def unit(u, carry):
            r0 = pl.multiple_of(u * UNIT, UNIT)
            pt = _softmax_t(k_ref[pl.ds(r0, BAND2), :], _pair_weights(qt_ref[:, pl.ds(r0, UNIT)]), b_ref[...], r0)
            raw = jnp.dot(vt_ref[:, pl.ds(r0, BAND2)], pt.astype(BF16), preferred_element_type=F32)
            o_ref[pl.ds(r0, UNIT), :] = _unpair(raw).astype(BF16)
            return carry

        lax.fori_loop(0, s // UNIT, unit, 0)

    return pl.pallas_call(
        body, name=name, grid=(AW // 128,),
        in_specs=[pl.BlockSpec((s + PADR, 128), lambda h: (0, h)),
                  pl.BlockSpec((128, s), lambda h: (h, 0)),
                  pl.BlockSpec((128, s + PADR), lambda h: (h, 0)),
                  pl.BlockSpec((None, BAND2, 256), lambda h: (h, 0, 0))],
        out_specs=pl.BlockSpec((s, 128), lambda h: (0, h)),
        out_shape=jax.ShapeDtypeStruct((s, AW), BF16),
        compiler_params=_cp("parallel"),
    )(kp, qt, vt, bias2)


def _attn_bwd(name, q, qt, kp, kt, vp, bias2, do, dot):
    s = q.shape[0]

    def body(q_ref, qt_ref, k_ref, kt_ref, v_ref, b_ref, do_ref, dot_ref, dq_ref, dk_ref, dv_ref, db_ref):
        dk_ref[...] = jnp.zeros_like(dk_ref)
        dv_ref[...] = jnp.zeros_like(dv_ref)
        db_ref[...] = jnp.zeros_like(db_ref)

        def unit(u, carry):
            r0 = pl.multiple_of(u * UNIT, UNIT)
            rows, band = pl.ds(r0, UNIT), pl.ds(r0, BAND2)
            pt = _softmax_t(k_ref[band, :], _pair_weights(qt_ref[:, rows]), b_ref[...], r0)
            dpt = jnp.dot(v_ref[band, :], _pair_weights(dot_ref[:, rows]), preferred_element_type=F32)
            ds = pt * (dpt - jnp.sum(dpt * pt, axis=0, keepdims=True))
            db_ref[...] += ds[0:KB, 0:128] + ds[CH:BAND2, 128:256]
            dsb = ds.astype(BF16)
            dq_ref[rows, :] = _unpair(jnp.dot(kt_ref[:, band], dsb, preferred_element_type=F32))
            dk_ref[band, :] += jnp.dot(dsb, _pair_rows(q_ref[rows, :]), preferred_element_type=F32)
            dv_ref[band, :] += jnp.dot(pt.astype(BF16), _pair_rows(do_ref[rows, :]), preferred_element_type=F32)
            return carry

        lax.fori_loop(0, s // UNIT, unit, 0)

    row_q = pl.BlockSpec((s, 128), lambda h: (0, h))
    col_q = pl.BlockSpec((128, s), lambda h: (h, 0))
    row_k = pl.BlockSpec((s + PADR, 128), lambda h: (0, h))
    col_k = pl.BlockSpec((128, s + PADR), lambda h: (h, 0))
    return pl.pallas_call(
        body, name=name, grid=(AW // 128,),
        in_specs=[row_q, col_q, row_k, col_k, row_k,
                  pl.BlockSpec((None, BAND2, 256), lambda h: (h, 0, 0)), row_q, col_q],
        out_specs=[row_q, row_k, row_k, pl.BlockSpec((None, KB, 128), lambda h: (h, 0, 0))],
        out_shape=[jax.ShapeDtypeStruct((s, AW), F32),
                   jax.ShapeDtypeStruct((s + PADR, AW), F32),
                   jax.ShapeDtypeStruct((s + PADR, AW), F32),
                   jax.ShapeDtypeStruct((4, KB, 128), F32)],
        compiler_params=_cp("parallel"),
    )(q, qt, kp, kt, vp, bias2, do, dot)


def _rows_before(cur, prev, k):
    row = lax.broadcasted_iota(jnp.int32, cur.shape, 0)
    return jnp.where(row >= k, pltpu.roll(cur, k, 0), pltpu.roll(prev, k, 0))


def _rows_after(cur, nxt, k):
    n = cur.shape[0]
    row = lax.broadcasted_iota(jnp.int32, cur.shape, 0)
    return jnp.where(row < n - k, pltpu.roll(cur, n - k, 0), pltpu.roll(nxt, n - k, 0))


def _pool_window_lanes():
    lg = lax.broadcasted_iota(jnp.int32, (1, PWD), 1) // 64
    return lg, jnp.where(lg == 0, 2.0, jnp.where(lg == 1, 4.0, jnp.where(lg == 2, 8.0, 16.0))).astype(F32)


def _pool_mean_minus_token(u, up, row0):
    lg, wv = _pool_window_lanes()
    sums = []
    c, p = u, up
    for k in (1, 2, 4, 8):
        c2 = c + _rows_before(c, p, k)
        p = p + pltpu.roll(p, k, 0)
        c = c2
        sums.append(c)
    win = jnp.where(lg == 0, sums[0], jnp.where(lg == 1, sums[1], jnp.where(lg == 2, sums[2], sums[3])))
    pos1 = (row0 + lax.broadcasted_iota(jnp.int32, u.shape, 0) + 1).astype(F32)
    cnt = jnp.minimum(pos1, wv)
    return win / cnt - u, cnt


def _conv_taps(z, zp, w0, w1, w2):
    z1 = _rows_before(z, zp, 1)
    z2 = _rows_before(z, zp, 2)
    return (w0 * z2 + w1 * z1) + w2 * z, z1, z2


CP_TM = 512


def _convpool_fwd(name, p, o, cw, pwbd, ps):
    s = p.shape[0]
    tm = CP_TM
    nb = s // tm

    def body(gb_ref, gc_ref, hin_ref, u_ref, gcp_ref, hinp_ref, up_ref, o_ref, cw_ref, pw_ref, ps_ref, mix_ref):
        i = pl.program_id(0)
        has_prev = i > 0
        z = gc_ref[...] * hin_ref[...]
        zp = jnp.where(has_prev, gcp_ref[...] * hinp_ref[...], 0.0)
        y3, _, _ = _conv_taps(z, zp, cw_ref[0:1, :], cw_ref[1:2, :], cw_ref[2:3, :])
        m, _ = _pool_mean_minus_token(u_ref[...], jnp.where(has_prev, up_ref[...], 0.0), i * tm)
        yp = jnp.dot(m.astype(BF16), pw_ref[...].astype(BF16), preferred_element_type=F32) * ps_ref[...]
        mix_ref[:, 0:AW] = o_ref[...]
        mix_ref[:, AW:AW + CW] = (gb_ref[...] * y3).astype(BF16)
        mix_ref[:, AW + CW:D] = yp.astype(BF16)

    def cur(col):
        return pl.BlockSpec((tm, CW), lambda i: (i, col))

    def prev(col):
        return pl.BlockSpec((tm, CW), lambda i: (jnp.maximum(i - 1, 0), col))

    def whole(a):
        return pl.BlockSpec(a.shape, lambda i: (0,) * a.ndim)

    return pl.pallas_call(
        body, name=name, grid=(nb,),
        in_specs=[cur(6), cur(7), cur(8), cur(9), prev(7), prev(8), prev(9),
                  pl.BlockSpec((tm, AW), lambda i: (i, 0)), whole(cw), whole(pwbd), whole(ps)],
        out_specs=pl.BlockSpec((tm, D), lambda i: (i, 0)),
        out_shape=jax.ShapeDtypeStruct((s, D), BF16),
        compiler_params=_cp("parallel"),
    )(p, p, p, p, p, p, p, o, cw, pwbd, ps)


def _convpool_bwd(name, p, dmix, cw, pwbd, ps):
    s = p.shape[0]
    tm = CP_TM
    nb = s // tm

    def body(gb_ref, gc_ref, hin_ref, u_ref, gcp_ref, hinp_ref, up_ref, gbn_ref, dyc_ref, dyp_ref, dycn_ref, dypn_ref,
             cw_ref, pw_ref, ps_ref, dcp_ref, dw0_ref, dw1_ref, dw2_ref, dps_ref, dpw_ref):
        i = pl.program_id(0)
        has_prev = i > 0
        has_next = i < nb - 1
        w0, w1, w2 = cw_ref[0:1, :], cw_ref[1:2, :], cw_ref[2:3, :]
        gb, gc, hin = gb_ref[...], gc_ref[...], hin_ref[...]
        dyc = dyc_ref[...]
        z = gc * hin
        zp = jnp.where(has_prev, gcp_ref[...] * hinp_ref[...], 0.0)
        y3, z1, z2 = _conv_taps(z, zp, w0, w1, w2)
        dy3 = dyc * gb
        dy3n = jnp.where(has_next, dycn_ref[...] * gbn_ref[...], 0.0)
        dz = w2 * dy3 + w1 * _rows_after(dy3, dy3n, 1) + w0 * _rows_after(dy3, dy3n, 2)
        pw = pw_ref[...].astype(BF16)
        psv = ps_ref[...]
        m, cnt = _pool_mean_minus_token(u_ref[...], jnp.where(has_prev, up_ref[...], 0.0), i * tm)
        mb = m.astype(BF16)
        dyp = dyp_ref[...]
        dmp = (dyp * psv).astype(BF16)
        dmpn = jnp.where(has_next, dypn_ref[...] * psv, 0.0).astype(BF16)
        nt = (((1,), (1,)), ((), ()))
        dm = lax.dot_general(dmp, pw, nt, preferred_element_type=F32)
        dmn = lax.dot_general(dmpn, pw, nt, preferred_element_type=F32)
        lg, wv = _pool_window_lanes()
        cc, cn = dm / cnt, dmn / wv
        sums = []
        for k in (1, 2, 4, 8):
            c2 = cc + _rows_after(cc, cn, k)
            cn = cn + pltpu.roll(cn, tm - k, 0)
            cc = c2
            sums.append(cc)
        du = jnp.where(lg == 0, sums[0], jnp.where(lg == 1, sums[1], jnp.where(lg == 2, sums[2], sums[3]))) - dm
        dcp_ref[:, 0:CW] = (dyc * y3).astype(BF16)
        dcp_ref[:, CW:2 * CW] = (dz * hin).astype(BF16)
        dcp_ref[:, 2 * CW:3 * CW] = (dz * gc).astype(BF16)
        dcp_ref[:, 3 * CW:4 * CW] = du.astype(BF16)
        parts = (jnp.sum(dy3 * z2, axis=0, keepdims=True),
                 jnp.sum(dy3 * z1, axis=0, keepdims=True),
                 jnp.sum(dy3 * z, axis=0, keepdims=True),
                 jnp.sum(dyp * jnp.dot(mb, pw, preferred_element_type=F32), axis=0, keepdims=True),
                 lax.dot_general(mb, dmp, (((0,), (0,)), ((), ())), preferred_element_type=F32))
        accs = (dw0_ref, dw1_ref, dw2_ref, dps_ref, dpw_ref)

        @pl.when(i == 0)
        def _():
            for a, v in zip(accs, parts):
                a[...] = v

        @pl.when(i > 0)
        def _():
            for a, v in zip(accs, parts):
                a[...] += v

    def cur(col):
        return pl.BlockSpec((tm, CW), lambda i: (i, col))

    def prev(col):
        return pl.BlockSpec((tm, CW), lambda i: (jnp.maximum(i - 1, 0), col))

    def nxt(col):
        return pl.BlockSpec((tm, CW), lambda i: (jnp.minimum(i + 1, nb - 1), col))

    def whole(shape):
        return pl.BlockSpec(shape, lambda i: (0,) * len(shape))

    row = jax.ShapeDtypeStruct((1, CW), F32)
    return pl.pallas_call(
        body, name=name, grid=(nb,),
        in_specs=[cur(6), cur(7), cur(8), cur(9), prev(7), prev(8), prev(9), nxt(6),
                  cur(0), cur(1), nxt(0), nxt(1), whole(cw.shape), whole(pwbd.shape), whole(ps.shape)],
        out_specs=[pl.BlockSpec((tm, D), lambda i: (i, 0)), whole((1, CW)), whole((1, CW)), whole((1, CW)),
                   whole((1, PWD)), whole((PWD, PWD))],
        out_shape=[jax.ShapeDtypeStruct((s, D), BF16), row, row, row, row,
                   jax.ShapeDtypeStruct((PWD, PWD), F32)],
        compiler_params=_cp("arbitrary"),
    )(p, p, p, p, p, p, p, p, dmix, dmix, dmix, dmix, cw, pwbd, ps)


def _qkv_bwd(name, p, dq, dkp, dvp, dcp, qg, kg):
    s = p.shape[0]
    tm = 256
    off = PADR // tm

    def body(pq_ref, pk_ref, dq_ref, dk_ref, dv_ref, dcp_ref, qg_ref, kg_ref, dp_ref, dqg_ref, dkg_ref):
        i = pl.program_id(0)
        hm = _head_mean_matrix()

        def nrm_bwd(x, g, dy):
            r = lax.rsqrt(_head_mean(x * x, hm) + EPS)
            xn = x * r
            dxn = dy * g
            dx = r * (dxn - xn * _head_mean(dxn * xn, hm))
            dg = jnp.sum(dy * xn, axis=0, keepdims=True)
            dg = (dg[:, 0:128] + dg[:, 128:256]) + (dg[:, 256:384] + dg[:, 384:512])
            return dx, dg + pltpu.roll(dg, HD, 1)

        dxq, dgq = nrm_bwd(pq_ref[...], qg_ref[...], dq_ref[...] * 0.125)
        dxk, dgk = nrm_bwd(pk_ref[...], kg_ref[...], dk_ref[...])
        dp_ref[:, 0:AW] = dxq.astype(BF16)
        dp_ref[:, AW:2 * AW] = dxk.astype(BF16)
        dp_ref[:, 2 * AW:3 * AW] = dv_ref[...].astype(BF16)
        dp_ref[:, 3 * AW:DIN] = dcp_ref[...]

        @pl.when(i == 0)
        def _():
            dqg_ref[...] = dgq
            dkg_ref[...] = dgk

        @pl.when(i > 0)
        def _():
            dqg_ref[...] += dgq
            dkg_ref[...] += dgk

    gspec = pl.BlockSpec((1, AW), lambda i: (0, 0))
    gout = pl.BlockSpec((1, 128), lambda i: (0, 0))
    return pl.pallas_call(
        body, name=name, grid=(s // tm,),
        in_specs=[pl.BlockSpec((tm, AW), lambda i: (i, 0)), pl.BlockSpec((tm, AW), lambda i: (i, 1)),
                  pl.BlockSpec((tm, AW), lambda i: (i, 0)),
                  pl.BlockSpec((tm, AW), lambda i: (i + off, 0)),
                  pl.BlockSpec((tm, AW), lambda i: (i + off, 0)),
                  pl.BlockSpec((tm, D), lambda i: (i, 0)), gspec, gspec],
        out_specs=[pl.BlockSpec((tm, DIN), lambda i: (i, 0)), gout, gout],
        out_shape=[jax.ShapeDtypeStruct((s, DIN), BF16), jax.ShapeDtypeStruct((1, 128), F32),
                   jax.ShapeDtypeStruct((1, 128), F32)],
        compiler_params=_cp("arbitrary"),
    )(p, p, dq, dkp, dvp, dcp, qg, kg)


def _loss_grad(name, y, t):
    s = y.shape[0]
    tm = 512

    def body(y_ref, t_ref, dy_ref, dyb_ref, l_ref):
        i = pl.program_id(0)
        e = y_ref[...] - t_ref[...]
        dy = e * (1.0 / D)
        dy_ref[...] = dy
        dyb_ref[...] = dy.astype(BF16)
        part = 0.5 * jnp.sum(jnp.mean(e * e, axis=-1, keepdims=True), axis=0, keepdims=True)

        @pl.when(i == 0)
        def _():
            l_ref[...] = part

        @pl.when(i > 0)
        def _():
            l_ref[...] += part

    blk = pl.BlockSpec((tm, D), lambda i: (i, 0))
    return pl.pallas_call(
        body, name=name, grid=(s // tm,),
        in_specs=[blk, blk],
        out_specs=[blk, blk, pl.BlockSpec((1, 1), lambda i: (0, 0))],
        out_shape=[jax.ShapeDtypeStruct((s, D), F32), jax.ShapeDtypeStruct((s, D), BF16),
                   jax.ShapeDtypeStruct((1, 1), F32)],
        compiler_params=_cp("arbitrary"),
    )(y, t)


def _mm_nt_relu(name, dxb, w, l, a):
    s = dxb.shape[0]
    tm, tn = 512, 1024

    def body(d_ref, w_ref, a_ref, o_ref):
        df = lax.dot_general(d_ref[...], w_ref[...], (((1,), (1,)), ((), ())), preferred_element_type=F32)
        o_ref[...] = (df * (2.0 * jnp.maximum(a_ref[...].astype(F32), 0.0))).astype(BF16)

    return pl.pallas_call(
        body, name=name, grid=(s // tm, DFF // tn),
        in_specs=[pl.BlockSpec((tm, D), lambda i, j: (i, 0)),
                  pl.BlockSpec((None, tn, D), lambda i, j: (l, j, 0)),
                  pl.BlockSpec((tm, tn), lambda i, j: (i, j))],
        out_specs=pl.BlockSpec((tm, tn), lambda i, j: (i, j)),
        out_shape=jax.ShapeDtypeStruct((s, DFF), BF16),
        compiler_params=_cp("parallel", "parallel"),
    )(dxb, w, a)


def _proj_out_bwd(name, dxb, w, l):
    s = dxb.shape[0]
    tm = 512

    def body(d_ref, w_ref, do_ref, dot_ref, dcp_ref):
        d = d_ref[...]
        wa, wc = w_ref[0:AW, :], w_ref[AW:D, :]
        do_ref[...] = lax.dot_general(d, wa, NT_DIMS, preferred_element_type=F32).astype(BF16)
        dot_ref[...] = lax.dot_general(wa, d, NT_DIMS, preferred_element_type=F32).astype(BF16)
        dcp_ref[...] = lax.dot_general(d, wc, NT_DIMS, preferred_element_type=F32)

    return pl.pallas_call(
        body, name=name, grid=(s // tm,),
        in_specs=[pl.BlockSpec((tm, D), lambda i: (i, 0)),
                  pl.BlockSpec((None, D, D), lambda i: (l, 0, 0))],
        out_specs=[pl.BlockSpec((tm, AW), lambda i: (i, 0)), pl.BlockSpec((AW, tm), lambda i: (0, i)),
                   pl.BlockSpec((tm, D - AW), lambda i: (i, 0))],
        out_shape=[jax.ShapeDtypeStruct((s, AW), BF16), jax.ShapeDtypeStruct((AW, s), BF16),
                   jax.ShapeDtypeStruct((s, D - AW), F32)],
        compiler_params=_cp("parallel"),
    )(dxb, w)


def _mm_nt_normbwd(name, gy, w, l, x, g, dres):
    s, k = gy.shape
    tm = 256

    def body(gy_ref, w_ref, x_ref, g_ref, dr_ref, dx_ref, dxb_ref, dg_ref):
        i = pl.program_id(0)
        dh = lax.dot_general(gy_ref[...], w_ref[...], (((1,), (1,)), ((), ())), preferred_element_type=F32)
        xv = x_ref[...]
        r = _inv_rms(xv)
        xn = xv * r
        dxn = dh * g_ref[...]
        dx = r * (dxn - xn * jnp.mean(dxn * xn, axis=-1, keepdims=True)) + dr_ref[...]
        dx_ref[...] = dx
        dxb_ref[...] = dx.astype(BF16)
        part = jnp.sum(dh * xn, axis=0, keepdims=True)

        @pl.when(i == 0)
        def _():
            dg_ref[...] = part

        @pl.when(i > 0)
        def _():
            dg_ref[...] += part

    blk = pl.BlockSpec((tm, D), lambda i: (i, 0))
    vec = pl.BlockSpec((1, D), lambda i: (0, 0))
    return pl.pallas_call(
        body, name=name, grid=(s // tm,),
        in_specs=[pl.BlockSpec((tm, k), lambda i: (i, 0)),
                  pl.BlockSpec((None, D, k), lambda i: (l, 0, 0)), blk, vec, blk],
        out_specs=[blk, blk, vec],
        out_shape=[jax.ShapeDtypeStruct((s, D), F32), jax.ShapeDtypeStruct((s, D), BF16),
                   jax.ShapeDtypeStruct((1, D), F32)],
        compiler_params=_cp("arbitrary"),
    )(gy, w, x, g, dres)


def _mm_tn(name, a, b, acc, l, tma, tnb):
    s, m = a.shape
    n = b.shape[1]

    def body(a_ref, b_ref, acc_ref, o_ref):
        del acc_ref
        o_ref[...] = lax.dot_general(a_ref[...], b_ref[...], (((0,), (0,)), ((), ())),
                                     preferred_element_type=F32).astype(BF16)

    return pl.pallas_call(
        body, name=name, grid=(m // tma, n // tnb),
        in_specs=[pl.BlockSpec((s, tma), lambda i, j: (0, i)),
                  pl.BlockSpec((s, tnb), lambda i, j: (0, j)), ANY],
        out_specs=pl.BlockSpec((None, tma, tnb), lambda i, j: (l, i, j)),
        out_shape=jax.ShapeDtypeStruct(acc.shape, BF16),
        input_output_aliases={2: 0},
        compiler_params=_cp("parallel", "parallel"),
    )(a, b, acc)


def _adamw(name, g, w, m, v):
    r, c = g.shape
    tm = 256 if r % 256 == 0 else r

    def body(g_ref, w_ref, m_ref, v_ref, go_ref, d_ref, mo_ref, vo_ref):
        gv = g_ref[...]
        mn = ADAM_B1 * m_ref[...] + (1.0 - ADAM_B1) * gv
        vn = ADAM_B2 * v_ref[...] + (1.0 - ADAM_B2) * jnp.square(gv)
        m_hat = mn / (1.0 - ADAM_B1 ** ADAM_STEP)
        v_hat = vn / (1.0 - ADAM_B2 ** ADAM_STEP)
        go_ref[...] = gv
        d_ref[...] = -ADAM_LR * (m_hat / (jnp.sqrt(v_hat) + ADAM_EPS) + ADAM_WD * w_ref[...])
        mo_ref[...] = mn
        vo_ref[...] = vn

    blk = pl.BlockSpec((tm, c), lambda i: (i, 0))
    return pl.pallas_call(
        body, name=name, grid=(r // tm,),
        in_specs=[blk] * 4, out_specs=[blk] * 4,
        out_shape=[jax.ShapeDtypeStruct((r, c), F32)] * 4,
        compiler_params=_cp("parallel"),
    )(g, w, m, v)


def _place():
    x, y, c = lax.axis_index("x"), lax.axis_index("y"), lax.axis_index("c")
    chips = [(1 - x, y), (x, 1 - y), (1 - x, 1 - y)]
    return x, y, c, chips


BLOCK_AXIS = (2, 1, 2, 1)
LARGE_DIMS = ((D, DIN), (D, D), (D, DFF), (DFF, D))


def _block(ref, layers, t, b):
    width = LARGE_DIMS[t][BLOCK_AXIS[t] - 1] // 4
    if BLOCK_AXIS[t] == 1:
        return ref.at[layers, pl.ds(pl.multiple_of(b * width, 16), width), :]
    return ref.at[layers, :, pl.ds(pl.multiple_of(b * width, 128), width)]


def _full_shape(t, layers, dtype):
    r, c = LARGE_DIMS[t]
    return jax.ShapeDtypeStruct((layers, r, c), dtype)


def _block_shape(t, lead, dtype):
    r, c = LARGE_DIMS[t]
    if BLOCK_AXIS[t] == 1:
        return jax.ShapeDtypeStruct(lead + (r // 4, c), dtype)
    return jax.ShapeDtypeStruct(lead + (r, c // 4), dtype)


def _cast_into_full(name, t, shard, b1):
    _, r, c = shard.shape
    tm = min(256, r)
    if BLOCK_AXIS[t] == 1:
        out_spec = pl.BlockSpec((None, tm, c), lambda l, i, br: (l, br[0] * (r // tm) + i, 0))
    else:
        out_spec = pl.BlockSpec((None, tm, c), lambda l, i, br: (l, i, br[0]))

    def body(b_ref, x_ref, o_ref):
        del b_ref
        o_ref[...] = x_ref[...].astype(BF16)

    return pl.pallas_call(
        body, name=name,
        grid_spec=pltpu.PrefetchScalarGridSpec(
            num_scalar_prefetch=1, grid=(DEPTH, r // tm),
            in_specs=[pl.BlockSpec((None, tm, c), lambda l, i, br: (l, i, 0))],
            out_specs=out_spec),
        out_shape=_full_shape(t, DEPTH, BF16),
        compiler_params=_cp("parallel", "parallel"),
    )(b1, shard)


HBM = pl.BlockSpec(memory_space=pltpu.HBM)
SEM = pl.BlockSpec(memory_space=pltpu.SEMAPHORE)
DATAFLOW = pltpu.SideEffectType.DATAFLOW_SIDE_EFFECTING


def _half(ref, l, t, b, c):
    r, cols = LARGE_DIMS[t]
    if BLOCK_AXIS[t] == 1:
        n = r // 8
        return ref.at[l, pl.ds(pl.multiple_of(b * (2 * n) + c * n, 16), n), :]
    n, w = r // 2, cols // 4
    return ref.at[l, pl.ds(pl.multiple_of(c * n, 16), n), pl.ds(pl.multiple_of(b * w, 128), w)]


def _gather_start(fulls):
    def body(*refs):
        f_refs, sems = refs[4:8], refs[8:8 + 2 * DEPTH]
        x, y, c, chips = _place()
        for l in range(DEPTH):
            for t in range(4):
                own = _half(f_refs[t], l, t, 2 * x + y, c)
                for j, (cx, cy) in enumerate(chips):
                    pltpu.make_async_remote_copy(src_ref=own, dst_ref=own, send_sem=sems[2 * l].at[3 * t + j],
                                                 recv_sem=sems[2 * l + 1].at[3 * t + j], device_id=(cx, cy, c),
                                                 device_id_type=MESH).start()

    outs = pl.pallas_call(
        body, name="gather_start",
        in_specs=[HBM] * 4, out_specs=[HBM] * 4 + [SEM] * (2 * DEPTH),
        out_shape=[pltpu.HBM(s.shape, s.dtype) for s in (_full_shape(t, DEPTH, BF16) for t in range(4))]
        + [pltpu.SemaphoreType.DMA((12,))] * (2 * DEPTH),
        input_output_aliases={t: t for t in range(4)},
        compiler_params=pltpu.CompilerParams(has_side_effects=DATAFLOW),
    )(*[pltpu.with_memory_space_constraint(f, pltpu.HBM) for f in fulls])
    return outs[0:4], [(outs[4 + 2 * l], outs[5 + 2 * l]) for l in range(DEPTH)]


def _gather_wait(l, fulls, sems, after):
    def body(*refs):
        send_sems, recv_sems, f_refs = refs[4], refs[5], refs[7:11]
        x, y, c, chips = _place()
        for t in range(4):
            own = _half(f_refs[t], l, t, 2 * x + y, c)
            for j, (cx, cy) in enumerate(chips):
                landed = _half(f_refs[t], l, t, 2 * cx + cy, c)
                pltpu.make_async_remote_copy(src_ref=own, dst_ref=landed, send_sem=send_sems.at[3 * t + j],
                                             recv_sem=recv_sems.at[3 * t + j], device_id=(cx, cy, c),
                                             device_id_type=MESH).wait()

    return pl.pallas_call(
        body, name=f"gather_wait_{l}",
        in_specs=[HBM] * 4 + [SEM, SEM, ANY], out_specs=[HBM] * 4,
        out_shape=[pltpu.HBM(s.shape, s.dtype) for s in (_full_shape(t, DEPTH, BF16) for t in range(4))],
        input_output_aliases={t: t for t in range(4)},
        compiler_params=pltpu.CompilerParams(has_side_effects=DATAFLOW),
    )(*fulls, sems[0], sems[1], after)


def _pass_on(l, fulls):
    def body(*refs):
        f_refs, send_sems, recv_sems = refs[4:8], refs[8], refs[9]
        x, y, c, chips = _place()

        def copy(t, j, half):
            cx, cy = chips[j]
            part = _half(f_refs[t], l, t, 2 * cx + cy, half)
            return pltpu.make_async_remote_copy(src_ref=part, dst_ref=part, send_sem=send_sems.at[3 * t + j],
                                                recv_sem=recv_sems.at[3 * t + j], device_id=(x, y, 1 - c),
                                                device_id_type=MESH)

        for t in range(4):
            for j in range(3):
                copy(t, j, c).start()
        for t in range(4):
            for j in range(3):
                copy(t, j, 1 - c).wait_recv()
                copy(t, j, c).wait_send()

    return pl.pallas_call(
        body, name=f"pass_on_{l}",
        in_specs=[ANY] * 4, out_specs=[ANY] * 4,
        out_shape=[_full_shape(t, DEPTH, BF16) for t in range(4)],
        input_output_aliases={t: t for t in range(4)},
        scratch_shapes=[pltpu.SemaphoreType.DMA((12,)), pltpu.SemaphoreType.DMA((12,))],
    )(*fulls)


def _swap_halves(grads):
    def body(*refs):
        d_refs, a_refs, send_sems, recv_sems = refs[0:4], refs[4:8], refs[8], refs[9]
        x, y, c, _ = _place()
        cps = [pltpu.make_async_remote_copy(src_ref=d_refs[t].at[pl.ds(2 * (1 - c), 2)], dst_ref=a_refs[t],
                                            send_sem=send_sems.at[t], recv_sem=recv_sems.at[t],
                                            device_id=(x, y, 1 - c), device_id_type=MESH) for t in range(4)]
        for cp in cps:
            cp.start()
        for cp in cps:
            cp.wait()

    return pl.pallas_call(
        body, name="swap_halves",
        in_specs=[ANY] * 4, out_specs=[ANY] * 4,
        out_shape=[_full_shape(t, 2, BF16) for t in range(4)],
        scratch_shapes=[pltpu.SemaphoreType.DMA((4,)), pltpu.SemaphoreType.DMA((4,))],
    )(*grads)


def _add_halves(name, grad, other, c1):
    _, r, c = grad.shape
    tm = min(512, r)

    def body(c_ref, g_ref, o_ref, s_ref):
        del c_ref
        s_ref[...] = (g_ref[...].astype(F32) + o_ref[...].astype(F32)).astype(BF16)

    return pl.pallas_call(
        body, name=name,
        grid_spec=pltpu.PrefetchScalarGridSpec(
            num_scalar_prefetch=1, grid=(2, r // tm),
            in_specs=[pl.BlockSpec((None, tm, c), lambda l, i, cr: (2 * cr[0] + l, i, 0)),
                      pl.BlockSpec((None, tm, c), lambda l, i, cr: (l, i, 0))],
            out_specs=pl.BlockSpec((None, tm, c), lambda l, i, cr: (l, i, 0))),
        out_shape=jax.ShapeDtypeStruct((2, r, c), BF16),
        compiler_params=_cp("parallel", "parallel"),
    )(c1, grad, other)


def _swap_blocks(sums):
    def body(*refs):
        s_refs, r_refs, send_sems, recv_sems = refs[0:4], refs[4:8], refs[8], refs[9]
        x, y, c, chips = _place()
        both = pl.ds(0, 2)
        cps = []
        for t in range(4):
            for j, (cx, cy) in enumerate(chips):
                cps.append(pltpu.make_async_remote_copy(
                    src_ref=_block(s_refs[t], both, t, 2 * cx + cy), dst_ref=r_refs[t].at[j],
                    send_sem=send_sems.at[3 * t + j], recv_sem=recv_sems.at[3 * t + j],
                    device_id=(cx, cy, c), device_id_type=MESH))
                cps[-1].start()
        for cp in cps:
            cp.wait()

    return pl.pallas_call(
        body, name="swap_blocks",
        in_specs=[ANY] * 4, out_specs=[ANY] * 4,
        out_shape=[_block_shape(t, (3, 2), BF16) for t in range(4)],
        scratch_shapes=[pltpu.SemaphoreType.DMA((12,)), pltpu.SemaphoreType.DMA((12,))],
    )(*sums)


def _add_blocks(name, t, own, others, bc):
    _, _, rb, cb = others.shape
    tm = min(256, rb)
    if BLOCK_AXIS[t] == 1:
        own_spec = pl.BlockSpec((None, tm, cb), lambda l, i, br: (l, br[0] * (rb // tm) + i, 0))
    else:
        own_spec = pl.BlockSpec((None, tm, cb), lambda l, i, br: (l, i, br[0]))

    def body(b_ref, o_ref, r0_ref, r1_ref, r2_ref, s_ref):
        del b_ref
        s_ref[...] = ((o_ref[...].astype(F32) + r0_ref[...].astype(F32))
                      + (r1_ref[...].astype(F32) + r2_ref[...].astype(F32)))

    def got(j):
        return pl.BlockSpec((None, None, tm, cb), lambda l, i, br: (j, l, i, 0))

    return pl.pallas_call(
        body, name=name,
        grid_spec=pltpu.PrefetchScalarGridSpec(
            num_scalar_prefetch=1, grid=(2, rb // tm),
            in_specs=[own_spec, got(0), got(1), got(2)],
            out_specs=pl.BlockSpec((None, tm, cb), lambda l, i, br: (2 * br[1] + l, i, 0))),
        out_shape=jax.ShapeDtypeStruct((DEPTH, rb, cb), F32),
        compiler_params=_cp("parallel", "parallel"),
    )(bc, own, others, others, others)


def _join_halves(totals):
    def body(*refs):
        g_refs, send_sems, recv_sems = refs[4:8], refs[8], refs[9]
        x, y, c, _ = _place()

        def layers(t, first):
            return g_refs[t].at[pl.ds(first, 2)]

        cps = [pltpu.make_async_remote_copy(src_ref=layers(t, 2 * c), dst_ref=layers(t, 2 * c),
                                            send_sem=send_sems.at[t], recv_sem=recv_sems.at[t],
                                            device_id=(x, y, 1 - c), device_id_type=MESH) for t in range(4)]
        for cp in cps:
            cp.start()
        for t in range(4):
            theirs = layers(t, 2 * (1 - c))
            pltpu.make_async_remote_copy(src_ref=theirs, dst_ref=theirs, send_sem=send_sems.at[t],
                                         recv_sem=recv_sems.at[t], device_id=(x, y, 1 - c),
                                         device_id_type=MESH).wait_recv()
        for cp in cps:
            cp.wait_send()

    return pl.pallas_call(
        body, name="join_halves",
        in_specs=[ANY] * 4, out_specs=[ANY] * 4,
        out_shape=[_block_shape(t, (DEPTH,), F32) for t in range(4)],
        input_output_aliases={t: t for t in range(4)},
        scratch_shapes=[pltpu.SemaphoreType.DMA((4,)), pltpu.SemaphoreType.DMA((4,))],
    )(*totals)


def _all_gather8(name, v):
    m_per, n = v.shape

    def body(v_ref, out_ref, send_sems, recv_sems, local_sem):
        x, y, c, chips = _place()
        me, sib = (x, y, c), (x, y, 1 - c)

        def rows(px, py, pc):
            return out_ref.at[pl.ds((4 * px + 2 * py + pc) * m_per, m_per), :]

        def copy(k, block, to, src=None):
            return pltpu.make_async_remote_copy(
                src_ref=rows(*block) if src is None else src, dst_ref=rows(*block),
                send_sem=send_sems.at[k], recv_sem=recv_sems.at[k], device_id=to, device_id_type=MESH)

        mine = pltpu.make_async_copy(v_ref, rows(*me), local_sem)
        mine.start()
        first = [copy(0, me, sib, src=v_ref)]
        first += [copy(1 + j, me, (*chip, c), src=v_ref) for j, chip in enumerate(chips)]
        for cp in first:
            cp.start()
        passed = [copy(4 + j, (*chip, c), sib) for j, chip in enumerate(chips)]
        for j, chip in enumerate(chips):
            copy(1 + j, (*chip, c), me).wait_recv()
            passed[j].start()
        copy(0, sib, me).wait_recv()
        for j, chip in enumerate(chips):
            copy(4 + j, (*chip, 1 - c), me).wait_recv()
        for cp in first + passed:
            cp.wait_send()
        mine.wait()

    return pl.pallas_call(
        body, name=name,
        out_shape=jax.ShapeDtypeStruct((8 * m_per, n), v.dtype),
        in_specs=[pl.BlockSpec(memory_space=pltpu.VMEM)],
        out_specs=pl.BlockSpec(memory_space=pltpu.VMEM),
        scratch_shapes=[pltpu.SemaphoreType.DMA((7,)), pltpu.SemaphoreType.DMA((7,)), pltpu.SemaphoreType.DMA],
    )(v)


def _sum8(name, g):
    def body(g_ref, o_ref):
        acc = g_ref[0]
        for d in range(1, 8):
            acc = acc + g_ref[d]
        o_ref[...] = acc

    return pl.pallas_call(body, name=name, out_shape=jax.ShapeDtypeStruct(g.shape[1:], F32))(g)


def _pack(parts):
    flat = []
    for a in parts:
        a = a.reshape(-1)
        flat.append(jnp.pad(a, (0, (-a.shape[0]) % 128)))
    cat = jnp.concatenate(flat)
    cat = jnp.pad(cat, (0, (-cat.shape[0]) % 1024))
    return cat.reshape(-1, 128)


def _unpack(packed, shapes):
    flat = packed.reshape(-1)
    out, at = [], 0
    for shp in shapes:
        n = 1
        for d in shp:
            n *= d
        out.append(flat[at:at + n].reshape(shp))
        at += n + (-n) % 128
    return out


def _local_step(x, target, layer_weights, small):
    saved = []
    xin = x
    h = _rmsnorm("norm_first", x, small["norm1_g"][0:1])
    for l in range(DEPTH):
        w_in, w_out, w_1, w_2 = layer_weights(l, xin)
        qg = jnp.tile(small["q_norm_g"][l], 8)[None]
        kg = jnp.tile(small["k_norm_g"][l], 8)[None]
        rb = jnp.pad(small["rel_bias"][l], ((0, 0), (0, NIDX - 257)))
        bias = _bias_layout(_bias_expand(f"bias_expand_{l}", rb))
        cw = small["conv_w"][l]
        pwbd = jax.scipy.linalg.block_diag(*[small["pool_w"][l, g] for g in range(4)])
        ps = small["pool_scale"][l][None]
        p = _mm_nn(f"proj_in_{l}", h, w_in, l, 512, 512, F32)
        q, qt, kp, kt, vp, vt = _qkv(f"qkv_{l}", p, qg, kg)
        o = _attn_fwd(f"attn_fwd_{l}", kp, qt, vt, bias)
        mix = _convpool_fwd(f"convpool_fwd_{l}", p, o, cw, pwbd, ps)
        x1, h2 = _mm_res_norm(f"proj_out_{l}", mix, w_out, l, xin, small["norm2_g"][l:l + 1])
        a, f = _mm_mlp1(f"mlp1_{l}", h2, w_1, l)
        gnext = small["norm1_g"][(l + 1) % DEPTH][None]
        x2, hnext = _mm_res_norm(f"mlp2_{l}", f, w_2, l, x1, gnext)
        saved.append(dict(xin=xin, h=h, p=p, q=q, qt=qt, kp=kp, kt=kt, vp=vp, bias=bias, mix=mix, x1=x1, h2=h2, a=a, f=f,
                          qg=qg, kg=kg, cw=cw, pwbd=pwbd, ps=ps))
        xin, h = x2, hnext

    dx, dxb, loss = _loss_grad("loss_grad", xin, target)
    g_in = lax.empty((DEPTH, D, DIN), BF16)
    g_out = lax.empty((DEPTH, D, D), BF16)
    g_1 = lax.empty((DEPTH, D, DFF), BF16)
    g_2 = lax.empty((DEPTH, DFF, D), BF16)
    gs = {k: [None] * DEPTH for k in ("norm1_g", "q_norm_g", "k_norm_g", "rel_bias", "conv_w", "pool_w",
                                      "pool_scale", "norm2_g")}
    for l in reversed(range(DEPTH)):
        sv = saved[l]
        da = _mm_nt_relu(f"mlp2_bwd_{l}", dxb, w_2, l, sv["a"])
        g_2 = _mm_tn(f"mlp2_wgrad_{l}", sv["f"], dxb, g_2, l, 512, 512)
        g_1 = _mm_tn(f"mlp1_wgrad_{l}", sv["h2"], da, g_1, l, 512, 512)
        dx1, dx1b, dg2 = _mm_nt_normbwd(f"mlp1_bwd_{l}", da, w_1, l, sv["x1"], small["norm2_g"][l:l + 1], dx)
        do, dot, dmix = _proj_out_bwd(f"proj_out_bwd_{l}", dx1b, w_out, l)
        g_out = _mm_tn(f"proj_out_wgrad_{l}", sv["mix"], dx1b, g_out, l, 512, 512)
        dcp, dw0, dw1, dw2, dps, dpw = _convpool_bwd(f"convpool_bwd_{l}", sv["p"], dmix, sv["cw"], sv["pwbd"], sv["ps"])
        dq, dkp, dvp, db = _attn_bwd(f"attn_bwd_{l}", sv["q"], sv["qt"], sv["kp"], sv["kt"], sv["vp"], sv["bias"],
                                     do, dot)
        drb = _bias_reduce(f"bias_reduce_{l}", _bias_unlayout(db))
        dp, dqg, dkg = _qkv_bwd(f"qkv_bwd_{l}", sv["p"], dq, dkp, dvp, dcp, sv["qg"], sv["kg"])
        g_in = _mm_tn(f"proj_in_wgrad_{l}", sv["h"], dp, g_in, l, 512, 640)
        dx, dxb, dg1 = _mm_nt_normbwd(f"proj_in_bwd_{l}", dp, w_in, l, sv["xin"], small["norm1_g"][l:l + 1], dx1)
        gs["norm1_g"][l] = dg1[0]
        gs["q_norm_g"][l] = dqg[0, :HD]
        gs["k_norm_g"][l] = dkg[0, :HD]
        gs["rel_bias"][l] = drb[:, :257]
        gs["conv_w"][l] = jnp.concatenate([dw0, dw1, dw2], axis=0)
        gs["pool_w"][l] = jnp.stack([dpw[g * 64:(g + 1) * 64, g * 64:(g + 1) * 64] for g in range(4)])
        gs["pool_scale"][l] = dps[0]
        gs["norm2_g"][l] = dg2[0]
    gsmall = {k: jnp.stack(v) for k, v in gs.items()}
    return loss, dx, (g_in, g_out, g_1, g_2), gsmall


SMALL = ("norm1_g", "q_norm_g", "k_norm_g", "rel_bias", "conv_w", "pool_w", "pool_scale", "norm2_g")
LARGE = ("w_in", "w_out", "w_mlp1", "w_mlp2")


def kernel(x, norm1_g, w_in, q_norm_g, k_norm_g, rel_bias, conv_w, pool_w, pool_scale, w_out, norm2_g, w_mlp1, w_mlp2, loss_target, m_norm1_g, m_w_in, m_q_norm_g, m_k_norm_g, m_rel_bias, m_conv_w, m_pool_w, m_pool_scale, m_w_out, m_norm2_g, m_w_mlp1, m_w_mlp2, v_norm1_g, v_w_in, v_q_norm_g, v_k_norm_g, v_rel_bias, v_conv_w, v_pool_w, v_pool_scale, v_w_out, v_norm2_g, v_w_mlp1, v_w_mlp2):
    w = dict(norm1_g=norm1_g, w_in=w_in, q_norm_g=q_norm_g, k_norm_g=k_norm_g, rel_bias=rel_bias, conv_w=conv_w,
             pool_w=pool_w, pool_scale=pool_scale, w_out=w_out, norm2_g=norm2_g, w_mlp1=w_mlp1, w_mlp2=w_mlp2)
    m = dict(norm1_g=m_norm1_g, w_in=m_w_in, q_norm_g=m_q_norm_g, k_norm_g=m_k_norm_g, rel_bias=m_rel_bias,
             conv_w=m_conv_w, pool_w=m_pool_w, pool_scale=m_pool_scale, w_out=m_w_out, norm2_g=m_norm2_g,
             w_mlp1=m_w_mlp1, w_mlp2=m_w_mlp2)
    v = dict(norm1_g=v_norm1_g, w_in=v_w_in, q_norm_g=v_q_norm_g, k_norm_g=v_k_norm_g, rel_bias=v_rel_bias,
             conv_w=v_conv_w, pool_w=v_pool_w, pool_scale=v_pool_scale, w_out=v_w_out, norm2_g=v_norm2_g,
             w_mlp1=v_w_mlp1, w_mlp2=v_w_mlp2)
    ax, ay, ac = lax.axis_index("x"), lax.axis_index("y"), lax.axis_index("c")
    c1 = jnp.reshape(ac, (1,)).astype(jnp.int32)
    b1 = jnp.reshape(2 * ax + ay, (1,)).astype(jnp.int32)

    arriving, sems = _gather_start([_cast_into_full(f"cast_{n}", t, w[n], b1) for t, n in enumerate(LARGE)])
    held = [arriving]

    def layer_weights(l, after):
        held[0] = _pass_on(l, _gather_wait(l, held[0], sems[l], after))
        return held[0]

    cw_rows = _all_gather8("gather_conv_w", jnp.pad(conv_w.reshape(DEPTH * 3, 64), ((0, 4), (0, 64))))
    cw_chips = [cw_rows[(4 * cx + 2 * cy) * 16:(4 * cx + 2 * cy) * 16 + 12, :64] for cx in range(2) for cy in range(2)]
    small = {n: w[n] for n in SMALL}
    small["conv_w"] = jnp.concatenate(cw_chips, axis=1).reshape(DEPTH, 3, CW)

    loss_part, grad_x, glarge, gsmall = _local_step(x[0], loss_target[0], layer_weights, small)
    loss = lax.psum(loss_part[0, 0], ("x", "y", "c"))

    other = _swap_halves(glarge)
    sums = [_add_halves(f"add_halves_{LARGE[t]}", glarge[t], other[t], c1) for t in range(4)]
    got = _swap_blocks(sums)
    bc = jnp.stack([2 * ax + ay, ac]).astype(jnp.int32)
    totals = [_add_blocks(f"add_blocks_{LARGE[t]}", t, sums[t], got[t], bc) for t in range(4)]
    reduced = _join_halves(totals)
    out = {}
    for t, n in enumerate(LARGE):
        shp = w[n].shape
        two = lambda a: a.reshape(-1, shp[-1])
        res = _adamw(f"adamw_{n}", two(reduced[t]), two(w[n]), two(m[n]), two(v[n]))
        out[n] = [r.reshape(shp) for r in res]

    order = [n for n in SMALL]
    packed = _pack([gsmall[n] for n in order])
    rows = packed.shape[0]
    summed = _sum8("sum_small", _all_gather8("gather_small", packed).reshape(8, rows, 128))
    gfull = dict(zip(order, _unpack(summed, [gsmall[n].shape for n in order])))
    gfull["conv_w"] = lax.dynamic_slice_in_dim(gfull["conv_w"], (2 * ax + ay) * 64, 64, axis=2)
    res = _adamw("adamw_small", _pack([gfull[n] for n in order]), _pack([w[n] for n in order]),
                 _pack([m[n] for n in order]), _pack([v[n] for n in order]))
    for n, parts in zip(order, zip(*[_unpack(r, [w[k].shape for k in order]) for r in res])):
        out[n] = list(parts)

    names = ("norm1_g", "w_in", "q_norm_g", "k_norm_g", "rel_bias", "conv_w", "pool_w", "pool_scale", "w_out",
             "norm2_g", "w_mlp1", "w_mlp2")
    flat = [loss, grad_x[None]]
    for i in range(4):
        flat += [out[n][i] for n in names]
    return tuple(flat)
```

```python
import functools

import jax
import jax.numpy as jnp
from jax import lax
from jax.experimental import pallas as pl
from jax.experimental.pallas import tpu as pltpu

F32 = jnp.float32
BF16 = jnp.bfloat16

D = 1024
DEPTH = 4
CH = 64
NPREV = 8
KB = (NPREV + 1) * CH
PADR = NPREV * CH
HD = 64
AW = 512
CW = 256
PWD = 256
DIN = 3 * AW + 3 * CW + PWD
DFF = 4 * D
NIDX = 384
EPS = 1e-6
NEG_INF = -1e30

ADAM_LR = 0.001
ADAM_B1 = 0.9
ADAM_B2 = 0.999
ADAM_EPS = 1e-08
ADAM_WD = 0.01
ADAM_STEP = 10

VMEM_LIMIT = 52 * 1024 * 1024
MESH = pl.DeviceIdType.MESH
ANY = pl.BlockSpec(memory_space=pl.ANY)


def _cp(*sem):
    return pltpu.CompilerParams(dimension_semantics=sem, vmem_limit_bytes=VMEM_LIMIT)


def _inv_rms(x):
    return lax.rsqrt(jnp.mean(x * x, axis=-1, keepdims=True) + EPS)


def _head_mean_matrix():
    r = lax.broadcasted_iota(jnp.int32, (AW, AW), 0) // HD
    c = lax.broadcasted_iota(jnp.int32, (AW, AW), 1) // HD
    return jnp.where(r == c, 1.0 / HD, 0.0).astype(BF16)


def _head_mean(x, hm):
    hi = x.astype(BF16)
    lo = (x - hi.astype(F32)).astype(BF16)
    return (jnp.dot(hi, hm, preferred_element_type=F32)
            + jnp.dot(lo, hm, preferred_element_type=F32))


def _rmsnorm(name, x, g):
    s = x.shape[0]
    tm = 512

    def body(x_ref, g_ref, h_ref):
        xv = x_ref[...]
        h_ref[...] = (xv * _inv_rms(xv) * g_ref[...]).astype(BF16)

    return pl.pallas_call(
        body, name=name, grid=(s // tm,),
        in_specs=[pl.BlockSpec((tm, D), lambda i: (i, 0)), pl.BlockSpec((1, D), lambda i: (0, 0))],
        out_specs=pl.BlockSpec((tm, D), lambda i: (i, 0)),
        out_shape=jax.ShapeDtypeStruct((s, D), BF16),
        compiler_params=_cp("parallel"),
    )(x, g)


def _mm_nn(name, a, w, l, tm, tn, out_dtype):
    s, k = a.shape
    n = w.shape[2]

    def body(a_ref, w_ref, o_ref):
        o_ref[...] = jnp.dot(a_ref[...], w_ref[...], preferred_element_type=F32).astype(o_ref.dtype)

    return pl.pallas_call(
        body, name=name, grid=(s // tm, n // tn),
        in_specs=[pl.BlockSpec((tm, k), lambda i, j: (i, 0)),
                  pl.BlockSpec((None, k, tn), lambda i, j: (l, 0, j))],
        out_specs=pl.BlockSpec((tm, tn), lambda i, j: (i, j)),
        out_shape=jax.ShapeDtypeStruct((s, n), out_dtype),
        compiler_params=_cp("parallel", "parallel"),
    )(a, w)


def _mm_mlp1(name, h2, w, l):
    s, k = h2.shape
    n = w.shape[2]
    tm, tn = 512, 1024

    def body(a_ref, w_ref, o_ref, f_ref):
        acc = jnp.dot(a_ref[...], w_ref[...], preferred_element_type=F32)
        o_ref[...] = acc.astype(BF16)
        f_ref[...] = jnp.square(jnp.maximum(acc, 0.0)).astype(BF16)

    return pl.pallas_call(
        body, name=name, grid=(s // tm, n // tn),
        in_specs=[pl.BlockSpec((tm, k), lambda i, j: (i, 0)),
                  pl.BlockSpec((None, k, tn), lambda i, j: (l, 0, j))],
        out_specs=[pl.BlockSpec((tm, tn), lambda i, j: (i, j))] * 2,
        out_shape=[jax.ShapeDtypeStruct((s, n), BF16)] * 2,
        compiler_params=_cp("parallel", "parallel"),
    )(h2, w)


def _mm_res_norm(name, a, w, l, res, g):
    s, k = a.shape
    tm = 256

    def body(a_ref, w_ref, r_ref, g_ref, x_ref, h_ref):
        acc = r_ref[...] + jnp.dot(a_ref[...], w_ref[...], preferred_element_type=F32)
        x_ref[...] = acc
        h_ref[...] = (acc * _inv_rms(acc) * g_ref[...]).astype(BF16)

    return pl.pallas_call(
        body, name=name, grid=(s // tm,),
        in_specs=[pl.BlockSpec((tm, k), lambda i: (i, 0)),
                  pl.BlockSpec((None, k, D), lambda i: (l, 0, 0)),
                  pl.BlockSpec((tm, D), lambda i: (i, 0)),
                  pl.BlockSpec((1, D), lambda i: (0, 0))],
        out_specs=[pl.BlockSpec((tm, D), lambda i: (i, 0))] * 2,
        out_shape=[jax.ShapeDtypeStruct((s, D), F32), jax.ShapeDtypeStruct((s, D), BF16)],
        compiler_params=_cp("parallel"),
    )(a, w, res, g)


def _qkv(name, p, qg, kg):
    s = p.shape[0]
    tm = PADR
    nb = s // tm

    def body(pq_ref, pk_ref, pv_ref, qg_ref, kg_ref, q_ref, qt_ref, k_ref, kt_ref, v_ref, vt_ref):
        t = pl.program_id(0)
        hm = _head_mean_matrix()

        def nrm(x, g):
            return x * lax.rsqrt(_head_mean(x * x, hm) + EPS) * g

        first = t == 0
        qq = nrm(pq_ref[...], qg_ref[...]) * 0.125
        kk = jnp.where(first, 0.0, nrm(pk_ref[...], kg_ref[...]))
        vv = jnp.where(first, 0.0, pv_ref[...])
        q_ref[...] = qq.astype(BF16)
        qt_ref[...] = qq.T.astype(BF16)
        k_ref[...] = kk.astype(BF16)
        kt_ref[...] = kk.T.astype(BF16)
        v_ref[...] = vv.astype(BF16)
        vt_ref[...] = vv.T.astype(BF16)

    def src(col):
        return pl.BlockSpec((tm, AW), lambda t: (jnp.maximum(t - 1, 0), col))

    gspec = pl.BlockSpec((1, AW), lambda t: (0, 0))
    rows = pl.BlockSpec((tm, AW), lambda t: (t, 0))
    cols = pl.BlockSpec((AW, tm), lambda t: (0, t))
    return pl.pallas_call(
        body, name=name, grid=(nb + 1,),
        in_specs=[src(0), src(1), src(2), gspec, gspec],
        out_specs=[pl.BlockSpec((tm, AW), lambda t: (jnp.maximum(t - 1, 0), 0)),
                   pl.BlockSpec((AW, tm), lambda t: (0, jnp.maximum(t - 1, 0))),
                   rows, cols, rows, cols],
        out_shape=[jax.ShapeDtypeStruct((s, AW), BF16), jax.ShapeDtypeStruct((AW, s), BF16),
                   jax.ShapeDtypeStruct((s + PADR, AW), BF16), jax.ShapeDtypeStruct((AW, s + PADR), BF16),
                   jax.ShapeDtypeStruct((s + PADR, AW), BF16), jax.ShapeDtypeStruct((AW, s + PADR), BF16)],
        compiler_params=_cp("arbitrary"),
    )(p, p, p, qg, kg)


NBAND = KB // CH
HIGHEST = lax.Precision.HIGHEST
NT_DIMS = (((1,), (1,)), ((), ()))


def _onehot_table(a):
    m = lax.broadcasted_iota(jnp.int32, (128, NIDX), 0)
    idx = lax.broadcasted_iota(jnp.int32, (128, NIDX), 1)
    rel = jnp.clip(KB - 1 - (CH * a + m), -128, 128) + 128
    return jnp.where(rel == idx, 1.0, 0.0).astype(F32)


def _onehot_diagonal():
    r = lax.broadcasted_iota(jnp.int32, (CH * CH, 128), 0)
    m = lax.broadcasted_iota(jnp.int32, (CH * CH, 128), 1)
    return jnp.where((r % CH) - (r // CH) + (CH - 1) == m, 1.0, 0.0).astype(F32)


def _bias_expand(name, rb):
    def body(rb_ref, o_ref):
        along = [lax.dot_general(rb_ref[...], _onehot_table(a), NT_DIMS, preferred_element_type=F32,
                                 precision=HIGHEST) for a in range(NBAND)]
        o_ref[...] = lax.dot_general(jnp.concatenate(along, axis=0), _onehot_diagonal(), NT_DIMS,
                                     preferred_element_type=F32, precision=HIGHEST)

    return pl.pallas_call(
        body, name=name,
        out_shape=jax.ShapeDtypeStruct((NBAND * 8, CH * CH), F32),
    )(rb)


def _bias_reduce(name, db):
    def body(db_ref, o_ref):
        along = jnp.dot(db_ref[...], _onehot_diagonal(), preferred_element_type=F32, precision=HIGHEST)
        acc = jnp.zeros((8, NIDX), F32)
        for a in range(NBAND):
            acc = acc + jnp.dot(along[8 * a:8 * a + 8, :], _onehot_table(a), preferred_element_type=F32,
                                precision=HIGHEST)
        o_ref[...] = acc

    return pl.pallas_call(
        body, name=name,
        out_shape=jax.ShapeDtypeStruct((8, NIDX), F32),
    )(db)


def _bias_layout(flat):
    b = flat.reshape(NBAND, 8, CH, CH).transpose(1, 0, 3, 2).reshape(4, 2, KB, CH)
    pair = b.transpose(0, 2, 1, 3).reshape(4, KB, 128)
    first = jnp.pad(pair, ((0, 0), (0, CH), (0, 0)), constant_values=NEG_INF)
    second = jnp.pad(pair, ((0, 0), (CH, 0), (0, 0)), constant_values=NEG_INF)
    return jnp.concatenate([first, second], axis=2)


def _bias_unlayout(dbt):
    b = dbt.reshape(4, NBAND, CH, 2, CH)
    return b.transpose(1, 0, 3, 4, 2).reshape(NBAND * 8, CH * CH)


UNIT = 2 * CH
BAND2 = KB + CH


def _pair_weights(xt):
    x = xt.astype(F32)
    row = lax.broadcasted_iota(jnp.int32, (128, UNIT), 0)
    low = lax.broadcasted_iota(jnp.int32, (128, UNIT), 1) < HD
    swapped = pltpu.roll(x, HD, 1)
    same = (row < HD) == low
    first = jnp.where(same, jnp.where(low, x, swapped), 0.0)
    second = jnp.where(same, jnp.where(low, swapped, x), 0.0)
    return jnp.concatenate([first, second], axis=1).astype(BF16)


def _pair_rows(x):
    low = lax.broadcasted_iota(jnp.int32, (CH, 128), 1) < HD
    zero = jnp.zeros((CH, 128), x.dtype)
    parts = []
    for c in range(2):
        xc = x[c * CH:(c + 1) * CH, :]
        parts += [jnp.where(low, xc, zero), jnp.where(low, zero, xc)]
    return jnp.concatenate(parts, axis=0)


def _unpair(raw):
    b0, b1 = raw[:, 0:128], raw[:, 128:256]
    row = lax.broadcasted_iota(jnp.int32, (128, 128), 0)
    low = lax.broadcasted_iota(jnp.int32, (128, 128), 1) < HD
    top = jnp.where(low, b0, pltpu.roll(b1, HD, 1))
    bottom = jnp.where(low, pltpu.roll(b0, HD, 1), b1)
    return jnp.where(row < HD, top, bottom).T


def _softmax_t(kb, qw, bias2, row0):
    s = jnp.dot(kb, qw, preferred_element_type=F32)
    valid = (row0 + lax.broadcasted_iota(jnp.int32, (BAND2, 256), 0)) >= PADR
    s = jnp.where(valid, s + bias2, NEG_INF)
    e = jnp.exp(s - jnp.max(s, axis=0, keepdims=True))
    return e * (1.0 / jnp.sum(e, axis=0, keepdims=True))


def _attn_fwd(name, kp, qt, vt, bias2):
    s = qt.shape[1]

    def body(k_ref, qt_ref, vt_ref, b_ref, o_ref):
        def unit(u, carry):
            r0 = pl.multiple_of(u * UNIT, UNIT)
            pt = _softmax_t(k_ref[pl.ds(r0, BAND2), :], _pair_weights(qt_ref[:, pl.ds(r0, UNIT)]), b_ref[...], r0)
            raw = jnp.dot(vt_ref[:, pl.ds(r0, BAND2)], pt.astype(BF16), preferred_element_type=F32)
            o_ref[pl.ds(r0, UNIT), :] = _unpair(raw).astype(BF16)
            return carry

        lax.fori_loop(0, s // UNIT, unit, 0)

    return pl.pallas_call(
        body, name=name, grid=(AW // 128,),
        in_specs=[pl.BlockSpec((s + PADR, 128), lambda h: (0, h)),
                  pl.BlockSpec((128, s), lambda h: (h, 0)),
                  pl.BlockSpec((128, s + PADR), lambda h: (h, 0)),
                  pl.BlockSpec((None, BAND2, 256), lambda h: (h, 0, 0))],
        out_specs=pl.BlockSpec((s, 128), lambda h: (0, h)),
        out_shape=jax.ShapeDtypeStruct((s, AW), BF16),
        compiler_params=_cp("parallel"),
    )(kp, qt, vt, bias2)


def _attn_bwd(name, q, qt, kp, kt, vp, bias2, do, dot):
    s = q.shape[0]

    def body(q_ref, qt_ref, k_ref, kt_ref, v_ref, b_ref, do_ref, dot_ref, dq_ref, dk_ref, dv_ref, db_ref):
        dk_ref[...] = jnp.zeros_like(dk_ref)
        dv_ref[...] = jnp.zeros_like(dv_ref)
        db_ref[...] = jnp.zeros_like(db_ref)

        def unit(u, carry):
            r0 = pl.multiple_of(u * UNIT, UNIT)
            rows, band = pl.ds(r0, UNIT), pl.ds(r0, BAND2)
            pt = _softmax_t(k_ref[band, :], _pair_weights(qt_ref[:, rows]), b_ref[...], r0)
            dpt = jnp.dot(v_ref[band, :], _pair_weights(dot_ref[:, rows]), preferred_element_type=F32)
            ds = pt * (dpt - jnp.sum(dpt * pt, axis=0, keepdims=True))
            db_ref[...] += ds[0:KB, 0:128] + ds[CH:BAND2, 128:256]
            dsb = ds.astype(BF16)
            dq_ref[rows, :] = _unpair(jnp.dot(kt_ref[:, band], dsb, preferred_element_type=F32))
            dk_ref[band, :] += jnp.dot(dsb, _pair_rows(q_ref[rows, :]), preferred_element_type=F32)
            dv_ref[band, :] += jnp.dot(pt.astype(BF16), _pair_rows(do_ref[rows, :]), preferred_element_type=F32)
            return carry

        lax.fori_loop(0, s // UNIT, unit, 0)

    row_q = pl.BlockSpec((s, 128), lambda h: (0, h))
    col_q = pl.BlockSpec((128, s), lambda h: (h, 0))
    row_k = pl.BlockSpec((s + PADR, 128), lambda h: (0, h))
    col_k = pl.BlockSpec((128, s + PADR), lambda h: (h, 0))
    return pl.pallas_call(
        body, name=name, grid=(AW // 128,),
        in_specs=[row_q, col_q, row_k, col_k, row_k,
                  pl.BlockSpec((None, BAND2, 256), lambda h: (h, 0, 0)), row_q, col_q],
        out_specs=[row_q, row_k, row_k, pl.BlockSpec((None, KB, 128), lambda h: (h, 0, 0))],
        out_shape=[jax.ShapeDtypeStruct((s, AW), F32),
                   jax.ShapeDtypeStruct((s + PADR, AW), F32),
                   jax.ShapeDtypeStruct((s + PADR, AW), F32),
                   jax.ShapeDtypeStruct((4, KB, 128), F32)],
        compiler_params=_cp("parallel"),
    )(q, qt, kp, kt, vp, bias2, do, dot)


def _rows_before(cur, prev, k):
    row = lax.broadcasted_iota(jnp.int32, cur.shape, 0)
    return jnp.where(row >= k, pltpu.roll(cur, k, 0), pltpu.roll(prev, k, 0))


def _rows_after(cur, nxt, k):
    n = cur.shape[0]
    row = lax.broadcasted_iota(jnp.int32, cur.shape, 0)
    return jnp.where(row < n - k, pltpu.roll(cur, n - k, 0), pltpu.roll(nxt, n - k, 0))


def _pool_window_lanes():
    lg = lax.broadcasted_iota(jnp.int32, (1, PWD), 1) // 64
    return lg, jnp.where(lg == 0, 2.0, jnp.where(lg == 1, 4.0, jnp.where(lg == 2, 8.0, 16.0))).astype(F32)


def _pool_mean_minus_token(u, up, row0):
    lg, wv = _pool_window_lanes()
    sums = []
    c, p = u, up
    for k in (1, 2, 4, 8):
        c2 = c + _rows_before(c, p, k)
        p = p + pltpu.roll(p, k, 0)
        c = c2
        sums.append(c)
    win = jnp.where(lg == 0, sums[0], jnp.where(lg == 1, sums[1], jnp.where(lg == 2, sums[2], sums[3])))
    pos1 = (row0 + lax.broadcasted_iota(jnp.int32, u.shape, 0) + 1).astype(F32)
    cnt = jnp.minimum(pos1, wv)
    return win / cnt - u, cnt


def _conv_taps(z, zp, w0, w1, w2):
    z1 = _rows_before(z, zp, 1)
    z2 = _rows_before(z, zp, 2)
    return (w0 * z2 + w1 * z1) + w2 * z, z1, z2


CP_TM = 512


def _convpool_fwd(name, p, o, cw, pwbd, ps):
    s = p.shape[0]
    tm = CP_TM
    nb = s // tm

    def body(gb_ref, gc_ref, hin_ref, u_ref, gcp_ref, hinp_ref, up_ref, o_ref, cw_ref, pw_ref, ps_ref, mix_ref):
        i = pl.program_id(0)
        has_prev = i > 0
        z = gc_ref[...] * hin_ref[...]
        zp = jnp.where(has_prev, gcp_ref[...] * hinp_ref[...], 0.0)
        y3, _, _ = _conv_taps(z, zp, cw_ref[0:1, :], cw_ref[1:2, :], cw_ref[2:3, :])
        m, _ = _pool_mean_minus_token(u_ref[...], jnp.where(has_prev, up_ref[...], 0.0), i * tm)
        yp = jnp.dot(m.astype(BF16), pw_ref[...].astype(BF16), preferred_element_type=F32) * ps_ref[...]
        mix_ref[:, 0:AW] = o_ref[...]
        mix_ref[:, AW:AW + CW] = (gb_ref[...] * y3).astype(BF16)
        mix_ref[:, AW + CW:D] = yp.astype(BF16)

    def cur(col):
        return pl.BlockSpec((tm, CW), lambda i: (i, col))

    def prev(col):
        return pl.BlockSpec((tm, CW), lambda i: (jnp.maximum(i - 1, 0), col))

    def whole(a):
        return pl.BlockSpec(a.shape, lambda i: (0,) * a.ndim)

    return pl.pallas_call(
        body, name=name, grid=(nb,),
        in_specs=[cur(6), cur(7), cur(8), cur(9), prev(7), prev(8), prev(9),
                  pl.BlockSpec((tm, AW), lambda i: (i, 0)), whole(cw), whole(pwbd), whole(ps)],
        out_specs=pl.BlockSpec((tm, D), lambda i: (i, 0)),
        out_shape=jax.ShapeDtypeStruct((s, D), BF16),
        compiler_params=_cp("parallel"),
    )(p, p, p, p, p, p, p, o, cw, pwbd, ps)


def _convpool_bwd(name, p, dmix, cw, pwbd, ps):
    s = p.shape[0]
    tm = CP_TM
    nb = s // tm

    def body(gb_ref, gc_ref, hin_ref, u_ref, gcp_ref, hinp_ref, up_ref, gbn_ref, dyc_ref, dyp_ref, dycn_ref, dypn_ref,
             cw_ref, pw_ref, ps_ref, dcp_ref, dw0_ref, dw1_ref, dw2_ref, dps_ref, dpw_ref):
        i = pl.program_id(0)
        has_prev = i > 0
        has_next = i < nb - 1
        w0, w1, w2 = cw_ref[0:1, :], cw_ref[1:2, :], cw_ref[2:3, :]
        gb, gc, hin = gb_ref[...], gc_ref[...], hin_ref[...]
        dyc = dyc_ref[...]
        z = gc * hin
        zp = jnp.where(has_prev, gcp_ref[...] * hinp_ref[...], 0.0)
        y3, z1, z2 = _conv_taps(z, zp, w0, w1, w2)
        dy3 = dyc * gb
        dy3n = jnp.where(has_next, dycn_ref[...] * gbn_ref[...], 0.0)
        dz = w2 * dy3 + w1 * _rows_after(dy3, dy3n, 1) + w0 * _rows_after(dy3, dy3n, 2)
        pw = pw_ref[...].astype(BF16)
        psv = ps_ref[...]
        m, cnt = _pool_mean_minus_token(u_ref[...], jnp.where(has_prev, up_ref[...], 0.0), i * tm)
        mb = m.astype(BF16)
        dyp = dyp_ref[...]
        dmp = (dyp * psv).astype(BF16)
        dmpn = jnp.where(has_next, dypn_ref[...] * psv, 0.0).astype(BF16)
        nt = (((1,), (1,)), ((), ()))
        dm = lax.dot_general(dmp, pw, nt, preferred_element_type=F32)
        dmn = lax.dot_general(dmpn, pw, nt, preferred_element_type=F32)
        lg, wv = _pool_window_lanes()
        cc, cn = dm / cnt, dmn / wv
        sums = []
        for k in (1, 2, 4, 8):
            c2 = cc + _rows_after(cc, cn, k)
            cn = cn + pltpu.roll(cn, tm - k, 0)
            cc = c2
            sums.append(cc)
        du = jnp.where(lg == 0, sums[0], jnp.where(lg == 1, sums[1], jnp.where(lg == 2, sums[2], sums[3]))) - dm
        dcp_ref[:, 0:CW] = (dyc * y3).astype(BF16)
        dcp_ref[:, CW:2 * CW] = (dz * hin).astype(BF16)
        dcp_ref[:, 2 * CW:3 * CW] = (dz * gc).astype(BF16)
        dcp_ref[:, 3 * CW:4 * CW] = du.astype(BF16)
        parts = (jnp.sum(dy3 * z2, axis=0, keepdims=True),
                 jnp.sum(dy3 * z1, axis=0, keepdims=True),
                 jnp.sum(dy3 * z, axis=0, keepdims=True),
                 jnp.sum(dyp * jnp.dot(mb, pw, preferred_element_type=F32), axis=0, keepdims=True),
                 lax.dot_general(mb, dmp, (((0,), (0,)), ((), ())), preferred_element_type=F32))
        accs = (dw0_ref, dw1_ref, dw2_ref, dps_ref, dpw_ref)

        @pl.when(i == 0)
        def _():
            for a, v in zip(accs, parts):
                a[...] = v

        @pl.when(i > 0)
        def _():
            for a, v in zip(accs, parts):
                a[...] += v

    def cur(col):
        return pl.BlockSpec((tm, CW), lambda i: (i, col))

    def prev(col):
        return pl.BlockSpec((tm, CW), lambda i: (jnp.maximum(i - 1, 0), col))

    def nxt(col):
        return pl.BlockSpec((tm, CW), lambda i: (jnp.minimum(i + 1, nb - 1), col))

    def whole(shape):
        return pl.BlockSpec(shape, lambda i: (0,) * len(shape))

    row = jax.ShapeDtypeStruct((1, CW), F32)
    return pl.pallas_call(
        body, name=name, grid=(nb,),
        in_specs=[cur(6), cur(7), cur(8), cur(9), prev(7), prev(8), prev(9), nxt(6),
                  cur(0), cur(1), nxt(0), nxt(1), whole(cw.shape), whole(pwbd.shape), whole(ps.shape)],
        out_specs=[pl.BlockSpec((tm, D), lambda i: (i, 0)), whole((1, CW)), whole((1, CW)), whole((1, CW)),
                   whole((1, PWD)), whole((PWD, PWD))],
        out_shape=[jax.ShapeDtypeStruct((s, D), BF16), row, row, row, row,
                   jax.ShapeDtypeStruct((PWD, PWD), F32)],
        compiler_params=_cp("arbitrary"),
    )(p, p, p, p, p, p, p, p, dmix, dmix, dmix, dmix, cw, pwbd, ps)


def _qkv_bwd(name, p, dq, dkp, dvp, dcp, qg, kg):
    s = p.shape[0]
    tm = 256
    off = PADR // tm

    def body(pq_ref, pk_ref, dq_ref, dk_ref, dv_ref, dcp_ref, qg_ref, kg_ref, dp_ref, dqg_ref, dkg_ref):
        i = pl.program_id(0)
        hm = _head_mean_matrix()

        def nrm_bwd(x, g, dy):
            r = lax.rsqrt(_head_mean(x * x, hm) + EPS)
            xn = x * r
            dxn = dy * g
            dx = r * (dxn - xn * _head_mean(dxn * xn, hm))
            dg = jnp.sum(dy * xn, axis=0, keepdims=True)
            dg = (dg[:, 0:128] + dg[:, 128:256]) + (dg[:, 256:384] + dg[:, 384:512])
            return dx, dg + pltpu.roll(dg, HD, 1)

        dxq, dgq = nrm_bwd(pq_ref[...], qg_ref[...], dq_ref[...] * 0.125)
        dxk, dgk = nrm_bwd(pk_ref[...], kg_ref[...], dk_ref[...])
        dp_ref[:, 0:AW] = dxq.astype(BF16)
        dp_ref[:, AW:2 * AW] = dxk.astype(BF16)
        dp_ref[:, 2 * AW:3 * AW] = dv_ref[...].astype(BF16)
        dp_ref[:, 3 * AW:DIN] = dcp_ref[...]

        @pl.when(i == 0)
        def _():
            dqg_ref[...] = dgq
            dkg_ref[...] = dgk

        @pl.when(i > 0)
        def _():
            dqg_ref[...] += dgq
            dkg_ref[...] += dgk

    gspec = pl.BlockSpec((1, AW), lambda i: (0, 0))
    gout = pl.BlockSpec((1, 128), lambda i: (0, 0))
    return pl.pallas_call(
        body, name=name, grid=(s // tm,),
        in_specs=[pl.BlockSpec((tm, AW), lambda i: (i, 0)), pl.BlockSpec((tm, AW), lambda i: (i, 1)),
                  pl.BlockSpec((tm, AW), lambda i: (i, 0)),
                  pl.BlockSpec((tm, AW), lambda i: (i + off, 0)),
                  pl.BlockSpec((tm, AW), lambda i: (i + off, 0)),
                  pl.BlockSpec((tm, D), lambda i: (i, 0)), gspec, gspec],
        out_specs=[pl.BlockSpec((tm, DIN), lambda i: (i, 0)), gout, gout],
        out_shape=[jax.ShapeDtypeStruct((s, DIN), BF16), jax.ShapeDtypeStruct((1, 128), F32),
                   jax.ShapeDtypeStruct((1, 128), F32)],
        compiler_params=_cp("arbitrary"),
    )(p, p, dq, dkp, dvp, dcp, qg, kg)


def _loss_grad(name, y, t):
    s = y.shape[0]
    tm = 512

    def body(y_ref, t_ref, dy_ref, dyb_ref, l_ref):
        i = pl.program_id(0)
        e = y_ref[...] - t_ref[...]
        dy = e * (1.0 / D)
        dy_ref[...] = dy
        dyb_ref[...] = dy.astype(BF16)
        part = 0.5 * jnp.sum(jnp.mean(e * e, axis=-1, keepdims=True), axis=0, keepdims=True)

        @pl.when(i == 0)
        def _():
            l_ref[...] = part

        @pl.when(i > 0)
        def _():
            l_ref[...] += part

    blk = pl.BlockSpec((tm, D), lambda i: (i, 0))
    return pl.pallas_call(
        body, name=name, grid=(s // tm,),
        in_specs=[blk, blk],
        out_specs=[blk, blk, pl.BlockSpec((1, 1), lambda i: (0, 0))],
        out_shape=[jax.ShapeDtypeStruct((s, D), F32), jax.ShapeDtypeStruct((s, D), BF16),
                   jax.ShapeDtypeStruct((1, 1), F32)],
        compiler_params=_cp("arbitrary"),
    )(y, t)


def _mm_nt_relu(name, dxb, w, l, a):
    s = dxb.shape[0]
    tm, tn = 512, 1024

    def body(d_ref, w_ref, a_ref, o_ref):
        df = lax.dot_general(d_ref[...], w_ref[...], (((1,), (1,)), ((), ())), preferred_element_type=F32)
        o_ref[...] = (df * (2.0 * jnp.maximum(a_ref[...].astype(F32), 0.0))).astype(BF16)

    return pl.pallas_call(
        body, name=name, grid=(s // tm, DFF // tn),
        in_specs=[pl.BlockSpec((tm, D), lambda i, j: (i, 0)),
                  pl.BlockSpec((None, tn, D), lambda i, j: (l, j, 0)),
                  pl.BlockSpec((tm, tn), lambda i, j: (i, j))],
        out_specs=pl.BlockSpec((tm, tn), lambda i, j: (i, j)),
        out_shape=jax.ShapeDtypeStruct((s, DFF), BF16),
        compiler_params=_cp("parallel", "parallel"),
    )(dxb, w, a)


def _proj_out_bwd(name, dxb, w, l):
    s = dxb.shape[0]
    tm = 512

    def body(d_ref, w_ref, do_ref, dot_ref, dcp_ref):
        d = d_ref[...]
        wa, wc = w_ref[0:AW, :], w_ref[AW:D, :]
        do_ref[...] = lax.dot_general(d, wa, NT_DIMS, preferred_element_type=F32).astype(BF16)
        dot_ref[...] = lax.dot_general(wa, d, NT_DIMS, preferred_element_type=F32).astype(BF16)
        dcp_ref[...] = lax.dot_general(d, wc, NT_DIMS, preferred_element_type=F32)

    return pl.pallas_call(
        body, name=name, grid=(s // tm,),
        in_specs=[pl.BlockSpec((tm, D), lambda i: (i, 0)),
                  pl.BlockSpec((None, D, D), lambda i: (l, 0, 0))],
        out_specs=[pl.BlockSpec((tm, AW), lambda i: (i, 0)), pl.BlockSpec((AW, tm), lambda i: (0, i)),
                   pl.BlockSpec((tm, D - AW), lambda i: (i, 0))],
        out_shape=[jax.ShapeDtypeStruct((s, AW), BF16), jax.ShapeDtypeStruct((AW, s), BF16),
                   jax.ShapeDtypeStruct((s, D - AW), F32)],
        compiler_params=_cp("parallel"),
    )(dxb, w)


def _mm_nt_normbwd(name, gy, w, l, x, g, dres, dep):
    s, k = gy.shape
    tm = 256

    def body(gy_ref, w_ref, x_ref, g_ref, dr_ref, dep_ref, dx_ref, dxb_ref, dg_ref):
        del dep_ref
        i = pl.program_id(0)
        dh = lax.dot_general(gy_ref[...], w_ref[...], (((1,), (1,)), ((), ())), preferred_element_type=F32)
        xv = x_ref[...]
        r = _inv_rms(xv)
        xn = xv * r
        dxn = dh * g_ref[...]
        dx = r * (dxn - xn * jnp.mean(dxn * xn, axis=-1, keepdims=True)) + dr_ref[...]
        dx_ref[...] = dx
        dxb_ref[...] = dx.astype(BF16)
        part = jnp.sum(dh * xn, axis=0, keepdims=True)

        @pl.when(i == 0)
        def _():
            dg_ref[...] = part

        @pl.when(i > 0)
        def _():
            dg_ref[...] += part

    blk = pl.BlockSpec((tm, D), lambda i: (i, 0))
    vec = pl.BlockSpec((1, D), lambda i: (0, 0))
    return pl.pallas_call(
        body, name=name, grid=(s // tm,),
        in_specs=[pl.BlockSpec((tm, k), lambda i: (i, 0)),
                  pl.BlockSpec((None, D, k), lambda i: (l, 0, 0)), blk, vec, blk, ANY],
        out_specs=[blk, blk, vec],
        out_shape=[jax.ShapeDtypeStruct((s, D), F32), jax.ShapeDtypeStruct((s, D), BF16),
                   jax.ShapeDtypeStruct((1, D), F32)],
        compiler_params=_cp("arbitrary"),
    )(gy, w, x, g, dres, dep)


def _mm_tn(name, a, b, tma, tnb):
    s, m = a.shape
    n = b.shape[1]

    def body(a_ref, b_ref, o_ref):
        o_ref[...] = lax.dot_general(a_ref[...], b_ref[...], (((0,), (0,)), ((), ())),
                                     preferred_element_type=F32).astype(BF16)

    return pl.pallas_call(
        body, name=name, grid=(m // tma, n // tnb),
        in_specs=[pl.BlockSpec((s, tma), lambda i, j: (0, i)),
                  pl.BlockSpec((s, tnb), lambda i, j: (0, j))],
        out_specs=pl.BlockSpec((tma, tnb), lambda i, j: (i, j)),
        out_shape=jax.ShapeDtypeStruct((m, n), BF16),
        compiler_params=_cp("parallel", "parallel"),
    )(a, b)


def _adamw_math(gv, wv, mv, vv):
    mn = ADAM_B1 * mv + (1.0 - ADAM_B1) * gv
    vn = ADAM_B2 * vv + (1.0 - ADAM_B2) * jnp.square(gv)
    m_hat = mn / (1.0 - ADAM_B1 ** ADAM_STEP)
    v_hat = vn / (1.0 - ADAM_B2 ** ADAM_STEP)
    return gv, -ADAM_LR * (m_hat / (jnp.sqrt(v_hat) + ADAM_EPS) + ADAM_WD * wv), mn, vn


def _adamw(name, g, w, m, v):
    r, c = g.shape
    tm = 256 if r % 256 == 0 else r

    def body(g_ref, w_ref, m_ref, v_ref, go_ref, d_ref, mo_ref, vo_ref):
        go_ref[...], d_ref[...], mo_ref[...], vo_ref[...] = _adamw_math(g_ref[...], w_ref[...], m_ref[...], v_ref[...])

    blk = pl.BlockSpec((tm, c), lambda i: (i, 0))
    return pl.pallas_call(
        body, name=name, grid=(r // tm,),
        in_specs=[blk] * 4, out_specs=[blk] * 4,
        out_shape=[jax.ShapeDtypeStruct((r, c), F32)] * 4,
        compiler_params=_cp("parallel"),
    )(g, w, m, v)


def _place():
    x, y, c = lax.axis_index("x"), lax.axis_index("y"), lax.axis_index("c")
    chips = [(1 - x, y), (x, 1 - y), (1 - x, 1 - y)]
    return x, y, c, chips


BLOCK_AXIS = (2, 1, 2, 1)
LARGE_DIMS = ((D, DIN), (D, D), (D, DFF), (DFF, D))


def _full_shape(t, layers, dtype):
    r, c = LARGE_DIMS[t]
    return jax.ShapeDtypeStruct((layers, r, c), dtype)


def _cast_into_full(name, t, shard, b1):
    _, r, c = shard.shape
    tm = min(256, r)
    if BLOCK_AXIS[t] == 1:
        out_spec = pl.BlockSpec((None, tm, c), lambda l, i, br: (l, br[0] * (r // tm) + i, 0))
    else:
        out_spec = pl.BlockSpec((None, tm, c), lambda l, i, br: (l, i, br[0]))

    def body(b_ref, x_ref, o_ref):
        del b_ref
        o_ref[...] = x_ref[...].astype(BF16)

    return pl.pallas_call(
        body, name=name,
        grid_spec=pltpu.PrefetchScalarGridSpec(
            num_scalar_prefetch=1, grid=(DEPTH, r // tm),
            in_specs=[pl.BlockSpec((None, tm, c), lambda l, i, br: (l, i, 0))],
            out_specs=out_spec),
        out_shape=_full_shape(t, DEPTH, BF16),
        compiler_params=_cp("parallel", "parallel"),
    )(b1, shard)


HBM = pl.BlockSpec(memory_space=pltpu.HBM)
SEM = pl.BlockSpec(memory_space=pltpu.SEMAPHORE)
DATAFLOW = pltpu.SideEffectType.DATAFLOW_SIDE_EFFECTING


def _half(ref, l, t, b, c):
    r, cols = LARGE_DIMS[t]
    if BLOCK_AXIS[t] == 1:
        n = r // 8
        return ref.at[l, pl.ds(pl.multiple_of(b * (2 * n) + c * n, 16), n), :]
    n, w = r // 2, cols // 4
    return ref.at[l, pl.ds(pl.multiple_of(c * n, 16), n), pl.ds(pl.multiple_of(b * w, 128), w)]


def _gather_start(name, layers, fulls):
    def body(*refs):
        f_refs, sems = refs[4:8], refs[8:8 + 2 * len(layers)]
        x, y, c, chips = _place()
        for i, l in enumerate(layers):
            for t in range(4):
                own = _half(f_refs[t], l, t, 2 * x + y, c)
                for j, (cx, cy) in enumerate(chips):
                    pltpu.make_async_remote_copy(src_ref=own, dst_ref=own, send_sem=sems[2 * i].at[3 * t + j],
                                                 recv_sem=sems[2 * i + 1].at[3 * t + j], device_id=(cx, cy, c),
                                                 device_id_type=MESH).start()

    outs = pl.pallas_call(
        body, name=name,
        in_specs=[HBM] * 4, out_specs=[HBM] * 4 + [SEM] * (2 * len(layers)),
        out_shape=[pltpu.HBM(s.shape, s.dtype) for s in (_full_shape(t, DEPTH, BF16) for t in range(4))]
        + [pltpu.SemaphoreType.DMA((12,))] * (2 * len(layers)),
        input_output_aliases={t: t for t in range(4)},
        compiler_params=pltpu.CompilerParams(has_side_effects=DATAFLOW),
    )(*[pltpu.with_memory_space_constraint(f, pltpu.HBM) for f in fulls])
    return outs[0:4], {l: (outs[4 + 2 * i], outs[5 + 2 * i]) for i, l in enumerate(layers)}


def _gather_wait(l, fulls, sems, after):
    def body(*refs):
        send_sems, recv_sems, f_refs = refs[4], refs[5], refs[7:11]
        x, y, c, chips = _place()
        for t in range(4):
            own = _half(f_refs[t], l, t, 2 * x + y, c)
            for j, (cx, cy) in enumerate(chips):
                landed = _half(f_refs[t], l, t, 2 * cx + cy, c)
                pltpu.make_async_remote_copy(src_ref=own, dst_ref=landed, send_sem=send_sems.at[3 * t + j],
                                             recv_sem=recv_sems.at[3 * t + j], device_id=(cx, cy, c),
                                             device_id_type=MESH).wait()

    return pl.pallas_call(
        body, name=f"gather_wait_{l}",
        in_specs=[HBM] * 4 + [SEM, SEM, ANY], out_specs=[HBM] * 4,
        out_shape=[pltpu.HBM(s.shape, s.dtype) for s in (_full_shape(t, DEPTH, BF16) for t in range(4))],
        input_output_aliases={t: t for t in range(4)},
        compiler_params=pltpu.CompilerParams(has_side_effects=DATAFLOW),
    )(*fulls, sems[0], sems[1], after)


def _pass_on(l, fulls):
    def body(*refs):
        f_refs, send_sems, recv_sems = refs[4:8], refs[8], refs[9]
        x, y, c, chips = _place()

        def copy(t, j, half):
            cx, cy = chips[j]
            part = _half(f_refs[t], l, t, 2 * cx + cy, half)
            return pltpu.make_async_remote_copy(src_ref=part, dst_ref=part, send_sem=send_sems.at[3 * t + j],
                                                recv_sem=recv_sems.at[3 * t + j], device_id=(x, y, 1 - c),
                                                device_id_type=MESH)

        for t in range(4):
            for j in range(3):
                copy(t, j, c).start()
        for t in range(4):
            for j in range(3):
                copy(t, j, 1 - c).wait_recv()
                copy(t, j, c).wait_send()

    return pl.pallas_call(
        body, name=f"pass_on_{l}",
        in_specs=[ANY] * 4, out_specs=[ANY] * 4,
        out_shape=[_full_shape(t, DEPTH, BF16) for t in range(4)],
        input_output_aliases={t: t for t in range(4)},
        scratch_shapes=[pltpu.SemaphoreType.DMA((12,)), pltpu.SemaphoreType.DMA((12,))],
    )(*fulls)


def _block2d(ref, t, b):
    r, cols = LARGE_DIMS[t]
    if BLOCK_AXIS[t] == 1:
        return ref.at[pl.ds(pl.multiple_of(b * (r // 4), 16), r // 4), :]
    return ref.at[:, pl.ds(pl.multiple_of(b * (cols // 4), 128), cols // 4)]


def _block_dims(t):
    r, cols = LARGE_DIMS[t]
    return (r // 4, cols) if BLOCK_AXIS[t] == 1 else (r, cols // 4)


def _reduce_copies(g_refs, r_refs, send_sems, recv_sems):
    _, _, c, chips = _place()
    return [pltpu.make_async_remote_copy(src_ref=_block2d(g_refs[t], t, 2 * cx + cy), dst_ref=r_refs[t].at[j],
                                         send_sem=send_sems.at[3 * t + j], recv_sem=recv_sems.at[3 * t + j],
                                         device_id=(cx, cy, c), device_id_type=MESH)
            for t in range(4) for j, (cx, cy) in enumerate(chips)]


def _reduce_start(l, grads):
    def body(*refs):
        for cp in _reduce_copies(refs[4:8], refs[8:12], refs[12], refs[13]):
            cp.start()
        refs[14][...] = jnp.zeros((8, 128), F32)

    outs = pl.pallas_call(
        body, name=f"reduce_start_{l}",
        in_specs=[HBM] * 4,
        out_specs=[HBM] * 8 + [SEM, SEM, pl.BlockSpec(memory_space=pltpu.VMEM)],
        out_shape=[pltpu.HBM(g.shape, BF16) for g in grads]
        + [pltpu.HBM((3,) + _block_dims(t), BF16) for t in range(4)]
        + [pltpu.SemaphoreType.DMA((12,)), pltpu.SemaphoreType.DMA((12,)), jax.ShapeDtypeStruct((8, 128), F32)],
        input_output_aliases={t: t for t in range(4)},
        compiler_params=pltpu.CompilerParams(has_side_effects=DATAFLOW),
    )(*[pltpu.with_memory_space_constraint(g, pltpu.HBM) for g in grads])
    return outs[0:4], outs[4:8], (outs[8], outs[9]), outs[10]


def _reduce_wait(l, grads, landing, sems, after):
    def body(*refs):
        for cp in _reduce_copies(refs[11:15], refs[15:19], refs[8], refs[9]):
            cp.wait()

    outs = pl.pallas_call(
        body, name=f"reduce_wait_{l}",
        in_specs=[HBM] * 8 + [SEM, SEM, ANY], out_specs=[HBM] * 8,
        out_shape=[pltpu.HBM(g.shape, BF16) for g in grads] + [pltpu.HBM(r.shape, BF16) for r in landing],
        input_output_aliases={i: i for i in range(8)},
        compiler_params=pltpu.CompilerParams(has_side_effects=DATAFLOW),
    )(*grads, *landing, sems[0], sems[1], after)
    return outs[0:4], outs[4:8]


def _add4(name, t, own, landed, b1):
    rb, cb = _block_dims(t)
    tm = min(256, rb)
    if BLOCK_AXIS[t] == 1:
        own_spec = pl.BlockSpec((tm, cb), lambda i, br: (br[0] * (rb // tm) + i, 0))
    else:
        own_spec = pl.BlockSpec((tm, cb), lambda i, br: (i, br[0]))

    def body(b_ref, o_ref, r0_ref, r1_ref, r2_ref, s_ref):
        del b_ref
        s_ref[...] = ((o_ref[...].astype(F32) + r0_ref[...].astype(F32))
                      + (r1_ref[...].astype(F32) + r2_ref[...].astype(F32))).astype(BF16)

    def got(j):
        return pl.BlockSpec((None, tm, cb), lambda i, br: (j, i, 0))

    return pl.pallas_call(
        body, name=name,
        grid_spec=pltpu.PrefetchScalarGridSpec(
            num_scalar_prefetch=1, grid=(rb // tm,),
            in_specs=[own_spec, got(0), got(1), got(2)],
            out_specs=pl.BlockSpec((tm, cb), lambda i, br: (i, 0))),
        out_shape=jax.ShapeDtypeStruct((rb, cb), BF16),
        compiler_params=_cp("parallel"),
    )(b1, own, landed, landed, landed)


def _swap_sib(name, sums):
    def body(*refs):
        s_refs, t_refs, send_sems, recv_sems = refs[0:4], refs[4:8], refs[8], refs[9]
        x, y, c, _ = _place()
        cps = [pltpu.make_async_remote_copy(src_ref=s_refs[t], dst_ref=t_refs[t], send_sem=send_sems.at[t],
                                            recv_sem=recv_sems.at[t], device_id=(x, y, 1 - c), device_id_type=MESH)
               for t in range(4)]
        for cp in cps:
            cp.start()
        for cp in cps:
            cp.wait()

    return pl.pallas_call(
        body, name=name,
        in_specs=[ANY] * 4, out_specs=[ANY] * 4,
        out_shape=[jax.ShapeDtypeStruct(s.shape, BF16) for s in sums],
        scratch_shapes=[pltpu.SemaphoreType.DMA((4,)), pltpu.SemaphoreType.DMA((4,))],
    )(*sums)


def _adamw_pair(name, l, s_own, s_sib, w, m, v, outs):
    rb, cb = s_own.shape
    tm = min(256, rb)

    def body(a_ref, b_ref, w_ref, m_ref, v_ref, g0, d0, m0, v0, go_ref, d_ref, mo_ref, vo_ref):
        del g0, d0, m0, v0
        gv = a_ref[...].astype(F32) + b_ref[...].astype(F32)
        go_ref[...], d_ref[...], mo_ref[...], vo_ref[...] = _adamw_math(gv, w_ref[...], m_ref[...], v_ref[...])

    part = pl.BlockSpec((tm, cb), lambda i: (i, 0))
    layer = pl.BlockSpec((None, tm, cb), lambda i: (l, i, 0))
    return pl.pallas_call(
        body, name=name, grid=(rb // tm,),
        in_specs=[part, part, layer, layer, layer] + [ANY] * 4,
        out_specs=[layer] * 4,
        out_shape=[jax.ShapeDtypeStruct((DEPTH, rb, cb), F32)] * 4,
        input_output_aliases={5 + i: i for i in range(4)},
        compiler_params=_cp("parallel"),
    )(s_own, s_sib, w, m, v, *outs)


def _all_gather8(name, v):
    m_per, n = v.shape

    def body(v_ref, out_ref, send_sems, recv_sems, local_sem):
        x, y, c, chips = _place()
        me, sib = (x, y, c), (x, y, 1 - c)

        def rows(px, py, pc):
            return out_ref.at[pl.ds((4 * px + 2 * py + pc) * m_per, m_per), :]

        def copy(k, block, to, src=None):
            return pltpu.make_async_remote_copy(
                src_ref=rows(*block) if src is None else src, dst_ref=rows(*block),
                send_sem=send_sems.at[k], recv_sem=recv_sems.at[k], device_id=to, device_id_type=MESH)

        mine = pltpu.make_async_copy(v_ref, rows(*me), local_sem)
        mine.start()
        first = [copy(0, me, sib, src=v_ref)]
        first += [copy(1 + j, me, (*chip, c), src=v_ref) for j, chip in enumerate(chips)]
        for cp in first:
            cp.start()
        passed = [copy(4 + j, (*chip, c), sib) for j, chip in enumerate(chips)]
        for j, chip in enumerate(chips):
            copy(1 + j, (*chip, c), me).wait_recv()
            passed[j].start()
        copy(0, sib, me).wait_recv()
        for j, chip in enumerate(chips):
            copy(4 + j, (*chip, 1 - c), me).wait_recv()
        for cp in first + passed:
            cp.wait_send()
        mine.wait()

    return pl.pallas_call(
        body, name=name,
        out_shape=jax.ShapeDtypeStruct((8 * m_per, n), v.dtype),
        in_specs=[pl.BlockSpec(memory_space=pltpu.VMEM)],
        out_specs=pl.BlockSpec(memory_space=pltpu.VMEM),
        scratch_shapes=[pltpu.SemaphoreType.DMA((7,)), pltpu.SemaphoreType.DMA((7,)), pltpu.SemaphoreType.DMA],
    )(v)


def _sum8(name, g):
    def body(g_ref, o_ref):
        acc = g_ref[0]
        for d in range(1, 8):
            acc = acc + g_ref[d]
        o_ref[...] = acc

    return pl.pallas_call(body, name=name, out_shape=jax.ShapeDtypeStruct(g.shape[1:], F32))(g)


def _pack(parts):
    flat = []
    for a in parts:
        a = a.reshape(-1)
        flat.append(jnp.pad(a, (0, (-a.shape[0]) % 128)))
    cat = jnp.concatenate(flat)
    cat = jnp.pad(cat, (0, (-cat.shape[0]) % 1024))
    return cat.reshape(-1, 128)


def _unpack(packed, shapes):
    flat = packed.reshape(-1)
    out, at = [], 0
    for shp in shapes:
        n = 1
        for d in shp:
            n *= d
        out.append(flat[at:at + n].reshape(shp))
        at += n + (-n) % 128
    return out


def _local_step(x, target, layer_weights, on_grads, small):
    saved = []
    xin = x
    h = _rmsnorm("norm_first", x, small["norm1_g"][0:1])
    for l in range(DEPTH):
        w_in, w_out, w_1, w_2 = layer_weights(l, xin)
        qg = jnp.tile(small["q_norm_g"][l], 8)[None]
        kg = jnp.tile(small["k_norm_g"][l], 8)[None]
        rb = jnp.pad(small["rel_bias"][l], ((0, 0), (0, NIDX - 257)))
        bias = _bias_layout(_bias_expand(f"bias_expand_{l}", rb))
        cw = small["conv_w"][l]
        pwbd = jax.scipy.linalg.block_diag(*[small["pool_w"][l, g] for g in range(4)])
        ps = small["pool_scale"][l][None]
        p = _mm_nn(f"proj_in_{l}", h, w_in, l, 512, 512, F32)
        q, qt, kp, kt, vp, vt = _qkv(f"qkv_{l}", p, qg, kg)
        o = _attn_fwd(f"attn_fwd_{l}", kp, qt, vt, bias)
        mix = _convpool_fwd(f"convpool_fwd_{l}", p, o, cw, pwbd, ps)
        x1, h2 = _mm_res_norm(f"proj_out_{l}", mix, w_out, l, xin, small["norm2_g"][l:l + 1])
        a, f = _mm_mlp1(f"mlp1_{l}", h2, w_1, l)
        gnext = small["norm1_g"][(l + 1) % DEPTH][None]
        x2, hnext = _mm_res_norm(f"mlp2_{l}", f, w_2, l, x1, gnext)
        saved.append(dict(xin=xin, h=h, p=p, q=q, qt=qt, kp=kp, kt=kt, vp=vp, bias=bias, mix=mix, x1=x1, h2=h2, a=a, f=f,
                          qg=qg, kg=kg, cw=cw, pwbd=pwbd, ps=ps))
        xin, h = x2, hnext

    dx, dxb, loss = _loss_grad("loss_grad", xin, target)
    gs = {k: [None] * DEPTH for k in ("norm1_g", "q_norm_g", "k_norm_g", "rel_bias", "conv_w", "pool_w",
                                      "pool_scale", "norm2_g")}
    for l in reversed(range(DEPTH)):
        sv = saved[l]
        da = _mm_nt_relu(f"mlp2_bwd_{l}", dxb, w_2, l, sv["a"])
        g_2 = _mm_tn(f"mlp2_wgrad_{l}", sv["f"], dxb, 512, 512)
        g_1 = _mm_tn(f"mlp1_wgrad_{l}", sv["h2"], da, 512, 512)
        dx1, dx1b, dg2 = _mm_nt_normbwd(f"mlp1_bwd_{l}", da, w_1, l, sv["x1"], small["norm2_g"][l:l + 1], dx, dx)
        do, dot, dmix = _proj_out_bwd(f"proj_out_bwd_{l}", dx1b, w_out, l)
        g_out = _mm_tn(f"proj_out_wgrad_{l}", sv["mix"], dx1b, 512, 512)
        dcp, dw0, dw1, dw2, dps, dpw = _convpool_bwd(f"convpool_bwd_{l}", sv["p"], dmix, sv["cw"], sv["pwbd"], sv["ps"])
        dq, dkp, dvp, db = _attn_bwd(f"attn_bwd_{l}", sv["q"], sv["qt"], sv["kp"], sv["kt"], sv["vp"], sv["bias"],
                                     do, dot)
        drb = _bias_reduce(f"bias_reduce_{l}", _bias_unlayout(db))
        dp, dqg, dkg = _qkv_bwd(f"qkv_bwd_{l}", sv["p"], dq, dkp, dvp, dcp, sv["qg"], sv["kg"])
        g_in = _mm_tn(f"proj_in_wgrad_{l}", sv["h"], dp, 512, 640)
        dep = on_grads(l, (g_in, g_out, g_1, g_2))
        dx, dxb, dg1 = _mm_nt_normbwd(f"proj_in_bwd_{l}", dp, w_in, l, sv["xin"], small["norm1_g"][l:l + 1], dx1, dep)
        gs["norm1_g"][l] = dg1[0]
        gs["q_norm_g"][l] = dqg[0, :HD]
        gs["k_norm_g"][l] = dkg[0, :HD]
        gs["rel_bias"][l] = drb[:, :257]
        gs["conv_w"][l] = jnp.concatenate([dw0, dw1, dw2], axis=0)
        gs["pool_w"][l] = jnp.stack([dpw[g * 64:(g + 1) * 64, g * 64:(g + 1) * 64] for g in range(4)])
        gs["pool_scale"][l] = dps[0]
        gs["norm2_g"][l] = dg2[0]
    gsmall = {k: jnp.stack(v) for k, v in gs.items()}
    return loss, dx, gsmall


SMALL = ("norm1_g", "q_norm_g", "k_norm_g", "rel_bias", "conv_w", "pool_w", "pool_scale", "norm2_g")
LARGE = ("w_in", "w_out", "w_mlp1", "w_mlp2")


def kernel(x, norm1_g, w_in, q_norm_g, k_norm_g, rel_bias, conv_w, pool_w, pool_scale, w_out, norm2_g, w_mlp1, w_mlp2, loss_target, m_norm1_g, m_w_in, m_q_norm_g, m_k_norm_g, m_rel_bias, m_conv_w, m_pool_w, m_pool_scale, m_w_out, m_norm2_g, m_w_mlp1, m_w_mlp2, v_norm1_g, v_w_in, v_q_norm_g, v_k_norm_g, v_rel_bias, v_conv_w, v_pool_w, v_pool_scale, v_w_out, v_norm2_g, v_w_mlp1, v_w_mlp2):
    w = dict(norm1_g=norm1_g, w_in=w_in, q_norm_g=q_norm_g, k_norm_g=k_norm_g, rel_bias=rel_bias, conv_w=conv_w,
             pool_w=pool_w, pool_scale=pool_scale, w_out=w_out, norm2_g=norm2_g, w_mlp1=w_mlp1, w_mlp2=w_mlp2)
    m = dict(norm1_g=m_norm1_g, w_in=m_w_in, q_norm_g=m_q_norm_g, k_norm_g=m_k_norm_g, rel_bias=m_rel_bias,
             conv_w=m_conv_w, pool_w=m_pool_w, pool_scale=m_pool_scale, w_out=m_w_out, norm2_g=m_norm2_g,
             w_mlp1=m_w_mlp1, w_mlp2=m_w_mlp2)
    v = dict(norm1_g=v_norm1_g, w_in=v_w_in, q_norm_g=v_q_norm_g, k_norm_g=v_k_norm_g, rel_bias=v_rel_bias,
             conv_w=v_conv_w, pool_w=v_pool_w, pool_scale=v_pool_scale, w_out=v_w_out, norm2_g=v_norm2_g,
             w_mlp1=v_w_mlp1, w_mlp2=v_w_mlp2)
    ax, ay, ac = lax.axis_index("x"), lax.axis_index("y"), lax.axis_index("c")
    b1 = jnp.reshape(2 * ax + ay, (1,)).astype(jnp.int32)

    cw_rows = _all_gather8("gather_conv_w", jnp.pad(conv_w.reshape(DEPTH * 3, 64), ((0, 4), (0, 64))))
    cw_chips = [cw_rows[(4 * cx + 2 * cy) * 16:(4 * cx + 2 * cy) * 16 + 12, :64] for cx in range(2) for cy in range(2)]
    small = {n: w[n] for n in SMALL}
    small["conv_w"] = jnp.concatenate(cw_chips, axis=1).reshape(DEPTH, 3, CW)

    casts = [_cast_into_full(f"cast_{n}", t, w[n], b1) for t, n in enumerate(LARGE)]
    first, first_sems = _gather_start("gather_start_first", (0,), casts)
    held = [first]
    sems = dict(first_sems)

    def layer_weights(l, after):
        arrived = _gather_wait(l, held[0], sems[l], after)
        if l == 0:
            arrived, rest_sems = _gather_start("gather_start_rest", tuple(range(1, DEPTH)), arrived)
            sems.update(rest_sems)
        held[0] = _pass_on(l, arrived)
        return held[0]

    flights = {}

    def on_grads(l, grads):
        if l + 1 in flights:
            g, landing, sm, _ = flights[l + 1]
            flights[l + 1] = _reduce_wait(l + 1, g, landing, sm, grads[0])
        flights[l] = _reduce_start(l, grads)
        return flights[l][3]

    loss_part, grad_x, gsmall = _local_step(x[0], loss_target[0], layer_weights, on_grads, small)
    loss = lax.psum(loss_part[0, 0], ("x", "y", "c"))

    out = {n: [lax.empty(w[n].shape, F32) for _ in range(4)] for n in LARGE}
    last = None
    for l in reversed(range(DEPTH)):
        if l == 0:
            g, landing, sm, _ = flights[0]
            flights[0] = _reduce_wait(0, g, landing, sm, last)
        g, landing = flights[l]
        sums = [_add4(f"add4_{LARGE[t]}_{l}", t, g[t], landing[t], b1) for t in range(4)]
        theirs = _swap_sib(f"swap_sib_{l}", sums)
        for t, n in enumerate(LARGE):
            out[n] = _adamw_pair(f"adamw_{n}_{l}", l, sums[t], theirs[t], w[n], m[n], v[n], out[n])
        last = out[LARGE[3]][0]

    order = [n for n in SMALL]
    packed = _pack([gsmall[n] for n in order])
    rows = packed.shape[0]
    summed = _sum8("sum_small", _all_gather8("gather_small", packed).reshape(8, rows, 128))
    gfull = dict(zip(order, _unpack(summed, [gsmall[n].shape for n in order])))
    gfull["conv_w"] = lax.dynamic_slice_in_dim(gfull["conv_w"], (2 * ax + ay) * 64, 64, axis=2)
    res = _adamw("adamw_small", _pack([gfull[n] for n in order]), _pack([w[n] for n in order]),
                 _pack([m[n] for n in order]), _pack([v[n] for n in order]))
    for n, parts in zip(order, zip(*[_unpack(r, [w[k].shape for k in order]) for r in res])):
        out[n] = list(parts)

    names = ("norm1_g", "w_in", "q_norm_g", "k_norm_g", "rel_bias", "conv_w", "pool_w", "pool_scale", "w_out",
             "norm2_g", "w_mlp1", "w_mlp2")
    flat = [loss, grad_x[None]]
    for i in range(4):
        flat += [out[n][i] for n in names]
    return tuple(flat)
```

```python
import functools

import jax
import jax.numpy as jnp
from jax import lax
from jax.experimental import pallas as pl
from jax.experimental.pallas import tpu as pltpu

F32 = jnp.float32
BF16 = jnp.bfloat16

D = 1024
DEPTH = 4
CH = 64
NPREV = 8
KB = (NPREV + 1) * CH
PADR = NPREV * CH
HD = 64
AW = 512
CW = 256
PWD = 256
DIN = 3 * AW + 3 * CW + PWD
DFF = 4 * D
NIDX = 384
EPS = 1e-6
NEG_INF = -1e30

ADAM_LR = 0.001
ADAM_B1 = 0.9
ADAM_B2 = 0.999
ADAM_EPS = 1e-08
ADAM_WD = 0.01
ADAM_STEP = 10

VMEM_LIMIT = 52 * 1024 * 1024
MESH = pl.DeviceIdType.MESH
ANY = pl.BlockSpec(memory_space=pl.ANY)


def _cp(*sem):
    return pltpu.CompilerParams(dimension_semantics=sem, vmem_limit_bytes=VMEM_LIMIT)


def _inv_rms(x):
    return lax.rsqrt(jnp.mean(x * x, axis=-1, keepdims=True) + EPS)


def _head_mean_matrix():
    r = lax.broadcasted_iota(jnp.int32, (AW, AW), 0) // HD
    c = lax.broadcasted_iota(jnp.int32, (AW, AW), 1) // HD
    return jnp.where(r == c, 1.0 / HD, 0.0).astype(BF16)


def _head_mean(x, hm):
    hi = x.astype(BF16)
    lo = (x - hi.astype(F32)).astype(BF16)
    return (jnp.dot(hi, hm, preferred_element_type=F32)
            + jnp.dot(lo, hm, preferred_element_type=F32))


def _rmsnorm(name, x, g):
    s = x.shape[0]
    tm = 512

    def body(x_ref, g_ref, h_ref):
        xv = x_ref[...]
        h_ref[...] = (xv * _inv_rms(xv) * g_ref[...]).astype(BF16)

    return pl.pallas_call(
        body, name=name, grid=(s // tm,),
        in_specs=[pl.BlockSpec((tm, D), lambda i: (i, 0)), pl.BlockSpec((1, D), lambda i: (0, 0))],
        out_specs=pl.BlockSpec((tm, D), lambda i: (i, 0)),
        out_shape=jax.ShapeDtypeStruct((s, D), BF16),
        compiler_params=_cp("parallel"),
    )(x, g)


def _mm_nn(name, a, w, l, tm, tn, out_dtype):
    s, k = a.shape
    n = w.shape[2]

    def body(a_ref, w_ref, o_ref):
        o_ref[...] = jnp.dot(a_ref[...], w_ref[...], preferred_element_type=F32).astype(o_ref.dtype)

    return pl.pallas_call(
        body, name=name, grid=(s // tm, n // tn),
        in_specs=[pl.BlockSpec((tm, k), lambda i, j: (i, 0)),
                  pl.BlockSpec((None, k, tn), lambda i, j: (l, 0, j))],
        out_specs=pl.BlockSpec((tm, tn), lambda i, j: (i, j)),
        out_shape=jax.ShapeDtypeStruct((s, n), out_dtype),
        compiler_params=_cp("parallel", "parallel"),
    )(a, w)


def _mm_mlp1(name, h2, w, l):
    s, k = h2.shape
    n = w.shape[2]
    tm, tn = 512, 1024

    def body(a_ref, w_ref, o_ref, f_ref):
        acc = jnp.dot(a_ref[...], w_ref[...], preferred_element_type=F32)
        o_ref[...] = acc.astype(BF16)
        f_ref[...] = jnp.square(jnp.maximum(acc, 0.0)).astype(BF16)

    return pl.pallas_call(
        body, name=name, grid=(s // tm, n // tn),
        in_specs=[pl.BlockSpec((tm, k), lambda i, j: (i, 0)),
                  pl.BlockSpec((None, k, tn), lambda i, j: (l, 0, j))],
        out_specs=[pl.BlockSpec((tm, tn), lambda i, j: (i, j))] * 2,
        out_shape=[jax.ShapeDtypeStruct((s, n), BF16)] * 2,
        compiler_params=_cp("parallel", "parallel"),
    )(h2, w)


def _mm_res_norm(name, a, w, l, res, g):
    s, k = a.shape
    tm = 256

    def body(a_ref, w_ref, r_ref, g_ref, x_ref, h_ref):
        acc = r_ref[...] + jnp.dot(a_ref[...], w_ref[...], preferred_element_type=F32)
        x_ref[...] = acc
        h_ref[...] = (acc * _inv_rms(acc) * g_ref[...]).astype(BF16)

    return pl.pallas_call(
        body, name=name, grid=(s // tm,),
        in_specs=[pl.BlockSpec((tm, k), lambda i: (i, 0)),
                  pl.BlockSpec((None, k, D), lambda i: (l, 0, 0)),
                  pl.BlockSpec((tm, D), lambda i: (i, 0)),
                  pl.BlockSpec((1, D), lambda i: (0, 0))],
        out_specs=[pl.BlockSpec((tm, D), lambda i: (i, 0))] * 2,
        out_shape=[jax.ShapeDtypeStruct((s, D), F32), jax.ShapeDtypeStruct((s, D), BF16)],
        compiler_params=_cp("parallel"),
    )(a, w, res, g)


def _qkv(name, p, qg, kg):
    s = p.shape[0]
    tm = PADR
    nb = s // tm

    def body(pq_ref, pk_ref, pv_ref, qg_ref, kg_ref, q_ref, qt_ref, k_ref, kt_ref, v_ref, vt_ref):
        t = pl.program_id(0)
        hm = _head_mean_matrix()

        def nrm(x, g):
            return x * lax.rsqrt(_head_mean(x * x, hm) + EPS) * g

        first = t == 0
        qq = nrm(pq_ref[...], qg_ref[...]) * 0.125
        kk = jnp.where(first, 0.0, nrm(pk_ref[...], kg_ref[...]))
        vv = jnp.where(first, 0.0, pv_ref[...])
        q_ref[...] = qq.astype(BF16)
        qt_ref[...] = qq.T.astype(BF16)
        k_ref[...] = kk.astype(BF16)
        kt_ref[...] = kk.T.astype(BF16)
        v_ref[...] = vv.astype(BF16)
        vt_ref[...] = vv.T.astype(BF16)

    def src(col):
        return pl.BlockSpec((tm, AW), lambda t: (jnp.maximum(t - 1, 0), col))

    gspec = pl.BlockSpec((1, AW), lambda t: (0, 0))
    rows = pl.BlockSpec((tm, AW), lambda t: (t, 0))
    cols = pl.BlockSpec((AW, tm), lambda t: (0, t))
    return pl.pallas_call(
        body, name=name, grid=(nb + 1,),
        in_specs=[src(0), src(1), src(2), gspec, gspec],
        out_specs=[pl.BlockSpec((tm, AW), lambda t: (jnp.maximum(t - 1, 0), 0)),
                   pl.BlockSpec((AW, tm), lambda t: (0, jnp.maximum(t - 1, 0))),
                   rows, cols, rows, cols],
        out_shape=[jax.ShapeDtypeStruct((s, AW), BF16), jax.ShapeDtypeStruct((AW, s), BF16),
                   jax.ShapeDtypeStruct((s + PADR, AW), BF16), jax.ShapeDtypeStruct((AW, s + PADR), BF16),
                   jax.ShapeDtypeStruct((s + PADR, AW), BF16), jax.ShapeDtypeStruct((AW, s + PADR), BF16)],
        compiler_params=_cp("arbitrary"),
    )(p, p, p, qg, kg)


NBAND = KB // CH
HIGHEST = lax.Precision.HIGHEST
NT_DIMS = (((1,), (1,)), ((), ()))


def _onehot_table(a):
    m = lax.broadcasted_iota(jnp.int32, (128, NIDX), 0)
    idx = lax.broadcasted_iota(jnp.int32, (128, NIDX), 1)
    rel = jnp.clip(KB - 1 - (CH * a + m), -128, 128) + 128
    return jnp.where(rel == idx, 1.0, 0.0).astype(F32)


def _onehot_diagonal():
    r = lax.broadcasted_iota(jnp.int32, (CH * CH, 128), 0)
    m = lax.broadcasted_iota(jnp.int32, (CH * CH, 128), 1)
    return jnp.where((r % CH) - (r // CH) + (CH - 1) == m, 1.0, 0.0).astype(F32)


def _bias_expand(name, rb):
    def body(rb_ref, o_ref):
        along = [lax.dot_general(rb_ref[...], _onehot_table(a), NT_DIMS, preferred_element_type=F32,
                                 precision=HIGHEST) for a in range(NBAND)]
        o_ref[...] = lax.dot_general(jnp.concatenate(along, axis=0), _onehot_diagonal(), NT_DIMS,
                                     preferred_element_type=F32, precision=HIGHEST)

    return pl.pallas_call(
        body, name=name,
        out_shape=jax.ShapeDtypeStruct((NBAND * 8, CH * CH), F32),
    )(rb)


def _bias_reduce(name, db):
    def body(db_ref, o_ref):
        along = jnp.dot(db_ref[...], _onehot_diagonal(), preferred_element_type=F32, precision=HIGHEST)
        acc = jnp.zeros((8, NIDX), F32)
        for a in range(NBAND):
            acc = acc + jnp.dot(along[8 * a:8 * a + 8, :], _onehot_table(a), preferred_element_type=F32,
                                precision=HIGHEST)
        o_ref[...] = acc

    return pl.pallas_call(
        body, name=name,
        out_shape=jax.ShapeDtypeStruct((8, NIDX), F32),
    )(db)


def _bias_layout(flat):
    b = flat.reshape(NBAND, 8, CH, CH).transpose(1, 0, 3, 2).reshape(4, 2, KB, CH)
    pair = b.transpose(0, 2, 1, 3).reshape(4, KB, 128)
    first = jnp.pad(pair, ((0, 0), (0, CH), (0, 0)), constant_values=NEG_INF)
    second = jnp.pad(pair, ((0, 0), (CH, 0), (0, 0)), constant_values=NEG_INF)
    return jnp.concatenate([first, second], axis=2)


def _bias_unlayout(dbt):
    b = dbt.reshape(4, NBAND, CH, 2, CH)
    return b.transpose(1, 0, 3, 4, 2).reshape(NBAND * 8, CH * CH)


UNIT = 2 * CH
BAND2 = KB + CH


def _pair_weights(xt):
    x = xt.astype(F32)
    row = lax.broadcasted_iota(jnp.int32, (128, UNIT), 0)
    low = lax.broadcasted_iota(jnp.int32, (128, UNIT), 1) < HD
    swapped = pltpu.roll(x, HD, 1)
    same = (row < HD) == low
    first = jnp.where(same, jnp.where(low, x, swapped), 0.0)
    second = jnp.where(same, jnp.where(low, swapped, x), 0.0)
    return jnp.concatenate([first, second], axis=1).astype(BF16)


def _pair_rows(x):
    low = lax.broadcasted_iota(jnp.int32, (CH, 128), 1) < HD
    zero = jnp.zeros((CH, 128), x.dtype)
    parts = []
    for c in range(2):
        xc = x[c * CH:(c + 1) * CH, :]
        parts += [jnp.where(low, xc, zero), jnp.where(low, zero, xc)]
    return jnp.concatenate(parts, axis=0)


def _unpair(raw):
    b0, b1 = raw[:, 0:128], raw[:, 128:256]
    row = lax.broadcasted_iota(jnp.int32, (128, 128), 0)
    low = lax.broadcasted_iota(jnp.int32, (128, 128), 1) < HD
    top = jnp.where(low, b0, pltpu.roll(b1, HD, 1))
    bottom = jnp.where(low, pltpu.roll(b0, HD, 1), b1)
    return jnp.where(row < HD, top, bottom).T


def _softmax_t(kb, qw, bias2, row0):
    s = jnp.dot(kb, qw, preferred_element_type=F32)
    valid = (row0 + lax.broadcasted_iota(jnp.int32, (BAND2, 256), 0)) >= PADR
    s = jnp.where(valid, s + bias2, NEG_INF)
    e = jnp.exp(s - jnp.max(s, axis=0, keepdims=True))
    return e * (1.0 / jnp.sum(e, axis=0, keepdims=True))


def _attn_fwd(name, kp, qt, vt, bias2):
    s = qt.shape[1]

    def body(k_ref, qt_ref, vt_ref, b_ref, o_ref):
        def unit(u, carry):
            r0 = pl.multiple_of(u * UNIT, UNIT)
            pt = _softmax_t(k_ref[pl.ds(r0, BAND2), :], _pair_weights(qt_ref[:, pl.ds(r0, UNIT)]), b_ref[...], r0)
            raw = jnp.dot(vt_ref[:, pl.ds(r0, BAND2)], pt.astype(BF16), preferred_element_type=F32)
            o_ref[pl.ds(r0, UNIT), :] = _unpair(raw).astype(BF16)
            return carry

        lax.fori_loop(0, s // UNIT, unit, 0)

    return pl.pallas_call(
        body, name=name, grid=(AW // 128,),
        in_specs=[pl.BlockSpec((s + PADR, 128), lambda h: (0, h)),
                  pl.BlockSpec((128, s), lambda h: (h, 0)),
                  pl.BlockSpec((128, s + PADR), lambda h: (h, 0)),
                  pl.BlockSpec((None, BAND2, 256), lambda h: (h, 0, 0))],
        out_specs=pl.BlockSpec((s, 128), lambda h: (0, h)),
        out_shape=jax.ShapeDtypeStruct((s, AW), BF16),
        compiler_params=_cp("parallel"),
    )(kp, qt, vt, bias2)


def _attn_bwd(name, q, qt, kp, kt, vp, bias2, do, dot):
    s = q.shape[0]

    def body(q_ref, qt_ref, k_ref, kt_ref, v_ref, b_ref, do_ref, dot_ref, dq_ref, dk_ref, dv_ref, db_ref):
        dk_ref[...] = jnp.zeros_like(dk_ref)
        dv_ref[...] = jnp.zeros_like(dv_ref)
        db_ref[...] = jnp.zeros_like(db_ref)

        def unit(u, carry):
            r0 = pl.multiple_of(u * UNIT, UNIT)
            rows, band = pl.ds(r0, UNIT), pl.ds(r0, BAND2)
            pt = _softmax_t(k_ref[band, :], _pair_weights(qt_ref[:, rows]), b_ref[...], r0)
            dpt = jnp.dot(v_ref[band, :], _pair_weights(dot_ref[:, rows]), preferred_element_type=F32)
            ds = pt * (dpt - jnp.sum(dpt * pt, axis=0, keepdims=True))
            db_ref[...] += ds[0:KB, 0:128] + ds[CH:BAND2, 128:256]
            dsb = ds.astype(BF16)
            dq_ref[rows, :] = _unpair(jnp.dot(kt_ref[:, band], dsb, preferred_element_type=F32))
            dk_ref[band, :] += jnp.dot(dsb, _pair_rows(q_ref[rows, :]), preferred_element_type=F32)
            dv_ref[band, :] += jnp.dot(pt.astype(BF16), _pair_rows(do_ref[rows, :]), preferred_element_type=F32)
            return carry

        lax.fori_loop(0, s // UNIT, unit, 0)

    row_q = pl.BlockSpec((s, 128), lambda h: (0, h))
    col_q = pl.BlockSpec((128, s), lambda h: (h, 0))
    row_k = pl.BlockSpec((s + PADR, 128), lambda h: (0, h))
    col_k = pl.BlockSpec((128, s + PADR), lambda h: (h, 0))
    return pl.pallas_call(
        body, name=name, grid=(AW // 128,),
        in_specs=[row_q, col_q, row_k, col_k, row_k,
                  pl.BlockSpec((None, BAND2, 256), lambda h: (h, 0, 0)), row_q, col_q],
        out_specs=[row_q, row_k, row_k, pl.BlockSpec((None, KB, 128), lambda h: (h, 0, 0))],
        out_shape=[jax.ShapeDtypeStruct((s, AW), F32),
                   jax.ShapeDtypeStruct((s + PADR, AW), F32),
                   jax.ShapeDtypeStruct((s + PADR, AW), F32),
                   jax.ShapeDtypeStruct((4, KB, 128), F32)],
        compiler_params=_cp("parallel"),
    )(q, qt, kp, kt, vp, bias2, do, dot)


def _rows_before(cur, prev, k):
    row = lax.broadcasted_iota(jnp.int32, cur.shape, 0)
    return jnp.where(row >= k, pltpu.roll(cur, k, 0), pltpu.roll(prev, k, 0))


def _rows_after(cur, nxt, k):
    n = cur.shape[0]
    row = lax.broadcasted_iota(jnp.int32, cur.shape, 0)
    return jnp.where(row < n - k, pltpu.roll(cur, n - k, 0), pltpu.roll(nxt, n - k, 0))


def _pool_window_lanes():
    lg = lax.broadcasted_iota(jnp.int32, (1, PWD), 1) // 64
    return lg, jnp.where(lg == 0, 2.0, jnp.where(lg == 1, 4.0, jnp.where(lg == 2, 8.0, 16.0))).astype(F32)


def _pool_mean_minus_token(u, up, row0):
    lg, wv = _pool_window_lanes()
    sums = []
    c, p = u, up
    for k in (1, 2, 4, 8):
        c2 = c + _rows_before(c, p, k)
        p = p + pltpu.roll(p, k, 0)
        c = c2
        sums.append(c)
    win = jnp.where(lg == 0, sums[0], jnp.where(lg == 1, sums[1], jnp.where(lg == 2, sums[2], sums[3])))
    pos1 = (row0 + lax.broadcasted_iota(jnp.int32, u.shape, 0) + 1).astype(F32)
    cnt = jnp.minimum(pos1, wv)
    return win / cnt - u, cnt


def _conv_taps(z, zp, w0, w1, w2):
    z1 = _rows_before(z, zp, 1)
    z2 = _rows_before(z, zp, 2)
    return (w0 * z2 + w1 * z1) + w2 * z, z1, z2


CP_TM = 512


def _convpool_fwd(name, p, o, cw, pwbd, ps):
    s = p.shape[0]
    tm = CP_TM
    nb = s // tm

    def body(gb_ref, gc_ref, hin_ref, u_ref, gcp_ref, hinp_ref, up_ref, o_ref, cw_ref, pw_ref, ps_ref, mix_ref):
        i = pl.program_id(0)
        has_prev = i > 0
        z = gc_ref[...] * hin_ref[...]
        zp = jnp.where(has_prev, gcp_ref[...] * hinp_ref[...], 0.0)
        y3, _, _ = _conv_taps(z, zp, cw_ref[0:1, :], cw_ref[1:2, :], cw_ref[2:3, :])
        m, _ = _pool_mean_minus_token(u_ref[...], jnp.where(has_prev, up_ref[...], 0.0), i * tm)
        yp = jnp.dot(m.astype(BF16), pw_ref[...].astype(BF16), preferred_element_type=F32) * ps_ref[...]
        mix_ref[:, 0:AW] = o_ref[...]
        mix_ref[:, AW:AW + CW] = (gb_ref[...] * y3).astype(BF16)
        mix_ref[:, AW + CW:D] = yp.astype(BF16)

    def cur(col):
        return pl.BlockSpec((tm, CW), lambda i: (i, col))

    def prev(col):
        return pl.BlockSpec((tm, CW), lambda i: (jnp.maximum(i - 1, 0), col))

    def whole(a):
        return pl.BlockSpec(a.shape, lambda i: (0,) * a.ndim)

    return pl.pallas_call(
        body, name=name, grid=(nb,),
        in_specs=[cur(6), cur(7), cur(8), cur(9), prev(7), prev(8), prev(9),
                  pl.BlockSpec((tm, AW), lambda i: (i, 0)), whole(cw), whole(pwbd), whole(ps)],
        out_specs=pl.BlockSpec((tm, D), lambda i: (i, 0)),
        out_shape=jax.ShapeDtypeStruct((s, D), BF16),
        compiler_params=_cp("parallel"),
    )(p, p, p, p, p, p, p, o, cw, pwbd, ps)


def _convpool_bwd(name, p, dmix, cw, pwbd, ps):
    s = p.shape[0]
    tm = CP_TM
    nb = s // tm

    def body(gb_ref, gc_ref, hin_ref, u_ref, gcp_ref, hinp_ref, up_ref, gbn_ref, dyc_ref, dyp_ref, dycn_ref, dypn_ref,
             cw_ref, pw_ref, ps_ref, dcp_ref, dw0_ref, dw1_ref, dw2_ref, dps_ref, dpw_ref):
        i = pl.program_id(0)
        has_prev = i > 0
        has_next = i < nb - 1
        w0, w1, w2 = cw_ref[0:1, :], cw_ref[1:2, :], cw_ref[2:3, :]
        gb, gc, hin = gb_ref[...], gc_ref[...], hin_ref[...]
        dyc = dyc_ref[...]
        z = gc * hin
        zp = jnp.where(has_prev, gcp_ref[...] * hinp_ref[...], 0.0)
        y3, z1, z2 = _conv_taps(z, zp, w0, w1, w2)
        dy3 = dyc * gb
        dy3n = jnp.where(has_next, dycn_ref[...] * gbn_ref[...], 0.0)
        dz = w2 * dy3 + w1 * _rows_after(dy3, dy3n, 1) + w0 * _rows_after(dy3, dy3n, 2)
        pw = pw_ref[...].astype(BF16)
        psv = ps_ref[...]
        m, cnt = _pool_mean_minus_token(u_ref[...], jnp.where(has_prev, up_ref[...], 0.0), i * tm)
        mb = m.astype(BF16)
        dyp = dyp_ref[...]
        dmp = (dyp * psv).astype(BF16)
        dmpn = jnp.where(has_next, dypn_ref[...] * psv, 0.0).astype(BF16)
        nt = (((1,), (1,)), ((), ()))
        dm = lax.dot_general(dmp, pw, nt, preferred_element_type=F32)
        dmn = lax.dot_general(dmpn, pw, nt, preferred_element_type=F32)
        lg, wv = _pool_window_lanes()
        cc, cn = dm / cnt, dmn / wv
        sums = []
        for k in (1, 2, 4, 8):
            c2 = cc + _rows_after(cc, cn, k)
            cn = cn + pltpu.roll(cn, tm - k, 0)
            cc = c2
            sums.append(cc)
        du = jnp.where(lg == 0, sums[0], jnp.where(lg == 1, sums[1], jnp.where(lg == 2, sums[2], sums[3]))) - dm
        dcp_ref[:, 0:CW] = (dyc * y3).astype(BF16)
        dcp_ref[:, CW:2 * CW] = (dz * hin).astype(BF16)
        dcp_ref[:, 2 * CW:3 * CW] = (dz * gc).astype(BF16)
        dcp_ref[:, 3 * CW:4 * CW] = du.astype(BF16)
        parts = (jnp.sum(dy3 * z2, axis=0, keepdims=True),
                 jnp.sum(dy3 * z1, axis=0, keepdims=True),
                 jnp.sum(dy3 * z, axis=0, keepdims=True),
                 jnp.sum(dyp * jnp.dot(mb, pw, preferred_element_type=F32), axis=0, keepdims=True),
                 lax.dot_general(mb, dmp, (((0,), (0,)), ((), ())), preferred_element_type=F32))
        accs = (dw0_ref, dw1_ref, dw2_ref, dps_ref, dpw_ref)

        @pl.when(i == 0)
        def _():
            for a, v in zip(accs, parts):
                a[...] = v

        @pl.when(i > 0)
        def _():
            for a, v in zip(accs, parts):
                a[...] += v

    def cur(col):
        return pl.BlockSpec((tm, CW), lambda i: (i, col))

    def prev(col):
        return pl.BlockSpec((tm, CW), lambda i: (jnp.maximum(i - 1, 0), col))

    def nxt(col):
        return pl.BlockSpec((tm, CW), lambda i: (jnp.minimum(i + 1, nb - 1), col))

    def whole(shape):
        return pl.BlockSpec(shape, lambda i: (0,) * len(shape))

    row = jax.ShapeDtypeStruct((1, CW), F32)
    return pl.pallas_call(
        body, name=name, grid=(nb,),
        in_specs=[cur(6), cur(7), cur(8), cur(9), prev(7), prev(8), prev(9), nxt(6),
                  cur(0), cur(1), nxt(0), nxt(1), whole(cw.shape), whole(pwbd.shape), whole(ps.shape)],
        out_specs=[pl.BlockSpec((tm, D), lambda i: (i, 0)), whole((1, CW)), whole((1, CW)), whole((1, CW)),
                   whole((1, PWD)), whole((PWD, PWD))],
        out_shape=[jax.ShapeDtypeStruct((s, D), BF16), row, row, row, row,
                   jax.ShapeDtypeStruct((PWD, PWD), F32)],
        compiler_params=_cp("arbitrary"),
    )(p, p, p, p, p, p, p, p, dmix, dmix, dmix, dmix, cw, pwbd, ps)


def _qkv_bwd(name, p, dq, dkp, dvp, dcp, qg, kg):
    s = p.shape[0]
    tm = 256
    off = PADR // tm

    def body(pq_ref, pk_ref, dq_ref, dk_ref, dv_ref, dcp_ref, qg_ref, kg_ref, dp_ref, dqg_ref, dkg_ref):
        i = pl.program_id(0)
        hm = _head_mean_matrix()

        def nrm_bwd(x, g, dy):
            r = lax.rsqrt(_head_mean(x * x, hm) + EPS)
            xn = x * r
            dxn = dy * g
            dx = r * (dxn - xn * _head_mean(dxn * xn, hm))
            dg = jnp.sum(dy * xn, axis=0, keepdims=True)
            dg = (dg[:, 0:128] + dg[:, 128:256]) + (dg[:, 256:384] + dg[:, 384:512])
            return dx, dg + pltpu.roll(dg, HD, 1)

        dxq, dgq = nrm_bwd(pq_ref[...], qg_ref[...], dq_ref[...] * 0.125)
        dxk, dgk = nrm_bwd(pk_ref[...], kg_ref[...], dk_ref[...])
        dp_ref[:, 0:AW] = dxq.astype(BF16)
        dp_ref[:, AW:2 * AW] = dxk.astype(BF16)
        dp_ref[:, 2 * AW:3 * AW] = dv_ref[...].astype(BF16)
        dp_ref[:, 3 * AW:DIN] = dcp_ref[...]

        @pl.when(i == 0)
        def _():
            dqg_ref[...] = dgq
            dkg_ref[...] = dgk

        @pl.when(i > 0)
        def _():
            dqg_ref[...] += dgq
            dkg_ref[...] += dgk

    gspec = pl.BlockSpec((1, AW), lambda i: (0, 0))
    gout = pl.BlockSpec((1, 128), lambda i: (0, 0))
    return pl.pallas_call(
        body, name=name, grid=(s // tm,),
        in_specs=[pl.BlockSpec((tm, AW), lambda i: (i, 0)), pl.BlockSpec((tm, AW), lambda i: (i, 1)),
                  pl.BlockSpec((tm, AW), lambda i: (i, 0)),
                  pl.BlockSpec((tm, AW), lambda i: (i + off, 0)),
                  pl.BlockSpec((tm, AW), lambda i: (i + off, 0)),
                  pl.BlockSpec((tm, D), lambda i: (i, 0)), gspec, gspec],
        out_specs=[pl.BlockSpec((tm, DIN), lambda i: (i, 0)), gout, gout],
        out_shape=[jax.ShapeDtypeStruct((s, DIN), BF16), jax.ShapeDtypeStruct((1, 128), F32),
                   jax.ShapeDtypeStruct((1, 128), F32)],
        compiler_params=_cp("arbitrary"),
    )(p, p, dq, dkp, dvp, dcp, qg, kg)


def _loss_grad(name, y, t):
    s = y.shape[0]
    tm = 512

    def body(y_ref, t_ref, dy_ref, dyb_ref, l_ref):
        i = pl.program_id(0)
        e = y_ref[...] - t_ref[...]
        dy = e * (1.0 / D)
        dy_ref[...] = dy
        dyb_ref[...] = dy.astype(BF16)
        part = 0.5 * jnp.sum(jnp.mean(e * e, axis=-1, keepdims=True), axis=0, keepdims=True)

        @pl.when(i == 0)
        def _():
            l_ref[...] = part

        @pl.when(i > 0)
        def _():
            l_ref[...] += part

    blk = pl.BlockSpec((tm, D), lambda i: (i, 0))
    return pl.pallas_call(
        body, name=name, grid=(s // tm,),
        in_specs=[blk, blk],
        out_specs=[blk, blk, pl.BlockSpec((1, 1), lambda i: (0, 0))],
        out_shape=[jax.ShapeDtypeStruct((s, D), F32), jax.ShapeDtypeStruct((s, D), BF16),
                   jax.ShapeDtypeStruct((1, 1), F32)],
        compiler_params=_cp("arbitrary"),
    )(y, t)


def _mm_nt_relu(name, dxb, w, l, a):
    s = dxb.shape[0]
    tm, tn = 512, 1024

    def body(d_ref, w_ref, a_ref, o_ref):
        df = lax.dot_general(d_ref[...], w_ref[...], (((1,), (1,)), ((), ())), preferred_element_type=F32)
        o_ref[...] = (df * (2.0 * jnp.maximum(a_ref[...].astype(F32), 0.0))).astype(BF16)

    return pl.pallas_call(
        body, name=name, grid=(s // tm, DFF // tn),
        in_specs=[pl.BlockSpec((tm, D), lambda i, j: (i, 0)),
                  pl.BlockSpec((None, tn, D), lambda i, j: (l, j, 0)),
                  pl.BlockSpec((tm, tn), lambda i, j: (i, j))],
        out_specs=pl.BlockSpec((tm, tn), lambda i, j: (i, j)),
        out_shape=jax.ShapeDtypeStruct((s, DFF), BF16),
        compiler_params=_cp("parallel", "parallel"),
    )(dxb, w, a)


def _proj_out_bwd(name, dxb, w, l):
    s = dxb.shape[0]
    tm = 512

    def body(d_ref, w_ref, do_ref, dot_ref, dcp_ref):
        d = d_ref[...]
        wa, wc = w_ref[0:AW, :], w_ref[AW:D, :]
        do_ref[...] = lax.dot_general(d, wa, NT_DIMS, preferred_element_type=F32).astype(BF16)
        dot_ref[...] = lax.dot_general(wa, d, NT_DIMS, preferred_element_type=F32).astype(BF16)
        dcp_ref[...] = lax.dot_general(d, wc, NT_DIMS, preferred_element_type=F32)

    return pl.pallas_call(
        body, name=name, grid=(s // tm,),
        in_specs=[pl.BlockSpec((tm, D), lambda i: (i, 0)),
                  pl.BlockSpec((None, D, D), lambda i: (l, 0, 0))],
        out_specs=[pl.BlockSpec((tm, AW), lambda i: (i, 0)), pl.BlockSpec((AW, tm), lambda i: (0, i)),
                   pl.BlockSpec((tm, D - AW), lambda i: (i, 0))],
        out_shape=[jax.ShapeDtypeStruct((s, AW), BF16), jax.ShapeDtypeStruct((AW, s), BF16),
                   jax.ShapeDtypeStruct((s, D - AW), F32)],
        compiler_params=_cp("parallel"),
    )(dxb, w)


def _mm_nt_normbwd(name, gy, w, l, x, g, dres, dep):
    s, k = gy.shape
    tm = 256

    def body(gy_ref, w_ref, x_ref, g_ref, dr_ref, dep_ref, dx_ref, dxb_ref, dg_ref):
        del dep_ref
        i = pl.program_id(0)
        dh = lax.dot_general(gy_ref[...], w_ref[...], (((1,), (1,)), ((), ())), preferred_element_type=F32)
        xv = x_ref[...]
        r = _inv_rms(xv)
        xn = xv * r
        dxn = dh * g_ref[...]
        dx = r * (dxn - xn * jnp.mean(dxn * xn, axis=-1, keepdims=True)) + dr_ref[...]
        dx_ref[...] = dx
        dxb_ref[...] = dx.astype(BF16)
        part = jnp.sum(dh * xn, axis=0, keepdims=True)

        @pl.when(i == 0)
        def _():
            dg_ref[...] = part

        @pl.when(i > 0)
        def _():
            dg_ref[...] += part

    blk = pl.BlockSpec((tm, D), lambda i: (i, 0))
    vec = pl.BlockSpec((1, D), lambda i: (0, 0))
    return pl.pallas_call(
        body, name=name, grid=(s // tm,),
        in_specs=[pl.BlockSpec((tm, k), lambda i: (i, 0)),
                  pl.BlockSpec((None, D, k), lambda i: (l, 0, 0)), blk, vec, blk, ANY],
        out_specs=[blk, blk, vec],
        out_shape=[jax.ShapeDtypeStruct((s, D), F32), jax.ShapeDtypeStruct((s, D), BF16),
                   jax.ShapeDtypeStruct((1, D), F32)],
        compiler_params=_cp("arbitrary"),
    )(gy, w, x, g, dres, dep)


def _mm_tn(name, a, b, tma, tnb):
    s, m = a.shape
    n = b.shape[1]

    def body(a_ref, b_ref, o_ref):
        o_ref[...] = lax.dot_general(a_ref[...], b_ref[...], (((0,), (0,)), ((), ())),
                                     preferred_element_type=F32).astype(BF16)

    return pl.pallas_call(
        body, name=name, grid=(m // tma, n // tnb),
        in_specs=[pl.BlockSpec((s, tma), lambda i, j: (0, i)),
                  pl.BlockSpec((s, tnb), lambda i, j: (0, j))],
        out_specs=pl.BlockSpec((tma, tnb), lambda i, j: (i, j)),
        out_shape=jax.ShapeDtypeStruct((m, n), BF16),
        compiler_params=_cp("parallel", "parallel"),
    )(a, b)


def _adamw_math(gv, wv, mv, vv):
    mn = ADAM_B1 * mv + (1.0 - ADAM_B1) * gv
    vn = ADAM_B2 * vv + (1.0 - ADAM_B2) * jnp.square(gv)
    m_hat = mn / (1.0 - ADAM_B1 ** ADAM_STEP)
    v_hat = vn / (1.0 - ADAM_B2 ** ADAM_STEP)
    return gv, -ADAM_LR * (m_hat / (jnp.sqrt(v_hat) + ADAM_EPS) + ADAM_WD * wv), mn, vn


def _adamw(name, g, w, m, v):
    r, c = g.shape
    tm = 256 if r % 256 == 0 else r

    def body(g_ref, w_ref, m_ref, v_ref, go_ref, d_ref, mo_ref, vo_ref):
        go_ref[...], d_ref[...], mo_ref[...], vo_ref[...] = _adamw_math(g_ref[...], w_ref[...], m_ref[...], v_ref[...])

    blk = pl.BlockSpec((tm, c), lambda i: (i, 0))
    return pl.pallas_call(
        body, name=name, grid=(r // tm,),
        in_specs=[blk] * 4, out_specs=[blk] * 4,
        out_shape=[jax.ShapeDtypeStruct((r, c), F32)] * 4,
        compiler_params=_cp("parallel"),
    )(g, w, m, v)


def _place():
    x, y, c = lax.axis_index("x"), lax.axis_index("y"), lax.axis_index("c")
    chips = [(1 - x, y), (x, 1 - y), (1 - x, 1 - y)]
    return x, y, c, chips


BLOCK_AXIS = (2, 1, 2, 1)
LARGE_DIMS = ((D, DIN), (D, D), (D, DFF), (DFF, D))


def _full_shape(t, layers, dtype):
    r, c = LARGE_DIMS[t]
    return jax.ShapeDtypeStruct((layers, r, c), dtype)


def _cast_into_full(name, t, shard, b1, dep):
    _, r, c = shard.shape
    tm = min(256, r)
    if BLOCK_AXIS[t] == 1:
        out_spec = pl.BlockSpec((None, tm, c), lambda l, i, br: (l, br[0] * (r // tm) + i, 0))
    else:
        out_spec = pl.BlockSpec((None, tm, c), lambda l, i, br: (l, i, br[0]))

    def body(b_ref, x_ref, dep_ref, o_ref):
        del b_ref, dep_ref
        o_ref[...] = x_ref[...].astype(BF16)

    return pl.pallas_call(
        body, name=name,
        grid_spec=pltpu.PrefetchScalarGridSpec(
            num_scalar_prefetch=1, grid=(DEPTH, r // tm),
            in_specs=[pl.BlockSpec((None, tm, c), lambda l, i, br: (l, i, 0)), ANY],
            out_specs=out_spec),
        out_shape=_full_shape(t, DEPTH, BF16),
        compiler_params=_cp("parallel", "parallel"),
    )(b1, shard, dep)


HBM = pl.BlockSpec(memory_space=pltpu.HBM)
SEM = pl.BlockSpec(memory_space=pltpu.SEMAPHORE)
DATAFLOW = pltpu.SideEffectType.DATAFLOW_SIDE_EFFECTING


def _half(ref, l, t, b, c):
    r, cols = LARGE_DIMS[t]
    if BLOCK_AXIS[t] == 1:
        n = r // 8
        return ref.at[l, pl.ds(pl.multiple_of(b * (2 * n) + c * n, 16), n), :]
    n, w = r // 2, cols // 4
    return ref.at[l, pl.ds(pl.multiple_of(c * n, 16), n), pl.ds(pl.multiple_of(b * w, 128), w)]


def _gather_start(name, layers, fulls):
    def body(*refs):
        f_refs, sems = refs[4:8], refs[8:8 + 2 * len(layers)]
        x, y, c, chips = _place()
        for i, l in enumerate(layers):
            for t in range(4):
                own = _half(f_refs[t], l, t, 2 * x + y, c)
                for j, (cx, cy) in enumerate(chips):
                    pltpu.make_async_remote_copy(src_ref=own, dst_ref=own, send_sem=sems[2 * i].at[3 * t + j],
                                                 recv_sem=sems[2 * i + 1].at[3 * t + j], device_id=(cx, cy, c),
                                                 device_id_type=MESH).start()

    outs = pl.pallas_call(
        body, name=name,
        in_specs=[HBM] * 4, out_specs=[HBM] * 4 + [SEM] * (2 * len(layers)),
        out_shape=[pltpu.HBM(s.shape, s.dtype) for s in (_full_shape(t, DEPTH, BF16) for t in range(4))]
        + [pltpu.SemaphoreType.DMA((12,))] * (2 * len(layers)),
        input_output_aliases={t: t for t in range(4)},
        compiler_params=pltpu.CompilerParams(has_side_effects=DATAFLOW),
    )(*[pltpu.with_memory_space_constraint(f, pltpu.HBM) for f in fulls])
    return outs[0:4], {l: (outs[4 + 2 * i], outs[5 + 2 * i]) for i, l in enumerate(layers)}


def _gather_wait(l, fulls, sems, after):
    def body(*refs):
        send_sems, recv_sems, f_refs = refs[4], refs[5], refs[7:11]
        x, y, c, chips = _place()
        for t in range(4):
            own = _half(f_refs[t], l, t, 2 * x + y, c)
            for j, (cx, cy) in enumerate(chips):
                landed = _half(f_refs[t], l, t, 2 * cx + cy, c)
                pltpu.make_async_remote_copy(src_ref=own, dst_ref=landed, send_sem=send_sems.at[3 * t + j],
                                             recv_sem=recv_sems.at[3 * t + j], device_id=(cx, cy, c),
                                             device_id_type=MESH).wait()

    return pl.pallas_call(
        body, name=f"gather_wait_{l}",
        in_specs=[HBM] * 4 + [SEM, SEM, ANY], out_specs=[HBM] * 4,
        out_shape=[pltpu.HBM(s.shape, s.dtype) for s in (_full_shape(t, DEPTH, BF16) for t in range(4))],
        input_output_aliases={t: t for t in range(4)},
        compiler_params=pltpu.CompilerParams(has_side_effects=DATAFLOW),
    )(*fulls, sems[0], sems[1], after)


def _pass_on(l, fulls):
    def body(*refs):
        f_refs, send_sems, recv_sems = refs[4:8], refs[8], refs[9]
        x, y, c, chips = _place()

        def copy(t, j, half):
            cx, cy = chips[j]
            part = _half(f_refs[t], l, t, 2 * cx + cy, half)
            return pltpu.make_async_remote_copy(src_ref=part, dst_ref=part, send_sem=send_sems.at[3 * t + j],
                                                recv_sem=recv_sems.at[3 * t + j], device_id=(x, y, 1 - c),
                                                device_id_type=MESH)

        for t in range(4):
            for j in range(3):
                copy(t, j, c).start()
        for t in range(4):
            for j in range(3):
                copy(t, j, 1 - c).wait_recv()
                copy(t, j, c).wait_send()

    return pl.pallas_call(
        body, name=f"pass_on_{l}",
        in_specs=[ANY] * 4, out_specs=[ANY] * 4,
        out_shape=[_full_shape(t, DEPTH, BF16) for t in range(4)],
        input_output_aliases={t: t for t in range(4)},
        scratch_shapes=[pltpu.SemaphoreType.DMA((12,)), pltpu.SemaphoreType.DMA((12,))],
    )(*fulls)


def _block2d(ref, t, b):
    r, cols = LARGE_DIMS[t]
    if BLOCK_AXIS[t] == 1:
        return ref.at[pl.ds(pl.multiple_of(b * (r // 4), 16), r // 4), :]
    return ref.at[:, pl.ds(pl.multiple_of(b * (cols // 4), 128), cols // 4)]


def _block_dims(t):
    r, cols = LARGE_DIMS[t]
    return (r // 4, cols) if BLOCK_AXIS[t] == 1 else (r, cols // 4)


def _reduce_copies(ts, g_refs, r_refs, send_sems, recv_sems):
    _, _, c, chips = _place()
    return [pltpu.make_async_remote_copy(src_ref=_block2d(g_refs[i], t, 2 * cx + cy), dst_ref=r_refs[i].at[j],
                                         send_sem=send_sems.at[3 * i + j], recv_sem=recv_sems.at[3 * i + j],
                                         device_id=(cx, cy, c), device_id_type=MESH)
            for i, t in enumerate(ts) for j, (cx, cy) in enumerate(chips)]


def _reduce_start(name, ts, grads):
    n = len(ts)

    def body(*refs):
        for cp in _reduce_copies(ts, refs[n:2 * n], refs[2 * n:3 * n], refs[3 * n], refs[3 * n + 1]):
            cp.start()
        refs[3 * n + 2][...] = jnp.zeros((8, 128), F32)

    outs = pl.pallas_call(
        body, name=name,
        in_specs=[HBM] * n,
        out_specs=[HBM] * (2 * n) + [SEM, SEM, pl.BlockSpec(memory_space=pltpu.VMEM)],
        out_shape=[pltpu.HBM(g.shape, BF16) for g in grads]
        + [pltpu.HBM((3,) + _block_dims(t), BF16) for t in ts]
        + [pltpu.SemaphoreType.DMA((3 * n,)), pltpu.SemaphoreType.DMA((3 * n,)), jax.ShapeDtypeStruct((8, 128), F32)],
        input_output_aliases={i: i for i in range(n)},
        compiler_params=pltpu.CompilerParams(has_side_effects=DATAFLOW),
    )(*[pltpu.with_memory_space_constraint(g, pltpu.HBM) for g in grads])
    return outs[0:n], outs[n:2 * n], (outs[2 * n], outs[2 * n + 1]), outs[2 * n + 2]


def _reduce_wait(name, ts, grads, landing, sems, afters):
    n = len(ts)
    first_out = 2 * n + 2 + len(afters)

    def body(*refs):
        for cp in _reduce_copies(ts, refs[first_out:first_out + n], refs[first_out + n:first_out + 2 * n],
                                 refs[2 * n], refs[2 * n + 1]):
            cp.wait()

    outs = pl.pallas_call(
        body, name=name,
        in_specs=[HBM] * (2 * n) + [SEM, SEM] + [ANY] * len(afters), out_specs=[HBM] * (2 * n),
        out_shape=[pltpu.HBM(g.shape, BF16) for g in grads] + [pltpu.HBM(r.shape, BF16) for r in landing],
        input_output_aliases={i: i for i in range(2 * n)},
        compiler_params=pltpu.CompilerParams(has_side_effects=DATAFLOW),
    )(*grads, *landing, sems[0], sems[1], *afters)
    return outs[0:n], outs[n:2 * n]


def _add4(name, t, own, landed, b1):
    rb, cb = _block_dims(t)
    tm = min(256, rb)
    if BLOCK_AXIS[t] == 1:
        own_spec = pl.BlockSpec((tm, cb), lambda i, br: (br[0] * (rb // tm) + i, 0))
    else:
        own_spec = pl.BlockSpec((tm, cb), lambda i, br: (i, br[0]))

    def body(b_ref, o_ref, r0_ref, r1_ref, r2_ref, s_ref):
        del b_ref
        s_ref[...] = ((o_ref[...].astype(F32) + r0_ref[...].astype(F32))
                      + (r1_ref[...].astype(F32) + r2_ref[...].astype(F32))).astype(BF16)

    def got(j):
        return pl.BlockSpec((None, tm, cb), lambda i, br: (j, i, 0))

    return pl.pallas_call(
        body, name=name,
        grid_spec=pltpu.PrefetchScalarGridSpec(
            num_scalar_prefetch=1, grid=(rb // tm,),
            in_specs=[own_spec, got(0), got(1), got(2)],
            out_specs=pl.BlockSpec((tm, cb), lambda i, br: (i, 0))),
        out_shape=jax.ShapeDtypeStruct((rb, cb), BF16),
        compiler_params=_cp("parallel"),
    )(b1, own, landed, landed, landed)


def _swap_sib(name, sums):
    def body(*refs):
        s_refs, t_refs, send_sems, recv_sems = refs[0:4], refs[4:8], refs[8], refs[9]
        x, y, c, _ = _place()
        cps = [pltpu.make_async_remote_copy(src_ref=s_refs[t], dst_ref=t_refs[t], send_sem=send_sems.at[t],
                                            recv_sem=recv_sems.at[t], device_id=(x, y, 1 - c), device_id_type=MESH)
               for t in range(4)]
        for cp in cps:
            cp.start()
        for cp in cps:
            cp.wait()

    return pl.pallas_call(
        body, name=name,
        in_specs=[ANY] * 4, out_specs=[ANY] * 4,
        out_shape=[jax.ShapeDtypeStruct(s.shape, BF16) for s in sums],
        scratch_shapes=[pltpu.SemaphoreType.DMA((4,)), pltpu.SemaphoreType.DMA((4,))],
    )(*sums)


def _adamw_pair(name, l, s_own, s_sib, w, m, v, outs):
    rb, cb = s_own.shape
    tm = min(256, rb)

    def body(a_ref, b_ref, w_ref, m_ref, v_ref, g0, d0, m0, v0, go_ref, d_ref, mo_ref, vo_ref):
        del g0, d0, m0, v0
        gv = a_ref[...].astype(F32) + b_ref[...].astype(F32)
        go_ref[...], d_ref[...], mo_ref[...], vo_ref[...] = _adamw_math(gv, w_ref[...], m_ref[...], v_ref[...])

    part = pl.BlockSpec((tm, cb), lambda i: (i, 0))
    layer = pl.BlockSpec((None, tm, cb), lambda i: (l, i, 0))
    return pl.pallas_call(
        body, name=name, grid=(rb // tm,),
        in_specs=[part, part, layer, layer, layer] + [ANY] * 4,
        out_specs=[layer] * 4,
        out_shape=[jax.ShapeDtypeStruct((DEPTH, rb, cb), F32)] * 4,
        input_output_aliases={5 + i: i for i in range(4)},
        compiler_params=_cp("parallel"),
    )(s_own, s_sib, w, m, v, *outs)


def _all_gather8(name, v, dep):
    m_per, n = v.shape

    def body(v_ref, dep_ref, out_ref, send_sems, recv_sems, local_sem):
        del dep_ref
        x, y, c, chips = _place()
        me, sib = (x, y, c), (x, y, 1 - c)

        def rows(px, py, pc):
            return out_ref.at[pl.ds((4 * px + 2 * py + pc) * m_per, m_per), :]

        def copy(k, block, to, src=None):
            return pltpu.make_async_remote_copy(
                src_ref=rows(*block) if src is None else src, dst_ref=rows(*block),
                send_sem=send_sems.at[k], recv_sem=recv_sems.at[k], device_id=to, device_id_type=MESH)

        mine = pltpu.make_async_copy(v_ref, rows(*me), local_sem)
        mine.start()
        first = [copy(0, me, sib, src=v_ref)]
        first += [copy(1 + j, me, (*chip, c), src=v_ref) for j, chip in enumerate(chips)]
        for cp in first:
            cp.start()
        passed = [copy(4 + j, (*chip, c), sib) for j, chip in enumerate(chips)]
        for j, chip in enumerate(chips):
            copy(1 + j, (*chip, c), me).wait_recv()
            passed[j].start()
        copy(0, sib, me).wait_recv()
        for j, chip in enumerate(chips):
            copy(4 + j, (*chip, 1 - c), me).wait_recv()
        for cp in first + passed:
            cp.wait_send()
        mine.wait()

    return pl.pallas_call(
        body, name=name,
        out_shape=jax.ShapeDtypeStruct((8 * m_per, n), v.dtype),
        in_specs=[pl.BlockSpec(memory_space=pltpu.VMEM), ANY],
        out_specs=pl.BlockSpec(memory_space=pltpu.VMEM),
        scratch_shapes=[pltpu.SemaphoreType.DMA((7,)), pltpu.SemaphoreType.DMA((7,)), pltpu.SemaphoreType.DMA],
    )(v, dep)


def _sum8(name, g):
    def body(g_ref, o_ref):
        acc = g_ref[0]
        for d in range(1, 8):
            acc = acc + g_ref[d]
        o_ref[...] = acc

    return pl.pallas_call(body, name=name, out_shape=jax.ShapeDtypeStruct(g.shape[1:], F32))(g)


def _pack(parts):
    flat = []
    for a in parts:
        a = a.reshape(-1)
        flat.append(jnp.pad(a, (0, (-a.shape[0]) % 128)))
    cat = jnp.concatenate(flat)
    cat = jnp.pad(cat, (0, (-cat.shape[0]) % 1024))
    return cat.reshape(-1, 128)


def _unpack(packed, shapes):
    flat = packed.reshape(-1)
    out, at = [], 0
    for shp in shapes:
        n = 1
        for d in shp:
            n *= d
        out.append(flat[at:at + n].reshape(shp))
        at += n + (-n) % 128
    return out


def _local_step(x, target, layer_weights, on_grads, small):
    saved = []
    xin = x
    h = _rmsnorm("norm_first", x, small["norm1_g"][0:1])
    for l in range(DEPTH):
        w_in, w_out, w_1, w_2 = layer_weights(l, xin)
        qg = jnp.tile(small["q_norm_g"][l], 8)[None]
        kg = jnp.tile(small["k_norm_g"][l], 8)[None]
        rb = jnp.pad(small["rel_bias"][l], ((0, 0), (0, NIDX - 257)))
        bias = _bias_layout(_bias_expand(f"bias_expand_{l}", rb))
        cw = small["conv_w"][l]
        pwbd = jax.scipy.linalg.block_diag(*[small["pool_w"][l, g] for g in range(4)])
        ps = small["pool_scale"][l][None]
        p = _mm_nn(f"proj_in_{l}", h, w_in, l, 512, 512, F32)
        q, qt, kp, kt, vp, vt = _qkv(f"qkv_{l}", p, qg, kg)
        o = _attn_fwd(f"attn_fwd_{l}", kp, qt, vt, bias)
        mix = _convpool_fwd(f"convpool_fwd_{l}", p, o, cw, pwbd, ps)
        x1, h2 = _mm_res_norm(f"proj_out_{l}", mix, w_out, l, xin, small["norm2_g"][l:l + 1])
        a, f = _mm_mlp1(f"mlp1_{l}", h2, w_1, l)
        gnext = small["norm1_g"][(l + 1) % DEPTH][None]
        x2, hnext = _mm_res_norm(f"mlp2_{l}", f, w_2, l, x1, gnext)
        saved.append(dict(xin=xin, h=h, p=p, q=q, qt=qt, kp=kp, kt=kt, vp=vp, bias=bias, mix=mix, x1=x1, h2=h2, a=a, f=f,
                          qg=qg, kg=kg, cw=cw, pwbd=pwbd, ps=ps))
        xin, h = x2, hnext

    dx, dxb, loss = _loss_grad("loss_grad", xin, target)
    gs = {k: [None] * DEPTH for k in ("norm1_g", "q_norm_g", "k_norm_g", "rel_bias", "conv_w", "pool_w",
                                      "pool_scale", "norm2_g")}
    for l in reversed(range(DEPTH)):
        sv = saved[l]
        da = _mm_nt_relu(f"mlp2_bwd_{l}", dxb, w_2, l, sv["a"])
        g_2 = _mm_tn(f"mlp2_wgrad_{l}", sv["f"], dxb, 512, 512)
        g_1 = _mm_tn(f"mlp1_wgrad_{l}", sv["h2"], da, 512, 512)
        dep = on_grads(l, (2, 3), (g_1, g_2))
        dx1, dx1b, dg2 = _mm_nt_normbwd(f"mlp1_bwd_{l}", da, w_1, l, sv["x1"], small["norm2_g"][l:l + 1], dx, dep)
        do, dot, dmix = _proj_out_bwd(f"proj_out_bwd_{l}", dx1b, w_out, l)
        g_out = _mm_tn(f"proj_out_wgrad_{l}", sv["mix"], dx1b, 512, 512)
        dcp, dw0, dw1, dw2, dps, dpw = _convpool_bwd(f"convpool_bwd_{l}", sv["p"], dmix, sv["cw"], sv["pwbd"], sv["ps"])
        dq, dkp, dvp, db = _attn_bwd(f"attn_bwd_{l}", sv["q"], sv["qt"], sv["kp"], sv["kt"], sv["vp"], sv["bias"],
                                     do, dot)
        drb = _bias_reduce(f"bias_reduce_{l}", _bias_unlayout(db))
        dp, dqg, dkg = _qkv_bwd(f"qkv_bwd_{l}", sv["p"], dq, dkp, dvp, dcp, sv["qg"], sv["kg"])
        g_in = _mm_tn(f"proj_in_wgrad_{l}", sv["h"], dp, 512, 640)
        dep = on_grads(l, (0, 1), (g_in, g_out))
        dx, dxb, dg1 = _mm_nt_normbwd(f"proj_in_bwd_{l}", dp, w_in, l, sv["xin"], small["norm1_g"][l:l + 1], dx1, dep)
        gs["norm1_g"][l] = dg1[0]
        gs["q_norm_g"][l] = dqg[0, :HD]
        gs["k_norm_g"][l] = dkg[0, :HD]
        gs["rel_bias"][l] = drb[:, :257]
        gs["conv_w"][l] = jnp.concatenate([dw0, dw1, dw2], axis=0)
        gs["pool_w"][l] = jnp.stack([dpw[g * 64:(g + 1) * 64, g * 64:(g + 1) * 64] for g in range(4)])
        gs["pool_scale"][l] = dps[0]
        gs["norm2_g"][l] = dg2[0]
    gsmall = {k: jnp.stack(v) for k, v in gs.items()}
    return loss, dx, gsmall


SMALL = ("norm1_g", "q_norm_g", "k_norm_g", "rel_bias", "conv_w", "pool_w", "pool_scale", "norm2_g")
LARGE = ("w_in", "w_out", "w_mlp1", "w_mlp2")


def kernel(x, norm1_g, w_in, q_norm_g, k_norm_g, rel_bias, conv_w, pool_w, pool_scale, w_out, norm2_g, w_mlp1, w_mlp2, loss_target, m_norm1_g, m_w_in, m_q_norm_g, m_k_norm_g, m_rel_bias, m_conv_w, m_pool_w, m_pool_scale, m_w_out, m_norm2_g, m_w_mlp1, m_w_mlp2, v_norm1_g, v_w_in, v_q_norm_g, v_k_norm_g, v_rel_bias, v_conv_w, v_pool_w, v_pool_scale, v_w_out, v_norm2_g, v_w_mlp1, v_w_mlp2):
    w = dict(norm1_g=norm1_g, w_in=w_in, q_norm_g=q_norm_g, k_norm_g=k_norm_g, rel_bias=rel_bias, conv_w=conv_w,
             pool_w=pool_w, pool_scale=pool_scale, w_out=w_out, norm2_g=norm2_g, w_mlp1=w_mlp1, w_mlp2=w_mlp2)
    m = dict(norm1_g=m_norm1_g, w_in=m_w_in, q_norm_g=m_q_norm_g, k_norm_g=m_k_norm_g, rel_bias=m_rel_bias,
             conv_w=m_conv_w, pool_w=m_pool_w, pool_scale=m_pool_scale, w_out=m_w_out, norm2_g=m_norm2_g,
             w_mlp1=m_w_mlp1, w_mlp2=m_w_mlp2)
    v = dict(norm1_g=v_norm1_g, w_in=v_w_in, q_norm_g=v_q_norm_g, k_norm_g=v_k_norm_g, rel_bias=v_rel_bias,
             conv_w=v_conv_w, pool_w=v_pool_w, pool_scale=v_pool_scale, w_out=v_w_out, norm2_g=v_norm2_g,
             w_mlp1=v_w_mlp1, w_mlp2=v_w_mlp2)
    ax, ay, ac = lax.axis_index("x"), lax.axis_index("y"), lax.axis_index("c")
    b1 = jnp.reshape(2 * ax + ay, (1,)).astype(jnp.int32)

    cw_rows = _all_gather8("gather_conv_w", jnp.pad(conv_w.reshape(DEPTH * 3, 64), ((0, 4), (0, 64))), b1)
    cw_chips = [cw_rows[(4 * cx + 2 * cy) * 16:(4 * cx + 2 * cy) * 16 + 12, :64] for cx in range(2) for cy in range(2)]
    small = {n: w[n] for n in SMALL}
    small["conv_w"] = jnp.concatenate(cw_chips, axis=1).reshape(DEPTH, 3, CW)

    casts = [_cast_into_full(f"cast_{n}", t, w[n], b1, cw_rows) for t, n in enumerate(LARGE)]
    first, first_sems = _gather_start("gather_start_first", (0,), casts)
    held = [first]
    sems = dict(first_sems)

    def layer_weights(l, after):
        arrived = _gather_wait(l, held[0], sems[l], after)
        if l == 0:
            arrived, rest_sems = _gather_start("gather_start_rest", tuple(range(1, DEPTH)), arrived)
            sems.update(rest_sems)
        held[0] = _pass_on(l, arrived)
        return held[0]

    flights = {}

    def await_flight(l, ts, afters):
        g, landing, sm, _ = flights[l, ts]
        flights[l, ts] = _reduce_wait(f"reduce_wait_{l}_{ts[0]}", ts, g, landing, sm, afters)

    def on_grads(l, ts, grads):
        if ts == (0, 1) and l + 1 < DEPTH:
            await_flight(l + 1, (2, 3), [grads[0]])
            await_flight(l + 1, (0, 1), [grads[0]])
        flights[l, ts] = _reduce_start(f"reduce_start_{l}_{ts[0]}", ts, grads)
        return flights[l, ts][3]

    loss_part, grad_x, gsmall = _local_step(x[0], loss_target[0], layer_weights, on_grads, small)
    loss = lax.psum(loss_part[0, 0], ("x", "y", "c"))
    order = [n for n in SMALL]
    packed = _pack([gsmall[n] for n in order])

    out = {n: [lax.empty(w[n].shape, F32) for _ in range(4)] for n in LARGE}
    for l in reversed(range(DEPTH)):
        if l == 0:
            afters = [grad_x, packed] + [out[n][0] for n in LARGE]
            await_flight(0, (2, 3), afters)
            await_flight(0, (0, 1), afters)
        sums = [None] * 4
        for ts in ((0, 1), (2, 3)):
            g, landing = flights[l, ts]
            for i, t in enumerate(ts):
                sums[t] = _add4(f"add4_{LARGE[t]}_{l}", t, g[i], landing[i], b1)
        theirs = _swap_sib(f"swap_sib_{l}", sums)
        for t, n in enumerate(LARGE):
            out[n] = _adamw_pair(f"adamw_{n}_{l}", l, sums[t], theirs[t], w[n], m[n], v[n], out[n])

    rows = packed.shape[0]
    summed = _sum8("sum_small", _all_gather8("gather_small", packed, out[LARGE[0]][0]).reshape(8, rows, 128))
    gfull = dict(zip(order, _unpack(summed, [gsmall[n].shape for n in order])))
    gfull["conv_w"] = lax.dynamic_slice_in_dim(gfull["conv_w"], (2 * ax + ay) * 64, 64, axis=2)
    res = _adamw("adamw_small", _pack([gfull[n] for n in order]), _pack([w[n] for n in order]),
                 _pack([m[n] for n in order]), _pack([v[n] for n in order]))
    for n, parts in zip(order, zip(*[_unpack(r, [w[k].shape for k in order]) for r in res])):
        out[n] = list(parts)

    names = ("norm1_g", "w_in", "q_norm_g", "k_norm_g", "rel_bias", "conv_w", "pool_w", "pool_scale", "w_out",
             "norm2_g", "w_mlp1", "w_mlp2")
    flat = [loss, grad_x[None]]
    for i in range(4):
        flat += [out[n][i] for n in names]
    return tuple(flat)
```

```python
import functools

import jax
import jax.numpy as jnp
from jax import lax
from jax.experimental import pallas as pl
from jax.experimental.pallas import tpu as pltpu

F32 = jnp.float32
BF16 = jnp.bfloat16

D = 1024
DEPTH = 4
CH = 64
NPREV = 8
KB = (NPREV + 1) * CH
PADR = NPREV * CH
HD = 64
AW = 512
CW = 256
PWD = 256
DIN = 3 * AW + 3 * CW + PWD
DFF = 4 * D
NIDX = 384
EPS = 1e-6
NEG_INF = -1e30

ADAM_LR = 0.001
ADAM_B1 = 0.9
ADAM_B2 = 0.999
ADAM_EPS = 1e-08
ADAM_WD = 0.01
ADAM_STEP = 10

VMEM_LIMIT = 52 * 1024 * 1024
MESH = pl.DeviceIdType.MESH
ANY = pl.BlockSpec(memory_space=pl.ANY)


def _cp(*sem):
    return pltpu.CompilerParams(dimension_semantics=sem, vmem_limit_bytes=VMEM_LIMIT)


def _inv_rms(x):
    return lax.rsqrt(jnp.mean(x * x, axis=-1, keepdims=True) + EPS)


def _head_mean_matrix():
    r = lax.broadcasted_iota(jnp.int32, (AW, AW), 0) // HD
    c = lax.broadcasted_iota(jnp.int32, (AW, AW), 1) // HD
    return jnp.where(r == c, 1.0 / HD, 0.0).astype(BF16)


def _head_mean(x, hm):
    hi = x.astype(BF16)
    lo = (x - hi.astype(F32)).astype(BF16)
    return (jnp.dot(hi, hm, preferred_element_type=F32)
            + jnp.dot(lo, hm, preferred_element_type=F32))


def _rmsnorm(name, x, g):
    s = x.shape[0]
    tm = 512

    def body(x_ref, g_ref, h_ref):
        xv = x_ref[...]
        h_ref[...] = (xv * _inv_rms(xv) * g_ref[...]).astype(BF16)

    return pl.pallas_call(
        body, name=name, grid=(s // tm,),
        in_specs=[pl.BlockSpec((tm, D), lambda i: (i, 0)), pl.BlockSpec((1, D), lambda i: (0, 0))],
        out_specs=pl.BlockSpec((tm, D), lambda i: (i, 0)),
        out_shape=jax.ShapeDtypeStruct((s, D), BF16),
        compiler_params=_cp("parallel"),
    )(x, g)


def _relu2(a):
    r = jnp.maximum(a, jnp.zeros_like(a))
    return r * r


def _mm_nn(name, a, w, l, out_dtype):
    s, k = a.shape
    n = w.shape[2]
    tm = 256

    def body(a_ref, w_ref, o_ref):
        o_ref[...] = jnp.dot(a_ref[...], w_ref[...], preferred_element_type=F32).astype(o_ref.dtype)

    return pl.pallas_call(
        body, name=name, grid=(s // tm,),
        in_specs=[pl.BlockSpec((tm, k), lambda i: (i, 0)),
                  pl.BlockSpec((None, k, n), lambda i: (l, 0, 0))],
        out_specs=pl.BlockSpec((tm, n), lambda i: (i, 0)),
        out_shape=jax.ShapeDtypeStruct((s, n), out_dtype),
        compiler_params=_cp("parallel"),
    )(a, w)


def _mm_res_norm(name, a, w, l, res, g, relu2=False):
    s, k = a.shape
    tm = 256

    def body(a_ref, w_ref, r_ref, g_ref, x_ref, h_ref):
        av = _relu2(a_ref[...]) if relu2 else a_ref[...]
        acc = r_ref[...] + jnp.dot(av, w_ref[...], preferred_element_type=F32)
        x_ref[...] = acc
        h_ref[...] = (acc * _inv_rms(acc) * g_ref[...]).astype(BF16)

    return pl.pallas_call(
        body, name=name, grid=(s // tm,),
        in_specs=[pl.BlockSpec((tm, k), lambda i: (i, 0)),
                  pl.BlockSpec((None, k, D), lambda i: (l, 0, 0)),
                  pl.BlockSpec((tm, D), lambda i: (i, 0)),
                  pl.BlockSpec((1, D), lambda i: (0, 0))],
        out_specs=[pl.BlockSpec((tm, D), lambda i: (i, 0))] * 2,
        out_shape=[jax.ShapeDtypeStruct((s, D), F32), jax.ShapeDtypeStruct((s, D), BF16)],
        compiler_params=_cp("parallel"),
    )(a, w, res, g)


def _qkv(name, p, qg, kg):
    s = p.shape[0]
    tm = PADR
    nb = s // tm

    def body(pq_ref, pk_ref, pv_ref, qg_ref, kg_ref, q_ref, qt_ref, k_ref, kt_ref, v_ref, vt_ref):
        t = pl.program_id(0)
        hm = _head_mean_matrix()

        def nrm(x, g):
            return x * lax.rsqrt(_head_mean(x * x, hm) + EPS) * g

        first = t == 0
        qq = nrm(pq_ref[...], qg_ref[...]) * 0.125
        kk = jnp.where(first, 0.0, nrm(pk_ref[...], kg_ref[...]))
        vv = jnp.where(first, 0.0, pv_ref[...])
        q_ref[...] = qq.astype(BF16)
        qt_ref[...] = qq.T.astype(BF16)
        k_ref[...] = kk.astype(BF16)
        kt_ref[...] = kk.T.astype(BF16)
        v_ref[...] = vv.astype(BF16)
        vt_ref[...] = vv.T.astype(BF16)

    def src(col):
        return pl.BlockSpec((tm, AW), lambda t: (jnp.maximum(t - 1, 0), col))

    gspec = pl.BlockSpec((1, AW), lambda t: (0, 0))
    rows = pl.BlockSpec((tm, AW), lambda t: (t, 0))
    cols = pl.BlockSpec((AW, tm), lambda t: (0, t))
    return pl.pallas_call(
        body, name=name, grid=(nb + 1,),
        in_specs=[src(0), src(1), src(2), gspec, gspec],
        out_specs=[pl.BlockSpec((tm, AW), lambda t: (jnp.maximum(t - 1, 0), 0)),
                   pl.BlockSpec((AW, tm), lambda t: (0, jnp.maximum(t - 1, 0))),
                   rows, cols, rows, cols],
        out_shape=[jax.ShapeDtypeStruct((s, AW), BF16), jax.ShapeDtypeStruct((AW, s), BF16),
                   jax.ShapeDtypeStruct((s + PADR, AW), BF16), jax.ShapeDtypeStruct((AW, s + PADR), BF16),
                   jax.ShapeDtypeStruct((s + PADR, AW), BF16), jax.ShapeDtypeStruct((AW, s + PADR), BF16)],
        compiler_params=_cp("arbitrary"),
    )(p, p, p, qg, kg)


NBAND = KB // CH
HIGHEST = lax.Precision.HIGHEST
NT_DIMS = (((1,), (1,)), ((), ()))


def _onehot_table(a):
    m = lax.broadcasted_iota(jnp.int32, (128, NIDX), 0)
    idx = lax.broadcasted_iota(jnp.int32, (128, NIDX), 1)
    rel = jnp.clip(KB - 1 - (CH * a + m), -128, 128) + 128
    return jnp.where(rel == idx, 1.0, 0.0).astype(F32)


def _onehot_diagonal():
    r = lax.broadcasted_iota(jnp.int32, (CH * CH, 128), 0)
    m = lax.broadcasted_iota(jnp.int32, (CH * CH, 128), 1)
    return jnp.where((r % CH) - (r // CH) + (CH - 1) == m, 1.0, 0.0).astype(F32)


def _bias_expand(name, rb):
    def body(rb_ref, o_ref):
        along = [lax.dot_general(rb_ref[...], _onehot_table(a), NT_DIMS, preferred_element_type=F32,
                                 precision=HIGHEST) for a in range(NBAND)]
        o_ref[...] = lax.dot_general(jnp.concatenate(along, axis=0), _onehot_diagonal(), NT_DIMS,
                                     preferred_element_type=F32, precision=HIGHEST)

    return pl.pallas_call(
        body, name=name,
        out_shape=jax.ShapeDtypeStruct((NBAND * 8, CH * CH), F32),
    )(rb)


def _bias_reduce(name, db):
    def body(db_ref, o_ref):
        along = jnp.dot(db_ref[...], _onehot_diagonal(), preferred_element_type=F32, precision=HIGHEST)
        acc = jnp.zeros((8, NIDX), F32)
        for a in range(NBAND):
            acc = acc + jnp.dot(along[8 * a:8 * a + 8, :], _onehot_table(a), preferred_element_type=F32,
                                precision=HIGHEST)
        o_ref[...] = acc

    return pl.pallas_call(
        body, name=name,
        out_shape=jax.ShapeDtypeStruct((8, NIDX), F32),
    )(db)


def _bias_layout(flat):
    b = flat.reshape(NBAND, 8, CH, CH).transpose(1, 0, 3, 2).reshape(4, 2, KB, CH)
    pair = b.transpose(0, 2, 1, 3).reshape(4, KB, 128)
    first = jnp.pad(pair, ((0, 0), (0, CH), (0, 0)), constant_values=NEG_INF)
    second = jnp.pad(pair, ((0, 0), (CH, 0), (0, 0)), constant_values=NEG_INF)
    return jnp.concatenate([first, second], axis=2)


def _bias_unlayout(dbt):
    b = dbt.reshape(4, NBAND, CH, 2, CH)
    return b.transpose(1, 0, 3, 4, 2).reshape(NBAND * 8, CH * CH)


UNIT = 2 * CH
BAND2 = KB + CH


def _pair_weights(xt):
    x = xt.astype(F32)
    row = lax.broadcasted_iota(jnp.int32, (128, UNIT), 0)
    low = lax.broadcasted_iota(jnp.int32, (128, UNIT), 1) < HD
    swapped = pltpu.roll(x, HD, 1)
    same = (row < HD) == low
    first = jnp.where(same, jnp.where(low, x, swapped), 0.0)
    second = jnp.where(same, jnp.where(low, swapped, x), 0.0)
    return jnp.concatenate([first, second], axis=1).astype(BF16)


def _pair_rows(x):
    low = lax.broadcasted_iota(jnp.int32, (CH, 128), 1) < HD
    zero = jnp.zeros((CH, 128), x.dtype)
    parts = []
    for c in range(2):
        xc = x[c * CH:(c + 1) * CH, :]
        parts += [jnp.where(low, xc, zero), jnp.where(low, zero, xc)]
    return jnp.concatenate(parts, axis=0)


def _unpair(raw):
    b0, b1 = raw[:, 0:128], raw[:, 128:256]
    row = lax.broadcasted_iota(jnp.int32, (128, 128), 0)
    low = lax.broadcasted_iota(jnp.int32, (128, 128), 1) < HD
    top = jnp.where(low, b0, pltpu.roll(b1, HD, 1))
    bottom = jnp.where(low, pltpu.roll(b0, HD, 1), b1)
    return jnp.where(row < HD, top, bottom).T


def _softmax_t(kb, qw, bias2, row0):
    s = jnp.dot(kb, qw, preferred_element_type=F32)
    valid = (row0 + lax.broadcasted_iota(jnp.int32, (BAND2, 256), 0)) >= PADR
    s = jnp.where(valid, s + bias2, NEG_INF)
    e = jnp.exp(s - jnp.max(s, axis=0, keepdims=True))
    return e * (1.0 / jnp.sum(e, axis=0, keepdims=True))


def _attn_fwd(name, kp, qt, vt, bias2):
    s = qt.shape[1]

    def body(k_ref, qt_ref, vt_ref, b_ref, o_ref):
        def unit(u, carry):
            r0 = pl.multiple_of(u * UNIT, UNIT)
            pt = _softmax_t(k_ref[pl.ds(r0, BAND2), :], _pair_weights(qt_ref[:, pl.ds(r0, UNIT)]), b_ref[...], r0)
            raw = jnp.dot(vt_ref[:, pl.ds(r0, BAND2)], pt.astype(BF16), preferred_element_type=F32)
            o_ref[pl.ds(r0, UNIT), :] = _unpair(raw).astype(BF16)
            return carry

        lax.fori_loop(0, s // UNIT, unit, 0, unroll=2)

    return pl.pallas_call(
        body, name=name, grid=(AW // 128,),
        in_specs=[pl.BlockSpec((s + PADR, 128), lambda h: (0, h)),
                  pl.BlockSpec((128, s), lambda h: (h, 0)),
                  pl.BlockSpec((128, s + PADR), lambda h: (h, 0)),
                  pl.BlockSpec((None, BAND2, 256), lambda h: (h, 0, 0))],
        out_specs=pl.BlockSpec((s, 128), lambda h: (0, h)),
        out_shape=jax.ShapeDtypeStruct((s, AW), BF16),
        compiler_params=_cp("parallel"),
    )(kp, qt, vt, bias2)


def _attn_bwd(name, q, qt, kp, kt, vp, bias2, do, dot):
    s = q.shape[0]

    def body(q_ref, qt_ref, k_ref, kt_ref, v_ref, b_ref, do_ref, dot_ref, dq_ref, dk_ref, dv_ref, db_ref):
        dk_ref[...] = jnp.zeros_like(dk_ref)
        dv_ref[...] = jnp.zeros_like(dv_ref)
        db_ref[...] = jnp.zeros_like(db_ref)

        def unit(u, carry):
            r0 = pl.multiple_of(u * UNIT, UNIT)
            rows, band = pl.ds(r0, UNIT), pl.ds(r0, BAND2)
            pt = _softmax_t(k_ref[band, :], _pair_weights(qt_ref[:, rows]), b_ref[...], r0)
            dpt = jnp.dot(v_ref[band, :], _pair_weights(dot_ref[:, rows]), preferred_element_type=F32)
            ds = pt * (dpt - jnp.sum(dpt * pt, axis=0, keepdims=True))
            db_ref[...] += ds[0:KB, 0:128] + ds[CH:BAND2, 128:256]
            dsb = ds.astype(BF16)
            dq_ref[rows, :] = _unpair(jnp.dot(kt_ref[:, band], dsb, preferred_element_type=F32))
            dk_ref[band, :] += jnp.dot(dsb, _pair_rows(q_ref[rows, :]), preferred_element_type=F32)
            dv_ref[band, :] += jnp.dot(pt.astype(BF16), _pair_rows(do_ref[rows, :]), preferred_element_type=F32)
            return carry

        lax.fori_loop(0, s // UNIT, unit, 0, unroll=2)

    row_q = pl.BlockSpec((s, 128), lambda h: (0, h))
    col_q = pl.BlockSpec((128, s), lambda h: (h, 0))
    row_k = pl.BlockSpec((s + PADR, 128), lambda h: (0, h))
    col_k = pl.BlockSpec((128, s + PADR), lambda h: (h, 0))
    return pl.pallas_call(
        body, name=name, grid=(AW // 128,),
        in_specs=[row_q, col_q, row_k, col_k, row_k,
                  pl.BlockSpec((None, BAND2, 256), lambda h: (h, 0, 0)), row_q, col_q],
        out_specs=[row_q, row_k, row_k, pl.BlockSpec((None, KB, 128), lambda h: (h, 0, 0))],
        out_shape=[jax.ShapeDtypeStruct((s, AW), F32),
                   jax.ShapeDtypeStruct((s + PADR, AW), F32),
                   jax.ShapeDtypeStruct((s + PADR, AW), F32),
                   jax.ShapeDtypeStruct((4, KB, 128), F32)],
        compiler_params=_cp("parallel"),
    )(q, qt, kp, kt, vp, bias2, do, dot)


def _rows_before(cur, prev, k):
    row = lax.broadcasted_iota(jnp.int32, cur.shape, 0)
    return jnp.where(row >= k, pltpu.roll(cur, k, 0), pltpu.roll(prev, k, 0))


def _rows_after(cur, nxt, k):
    n = cur.shape[0]
    row = lax.broadcasted_iota(jnp.int32, cur.shape, 0)
    return jnp.where(row < n - k, pltpu.roll(cur, n - k, 0), pltpu.roll(nxt, n - k, 0))


def _pool_window_lanes():
    lg = lax.broadcasted_iota(jnp.int32, (1, PWD), 1) // 64
    return lg, jnp.where(lg == 0, 2.0, jnp.where(lg == 1, 4.0, jnp.where(lg == 2, 8.0, 16.0))).astype(F32)


def _pool_mean_minus_token(u, up, row0):
    lg, wv = _pool_window_lanes()
    sums = []
    c, p = u, up
    for k in (1, 2, 4, 8):
        c2 = c + _rows_before(c, p, k)
        p = p + pltpu.roll(p, k, 0)
        c = c2
        sums.append(c)
    win = jnp.where(lg == 0, sums[0], jnp.where(lg == 1, sums[1], jnp.where(lg == 2, sums[2], sums[3])))
    pos1 = (row0 + lax.broadcasted_iota(jnp.int32, u.shape, 0) + 1).astype(F32)
    cnt = jnp.minimum(pos1, wv)
    return win / cnt - u, cnt


def _conv_taps(z, zp, w0, w1, w2):
    z1 = _rows_before(z, zp, 1)
    z2 = _rows_before(z, zp, 2)
    return (w0 * z2 + w1 * z1) + w2 * z, z1, z2


CP_TM = 512


def _convpool_fwd(name, p, o, cw, pwbd, ps):
    s = p.shape[0]
    tm = CP_TM
    nb = s // tm

    def body(gb_ref, gc_ref, hin_ref, u_ref, gcp_ref, hinp_ref, up_ref, o_ref, cw_ref, pw_ref, ps_ref, mix_ref):
        i = pl.program_id(0)
        has_prev = i > 0
        z = gc_ref[...] * hin_ref[...]
        zp = jnp.where(has_prev, gcp_ref[...] * hinp_ref[...], 0.0)
        y3, _, _ = _conv_taps(z, zp, cw_ref[0:1, :], cw_ref[1:2, :], cw_ref[2:3, :])
        m, _ = _pool_mean_minus_token(u_ref[...], jnp.where(has_prev, up_ref[...], 0.0), i * tm)
        yp = jnp.dot(m.astype(BF16), pw_ref[...].astype(BF16), preferred_element_type=F32) * ps_ref[...]
        mix_ref[:, 0:AW] = o_ref[...]
        mix_ref[:, AW:AW + CW] = (gb_ref[...] * y3).astype(BF16)
        mix_ref[:, AW + CW:D] = yp.astype(BF16)

    def cur(col):
        return pl.BlockSpec((tm, CW), lambda i: (i, col))

    def prev(col):
        return pl.BlockSpec((tm, CW), lambda i: (jnp.maximum(i - 1, 0), col))

    def whole(a):
        return pl.BlockSpec(a.shape, lambda i: (0,) * a.ndim)

    return pl.pallas_call(
        body, name=name, grid=(nb,),
        in_specs=[cur(6), cur(7), cur(8), cur(9), prev(7), prev(8), prev(9),
                  pl.BlockSpec((tm, AW), lambda i: (i, 0)), whole(cw), whole(pwbd), whole(ps)],
        out_specs=pl.BlockSpec((tm, D), lambda i: (i, 0)),
        out_shape=jax.ShapeDtypeStruct((s, D), BF16),
        compiler_params=_cp("parallel"),
    )(p, p, p, p, p, p, p, o, cw, pwbd, ps)


def _convpool_bwd(name, p, dmix, cw, pwbd, ps):
    s = p.shape[0]
    tm = CP_TM
    nb = s // tm

    def body(gb_ref, gc_ref, hin_ref, u_ref, gcp_ref, hinp_ref, up_ref, gbn_ref, dyc_ref, dyp_ref, dycn_ref, dypn_ref,
             cw_ref, pw_ref, ps_ref, dcp_ref, dw0_ref, dw1_ref, dw2_ref, dps_ref, dpw_ref):
        i = pl.program_id(0)
        has_prev = i > 0
        has_next = i < nb - 1
        w0, w1, w2 = cw_ref[0:1, :], cw_ref[1:2, :], cw_ref[2:3, :]
        gb, gc, hin = gb_ref[...], gc_ref[...], hin_ref[...]
        dyc = dyc_ref[...]
        z = gc * hin
        zp = jnp.where(has_prev, gcp_ref[...] * hinp_ref[...], 0.0)
        y3, z1, z2 = _conv_taps(z, zp, w0, w1, w2)
        dy3 = dyc * gb
        dy3n = jnp.where(has_next, dycn_ref[...] * gbn_ref[...], 0.0)
        dz = w2 * dy3 + w1 * _rows_after(dy3, dy3n, 1) + w0 * _rows_after(dy3, dy3n, 2)
        pw = pw_ref[...].astype(BF16)
        psv = ps_ref[...]
        m, cnt = _pool_mean_minus_token(u_ref[...], jnp.where(has_prev, up_ref[...], 0.0), i * tm)
        mb = m.astype(BF16)
        dyp = dyp_ref[...]
        dmp = (dyp * psv).astype(BF16)
        dmpn = jnp.where(has_next, dypn_ref[...] * psv, 0.0).astype(BF16)
        nt = (((1,), (1,)), ((), ()))
        dm = lax.dot_general(dmp, pw, nt, preferred_element_type=F32)
        dmn = lax.dot_general(dmpn, pw, nt, preferred_element_type=F32)
        lg, wv = _pool_window_lanes()
        cc, cn = dm / cnt, dmn / wv
        sums = []
        for k in (1, 2, 4, 8):
            c2 = cc + _rows_after(cc, cn, k)
            cn = cn + pltpu.roll(cn, tm - k, 0)
            cc = c2
            sums.append(cc)
        du = jnp.where(lg == 0, sums[0], jnp.where(lg == 1, sums[1], jnp.where(lg == 2, sums[2], sums[3]))) - dm
        dcp_ref[:, 0:CW] = (dyc * y3).astype(BF16)
        dcp_ref[:, CW:2 * CW] = (dz * hin).astype(BF16)
        dcp_ref[:, 2 * CW:3 * CW] = (dz * gc).astype(BF16)
        dcp_ref[:, 3 * CW:4 * CW] = du.astype(BF16)
        parts = (jnp.sum(dy3 * z2, axis=0, keepdims=True),
                 jnp.sum(dy3 * z1, axis=0, keepdims=True),
                 jnp.sum(dy3 * z, axis=0, keepdims=True),
                 jnp.sum(dyp * jnp.dot(mb, pw, preferred_element_type=F32), axis=0, keepdims=True),
                 lax.dot_general(mb, dmp, (((0,), (0,)), ((), ())), preferred_element_type=F32))
        accs = (dw0_ref, dw1_ref, dw2_ref, dps_ref, dpw_ref)

        @pl.when(i == 0)
        def _():
            for a, v in zip(accs, parts):
                a[...] = v

        @pl.when(i > 0)
        def _():
            for a, v in zip(accs, parts):
                a[...] += v

    def cur(col):
        return pl.BlockSpec((tm, CW), lambda i: (i, col))

    def prev(col):
        return pl.BlockSpec((tm, CW), lambda i: (jnp.maximum(i - 1, 0), col))

    def nxt(col):
        return pl.BlockSpec((tm, CW), lambda i: (jnp.minimum(i + 1, nb - 1), col))

    def whole(shape):
        return pl.BlockSpec(shape, lambda i: (0,) * len(shape))

    row = jax.ShapeDtypeStruct((1, CW), F32)
    return pl.pallas_call(
        body, name=name, grid=(nb,),
        in_specs=[cur(6), cur(7), cur(8), cur(9), prev(7), prev(8), prev(9), nxt(6),
                  cur(0), cur(1), nxt(0), nxt(1), whole(cw.shape), whole(pwbd.shape), whole(ps.shape)],
        out_specs=[pl.BlockSpec((tm, D), lambda i: (i, 0)), whole((1, CW)), whole((1, CW)), whole((1, CW)),
                   whole((1, PWD)), whole((PWD, PWD))],
        out_shape=[jax.ShapeDtypeStruct((s, D), BF16), row, row, row, row,
                   jax.ShapeDtypeStruct((PWD, PWD), F32)],
        compiler_params=_cp("arbitrary"),
    )(p, p, p, p, p, p, p, p, dmix, dmix, dmix, dmix, cw, pwbd, ps)


def _qkv_bwd(name, p, dq, dkp, dvp, dcp, qg, kg):
    s = p.shape[0]
    tm = 256
    off = PADR // tm

    def body(pq_ref, pk_ref, dq_ref, dk_ref, dv_ref, dcp_ref, qg_ref, kg_ref, dp_ref, dqg_ref, dkg_ref):
        i = pl.program_id(0)
        hm = _head_mean_matrix()

        def nrm_bwd(x, g, dy):
            r = lax.rsqrt(_head_mean(x * x, hm) + EPS)
            xn = x * r
            dxn = dy * g
            dx = r * (dxn - xn * _head_mean(dxn * xn, hm))
            dg = jnp.sum(dy * xn, axis=0, keepdims=True)
            dg = (dg[:, 0:128] + dg[:, 128:256]) + (dg[:, 256:384] + dg[:, 384:512])
            return dx, dg + pltpu.roll(dg, HD, 1)

        dxq, dgq = nrm_bwd(pq_ref[...], qg_ref[...], dq_ref[...] * 0.125)
        dxk, dgk = nrm_bwd(pk_ref[...], kg_ref[...], dk_ref[...])
        dp_ref[:, 0:AW] = dxq.astype(BF16)
        dp_ref[:, AW:2 * AW] = dxk.astype(BF16)
        dp_ref[:, 2 * AW:3 * AW] = dv_ref[...].astype(BF16)
        dp_ref[:, 3 * AW:DIN] = dcp_ref[...]

        @pl.when(i == 0)
        def _():
            dqg_ref[...] = dgq
            dkg_ref[...] = dgk

        @pl.when(i > 0)
        def _():
            dqg_ref[...] += dgq
            dkg_ref[...] += dgk

    gspec = pl.BlockSpec((1, AW), lambda i: (0, 0))
    gout = pl.BlockSpec((1, 128), lambda i: (0, 0))
    return pl.pallas_call(
        body, name=name, grid=(s // tm,),
        in_specs=[pl.BlockSpec((tm, AW), lambda i: (i, 0)), pl.BlockSpec((tm, AW), lambda i: (i, 1)),
                  pl.BlockSpec((tm, AW), lambda i: (i, 0)),
                  pl.BlockSpec((tm, AW), lambda i: (i + off, 0)),
                  pl.BlockSpec((tm, AW), lambda i: (i + off, 0)),
                  pl.BlockSpec((tm, D), lambda i: (i, 0)), gspec, gspec],
        out_specs=[pl.BlockSpec((tm, DIN), lambda i: (i, 0)), gout, gout],
        out_shape=[jax.ShapeDtypeStruct((s, DIN), BF16), jax.ShapeDtypeStruct((1, 128), F32),
                   jax.ShapeDtypeStruct((1, 128), F32)],
        compiler_params=_cp("arbitrary"),
    )(p, p, dq, dkp, dvp, dcp, qg, kg)


def _loss_grad(name, y, t):
    s = y.shape[0]
    tm = 512

    def body(y_ref, t_ref, dy_ref, dyb_ref, l_ref):
        i = pl.program_id(0)
        e = y_ref[...] - t_ref[...]
        dy = e * (1.0 / D)
        dy_ref[...] = dy
        dyb_ref[...] = dy.astype(BF16)
        part = 0.5 * jnp.sum(jnp.mean(e * e, axis=-1, keepdims=True), axis=0, keepdims=True)

        @pl.when(i == 0)
        def _():
            l_ref[...] = part

        @pl.when(i > 0)
        def _():
            l_ref[...] += part

    blk = pl.BlockSpec((tm, D), lambda i: (i, 0))
    return pl.pallas_call(
        body, name=name, grid=(s // tm,),
        in_specs=[blk, blk],
        out_specs=[blk, blk, pl.BlockSpec((1, 1), lambda i: (0, 0))],
        out_shape=[jax.ShapeDtypeStruct((s, D), F32), jax.ShapeDtypeStruct((s, D), BF16),
                   jax.ShapeDtypeStruct((1, 1), F32)],
        compiler_params=_cp("arbitrary"),
    )(y, t)


def _mm_nt_relu(name, dxb, w, l, a):
    s = dxb.shape[0]
    tm = 256

    def body(d_ref, w_ref, a_ref, o_ref):
        df = lax.dot_general(d_ref[...], w_ref[...], (((1,), (1,)), ((), ())), preferred_element_type=F32)
        o_ref[...] = (df * (2.0 * jnp.maximum(a_ref[...].astype(F32), 0.0))).astype(BF16)

    return pl.pallas_call(
        body, name=name, grid=(s // tm,),
        in_specs=[pl.BlockSpec((tm, D), lambda i: (i, 0)),
                  pl.BlockSpec((None, DFF, D), lambda i: (l, 0, 0)),
                  pl.BlockSpec((tm, DFF), lambda i: (i, 0))],
        out_specs=pl.BlockSpec((tm, DFF), lambda i: (i, 0)),
        out_shape=jax.ShapeDtypeStruct((s, DFF), BF16),
        compiler_params=_cp("parallel"),
    )(dxb, w, a)


def _proj_out_bwd(name, dxb, w, l):
    s = dxb.shape[0]
    tm = 512

    def body(d_ref, w_ref, do_ref, dot_ref, dcp_ref):
        d = d_ref[...]
        wa, wc = w_ref[0:AW, :], w_ref[AW:D, :]
        do_ref[...] = lax.dot_general(d, wa, NT_DIMS, preferred_element_type=F32).astype(BF16)
        dot_ref[...] = lax.dot_general(wa, d, NT_DIMS, preferred_element_type=F32).astype(BF16)
        dcp_ref[...] = lax.dot_general(d, wc, NT_DIMS, preferred_element_type=F32)

    return pl.pallas_call(
        body, name=name, grid=(s // tm,),
        in_specs=[pl.BlockSpec((tm, D), lambda i: (i, 0)),
                  pl.BlockSpec((None, D, D), lambda i: (l, 0, 0))],
        out_specs=[pl.BlockSpec((tm, AW), lambda i: (i, 0)), pl.BlockSpec((AW, tm), lambda i: (0, i)),
                   pl.BlockSpec((tm, D - AW), lambda i: (i, 0))],
        out_shape=[jax.ShapeDtypeStruct((s, AW), BF16), jax.ShapeDtypeStruct((AW, s), BF16),
                   jax.ShapeDtypeStruct((s, D - AW), F32)],
        compiler_params=_cp("parallel"),
    )(dxb, w)


def _mm_nt_normbwd(name, gy, w, l, x, g, dres, dep):
    s, k = gy.shape
    tm = 256

    def body(gy_ref, w_ref, x_ref, g_ref, dr_ref, dep_ref, dx_ref, dxb_ref, dg_ref):
        del dep_ref
        i = pl.program_id(0)
        dh = lax.dot_general(gy_ref[...], w_ref[...], (((1,), (1,)), ((), ())), preferred_element_type=F32)
        xv = x_ref[...]
        r = _inv_rms(xv)
        xn = xv * r
        dxn = dh * g_ref[...]
        dx = r * (dxn - xn * jnp.mean(dxn * xn, axis=-1, keepdims=True)) + dr_ref[...]
        dx_ref[...] = dx
        dxb_ref[...] = dx.astype(BF16)
        part = jnp.sum(dh * xn, axis=0, keepdims=True)

        @pl.when(i == 0)
        def _():
            dg_ref[...] = part

        @pl.when(i > 0)
        def _():
            dg_ref[...] += part

    blk = pl.BlockSpec((tm, D), lambda i: (i, 0))
    vec = pl.BlockSpec((1, D), lambda i: (0, 0))
    return pl.pallas_call(
        body, name=name, grid=(s // tm,),
        in_specs=[pl.BlockSpec((tm, k), lambda i: (i, 0)),
                  pl.BlockSpec((None, D, k), lambda i: (l, 0, 0)), blk, vec, blk, ANY],
        out_specs=[blk, blk, vec],
        out_shape=[jax.ShapeDtypeStruct((s, D), F32), jax.ShapeDtypeStruct((s, D), BF16),
                   jax.ShapeDtypeStruct((1, D), F32)],
        compiler_params=_cp("arbitrary"),
    )(gy, w, x, g, dres, dep)


def _mm_tn(name, a, b, tma, tnb, relu2=False):
    s, m = a.shape
    n = b.shape[1]

    def body(a_ref, b_ref, o_ref):
        av = _relu2(a_ref[...]) if relu2 else a_ref[...]
        o_ref[...] = lax.dot_general(av, b_ref[...], (((0,), (0,)), ((), ())),
                                     preferred_element_type=F32).astype(BF16)

    return pl.pallas_call(
        body, name=name, grid=(m // tma, n // tnb),
        in_specs=[pl.BlockSpec((s, tma), lambda i, j: (0, i)),
                  pl.BlockSpec((s, tnb), lambda i, j: (0, j))],
        out_specs=pl.BlockSpec((tma, tnb), lambda i, j: (i, j)),
        out_shape=jax.ShapeDtypeStruct((m, n), BF16),
        compiler_params=_cp("parallel", "parallel"),
    )(a, b)


def _adamw_math(gv, wv, mv, vv):
    mn = ADAM_B1 * mv + (1.0 - ADAM_B1) * gv
    vn = ADAM_B2 * vv + (1.0 - ADAM_B2) * jnp.square(gv)
    m_hat = mn / (1.0 - ADAM_B1 ** ADAM_STEP)
    v_hat = vn / (1.0 - ADAM_B2 ** ADAM_STEP)
    return gv, -ADAM_LR * (m_hat / (jnp.sqrt(v_hat) + ADAM_EPS) + ADAM_WD * wv), mn, vn


def _adamw(name, g, w, m, v):
    r, c = g.shape
    tm = 256 if r % 256 == 0 else r

    def body(g_ref, w_ref, m_ref, v_ref, go_ref, d_ref, mo_ref, vo_ref):
        go_ref[...], d_ref[...], mo_ref[...], vo_ref[...] = _adamw_math(g_ref[...], w_ref[...], m_ref[...], v_ref[...])

    blk = pl.BlockSpec((tm, c), lambda i: (i, 0))
    return pl.pallas_call(
        body, name=name, grid=(r // tm,),
        in_specs=[blk] * 4, out_specs=[blk] * 4,
        out_shape=[jax.ShapeDtypeStruct((r, c), F32)] * 4,
        compiler_params=_cp("parallel"),
    )(g, w, m, v)


def _place():
    x, y, c = lax.axis_index("x"), lax.axis_index("y"), lax.axis_index("c")
    chips = [(1 - x, y), (x, 1 - y), (1 - x, 1 - y)]
    return x, y, c, chips


BLOCK_AXIS = (2, 1, 2, 1)
LARGE_DIMS = ((D, DIN), (D, D), (D, DFF), (DFF, D))


def _full_shape(t, layers, dtype):
    r, c = LARGE_DIMS[t]
    return jax.ShapeDtypeStruct((layers, r, c), dtype)


def _cast_into_full(name, t, shard, b1, dep):
    _, r, c = shard.shape
    tm = min(256, r)
    if BLOCK_AXIS[t] == 1:
        out_spec = pl.BlockSpec((None, tm, c), lambda l, i, br: (l, br[0] * (r // tm) + i, 0))
    else:
        out_spec = pl.BlockSpec((None, tm, c), lambda l, i, br: (l, i, br[0]))

    def body(b_ref, x_ref, dep_ref, o_ref):
        del b_ref, dep_ref
        o_ref[...] = x_ref[...].astype(BF16)

    return pl.pallas_call(
        body, name=name,
        grid_spec=pltpu.PrefetchScalarGridSpec(
            num_scalar_prefetch=1, grid=(DEPTH, r // tm),
            in_specs=[pl.BlockSpec((None, tm, c), lambda l, i, br: (l, i, 0)), ANY],
            out_specs=out_spec),
        out_shape=_full_shape(t, DEPTH, BF16),
        compiler_params=_cp("parallel", "parallel"),
    )(b1, shard, dep)


HBM = pl.BlockSpec(memory_space=pltpu.HBM)
SEM = pl.BlockSpec(memory_space=pltpu.SEMAPHORE)
DATAFLOW = pltpu.SideEffectType.DATAFLOW_SIDE_EFFECTING


def _half(ref, l, t, b, c):
    r, cols = LARGE_DIMS[t]
    if BLOCK_AXIS[t] == 1:
        n = r // 8
        return ref.at[l, pl.ds(pl.multiple_of(b * (2 * n) + c * n, 16), n), :]
    n, w = r // 2, cols // 4
    return ref.at[l, pl.ds(pl.multiple_of(c * n, 16), n), pl.ds(pl.multiple_of(b * w, 128), w)]


def _gather_start(name, layers, fulls):
    def body(*refs):
        f_refs, sems = refs[4:8], refs[8:8 + 2 * len(layers)]
        x, y, c, chips = _place()
        for i, l in enumerate(layers):
            for t in range(4):
                own = _half(f_refs[t], l, t, 2 * x + y, c)
                for j, (cx, cy) in enumerate(chips):
                    pltpu.make_async_remote_copy(src_ref=own, dst_ref=own, send_sem=sems[2 * i].at[3 * t + j],
                                                 recv_sem=sems[2 * i + 1].at[3 * t + j], device_id=(cx, cy, c),
                                                 device_id_type=MESH).start()

    outs = pl.pallas_call(
        body, name=name,
        in_specs=[HBM] * 4, out_specs=[HBM] * 4 + [SEM] * (2 * len(layers)),
        out_shape=[pltpu.HBM(s.shape, s.dtype) for s in (_full_shape(t, DEPTH, BF16) for t in range(4))]
        + [pltpu.SemaphoreType.DMA((12,))] * (2 * len(layers)),
        input_output_aliases={t: t for t in range(4)},
        compiler_params=pltpu.CompilerParams(has_side_effects=DATAFLOW),
    )(*[pltpu.with_memory_space_constraint(f, pltpu.HBM) for f in fulls])
    return outs[0:4], {l: (outs[4 + 2 * i], outs[5 + 2 * i]) for i, l in enumerate(layers)}


def _gather_wait(l, fulls, sems, after):
    def body(*refs):
        send_sems, recv_sems, f_refs = refs[4], refs[5], refs[7:11]
        x, y, c, chips = _place()
        for t in range(4):
            own = _half(f_refs[t], l, t, 2 * x + y, c)
            for j, (cx, cy) in enumerate(chips):
                landed = _half(f_refs[t], l, t, 2 * cx + cy, c)
                pltpu.make_async_remote_copy(src_ref=own, dst_ref=landed, send_sem=send_sems.at[3 * t + j],
                                             recv_sem=recv_sems.at[3 * t + j], device_id=(cx, cy, c),
                                             device_id_type=MESH).wait()

    return pl.pallas_call(
        body, name=f"gather_wait_{l}",
        in_specs=[HBM] * 4 + [SEM, SEM, ANY], out_specs=[HBM] * 4,
        out_shape=[pltpu.HBM(s.shape, s.dtype) for s in (_full_shape(t, DEPTH, BF16) for t in range(4))],
        input_output_aliases={t: t for t in range(4)},
        compiler_params=pltpu.CompilerParams(has_side_effects=DATAFLOW),
    )(*fulls, sems[0], sems[1], after)


def _pass_on(l, fulls):
    def body(*refs):
        f_refs, send_sems, recv_sems = refs[4:8], refs[8], refs[9]
        x, y, c, chips = _place()

        def copy(t, j, half):
            cx, cy = chips[j]
            part = _half(f_refs[t], l, t, 2 * cx + cy, half)
            return pltpu.make_async_remote_copy(src_ref=part, dst_ref=part, send_sem=send_sems.at[3 * t + j],
                                                recv_sem=recv_sems.at[3 * t + j], device_id=(x, y, 1 - c),
                                                device_id_type=MESH)

        for t in range(4):
            for j in range(3):
                copy(t, j, c).start()
        for t in range(4):
            for j in range(3):
                copy(t, j, 1 - c).wait_recv()
                copy(t, j, c).wait_send()

    return pl.pallas_call(
        body, name=f"pass_on_{l}",
        in_specs=[ANY] * 4, out_specs=[ANY] * 4,
        out_shape=[_full_shape(t, DEPTH, BF16) for t in range(4)],
        input_output_aliases={t: t for t in range(4)},
        scratch_shapes=[pltpu.SemaphoreType.DMA((12,)), pltpu.SemaphoreType.DMA((12,))],
    )(*fulls)


def _block2d(ref, t, b):
    r, cols = LARGE_DIMS[t]
    if BLOCK_AXIS[t] == 1:
        return ref.at[pl.ds(pl.multiple_of(b * (r // 4), 16), r // 4), :]
    return ref.at[:, pl.ds(pl.multiple_of(b * (cols // 4), 128), cols // 4)]


def _block_dims(t):
    r, cols = LARGE_DIMS[t]
    return (r // 4, cols) if BLOCK_AXIS[t] == 1 else (r, cols // 4)


def _reduce_copies(ts, g_refs, r_refs, send_sems, recv_sems):
    _, _, c, chips = _place()
    return [pltpu.make_async_remote_copy(src_ref=_block2d(g_refs[i], t, 2 * cx + cy), dst_ref=r_refs[i].at[j],
                                         send_sem=send_sems.at[3 * i + j], recv_sem=recv_sems.at[3 * i + j],
                                         device_id=(cx, cy, c), device_id_type=MESH)
            for i, t in enumerate(ts) for j, (cx, cy) in enumerate(chips)]


def _reduce_start(name, ts, grads):
    n = len(ts)

    def body(*refs):
        for cp in _reduce_copies(ts, refs[n:2 * n], refs[2 * n:3 * n], refs[3 * n], refs[3 * n + 1]):
            cp.start()
        refs[3 * n + 2][...] = jnp.zeros((8, 128), F32)

    outs = pl.pallas_call(
        body, name=name,
        in_specs=[HBM] * n,
        out_specs=[HBM] * (2 * n) + [SEM, SEM, pl.BlockSpec(memory_space=pltpu.VMEM)],
        out_shape=[pltpu.HBM(g.shape, BF16) for g in grads]
        + [pltpu.HBM((3,) + _block_dims(t), BF16) for t in ts]
        + [pltpu.SemaphoreType.DMA((3 * n,)), pltpu.SemaphoreType.DMA((3 * n,)), jax.ShapeDtypeStruct((8, 128), F32)],
        input_output_aliases={i: i for i in range(n)},
        compiler_params=pltpu.CompilerParams(has_side_effects=DATAFLOW),
    )(*[pltpu.with_memory_space_constraint(g, pltpu.HBM) for g in grads])
    return outs[0:n], outs[n:2 * n], (outs[2 * n], outs[2 * n + 1]), outs[2 * n + 2]


def _reduce_wait(name, ts, grads, landing, sems, afters):
    n = len(ts)
    first_out = 2 * n + 2 + len(afters)

    def body(*refs):
        for cp in _reduce_copies(ts, refs[first_out:first_out + n], refs[first_out + n:first_out + 2 * n],
                                 refs[2 * n], refs[2 * n + 1]):
            cp.wait()

    outs = pl.pallas_call(
        body, name=name,
        in_specs=[HBM] * (2 * n) + [SEM, SEM] + [ANY] * len(afters), out_specs=[HBM] * (2 * n),
        out_shape=[pltpu.HBM(g.shape, BF16) for g in grads] + [pltpu.HBM(r.shape, BF16) for r in landing],
        input_output_aliases={i: i for i in range(2 * n)},
        compiler_params=pltpu.CompilerParams(has_side_effects=DATAFLOW),
    )(*grads, *landing, sems[0], sems[1], *afters)
    return outs[0:n], outs[n:2 * n]


def _add4(name, t, own, landed, b1):
    rb, cb = _block_dims(t)
    tm = min(256, rb)
    if BLOCK_AXIS[t] == 1:
        own_spec = pl.BlockSpec((tm, cb), lambda i, br: (br[0] * (rb // tm) + i, 0))
    else:
        own_spec = pl.BlockSpec((tm, cb), lambda i, br: (i, br[0]))

    def body(b_ref, o_ref, r0_ref, r1_ref, r2_ref, s_ref):
        del b_ref
        s_ref[...] = ((o_ref[...].astype(F32) + r0_ref[...].astype(F32))
                      + (r1_ref[...].astype(F32) + r2_ref[...].astype(F32))).astype(BF16)

    def got(j):
        return pl.BlockSpec((None, tm, cb), lambda i, br: (j, i, 0))

    return pl.pallas_call(
        body, name=name,
        grid_spec=pltpu.PrefetchScalarGridSpec(
            num_scalar_prefetch=1, grid=(rb // tm,),
            in_specs=[own_spec, got(0), got(1), got(2)],
            out_specs=pl.BlockSpec((tm, cb), lambda i, br: (i, 0))),
        out_shape=jax.ShapeDtypeStruct((rb, cb), BF16),
        compiler_params=_cp("parallel"),
    )(b1, own, landed, landed, landed)


def _swap_sib(name, sums):
    def body(*refs):
        s_refs, t_refs, send_sems, recv_sems = refs[0:4], refs[4:8], refs[8], refs[9]
        x, y, c, _ = _place()
        cps = [pltpu.make_async_remote_copy(src_ref=s_refs[t], dst_ref=t_refs[t], send_sem=send_sems.at[t],
                                            recv_sem=recv_sems.at[t], device_id=(x, y, 1 - c), device_id_type=MESH)
               for t in range(4)]
        for cp in cps:
            cp.start()
        for cp in cps:
            cp.wait()

    return pl.pallas_call(
        body, name=name,
        in_specs=[ANY] * 4, out_specs=[ANY] * 4,
        out_shape=[jax.ShapeDtypeStruct(s.shape, BF16) for s in sums],
        scratch_shapes=[pltpu.SemaphoreType.DMA((4,)), pltpu.SemaphoreType.DMA((4,))],
    )(*sums)


def _adamw_pair(name, l, s_own, s_sib, w, m, v, outs):
    rb, cb = s_own.shape
    tm = min(256, rb)

    def body(a_ref, b_ref, w_ref, m_ref, v_ref, g0, d0, m0, v0, go_ref, d_ref, mo_ref, vo_ref):
        del g0, d0, m0, v0
        gv = a_ref[...].astype(F32) + b_ref[...].astype(F32)
        go_ref[...], d_ref[...], mo_ref[...], vo_ref[...] = _adamw_math(gv, w_ref[...], m_ref[...], v_ref[...])

    part = pl.BlockSpec((tm, cb), lambda i: (i, 0))
    layer = pl.BlockSpec((None, tm, cb), lambda i: (l, i, 0))
    return pl.pallas_call(
        body, name=name, grid=(rb // tm,),
        in_specs=[part, part, layer, layer, layer] + [ANY] * 4,
        out_specs=[layer] * 4,
        out_shape=[jax.ShapeDtypeStruct((DEPTH, rb, cb), F32)] * 4,
        input_output_aliases={5 + i: i for i in range(4)},
        compiler_params=_cp("parallel"),
    )(s_own, s_sib, w, m, v, *outs)


def _all_gather8(name, v, dep):
    m_per, n = v.shape

    def body(v_ref, dep_ref, out_ref, send_sems, recv_sems, local_sem):
        del dep_ref
        x, y, c, chips = _place()
        me, sib = (x, y, c), (x, y, 1 - c)

        def rows(px, py, pc):
            return out_ref.at[pl.ds((4 * px + 2 * py + pc) * m_per, m_per), :]

        def copy(k, block, to, src=None):
            return pltpu.make_async_remote_copy(
                src_ref=rows(*block) if src is None else src, dst_ref=rows(*block),
                send_sem=send_sems.at[k], recv_sem=recv_sems.at[k], device_id=to, device_id_type=MESH)

        mine = pltpu.make_async_copy(v_ref, rows(*me), local_sem)
        mine.start()
        first = [copy(0, me, sib, src=v_ref)]
        first += [copy(1 + j, me, (*chip, c), src=v_ref) for j, chip in enumerate(chips)]
        for cp in first:
            cp.start()
        passed = [copy(4 + j, (*chip, c), sib) for j, chip in enumerate(chips)]
        for j, chip in enumerate(chips):
            copy(1 + j, (*chip, c), me).wait_recv()
            passed[j].start()
        copy(0, sib, me).wait_recv()
        for j, chip in enumerate(chips):
            copy(4 + j, (*chip, 1 - c), me).wait_recv()
        for cp in first + passed:
            cp.wait_send()
        mine.wait()

    return pl.pallas_call(
        body, name=name,
        out_shape=jax.ShapeDtypeStruct((8 * m_per, n), v.dtype),
        in_specs=[pl.BlockSpec(memory_space=pltpu.VMEM), ANY],
        out_specs=pl.BlockSpec(memory_space=pltpu.VMEM),
        scratch_shapes=[pltpu.SemaphoreType.DMA((7,)), pltpu.SemaphoreType.DMA((7,)), pltpu.SemaphoreType.DMA],
    )(v, dep)


def _sum8(name, g):
    def body(g_ref, o_ref):
        acc = g_ref[0]
        for d in range(1, 8):
            acc = acc + g_ref[d]
        o_ref[...] = acc

    return pl.pallas_call(body, name=name, out_shape=jax.ShapeDtypeStruct(g.shape[1:], F32))(g)


def _pack(parts):
    flat = []
    for a in parts:
        a = a.reshape(-1)
        flat.append(jnp.pad(a, (0, (-a.shape[0]) % 128)))
    cat = jnp.concatenate(flat)
    cat = jnp.pad(cat, (0, (-cat.shape[0]) % 1024))
    return cat.reshape(-1, 128)


def _unpack(packed, shapes):
    flat = packed.reshape(-1)
    out, at = [], 0
    for shp in shapes:
        n = 1
        for d in shp:
            n *= d
        out.append(flat[at:at + n].reshape(shp))
        at += n + (-n) % 128
    return out


def _local_step(x, target, layer_weights, on_grads, small):
    saved = []
    xin = x
    h = _rmsnorm("norm_first", x, small["norm1_g"][0:1])
    for l in range(DEPTH):
        w_in, w_out, w_1, w_2 = layer_weights(l, xin)
        qg = jnp.tile(small["q_norm_g"][l], 8)[None]
        kg = jnp.tile(small["k_norm_g"][l], 8)[None]
        rb = jnp.pad(small["rel_bias"][l], ((0, 0), (0, NIDX - 257)))
        bias = _bias_layout(_bias_expand(f"bias_expand_{l}", rb))
        cw = small["conv_w"][l]
        pwbd = jax.scipy.linalg.block_diag(*[small["pool_w"][l, g] for g in range(4)])
        ps = small["pool_scale"][l][None]
        p = _mm_nn(f"proj_in_{l}", h, w_in, l, F32)
        q, qt, kp, kt, vp, vt = _qkv(f"qkv_{l}", p, qg, kg)
        o = _attn_fwd(f"attn_fwd_{l}", kp, qt, vt, bias)
        mix = _convpool_fwd(f"convpool_fwd_{l}", p, o, cw, pwbd, ps)
        x1, h2 = _mm_res_norm(f"proj_out_{l}", mix, w_out, l, xin, small["norm2_g"][l:l + 1])
        a = _mm_nn(f"mlp1_{l}", h2, w_1, l, BF16)
        gnext = small["norm1_g"][(l + 1) % DEPTH][None]
        x2, hnext = _mm_res_norm(f"mlp2_{l}", a, w_2, l, x1, gnext, relu2=True)
        saved.append(dict(xin=xin, h=h, p=p, q=q, qt=qt, kp=kp, kt=kt, vp=vp, bias=bias, mix=mix, x1=x1, h2=h2, a=a,
                          qg=qg, kg=kg, cw=cw, pwbd=pwbd, ps=ps))
        xin, h = x2, hnext

    dx, dxb, loss = _loss_grad("loss_grad", xin, target)
    gs = {k: [None] * DEPTH for k in ("norm1_g", "q_norm_g", "k_norm_g", "rel_bias", "conv_w", "pool_w",
                                      "pool_scale", "norm2_g")}
    for l in reversed(range(DEPTH)):
        sv = saved[l]
        da = _mm_nt_relu(f"mlp2_bwd_{l}", dxb, w_2, l, sv["a"])
        g_2 = _mm_tn(f"mlp2_wgrad_{l}", sv["a"], dxb, 512, 1024, relu2=True)
        g_1 = _mm_tn(f"mlp1_wgrad_{l}", sv["h2"], da, 1024, 512)
        dep = on_grads(l, (2, 3), (g_1, g_2))
        dx1, dx1b, dg2 = _mm_nt_normbwd(f"mlp1_bwd_{l}", da, w_1, l, sv["x1"], small["norm2_g"][l:l + 1], dx, dep)
        do, dot, dmix = _proj_out_bwd(f"proj_out_bwd_{l}", dx1b, w_out, l)
        g_out = _mm_tn(f"proj_out_wgrad_{l}", sv["mix"], dx1b, 512, 1024)
        dcp, dw0, dw1, dw2, dps, dpw = _convpool_bwd(f"convpool_bwd_{l}", sv["p"], dmix, sv["cw"], sv["pwbd"], sv["ps"])
        dq, dkp, dvp, db = _attn_bwd(f"attn_bwd_{l}", sv["q"], sv["qt"], sv["kp"], sv["kt"], sv["vp"], sv["bias"],
                                     do, dot)
        drb = _bias_reduce(f"bias_reduce_{l}", _bias_unlayout(db))
        dp, dqg, dkg = _qkv_bwd(f"qkv_bwd_{l}", sv["p"], dq, dkp, dvp, dcp, sv["qg"], sv["kg"])
        g_in = _mm_tn(f"proj_in_wgrad_{l}", sv["h"], dp, 1024, 640)
        dep = on_grads(l, (0, 1), (g_in, g_out))
        dx, dxb, dg1 = _mm_nt_normbwd(f"proj_in_bwd_{l}", dp, w_in, l, sv["xin"], small["norm1_g"][l:l + 1], dx1, dep)
        gs["norm1_g"][l] = dg1[0]
        gs["q_norm_g"][l] = dqg[0, :HD]
        gs["k_norm_g"][l] = dkg[0, :HD]
        gs["rel_bias"][l] = drb[:, :257]
        gs["conv_w"][l] = jnp.concatenate([dw0, dw1, dw2], axis=0)
        gs["pool_w"][l] = jnp.stack([dpw[g * 64:(g + 1) * 64, g * 64:(g + 1) * 64] for g in range(4)])
        gs["pool_scale"][l] = dps[0]
        gs["norm2_g"][l] = dg2[0]
    gsmall = {k: jnp.stack(v) for k, v in gs.items()}
    return loss, dx, gsmall


SMALL = ("norm1_g", "q_norm_g", "k_norm_g", "rel_bias", "conv_w", "pool_w", "pool_scale", "norm2_g")
LARGE = ("w_in", "w_out", "w_mlp1", "w_mlp2")


def kernel(x, norm1_g, w_in, q_norm_g, k_norm_g, rel_bias, conv_w, pool_w, pool_scale, w_out, norm2_g, w_mlp1, w_mlp2, loss_target, m_norm1_g, m_w_in, m_q_norm_g, m_k_norm_g, m_rel_bias, m_conv_w, m_pool_w, m_pool_scale, m_w_out, m_norm2_g, m_w_mlp1, m_w_mlp2, v_norm1_g, v_w_in, v_q_norm_g, v_k_norm_g, v_rel_bias, v_conv_w, v_pool_w, v_pool_scale, v_w_out, v_norm2_g, v_w_mlp1, v_w_mlp2):
    w = dict(norm1_g=norm1_g, w_in=w_in, q_norm_g=q_norm_g, k_norm_g=k_norm_g, rel_bias=rel_bias, conv_w=conv_w,
             pool_w=pool_w, pool_scale=pool_scale, w_out=w_out, norm2_g=norm2_g, w_mlp1=w_mlp1, w_mlp2=w_mlp2)
    m = dict(norm1_g=m_norm1_g, w_in=m_w_in, q_norm_g=m_q_norm_g, k_norm_g=m_k_norm_g, rel_bias=m_rel_bias,
             conv_w=m_conv_w, pool_w=m_pool_w, pool_scale=m_pool_scale, w_out=m_w_out, norm2_g=m_norm2_g,
             w_mlp1=m_w_mlp1, w_mlp2=m_w_mlp2)
    v = dict(norm1_g=v_norm1_g, w_in=v_w_in, q_norm_g=v_q_norm_g, k_norm_g=v_k_norm_g, rel_bias=v_rel_bias,
             conv_w=v_conv_w, pool_w=v_pool_w, pool_scale=v_pool_scale, w_out=v_w_out, norm2_g=v_norm2_g,
             w_mlp1=v_w_mlp1, w_mlp2=v_w_mlp2)
    ax, ay, ac = lax.axis_index("x"), lax.axis_index("y"), lax.axis_index("c")
    b1 = jnp.reshape(2 * ax + ay, (1,)).astype(jnp.int32)

    cw_rows = _all_gather8("gather_conv_w", jnp.pad(conv_w.reshape(DEPTH * 3, 64), ((0, 4), (0, 64))), b1)
    cw_chips = [cw_rows[(4 * cx + 2 * cy) * 16:(4 * cx + 2 * cy) * 16 + 12, :64] for cx in range(2) for cy in range(2)]
    small = {n: w[n] for n in SMALL}
    small["conv_w"] = jnp.concatenate(cw_chips, axis=1).reshape(DEPTH, 3, CW)

    casts = [_cast_into_full(f"cast_{n}", t, w[n], b1, cw_rows) for t, n in enumerate(LARGE)]
    first, first_sems = _gather_start("gather_start_first", (0,), casts)
    held = [first]
    sems = dict(first_sems)

    def layer_weights(l, after):
        arrived = _gather_wait(l, held[0], sems[l], after)
        if l == 0:
            arrived, rest_sems = _gather_start("gather_start_rest", tuple(range(1, DEPTH)), arrived)
            sems.update(rest_sems)
        held[0] = _pass_on(l, arrived)
        return held[0]

    flights = {}

    def await_flight(l, ts, afters):
        g, landing, sm, _ = flights[l, ts]
        flights[l, ts] = _reduce_wait(f"reduce_wait_{l}_{ts[0]}", ts, g, landing, sm, afters)

    def on_grads(l, ts, grads):
        if ts == (0, 1) and l + 1 < DEPTH:
            await_flight(l + 1, (2, 3), [grads[0]])
            await_flight(l + 1, (0, 1), [grads[0]])
        flights[l, ts] = _reduce_start(f"reduce_start_{l}_{ts[0]}", ts, grads)
        return flights[l, ts][3]

    loss_part, grad_x, gsmall = _local_step(x[0], loss_target[0], layer_weights, on_grads, small)
    loss = lax.psum(loss_part[0, 0], ("x", "y", "c"))
    order = [n for n in SMALL]
    packed = _pack([gsmall[n] for n in order])

    out = {n: [lax.empty(w[n].shape, F32) for _ in range(4)] for n in LARGE}
    for l in reversed(range(DEPTH)):
        if l == 0:
            afters = [grad_x, packed] + [out[n][0] for n in LARGE]
            await_flight(0, (2, 3), afters)
            await_flight(0, (0, 1), afters)
        sums = [None] * 4
        for ts in ((0, 1), (2, 3)):
            g, landing = flights[l, ts]
            for i, t in enumerate(ts):
                sums[t] = _add4(f"add4_{LARGE[t]}_{l}", t, g[i], landing[i], b1)
        theirs = _swap_sib(f"swap_sib_{l}", sums)
        for t, n in enumerate(LARGE):
            out[n] = _adamw_pair(f"adamw_{n}_{l}", l, sums[t], theirs[t], w[n], m[n], v[n], out[n])

    rows = packed.shape[0]
    summed = _sum8("sum_small", _all_gather8("gather_small", packed, out[LARGE[0]][0]).reshape(8, rows, 128))
    gfull = dict(zip(order, _unpack(summed, [gsmall[n].shape for n in order])))
    gfull["conv_w"] = lax.dynamic_slice_in_dim(gfull["conv_w"], (2 * ax + ay) * 64, 64, axis=2)
    res = _adamw("adamw_small", _pack([gfull[n] for n in order]), _pack([w[n] for n in order]),
                 _pack([m[n] for n in order]), _pack([v[n] for n in order]))
    for n, parts in zip(order, zip(*[_unpack(r, [w[k].shape for k in order]) for r in res])):
        out[n] = list(parts)

    names = ("norm1_g", "w_in", "q_norm_g", "k_norm_g", "rel_bias", "conv_w", "pool_w", "pool_scale", "w_out",
             "norm2_g", "w_mlp1", "w_mlp2")
    flat = [loss, grad_x[None]]
    for i in range(4):
        flat += [out[n][i] for n in names]
    return tuple(flat)
```

```python
import functools

import jax
import jax.numpy as jnp
from jax import lax
from jax.experimental import pallas as pl
from jax.experimental.pallas import tpu as pltpu

F32 = jnp.float32
BF16 = jnp.bfloat16

D = 1024
DEPTH = 4
CH = 64
NPREV = 8
KB = (NPREV + 1) * CH
PADR = NPREV * CH
HD = 64
AW = 512
CW = 256
PWD = 256
DIN = 3 * AW + 3 * CW + PWD
DFF = 4 * D
NIDX = 384
EPS = 1e-6
NEG_INF = -1e30

ADAM_LR = 0.001
ADAM_B1 = 0.9
ADAM_B2 = 0.999
ADAM_EPS = 1e-08
ADAM_WD = 0.01
ADAM_STEP = 10

VMEM_LIMIT = 52 * 1024 * 1024
MESH = pl.DeviceIdType.MESH
ANY = pl.BlockSpec(memory_space=pl.ANY)


def _cp(*sem):
    return pltpu.CompilerParams(dimension_semantics=sem, vmem_limit_bytes=VMEM_LIMIT)


def _inv_rms(x):
    return lax.rsqrt(jnp.mean(x * x, axis=-1, keepdims=True) + EPS)


def _head_mean_matrix():
    r = lax.broadcasted_iota(jnp.int32, (AW, AW), 0) // HD
    c = lax.broadcasted_iota(jnp.int32, (AW, AW), 1) // HD
    return jnp.where(r == c, 1.0 / HD, 0.0).astype(BF16)


def _head_mean(x, hm):
    hi = x.astype(BF16)
    lo = (x - hi.astype(F32)).astype(BF16)
    return (jnp.dot(hi, hm, preferred_element_type=F32)
            + jnp.dot(lo, hm, preferred_element_type=F32))


def _rmsnorm(name, x, g):
    s = x.shape[0]
    tm = 512

    def body(x_ref, g_ref, h_ref):
        xv = x_ref[...]
        h_ref[...] = (xv * _inv_rms(xv) * g_ref[...]).astype(BF16)

    return pl.pallas_call(
        body, name=name, grid=(s // tm,),
        in_specs=[pl.BlockSpec((tm, D), lambda i: (i, 0)), pl.BlockSpec((1, D), lambda i: (0, 0))],
        out_specs=pl.BlockSpec((tm, D), lambda i: (i, 0)),
        out_shape=jax.ShapeDtypeStruct((s, D), BF16),
        compiler_params=_cp("parallel"),
    )(x, g)


def _relu2(a):
    r = jnp.maximum(a, jnp.zeros_like(a))
    return r * r


def _mm_nn(name, a, w, l, out_dtype):
    s, k = a.shape
    n = w.shape[2]
    tm = 256

    def body(a_ref, w_ref, o_ref):
        o_ref[...] = jnp.dot(a_ref[...], w_ref[...], preferred_element_type=F32).astype(o_ref.dtype)

    return pl.pallas_call(
        body, name=name, grid=(s // tm,),
        in_specs=[pl.BlockSpec((tm, k), lambda i: (i, 0)),
                  pl.BlockSpec((None, k, n), lambda i: (l, 0, 0))],
        out_specs=pl.BlockSpec((tm, n), lambda i: (i, 0)),
        out_shape=jax.ShapeDtypeStruct((s, n), out_dtype),
        compiler_params=_cp("parallel"),
    )(a, w)


def _mm_res_norm(name, a, w, l, res, g, relu2=False):
    s, k = a.shape
    tm = 256

    def body(a_ref, w_ref, r_ref, g_ref, x_ref, h_ref):
        av = _relu2(a_ref[...]) if relu2 else a_ref[...]
        acc = r_ref[...] + jnp.dot(av, w_ref[...], preferred_element_type=F32)
        x_ref[...] = acc
        h_ref[...] = (acc * _inv_rms(acc) * g_ref[...]).astype(BF16)

    return pl.pallas_call(
        body, name=name, grid=(s // tm,),
        in_specs=[pl.BlockSpec((tm, k), lambda i: (i, 0)),
                  pl.BlockSpec((None, k, D), lambda i: (l, 0, 0)),
                  pl.BlockSpec((tm, D), lambda i: (i, 0)),
                  pl.BlockSpec((1, D), lambda i: (0, 0))],
        out_specs=[pl.BlockSpec((tm, D), lambda i: (i, 0))] * 2,
        out_shape=[jax.ShapeDtypeStruct((s, D), F32), jax.ShapeDtypeStruct((s, D), BF16)],
        compiler_params=_cp("parallel"),
    )(a, w, res, g)


def _qkv(name, p, qg, kg):
    s = p.shape[0]
    tm = PADR
    nb = s // tm

    def body(pq_ref, pk_ref, pv_ref, qg_ref, kg_ref, q_ref, qt_ref, k_ref, kt_ref, v_ref, vt_ref):
        t = pl.program_id(0)
        hm = _head_mean_matrix()

        def nrm(x, g):
            return x * lax.rsqrt(_head_mean(x * x, hm) + EPS) * g

        first = t == 0
        qq = nrm(pq_ref[...], qg_ref[...]) * 0.125
        kk = jnp.where(first, 0.0, nrm(pk_ref[...], kg_ref[...]))
        vv = jnp.where(first, 0.0, pv_ref[...])
        q_ref[...] = qq.astype(BF16)
        qt_ref[...] = qq.T.astype(BF16)
        k_ref[...] = kk.astype(BF16)
        kt_ref[...] = kk.T.astype(BF16)
        v_ref[...] = vv.astype(BF16)
        vt_ref[...] = vv.T.astype(BF16)

    def src(col):
        return pl.BlockSpec((tm, AW), lambda t: (jnp.maximum(t - 1, 0), col))

    gspec = pl.BlockSpec((1, AW), lambda t: (0, 0))
    rows = pl.BlockSpec((tm, AW), lambda t: (t, 0))
    cols = pl.BlockSpec((AW, tm), lambda t: (0, t))
    return pl.pallas_call(
        body, name=name, grid=(nb + 1,),
        in_specs=[src(0), src(1), src(2), gspec, gspec],
        out_specs=[pl.BlockSpec((tm, AW), lambda t: (jnp.maximum(t - 1, 0), 0)),
                   pl.BlockSpec((AW, tm), lambda t: (0, jnp.maximum(t - 1, 0))),
                   rows, cols, rows, cols],
        out_shape=[jax.ShapeDtypeStruct((s, AW), BF16), jax.ShapeDtypeStruct((AW, s), BF16),
                   jax.ShapeDtypeStruct((s + PADR, AW), BF16), jax.ShapeDtypeStruct((AW, s + PADR), BF16),
                   jax.ShapeDtypeStruct((s + PADR, AW), BF16), jax.ShapeDtypeStruct((AW, s + PADR), BF16)],
        compiler_params=_cp("arbitrary"),
    )(p, p, p, qg, kg)


NBAND = KB // CH
HIGHEST = lax.Precision.HIGHEST
NT_DIMS = (((1,), (1,)), ((), ()))


def _onehot_table(a):
    m = lax.broadcasted_iota(jnp.int32, (128, NIDX), 0)
    idx = lax.broadcasted_iota(jnp.int32, (128, NIDX), 1)
    rel = jnp.clip(KB - 1 - (CH * a + m), -128, 128) + 128
    return jnp.where(rel == idx, 1.0, 0.0).astype(F32)


def _onehot_diagonal():
    r = lax.broadcasted_iota(jnp.int32, (CH * CH, 128), 0)
    m = lax.broadcasted_iota(jnp.int32, (CH * CH, 128), 1)
    return jnp.where((r % CH) - (r // CH) + (CH - 1) == m, 1.0, 0.0).astype(F32)


def _bias_expand(name, rb):
    def body(rb_ref, o_ref):
        along = [lax.dot_general(rb_ref[...], _onehot_table(a), NT_DIMS, preferred_element_type=F32,
                                 precision=HIGHEST) for a in range(NBAND)]
        o_ref[...] = lax.dot_general(jnp.concatenate(along, axis=0), _onehot_diagonal(), NT_DIMS,
                                     preferred_element_type=F32, precision=HIGHEST)

    return pl.pallas_call(
        body, name=name,
        out_shape=jax.ShapeDtypeStruct((NBAND * 8, CH * CH), F32),
    )(rb)


def _bias_reduce(name, db):
    def body(db_ref, o_ref):
        along = jnp.dot(db_ref[...], _onehot_diagonal(), preferred_element_type=F32, precision=HIGHEST)
        acc = jnp.zeros((8, NIDX), F32)
        for a in range(NBAND):
            acc = acc + jnp.dot(along[8 * a:8 * a + 8, :], _onehot_table(a), preferred_element_type=F32,
                                precision=HIGHEST)
        o_ref[...] = acc

    return pl.pallas_call(
        body, name=name,
        out_shape=jax.ShapeDtypeStruct((8, NIDX), F32),
    )(db)


def _bias_layout(flat):
    b = flat.reshape(NBAND, 8, CH, CH).transpose(1, 0, 3, 2).reshape(4, 2, KB, CH)
    pair = b.transpose(0, 2, 1, 3).reshape(4, KB, 128)
    first = jnp.pad(pair, ((0, 0), (0, CH), (0, 0)), constant_values=NEG_INF)
    second = jnp.pad(pair, ((0, 0), (CH, 0), (0, 0)), constant_values=NEG_INF)
    return jnp.concatenate([first, second], axis=2)


def _bias_unlayout(dbt):
    b = dbt.reshape(4, NBAND, CH, 2, CH)
    return b.transpose(1, 0, 3, 4, 2).reshape(NBAND * 8, CH * CH)


UNIT = 2 * CH
BAND2 = KB + CH


def _pair_weights(xt):
    x = xt.astype(F32)
    row = lax.broadcasted_iota(jnp.int32, (128, UNIT), 0)
    low = lax.broadcasted_iota(jnp.int32, (128, UNIT), 1) < HD
    swapped = pltpu.roll(x, HD, 1)
    same = (row < HD) == low
    first = jnp.where(same, jnp.where(low, x, swapped), 0.0)
    second = jnp.where(same, jnp.where(low, swapped, x), 0.0)
    return jnp.concatenate([first, second], axis=1).astype(BF16)


def _pair_rows(x):
    low = lax.broadcasted_iota(jnp.int32, (CH, 128), 1) < HD
    zero = jnp.zeros((CH, 128), x.dtype)
    parts = []
    for c in range(2):
        xc = x[c * CH:(c + 1) * CH, :]
        parts += [jnp.where(low, xc, zero), jnp.where(low, zero, xc)]
    return jnp.concatenate(parts, axis=0)


def _unpair(raw):
    b0, b1 = raw[:, 0:128], raw[:, 128:256]
    row = lax.broadcasted_iota(jnp.int32, (128, 128), 0)
    low = lax.broadcasted_iota(jnp.int32, (128, 128), 1) < HD
    top = jnp.where(low, b0, pltpu.roll(b1, HD, 1))
    bottom = jnp.where(low, pltpu.roll(b0, HD, 1), b1)
    return jnp.where(row < HD, top, bottom).T


def _softmax_t(kb, qw, bias2, row0, padded):
    s = jnp.dot(kb, qw, preferred_element_type=F32) + bias2
    if padded:
        s = jnp.where(row0 + lax.broadcasted_iota(jnp.int32, (BAND2, 256), 0) >= PADR, s, NEG_INF)
    e = jnp.exp(s - jnp.max(s, axis=0, keepdims=True))
    return e, 1.0 / jnp.sum(e, axis=0, keepdims=True)


def _unit_loops(s, unit):
    lax.fori_loop(0, PADR // UNIT, lambda u, c: unit(u, True) or c, 0)
    lax.fori_loop(PADR // UNIT, s // UNIT, lambda u, c: unit(u, False) or c, 0, unroll=2)


def _attn_fwd(name, kp, qt, vt, bias2):
    s = qt.shape[1]

    def body(k_ref, qt_ref, vt_ref, b_ref, o_ref):
        def unit(u, padded):
            r0 = pl.multiple_of(u * UNIT, UNIT)
            e, inv = _softmax_t(k_ref[pl.ds(r0, BAND2), :], _pair_weights(qt_ref[:, pl.ds(r0, UNIT)]), b_ref[...],
                                r0, padded)
            raw = jnp.dot(vt_ref[:, pl.ds(r0, BAND2)], e.astype(BF16), preferred_element_type=F32) * inv
            o_ref[pl.ds(r0, UNIT), :] = _unpair(raw).astype(BF16)

        _unit_loops(s, unit)

    return pl.pallas_call(
        body, name=name, grid=(AW // 128,),
        in_specs=[pl.BlockSpec((s + PADR, 128), lambda h: (0, h)),
                  pl.BlockSpec((128, s), lambda h: (h, 0)),
                  pl.BlockSpec((128, s + PADR), lambda h: (h, 0)),
                  pl.BlockSpec((None, BAND2, 256), lambda h: (h, 0, 0))],
        out_specs=pl.BlockSpec((s, 128), lambda h: (0, h)),
        out_shape=jax.ShapeDtypeStruct((s, AW), BF16),
        compiler_params=_cp("parallel"),
    )(kp, qt, vt, bias2)


def _attn_bwd(name, q, qt, kp, kt, vp, bias2, do, dot):
    s = q.shape[0]

    def body(q_ref, qt_ref, k_ref, kt_ref, v_ref, b_ref, do_ref, dot_ref, dq_ref, dk_ref, dv_ref, db_ref):
        dk_ref[...] = jnp.zeros_like(dk_ref)
        dv_ref[...] = jnp.zeros_like(dv_ref)
        db_ref[...] = jnp.zeros_like(db_ref)

        def unit(u, padded):
            r0 = pl.multiple_of(u * UNIT, UNIT)
            rows, band = pl.ds(r0, UNIT), pl.ds(r0, BAND2)
            e, inv = _softmax_t(k_ref[band, :], _pair_weights(qt_ref[:, rows]), b_ref[...], r0, padded)
            pt = e * inv
            dpt = jnp.dot(v_ref[band, :], _pair_weights(dot_ref[:, rows]), preferred_element_type=F32)
            ds = pt * (dpt - jnp.sum(dpt * pt, axis=0, keepdims=True))
            db_ref[...] += ds[0:KB, 0:128] + ds[CH:BAND2, 128:256]
            dsb = ds.astype(BF16)
            dq_ref[rows, :] = _unpair(jnp.dot(kt_ref[:, band], dsb, preferred_element_type=F32))
            dk_ref[band, :] += jnp.dot(dsb, _pair_rows(q_ref[rows, :]), preferred_element_type=F32)
            dv_ref[band, :] += jnp.dot(pt.astype(BF16), _pair_rows(do_ref[rows, :]), preferred_element_type=F32)

        _unit_loops(s, unit)

    row_q = pl.BlockSpec((s, 128), lambda h: (0, h))
    col_q = pl.BlockSpec((128, s), lambda h: (h, 0))
    row_k = pl.BlockSpec((s + PADR, 128), lambda h: (0, h))
    col_k = pl.BlockSpec((128, s + PADR), lambda h: (h, 0))
    return pl.pallas_call(
        body, name=name, grid=(AW // 128,),
        in_specs=[row_q, col_q, row_k, col_k, row_k,
                  pl.BlockSpec((None, BAND2, 256), lambda h: (h, 0, 0)), row_q, col_q],
        out_specs=[row_q, row_k, row_k, pl.BlockSpec((None, KB, 128), lambda h: (h, 0, 0))],
        out_shape=[jax.ShapeDtypeStruct((s, AW), F32),
                   jax.ShapeDtypeStruct((s + PADR, AW), F32),
                   jax.ShapeDtypeStruct((s + PADR, AW), F32),
                   jax.ShapeDtypeStruct((4, KB, 128), F32)],
        compiler_params=_cp("parallel"),
    )(q, qt, kp, kt, vp, bias2, do, dot)


def _rows_before(cur, prev, k):
    row = lax.broadcasted_iota(jnp.int32, cur.shape, 0)
    return jnp.where(row >= k, pltpu.roll(cur, k, 0), pltpu.roll(prev, k, 0))


def _rows_after(cur, nxt, k):
    n = cur.shape[0]
    row = lax.broadcasted_iota(jnp.int32, cur.shape, 0)
    return jnp.where(row < n - k, pltpu.roll(cur, n - k, 0), pltpu.roll(nxt, n - k, 0))


def _pool_window_lanes():
    lg = lax.broadcasted_iota(jnp.int32, (1, PWD), 1) // 64
    return lg, jnp.where(lg == 0, 2.0, jnp.where(lg == 1, 4.0, jnp.where(lg == 2, 8.0, 16.0))).astype(F32)


def _pool_mean_minus_token(u, up, row0):
    lg, wv = _pool_window_lanes()
    sums = []
    c, p = u, up
    for k in (1, 2, 4, 8):
        c2 = c + _rows_before(c, p, k)
        p = p + pltpu.roll(p, k, 0)
        c = c2
        sums.append(c)
    win = jnp.where(lg == 0, sums[0], jnp.where(lg == 1, sums[1], jnp.where(lg == 2, sums[2], sums[3])))
    pos1 = (row0 + lax.broadcasted_iota(jnp.int32, u.shape, 0) + 1).astype(F32)
    cnt = jnp.minimum(pos1, wv)
    return win / cnt - u, cnt


def _conv_taps(z, zp, w0, w1, w2):
    z1 = _rows_before(z, zp, 1)
    z2 = _rows_before(z, zp, 2)
    return (w0 * z2 + w1 * z1) + w2 * z, z1, z2


CP_TM = 512


def _convpool_fwd(name, p, o, cw, pwbd, ps):
    s = p.shape[0]
    tm = CP_TM
    nb = s // tm

    def body(gb_ref, gc_ref, hin_ref, u_ref, gcp_ref, hinp_ref, up_ref, o_ref, cw_ref, pw_ref, ps_ref, mix_ref):
        i = pl.program_id(0)
        has_prev = i > 0
        z = gc_ref[...] * hin_ref[...]
        zp = jnp.where(has_prev, gcp_ref[...] * hinp_ref[...], 0.0)
        y3, _, _ = _conv_taps(z, zp, cw_ref[0:1, :], cw_ref[1:2, :], cw_ref[2:3, :])
        m, _ = _pool_mean_minus_token(u_ref[...], jnp.where(has_prev, up_ref[...], 0.0), i * tm)
        yp = jnp.dot(m.astype(BF16), pw_ref[...].astype(BF16), preferred_element_type=F32) * ps_ref[...]
        mix_ref[:, 0:AW] = o_ref[...]
        mix_ref[:, AW:AW + CW] = (gb_ref[...] * y3).astype(BF16)
        mix_ref[:, AW + CW:D] = yp.astype(BF16)

    def cur(col):
        return pl.BlockSpec((tm, CW), lambda i: (i, col))

    def prev(col):
        return pl.BlockSpec((tm, CW), lambda i: (jnp.maximum(i - 1, 0), col))

    def whole(a):
        return pl.BlockSpec(a.shape, lambda i: (0,) * a.ndim)

    return pl.pallas_call(
        body, name=name, grid=(nb,),
        in_specs=[cur(6), cur(7), cur(8), cur(9), prev(7), prev(8), prev(9),
                  pl.BlockSpec((tm, AW), lambda i: (i, 0)), whole(cw), whole(pwbd), whole(ps)],
        out_specs=pl.BlockSpec((tm, D), lambda i: (i, 0)),
        out_shape=jax.ShapeDtypeStruct((s, D), BF16),
        compiler_params=_cp("parallel"),
    )(p, p, p, p, p, p, p, o, cw, pwbd, ps)


def _convpool_bwd(name, p, dmix, cw, pwbd, ps):
    s = p.shape[0]
    tm = CP_TM
    nb = s // tm

    def body(gb_ref, gc_ref, hin_ref, u_ref, gcp_ref, hinp_ref, up_ref, gbn_ref, dyc_ref, dyp_ref, dycn_ref, dypn_ref,
             cw_ref, pw_ref, ps_ref, dcp_ref, dw0_ref, dw1_ref, dw2_ref, dps_ref, dpw_ref):
        i = pl.program_id(0)
        has_prev = i > 0
        has_next = i < nb - 1
        w0, w1, w2 = cw_ref[0:1, :], cw_ref[1:2, :], cw_ref[2:3, :]
        gb, gc, hin = gb_ref[...], gc_ref[...], hin_ref[...]
        dyc = dyc_ref[...]
        z = gc * hin
        zp = jnp.where(has_prev, gcp_ref[...] * hinp_ref[...], 0.0)
        y3, z1, z2 = _conv_taps(z, zp, w0, w1, w2)
        dy3 = dyc * gb
        dy3n = jnp.where(has_next, dycn_ref[...] * gbn_ref[...], 0.0)
        dz = w2 * dy3 + w1 * _rows_after(dy3, dy3n, 1) + w0 * _rows_after(dy3, dy3n, 2)
        pw = pw_ref[...].astype(BF16)
        psv = ps_ref[...]
        m, cnt = _pool_mean_minus_token(u_ref[...], jnp.where(has_prev, up_ref[...], 0.0), i * tm)
        mb = m.astype(BF16)
        dyp = dyp_ref[...]
        dmp = (dyp * psv).astype(BF16)
        dmpn = jnp.where(has_next, dypn_ref[...] * psv, 0.0).astype(BF16)
        nt = (((1,), (1,)), ((), ()))
        dm = lax.dot_general(dmp, pw, nt, preferred_element_type=F32)
        dmn = lax.dot_general(dmpn, pw, nt, preferred_element_type=F32)
        lg, wv = _pool_window_lanes()
        cc, cn = dm / cnt, dmn / wv
        sums = []
        for k in (1, 2, 4, 8):
            c2 = cc + _rows_after(cc, cn, k)
            cn = cn + pltpu.roll(cn, tm - k, 0)
            cc = c2
            sums.append(cc)
        du = jnp.where(lg == 0, sums[0], jnp.where(lg == 1, sums[1], jnp.where(lg == 2, sums[2], sums[3]))) - dm
        dcp_ref[:, 0:CW] = (dyc * y3).astype(BF16)
        dcp_ref[:, CW:2 * CW] = (dz * hin).astype(BF16)
        dcp_ref[:, 2 * CW:3 * CW] = (dz * gc).astype(BF16)
        dcp_ref[:, 3 * CW:4 * CW] = du.astype(BF16)
        parts = (jnp.sum(dy3 * z2, axis=0, keepdims=True),
                 jnp.sum(dy3 * z1, axis=0, keepdims=True),
                 jnp.sum(dy3 * z, axis=0, keepdims=True),
                 jnp.sum(dyp * jnp.dot(mb, pw, preferred_element_type=F32), axis=0, keepdims=True),
                 lax.dot_general(mb, dmp, (((0,), (0,)), ((), ())), preferred_element_type=F32))
        accs = (dw0_ref, dw1_ref, dw2_ref, dps_ref, dpw_ref)

        @pl.when(i == 0)
        def _():
            for a, v in zip(accs, parts):
                a[...] = v

        @pl.when(i > 0)
        def _():
            for a, v in zip(accs, parts):
                a[...] += v

    def cur(col):
        return pl.BlockSpec((tm, CW), lambda i: (i, col))

    def prev(col):
        return pl.BlockSpec((tm, CW), lambda i: (jnp.maximum(i - 1, 0), col))

    def nxt(col):
        return pl.BlockSpec((tm, CW), lambda i: (jnp.minimum(i + 1, nb - 1), col))

    def whole(shape):
        return pl.BlockSpec(shape, lambda i: (0,) * len(shape))

    row = jax.ShapeDtypeStruct((1, CW), F32)
    return pl.pallas_call(
        body, name=name, grid=(nb,),
        in_specs=[cur(6), cur(7), cur(8), cur(9), prev(7), prev(8), prev(9), nxt(6),
                  cur(0), cur(1), nxt(0), nxt(1), whole(cw.shape), whole(pwbd.shape), whole(ps.shape)],
        out_specs=[pl.BlockSpec((tm, D), lambda i: (i, 0)), whole((1, CW)), whole((1, CW)), whole((1, CW)),
                   whole((1, PWD)), whole((PWD, PWD))],
        out_shape=[jax.ShapeDtypeStruct((s, D), BF16), row, row, row, row,
                   jax.ShapeDtypeStruct((PWD, PWD), F32)],
        compiler_params=_cp("arbitrary"),
    )(p, p, p, p, p, p, p, p, dmix, dmix, dmix, dmix, cw, pwbd, ps)


def _qkv_bwd(name, p, dq, dkp, dvp, dcp, qg, kg):
    s = p.shape[0]
    tm = 256
    off = PADR // tm

    def body(pq_ref, pk_ref, dq_ref, dk_ref, dv_ref, dcp_ref, qg_ref, kg_ref, dp_ref, dqg_ref, dkg_ref):
        i = pl.program_id(0)
        hm = _head_mean_matrix()

        def nrm_bwd(x, g, dy):
            r = lax.rsqrt(_head_mean(x * x, hm) + EPS)
            xn = x * r
            dxn = dy * g
            dx = r * (dxn - xn * _head_mean(dxn * xn, hm))
            dg = jnp.sum(dy * xn, axis=0, keepdims=True)
            dg = (dg[:, 0:128] + dg[:, 128:256]) + (dg[:, 256:384] + dg[:, 384:512])
            return dx, dg + pltpu.roll(dg, HD, 1)

        dxq, dgq = nrm_bwd(pq_ref[...], qg_ref[...], dq_ref[...] * 0.125)
        dxk, dgk = nrm_bwd(pk_ref[...], kg_ref[...], dk_ref[...])
        dp_ref[:, 0:AW] = dxq.astype(BF16)
        dp_ref[:, AW:2 * AW] = dxk.astype(BF16)
        dp_ref[:, 2 * AW:3 * AW] = dv_ref[...].astype(BF16)
        dp_ref[:, 3 * AW:DIN] = dcp_ref[...]

        @pl.when(i == 0)
        def _():
            dqg_ref[...] = dgq
            dkg_ref[...] = dgk

        @pl.when(i > 0)
        def _():
            dqg_ref[...] += dgq
            dkg_ref[...] += dgk

    gspec = pl.BlockSpec((1, AW), lambda i: (0, 0))
    gout = pl.BlockSpec((1, 128), lambda i: (0, 0))
    return pl.pallas_call(
        body, name=name, grid=(s // tm,),
        in_specs=[pl.BlockSpec((tm, AW), lambda i: (i, 0)), pl.BlockSpec((tm, AW), lambda i: (i, 1)),
                  pl.BlockSpec((tm, AW), lambda i: (i, 0)),
                  pl.BlockSpec((tm, AW), lambda i: (i + off, 0)),
                  pl.BlockSpec((tm, AW), lambda i: (i + off, 0)),
                  pl.BlockSpec((tm, D), lambda i: (i, 0)), gspec, gspec],
        out_specs=[pl.BlockSpec((tm, DIN), lambda i: (i, 0)), gout, gout],
        out_shape=[jax.ShapeDtypeStruct((s, DIN), BF16), jax.ShapeDtypeStruct((1, 128), F32),
                   jax.ShapeDtypeStruct((1, 128), F32)],
        compiler_params=_cp("arbitrary"),
    )(p, p, dq, dkp, dvp, dcp, qg, kg)


def _loss_grad(name, y, t):
    s = y.shape[0]
    tm = 512

    def body(y_ref, t_ref, dy_ref, dyb_ref, l_ref):
        i = pl.program_id(0)
        e = y_ref[...] - t_ref[...]
        dy = e * (1.0 / D)
        dy_ref[...] = dy
        dyb_ref[...] = dy.astype(BF16)
        part = 0.5 * jnp.sum(jnp.mean(e * e, axis=-1, keepdims=True), axis=0, keepdims=True)

        @pl.when(i == 0)
        def _():
            l_ref[...] = part

        @pl.when(i > 0)
        def _():
            l_ref[...] += part

    blk = pl.BlockSpec((tm, D), lambda i: (i, 0))
    return pl.pallas_call(
        body, name=name, grid=(s // tm,),
        in_specs=[blk, blk],
        out_specs=[blk, blk, pl.BlockSpec((1, 1), lambda i: (0, 0))],
        out_shape=[jax.ShapeDtypeStruct((s, D), F32), jax.ShapeDtypeStruct((s, D), BF16),
                   jax.ShapeDtypeStruct((1, 1), F32)],
        compiler_params=_cp("arbitrary"),
    )(y, t)


def _mm_nt_relu(name, dxb, w, l, a):
    s = dxb.shape[0]
    tm = 256

    def body(d_ref, w_ref, a_ref, o_ref):
        df = lax.dot_general(d_ref[...], w_ref[...], (((1,), (1,)), ((), ())), preferred_element_type=F32)
        o_ref[...] = (df * (2.0 * jnp.maximum(a_ref[...].astype(F32), 0.0))).astype(BF16)

    return pl.pallas_call(
        body, name=name, grid=(s // tm,),
        in_specs=[pl.BlockSpec((tm, D), lambda i: (i, 0)),
                  pl.BlockSpec((None, DFF, D), lambda i: (l, 0, 0)),
                  pl.BlockSpec((tm, DFF), lambda i: (i, 0))],
        out_specs=pl.BlockSpec((tm, DFF), lambda i: (i, 0)),
        out_shape=jax.ShapeDtypeStruct((s, DFF), BF16),
        compiler_params=_cp("parallel"),
    )(dxb, w, a)


def _proj_out_bwd(name, dxb, w, l):
    s = dxb.shape[0]
    tm = 512

    def body(d_ref, w_ref, do_ref, dot_ref, dcp_ref):
        d = d_ref[...]
        wa, wc = w_ref[0:AW, :], w_ref[AW:D, :]
        do_ref[...] = lax.dot_general(d, wa, NT_DIMS, preferred_element_type=F32).astype(BF16)
        dot_ref[...] = lax.dot_general(wa, d, NT_DIMS, preferred_element_type=F32).astype(BF16)
        dcp_ref[...] = lax.dot_general(d, wc, NT_DIMS, preferred_element_type=F32)

    return pl.pallas_call(
        body, name=name, grid=(s // tm,),
        in_specs=[pl.BlockSpec((tm, D), lambda i: (i, 0)),
                  pl.BlockSpec((None, D, D), lambda i: (l, 0, 0))],
        out_specs=[pl.BlockSpec((tm, AW), lambda i: (i, 0)), pl.BlockSpec((AW, tm), lambda i: (0, i)),
                   pl.BlockSpec((tm, D - AW), lambda i: (i, 0))],
        out_shape=[jax.ShapeDtypeStruct((s, AW), BF16), jax.ShapeDtypeStruct((AW, s), BF16),
                   jax.ShapeDtypeStruct((s, D - AW), F32)],
        compiler_params=_cp("parallel"),
    )(dxb, w)


def _mm_nt_normbwd(name, gy, w, l, x, g, dres, dep):
    s, k = gy.shape
    tm = 256

    def body(gy_ref, w_ref, x_ref, g_ref, dr_ref, dep_ref, dx_ref, dxb_ref, dg_ref):
        del dep_ref
        i = pl.program_id(0)
        dh = lax.dot_general(gy_ref[...], w_ref[...], (((1,), (1,)), ((), ())), preferred_element_type=F32)
        xv = x_ref[...]
        r = _inv_rms(xv)
        xn = xv * r
        dxn = dh * g_ref[...]
        dx = r * (dxn - xn * jnp.mean(dxn * xn, axis=-1, keepdims=True)) + dr_ref[...]
        dx_ref[...] = dx
        dxb_ref[...] = dx.astype(BF16)
        part = jnp.sum(dh * xn, axis=0, keepdims=True)

        @pl.when(i == 0)
        def _():
            dg_ref[...] = part

        @pl.when(i > 0)
        def _():
            dg_ref[...] += part

    blk = pl.BlockSpec((tm, D), lambda i: (i, 0))
    vec = pl.BlockSpec((1, D), lambda i: (0, 0))
    return pl.pallas_call(
        body, name=name, grid=(s // tm,),
        in_specs=[pl.BlockSpec((tm, k), lambda i: (i, 0)),
                  pl.BlockSpec((None, D, k), lambda i: (l, 0, 0)), blk, vec, blk, ANY],
        out_specs=[blk, blk, vec],
        out_shape=[jax.ShapeDtypeStruct((s, D), F32), jax.ShapeDtypeStruct((s, D), BF16),
                   jax.ShapeDtypeStruct((1, D), F32)],
        compiler_params=_cp("arbitrary"),
    )(gy, w, x, g, dres, dep)


def _mm_tn(name, a, b, tma, tnb, relu2=False):
    s, m = a.shape
    n = b.shape[1]

    def body(a_ref, b_ref, o_ref):
        av = _relu2(a_ref[...]) if relu2 else a_ref[...]
        o_ref[...] = lax.dot_general(av, b_ref[...], (((0,), (0,)), ((), ())),
                                     preferred_element_type=F32).astype(BF16)

    return pl.pallas_call(
        body, name=name, grid=(m // tma, n // tnb),
        in_specs=[pl.BlockSpec((s, tma), lambda i, j: (0, i)),
                  pl.BlockSpec((s, tnb), lambda i, j: (0, j))],
        out_specs=pl.BlockSpec((tma, tnb), lambda i, j: (i, j)),
        out_shape=jax.ShapeDtypeStruct((m, n), BF16),
        compiler_params=_cp("parallel", "parallel"),
    )(a, b)


def _adamw_math(gv, wv, mv, vv):
    mn = ADAM_B1 * mv + (1.0 - ADAM_B1) * gv
    vn = ADAM_B2 * vv + (1.0 - ADAM_B2) * jnp.square(gv)
    m_hat = mn / (1.0 - ADAM_B1 ** ADAM_STEP)
    v_hat = vn / (1.0 - ADAM_B2 ** ADAM_STEP)
    return gv, -ADAM_LR * (m_hat / (jnp.sqrt(v_hat) + ADAM_EPS) + ADAM_WD * wv), mn, vn


def _adamw(name, g, w, m, v):
    r, c = g.shape
    tm = 256 if r % 256 == 0 else r

    def body(g_ref, w_ref, m_ref, v_ref, go_ref, d_ref, mo_ref, vo_ref):
        go_ref[...], d_ref[...], mo_ref[...], vo_ref[...] = _adamw_math(g_ref[...], w_ref[...], m_ref[...], v_ref[...])

    blk = pl.BlockSpec((tm, c), lambda i: (i, 0))
    return pl.pallas_call(
        body, name=name, grid=(r // tm,),
        in_specs=[blk] * 4, out_specs=[blk] * 4,
        out_shape=[jax.ShapeDtypeStruct((r, c), F32)] * 4,
        compiler_params=_cp("parallel"),
    )(g, w, m, v)


def _place():
    x, y, c = lax.axis_index("x"), lax.axis_index("y"), lax.axis_index("c")
    chips = [(1 - x, y), (x, 1 - y), (1 - x, 1 - y)]
    return x, y, c, chips


BLOCK_AXIS = (2, 1, 2, 1)
LARGE_DIMS = ((D, DIN), (D, D), (D, DFF), (DFF, D))


def _full_shape(t, layers, dtype):
    r, c = LARGE_DIMS[t]
    return jax.ShapeDtypeStruct((layers, r, c), dtype)


def _cast_into_full(name, t, shard, b1, dep):
    _, r, c = shard.shape
    tm = min(256, r)
    if BLOCK_AXIS[t] == 1:
        out_spec = pl.BlockSpec((None, tm, c), lambda l, i, br: (l, br[0] * (r // tm) + i, 0))
    else:
        out_spec = pl.BlockSpec((None, tm, c), lambda l, i, br: (l, i, br[0]))

    def body(b_ref, x_ref, dep_ref, o_ref):
        del b_ref, dep_ref
        o_ref[...] = x_ref[...].astype(BF16)

    return pl.pallas_call(
        body, name=name,
        grid_spec=pltpu.PrefetchScalarGridSpec(
            num_scalar_prefetch=1, grid=(DEPTH, r // tm),
            in_specs=[pl.BlockSpec((None, tm, c), lambda l, i, br: (l, i, 0)), ANY],
            out_specs=out_spec),
        out_shape=_full_shape(t, DEPTH, BF16),
        compiler_params=_cp("parallel", "parallel"),
    )(b1, shard, dep)


HBM = pl.BlockSpec(memory_space=pltpu.HBM)
SEM = pl.BlockSpec(memory_space=pltpu.SEMAPHORE)
DATAFLOW = pltpu.SideEffectType.DATAFLOW_SIDE_EFFECTING


def _half(ref, l, t, b, c):
    r, cols = LARGE_DIMS[t]
    if BLOCK_AXIS[t] == 1:
        n = r // 8
        return ref.at[l, pl.ds(pl.multiple_of(b * (2 * n) + c * n, 16), n), :]
    n, w = r // 2, cols // 4
    return ref.at[l, pl.ds(pl.multiple_of(c * n, 16), n), pl.ds(pl.multiple_of(b * w, 128), w)]


def _gather_start(name, layers, fulls):
    def body(*refs):
        f_refs, sems = refs[4:8], refs[8:8 + 2 * len(layers)]
        x, y, c, chips = _place()
        for i, l in enumerate(layers):
            for t in range(4):
                own = _half(f_refs[t], l, t, 2 * x + y, c)
                for j, (cx, cy) in enumerate(chips):
                    pltpu.make_async_remote_copy(src_ref=own, dst_ref=own, send_sem=sems[2 * i].at[3 * t + j],
                                                 recv_sem=sems[2 * i + 1].at[3 * t + j], device_id=(cx, cy, c),
                                                 device_id_type=MESH).start()

    outs = pl.pallas_call(
        body, name=name,
        in_specs=[HBM] * 4, out_specs=[HBM] * 4 + [SEM] * (2 * len(layers)),
        out_shape=[pltpu.HBM(s.shape, s.dtype) for s in (_full_shape(t, DEPTH, BF16) for t in range(4))]
        + [pltpu.SemaphoreType.DMA((12,))] * (2 * len(layers)),
        input_output_aliases={t: t for t in range(4)},
        compiler_params=pltpu.CompilerParams(has_side_effects=DATAFLOW),
    )(*[pltpu.with_memory_space_constraint(f, pltpu.HBM) for f in fulls])
    return outs[0:4], {l: (outs[4 + 2 * i], outs[5 + 2 * i]) for i, l in enumerate(layers)}


def _gather_wait(name, l, ts, fulls, sems, after):
    def body(*refs):
        send_sems, recv_sems, f_refs = refs[4], refs[5], refs[7:11]
        x, y, c, chips = _place()
        for t in ts:
            own = _half(f_refs[t], l, t, 2 * x + y, c)
            for j, (cx, cy) in enumerate(chips):
                landed = _half(f_refs[t], l, t, 2 * cx + cy, c)
                pltpu.make_async_remote_copy(src_ref=own, dst_ref=landed, send_sem=send_sems.at[3 * t + j],
                                             recv_sem=recv_sems.at[3 * t + j], device_id=(cx, cy, c),
                                             device_id_type=MESH).wait()

    return pl.pallas_call(
        body, name=name,
        in_specs=[HBM] * 4 + [SEM, SEM, ANY], out_specs=[HBM] * 4,
        out_shape=[pltpu.HBM(s.shape, s.dtype) for s in (_full_shape(t, DEPTH, BF16) for t in range(4))],
        input_output_aliases={t: t for t in range(4)},
        compiler_params=pltpu.CompilerParams(has_side_effects=DATAFLOW),
    )(*fulls, sems[0], sems[1], after)


def _pass_on(name, l, ts, fulls):
    def body(*refs):
        f_refs, send_sems, recv_sems = refs[4:8], refs[8], refs[9]
        x, y, c, chips = _place()

        def copy(t, j, half):
            cx, cy = chips[j]
            part = _half(f_refs[t], l, t, 2 * cx + cy, half)
            return pltpu.make_async_remote_copy(src_ref=part, dst_ref=part, send_sem=send_sems.at[3 * t + j],
                                                recv_sem=recv_sems.at[3 * t + j], device_id=(x, y, 1 - c),
                                                device_id_type=MESH)

        for t in ts:
            for j in range(3):
                copy(t, j, c).start()
        for t in ts:
            for j in range(3):
                copy(t, j, 1 - c).wait_recv()
                copy(t, j, c).wait_send()

    return pl.pallas_call(
        body, name=name,
        in_specs=[ANY] * 4, out_specs=[ANY] * 4,
        out_shape=[_full_shape(t, DEPTH, BF16) for t in range(4)],
        input_output_aliases={t: t for t in range(4)},
        scratch_shapes=[pltpu.SemaphoreType.DMA((12,)), pltpu.SemaphoreType.DMA((12,))],
    )(*fulls)


def _block2d(ref, t, b):
    r, cols = LARGE_DIMS[t]
    if BLOCK_AXIS[t] == 1:
        return ref.at[pl.ds(pl.multiple_of(b * (r // 4), 16), r // 4), :]
    return ref.at[:, pl.ds(pl.multiple_of(b * (cols // 4), 128), cols // 4)]


def _block_dims(t):
    r, cols = LARGE_DIMS[t]
    return (r // 4, cols) if BLOCK_AXIS[t] == 1 else (r, cols // 4)


def _reduce_copies(ts, g_refs, r_refs, send_sems, recv_sems):
    _, _, c, chips = _place()
    return [pltpu.make_async_remote_copy(src_ref=_block2d(g_refs[i], t, 2 * cx + cy), dst_ref=r_refs[i].at[j],
                                         send_sem=send_sems.at[3 * i + j], recv_sem=recv_sems.at[3 * i + j],
                                         device_id=(cx, cy, c), device_id_type=MESH)
            for i, t in enumerate(ts) for j, (cx, cy) in enumerate(chips)]


def _reduce_start(name, ts, grads):
    n = len(ts)

    def body(*refs):
        for cp in _reduce_copies(ts, refs[n:2 * n], refs[2 * n:3 * n], refs[3 * n], refs[3 * n + 1]):
            cp.start()
        refs[3 * n + 2][...] = jnp.zeros((8, 128), F32)

    outs = pl.pallas_call(
        body, name=name,
        in_specs=[HBM] * n,
        out_specs=[HBM] * (2 * n) + [SEM, SEM, pl.BlockSpec(memory_space=pltpu.VMEM)],
        out_shape=[pltpu.HBM(g.shape, BF16) for g in grads]
        + [pltpu.HBM((3,) + _block_dims(t), BF16) for t in ts]
        + [pltpu.SemaphoreType.DMA((3 * n,)), pltpu.SemaphoreType.DMA((3 * n,)), jax.ShapeDtypeStruct((8, 128), F32)],
        input_output_aliases={i: i for i in range(n)},
        compiler_params=pltpu.CompilerParams(has_side_effects=DATAFLOW),
    )(*[pltpu.with_memory_space_constraint(g, pltpu.HBM) for g in grads])
    return outs[0:n], outs[n:2 * n], (outs[2 * n], outs[2 * n + 1]), outs[2 * n + 2]


def _reduce_wait(name, ts, grads, landing, sems, afters):
    n = len(ts)
    first_out = 2 * n + 2 + len(afters)

    def body(*refs):
        for cp in _reduce_copies(ts, refs[first_out:first_out + n], refs[first_out + n:first_out + 2 * n],
                                 refs[2 * n], refs[2 * n + 1]):
            cp.wait()

    outs = pl.pallas_call(
        body, name=name,
        in_specs=[HBM] * (2 * n) + [SEM, SEM] + [ANY] * len(afters), out_specs=[HBM] * (2 * n),
        out_shape=[pltpu.HBM(g.shape, BF16) for g in grads] + [pltpu.HBM(r.shape, BF16) for r in landing],
        input_output_aliases={i: i for i in range(2 * n)},
        compiler_params=pltpu.CompilerParams(has_side_effects=DATAFLOW),
    )(*grads, *landing, sems[0], sems[1], *afters)
    return outs[0:n], outs[n:2 * n]


def _add4(name, t, own, landed, b1):
    rb, cb = _block_dims(t)
    tm = min(256, rb)
    if BLOCK_AXIS[t] == 1:
        own_spec = pl.BlockSpec((tm, cb), lambda i, br: (br[0] * (rb // tm) + i, 0))
    else:
        own_spec = pl.BlockSpec((tm, cb), lambda i, br: (i, br[0]))

    def body(b_ref, o_ref, r0_ref, r1_ref, r2_ref, s_ref):
        del b_ref
        s_ref[...] = ((o_ref[...].astype(F32) + r0_ref[...].astype(F32))
                      + (r1_ref[...].astype(F32) + r2_ref[...].astype(F32))).astype(BF16)

    def got(j):
        return pl.BlockSpec((None, tm, cb), lambda i, br: (j, i, 0))

    return pl.pallas_call(
        body, name=name,
        grid_spec=pltpu.PrefetchScalarGridSpec(
            num_scalar_prefetch=1, grid=(rb // tm,),
            in_specs=[own_spec, got(0), got(1), got(2)],
            out_specs=pl.BlockSpec((tm, cb), lambda i, br: (i, 0))),
        out_shape=jax.ShapeDtypeStruct((rb, cb), BF16),
        compiler_params=_cp("parallel"),
    )(b1, own, landed, landed, landed)


def _swap_sib(name, sums):
    def body(*refs):
        s_refs, t_refs, send_sems, recv_sems = refs[0:4], refs[4:8], refs[8], refs[9]
        x, y, c, _ = _place()
        cps = [pltpu.make_async_remote_copy(src_ref=s_refs[t], dst_ref=t_refs[t], send_sem=send_sems.at[t],
                                            recv_sem=recv_sems.at[t], device_id=(x, y, 1 - c), device_id_type=MESH)
               for t in range(4)]
        for cp in cps:
            cp.start()
        for cp in cps:
            cp.wait()

    return pl.pallas_call(
        body, name=name,
        in_specs=[ANY] * 4, out_specs=[ANY] * 4,
        out_shape=[jax.ShapeDtypeStruct(s.shape, BF16) for s in sums],
        scratch_shapes=[pltpu.SemaphoreType.DMA((4,)), pltpu.SemaphoreType.DMA((4,))],
    )(*sums)


def _adamw_pair(name, l, s_own, s_sib, w, m, v, outs):
    rb, cb = s_own.shape
    tm = min(256, rb)

    def body(a_ref, b_ref, w_ref, m_ref, v_ref, g0, d0, m0, v0, go_ref, d_ref, mo_ref, vo_ref):
        del g0, d0, m0, v0
        gv = a_ref[...].astype(F32) + b_ref[...].astype(F32)
        go_ref[...], d_ref[...], mo_ref[...], vo_ref[...] = _adamw_math(gv, w_ref[...], m_ref[...], v_ref[...])

    part = pl.BlockSpec((tm, cb), lambda i: (i, 0))
    layer = pl.BlockSpec((None, tm, cb), lambda i: (l, i, 0))
    return pl.pallas_call(
        body, name=name, grid=(rb // tm,),
        in_specs=[part, part, layer, layer, layer] + [ANY] * 4,
        out_specs=[layer] * 4,
        out_shape=[jax.ShapeDtypeStruct((DEPTH, rb, cb), F32)] * 4,
        input_output_aliases={5 + i: i for i in range(4)},
        compiler_params=_cp("parallel"),
    )(s_own, s_sib, w, m, v, *outs)


def _all_gather8(name, v, dep):
    m_per, n = v.shape

    def body(v_ref, dep_ref, out_ref, send_sems, recv_sems, local_sem):
        del dep_ref
        x, y, c, chips = _place()
        me, sib = (x, y, c), (x, y, 1 - c)

        def rows(px, py, pc):
            return out_ref.at[pl.ds((4 * px + 2 * py + pc) * m_per, m_per), :]

        def copy(k, block, to, src=None):
            return pltpu.make_async_remote_copy(
                src_ref=rows(*block) if src is None else src, dst_ref=rows(*block),
                send_sem=send_sems.at[k], recv_sem=recv_sems.at[k], device_id=to, device_id_type=MESH)

        mine = pltpu.make_async_copy(v_ref, rows(*me), local_sem)
        mine.start()
        first = [copy(0, me, sib, src=v_ref)]
        first += [copy(1 + j, me, (*chip, c), src=v_ref) for j, chip in enumerate(chips)]
        for cp in first:
            cp.start()
        passed = [copy(4 + j, (*chip, c), sib) for j, chip in enumerate(chips)]
        for j, chip in enumerate(chips):
            copy(1 + j, (*chip, c), me).wait_recv()
            passed[j].start()
        copy(0, sib, me).wait_recv()
        for j, chip in enumerate(chips):
            copy(4 + j, (*chip, 1 - c), me).wait_recv()
        for cp in first + passed:
            cp.wait_send()
        mine.wait()

    return pl.pallas_call(
        body, name=name,
        out_shape=jax.ShapeDtypeStruct((8 * m_per, n), v.dtype),
        in_specs=[pl.BlockSpec(memory_space=pltpu.VMEM), ANY],
        out_specs=pl.BlockSpec(memory_space=pltpu.VMEM),
        scratch_shapes=[pltpu.SemaphoreType.DMA((7,)), pltpu.SemaphoreType.DMA((7,)), pltpu.SemaphoreType.DMA],
    )(v, dep)


def _sum8(name, g):
    def body(g_ref, o_ref):
        acc = g_ref[0]
        for d in range(1, 8):
            acc = acc + g_ref[d]
        o_ref[...] = acc

    return pl.pallas_call(body, name=name, out_shape=jax.ShapeDtypeStruct(g.shape[1:], F32))(g)


def _pack(parts):
    flat = []
    for a in parts:
        a = a.reshape(-1)
        flat.append(jnp.pad(a, (0, (-a.shape[0]) % 128)))
    cat = jnp.concatenate(flat)
    cat = jnp.pad(cat, (0, (-cat.shape[0]) % 1024))
    return cat.reshape(-1, 128)


def _unpack(packed, shapes):
    flat = packed.reshape(-1)
    out, at = [], 0
    for shp in shapes:
        n = 1
        for d in shp:
            n *= d
        out.append(flat[at:at + n].reshape(shp))
        at += n + (-n) % 128
    return out


def _local_step(x, target, layer_weights, on_grads, small):
    saved = []
    xin = x
    h = _rmsnorm("norm_first", x, small["norm1_g"][0:1])
    for l in range(DEPTH):
        w_in = layer_weights(l, (0,), xin)[0]
        qg = jnp.tile(small["q_norm_g"][l], 8)[None]
        kg = jnp.tile(small["k_norm_g"][l], 8)[None]
        rb = jnp.pad(small["rel_bias"][l], ((0, 0), (0, NIDX - 257)))
        bias = _bias_layout(_bias_expand(f"bias_expand_{l}", rb))
        cw = small["conv_w"][l]
        pwbd = jax.scipy.linalg.block_diag(*[small["pool_w"][l, g] for g in range(4)])
        ps = small["pool_scale"][l][None]
        p = _mm_nn(f"proj_in_{l}", h, w_in, l, F32)
        q, qt, kp, kt, vp, vt = _qkv(f"qkv_{l}", p, qg, kg)
        o = _attn_fwd(f"attn_fwd_{l}", kp, qt, vt, bias)
        w_in, w_out, w_1, w_2 = layer_weights(l, (1, 2, 3), o)
        mix = _convpool_fwd(f"convpool_fwd_{l}", p, o, cw, pwbd, ps)
        x1, h2 = _mm_res_norm(f"proj_out_{l}", mix, w_out, l, xin, small["norm2_g"][l:l + 1])
        a = _mm_nn(f"mlp1_{l}", h2, w_1, l, BF16)
        gnext = small["norm1_g"][(l + 1) % DEPTH][None]
        x2, hnext = _mm_res_norm(f"mlp2_{l}", a, w_2, l, x1, gnext, relu2=True)
        saved.append(dict(xin=xin, h=h, p=p, q=q, qt=qt, kp=kp, kt=kt, vp=vp, bias=bias, mix=mix, x1=x1, h2=h2, a=a,
                          qg=qg, kg=kg, cw=cw, pwbd=pwbd, ps=ps))
        xin, h = x2, hnext

    dx, dxb, loss = _loss_grad("loss_grad", xin, target)
    gs = {k: [None] * DEPTH for k in ("norm1_g", "q_norm_g", "k_norm_g", "rel_bias", "conv_w", "pool_w",
                                      "pool_scale", "norm2_g")}
    for l in reversed(range(DEPTH)):
        sv = saved[l]
        da = _mm_nt_relu(f"mlp2_bwd_{l}", dxb, w_2, l, sv["a"])
        g_2 = _mm_tn(f"mlp2_wgrad_{l}", sv["a"], dxb, 512, 1024, relu2=True)
        g_1 = _mm_tn(f"mlp1_wgrad_{l}", sv["h2"], da, 1024, 512)
        dep = on_grads(l, (2, 3), (g_1, g_2))
        dx1, dx1b, dg2 = _mm_nt_normbwd(f"mlp1_bwd_{l}", da, w_1, l, sv["x1"], small["norm2_g"][l:l + 1], dx, dep)
        do, dot, dmix = _proj_out_bwd(f"proj_out_bwd_{l}", dx1b, w_out, l)
        g_out = _mm_tn(f"proj_out_wgrad_{l}", sv["mix"], dx1b, 512, 1024)
        dcp, dw0, dw1, dw2, dps, dpw = _convpool_bwd(f"convpool_bwd_{l}", sv["p"], dmix, sv["cw"], sv["pwbd"], sv["ps"])
        dq, dkp, dvp, db = _attn_bwd(f"attn_bwd_{l}", sv["q"], sv["qt"], sv["kp"], sv["kt"], sv["vp"], sv["bias"],
                                     do, dot)
        drb = _bias_reduce(f"bias_reduce_{l}", _bias_unlayout(db))
        dp, dqg, dkg = _qkv_bwd(f"qkv_bwd_{l}", sv["p"], dq, dkp, dvp, dcp, sv["qg"], sv["kg"])
        g_in = _mm_tn(f"proj_in_wgrad_{l}", sv["h"], dp, 1024, 640)
        dep = on_grads(l, (0, 1), (g_in, g_out))
        dx, dxb, dg1 = _mm_nt_normbwd(f"proj_in_bwd_{l}", dp, w_in, l, sv["xin"], small["norm1_g"][l:l + 1], dx1, dep)
        gs["norm1_g"][l] = dg1[0]
        gs["q_norm_g"][l] = dqg[0, :HD]
        gs["k_norm_g"][l] = dkg[0, :HD]
        gs["rel_bias"][l] = drb[:, :257]
        gs["conv_w"][l] = jnp.concatenate([dw0, dw1, dw2], axis=0)
        gs["pool_w"][l] = jnp.stack([dpw[g * 64:(g + 1) * 64, g * 64:(g + 1) * 64] for g in range(4)])
        gs["pool_scale"][l] = dps[0]
        gs["norm2_g"][l] = dg2[0]
    gsmall = {k: jnp.stack(v) for k, v in gs.items()}
    return loss, dx, gsmall


SMALL = ("norm1_g", "q_norm_g", "k_norm_g", "rel_bias", "conv_w", "pool_w", "pool_scale", "norm2_g")
LARGE = ("w_in", "w_out", "w_mlp1", "w_mlp2")


def kernel(x, norm1_g, w_in, q_norm_g, k_norm_g, rel_bias, conv_w, pool_w, pool_scale, w_out, norm2_g, w_mlp1, w_mlp2, loss_target, m_norm1_g, m_w_in, m_q_norm_g, m_k_norm_g, m_rel_bias, m_conv_w, m_pool_w, m_pool_scale, m_w_out, m_norm2_g, m_w_mlp1, m_w_mlp2, v_norm1_g, v_w_in, v_q_norm_g, v_k_norm_g, v_rel_bias, v_conv_w, v_pool_w, v_pool_scale, v_w_out, v_norm2_g, v_w_mlp1, v_w_mlp2):
    w = dict(norm1_g=norm1_g, w_in=w_in, q_norm_g=q_norm_g, k_norm_g=k_norm_g, rel_bias=rel_bias, conv_w=conv_w,
             pool_w=pool_w, pool_scale=pool_scale, w_out=w_out, norm2_g=norm2_g, w_mlp1=w_mlp1, w_mlp2=w_mlp2)
    m = dict(norm1_g=m_norm1_g, w_in=m_w_in, q_norm_g=m_q_norm_g, k_norm_g=m_k_norm_g, rel_bias=m_rel_bias,
             conv_w=m_conv_w, pool_w=m_pool_w, pool_scale=m_pool_scale, w_out=m_w_out, norm2_g=m_norm2_g,
             w_mlp1=m_w_mlp1, w_mlp2=m_w_mlp2)
    v = dict(norm1_g=v_norm1_g, w_in=v_w_in, q_norm_g=v_q_norm_g, k_norm_g=v_k_norm_g, rel_bias=v_rel_bias,
             conv_w=v_conv_w, pool_w=v_pool_w, pool_scale=v_pool_scale, w_out=v_w_out, norm2_g=v_norm2_g,
             w_mlp1=v_w_mlp1, w_mlp2=v_w_mlp2)
    ax, ay, ac = lax.axis_index("x"), lax.axis_index("y"), lax.axis_index("c")
    b1 = jnp.reshape(2 * ax + ay, (1,)).astype(jnp.int32)

    cw_rows = _all_gather8("gather_conv_w", jnp.pad(conv_w.reshape(DEPTH * 3, 64), ((0, 4), (0, 64))), b1)
    cw_chips = [cw_rows[(4 * cx + 2 * cy) * 16:(4 * cx + 2 * cy) * 16 + 12, :64] for cx in range(2) for cy in range(2)]
    small = {n: w[n] for n in SMALL}
    small["conv_w"] = jnp.concatenate(cw_chips, axis=1).reshape(DEPTH, 3, CW)

    casts = [_cast_into_full(f"cast_{n}", t, w[n], b1, cw_rows) for t, n in enumerate(LARGE)]
    first, first_sems = _gather_start("gather_start_first", (0,), casts)
    held = [first]
    sems = dict(first_sems)

    def layer_weights(l, ts, after):
        if l > 0:
            ts = (0, 1, 2, 3) if ts == (0,) else ()
        if ts:
            tag = f"{l}_{ts[0]}"
            arrived = _gather_wait(f"gather_wait_{tag}", l, ts, held[0], sems[l], after)
            if l == 0 and ts == (0,):
                arrived, rest_sems = _gather_start("gather_start_rest", tuple(range(1, DEPTH)), arrived)
                sems.update(rest_sems)
            held[0] = _pass_on(f"pass_on_{tag}", l, ts, arrived)
        return held[0]

    flights = {}

    def await_flight(l, ts, afters):
        g, landing, sm, _ = flights[l, ts]
        flights[l, ts] = _reduce_wait(f"reduce_wait_{l}_{ts[0]}", ts, g, landing, sm, afters)

    def on_grads(l, ts, grads):
        if ts == (0, 1) and l + 1 < DEPTH:
            await_flight(l + 1, (2, 3), [grads[0]])
            await_flight(l + 1, (0, 1), [grads[0]])
        flights[l, ts] = _reduce_start(f"reduce_start_{l}_{ts[0]}", ts, grads)
        return flights[l, ts][3]

    loss_part, grad_x, gsmall = _local_step(x[0], loss_target[0], layer_weights, on_grads, small)
    loss = lax.psum(loss_part[0, 0], ("x", "y", "c"))
    order = [n for n in SMALL]
    packed = _pack([gsmall[n] for n in order])

    out = {n: [lax.empty(w[n].shape, F32) for _ in range(4)] for n in LARGE}
    for l in reversed(range(DEPTH)):
        if l == 0:
            afters = [grad_x, packed] + [out[n][0] for n in LARGE]
            await_flight(0, (2, 3), afters)
            await_flight(0, (0, 1), afters)
        sums = [None] * 4
        for ts in ((0, 1), (2, 3)):
            g, landing = flights[l, ts]
            for i, t in enumerate(ts):
                sums[t] = _add4(f"add4_{LARGE[t]}_{l}", t, g[i], landing[i], b1)
        theirs = _swap_sib(f"swap_sib_{l}", sums)
        for t, n in enumerate(LARGE):
            out[n] = _adamw_pair(f"adamw_{n}_{l}", l, sums[t], theirs[t], w[n], m[n], v[n], out[n])

    rows = packed.shape[0]
    summed = _sum8("sum_small", _all_gather8("gather_small", packed, out[LARGE[0]][0]).reshape(8, rows, 128))
    gfull = dict(zip(order, _unpack(summed, [gsmall[n].shape for n in order])))
    gfull["conv_w"] = lax.dynamic_slice_in_dim(gfull["conv_w"], (2 * ax + ay) * 64, 64, axis=2)
    res = _adamw("adamw_small", _pack([gfull[n] for n in order]), _pack([w[n] for n in order]),
                 _pack([m[n] for n in order]), _pack([v[n] for n in order]))
    for n, parts in zip(order, zip(*[_unpack(r, [w[k].shape for k in order]) for r in res])):
        out[n] = list(parts)

    names = ("norm1_g", "w_in", "q_norm_g", "k_norm_g", "rel_bias", "conv_w", "pool_w", "pool_scale", "w_out",
             "norm2_g", "w_mlp1", "w_mlp2")
    flat = [loss, grad_x[None]]
    for i in range(4):
        flat += [out[n][i] for n in names]
    return tuple(flat)
```

```python
import functools

import jax
import jax.numpy as jnp
from jax import lax
from jax.experimental import pallas as pl
from jax.experimental.pallas import tpu as pltpu

F32 = jnp.float32
BF16 = jnp.bfloat16

D = 1024
DEPTH = 4
CH = 64
NPREV = 8
KB = (NPREV + 1) * CH
PADR = NPREV * CH
HD = 64
AW = 512
CW = 256
PWD = 256
DIN = 3 * AW + 3 * CW + PWD
DFF = 4 * D
NIDX = 384
EPS = 1e-6
NEG_INF = -1e30

ADAM_LR = 0.001
ADAM_B1 = 0.9
ADAM_B2 = 0.999
ADAM_EPS = 1e-08
ADAM_WD = 0.01
ADAM_STEP = 10

VMEM_LIMIT = 52 * 1024 * 1024
MESH = pl.DeviceIdType.MESH
ANY = pl.BlockSpec(memory_space=pl.ANY)


def _cp(*sem):
    return pltpu.CompilerParams(dimension_semantics=sem, vmem_limit_bytes=VMEM_LIMIT)


def _inv_rms(x):
    return lax.rsqrt(jnp.mean(x * x, axis=-1, keepdims=True) + EPS)


def _head_mean_matrix():
    r = lax.broadcasted_iota(jnp.int32, (AW, AW), 0) // HD
    c = lax.broadcasted_iota(jnp.int32, (AW, AW), 1) // HD
    return jnp.where(r == c, 1.0 / HD, 0.0).astype(BF16)


def _two_pass_dot(x, m):
    hi = x.astype(BF16)
    lo = (x - hi.astype(F32)).astype(BF16)
    return (jnp.dot(hi, m, preferred_element_type=F32)
            + jnp.dot(lo, m, preferred_element_type=F32))


def _head_mean(x, hm):
    return _two_pass_dot(x, hm)


def _rmsnorm(name, x, g):
    s = x.shape[0]
    tm = 512

    def body(x_ref, g_ref, h_ref):
        xv = x_ref[...]
        h_ref[...] = (xv * _inv_rms(xv) * g_ref[...]).astype(BF16)

    return pl.pallas_call(
        body, name=name, grid=(s // tm,),
        in_specs=[pl.BlockSpec((tm, D), lambda i: (i, 0)), pl.BlockSpec((1, D), lambda i: (0, 0))],
        out_specs=pl.BlockSpec((tm, D), lambda i: (i, 0)),
        out_shape=jax.ShapeDtypeStruct((s, D), BF16),
        compiler_params=_cp("parallel"),
    )(x, g)


def _relu2(a):
    r = jnp.maximum(a, jnp.zeros_like(a))
    return r * r


def _mm_nn(name, a, w, l, out_dtype):
    s, k = a.shape
    n = w.shape[2]
    tm = 256

    def body(a_ref, w_ref, o_ref):
        o_ref[...] = jnp.dot(a_ref[...], w_ref[...], preferred_element_type=F32).astype(o_ref.dtype)

    return pl.pallas_call(
        body, name=name, grid=(s // tm,),
        in_specs=[pl.BlockSpec((tm, k), lambda i: (i, 0)),
                  pl.BlockSpec((None, k, n), lambda i: (l, 0, 0))],
        out_specs=pl.BlockSpec((tm, n), lambda i: (i, 0)),
        out_shape=jax.ShapeDtypeStruct((s, n), out_dtype),
        compiler_params=_cp("parallel"),
    )(a, w)


def _mm_res_norm(name, a, w, l, res, g, relu2=False):
    s, k = a.shape
    tm = 256

    def body(a_ref, w_ref, r_ref, g_ref, x_ref, h_ref):
        av = _relu2(a_ref[...]) if relu2 else a_ref[...]
        acc = r_ref[...] + jnp.dot(av, w_ref[...], preferred_element_type=F32)
        x_ref[...] = acc
        h_ref[...] = (acc * _inv_rms(acc) * g_ref[...]).astype(BF16)

    return pl.pallas_call(
        body, name=name, grid=(s // tm,),
        in_specs=[pl.BlockSpec((tm, k), lambda i: (i, 0)),
                  pl.BlockSpec((None, k, D), lambda i: (l, 0, 0)),
                  pl.BlockSpec((tm, D), lambda i: (i, 0)),
                  pl.BlockSpec((1, D), lambda i: (0, 0))],
        out_specs=[pl.BlockSpec((tm, D), lambda i: (i, 0))] * 2,
        out_shape=[jax.ShapeDtypeStruct((s, D), F32), jax.ShapeDtypeStruct((s, D), BF16)],
        compiler_params=_cp("parallel"),
    )(a, w, res, g)


def _qkv(name, p, qg, kg):
    s = p.shape[0]
    tm = PADR
    nb = s // tm

    def body(pq_ref, pk_ref, pv_ref, qg_ref, kg_ref, q_ref, qt_ref, k_ref, kt_ref, v_ref, vt_ref):
        t = pl.program_id(0)
        hm = _head_mean_matrix()

        def nrm(x, g):
            return x * lax.rsqrt(_head_mean(x * x, hm) + EPS) * g

        first = t == 0
        qq = nrm(pq_ref[...], qg_ref[...]) * 0.125
        kk = jnp.where(first, 0.0, nrm(pk_ref[...], kg_ref[...]))
        vv = jnp.where(first, 0.0, pv_ref[...])
        q_ref[...] = qq.astype(BF16)
        qt_ref[...] = qq.T.astype(BF16)
        k_ref[...] = kk.astype(BF16)
        kt_ref[...] = kk.T.astype(BF16)
        v_ref[...] = vv.astype(BF16)
        vt_ref[...] = vv.T.astype(BF16)

    def src(col):
        return pl.BlockSpec((tm, AW), lambda t: (jnp.maximum(t - 1, 0), col))

    gspec = pl.BlockSpec((1, AW), lambda t: (0, 0))
    rows = pl.BlockSpec((tm, AW), lambda t: (t, 0))
    cols = pl.BlockSpec((AW, tm), lambda t: (0, t))
    return pl.pallas_call(
        body, name=name, grid=(nb + 1,),
        in_specs=[src(0), src(1), src(2), gspec, gspec],
        out_specs=[pl.BlockSpec((tm, AW), lambda t: (jnp.maximum(t - 1, 0), 0)),
                   pl.BlockSpec((AW, tm), lambda t: (0, jnp.maximum(t - 1, 0))),
                   rows, cols, rows, cols],
        out_shape=[jax.ShapeDtypeStruct((s, AW), BF16), jax.ShapeDtypeStruct((AW, s), BF16),
                   jax.ShapeDtypeStruct((s + PADR, AW), BF16), jax.ShapeDtypeStruct((AW, s + PADR), BF16),
                   jax.ShapeDtypeStruct((s + PADR, AW), BF16), jax.ShapeDtypeStruct((AW, s + PADR), BF16)],
        compiler_params=_cp("arbitrary"),
    )(p, p, p, qg, kg)


NBAND = KB // CH
HIGHEST = lax.Precision.HIGHEST
NT_DIMS = (((1,), (1,)), ((), ()))


def _onehot_table(a):
    m = lax.broadcasted_iota(jnp.int32, (128, NIDX), 0)
    idx = lax.broadcasted_iota(jnp.int32, (128, NIDX), 1)
    rel = jnp.clip(KB - 1 - (CH * a + m), -128, 128) + 128
    return jnp.where(rel == idx, 1.0, 0.0).astype(F32)


def _onehot_diagonal():
    r = lax.broadcasted_iota(jnp.int32, (CH * CH, 128), 0)
    m = lax.broadcasted_iota(jnp.int32, (CH * CH, 128), 1)
    return jnp.where((r % CH) - (r // CH) + (CH - 1) == m, 1.0, 0.0).astype(F32)


def _bias_expand(name, rb):
    def body(rb_ref, o_ref):
        along = [lax.dot_general(rb_ref[...], _onehot_table(a), NT_DIMS, preferred_element_type=F32,
                                 precision=HIGHEST) for a in range(NBAND)]
        o_ref[...] = lax.dot_general(jnp.concatenate(along, axis=0), _onehot_diagonal(), NT_DIMS,
                                     preferred_element_type=F32, precision=HIGHEST)

    return pl.pallas_call(
        body, name=name,
        out_shape=jax.ShapeDtypeStruct((NBAND * 8, CH * CH), F32),
    )(rb)


def _bias_reduce(name, db):
    def body(db_ref, o_ref):
        along = jnp.dot(db_ref[...], _onehot_diagonal(), preferred_element_type=F32, precision=HIGHEST)
        acc = jnp.zeros((8, NIDX), F32)
        for a in range(NBAND):
            acc = acc + jnp.dot(along[8 * a:8 * a + 8, :], _onehot_table(a), preferred_element_type=F32,
                                precision=HIGHEST)
        o_ref[...] = acc

    return pl.pallas_call(
        body, name=name,
        out_shape=jax.ShapeDtypeStruct((8, NIDX), F32),
    )(db)


def _bias_layout(flat):
    b = flat.reshape(NBAND, 8, CH, CH).transpose(1, 0, 3, 2).reshape(4, 2, KB, CH)
    pair = b.transpose(0, 2, 1, 3).reshape(4, KB, 128)
    first = jnp.pad(pair, ((0, 0), (0, CH), (0, 0)), constant_values=NEG_INF)
    second = jnp.pad(pair, ((0, 0), (CH, 0), (0, 0)), constant_values=NEG_INF)
    return jnp.concatenate([first, second], axis=2)


def _bias_unlayout(dbt):
    b = dbt.reshape(4, NBAND, CH, 2, CH)
    return b.transpose(1, 0, 3, 4, 2).reshape(NBAND * 8, CH * CH)


UNIT = 2 * CH
BAND2 = KB + CH


def _pair_weights(xt):
    x = xt.astype(F32)
    row = lax.broadcasted_iota(jnp.int32, (128, UNIT), 0)
    low = lax.broadcasted_iota(jnp.int32, (128, UNIT), 1) < HD
    swapped = pltpu.roll(x, HD, 1)
    same = (row < HD) == low
    first = jnp.where(same, jnp.where(low, x, swapped), 0.0)
    second = jnp.where(same, jnp.where(low, swapped, x), 0.0)
    return jnp.concatenate([first, second], axis=1).astype(BF16)


def _pair_rows(x):
    low = lax.broadcasted_iota(jnp.int32, (CH, 128), 1) < HD
    zero = jnp.zeros((CH, 128), x.dtype)
    parts = []
    for c in range(2):
        xc = x[c * CH:(c + 1) * CH, :]
        parts += [jnp.where(low, xc, zero), jnp.where(low, zero, xc)]
    return jnp.concatenate(parts, axis=0)


def _unpair(raw):
    b0, b1 = raw[:, 0:128], raw[:, 128:256]
    row = lax.broadcasted_iota(jnp.int32, (128, 128), 0)
    low = lax.broadcasted_iota(jnp.int32, (128, 128), 1) < HD
    top = jnp.where(low, b0, pltpu.roll(b1, HD, 1))
    bottom = jnp.where(low, pltpu.roll(b0, HD, 1), b1)
    return jnp.where(row < HD, top, bottom).T


def _scores_t(kb, qw, bias2, row0, padded):
    s = jnp.dot(kb, qw, preferred_element_type=F32) + bias2
    if padded:
        s = jnp.where(row0 + lax.broadcasted_iota(jnp.int32, (BAND2, 256), 0) >= PADR, s, NEG_INF)
    return s


def _unit_loops(s, unit):
    lax.fori_loop(0, PADR // UNIT, lambda u, c: unit(u, True) or c, 0)
    lax.fori_loop(PADR // UNIT, s // UNIT, lambda u, c: unit(u, False) or c, 0, unroll=2)


def _attn_fwd(name, kp, qt, vt, bias2):
    s = qt.shape[1]
    nu = s // UNIT

    def body(k_ref, qt_ref, vt_ref, b_ref, o_ref, lse_ref):
        def unit(u, padded):
            r0 = pl.multiple_of(u * UNIT, UNIT)
            sc = _scores_t(k_ref[pl.ds(r0, BAND2), :], _pair_weights(qt_ref[:, pl.ds(r0, UNIT)]), b_ref[...],
                           r0, padded)
            top = jnp.max(sc, axis=0, keepdims=True)
            e = jnp.exp(sc - top)
            total = jnp.sum(e, axis=0, keepdims=True)
            raw = jnp.dot(vt_ref[:, pl.ds(r0, BAND2)], e.astype(BF16), preferred_element_type=F32) * (1.0 / total)
            o_ref[pl.ds(r0, UNIT), :] = _unpair(raw).astype(BF16)
            lse_ref[u] = jnp.broadcast_to(top + jnp.log(total), (8, 256))

        _unit_loops(s, unit)

    return pl.pallas_call(
        body, name=name, grid=(AW // 128,),
        in_specs=[pl.BlockSpec((s + PADR, 128), lambda h: (0, h)),
                  pl.BlockSpec((128, s), lambda h: (h, 0)),
                  pl.BlockSpec((128, s + PADR), lambda h: (h, 0)),
                  pl.BlockSpec((None, BAND2, 256), lambda h: (h, 0, 0))],
        out_specs=[pl.BlockSpec((s, 128), lambda h: (0, h)),
                   pl.BlockSpec((None, nu, 8, 256), lambda h: (h, 0, 0, 0))],
        out_shape=[jax.ShapeDtypeStruct((s, AW), BF16), jax.ShapeDtypeStruct((4, nu, 8, 256), F32)],
        compiler_params=_cp("parallel"),
    )(kp, qt, vt, bias2)


def _attn_bwd(name, q, qt, kp, kt, vp, bias2, do, dot, lse, dl):
    s = q.shape[0]
    nu = s // UNIT

    def body(q_ref, qt_ref, k_ref, kt_ref, v_ref, b_ref, do_ref, dot_ref, lse_ref, dl_ref,
             dq_ref, dk_ref, dv_ref, db_ref):
        dk_ref[...] = jnp.zeros_like(dk_ref)
        dv_ref[...] = jnp.zeros_like(dv_ref)
        db_ref[...] = jnp.zeros_like(db_ref)

        def unit(u, padded):
            r0 = pl.multiple_of(u * UNIT, UNIT)
            rows, band = pl.ds(r0, UNIT), pl.ds(r0, BAND2)
            sc = _scores_t(k_ref[band, :], _pair_weights(qt_ref[:, rows]), b_ref[...], r0, padded)
            pt = jnp.exp(sc - lse_ref[u][0:1, :])
            dpt = jnp.dot(v_ref[band, :], _pair_weights(dot_ref[:, rows]), preferred_element_type=F32)
            ds = pt * (dpt - dl_ref[u][0:1, :])
            db_ref[...] += ds[0:KB, 0:128] + ds[CH:BAND2, 128:256]
            dsb = ds.astype(BF16)
            dq_ref[rows, :] = _unpair(jnp.dot(kt_ref[:, band], dsb, preferred_element_type=F32))
            dk_ref[band, :] += jnp.dot(dsb, _pair_rows(q_ref[rows, :]), preferred_element_type=F32)
            dv_ref[band, :] += jnp.dot(pt.astype(BF16), _pair_rows(do_ref[rows, :]), preferred_element_type=F32)

        _unit_loops(s, unit)

    row_q = pl.BlockSpec((s, 128), lambda h: (0, h))
    col_q = pl.BlockSpec((128, s), lambda h: (h, 0))
    row_k = pl.BlockSpec((s + PADR, 128), lambda h: (0, h))
    col_k = pl.BlockSpec((128, s + PADR), lambda h: (h, 0))
    stat = pl.BlockSpec((None, nu, 8, 256), lambda h: (h, 0, 0, 0))
    return pl.pallas_call(
        body, name=name, grid=(AW // 128,),
        in_specs=[row_q, col_q, row_k, col_k, row_k,
                  pl.BlockSpec((None, BAND2, 256), lambda h: (h, 0, 0)), row_q, col_q, stat, stat],
        out_specs=[row_q, row_k, row_k, pl.BlockSpec((None, KB, 128), lambda h: (h, 0, 0))],
        out_shape=[jax.ShapeDtypeStruct((s, AW), F32),
                   jax.ShapeDtypeStruct((s + PADR, AW), F32),
                   jax.ShapeDtypeStruct((s + PADR, AW), F32),
                   jax.ShapeDtypeStruct((4, KB, 128), F32)],
        compiler_params=_cp("parallel"),
    )(q, qt, kp, kt, vp, bias2, do, dot, lse, dl)


def _rowsum_layout(dl, nu):
    d = dl[:, :8].reshape(nu, 2, CH, 4, 2)
    d = d.transpose(3, 0, 1, 4, 2).reshape(4, nu, 1, 256)
    return jnp.broadcast_to(d, (4, nu, 8, 256))


def _rows_before(cur, prev, k):
    row = lax.broadcasted_iota(jnp.int32, cur.shape, 0)
    return jnp.where(row >= k, pltpu.roll(cur, k, 0), pltpu.roll(prev, k, 0))


def _rows_after(cur, nxt, k):
    n = cur.shape[0]
    row = lax.broadcasted_iota(jnp.int32, cur.shape, 0)
    return jnp.where(row < n - k, pltpu.roll(cur, n - k, 0), pltpu.roll(nxt, n - k, 0))


def _pool_window_lanes():
    lg = lax.broadcasted_iota(jnp.int32, (1, PWD), 1) // 64
    return lg, jnp.where(lg == 0, 2.0, jnp.where(lg == 1, 4.0, jnp.where(lg == 2, 8.0, 16.0))).astype(F32)


def _pool_mean_minus_token(u, up, row0):
    lg, wv = _pool_window_lanes()
    sums = []
    c, p = u, up
    for k in (1, 2, 4, 8):
        c2 = c + _rows_before(c, p, k)
        p = p + pltpu.roll(p, k, 0)
        c = c2
        sums.append(c)
    win = jnp.where(lg == 0, sums[0], jnp.where(lg == 1, sums[1], jnp.where(lg == 2, sums[2], sums[3])))
    pos1 = (row0 + lax.broadcasted_iota(jnp.int32, u.shape, 0) + 1).astype(F32)
    cnt = jnp.minimum(pos1, wv)
    return win / cnt - u, cnt


def _conv_taps(z, zp, w0, w1, w2):
    z1 = _rows_before(z, zp, 1)
    z2 = _rows_before(z, zp, 2)
    return (w0 * z2 + w1 * z1) + w2 * z, z1, z2


CP_TM = 512


def _convpool_fwd(name, p, o, cw, pwbd, ps):
    s = p.shape[0]
    tm = CP_TM
    nb = s // tm

    def body(gb_ref, gc_ref, hin_ref, u_ref, gcp_ref, hinp_ref, up_ref, o_ref, cw_ref, pw_ref, ps_ref, mix_ref):
        i = pl.program_id(0)
        has_prev = i > 0
        z = gc_ref[...] * hin_ref[...]
        zp = jnp.where(has_prev, gcp_ref[...] * hinp_ref[...], 0.0)
        y3, _, _ = _conv_taps(z, zp, cw_ref[0:1, :], cw_ref[1:2, :], cw_ref[2:3, :])
        m, _ = _pool_mean_minus_token(u_ref[...], jnp.where(has_prev, up_ref[...], 0.0), i * tm)
        yp = jnp.dot(m.astype(BF16), pw_ref[...].astype(BF16), preferred_element_type=F32) * ps_ref[...]
        mix_ref[:, 0:AW] = o_ref[...]
        mix_ref[:, AW:AW + CW] = (gb_ref[...] * y3).astype(BF16)
        mix_ref[:, AW + CW:D] = yp.astype(BF16)

    def cur(col):
        return pl.BlockSpec((tm, CW), lambda i: (i, col))

    def prev(col):
        return pl.BlockSpec((tm, CW), lambda i: (jnp.maximum(i - 1, 0), col))

    def whole(a):
        return pl.BlockSpec(a.shape, lambda i: (0,) * a.ndim)

    return pl.pallas_call(
        body, name=name, grid=(nb,),
        in_specs=[cur(6), cur(7), cur(8), cur(9), prev(7), prev(8), prev(9),
                  pl.BlockSpec((tm, AW), lambda i: (i, 0)), whole(cw), whole(pwbd), whole(ps)],
        out_specs=pl.BlockSpec((tm, D), lambda i: (i, 0)),
        out_shape=jax.ShapeDtypeStruct((s, D), BF16),
        compiler_params=_cp("parallel"),
    )(p, p, p, p, p, p, p, o, cw, pwbd, ps)


def _convpool_bwd(name, p, dmix, cw, pwbd, ps):
    s = p.shape[0]
    tm = CP_TM
    nb = s // tm

    def body(gb_ref, gc_ref, hin_ref, u_ref, gcp_ref, hinp_ref, up_ref, gbn_ref, dyc_ref, dyp_ref, dycn_ref, dypn_ref,
             cw_ref, pw_ref, ps_ref, dcp_ref, dw0_ref, dw1_ref, dw2_ref, dps_ref, dpw_ref):
        i = pl.program_id(0)
        has_prev = i > 0
        has_next = i < nb - 1
        w0, w1, w2 = cw_ref[0:1, :], cw_ref[1:2, :], cw_ref[2:3, :]
        gb, gc, hin = gb_ref[...], gc_ref[...], hin_ref[...]
        dyc = dyc_ref[...]
        z = gc * hin
        zp = jnp.where(has_prev, gcp_ref[...] * hinp_ref[...], 0.0)
        y3, z1, z2 = _conv_taps(z, zp, w0, w1, w2)
        dy3 = dyc * gb
        dy3n = jnp.where(has_next, dycn_ref[...] * gbn_ref[...], 0.0)
        dz = w2 * dy3 + w1 * _rows_after(dy3, dy3n, 1) + w0 * _rows_after(dy3, dy3n, 2)
        pw = pw_ref[...].astype(BF16)
        psv = ps_ref[...]
        m, cnt = _pool_mean_minus_token(u_ref[...], jnp.where(has_prev, up_ref[...], 0.0), i * tm)
        mb = m.astype(BF16)
        dyp = dyp_ref[...]
        dmp = (dyp * psv).astype(BF16)
        dmpn = jnp.where(has_next, dypn_ref[...] * psv, 0.0).astype(BF16)
        nt = (((1,), (1,)), ((), ()))
        dm = lax.dot_general(dmp, pw, nt, preferred_element_type=F32)
        dmn = lax.dot_general(dmpn, pw, nt, preferred_element_type=F32)
        lg, wv = _pool_window_lanes()
        cc, cn = dm / cnt, dmn / wv
        sums = []
        for k in (1, 2, 4, 8):
            c2 = cc + _rows_after(cc, cn, k)
            cn = cn + pltpu.roll(cn, tm - k, 0)
            cc = c2
            sums.append(cc)
        du = jnp.where(lg == 0, sums[0], jnp.where(lg == 1, sums[1], jnp.where(lg == 2, sums[2], sums[3]))) - dm
        dcp_ref[:, 0:CW] = (dyc * y3).astype(BF16)
        dcp_ref[:, CW:2 * CW] = (dz * hin).astype(BF16)
        dcp_ref[:, 2 * CW:3 * CW] = (dz * gc).astype(BF16)
        dcp_ref[:, 3 * CW:4 * CW] = du.astype(BF16)
        parts = (jnp.sum(dy3 * z2, axis=0, keepdims=True),
                 jnp.sum(dy3 * z1, axis=0, keepdims=True),
                 jnp.sum(dy3 * z, axis=0, keepdims=True),
                 jnp.sum(dyp * jnp.dot(mb, pw, preferred_element_type=F32), axis=0, keepdims=True),
                 lax.dot_general(mb, dmp, (((0,), (0,)), ((), ())), preferred_element_type=F32))
        accs = (dw0_ref, dw1_ref, dw2_ref, dps_ref, dpw_ref)

        @pl.when(i == 0)
        def _():
            for a, v in zip(accs, parts):
                a[...] = v

        @pl.when(i > 0)
        def _():
            for a, v in zip(accs, parts):
                a[...] += v

    def cur(col):
        return pl.BlockSpec((tm, CW), lambda i: (i, col))

    def prev(col):
        return pl.BlockSpec((tm, CW), lambda i: (jnp.maximum(i - 1, 0), col))

    def nxt(col):
        return pl.BlockSpec((tm, CW), lambda i: (jnp.minimum(i + 1, nb - 1), col))

    def whole(shape):
        return pl.BlockSpec(shape, lambda i: (0,) * len(shape))

    row = jax.ShapeDtypeStruct((1, CW), F32)
    return pl.pallas_call(
        body, name=name, grid=(nb,),
        in_specs=[cur(6), cur(7), cur(8), cur(9), prev(7), prev(8), prev(9), nxt(6),
                  cur(0), cur(1), nxt(0), nxt(1), whole(cw.shape), whole(pwbd.shape), whole(ps.shape)],
        out_specs=[pl.BlockSpec((tm, D), lambda i: (i, 0)), whole((1, CW)), whole((1, CW)), whole((1, CW)),
                   whole((1, PWD)), whole((PWD, PWD))],
        out_shape=[jax.ShapeDtypeStruct((s, D), BF16), row, row, row, row,
                   jax.ShapeDtypeStruct((PWD, PWD), F32)],
        compiler_params=_cp("arbitrary"),
    )(p, p, p, p, p, p, p, p, dmix, dmix, dmix, dmix, cw, pwbd, ps)


def _qkv_bwd(name, p, dq, dkp, dvp, dcp, qg, kg):
    s = p.shape[0]
    tm = 256
    off = PADR // tm

    def body(pq_ref, pk_ref, dq_ref, dk_ref, dv_ref, dcp_ref, qg_ref, kg_ref, dp_ref, dqg_ref, dkg_ref):
        i = pl.program_id(0)
        hm = _head_mean_matrix()

        def nrm_bwd(x, g, dy):
            r = lax.rsqrt(_head_mean(x * x, hm) + EPS)
            xn = x * r
            dxn = dy * g
            dx = r * (dxn - xn * _head_mean(dxn * xn, hm))
            dg = jnp.sum(dy * xn, axis=0, keepdims=True)
            dg = (dg[:, 0:128] + dg[:, 128:256]) + (dg[:, 256:384] + dg[:, 384:512])
            return dx, dg + pltpu.roll(dg, HD, 1)

        dxq, dgq = nrm_bwd(pq_ref[...], qg_ref[...], dq_ref[...] * 0.125)
        dxk, dgk = nrm_bwd(pk_ref[...], kg_ref[...], dk_ref[...])
        dp_ref[:, 0:AW] = dxq.astype(BF16)
        dp_ref[:, AW:2 * AW] = dxk.astype(BF16)
        dp_ref[:, 2 * AW:3 * AW] = dv_ref[...].astype(BF16)
        dp_ref[:, 3 * AW:DIN] = dcp_ref[...]

        @pl.when(i == 0)
        def _():
            dqg_ref[...] = dgq
            dkg_ref[...] = dgk

        @pl.when(i > 0)
        def _():
            dqg_ref[...] += dgq
            dkg_ref[...] += dgk

    gspec = pl.BlockSpec((1, AW), lambda i: (0, 0))
    gout = pl.BlockSpec((1, 128), lambda i: (0, 0))
    return pl.pallas_call(
        body, name=name, grid=(s // tm,),
        in_specs=[pl.BlockSpec((tm, AW), lambda i: (i, 0)), pl.BlockSpec((tm, AW), lambda i: (i, 1)),
                  pl.BlockSpec((tm, AW), lambda i: (i, 0)),
                  pl.BlockSpec((tm, AW), lambda i: (i + off, 0)),
                  pl.BlockSpec((tm, AW), lambda i: (i + off, 0)),
                  pl.BlockSpec((tm, D), lambda i: (i, 0)), gspec, gspec],
        out_specs=[pl.BlockSpec((tm, DIN), lambda i: (i, 0)), gout, gout],
        out_shape=[jax.ShapeDtypeStruct((s, DIN), BF16), jax.ShapeDtypeStruct((1, 128), F32),
                   jax.ShapeDtypeStruct((1, 128), F32)],
        compiler_params=_cp("arbitrary"),
    )(p, p, dq, dkp, dvp, dcp, qg, kg)


def _loss_grad(name, y, t):
    s = y.shape[0]
    tm = 512

    def body(y_ref, t_ref, dy_ref, dyb_ref, l_ref):
        i = pl.program_id(0)
        e = y_ref[...] - t_ref[...]
        dy = e * (1.0 / D)
        dy_ref[...] = dy
        dyb_ref[...] = dy.astype(BF16)
        part = 0.5 * jnp.sum(jnp.mean(e * e, axis=-1, keepdims=True), axis=0, keepdims=True)

        @pl.when(i == 0)
        def _():
            l_ref[...] = part

        @pl.when(i > 0)
        def _():
            l_ref[...] += part

    blk = pl.BlockSpec((tm, D), lambda i: (i, 0))
    return pl.pallas_call(
        body, name=name, grid=(s // tm,),
        in_specs=[blk, blk],
        out_specs=[blk, blk, pl.BlockSpec((1, 1), lambda i: (0, 0))],
        out_shape=[jax.ShapeDtypeStruct((s, D), F32), jax.ShapeDtypeStruct((s, D), BF16),
                   jax.ShapeDtypeStruct((1, 1), F32)],
        compiler_params=_cp("arbitrary"),
    )(y, t)


def _mm_nt_relu(name, dxb, w, l, a):
    s = dxb.shape[0]
    tm = 256

    def body(d_ref, w_ref, a_ref, o_ref):
        df = lax.dot_general(d_ref[...], w_ref[...], (((1,), (1,)), ((), ())), preferred_element_type=F32)
        o_ref[...] = (df * (2.0 * jnp.maximum(a_ref[...].astype(F32), 0.0))).astype(BF16)

    return pl.pallas_call(
        body, name=name, grid=(s // tm,),
        in_specs=[pl.BlockSpec((tm, D), lambda i: (i, 0)),
                  pl.BlockSpec((None, DFF, D), lambda i: (l, 0, 0)),
                  pl.BlockSpec((tm, DFF), lambda i: (i, 0))],
        out_specs=pl.BlockSpec((tm, DFF), lambda i: (i, 0)),
        out_shape=jax.ShapeDtypeStruct((s, DFF), BF16),
        compiler_params=_cp("parallel"),
    )(dxb, w, a)


def _proj_out_bwd(name, dxb, w, l, mix):
    s = dxb.shape[0]
    tm = 512

    def body(d_ref, w_ref, o_ref, do_ref, dot_ref, dcp_ref, dl_ref):
        d = d_ref[...]
        wa, wc = w_ref[0:AW, :], w_ref[AW:D, :]
        do = lax.dot_general(d, wa, NT_DIMS, preferred_element_type=F32)
        do_ref[...] = do.astype(BF16)
        dot_ref[...] = lax.dot_general(wa, d, NT_DIMS, preferred_element_type=F32).astype(BF16)
        dcp_ref[...] = lax.dot_general(d, wc, NT_DIMS, preferred_element_type=F32)
        head = lax.broadcasted_iota(jnp.int32, (AW, 128), 0) // HD
        pick = jnp.where(head == lax.broadcasted_iota(jnp.int32, (AW, 128), 1), 1.0, 0.0).astype(BF16)
        dl_ref[...] = _two_pass_dot(do * o_ref[...].astype(F32), pick)

    return pl.pallas_call(
        body, name=name, grid=(s // tm,),
        in_specs=[pl.BlockSpec((tm, D), lambda i: (i, 0)),
                  pl.BlockSpec((None, D, D), lambda i: (l, 0, 0)),
                  pl.BlockSpec((tm, AW), lambda i: (i, 0))],
        out_specs=[pl.BlockSpec((tm, AW), lambda i: (i, 0)), pl.BlockSpec((AW, tm), lambda i: (0, i)),
                   pl.BlockSpec((tm, D - AW), lambda i: (i, 0)), pl.BlockSpec((tm, 128), lambda i: (i, 0))],
        out_shape=[jax.ShapeDtypeStruct((s, AW), BF16), jax.ShapeDtypeStruct((AW, s), BF16),
                   jax.ShapeDtypeStruct((s, D - AW), F32), jax.ShapeDtypeStruct((s, 128), F32)],
        compiler_params=_cp("parallel"),
    )(dxb, w, mix)


def _mm_nt_normbwd(name, gy, w, l, x, g, dres, dep):
    s, k = gy.shape
    tm = 256

    def body(gy_ref, w_ref, x_ref, g_ref, dr_ref, dep_ref, dx_ref, dxb_ref, dg_ref):
        del dep_ref
        i = pl.program_id(0)
        dh = lax.dot_general(gy_ref[...], w_ref[...], (((1,), (1,)), ((), ())), preferred_element_type=F32)
        xv = x_ref[...]
        r = _inv_rms(xv)
        xn = xv * r
        dxn = dh * g_ref[...]
        dx = r * (dxn - xn * jnp.mean(dxn * xn, axis=-1, keepdims=True)) + dr_ref[...]
        dx_ref[...] = dx
        dxb_ref[...] = dx.astype(BF16)
        part = jnp.sum(dh * xn, axis=0, keepdims=True)

        @pl.when(i == 0)
        def _():
            dg_ref[...] = part

        @pl.when(i > 0)
        def _():
            dg_ref[...] += part

    blk = pl.BlockSpec((tm, D), lambda i: (i, 0))
    vec = pl.BlockSpec((1, D), lambda i: (0, 0))
    return pl.pallas_call(
        body, name=name, grid=(s // tm,),
        in_specs=[pl.BlockSpec((tm, k), lambda i: (i, 0)),
                  pl.BlockSpec((None, D, k), lambda i: (l, 0, 0)), blk, vec, blk, ANY],
        out_specs=[blk, blk, vec],
        out_shape=[jax.ShapeDtypeStruct((s, D), F32), jax.ShapeDtypeStruct((s, D), BF16),
                   jax.ShapeDtypeStruct((1, D), F32)],
        compiler_params=_cp("arbitrary"),
    )(gy, w, x, g, dres, dep)


def _mm_tn(name, a, b, tma, tnb, relu2=False):
    s, m = a.shape
    n = b.shape[1]

    def body(a_ref, b_ref, o_ref):
        av = _relu2(a_ref[...]) if relu2 else a_ref[...]
        o_ref[...] = lax.dot_general(av, b_ref[...], (((0,), (0,)), ((), ())),
                                     preferred_element_type=F32).astype(BF16)

    return pl.pallas_call(
        body, name=name, grid=(m // tma, n // tnb),
        in_specs=[pl.BlockSpec((s, tma), lambda i, j: (0, i)),
                  pl.BlockSpec((s, tnb), lambda i, j: (0, j))],
        out_specs=pl.BlockSpec((tma, tnb), lambda i, j: (i, j)),
        out_shape=jax.ShapeDtypeStruct((m, n), BF16),
        compiler_params=_cp("parallel", "parallel"),
    )(a, b)


def _adamw_math(gv, wv, mv, vv):
    mn = ADAM_B1 * mv + (1.0 - ADAM_B1) * gv
    vn = ADAM_B2 * vv + (1.0 - ADAM_B2) * jnp.square(gv)
    m_hat = mn / (1.0 - ADAM_B1 ** ADAM_STEP)
    v_hat = vn / (1.0 - ADAM_B2 ** ADAM_STEP)
    return gv, -ADAM_LR * (m_hat / (jnp.sqrt(v_hat) + ADAM_EPS) + ADAM_WD * wv), mn, vn


def _adamw(name, g, w, m, v):
    r, c = g.shape
    tm = 256 if r % 256 == 0 else r

    def body(g_ref, w_ref, m_ref, v_ref, go_ref, d_ref, mo_ref, vo_ref):
        go_ref[...], d_ref[...], mo_ref[...], vo_ref[...] = _adamw_math(g_ref[...], w_ref[...], m_ref[...], v_ref[...])

    blk = pl.BlockSpec((tm, c), lambda i: (i, 0))
    return pl.pallas_call(
        body, name=name, grid=(r // tm,),
        in_specs=[blk] * 4, out_specs=[blk] * 4,
        out_shape=[jax.ShapeDtypeStruct((r, c), F32)] * 4,
        compiler_params=_cp("parallel"),
    )(g, w, m, v)


def _place():
    x, y, c = lax.axis_index("x"), lax.axis_index("y"), lax.axis_index("c")
    chips = [(1 - x, y), (x, 1 - y), (1 - x, 1 - y)]
    return x, y, c, chips


BLOCK_AXIS = (2, 1, 2, 1)
LARGE_DIMS = ((D, DIN), (D, D), (D, DFF), (DFF, D))


def _full_shape(t, layers, dtype):
    r, c = LARGE_DIMS[t]
    return jax.ShapeDtypeStruct((layers, r, c), dtype)


def _cast_into_full(name, t, shard, b1, dep):
    _, r, c = shard.shape
    tm = min(256, r)
    if BLOCK_AXIS[t] == 1:
        out_spec = pl.BlockSpec((None, tm, c), lambda l, i, br: (l, br[0] * (r // tm) + i, 0))
    else:
        out_spec = pl.BlockSpec((None, tm, c), lambda l, i, br: (l, i, br[0]))

    def body(b_ref, x_ref, dep_ref, o_ref):
        del b_ref, dep_ref
        o_ref[...] = x_ref[...].astype(BF16)

    return pl.pallas_call(
        body, name=name,
        grid_spec=pltpu.PrefetchScalarGridSpec(
            num_scalar_prefetch=1, grid=(DEPTH, r // tm),
            in_specs=[pl.BlockSpec((None, tm, c), lambda l, i, br: (l, i, 0)), ANY],
            out_specs=out_spec),
        out_shape=_full_shape(t, DEPTH, BF16),
        compiler_params=_cp("parallel", "parallel"),
    )(b1, shard, dep)


HBM = pl.BlockSpec(memory_space=pltpu.HBM)
SEM = pl.BlockSpec(memory_space=pltpu.SEMAPHORE)
DATAFLOW = pltpu.SideEffectType.DATAFLOW_SIDE_EFFECTING


def _half(ref, l, t, b, c):
    r, cols = LARGE_DIMS[t]
    if BLOCK_AXIS[t] == 1:
        n = r // 8
        return ref.at[l, pl.ds(pl.multiple_of(b * (2 * n) + c * n, 16), n), :]
    n, w = r // 2, cols // 4
    return ref.at[l, pl.ds(pl.multiple_of(c * n, 16), n), pl.ds(pl.multiple_of(b * w, 128), w)]


def _gather_start(name, layers, fulls):
    def body(*refs):
        f_refs, sems = refs[4:8], refs[8:8 + 2 * len(layers)]
        x, y, c, chips = _place()
        for i, l in enumerate(layers):
            for t in range(4):
                own = _half(f_refs[t], l, t, 2 * x + y, c)
                for j, (cx, cy) in enumerate(chips):
                    pltpu.make_async_remote_copy(src_ref=own, dst_ref=own, send_sem=sems[2 * i].at[3 * t + j],
                                                 recv_sem=sems[2 * i + 1].at[3 * t + j], device_id=(cx, cy, c),
                                                 device_id_type=MESH).start()

    outs = pl.pallas_call(
        body, name=name,
        in_specs=[HBM] * 4, out_specs=[HBM] * 4 + [SEM] * (2 * len(layers)),
        out_shape=[pltpu.HBM(s.shape, s.dtype) for s in (_full_shape(t, DEPTH, BF16) for t in range(4))]
        + [pltpu.SemaphoreType.DMA((12,))] * (2 * len(layers)),
        input_output_aliases={t: t for t in range(4)},
        compiler_params=pltpu.CompilerParams(has_side_effects=DATAFLOW),
    )(*[pltpu.with_memory_space_constraint(f, pltpu.HBM) for f in fulls])
    return outs[0:4], {l: (outs[4 + 2 * i], outs[5 + 2 * i]) for i, l in enumerate(layers)}


def _gather_wait(name, l, ts, fulls, sems, after):
    def body(*refs):
        send_sems, recv_sems, f_refs = refs[4], refs[5], refs[7:11]
        x, y, c, chips = _place()
        for t in ts:
            own = _half(f_refs[t], l, t, 2 * x + y, c)
            for j, (cx, cy) in enumerate(chips):
                landed = _half(f_refs[t], l, t, 2 * cx + cy, c)
                pltpu.make_async_remote_copy(src_ref=own, dst_ref=landed, send_sem=send_sems.at[3 * t + j],
                                             recv_sem=recv_sems.at[3 * t + j], device_id=(cx, cy, c),
                                             device_id_type=MESH).wait()

    return pl.pallas_call(
        body, name=name,
        in_specs=[HBM] * 4 + [SEM, SEM, ANY], out_specs=[HBM] * 4,
        out_shape=[pltpu.HBM(s.shape, s.dtype) for s in (_full_shape(t, DEPTH, BF16) for t in range(4))],
        input_output_aliases={t: t for t in range(4)},
        compiler_params=pltpu.CompilerParams(has_side_effects=DATAFLOW),
    )(*fulls, sems[0], sems[1], after)


def _pass_on(name, l, ts, fulls):
    def body(*refs):
        f_refs, send_sems, recv_sems = refs[4:8], refs[8], refs[9]
        x, y, c, chips = _place()

        def copy(t, j, half):
            cx, cy = chips[j]
            part = _half(f_refs[t], l, t, 2 * cx + cy, half)
            return pltpu.make_async_remote_copy(src_ref=part, dst_ref=part, send_sem=send_sems.at[3 * t + j],
                                                recv_sem=recv_sems.at[3 * t + j], device_id=(x, y, 1 - c),
                                                device_id_type=MESH)

        for t in ts:
            for j in range(3):
                copy(t, j, c).start()
        for t in ts:
            for j in range(3):
                copy(t, j, 1 - c).wait_recv()
                copy(t, j, c).wait_send()

    return pl.pallas_call(
        body, name=name,
        in_specs=[ANY] * 4, out_specs=[ANY] * 4,
        out_shape=[_full_shape(t, DEPTH, BF16) for t in range(4)],
        input_output_aliases={t: t for t in range(4)},
        scratch_shapes=[pltpu.SemaphoreType.DMA((12,)), pltpu.SemaphoreType.DMA((12,))],
    )(*fulls)


def _block2d(ref, t, b):
    r, cols = LARGE_DIMS[t]
    if BLOCK_AXIS[t] == 1:
        return ref.at[pl.ds(pl.multiple_of(b * (r // 4), 16), r // 4), :]
    return ref.at[:, pl.ds(pl.multiple_of(b * (cols // 4), 128), cols // 4)]


def _block_dims(t):
    r, cols = LARGE_DIMS[t]
    return (r // 4, cols) if BLOCK_AXIS[t] == 1 else (r, cols // 4)


def _reduce_copies(ts, g_refs, r_refs, send_sems, recv_sems):
    _, _, c, chips = _place()
    return [pltpu.make_async_remote_copy(src_ref=_block2d(g_refs[i], t, 2 * cx + cy), dst_ref=r_refs[i].at[j],
                                         send_sem=send_sems.at[3 * i + j], recv_sem=recv_sems.at[3 * i + j],
                                         device_id=(cx, cy, c), device_id_type=MESH)
            for i, t in enumerate(ts) for j, (cx, cy) in enumerate(chips)]


def _reduce_start(name, ts, grads):
    n = len(ts)

    def body(*refs):
        for cp in _reduce_copies(ts, refs[n:2 * n], refs[2 * n:3 * n], refs[3 * n], refs[3 * n + 1]):
            cp.start()
        refs[3 * n + 2][...] = jnp.zeros((8, 128), F32)

    outs = pl.pallas_call(
        body, name=name,
        in_specs=[HBM] * n,
        out_specs=[HBM] * (2 * n) + [SEM, SEM, pl.BlockSpec(memory_space=pltpu.VMEM)],
        out_shape=[pltpu.HBM(g.shape, BF16) for g in grads]
        + [pltpu.HBM((3,) + _block_dims(t), BF16) for t in ts]
        + [pltpu.SemaphoreType.DMA((3 * n,)), pltpu.SemaphoreType.DMA((3 * n,)), jax.ShapeDtypeStruct((8, 128), F32)],
        input_output_aliases={i: i for i in range(n)},
        compiler_params=pltpu.CompilerParams(has_side_effects=DATAFLOW),
    )(*[pltpu.with_memory_space_constraint(g, pltpu.HBM) for g in grads])
    return outs[0:n], outs[n:2 * n], (outs[2 * n], outs[2 * n + 1]), outs[2 * n + 2]


def _reduce_wait(name, ts, grads, landing, sems, afters):
    n = len(ts)
    first_out = 2 * n + 2 + len(afters)

    def body(*refs):
        for cp in _reduce_copies(ts, refs[first_out:first_out + n], refs[first_out + n:first_out + 2 * n],
                                 refs[2 * n], refs[2 * n + 1]):
            cp.wait()

    outs = pl.pallas_call(
        body, name=name,
        in_specs=[HBM] * (2 * n) + [SEM, SEM] + [ANY] * len(afters), out_specs=[HBM] * (2 * n),
        out_shape=[pltpu.HBM(g.shape, BF16) for g in grads] + [pltpu.HBM(r.shape, BF16) for r in landing],
        input_output_aliases={i: i for i in range(2 * n)},
        compiler_params=pltpu.CompilerParams(has_side_effects=DATAFLOW),
    )(*grads, *landing, sems[0], sems[1], *afters)
    return outs[0:n], outs[n:2 * n]


def _add4(name, t, own, landed, b1):
    rb, cb = _block_dims(t)
    tm = min(256, rb)
    if BLOCK_AXIS[t] == 1:
        own_spec = pl.BlockSpec((tm, cb), lambda i, br: (br[0] * (rb // tm) + i, 0))
    else:
        own_spec = pl.BlockSpec((tm, cb), lambda i, br: (i, br[0]))

    def body(b_ref, o_ref, r0_ref, r1_ref, r2_ref, s_ref):
        del b_ref
        s_ref[...] = ((o_ref[...].astype(F32) + r0_ref[...].astype(F32))
                      + (r1_ref[...].astype(F32) + r2_ref[...].astype(F32))).astype(BF16)

    def got(j):
        return pl.BlockSpec((None, tm, cb), lambda i, br: (j, i, 0))

    return pl.pallas_call(
        body, name=name,
        grid_spec=pltpu.PrefetchScalarGridSpec(
            num_scalar_prefetch=1, grid=(rb // tm,),
            in_specs=[own_spec, got(0), got(1), got(2)],
            out_specs=pl.BlockSpec((tm, cb), lambda i, br: (i, 0))),
        out_shape=jax.ShapeDtypeStruct((rb, cb), BF16),
        compiler_params=_cp("parallel"),
    )(b1, own, landed, landed, landed)


def _swap_sib(name, sums):
    def body(*refs):
        s_refs, t_refs, send_sems, recv_sems = refs[0:4], refs[4:8], refs[8], refs[9]
        x, y, c, _ = _place()
        cps = [pltpu.make_async_remote_copy(src_ref=s_refs[t], dst_ref=t_refs[t], send_sem=send_sems.at[t],
                                            recv_sem=recv_sems.at[t], device_id=(x, y, 1 - c), device_id_type=MESH)
               for t in range(4)]
        for cp in cps:
            cp.start()
        for cp in cps:
            cp.wait()

    return pl.pallas_call(
        body, name=name,
        in_specs=[ANY] * 4, out_specs=[ANY] * 4,
        out_shape=[jax.ShapeDtypeStruct(s.shape, BF16) for s in sums],
        scratch_shapes=[pltpu.SemaphoreType.DMA((4,)), pltpu.SemaphoreType.DMA((4,))],
    )(*sums)


def _adamw_pair(name, l, s_own, s_sib, w, m, v, outs):
    rb, cb = s_own.shape
    tm = min(256, rb)

    def body(a_ref, b_ref, w_ref, m_ref, v_ref, g0, d0, m0, v0, go_ref, d_ref, mo_ref, vo_ref):
        del g0, d0, m0, v0
        gv = a_ref[...].astype(F32) + b_ref[...].astype(F32)
        go_ref[...], d_ref[...], mo_ref[...], vo_ref[...] = _adamw_math(gv, w_ref[...], m_ref[...], v_ref[...])

    part = pl.BlockSpec((tm, cb), lambda i: (i, 0))
    layer = pl.BlockSpec((None, tm, cb), lambda i: (l, i, 0))
    return pl.pallas_call(
        body, name=name, grid=(rb // tm,),
        in_specs=[part, part, layer, layer, layer] + [ANY] * 4,
        out_specs=[layer] * 4,
        out_shape=[jax.ShapeDtypeStruct((DEPTH, rb, cb), F32)] * 4,
        input_output_aliases={5 + i: i for i in range(4)},
        compiler_params=_cp("parallel"),
    )(s_own, s_sib, w, m, v, *outs)


def _all_gather8(name, v, dep):
    m_per, n = v.shape

    def body(v_ref, dep_ref, out_ref, send_sems, recv_sems, local_sem):
        del dep_ref
        x, y, c, chips = _place()
        me, sib = (x, y, c), (x, y, 1 - c)

        def rows(px, py, pc):
            return out_ref.at[pl.ds((4 * px + 2 * py + pc) * m_per, m_per), :]

        def copy(k, block, to, src=None):
            return pltpu.make_async_remote_copy(
                src_ref=rows(*block) if src is None else src, dst_ref=rows(*block),
                send_sem=send_sems.at[k], recv_sem=recv_sems.at[k], device_id=to, device_id_type=MESH)

        mine = pltpu.make_async_copy(v_ref, rows(*me), local_sem)
        mine.start()
        first = [copy(0, me, sib, src=v_ref)]
        first += [copy(1 + j, me, (*chip, c), src=v_ref) for j, chip in enumerate(chips)]
        for cp in first:
            cp.start()
        passed = [copy(4 + j, (*chip, c), sib) for j, chip in enumerate(chips)]
        for j, chip in enumerate(chips):
            copy(1 + j, (*chip, c), me).wait_recv()
            passed[j].start()
        copy(0, sib, me).wait_recv()
        for j, chip in enumerate(chips):
            copy(4 + j, (*chip, 1 - c), me).wait_recv()
        for cp in first + passed:
            cp.wait_send()
        mine.wait()

    return pl.pallas_call(
        body, name=name,
        out_shape=jax.ShapeDtypeStruct((8 * m_per, n), v.dtype),
        in_specs=[pl.BlockSpec(memory_space=pltpu.VMEM), ANY],
        out_specs=pl.BlockSpec(memory_space=pltpu.VMEM),
        scratch_shapes=[pltpu.SemaphoreType.DMA((7,)), pltpu.SemaphoreType.DMA((7,)), pltpu.SemaphoreType.DMA],
    )(v, dep)


def _sum8(name, g):
    def body(g_ref, o_ref):
        acc = g_ref[0]
        for d in range(1, 8):
            acc = acc + g_ref[d]
        o_ref[...] = acc

    return pl.pallas_call(body, name=name, out_shape=jax.ShapeDtypeStruct(g.shape[1:], F32))(g)


def _pack(parts):
    flat = []
    for a in parts:
        a = a.reshape(-1)
        flat.append(jnp.pad(a, (0, (-a.shape[0]) % 128)))
    cat = jnp.concatenate(flat)
    cat = jnp.pad(cat, (0, (-cat.shape[0]) % 1024))
    return cat.reshape(-1, 128)


def _unpack(packed, shapes):
    flat = packed.reshape(-1)
    out, at = [], 0
    for shp in shapes:
        n = 1
        for d in shp:
            n *= d
        out.append(flat[at:at + n].reshape(shp))
        at += n + (-n) % 128
    return out


def _local_step(x, target, layer_weights, on_grads, small):
    saved = []
    xin = x
    h = _rmsnorm("norm_first", x, small["norm1_g"][0:1])
    for l in range(DEPTH):
        w_in = layer_weights(l, (0,), xin)[0]
        qg = jnp.tile(small["q_norm_g"][l], 8)[None]
        kg = jnp.tile(small["k_norm_g"][l], 8)[None]
        rb = jnp.pad(small["rel_bias"][l], ((0, 0), (0, NIDX - 257)))
        bias = _bias_layout(_bias_expand(f"bias_expand_{l}", rb))
        cw = small["conv_w"][l]
        pwbd = jax.scipy.linalg.block_diag(*[small["pool_w"][l, g] for g in range(4)])
        ps = small["pool_scale"][l][None]
        p = _mm_nn(f"proj_in_{l}", h, w_in, l, F32)
        q, qt, kp, kt, vp, vt = _qkv(f"qkv_{l}", p, qg, kg)
        o, lse = _attn_fwd(f"attn_fwd_{l}", kp, qt, vt, bias)
        w_in, w_out, w_1, w_2 = layer_weights(l, (1, 2, 3), o)
        mix = _convpool_fwd(f"convpool_fwd_{l}", p, o, cw, pwbd, ps)
        x1, h2 = _mm_res_norm(f"proj_out_{l}", mix, w_out, l, xin, small["norm2_g"][l:l + 1])
        a = _mm_nn(f"mlp1_{l}", h2, w_1, l, BF16)
        gnext = small["norm1_g"][(l + 1) % DEPTH][None]
        x2, hnext = _mm_res_norm(f"mlp2_{l}", a, w_2, l, x1, gnext, relu2=True)
        saved.append(dict(xin=xin, h=h, p=p, q=q, qt=qt, kp=kp, kt=kt, vp=vp, bias=bias, mix=mix, x1=x1, h2=h2, a=a, lse=lse,
                          qg=qg, kg=kg, cw=cw, pwbd=pwbd, ps=ps))
        xin, h = x2, hnext

    dx, dxb, loss = _loss_grad("loss_grad", xin, target)
    gs = {k: [None] * DEPTH for k in ("norm1_g", "q_norm_g", "k_norm_g", "rel_bias", "conv_w", "pool_w",
                                      "pool_scale", "norm2_g")}
    for l in reversed(range(DEPTH)):
        sv = saved[l]
        da = _mm_nt_relu(f"mlp2_bwd_{l}", dxb, w_2, l, sv["a"])
        g_2 = _mm_tn(f"mlp2_wgrad_{l}", sv["a"], dxb, 512, 1024, relu2=True)
        g_1 = _mm_tn(f"mlp1_wgrad_{l}", sv["h2"], da, 1024, 512)
        dep = on_grads(l, (2, 3), (g_1, g_2))
        dx1, dx1b, dg2 = _mm_nt_normbwd(f"mlp1_bwd_{l}", da, w_1, l, sv["x1"], small["norm2_g"][l:l + 1], dx, dep)
        do, dot, dmix, dl = _proj_out_bwd(f"proj_out_bwd_{l}", dx1b, w_out, l, sv["mix"])
        g_out = _mm_tn(f"proj_out_wgrad_{l}", sv["mix"], dx1b, 512, 1024)
        dcp, dw0, dw1, dw2, dps, dpw = _convpool_bwd(f"convpool_bwd_{l}", sv["p"], dmix, sv["cw"], sv["pwbd"], sv["ps"])
        dq, dkp, dvp, db = _attn_bwd(f"attn_bwd_{l}", sv["q"], sv["qt"], sv["kp"], sv["kt"], sv["vp"], sv["bias"],
                                     do, dot, sv["lse"], _rowsum_layout(dl, x.shape[0] // UNIT))
        drb = _bias_reduce(f"bias_reduce_{l}", _bias_unlayout(db))
        dp, dqg, dkg = _qkv_bwd(f"qkv_bwd_{l}", sv["p"], dq, dkp, dvp, dcp, sv["qg"], sv["kg"])
        g_in = _mm_tn(f"proj_in_wgrad_{l}", sv["h"], dp, 1024, 640)
        dep = on_grads(l, (0, 1), (g_in, g_out))
        dx, dxb, dg1 = _mm_nt_normbwd(f"proj_in_bwd_{l}", dp, w_in, l, sv["xin"], small["norm1_g"][l:l + 1], dx1, dep)
        gs["norm1_g"][l] = dg1[0]
        gs["q_norm_g"][l] = dqg[0, :HD]
        gs["k_norm_g"][l] = dkg[0, :HD]
        gs["rel_bias"][l] = drb[:, :257]
        gs["conv_w"][l] = jnp.concatenate([dw0, dw1, dw2], axis=0)
        gs["pool_w"][l] = jnp.stack([dpw[g * 64:(g + 1) * 64, g * 64:(g + 1) * 64] for g in range(4)])
        gs["pool_scale"][l] = dps[0]
        gs["norm2_g"][l] = dg2[0]
    gsmall = {k: jnp.stack(v) for k, v in gs.items()}
    return loss, dx, gsmall


SMALL = ("norm1_g", "q_norm_g", "k_norm_g", "rel_bias", "conv_w", "pool_w", "pool_scale", "norm2_g")
LARGE = ("w_in", "w_out", "w_mlp1", "w_mlp2")


def kernel(x, norm1_g, w_in, q_norm_g, k_norm_g, rel_bias, conv_w, pool_w, pool_scale, w_out, norm2_g, w_mlp1, w_mlp2, loss_target, m_norm1_g, m_w_in, m_q_norm_g, m_k_norm_g, m_rel_bias, m_conv_w, m_pool_w, m_pool_scale, m_w_out, m_norm2_g, m_w_mlp1, m_w_mlp2, v_norm1_g, v_w_in, v_q_norm_g, v_k_norm_g, v_rel_bias, v_conv_w, v_pool_w, v_pool_scale, v_w_out, v_norm2_g, v_w_mlp1, v_w_mlp2):
    w = dict(norm1_g=norm1_g, w_in=w_in, q_norm_g=q_norm_g, k_norm_g=k_norm_g, rel_bias=rel_bias, conv_w=conv_w,
             pool_w=pool_w, pool_scale=pool_scale, w_out=w_out, norm2_g=norm2_g, w_mlp1=w_mlp1, w_mlp2=w_mlp2)
    m = dict(norm1_g=m_norm1_g, w_in=m_w_in, q_norm_g=m_q_norm_g, k_norm_g=m_k_norm_g, rel_bias=m_rel_bias,
             conv_w=m_conv_w, pool_w=m_pool_w, pool_scale=m_pool_scale, w_out=m_w_out, norm2_g=m_norm2_g,
             w_mlp1=m_w_mlp1, w_mlp2=m_w_mlp2)
    v = dict(norm1_g=v_norm1_g, w_in=v_w_in, q_norm_g=v_q_norm_g, k_norm_g=v_k_norm_g, rel_bias=v_rel_bias,
             conv_w=v_conv_w, pool_w=v_pool_w, pool_scale=v_pool_scale, w_out=v_w_out, norm2_g=v_norm2_g,
             w_mlp1=v_w_mlp1, w_mlp2=v_w_mlp2)
    ax, ay, ac = lax.axis_index("x"), lax.axis_index("y"), lax.axis_index("c")
    b1 = jnp.reshape(2 * ax + ay, (1,)).astype(jnp.int32)

    cw_rows = _all_gather8("gather_conv_w", jnp.pad(conv_w.reshape(DEPTH * 3, 64), ((0, 4), (0, 64))), b1)
    cw_chips = [cw_rows[(4 * cx + 2 * cy) * 16:(4 * cx + 2 * cy) * 16 + 12, :64] for cx in range(2) for cy in range(2)]
    small = {n: w[n] for n in SMALL}
    small["conv_w"] = jnp.concatenate(cw_chips, axis=1).reshape(DEPTH, 3, CW)

    casts = [_cast_into_full(f"cast_{n}", t, w[n], b1, cw_rows) for t, n in enumerate(LARGE)]
    first, first_sems = _gather_start("gather_start_first", (0,), casts)
    held = [first]
    sems = dict(first_sems)

    def layer_weights(l, ts, after):
        if l > 0:
            ts = (0, 1, 2, 3) if ts == (0,) else ()
        if ts:
            tag = f"{l}_{ts[0]}"
            arrived = _gather_wait(f"gather_wait_{tag}", l, ts, held[0], sems[l], after)
            if l == 0 and ts == (0,):
                arrived, rest_sems = _gather_start("gather_start_rest", tuple(range(1, DEPTH)), arrived)
                sems.update(rest_sems)
            held[0] = _pass_on(f"pass_on_{tag}", l, ts, arrived)
        return held[0]

    flights = {}

    def await_flight(l, ts, afters):
        g, landing, sm, _ = flights[l, ts]
        flights[l, ts] = _reduce_wait(f"reduce_wait_{l}_{ts[0]}", ts, g, landing, sm, afters)

    def on_grads(l, ts, grads):
        if ts == (0, 1) and l + 1 < DEPTH:
            await_flight(l + 1, (2, 3), [grads[0]])
            await_flight(l + 1, (0, 1), [grads[0]])
        flights[l, ts] = _reduce_start(f"reduce_start_{l}_{ts[0]}", ts, grads)
        return flights[l, ts][3]

    loss_part, grad_x, gsmall = _local_step(x[0], loss_target[0], layer_weights, on_grads, small)
    loss = lax.psum(loss_part[0, 0], ("x", "y", "c"))
    order = [n for n in SMALL]
    packed = _pack([gsmall[n] for n in order])

    out = {n: [lax.empty(w[n].shape, F32) for _ in range(4)] for n in LARGE}
    for l in reversed(range(DEPTH)):
        if l == 0:
            afters = [grad_x, packed] + [out[n][0] for n in LARGE]
            await_flight(0, (2, 3), afters)
            await_flight(0, (0, 1), afters)
        sums = [None] * 4
        for ts in ((0, 1), (2, 3)):
            g, landing = flights[l, ts]
            for i, t in enumerate(ts):
                sums[t] = _add4(f"add4_{LARGE[t]}_{l}", t, g[i], landing[i], b1)
        theirs = _swap_sib(f"swap_sib_{l}", sums)
        for t, n in enumerate(LARGE):
            out[n] = _adamw_pair(f"adamw_{n}_{l}", l, sums[t], theirs[t], w[n], m[n], v[n], out[n])

    rows = packed.shape[0]
    summed = _sum8("sum_small", _all_gather8("gather_small", packed, out[LARGE[0]][0]).reshape(8, rows, 128))
    gfull = dict(zip(order, _unpack(summed, [gsmall[n].shape for n in order])))
    gfull["conv_w"] = lax.dynamic_slice_in_dim(gfull["conv_w"], (2 * ax + ay) * 64, 64, axis=2)
    res = _adamw("adamw_small", _pack([gfull[n] for n in order]), _pack([w[n] for n in order]),
                 _pack([m[n] for n in order]), _pack([v[n] for n in order]))
    for n, parts in zip(order, zip(*[_unpack(r, [w[k].shape for k in order]) for r in res])):
        out[n] = list(parts)

    names = ("norm1_g", "w_in", "q_norm_g", "k_norm_g", "rel_bias", "conv_w", "pool_w", "pool_scale", "w_out",
             "norm2_g", "w_mlp1", "w_mlp2")
    flat = [loss, grad_x[None]]
    for i in range(4):
        flat += [out[n][i] for n in names]
    return tuple(flat)
```

```python
import functools

import jax
import jax.numpy as jnp
from jax import lax
from jax.experimental import pallas as pl
from jax.experimental.pallas import tpu as pltpu

F32 = jnp.float32
BF16 = jnp.bfloat16

D = 1024
DEPTH = 4
CH = 64
NPREV = 8
KB = (NPREV + 1) * CH
PADR = NPREV * CH
HD = 64
AW = 512
CW = 256
PWD = 256
DIN = 3 * AW + 3 * CW + PWD
DFF = 4 * D
NIDX = 384
EPS = 1e-6
NEG_INF = -1e30

ADAM_LR = 0.001
ADAM_B1 = 0.9
ADAM_B2 = 0.999
ADAM_EPS = 1e-08
ADAM_WD = 0.01
ADAM_STEP = 10

VMEM_LIMIT = 52 * 1024 * 1024
MM_ROWS = 512
MESH = pl.DeviceIdType.MESH
ANY = pl.BlockSpec(memory_space=pl.ANY)


def _cp(*sem):
    return pltpu.CompilerParams(dimension_semantics=sem, vmem_limit_bytes=VMEM_LIMIT)


def _inv_rms(x):
    return lax.rsqrt(jnp.mean(x * x, axis=-1, keepdims=True) + EPS)


def _head_mean_matrix():
    r = lax.broadcasted_iota(jnp.int32, (AW, AW), 0) // HD
    c = lax.broadcasted_iota(jnp.int32, (AW, AW), 1) // HD
    return jnp.where(r == c, 1.0 / HD, 0.0).astype(BF16)


def _two_pass_dot(x, m):
    hi = x.astype(BF16)
    lo = (x - hi.astype(F32)).astype(BF16)
    return (jnp.dot(hi, m, preferred_element_type=F32)
            + jnp.dot(lo, m, preferred_element_type=F32))


def _head_mean(x, hm):
    return _two_pass_dot(x, hm)


def _rmsnorm(name, x, g):
    s = x.shape[0]
    tm = 512

    def body(x_ref, g_ref, h_ref):
        xv = x_ref[...]
        h_ref[...] = (xv * _inv_rms(xv) * g_ref[...]).astype(BF16)

    return pl.pallas_call(
        body, name=name, grid=(s // tm,),
        in_specs=[pl.BlockSpec((tm, D), lambda i: (i, 0)), pl.BlockSpec((1, D), lambda i: (0, 0))],
        out_specs=pl.BlockSpec((tm, D), lambda i: (i, 0)),
        out_shape=jax.ShapeDtypeStruct((s, D), BF16),
        compiler_params=_cp("parallel"),
    )(x, g)


def _relu2(a):
    r = jnp.maximum(a, jnp.zeros_like(a))
    return r * r


def _mm_nn(name, a, w, l, out_dtype):
    s, k = a.shape
    n = w.shape[2]
    tm = MM_ROWS

    def body(a_ref, w_ref, o_ref):
        o_ref[...] = jnp.dot(a_ref[...], w_ref[...], preferred_element_type=F32).astype(o_ref.dtype)

    return pl.pallas_call(
        body, name=name, grid=(s // tm,),
        in_specs=[pl.BlockSpec((tm, k), lambda i: (i, 0)),
                  pl.BlockSpec((None, k, n), lambda i: (l, 0, 0))],
        out_specs=pl.BlockSpec((tm, n), lambda i: (i, 0)),
        out_shape=jax.ShapeDtypeStruct((s, n), out_dtype),
        compiler_params=_cp("parallel"),
    )(a, w)


def _mm_res_norm(name, a, w, l, res, g, relu2=False):
    s, k = a.shape
    tm = MM_ROWS

    def body(a_ref, w_ref, r_ref, g_ref, x_ref, h_ref):
        av = _relu2(a_ref[...]) if relu2 else a_ref[...]
        acc = r_ref[...] + jnp.dot(av, w_ref[...], preferred_element_type=F32)
        x_ref[...] = acc
        h_ref[...] = (acc * _inv_rms(acc) * g_ref[...]).astype(BF16)

    return pl.pallas_call(
        body, name=name, grid=(s // tm,),
        in_specs=[pl.BlockSpec((tm, k), lambda i: (i, 0)),
                  pl.BlockSpec((None, k, D), lambda i: (l, 0, 0)),
                  pl.BlockSpec((tm, D), lambda i: (i, 0)),
                  pl.BlockSpec((1, D), lambda i: (0, 0))],
        out_specs=[pl.BlockSpec((tm, D), lambda i: (i, 0))] * 2,
        out_shape=[jax.ShapeDtypeStruct((s, D), F32), jax.ShapeDtypeStruct((s, D), BF16)],
        compiler_params=_cp("parallel"),
    )(a, w, res, g)


def _qkv(name, p, qg, kg):
    s = p.shape[0]
    tm = PADR
    nb = s // tm

    def body(pq_ref, pk_ref, pv_ref, qg_ref, kg_ref, q_ref, qt_ref, k_ref, kt_ref, v_ref, vt_ref):
        t = pl.program_id(0)
        hm = _head_mean_matrix()

        def nrm(x, g):
            return x * lax.rsqrt(_head_mean(x * x, hm) + EPS) * g

        first = t == 0
        qq = nrm(pq_ref[...], qg_ref[...]) * 0.125
        kk = jnp.where(first, 0.0, nrm(pk_ref[...], kg_ref[...]))
        vv = jnp.where(first, 0.0, pv_ref[...])
        q_ref[...] = qq.astype(BF16)
        qt_ref[...] = qq.T.astype(BF16)
        k_ref[...] = kk.astype(BF16)
        kt_ref[...] = kk.T.astype(BF16)
        v_ref[...] = vv.astype(BF16)
        vt_ref[...] = vv.T.astype(BF16)

    def src(col):
        return pl.BlockSpec((tm, AW), lambda t: (jnp.maximum(t - 1, 0), col))

    gspec = pl.BlockSpec((1, AW), lambda t: (0, 0))
    rows = pl.BlockSpec((tm, AW), lambda t: (t, 0))
    cols = pl.BlockSpec((AW, tm), lambda t: (0, t))
    return pl.pallas_call(
        body, name=name, grid=(nb + 1,),
        in_specs=[src(0), src(1), src(2), gspec, gspec],
        out_specs=[pl.BlockSpec((tm, AW), lambda t: (jnp.maximum(t - 1, 0), 0)),
                   pl.BlockSpec((AW, tm), lambda t: (0, jnp.maximum(t - 1, 0))),
                   rows, cols, rows, cols],
        out_shape=[jax.ShapeDtypeStruct((s, AW), BF16), jax.ShapeDtypeStruct((AW, s), BF16),
                   jax.ShapeDtypeStruct((s + PADR, AW), BF16), jax.ShapeDtypeStruct((AW, s + PADR), BF16),
                   jax.ShapeDtypeStruct((s + PADR, AW), BF16), jax.ShapeDtypeStruct((AW, s + PADR), BF16)],
        compiler_params=_cp("arbitrary"),
    )(p, p, p, qg, kg)


NBAND = KB // CH
HIGHEST = lax.Precision.HIGHEST
NT_DIMS = (((1,), (1,)), ((), ()))


def _onehot_table(a):
    m = lax.broadcasted_iota(jnp.int32, (128, NIDX), 0)
    idx = lax.broadcasted_iota(jnp.int32, (128, NIDX), 1)
    rel = jnp.clip(KB - 1 - (CH * a + m), -128, 128) + 128
    return jnp.where(rel == idx, 1.0, 0.0).astype(F32)


def _onehot_diagonal():
    r = lax.broadcasted_iota(jnp.int32, (CH * CH, 128), 0)
    m = lax.broadcasted_iota(jnp.int32, (CH * CH, 128), 1)
    return jnp.where((r % CH) - (r // CH) + (CH - 1) == m, 1.0, 0.0).astype(F32)


def _bias_expand(name, rb):
    def body(rb_ref, o_ref):
        along = [lax.dot_general(rb_ref[...], _onehot_table(a), NT_DIMS, preferred_element_type=F32,
                                 precision=HIGHEST) for a in range(NBAND)]
        o_ref[...] = lax.dot_general(jnp.concatenate(along, axis=0), _onehot_diagonal(), NT_DIMS,
                                     preferred_element_type=F32, precision=HIGHEST)

    return pl.pallas_call(
        body, name=name, grid=(DEPTH,),
        in_specs=[pl.BlockSpec((None, 8, NIDX), lambda l: (l, 0, 0))],
        out_specs=pl.BlockSpec((None, NBAND * 8, CH * CH), lambda l: (l, 0, 0)),
        out_shape=jax.ShapeDtypeStruct((DEPTH, NBAND * 8, CH * CH), F32),
        compiler_params=_cp("parallel"),
    )(rb)


def _bias_reduce(name, db):
    def body(db_ref, o_ref):
        along = jnp.dot(db_ref[...], _onehot_diagonal(), preferred_element_type=F32, precision=HIGHEST)
        acc = jnp.zeros((8, NIDX), F32)
        for a in range(NBAND):
            acc = acc + jnp.dot(along[8 * a:8 * a + 8, :], _onehot_table(a), preferred_element_type=F32,
                                precision=HIGHEST)
        o_ref[...] = acc

    return pl.pallas_call(
        body, name=name, grid=(DEPTH,),
        in_specs=[pl.BlockSpec((None, NBAND * 8, CH * CH), lambda l: (l, 0, 0))],
        out_specs=pl.BlockSpec((None, 8, NIDX), lambda l: (l, 0, 0)),
        out_shape=jax.ShapeDtypeStruct((DEPTH, 8, NIDX), F32),
        compiler_params=_cp("parallel"),
    )(db)


def _bias_layout(flat):
    b = flat.reshape(DEPTH, NBAND, 8, CH, CH).transpose(0, 2, 1, 4, 3).reshape(DEPTH, 4, 2, KB, CH)
    pair = b.transpose(0, 1, 3, 2, 4).reshape(DEPTH, 4, KB, 128)
    first = jnp.pad(pair, ((0, 0), (0, 0), (0, CH), (0, 0)), constant_values=NEG_INF)
    second = jnp.pad(pair, ((0, 0), (0, 0), (CH, 0), (0, 0)), constant_values=NEG_INF)
    return jnp.concatenate([first, second], axis=3)


def _bias_unlayout(dbt):
    b = dbt.reshape(DEPTH, 4, NBAND, CH, 2, CH)
    return b.transpose(0, 2, 1, 4, 5, 3).reshape(DEPTH, NBAND * 8, CH * CH)


UNIT = 2 * CH
BAND2 = KB + CH


def _pair_weights(xt):
    x = xt.astype(F32)
    row = lax.broadcasted_iota(jnp.int32, (128, UNIT), 0)
    low = lax.broadcasted_iota(jnp.int32, (128, UNIT), 1) < HD
    swapped = pltpu.roll(x, HD, 1)
    same = (row < HD) == low
    first = jnp.where(same, jnp.where(low, x, swapped), 0.0)
    second = jnp.where(same, jnp.where(low, swapped, x), 0.0)
    return jnp.concatenate([first, second], axis=1).astype(BF16)


def _pair_rows(x):
    low = lax.broadcasted_iota(jnp.int32, (CH, 128), 1) < HD
    zero = jnp.zeros((CH, 128), x.dtype)
    parts = []
    for c in range(2):
        xc = x[c * CH:(c + 1) * CH, :]
        parts += [jnp.where(low, xc, zero), jnp.where(low, zero, xc)]
    return jnp.concatenate(parts, axis=0)


def _unpair(raw):
    b0, b1 = raw[:, 0:128], raw[:, 128:256]
    row = lax.broadcasted_iota(jnp.int32, (128, 128), 0)
    low = lax.broadcasted_iota(jnp.int32, (128, 128), 1) < HD
    top = jnp.where(low, b0, pltpu.roll(b1, HD, 1))
    bottom = jnp.where(low, pltpu.roll(b0, HD, 1), b1)
    return jnp.where(row < HD, top, bottom).T


def _scores_t(kb, qw, bias2, row0, padded):
    s = jnp.dot(kb, qw, preferred_element_type=F32) + bias2
    if padded:
        s = jnp.where(row0 + lax.broadcasted_iota(jnp.int32, (BAND2, 256), 0) >= PADR, s, NEG_INF)
    return s


def _unit_loops(s, unit):
    lax.fori_loop(0, PADR // UNIT, lambda u, c: unit(u, True) or c, 0)
    lax.fori_loop(PADR // UNIT, s // UNIT, lambda u, c: unit(u, False) or c, 0, unroll=2)


def _attn_fwd(name, kp, qt, vt, bias2):
    s = qt.shape[1]
    nu = s // UNIT

    def body(k_ref, qt_ref, vt_ref, b_ref, o_ref, lse_ref):
        def unit(u, padded):
            r0 = pl.multiple_of(u * UNIT, UNIT)
            sc = _scores_t(k_ref[pl.ds(r0, BAND2), :], _pair_weights(qt_ref[:, pl.ds(r0, UNIT)]), b_ref[...],
                           r0, padded)
            top = jnp.max(sc, axis=0, keepdims=True)
            e = jnp.exp(sc - top)
            total = jnp.sum(e, axis=0, keepdims=True)
            raw = jnp.dot(vt_ref[:, pl.ds(r0, BAND2)], e.astype(BF16), preferred_element_type=F32) * (1.0 / total)
            o_ref[pl.ds(r0, UNIT), :] = _unpair(raw).astype(BF16)
            lse_ref[u] = jnp.broadcast_to(top + jnp.log(total), (8, 256))

        _unit_loops(s, unit)

    return pl.pallas_call(
        body, name=name, grid=(AW // 128,),
        in_specs=[pl.BlockSpec((s + PADR, 128), lambda h: (0, h)),
                  pl.BlockSpec((128, s), lambda h: (h, 0)),
                  pl.BlockSpec((128, s + PADR), lambda h: (h, 0)),
                  pl.BlockSpec((None, BAND2, 256), lambda h: (h, 0, 0))],
        out_specs=[pl.BlockSpec((s, 128), lambda h: (0, h)),
                   pl.BlockSpec((None, nu, 8, 256), lambda h: (h, 0, 0, 0))],
        out_shape=[jax.ShapeDtypeStruct((s, AW), BF16), jax.ShapeDtypeStruct((4, nu, 8, 256), F32)],
        compiler_params=_cp("parallel"),
    )(kp, qt, vt, bias2)


def _attn_bwd(name, q, qt, kp, kt, vp, bias2, do, dot, lse, dl):
    s = q.shape[0]
    nu = s // UNIT

    def body(q_ref, qt_ref, k_ref, kt_ref, v_ref, b_ref, do_ref, dot_ref, lse_ref, dl_ref,
             dq_ref, dk_ref, dv_ref, db_ref):
        dk_ref[...] = jnp.zeros_like(dk_ref)
        dv_ref[...] = jnp.zeros_like(dv_ref)
        db_ref[...] = jnp.zeros_like(db_ref)

        def unit(u, padded):
            r0 = pl.multiple_of(u * UNIT, UNIT)
            rows, band = pl.ds(r0, UNIT), pl.ds(r0, BAND2)
            sc = _scores_t(k_ref[band, :], _pair_weights(qt_ref[:, rows]), b_ref[...], r0, padded)
            pt = jnp.exp(sc - lse_ref[u][0:1, :])
            dpt = jnp.dot(v_ref[band, :], _pair_weights(dot_ref[:, rows]), preferred_element_type=F32)
            ds = pt * (dpt - dl_ref[u][0:1, :])
            db_ref[...] += ds[0:KB, 0:128] + ds[CH:BAND2, 128:256]
            dsb = ds.astype(BF16)
            dq_ref[rows, :] = _unpair(jnp.dot(kt_ref[:, band], dsb, preferred_element_type=F32))
            dk_ref[band, :] += jnp.dot(dsb, _pair_rows(q_ref[rows, :]), preferred_element_type=F32)
            dv_ref[band, :] += jnp.dot(pt.astype(BF16), _pair_rows(do_ref[rows, :]), preferred_element_type=F32)

        _unit_loops(s, unit)

    row_q = pl.BlockSpec((s, 128), lambda h: (0, h))
    col_q = pl.BlockSpec((128, s), lambda h: (h, 0))
    row_k = pl.BlockSpec((s + PADR, 128), lambda h: (0, h))
    col_k = pl.BlockSpec((128, s + PADR), lambda h: (h, 0))
    stat = pl.BlockSpec((None, nu, 8, 256), lambda h: (h, 0, 0, 0))
    return pl.pallas_call(
        body, name=name, grid=(AW // 128,),
        in_specs=[row_q, col_q, row_k, col_k, row_k,
                  pl.BlockSpec((None, BAND2, 256), lambda h: (h, 0, 0)), row_q, col_q, stat, stat],
        out_specs=[row_q, row_k, row_k, pl.BlockSpec((None, KB, 128), lambda h: (h, 0, 0))],
        out_shape=[jax.ShapeDtypeStruct((s, AW), F32),
                   jax.ShapeDtypeStruct((s + PADR, AW), F32),
                   jax.ShapeDtypeStruct((s + PADR, AW), F32),
                   jax.ShapeDtypeStruct((4, KB, 128), F32)],
        compiler_params=_cp("parallel"),
    )(q, qt, kp, kt, vp, bias2, do, dot, lse, dl)


def _rowsum_layout(dl, nu):
    d = dl[:, :8].reshape(nu, 2, CH, 4, 2)
    d = d.transpose(3, 0, 1, 4, 2).reshape(4, nu, 1, 256)
    return jnp.broadcast_to(d, (4, nu, 8, 256))


def _rows_before(cur, prev, k):
    row = lax.broadcasted_iota(jnp.int32, cur.shape, 0)
    return jnp.where(row >= k, pltpu.roll(cur, k, 0), pltpu.roll(prev, k, 0))


def _rows_after(cur, nxt, k):
    n = cur.shape[0]
    row = lax.broadcasted_iota(jnp.int32, cur.shape, 0)
    return jnp.where(row < n - k, pltpu.roll(cur, n - k, 0), pltpu.roll(nxt, n - k, 0))


def _pool_window_lanes():
    lg = lax.broadcasted_iota(jnp.int32, (1, PWD), 1) // 64
    return lg, jnp.where(lg == 0, 2.0, jnp.where(lg == 1, 4.0, jnp.where(lg == 2, 8.0, 16.0))).astype(F32)


def _pool_mean_minus_token(u, up, row0):
    lg, wv = _pool_window_lanes()
    sums = []
    c, p = u, up
    for k in (1, 2, 4, 8):
        c2 = c + _rows_before(c, p, k)
        p = p + pltpu.roll(p, k, 0)
        c = c2
        sums.append(c)
    win = jnp.where(lg == 0, sums[0], jnp.where(lg == 1, sums[1], jnp.where(lg == 2, sums[2], sums[3])))
    pos1 = (row0 + lax.broadcasted_iota(jnp.int32, u.shape, 0) + 1).astype(F32)
    cnt = jnp.minimum(pos1, wv)
    return win / cnt - u, cnt


def _conv_taps(z, zp, w0, w1, w2):
    z1 = _rows_before(z, zp, 1)
    z2 = _rows_before(z, zp, 2)
    return (w0 * z2 + w1 * z1) + w2 * z, z1, z2


CP_TM = 512


def _convpool_fwd(name, p, o, cw, pwbd, ps):
    s = p.shape[0]
    tm = CP_TM
    nb = s // tm

    def body(gb_ref, gc_ref, hin_ref, u_ref, gcp_ref, hinp_ref, up_ref, o_ref, cw_ref, pw_ref, ps_ref, mix_ref):
        i = pl.program_id(0)
        has_prev = i > 0
        z = gc_ref[...] * hin_ref[...]
        zp = jnp.where(has_prev, gcp_ref[...] * hinp_ref[...], 0.0)
        y3, _, _ = _conv_taps(z, zp, cw_ref[0:1, :], cw_ref[1:2, :], cw_ref[2:3, :])
        m, _ = _pool_mean_minus_token(u_ref[...], jnp.where(has_prev, up_ref[...], 0.0), i * tm)
        yp = jnp.dot(m.astype(BF16), pw_ref[...].astype(BF16), preferred_element_type=F32) * ps_ref[...]
        mix_ref[:, 0:AW] = o_ref[...]
        mix_ref[:, AW:AW + CW] = (gb_ref[...] * y3).astype(BF16)
        mix_ref[:, AW + CW:D] = yp.astype(BF16)

    def cur(col):
        return pl.BlockSpec((tm, CW), lambda i: (i, col))

    def prev(col):
        return pl.BlockSpec((tm, CW), lambda i: (jnp.maximum(i - 1, 0), col))

    def whole(a):
        return pl.BlockSpec(a.shape, lambda i: (0,) * a.ndim)

    return pl.pallas_call(
        body, name=name, grid=(nb,),
        in_specs=[cur(6), cur(7), cur(8), cur(9), prev(7), prev(8), prev(9),
                  pl.BlockSpec((tm, AW), lambda i: (i, 0)), whole(cw), whole(pwbd), whole(ps)],
        out_specs=pl.BlockSpec((tm, D), lambda i: (i, 0)),
        out_shape=jax.ShapeDtypeStruct((s, D), BF16),
        compiler_params=_cp("parallel"),
    )(p, p, p, p, p, p, p, o, cw, pwbd, ps)


def _convpool_bwd(name, p, dmix, cw, pwbd, ps):
    s = p.shape[0]
    tm = CP_TM
    nb = s // tm

    def body(gb_ref, gc_ref, hin_ref, u_ref, gcp_ref, hinp_ref, up_ref, gbn_ref, dyc_ref, dyp_ref, dycn_ref, dypn_ref,
             cw_ref, pw_ref, ps_ref, dcp_ref, dw0_ref, dw1_ref, dw2_ref, dps_ref, dpw_ref):
        i = pl.program_id(0)
        has_prev = i > 0
        has_next = i < nb - 1
        w0, w1, w2 = cw_ref[0:1, :], cw_ref[1:2, :], cw_ref[2:3, :]
        gb, gc, hin = gb_ref[...], gc_ref[...], hin_ref[...]
        dyc = dyc_ref[...]
        z = gc * hin
        zp = jnp.where(has_prev, gcp_ref[...] * hinp_ref[...], 0.0)
        y3, z1, z2 = _conv_taps(z, zp, w0, w1, w2)
        dy3 = dyc * gb
        dy3n = jnp.where(has_next, dycn_ref[...] * gbn_ref[...], 0.0)
        dz = w2 * dy3 + w1 * _rows_after(dy3, dy3n, 1) + w0 * _rows_after(dy3, dy3n, 2)
        pw = pw_ref[...].astype(BF16)
        psv = ps_ref[...]
        m, cnt = _pool_mean_minus_token(u_ref[...], jnp.where(has_prev, up_ref[...], 0.0), i * tm)
        mb = m.astype(BF16)
        dyp = dyp_ref[...]
        dmp = (dyp * psv).astype(BF16)
        dmpn = jnp.where(has_next, dypn_ref[...] * psv, 0.0).astype(BF16)
        nt = (((1,), (1,)), ((), ()))
        dm = lax.dot_general(dmp, pw, nt, preferred_element_type=F32)
        dmn = lax.dot_general(dmpn, pw, nt, preferred_element_type=F32)
        lg, wv = _pool_window_lanes()
        cc, cn = dm / cnt, dmn / wv
        sums = []
        for k in (1, 2, 4, 8):
            c2 = cc + _rows_after(cc, cn, k)
            cn = cn + pltpu.roll(cn, tm - k, 0)
            cc = c2
            sums.append(cc)
        du = jnp.where(lg == 0, sums[0], jnp.where(lg == 1, sums[1], jnp.where(lg == 2, sums[2], sums[3]))) - dm
        dcp_ref[:, 0:CW] = (dyc * y3).astype(BF16)
        dcp_ref[:, CW:2 * CW] = (dz * hin).astype(BF16)
        dcp_ref[:, 2 * CW:3 * CW] = (dz * gc).astype(BF16)
        dcp_ref[:, 3 * CW:4 * CW] = du.astype(BF16)
        parts = (jnp.sum(dy3 * z2, axis=0, keepdims=True),
                 jnp.sum(dy3 * z1, axis=0, keepdims=True),
                 jnp.sum(dy3 * z, axis=0, keepdims=True),
                 jnp.sum(dyp * jnp.dot(mb, pw, preferred_element_type=F32), axis=0, keepdims=True),
                 lax.dot_general(mb, dmp, (((0,), (0,)), ((), ())), preferred_element_type=F32))
        accs = (dw0_ref, dw1_ref, dw2_ref, dps_ref, dpw_ref)

        @pl.when(i == 0)
        def _():
            for a, v in zip(accs, parts):
                a[...] = v

        @pl.when(i > 0)
        def _():
            for a, v in zip(accs, parts):
                a[...] += v

    def cur(col):
        return pl.BlockSpec((tm, CW), lambda i: (i, col))

    def prev(col):
        return pl.BlockSpec((tm, CW), lambda i: (jnp.maximum(i - 1, 0), col))

    def nxt(col):
        return pl.BlockSpec((tm, CW), lambda i: (jnp.minimum(i + 1, nb - 1), col))

    def whole(shape):
        return pl.BlockSpec(shape, lambda i: (0,) * len(shape))

    row = jax.ShapeDtypeStruct((1, CW), F32)
    return pl.pallas_call(
        body, name=name, grid=(nb,),
        in_specs=[cur(6), cur(7), cur(8), cur(9), prev(7), prev(8), prev(9), nxt(6),
                  cur(0), cur(1), nxt(0), nxt(1), whole(cw.shape), whole(pwbd.shape), whole(ps.shape)],
        out_specs=[pl.BlockSpec((tm, D), lambda i: (i, 0)), whole((1, CW)), whole((1, CW)), whole((1, CW)),
                   whole((1, PWD)), whole((PWD, PWD))],
        out_shape=[jax.ShapeDtypeStruct((s, D), BF16), row, row, row, row,
                   jax.ShapeDtypeStruct((PWD, PWD), F32)],
        compiler_params=_cp("arbitrary"),
    )(p, p, p, p, p, p, p, p, dmix, dmix, dmix, dmix, cw, pwbd, ps)


def _qkv_bwd(name, p, dq, dkp, dvp, dcp, qg, kg):
    s = p.shape[0]
    tm = 256
    off = PADR // tm

    def body(pq_ref, pk_ref, dq_ref, dk_ref, dv_ref, dcp_ref, qg_ref, kg_ref, dp_ref, dqg_ref, dkg_ref):
        i = pl.program_id(0)
        hm = _head_mean_matrix()

        def nrm_bwd(x, g, dy):
            r = lax.rsqrt(_head_mean(x * x, hm) + EPS)
            xn = x * r
            dxn = dy * g
            dx = r * (dxn - xn * _head_mean(dxn * xn, hm))
            dg = jnp.sum(dy * xn, axis=0, keepdims=True)
            dg = (dg[:, 0:128] + dg[:, 128:256]) + (dg[:, 256:384] + dg[:, 384:512])
            return dx, dg + pltpu.roll(dg, HD, 1)

        dxq, dgq = nrm_bwd(pq_ref[...], qg_ref[...], dq_ref[...] * 0.125)
        dxk, dgk = nrm_bwd(pk_ref[...], kg_ref[...], dk_ref[...])
        dp_ref[:, 0:AW] = dxq.astype(BF16)
        dp_ref[:, AW:2 * AW] = dxk.astype(BF16)
        dp_ref[:, 2 * AW:3 * AW] = dv_ref[...].astype(BF16)
        dp_ref[:, 3 * AW:DIN] = dcp_ref[...]

        @pl.when(i == 0)
        def _():
            dqg_ref[...] = dgq
            dkg_ref[...] = dgk

        @pl.when(i > 0)
        def _():
            dqg_ref[...] += dgq
            dkg_ref[...] += dgk

    gspec = pl.BlockSpec((1, AW), lambda i: (0, 0))
    gout = pl.BlockSpec((1, 128), lambda i: (0, 0))
    return pl.pallas_call(
        body, name=name, grid=(s // tm,),
        in_specs=[pl.BlockSpec((tm, AW), lambda i: (i, 0)), pl.BlockSpec((tm, AW), lambda i: (i, 1)),
                  pl.BlockSpec((tm, AW), lambda i: (i, 0)),
                  pl.BlockSpec((tm, AW), lambda i: (i + off, 0)),
                  pl.BlockSpec((tm, AW), lambda i: (i + off, 0)),
                  pl.BlockSpec((tm, D), lambda i: (i, 0)), gspec, gspec],
        out_specs=[pl.BlockSpec((tm, DIN), lambda i: (i, 0)), gout, gout],
        out_shape=[jax.ShapeDtypeStruct((s, DIN), BF16), jax.ShapeDtypeStruct((1, 128), F32),
                   jax.ShapeDtypeStruct((1, 128), F32)],
        compiler_params=_cp("arbitrary"),
    )(p, p, dq, dkp, dvp, dcp, qg, kg)


def _loss_grad(name, y, t):
    s = y.shape[0]
    tm = 512

    def body(y_ref, t_ref, dy_ref, dyb_ref, l_ref):
        i = pl.program_id(0)
        e = y_ref[...] - t_ref[...]
        dy = e * (1.0 / D)
        dy_ref[...] = dy
        dyb_ref[...] = dy.astype(BF16)
        part = 0.5 * jnp.sum(jnp.mean(e * e, axis=-1, keepdims=True), axis=0, keepdims=True)

        @pl.when(i == 0)
        def _():
            l_ref[...] = part

        @pl.when(i > 0)
        def _():
            l_ref[...] += part

    blk = pl.BlockSpec((tm, D), lambda i: (i, 0))
    return pl.pallas_call(
        body, name=name, grid=(s // tm,),
        in_specs=[blk, blk],
        out_specs=[blk, blk, pl.BlockSpec((1, 1), lambda i: (0, 0))],
        out_shape=[jax.ShapeDtypeStruct((s, D), F32), jax.ShapeDtypeStruct((s, D), BF16),
                   jax.ShapeDtypeStruct((1, 1), F32)],
        compiler_params=_cp("arbitrary"),
    )(y, t)


def _mm_nt_relu(name, dxb, w, l, a):
    s = dxb.shape[0]
    tm = MM_ROWS

    def body(d_ref, w_ref, a_ref, o_ref):
        df = lax.dot_general(d_ref[...], w_ref[...], (((1,), (1,)), ((), ())), preferred_element_type=F32)
        o_ref[...] = (df * (2.0 * jnp.maximum(a_ref[...].astype(F32), 0.0))).astype(BF16)

    return pl.pallas_call(
        body, name=name, grid=(s // tm,),
        in_specs=[pl.BlockSpec((tm, D), lambda i: (i, 0)),
                  pl.BlockSpec((None, DFF, D), lambda i: (l, 0, 0)),
                  pl.BlockSpec((tm, DFF), lambda i: (i, 0))],
        out_specs=pl.BlockSpec((tm, DFF), lambda i: (i, 0)),
        out_shape=jax.ShapeDtypeStruct((s, DFF), BF16),
        compiler_params=_cp("parallel"),
    )(dxb, w, a)


def _proj_out_bwd(name, dxb, w, l, mix):
    s = dxb.shape[0]
    tm = 512

    def body(d_ref, w_ref, o_ref, do_ref, dot_ref, dcp_ref, dl_ref):
        d = d_ref[...]
        wa, wc = w_ref[0:AW, :], w_ref[AW:D, :]
        do = lax.dot_general(d, wa, NT_DIMS, preferred_element_type=F32)
        do_ref[...] = do.astype(BF16)
        dot_ref[...] = lax.dot_general(wa, d, NT_DIMS, preferred_element_type=F32).astype(BF16)
        dcp_ref[...] = lax.dot_general(d, wc, NT_DIMS, preferred_element_type=F32)
        head = lax.broadcasted_iota(jnp.int32, (AW, 128), 0) // HD
        pick = jnp.where(head == lax.broadcasted_iota(jnp.int32, (AW, 128), 1), 1.0, 0.0).astype(BF16)
        dl_ref[...] = _two_pass_dot(do * o_ref[...].astype(F32), pick)

    return pl.pallas_call(
        body, name=name, grid=(s // tm,),
        in_specs=[pl.BlockSpec((tm, D), lambda i: (i, 0)),
                  pl.BlockSpec((None, D, D), lambda i: (l, 0, 0)),
                  pl.BlockSpec((tm, AW), lambda i: (i, 0))],
        out_specs=[pl.BlockSpec((tm, AW), lambda i: (i, 0)), pl.BlockSpec((AW, tm), lambda i: (0, i)),
                   pl.BlockSpec((tm, D - AW), lambda i: (i, 0)), pl.BlockSpec((tm, 128), lambda i: (i, 0))],
        out_shape=[jax.ShapeDtypeStruct((s, AW), BF16), jax.ShapeDtypeStruct((AW, s), BF16),
                   jax.ShapeDtypeStruct((s, D - AW), F32), jax.ShapeDtypeStruct((s, 128), F32)],
        compiler_params=_cp("parallel"),
    )(dxb, w, mix)


def _mm_nt_normbwd(name, gy, w, l, x, g, dres, dep):
    s, k = gy.shape
    tm = MM_ROWS

    def body(gy_ref, w_ref, x_ref, g_ref, dr_ref, dep_ref, dx_ref, dxb_ref, dg_ref):
        del dep_ref
        i = pl.program_id(0)
        dh = lax.dot_general(gy_ref[...], w_ref[...], (((1,), (1,)), ((), ())), preferred_element_type=F32)
        xv = x_ref[...]
        r = _inv_rms(xv)
        xn = xv * r
        dxn = dh * g_ref[...]
        dx = r * (dxn - xn * jnp.mean(dxn * xn, axis=-1, keepdims=True)) + dr_ref[...]
        dx_ref[...] = dx
        dxb_ref[...] = dx.astype(BF16)
        part = jnp.sum(dh * xn, axis=0, keepdims=True)

        @pl.when(i == 0)
        def _():
            dg_ref[...] = part

        @pl.when(i > 0)
        def _():
            dg_ref[...] += part

    blk = pl.BlockSpec((tm, D), lambda i: (i, 0))
    vec = pl.BlockSpec((1, D), lambda i: (0, 0))
    return pl.pallas_call(
        body, name=name, grid=(s // tm,),
        in_specs=[pl.BlockSpec((tm, k), lambda i: (i, 0)),
                  pl.BlockSpec((None, D, k), lambda i: (l, 0, 0)), blk, vec, blk, ANY],
        out_specs=[blk, blk, vec],
        out_shape=[jax.ShapeDtypeStruct((s, D), F32), jax.ShapeDtypeStruct((s, D), BF16),
                   jax.ShapeDtypeStruct((1, D), F32)],
        compiler_params=_cp("arbitrary"),
    )(gy, w, x, g, dres, dep)


def _mm_tn(name, a, b, tma, tnb, relu2=False):
    s, m = a.shape
    n = b.shape[1]

    def body(a_ref, b_ref, o_ref):
        av = _relu2(a_ref[...]) if relu2 else a_ref[...]
        o_ref[...] = lax.dot_general(av, b_ref[...], (((0,), (0,)), ((), ())),
                                     preferred_element_type=F32).astype(BF16)

    return pl.pallas_call(
        body, name=name, grid=(m // tma, n // tnb),
        in_specs=[pl.BlockSpec((s, tma), lambda i, j: (0, i)),
                  pl.BlockSpec((s, tnb), lambda i, j: (0, j))],
        out_specs=pl.BlockSpec((tma, tnb), lambda i, j: (i, j)),
        out_shape=jax.ShapeDtypeStruct((m, n), BF16),
        compiler_params=_cp("parallel", "parallel"),
    )(a, b)


def _adamw_math(gv, wv, mv, vv):
    mn = ADAM_B1 * mv + (1.0 - ADAM_B1) * gv
    vn = ADAM_B2 * vv + (1.0 - ADAM_B2) * jnp.square(gv)
    m_hat = mn / (1.0 - ADAM_B1 ** ADAM_STEP)
    v_hat = vn / (1.0 - ADAM_B2 ** ADAM_STEP)
    return gv, -ADAM_LR * (m_hat / (jnp.sqrt(v_hat) + ADAM_EPS) + ADAM_WD * wv), mn, vn


def _adamw(name, g, w, m, v):
    r, c = g.shape
    tm = 256 if r % 256 == 0 else r

    def body(g_ref, w_ref, m_ref, v_ref, go_ref, d_ref, mo_ref, vo_ref):
        go_ref[...], d_ref[...], mo_ref[...], vo_ref[...] = _adamw_math(g_ref[...], w_ref[...], m_ref[...], v_ref[...])

    blk = pl.BlockSpec((tm, c), lambda i: (i, 0))
    return pl.pallas_call(
        body, name=name, grid=(r // tm,),
        in_specs=[blk] * 4, out_specs=[blk] * 4,
        out_shape=[jax.ShapeDtypeStruct((r, c), F32)] * 4,
        compiler_params=_cp("parallel"),
    )(g, w, m, v)


def _place():
    x, y, c = lax.axis_index("x"), lax.axis_index("y"), lax.axis_index("c")
    chips = [(1 - x, y), (x, 1 - y), (1 - x, 1 - y)]
    return x, y, c, chips


BLOCK_AXIS = (2, 1, 2, 1)
LARGE_DIMS = ((D, DIN), (D, D), (D, DFF), (DFF, D))


def _full_shape(t, layers, dtype):
    r, c = LARGE_DIMS[t]
    return jax.ShapeDtypeStruct((layers, r, c), dtype)


def _cast_into_full(name, t, shard, b1, dep):
    _, r, c = shard.shape
    tm = min(256, r)
    if BLOCK_AXIS[t] == 1:
        out_spec = pl.BlockSpec((None, tm, c), lambda l, i, br: (l, br[0] * (r // tm) + i, 0))
    else:
        out_spec = pl.BlockSpec((None, tm, c), lambda l, i, br: (l, i, br[0]))

    def body(b_ref, x_ref, dep_ref, o_ref):
        del b_ref, dep_ref
        o_ref[...] = x_ref[...].astype(BF16)

    return pl.pallas_call(
        body, name=name,
        grid_spec=pltpu.PrefetchScalarGridSpec(
            num_scalar_prefetch=1, grid=(DEPTH, r // tm),
            in_specs=[pl.BlockSpec((None, tm, c), lambda l, i, br: (l, i, 0)), ANY],
            out_specs=out_spec),
        out_shape=_full_shape(t, DEPTH, BF16),
        compiler_params=_cp("parallel", "parallel"),
    )(b1, shard, dep)


HBM = pl.BlockSpec(memory_space=pltpu.HBM)
SEM = pl.BlockSpec(memory_space=pltpu.SEMAPHORE)
DATAFLOW = pltpu.SideEffectType.DATAFLOW_SIDE_EFFECTING


def _half(ref, l, t, b, c):
    r, cols = LARGE_DIMS[t]
    if BLOCK_AXIS[t] == 1:
        n = r // 8
        return ref.at[l, pl.ds(pl.multiple_of(b * (2 * n) + c * n, 16), n), :]
    n, w = r // 2, cols // 4
    return ref.at[l, pl.ds(pl.multiple_of(c * n, 16), n), pl.ds(pl.multiple_of(b * w, 128), w)]


def _gather_start(name, layers, fulls):
    def body(*refs):
        f_refs, sems = refs[4:8], refs[8:8 + 2 * len(layers)]
        x, y, c, chips = _place()
        for i, l in enumerate(layers):
            for t in range(4):
                own = _half(f_refs[t], l, t, 2 * x + y, c)
                for j, (cx, cy) in enumerate(chips):
                    pltpu.make_async_remote_copy(src_ref=own, dst_ref=own, send_sem=sems[2 * i].at[3 * t + j],
                                                 recv_sem=sems[2 * i + 1].at[3 * t + j], device_id=(cx, cy, c),
                                                 device_id_type=MESH).start()

    outs = pl.pallas_call(
        body, name=name,
        in_specs=[HBM] * 4, out_specs=[HBM] * 4 + [SEM] * (2 * len(layers)),
        out_shape=[pltpu.HBM(s.shape, s.dtype) for s in (_full_shape(t, DEPTH, BF16) for t in range(4))]
        + [pltpu.SemaphoreType.DMA((12,))] * (2 * len(layers)),
        input_output_aliases={t: t for t in range(4)},
        compiler_params=pltpu.CompilerParams(has_side_effects=DATAFLOW),
    )(*[pltpu.with_memory_space_constraint(f, pltpu.HBM) for f in fulls])
    return outs[0:4], {l: (outs[4 + 2 * i], outs[5 + 2 * i]) for i, l in enumerate(layers)}


def _gather_wait(name, l, ts, fulls, sems, after):
    def body(*refs):
        send_sems, recv_sems, f_refs = refs[4], refs[5], refs[7:11]
        x, y, c, chips = _place()
        for t in ts:
            own = _half(f_refs[t], l, t, 2 * x + y, c)
            for j, (cx, cy) in enumerate(chips):
                landed = _half(f_refs[t], l, t, 2 * cx + cy, c)
                pltpu.make_async_remote_copy(src_ref=own, dst_ref=landed, send_sem=send_sems.at[3 * t + j],
                                             recv_sem=recv_sems.at[3 * t + j], device_id=(cx, cy, c),
                                             device_id_type=MESH).wait()

    return pl.pallas_call(
        body, name=name,
        in_specs=[HBM] * 4 + [SEM, SEM, ANY], out_specs=[HBM] * 4,
        out_shape=[pltpu.HBM(s.shape, s.dtype) for s in (_full_shape(t, DEPTH, BF16) for t in range(4))],
        input_output_aliases={t: t for t in range(4)},
        compiler_params=pltpu.CompilerParams(has_side_effects=DATAFLOW),
    )(*fulls, sems[0], sems[1], after)


def _pass_on(name, l, ts, fulls):
    def body(*refs):
        f_refs, send_sems, recv_sems = refs[4:8], refs[8], refs[9]
        x, y, c, chips = _place()

        def copy(t, j, half):
            cx, cy = chips[j]
            part = _half(f_refs[t], l, t, 2 * cx + cy, half)
            return pltpu.make_async_remote_copy(src_ref=part, dst_ref=part, send_sem=send_sems.at[3 * t + j],
                                                recv_sem=recv_sems.at[3 * t + j], device_id=(x, y, 1 - c),
                                                device_id_type=MESH)

        for t in ts:
            for j in range(3):
                copy(t, j, c).start()
        for t in ts:
            for j in range(3):
                copy(t, j, 1 - c).wait_recv()
                copy(t, j, c).wait_send()

    return pl.pallas_call(
        body, name=name,
        in_specs=[ANY] * 4, out_specs=[ANY] * 4,
        out_shape=[_full_shape(t, DEPTH, BF16) for t in range(4)],
        input_output_aliases={t: t for t in range(4)},
        scratch_shapes=[pltpu.SemaphoreType.DMA((12,)), pltpu.SemaphoreType.DMA((12,))],
    )(*fulls)


def _block2d(ref, t, b):
    r, cols = LARGE_DIMS[t]
    if BLOCK_AXIS[t] == 1:
        return ref.at[pl.ds(pl.multiple_of(b * (r // 4), 16), r // 4), :]
    return ref.at[:, pl.ds(pl.multiple_of(b * (cols // 4), 128), cols // 4)]


def _block_dims(t):
    r, cols = LARGE_DIMS[t]
    return (r // 4, cols) if BLOCK_AXIS[t] == 1 else (r, cols // 4)


def _reduce_copies(ts, g_refs, r_refs, send_sems, recv_sems):
    _, _, c, chips = _place()
    return [pltpu.make_async_remote_copy(src_ref=_block2d(g_refs[i], t, 2 * cx + cy), dst_ref=r_refs[i].at[j],
                                         send_sem=send_sems.at[3 * i + j], recv_sem=recv_sems.at[3 * i + j],
                                         device_id=(cx, cy, c), device_id_type=MESH)
            for i, t in enumerate(ts) for j, (cx, cy) in enumerate(chips)]


def _reduce_start(name, ts, grads):
    n = len(ts)

    def body(*refs):
        for cp in _reduce_copies(ts, refs[n:2 * n], refs[2 * n:3 * n], refs[3 * n], refs[3 * n + 1]):
            cp.start()
        refs[3 * n + 2][...] = jnp.zeros((8, 128), F32)

    outs = pl.pallas_call(
        body, name=name,
        in_specs=[HBM] * n,
        out_specs=[HBM] * (2 * n) + [SEM, SEM, pl.BlockSpec(memory_space=pltpu.VMEM)],
        out_shape=[pltpu.HBM(g.shape, BF16) for g in grads]
        + [pltpu.HBM((3,) + _block_dims(t), BF16) for t in ts]
        + [pltpu.SemaphoreType.DMA((3 * n,)), pltpu.SemaphoreType.DMA((3 * n,)), jax.ShapeDtypeStruct((8, 128), F32)],
        input_output_aliases={i: i for i in range(n)},
        compiler_params=pltpu.CompilerParams(has_side_effects=DATAFLOW),
    )(*[pltpu.with_memory_space_constraint(g, pltpu.HBM) for g in grads])
    return outs[0:n], outs[n:2 * n], (outs[2 * n], outs[2 * n + 1]), outs[2 * n + 2]


def _reduce_wait(name, ts, grads, landing, sems, afters):
    n = len(ts)
    first_out = 2 * n + 2 + len(afters)

    def body(*refs):
        for cp in _reduce_copies(ts, refs[first_out:first_out + n], refs[first_out + n:first_out + 2 * n],
                                 refs[2 * n], refs[2 * n + 1]):
            cp.wait()

    outs = pl.pallas_call(
        body, name=name,
        in_specs=[HBM] * (2 * n) + [SEM, SEM] + [ANY] * len(afters), out_specs=[HBM] * (2 * n),
        out_shape=[pltpu.HBM(g.shape, BF16) for g in grads] + [pltpu.HBM(r.shape, BF16) for r in landing],
        input_output_aliases={i: i for i in range(2 * n)},
        compiler_params=pltpu.CompilerParams(has_side_effects=DATAFLOW),
    )(*grads, *landing, sems[0], sems[1], *afters)
    return outs[0:n], outs[n:2 * n]


def _add4(name, t, own, landed, b1):
    rb, cb = _block_dims(t)
    tm = min(256, rb)
    if BLOCK_AXIS[t] == 1:
        own_spec = pl.BlockSpec((tm, cb), lambda i, br: (br[0] * (rb // tm) + i, 0))
    else:
        own_spec = pl.BlockSpec((tm, cb), lambda i, br: (i, br[0]))

    def body(b_ref, o_ref, r0_ref, r1_ref, r2_ref, s_ref):
        del b_ref
        s_ref[...] = ((o_ref[...].astype(F32) + r0_ref[...].astype(F32))
                      + (r1_ref[...].astype(F32) + r2_ref[...].astype(F32))).astype(BF16)

    def got(j):
        return pl.BlockSpec((None, tm, cb), lambda i, br: (j, i, 0))

    return pl.pallas_call(
        body, name=name,
        grid_spec=pltpu.PrefetchScalarGridSpec(
            num_scalar_prefetch=1, grid=(rb // tm,),
            in_specs=[own_spec, got(0), got(1), got(2)],
            out_specs=pl.BlockSpec((tm, cb), lambda i, br: (i, 0))),
        out_shape=jax.ShapeDtypeStruct((rb, cb), BF16),
        compiler_params=_cp("parallel"),
    )(b1, own, landed, landed, landed)


def _swap_sib(name, sums):
    def body(*refs):
        s_refs, t_refs, send_sems, recv_sems = refs[0:4], refs[4:8], refs[8], refs[9]
        x, y, c, _ = _place()
        cps = [pltpu.make_async_remote_copy(src_ref=s_refs[t], dst_ref=t_refs[t], send_sem=send_sems.at[t],
                                            recv_sem=recv_sems.at[t], device_id=(x, y, 1 - c), device_id_type=MESH)
               for t in range(4)]
        for cp in cps:
            cp.start()
        for cp in cps:
            cp.wait()

    return pl.pallas_call(
        body, name=name,
        in_specs=[ANY] * 4, out_specs=[ANY] * 4,
        out_shape=[jax.ShapeDtypeStruct(s.shape, BF16) for s in sums],
        scratch_shapes=[pltpu.SemaphoreType.DMA((4,)), pltpu.SemaphoreType.DMA((4,))],
    )(*sums)


def _adamw_pair(name, l, s_own, s_sib, w, m, v, outs):
    rb, cb = s_own.shape
    tm = min(256, rb)

    def body(a_ref, b_ref, w_ref, m_ref, v_ref, g0, d0, m0, v0, go_ref, d_ref, mo_ref, vo_ref):
        del g0, d0, m0, v0
        gv = a_ref[...].astype(F32) + b_ref[...].astype(F32)
        go_ref[...], d_ref[...], mo_ref[...], vo_ref[...] = _adamw_math(gv, w_ref[...], m_ref[...], v_ref[...])

    part = pl.BlockSpec((tm, cb), lambda i: (i, 0))
    layer = pl.BlockSpec((None, tm, cb), lambda i: (l, i, 0))
    return pl.pallas_call(
        body, name=name, grid=(rb // tm,),
        in_specs=[part, part, layer, layer, layer] + [ANY] * 4,
        out_specs=[layer] * 4,
        out_shape=[jax.ShapeDtypeStruct((DEPTH, rb, cb), F32)] * 4,
        input_output_aliases={5 + i: i for i in range(4)},
        compiler_params=_cp("parallel"),
    )(s_own, s_sib, w, m, v, *outs)


def _all_gather8(name, v, dep):
    m_per, n = v.shape

    def body(v_ref, dep_ref, out_ref, send_sems, recv_sems, local_sem):
        del dep_ref
        x, y, c, chips = _place()
        me, sib = (x, y, c), (x, y, 1 - c)

        def rows(px, py, pc):
            return out_ref.at[pl.ds((4 * px + 2 * py + pc) * m_per, m_per), :]

        def copy(k, block, to, src=None):
            return pltpu.make_async_remote_copy(
                src_ref=rows(*block) if src is None else src, dst_ref=rows(*block),
                send_sem=send_sems.at[k], recv_sem=recv_sems.at[k], device_id=to, device_id_type=MESH)

        mine = pltpu.make_async_copy(v_ref, rows(*me), local_sem)
        mine.start()
        first = [copy(0, me, sib, src=v_ref)]
        first += [copy(1 + j, me, (*chip, c), src=v_ref) for j, chip in enumerate(chips)]
        for cp in first:
            cp.start()
        passed = [copy(4 + j, (*chip, c), sib) for j, chip in enumerate(chips)]
        for j, chip in enumerate(chips):
            copy(1 + j, (*chip, c), me).wait_recv()
            passed[j].start()
        copy(0, sib, me).wait_recv()
        for j, chip in enumerate(chips):
            copy(4 + j, (*chip, 1 - c), me).wait_recv()
        for cp in first + passed:
            cp.wait_send()
        mine.wait()

    return pl.pallas_call(
        body, name=name,
        out_shape=jax.ShapeDtypeStruct((8 * m_per, n), v.dtype),
        in_specs=[pl.BlockSpec(memory_space=pltpu.VMEM), ANY],
        out_specs=pl.BlockSpec(memory_space=pltpu.VMEM),
        scratch_shapes=[pltpu.SemaphoreType.DMA((7,)), pltpu.SemaphoreType.DMA((7,)), pltpu.SemaphoreType.DMA],
    )(v, dep)


def _sum8(name, g):
    def body(g_ref, o_ref):
        acc = g_ref[0]
        for d in range(1, 8):
            acc = acc + g_ref[d]
        o_ref[...] = acc

    return pl.pallas_call(body, name=name, out_shape=jax.ShapeDtypeStruct(g.shape[1:], F32))(g)


def _pack(parts):
    flat = []
    for a in parts:
        a = a.reshape(-1)
        flat.append(jnp.pad(a, (0, (-a.shape[0]) % 128)))
    cat = jnp.concatenate(flat)
    cat = jnp.pad(cat, (0, (-cat.shape[0]) % 1024))
    return cat.reshape(-1, 128)


def _unpack(packed, shapes):
    flat = packed.reshape(-1)
    out, at = [], 0
    for shp in shapes:
        n = 1
        for d in shp:
            n *= d
        out.append(flat[at:at + n].reshape(shp))
        at += n + (-n) % 128
    return out


def _local_step(x, target, layer_weights, on_grads, small):
    qg_all = jnp.tile(small["q_norm_g"], (1, 8))
    kg_all = jnp.tile(small["k_norm_g"], (1, 8))
    bias_all = _bias_layout(_bias_expand("bias_expand", jnp.pad(small["rel_bias"], ((0, 0), (0, 0), (0, NIDX - 257)))))
    same_group = jnp.eye(4, dtype=F32)[None, :, None, :, None]
    pwbd_all = (small["pool_w"][:, :, :, None, :] * same_group).reshape(DEPTH, PWD, PWD)
    saved = []
    xin = x
    h = _rmsnorm("norm_first", x, small["norm1_g"][0:1])
    for l in range(DEPTH):
        w_in = layer_weights(l, (0,), xin)[0]
        qg, kg, bias = qg_all[l:l + 1], kg_all[l:l + 1], bias_all[l]
        cw, pwbd, ps = small["conv_w"][l], pwbd_all[l], small["pool_scale"][l:l + 1]
        p = _mm_nn(f"proj_in_{l}", h, w_in, l, F32)
        q, qt, kp, kt, vp, vt = _qkv(f"qkv_{l}", p, qg, kg)
        o, lse = _attn_fwd(f"attn_fwd_{l}", kp, qt, vt, bias)
        w_in, w_out, w_1, w_2 = layer_weights(l, (1, 2, 3), o)
        mix = _convpool_fwd(f"convpool_fwd_{l}", p, o, cw, pwbd, ps)
        x1, h2 = _mm_res_norm(f"proj_out_{l}", mix, w_out, l, xin, small["norm2_g"][l:l + 1])
        a = _mm_nn(f"mlp1_{l}", h2, w_1, l, BF16)
        gnext = small["norm1_g"][(l + 1) % DEPTH][None]
        x2, hnext = _mm_res_norm(f"mlp2_{l}", a, w_2, l, x1, gnext, relu2=True)
        saved.append(dict(xin=xin, h=h, p=p, q=q, qt=qt, kp=kp, kt=kt, vp=vp, bias=bias, mix=mix, x1=x1, h2=h2, a=a, lse=lse,
                          qg=qg, kg=kg, cw=cw, pwbd=pwbd, ps=ps))
        xin, h = x2, hnext

    dx, dxb, loss = _loss_grad("loss_grad", xin, target)
    raw = {k: [None] * DEPTH for k in ("dg1", "dqg", "dkg", "db", "dw0", "dw1", "dw2", "dpw", "dps", "dg2")}
    for l in reversed(range(DEPTH)):
        sv = saved[l]
        da = _mm_nt_relu(f"mlp2_bwd_{l}", dxb, w_2, l, sv["a"])
        g_2 = _mm_tn(f"mlp2_wgrad_{l}", sv["a"], dxb, 512, 1024, relu2=True)
        g_1 = _mm_tn(f"mlp1_wgrad_{l}", sv["h2"], da, 1024, 512)
        dep = on_grads(l, (2, 3), (g_1, g_2))
        dx1, dx1b, dg2 = _mm_nt_normbwd(f"mlp1_bwd_{l}", da, w_1, l, sv["x1"], small["norm2_g"][l:l + 1], dx, dep)
        do, dot, dmix, dl = _proj_out_bwd(f"proj_out_bwd_{l}", dx1b, w_out, l, sv["mix"])
        g_out = _mm_tn(f"proj_out_wgrad_{l}", sv["mix"], dx1b, 512, 1024)
        dcp, dw0, dw1, dw2, dps, dpw = _convpool_bwd(f"convpool_bwd_{l}", sv["p"], dmix, sv["cw"], sv["pwbd"], sv["ps"])
        dq, dkp, dvp, db = _attn_bwd(f"attn_bwd_{l}", sv["q"], sv["qt"], sv["kp"], sv["kt"], sv["vp"], sv["bias"],
                                     do, dot, sv["lse"], _rowsum_layout(dl, x.shape[0] // UNIT))
        dp, dqg, dkg = _qkv_bwd(f"qkv_bwd_{l}", sv["p"], dq, dkp, dvp, dcp, sv["qg"], sv["kg"])
        g_in = _mm_tn(f"proj_in_wgrad_{l}", sv["h"], dp, 1024, 640)
        dep = on_grads(l, (0, 1), (g_in, g_out))
        dx, dxb, dg1 = _mm_nt_normbwd(f"proj_in_bwd_{l}", dp, w_in, l, sv["xin"], small["norm1_g"][l:l + 1], dx1, dep)
        for k, val in dict(dg1=dg1, dqg=dqg, dkg=dkg, db=db, dw0=dw0, dw1=dw1, dw2=dw2, dpw=dpw, dps=dps, dg2=dg2).items():
            raw[k][l] = val
    cat = {k: jnp.concatenate(v, axis=0) for k, v in raw.items() if k not in ("db", "dpw")}
    drb = _bias_reduce("bias_reduce", _bias_unlayout(jnp.stack(raw["db"])))
    dpw = jnp.stack(raw["dpw"])
    gsmall = {
        "norm1_g": cat["dg1"], "q_norm_g": cat["dqg"][:, :HD], "k_norm_g": cat["dkg"][:, :HD],
        "rel_bias": drb[:, :, :257],
        "conv_w": jnp.stack([cat["dw0"], cat["dw1"], cat["dw2"]], axis=1),
        "pool_w": jnp.stack([dpw[:, g * 64:(g + 1) * 64, g * 64:(g + 1) * 64] for g in range(4)], axis=1),
        "pool_scale": cat["dps"], "norm2_g": cat["dg2"],
    }
    return loss, dx, gsmall


SMALL = ("norm1_g", "q_norm_g", "k_norm_g", "rel_bias", "conv_w", "pool_w", "pool_scale", "norm2_g")
LARGE = ("w_in", "w_out", "w_mlp1", "w_mlp2")


def kernel(x, norm1_g, w_in, q_norm_g, k_norm_g, rel_bias, conv_w, pool_w, pool_scale, w_out, norm2_g, w_mlp1, w_mlp2, loss_target, m_norm1_g, m_w_in, m_q_norm_g, m_k_norm_g, m_rel_bias, m_conv_w, m_pool_w, m_pool_scale, m_w_out, m_norm2_g, m_w_mlp1, m_w_mlp2, v_norm1_g, v_w_in, v_q_norm_g, v_k_norm_g, v_rel_bias, v_conv_w, v_pool_w, v_pool_scale, v_w_out, v_norm2_g, v_w_mlp1, v_w_mlp2):
    w = dict(norm1_g=norm1_g, w_in=w_in, q_norm_g=q_norm_g, k_norm_g=k_norm_g, rel_bias=rel_bias, conv_w=conv_w,
             pool_w=pool_w, pool_scale=pool_scale, w_out=w_out, norm2_g=norm2_g, w_mlp1=w_mlp1, w_mlp2=w_mlp2)
    m = dict(norm1_g=m_norm1_g, w_in=m_w_in, q_norm_g=m_q_norm_g, k_norm_g=m_k_norm_g, rel_bias=m_rel_bias,
             conv_w=m_conv_w, pool_w=m_pool_w, pool_scale=m_pool_scale, w_out=m_w_out, norm2_g=m_norm2_g,
             w_mlp1=m_w_mlp1, w_mlp2=m_w_mlp2)
    v = dict(norm1_g=v_norm1_g, w_in=v_w_in, q_norm_g=v_q_norm_g, k_norm_g=v_k_norm_g, rel_bias=v_rel_bias,
             conv_w=v_conv_w, pool_w=v_pool_w, pool_scale=v_pool_scale, w_out=v_w_out, norm2_g=v_norm2_g,
             w_mlp1=v_w_mlp1, w_mlp2=v_w_mlp2)
    ax, ay, ac = lax.axis_index("x"), lax.axis_index("y"), lax.axis_index("c")
    b1 = jnp.reshape(2 * ax + ay, (1,)).astype(jnp.int32)

    cw_rows = _all_gather8("gather_conv_w", jnp.pad(conv_w.reshape(DEPTH * 3, 64), ((0, 4), (0, 64))), b1)
    cw_chips = [cw_rows[(4 * cx + 2 * cy) * 16:(4 * cx + 2 * cy) * 16 + 12, :64] for cx in range(2) for cy in range(2)]
    small = {n: w[n] for n in SMALL}
    small["conv_w"] = jnp.concatenate(cw_chips, axis=1).reshape(DEPTH, 3, CW)

    casts = [_cast_into_full(f"cast_{n}", t, w[n], b1, cw_rows) for t, n in enumerate(LARGE)]
    first, first_sems = _gather_start("gather_start_first", (0,), casts)
    held = [first]
    sems = dict(first_sems)

    def layer_weights(l, ts, after):
        if l > 0:
            ts = (0, 1, 2, 3) if ts == (0,) else ()
        if ts:
            tag = f"{l}_{ts[0]}"
            arrived = _gather_wait(f"gather_wait_{tag}", l, ts, held[0], sems[l], after)
            if l == 0 and ts == (0,):
                arrived, rest_sems = _gather_start("gather_start_rest", tuple(range(1, DEPTH)), arrived)
                sems.update(rest_sems)
            held[0] = _pass_on(f"pass_on_{tag}", l, ts, arrived)
        return held[0]

    flights = {}

    def await_flight(l, ts, afters):
        g, landing, sm, _ = flights[l, ts]
        flights[l, ts] = _reduce_wait(f"reduce_wait_{l}_{ts[0]}", ts, g, landing, sm, afters)

    def on_grads(l, ts, grads):
        if ts == (0, 1) and l + 1 < DEPTH:
            await_flight(l + 1, (2, 3), [grads[0]])
            await_flight(l + 1, (0, 1), [grads[0]])
        flights[l, ts] = _reduce_start(f"reduce_start_{l}_{ts[0]}", ts, grads)
        return flights[l, ts][3]

    loss_part, grad_x, gsmall = _local_step(x[0], loss_target[0], layer_weights, on_grads, small)
    loss = lax.psum(loss_part[0, 0], ("x", "y", "c"))
    order = [n for n in SMALL]
    packed = _pack([gsmall[n] for n in order])

    out = {n: [lax.empty(w[n].shape, F32) for _ in range(4)] for n in LARGE}
    for l in reversed(range(DEPTH)):
        if l == 0:
            afters = [grad_x, packed] + [out[n][0] for n in LARGE]
            await_flight(0, (2, 3), afters)
            await_flight(0, (0, 1), afters)
        sums = [None] * 4
        for ts in ((0, 1), (2, 3)):
            g, landing = flights[l, ts]
            for i, t in enumerate(ts):
                sums[t] = _add4(f"add4_{LARGE[t]}_{l}", t, g[i], landing[i], b1)
        theirs = _swap_sib(f"swap_sib_{l}", sums)
        for t, n in enumerate(LARGE):
            out[n] = _adamw_pair(f"adamw_{n}_{l}", l, sums[t], theirs[t], w[n], m[n], v[n], out[n])

    rows = packed.shape[0]
    summed = _sum8("sum_small", _all_gather8("gather_small", packed, out[LARGE[0]][0]).reshape(8, rows, 128))
    gfull = dict(zip(order, _unpack(summed, [gsmall[n].shape for n in order])))
    gfull["conv_w"] = lax.dynamic_slice_in_dim(gfull["conv_w"], (2 * ax + ay) * 64, 64, axis=2)
    res = _adamw("adamw_small", _pack([gfull[n] for n in order]), _pack([w[n] for n in order]),
                 _pack([m[n] for n in order]), _pack([v[n] for n in order]))
    for n, parts in zip(order, zip(*[_unpack(r, [w[k].shape for k in order]) for r in res])):
        out[n] = list(parts)

    names = ("norm1_g", "w_in", "q_norm_g", "k_norm_g", "rel_bias", "conv_w", "pool_w", "pool_scale", "w_out",
             "norm2_g", "w_mlp1", "w_mlp2")
    flat = [loss, grad_x[None]]
    for i in range(4):
        flat += [out[n][i] for n in names]
    return tuple(flat)
```

```python
import functools

import jax
import jax.numpy as jnp
from jax import lax
from jax.experimental import pallas as pl
from jax.experimental.pallas import tpu as pltpu

F32 = jnp.float32
BF16 = jnp.bfloat16

D = 1024
DEPTH = 4
CH = 64
NPREV = 8
KB = (NPREV + 1) * CH
PADR = NPREV * CH
HD = 64
AW = 512
CW = 256
PWD = 256
DIN = 3 * AW + 3 * CW + PWD
DFF = 4 * D
NIDX = 384
EPS = 1e-6
NEG_INF = -1e30

ADAM_LR = 0.001
ADAM_B1 = 0.9
ADAM_B2 = 0.999
ADAM_EPS = 1e-08
ADAM_WD = 0.01
ADAM_STEP = 10

VMEM_LIMIT = 52 * 1024 * 1024
MM_ROWS = 512
MESH = pl.DeviceIdType.MESH
ANY = pl.BlockSpec(memory_space=pl.ANY)


def _cp(*sem):
    return pltpu.CompilerParams(dimension_semantics=sem, vmem_limit_bytes=VMEM_LIMIT)


def _inv_rms(x):
    return lax.rsqrt(jnp.mean(x * x, axis=-1, keepdims=True) + EPS)


def _head_mean_matrix():
    r = lax.broadcasted_iota(jnp.int32, (AW, AW), 0) // HD
    c = lax.broadcasted_iota(jnp.int32, (AW, AW), 1) // HD
    return jnp.where(r == c, 1.0 / HD, 0.0).astype(BF16)


def _two_pass_dot(x, m):
    hi = x.astype(BF16)
    lo = (x - hi.astype(F32)).astype(BF16)
    return (jnp.dot(hi, m, preferred_element_type=F32)
            + jnp.dot(lo, m, preferred_element_type=F32))


def _head_mean(x, hm):
    return _two_pass_dot(x, hm)


def _rmsnorm(name, x, g):
    s = x.shape[0]
    tm = 512

    def body(x_ref, g_ref, h_ref):
        xv = x_ref[...]
        h_ref[...] = (xv * _inv_rms(xv) * g_ref[...]).astype(BF16)

    return pl.pallas_call(
        body, name=name, grid=(s // tm,),
        in_specs=[pl.BlockSpec((tm, D), lambda i: (i, 0)), pl.BlockSpec((1, D), lambda i: (0, 0))],
        out_specs=pl.BlockSpec((tm, D), lambda i: (i, 0)),
        out_shape=jax.ShapeDtypeStruct((s, D), BF16),
        compiler_params=_cp("parallel"),
    )(x, g)


def _relu2(a):
    r = jnp.maximum(a, jnp.zeros_like(a))
    return r * r


def _mm_nn(name, a, w, l, out_dtype):
    s, k = a.shape
    n = w.shape[2]
    tm = MM_ROWS

    def body(a_ref, w_ref, o_ref):
        o_ref[...] = jnp.dot(a_ref[...], w_ref[...], preferred_element_type=F32).astype(o_ref.dtype)

    return pl.pallas_call(
        body, name=name, grid=(s // tm,),
        in_specs=[pl.BlockSpec((tm, k), lambda i: (i, 0)),
                  pl.BlockSpec((None, k, n), lambda i: (l, 0, 0))],
        out_specs=pl.BlockSpec((tm, n), lambda i: (i, 0)),
        out_shape=jax.ShapeDtypeStruct((s, n), out_dtype),
        compiler_params=_cp("parallel"),
    )(a, w)


def _mm_res_norm(name, a, w, l, res, g, relu2=False):
    s, k = a.shape
    tm = MM_ROWS

    def body(a_ref, w_ref, r_ref, g_ref, x_ref, h_ref):
        av = _relu2(a_ref[...]) if relu2 else a_ref[...]
        acc = r_ref[...] + jnp.dot(av, w_ref[...], preferred_element_type=F32)
        x_ref[...] = acc
        h_ref[...] = (acc * _inv_rms(acc) * g_ref[...]).astype(BF16)

    return pl.pallas_call(
        body, name=name, grid=(s // tm,),
        in_specs=[pl.BlockSpec((tm, k), lambda i: (i, 0)),
                  pl.BlockSpec((None, k, D), lambda i: (l, 0, 0)),
                  pl.BlockSpec((tm, D), lambda i: (i, 0)),
                  pl.BlockSpec((1, D), lambda i: (0, 0))],
        out_specs=[pl.BlockSpec((tm, D), lambda i: (i, 0))] * 2,
        out_shape=[jax.ShapeDtypeStruct((s, D), F32), jax.ShapeDtypeStruct((s, D), BF16)],
        compiler_params=_cp("parallel"),
    )(a, w, res, g)


def _qkv(name, p, qg, kg):
    s = p.shape[0]
    tm = PADR
    nb = s // tm

    def body(pq_ref, pk_ref, pv_ref, qg_ref, kg_ref, q_ref, qt_ref, k_ref, kt_ref, v_ref, vt_ref):
        t = pl.program_id(0)
        hm = _head_mean_matrix()

        def nrm(x, g):
            return x * lax.rsqrt(_head_mean(x * x, hm) + EPS) * g

        first = t == 0
        qq = nrm(pq_ref[...], qg_ref[...]) * 0.125
        kk = jnp.where(first, 0.0, nrm(pk_ref[...], kg_ref[...]))
        vv = jnp.where(first, 0.0, pv_ref[...])
        q_ref[...] = qq.astype(BF16)
        qt_ref[...] = qq.T.astype(BF16)
        k_ref[...] = kk.astype(BF16)
        kt_ref[...] = kk.T.astype(BF16)
        v_ref[...] = vv.astype(BF16)
        vt_ref[...] = vv.T.astype(BF16)

    def src(col):
        return pl.BlockSpec((tm, AW), lambda t: (jnp.maximum(t - 1, 0), col))

    gspec = pl.BlockSpec((1, AW), lambda t: (0, 0))
    rows = pl.BlockSpec((tm, AW), lambda t: (t, 0))
    cols = pl.BlockSpec((AW, tm), lambda t: (0, t))
    return pl.pallas_call(
        body, name=name, grid=(nb + 1,),
        in_specs=[src(0), src(1), src(2), gspec, gspec],
        out_specs=[pl.BlockSpec((tm, AW), lambda t: (jnp.maximum(t - 1, 0), 0)),
                   pl.BlockSpec((AW, tm), lambda t: (0, jnp.maximum(t - 1, 0))),
                   rows, cols, rows, cols],
        out_shape=[jax.ShapeDtypeStruct((s, AW), BF16), jax.ShapeDtypeStruct((AW, s), BF16),
                   jax.ShapeDtypeStruct((s + PADR, AW), BF16), jax.ShapeDtypeStruct((AW, s + PADR), BF16),
                   jax.ShapeDtypeStruct((s + PADR, AW), BF16), jax.ShapeDtypeStruct((AW, s + PADR), BF16)],
        compiler_params=_cp("arbitrary"),
    )(p, p, p, qg, kg)


NBAND = KB // CH
HIGHEST = lax.Precision.HIGHEST
NT_DIMS = (((1,), (1,)), ((), ()))


def _onehot_table(a):
    m = lax.broadcasted_iota(jnp.int32, (128, NIDX), 0)
    idx = lax.broadcasted_iota(jnp.int32, (128, NIDX), 1)
    rel = jnp.clip(KB - 1 - (CH * a + m), -128, 128) + 128
    return jnp.where(rel == idx, 1.0, 0.0).astype(F32)


def _onehot_diagonal():
    r = lax.broadcasted_iota(jnp.int32, (CH * CH, 128), 0)
    m = lax.broadcasted_iota(jnp.int32, (CH * CH, 128), 1)
    return jnp.where((r % CH) - (r // CH) + (CH - 1) == m, 1.0, 0.0).astype(F32)


def _bias_expand(name, rb):
    def body(rb_ref, o_ref):
        along = [lax.dot_general(rb_ref[...], _onehot_table(a), NT_DIMS, preferred_element_type=F32,
                                 precision=HIGHEST) for a in range(NBAND)]
        o_ref[...] = lax.dot_general(jnp.concatenate(along, axis=0), _onehot_diagonal(), NT_DIMS,
                                     preferred_element_type=F32, precision=HIGHEST)

    return pl.pallas_call(
        body, name=name, grid=(DEPTH,),
        in_specs=[pl.BlockSpec((None, 8, NIDX), lambda l: (l, 0, 0))],
        out_specs=pl.BlockSpec((None, NBAND * 8, CH * CH), lambda l: (l, 0, 0)),
        out_shape=jax.ShapeDtypeStruct((DEPTH, NBAND * 8, CH * CH), F32),
        compiler_params=_cp("parallel"),
    )(rb)


def _bias_reduce(name, db):
    def body(db_ref, o_ref):
        along = jnp.dot(db_ref[...], _onehot_diagonal(), preferred_element_type=F32, precision=HIGHEST)
        acc = jnp.zeros((8, NIDX), F32)
        for a in range(NBAND):
            acc = acc + jnp.dot(along[8 * a:8 * a + 8, :], _onehot_table(a), preferred_element_type=F32,
                                precision=HIGHEST)
        o_ref[...] = acc

    return pl.pallas_call(
        body, name=name, grid=(DEPTH,),
        in_specs=[pl.BlockSpec((None, NBAND * 8, CH * CH), lambda l: (l, 0, 0))],
        out_specs=pl.BlockSpec((None, 8, NIDX), lambda l: (l, 0, 0)),
        out_shape=jax.ShapeDtypeStruct((DEPTH, 8, NIDX), F32),
        compiler_params=_cp("parallel"),
    )(db)


def _bias_layout(flat):
    b = flat.reshape(DEPTH, NBAND, 8, CH, CH).transpose(0, 2, 1, 4, 3).reshape(DEPTH, 4, 2, KB, CH)
    pair = b.transpose(0, 1, 3, 2, 4).reshape(DEPTH, 4, KB, 128)
    first = jnp.pad(pair, ((0, 0), (0, 0), (0, CH), (0, 0)), constant_values=NEG_INF)
    second = jnp.pad(pair, ((0, 0), (0, 0), (CH, 0), (0, 0)), constant_values=NEG_INF)
    return jnp.concatenate([first, second], axis=3)


def _bias_unlayout(dbt):
    b = dbt.reshape(DEPTH, 4, NBAND, CH, 2, CH)
    return b.transpose(0, 2, 1, 4, 5, 3).reshape(DEPTH, NBAND * 8, CH * CH)


UNIT = 2 * CH
BAND2 = KB + CH


def _pair_weights(xt):
    x = xt.astype(F32)
    row = lax.broadcasted_iota(jnp.int32, (128, UNIT), 0)
    low = lax.broadcasted_iota(jnp.int32, (128, UNIT), 1) < HD
    swapped = pltpu.roll(x, HD, 1)
    same = (row < HD) == low
    first = jnp.where(same, jnp.where(low, x, swapped), 0.0)
    second = jnp.where(same, jnp.where(low, swapped, x), 0.0)
    return jnp.concatenate([first, second], axis=1).astype(BF16)


def _pair_rows(x):
    low = lax.broadcasted_iota(jnp.int32, (CH, 128), 1) < HD
    zero = jnp.zeros((CH, 128), x.dtype)
    parts = []
    for c in range(2):
        xc = x[c * CH:(c + 1) * CH, :]
        parts += [jnp.where(low, xc, zero), jnp.where(low, zero, xc)]
    return jnp.concatenate(parts, axis=0)


def _unpair(raw):
    b0, b1 = raw[:, 0:128], raw[:, 128:256]
    row = lax.broadcasted_iota(jnp.int32, (128, 128), 0)
    low = lax.broadcasted_iota(jnp.int32, (128, 128), 1) < HD
    top = jnp.where(low, b0, pltpu.roll(b1, HD, 1))
    bottom = jnp.where(low, pltpu.roll(b0, HD, 1), b1)
    return jnp.where(row < HD, top, bottom).T


def _scores_t(kb, qw, bias2, row0, padded):
    s = jnp.dot(kb, qw, preferred_element_type=F32) + bias2
    if padded:
        s = jnp.where(row0 + lax.broadcasted_iota(jnp.int32, (BAND2, 256), 0) >= PADR, s, NEG_INF)
    return s


def _unit_loops(s, unit):
    lax.fori_loop(0, PADR // UNIT, lambda u, c: unit(u, True, c), 0, unroll=2)
    lax.fori_loop(PADR // UNIT, s // UNIT, lambda u, c: unit(u, False, c), 0, unroll=4)


def _attn_fwd(name, kp, qt, vt, bias2):
    s = qt.shape[1]
    nu = s // UNIT

    def body(k_ref, qt_ref, vt_ref, b_ref, o_ref, lse_ref):
        def unit(u, padded, carry):
            r0 = pl.multiple_of(u * UNIT, UNIT)
            sc = _scores_t(k_ref[pl.ds(r0, BAND2), :], _pair_weights(qt_ref[:, pl.ds(r0, UNIT)]), b_ref[...],
                           r0, padded)
            top = jnp.max(sc, axis=0, keepdims=True)
            e = jnp.exp(sc - top)
            total = jnp.sum(e, axis=0, keepdims=True)
            raw = jnp.dot(vt_ref[:, pl.ds(r0, BAND2)], e.astype(BF16), preferred_element_type=F32) * (1.0 / total)
            o_ref[pl.ds(r0, UNIT), :] = _unpair(raw).astype(BF16)
            lse_ref[u] = jnp.broadcast_to(top + jnp.log(total), (8, 256))
            return carry

        _unit_loops(s, unit)

    return pl.pallas_call(
        body, name=name, grid=(AW // 128,),
        in_specs=[pl.BlockSpec((s + PADR, 128), lambda h: (0, h)),
                  pl.BlockSpec((128, s), lambda h: (h, 0)),
                  pl.BlockSpec((128, s + PADR), lambda h: (h, 0)),
                  pl.BlockSpec((None, BAND2, 256), lambda h: (h, 0, 0))],
        out_specs=[pl.BlockSpec((s, 128), lambda h: (0, h)),
                   pl.BlockSpec((None, nu, 8, 256), lambda h: (h, 0, 0, 0))],
        out_shape=[jax.ShapeDtypeStruct((s, AW), BF16), jax.ShapeDtypeStruct((4, nu, 8, 256), F32)],
        compiler_params=_cp("parallel"),
    )(kp, qt, vt, bias2)


def _attn_bwd(name, q, qt, kp, kt, vp, bias2, do, dot, lse, dl):
    s = q.shape[0]
    nu = s // UNIT

    def body(q_ref, qt_ref, k_ref, kt_ref, v_ref, b_ref, do_ref, dot_ref, lse_ref, dl_ref,
             dq_ref, dk_ref, dv_ref, db_ref):
        dk_ref[...] = jnp.zeros_like(dk_ref)
        dv_ref[...] = jnp.zeros_like(dv_ref)
        db_ref[...] = jnp.zeros_like(db_ref)

        def unit(u, padded, carry):
            r0 = pl.multiple_of(u * UNIT, UNIT)
            rows, band = pl.ds(r0, UNIT), pl.ds(r0, BAND2)
            sc = _scores_t(k_ref[band, :], _pair_weights(qt_ref[:, rows]), b_ref[...], r0, padded)
            pt = jnp.exp(sc - lse_ref[u][0:1, :])
            dpt = jnp.dot(v_ref[band, :], _pair_weights(dot_ref[:, rows]), preferred_element_type=F32)
            ds = pt * (dpt - dl_ref[u][0:1, :])
            db_ref[...] += ds[0:KB, 0:128] + ds[CH:BAND2, 128:256]
            dsb = ds.astype(BF16)
            dq_ref[rows, :] = _unpair(jnp.dot(kt_ref[:, band], dsb, preferred_element_type=F32))
            dk_ref[band, :] += jnp.dot(dsb, _pair_rows(q_ref[rows, :]), preferred_element_type=F32)
            dv_ref[band, :] += jnp.dot(pt.astype(BF16), _pair_rows(do_ref[rows, :]), preferred_element_type=F32)
            return carry

        _unit_loops(s, unit)

    row_q = pl.BlockSpec((s, 128), lambda h: (0, h))
    col_q = pl.BlockSpec((128, s), lambda h: (h, 0))
    row_k = pl.BlockSpec((s + PADR, 128), lambda h: (0, h))
    col_k = pl.BlockSpec((128, s + PADR), lambda h: (h, 0))
    stat = pl.BlockSpec((None, nu, 8, 256), lambda h: (h, 0, 0, 0))
    return pl.pallas_call(
        body, name=name, grid=(AW // 128,),
        in_specs=[row_q, col_q, row_k, col_k, row_k,
                  pl.BlockSpec((None, BAND2, 256), lambda h: (h, 0, 0)), row_q, col_q, stat, stat],
        out_specs=[row_q, row_k, row_k, pl.BlockSpec((None, KB, 128), lambda h: (h, 0, 0))],
        out_shape=[jax.ShapeDtypeStruct((s, AW), F32),
                   jax.ShapeDtypeStruct((s + PADR, AW), F32),
                   jax.ShapeDtypeStruct((s + PADR, AW), F32),
                   jax.ShapeDtypeStruct((4, KB, 128), F32)],
        compiler_params=_cp("parallel"),
    )(q, qt, kp, kt, vp, bias2, do, dot, lse, dl)


def _rowsum_layout(dl, nu):
    d = dl[:, :8].reshape(nu, 2, CH, 4, 2)
    d = d.transpose(3, 0, 1, 4, 2).reshape(4, nu, 1, 256)
    return jnp.broadcast_to(d, (4, nu, 8, 256))


def _rows_before(cur, prev, k):
    row = lax.broadcasted_iota(jnp.int32, cur.shape, 0)
    return jnp.where(row >= k, pltpu.roll(cur, k, 0), pltpu.roll(prev, k, 0))


def _rows_after(cur, nxt, k):
    n = cur.shape[0]
    row = lax.broadcasted_iota(jnp.int32, cur.shape, 0)
    return jnp.where(row < n - k, pltpu.roll(cur, n - k, 0), pltpu.roll(nxt, n - k, 0))


def _pool_window_lanes():
    lg = lax.broadcasted_iota(jnp.int32, (1, PWD), 1) // 64
    return lg, jnp.where(lg == 0, 2.0, jnp.where(lg == 1, 4.0, jnp.where(lg == 2, 8.0, 16.0))).astype(F32)


def _pool_mean_minus_token(u, up, row0):
    lg, wv = _pool_window_lanes()
    sums = []
    c, p = u, up
    for k in (1, 2, 4, 8):
        c2 = c + _rows_before(c, p, k)
        p = p + pltpu.roll(p, k, 0)
        c = c2
        sums.append(c)
    win = jnp.where(lg == 0, sums[0], jnp.where(lg == 1, sums[1], jnp.where(lg == 2, sums[2], sums[3])))
    pos1 = (row0 + lax.broadcasted_iota(jnp.int32, u.shape, 0) + 1).astype(F32)
    cnt = jnp.minimum(pos1, wv)
    return win / cnt - u, cnt


def _conv_taps(z, zp, w0, w1, w2):
    z1 = _rows_before(z, zp, 1)
    z2 = _rows_before(z, zp, 2)
    return (w0 * z2 + w1 * z1) + w2 * z, z1, z2


CP_TM = 512


def _convpool_fwd(name, p, o, cw, pwbd, ps):
    s = p.shape[0]
    tm = CP_TM
    nb = s // tm

    def body(gb_ref, gc_ref, hin_ref, u_ref, gcp_ref, hinp_ref, up_ref, o_ref, cw_ref, pw_ref, ps_ref, mix_ref):
        i = pl.program_id(0)
        has_prev = i > 0
        z = gc_ref[...] * hin_ref[...]
        zp = jnp.where(has_prev, gcp_ref[...] * hinp_ref[...], 0.0)
        y3, _, _ = _conv_taps(z, zp, cw_ref[0:1, :], cw_ref[1:2, :], cw_ref[2:3, :])
        m, _ = _pool_mean_minus_token(u_ref[...], jnp.where(has_prev, up_ref[...], 0.0), i * tm)
        yp = jnp.dot(m.astype(BF16), pw_ref[...].astype(BF16), preferred_element_type=F32) * ps_ref[...]
        mix_ref[:, 0:AW] = o_ref[...]
        mix_ref[:, AW:AW + CW] = (gb_ref[...] * y3).astype(BF16)
        mix_ref[:, AW + CW:D] = yp.astype(BF16)

    def cur(col):
        return pl.BlockSpec((tm, CW), lambda i: (i, col))

    def prev(col):
        return pl.BlockSpec((tm, CW), lambda i: (jnp.maximum(i - 1, 0), col))

    def whole(a):
        return pl.BlockSpec(a.shape, lambda i: (0,) * a.ndim)

    return pl.pallas_call(
        body, name=name, grid=(nb,),
        in_specs=[cur(6), cur(7), cur(8), cur(9), prev(7), prev(8), prev(9),
                  pl.BlockSpec((tm, AW), lambda i: (i, 0)), whole(cw), whole(pwbd), whole(ps)],
        out_specs=pl.BlockSpec((tm, D), lambda i: (i, 0)),
        out_shape=jax.ShapeDtypeStruct((s, D), BF16),
        compiler_params=_cp("parallel"),
    )(p, p, p, p, p, p, p, o, cw, pwbd, ps)


def _convpool_bwd(name, p, dmix, cw, pwbd, ps):
    s = p.shape[0]
    tm = CP_TM
    nb = s // tm

    def body(gb_ref, gc_ref, hin_ref, u_ref, gcp_ref, hinp_ref, up_ref, gbn_ref, dyc_ref, dyp_ref, dycn_ref, dypn_ref,
             cw_ref, pw_ref, ps_ref, dcp_ref, dw0_ref, dw1_ref, dw2_ref, dps_ref, dpw_ref):
        i = pl.program_id(0)
        has_prev = i > 0
        has_next = i < nb - 1
        w0, w1, w2 = cw_ref[0:1, :], cw_ref[1:2, :], cw_ref[2:3, :]
        gb, gc, hin = gb_ref[...], gc_ref[...], hin_ref[...]
        dyc = dyc_ref[...]
        z = gc * hin
        zp = jnp.where(has_prev, gcp_ref[...] * hinp_ref[...], 0.0)
        y3, z1, z2 = _conv_taps(z, zp, w0, w1, w2)
        dy3 = dyc * gb
        dy3n = jnp.where(has_next, dycn_ref[...] * gbn_ref[...], 0.0)
        dz = w2 * dy3 + w1 * _rows_after(dy3, dy3n, 1) + w0 * _rows_after(dy3, dy3n, 2)
        pw = pw_ref[...].astype(BF16)
        psv = ps_ref[...]
        m, cnt = _pool_mean_minus_token(u_ref[...], jnp.where(has_prev, up_ref[...], 0.0), i * tm)
        mb = m.astype(BF16)
        dyp = dyp_ref[...]
        dmp = (dyp * psv).astype(BF16)
        dmpn = jnp.where(has_next, dypn_ref[...] * psv, 0.0).astype(BF16)
        nt = (((1,), (1,)), ((), ()))
        dm = lax.dot_general(dmp, pw, nt, preferred_element_type=F32)
        dmn = lax.dot_general(dmpn, pw, nt, preferred_element_type=F32)
        lg, wv = _pool_window_lanes()
        cc, cn = dm / cnt, dmn / wv
        sums = []
        for k in (1, 2, 4, 8):
            c2 = cc + _rows_after(cc, cn, k)
            cn = cn + pltpu.roll(cn, tm - k, 0)
            cc = c2
            sums.append(cc)
        du = jnp.where(lg == 0, sums[0], jnp.where(lg == 1, sums[1], jnp.where(lg == 2, sums[2], sums[3]))) - dm
        dcp_ref[:, 0:CW] = (dyc * y3).astype(BF16)
        dcp_ref[:, CW:2 * CW] = (dz * hin).astype(BF16)
        dcp_ref[:, 2 * CW:3 * CW] = (dz * gc).astype(BF16)
        dcp_ref[:, 3 * CW:4 * CW] = du.astype(BF16)
        parts = (jnp.sum(dy3 * z2, axis=0, keepdims=True),
                 jnp.sum(dy3 * z1, axis=0, keepdims=True),
                 jnp.sum(dy3 * z, axis=0, keepdims=True),
                 jnp.sum(dyp * jnp.dot(mb, pw, preferred_element_type=F32), axis=0, keepdims=True),
                 lax.dot_general(mb, dmp, (((0,), (0,)), ((), ())), preferred_element_type=F32))
        accs = (dw0_ref, dw1_ref, dw2_ref, dps_ref, dpw_ref)

        @pl.when(i == 0)
        def _():
            for a, v in zip(accs, parts):
                a[...] = v

        @pl.when(i > 0)
        def _():
            for a, v in zip(accs, parts):
                a[...] += v

    def cur(col):
        return pl.BlockSpec((tm, CW), lambda i: (i, col))

    def prev(col):
        return pl.BlockSpec((tm, CW), lambda i: (jnp.maximum(i - 1, 0), col))

    def nxt(col):
        return pl.BlockSpec((tm, CW), lambda i: (jnp.minimum(i + 1, nb - 1), col))

    def whole(shape):
        return pl.BlockSpec(shape, lambda i: (0,) * len(shape))

    row = jax.ShapeDtypeStruct((1, CW), F32)
    return pl.pallas_call(
        body, name=name, grid=(nb,),
        in_specs=[cur(6), cur(7), cur(8), cur(9), prev(7), prev(8), prev(9), nxt(6),
                  cur(0), cur(1), nxt(0), nxt(1), whole(cw.shape), whole(pwbd.shape), whole(ps.shape)],
        out_specs=[pl.BlockSpec((tm, D), lambda i: (i, 0)), whole((1, CW)), whole((1, CW)), whole((1, CW)),
                   whole((1, PWD)), whole((PWD, PWD))],
        out_shape=[jax.ShapeDtypeStruct((s, D), BF16), row, row, row, row,
                   jax.ShapeDtypeStruct((PWD, PWD), F32)],
        compiler_params=_cp("arbitrary"),
    )(p, p, p, p, p, p, p, p, dmix, dmix, dmix, dmix, cw, pwbd, ps)


def _qkv_bwd(name, p, dq, dkp, dvp, dcp, qg, kg):
    s = p.shape[0]
    tm = 256
    off = PADR // tm

    def body(pq_ref, pk_ref, dq_ref, dk_ref, dv_ref, dcp_ref, qg_ref, kg_ref, dp_ref, dqg_ref, dkg_ref):
        i = pl.program_id(0)
        hm = _head_mean_matrix()

        def nrm_bwd(x, g, dy):
            r = lax.rsqrt(_head_mean(x * x, hm) + EPS)
            xn = x * r
            dxn = dy * g
            dx = r * (dxn - xn * _head_mean(dxn * xn, hm))
            dg = jnp.sum(dy * xn, axis=0, keepdims=True)
            dg = (dg[:, 0:128] + dg[:, 128:256]) + (dg[:, 256:384] + dg[:, 384:512])
            return dx, dg + pltpu.roll(dg, HD, 1)

        dxq, dgq = nrm_bwd(pq_ref[...], qg_ref[...], dq_ref[...] * 0.125)
        dxk, dgk = nrm_bwd(pk_ref[...], kg_ref[...], dk_ref[...])
        dp_ref[:, 0:AW] = dxq.astype(BF16)
        dp_ref[:, AW:2 * AW] = dxk.astype(BF16)
        dp_ref[:, 2 * AW:3 * AW] = dv_ref[...].astype(BF16)
        dp_ref[:, 3 * AW:DIN] = dcp_ref[...]

        @pl.when(i == 0)
        def _():
            dqg_ref[...] = dgq
            dkg_ref[...] = dgk

        @pl.when(i > 0)
        def _():
            dqg_ref[...] += dgq
            dkg_ref[...] += dgk

    gspec = pl.BlockSpec((1, AW), lambda i: (0, 0))
    gout = pl.BlockSpec((1, 128), lambda i: (0, 0))
    return pl.pallas_call(
        body, name=name, grid=(s // tm,),
        in_specs=[pl.BlockSpec((tm, AW), lambda i: (i, 0)), pl.BlockSpec((tm, AW), lambda i: (i, 1)),
                  pl.BlockSpec((tm, AW), lambda i: (i, 0)),
                  pl.BlockSpec((tm, AW), lambda i: (i + off, 0)),
                  pl.BlockSpec((tm, AW), lambda i: (i + off, 0)),
                  pl.BlockSpec((tm, D), lambda i: (i, 0)), gspec, gspec],
        out_specs=[pl.BlockSpec((tm, DIN), lambda i: (i, 0)), gout, gout],
        out_shape=[jax.ShapeDtypeStruct((s, DIN), BF16), jax.ShapeDtypeStruct((1, 128), F32),
                   jax.ShapeDtypeStruct((1, 128), F32)],
        compiler_params=_cp("arbitrary"),
    )(p, p, dq, dkp, dvp, dcp, qg, kg)


def _loss_grad(name, y, t):
    s = y.shape[0]
    tm = 512

    def body(y_ref, t_ref, dy_ref, dyb_ref, l_ref):
        i = pl.program_id(0)
        e = y_ref[...] - t_ref[...]
        dy = e * (1.0 / D)
        dy_ref[...] = dy
        dyb_ref[...] = dy.astype(BF16)
        part = 0.5 * jnp.sum(jnp.mean(e * e, axis=-1, keepdims=True), axis=0, keepdims=True)

        @pl.when(i == 0)
        def _():
            l_ref[...] = part

        @pl.when(i > 0)
        def _():
            l_ref[...] += part

    blk = pl.BlockSpec((tm, D), lambda i: (i, 0))
    return pl.pallas_call(
        body, name=name, grid=(s // tm,),
        in_specs=[blk, blk],
        out_specs=[blk, blk, pl.BlockSpec((1, 1), lambda i: (0, 0))],
        out_shape=[jax.ShapeDtypeStruct((s, D), F32), jax.ShapeDtypeStruct((s, D), BF16),
                   jax.ShapeDtypeStruct((1, 1), F32)],
        compiler_params=_cp("arbitrary"),
    )(y, t)


def _mm_nt_relu(name, dxb, w, l, a):
    s = dxb.shape[0]
    tm = MM_ROWS

    def body(d_ref, w_ref, a_ref, o_ref):
        df = lax.dot_general(d_ref[...], w_ref[...], (((1,), (1,)), ((), ())), preferred_element_type=F32)
        o_ref[...] = (df * (2.0 * jnp.maximum(a_ref[...].astype(F32), 0.0))).astype(BF16)

    return pl.pallas_call(
        body, name=name, grid=(s // tm,),
        in_specs=[pl.BlockSpec((tm, D), lambda i: (i, 0)),
                  pl.BlockSpec((None, DFF, D), lambda i: (l, 0, 0)),
                  pl.BlockSpec((tm, DFF), lambda i: (i, 0))],
        out_specs=pl.BlockSpec((tm, DFF), lambda i: (i, 0)),
        out_shape=jax.ShapeDtypeStruct((s, DFF), BF16),
        compiler_params=_cp("parallel"),
    )(dxb, w, a)


def _proj_out_bwd(name, dxb, w, l, mix):
    s = dxb.shape[0]
    tm = 512

    def body(d_ref, w_ref, o_ref, do_ref, dot_ref, dcp_ref, dl_ref):
        d = d_ref[...]
        wa, wc = w_ref[0:AW, :], w_ref[AW:D, :]
        do = lax.dot_general(d, wa, NT_DIMS, preferred_element_type=F32)
        do_ref[...] = do.astype(BF16)
        dot_ref[...] = lax.dot_general(wa, d, NT_DIMS, preferred_element_type=F32).astype(BF16)
        dcp_ref[...] = lax.dot_general(d, wc, NT_DIMS, preferred_element_type=F32)
        head = lax.broadcasted_iota(jnp.int32, (AW, 128), 0) // HD
        pick = jnp.where(head == lax.broadcasted_iota(jnp.int32, (AW, 128), 1), 1.0, 0.0).astype(BF16)
        dl_ref[...] = _two_pass_dot(do * o_ref[...].astype(F32), pick)

    return pl.pallas_call(
        body, name=name, grid=(s // tm,),
        in_specs=[pl.BlockSpec((tm, D), lambda i: (i, 0)),
                  pl.BlockSpec((None, D, D), lambda i: (l, 0, 0)),
                  pl.BlockSpec((tm, AW), lambda i: (i, 0))],
        out_specs=[pl.BlockSpec((tm, AW), lambda i: (i, 0)), pl.BlockSpec((AW, tm), lambda i: (0, i)),
                   pl.BlockSpec((tm, D - AW), lambda i: (i, 0)), pl.BlockSpec((tm, 128), lambda i: (i, 0))],
        out_shape=[jax.ShapeDtypeStruct((s, AW), BF16), jax.ShapeDtypeStruct((AW, s), BF16),
                   jax.ShapeDtypeStruct((s, D - AW), F32), jax.ShapeDtypeStruct((s, 128), F32)],
        compiler_params=_cp("parallel"),
    )(dxb, w, mix)


def _mm_nt_normbwd(name, gy, w, l, x, g, dres, dep):
    s, k = gy.shape
    tm = MM_ROWS

    def body(gy_ref, w_ref, x_ref, g_ref, dr_ref, dep_ref, dx_ref, dxb_ref, dg_ref):
        del dep_ref
        i = pl.program_id(0)
        dh = lax.dot_general(gy_ref[...], w_ref[...], (((1,), (1,)), ((), ())), preferred_element_type=F32)
        xv = x_ref[...]
        r = _inv_rms(xv)
        xn = xv * r
        dxn = dh * g_ref[...]
        dx = r * (dxn - xn * jnp.mean(dxn * xn, axis=-1, keepdims=True)) + dr_ref[...]
        dx_ref[...] = dx
        dxb_ref[...] = dx.astype(BF16)
        part = jnp.sum(dh * xn, axis=0, keepdims=True)

        @pl.when(i == 0)
        def _():
            dg_ref[...] = part

        @pl.when(i > 0)
        def _():
            dg_ref[...] += part

    blk = pl.BlockSpec((tm, D), lambda i: (i, 0))
    vec = pl.BlockSpec((1, D), lambda i: (0, 0))
    return pl.pallas_call(
        body, name=name, grid=(s // tm,),
        in_specs=[pl.BlockSpec((tm, k), lambda i: (i, 0)),
                  pl.BlockSpec((None, D, k), lambda i: (l, 0, 0)), blk, vec, blk, ANY],
        out_specs=[blk, blk, vec],
        out_shape=[jax.ShapeDtypeStruct((s, D), F32), jax.ShapeDtypeStruct((s, D), BF16),
                   jax.ShapeDtypeStruct((1, D), F32)],
        compiler_params=_cp("arbitrary"),
    )(gy, w, x, g, dres, dep)


def _mm_tn(name, a, b, tma, tnb, relu2=False):
    s, m = a.shape
    n = b.shape[1]

    def body(a_ref, b_ref, o_ref):
        av = _relu2(a_ref[...]) if relu2 else a_ref[...]
        o_ref[...] = lax.dot_general(av, b_ref[...], (((0,), (0,)), ((), ())),
                                     preferred_element_type=F32).astype(BF16)

    return pl.pallas_call(
        body, name=name, grid=(m // tma, n // tnb),
        in_specs=[pl.BlockSpec((s, tma), lambda i, j: (0, i)),
                  pl.BlockSpec((s, tnb), lambda i, j: (0, j))],
        out_specs=pl.BlockSpec((tma, tnb), lambda i, j: (i, j)),
        out_shape=jax.ShapeDtypeStruct((m, n), BF16),
        compiler_params=_cp("parallel", "parallel"),
    )(a, b)


def _adamw_math(gv, wv, mv, vv):
    mn = ADAM_B1 * mv + (1.0 - ADAM_B1) * gv
    vn = ADAM_B2 * vv + (1.0 - ADAM_B2) * jnp.square(gv)
    m_hat = mn / (1.0 - ADAM_B1 ** ADAM_STEP)
    v_hat = vn / (1.0 - ADAM_B2 ** ADAM_STEP)
    return gv, -ADAM_LR * (m_hat / (jnp.sqrt(v_hat) + ADAM_EPS) + ADAM_WD * wv), mn, vn


def _adamw(name, g, w, m, v):
    r, c = g.shape
    tm = 256 if r % 256 == 0 else r

    def body(g_ref, w_ref, m_ref, v_ref, go_ref, d_ref, mo_ref, vo_ref):
        go_ref[...], d_ref[...], mo_ref[...], vo_ref[...] = _adamw_math(g_ref[...], w_ref[...], m_ref[...], v_ref[...])

    blk = pl.BlockSpec((tm, c), lambda i: (i, 0))
    return pl.pallas_call(
        body, name=name, grid=(r // tm,),
        in_specs=[blk] * 4, out_specs=[blk] * 4,
        out_shape=[jax.ShapeDtypeStruct((r, c), F32)] * 4,
        compiler_params=_cp("parallel"),
    )(g, w, m, v)


def _place():
    x, y, c = lax.axis_index("x"), lax.axis_index("y"), lax.axis_index("c")
    chips = [(1 - x, y), (x, 1 - y), (1 - x, 1 - y)]
    return x, y, c, chips


BLOCK_AXIS = (2, 1, 2, 1)
LARGE_DIMS = ((D, DIN), (D, D), (D, DFF), (DFF, D))


def _full_shape(t, layers, dtype):
    r, c = LARGE_DIMS[t]
    return jax.ShapeDtypeStruct((layers, r, c), dtype)


def _cast_into_full(name, t, shard, b1, dep):
    _, r, c = shard.shape
    tm = min(256, r)
    if BLOCK_AXIS[t] == 1:
        out_spec = pl.BlockSpec((None, tm, c), lambda l, i, br: (l, br[0] * (r // tm) + i, 0))
    else:
        out_spec = pl.BlockSpec((None, tm, c), lambda l, i, br: (l, i, br[0]))

    def body(b_ref, x_ref, dep_ref, o_ref):
        del b_ref, dep_ref
        o_ref[...] = x_ref[...].astype(BF16)

    return pl.pallas_call(
        body, name=name,
        grid_spec=pltpu.PrefetchScalarGridSpec(
            num_scalar_prefetch=1, grid=(DEPTH, r // tm),
            in_specs=[pl.BlockSpec((None, tm, c), lambda l, i, br: (l, i, 0)), ANY],
            out_specs=out_spec),
        out_shape=_full_shape(t, DEPTH, BF16),
        compiler_params=_cp("parallel", "parallel"),
    )(b1, shard, dep)


HBM = pl.BlockSpec(memory_space=pltpu.HBM)
SEM = pl.BlockSpec(memory_space=pltpu.SEMAPHORE)
DATAFLOW = pltpu.SideEffectType.DATAFLOW_SIDE_EFFECTING


def _half(ref, l, t, b, c):
    r, cols = LARGE_DIMS[t]
    if BLOCK_AXIS[t] == 1:
        n = r // 8
        return ref.at[l, pl.ds(pl.multiple_of(b * (2 * n) + c * n, 16), n), :]
    n, w = r // 2, cols // 4
    return ref.at[l, pl.ds(pl.multiple_of(c * n, 16), n), pl.ds(pl.multiple_of(b * w, 128), w)]


def _gather_start(name, layers, fulls):
    def body(*refs):
        f_refs, sems = refs[4:8], refs[8:8 + 2 * len(layers)]
        x, y, c, chips = _place()
        for i, l in enumerate(layers):
            for t in range(4):
                own = _half(f_refs[t], l, t, 2 * x + y, c)
                for j, (cx, cy) in enumerate(chips):
                    pltpu.make_async_remote_copy(src_ref=own, dst_ref=own, send_sem=sems[2 * i].at[3 * t + j],
                                                 recv_sem=sems[2 * i + 1].at[3 * t + j], device_id=(cx, cy, c),
                                                 device_id_type=MESH).start()

    outs = pl.pallas_call(
        body, name=name,
        in_specs=[HBM] * 4, out_specs=[HBM] * 4 + [SEM] * (2 * len(layers)),
        out_shape=[pltpu.HBM(s.shape, s.dtype) for s in (_full_shape(t, DEPTH, BF16) for t in range(4))]
        + [pltpu.SemaphoreType.DMA((12,))] * (2 * len(layers)),
        input_output_aliases={t: t for t in range(4)},
        compiler_params=pltpu.CompilerParams(has_side_effects=DATAFLOW),
    )(*[pltpu.with_memory_space_constraint(f, pltpu.HBM) for f in fulls])
    return outs[0:4], {l: (outs[4 + 2 * i], outs[5 + 2 * i]) for i, l in enumerate(layers)}


def _gather_wait(name, l, ts, fulls, sems, after):
    def body(*refs):
        send_sems, recv_sems, f_refs = refs[4], refs[5], refs[7:11]
        x, y, c, chips = _place()
        for t in ts:
            own = _half(f_refs[t], l, t, 2 * x + y, c)
            for j, (cx, cy) in enumerate(chips):
                landed = _half(f_refs[t], l, t, 2 * cx + cy, c)
                pltpu.make_async_remote_copy(src_ref=own, dst_ref=landed, send_sem=send_sems.at[3 * t + j],
                                             recv_sem=recv_sems.at[3 * t + j], device_id=(cx, cy, c),
                                             device_id_type=MESH).wait()

    return pl.pallas_call(
        body, name=name,
        in_specs=[HBM] * 4 + [SEM, SEM, ANY], out_specs=[HBM] * 4,
        out_shape=[pltpu.HBM(s.shape, s.dtype) for s in (_full_shape(t, DEPTH, BF16) for t in range(4))],
        input_output_aliases={t: t for t in range(4)},
        compiler_params=pltpu.CompilerParams(has_side_effects=DATAFLOW),
    )(*fulls, sems[0], sems[1], after)


def _pass_on(name, l, ts, fulls):
    def body(*refs):
        f_refs, send_sems, recv_sems = refs[4:8], refs[8], refs[9]
        x, y, c, chips = _place()

        def copy(t, j, half):
            cx, cy = chips[j]
            part = _half(f_refs[t], l, t, 2 * cx + cy, half)
            return pltpu.make_async_remote_copy(src_ref=part, dst_ref=part, send_sem=send_sems.at[3 * t + j],
                                                recv_sem=recv_sems.at[3 * t + j], device_id=(x, y, 1 - c),
                                                device_id_type=MESH)

        for t in ts:
            for j in range(3):
                copy(t, j, c).start()
        for t in ts:
            for j in range(3):
                copy(t, j, 1 - c).wait_recv()
                copy(t, j, c).wait_send()

    return pl.pallas_call(
        body, name=name,
        in_specs=[ANY] * 4, out_specs=[ANY] * 4,
        out_shape=[_full_shape(t, DEPTH, BF16) for t in range(4)],
        input_output_aliases={t: t for t in range(4)},
        scratch_shapes=[pltpu.SemaphoreType.DMA((12,)), pltpu.SemaphoreType.DMA((12,))],
    )(*fulls)


def _block2d(ref, t, b):
    r, cols = LARGE_DIMS[t]
    if BLOCK_AXIS[t] == 1:
        return ref.at[pl.ds(pl.multiple_of(b * (r // 4), 16), r // 4), :]
    return ref.at[:, pl.ds(pl.multiple_of(b * (cols // 4), 128), cols // 4)]


def _block_dims(t):
    r, cols = LARGE_DIMS[t]
    return (r // 4, cols) if BLOCK_AXIS[t] == 1 else (r, cols // 4)


def _reduce_copies(ts, g_refs, r_refs, send_sems, recv_sems):
    _, _, c, chips = _place()
    return [pltpu.make_async_remote_copy(src_ref=_block2d(g_refs[i], t, 2 * cx + cy), dst_ref=r_refs[i].at[j],
                                         send_sem=send_sems.at[3 * i + j], recv_sem=recv_sems.at[3 * i + j],
                                         device_id=(cx, cy, c), device_id_type=MESH)
            for i, t in enumerate(ts) for j, (cx, cy) in enumerate(chips)]


def _reduce_start(name, ts, grads):
    n = len(ts)

    def body(*refs):
        for cp in _reduce_copies(ts, refs[n:2 * n], refs[2 * n:3 * n], refs[3 * n], refs[3 * n + 1]):
            cp.start()
        refs[3 * n + 2][...] = jnp.zeros((8, 128), F32)

    outs = pl.pallas_call(
        body, name=name,
        in_specs=[HBM] * n,
        out_specs=[HBM] * (2 * n) + [SEM, SEM, pl.BlockSpec(memory_space=pltpu.VMEM)],
        out_shape=[pltpu.HBM(g.shape, BF16) for g in grads]
        + [pltpu.HBM((3,) + _block_dims(t), BF16) for t in ts]
        + [pltpu.SemaphoreType.DMA((3 * n,)), pltpu.SemaphoreType.DMA((3 * n,)), jax.ShapeDtypeStruct((8, 128), F32)],
        input_output_aliases={i: i for i in range(n)},
        compiler_params=pltpu.CompilerParams(has_side_effects=DATAFLOW),
    )(*[pltpu.with_memory_space_constraint(g, pltpu.HBM) for g in grads])
    return outs[0:n], outs[n:2 * n], (outs[2 * n], outs[2 * n + 1]), outs[2 * n + 2]


def _reduce_wait(name, ts, grads, landing, sems, afters):
    n = len(ts)
    first_out = 2 * n + 2 + len(afters)

    def body(*refs):
        for cp in _reduce_copies(ts, refs[first_out:first_out + n], refs[first_out + n:first_out + 2 * n],
                                 refs[2 * n], refs[2 * n + 1]):
            cp.wait()

    outs = pl.pallas_call(
        body, name=name,
        in_specs=[HBM] * (2 * n) + [SEM, SEM] + [ANY] * len(afters), out_specs=[HBM] * (2 * n),
        out_shape=[pltpu.HBM(g.shape, BF16) for g in grads] + [pltpu.HBM(r.shape, BF16) for r in landing],
        input_output_aliases={i: i for i in range(2 * n)},
        compiler_params=pltpu.CompilerParams(has_side_effects=DATAFLOW),
    )(*grads, *landing, sems[0], sems[1], *afters)
    return outs[0:n], outs[n:2 * n]


def _add4(name, t, own, landed, b1):
    rb, cb = _block_dims(t)
    tm = min(256, rb)
    if BLOCK_AXIS[t] == 1:
        own_spec = pl.BlockSpec((tm, cb), lambda i, br: (br[0] * (rb // tm) + i, 0))
    else:
        own_spec = pl.BlockSpec((tm, cb), lambda i, br: (i, br[0]))

    def body(b_ref, o_ref, r0_ref, r1_ref, r2_ref, s_ref):
        del b_ref
        s_ref[...] = ((o_ref[...].astype(F32) + r0_ref[...].astype(F32))
                      + (r1_ref[...].astype(F32) + r2_ref[...].astype(F32))).astype(BF16)

    def got(j):
        return pl.BlockSpec((None, tm, cb), lambda i, br: (j, i, 0))

    return pl.pallas_call(
        body, name=name,
        grid_spec=pltpu.PrefetchScalarGridSpec(
            num_scalar_prefetch=1, grid=(rb // tm,),
            in_specs=[own_spec, got(0), got(1), got(2)],
            out_specs=pl.BlockSpec((tm, cb), lambda i, br: (i, 0))),
        out_shape=jax.ShapeDtypeStruct((rb, cb), BF16),
        compiler_params=_cp("parallel"),
    )(b1, own, landed, landed, landed)


def _swap_sib(name, sums):
    def body(*refs):
        s_refs, t_refs, send_sems, recv_sems = refs[0:4], refs[4:8], refs[8], refs[9]
        x, y, c, _ = _place()
        cps = [pltpu.make_async_remote_copy(src_ref=s_refs[t], dst_ref=t_refs[t], send_sem=send_sems.at[t],
                                            recv_sem=recv_sems.at[t], device_id=(x, y, 1 - c), device_id_type=MESH)
               for t in range(4)]
        for cp in cps:
            cp.start()
        for cp in cps:
            cp.wait()

    return pl.pallas_call(
        body, name=name,
        in_specs=[ANY] * 4, out_specs=[ANY] * 4,
        out_shape=[jax.ShapeDtypeStruct(s.shape, BF16) for s in sums],
        scratch_shapes=[pltpu.SemaphoreType.DMA((4,)), pltpu.SemaphoreType.DMA((4,))],
    )(*sums)


def _adamw_pair(name, l, s_own, s_sib, w, m, v, outs):
    rb, cb = s_own.shape
    tm = min(256, rb)

    def body(a_ref, b_ref, w_ref, m_ref, v_ref, g0, d0, m0, v0, go_ref, d_ref, mo_ref, vo_ref):
        del g0, d0, m0, v0
        gv = a_ref[...].astype(F32) + b_ref[...].astype(F32)
        go_ref[...], d_ref[...], mo_ref[...], vo_ref[...] = _adamw_math(gv, w_ref[...], m_ref[...], v_ref[...])

    part = pl.BlockSpec((tm, cb), lambda i: (i, 0))
    layer = pl.BlockSpec((None, tm, cb), lambda i: (l, i, 0))
    return pl.pallas_call(
        body, name=name, grid=(rb // tm,),
        in_specs=[part, part, layer, layer, layer] + [ANY] * 4,
        out_specs=[layer] * 4,
        out_shape=[jax.ShapeDtypeStruct((DEPTH, rb, cb), F32)] * 4,
        input_output_aliases={5 + i: i for i in range(4)},
        compiler_params=_cp("parallel"),
    )(s_own, s_sib, w, m, v, *outs)


def _all_gather8(name, v, dep):
    m_per, n = v.shape

    def body(v_ref, dep_ref, out_ref, send_sems, recv_sems, local_sem):
        del dep_ref
        x, y, c, chips = _place()
        me, sib = (x, y, c), (x, y, 1 - c)

        def rows(px, py, pc):
            return out_ref.at[pl.ds((4 * px + 2 * py + pc) * m_per, m_per), :]

        def copy(k, block, to, src=None):
            return pltpu.make_async_remote_copy(
                src_ref=rows(*block) if src is None else src, dst_ref=rows(*block),
                send_sem=send_sems.at[k], recv_sem=recv_sems.at[k], device_id=to, device_id_type=MESH)

        mine = pltpu.make_async_copy(v_ref, rows(*me), local_sem)
        mine.start()
        first = [copy(0, me, sib, src=v_ref)]
        first += [copy(1 + j, me, (*chip, c), src=v_ref) for j, chip in enumerate(chips)]
        for cp in first:
            cp.start()
        passed = [copy(4 + j, (*chip, c), sib) for j, chip in enumerate(chips)]
        for j, chip in enumerate(chips):
            copy(1 + j, (*chip, c), me).wait_recv()
            passed[j].start()
        copy(0, sib, me).wait_recv()
        for j, chip in enumerate(chips):
            copy(4 + j, (*chip, 1 - c), me).wait_recv()
        for cp in first + passed:
            cp.wait_send()
        mine.wait()

    return pl.pallas_call(
        body, name=name,
        out_shape=jax.ShapeDtypeStruct((8 * m_per, n), v.dtype),
        in_specs=[pl.BlockSpec(memory_space=pltpu.VMEM), ANY],
        out_specs=pl.BlockSpec(memory_space=pltpu.VMEM),
        scratch_shapes=[pltpu.SemaphoreType.DMA((7,)), pltpu.SemaphoreType.DMA((7,)), pltpu.SemaphoreType.DMA],
    )(v, dep)


def _sum8(name, g):
    def body(g_ref, o_ref):
        acc = g_ref[0]
        for d in range(1, 8):
            acc = acc + g_ref[d]
        o_ref[...] = acc

    return pl.pallas_call(body, name=name, out_shape=jax.ShapeDtypeStruct(g.shape[1:], F32))(g)


def _pack(parts):
    flat = []
    for a in parts:
        a = a.reshape(-1)
        flat.append(jnp.pad(a, (0, (-a.shape[0]) % 128)))
    cat = jnp.concatenate(flat)
    cat = jnp.pad(cat, (0, (-cat.shape[0]) % 1024))
    return cat.reshape(-1, 128)


def _unpack(packed, shapes):
    flat = packed.reshape(-1)
    out, at = [], 0
    for shp in shapes:
        n = 1
        for d in shp:
            n *= d
        out.append(flat[at:at + n].reshape(shp))
        at += n + (-n) % 128
    return out


def _local_step(x, target, layer_weights, on_grads, small):
    qg_all = jnp.tile(small["q_norm_g"], (1, 8))
    kg_all = jnp.tile(small["k_norm_g"], (1, 8))
    bias_all = _bias_layout(_bias_expand("bias_expand", jnp.pad(small["rel_bias"], ((0, 0), (0, 0), (0, NIDX - 257)))))
    same_group = jnp.eye(4, dtype=F32)[None, :, None, :, None]
    pwbd_all = (small["pool_w"][:, :, :, None, :] * same_group).reshape(DEPTH, PWD, PWD)
    saved = []
    xin = x
    h = _rmsnorm("norm_first", x, small["norm1_g"][0:1])
    for l in range(DEPTH):
        w_in = layer_weights(l, (0,), xin)[0]
        qg, kg, bias = qg_all[l:l + 1], kg_all[l:l + 1], bias_all[l]
        cw, pwbd, ps = small["conv_w"][l], pwbd_all[l], small["pool_scale"][l:l + 1]
        p = _mm_nn(f"proj_in_{l}", h, w_in, l, F32)
        q, qt, kp, kt, vp, vt = _qkv(f"qkv_{l}", p, qg, kg)
        o, lse = _attn_fwd(f"attn_fwd_{l}", kp, qt, vt, bias)
        w_in, w_out, w_1, w_2 = layer_weights(l, (1, 2, 3), o)
        mix = _convpool_fwd(f"convpool_fwd_{l}", p, o, cw, pwbd, ps)
        x1, h2 = _mm_res_norm(f"proj_out_{l}", mix, w_out, l, xin, small["norm2_g"][l:l + 1])
        a = _mm_nn(f"mlp1_{l}", h2, w_1, l, BF16)
        gnext = small["norm1_g"][(l + 1) % DEPTH][None]
        x2, hnext = _mm_res_norm(f"mlp2_{l}", a, w_2, l, x1, gnext, relu2=True)
        saved.append(dict(xin=xin, h=h, p=p, q=q, qt=qt, kp=kp, kt=kt, vp=vp, bias=bias, mix=mix, x1=x1, h2=h2, a=a, lse=lse,
                          qg=qg, kg=kg, cw=cw, pwbd=pwbd, ps=ps))
        xin, h = x2, hnext

    dx, dxb, loss = _loss_grad("loss_grad", xin, target)
    raw = {k: [None] * DEPTH for k in ("dg1", "dqg", "dkg", "db", "dw0", "dw1", "dw2", "dpw", "dps", "dg2")}
    for l in reversed(range(DEPTH)):
        sv = saved[l]
        da = _mm_nt_relu(f"mlp2_bwd_{l}", dxb, w_2, l, sv["a"])
        g_2 = _mm_tn(f"mlp2_wgrad_{l}", sv["a"], dxb, 512, 1024, relu2=True)
        g_1 = _mm_tn(f"mlp1_wgrad_{l}", sv["h2"], da, 1024, 512)
        dep = on_grads(l, (2, 3), (g_1, g_2))
        dx1, dx1b, dg2 = _mm_nt_normbwd(f"mlp1_bwd_{l}", da, w_1, l, sv["x1"], small["norm2_g"][l:l + 1], dx, dep)
        do, dot, dmix, dl = _proj_out_bwd(f"proj_out_bwd_{l}", dx1b, w_out, l, sv["mix"])
        g_out = _mm_tn(f"proj_out_wgrad_{l}", sv["mix"], dx1b, 512, 1024)
        dcp, dw0, dw1, dw2, dps, dpw = _convpool_bwd(f"convpool_bwd_{l}", sv["p"], dmix, sv["cw"], sv["pwbd"], sv["ps"])
        dq, dkp, dvp, db = _attn_bwd(f"attn_bwd_{l}", sv["q"], sv["qt"], sv["kp"], sv["kt"], sv["vp"], sv["bias"],
                                     do, dot, sv["lse"], _rowsum_layout(dl, x.shape[0] // UNIT))
        dp, dqg, dkg = _qkv_bwd(f"qkv_bwd_{l}", sv["p"], dq, dkp, dvp, dcp, sv["qg"], sv["kg"])
        g_in = _mm_tn(f"proj_in_wgrad_{l}", sv["h"], dp, 1024, 640)
        dep = on_grads(l, (0, 1), (g_in, g_out))
        dx, dxb, dg1 = _mm_nt_normbwd(f"proj_in_bwd_{l}", dp, w_in, l, sv["xin"], small["norm1_g"][l:l + 1], dx1, dep)
        for k, val in dict(dg1=dg1, dqg=dqg, dkg=dkg, db=db, dw0=dw0, dw1=dw1, dw2=dw2, dpw=dpw, dps=dps, dg2=dg2).items():
            raw[k][l] = val
    cat = {k: jnp.concatenate(v, axis=0) for k, v in raw.items() if k not in ("db", "dpw")}
    drb = _bias_reduce("bias_reduce", _bias_unlayout(jnp.stack(raw["db"])))
    dpw = jnp.stack(raw["dpw"])
    gsmall = {
        "norm1_g": cat["dg1"], "q_norm_g": cat["dqg"][:, :HD], "k_norm_g": cat["dkg"][:, :HD],
        "rel_bias": drb[:, :, :257],
        "conv_w": jnp.stack([cat["dw0"], cat["dw1"], cat["dw2"]], axis=1),
        "pool_w": jnp.stack([dpw[:, g * 64:(g + 1) * 64, g * 64:(g + 1) * 64] for g in range(4)], axis=1),
        "pool_scale": cat["dps"], "norm2_g": cat["dg2"],
    }
    return loss, dx, gsmall


SMALL = ("norm1_g", "q_norm_g", "k_norm_g", "rel_bias", "conv_w", "pool_w", "pool_scale", "norm2_g")
LARGE = ("w_in", "w_out", "w_mlp1", "w_mlp2")


def kernel(x, norm1_g, w_in, q_norm_g, k_norm_g, rel_bias, conv_w, pool_w, pool_scale, w_out, norm2_g, w_mlp1, w_mlp2, loss_target, m_norm1_g, m_w_in, m_q_norm_g, m_k_norm_g, m_rel_bias, m_conv_w, m_pool_w, m_pool_scale, m_w_out, m_norm2_g, m_w_mlp1, m_w_mlp2, v_norm1_g, v_w_in, v_q_norm_g, v_k_norm_g, v_rel_bias, v_conv_w, v_pool_w, v_pool_scale, v_w_out, v_norm2_g, v_w_mlp1, v_w_mlp2):
    w = dict(norm1_g=norm1_g, w_in=w_in, q_norm_g=q_norm_g, k_norm_g=k_norm_g, rel_bias=rel_bias, conv_w=conv_w,
             pool_w=pool_w, pool_scale=pool_scale, w_out=w_out, norm2_g=norm2_g, w_mlp1=w_mlp1, w_mlp2=w_mlp2)
    m = dict(norm1_g=m_norm1_g, w_in=m_w_in, q_norm_g=m_q_norm_g, k_norm_g=m_k_norm_g, rel_bias=m_rel_bias,
             conv_w=m_conv_w, pool_w=m_pool_w, pool_scale=m_pool_scale, w_out=m_w_out, norm2_g=m_norm2_g,
             w_mlp1=m_w_mlp1, w_mlp2=m_w_mlp2)
    v = dict(norm1_g=v_norm1_g, w_in=v_w_in, q_norm_g=v_q_norm_g, k_norm_g=v_k_norm_g, rel_bias=v_rel_bias,
             conv_w=v_conv_w, pool_w=v_pool_w, pool_scale=v_pool_scale, w_out=v_w_out, norm2_g=v_norm2_g,
             w_mlp1=v_w_mlp1, w_mlp2=v_w_mlp2)
    ax, ay, ac = lax.axis_index("x"), lax.axis_index("y"), lax.axis_index("c")
    b1 = jnp.reshape(2 * ax + ay, (1,)).astype(jnp.int32)

    cw_rows = _all_gather8("gather_conv_w", jnp.pad(conv_w.reshape(DEPTH * 3, 64), ((0, 4), (0, 64))), b1)
    cw_chips = [cw_rows[(4 * cx + 2 * cy) * 16:(4 * cx + 2 * cy) * 16 + 12, :64] for cx in range(2) for cy in range(2)]
    small = {n: w[n] for n in SMALL}
    small["conv_w"] = jnp.concatenate(cw_chips, axis=1).reshape(DEPTH, 3, CW)

    casts = [_cast_into_full(f"cast_{n}", t, w[n], b1, cw_rows) for t, n in enumerate(LARGE)]
    first, first_sems = _gather_start("gather_start_first", (0,), casts)
    held = [first]
    sems = dict(first_sems)

    def layer_weights(l, ts, after):
        if l > 0:
            ts = (0, 1, 2, 3) if ts == (0,) else ()
        if ts:
            tag = f"{l}_{ts[0]}"
            arrived = _gather_wait(f"gather_wait_{tag}", l, ts, held[0], sems[l], after)
            if l == 0 and ts == (0,):
                arrived, rest_sems = _gather_start("gather_start_rest", tuple(range(1, DEPTH)), arrived)
                sems.update(rest_sems)
            held[0] = _pass_on(f"pass_on_{tag}", l, ts, arrived)
        return held[0]

    flights = {}

    def await_flight(l, ts, afters):
        g, landing, sm, _ = flights[l, ts]
        flights[l, ts] = _reduce_wait(f"reduce_wait_{l}_{ts[0]}", ts, g, landing, sm, afters)

    def on_grads(l, ts, grads):
        if ts == (0, 1) and l + 1 < DEPTH:
            await_flight(l + 1, (2, 3), [grads[0]])
            await_flight(l + 1, (0, 1), [grads[0]])
        flights[l, ts] = _reduce_start(f"reduce_start_{l}_{ts[0]}", ts, grads)
        return flights[l, ts][3]

    loss_part, grad_x, gsmall = _local_step(x[0], loss_target[0], layer_weights, on_grads, small)
    loss = lax.psum(loss_part[0, 0], ("x", "y", "c"))
    order = [n for n in SMALL]
    packed = _pack([gsmall[n] for n in order])

    out = {n: [lax.empty(w[n].shape, F32) for _ in range(4)] for n in LARGE}
    for l in reversed(range(DEPTH)):
        if l == 0:
            afters = [grad_x, packed] + [out[n][0] for n in LARGE]
            await_flight(0, (2, 3), afters)
            await_flight(0, (0, 1), afters)
        sums = [None] * 4
        for ts in ((0, 1), (2, 3)):
            g, landing = flights[l, ts]
            for i, t in enumerate(ts):
                sums[t] = _add4(f"add4_{LARGE[t]}_{l}", t, g[i], landing[i], b1)
        theirs = _swap_sib(f"swap_sib_{l}", sums)
        for t, n in enumerate(LARGE):
            out[n] = _adamw_pair(f"adamw_{n}_{l}", l, sums[t], theirs[t], w[n], m[n], v[n], out[n])

    rows = packed.shape[0]
    summed = _sum8("sum_small", _all_gather8("gather_small", packed, out[LARGE[0]][0]).reshape(8, rows, 128))
    gfull = dict(zip(order, _unpack(summed, [gsmall[n].shape for n in order])))
    gfull["conv_w"] = lax.dynamic_slice_in_dim(gfull["conv_w"], (2 * ax + ay) * 64, 64, axis=2)
    res = _adamw("adamw_small", _pack([gfull[n] for n in order]), _pack([w[n] for n in order]),
                 _pack([m[n] for n in order]), _pack([v[n] for n in order]))
    for n, parts in zip(order, zip(*[_unpack(r, [w[k].shape for k in order]) for r in res])):
        out[n] = list(parts)

    names = ("norm1_g", "w_in", "q_norm_g", "k_norm_g", "rel_bias", "conv_w", "pool_w", "pool_scale", "w_out",
             "norm2_g", "w_mlp1", "w_mlp2")
    flat = [loss, grad_x[None]]
    for i in range(4):
        flat += [out[n][i] for n in names]
    return tuple(flat)
```

```python
import functools

import jax
import jax.numpy as jnp
from jax import lax
from jax.experimental import pallas as pl
from jax.experimental.pallas import tpu as pltpu

F32 = jnp.float32
BF16 = jnp.bfloat16

D = 1024
DEPTH = 4
CH = 64
NPREV = 8
KB = (NPREV + 1) * CH
PADR = NPREV * CH
HD = 64
AW = 512
CW = 256
PWD = 256
DIN = 3 * AW + 3 * CW + PWD
DFF = 4 * D
NIDX = 384
EPS = 1e-6
NEG_INF = -1e30

ADAM_LR = 0.001
ADAM_B1 = 0.9
ADAM_B2 = 0.999
ADAM_EPS = 1e-08
ADAM_WD = 0.01
ADAM_STEP = 10

VMEM_LIMIT = 52 * 1024 * 1024
MM_ROWS = 512
MESH = pl.DeviceIdType.MESH
ANY = pl.BlockSpec(memory_space=pl.ANY)


def _cp(*sem):
    return pltpu.CompilerParams(dimension_semantics=sem, vmem_limit_bytes=VMEM_LIMIT)


def _inv_rms(x):
    return lax.rsqrt(jnp.mean(x * x, axis=-1, keepdims=True) + EPS)


def _head_mean_matrix():
    r = lax.broadcasted_iota(jnp.int32, (AW, AW), 0) // HD
    c = lax.broadcasted_iota(jnp.int32, (AW, AW), 1) // HD
    return jnp.where(r == c, 1.0 / HD, 0.0).astype(BF16)


def _two_pass_dot(x, m):
    hi = x.astype(BF16)
    lo = (x - hi.astype(F32)).astype(BF16)
    return (jnp.dot(hi, m, preferred_element_type=F32)
            + jnp.dot(lo, m, preferred_element_type=F32))


def _head_mean(x, hm):
    return _two_pass_dot(x, hm)


def _rmsnorm(name, x, g):
    s = x.shape[0]
    tm = 512

    def body(x_ref, g_ref, h_ref):
        xv = x_ref[...]
        h_ref[...] = (xv * _inv_rms(xv) * g_ref[...]).astype(BF16)

    return pl.pallas_call(
        body, name=name, grid=(s // tm,),
        in_specs=[pl.BlockSpec((tm, D), lambda i: (i, 0)), pl.BlockSpec((1, D), lambda i: (0, 0))],
        out_specs=pl.BlockSpec((tm, D), lambda i: (i, 0)),
        out_shape=jax.ShapeDtypeStruct((s, D), BF16),
        compiler_params=_cp("parallel"),
    )(x, g)


def _relu2(a):
    r = jnp.maximum(a, jnp.zeros_like(a))
    return r * r


def _mm_nn(name, a, w, l, out_dtype):
    s, k = a.shape
    n = w.shape[2]
    tm = MM_ROWS

    def body(a_ref, w_ref, o_ref):
        o_ref[...] = jnp.dot(a_ref[...], w_ref[...], preferred_element_type=F32).astype(o_ref.dtype)

    return pl.pallas_call(
        body, name=name, grid=(s // tm,),
        in_specs=[pl.BlockSpec((tm, k), lambda i: (i, 0)),
                  pl.BlockSpec((None, k, n), lambda i: (l, 0, 0))],
        out_specs=pl.BlockSpec((tm, n), lambda i: (i, 0)),
        out_shape=jax.ShapeDtypeStruct((s, n), out_dtype),
        compiler_params=_cp("parallel"),
    )(a, w)


def _mm_res_norm(name, a, w, l, res, g):
    s, k = a.shape
    tm = MM_ROWS

    def body(a_ref, w_ref, r_ref, g_ref, x_ref, h_ref):
        acc = r_ref[...] + jnp.dot(a_ref[...], w_ref[...], preferred_element_type=F32)
        x_ref[...] = acc
        h_ref[...] = (acc * _inv_rms(acc) * g_ref[...]).astype(BF16)

    return pl.pallas_call(
        body, name=name, grid=(s // tm,),
        in_specs=[pl.BlockSpec((tm, k), lambda i: (i, 0)),
                  pl.BlockSpec((None, k, D), lambda i: (l, 0, 0)),
                  pl.BlockSpec((tm, D), lambda i: (i, 0)),
                  pl.BlockSpec((1, D), lambda i: (0, 0))],
        out_specs=[pl.BlockSpec((tm, D), lambda i: (i, 0))] * 2,
        out_shape=[jax.ShapeDtypeStruct((s, D), F32), jax.ShapeDtypeStruct((s, D), BF16)],
        compiler_params=_cp("parallel"),
    )(a, w, res, g)


def _mlp_fwd(name, h2, w1, w2, l, res, g):
    s = h2.shape[0]
    tm = 256

    def body(h_ref, w1_ref, w2_ref, r_ref, g_ref, a_ref, x_ref, hn_ref):
        a = jnp.dot(h_ref[...], w1_ref[...], preferred_element_type=F32).astype(BF16)
        a_ref[...] = a
        acc = r_ref[...] + jnp.dot(_relu2(a), w2_ref[...], preferred_element_type=F32)
        x_ref[...] = acc
        hn_ref[...] = (acc * _inv_rms(acc) * g_ref[...]).astype(BF16)

    once = pl.Buffered(1)
    rows = pl.BlockSpec((tm, D), lambda i: (i, 0))
    return pl.pallas_call(
        body, name=name, grid=(s // tm,),
        in_specs=[rows,
                  pl.BlockSpec((None, D, DFF), lambda i: (l, 0, 0), pipeline_mode=once),
                  pl.BlockSpec((None, DFF, D), lambda i: (l, 0, 0), pipeline_mode=once),
                  rows, pl.BlockSpec((1, D), lambda i: (0, 0))],
        out_specs=[pl.BlockSpec((tm, DFF), lambda i: (i, 0)), rows, rows],
        out_shape=[jax.ShapeDtypeStruct((s, DFF), BF16), jax.ShapeDtypeStruct((s, D), F32),
                   jax.ShapeDtypeStruct((s, D), BF16)],
        compiler_params=_cp("parallel"),
    )(h2, w1, w2, res, g)


def _qkv(name, p, qg, kg):
    s = p.shape[0]
    tm = PADR
    nb = s // tm

    def body(pq_ref, pk_ref, pv_ref, qg_ref, kg_ref, q_ref, qt_ref, k_ref, kt_ref, v_ref, vt_ref):
        t = pl.program_id(0)
        hm = _head_mean_matrix()

        def nrm(x, g):
            return x * lax.rsqrt(_head_mean(x * x, hm) + EPS) * g

        first = t == 0
        qq = nrm(pq_ref[...], qg_ref[...]) * 0.125
        kk = jnp.where(first, 0.0, nrm(pk_ref[...], kg_ref[...]))
        vv = jnp.where(first, 0.0, pv_ref[...])
        q_ref[...] = qq.astype(BF16)
        qt_ref[...] = qq.T.astype(BF16)
        k_ref[...] = kk.astype(BF16)
        kt_ref[...] = kk.T.astype(BF16)
        v_ref[...] = vv.astype(BF16)
        vt_ref[...] = vv.T.astype(BF16)

    def src(col):
        return pl.BlockSpec((tm, AW), lambda t: (jnp.maximum(t - 1, 0), col))

    gspec = pl.BlockSpec((1, AW), lambda t: (0, 0))
    rows = pl.BlockSpec((tm, AW), lambda t: (t, 0))
    cols = pl.BlockSpec((AW, tm), lambda t: (0, t))
    return pl.pallas_call(
        body, name=name, grid=(nb + 1,),
        in_specs=[src(0), src(1), src(2), gspec, gspec],
        out_specs=[pl.BlockSpec((tm, AW), lambda t: (jnp.maximum(t - 1, 0), 0)),
                   pl.BlockSpec((AW, tm), lambda t: (0, jnp.maximum(t - 1, 0))),
                   rows, cols, rows, cols],
        out_shape=[jax.ShapeDtypeStruct((s, AW), BF16), jax.ShapeDtypeStruct((AW, s), BF16),
                   jax.ShapeDtypeStruct((s + PADR, AW), BF16), jax.ShapeDtypeStruct((AW, s + PADR), BF16),
                   jax.ShapeDtypeStruct((s + PADR, AW), BF16), jax.ShapeDtypeStruct((AW, s + PADR), BF16)],
        compiler_params=_cp("arbitrary"),
    )(p, p, p, qg, kg)


NBAND = KB // CH
HIGHEST = lax.Precision.HIGHEST
NT_DIMS = (((1,), (1,)), ((), ()))


def _onehot_table(a):
    m = lax.broadcasted_iota(jnp.int32, (128, NIDX), 0)
    idx = lax.broadcasted_iota(jnp.int32, (128, NIDX), 1)
    rel = jnp.clip(KB - 1 - (CH * a + m), -128, 128) + 128
    return jnp.where(rel == idx, 1.0, 0.0).astype(F32)


def _onehot_diagonal():
    r = lax.broadcasted_iota(jnp.int32, (CH * CH, 128), 0)
    m = lax.broadcasted_iota(jnp.int32, (CH * CH, 128), 1)
    return jnp.where((r % CH) - (r // CH) + (CH - 1) == m, 1.0, 0.0).astype(F32)


def _bias_expand(name, rb):
    def body(rb_ref, o_ref):
        along = [lax.dot_general(rb_ref[...], _onehot_table(a), NT_DIMS, preferred_element_type=F32,
                                 precision=HIGHEST) for a in range(NBAND)]
        o_ref[...] = lax.dot_general(jnp.concatenate(along, axis=0), _onehot_diagonal(), NT_DIMS,
                                     preferred_element_type=F32, precision=HIGHEST)

    return pl.pallas_call(
        body, name=name, grid=(DEPTH,),
        in_specs=[pl.BlockSpec((None, 8, NIDX), lambda l: (l, 0, 0))],
        out_specs=pl.BlockSpec((None, NBAND * 8, CH * CH), lambda l: (l, 0, 0)),
        out_shape=jax.ShapeDtypeStruct((DEPTH, NBAND * 8, CH * CH), F32),
        compiler_params=_cp("parallel"),
    )(rb)


def _bias_reduce(name, db):
    def body(db_ref, o_ref):
        along = jnp.dot(db_ref[...], _onehot_diagonal(), preferred_element_type=F32, precision=HIGHEST)
        acc = jnp.zeros((8, NIDX), F32)
        for a in range(NBAND):
            acc = acc + jnp.dot(along[8 * a:8 * a + 8, :], _onehot_table(a), preferred_element_type=F32,
                                precision=HIGHEST)
        o_ref[...] = acc

    return pl.pallas_call(
        body, name=name, grid=(DEPTH,),
        in_specs=[pl.BlockSpec((None, NBAND * 8, CH * CH), lambda l: (l, 0, 0))],
        out_specs=pl.BlockSpec((None, 8, NIDX), lambda l: (l, 0, 0)),
        out_shape=jax.ShapeDtypeStruct((DEPTH, 8, NIDX), F32),
        compiler_params=_cp("parallel"),
    )(db)


def _bias_layout(flat):
    b = flat.reshape(DEPTH, NBAND, 8, CH, CH).transpose(0, 2, 1, 4, 3).reshape(DEPTH, 4, 2, KB, CH)
    pair = b.transpose(0, 1, 3, 2, 4).reshape(DEPTH, 4, KB, 128)
    first = jnp.pad(pair, ((0, 0), (0, 0), (0, CH), (0, 0)), constant_values=NEG_INF)
    second = jnp.pad(pair, ((0, 0), (0, 0), (CH, 0), (0, 0)), constant_values=NEG_INF)
    return jnp.concatenate([first, second], axis=3)


def _bias_unlayout(dbt):
    b = dbt.reshape(DEPTH, 4, NBAND, CH, 2, CH)
    return b.transpose(0, 2, 1, 4, 5, 3).reshape(DEPTH, NBAND * 8, CH * CH)


UNIT = 2 * CH
BAND2 = KB + CH


def _pair_weights(xt):
    x = xt.astype(F32)
    row = lax.broadcasted_iota(jnp.int32, (128, UNIT), 0)
    low = lax.broadcasted_iota(jnp.int32, (128, UNIT), 1) < HD
    swapped = pltpu.roll(x, HD, 1)
    same = (row < HD) == low
    first = jnp.where(same, jnp.where(low, x, swapped), 0.0)
    second = jnp.where(same, jnp.where(low, swapped, x), 0.0)
    return jnp.concatenate([first, second], axis=1).astype(BF16)


def _pair_rows(x):
    low = lax.broadcasted_iota(jnp.int32, (CH, 128), 1) < HD
    zero = jnp.zeros((CH, 128), x.dtype)
    parts = []
    for c in range(2):
        xc = x[c * CH:(c + 1) * CH, :]
        parts += [jnp.where(low, xc, zero), jnp.where(low, zero, xc)]
    return jnp.concatenate(parts, axis=0)


def _unpair(raw):
    b0, b1 = raw[:, 0:128], raw[:, 128:256]
    row = lax.broadcasted_iota(jnp.int32, (128, 128), 0)
    low = lax.broadcasted_iota(jnp.int32, (128, 128), 1) < HD
    top = jnp.where(low, b0, pltpu.roll(b1, HD, 1))
    bottom = jnp.where(low, pltpu.roll(b0, HD, 1), b1)
    return jnp.where(row < HD, top, bottom).T


def _scores_t(kb, qw, bias2, row0, padded):
    s = jnp.dot(kb, qw, preferred_element_type=F32) + bias2
    if padded:
        s = jnp.where(row0 + lax.broadcasted_iota(jnp.int32, (BAND2, 256), 0) >= PADR, s, NEG_INF)
    return s


def _unit_loops(s, unit):
    lax.fori_loop(0, PADR // UNIT, lambda u, c: unit(u, True, c), 0, unroll=2)
    lax.fori_loop(PADR // UNIT, s // UNIT, lambda u, c: unit(u, False, c), 0, unroll=4)


def _attn_fwd(name, kp, qt, vt, bias2):
    s = qt.shape[1]
    nu = s // UNIT

    def body(k_ref, qt_ref, vt_ref, b_ref, o_ref, lse_ref):
        def unit(u, padded, carry):
            r0 = pl.multiple_of(u * UNIT, UNIT)
            sc = _scores_t(k_ref[pl.ds(r0, BAND2), :], _pair_weights(qt_ref[:, pl.ds(r0, UNIT)]), b_ref[...],
                           r0, padded)
            top = jnp.max(sc, axis=0, keepdims=True)
            e = jnp.exp(sc - top)
            total = jnp.sum(e, axis=0, keepdims=True)
            raw = jnp.dot(vt_ref[:, pl.ds(r0, BAND2)], e.astype(BF16), preferred_element_type=F32) * (1.0 / total)
            o_ref[pl.ds(r0, UNIT), :] = _unpair(raw).astype(BF16)
            lse_ref[u] = jnp.broadcast_to(top + jnp.log(total), (8, 256))
            return carry

        _unit_loops(s, unit)

    return pl.pallas_call(
        body, name=name, grid=(AW // 128,),
        in_specs=[pl.BlockSpec((s + PADR, 128), lambda h: (0, h)),
                  pl.BlockSpec((128, s), lambda h: (h, 0)),
                  pl.BlockSpec((128, s + PADR), lambda h: (h, 0)),
                  pl.BlockSpec((None, BAND2, 256), lambda h: (h, 0, 0))],
        out_specs=[pl.BlockSpec((s, 128), lambda h: (0, h)),
                   pl.BlockSpec((None, nu, 8, 256), lambda h: (h, 0, 0, 0))],
        out_shape=[jax.ShapeDtypeStruct((s, AW), BF16), jax.ShapeDtypeStruct((4, nu, 8, 256), F32)],
        compiler_params=_cp("parallel"),
    )(kp, qt, vt, bias2)


def _attn_bwd(name, q, qt, kp, kt, vp, bias2, do, dot, lse, dl):
    s = q.shape[0]
    nu = s // UNIT

    def body(q_ref, qt_ref, k_ref, kt_ref, v_ref, b_ref, do_ref, dot_ref, lse_ref, dl_ref,
             dq_ref, dk_ref, dv_ref, db_ref):
        dk_ref[...] = jnp.zeros_like(dk_ref)
        dv_ref[...] = jnp.zeros_like(dv_ref)
        db_ref[...] = jnp.zeros_like(db_ref)

        def unit(u, padded, carry):
            r0 = pl.multiple_of(u * UNIT, UNIT)
            rows, band = pl.ds(r0, UNIT), pl.ds(r0, BAND2)
            sc = _scores_t(k_ref[band, :], _pair_weights(qt_ref[:, rows]), b_ref[...], r0, padded)
            pt = jnp.exp(sc - lse_ref[u][0:1, :])
            dpt = jnp.dot(v_ref[band, :], _pair_weights(dot_ref[:, rows]), preferred_element_type=F32)
            ds = pt * (dpt - dl_ref[u][0:1, :])
            db_ref[...] += ds[0:KB, 0:128] + ds[CH:BAND2, 128:256]
            dsb = ds.astype(BF16)
            dq_ref[rows, :] = _unpair(jnp.dot(kt_ref[:, band], dsb, preferred_element_type=F32))
            dk_ref[band, :] += jnp.dot(dsb, _pair_rows(q_ref[rows, :]), preferred_element_type=F32)
            dv_ref[band, :] += jnp.dot(pt.astype(BF16), _pair_rows(do_ref[rows, :]), preferred_element_type=F32)
            return carry

        _unit_loops(s, unit)

    row_q = pl.BlockSpec((s, 128), lambda h: (0, h))
    col_q = pl.BlockSpec((128, s), lambda h: (h, 0))
    row_k = pl.BlockSpec((s + PADR, 128), lambda h: (0, h))
    col_k = pl.BlockSpec((128, s + PADR), lambda h: (h, 0))
    stat = pl.BlockSpec((None, nu, 8, 256), lambda h: (h, 0, 0, 0))
    return pl.pallas_call(
        body, name=name, grid=(AW // 128,),
        in_specs=[row_q, col_q, row_k, col_k, row_k,
                  pl.BlockSpec((None, BAND2, 256), lambda h: (h, 0, 0)), row_q, col_q, stat, stat],
        out_specs=[row_q, row_k, row_k, pl.BlockSpec((None, KB, 128), lambda h: (h, 0, 0))],
        out_shape=[jax.ShapeDtypeStruct((s, AW), F32),
                   jax.ShapeDtypeStruct((s + PADR, AW), F32),
                   jax.ShapeDtypeStruct((s + PADR, AW), F32),
                   jax.ShapeDtypeStruct((4, KB, 128), F32)],
        compiler_params=_cp("parallel"),
    )(q, qt, kp, kt, vp, bias2, do, dot, lse, dl)


def _rowsum_layout(dl, nu):
    d = dl[:, :8].reshape(nu, 2, CH, 4, 2)
    d = d.transpose(3, 0, 1, 4, 2).reshape(4, nu, 1, 256)
    return jnp.broadcast_to(d, (4, nu, 8, 256))


def _rows_before(cur, prev, k):
    row = lax.broadcasted_iota(jnp.int32, cur.shape, 0)
    return jnp.where(row >= k, pltpu.roll(cur, k, 0), pltpu.roll(prev, k, 0))


def _rows_after(cur, nxt, k):
    n = cur.shape[0]
    row = lax.broadcasted_iota(jnp.int32, cur.shape, 0)
    return jnp.where(row < n - k, pltpu.roll(cur, n - k, 0), pltpu.roll(nxt, n - k, 0))


def _pool_window_lanes():
    lg = lax.broadcasted_iota(jnp.int32, (1, PWD), 1) // 64
    return lg, jnp.where(lg == 0, 2.0, jnp.where(lg == 1, 4.0, jnp.where(lg == 2, 8.0, 16.0))).astype(F32)


def _pool_mean_minus_token(u, up, row0):
    lg, wv = _pool_window_lanes()
    sums = []
    c, p = u, up
    for k in (1, 2, 4, 8):
        c2 = c + _rows_before(c, p, k)
        p = p + pltpu.roll(p, k, 0)
        c = c2
        sums.append(c)
    win = jnp.where(lg == 0, sums[0], jnp.where(lg == 1, sums[1], jnp.where(lg == 2, sums[2], sums[3])))
    pos1 = (row0 + lax.broadcasted_iota(jnp.int32, u.shape, 0) + 1).astype(F32)
    cnt = jnp.minimum(pos1, wv)
    return win / cnt - u, cnt


def _conv_taps(z, zp, w0, w1, w2):
    z1 = _rows_before(z, zp, 1)
    z2 = _rows_before(z, zp, 2)
    return (w0 * z2 + w1 * z1) + w2 * z, z1, z2


CP_TM = 512


def _convpool_fwd(name, p, o, cw, pwbd, ps):
    s = p.shape[0]
    tm = CP_TM
    nb = s // tm

    def body(gb_ref, gc_ref, hin_ref, u_ref, gcp_ref, hinp_ref, up_ref, o_ref, cw_ref, pw_ref, ps_ref, mix_ref):
        i = pl.program_id(0)
        has_prev = i > 0
        z = gc_ref[...] * hin_ref[...]
        zp = jnp.where(has_prev, gcp_ref[...] * hinp_ref[...], 0.0)
        y3, _, _ = _conv_taps(z, zp, cw_ref[0:1, :], cw_ref[1:2, :], cw_ref[2:3, :])
        m, _ = _pool_mean_minus_token(u_ref[...], jnp.where(has_prev, up_ref[...], 0.0), i * tm)
        yp = jnp.dot(m.astype(BF16), pw_ref[...].astype(BF16), preferred_element_type=F32) * ps_ref[...]
        mix_ref[:, 0:AW] = o_ref[...]
        mix_ref[:, AW:AW + CW] = (gb_ref[...] * y3).astype(BF16)
        mix_ref[:, AW + CW:D] = yp.astype(BF16)

    def cur(col):
        return pl.BlockSpec((tm, CW), lambda i: (i, col))

    def prev(col):
        return pl.BlockSpec((tm, CW), lambda i: (jnp.maximum(i - 1, 0), col))

    def whole(a):
        return pl.BlockSpec(a.shape, lambda i: (0,) * a.ndim)

    return pl.pallas_call(
        body, name=name, grid=(nb,),
        in_specs=[cur(6), cur(7), cur(8), cur(9), prev(7), prev(8), prev(9),
                  pl.BlockSpec((tm, AW), lambda i: (i, 0)), whole(cw), whole(pwbd), whole(ps)],
        out_specs=pl.BlockSpec((tm, D), lambda i: (i, 0)),
        out_shape=jax.ShapeDtypeStruct((s, D), BF16),
        compiler_params=_cp("parallel"),
    )(p, p, p, p, p, p, p, o, cw, pwbd, ps)


def _convpool_bwd(name, p, dmix, cw, pwbd, ps):
    s = p.shape[0]
    tm = CP_TM
    nb = s // tm

    def body(gb_ref, gc_ref, hin_ref, u_ref, gcp_ref, hinp_ref, up_ref, gbn_ref, dyc_ref, dyp_ref, dycn_ref, dypn_ref,
             cw_ref, pw_ref, ps_ref, dcp_ref, dw0_ref, dw1_ref, dw2_ref, dps_ref, dpw_ref):
        i = pl.program_id(0)
        has_prev = i > 0
        has_next = i < nb - 1
        w0, w1, w2 = cw_ref[0:1, :], cw_ref[1:2, :], cw_ref[2:3, :]
        gb, gc, hin = gb_ref[...], gc_ref[...], hin_ref[...]
        dyc = dyc_ref[...]
        z = gc * hin
        zp = jnp.where(has_prev, gcp_ref[...] * hinp_ref[...], 0.0)
        y3, z1, z2 = _conv_taps(z, zp, w0, w1, w2)
        dy3 = dyc * gb
        dy3n = jnp.where(has_next, dycn_ref[...] * gbn_ref[...], 0.0)
        dz = w2 * dy3 + w1 * _rows_after(dy3, dy3n, 1) + w0 * _rows_after(dy3, dy3n, 2)
        pw = pw_ref[...].astype(BF16)
        psv = ps_ref[...]
        m, cnt = _pool_mean_minus_token(u_ref[...], jnp.where(has_prev, up_ref[...], 0.0), i * tm)
        mb = m.astype(BF16)
        dyp = dyp_ref[...]
        dmp = (dyp * psv).astype(BF16)
        dmpn = jnp.where(has_next, dypn_ref[...] * psv, 0.0).astype(BF16)
        nt = (((1,), (1,)), ((), ()))
        dm = lax.dot_general(dmp, pw, nt, preferred_element_type=F32)
        dmn = lax.dot_general(dmpn, pw, nt, preferred_element_type=F32)
        lg, wv = _pool_window_lanes()
        cc, cn = dm / cnt, dmn / wv
        sums = []
        for k in (1, 2, 4, 8):
            c2 = cc + _rows_after(cc, cn, k)
            cn = cn + pltpu.roll(cn, tm - k, 0)
            cc = c2
            sums.append(cc)
        du = jnp.where(lg == 0, sums[0], jnp.where(lg == 1, sums[1], jnp.where(lg == 2, sums[2], sums[3]))) - dm
        dcp_ref[:, 0:CW] = (dyc * y3).astype(BF16)
        dcp_ref[:, CW:2 * CW] = (dz * hin).astype(BF16)
        dcp_ref[:, 2 * CW:3 * CW] = (dz * gc).astype(BF16)
        dcp_ref[:, 3 * CW:4 * CW] = du.astype(BF16)
        parts = (jnp.sum(dy3 * z2, axis=0, keepdims=True),
                 jnp.sum(dy3 * z1, axis=0, keepdims=True),
                 jnp.sum(dy3 * z, axis=0, keepdims=True),
                 jnp.sum(dyp * jnp.dot(mb, pw, preferred_element_type=F32), axis=0, keepdims=True),
                 lax.dot_general(mb, dmp, (((0,), (0,)), ((), ())), preferred_element_type=F32))
        accs = (dw0_ref, dw1_ref, dw2_ref, dps_ref, dpw_ref)

        @pl.when(i == 0)
        def _():
            for a, v in zip(accs, parts):
                a[...] = v

        @pl.when(i > 0)
        def _():
            for a, v in zip(accs, parts):
                a[...] += v

    def cur(col):
        return pl.BlockSpec((tm, CW), lambda i: (i, col))

    def prev(col):
        return pl.BlockSpec((tm, CW), lambda i: (jnp.maximum(i - 1, 0), col))

    def nxt(col):
        return pl.BlockSpec((tm, CW), lambda i: (jnp.minimum(i + 1, nb - 1), col))

    def whole(shape):
        return pl.BlockSpec(shape, lambda i: (0,) * len(shape))

    row = jax.ShapeDtypeStruct((1, CW), F32)
    return pl.pallas_call(
        body, name=name, grid=(nb,),
        in_specs=[cur(6), cur(7), cur(8), cur(9), prev(7), prev(8), prev(9), nxt(6),
                  cur(0), cur(1), nxt(0), nxt(1), whole(cw.shape), whole(pwbd.shape), whole(ps.shape)],
        out_specs=[pl.BlockSpec((tm, D), lambda i: (i, 0)), whole((1, CW)), whole((1, CW)), whole((1, CW)),
                   whole((1, PWD)), whole((PWD, PWD))],
        out_shape=[jax.ShapeDtypeStruct((s, D), BF16), row, row, row, row,
                   jax.ShapeDtypeStruct((PWD, PWD), F32)],
        compiler_params=_cp("arbitrary"),
    )(p, p, p, p, p, p, p, p, dmix, dmix, dmix, dmix, cw, pwbd, ps)


def _qkv_bwd(name, p, dq, dkp, dvp, dcp, qg, kg):
    s = p.shape[0]
    tm = 256
    off = PADR // tm

    def body(pq_ref, pk_ref, dq_ref, dk_ref, dv_ref, dcp_ref, qg_ref, kg_ref, dp_ref, dqg_ref, dkg_ref):
        i = pl.program_id(0)
        hm = _head_mean_matrix()

        def nrm_bwd(x, g, dy):
            r = lax.rsqrt(_head_mean(x * x, hm) + EPS)
            xn = x * r
            dxn = dy * g
            dx = r * (dxn - xn * _head_mean(dxn * xn, hm))
            dg = jnp.sum(dy * xn, axis=0, keepdims=True)
            dg = (dg[:, 0:128] + dg[:, 128:256]) + (dg[:, 256:384] + dg[:, 384:512])
            return dx, dg + pltpu.roll(dg, HD, 1)

        dxq, dgq = nrm_bwd(pq_ref[...], qg_ref[...], dq_ref[...] * 0.125)
        dxk, dgk = nrm_bwd(pk_ref[...], kg_ref[...], dk_ref[...])
        dp_ref[:, 0:AW] = dxq.astype(BF16)
        dp_ref[:, AW:2 * AW] = dxk.astype(BF16)
        dp_ref[:, 2 * AW:3 * AW] = dv_ref[...].astype(BF16)
        dp_ref[:, 3 * AW:DIN] = dcp_ref[...]

        @pl.when(i == 0)
        def _():
            dqg_ref[...] = dgq
            dkg_ref[...] = dgk

        @pl.when(i > 0)
        def _():
            dqg_ref[...] += dgq
            dkg_ref[...] += dgk

    gspec = pl.BlockSpec((1, AW), lambda i: (0, 0))
    gout = pl.BlockSpec((1, 128), lambda i: (0, 0))
    return pl.pallas_call(
        body, name=name, grid=(s // tm,),
        in_specs=[pl.BlockSpec((tm, AW), lambda i: (i, 0)), pl.BlockSpec((tm, AW), lambda i: (i, 1)),
                  pl.BlockSpec((tm, AW), lambda i: (i, 0)),
                  pl.BlockSpec((tm, AW), lambda i: (i + off, 0)),
                  pl.BlockSpec((tm, AW), lambda i: (i + off, 0)),
                  pl.BlockSpec((tm, D), lambda i: (i, 0)), gspec, gspec],
        out_specs=[pl.BlockSpec((tm, DIN), lambda i: (i, 0)), gout, gout],
        out_shape=[jax.ShapeDtypeStruct((s, DIN), BF16), jax.ShapeDtypeStruct((1, 128), F32),
                   jax.ShapeDtypeStruct((1, 128), F32)],
        compiler_params=_cp("arbitrary"),
    )(p, p, dq, dkp, dvp, dcp, qg, kg)


def _loss_grad(name, y, t):
    s = y.shape[0]
    tm = 512

    def body(y_ref, t_ref, dy_ref, dyb_ref, l_ref):
        i = pl.program_id(0)
        e = y_ref[...] - t_ref[...]
        dy = e * (1.0 / D)
        dy_ref[...] = dy
        dyb_ref[...] = dy.astype(BF16)
        part = 0.5 * jnp.sum(jnp.mean(e * e, axis=-1, keepdims=True), axis=0, keepdims=True)

        @pl.when(i == 0)
        def _():
            l_ref[...] = part

        @pl.when(i > 0)
        def _():
            l_ref[...] += part

    blk = pl.BlockSpec((tm, D), lambda i: (i, 0))
    return pl.pallas_call(
        body, name=name, grid=(s // tm,),
        in_specs=[blk, blk],
        out_specs=[blk, blk, pl.BlockSpec((1, 1), lambda i: (0, 0))],
        out_shape=[jax.ShapeDtypeStruct((s, D), F32), jax.ShapeDtypeStruct((s, D), BF16),
                   jax.ShapeDtypeStruct((1, 1), F32)],
        compiler_params=_cp("arbitrary"),
    )(y, t)


def _mm_nt_relu(name, dxb, w, l, a):
    s = dxb.shape[0]
    tm = MM_ROWS

    def body(d_ref, w_ref, a_ref, o_ref):
        df = lax.dot_general(d_ref[...], w_ref[...], NT_DIMS, preferred_element_type=F32)
        o_ref[...] = (df * (2.0 * jnp.maximum(a_ref[...].astype(F32), 0.0))).astype(BF16)

    return pl.pallas_call(
        body, name=name, grid=(s // tm,),
        in_specs=[pl.BlockSpec((tm, D), lambda i: (i, 0)),
                  pl.BlockSpec((None, DFF, D), lambda i: (l, 0, 0)),
                  pl.BlockSpec((tm, DFF), lambda i: (i, 0))],
        out_specs=pl.BlockSpec((tm, DFF), lambda i: (i, 0)),
        out_shape=jax.ShapeDtypeStruct((s, DFF), BF16),
        compiler_params=_cp("parallel"),
    )(dxb, w, a)


def _proj_out_bwd(name, dxb, w, l, mix):
    s = dxb.shape[0]
    tm = 512

    def body(d_ref, w_ref, o_ref, do_ref, dot_ref, dcp_ref, dl_ref):
        d = d_ref[...]
        wa, wc = w_ref[0:AW, :], w_ref[AW:D, :]
        do = lax.dot_general(d, wa, NT_DIMS, preferred_element_type=F32)
        do_ref[...] = do.astype(BF16)
        dot_ref[...] = lax.dot_general(wa, d, NT_DIMS, preferred_element_type=F32).astype(BF16)
        dcp_ref[...] = lax.dot_general(d, wc, NT_DIMS, preferred_element_type=F32)
        head = lax.broadcasted_iota(jnp.int32, (AW, 128), 0) // HD
        pick = jnp.where(head == lax.broadcasted_iota(jnp.int32, (AW, 128), 1), 1.0, 0.0).astype(BF16)
        dl_ref[...] = _two_pass_dot(do * o_ref[...].astype(F32), pick)

    return pl.pallas_call(
        body, name=name, grid=(s // tm,),
        in_specs=[pl.BlockSpec((tm, D), lambda i: (i, 0)),
                  pl.BlockSpec((None, D, D), lambda i: (l, 0, 0)),
                  pl.BlockSpec((tm, AW), lambda i: (i, 0))],
        out_specs=[pl.BlockSpec((tm, AW), lambda i: (i, 0)), pl.BlockSpec((AW, tm), lambda i: (0, i)),
                   pl.BlockSpec((tm, D - AW), lambda i: (i, 0)), pl.BlockSpec((tm, 128), lambda i: (i, 0))],
        out_shape=[jax.ShapeDtypeStruct((s, AW), BF16), jax.ShapeDtypeStruct((AW, s), BF16),
                   jax.ShapeDtypeStruct((s, D - AW), F32), jax.ShapeDtypeStruct((s, 128), F32)],
        compiler_params=_cp("parallel"),
    )(dxb, w, mix)


def _mm_nt_normbwd(name, gy, w, l, x, g, dres, dep):
    s, k = gy.shape
    tm = MM_ROWS

    def body(gy_ref, w_ref, x_ref, g_ref, dr_ref, dep_ref, dx_ref, dxb_ref, dg_ref):
        del dep_ref
        i = pl.program_id(0)
        dh = lax.dot_general(gy_ref[...], w_ref[...], NT_DIMS, preferred_element_type=F32)
        xv = x_ref[...]
        r = _inv_rms(xv)
        xn = xv * r
        dxn = dh * g_ref[...]
        dx = r * (dxn - xn * jnp.mean(dxn * xn, axis=-1, keepdims=True)) + dr_ref[...]
        dx_ref[...] = dx
        dxb_ref[...] = dx.astype(BF16)
        part = jnp.sum(dh * xn, axis=0, keepdims=True)

        @pl.when(i == 0)
        def _():
            dg_ref[...] = part

        @pl.when(i > 0)
        def _():
            dg_ref[...] += part

    blk = pl.BlockSpec((tm, D), lambda i: (i, 0))
    vec = pl.BlockSpec((1, D), lambda i: (0, 0))
    return pl.pallas_call(
        body, name=name, grid=(s // tm,),
        in_specs=[pl.BlockSpec((tm, k), lambda i: (i, 0)),
                  pl.BlockSpec((None, D, k), lambda i: (l, 0, 0)), blk, vec, blk, ANY],
        out_specs=[blk, blk, vec],
        out_shape=[jax.ShapeDtypeStruct((s, D), F32), jax.ShapeDtypeStruct((s, D), BF16),
                   jax.ShapeDtypeStruct((1, D), F32)],
        compiler_params=_cp("arbitrary"),
    )(gy, w, x, g, dres, dep)


def _mm_tn(name, a, b, tma, tnb, relu2=False):
    s, m = a.shape
    n = b.shape[1]

    def body(a_ref, b_ref, o_ref):
        av = _relu2(a_ref[...]) if relu2 else a_ref[...]
        o_ref[...] = lax.dot_general(av, b_ref[...], (((0,), (0,)), ((), ())),
                                     preferred_element_type=F32).astype(BF16)

    return pl.pallas_call(
        body, name=name, grid=(m // tma, n // tnb),
        in_specs=[pl.BlockSpec((s, tma), lambda i, j: (0, i)),
                  pl.BlockSpec((s, tnb), lambda i, j: (0, j))],
        out_specs=pl.BlockSpec((tma, tnb), lambda i, j: (i, j)),
        out_shape=jax.ShapeDtypeStruct((m, n), BF16),
        compiler_params=_cp("parallel", "parallel"),
    )(a, b)


def _adamw_math(gv, wv, mv, vv):
    mn = ADAM_B1 * mv + (1.0 - ADAM_B1) * gv
    vn = ADAM_B2 * vv + (1.0 - ADAM_B2) * jnp.square(gv)
    m_hat = mn / (1.0 - ADAM_B1 ** ADAM_STEP)
    v_hat = vn / (1.0 - ADAM_B2 ** ADAM_STEP)
    return gv, -ADAM_LR * (m_hat / (jnp.sqrt(v_hat) + ADAM_EPS) + ADAM_WD * wv), mn, vn


def _adamw(name, g, w, m, v):
    r, c = g.shape
    tm = 256 if r % 256 == 0 else r

    def body(g_ref, w_ref, m_ref, v_ref, go_ref, d_ref, mo_ref, vo_ref):
        go_ref[...], d_ref[...], mo_ref[...], vo_ref[...] = _adamw_math(g_ref[...], w_ref[...], m_ref[...], v_ref[...])

    blk = pl.BlockSpec((tm, c), lambda i: (i, 0))
    return pl.pallas_call(
        body, name=name, grid=(r // tm,),
        in_specs=[blk] * 4, out_specs=[blk] * 4,
        out_shape=[jax.ShapeDtypeStruct((r, c), F32)] * 4,
        compiler_params=_cp("parallel"),
    )(g, w, m, v)


def _place():
    x, y, c = lax.axis_index("x"), lax.axis_index("y"), lax.axis_index("c")
    chips = [(1 - x, y), (x, 1 - y), (1 - x, 1 - y)]
    return x, y, c, chips


BLOCK_AXIS = (2, 1, 2, 1)
LARGE_DIMS = ((D, DIN), (D, D), (D, DFF), (DFF, D))


def _full_shape(t, layers, dtype):
    r, c = LARGE_DIMS[t]
    return jax.ShapeDtypeStruct((layers, r, c), dtype)


def _cast_into_full(name, t, shard, b1, dep):
    _, r, c = shard.shape
    tm = min(256, r)
    if BLOCK_AXIS[t] == 1:
        out_spec = pl.BlockSpec((None, tm, c), lambda l, i, br: (l, br[0] * (r // tm) + i, 0))
    else:
        out_spec = pl.BlockSpec((None, tm, c), lambda l, i, br: (l, i, br[0]))

    def body(b_ref, x_ref, dep_ref, o_ref):
        del b_ref, dep_ref
        o_ref[...] = x_ref[...].astype(BF16)

    return pl.pallas_call(
        body, name=name,
        grid_spec=pltpu.PrefetchScalarGridSpec(
            num_scalar_prefetch=1, grid=(DEPTH, r // tm),
            in_specs=[pl.BlockSpec((None, tm, c), lambda l, i, br: (l, i, 0)), ANY],
            out_specs=out_spec),
        out_shape=_full_shape(t, DEPTH, BF16),
        compiler_params=_cp("parallel", "parallel"),
    )(b1, shard, dep)


HBM = pl.BlockSpec(memory_space=pltpu.HBM)
SEM = pl.BlockSpec(memory_space=pltpu.SEMAPHORE)
DATAFLOW = pltpu.SideEffectType.DATAFLOW_SIDE_EFFECTING


def _half(ref, l, t, b, c):
    r, cols = LARGE_DIMS[t]
    if BLOCK_AXIS[t] == 1:
        n = r // 8
        return ref.at[l, pl.ds(pl.multiple_of(b * (2 * n) + c * n, 16), n), :]
    n, w = r // 2, cols // 4
    return ref.at[l, pl.ds(pl.multiple_of(c * n, 16), n), pl.ds(pl.multiple_of(b * w, 128), w)]


def _gather_start(name, layers, fulls):
    def body(*refs):
        f_refs, sems = refs[4:8], refs[8:8 + 2 * len(layers)]
        x, y, c, chips = _place()
        for i, l in enumerate(layers):
            for t in range(4):
                own = _half(f_refs[t], l, t, 2 * x + y, c)
                for j, (cx, cy) in enumerate(chips):
                    pltpu.make_async_remote_copy(src_ref=own, dst_ref=own, send_sem=sems[2 * i].at[3 * t + j],
                                                 recv_sem=sems[2 * i + 1].at[3 * t + j], device_id=(cx, cy, c),
                                                 device_id_type=MESH).start()

    outs = pl.pallas_call(
        body, name=name,
        in_specs=[HBM] * 4, out_specs=[HBM] * 4 + [SEM] * (2 * len(layers)),
        out_shape=[pltpu.HBM(s.shape, s.dtype) for s in (_full_shape(t, DEPTH, BF16) for t in range(4))]
        + [pltpu.SemaphoreType.DMA((12,))] * (2 * len(layers)),
        input_output_aliases={t: t for t in range(4)},
        compiler_params=pltpu.CompilerParams(has_side_effects=DATAFLOW),
    )(*[pltpu.with_memory_space_constraint(f, pltpu.HBM) for f in fulls])
    return outs[0:4], {l: (outs[4 + 2 * i], outs[5 + 2 * i]) for i, l in enumerate(layers)}


def _gather_wait(name, l, ts, fulls, sems, after):
    def body(*refs):
        send_sems, recv_sems, f_refs = refs[4], refs[5], refs[7:11]
        x, y, c, chips = _place()
        for t in ts:
            own = _half(f_refs[t], l, t, 2 * x + y, c)
            for j, (cx, cy) in enumerate(chips):
                landed = _half(f_refs[t], l, t, 2 * cx + cy, c)
                pltpu.make_async_remote_copy(src_ref=own, dst_ref=landed, send_sem=send_sems.at[3 * t + j],
                                             recv_sem=recv_sems.at[3 * t + j], device_id=(cx, cy, c),
                                             device_id_type=MESH).wait()

    return pl.pallas_call(
        body, name=name,
        in_specs=[HBM] * 4 + [SEM, SEM, ANY], out_specs=[HBM] * 4,
        out_shape=[pltpu.HBM(s.shape, s.dtype) for s in (_full_shape(t, DEPTH, BF16) for t in range(4))],
        input_output_aliases={t: t for t in range(4)},
        compiler_params=pltpu.CompilerParams(has_side_effects=DATAFLOW),
    )(*fulls, sems[0], sems[1], after)


def _pass_on(name, l, ts, fulls):
    def body(*refs):
        f_refs, send_sems, recv_sems = refs[4:8], refs[8], refs[9]
        x, y, c, chips = _place()

        def copy(t, j, half):
            cx, cy = chips[j]
            part = _half(f_refs[t], l, t, 2 * cx + cy, half)
            return pltpu.make_async_remote_copy(src_ref=part, dst_ref=part, send_sem=send_sems.at[3 * t + j],
                                                recv_sem=recv_sems.at[3 * t + j], device_id=(x, y, 1 - c),
                                                device_id_type=MESH)

        for t in ts:
            for j in range(3):
                copy(t, j, c).start()
        for t in ts:
            for j in range(3):
                copy(t, j, 1 - c).wait_recv()
                copy(t, j, c).wait_send()

    return pl.pallas_call(
        body, name=name,
        in_specs=[ANY] * 4, out_specs=[ANY] * 4,
        out_shape=[_full_shape(t, DEPTH, BF16) for t in range(4)],
        input_output_aliases={t: t for t in range(4)},
        scratch_shapes=[pltpu.SemaphoreType.DMA((12,)), pltpu.SemaphoreType.DMA((12,))],
    )(*fulls)


def _block2d(ref, t, b):
    r, cols = LARGE_DIMS[t]
    if BLOCK_AXIS[t] == 1:
        return ref.at[pl.ds(pl.multiple_of(b * (r // 4), 16), r // 4), :]
    return ref.at[:, pl.ds(pl.multiple_of(b * (cols // 4), 128), cols // 4)]


def _block_dims(t):
    r, cols = LARGE_DIMS[t]
    return (r // 4, cols) if BLOCK_AXIS[t] == 1 else (r, cols // 4)


def _reduce_copies(ts, g_refs, r_refs, send_sems, recv_sems):
    _, _, c, chips = _place()
    return [pltpu.make_async_remote_copy(src_ref=_block2d(g_refs[i], t, 2 * cx + cy), dst_ref=r_refs[i].at[j],
                                         send_sem=send_sems.at[3 * i + j], recv_sem=recv_sems.at[3 * i + j],
                                         device_id=(cx, cy, c), device_id_type=MESH)
            for i, t in enumerate(ts) for j, (cx, cy) in enumerate(chips)]


def _reduce_start(name, ts, grads):
    n = len(ts)

    def body(*refs):
        for cp in _reduce_copies(ts, refs[n:2 * n], refs[2 * n:3 * n], refs[3 * n], refs[3 * n + 1]):
            cp.start()
        refs[3 * n + 2][...] = jnp.zeros((8, 128), F32)

    outs = pl.pallas_call(
        body, name=name,
        in_specs=[HBM] * n,
        out_specs=[HBM] * (2 * n) + [SEM, SEM, pl.BlockSpec(memory_space=pltpu.VMEM)],
        out_shape=[pltpu.HBM(g.shape, BF16) for g in grads]
        + [pltpu.HBM((3,) + _block_dims(t), BF16) for t in ts]
        + [pltpu.SemaphoreType.DMA((3 * n,)), pltpu.SemaphoreType.DMA((3 * n,)), jax.ShapeDtypeStruct((8, 128), F32)],
        input_output_aliases={i: i for i in range(n)},
        compiler_params=pltpu.CompilerParams(has_side_effects=DATAFLOW),
    )(*[pltpu.with_memory_space_constraint(g, pltpu.HBM) for g in grads])
    return outs[0:n], outs[n:2 * n], (outs[2 * n], outs[2 * n + 1]), outs[2 * n + 2]


def _reduce_wait(name, ts, grads, landing, sems, afters):
    n = len(ts)
    first_out = 2 * n + 2 + len(afters)

    def body(*refs):
        for cp in _reduce_copies(ts, refs[first_out:first_out + n], refs[first_out + n:first_out + 2 * n],
                                 refs[2 * n], refs[2 * n + 1]):
            cp.wait()

    outs = pl.pallas_call(
        body, name=name,
        in_specs=[HBM] * (2 * n) + [SEM, SEM] + [ANY] * len(afters), out_specs=[HBM] * (2 * n),
        out_shape=[pltpu.HBM(g.shape, BF16) for g in grads] + [pltpu.HBM(r.shape, BF16) for r in landing],
        input_output_aliases={i: i for i in range(2 * n)},
        compiler_params=pltpu.CompilerParams(has_side_effects=DATAFLOW),
    )(*grads, *landing, sems[0], sems[1], *afters)
    return outs[0:n], outs[n:2 * n]


def _add4(name, t, own, landed, b1):
    rb, cb = _block_dims(t)
    tm = min(256, rb)
    if BLOCK_AXIS[t] == 1:
        own_spec = pl.BlockSpec((tm, cb), lambda i, br: (br[0] * (rb // tm) + i, 0))
    else:
        own_spec = pl.BlockSpec((tm, cb), lambda i, br: (i, br[0]))

    def body(b_ref, o_ref, r0_ref, r1_ref, r2_ref, s_ref):
        del b_ref
        s_ref[...] = ((o_ref[...].astype(F32) + r0_ref[...].astype(F32))
                      + (r1_ref[...].astype(F32) + r2_ref[...].astype(F32))).astype(BF16)

    def got(j):
        return pl.BlockSpec((None, tm, cb), lambda i, br: (j, i, 0))

    return pl.pallas_call(
        body, name=name,
        grid_spec=pltpu.PrefetchScalarGridSpec(
            num_scalar_prefetch=1, grid=(rb // tm,),
            in_specs=[own_spec, got(0), got(1), got(2)],
            out_specs=pl.BlockSpec((tm, cb), lambda i, br: (i, 0))),
        out_shape=jax.ShapeDtypeStruct((rb, cb), BF16),
        compiler_params=_cp("parallel"),
    )(b1, own, landed, landed, landed)


def _swap_sib(name, sums):
    def body(*refs):
        s_refs, t_refs, send_sems, recv_sems = refs[0:4], refs[4:8], refs[8], refs[9]
        x, y, c, _ = _place()
        cps = [pltpu.make_async_remote_copy(src_ref=s_refs[t], dst_ref=t_refs[t], send_sem=send_sems.at[t],
                                            recv_sem=recv_sems.at[t], device_id=(x, y, 1 - c), device_id_type=MESH)
               for t in range(4)]
        for cp in cps:
            cp.start()
        for cp in cps:
            cp.wait()

    return pl.pallas_call(
        body, name=name,
        in_specs=[ANY] * 4, out_specs=[ANY] * 4,
        out_shape=[jax.ShapeDtypeStruct(s.shape, BF16) for s in sums],
        scratch_shapes=[pltpu.SemaphoreType.DMA((4,)), pltpu.SemaphoreType.DMA((4,))],
    )(*sums)


def _adamw_pair(name, l, s_own, s_sib, w, m, v, outs):
    rb, cb = s_own.shape
    tm = min(256, rb)

    def body(a_ref, b_ref, w_ref, m_ref, v_ref, g0, d0, m0, v0, go_ref, d_ref, mo_ref, vo_ref):
        del g0, d0, m0, v0
        gv = a_ref[...].astype(F32) + b_ref[...].astype(F32)
        go_ref[...], d_ref[...], mo_ref[...], vo_ref[...] = _adamw_math(gv, w_ref[...], m_ref[...], v_ref[...])

    part = pl.BlockSpec((tm, cb), lambda i: (i, 0))
    layer = pl.BlockSpec((None, tm, cb), lambda i: (l, i, 0))
    return pl.pallas_call(
        body, name=name, grid=(rb // tm,),
        in_specs=[part, part, layer, layer, layer] + [ANY] * 4,
        out_specs=[layer] * 4,
        out_shape=[jax.ShapeDtypeStruct((DEPTH, rb, cb), F32)] * 4,
        input_output_aliases={5 + i: i for i in range(4)},
        compiler_params=_cp("parallel"),
    )(s_own, s_sib, w, m, v, *outs)


def _all_gather8(name, v, dep):
    m_per, n = v.shape

    def body(v_ref, dep_ref, out_ref, send_sems, recv_sems, local_sem):
        del dep_ref
        x, y, c, chips = _place()
        me, sib = (x, y, c), (x, y, 1 - c)

        def rows(px, py, pc):
            return out_ref.at[pl.ds((4 * px + 2 * py + pc) * m_per, m_per), :]

        def copy(k, block, to, src=None):
            return pltpu.make_async_remote_copy(
                src_ref=rows(*block) if src is None else src, dst_ref=rows(*block),
                send_sem=send_sems.at[k], recv_sem=recv_sems.at[k], device_id=to, device_id_type=MESH)

        mine = pltpu.make_async_copy(v_ref, rows(*me), local_sem)
        mine.start()
        first = [copy(0, me, sib, src=v_ref)]
        first += [copy(1 + j, me, (*chip, c), src=v_ref) for j, chip in enumerate(chips)]
        for cp in first:
            cp.start()
        passed = [copy(4 + j, (*chip, c), sib) for j, chip in enumerate(chips)]
        for j, chip in enumerate(chips):
            copy(1 + j, (*chip, c), me).wait_recv()
            passed[j].start()
        copy(0, sib, me).wait_recv()
        for j, chip in enumerate(chips):
            copy(4 + j, (*chip, 1 - c), me).wait_recv()
        for cp in first + passed:
            cp.wait_send()
        mine.wait()

    return pl.pallas_call(
        body, name=name,
        out_shape=jax.ShapeDtypeStruct((8 * m_per, n), v.dtype),
        in_specs=[pl.BlockSpec(memory_space=pltpu.VMEM), ANY],
        out_specs=pl.BlockSpec(memory_space=pltpu.VMEM),
        scratch_shapes=[pltpu.SemaphoreType.DMA((7,)), pltpu.SemaphoreType.DMA((7,)), pltpu.SemaphoreType.DMA],
    )(v, dep)


def _sum8(name, g):
    def body(g_ref, o_ref):
        acc = g_ref[0]
        for d in range(1, 8):
            acc = acc + g_ref[d]
        o_ref[...] = acc

    return pl.pallas_call(body, name=name, out_shape=jax.ShapeDtypeStruct(g.shape[1:], F32))(g)


def _pack(parts):
    flat = []
    for a in parts:
        a = a.reshape(-1)
        flat.append(jnp.pad(a, (0, (-a.shape[0]) % 128)))
    cat = jnp.concatenate(flat)
    cat = jnp.pad(cat, (0, (-cat.shape[0]) % 1024))
    return cat.reshape(-1, 128)


def _unpack(packed, shapes):
    flat = packed.reshape(-1)
    out, at = [], 0
    for shp in shapes:
        n = 1
        for d in shp:
            n *= d
        out.append(flat[at:at + n].reshape(shp))
        at += n + (-n) % 128
    return out


def _local_step(x, target, layer_weights, on_grads, small):
    qg_all = jnp.tile(small["q_norm_g"], (1, 8))
    kg_all = jnp.tile(small["k_norm_g"], (1, 8))
    bias_all = _bias_layout(_bias_expand("bias_expand", jnp.pad(small["rel_bias"], ((0, 0), (0, 0), (0, NIDX - 257)))))
    same_group = jnp.eye(4, dtype=F32)[None, :, None, :, None]
    pwbd_all = (small["pool_w"][:, :, :, None, :] * same_group).reshape(DEPTH, PWD, PWD)
    saved = []
    xin = x
    h = _rmsnorm("norm_first", x, small["norm1_g"][0:1])
    for l in range(DEPTH):
        w_in = layer_weights(l, (0,), xin)[0]
        qg, kg, bias = qg_all[l:l + 1], kg_all[l:l + 1], bias_all[l]
        cw, pwbd, ps = small["conv_w"][l], pwbd_all[l], small["pool_scale"][l:l + 1]
        p = _mm_nn(f"proj_in_{l}", h, w_in, l, F32)
        q, qt, kp, kt, vp, vt = _qkv(f"qkv_{l}", p, qg, kg)
        o, lse = _attn_fwd(f"attn_fwd_{l}", kp, qt, vt, bias)
        w_in, w_out, w_1, w_2 = layer_weights(l, (1, 2, 3), o)
        mix = _convpool_fwd(f"convpool_fwd_{l}", p, o, cw, pwbd, ps)
        x1, h2 = _mm_res_norm(f"proj_out_{l}", mix, w_out, l, xin, small["norm2_g"][l:l + 1])
        gnext = small["norm1_g"][(l + 1) % DEPTH][None]
        a, x2, hnext = _mlp_fwd(f"mlp_{l}", h2, w_1, w_2, l, x1, gnext)
        saved.append(dict(xin=xin, h=h, p=p, q=q, qt=qt, kp=kp, kt=kt, vp=vp, bias=bias, mix=mix, x1=x1, h2=h2, a=a, lse=lse,
                          qg=qg, kg=kg, cw=cw, pwbd=pwbd, ps=ps))
        xin, h = x2, hnext

    dx, dxb, loss = _loss_grad("loss_grad", xin, target)
    raw = {k: [None] * DEPTH for k in ("dg1", "dqg", "dkg", "db", "dw0", "dw1", "dw2", "dpw", "dps", "dg2")}
    for l in reversed(range(DEPTH)):
        sv = saved[l]
        da = _mm_nt_relu(f"mlp2_bwd_{l}", dxb, w_2, l, sv["a"])
        g_2 = _mm_tn(f"mlp2_wgrad_{l}", sv["a"], dxb, 512, 1024, relu2=True)
        g_1 = _mm_tn(f"mlp1_wgrad_{l}", sv["h2"], da, 1024, 512)
        dep = on_grads(l, (2, 3), (g_1, g_2))
        dx1, dx1b, dg2 = _mm_nt_normbwd(f"mlp1_bwd_{l}", da, w_1, l, sv["x1"], small["norm2_g"][l:l + 1], dx, dep)
        do, dot, dmix, dl = _proj_out_bwd(f"proj_out_bwd_{l}", dx1b, w_out, l, sv["mix"])
        g_out = _mm_tn(f"proj_out_wgrad_{l}", sv["mix"], dx1b, 512, 1024)
        dcp, dw0, dw1, dw2, dps, dpw = _convpool_bwd(f"convpool_bwd_{l}", sv["p"], dmix, sv["cw"], sv["pwbd"], sv["ps"])
        dq, dkp, dvp, db = _attn_bwd(f"attn_bwd_{l}", sv["q"], sv["qt"], sv["kp"], sv["kt"], sv["vp"], sv["bias"],
                                     do, dot, sv["lse"], _rowsum_layout(dl, x.shape[0] // UNIT))
        dp, dqg, dkg = _qkv_bwd(f"qkv_bwd_{l}", sv["p"], dq, dkp, dvp, dcp, sv["qg"], sv["kg"])
        g_in = _mm_tn(f"proj_in_wgrad_{l}", sv["h"], dp, 1024, 640)
        dep = on_grads(l, (0, 1), (g_in, g_out))
        dx, dxb, dg1 = _mm_nt_normbwd(f"proj_in_bwd_{l}", dp, w_in, l, sv["xin"], small["norm1_g"][l:l + 1], dx1, dep)
        for k, val in dict(dg1=dg1, dqg=dqg, dkg=dkg, db=db, dw0=dw0, dw1=dw1, dw2=dw2, dpw=dpw, dps=dps, dg2=dg2).items():
            raw[k][l] = val
    cat = {k: jnp.concatenate(v, axis=0) for k, v in raw.items() if k not in ("db", "dpw")}
    drb = _bias_reduce("bias_reduce", _bias_unlayout(jnp.stack(raw["db"])))
    dpw = jnp.stack(raw["dpw"])
    gsmall = {
        "norm1_g": cat["dg1"], "q_norm_g": cat["dqg"][:, :HD], "k_norm_g": cat["dkg"][:, :HD],
        "rel_bias": drb[:, :, :257],
        "conv_w": jnp.stack([cat["dw0"], cat["dw1"], cat["dw2"]], axis=1),
        "pool_w": jnp.stack([dpw[:, g * 64:(g + 1) * 64, g * 64:(g + 1) * 64] for g in range(4)], axis=1),
        "pool_scale": cat["dps"], "norm2_g": cat["dg2"],
    }
    return loss, dx, gsmall


SMALL = ("norm1_g", "q_norm_g", "k_norm_g", "rel_bias", "conv_w", "pool_w", "pool_scale", "norm2_g")
LARGE = ("w_in", "w_out", "w_mlp1", "w_mlp2")


def kernel(x, norm1_g, w_in, q_norm_g, k_norm_g, rel_bias, conv_w, pool_w, pool_scale, w_out, norm2_g, w_mlp1, w_mlp2, loss_target, m_norm1_g, m_w_in, m_q_norm_g, m_k_norm_g, m_rel_bias, m_conv_w, m_pool_w, m_pool_scale, m_w_out, m_norm2_g, m_w_mlp1, m_w_mlp2, v_norm1_g, v_w_in, v_q_norm_g, v_k_norm_g, v_rel_bias, v_conv_w, v_pool_w, v_pool_scale, v_w_out, v_norm2_g, v_w_mlp1, v_w_mlp2):
    w = dict(norm1_g=norm1_g, w_in=w_in, q_norm_g=q_norm_g, k_norm_g=k_norm_g, rel_bias=rel_bias, conv_w=conv_w,
             pool_w=pool_w, pool_scale=pool_scale, w_out=w_out, norm2_g=norm2_g, w_mlp1=w_mlp1, w_mlp2=w_mlp2)
    m = dict(norm1_g=m_norm1_g, w_in=m_w_in, q_norm_g=m_q_norm_g, k_norm_g=m_k_norm_g, rel_bias=m_rel_bias,
             conv_w=m_conv_w, pool_w=m_pool_w, pool_scale=m_pool_scale, w_out=m_w_out, norm2_g=m_norm2_g,
             w_mlp1=m_w_mlp1, w_mlp2=m_w_mlp2)
    v = dict(norm1_g=v_norm1_g, w_in=v_w_in, q_norm_g=v_q_norm_g, k_norm_g=v_k_norm_g, rel_bias=v_rel_bias,
             conv_w=v_conv_w, pool_w=v_pool_w, pool_scale=v_pool_scale, w_out=v_w_out, norm2_g=v_norm2_g,
             w_mlp1=v_w_mlp1, w_mlp2=v_w_mlp2)
    ax, ay, ac = lax.axis_index("x"), lax.axis_index("y"), lax.axis_index("c")
    b1 = jnp.reshape(2 * ax + ay, (1,)).astype(jnp.int32)

    cw_rows = _all_gather8("gather_conv_w", jnp.pad(conv_w.reshape(DEPTH * 3, 64), ((0, 4), (0, 64))), b1)
    cw_chips = [cw_rows[(4 * cx + 2 * cy) * 16:(4 * cx + 2 * cy) * 16 + 12, :64] for cx in range(2) for cy in range(2)]
    small = {n: w[n] for n in SMALL}
    small["conv_w"] = jnp.concatenate(cw_chips, axis=1).reshape(DEPTH, 3, CW)

    casts = [_cast_into_full(f"cast_{n}", t, w[n], b1, cw_rows) for t, n in enumerate(LARGE)]
    first, first_sems = _gather_start("gather_start_first", (0,), casts)
    held = [first]
    sems = dict(first_sems)

    def layer_weights(l, ts, after):
        if l > 0:
            ts = (0, 1, 2, 3) if ts == (0,) else ()
        if ts:
            tag = f"{l}_{ts[0]}"
            arrived = _gather_wait(f"gather_wait_{tag}", l, ts, held[0], sems[l], after)
            if l == 0 and ts == (0,):
                arrived, rest_sems = _gather_start("gather_start_rest", tuple(range(1, DEPTH)), arrived)
                sems.update(rest_sems)
            held[0] = _pass_on(f"pass_on_{tag}", l, ts, arrived)
        return held[0]

    flights = {}

    def await_flight(l, ts, afters):
        g, landing, sm, _ = flights[l, ts]
        flights[l, ts] = _reduce_wait(f"reduce_wait_{l}_{ts[0]}", ts, g, landing, sm, afters)

    def on_grads(l, ts, grads):
        if ts == (0, 1) and l + 1 < DEPTH:
            await_flight(l + 1, (2, 3), [grads[0]])
            await_flight(l + 1, (0, 1), [grads[0]])
        flights[l, ts] = _reduce_start(f"reduce_start_{l}_{ts[0]}", ts, grads)
        return flights[l, ts][3]

    loss_part, grad_x, gsmall = _local_step(x[0], loss_target[0], layer_weights, on_grads, small)
    loss = lax.psum(loss_part[0, 0], ("x", "y", "c"))
    order = [n for n in SMALL]
    packed = _pack([gsmall[n] for n in order])

    out = {n: [lax.empty(w[n].shape, F32) for _ in range(4)] for n in LARGE}
    for l in reversed(range(DEPTH)):
        if l == 0:
            afters = [grad_x, packed] + [out[n][0] for n in LARGE]
            await_flight(0, (2, 3), afters)
            await_flight(0, (0, 1), afters)
        sums = [None] * 4
        for ts in ((0, 1), (2, 3)):
            g, landing = flights[l, ts]
            for i, t in enumerate(ts):
                sums[t] = _add4(f"add4_{LARGE[t]}_{l}", t, g[i], landing[i], b1)
        theirs = _swap_sib(f"swap_sib_{l}", sums)
        for t, n in enumerate(LARGE):
            out[n] = _adamw_pair(f"adamw_{n}_{l}", l, sums[t], theirs[t], w[n], m[n], v[n], out[n])

    rows = packed.shape[0]
    summed = _sum8("sum_small", _all_gather8("gather_small", packed, out[LARGE[0]][0]).reshape(8, rows, 128))
    gfull = dict(zip(order, _unpack(summed, [gsmall[n].shape for n in order])))
    gfull["conv_w"] = lax.dynamic_slice_in_dim(gfull["conv_w"], (2 * ax + ay) * 64, 64, axis=2)
    res = _adamw("adamw_small", _pack([gfull[n] for n in order]), _pack([w[n] for n in order]),
                 _pack([m[n] for n in order]), _pack([v[n] for n in order]))
    for n, parts in zip(order, zip(*[_unpack(r, [w[k].shape for k in order]) for r in res])):
        out[n] = list(parts)

    names = ("norm1_g", "w_in", "q_norm_g", "k_norm_g", "rel_bias", "conv_w", "pool_w", "pool_scale", "w_out",
             "norm2_g", "w_mlp1", "w_mlp2")
    flat = [loss, grad_x[None]]
    for i in range(4):
        flat += [out[n][i] for n in names]
    return tuple(flat)
```

```python
import functools

import jax
import jax.numpy as jnp
from jax import lax
from jax.experimental import pallas as pl
from jax.experimental.pallas import tpu as pltpu

F32 = jnp.float32
BF16 = jnp.bfloat16

D = 1024
DEPTH = 4
CH = 64
NPREV = 8
KB = (NPREV + 1) * CH
PADR = NPREV * CH
HD = 64
AW = 512
CW = 256
PWD = 256
DIN = 3 * AW + 3 * CW + PWD
DFF = 4 * D
NIDX = 384
EPS = 1e-6
NEG_INF = -1e30

ADAM_LR = 0.001
ADAM_B1 = 0.9
ADAM_B2 = 0.999
ADAM_EPS = 1e-08
ADAM_WD = 0.01
ADAM_STEP = 10

VMEM_LIMIT = 52 * 1024 * 1024
MM_ROWS = 512
MESH = pl.DeviceIdType.MESH
ANY = pl.BlockSpec(memory_space=pl.ANY)


def _cp(*sem):
    return pltpu.CompilerParams(dimension_semantics=sem, vmem_limit_bytes=VMEM_LIMIT)


def _inv_rms(x):
    return lax.rsqrt(jnp.mean(x * x, axis=-1, keepdims=True) + EPS)


def _head_mean_matrix():
    r = lax.broadcasted_iota(jnp.int32, (AW, AW), 0) // HD
    c = lax.broadcasted_iota(jnp.int32, (AW, AW), 1) // HD
    return jnp.where(r == c, 1.0 / HD, 0.0).astype(BF16)


def _two_pass_dot(x, m):
    hi = x.astype(BF16)
    lo = (x - hi.astype(F32)).astype(BF16)
    return (jnp.dot(hi, m, preferred_element_type=F32)
            + jnp.dot(lo, m, preferred_element_type=F32))


def _head_mean(x, hm):
    return _two_pass_dot(x, hm)


def _rmsnorm(name, x, g):
    s = x.shape[0]
    tm = 512

    def body(x_ref, g_ref, h_ref):
        xv = x_ref[...]
        h_ref[...] = (xv * _inv_rms(xv) * g_ref[...]).astype(BF16)

    return pl.pallas_call(
        body, name=name, grid=(s // tm,),
        in_specs=[pl.BlockSpec((tm, D), lambda i: (i, 0)), pl.BlockSpec((1, D), lambda i: (0, 0))],
        out_specs=pl.BlockSpec((tm, D), lambda i: (i, 0)),
        out_shape=jax.ShapeDtypeStruct((s, D), BF16),
        compiler_params=_cp("parallel"),
    )(x, g)


def _relu2(a):
    r = jnp.maximum(a, jnp.zeros_like(a))
    return r * r


def _mm_nn(name, a, w, l, out_dtype):
    s, k = a.shape
    n = w.shape[2]
    tm = MM_ROWS

    def body(a_ref, w_ref, o_ref):
        o_ref[...] = jnp.dot(a_ref[...], w_ref[...], preferred_element_type=F32).astype(o_ref.dtype)

    return pl.pallas_call(
        body, name=name, grid=(s // tm,),
        in_specs=[pl.BlockSpec((tm, k), lambda i: (i, 0)),
                  pl.BlockSpec((None, k, n), lambda i: (l, 0, 0))],
        out_specs=pl.BlockSpec((tm, n), lambda i: (i, 0)),
        out_shape=jax.ShapeDtypeStruct((s, n), out_dtype),
        compiler_params=_cp("parallel"),
    )(a, w)


def _mm_res_norm(name, a, w, l, res, g):
    s, k = a.shape
    tm = MM_ROWS

    def body(a_ref, w_ref, r_ref, g_ref, x_ref, h_ref):
        acc = r_ref[...] + jnp.dot(a_ref[...], w_ref[...], preferred_element_type=F32)
        x_ref[...] = acc
        h_ref[...] = (acc * _inv_rms(acc) * g_ref[...]).astype(BF16)

    return pl.pallas_call(
        body, name=name, grid=(s // tm,),
        in_specs=[pl.BlockSpec((tm, k), lambda i: (i, 0)),
                  pl.BlockSpec((None, k, D), lambda i: (l, 0, 0)),
                  pl.BlockSpec((tm, D), lambda i: (i, 0)),
                  pl.BlockSpec((1, D), lambda i: (0, 0))],
        out_specs=[pl.BlockSpec((tm, D), lambda i: (i, 0))] * 2,
        out_shape=[jax.ShapeDtypeStruct((s, D), F32), jax.ShapeDtypeStruct((s, D), BF16)],
        compiler_params=_cp("parallel"),
    )(a, w, res, g)


def _mlp_fwd(name, h2, w1, w2, l, res, g):
    s = h2.shape[0]
    tm = 256

    def body(h_ref, w1_ref, w2_ref, r_ref, g_ref, a_ref, x_ref, hn_ref):
        a = jnp.dot(h_ref[...], w1_ref[...], preferred_element_type=F32).astype(BF16)
        a_ref[...] = a
        acc = r_ref[...] + jnp.dot(_relu2(a), w2_ref[...], preferred_element_type=F32)
        x_ref[...] = acc
        hn_ref[...] = (acc * _inv_rms(acc) * g_ref[...]).astype(BF16)

    once = pl.Buffered(1)
    rows = pl.BlockSpec((tm, D), lambda i: (i, 0))
    return pl.pallas_call(
        body, name=name, grid=(s // tm,),
        in_specs=[rows,
                  pl.BlockSpec((None, D, DFF), lambda i: (l, 0, 0), pipeline_mode=once),
                  pl.BlockSpec((None, DFF, D), lambda i: (l, 0, 0), pipeline_mode=once),
                  rows, pl.BlockSpec((1, D), lambda i: (0, 0))],
        out_specs=[pl.BlockSpec((tm, DFF), lambda i: (i, 0)), rows, rows],
        out_shape=[jax.ShapeDtypeStruct((s, DFF), BF16), jax.ShapeDtypeStruct((s, D), F32),
                   jax.ShapeDtypeStruct((s, D), BF16)],
        compiler_params=_cp("parallel"),
    )(h2, w1, w2, res, g)


def _qkv(name, p, qg, kg):
    s = p.shape[0]
    tm = PADR
    nb = s // tm

    def body(pq_ref, pk_ref, pv_ref, qg_ref, kg_ref, q_ref, qt_ref, k_ref, kt_ref, v_ref, vt_ref):
        t = pl.program_id(0)
        hm = _head_mean_matrix()

        def nrm(x, g):
            return x * lax.rsqrt(_head_mean(x * x, hm) + EPS) * g

        first = t == 0
        qq = nrm(pq_ref[...], qg_ref[...]) * 0.125
        kk = jnp.where(first, 0.0, nrm(pk_ref[...], kg_ref[...]))
        vv = jnp.where(first, 0.0, pv_ref[...])
        q_ref[...] = qq.astype(BF16)
        qt_ref[...] = qq.T.astype(BF16)
        k_ref[...] = kk.astype(BF16)
        kt_ref[...] = kk.T.astype(BF16)
        v_ref[...] = vv.astype(BF16)
        vt_ref[...] = vv.T.astype(BF16)

    def src(col):
        return pl.BlockSpec((tm, AW), lambda t: (jnp.maximum(t - 1, 0), col))

    gspec = pl.BlockSpec((1, AW), lambda t: (0, 0))
    rows = pl.BlockSpec((tm, AW), lambda t: (t, 0))
    cols = pl.BlockSpec((AW, tm), lambda t: (0, t))
    return pl.pallas_call(
        body, name=name, grid=(nb + 1,),
        in_specs=[src(0), src(1), src(2), gspec, gspec],
        out_specs=[pl.BlockSpec((tm, AW), lambda t: (jnp.maximum(t - 1, 0), 0)),
                   pl.BlockSpec((AW, tm), lambda t: (0, jnp.maximum(t - 1, 0))),
                   rows, cols, rows, cols],
        out_shape=[jax.ShapeDtypeStruct((s, AW), BF16), jax.ShapeDtypeStruct((AW, s), BF16),
                   jax.ShapeDtypeStruct((s + PADR, AW), BF16), jax.ShapeDtypeStruct((AW, s + PADR), BF16),
                   jax.ShapeDtypeStruct((s + PADR, AW), BF16), jax.ShapeDtypeStruct((AW, s + PADR), BF16)],
        compiler_params=_cp("arbitrary"),
    )(p, p, p, qg, kg)


NBAND = KB // CH
HIGHEST = lax.Precision.HIGHEST
NT_DIMS = (((1,), (1,)), ((), ()))


def _onehot_table(a):
    m = lax.broadcasted_iota(jnp.int32, (128, NIDX), 0)
    idx = lax.broadcasted_iota(jnp.int32, (128, NIDX), 1)
    rel = jnp.clip(KB - 1 - (CH * a + m), -128, 128) + 128
    return jnp.where(rel == idx, 1.0, 0.0).astype(F32)


def _onehot_diagonal():
    r = lax.broadcasted_iota(jnp.int32, (CH * CH, 128), 0)
    m = lax.broadcasted_iota(jnp.int32, (CH * CH, 128), 1)
    return jnp.where((r % CH) - (r // CH) + (CH - 1) == m, 1.0, 0.0).astype(F32)


def _bias_expand(name, rb):
    def body(rb_ref, o_ref):
        along = [lax.dot_general(rb_ref[...], _onehot_table(a), NT_DIMS, preferred_element_type=F32,
                                 precision=HIGHEST) for a in range(NBAND)]
        o_ref[...] = lax.dot_general(jnp.concatenate(along, axis=0), _onehot_diagonal(), NT_DIMS,
                                     preferred_element_type=F32, precision=HIGHEST)

    return pl.pallas_call(
        body, name=name, grid=(DEPTH,),
        in_specs=[pl.BlockSpec((None, 8, NIDX), lambda l: (l, 0, 0))],
        out_specs=pl.BlockSpec((None, NBAND * 8, CH * CH), lambda l: (l, 0, 0)),
        out_shape=jax.ShapeDtypeStruct((DEPTH, NBAND * 8, CH * CH), F32),
        compiler_params=_cp("parallel"),
    )(rb)


def _bias_reduce(name, db):
    def body(db_ref, o_ref):
        along = jnp.dot(db_ref[...], _onehot_diagonal(), preferred_element_type=F32, precision=HIGHEST)
        acc = jnp.zeros((8, NIDX), F32)
        for a in range(NBAND):
            acc = acc + jnp.dot(along[8 * a:8 * a + 8, :], _onehot_table(a), preferred_element_type=F32,
                                precision=HIGHEST)
        o_ref[...] = acc

    return pl.pallas_call(
        body, name=name, grid=(DEPTH,),
        in_specs=[pl.BlockSpec((None, NBAND * 8, CH * CH), lambda l: (l, 0, 0))],
        out_specs=pl.BlockSpec((None, 8, NIDX), lambda l: (l, 0, 0)),
        out_shape=jax.ShapeDtypeStruct((DEPTH, 8, NIDX), F32),
        compiler_params=_cp("parallel"),
    )(db)


def _bias_layout(flat):
    b = flat.reshape(DEPTH, NBAND, 8, CH, CH).transpose(0, 2, 1, 4, 3).reshape(DEPTH, 4, 2, KB, CH)
    pair = b.transpose(0, 1, 3, 2, 4).reshape(DEPTH, 4, KB, 128)
    first = jnp.pad(pair, ((0, 0), (0, 0), (0, CH), (0, 0)), constant_values=NEG_INF)
    second = jnp.pad(pair, ((0, 0), (0, 0), (CH, 0), (0, 0)), constant_values=NEG_INF)
    return jnp.concatenate([first, second], axis=3)


def _bias_unlayout(dbt):
    b = dbt.reshape(DEPTH, 4, NBAND, CH, 2, CH)
    return b.transpose(0, 2, 1, 4, 5, 3).reshape(DEPTH, NBAND * 8, CH * CH)


UNIT = 2 * CH
BAND2 = KB + CH


def _pair_weights(xt):
    x = xt.astype(F32)
    row = lax.broadcasted_iota(jnp.int32, (128, UNIT), 0)
    low = lax.broadcasted_iota(jnp.int32, (128, UNIT), 1) < HD
    swapped = pltpu.roll(x, HD, 1)
    same = (row < HD) == low
    first = jnp.where(same, jnp.where(low, x, swapped), 0.0)
    second = jnp.where(same, jnp.where(low, swapped, x), 0.0)
    return jnp.concatenate([first, second], axis=1).astype(BF16)


def _pair_rows(x):
    low = lax.broadcasted_iota(jnp.int32, (CH, 128), 1) < HD
    zero = jnp.zeros((CH, 128), x.dtype)
    parts = []
    for c in range(2):
        xc = x[c * CH:(c + 1) * CH, :]
        parts += [jnp.where(low, xc, zero), jnp.where(low, zero, xc)]
    return jnp.concatenate(parts, axis=0)


def _unpair(raw):
    b0, b1 = raw[:, 0:128], raw[:, 128:256]
    row = lax.broadcasted_iota(jnp.int32, (128, 128), 0)
    low = lax.broadcasted_iota(jnp.int32, (128, 128), 1) < HD
    top = jnp.where(low, b0, pltpu.roll(b1, HD, 1))
    bottom = jnp.where(low, pltpu.roll(b0, HD, 1), b1)
    return jnp.where(row < HD, top, bottom).T


def _scores_t(kb, qw, bias2, row0, padded):
    s = jnp.dot(kb, qw, preferred_element_type=F32) + bias2
    if padded:
        s = jnp.where(row0 + lax.broadcasted_iota(jnp.int32, (BAND2, 256), 0) >= PADR, s, NEG_INF)
    return s


def _unit_loops(s, unit):
    lax.fori_loop(0, PADR // UNIT, lambda u, c: unit(u, True, c), 0, unroll=2)
    lax.fori_loop(PADR // UNIT, s // UNIT, lambda u, c: unit(u, False, c), 0, unroll=4)


def _attn_fwd(name, kp, qt, vt, bias2):
    s = qt.shape[1]
    nu = s // UNIT

    def body(k_ref, qt_ref, vt_ref, b_ref, o_ref, lse_ref):
        def unit(u, padded, carry):
            r0 = pl.multiple_of(u * UNIT, UNIT)
            sc = _scores_t(k_ref[pl.ds(r0, BAND2), :], _pair_weights(qt_ref[:, pl.ds(r0, UNIT)]), b_ref[...],
                           r0, padded)
            top = jnp.max(sc, axis=0, keepdims=True)
            e = jnp.exp(sc - top)
            total = jnp.sum(e, axis=0, keepdims=True)
            raw = jnp.dot(vt_ref[:, pl.ds(r0, BAND2)], e.astype(BF16), preferred_element_type=F32) * (1.0 / total)
            o_ref[pl.ds(r0, UNIT), :] = _unpair(raw).astype(BF16)
            lse_ref[u] = jnp.broadcast_to(top + jnp.log(total), (8, 256))
            return carry

        _unit_loops(s, unit)

    return pl.pallas_call(
        body, name=name, grid=(AW // 128,),
        in_specs=[pl.BlockSpec((s + PADR, 128), lambda h: (0, h)),
                  pl.BlockSpec((128, s), lambda h: (h, 0)),
                  pl.BlockSpec((128, s + PADR), lambda h: (h, 0)),
                  pl.BlockSpec((None, BAND2, 256), lambda h: (h, 0, 0))],
        out_specs=[pl.BlockSpec((s, 128), lambda h: (0, h)),
                   pl.BlockSpec((None, nu, 8, 256), lambda h: (h, 0, 0, 0))],
        out_shape=[jax.ShapeDtypeStruct((s, AW), BF16), jax.ShapeDtypeStruct((4, nu, 8, 256), F32)],
        compiler_params=_cp("parallel"),
    )(kp, qt, vt, bias2)


def _attn_bwd(name, q, qt, kp, kt, vp, bias2, do, dot, lse, dl):
    s = q.shape[0]
    nu = s // UNIT

    def body(q_ref, qt_ref, k_ref, kt_ref, v_ref, b_ref, do_ref, dot_ref, lse_ref, dl_ref,
             dq_ref, dk_ref, dv_ref, db_ref):
        dk_ref[...] = jnp.zeros_like(dk_ref)
        dv_ref[...] = jnp.zeros_like(dv_ref)
        db_ref[...] = jnp.zeros_like(db_ref)

        def unit(u, padded, carry):
            r0 = pl.multiple_of(u * UNIT, UNIT)
            rows, band = pl.ds(r0, UNIT), pl.ds(r0, BAND2)
            sc = _scores_t(k_ref[band, :], _pair_weights(qt_ref[:, rows]), b_ref[...], r0, padded)
            pt = jnp.exp(sc - lse_ref[u][0:1, :])
            dpt = jnp.dot(v_ref[band, :], _pair_weights(dot_ref[:, rows]), preferred_element_type=F32)
            ds = pt * (dpt - dl_ref[u][0:1, :])
            db_ref[...] += ds[0:KB, 0:128] + ds[CH:BAND2, 128:256]
            dsb = ds.astype(BF16)
            dq_ref[rows, :] = _unpair(jnp.dot(kt_ref[:, band], dsb, preferred_element_type=F32))
            dk_ref[band, :] += jnp.dot(dsb, _pair_rows(q_ref[rows, :]), preferred_element_type=F32)
            dv_ref[band, :] += jnp.dot(pt.astype(BF16), _pair_rows(do_ref[rows, :]), preferred_element_type=F32)
            return carry

        _unit_loops(s, unit)

    row_q = pl.BlockSpec((s, 128), lambda h: (0, h))
    col_q = pl.BlockSpec((128, s), lambda h: (h, 0))
    row_k = pl.BlockSpec((s + PADR, 128), lambda h: (0, h))
    col_k = pl.BlockSpec((128, s + PADR), lambda h: (h, 0))
    stat = pl.BlockSpec((None, nu, 8, 256), lambda h: (h, 0, 0, 0))
    return pl.pallas_call(
        body, name=name, grid=(AW // 128,),
        in_specs=[row_q, col_q, row_k, col_k, row_k,
                  pl.BlockSpec((None, BAND2, 256), lambda h: (h, 0, 0)), row_q, col_q, stat, stat],
        out_specs=[row_q, row_k, row_k, pl.BlockSpec((None, KB, 128), lambda h: (h, 0, 0))],
        out_shape=[jax.ShapeDtypeStruct((s, AW), F32),
                   jax.ShapeDtypeStruct((s + PADR, AW), F32),
                   jax.ShapeDtypeStruct((s + PADR, AW), F32),
                   jax.ShapeDtypeStruct((4, KB, 128), F32)],
        compiler_params=_cp("parallel"),
    )(q, qt, kp, kt, vp, bias2, do, dot, lse, dl)


def _rowsum_layout(dl, nu):
    d = dl[:, :8].reshape(nu, 2, CH, 4, 2)
    d = d.transpose(3, 0, 1, 4, 2).reshape(4, nu, 1, 256)
    return jnp.broadcast_to(d, (4, nu, 8, 256))


def _rows_before(cur, prev, k):
    row = lax.broadcasted_iota(jnp.int32, cur.shape, 0)
    return jnp.where(row >= k, pltpu.roll(cur, k, 0), pltpu.roll(prev, k, 0))


def _rows_after(cur, nxt, k):
    n = cur.shape[0]
    row = lax.broadcasted_iota(jnp.int32, cur.shape, 0)
    return jnp.where(row < n - k, pltpu.roll(cur, n - k, 0), pltpu.roll(nxt, n - k, 0))


def _pool_window_lanes():
    lg = lax.broadcasted_iota(jnp.int32, (1, PWD), 1) // 64
    return lg, jnp.where(lg == 0, 2.0, jnp.where(lg == 1, 4.0, jnp.where(lg == 2, 8.0, 16.0))).astype(F32)


def _pool_mean_minus_token(u, up, row0):
    lg, wv = _pool_window_lanes()
    sums = []
    c, p = u, up
    for k in (1, 2, 4, 8):
        c2 = c + _rows_before(c, p, k)
        p = p + pltpu.roll(p, k, 0)
        c = c2
        sums.append(c)
    win = jnp.where(lg == 0, sums[0], jnp.where(lg == 1, sums[1], jnp.where(lg == 2, sums[2], sums[3])))
    pos1 = (row0 + lax.broadcasted_iota(jnp.int32, u.shape, 0) + 1).astype(F32)
    cnt = jnp.minimum(pos1, wv)
    return win / cnt - u, cnt


def _conv_taps(z, zp, w0, w1, w2):
    z1 = _rows_before(z, zp, 1)
    z2 = _rows_before(z, zp, 2)
    return (w0 * z2 + w1 * z1) + w2 * z, z1, z2


CP_TM = 512


def _convpool_fwd(name, p, o, cw, pwbd, ps):
    s = p.shape[0]
    tm = CP_TM
    nb = s // tm

    def body(gb_ref, gc_ref, hin_ref, u_ref, gcp_ref, hinp_ref, up_ref, o_ref, cw_ref, pw_ref, ps_ref, mix_ref):
        i = pl.program_id(0)
        has_prev = i > 0
        z = gc_ref[...] * hin_ref[...]
        zp = jnp.where(has_prev, gcp_ref[...] * hinp_ref[...], 0.0)
        y3, _, _ = _conv_taps(z, zp, cw_ref[0:1, :], cw_ref[1:2, :], cw_ref[2:3, :])
        m, _ = _pool_mean_minus_token(u_ref[...], jnp.where(has_prev, up_ref[...], 0.0), i * tm)
        yp = jnp.dot(m.astype(BF16), pw_ref[...].astype(BF16), preferred_element_type=F32) * ps_ref[...]
        mix_ref[:, 0:AW] = o_ref[...]
        mix_ref[:, AW:AW + CW] = (gb_ref[...] * y3).astype(BF16)
        mix_ref[:, AW + CW:D] = yp.astype(BF16)

    def cur(col):
        return pl.BlockSpec((tm, CW), lambda i: (i, col))

    def prev(col):
        return pl.BlockSpec((tm, CW), lambda i: (jnp.maximum(i - 1, 0), col))

    def whole(a):
        return pl.BlockSpec(a.shape, lambda i: (0,) * a.ndim)

    return pl.pallas_call(
        body, name=name, grid=(nb,),
        in_specs=[cur(6), cur(7), cur(8), cur(9), prev(7), prev(8), prev(9),
                  pl.BlockSpec((tm, AW), lambda i: (i, 0)), whole(cw), whole(pwbd), whole(ps)],
        out_specs=pl.BlockSpec((tm, D), lambda i: (i, 0)),
        out_shape=jax.ShapeDtypeStruct((s, D), BF16),
        compiler_params=_cp("parallel"),
    )(p, p, p, p, p, p, p, o, cw, pwbd, ps)


def _convpool_bwd(name, p, dmix, cw, pwbd, ps):
    s = p.shape[0]
    tm = CP_TM
    nb = s // tm

    def body(gb_ref, gc_ref, hin_ref, u_ref, gcp_ref, hinp_ref, up_ref, gbn_ref, dyc_ref, dyp_ref, dycn_ref, dypn_ref,
             cw_ref, pw_ref, ps_ref, dcp_ref, dw0_ref, dw1_ref, dw2_ref, dps_ref, dpw_ref):
        i = pl.program_id(0)
        has_prev = i > 0
        has_next = i < nb - 1
        w0, w1, w2 = cw_ref[0:1, :], cw_ref[1:2, :], cw_ref[2:3, :]
        gb, gc, hin = gb_ref[...], gc_ref[...], hin_ref[...]
        dyc = dyc_ref[...]
        z = gc * hin
        zp = jnp.where(has_prev, gcp_ref[...] * hinp_ref[...], 0.0)
        y3, z1, z2 = _conv_taps(z, zp, w0, w1, w2)
        dy3 = dyc * gb
        dy3n = jnp.where(has_next, dycn_ref[...] * gbn_ref[...], 0.0)
        dz = w2 * dy3 + w1 * _rows_after(dy3, dy3n, 1) + w0 * _rows_after(dy3, dy3n, 2)
        pw = pw_ref[...].astype(BF16)
        psv = ps_ref[...]
        m, cnt = _pool_mean_minus_token(u_ref[...], jnp.where(has_prev, up_ref[...], 0.0), i * tm)
        mb = m.astype(BF16)
        dyp = dyp_ref[...]
        dmp = (dyp * psv).astype(BF16)
        dmpn = jnp.where(has_next, dypn_ref[...] * psv, 0.0).astype(BF16)
        nt = (((1,), (1,)), ((), ()))
        dm = lax.dot_general(dmp, pw, nt, preferred_element_type=F32)
        dmn = lax.dot_general(dmpn, pw, nt, preferred_element_type=F32)
        lg, wv = _pool_window_lanes()
        cc, cn = dm / cnt, dmn / wv
        sums = []
        for k in (1, 2, 4, 8):
            c2 = cc + _rows_after(cc, cn, k)
            cn = cn + pltpu.roll(cn, tm - k, 0)
            cc = c2
            sums.append(cc)
        du = jnp.where(lg == 0, sums[0], jnp.where(lg == 1, sums[1], jnp.where(lg == 2, sums[2], sums[3]))) - dm
        dcp_ref[:, 0:CW] = (dyc * y3).astype(BF16)
        dcp_ref[:, CW:2 * CW] = (dz * hin).astype(BF16)
        dcp_ref[:, 2 * CW:3 * CW] = (dz * gc).astype(BF16)
        dcp_ref[:, 3 * CW:4 * CW] = du.astype(BF16)
        parts = (jnp.sum(dy3 * z2, axis=0, keepdims=True),
                 jnp.sum(dy3 * z1, axis=0, keepdims=True),
                 jnp.sum(dy3 * z, axis=0, keepdims=True),
                 jnp.sum(dyp * jnp.dot(mb, pw, preferred_element_type=F32), axis=0, keepdims=True),
                 lax.dot_general(mb, dmp, (((0,), (0,)), ((), ())), preferred_element_type=F32))
        accs = (dw0_ref, dw1_ref, dw2_ref, dps_ref, dpw_ref)

        @pl.when(i == 0)
        def _():
            for a, v in zip(accs, parts):
                a[...] = v

        @pl.when(i > 0)
        def _():
            for a, v in zip(accs, parts):
                a[...] += v

    def cur(col):
        return pl.BlockSpec((tm, CW), lambda i: (i, col))

    def prev(col):
        return pl.BlockSpec((tm, CW), lambda i: (jnp.maximum(i - 1, 0), col))

    def nxt(col):
        return pl.BlockSpec((tm, CW), lambda i: (jnp.minimum(i + 1, nb - 1), col))

    def whole(shape):
        return pl.BlockSpec(shape, lambda i: (0,) * len(shape))

    row = jax.ShapeDtypeStruct((1, CW), F32)
    return pl.pallas_call(
        body, name=name, grid=(nb,),
        in_specs=[cur(6), cur(7), cur(8), cur(9), prev(7), prev(8), prev(9), nxt(6),
                  cur(0), cur(1), nxt(0), nxt(1), whole(cw.shape), whole(pwbd.shape), whole(ps.shape)],
        out_specs=[pl.BlockSpec((tm, D), lambda i: (i, 0)), whole((1, CW)), whole((1, CW)), whole((1, CW)),
                   whole((1, PWD)), whole((PWD, PWD))],
        out_shape=[jax.ShapeDtypeStruct((s, D), BF16), row, row, row, row,
                   jax.ShapeDtypeStruct((PWD, PWD), F32)],
        compiler_params=_cp("arbitrary"),
    )(p, p, p, p, p, p, p, p, dmix, dmix, dmix, dmix, cw, pwbd, ps)


def _qkv_bwd(name, p, dq, dkp, dvp, dcp, qg, kg):
    s = p.shape[0]
    tm = 256
    off = PADR // tm

    def body(pq_ref, pk_ref, dq_ref, dk_ref, dv_ref, dcp_ref, qg_ref, kg_ref, dp_ref, dqg_ref, dkg_ref):
        i = pl.program_id(0)
        hm = _head_mean_matrix()

        def nrm_bwd(x, g, dy):
            r = lax.rsqrt(_head_mean(x * x, hm) + EPS)
            xn = x * r
            dxn = dy * g
            dx = r * (dxn - xn * _head_mean(dxn * xn, hm))
            dg = jnp.sum(dy * xn, axis=0, keepdims=True)
            dg = (dg[:, 0:128] + dg[:, 128:256]) + (dg[:, 256:384] + dg[:, 384:512])
            return dx, dg + pltpu.roll(dg, HD, 1)

        dxq, dgq = nrm_bwd(pq_ref[...], qg_ref[...], dq_ref[...] * 0.125)
        dxk, dgk = nrm_bwd(pk_ref[...], kg_ref[...], dk_ref[...])
        dp_ref[:, 0:AW] = dxq.astype(BF16)
        dp_ref[:, AW:2 * AW] = dxk.astype(BF16)
        dp_ref[:, 2 * AW:3 * AW] = dv_ref[...].astype(BF16)
        dp_ref[:, 3 * AW:DIN] = dcp_ref[...]

        @pl.when(i == 0)
        def _():
            dqg_ref[...] = dgq
            dkg_ref[...] = dgk

        @pl.when(i > 0)
        def _():
            dqg_ref[...] += dgq
            dkg_ref[...] += dgk

    gspec = pl.BlockSpec((1, AW), lambda i: (0, 0))
    gout = pl.BlockSpec((1, 128), lambda i: (0, 0))
    return pl.pallas_call(
        body, name=name, grid=(s // tm,),
        in_specs=[pl.BlockSpec((tm, AW), lambda i: (i, 0)), pl.BlockSpec((tm, AW), lambda i: (i, 1)),
                  pl.BlockSpec((tm, AW), lambda i: (i, 0)),
                  pl.BlockSpec((tm, AW), lambda i: (i + off, 0)),
                  pl.BlockSpec((tm, AW), lambda i: (i + off, 0)),
                  pl.BlockSpec((tm, D), lambda i: (i, 0)), gspec, gspec],
        out_specs=[pl.BlockSpec((tm, DIN), lambda i: (i, 0)), gout, gout],
        out_shape=[jax.ShapeDtypeStruct((s, DIN), BF16), jax.ShapeDtypeStruct((1, 128), F32),
                   jax.ShapeDtypeStruct((1, 128), F32)],
        compiler_params=_cp("arbitrary"),
    )(p, p, dq, dkp, dvp, dcp, qg, kg)


def _loss_grad(name, y, t):
    s = y.shape[0]
    tm = 512

    def body(y_ref, t_ref, dy_ref, dyb_ref, l_ref):
        i = pl.program_id(0)
        e = y_ref[...] - t_ref[...]
        dy = e * (1.0 / D)
        dy_ref[...] = dy
        dyb_ref[...] = dy.astype(BF16)
        part = 0.5 * jnp.sum(jnp.mean(e * e, axis=-1, keepdims=True), axis=0, keepdims=True)

        @pl.when(i == 0)
        def _():
            l_ref[...] = part

        @pl.when(i > 0)
        def _():
            l_ref[...] += part

    blk = pl.BlockSpec((tm, D), lambda i: (i, 0))
    return pl.pallas_call(
        body, name=name, grid=(s // tm,),
        in_specs=[blk, blk],
        out_specs=[blk, blk, pl.BlockSpec((1, 1), lambda i: (0, 0))],
        out_shape=[jax.ShapeDtypeStruct((s, D), F32), jax.ShapeDtypeStruct((s, D), BF16),
                   jax.ShapeDtypeStruct((1, 1), F32)],
        compiler_params=_cp("arbitrary"),
    )(y, t)


def _mm_nt_relu(name, dxb, w, l, a):
    s = dxb.shape[0]
    tm = MM_ROWS

    def body(d_ref, w_ref, a_ref, o_ref):
        df = lax.dot_general(d_ref[...], w_ref[...], NT_DIMS, preferred_element_type=F32)
        o_ref[...] = (df * (2.0 * jnp.maximum(a_ref[...].astype(F32), 0.0))).astype(BF16)

    return pl.pallas_call(
        body, name=name, grid=(s // tm,),
        in_specs=[pl.BlockSpec((tm, D), lambda i: (i, 0)),
                  pl.BlockSpec((None, DFF, D), lambda i: (l, 0, 0)),
                  pl.BlockSpec((tm, DFF), lambda i: (i, 0))],
        out_specs=pl.BlockSpec((tm, DFF), lambda i: (i, 0)),
        out_shape=jax.ShapeDtypeStruct((s, DFF), BF16),
        compiler_params=_cp("parallel"),
    )(dxb, w, a)


def _proj_out_bwd(name, dxb, w, l, mix):
    s = dxb.shape[0]
    tm = 512

    def body(d_ref, w_ref, o_ref, do_ref, dot_ref, dcp_ref, dl_ref):
        d = d_ref[...]
        wa, wc = w_ref[0:AW, :], w_ref[AW:D, :]
        do = lax.dot_general(d, wa, NT_DIMS, preferred_element_type=F32)
        do_ref[...] = do.astype(BF16)
        dot_ref[...] = lax.dot_general(wa, d, NT_DIMS, preferred_element_type=F32).astype(BF16)
        dcp_ref[...] = lax.dot_general(d, wc, NT_DIMS, preferred_element_type=F32)
        head = lax.broadcasted_iota(jnp.int32, (AW, 128), 0) // HD
        pick = jnp.where(head == lax.broadcasted_iota(jnp.int32, (AW, 128), 1), 1.0, 0.0).astype(BF16)
        dl_ref[...] = _two_pass_dot(do * o_ref[...].astype(F32), pick)

    return pl.pallas_call(
        body, name=name, grid=(s // tm,),
        in_specs=[pl.BlockSpec((tm, D), lambda i: (i, 0)),
                  pl.BlockSpec((None, D, D), lambda i: (l, 0, 0)),
                  pl.BlockSpec((tm, AW), lambda i: (i, 0))],
        out_specs=[pl.BlockSpec((tm, AW), lambda i: (i, 0)), pl.BlockSpec((AW, tm), lambda i: (0, i)),
                   pl.BlockSpec((tm, D - AW), lambda i: (i, 0)), pl.BlockSpec((tm, 128), lambda i: (i, 0))],
        out_shape=[jax.ShapeDtypeStruct((s, AW), BF16), jax.ShapeDtypeStruct((AW, s), BF16),
                   jax.ShapeDtypeStruct((s, D - AW), F32), jax.ShapeDtypeStruct((s, 128), F32)],
        compiler_params=_cp("parallel"),
    )(dxb, w, mix)


def _mm_nt_normbwd(name, gy, w, l, x, g, dres, dep):
    s, k = gy.shape
    tm = MM_ROWS

    def body(gy_ref, w_ref, x_ref, g_ref, dr_ref, dep_ref, dx_ref, dxb_ref, dg_ref):
        del dep_ref
        i = pl.program_id(0)
        dh = lax.dot_general(gy_ref[...], w_ref[...], NT_DIMS, preferred_element_type=F32)
        xv = x_ref[...]
        r = _inv_rms(xv)
        xn = xv * r
        dxn = dh * g_ref[...]
        dx = r * (dxn - xn * jnp.mean(dxn * xn, axis=-1, keepdims=True)) + dr_ref[...]
        dx_ref[...] = dx
        dxb_ref[...] = dx.astype(BF16)
        part = jnp.sum(dh * xn, axis=0, keepdims=True)

        @pl.when(i == 0)
        def _():
            dg_ref[...] = part

        @pl.when(i > 0)
        def _():
            dg_ref[...] += part

    blk = pl.BlockSpec((tm, D), lambda i: (i, 0))
    vec = pl.BlockSpec((1, D), lambda i: (0, 0))
    return pl.pallas_call(
        body, name=name, grid=(s // tm,),
        in_specs=[pl.BlockSpec((tm, k), lambda i: (i, 0)),
                  pl.BlockSpec((None, D, k), lambda i: (l, 0, 0)), blk, vec, blk, ANY],
        out_specs=[blk, blk, vec],
        out_shape=[jax.ShapeDtypeStruct((s, D), F32), jax.ShapeDtypeStruct((s, D), BF16),
                   jax.ShapeDtypeStruct((1, D), F32)],
        compiler_params=_cp("arbitrary"),
    )(gy, w, x, g, dres, dep)


def _mm_tn(name, a, b, tma, tnb, relu2=False):
    s, m = a.shape
    n = b.shape[1]

    def body(a_ref, b_ref, o_ref):
        av = _relu2(a_ref[...]) if relu2 else a_ref[...]
        o_ref[...] = lax.dot_general(av, b_ref[...], (((0,), (0,)), ((), ())),
                                     preferred_element_type=F32).astype(BF16)

    return pl.pallas_call(
        body, name=name, grid=(m // tma, n // tnb),
        in_specs=[pl.BlockSpec((s, tma), lambda i, j: (0, i)),
                  pl.BlockSpec((s, tnb), lambda i, j: (0, j))],
        out_specs=pl.BlockSpec((tma, tnb), lambda i, j: (i, j)),
        out_shape=jax.ShapeDtypeStruct((m, n), BF16),
        compiler_params=_cp("parallel", "parallel"),
    )(a, b)


def _adamw_math(gv, wv, mv, vv):
    mn = ADAM_B1 * mv + (1.0 - ADAM_B1) * gv
    vn = ADAM_B2 * vv + (1.0 - ADAM_B2) * jnp.square(gv)
    m_hat = mn / (1.0 - ADAM_B1 ** ADAM_STEP)
    v_hat = vn / (1.0 - ADAM_B2 ** ADAM_STEP)
    return gv, -ADAM_LR * (m_hat / (jnp.sqrt(v_hat) + ADAM_EPS) + ADAM_WD * wv), mn, vn


def _adamw(name, g, w, m, v):
    r, c = g.shape
    tm = 256 if r % 256 == 0 else r

    def body(g_ref, w_ref, m_ref, v_ref, go_ref, d_ref, mo_ref, vo_ref):
        go_ref[...], d_ref[...], mo_ref[...], vo_ref[...] = _adamw_math(g_ref[...], w_ref[...], m_ref[...], v_ref[...])

    blk = pl.BlockSpec((tm, c), lambda i: (i, 0))
    return pl.pallas_call(
        body, name=name, grid=(r // tm,),
        in_specs=[blk] * 4, out_specs=[blk] * 4,
        out_shape=[jax.ShapeDtypeStruct((r, c), F32)] * 4,
        compiler_params=_cp("parallel"),
    )(g, w, m, v)


def _place():
    x, y, c = lax.axis_index("x"), lax.axis_index("y"), lax.axis_index("c")
    chips = [(1 - x, y), (x, 1 - y), (1 - x, 1 - y)]
    return x, y, c, chips


BLOCK_AXIS = (2, 1, 2, 1)
LARGE_DIMS = ((D, DIN), (D, D), (D, DFF), (DFF, D))


def _full_shape(t, layers, dtype):
    r, c = LARGE_DIMS[t]
    return jax.ShapeDtypeStruct((layers, r, c), dtype)


def _cast_into_full(name, t, shard, b1, dep):
    _, r, c = shard.shape
    tm = min(256, r)
    if BLOCK_AXIS[t] == 1:
        out_spec = pl.BlockSpec((None, tm, c), lambda l, i, br: (l, br[0] * (r // tm) + i, 0))
    else:
        out_spec = pl.BlockSpec((None, tm, c), lambda l, i, br: (l, i, br[0]))

    def body(b_ref, x_ref, dep_ref, o_ref):
        del b_ref, dep_ref
        o_ref[...] = x_ref[...].astype(BF16)

    return pl.pallas_call(
        body, name=name,
        grid_spec=pltpu.PrefetchScalarGridSpec(
            num_scalar_prefetch=1, grid=(DEPTH, r // tm),
            in_specs=[pl.BlockSpec((None, tm, c), lambda l, i, br: (l, i, 0)), ANY],
            out_specs=out_spec),
        out_shape=_full_shape(t, DEPTH, BF16),
        compiler_params=_cp("parallel", "parallel"),
    )(b1, shard, dep)


HBM = pl.BlockSpec(memory_space=pltpu.HBM)
SEM = pl.BlockSpec(memory_space=pltpu.SEMAPHORE)
DATAFLOW = pltpu.SideEffectType.DATAFLOW_SIDE_EFFECTING


def _half(ref, l, t, b, c):
    r, cols = LARGE_DIMS[t]
    if BLOCK_AXIS[t] == 1:
        n = r // 8
        return ref.at[l, pl.ds(pl.multiple_of(b * (2 * n) + c * n, 16), n), :]
    n, w = r // 2, cols // 4
    return ref.at[l, pl.ds(pl.multiple_of(c * n, 16), n), pl.ds(pl.multiple_of(b * w, 128), w)]


def _gather_start(name, layers, ts, fulls):
    n = len(ts)

    def body(*refs):
        f_refs, sems = refs[n:2 * n], refs[2 * n:2 * n + 2 * len(layers)]
        x, y, c, chips = _place()
        for i, l in enumerate(layers):
            for k, t in enumerate(ts):
                own = _half(f_refs[k], l, t, 2 * x + y, c)
                for j, (cx, cy) in enumerate(chips):
                    pltpu.make_async_remote_copy(src_ref=own, dst_ref=own, send_sem=sems[2 * i].at[3 * t + j],
                                                 recv_sem=sems[2 * i + 1].at[3 * t + j], device_id=(cx, cy, c),
                                                 device_id_type=MESH).start()
        refs[-1][...] = jnp.zeros((8, 128), F32)

    outs = pl.pallas_call(
        body, name=name,
        in_specs=[HBM] * n,
        out_specs=[HBM] * n + [SEM] * (2 * len(layers)) + [pl.BlockSpec(memory_space=pltpu.VMEM)],
        out_shape=[pltpu.HBM(f.shape, f.dtype) for f in fulls]
        + [pltpu.SemaphoreType.DMA((12,))] * (2 * len(layers)) + [jax.ShapeDtypeStruct((8, 128), F32)],
        input_output_aliases={k: k for k in range(n)},
        compiler_params=pltpu.CompilerParams(has_side_effects=DATAFLOW),
    )(*[pltpu.with_memory_space_constraint(f, pltpu.HBM) for f in fulls])
    return outs[0:n], {l: (outs[n + 2 * i], outs[n + 1 + 2 * i]) for i, l in enumerate(layers)}, outs[-1]


def _gather_wait(name, l, ts, fulls, sems, after):
    def body(*refs):
        send_sems, recv_sems, f_refs = refs[4], refs[5], refs[7:11]
        x, y, c, chips = _place()
        for t in ts:
            own = _half(f_refs[t], l, t, 2 * x + y, c)
            for j, (cx, cy) in enumerate(chips):
                landed = _half(f_refs[t], l, t, 2 * cx + cy, c)
                pltpu.make_async_remote_copy(src_ref=own, dst_ref=landed, send_sem=send_sems.at[3 * t + j],
                                             recv_sem=recv_sems.at[3 * t + j], device_id=(cx, cy, c),
                                             device_id_type=MESH).wait()

    return pl.pallas_call(
        body, name=name,
        in_specs=[HBM] * 4 + [SEM, SEM, ANY], out_specs=[HBM] * 4,
        out_shape=[pltpu.HBM(s.shape, s.dtype) for s in (_full_shape(t, DEPTH, BF16) for t in range(4))],
        input_output_aliases={t: t for t in range(4)},
        compiler_params=pltpu.CompilerParams(has_side_effects=DATAFLOW),
    )(*fulls, sems[0], sems[1], after)


def _pass_on(name, l, ts, fulls):
    def body(*refs):
        f_refs, send_sems, recv_sems = refs[4:8], refs[8], refs[9]
        x, y, c, chips = _place()

        def copy(t, j, half):
            cx, cy = chips[j]
            part = _half(f_refs[t], l, t, 2 * cx + cy, half)
            return pltpu.make_async_remote_copy(src_ref=part, dst_ref=part, send_sem=send_sems.at[3 * t + j],
                                                recv_sem=recv_sems.at[3 * t + j], device_id=(x, y, 1 - c),
                                                device_id_type=MESH)

        for t in ts:
            for j in range(3):
                copy(t, j, c).start()
        for t in ts:
            for j in range(3):
                copy(t, j, 1 - c).wait_recv()
                copy(t, j, c).wait_send()

    return pl.pallas_call(
        body, name=name,
        in_specs=[ANY] * 4, out_specs=[ANY] * 4,
        out_shape=[_full_shape(t, DEPTH, BF16) for t in range(4)],
        input_output_aliases={t: t for t in range(4)},
        scratch_shapes=[pltpu.SemaphoreType.DMA((12,)), pltpu.SemaphoreType.DMA((12,))],
    )(*fulls)


def _block2d(ref, t, b):
    r, cols = LARGE_DIMS[t]
    if BLOCK_AXIS[t] == 1:
        return ref.at[pl.ds(pl.multiple_of(b * (r // 4), 16), r // 4), :]
    return ref.at[:, pl.ds(pl.multiple_of(b * (cols // 4), 128), cols // 4)]


def _block_dims(t):
    r, cols = LARGE_DIMS[t]
    return (r // 4, cols) if BLOCK_AXIS[t] == 1 else (r, cols // 4)


def _reduce_copies(ts, g_refs, r_refs, send_sems, recv_sems):
    _, _, c, chips = _place()
    return [pltpu.make_async_remote_copy(src_ref=_block2d(g_refs[i], t, 2 * cx + cy), dst_ref=r_refs[i].at[j],
                                         send_sem=send_sems.at[3 * i + j], recv_sem=recv_sems.at[3 * i + j],
                                         device_id=(cx, cy, c), device_id_type=MESH)
            for i, t in enumerate(ts) for j, (cx, cy) in enumerate(chips)]


def _reduce_start(name, ts, grads):
    n = len(ts)

    def body(*refs):
        for cp in _reduce_copies(ts, refs[n:2 * n], refs[2 * n:3 * n], refs[3 * n], refs[3 * n + 1]):
            cp.start()
        refs[3 * n + 2][...] = jnp.zeros((8, 128), F32)

    outs = pl.pallas_call(
        body, name=name,
        in_specs=[HBM] * n,
        out_specs=[HBM] * (2 * n) + [SEM, SEM, pl.BlockSpec(memory_space=pltpu.VMEM)],
        out_shape=[pltpu.HBM(g.shape, BF16) for g in grads]
        + [pltpu.HBM((3,) + _block_dims(t), BF16) for t in ts]
        + [pltpu.SemaphoreType.DMA((3 * n,)), pltpu.SemaphoreType.DMA((3 * n,)), jax.ShapeDtypeStruct((8, 128), F32)],
        input_output_aliases={i: i for i in range(n)},
        compiler_params=pltpu.CompilerParams(has_side_effects=DATAFLOW),
    )(*[pltpu.with_memory_space_constraint(g, pltpu.HBM) for g in grads])
    return outs[0:n], outs[n:2 * n], (outs[2 * n], outs[2 * n + 1]), outs[2 * n + 2]


def _reduce_wait(name, ts, grads, landing, sems, afters):
    n = len(ts)
    first_out = 2 * n + 2 + len(afters)

    def body(*refs):
        for cp in _reduce_copies(ts, refs[first_out:first_out + n], refs[first_out + n:first_out + 2 * n],
                                 refs[2 * n], refs[2 * n + 1]):
            cp.wait()

    outs = pl.pallas_call(
        body, name=name,
        in_specs=[HBM] * (2 * n) + [SEM, SEM] + [ANY] * len(afters), out_specs=[HBM] * (2 * n),
        out_shape=[pltpu.HBM(g.shape, BF16) for g in grads] + [pltpu.HBM(r.shape, BF16) for r in landing],
        input_output_aliases={i: i for i in range(2 * n)},
        compiler_params=pltpu.CompilerParams(has_side_effects=DATAFLOW),
    )(*grads, *landing, sems[0], sems[1], *afters)
    return outs[0:n], outs[n:2 * n]


def _add4(name, t, own, landed, b1):
    rb, cb = _block_dims(t)
    tm = min(256, rb)
    if BLOCK_AXIS[t] == 1:
        own_spec = pl.BlockSpec((tm, cb), lambda i, br: (br[0] * (rb // tm) + i, 0))
    else:
        own_spec = pl.BlockSpec((tm, cb), lambda i, br: (i, br[0]))

    def body(b_ref, o_ref, r0_ref, r1_ref, r2_ref, s_ref):
        del b_ref
        s_ref[...] = ((o_ref[...].astype(F32) + r0_ref[...].astype(F32))
                      + (r1_ref[...].astype(F32) + r2_ref[...].astype(F32))).astype(BF16)

    def got(j):
        return pl.BlockSpec((None, tm, cb), lambda i, br: (j, i, 0))

    return pl.pallas_call(
        body, name=name,
        grid_spec=pltpu.PrefetchScalarGridSpec(
            num_scalar_prefetch=1, grid=(rb // tm,),
            in_specs=[own_spec, got(0), got(1), got(2)],
            out_specs=pl.BlockSpec((tm, cb), lambda i, br: (i, 0))),
        out_shape=jax.ShapeDtypeStruct((rb, cb), BF16),
        compiler_params=_cp("parallel"),
    )(b1, own, landed, landed, landed)


def _swap_sib(name, sums):
    def body(*refs):
        s_refs, t_refs, send_sems, recv_sems = refs[0:4], refs[4:8], refs[8], refs[9]
        x, y, c, _ = _place()
        cps = [pltpu.make_async_remote_copy(src_ref=s_refs[t], dst_ref=t_refs[t], send_sem=send_sems.at[t],
                                            recv_sem=recv_sems.at[t], device_id=(x, y, 1 - c), device_id_type=MESH)
               for t in range(4)]
        for cp in cps:
            cp.start()
        for cp in cps:
            cp.wait()

    return pl.pallas_call(
        body, name=name,
        in_specs=[ANY] * 4, out_specs=[ANY] * 4,
        out_shape=[jax.ShapeDtypeStruct(s.shape, BF16) for s in sums],
        scratch_shapes=[pltpu.SemaphoreType.DMA((4,)), pltpu.SemaphoreType.DMA((4,))],
    )(*sums)


def _adamw_pair(name, l, s_own, s_sib, w, m, v, outs):
    rb, cb = s_own.shape
    tm = min(256, rb)

    def body(a_ref, b_ref, w_ref, m_ref, v_ref, g0, d0, m0, v0, go_ref, d_ref, mo_ref, vo_ref):
        del g0, d0, m0, v0
        gv = a_ref[...].astype(F32) + b_ref[...].astype(F32)
        go_ref[...], d_ref[...], mo_ref[...], vo_ref[...] = _adamw_math(gv, w_ref[...], m_ref[...], v_ref[...])

    part = pl.BlockSpec((tm, cb), lambda i: (i, 0))
    layer = pl.BlockSpec((None, tm, cb), lambda i: (l, i, 0))
    return pl.pallas_call(
        body, name=name, grid=(rb // tm,),
        in_specs=[part, part, layer, layer, layer] + [ANY] * 4,
        out_specs=[layer] * 4,
        out_shape=[jax.ShapeDtypeStruct((DEPTH, rb, cb), F32)] * 4,
        input_output_aliases={5 + i: i for i in range(4)},
        compiler_params=_cp("parallel"),
    )(s_own, s_sib, w, m, v, *outs)


def _all_gather8(name, v, dep):
    m_per, n = v.shape

    def body(v_ref, dep_ref, out_ref, send_sems, recv_sems, local_sem):
        del dep_ref
        x, y, c, chips = _place()
        me, sib = (x, y, c), (x, y, 1 - c)

        def rows(px, py, pc):
            return out_ref.at[pl.ds((4 * px + 2 * py + pc) * m_per, m_per), :]

        def copy(k, block, to, src=None):
            return pltpu.make_async_remote_copy(
                src_ref=rows(*block) if src is None else src, dst_ref=rows(*block),
                send_sem=send_sems.at[k], recv_sem=recv_sems.at[k], device_id=to, device_id_type=MESH)

        mine = pltpu.make_async_copy(v_ref, rows(*me), local_sem)
        mine.start()
        first = [copy(0, me, sib, src=v_ref)]
        first += [copy(1 + j, me, (*chip, c), src=v_ref) for j, chip in enumerate(chips)]
        for cp in first:
            cp.start()
        passed = [copy(4 + j, (*chip, c), sib) for j, chip in enumerate(chips)]
        for j, chip in enumerate(chips):
            copy(1 + j, (*chip, c), me).wait_recv()
            passed[j].start()
        copy(0, sib, me).wait_recv()
        for j, chip in enumerate(chips):
            copy(4 + j, (*chip, 1 - c), me).wait_recv()
        for cp in first + passed:
            cp.wait_send()
        mine.wait()

    return pl.pallas_call(
        body, name=name,
        out_shape=jax.ShapeDtypeStruct((8 * m_per, n), v.dtype),
        in_specs=[pl.BlockSpec(memory_space=pltpu.VMEM), ANY],
        out_specs=pl.BlockSpec(memory_space=pltpu.VMEM),
        scratch_shapes=[pltpu.SemaphoreType.DMA((7,)), pltpu.SemaphoreType.DMA((7,)), pltpu.SemaphoreType.DMA],
    )(v, dep)


def _sum8(name, g):
    def body(g_ref, o_ref):
        acc = g_ref[0]
        for d in range(1, 8):
            acc = acc + g_ref[d]
        o_ref[...] = acc

    return pl.pallas_call(body, name=name, out_shape=jax.ShapeDtypeStruct(g.shape[1:], F32))(g)


def _pack(parts):
    flat = []
    for a in parts:
        a = a.reshape(-1)
        flat.append(jnp.pad(a, (0, (-a.shape[0]) % 128)))
    cat = jnp.concatenate(flat)
    cat = jnp.pad(cat, (0, (-cat.shape[0]) % 1024))
    return cat.reshape(-1, 128)


def _unpack(packed, shapes):
    flat = packed.reshape(-1)
    out, at = [], 0
    for shp in shapes:
        n = 1
        for d in shp:
            n *= d
        out.append(flat[at:at + n].reshape(shp))
        at += n + (-n) % 128
    return out


def _local_step(x, target, layer_weights, on_grads, small):
    qg_all = jnp.tile(small["q_norm_g"], (1, 8))
    kg_all = jnp.tile(small["k_norm_g"], (1, 8))
    bias_all = _bias_layout(_bias_expand("bias_expand", jnp.pad(small["rel_bias"], ((0, 0), (0, 0), (0, NIDX - 257)))))
    same_group = jnp.eye(4, dtype=F32)[None, :, None, :, None]
    pwbd_all = (small["pool_w"][:, :, :, None, :] * same_group).reshape(DEPTH, PWD, PWD)
    saved = []
    xin = x
    h = _rmsnorm("norm_first", x, small["norm1_g"][0:1])
    for l in range(DEPTH):
        w_in = layer_weights(l, (0,), xin)[0]
        qg, kg, bias = qg_all[l:l + 1], kg_all[l:l + 1], bias_all[l]
        cw, pwbd, ps = small["conv_w"][l], pwbd_all[l], small["pool_scale"][l:l + 1]
        p = _mm_nn(f"proj_in_{l}", h, w_in, l, F32)
        q, qt, kp, kt, vp, vt = _qkv(f"qkv_{l}", p, qg, kg)
        o, lse = _attn_fwd(f"attn_fwd_{l}", kp, qt, vt, bias)
        w_in, w_out, w_1, w_2 = layer_weights(l, (1, 2, 3), o)
        mix = _convpool_fwd(f"convpool_fwd_{l}", p, o, cw, pwbd, ps)
        x1, h2 = _mm_res_norm(f"proj_out_{l}", mix, w_out, l, xin, small["norm2_g"][l:l + 1])
        gnext = small["norm1_g"][(l + 1) % DEPTH][None]
        a, x2, hnext = _mlp_fwd(f"mlp_{l}", h2, w_1, w_2, l, x1, gnext)
        saved.append(dict(xin=xin, h=h, p=p, q=q, qt=qt, kp=kp, kt=kt, vp=vp, bias=bias, mix=mix, x1=x1, h2=h2, a=a, lse=lse,
                          qg=qg, kg=kg, cw=cw, pwbd=pwbd, ps=ps))
        xin, h = x2, hnext

    dx, dxb, loss = _loss_grad("loss_grad", xin, target)
    raw = {k: [None] * DEPTH for k in ("dg1", "dqg", "dkg", "db", "dw0", "dw1", "dw2", "dpw", "dps", "dg2")}
    for l in reversed(range(DEPTH)):
        sv = saved[l]
        da = _mm_nt_relu(f"mlp2_bwd_{l}", dxb, w_2, l, sv["a"])
        g_2 = _mm_tn(f"mlp2_wgrad_{l}", sv["a"], dxb, 512, 1024, relu2=True)
        g_1 = _mm_tn(f"mlp1_wgrad_{l}", sv["h2"], da, 1024, 512)
        dep = on_grads(l, (2, 3), (g_1, g_2))
        dx1, dx1b, dg2 = _mm_nt_normbwd(f"mlp1_bwd_{l}", da, w_1, l, sv["x1"], small["norm2_g"][l:l + 1], dx, dep)
        do, dot, dmix, dl = _proj_out_bwd(f"proj_out_bwd_{l}", dx1b, w_out, l, sv["mix"])
        g_out = _mm_tn(f"proj_out_wgrad_{l}", sv["mix"], dx1b, 512, 1024)
        dcp, dw0, dw1, dw2, dps, dpw = _convpool_bwd(f"convpool_bwd_{l}", sv["p"], dmix, sv["cw"], sv["pwbd"], sv["ps"])
        dq, dkp, dvp, db = _attn_bwd(f"attn_bwd_{l}", sv["q"], sv["qt"], sv["kp"], sv["kt"], sv["vp"], sv["bias"],
                                     do, dot, sv["lse"], _rowsum_layout(dl, x.shape[0] // UNIT))
        dp, dqg, dkg = _qkv_bwd(f"qkv_bwd_{l}", sv["p"], dq, dkp, dvp, dcp, sv["qg"], sv["kg"])
        g_in = _mm_tn(f"proj_in_wgrad_{l}", sv["h"], dp, 1024, 640)
        dep = on_grads(l, (0, 1), (g_in, g_out))
        dx, dxb, dg1 = _mm_nt_normbwd(f"proj_in_bwd_{l}", dp, w_in, l, sv["xin"], small["norm1_g"][l:l + 1], dx1, dep)
        for k, val in dict(dg1=dg1, dqg=dqg, dkg=dkg, db=db, dw0=dw0, dw1=dw1, dw2=dw2, dpw=dpw, dps=dps, dg2=dg2).items():
            raw[k][l] = val
    cat = {k: jnp.concatenate(v, axis=0) for k, v in raw.items() if k not in ("db", "dpw")}
    drb = _bias_reduce("bias_reduce", _bias_unlayout(jnp.stack(raw["db"])))
    dpw = jnp.stack(raw["dpw"])
    gsmall = {
        "norm1_g": cat["dg1"], "q_norm_g": cat["dqg"][:, :HD], "k_norm_g": cat["dkg"][:, :HD],
        "rel_bias": drb[:, :, :257],
        "conv_w": jnp.stack([cat["dw0"], cat["dw1"], cat["dw2"]], axis=1),
        "pool_w": jnp.stack([dpw[:, g * 64:(g + 1) * 64, g * 64:(g + 1) * 64] for g in range(4)], axis=1),
        "pool_scale": cat["dps"], "norm2_g": cat["dg2"],
    }
    return loss, dx, gsmall


SMALL = ("norm1_g", "q_norm_g", "k_norm_g", "rel_bias", "conv_w", "pool_w", "pool_scale", "norm2_g")
LARGE = ("w_in", "w_out", "w_mlp1", "w_mlp2")


def kernel(x, norm1_g, w_in, q_norm_g, k_norm_g, rel_bias, conv_w, pool_w, pool_scale, w_out, norm2_g, w_mlp1, w_mlp2, loss_target, m_norm1_g, m_w_in, m_q_norm_g, m_k_norm_g, m_rel_bias, m_conv_w, m_pool_w, m_pool_scale, m_w_out, m_norm2_g, m_w_mlp1, m_w_mlp2, v_norm1_g, v_w_in, v_q_norm_g, v_k_norm_g, v_rel_bias, v_conv_w, v_pool_w, v_pool_scale, v_w_out, v_norm2_g, v_w_mlp1, v_w_mlp2):
    w = dict(norm1_g=norm1_g, w_in=w_in, q_norm_g=q_norm_g, k_norm_g=k_norm_g, rel_bias=rel_bias, conv_w=conv_w,
             pool_w=pool_w, pool_scale=pool_scale, w_out=w_out, norm2_g=norm2_g, w_mlp1=w_mlp1, w_mlp2=w_mlp2)
    m = dict(norm1_g=m_norm1_g, w_in=m_w_in, q_norm_g=m_q_norm_g, k_norm_g=m_k_norm_g, rel_bias=m_rel_bias,
             conv_w=m_conv_w, pool_w=m_pool_w, pool_scale=m_pool_scale, w_out=m_w_out, norm2_g=m_norm2_g,
             w_mlp1=m_w_mlp1, w_mlp2=m_w_mlp2)
    v = dict(norm1_g=v_norm1_g, w_in=v_w_in, q_norm_g=v_q_norm_g, k_norm_g=v_k_norm_g, rel_bias=v_rel_bias,
             conv_w=v_conv_w, pool_w=v_pool_w, pool_scale=v_pool_scale, w_out=v_w_out, norm2_g=v_norm2_g,
             w_mlp1=v_w_mlp1, w_mlp2=v_w_mlp2)
    ax, ay, ac = lax.axis_index("x"), lax.axis_index("y"), lax.axis_index("c")
    b1 = jnp.reshape(2 * ax + ay, (1,)).astype(jnp.int32)

    cw_rows = _all_gather8("gather_conv_w", jnp.pad(conv_w.reshape(DEPTH * 3, 64), ((0, 4), (0, 64))), b1)
    cw_chips = [cw_rows[(4 * cx + 2 * cy) * 16:(4 * cx + 2 * cy) * 16 + 12, :64] for cx in range(2) for cy in range(2)]
    small = {n: w[n] for n in SMALL}
    small["conv_w"] = jnp.concatenate(cw_chips, axis=1).reshape(DEPTH, 3, CW)

    (w_in_full,), in_sems, in_token = _gather_start(
        "gather_start_in", (0,), (0,), [_cast_into_full("cast_w_in", 0, w["w_in"], b1, cw_rows)])
    others, first_sems, first_token = _gather_start(
        "gather_start_first", (0,), (1, 2, 3),
        [_cast_into_full(f"cast_{LARGE[t]}", t, w[LARGE[t]], b1, in_token) for t in (1, 2, 3)])
    held = [[w_in_full] + list(others)]
    sems = {(0, 0): in_sems[0], (0, 1): first_sems[0]}

    def layer_weights(l, ts, after):
        if l > 0:
            ts = (0, 1, 2, 3) if ts == (0,) else ()
        if ts:
            tag = f"{l}_{ts[0]}"
            first_in = l == 0 and ts == (0,)
            after = first_token if first_in else after
            arrived = _gather_wait(f"gather_wait_{tag}", l, ts, held[0], sems[l, ts[0] if l == 0 else 0], after)
            if first_in:
                arrived, rest_sems, _ = _gather_start("gather_start_rest", tuple(range(1, DEPTH)), (0, 1, 2, 3),
                                                      arrived)
                sems.update({(k, 0): v for k, v in rest_sems.items()})
            held[0] = _pass_on(f"pass_on_{tag}", l, ts, arrived)
        return held[0]

    flights = {}

    def await_flight(l, ts, afters):
        g, landing, sm, _ = flights[l, ts]
        flights[l, ts] = _reduce_wait(f"reduce_wait_{l}_{ts[0]}", ts, g, landing, sm, afters)

    def on_grads(l, ts, grads):
        if ts == (0, 1) and l + 1 < DEPTH:
            await_flight(l + 1, (2, 3), [grads[0]])
            await_flight(l + 1, (0, 1), [grads[0]])
        flights[l, ts] = _reduce_start(f"reduce_start_{l}_{ts[0]}", ts, grads)
        return flights[l, ts][3]

    loss_part, grad_x, gsmall = _local_step(x[0], loss_target[0], layer_weights, on_grads, small)
    loss = lax.psum(loss_part[0, 0], ("x", "y", "c"))
    order = [n for n in SMALL]
    packed = _pack([gsmall[n] for n in order])

    out = {n: [lax.empty(w[n].shape, F32) for _ in range(4)] for n in LARGE}
    for l in reversed(range(DEPTH)):
        if l == 0:
            afters = [grad_x, packed] + [out[n][0] for n in LARGE]
            await_flight(0, (2, 3), afters)
            await_flight(0, (0, 1), afters)
        sums = [None] * 4
        for ts in ((0, 1), (2, 3)):
            g, landing = flights[l, ts]
            for i, t in enumerate(ts):
                sums[t] = _add4(f"add4_{LARGE[t]}_{l}", t, g[i], landing[i], b1)
        theirs = _swap_sib(f"swap_sib_{l}", sums)
        for t, n in enumerate(LARGE):
            out[n] = _adamw_pair(f"adamw_{n}_{l}", l, sums[t], theirs[t], w[n], m[n], v[n], out[n])

    rows = packed.shape[0]
    summed = _sum8("sum_small", _all_gather8("gather_small", packed, out[LARGE[0]][0]).reshape(8, rows, 128))
    gfull = dict(zip(order, _unpack(summed, [gsmall[n].shape for n in order])))
    gfull["conv_w"] = lax.dynamic_slice_in_dim(gfull["conv_w"], (2 * ax + ay) * 64, 64, axis=2)
    res = _adamw("adamw_small", _pack([gfull[n] for n in order]), _pack([w[n] for n in order]),
                 _pack([m[n] for n in order]), _pack([v[n] for n in order]))
    for n, parts in zip(order, zip(*[_unpack(r, [w[k].shape for k in order]) for r in res])):
        out[n] = list(parts)

    names = ("norm1_g", "w_in", "q_norm_g", "k_norm_g", "rel_bias", "conv_w", "pool_w", "pool_scale", "w_out",
             "norm2_g", "w_mlp1", "w_mlp2")
    flat = [loss, grad_x[None]]
    for i in range(4):
        flat += [out[n][i] for n in names]
    return tuple(flat)
```

```python
import functools

import jax
import jax.numpy as jnp
from jax import lax
from jax.experimental import pallas as pl
from jax.experimental.pallas import tpu as pltpu

F32 = jnp.float32
BF16 = jnp.bfloat16

D = 1024
DEPTH = 4
CH = 64
NPREV = 8
KB = (NPREV + 1) * CH
PADR = NPREV * CH
HD = 64
AW = 512
CW = 256
PWD = 256
DIN = 3 * AW + 3 * CW + PWD
DFF = 4 * D
NIDX = 384
EPS = 1e-6
NEG_INF = -1e30

ADAM_LR = 0.001
ADAM_B1 = 0.9
ADAM_B2 = 0.999
ADAM_EPS = 1e-08
ADAM_WD = 0.01
ADAM_STEP = 10

VMEM_LIMIT = 52 * 1024 * 1024
MM_ROWS = 512
MESH = pl.DeviceIdType.MESH
ANY = pl.BlockSpec(memory_space=pl.ANY)


def _cp(*sem):
    return pltpu.CompilerParams(dimension_semantics=sem, vmem_limit_bytes=VMEM_LIMIT)


def _inv_rms(x):
    return lax.rsqrt(jnp.mean(x * x, axis=-1, keepdims=True) + EPS)


def _head_mean_matrix():
    r = lax.broadcasted_iota(jnp.int32, (AW, 128), 0) // HD
    c = lax.broadcasted_iota(jnp.int32, (AW, 128), 1)
    back_r = lax.broadcasted_iota(jnp.int32, (128, AW), 0)
    back_c = lax.broadcasted_iota(jnp.int32, (128, AW), 1) // HD
    return (jnp.where(r == c, 1.0 / HD, 0.0).astype(BF16), jnp.where(back_r == back_c, 1.0, 0.0).astype(BF16))


def _two_pass_dot(x, m):
    hi = x.astype(BF16)
    lo = (x - hi.astype(F32)).astype(BF16)
    return (jnp.dot(hi, m, preferred_element_type=F32)
            + jnp.dot(lo, m, preferred_element_type=F32))


def _head_mean(x, hm):
    return _two_pass_dot(_two_pass_dot(x, hm[0]), hm[1])


def _rmsnorm(name, x, g):
    s = x.shape[0]
    tm = 512

    def body(x_ref, g_ref, h_ref):
        xv = x_ref[...]
        h_ref[...] = (xv * _inv_rms(xv) * g_ref[...]).astype(BF16)

    return pl.pallas_call(
        body, name=name, grid=(s // tm,),
        in_specs=[pl.BlockSpec((tm, D), lambda i: (i, 0)), pl.BlockSpec((1, D), lambda i: (0, 0))],
        out_specs=pl.BlockSpec((tm, D), lambda i: (i, 0)),
        out_shape=jax.ShapeDtypeStruct((s, D), BF16),
        compiler_params=_cp("parallel"),
    )(x, g)


def _relu2(a):
    r = jnp.maximum(a, jnp.zeros_like(a))
    return r * r


def _mm_nn(name, a, w, l, out_dtype):
    s, k = a.shape
    n = w.shape[2]
    tm = MM_ROWS

    def body(a_ref, w_ref, o_ref):
        o_ref[...] = jnp.dot(a_ref[...], w_ref[...], preferred_element_type=F32).astype(o_ref.dtype)

    return pl.pallas_call(
        body, name=name, grid=(s // tm,),
        in_specs=[pl.BlockSpec((tm, k), lambda i: (i, 0)),
                  pl.BlockSpec((None, k, n), lambda i: (l, 0, 0))],
        out_specs=pl.BlockSpec((tm, n), lambda i: (i, 0)),
        out_shape=jax.ShapeDtypeStruct((s, n), out_dtype),
        compiler_params=_cp("parallel"),
    )(a, w)


def _mm_res_norm(name, a, w, l, res, g):
    s, k = a.shape
    tm = MM_ROWS

    def body(a_ref, w_ref, r_ref, g_ref, x_ref, h_ref):
        acc = r_ref[...] + jnp.dot(a_ref[...], w_ref[...], preferred_element_type=F32)
        x_ref[...] = acc
        h_ref[...] = (acc * _inv_rms(acc) * g_ref[...]).astype(BF16)

    return pl.pallas_call(
        body, name=name, grid=(s // tm,),
        in_specs=[pl.BlockSpec((tm, k), lambda i: (i, 0)),
                  pl.BlockSpec((None, k, D), lambda i: (l, 0, 0)),
                  pl.BlockSpec((tm, D), lambda i: (i, 0)),
                  pl.BlockSpec((1, D), lambda i: (0, 0))],
        out_specs=[pl.BlockSpec((tm, D), lambda i: (i, 0))] * 2,
        out_shape=[jax.ShapeDtypeStruct((s, D), F32), jax.ShapeDtypeStruct((s, D), BF16)],
        compiler_params=_cp("parallel"),
    )(a, w, res, g)


def _mlp_fwd(name, h2, w1, w2, l, res, g):
    s = h2.shape[0]
    tm = 256

    def body(h_ref, w1_ref, w2_ref, r_ref, g_ref, a_ref, x_ref, hn_ref):
        a = jnp.dot(h_ref[...], w1_ref[...], preferred_element_type=F32).astype(BF16)
        a_ref[...] = a
        acc = r_ref[...] + jnp.dot(_relu2(a), w2_ref[...], preferred_element_type=F32)
        x_ref[...] = acc
        hn_ref[...] = (acc * _inv_rms(acc) * g_ref[...]).astype(BF16)

    once = pl.Buffered(1)
    rows = pl.BlockSpec((tm, D), lambda i: (i, 0))
    return pl.pallas_call(
        body, name=name, grid=(s // tm,),
        in_specs=[rows,
                  pl.BlockSpec((None, D, DFF), lambda i: (l, 0, 0), pipeline_mode=once),
                  pl.BlockSpec((None, DFF, D), lambda i: (l, 0, 0), pipeline_mode=once),
                  rows, pl.BlockSpec((1, D), lambda i: (0, 0))],
        out_specs=[pl.BlockSpec((tm, DFF), lambda i: (i, 0)), rows, rows],
        out_shape=[jax.ShapeDtypeStruct((s, DFF), BF16), jax.ShapeDtypeStruct((s, D), F32),
                   jax.ShapeDtypeStruct((s, D), BF16)],
        compiler_params=_cp("parallel"),
    )(h2, w1, w2, res, g)


def _qkv(name, p, qg, kg):
    s = p.shape[0]
    tm = PADR
    nb = s // tm

    def body(pq_ref, pk_ref, pv_ref, qg_ref, kg_ref, q_ref, qt_ref, k_ref, kt_ref, v_ref, vt_ref):
        t = pl.program_id(0)
        hm = _head_mean_matrix()

        def nrm(x, g):
            return x * lax.rsqrt(_head_mean(x * x, hm) + EPS) * g

        first = t == 0
        qq = nrm(pq_ref[...], qg_ref[...]) * 0.125
        kk = jnp.where(first, 0.0, nrm(pk_ref[...], kg_ref[...]))
        vv = jnp.where(first, 0.0, pv_ref[...])
        q_ref[...] = qq.astype(BF16)
        qt_ref[...] = qq.T.astype(BF16)
        k_ref[...] = kk.astype(BF16)
        kt_ref[...] = kk.T.astype(BF16)
        v_ref[...] = vv.astype(BF16)
        vt_ref[...] = vv.T.astype(BF16)

    def src(col):
        return pl.BlockSpec((tm, AW), lambda t: (jnp.maximum(t - 1, 0), col))

    gspec = pl.BlockSpec((1, AW), lambda t: (0, 0))
    rows = pl.BlockSpec((tm, AW), lambda t: (t, 0))
    cols = pl.BlockSpec((AW, tm), lambda t: (0, t))
    return pl.pallas_call(
        body, name=name, grid=(nb + 1,),
        in_specs=[src(0), src(1), src(2), gspec, gspec],
        out_specs=[pl.BlockSpec((tm, AW), lambda t: (jnp.maximum(t - 1, 0), 0)),
                   pl.BlockSpec((AW, tm), lambda t: (0, jnp.maximum(t - 1, 0))),
                   rows, cols, rows, cols],
        out_shape=[jax.ShapeDtypeStruct((s, AW), BF16), jax.ShapeDtypeStruct((AW, s), BF16),
                   jax.ShapeDtypeStruct((s + PADR, AW), BF16), jax.ShapeDtypeStruct((AW, s + PADR), BF16),
                   jax.ShapeDtypeStruct((s + PADR, AW), BF16), jax.ShapeDtypeStruct((AW, s + PADR), BF16)],
        compiler_params=_cp("arbitrary"),
    )(p, p, p, qg, kg)


NBAND = KB // CH
HIGHEST = lax.Precision.HIGHEST
NT_DIMS = (((1,), (1,)), ((), ()))


def _onehot_table(a):
    m = lax.broadcasted_iota(jnp.int32, (128, NIDX), 0)
    idx = lax.broadcasted_iota(jnp.int32, (128, NIDX), 1)
    rel = jnp.clip(KB - 1 - (CH * a + m), -128, 128) + 128
    return jnp.where(rel == idx, 1.0, 0.0).astype(F32)


def _onehot_diagonal():
    r = lax.broadcasted_iota(jnp.int32, (CH * CH, 128), 0)
    m = lax.broadcasted_iota(jnp.int32, (CH * CH, 128), 1)
    return jnp.where((r % CH) - (r // CH) + (CH - 1) == m, 1.0, 0.0).astype(F32)


def _bias_expand(name, rb):
    def body(rb_ref, o_ref):
        along = [lax.dot_general(rb_ref[...], _onehot_table(a), NT_DIMS, preferred_element_type=F32,
                                 precision=HIGHEST) for a in range(NBAND)]
        o_ref[...] = lax.dot_general(jnp.concatenate(along, axis=0), _onehot_diagonal(), NT_DIMS,
                                     preferred_element_type=F32, precision=HIGHEST)

    return pl.pallas_call(
        body, name=name, grid=(DEPTH,),
        in_specs=[pl.BlockSpec((None, 8, NIDX), lambda l: (l, 0, 0))],
        out_specs=pl.BlockSpec((None, NBAND * 8, CH * CH), lambda l: (l, 0, 0)),
        out_shape=jax.ShapeDtypeStruct((DEPTH, NBAND * 8, CH * CH), F32),
        compiler_params=_cp("parallel"),
    )(rb)


def _bias_reduce(name, db):
    def body(db_ref, o_ref):
        along = jnp.dot(db_ref[...], _onehot_diagonal(), preferred_element_type=F32, precision=HIGHEST)
        acc = jnp.zeros((8, NIDX), F32)
        for a in range(NBAND):
            acc = acc + jnp.dot(along[8 * a:8 * a + 8, :], _onehot_table(a), preferred_element_type=F32,
                                precision=HIGHEST)
        o_ref[...] = acc

    return pl.pallas_call(
        body, name=name, grid=(DEPTH,),
        in_specs=[pl.BlockSpec((None, NBAND * 8, CH * CH), lambda l: (l, 0, 0))],
        out_specs=pl.BlockSpec((None, 8, NIDX), lambda l: (l, 0, 0)),
        out_shape=jax.ShapeDtypeStruct((DEPTH, 8, NIDX), F32),
        compiler_params=_cp("parallel"),
    )(db)


def _bias_layout(flat):
    b = flat.reshape(DEPTH, NBAND, 8, CH, CH).transpose(0, 2, 1, 4, 3).reshape(DEPTH, 4, 2, KB, CH)
    pair = b.transpose(0, 1, 3, 2, 4).reshape(DEPTH, 4, KB, 128)
    first = jnp.pad(pair, ((0, 0), (0, 0), (0, CH), (0, 0)), constant_values=NEG_INF)
    second = jnp.pad(pair, ((0, 0), (0, 0), (CH, 0), (0, 0)), constant_values=NEG_INF)
    return jnp.concatenate([first, second], axis=3)


def _bias_unlayout(dbt):
    b = dbt.reshape(DEPTH, 4, NBAND, CH, 2, CH)
    return b.transpose(0, 2, 1, 4, 5, 3).reshape(DEPTH, NBAND * 8, CH * CH)


UNIT = 2 * CH
BAND2 = KB + CH


def _pair_weights(xt):
    x = xt.astype(F32)
    row = lax.broadcasted_iota(jnp.int32, (128, UNIT), 0)
    low = lax.broadcasted_iota(jnp.int32, (128, UNIT), 1) < HD
    swapped = pltpu.roll(x, HD, 1)
    same = (row < HD) == low
    first = jnp.where(same, jnp.where(low, x, swapped), 0.0)
    second = jnp.where(same, jnp.where(low, swapped, x), 0.0)
    return jnp.concatenate([first, second], axis=1).astype(BF16)


def _pair_rows(x):
    low = lax.broadcasted_iota(jnp.int32, (CH, 128), 1) < HD
    zero = jnp.zeros((CH, 128), x.dtype)
    parts = []
    for c in range(2):
        xc = x[c * CH:(c + 1) * CH, :]
        parts += [jnp.where(low, xc, zero), jnp.where(low, zero, xc)]
    return jnp.concatenate(parts, axis=0)


def _unpair(raw):
    b0, b1 = raw[:, 0:128], raw[:, 128:256]
    row = lax.broadcasted_iota(jnp.int32, (128, 128), 0)
    low = lax.broadcasted_iota(jnp.int32, (128, 128), 1) < HD
    top = jnp.where(low, b0, pltpu.roll(b1, HD, 1))
    bottom = jnp.where(low, pltpu.roll(b0, HD, 1), b1)
    return jnp.where(row < HD, top, bottom).T


def _scores_t(kb, qw, bias2, row0, padded):
    s = jnp.dot(kb, qw, preferred_element_type=F32) + bias2
    if padded:
        s = jnp.where(row0 + lax.broadcasted_iota(jnp.int32, (BAND2, 256), 0) >= PADR, s, NEG_INF)
    return s


def _unit_loops(s, unit):
    lax.fori_loop(0, PADR // UNIT, lambda u, c: unit(u, True, c), 0, unroll=2)
    lax.fori_loop(PADR // UNIT, s // UNIT, lambda u, c: unit(u, False, c), 0, unroll=7)


def _attn_fwd(name, kp, qt, vt, bias2):
    s = qt.shape[1]
    nu = s // UNIT

    def body(k_ref, qt_ref, vt_ref, b_ref, o_ref, lse_ref):
        def unit(u, padded, carry):
            r0 = pl.multiple_of(u * UNIT, UNIT)
            sc = _scores_t(k_ref[pl.ds(r0, BAND2), :], _pair_weights(qt_ref[:, pl.ds(r0, UNIT)]), b_ref[...],
                           r0, padded)
            top = jnp.max(sc, axis=0, keepdims=True)
            e = jnp.exp(sc - top)
            total = jnp.sum(e, axis=0, keepdims=True)
            raw = jnp.dot(vt_ref[:, pl.ds(r0, BAND2)], e.astype(BF16), preferred_element_type=F32) * (1.0 / total)
            o_ref[pl.ds(r0, UNIT), :] = _unpair(raw).astype(BF16)
            lse_ref[u] = jnp.broadcast_to(top + jnp.log(total), (8, 256))
            return carry

        _unit_loops(s, unit)

    return pl.pallas_call(
        body, name=name, grid=(AW // 128,),
        in_specs=[pl.BlockSpec((s + PADR, 128), lambda h: (0, h)),
                  pl.BlockSpec((128, s), lambda h: (h, 0)),
                  pl.BlockSpec((128, s + PADR), lambda h: (h, 0)),
                  pl.BlockSpec((None, BAND2, 256), lambda h: (h, 0, 0))],
        out_specs=[pl.BlockSpec((s, 128), lambda h: (0, h)),
                   pl.BlockSpec((None, nu, 8, 256), lambda h: (h, 0, 0, 0))],
        out_shape=[jax.ShapeDtypeStruct((s, AW), BF16), jax.ShapeDtypeStruct((4, nu, 8, 256), F32)],
        compiler_params=_cp("parallel"),
    )(kp, qt, vt, bias2)


def _attn_bwd(name, q, qt, kp, kt, vp, bias2, do, dot, lse, dl):
    s = q.shape[0]
    nu = s // UNIT

    def body(q_ref, qt_ref, k_ref, kt_ref, v_ref, b_ref, do_ref, dot_ref, lse_ref, dl_ref,
             dq_ref, dk_ref, dv_ref, db_ref):
        dk_ref[...] = jnp.zeros_like(dk_ref)
        dv_ref[...] = jnp.zeros_like(dv_ref)
        db_ref[...] = jnp.zeros_like(db_ref)

        def unit(u, padded, carry):
            r0 = pl.multiple_of(u * UNIT, UNIT)
            rows, band = pl.ds(r0, UNIT), pl.ds(r0, BAND2)
            sc = _scores_t(k_ref[band, :], _pair_weights(qt_ref[:, rows]), b_ref[...], r0, padded)
            pt = jnp.exp(sc - lse_ref[u][0:1, :])
            dpt = jnp.dot(v_ref[band, :], _pair_weights(dot_ref[:, rows]), preferred_element_type=F32)
            ds = pt * (dpt - dl_ref[u][0:1, :])
            db_ref[...] += ds[0:KB, 0:128] + ds[CH:BAND2, 128:256]
            dsb = ds.astype(BF16)
            dq_ref[rows, :] = _unpair(jnp.dot(kt_ref[:, band], dsb, preferred_element_type=F32))
            dk_ref[band, :] += jnp.dot(dsb, _pair_rows(q_ref[rows, :]), preferred_element_type=F32)
            dv_ref[band, :] += jnp.dot(pt.astype(BF16), _pair_rows(do_ref[rows, :]), preferred_element_type=F32)
            return carry

        _unit_loops(s, unit)

    row_q = pl.BlockSpec((s, 128), lambda h: (0, h))
    col_q = pl.BlockSpec((128, s), lambda h: (h, 0))
    row_k = pl.BlockSpec((s + PADR, 128), lambda h: (0, h))
    col_k = pl.BlockSpec((128, s + PADR), lambda h: (h, 0))
    stat = pl.BlockSpec((None, nu, 8, 256), lambda h: (h, 0, 0, 0))
    return pl.pallas_call(
        body, name=name, grid=(AW // 128,),
        in_specs=[row_q, col_q, row_k, col_k, row_k,
                  pl.BlockSpec((None, BAND2, 256), lambda h: (h, 0, 0)), row_q, col_q, stat, stat],
        out_specs=[row_q, row_k, row_k, pl.BlockSpec((None, KB, 128), lambda h: (h, 0, 0))],
        out_shape=[jax.ShapeDtypeStruct((s, AW), F32),
                   jax.ShapeDtypeStruct((s + PADR, AW), F32),
                   jax.ShapeDtypeStruct((s + PADR, AW), F32),
                   jax.ShapeDtypeStruct((4, KB, 128), F32)],
        compiler_params=_cp("parallel"),
    )(q, qt, kp, kt, vp, bias2, do, dot, lse, dl)


def _rowsum_layout(dl, nu):
    d = dl[:, :8].reshape(nu, 2, CH, 4, 2)
    d = d.transpose(3, 0, 1, 4, 2).reshape(4, nu, 1, 256)
    return jnp.broadcast_to(d, (4, nu, 8, 256))


def _rows_before(cur, prev, k):
    row = lax.broadcasted_iota(jnp.int32, cur.shape, 0)
    return jnp.where(row >= k, pltpu.roll(cur, k, 0), pltpu.roll(prev, k, 0))


def _rows_after(cur, nxt, k):
    n = cur.shape[0]
    row = lax.broadcasted_iota(jnp.int32, cur.shape, 0)
    return jnp.where(row < n - k, pltpu.roll(cur, n - k, 0), pltpu.roll(nxt, n - k, 0))


def _pool_window_lanes():
    lg = lax.broadcasted_iota(jnp.int32, (1, PWD), 1) // 64
    return lg, jnp.where(lg == 0, 2.0, jnp.where(lg == 1, 4.0, jnp.where(lg == 2, 8.0, 16.0))).astype(F32)


def _pool_mean_minus_token(u, up, row0):
    lg, wv = _pool_window_lanes()
    sums = []
    c, p = u, up
    for k in (1, 2, 4, 8):
        c2 = c + _rows_before(c, p, k)
        p = p + pltpu.roll(p, k, 0)
        c = c2
        sums.append(c)
    win = jnp.where(lg == 0, sums[0], jnp.where(lg == 1, sums[1], jnp.where(lg == 2, sums[2], sums[3])))
    pos1 = (row0 + lax.broadcasted_iota(jnp.int32, u.shape, 0) + 1).astype(F32)
    cnt = jnp.minimum(pos1, wv)
    return win / cnt - u, cnt


def _conv_taps(z, zp, w0, w1, w2):
    z1 = _rows_before(z, zp, 1)
    z2 = _rows_before(z, zp, 2)
    return (w0 * z2 + w1 * z1) + w2 * z, z1, z2


CP_TM = 512


def _convpool_fwd(name, p, o, cw, pwbd, ps):
    s = p.shape[0]
    tm = CP_TM
    nb = s // tm

    def body(gb_ref, gc_ref, hin_ref, u_ref, gcp_ref, hinp_ref, up_ref, o_ref, cw_ref, pw_ref, ps_ref, mix_ref):
        i = pl.program_id(0)
        has_prev = i > 0
        z = gc_ref[...] * hin_ref[...]
        zp = jnp.where(has_prev, gcp_ref[...] * hinp_ref[...], 0.0)
        y3, _, _ = _conv_taps(z, zp, cw_ref[0:1, :], cw_ref[1:2, :], cw_ref[2:3, :])
        m, _ = _pool_mean_minus_token(u_ref[...], jnp.where(has_prev, up_ref[...], 0.0), i * tm)
        yp = jnp.dot(m.astype(BF16), pw_ref[...].astype(BF16), preferred_element_type=F32) * ps_ref[...]
        mix_ref[:, 0:AW] = o_ref[...]
        mix_ref[:, AW:AW + CW] = (gb_ref[...] * y3).astype(BF16)
        mix_ref[:, AW + CW:D] = yp.astype(BF16)

    def cur(col):
        return pl.BlockSpec((tm, CW), lambda i: (i, col))

    def prev(col):
        return pl.BlockSpec((tm, CW), lambda i: (jnp.maximum(i - 1, 0), col))

    def whole(a):
        return pl.BlockSpec(a.shape, lambda i: (0,) * a.ndim)

    return pl.pallas_call(
        body, name=name, grid=(nb,),
        in_specs=[cur(6), cur(7), cur(8), cur(9), prev(7), prev(8), prev(9),
                  pl.BlockSpec((tm, AW), lambda i: (i, 0)), whole(cw), whole(pwbd), whole(ps)],
        out_specs=pl.BlockSpec((tm, D), lambda i: (i, 0)),
        out_shape=jax.ShapeDtypeStruct((s, D), BF16),
        compiler_params=_cp("parallel"),
    )(p, p, p, p, p, p, p, o, cw, pwbd, ps)


def _convpool_bwd(name, p, dmix, cw, pwbd, ps):
    s = p.shape[0]
    tm = CP_TM
    nb = s // tm

    def body(gb_ref, gc_ref, hin_ref, u_ref, gcp_ref, hinp_ref, up_ref, gbn_ref, dyc_ref, dyp_ref, dycn_ref, dypn_ref,
             cw_ref, pw_ref, ps_ref, dcp_ref, dw0_ref, dw1_ref, dw2_ref, dps_ref, dpw_ref):
        i = pl.program_id(0)
        has_prev = i > 0
        has_next = i < nb - 1
        w0, w1, w2 = cw_ref[0:1, :], cw_ref[1:2, :], cw_ref[2:3, :]
        gb, gc, hin = gb_ref[...], gc_ref[...], hin_ref[...]
        dyc = dyc_ref[...]
        z = gc * hin
        zp = jnp.where(has_prev, gcp_ref[...] * hinp_ref[...], 0.0)
        y3, z1, z2 = _conv_taps(z, zp, w0, w1, w2)
        dy3 = dyc * gb
        dy3n = jnp.where(has_next, dycn_ref[...] * gbn_ref[...], 0.0)
        dz = w2 * dy3 + w1 * _rows_after(dy3, dy3n, 1) + w0 * _rows_after(dy3, dy3n, 2)
        pw = pw_ref[...].astype(BF16)
        psv = ps_ref[...]
        m, cnt = _pool_mean_minus_token(u_ref[...], jnp.where(has_prev, up_ref[...], 0.0), i * tm)
        mb = m.astype(BF16)
        dyp = dyp_ref[...]
        dmp = (dyp * psv).astype(BF16)
        dmpn = jnp.where(has_next, dypn_ref[...] * psv, 0.0).astype(BF16)
        nt = (((1,), (1,)), ((), ()))
        dm = lax.dot_general(dmp, pw, nt, preferred_element_type=F32)
        dmn = lax.dot_general(dmpn, pw, nt, preferred_element_type=F32)
        lg, wv = _pool_window_lanes()
        cc, cn = dm / cnt, dmn / wv
        sums = []
        for k in (1, 2, 4, 8):
            c2 = cc + _rows_after(cc, cn, k)
            cn = cn + pltpu.roll(cn, tm - k, 0)
            cc = c2
            sums.append(cc)
        du = jnp.where(lg == 0, sums[0], jnp.where(lg == 1, sums[1], jnp.where(lg == 2, sums[2], sums[3]))) - dm
        dcp_ref[:, 0:CW] = (dyc * y3).astype(BF16)
        dcp_ref[:, CW:2 * CW] = (dz * hin).astype(BF16)
        dcp_ref[:, 2 * CW:3 * CW] = (dz * gc).astype(BF16)
        dcp_ref[:, 3 * CW:4 * CW] = du.astype(BF16)
        parts = (jnp.sum(dy3 * z2, axis=0, keepdims=True),
                 jnp.sum(dy3 * z1, axis=0, keepdims=True),
                 jnp.sum(dy3 * z, axis=0, keepdims=True),
                 jnp.sum(dyp * jnp.dot(mb, pw, preferred_element_type=F32), axis=0, keepdims=True),
                 lax.dot_general(mb, dmp, (((0,), (0,)), ((), ())), preferred_element_type=F32))
        accs = (dw0_ref, dw1_ref, dw2_ref, dps_ref, dpw_ref)

        @pl.when(i == 0)
        def _():
            for a, v in zip(accs, parts):
                a[...] = v

        @pl.when(i > 0)
        def _():
            for a, v in zip(accs, parts):
                a[...] += v

    def cur(col):
        return pl.BlockSpec((tm, CW), lambda i: (i, col))

    def prev(col):
        return pl.BlockSpec((tm, CW), lambda i: (jnp.maximum(i - 1, 0), col))

    def nxt(col):
        return pl.BlockSpec((tm, CW), lambda i: (jnp.minimum(i + 1, nb - 1), col))

    def whole(shape):
        return pl.BlockSpec(shape, lambda i: (0,) * len(shape))

    row = jax.ShapeDtypeStruct((1, CW), F32)
    return pl.pallas_call(
        body, name=name, grid=(nb,),
        in_specs=[cur(6), cur(7), cur(8), cur(9), prev(7), prev(8), prev(9), nxt(6),
                  cur(0), cur(1), nxt(0), nxt(1), whole(cw.shape), whole(pwbd.shape), whole(ps.shape)],
        out_specs=[pl.BlockSpec((tm, D), lambda i: (i, 0)), whole((1, CW)), whole((1, CW)), whole((1, CW)),
                   whole((1, PWD)), whole((PWD, PWD))],
        out_shape=[jax.ShapeDtypeStruct((s, D), BF16), row, row, row, row,
                   jax.ShapeDtypeStruct((PWD, PWD), F32)],
        compiler_params=_cp("arbitrary"),
    )(p, p, p, p, p, p, p, p, dmix, dmix, dmix, dmix, cw, pwbd, ps)


def _qkv_bwd(name, p, dq, dkp, dvp, dcp, qg, kg):
    s = p.shape[0]
    tm = 256
    off = PADR // tm

    def body(pq_ref, pk_ref, dq_ref, dk_ref, dv_ref, dcp_ref, qg_ref, kg_ref, dp_ref, dqg_ref, dkg_ref):
        i = pl.program_id(0)
        hm = _head_mean_matrix()

        def nrm_bwd(x, g, dy):
            r = lax.rsqrt(_head_mean(x * x, hm) + EPS)
            xn = x * r
            dxn = dy * g
            dx = r * (dxn - xn * _head_mean(dxn * xn, hm))
            dg = jnp.sum(dy * xn, axis=0, keepdims=True)
            dg = (dg[:, 0:128] + dg[:, 128:256]) + (dg[:, 256:384] + dg[:, 384:512])
            return dx, dg + pltpu.roll(dg, HD, 1)

        dxq, dgq = nrm_bwd(pq_ref[...], qg_ref[...], dq_ref[...] * 0.125)
        dxk, dgk = nrm_bwd(pk_ref[...], kg_ref[...], dk_ref[...])
        dp_ref[:, 0:AW] = dxq.astype(BF16)
        dp_ref[:, AW:2 * AW] = dxk.astype(BF16)
        dp_ref[:, 2 * AW:3 * AW] = dv_ref[...].astype(BF16)
        dp_ref[:, 3 * AW:DIN] = dcp_ref[...]

        @pl.when(i == 0)
        def _():
            dqg_ref[...] = dgq
            dkg_ref[...] = dgk

        @pl.when(i > 0)
        def _():
            dqg_ref[...] += dgq
            dkg_ref[...] += dgk

    gspec = pl.BlockSpec((1, AW), lambda i: (0, 0))
    gout = pl.BlockSpec((1, 128), lambda i: (0, 0))
    return pl.pallas_call(
        body, name=name, grid=(s // tm,),
        in_specs=[pl.BlockSpec((tm, AW), lambda i: (i, 0)), pl.BlockSpec((tm, AW), lambda i: (i, 1)),
                  pl.BlockSpec((tm, AW), lambda i: (i, 0)),
                  pl.BlockSpec((tm, AW), lambda i: (i + off, 0)),
                  pl.BlockSpec((tm, AW), lambda i: (i + off, 0)),
                  pl.BlockSpec((tm, D), lambda i: (i, 0)), gspec, gspec],
        out_specs=[pl.BlockSpec((tm, DIN), lambda i: (i, 0)), gout, gout],
        out_shape=[jax.ShapeDtypeStruct((s, DIN), BF16), jax.ShapeDtypeStruct((1, 128), F32),
                   jax.ShapeDtypeStruct((1, 128), F32)],
        compiler_params=_cp("arbitrary"),
    )(p, p, dq, dkp, dvp, dcp, qg, kg)


def _loss_grad(name, y, t):
    s = y.shape[0]
    tm = 512

    def body(y_ref, t_ref, dy_ref, dyb_ref, l_ref):
        i = pl.program_id(0)
        e = y_ref[...] - t_ref[...]
        dy = e * (1.0 / D)
        dy_ref[...] = dy
        dyb_ref[...] = dy.astype(BF16)
        part = 0.5 * jnp.sum(jnp.mean(e * e, axis=-1, keepdims=True), axis=0, keepdims=True)

        @pl.when(i == 0)
        def _():
            l_ref[...] = part

        @pl.when(i > 0)
        def _():
            l_ref[...] += part

    blk = pl.BlockSpec((tm, D), lambda i: (i, 0))
    return pl.pallas_call(
        body, name=name, grid=(s // tm,),
        in_specs=[blk, blk],
        out_specs=[blk, blk, pl.BlockSpec((1, 1), lambda i: (0, 0))],
        out_shape=[jax.ShapeDtypeStruct((s, D), F32), jax.ShapeDtypeStruct((s, D), BF16),
                   jax.ShapeDtypeStruct((1, 1), F32)],
        compiler_params=_cp("arbitrary"),
    )(y, t)


def _mm_nt_relu(name, dxb, w, l, a):
    s = dxb.shape[0]
    tm = MM_ROWS

    def body(d_ref, w_ref, a_ref, o_ref):
        df = lax.dot_general(d_ref[...], w_ref[...], NT_DIMS, preferred_element_type=F32)
        o_ref[...] = (df * (2.0 * jnp.maximum(a_ref[...].astype(F32), 0.0))).astype(BF16)

    return pl.pallas_call(
        body, name=name, grid=(s // tm,),
        in_specs=[pl.BlockSpec((tm, D), lambda i: (i, 0)),
                  pl.BlockSpec((None, DFF, D), lambda i: (l, 0, 0)),
                  pl.BlockSpec((tm, DFF), lambda i: (i, 0))],
        out_specs=pl.BlockSpec((tm, DFF), lambda i: (i, 0)),
        out_shape=jax.ShapeDtypeStruct((s, DFF), BF16),
        compiler_params=_cp("parallel"),
    )(dxb, w, a)


def _proj_out_bwd(name, dxb, w, l, mix):
    s = dxb.shape[0]
    tm = 512

    def body(d_ref, w_ref, o_ref, do_ref, dot_ref, dcp_ref, dl_ref):
        d = d_ref[...]
        wa, wc = w_ref[0:AW, :], w_ref[AW:D, :]
        do = lax.dot_general(d, wa, NT_DIMS, preferred_element_type=F32)
        do_ref[...] = do.astype(BF16)
        dot_ref[...] = lax.dot_general(wa, d, NT_DIMS, preferred_element_type=F32).astype(BF16)
        dcp_ref[...] = lax.dot_general(d, wc, NT_DIMS, preferred_element_type=F32)
        head = lax.broadcasted_iota(jnp.int32, (AW, 128), 0) // HD
        pick = jnp.where(head == lax.broadcasted_iota(jnp.int32, (AW, 128), 1), 1.0, 0.0).astype(BF16)
        dl_ref[...] = _two_pass_dot(do * o_ref[...].astype(F32), pick)

    return pl.pallas_call(
        body, name=name, grid=(s // tm,),
        in_specs=[pl.BlockSpec((tm, D), lambda i: (i, 0)),
                  pl.BlockSpec((None, D, D), lambda i: (l, 0, 0)),
                  pl.BlockSpec((tm, AW), lambda i: (i, 0))],
        out_specs=[pl.BlockSpec((tm, AW), lambda i: (i, 0)), pl.BlockSpec((AW, tm), lambda i: (0, i)),
                   pl.BlockSpec((tm, D - AW), lambda i: (i, 0)), pl.BlockSpec((tm, 128), lambda i: (i, 0))],
        out_shape=[jax.ShapeDtypeStruct((s, AW), BF16), jax.ShapeDtypeStruct((AW, s), BF16),
                   jax.ShapeDtypeStruct((s, D - AW), F32), jax.ShapeDtypeStruct((s, 128), F32)],
        compiler_params=_cp("parallel"),
    )(dxb, w, mix)


def _mm_nt_normbwd(name, gy, w, l, x, g, dres, dep):
    s, k = gy.shape
    tm = MM_ROWS

    def body(gy_ref, w_ref, x_ref, g_ref, dr_ref, dep_ref, dx_ref, dxb_ref, dg_ref):
        del dep_ref
        i = pl.program_id(0)
        dh = lax.dot_general(gy_ref[...], w_ref[...], NT_DIMS, preferred_element_type=F32)
        xv = x_ref[...]
        r = _inv_rms(xv)
        xn = xv * r
        dxn = dh * g_ref[...]
        dx = r * (dxn - xn * jnp.mean(dxn * xn, axis=-1, keepdims=True)) + dr_ref[...]
        dx_ref[...] = dx
        dxb_ref[...] = dx.astype(BF16)
        part = jnp.sum(dh * xn, axis=0, keepdims=True)

        @pl.when(i == 0)
        def _():
            dg_ref[...] = part

        @pl.when(i > 0)
        def _():
            dg_ref[...] += part

    blk = pl.BlockSpec((tm, D), lambda i: (i, 0))
    vec = pl.BlockSpec((1, D), lambda i: (0, 0))
    return pl.pallas_call(
        body, name=name, grid=(s // tm,),
        in_specs=[pl.BlockSpec((tm, k), lambda i: (i, 0)),
                  pl.BlockSpec((None, D, k), lambda i: (l, 0, 0)), blk, vec, blk, ANY],
        out_specs=[blk, blk, vec],
        out_shape=[jax.ShapeDtypeStruct((s, D), F32), jax.ShapeDtypeStruct((s, D), BF16),
                   jax.ShapeDtypeStruct((1, D), F32)],
        compiler_params=_cp("arbitrary"),
    )(gy, w, x, g, dres, dep)


def _mm_tn(name, a, b, tma, tnb, relu2=False):
    s, m = a.shape
    n = b.shape[1]

    def body(a_ref, b_ref, o_ref):
        av = _relu2(a_ref[...]) if relu2 else a_ref[...]
        o_ref[...] = lax.dot_general(av, b_ref[...], (((0,), (0,)), ((), ())),
                                     preferred_element_type=F32).astype(BF16)

    return pl.pallas_call(
        body, name=name, grid=(m // tma, n // tnb),
        in_specs=[pl.BlockSpec((s, tma), lambda i, j: (0, i)),
                  pl.BlockSpec((s, tnb), lambda i, j: (0, j))],
        out_specs=pl.BlockSpec((tma, tnb), lambda i, j: (i, j)),
        out_shape=jax.ShapeDtypeStruct((m, n), BF16),
        compiler_params=_cp("parallel", "parallel"),
    )(a, b)


def _adamw_math(gv, wv, mv, vv):
    mn = ADAM_B1 * mv + (1.0 - ADAM_B1) * gv
    vn = ADAM_B2 * vv + (1.0 - ADAM_B2) * jnp.square(gv)
    m_hat = mn / (1.0 - ADAM_B1 ** ADAM_STEP)
    v_hat = vn / (1.0 - ADAM_B2 ** ADAM_STEP)
    return gv, -ADAM_LR * (m_hat / (jnp.sqrt(v_hat) + ADAM_EPS) + ADAM_WD * wv), mn, vn


def _adamw(name, g, w, m, v):
    r, c = g.shape
    tm = 256 if r % 256 == 0 else r

    def body(g_ref, w_ref, m_ref, v_ref, go_ref, d_ref, mo_ref, vo_ref):
        go_ref[...], d_ref[...], mo_ref[...], vo_ref[...] = _adamw_math(g_ref[...], w_ref[...], m_ref[...], v_ref[...])

    blk = pl.BlockSpec((tm, c), lambda i: (i, 0))
    return pl.pallas_call(
        body, name=name, grid=(r // tm,),
        in_specs=[blk] * 4, out_specs=[blk] * 4,
        out_shape=[jax.ShapeDtypeStruct((r, c), F32)] * 4,
        compiler_params=_cp("parallel"),
    )(g, w, m, v)


def _place():
    x, y, c = lax.axis_index("x"), lax.axis_index("y"), lax.axis_index("c")
    chips = [(1 - x, y), (x, 1 - y), (1 - x, 1 - y)]
    return x, y, c, chips


BLOCK_AXIS = (2, 1, 2, 1)
LARGE_DIMS = ((D, DIN), (D, D), (D, DFF), (DFF, D))


def _full_shape(t, layers, dtype):
    r, c = LARGE_DIMS[t]
    return jax.ShapeDtypeStruct((layers, r, c), dtype)


def _cast_into_full(name, t, shard, b1, dep):
    _, r, c = shard.shape
    tm = min(256, r)
    if BLOCK_AXIS[t] == 1:
        out_spec = pl.BlockSpec((None, tm, c), lambda l, i, br: (l, br[0] * (r // tm) + i, 0))
    else:
        out_spec = pl.BlockSpec((None, tm, c), lambda l, i, br: (l, i, br[0]))

    def body(b_ref, x_ref, dep_ref, o_ref):
        del b_ref, dep_ref
        o_ref[...] = x_ref[...].astype(BF16)

    return pl.pallas_call(
        body, name=name,
        grid_spec=pltpu.PrefetchScalarGridSpec(
            num_scalar_prefetch=1, grid=(DEPTH, r // tm),
            in_specs=[pl.BlockSpec((None, tm, c), lambda l, i, br: (l, i, 0)), ANY],
            out_specs=out_spec),
        out_shape=_full_shape(t, DEPTH, BF16),
        compiler_params=_cp("parallel", "parallel"),
    )(b1, shard, dep)


HBM = pl.BlockSpec(memory_space=pltpu.HBM)
SEM = pl.BlockSpec(memory_space=pltpu.SEMAPHORE)
DATAFLOW = pltpu.SideEffectType.DATAFLOW_SIDE_EFFECTING


def _half(ref, l, t, b, c):
    r, cols = LARGE_DIMS[t]
    if BLOCK_AXIS[t] == 1:
        n = r // 8
        return ref.at[l, pl.ds(pl.multiple_of(b * (2 * n) + c * n, 16), n), :]
    n, w = r // 2, cols // 4
    return ref.at[l, pl.ds(pl.multiple_of(c * n, 16), n), pl.ds(pl.multiple_of(b * w, 128), w)]


def _gather_start(name, layers, ts, fulls):
    n = len(ts)

    def body(*refs):
        f_refs, sems = refs[n:2 * n], refs[2 * n:2 * n + 2 * len(layers)]
        x, y, c, chips = _place()
        for i, l in enumerate(layers):
            for k, t in enumerate(ts):
                own = _half(f_refs[k], l, t, 2 * x + y, c)
                for j, (cx, cy) in enumerate(chips):
                    pltpu.make_async_remote_copy(src_ref=own, dst_ref=own, send_sem=sems[2 * i].at[3 * t + j],
                                                 recv_sem=sems[2 * i + 1].at[3 * t + j], device_id=(cx, cy, c),
                                                 device_id_type=MESH).start()
        refs[-1][...] = jnp.zeros((8, 128), F32)

    outs = pl.pallas_call(
        body, name=name,
        in_specs=[HBM] * n,
        out_specs=[HBM] * n + [SEM] * (2 * len(layers)) + [pl.BlockSpec(memory_space=pltpu.VMEM)],
        out_shape=[pltpu.HBM(f.shape, f.dtype) for f in fulls]
        + [pltpu.SemaphoreType.DMA((12,))] * (2 * len(layers)) + [jax.ShapeDtypeStruct((8, 128), F32)],
        input_output_aliases={k: k for k in range(n)},
        compiler_params=pltpu.CompilerParams(has_side_effects=DATAFLOW),
    )(*[pltpu.with_memory_space_constraint(f, pltpu.HBM) for f in fulls])
    return outs[0:n], {l: (outs[n + 2 * i], outs[n + 1 + 2 * i]) for i, l in enumerate(layers)}, outs[-1]


def _gather_wait(name, l, ts, fulls, sems, after):
    def body(*refs):
        send_sems, recv_sems, f_refs = refs[4], refs[5], refs[7:11]
        x, y, c, chips = _place()
        for t in ts:
            own = _half(f_refs[t], l, t, 2 * x + y, c)
            for j, (cx, cy) in enumerate(chips):
                landed = _half(f_refs[t], l, t, 2 * cx + cy, c)
                pltpu.make_async_remote_copy(src_ref=own, dst_ref=landed, send_sem=send_sems.at[3 * t + j],
                                             recv_sem=recv_sems.at[3 * t + j], device_id=(cx, cy, c),
                                             device_id_type=MESH).wait()

    return pl.pallas_call(
        body, name=name,
        in_specs=[HBM] * 4 + [SEM, SEM, ANY], out_specs=[HBM] * 4,
        out_shape=[pltpu.HBM(s.shape, s.dtype) for s in (_full_shape(t, DEPTH, BF16) for t in range(4))],
        input_output_aliases={t: t for t in range(4)},
        compiler_params=pltpu.CompilerParams(has_side_effects=DATAFLOW),
    )(*fulls, sems[0], sems[1], after)


def _pass_on(name, l, ts, fulls):
    def body(*refs):
        f_refs, send_sems, recv_sems = refs[4:8], refs[8], refs[9]
        x, y, c, chips = _place()

        def copy(t, j, half):
            cx, cy = chips[j]
            part = _half(f_refs[t], l, t, 2 * cx + cy, half)
            return pltpu.make_async_remote_copy(src_ref=part, dst_ref=part, send_sem=send_sems.at[3 * t + j],
                                                recv_sem=recv_sems.at[3 * t + j], device_id=(x, y, 1 - c),
                                                device_id_type=MESH)

        for t in ts:
            for j in range(3):
                copy(t, j, c).start()
        for t in ts:
            for j in range(3):
                copy(t, j, 1 - c).wait_recv()
                copy(t, j, c).wait_send()

    return pl.pallas_call(
        body, name=name,
        in_specs=[ANY] * 4, out_specs=[ANY] * 4,
        out_shape=[_full_shape(t, DEPTH, BF16) for t in range(4)],
        input_output_aliases={t: t for t in range(4)},
        scratch_shapes=[pltpu.SemaphoreType.DMA((12,)), pltpu.SemaphoreType.DMA((12,))],
    )(*fulls)


def _block2d(ref, t, b):
    r, cols = LARGE_DIMS[t]
    if BLOCK_AXIS[t] == 1:
        return ref.at[pl.ds(pl.multiple_of(b * (r // 4), 16), r // 4), :]
    return ref.at[:, pl.ds(pl.multiple_of(b * (cols // 4), 128), cols // 4)]


def _block_dims(t):
    r, cols = LARGE_DIMS[t]
    return (r // 4, cols) if BLOCK_AXIS[t] == 1 else (r, cols // 4)


def _reduce_copies(ts, g_refs, r_refs, send_sems, recv_sems):
    _, _, c, chips = _place()
    return [pltpu.make_async_remote_copy(src_ref=_block2d(g_refs[i], t, 2 * cx + cy), dst_ref=r_refs[i].at[j],
                                         send_sem=send_sems.at[3 * i + j], recv_sem=recv_sems.at[3 * i + j],
                                         device_id=(cx, cy, c), device_id_type=MESH)
            for i, t in enumerate(ts) for j, (cx, cy) in enumerate(chips)]


def _reduce_start(name, ts, grads):
    n = len(ts)

    def body(*refs):
        for cp in _reduce_copies(ts, refs[n:2 * n], refs[2 * n:3 * n], refs[3 * n], refs[3 * n + 1]):
            cp.start()
        refs[3 * n + 2][...] = jnp.zeros((8, 128), F32)

    outs = pl.pallas_call(
        body, name=name,
        in_specs=[HBM] * n,
        out_specs=[HBM] * (2 * n) + [SEM, SEM, pl.BlockSpec(memory_space=pltpu.VMEM)],
        out_shape=[pltpu.HBM(g.shape, BF16) for g in grads]
        + [pltpu.HBM((3,) + _block_dims(t), BF16) for t in ts]
        + [pltpu.SemaphoreType.DMA((3 * n,)), pltpu.SemaphoreType.DMA((3 * n,)), jax.ShapeDtypeStruct((8, 128), F32)],
        input_output_aliases={i: i for i in range(n)},
        compiler_params=pltpu.CompilerParams(has_side_effects=DATAFLOW),
    )(*[pltpu.with_memory_space_constraint(g, pltpu.HBM) for g in grads])
    return outs[0:n], outs[n:2 * n], (outs[2 * n], outs[2 * n + 1]), outs[2 * n + 2]


def _reduce_wait(name, ts, grads, landing, sems, afters):
    n = len(ts)
    first_out = 2 * n + 2 + len(afters)

    def body(*refs):
        for cp in _reduce_copies(ts, refs[first_out:first_out + n], refs[first_out + n:first_out + 2 * n],
                                 refs[2 * n], refs[2 * n + 1]):
            cp.wait()

    outs = pl.pallas_call(
        body, name=name,
        in_specs=[HBM] * (2 * n) + [SEM, SEM] + [ANY] * len(afters), out_specs=[HBM] * (2 * n),
        out_shape=[pltpu.HBM(g.shape, BF16) for g in grads] + [pltpu.HBM(r.shape, BF16) for r in landing],
        input_output_aliases={i: i for i in range(2 * n)},
        compiler_params=pltpu.CompilerParams(has_side_effects=DATAFLOW),
    )(*grads, *landing, sems[0], sems[1], *afters)
    return outs[0:n], outs[n:2 * n]


def _add4(name, t, own, landed, b1):
    rb, cb = _block_dims(t)
    tm = min(256, rb)
    if BLOCK_AXIS[t] == 1:
        own_spec = pl.BlockSpec((tm, cb), lambda i, br: (br[0] * (rb // tm) + i, 0))
    else:
        own_spec = pl.BlockSpec((tm, cb), lambda i, br: (i, br[0]))

    def body(b_ref, o_ref, r0_ref, r1_ref, r2_ref, s_ref):
        del b_ref
        s_ref[...] = ((o_ref[...].astype(F32) + r0_ref[...].astype(F32))
                      + (r1_ref[...].astype(F32) + r2_ref[...].astype(F32))).astype(BF16)

    def got(j):
        return pl.BlockSpec((None, tm, cb), lambda i, br: (j, i, 0))

    return pl.pallas_call(
        body, name=name,
        grid_spec=pltpu.PrefetchScalarGridSpec(
            num_scalar_prefetch=1, grid=(rb // tm,),
            in_specs=[own_spec, got(0), got(1), got(2)],
            out_specs=pl.BlockSpec((tm, cb), lambda i, br: (i, 0))),
        out_shape=jax.ShapeDtypeStruct((rb, cb), BF16),
        compiler_params=_cp("parallel"),
    )(b1, own, landed, landed, landed)


def _swap_sib(name, sums):
    def body(*refs):
        s_refs, t_refs, send_sems, recv_sems = refs[0:4], refs[4:8], refs[8], refs[9]
        x, y, c, _ = _place()
        cps = [pltpu.make_async_remote_copy(src_ref=s_refs[t], dst_ref=t_refs[t], send_sem=send_sems.at[t],
                                            recv_sem=recv_sems.at[t], device_id=(x, y, 1 - c), device_id_type=MESH)
               for t in range(4)]
        for cp in cps:
            cp.start()
        for cp in cps:
            cp.wait()

    return pl.pallas_call(
        body, name=name,
        in_specs=[ANY] * 4, out_specs=[ANY] * 4,
        out_shape=[jax.ShapeDtypeStruct(s.shape, BF16) for s in sums],
        scratch_shapes=[pltpu.SemaphoreType.DMA((4,)), pltpu.SemaphoreType.DMA((4,))],
    )(*sums)


def _adamw_pair(name, l, s_own, s_sib, w, m, v, outs):
    rb, cb = s_own.shape
    tm = min(512, rb)

    def body(a_ref, b_ref, w_ref, m_ref, v_ref, g0, d0, m0, v0, go_ref, d_ref, mo_ref, vo_ref):
        del g0, d0, m0, v0
        gv = a_ref[...].astype(F32) + b_ref[...].astype(F32)
        go_ref[...], d_ref[...], mo_ref[...], vo_ref[...] = _adamw_math(gv, w_ref[...], m_ref[...], v_ref[...])

    part = pl.BlockSpec((tm, cb), lambda i: (i, 0))
    layer = pl.BlockSpec((None, tm, cb), lambda i: (l, i, 0))
    return pl.pallas_call(
        body, name=name, grid=(rb // tm,),
        in_specs=[part, part, layer, layer, layer] + [ANY] * 4,
        out_specs=[layer] * 4,
        out_shape=[jax.ShapeDtypeStruct((DEPTH, rb, cb), F32)] * 4,
        input_output_aliases={5 + i: i for i in range(4)},
        compiler_params=_cp("parallel"),
    )(s_own, s_sib, w, m, v, *outs)


def _all_gather8(name, v, dep):
    m_per, n = v.shape

    def body(v_ref, dep_ref, out_ref, send_sems, recv_sems, local_sem):
        del dep_ref
        x, y, c, chips = _place()
        me, sib = (x, y, c), (x, y, 1 - c)

        def rows(px, py, pc):
            return out_ref.at[pl.ds((4 * px + 2 * py + pc) * m_per, m_per), :]

        def copy(k, block, to, src=None):
            return pltpu.make_async_remote_copy(
                src_ref=rows(*block) if src is None else src, dst_ref=rows(*block),
                send_sem=send_sems.at[k], recv_sem=recv_sems.at[k], device_id=to, device_id_type=MESH)

        mine = pltpu.make_async_copy(v_ref, rows(*me), local_sem)
        mine.start()
        first = [copy(0, me, sib, src=v_ref)]
        first += [copy(1 + j, me, (*chip, c), src=v_ref) for j, chip in enumerate(chips)]
        for cp in first:
            cp.start()
        passed = [copy(4 + j, (*chip, c), sib) for j, chip in enumerate(chips)]
        for j, chip in enumerate(chips):
            copy(1 + j, (*chip, c), me).wait_recv()
            passed[j].start()
        copy(0, sib, me).wait_recv()
        for j, chip in enumerate(chips):
            copy(4 + j, (*chip, 1 - c), me).wait_recv()
        for cp in first + passed:
            cp.wait_send()
        mine.wait()

    return pl.pallas_call(
        body, name=name,
        out_shape=jax.ShapeDtypeStruct((8 * m_per, n), v.dtype),
        in_specs=[pl.BlockSpec(memory_space=pltpu.VMEM), ANY],
        out_specs=pl.BlockSpec(memory_space=pltpu.VMEM),
        scratch_shapes=[pltpu.SemaphoreType.DMA((7,)), pltpu.SemaphoreType.DMA((7,)), pltpu.SemaphoreType.DMA],
    )(v, dep)


def _sum8(name, g):
    def body(g_ref, o_ref):
        acc = g_ref[0]
        for d in range(1, 8):
            acc = acc + g_ref[d]
        o_ref[...] = acc

    return pl.pallas_call(body, name=name, out_shape=jax.ShapeDtypeStruct(g.shape[1:], F32))(g)


def _pack(parts):
    flat = []
    for a in parts:
        a = a.reshape(-1)
        flat.append(jnp.pad(a, (0, (-a.shape[0]) % 128)))
    cat = jnp.concatenate(flat)
    cat = jnp.pad(cat, (0, (-cat.shape[0]) % 1024))
    return cat.reshape(-1, 128)


def _unpack(packed, shapes):
    flat = packed.reshape(-1)
    out, at = [], 0
    for shp in shapes:
        n = 1
        for d in shp:
            n *= d
        out.append(flat[at:at + n].reshape(shp))
        at += n + (-n) % 128
    return out


def _local_step(x, target, layer_weights, on_grads, small):
    qg_all = jnp.tile(small["q_norm_g"], (1, 8))
    kg_all = jnp.tile(small["k_norm_g"], (1, 8))
    bias_all = _bias_layout(_bias_expand("bias_expand", jnp.pad(small["rel_bias"], ((0, 0), (0, 0), (0, NIDX - 257)))))
    same_group = jnp.eye(4, dtype=F32)[None, :, None, :, None]
    pwbd_all = (small["pool_w"][:, :, :, None, :] * same_group).reshape(DEPTH, PWD, PWD)
    saved = []
    xin = x
    h = _rmsnorm("norm_first", x, small["norm1_g"][0:1])
    for l in range(DEPTH):
        w_in = layer_weights(l, (0,), xin)[0]
        qg, kg, bias = qg_all[l:l + 1], kg_all[l:l + 1], bias_all[l]
        cw, pwbd, ps = small["conv_w"][l], pwbd_all[l], small["pool_scale"][l:l + 1]
        p = _mm_nn(f"proj_in_{l}", h, w_in, l, F32)
        q, qt, kp, kt, vp, vt = _qkv(f"qkv_{l}", p, qg, kg)
        o, lse = _attn_fwd(f"attn_fwd_{l}", kp, qt, vt, bias)
        w_in, w_out, w_1, w_2 = layer_weights(l, (1, 2, 3), o)
        mix = _convpool_fwd(f"convpool_fwd_{l}", p, o, cw, pwbd, ps)
        x1, h2 = _mm_res_norm(f"proj_out_{l}", mix, w_out, l, xin, small["norm2_g"][l:l + 1])
        gnext = small["norm1_g"][(l + 1) % DEPTH][None]
        a, x2, hnext = _mlp_fwd(f"mlp_{l}", h2, w_1, w_2, l, x1, gnext)
        saved.append(dict(xin=xin, h=h, p=p, q=q, qt=qt, kp=kp, kt=kt, vp=vp, bias=bias, mix=mix, x1=x1, h2=h2, a=a, lse=lse,
                          qg=qg, kg=kg, cw=cw, pwbd=pwbd, ps=ps))
        xin, h = x2, hnext

    dx, dxb, loss = _loss_grad("loss_grad", xin, target)
    raw = {k: [None] * DEPTH for k in ("dg1", "dqg", "dkg", "db", "dw0", "dw1", "dw2", "dpw", "dps", "dg2")}
    for l in reversed(range(DEPTH)):
        sv = saved[l]
        da = _mm_nt_relu(f"mlp2_bwd_{l}", dxb, w_2, l, sv["a"])
        g_2 = _mm_tn(f"mlp2_wgrad_{l}", sv["a"], dxb, 512, 1024, relu2=True)
        g_1 = _mm_tn(f"mlp1_wgrad_{l}", sv["h2"], da, 1024, 512)
        dep = on_grads(l, (2, 3), (g_1, g_2))
        dx1, dx1b, dg2 = _mm_nt_normbwd(f"mlp1_bwd_{l}", da, w_1, l, sv["x1"], small["norm2_g"][l:l + 1], dx, dep)
        do, dot, dmix, dl = _proj_out_bwd(f"proj_out_bwd_{l}", dx1b, w_out, l, sv["mix"])
        g_out = _mm_tn(f"proj_out_wgrad_{l}", sv["mix"], dx1b, 512, 1024)
        dcp, dw0, dw1, dw2, dps, dpw = _convpool_bwd(f"convpool_bwd_{l}", sv["p"], dmix, sv["cw"], sv["pwbd"], sv["ps"])
        dq, dkp, dvp, db = _attn_bwd(f"attn_bwd_{l}", sv["q"], sv["qt"], sv["kp"], sv["kt"], sv["vp"], sv["bias"],
                                     do, dot, sv["lse"], _rowsum_layout(dl, x.shape[0] // UNIT))
        dp, dqg, dkg = _qkv_bwd(f"qkv_bwd_{l}", sv["p"], dq, dkp, dvp, dcp, sv["qg"], sv["kg"])
        g_in = _mm_tn(f"proj_in_wgrad_{l}", sv["h"], dp, 1024, 640)
        dep = on_grads(l, (0, 1), (g_in, g_out))
        dx, dxb, dg1 = _mm_nt_normbwd(f"proj_in_bwd_{l}", dp, w_in, l, sv["xin"], small["norm1_g"][l:l + 1], dx1, dep)
        for k, val in dict(dg1=dg1, dqg=dqg, dkg=dkg, db=db, dw0=dw0, dw1=dw1, dw2=dw2, dpw=dpw, dps=dps, dg2=dg2).items():
            raw[k][l] = val
    cat = {k: jnp.concatenate(v, axis=0) for k, v in raw.items() if k not in ("db", "dpw")}
    drb = _bias_reduce("bias_reduce", _bias_unlayout(jnp.stack(raw["db"])))
    dpw = jnp.stack(raw["dpw"])
    gsmall = {
        "norm1_g": cat["dg1"], "q_norm_g": cat["dqg"][:, :HD], "k_norm_g": cat["dkg"][:, :HD],
        "rel_bias": drb[:, :, :257],
        "conv_w": jnp.stack([cat["dw0"], cat["dw1"], cat["dw2"]], axis=1),
        "pool_w": jnp.stack([dpw[:, g * 64:(g + 1) * 64, g * 64:(g + 1) * 64] for g in range(4)], axis=1),
        "pool_scale": cat["dps"], "norm2_g": cat["dg2"],
    }
    return loss, dx, gsmall


SMALL = ("norm1_g", "q_norm_g", "k_norm_g", "rel_bias", "conv_w", "pool_w", "pool_scale", "norm2_g")
LARGE = ("w_in", "w_out", "w_mlp1", "w_mlp2")


def kernel(x, norm1_g, w_in, q_norm_g, k_norm_g, rel_bias, conv_w, pool_w, pool_scale, w_out, norm2_g, w_mlp1, w_mlp2, loss_target, m_norm1_g, m_w_in, m_q_norm_g, m_k_norm_g, m_rel_bias, m_conv_w, m_pool_w, m_pool_scale, m_w_out, m_norm2_g, m_w_mlp1, m_w_mlp2, v_norm1_g, v_w_in, v_q_norm_g, v_k_norm_g, v_rel_bias, v_conv_w, v_pool_w, v_pool_scale, v_w_out, v_norm2_g, v_w_mlp1, v_w_mlp2):
    w = dict(norm1_g=norm1_g, w_in=w_in, q_norm_g=q_norm_g, k_norm_g=k_norm_g, rel_bias=rel_bias, conv_w=conv_w,
             pool_w=pool_w, pool_scale=pool_scale, w_out=w_out, norm2_g=norm2_g, w_mlp1=w_mlp1, w_mlp2=w_mlp2)
    m = dict(norm1_g=m_norm1_g, w_in=m_w_in, q_norm_g=m_q_norm_g, k_norm_g=m_k_norm_g, rel_bias=m_rel_bias,
             conv_w=m_conv_w, pool_w=m_pool_w, pool_scale=m_pool_scale, w_out=m_w_out, norm2_g=m_norm2_g,
             w_mlp1=m_w_mlp1, w_mlp2=m_w_mlp2)
    v = dict(norm1_g=v_norm1_g, w_in=v_w_in, q_norm_g=v_q_norm_g, k_norm_g=v_k_norm_g, rel_bias=v_rel_bias,
             conv_w=v_conv_w, pool_w=v_pool_w, pool_scale=v_pool_scale, w_out=v_w_out, norm2_g=v_norm2_g,
             w_mlp1=v_w_mlp1, w_mlp2=v_w_mlp2)
    ax, ay, ac = lax.axis_index("x"), lax.axis_index("y"), lax.axis_index("c")
    b1 = jnp.reshape(2 * ax + ay, (1,)).astype(jnp.int32)

    cw_rows = _all_gather8("gather_conv_w", jnp.pad(conv_w.reshape(DEPTH * 3, 64), ((0, 4), (0, 64))), b1)
    cw_chips = [cw_rows[(4 * cx + 2 * cy) * 16:(4 * cx + 2 * cy) * 16 + 12, :64] for cx in range(2) for cy in range(2)]
    small = {n: w[n] for n in SMALL}
    small["conv_w"] = jnp.concatenate(cw_chips, axis=1).reshape(DEPTH, 3, CW)

    (w_in_full,), in_sems, in_token = _gather_start(
        "gather_start_in", (0,), (0,), [_cast_into_full("cast_w_in", 0, w["w_in"], b1, cw_rows)])
    others, first_sems, first_token = _gather_start(
        "gather_start_first", (0,), (1, 2, 3),
        [_cast_into_full(f"cast_{LARGE[t]}", t, w[LARGE[t]], b1, in_token) for t in (1, 2, 3)])
    held = [[w_in_full] + list(others)]
    sems = {(0, 0): in_sems[0], (0, 1): first_sems[0]}

    def layer_weights(l, ts, after):
        if l > 0:
            ts = (0, 1, 2, 3) if ts == (0,) else ()
        if ts:
            tag = f"{l}_{ts[0]}"
            first_in = l == 0 and ts == (0,)
            after = first_token if first_in else after
            arrived = _gather_wait(f"gather_wait_{tag}", l, ts, held[0], sems[l, ts[0] if l == 0 else 0], after)
            if first_in:
                arrived, rest_sems, _ = _gather_start("gather_start_rest", tuple(range(1, DEPTH)), (0, 1, 2, 3),
                                                      arrived)
                sems.update({(k, 0): v for k, v in rest_sems.items()})
            held[0] = _pass_on(f"pass_on_{tag}", l, ts, arrived)
        return held[0]

    flights = {}

    def await_flight(l, ts, afters):
        g, landing, sm, _ = flights[l, ts]
        flights[l, ts] = _reduce_wait(f"reduce_wait_{l}_{ts[0]}", ts, g, landing, sm, afters)

    def on_grads(l, ts, grads):
        if ts == (0, 1) and l + 1 < DEPTH:
            await_flight(l + 1, (2, 3), [grads[0]])
            await_flight(l + 1, (0, 1), [grads[0]])
        flights[l, ts] = _reduce_start(f"reduce_start_{l}_{ts[0]}", ts, grads)
        return flights[l, ts][3]

    loss_part, grad_x, gsmall = _local_step(x[0], loss_target[0], layer_weights, on_grads, small)
    loss = lax.psum(loss_part[0, 0], ("x", "y", "c"))
    order = [n for n in SMALL]
    packed = _pack([gsmall[n] for n in order])

    out = {n: [lax.empty(w[n].shape, F32) for _ in range(4)] for n in LARGE}
    for l in reversed(range(DEPTH)):
        if l == 0:
            afters = [grad_x, packed] + [out[n][0] for n in LARGE]
            await_flight(0, (2, 3), afters)
            await_flight(0, (0, 1), afters)
        sums = [None] * 4
        for ts in ((0, 1), (2, 3)):
            g, landing = flights[l, ts]
            for i, t in enumerate(ts):
                sums[t] = _add4(f"add4_{LARGE[t]}_{l}", t, g[i], landing[i], b1)
        theirs = _swap_sib(f"swap_sib_{l}", sums)
        for t, n in enumerate(LARGE):
            out[n] = _adamw_pair(f"adamw_{n}_{l}", l, sums[t], theirs[t], w[n], m[n], v[n], out[n])

    rows = packed.shape[0]
    summed = _sum8("sum_small", _all_gather8("gather_small", packed, out[LARGE[0]][0]).reshape(8, rows, 128))
    gfull = dict(zip(order, _unpack(summed, [gsmall[n].shape for n in order])))
    gfull["conv_w"] = lax.dynamic_slice_in_dim(gfull["conv_w"], (2 * ax + ay) * 64, 64, axis=2)
    res = _adamw("adamw_small", _pack([gfull[n] for n in order]), _pack([w[n] for n in order]),
                 _pack([m[n] for n in order]), _pack([v[n] for n in order]))
    for n, parts in zip(order, zip(*[_unpack(r, [w[k].shape for k in order]) for r in res])):
        out[n] = list(parts)

    names = ("norm1_g", "w_in", "q_norm_g", "k_norm_g", "rel_bias", "conv_w", "pool_w", "pool_scale", "w_out",
             "norm2_g", "w_mlp1", "w_mlp2")
    flat = [loss, grad_x[None]]
    for i in range(4):
        flat += [out[n][i] for n in names]
    return tuple(flat)
```

```python
import functools

import jax
import jax.numpy as jnp
from jax import lax
from jax.experimental import pallas as pl
from jax.experimental.pallas import tpu as pltpu

F32 = jnp.float32
BF16 = jnp.bfloat16

D = 1024
DEPTH = 4
CH = 64
NPREV = 8
KB = (NPREV + 1) * CH
PADR = NPREV * CH
HD = 64
AW = 512
CW = 256
PWD = 256
DIN = 3 * AW + 3 * CW + PWD
DFF = 4 * D
NIDX = 384
EPS = 1e-6
NEG_INF = -1e30

ADAM_LR = 0.001
ADAM_B1 = 0.9
ADAM_B2 = 0.999
ADAM_EPS = 1e-08
ADAM_WD = 0.01
ADAM_STEP = 10

VMEM_LIMIT = 52 * 1024 * 1024
MM_ROWS = 512
MESH = pl.DeviceIdType.MESH
ANY = pl.BlockSpec(memory_space=pl.ANY)


def _cp(*sem):
    return pltpu.CompilerParams(dimension_semantics=sem, vmem_limit_bytes=VMEM_LIMIT)


def _inv_rms(x):
    return lax.rsqrt(jnp.mean(x * x, axis=-1, keepdims=True) + EPS)


def _head_mean_matrix():
    r = lax.broadcasted_iota(jnp.int32, (AW, AW), 0) // HD
    c = lax.broadcasted_iota(jnp.int32, (AW, AW), 1) // HD
    return jnp.where(r == c, 1.0 / HD, 0.0).astype(BF16)


def _two_pass_dot(x, m):
    hi = x.astype(BF16)
    lo = (x - hi.astype(F32)).astype(BF16)
    return (jnp.dot(hi, m, preferred_element_type=F32)
            + jnp.dot(lo, m, preferred_element_type=F32))


def _head_mean(x, hm):
    return _two_pass_dot(x, hm)


def _rmsnorm(name, x, g):
    s = x.shape[0]
    tm = 512

    def body(x_ref, g_ref, h_ref):
        xv = x_ref[...]
        h_ref[...] = (xv * _inv_rms(xv) * g_ref[...]).astype(BF16)

    return pl.pallas_call(
        body, name=name, grid=(s // tm,),
        in_specs=[pl.BlockSpec((tm, D), lambda i: (i, 0)), pl.BlockSpec((1, D), lambda i: (0, 0))],
        out_specs=pl.BlockSpec((tm, D), lambda i: (i, 0)),
        out_shape=jax.ShapeDtypeStruct((s, D), BF16),
        compiler_params=_cp("parallel"),
    )(x, g)


def _relu2(a):
    r = jnp.maximum(a, jnp.zeros_like(a))
    return r * r


def _mm_nn(name, a, w, l, out_dtype):
    s, k = a.shape
    n = w.shape[2]
    tm = MM_ROWS

    def body(a_ref, w_ref, o_ref):
        o_ref[...] = jnp.dot(a_ref[...], w_ref[...], preferred_element_type=F32).astype(o_ref.dtype)

    return pl.pallas_call(
        body, name=name, grid=(s // tm,),
        in_specs=[pl.BlockSpec((tm, k), lambda i: (i, 0)),
                  pl.BlockSpec((None, k, n), lambda i: (l, 0, 0))],
        out_specs=pl.BlockSpec((tm, n), lambda i: (i, 0)),
        out_shape=jax.ShapeDtypeStruct((s, n), out_dtype),
        compiler_params=_cp("parallel"),
    )(a, w)


def _mm_res_norm(name, a, w, l, res, g):
    s, k = a.shape
    tm = MM_ROWS

    def body(a_ref, w_ref, r_ref, g_ref, x_ref, h_ref):
        acc = r_ref[...] + jnp.dot(a_ref[...], w_ref[...], preferred_element_type=F32)
        x_ref[...] = acc
        h_ref[...] = (acc * _inv_rms(acc) * g_ref[...]).astype(BF16)

    return pl.pallas_call(
        body, name=name, grid=(s // tm,),
        in_specs=[pl.BlockSpec((tm, k), lambda i: (i, 0)),
                  pl.BlockSpec((None, k, D), lambda i: (l, 0, 0)),
                  pl.BlockSpec((tm, D), lambda i: (i, 0)),
                  pl.BlockSpec((1, D), lambda i: (0, 0))],
        out_specs=[pl.BlockSpec((tm, D), lambda i: (i, 0))] * 2,
        out_shape=[jax.ShapeDtypeStruct((s, D), F32), jax.ShapeDtypeStruct((s, D), BF16)],
        compiler_params=_cp("parallel"),
    )(a, w, res, g)


def _mlp_fwd(name, h2, w1, w2, l, res, g):
    s = h2.shape[0]
    tm = 256

    def body(h_ref, w1_ref, w2_ref, r_ref, g_ref, a_ref, x_ref, hn_ref):
        a = jnp.dot(h_ref[...], w1_ref[...], preferred_element_type=F32).astype(BF16)
        a_ref[...] = a
        acc = r_ref[...] + jnp.dot(_relu2(a), w2_ref[...], preferred_element_type=F32)
        x_ref[...] = acc
        hn_ref[...] = (acc * _inv_rms(acc) * g_ref[...]).astype(BF16)

    once = pl.Buffered(1)
    rows = pl.BlockSpec((tm, D), lambda i: (i, 0))
    return pl.pallas_call(
        body, name=name, grid=(s // tm,),
        in_specs=[rows,
                  pl.BlockSpec((None, D, DFF), lambda i: (l, 0, 0), pipeline_mode=once),
                  pl.BlockSpec((None, DFF, D), lambda i: (l, 0, 0), pipeline_mode=once),
                  rows, pl.BlockSpec((1, D), lambda i: (0, 0))],
        out_specs=[pl.BlockSpec((tm, DFF), lambda i: (i, 0)), rows, rows],
        out_shape=[jax.ShapeDtypeStruct((s, DFF), BF16), jax.ShapeDtypeStruct((s, D), F32),
                   jax.ShapeDtypeStruct((s, D), BF16)],
        compiler_params=_cp("parallel"),
    )(h2, w1, w2, res, g)


def _qkv(name, p, qg, kg):
    s = p.shape[0]
    tm = PADR
    nb = s // tm

    def body(pq_ref, pk_ref, pv_ref, qg_ref, kg_ref, q_ref, qt_ref, k_ref, kt_ref, v_ref, vt_ref):
        t = pl.program_id(0)
        hm = _head_mean_matrix()

        def nrm(x, g):
            return x * lax.rsqrt(_head_mean(x * x, hm) + EPS) * g

        first = t == 0
        qq = nrm(pq_ref[...], qg_ref[...]) * 0.125
        kk = jnp.where(first, 0.0, nrm(pk_ref[...], kg_ref[...]))
        vv = jnp.where(first, 0.0, pv_ref[...])
        q_ref[...] = qq.astype(BF16)
        qt_ref[...] = qq.T.astype(BF16)
        k_ref[...] = kk.astype(BF16)
        kt_ref[...] = kk.T.astype(BF16)
        v_ref[...] = vv.astype(BF16)
        vt_ref[...] = vv.T.astype(BF16)

    def src(col):
        return pl.BlockSpec((tm, AW), lambda t: (jnp.maximum(t - 1, 0), col))

    gspec = pl.BlockSpec((1, AW), lambda t: (0, 0))
    rows = pl.BlockSpec((tm, AW), lambda t: (t, 0))
    cols = pl.BlockSpec((AW, tm), lambda t: (0, t))
    return pl.pallas_call(
        body, name=name, grid=(nb + 1,),
        in_specs=[src(0), src(1), src(2), gspec, gspec],
        out_specs=[pl.BlockSpec((tm, AW), lambda t: (jnp.maximum(t - 1, 0), 0)),
                   pl.BlockSpec((AW, tm), lambda t: (0, jnp.maximum(t - 1, 0))),
                   rows, cols, rows, cols],
        out_shape=[jax.ShapeDtypeStruct((s, AW), BF16), jax.ShapeDtypeStruct((AW, s), BF16),
                   jax.ShapeDtypeStruct((s + PADR, AW), BF16), jax.ShapeDtypeStruct((AW, s + PADR), BF16),
                   jax.ShapeDtypeStruct((s + PADR, AW), BF16), jax.ShapeDtypeStruct((AW, s + PADR), BF16)],
        compiler_params=_cp("arbitrary"),
    )(p, p, p, qg, kg)


NBAND = KB // CH
HIGHEST = lax.Precision.HIGHEST
NT_DIMS = (((1,), (1,)), ((), ()))


def _onehot_table(a):
    m = lax.broadcasted_iota(jnp.int32, (128, NIDX), 0)
    idx = lax.broadcasted_iota(jnp.int32, (128, NIDX), 1)
    rel = jnp.clip(KB - 1 - (CH * a + m), -128, 128) + 128
    return jnp.where(rel == idx, 1.0, 0.0).astype(F32)


def _onehot_diagonal():
    r = lax.broadcasted_iota(jnp.int32, (CH * CH, 128), 0)
    m = lax.broadcasted_iota(jnp.int32, (CH * CH, 128), 1)
    return jnp.where((r % CH) - (r // CH) + (CH - 1) == m, 1.0, 0.0).astype(F32)


def _bias_expand(name, rb):
    def body(rb_ref, o_ref):
        along = [lax.dot_general(rb_ref[...], _onehot_table(a), NT_DIMS, preferred_element_type=F32,
                                 precision=HIGHEST) for a in range(NBAND)]
        o_ref[...] = lax.dot_general(jnp.concatenate(along, axis=0), _onehot_diagonal(), NT_DIMS,
                                     preferred_element_type=F32, precision=HIGHEST)

    return pl.pallas_call(
        body, name=name, grid=(DEPTH,),
        in_specs=[pl.BlockSpec((None, 8, NIDX), lambda l: (l, 0, 0))],
        out_specs=pl.BlockSpec((None, NBAND * 8, CH * CH), lambda l: (l, 0, 0)),
        out_shape=jax.ShapeDtypeStruct((DEPTH, NBAND * 8, CH * CH), F32),
        compiler_params=_cp("parallel"),
    )(rb)


def _bias_reduce(name, db):
    def body(db_ref, o_ref):
        along = jnp.dot(db_ref[...], _onehot_diagonal(), preferred_element_type=F32, precision=HIGHEST)
        acc = jnp.zeros((8, NIDX), F32)
        for a in range(NBAND):
            acc = acc + jnp.dot(along[8 * a:8 * a + 8, :], _onehot_table(a), preferred_element_type=F32,
                                precision=HIGHEST)
        o_ref[...] = acc

    return pl.pallas_call(
        body, name=name, grid=(DEPTH,),
        in_specs=[pl.BlockSpec((None, NBAND * 8, CH * CH), lambda l: (l, 0, 0))],
        out_specs=pl.BlockSpec((None, 8, NIDX), lambda l: (l, 0, 0)),
        out_shape=jax.ShapeDtypeStruct((DEPTH, 8, NIDX), F32),
        compiler_params=_cp("parallel"),
    )(db)


def _bias_layout(flat):
    b = flat.reshape(DEPTH, NBAND, 8, CH, CH).transpose(0, 2, 1, 4, 3).reshape(DEPTH, 4, 2, KB, CH)
    pair = b.transpose(0, 1, 3, 2, 4).reshape(DEPTH, 4, KB, 128)
    first = jnp.pad(pair, ((0, 0), (0, 0), (0, CH), (0, 0)), constant_values=NEG_INF)
    second = jnp.pad(pair, ((0, 0), (0, 0), (CH, 0), (0, 0)), constant_values=NEG_INF)
    return jnp.concatenate([first, second], axis=3)


def _bias_unlayout(dbt):
    b = dbt.reshape(DEPTH, 4, NBAND, CH, 2, CH)
    return b.transpose(0, 2, 1, 4, 5, 3).reshape(DEPTH, NBAND * 8, CH * CH)


UNIT = 2 * CH
BAND2 = KB + CH


def _pair_weights(xt):
    x = xt.astype(F32)
    row = lax.broadcasted_iota(jnp.int32, (128, UNIT), 0)
    low = lax.broadcasted_iota(jnp.int32, (128, UNIT), 1) < HD
    swapped = pltpu.roll(x, HD, 1)
    same = (row < HD) == low
    first = jnp.where(same, jnp.where(low, x, swapped), 0.0)
    second = jnp.where(same, jnp.where(low, swapped, x), 0.0)
    return jnp.concatenate([first, second], axis=1).astype(BF16)


def _pair_rows(x):
    low = lax.broadcasted_iota(jnp.int32, (CH, 128), 1) < HD
    zero = jnp.zeros((CH, 128), x.dtype)
    parts = []
    for c in range(2):
        xc = x[c * CH:(c + 1) * CH, :]
        parts += [jnp.where(low, xc, zero), jnp.where(low, zero, xc)]
    return jnp.concatenate(parts, axis=0)


def _unpair(raw):
    b0, b1 = raw[:, 0:128], raw[:, 128:256]
    row = lax.broadcasted_iota(jnp.int32, (128, 128), 0)
    low = lax.broadcasted_iota(jnp.int32, (128, 128), 1) < HD
    top = jnp.where(low, b0, pltpu.roll(b1, HD, 1))
    bottom = jnp.where(low, pltpu.roll(b0, HD, 1), b1)
    return jnp.where(row < HD, top, bottom).T


def _scores_t(kb, qw, bias2, row0, padded):
    s = jnp.dot(kb, qw, preferred_element_type=F32) + bias2
    if padded:
        s = jnp.where(row0 + lax.broadcasted_iota(jnp.int32, (BAND2, 256), 0) >= PADR, s, NEG_INF)
    return s


def _unit_loops(s, unit):
    lax.fori_loop(0, PADR // UNIT, lambda u, c: unit(u, True, c), 0, unroll=2)
    lax.fori_loop(PADR // UNIT, s // UNIT, lambda u, c: unit(u, False, c), 0, unroll=7)


def _attn_fwd(name, kp, qt, vt, bias2):
    s = qt.shape[1]
    nu = s // UNIT

    def body(k_ref, qt_ref, vt_ref, b_ref, o_ref, lse_ref):
        def unit(u, padded, carry):
            r0 = pl.multiple_of(u * UNIT, UNIT)
            sc = _scores_t(k_ref[pl.ds(r0, BAND2), :], _pair_weights(qt_ref[:, pl.ds(r0, UNIT)]), b_ref[...],
                           r0, padded)
            top = jnp.max(sc, axis=0, keepdims=True)
            e = jnp.exp(sc - top)
            total = jnp.sum(e, axis=0, keepdims=True)
            raw = jnp.dot(vt_ref[:, pl.ds(r0, BAND2)], e.astype(BF16), preferred_element_type=F32) * (1.0 / total)
            o_ref[pl.ds(r0, UNIT), :] = _unpair(raw).astype(BF16)
            lse_ref[u] = jnp.broadcast_to(top + jnp.log(total), (8, 256))
            return carry

        _unit_loops(s, unit)

    return pl.pallas_call(
        body, name=name, grid=(AW // 128,),
        in_specs=[pl.BlockSpec((s + PADR, 128), lambda h: (0, h)),
                  pl.BlockSpec((128, s), lambda h: (h, 0)),
                  pl.BlockSpec((128, s + PADR), lambda h: (h, 0)),
                  pl.BlockSpec((None, BAND2, 256), lambda h: (h, 0, 0))],
        out_specs=[pl.BlockSpec((s, 128), lambda h: (0, h)),
                   pl.BlockSpec((None, nu, 8, 256), lambda h: (h, 0, 0, 0))],
        out_shape=[jax.ShapeDtypeStruct((s, AW), BF16), jax.ShapeDtypeStruct((4, nu, 8, 256), F32)],
        compiler_params=_cp("parallel"),
    )(kp, qt, vt, bias2)


def _attn_bwd(name, q, qt, kp, kt, vp, bias2, do, dot, lse, dl):
    s = q.shape[0]
    nu = s // UNIT

    def body(q_ref, qt_ref, k_ref, kt_ref, v_ref, b_ref, do_ref, dot_ref, lse_ref, dl_ref,
             dq_ref, dk_ref, dv_ref, db_ref):
        dk_ref[...] = jnp.zeros_like(dk_ref)
        dv_ref[...] = jnp.zeros_like(dv_ref)
        db_ref[...] = jnp.zeros_like(db_ref)

        def unit(u, padded, carry):
            r0 = pl.multiple_of(u * UNIT, UNIT)
            rows, band = pl.ds(r0, UNIT), pl.ds(r0, BAND2)
            sc = _scores_t(k_ref[band, :], _pair_weights(qt_ref[:, rows]), b_ref[...], r0, padded)
            pt = jnp.exp(sc - lse_ref[u][0:1, :])
            dpt = jnp.dot(v_ref[band, :], _pair_weights(dot_ref[:, rows]), preferred_element_type=F32)
            ds = pt * (dpt - dl_ref[u][0:1, :])
            db_ref[...] += ds[0:KB, 0:128] + ds[CH:BAND2, 128:256]
            dsb = ds.astype(BF16)
            dq_ref[rows, :] = _unpair(jnp.dot(kt_ref[:, band], dsb, preferred_element_type=F32))
            dk_ref[band, :] += jnp.dot(dsb, _pair_rows(q_ref[rows, :]), preferred_element_type=F32)
            dv_ref[band, :] += jnp.dot(pt.astype(BF16), _pair_rows(do_ref[rows, :]), preferred_element_type=F32)
            return carry

        _unit_loops(s, unit)

    row_q = pl.BlockSpec((s, 128), lambda h: (0, h))
    col_q = pl.BlockSpec((128, s), lambda h: (h, 0))
    row_k = pl.BlockSpec((s + PADR, 128), lambda h: (0, h))
    col_k = pl.BlockSpec((128, s + PADR), lambda h: (h, 0))
    stat = pl.BlockSpec((None, nu, 8, 256), lambda h: (h, 0, 0, 0))
    return pl.pallas_call(
        body, name=name, grid=(AW // 128,),
        in_specs=[row_q, col_q, row_k, col_k, row_k,
                  pl.BlockSpec((None, BAND2, 256), lambda h: (h, 0, 0)), row_q, col_q, stat, stat],
        out_specs=[row_q, row_k, row_k, pl.BlockSpec((None, KB, 128), lambda h: (h, 0, 0))],
        out_shape=[jax.ShapeDtypeStruct((s, AW), F32),
                   jax.ShapeDtypeStruct((s + PADR, AW), F32),
                   jax.ShapeDtypeStruct((s + PADR, AW), F32),
                   jax.ShapeDtypeStruct((4, KB, 128), F32)],
        compiler_params=_cp("parallel"),
    )(q, qt, kp, kt, vp, bias2, do, dot, lse, dl)


def _rowsum_layout(dl, nu):
    d = dl[:, :8].reshape(nu, 2, CH, 4, 2)
    d = d.transpose(3, 0, 1, 4, 2).reshape(4, nu, 1, 256)
    return jnp.broadcast_to(d, (4, nu, 8, 256))


def _rows_before(cur, prev, k):
    row = lax.broadcasted_iota(jnp.int32, cur.shape, 0)
    return jnp.where(row >= k, pltpu.roll(cur, k, 0), pltpu.roll(prev, k, 0))


def _rows_after(cur, nxt, k):
    n = cur.shape[0]
    row = lax.broadcasted_iota(jnp.int32, cur.shape, 0)
    return jnp.where(row < n - k, pltpu.roll(cur, n - k, 0), pltpu.roll(nxt, n - k, 0))


def _pool_window_lanes():
    lg = lax.broadcasted_iota(jnp.int32, (1, PWD), 1) // 64
    return lg, jnp.where(lg == 0, 2.0, jnp.where(lg == 1, 4.0, jnp.where(lg == 2, 8.0, 16.0))).astype(F32)


def _pool_mean_minus_token(u, up, row0):
    lg, wv = _pool_window_lanes()
    sums = []
    c, p = u, up
    for k in (1, 2, 4, 8):
        c2 = c + _rows_before(c, p, k)
        p = p + pltpu.roll(p, k, 0)
        c = c2
        sums.append(c)
    win = jnp.where(lg == 0, sums[0], jnp.where(lg == 1, sums[1], jnp.where(lg == 2, sums[2], sums[3])))
    pos1 = (row0 + lax.broadcasted_iota(jnp.int32, u.shape, 0) + 1).astype(F32)
    cnt = jnp.minimum(pos1, wv)
    return win / cnt - u, cnt


def _conv_taps(z, zp, w0, w1, w2):
    z1 = _rows_before(z, zp, 1)
    z2 = _rows_before(z, zp, 2)
    return (w0 * z2 + w1 * z1) + w2 * z, z1, z2


CP_TM = 512


def _convpool_fwd(name, p, o, cw, pwbd, ps):
    s = p.shape[0]
    tm = CP_TM
    nb = s // tm

    def body(gb_ref, gc_ref, hin_ref, u_ref, gcp_ref, hinp_ref, up_ref, o_ref, cw_ref, pw_ref, ps_ref, mix_ref):
        i = pl.program_id(0)
        has_prev = i > 0
        z = gc_ref[...] * hin_ref[...]
        zp = jnp.where(has_prev, gcp_ref[...] * hinp_ref[...], 0.0)
        y3, _, _ = _conv_taps(z, zp, cw_ref[0:1, :], cw_ref[1:2, :], cw_ref[2:3, :])
        m, _ = _pool_mean_minus_token(u_ref[...], jnp.where(has_prev, up_ref[...], 0.0), i * tm)
        yp = jnp.dot(m.astype(BF16), pw_ref[...].astype(BF16), preferred_element_type=F32) * ps_ref[...]
        mix_ref[:, 0:AW] = o_ref[...]
        mix_ref[:, AW:AW + CW] = (gb_ref[...] * y3).astype(BF16)
        mix_ref[:, AW + CW:D] = yp.astype(BF16)

    def cur(col):
        return pl.BlockSpec((tm, CW), lambda i: (i, col))

    def prev(col):
        return pl.BlockSpec((tm, CW), lambda i: (jnp.maximum(i - 1, 0), col))

    def whole(a):
        return pl.BlockSpec(a.shape, lambda i: (0,) * a.ndim)

    return pl.pallas_call(
        body, name=name, grid=(nb,),
        in_specs=[cur(6), cur(7), cur(8), cur(9), prev(7), prev(8), prev(9),
                  pl.BlockSpec((tm, AW), lambda i: (i, 0)), whole(cw), whole(pwbd), whole(ps)],
        out_specs=pl.BlockSpec((tm, D), lambda i: (i, 0)),
        out_shape=jax.ShapeDtypeStruct((s, D), BF16),
        compiler_params=_cp("parallel"),
    )(p, p, p, p, p, p, p, o, cw, pwbd, ps)


def _convpool_bwd(name, p, dmix, cw, pwbd, ps):
    s = p.shape[0]
    tm = CP_TM
    nb = s // tm

    def body(gb_ref, gc_ref, hin_ref, u_ref, gcp_ref, hinp_ref, up_ref, gbn_ref, dyc_ref, dyp_ref, dycn_ref, dypn_ref,
             cw_ref, pw_ref, ps_ref, dcp_ref, dw0_ref, dw1_ref, dw2_ref, dps_ref, dpw_ref):
        i = pl.program_id(0)
        has_prev = i > 0
        has_next = i < nb - 1
        w0, w1, w2 = cw_ref[0:1, :], cw_ref[1:2, :], cw_ref[2:3, :]
        gb, gc, hin = gb_ref[...], gc_ref[...], hin_ref[...]
        dyc = dyc_ref[...]
        z = gc * hin
        zp = jnp.where(has_prev, gcp_ref[...] * hinp_ref[...], 0.0)
        y3, z1, z2 = _conv_taps(z, zp, w0, w1, w2)
        dy3 = dyc * gb
        dy3n = jnp.where(has_next, dycn_ref[...] * gbn_ref[...], 0.0)
        dz = w2 * dy3 + w1 * _rows_after(dy3, dy3n, 1) + w0 * _rows_after(dy3, dy3n, 2)
        pw = pw_ref[...].astype(BF16)
        psv = ps_ref[...]
        m, cnt = _pool_mean_minus_token(u_ref[...], jnp.where(has_prev, up_ref[...], 0.0), i * tm)
        mb = m.astype(BF16)
        dyp = dyp_ref[...]
        dmp = (dyp * psv).astype(BF16)
        dmpn = jnp.where(has_next, dypn_ref[...] * psv, 0.0).astype(BF16)
        nt = (((1,), (1,)), ((), ()))
        dm = lax.dot_general(dmp, pw, nt, preferred_element_type=F32)
        dmn = lax.dot_general(dmpn, pw, nt, preferred_element_type=F32)
        lg, wv = _pool_window_lanes()
        cc, cn = dm / cnt, dmn / wv
        sums = []
        for k in (1, 2, 4, 8):
            c2 = cc + _rows_after(cc, cn, k)
            cn = cn + pltpu.roll(cn, tm - k, 0)
            cc = c2
            sums.append(cc)
        du = jnp.where(lg == 0, sums[0], jnp.where(lg == 1, sums[1], jnp.where(lg == 2, sums[2], sums[3]))) - dm
        dcp_ref[:, 0:CW] = (dyc * y3).astype(BF16)
        dcp_ref[:, CW:2 * CW] = (dz * hin).astype(BF16)
        dcp_ref[:, 2 * CW:3 * CW] = (dz * gc).astype(BF16)
        dcp_ref[:, 3 * CW:4 * CW] = du.astype(BF16)
        parts = (jnp.sum(dy3 * z2, axis=0, keepdims=True),
                 jnp.sum(dy3 * z1, axis=0, keepdims=True),
                 jnp.sum(dy3 * z, axis=0, keepdims=True),
                 jnp.sum(dyp * jnp.dot(mb, pw, preferred_element_type=F32), axis=0, keepdims=True),
                 lax.dot_general(mb, dmp, (((0,), (0,)), ((), ())), preferred_element_type=F32))
        accs = (dw0_ref, dw1_ref, dw2_ref, dps_ref, dpw_ref)

        @pl.when(i == 0)
        def _():
            for a, v in zip(accs, parts):
                a[...] = v

        @pl.when(i > 0)
        def _():
            for a, v in zip(accs, parts):
                a[...] += v

    def cur(col):
        return pl.BlockSpec((tm, CW), lambda i: (i, col))

    def prev(col):
        return pl.BlockSpec((tm, CW), lambda i: (jnp.maximum(i - 1, 0), col))

    def nxt(col):
        return pl.BlockSpec((tm, CW), lambda i: (jnp.minimum(i + 1, nb - 1), col))

    def whole(shape):
        return pl.BlockSpec(shape, lambda i: (0,) * len(shape))

    row = jax.ShapeDtypeStruct((1, CW), F32)
    return pl.pallas_call(
        body, name=name, grid=(nb,),
        in_specs=[cur(6), cur(7), cur(8), cur(9), prev(7), prev(8), prev(9), nxt(6),
                  cur(0), cur(1), nxt(0), nxt(1), whole(cw.shape), whole(pwbd.shape), whole(ps.shape)],
        out_specs=[pl.BlockSpec((tm, D), lambda i: (i, 0)), whole((1, CW)), whole((1, CW)), whole((1, CW)),
                   whole((1, PWD)), whole((PWD, PWD))],
        out_shape=[jax.ShapeDtypeStruct((s, D), BF16), row, row, row, row,
                   jax.ShapeDtypeStruct((PWD, PWD), F32)],
        compiler_params=_cp("arbitrary"),
    )(p, p, p, p, p, p, p, p, dmix, dmix, dmix, dmix, cw, pwbd, ps)


def _qkv_bwd(name, p, dq, dkp, dvp, dcp, qg, kg):
    s = p.shape[0]
    tm = 512
    off = PADR // tm

    def body(pq_ref, pk_ref, dq_ref, dk_ref, dv_ref, dcp_ref, qg_ref, kg_ref, dp_ref, dqg_ref, dkg_ref):
        i = pl.program_id(0)
        hm = _head_mean_matrix()

        def nrm_bwd(x, g, dy):
            r = lax.rsqrt(_head_mean(x * x, hm) + EPS)
            xn = x * r
            dxn = dy * g
            dx = r * (dxn - xn * _head_mean(dxn * xn, hm))
            dg = jnp.sum(dy * xn, axis=0, keepdims=True)
            dg = (dg[:, 0:128] + dg[:, 128:256]) + (dg[:, 256:384] + dg[:, 384:512])
            return dx, dg + pltpu.roll(dg, HD, 1)

        dxq, dgq = nrm_bwd(pq_ref[...], qg_ref[...], dq_ref[...] * 0.125)
        dxk, dgk = nrm_bwd(pk_ref[...], kg_ref[...], dk_ref[...])
        dp_ref[:, 0:AW] = dxq.astype(BF16)
        dp_ref[:, AW:2 * AW] = dxk.astype(BF16)
        dp_ref[:, 2 * AW:3 * AW] = dv_ref[...].astype(BF16)
        dp_ref[:, 3 * AW:DIN] = dcp_ref[...]

        @pl.when(i == 0)
        def _():
            dqg_ref[...] = dgq
            dkg_ref[...] = dgk

        @pl.when(i > 0)
        def _():
            dqg_ref[...] += dgq
            dkg_ref[...] += dgk

    gspec = pl.BlockSpec((1, AW), lambda i: (0, 0))
    gout = pl.BlockSpec((1, 128), lambda i: (0, 0))
    return pl.pallas_call(
        body, name=name, grid=(s // tm,),
        in_specs=[pl.BlockSpec((tm, AW), lambda i: (i, 0)), pl.BlockSpec((tm, AW), lambda i: (i, 1)),
                  pl.BlockSpec((tm, AW), lambda i: (i, 0)),
                  pl.BlockSpec((tm, AW), lambda i: (i + off, 0)),
                  pl.BlockSpec((tm, AW), lambda i: (i + off, 0)),
                  pl.BlockSpec((tm, D), lambda i: (i, 0)), gspec, gspec],
        out_specs=[pl.BlockSpec((tm, DIN), lambda i: (i, 0)), gout, gout],
        out_shape=[jax.ShapeDtypeStruct((s, DIN), BF16), jax.ShapeDtypeStruct((1, 128), F32),
                   jax.ShapeDtypeStruct((1, 128), F32)],
        compiler_params=_cp("arbitrary"),
    )(p, p, dq, dkp, dvp, dcp, qg, kg)


def _loss_grad(name, y, t):
    s = y.shape[0]
    tm = 512

    def body(y_ref, t_ref, dy_ref, dyb_ref, l_ref):
        i = pl.program_id(0)
        e = y_ref[...] - t_ref[...]
        dy = e * (1.0 / D)
        dy_ref[...] = dy
        dyb_ref[...] = dy.astype(BF16)
        part = 0.5 * jnp.sum(jnp.mean(e * e, axis=-1, keepdims=True), axis=0, keepdims=True)

        @pl.when(i == 0)
        def _():
            l_ref[...] = part

        @pl.when(i > 0)
        def _():
            l_ref[...] += part

    blk = pl.BlockSpec((tm, D), lambda i: (i, 0))
    return pl.pallas_call(
        body, name=name, grid=(s // tm,),
        in_specs=[blk, blk],
        out_specs=[blk, blk, pl.BlockSpec((1, 1), lambda i: (0, 0))],
        out_shape=[jax.ShapeDtypeStruct((s, D), F32), jax.ShapeDtypeStruct((s, D), BF16),
                   jax.ShapeDtypeStruct((1, 1), F32)],
        compiler_params=_cp("arbitrary"),
    )(y, t)


def _mm_nt_relu(name, dxb, w, l, a):
    s = dxb.shape[0]
    tm = MM_ROWS

    def body(d_ref, w_ref, a_ref, o_ref):
        df = lax.dot_general(d_ref[...], w_ref[...], NT_DIMS, preferred_element_type=F32)
        o_ref[...] = (df * (2.0 * jnp.maximum(a_ref[...].astype(F32), 0.0))).astype(BF16)

    return pl.pallas_call(
        body, name=name, grid=(s // tm,),
        in_specs=[pl.BlockSpec((tm, D), lambda i: (i, 0)),
                  pl.BlockSpec((None, DFF, D), lambda i: (l, 0, 0)),
                  pl.BlockSpec((tm, DFF), lambda i: (i, 0))],
        out_specs=pl.BlockSpec((tm, DFF), lambda i: (i, 0)),
        out_shape=jax.ShapeDtypeStruct((s, DFF), BF16),
        compiler_params=_cp("parallel"),
    )(dxb, w, a)


def _proj_out_bwd(name, dxb, w, l, mix):
    s = dxb.shape[0]
    tm = 512

    def body(d_ref, w_ref, o_ref, do_ref, dot_ref, dcp_ref, dl_ref):
        d = d_ref[...]
        wa, wc = w_ref[0:AW, :], w_ref[AW:D, :]
        do = lax.dot_general(d, wa, NT_DIMS, preferred_element_type=F32)
        do_ref[...] = do.astype(BF16)
        dot_ref[...] = lax.dot_general(wa, d, NT_DIMS, preferred_element_type=F32).astype(BF16)
        dcp_ref[...] = lax.dot_general(d, wc, NT_DIMS, preferred_element_type=F32)
        head = lax.broadcasted_iota(jnp.int32, (AW, 128), 0) // HD
        pick = jnp.where(head == lax.broadcasted_iota(jnp.int32, (AW, 128), 1), 1.0, 0.0).astype(BF16)
        dl_ref[...] = _two_pass_dot(do * o_ref[...].astype(F32), pick)

    return pl.pallas_call(
        body, name=name, grid=(s // tm,),
        in_specs=[pl.BlockSpec((tm, D), lambda i: (i, 0)),
                  pl.BlockSpec((None, D, D), lambda i: (l, 0, 0)),
                  pl.BlockSpec((tm, AW), lambda i: (i, 0))],
        out_specs=[pl.BlockSpec((tm, AW), lambda i: (i, 0)), pl.BlockSpec((AW, tm), lambda i: (0, i)),
                   pl.BlockSpec((tm, D - AW), lambda i: (i, 0)), pl.BlockSpec((tm, 128), lambda i: (i, 0))],
        out_shape=[jax.ShapeDtypeStruct((s, AW), BF16), jax.ShapeDtypeStruct((AW, s), BF16),
                   jax.ShapeDtypeStruct((s, D - AW), F32), jax.ShapeDtypeStruct((s, 128), F32)],
        compiler_params=_cp("parallel"),
    )(dxb, w, mix)


def _mm_nt_normbwd(name, gy, w, l, x, g, dres, dep):
    s, k = gy.shape
    tm = MM_ROWS

    def body(gy_ref, w_ref, x_ref, g_ref, dr_ref, dep_ref, dx_ref, dxb_ref, dg_ref):
        del dep_ref
        i = pl.program_id(0)
        dh = lax.dot_general(gy_ref[...], w_ref[...], NT_DIMS, preferred_element_type=F32)
        xv = x_ref[...]
        r = _inv_rms(xv)
        xn = xv * r
        dxn = dh * g_ref[...]
        dx = r * (dxn - xn * jnp.mean(dxn * xn, axis=-1, keepdims=True)) + dr_ref[...]
        dx_ref[...] = dx
        dxb_ref[...] = dx.astype(BF16)
        part = jnp.sum(dh * xn, axis=0, keepdims=True)

        @pl.when(i == 0)
        def _():
            dg_ref[...] = part

        @pl.when(i > 0)
        def _():
            dg_ref[...] += part

    blk = pl.BlockSpec((tm, D), lambda i: (i, 0))
    vec = pl.BlockSpec((1, D), lambda i: (0, 0))
    return pl.pallas_call(
        body, name=name, grid=(s // tm,),
        in_specs=[pl.BlockSpec((tm, k), lambda i: (i, 0)),
                  pl.BlockSpec((None, D, k), lambda i: (l, 0, 0)), blk, vec, blk, ANY],
        out_specs=[blk, blk, vec],
        out_shape=[jax.ShapeDtypeStruct((s, D), F32), jax.ShapeDtypeStruct((s, D), BF16),
                   jax.ShapeDtypeStruct((1, D), F32)],
        compiler_params=_cp("arbitrary"),
    )(gy, w, x, g, dres, dep)


def _mm_tn(name, a, b, tma, tnb, relu2=False):
    s, m = a.shape
    n = b.shape[1]

    def body(a_ref, b_ref, o_ref):
        av = _relu2(a_ref[...]) if relu2 else a_ref[...]
        o_ref[...] = lax.dot_general(av, b_ref[...], (((0,), (0,)), ((), ())),
                                     preferred_element_type=F32).astype(BF16)

    return pl.pallas_call(
        body, name=name, grid=(m // tma, n // tnb),
        in_specs=[pl.BlockSpec((s, tma), lambda i, j: (0, i)),
                  pl.BlockSpec((s, tnb), lambda i, j: (0, j))],
        out_specs=pl.BlockSpec((tma, tnb), lambda i, j: (i, j)),
        out_shape=jax.ShapeDtypeStruct((m, n), BF16),
        compiler_params=_cp("parallel", "parallel"),
    )(a, b)


def _adamw_math(gv, wv, mv, vv):
    mn = ADAM_B1 * mv + (1.0 - ADAM_B1) * gv
    vn = ADAM_B2 * vv + (1.0 - ADAM_B2) * jnp.square(gv)
    m_hat = mn / (1.0 - ADAM_B1 ** ADAM_STEP)
    v_hat = vn / (1.0 - ADAM_B2 ** ADAM_STEP)
    return gv, -ADAM_LR * (m_hat / (jnp.sqrt(v_hat) + ADAM_EPS) + ADAM_WD * wv), mn, vn


def _adamw(name, g, w, m, v):
    r, c = g.shape
    tm = 256 if r % 256 == 0 else r

    def body(g_ref, w_ref, m_ref, v_ref, go_ref, d_ref, mo_ref, vo_ref):
        go_ref[...], d_ref[...], mo_ref[...], vo_ref[...] = _adamw_math(g_ref[...], w_ref[...], m_ref[...], v_ref[...])

    blk = pl.BlockSpec((tm, c), lambda i: (i, 0))
    return pl.pallas_call(
        body, name=name, grid=(r // tm,),
        in_specs=[blk] * 4, out_specs=[blk] * 4,
        out_shape=[jax.ShapeDtypeStruct((r, c), F32)] * 4,
        compiler_params=_cp("parallel"),
    )(g, w, m, v)


def _place():
    x, y, c = lax.axis_index("x"), lax.axis_index("y"), lax.axis_index("c")
    chips = [(1 - x, y), (x, 1 - y), (1 - x, 1 - y)]
    return x, y, c, chips


BLOCK_AXIS = (2, 1, 2, 1)
LARGE_DIMS = ((D, DIN), (D, D), (D, DFF), (DFF, D))


def _full_shape(t, layers, dtype):
    r, c = LARGE_DIMS[t]
    return jax.ShapeDtypeStruct((layers, r, c), dtype)


def _cast_into_full(name, t, shard, b1, dep):
    _, r, c = shard.shape
    tm = min(512, r)
    if BLOCK_AXIS[t] == 1:
        out_spec = pl.BlockSpec((None, tm, c), lambda l, i, br: (l, br[0] * (r // tm) + i, 0))
    else:
        out_spec = pl.BlockSpec((None, tm, c), lambda l, i, br: (l, i, br[0]))

    def body(b_ref, x_ref, dep_ref, o_ref):
        del b_ref, dep_ref
        o_ref[...] = x_ref[...].astype(BF16)

    return pl.pallas_call(
        body, name=name,
        grid_spec=pltpu.PrefetchScalarGridSpec(
            num_scalar_prefetch=1, grid=(DEPTH, r // tm),
            in_specs=[pl.BlockSpec((None, tm, c), lambda l, i, br: (l, i, 0)), ANY],
            out_specs=out_spec),
        out_shape=_full_shape(t, DEPTH, BF16),
        compiler_params=_cp("parallel", "parallel"),
    )(b1, shard, dep)


HBM = pl.BlockSpec(memory_space=pltpu.HBM)
SEM = pl.BlockSpec(memory_space=pltpu.SEMAPHORE)
DATAFLOW = pltpu.SideEffectType.DATAFLOW_SIDE_EFFECTING


def _half(ref, l, t, b, c):
    r, cols = LARGE_DIMS[t]
    if BLOCK_AXIS[t] == 1:
        n = r // 8
        return ref.at[l, pl.ds(pl.multiple_of(b * (2 * n) + c * n, 16), n), :]
    n, w = r // 2, cols // 4
    return ref.at[l, pl.ds(pl.multiple_of(c * n, 16), n), pl.ds(pl.multiple_of(b * w, 128), w)]


def _gather_start(name, layers, ts, fulls):
    n = len(ts)

    def body(*refs):
        f_refs, sems = refs[n:2 * n], refs[2 * n:2 * n + 2 * len(layers)]
        x, y, c, chips = _place()
        for i, l in enumerate(layers):
            for k, t in enumerate(ts):
                own = _half(f_refs[k], l, t, 2 * x + y, c)
                for j, (cx, cy) in enumerate(chips):
                    pltpu.make_async_remote_copy(src_ref=own, dst_ref=own, send_sem=sems[2 * i].at[3 * t + j],
                                                 recv_sem=sems[2 * i + 1].at[3 * t + j], device_id=(cx, cy, c),
                                                 device_id_type=MESH).start()
        refs[-1][...] = jnp.zeros((8, 128), F32)

    outs = pl.pallas_call(
        body, name=name,
        in_specs=[HBM] * n,
        out_specs=[HBM] * n + [SEM] * (2 * len(layers)) + [pl.BlockSpec(memory_space=pltpu.VMEM)],
        out_shape=[pltpu.HBM(f.shape, f.dtype) for f in fulls]
        + [pltpu.SemaphoreType.DMA((12,))] * (2 * len(layers)) + [jax.ShapeDtypeStruct((8, 128), F32)],
        input_output_aliases={k: k for k in range(n)},
        compiler_params=pltpu.CompilerParams(has_side_effects=DATAFLOW),
    )(*[pltpu.with_memory_space_constraint(f, pltpu.HBM) for f in fulls])
    return outs[0:n], {l: (outs[n + 2 * i], outs[n + 1 + 2 * i]) for i, l in enumerate(layers)}, outs[-1]


def _gather_wait(name, l, ts, fulls, sems, after):
    def body(*refs):
        send_sems, recv_sems, f_refs = refs[4], refs[5], refs[7:11]
        x, y, c, chips = _place()
        for t in ts:
            own = _half(f_refs[t], l, t, 2 * x + y, c)
            for j, (cx, cy) in enumerate(chips):
                landed = _half(f_refs[t], l, t, 2 * cx + cy, c)
                pltpu.make_async_remote_copy(src_ref=own, dst_ref=landed, send_sem=send_sems.at[3 * t + j],
                                             recv_sem=recv_sems.at[3 * t + j], device_id=(cx, cy, c),
                                             device_id_type=MESH).wait()

    return pl.pallas_call(
        body, name=name,
        in_specs=[HBM] * 4 + [SEM, SEM, ANY], out_specs=[HBM] * 4,
        out_shape=[pltpu.HBM(s.shape, s.dtype) for s in (_full_shape(t, DEPTH, BF16) for t in range(4))],
        input_output_aliases={t: t for t in range(4)},
        compiler_params=pltpu.CompilerParams(has_side_effects=DATAFLOW),
    )(*fulls, sems[0], sems[1], after)


def _pass_on(name, l, ts, fulls):
    def body(*refs):
        f_refs, send_sems, recv_sems = refs[4:8], refs[8], refs[9]
        x, y, c, chips = _place()

        def copy(t, j, half):
            cx, cy = chips[j]
            part = _half(f_refs[t], l, t, 2 * cx + cy, half)
            return pltpu.make_async_remote_copy(src_ref=part, dst_ref=part, send_sem=send_sems.at[3 * t + j],
                                                recv_sem=recv_sems.at[3 * t + j], device_id=(x, y, 1 - c),
                                                device_id_type=MESH)

        for t in ts:
            for j in range(3):
                copy(t, j, c).start()
        for t in ts:
            for j in range(3):
                copy(t, j, 1 - c).wait_recv()
                copy(t, j, c).wait_send()

    return pl.pallas_call(
        body, name=name,
        in_specs=[ANY] * 4, out_specs=[ANY] * 4,
        out_shape=[_full_shape(t, DEPTH, BF16) for t in range(4)],
        input_output_aliases={t: t for t in range(4)},
        scratch_shapes=[pltpu.SemaphoreType.DMA((12,)), pltpu.SemaphoreType.DMA((12,))],
    )(*fulls)


def _block2d(ref, t, b):
    r, cols = LARGE_DIMS[t]
    if BLOCK_AXIS[t] == 1:
        return ref.at[pl.ds(pl.multiple_of(b * (r // 4), 16), r // 4), :]
    return ref.at[:, pl.ds(pl.multiple_of(b * (cols // 4), 128), cols // 4)]


def _block_dims(t):
    r, cols = LARGE_DIMS[t]
    return (r // 4, cols) if BLOCK_AXIS[t] == 1 else (r, cols // 4)


def _reduce_copies(ts, g_refs, r_refs, send_sems, recv_sems):
    _, _, c, chips = _place()
    return [pltpu.make_async_remote_copy(src_ref=_block2d(g_refs[i], t, 2 * cx + cy), dst_ref=r_refs[i].at[j],
                                         send_sem=send_sems.at[3 * i + j], recv_sem=recv_sems.at[3 * i + j],
                                         device_id=(cx, cy, c), device_id_type=MESH)
            for i, t in enumerate(ts) for j, (cx, cy) in enumerate(chips)]


def _reduce_start(name, ts, grads):
    n = len(ts)

    def body(*refs):
        for cp in _reduce_copies(ts, refs[n:2 * n], refs[2 * n:3 * n], refs[3 * n], refs[3 * n + 1]):
            cp.start()
        refs[3 * n + 2][...] = jnp.zeros((8, 128), F32)

    outs = pl.pallas_call(
        body, name=name,
        in_specs=[HBM] * n,
        out_specs=[HBM] * (2 * n) + [SEM, SEM, pl.BlockSpec(memory_space=pltpu.VMEM)],
        out_shape=[pltpu.HBM(g.shape, BF16) for g in grads]
        + [pltpu.HBM((3,) + _block_dims(t), BF16) for t in ts]
        + [pltpu.SemaphoreType.DMA((3 * n,)), pltpu.SemaphoreType.DMA((3 * n,)), jax.ShapeDtypeStruct((8, 128), F32)],
        input_output_aliases={i: i for i in range(n)},
        compiler_params=pltpu.CompilerParams(has_side_effects=DATAFLOW),
    )(*[pltpu.with_memory_space_constraint(g, pltpu.HBM) for g in grads])
    return outs[0:n], outs[n:2 * n], (outs[2 * n], outs[2 * n + 1]), outs[2 * n + 2]


def _reduce_wait(name, ts, grads, landing, sems, afters):
    n = len(ts)
    first_out = 2 * n + 2 + len(afters)

    def body(*refs):
        for cp in _reduce_copies(ts, refs[first_out:first_out + n], refs[first_out + n:first_out + 2 * n],
                                 refs[2 * n], refs[2 * n + 1]):
            cp.wait()

    outs = pl.pallas_call(
        body, name=name,
        in_specs=[HBM] * (2 * n) + [SEM, SEM] + [ANY] * len(afters), out_specs=[HBM] * (2 * n),
        out_shape=[pltpu.HBM(g.shape, BF16) for g in grads] + [pltpu.HBM(r.shape, BF16) for r in landing],
        input_output_aliases={i: i for i in range(2 * n)},
        compiler_params=pltpu.CompilerParams(has_side_effects=DATAFLOW),
    )(*grads, *landing, sems[0], sems[1], *afters)
    return outs[0:n], outs[n:2 * n]


def _add4(name, t, own, landed, b1):
    rb, cb = _block_dims(t)
    tm = min(512, rb)
    if BLOCK_AXIS[t] == 1:
        own_spec = pl.BlockSpec((tm, cb), lambda i, br: (br[0] * (rb // tm) + i, 0))
    else:
        own_spec = pl.BlockSpec((tm, cb), lambda i, br: (i, br[0]))

    def body(b_ref, o_ref, r0_ref, r1_ref, r2_ref, s_ref):
        del b_ref
        s_ref[...] = ((o_ref[...].astype(F32) + r0_ref[...].astype(F32))
                      + (r1_ref[...].astype(F32) + r2_ref[...].astype(F32))).astype(BF16)

    def got(j):
        return pl.BlockSpec((None, tm, cb), lambda i, br: (j, i, 0))

    return pl.pallas_call(
        body, name=name,
        grid_spec=pltpu.PrefetchScalarGridSpec(
            num_scalar_prefetch=1, grid=(rb // tm,),
            in_specs=[own_spec, got(0), got(1), got(2)],
            out_specs=pl.BlockSpec((tm, cb), lambda i, br: (i, 0))),
        out_shape=jax.ShapeDtypeStruct((rb, cb), BF16),
        compiler_params=_cp("parallel"),
    )(b1, own, landed, landed, landed)


def _swap_sib(name, sums):
    def body(*refs):
        s_refs, t_refs, send_sems, recv_sems = refs[0:4], refs[4:8], refs[8], refs[9]
        x, y, c, _ = _place()
        cps = [pltpu.make_async_remote_copy(src_ref=s_refs[t], dst_ref=t_refs[t], send_sem=send_sems.at[t],
                                            recv_sem=recv_sems.at[t], device_id=(x, y, 1 - c), device_id_type=MESH)
               for t in range(4)]
        for cp in cps:
            cp.start()
        for cp in cps:
            cp.wait()

    return pl.pallas_call(
        body, name=name,
        in_specs=[ANY] * 4, out_specs=[ANY] * 4,
        out_shape=[jax.ShapeDtypeStruct(s.shape, BF16) for s in sums],
        scratch_shapes=[pltpu.SemaphoreType.DMA((4,)), pltpu.SemaphoreType.DMA((4,))],
    )(*sums)


def _adamw_pair(name, l, s_own, s_sib, w, m, v, outs):
    rb, cb = s_own.shape
    tm = min(512, rb)

    def body(a_ref, b_ref, w_ref, m_ref, v_ref, g0, d0, m0, v0, go_ref, d_ref, mo_ref, vo_ref):
        del g0, d0, m0, v0
        gv = a_ref[...].astype(F32) + b_ref[...].astype(F32)
        go_ref[...], d_ref[...], mo_ref[...], vo_ref[...] = _adamw_math(gv, w_ref[...], m_ref[...], v_ref[...])

    part = pl.BlockSpec((tm, cb), lambda i: (i, 0))
    layer = pl.BlockSpec((None, tm, cb), lambda i: (l, i, 0))
    return pl.pallas_call(
        body, name=name, grid=(rb // tm,),
        in_specs=[part, part, layer, layer, layer] + [ANY] * 4,
        out_specs=[layer] * 4,
        out_shape=[jax.ShapeDtypeStruct((DEPTH, rb, cb), F32)] * 4,
        input_output_aliases={5 + i: i for i in range(4)},
        compiler_params=_cp("parallel"),
    )(s_own, s_sib, w, m, v, *outs)


def _all_gather8(name, v, dep):
    m_per, n = v.shape

    def body(v_ref, dep_ref, out_ref, send_sems, recv_sems, local_sem):
        del dep_ref
        x, y, c, chips = _place()
        me, sib = (x, y, c), (x, y, 1 - c)

        def rows(px, py, pc):
            return out_ref.at[pl.ds((4 * px + 2 * py + pc) * m_per, m_per), :]

        def copy(k, block, to, src=None):
            return pltpu.make_async_remote_copy(
                src_ref=rows(*block) if src is None else src, dst_ref=rows(*block),
                send_sem=send_sems.at[k], recv_sem=recv_sems.at[k], device_id=to, device_id_type=MESH)

        mine = pltpu.make_async_copy(v_ref, rows(*me), local_sem)
        mine.start()
        first = [copy(0, me, sib, src=v_ref)]
        first += [copy(1 + j, me, (*chip, c), src=v_ref) for j, chip in enumerate(chips)]
        for cp in first:
            cp.start()
        passed = [copy(4 + j, (*chip, c), sib) for j, chip in enumerate(chips)]
        for j, chip in enumerate(chips):
            copy(1 + j, (*chip, c), me).wait_recv()
            passed[j].start()
        copy(0, sib, me).wait_recv()
        for j, chip in enumerate(chips):
            copy(4 + j, (*chip, 1 - c), me).wait_recv()
        for cp in first + passed:
            cp.wait_send()
        mine.wait()

    return pl.pallas_call(
        body, name=name,
        out_shape=jax.ShapeDtypeStruct((8 * m_per, n), v.dtype),
        in_specs=[pl.BlockSpec(memory_space=pltpu.VMEM), ANY],
        out_specs=pl.BlockSpec(memory_space=pltpu.VMEM),
        scratch_shapes=[pltpu.SemaphoreType.DMA((7,)), pltpu.SemaphoreType.DMA((7,)), pltpu.SemaphoreType.DMA],
    )(v, dep)


def _sum8(name, g):
    def body(g_ref, o_ref):
        acc = g_ref[0]
        for d in range(1, 8):
            acc = acc + g_ref[d]
        o_ref[...] = acc

    return pl.pallas_call(body, name=name, out_shape=jax.ShapeDtypeStruct(g.shape[1:], F32))(g)


def _pack(parts):
    flat = []
    for a in parts:
        a = a.reshape(-1)
        flat.append(jnp.pad(a, (0, (-a.shape[0]) % 128)))
    cat = jnp.concatenate(flat)
    cat = jnp.pad(cat, (0, (-cat.shape[0]) % 1024))
    return cat.reshape(-1, 128)


def _unpack(packed, shapes):
    flat = packed.reshape(-1)
    out, at = [], 0
    for shp in shapes:
        n = 1
        for d in shp:
            n *= d
        out.append(flat[at:at + n].reshape(shp))
        at += n + (-n) % 128
    return out


def _local_step(x, target, layer_weights, on_grads, small):
    qg_all = jnp.tile(small["q_norm_g"], (1, 8))
    kg_all = jnp.tile(small["k_norm_g"], (1, 8))
    bias_all = _bias_layout(_bias_expand("bias_expand", jnp.pad(small["rel_bias"], ((0, 0), (0, 0), (0, NIDX - 257)))))
    same_group = jnp.eye(4, dtype=F32)[None, :, None, :, None]
    pwbd_all = (small["pool_w"][:, :, :, None, :] * same_group).reshape(DEPTH, PWD, PWD)
    saved = []
    xin = x
    h = _rmsnorm("norm_first", x, small["norm1_g"][0:1])
    for l in range(DEPTH):
        w_in = layer_weights(l, (0,), xin)[0]
        qg, kg, bias = qg_all[l:l + 1], kg_all[l:l + 1], bias_all[l]
        cw, pwbd, ps = small["conv_w"][l], pwbd_all[l], small["pool_scale"][l:l + 1]
        p = _mm_nn(f"proj_in_{l}", h, w_in, l, F32)
        q, qt, kp, kt, vp, vt = _qkv(f"qkv_{l}", p, qg, kg)
        o, lse = _attn_fwd(f"attn_fwd_{l}", kp, qt, vt, bias)
        w_in, w_out, w_1, w_2 = layer_weights(l, (1, 2, 3), o)
        mix = _convpool_fwd(f"convpool_fwd_{l}", p, o, cw, pwbd, ps)
        x1, h2 = _mm_res_norm(f"proj_out_{l}", mix, w_out, l, xin, small["norm2_g"][l:l + 1])
        gnext = small["norm1_g"][(l + 1) % DEPTH][None]
        a, x2, hnext = _mlp_fwd(f"mlp_{l}", h2, w_1, w_2, l, x1, gnext)
        saved.append(dict(xin=xin, h=h, p=p, q=q, qt=qt, kp=kp, kt=kt, vp=vp, bias=bias, mix=mix, x1=x1, h2=h2, a=a, lse=lse,
                          qg=qg, kg=kg, cw=cw, pwbd=pwbd, ps=ps))
        xin, h = x2, hnext

    dx, dxb, loss = _loss_grad("loss_grad", xin, target)
    raw = {k: [None] * DEPTH for k in ("dg1", "dqg", "dkg", "db", "dw0", "dw1", "dw2", "dpw", "dps", "dg2")}
    for l in reversed(range(DEPTH)):
        sv = saved[l]
        da = _mm_nt_relu(f"mlp2_bwd_{l}", dxb, w_2, l, sv["a"])
        g_2 = _mm_tn(f"mlp2_wgrad_{l}", sv["a"], dxb, 512, 1024, relu2=True)
        g_1 = _mm_tn(f"mlp1_wgrad_{l}", sv["h2"], da, 1024, 512)
        dep = on_grads(l, (2, 3), (g_1, g_2))
        dx1, dx1b, dg2 = _mm_nt_normbwd(f"mlp1_bwd_{l}", da, w_1, l, sv["x1"], small["norm2_g"][l:l + 1], dx, dep)
        do, dot, dmix, dl = _proj_out_bwd(f"proj_out_bwd_{l}", dx1b, w_out, l, sv["mix"])
        g_out = _mm_tn(f"proj_out_wgrad_{l}", sv["mix"], dx1b, 512, 1024)
        dcp, dw0, dw1, dw2, dps, dpw = _convpool_bwd(f"convpool_bwd_{l}", sv["p"], dmix, sv["cw"], sv["pwbd"], sv["ps"])
        dq, dkp, dvp, db = _attn_bwd(f"attn_bwd_{l}", sv["q"], sv["qt"], sv["kp"], sv["kt"], sv["vp"], sv["bias"],
                                     do, dot, sv["lse"], _rowsum_layout(dl, x.shape[0] // UNIT))
        dp, dqg, dkg = _qkv_bwd(f"qkv_bwd_{l}", sv["p"], dq, dkp, dvp, dcp, sv["qg"], sv["kg"])
        g_in = _mm_tn(f"proj_in_wgrad_{l}", sv["h"], dp, 1024, 640)
        dep = on_grads(l, (0, 1), (g_in, g_out))
        dx, dxb, dg1 = _mm_nt_normbwd(f"proj_in_bwd_{l}", dp, w_in, l, sv["xin"], small["norm1_g"][l:l + 1], dx1, dep)
        for k, val in dict(dg1=dg1, dqg=dqg, dkg=dkg, db=db, dw0=dw0, dw1=dw1, dw2=dw2, dpw=dpw, dps=dps, dg2=dg2).items():
            raw[k][l] = val
    cat = {k: jnp.concatenate(v, axis=0) for k, v in raw.items() if k not in ("db", "dpw")}
    drb = _bias_reduce("bias_reduce", _bias_unlayout(jnp.stack(raw["db"])))
    dpw = jnp.stack(raw["dpw"])
    gsmall = {
        "norm1_g": cat["dg1"], "q_norm_g": cat["dqg"][:, :HD], "k_norm_g": cat["dkg"][:, :HD],
        "rel_bias": drb[:, :, :257],
        "conv_w": jnp.stack([cat["dw0"], cat["dw1"], cat["dw2"]], axis=1),
        "pool_w": jnp.stack([dpw[:, g * 64:(g + 1) * 64, g * 64:(g + 1) * 64] for g in range(4)], axis=1),
        "pool_scale": cat["dps"], "norm2_g": cat["dg2"],
    }
    return loss, dx, gsmall


SMALL = ("norm1_g", "q_norm_g", "k_norm_g", "rel_bias", "conv_w", "pool_w", "pool_scale", "norm2_g")
LARGE = ("w_in", "w_out", "w_mlp1", "w_mlp2")


def kernel(x, norm1_g, w_in, q_norm_g, k_norm_g, rel_bias, conv_w, pool_w, pool_scale, w_out, norm2_g, w_mlp1, w_mlp2, loss_target, m_norm1_g, m_w_in, m_q_norm_g, m_k_norm_g, m_rel_bias, m_conv_w, m_pool_w, m_pool_scale, m_w_out, m_norm2_g, m_w_mlp1, m_w_mlp2, v_norm1_g, v_w_in, v_q_norm_g, v_k_norm_g, v_rel_bias, v_conv_w, v_pool_w, v_pool_scale, v_w_out, v_norm2_g, v_w_mlp1, v_w_mlp2):
    w = dict(norm1_g=norm1_g, w_in=w_in, q_norm_g=q_norm_g, k_norm_g=k_norm_g, rel_bias=rel_bias, conv_w=conv_w,
             pool_w=pool_w, pool_scale=pool_scale, w_out=w_out, norm2_g=norm2_g, w_mlp1=w_mlp1, w_mlp2=w_mlp2)
    m = dict(norm1_g=m_norm1_g, w_in=m_w_in, q_norm_g=m_q_norm_g, k_norm_g=m_k_norm_g, rel_bias=m_rel_bias,
             conv_w=m_conv_w, pool_w=m_pool_w, pool_scale=m_pool_scale, w_out=m_w_out, norm2_g=m_norm2_g,
             w_mlp1=m_w_mlp1, w_mlp2=m_w_mlp2)
    v = dict(norm1_g=v_norm1_g, w_in=v_w_in, q_norm_g=v_q_norm_g, k_norm_g=v_k_norm_g, rel_bias=v_rel_bias,
             conv_w=v_conv_w, pool_w=v_pool_w, pool_scale=v_pool_scale, w_out=v_w_out, norm2_g=v_norm2_g,
             w_mlp1=v_w_mlp1, w_mlp2=v_w_mlp2)
    ax, ay, ac = lax.axis_index("x"), lax.axis_index("y"), lax.axis_index("c")
    b1 = jnp.reshape(2 * ax + ay, (1,)).astype(jnp.int32)

    cw_rows = _all_gather8("gather_conv_w", jnp.pad(conv_w.reshape(DEPTH * 3, 64), ((0, 4), (0, 64))), b1)
    cw_chips = [cw_rows[(4 * cx + 2 * cy) * 16:(4 * cx + 2 * cy) * 16 + 12, :64] for cx in range(2) for cy in range(2)]
    small = {n: w[n] for n in SMALL}
    small["conv_w"] = jnp.concatenate(cw_chips, axis=1).reshape(DEPTH, 3, CW)

    (w_in_full,), in_sems, in_token = _gather_start(
        "gather_start_in", (0,), (0,), [_cast_into_full("cast_w_in", 0, w["w_in"], b1, cw_rows)])
    others, first_sems, first_token = _gather_start(
        "gather_start_first", (0,), (1, 2, 3),
        [_cast_into_full(f"cast_{LARGE[t]}", t, w[LARGE[t]], b1, in_token) for t in (1, 2, 3)])
    held = [[w_in_full] + list(others)]
    sems = {(0, 0): in_sems[0], (0, 1): first_sems[0]}

    def layer_weights(l, ts, after):
        if l > 0:
            ts = (0, 1, 2, 3) if ts == (0,) else ()
        if ts:
            tag = f"{l}_{ts[0]}"
            first_in = l == 0 and ts == (0,)
            after = first_token if first_in else after
            arrived = _gather_wait(f"gather_wait_{tag}", l, ts, held[0], sems[l, ts[0] if l == 0 else 0], after)
            if first_in:
                arrived, rest_sems, _ = _gather_start("gather_start_rest", tuple(range(1, DEPTH)), (0, 1, 2, 3),
                                                      arrived)
                sems.update({(k, 0): v for k, v in rest_sems.items()})
            held[0] = _pass_on(f"pass_on_{tag}", l, ts, arrived)
        return held[0]

    flights = {}

    def await_flight(l, ts, afters):
        g, landing, sm, _ = flights[l, ts]
        flights[l, ts] = _reduce_wait(f"reduce_wait_{l}_{ts[0]}", ts, g, landing, sm, afters)

    def on_grads(l, ts, grads):
        if ts == (0, 1) and l + 1 < DEPTH:
            await_flight(l + 1, (2, 3), [grads[0]])
            await_flight(l + 1, (0, 1), [grads[0]])
        flights[l, ts] = _reduce_start(f"reduce_start_{l}_{ts[0]}", ts, grads)
        return flights[l, ts][3]

    loss_part, grad_x, gsmall = _local_step(x[0], loss_target[0], layer_weights, on_grads, small)
    loss = lax.psum(loss_part[0, 0], ("x", "y", "c"))
    order = [n for n in SMALL]
    packed = _pack([gsmall[n] for n in order])

    out = {n: [lax.empty(w[n].shape, F32) for _ in range(4)] for n in LARGE}
    for l in reversed(range(DEPTH)):
        if l == 0:
            afters = [grad_x, packed] + [out[n][0] for n in LARGE]
            await_flight(0, (2, 3), afters)
            await_flight(0, (0, 1), afters)
        sums = [None] * 4
        for ts in ((0, 1), (2, 3)):
            g, landing = flights[l, ts]
            for i, t in enumerate(ts):
                sums[t] = _add4(f"add4_{LARGE[t]}_{l}", t, g[i], landing[i], b1)
        theirs = _swap_sib(f"swap_sib_{l}", sums)
        for t, n in enumerate(LARGE):
            out[n] = _adamw_pair(f"adamw_{n}_{l}", l, sums[t], theirs[t], w[n], m[n], v[n], out[n])

    rows = packed.shape[0]
    summed = _sum8("sum_small", _all_gather8("gather_small", packed, out[LARGE[0]][0]).reshape(8, rows, 128))
    gfull = dict(zip(order, _unpack(summed, [gsmall[n].shape for n in order])))
    gfull["conv_w"] = lax.dynamic_slice_in_dim(gfull["conv_w"], (2 * ax + ay) * 64, 64, axis=2)
    res = _adamw("adamw_small", _pack([gfull[n] for n in order]), _pack([w[n] for n in order]),
                 _pack([m[n] for n in order]), _pack([v[n] for n in order]))
    for n, parts in zip(order, zip(*[_unpack(r, [w[k].shape for k in order]) for r in res])):
        out[n] = list(parts)

    names = ("norm1_g", "w_in", "q_norm_g", "k_norm_g", "rel_bias", "conv_w", "pool_w", "pool_scale", "w_out",
             "norm2_g", "w_mlp1", "w_mlp2")
    flat = [loss, grad_x[None]]
    for i in range(4):
        flat += [out[n][i] for n in names]
    return tuple(flat)
```

```python
import functools

import jax
import jax.numpy as jnp
from jax import lax
from jax.experimental import pallas as pl
from jax.experimental.pallas import tpu as pltpu

F32 = jnp.float32
BF16 = jnp.bfloat16

D = 1024
DEPTH = 4
CH = 64
NPREV = 8
KB = (NPREV + 1) * CH
PADR = NPREV * CH
HD = 64
AW = 512
CW = 256
PWD = 256
DIN = 3 * AW + 3 * CW + PWD
DFF = 4 * D
NIDX = 384
EPS = 1e-6
NEG_INF = -1e30

ADAM_LR = 0.001
ADAM_B1 = 0.9
ADAM_B2 = 0.999
ADAM_EPS = 1e-08
ADAM_WD = 0.01
ADAM_STEP = 10

VMEM_LIMIT = 52 * 1024 * 1024
MM_ROWS = 512
MESH = pl.DeviceIdType.MESH
ANY = pl.BlockSpec(memory_space=pl.ANY)


def _cp(*sem):
    return pltpu.CompilerParams(dimension_semantics=sem, vmem_limit_bytes=VMEM_LIMIT)


def _inv_rms(x):
    return lax.rsqrt(jnp.mean(x * x, axis=-1, keepdims=True) + EPS)


def _head_mean_matrix():
    r = lax.broadcasted_iota(jnp.int32, (AW, AW), 0) // HD
    c = lax.broadcasted_iota(jnp.int32, (AW, AW), 1) // HD
    return jnp.where(r == c, 1.0 / HD, 0.0).astype(BF16)


def _two_pass_dot(x, m):
    hi = x.astype(BF16)
    lo = (x - hi.astype(F32)).astype(BF16)
    return (jnp.dot(hi, m, preferred_element_type=F32)
            + jnp.dot(lo, m, preferred_element_type=F32))


def _head_mean(x, hm):
    return _two_pass_dot(x, hm)


def _rmsnorm(name, x, g):
    s = x.shape[0]
    tm = 512

    def body(x_ref, g_ref, h_ref):
        xv = x_ref[...]
        h_ref[...] = (xv * _inv_rms(xv) * g_ref[...]).astype(BF16)

    return pl.pallas_call(
        body, name=name, grid=(s // tm,),
        in_specs=[pl.BlockSpec((tm, D), lambda i: (i, 0)), pl.BlockSpec((1, D), lambda i: (0, 0))],
        out_specs=pl.BlockSpec((tm, D), lambda i: (i, 0)),
        out_shape=jax.ShapeDtypeStruct((s, D), BF16),
        compiler_params=_cp("parallel"),
    )(x, g)


def _relu2(a):
    r = jnp.maximum(a, jnp.zeros_like(a))
    return r * r


def _mm_nn(name, a, w, l, out_dtype):
    s, k = a.shape
    n = w.shape[2]
    tm = MM_ROWS

    def body(a_ref, w_ref, o_ref):
        o_ref[...] = jnp.dot(a_ref[...], w_ref[...], preferred_element_type=F32).astype(o_ref.dtype)

    return pl.pallas_call(
        body, name=name, grid=(s // tm,),
        in_specs=[pl.BlockSpec((tm, k), lambda i: (i, 0)),
                  pl.BlockSpec((None, k, n), lambda i: (l, 0, 0))],
        out_specs=pl.BlockSpec((tm, n), lambda i: (i, 0)),
        out_shape=jax.ShapeDtypeStruct((s, n), out_dtype),
        compiler_params=_cp("parallel"),
    )(a, w)


def _mm_res_norm(name, a, w, l, res, g):
    s, k = a.shape
    tm = MM_ROWS

    def body(a_ref, w_ref, r_ref, g_ref, x_ref, h_ref):
        acc = r_ref[...] + jnp.dot(a_ref[...], w_ref[...], preferred_element_type=F32)
        x_ref[...] = acc
        h_ref[...] = (acc * _inv_rms(acc) * g_ref[...]).astype(BF16)

    return pl.pallas_call(
        body, name=name, grid=(s // tm,),
        in_specs=[pl.BlockSpec((tm, k), lambda i: (i, 0)),
                  pl.BlockSpec((None, k, D), lambda i: (l, 0, 0)),
                  pl.BlockSpec((tm, D), lambda i: (i, 0)),
                  pl.BlockSpec((1, D), lambda i: (0, 0))],
        out_specs=[pl.BlockSpec((tm, D), lambda i: (i, 0))] * 2,
        out_shape=[jax.ShapeDtypeStruct((s, D), F32), jax.ShapeDtypeStruct((s, D), BF16)],
        compiler_params=_cp("parallel"),
    )(a, w, res, g)


def _mlp_fwd(name, h2, w1, w2, l, res, g):
    s = h2.shape[0]
    tm = 256

    def body(h_ref, w1_ref, w2_ref, r_ref, g_ref, a_ref, x_ref, hn_ref):
        a = jnp.dot(h_ref[...], w1_ref[...], preferred_element_type=F32).astype(BF16)
        a_ref[...] = a
        acc = r_ref[...] + jnp.dot(_relu2(a), w2_ref[...], preferred_element_type=F32)
        x_ref[...] = acc
        hn_ref[...] = (acc * _inv_rms(acc) * g_ref[...]).astype(BF16)

    once = pl.Buffered(1)
    rows = pl.BlockSpec((tm, D), lambda i: (i, 0))
    return pl.pallas_call(
        body, name=name, grid=(s // tm,),
        in_specs=[rows,
                  pl.BlockSpec((None, D, DFF), lambda i: (l, 0, 0), pipeline_mode=once),
                  pl.BlockSpec((None, DFF, D), lambda i: (l, 0, 0), pipeline_mode=once),
                  rows, pl.BlockSpec((1, D), lambda i: (0, 0))],
        out_specs=[pl.BlockSpec((tm, DFF), lambda i: (i, 0)), rows, rows],
        out_shape=[jax.ShapeDtypeStruct((s, DFF), BF16), jax.ShapeDtypeStruct((s, D), F32),
                   jax.ShapeDtypeStruct((s, D), BF16)],
        compiler_params=_cp("parallel"),
    )(h2, w1, w2, res, g)


def _qkv(name, p, qg, kg):
    s = p.shape[0]
    tm = PADR
    nb = s // tm

    def body(pq_ref, pk_ref, pv_ref, qg_ref, kg_ref, q_ref, qt_ref, k_ref, kt_ref, v_ref, vt_ref):
        t = pl.program_id(0)
        hm = _head_mean_matrix()

        def nrm(x, g):
            return x * lax.rsqrt(_head_mean(x * x, hm) + EPS) * g

        first = t == 0
        qq = nrm(pq_ref[...], qg_ref[...]) * 0.125
        kk = jnp.where(first, 0.0, nrm(pk_ref[...], kg_ref[...]))
        vv = jnp.where(first, 0.0, pv_ref[...])
        q_ref[...] = qq.astype(BF16)
        qt_ref[...] = qq.T.astype(BF16)
        k_ref[...] = kk.astype(BF16)
        kt_ref[...] = kk.T.astype(BF16)
        v_ref[...] = vv.astype(BF16)
        vt_ref[...] = vv.T.astype(BF16)

    def src(col):
        return pl.BlockSpec((tm, AW), lambda t: (jnp.maximum(t - 1, 0), col))

    gspec = pl.BlockSpec((1, AW), lambda t: (0, 0))
    rows = pl.BlockSpec((tm, AW), lambda t: (t, 0))
    cols = pl.BlockSpec((AW, tm), lambda t: (0, t))
    return pl.pallas_call(
        body, name=name, grid=(nb + 1,),
        in_specs=[src(0), src(1), src(2), gspec, gspec],
        out_specs=[pl.BlockSpec((tm, AW), lambda t: (jnp.maximum(t - 1, 0), 0)),
                   pl.BlockSpec((AW, tm), lambda t: (0, jnp.maximum(t - 1, 0))),
                   rows, cols, rows, cols],
        out_shape=[jax.ShapeDtypeStruct((s, AW), BF16), jax.ShapeDtypeStruct((AW, s), BF16),
                   jax.ShapeDtypeStruct((s + PADR, AW), BF16), jax.ShapeDtypeStruct((AW, s + PADR), BF16),
                   jax.ShapeDtypeStruct((s + PADR, AW), BF16), jax.ShapeDtypeStruct((AW, s + PADR), BF16)],
        compiler_params=_cp("arbitrary"),
    )(p, p, p, qg, kg)


NBAND = KB // CH
HIGHEST = lax.Precision.HIGHEST
NT_DIMS = (((1,), (1,)), ((), ()))


def _onehot_table(a):
    m = lax.broadcasted_iota(jnp.int32, (128, NIDX), 0)
    idx = lax.broadcasted_iota(jnp.int32, (128, NIDX), 1)
    rel = jnp.clip(KB - 1 - (CH * a + m), -128, 128) + 128
    return jnp.where(rel == idx, 1.0, 0.0).astype(F32)


def _onehot_diagonal():
    r = lax.broadcasted_iota(jnp.int32, (CH * CH, 128), 0)
    m = lax.broadcasted_iota(jnp.int32, (CH * CH, 128), 1)
    return jnp.where((r % CH) - (r // CH) + (CH - 1) == m, 1.0, 0.0).astype(F32)


def _bias_expand(name, rb):
    def body(rb_ref, o_ref):
        along = [lax.dot_general(rb_ref[...], _onehot_table(a), NT_DIMS, preferred_element_type=F32,
                                 precision=HIGHEST) for a in range(NBAND)]
        o_ref[...] = lax.dot_general(jnp.concatenate(along, axis=0), _onehot_diagonal(), NT_DIMS,
                                     preferred_element_type=F32, precision=HIGHEST)

    return pl.pallas_call(
        body, name=name, grid=(DEPTH,),
        in_specs=[pl.BlockSpec((None, 8, NIDX), lambda l: (l, 0, 0))],
        out_specs=pl.BlockSpec((None, NBAND * 8, CH * CH), lambda l: (l, 0, 0)),
        out_shape=jax.ShapeDtypeStruct((DEPTH, NBAND * 8, CH * CH), F32),
        compiler_params=_cp("parallel"),
    )(rb)


def _bias_reduce(name, db):
    def body(db_ref, o_ref):
        along = jnp.dot(db_ref[...], _onehot_diagonal(), preferred_element_type=F32, precision=HIGHEST)
        acc = jnp.zeros((8, NIDX), F32)
        for a in range(NBAND):
            acc = acc + jnp.dot(along[8 * a:8 * a + 8, :], _onehot_table(a), preferred_element_type=F32,
                                precision=HIGHEST)
        o_ref[...] = acc

    return pl.pallas_call(
        body, name=name, grid=(DEPTH,),
        in_specs=[pl.BlockSpec((None, NBAND * 8, CH * CH), lambda l: (l, 0, 0))],
        out_specs=pl.BlockSpec((None, 8, NIDX), lambda l: (l, 0, 0)),
        out_shape=jax.ShapeDtypeStruct((DEPTH, 8, NIDX), F32),
        compiler_params=_cp("parallel"),
    )(db)


def _bias_layout(flat):
    b = flat.reshape(DEPTH, NBAND, 8, CH, CH).transpose(0, 2, 1, 4, 3).reshape(DEPTH, 4, 2, KB, CH)
    pair = b.transpose(0, 1, 3, 2, 4).reshape(DEPTH, 4, KB, 128)
    first = jnp.pad(pair, ((0, 0), (0, 0), (0, CH), (0, 0)), constant_values=NEG_INF)
    second = jnp.pad(pair, ((0, 0), (0, 0), (CH, 0), (0, 0)), constant_values=NEG_INF)
    return jnp.concatenate([first, second], axis=3)


def _bias_unlayout(dbt):
    b = dbt.reshape(DEPTH, 4, NBAND, CH, 2, CH)
    return b.transpose(0, 2, 1, 4, 5, 3).reshape(DEPTH, NBAND * 8, CH * CH)


UNIT = 2 * CH
BAND2 = KB + CH


def _pair_weights(xt):
    x = xt.astype(F32)
    row = lax.broadcasted_iota(jnp.int32, (128, UNIT), 0)
    low = lax.broadcasted_iota(jnp.int32, (128, UNIT), 1) < HD
    swapped = pltpu.roll(x, HD, 1)
    same = (row < HD) == low
    first = jnp.where(same, jnp.where(low, x, swapped), 0.0)
    second = jnp.where(same, jnp.where(low, swapped, x), 0.0)
    return jnp.concatenate([first, second], axis=1).astype(BF16)


def _pair_rows(x):
    low = lax.broadcasted_iota(jnp.int32, (CH, 128), 1) < HD
    zero = jnp.zeros((CH, 128), x.dtype)
    parts = []
    for c in range(2):
        xc = x[c * CH:(c + 1) * CH, :]
        parts += [jnp.where(low, xc, zero), jnp.where(low, zero, xc)]
    return jnp.concatenate(parts, axis=0)


def _unpair(raw):
    b0, b1 = raw[:, 0:128], raw[:, 128:256]
    row = lax.broadcasted_iota(jnp.int32, (128, 128), 0)
    low = lax.broadcasted_iota(jnp.int32, (128, 128), 1) < HD
    top = jnp.where(low, b0, pltpu.roll(b1, HD, 1))
    bottom = jnp.where(low, pltpu.roll(b0, HD, 1), b1)
    return jnp.where(row < HD, top, bottom).T


def _scores_t(kb, qw, bias2, row0, padded):
    s = jnp.dot(kb, qw, preferred_element_type=F32) + bias2
    if padded:
        s = jnp.where(row0 + lax.broadcasted_iota(jnp.int32, (BAND2, 256), 0) >= PADR, s, NEG_INF)
    return s


def _unit_loops(s, unit):
    lax.fori_loop(0, PADR // UNIT, lambda u, c: unit(u, True, c), 0, unroll=2)
    lax.fori_loop(PADR // UNIT, s // UNIT, lambda u, c: unit(u, False, c), 0, unroll=7)


def _attn_fwd(name, kp, qt, vt, bias2, l):
    s = qt.shape[1]
    nu = s // UNIT

    def body(k_ref, qt_ref, vt_ref, b_ref, o_ref, lse_ref):
        def unit(u, padded, carry):
            r0 = pl.multiple_of(u * UNIT, UNIT)
            sc = _scores_t(k_ref[pl.ds(r0, BAND2), :], _pair_weights(qt_ref[:, pl.ds(r0, UNIT)]), b_ref[...],
                           r0, padded)
            top = jnp.max(sc, axis=0, keepdims=True)
            e = jnp.exp(sc - top)
            total = jnp.sum(e, axis=0, keepdims=True)
            raw = jnp.dot(vt_ref[:, pl.ds(r0, BAND2)], e.astype(BF16), preferred_element_type=F32) * (1.0 / total)
            o_ref[pl.ds(r0, UNIT), :] = _unpair(raw).astype(BF16)
            lse_ref[u] = jnp.broadcast_to(top + jnp.log(total), (8, 256))
            return carry

        _unit_loops(s, unit)

    return pl.pallas_call(
        body, name=name, grid=(AW // 128,),
        in_specs=[pl.BlockSpec((s + PADR, 128), lambda h: (0, h)),
                  pl.BlockSpec((128, s), lambda h: (h, 0)),
                  pl.BlockSpec((128, s + PADR), lambda h: (h, 0)),
                  pl.BlockSpec((None, None, BAND2, 256), lambda h: (l, h, 0, 0))],
        out_specs=[pl.BlockSpec((s, 128), lambda h: (0, h)),
                   pl.BlockSpec((None, nu, 8, 256), lambda h: (h, 0, 0, 0))],
        out_shape=[jax.ShapeDtypeStruct((s, AW), BF16), jax.ShapeDtypeStruct((4, nu, 8, 256), F32)],
        compiler_params=_cp("parallel"),
    )(kp, qt, vt, bias2)


def _attn_bwd(name, q, qt, kp, kt, vp, bias2, l, do, dot, lse, dl):
    s = q.shape[0]
    nu = s // UNIT

    def body(q_ref, qt_ref, k_ref, kt_ref, v_ref, b_ref, do_ref, dot_ref, lse_ref, dl_ref,
             dq_ref, dk_ref, dv_ref, db_ref):
        dk_ref[...] = jnp.zeros_like(dk_ref)
        dv_ref[...] = jnp.zeros_like(dv_ref)
        db_ref[...] = jnp.zeros_like(db_ref)

        def unit(u, padded, carry):
            r0 = pl.multiple_of(u * UNIT, UNIT)
            rows, band = pl.ds(r0, UNIT), pl.ds(r0, BAND2)
            sc = _scores_t(k_ref[band, :], _pair_weights(qt_ref[:, rows]), b_ref[...], r0, padded)
            pt = jnp.exp(sc - lse_ref[u][0:1, :])
            dpt = jnp.dot(v_ref[band, :], _pair_weights(dot_ref[:, rows]), preferred_element_type=F32)
            ds = pt * (dpt - dl_ref[u][0:1, :])
            db_ref[...] += ds[0:KB, 0:128] + ds[CH:BAND2, 128:256]
            dsb = ds.astype(BF16)
            dq_ref[rows, :] = _unpair(jnp.dot(kt_ref[:, band], dsb, preferred_element_type=F32))
            dk_ref[band, :] += jnp.dot(dsb, _pair_rows(q_ref[rows, :]), preferred_element_type=F32)
            dv_ref[band, :] += jnp.dot(pt.astype(BF16), _pair_rows(do_ref[rows, :]), preferred_element_type=F32)
            return carry

        _unit_loops(s, unit)

    row_q = pl.BlockSpec((s, 128), lambda h: (0, h))
    col_q = pl.BlockSpec((128, s), lambda h: (h, 0))
    row_k = pl.BlockSpec((s + PADR, 128), lambda h: (0, h))
    col_k = pl.BlockSpec((128, s + PADR), lambda h: (h, 0))
    stat = pl.BlockSpec((None, nu, 8, 256), lambda h: (h, 0, 0, 0))
    return pl.pallas_call(
        body, name=name, grid=(AW // 128,),
        in_specs=[row_q, col_q, row_k, col_k, row_k,
                  pl.BlockSpec((None, None, BAND2, 256), lambda h: (l, h, 0, 0)), row_q, col_q, stat, stat],
        out_specs=[row_q, row_k, row_k, pl.BlockSpec((None, KB, 128), lambda h: (h, 0, 0))],
        out_shape=[jax.ShapeDtypeStruct((s, AW), F32),
                   jax.ShapeDtypeStruct((s + PADR, AW), F32),
                   jax.ShapeDtypeStruct((s + PADR, AW), F32),
                   jax.ShapeDtypeStruct((4, KB, 128), F32)],
        compiler_params=_cp("parallel"),
    )(q, qt, kp, kt, vp, bias2, do, dot, lse, dl)


def _rowsum_layout(dl, nu):
    d = dl[:, :8].reshape(nu, 2, CH, 4, 2)
    d = d.transpose(3, 0, 1, 4, 2).reshape(4, nu, 1, 256)
    return jnp.broadcast_to(d, (4, nu, 8, 256))


def _rows_before(cur, prev, k):
    row = lax.broadcasted_iota(jnp.int32, cur.shape, 0)
    return jnp.where(row >= k, pltpu.roll(cur, k, 0), pltpu.roll(prev, k, 0))


def _rows_after(cur, nxt, k):
    n = cur.shape[0]
    row = lax.broadcasted_iota(jnp.int32, cur.shape, 0)
    return jnp.where(row < n - k, pltpu.roll(cur, n - k, 0), pltpu.roll(nxt, n - k, 0))


def _pool_window_lanes():
    lg = lax.broadcasted_iota(jnp.int32, (1, PWD), 1) // 64
    return lg, jnp.where(lg == 0, 2.0, jnp.where(lg == 1, 4.0, jnp.where(lg == 2, 8.0, 16.0))).astype(F32)


def _pool_mean_minus_token(u, up, row0):
    lg, wv = _pool_window_lanes()
    sums = []
    c, p = u, up
    for k in (1, 2, 4, 8):
        c2 = c + _rows_before(c, p, k)
        p = p + pltpu.roll(p, k, 0)
        c = c2
        sums.append(c)
    win = jnp.where(lg == 0, sums[0], jnp.where(lg == 1, sums[1], jnp.where(lg == 2, sums[2], sums[3])))
    pos1 = (row0 + lax.broadcasted_iota(jnp.int32, u.shape, 0) + 1).astype(F32)
    cnt = jnp.minimum(pos1, wv)
    return win / cnt - u, cnt


def _conv_taps(z, zp, w0, w1, w2):
    z1 = _rows_before(z, zp, 1)
    z2 = _rows_before(z, zp, 2)
    return (w0 * z2 + w1 * z1) + w2 * z, z1, z2


CP_TM = 512


def _convpool_fwd(name, p, o, cw, pwbd, ps):
    s = p.shape[0]
    tm = CP_TM
    nb = s // tm

    def body(gb_ref, gc_ref, hin_ref, u_ref, gcp_ref, hinp_ref, up_ref, o_ref, cw_ref, pw_ref, ps_ref, mix_ref):
        i = pl.program_id(0)
        has_prev = i > 0
        z = gc_ref[...] * hin_ref[...]
        zp = jnp.where(has_prev, gcp_ref[...] * hinp_ref[...], 0.0)
        y3, _, _ = _conv_taps(z, zp, cw_ref[0:1, :], cw_ref[1:2, :], cw_ref[2:3, :])
        m, _ = _pool_mean_minus_token(u_ref[...], jnp.where(has_prev, up_ref[...], 0.0), i * tm)
        yp = jnp.dot(m.astype(BF16), pw_ref[...].astype(BF16), preferred_element_type=F32) * ps_ref[...]
        mix_ref[:, 0:AW] = o_ref[...]
        mix_ref[:, AW:AW + CW] = (gb_ref[...] * y3).astype(BF16)
        mix_ref[:, AW + CW:D] = yp.astype(BF16)

    def cur(col):
        return pl.BlockSpec((tm, CW), lambda i: (i, col))

    def prev(col):
        return pl.BlockSpec((tm, CW), lambda i: (jnp.maximum(i - 1, 0), col))

    def whole(a):
        return pl.BlockSpec(a.shape, lambda i: (0,) * a.ndim)

    return pl.pallas_call(
        body, name=name, grid=(nb,),
        in_specs=[cur(6), cur(7), cur(8), cur(9), prev(7), prev(8), prev(9),
                  pl.BlockSpec((tm, AW), lambda i: (i, 0)), whole(cw), whole(pwbd), whole(ps)],
        out_specs=pl.BlockSpec((tm, D), lambda i: (i, 0)),
        out_shape=jax.ShapeDtypeStruct((s, D), BF16),
        compiler_params=_cp("parallel"),
    )(p, p, p, p, p, p, p, o, cw, pwbd, ps)


def _convpool_bwd(name, p, dmix, cw, pwbd, ps):
    s = p.shape[0]
    tm = CP_TM
    nb = s // tm

    def body(gb_ref, gc_ref, hin_ref, u_ref, gcp_ref, hinp_ref, up_ref, gbn_ref, dyc_ref, dyp_ref, dycn_ref, dypn_ref,
             cw_ref, pw_ref, ps_ref, dcp_ref, dw0_ref, dw1_ref, dw2_ref, dps_ref, dpw_ref):
        i = pl.program_id(0)
        has_prev = i > 0
        has_next = i < nb - 1
        w0, w1, w2 = cw_ref[0:1, :], cw_ref[1:2, :], cw_ref[2:3, :]
        gb, gc, hin = gb_ref[...], gc_ref[...], hin_ref[...]
        dyc = dyc_ref[...]
        z = gc * hin
        zp = jnp.where(has_prev, gcp_ref[...] * hinp_ref[...], 0.0)
        y3, z1, z2 = _conv_taps(z, zp, w0, w1, w2)
        dy3 = dyc * gb
        dy3n = jnp.where(has_next, dycn_ref[...] * gbn_ref[...], 0.0)
        dz = w2 * dy3 + w1 * _rows_after(dy3, dy3n, 1) + w0 * _rows_after(dy3, dy3n, 2)
        pw = pw_ref[...].astype(BF16)
        psv = ps_ref[...]
        m, cnt = _pool_mean_minus_token(u_ref[...], jnp.where(has_prev, up_ref[...], 0.0), i * tm)
        mb = m.astype(BF16)
        dyp = dyp_ref[...]
        dmp = (dyp * psv).astype(BF16)
        dmpn = jnp.where(has_next, dypn_ref[...] * psv, 0.0).astype(BF16)
        nt = (((1,), (1,)), ((), ()))
        dm = lax.dot_general(dmp, pw, nt, preferred_element_type=F32)
        dmn = lax.dot_general(dmpn, pw, nt, preferred_element_type=F32)
        lg, wv = _pool_window_lanes()
        cc, cn = dm / cnt, dmn / wv
        sums = []
        for k in (1, 2, 4, 8):
            c2 = cc + _rows_after(cc, cn, k)
            cn = cn + pltpu.roll(cn, tm - k, 0)
            cc = c2
            sums.append(cc)
        du = jnp.where(lg == 0, sums[0], jnp.where(lg == 1, sums[1], jnp.where(lg == 2, sums[2], sums[3]))) - dm
        dcp_ref[:, 0:CW] = (dyc * y3).astype(BF16)
        dcp_ref[:, CW:2 * CW] = (dz * hin).astype(BF16)
        dcp_ref[:, 2 * CW:3 * CW] = (dz * gc).astype(BF16)
        dcp_ref[:, 3 * CW:4 * CW] = du.astype(BF16)
        parts = (jnp.sum(dy3 * z2, axis=0, keepdims=True),
                 jnp.sum(dy3 * z1, axis=0, keepdims=True),
                 jnp.sum(dy3 * z, axis=0, keepdims=True),
                 jnp.sum(dyp * jnp.dot(mb, pw, preferred_element_type=F32), axis=0, keepdims=True),
                 lax.dot_general(mb, dmp, (((0,), (0,)), ((), ())), preferred_element_type=F32))
        accs = (dw0_ref, dw1_ref, dw2_ref, dps_ref, dpw_ref)

        @pl.when(i == 0)
        def _():
            for a, v in zip(accs, parts):
                a[...] = v

        @pl.when(i > 0)
        def _():
            for a, v in zip(accs, parts):
                a[...] += v

    def cur(col):
        return pl.BlockSpec((tm, CW), lambda i: (i, col))

    def prev(col):
        return pl.BlockSpec((tm, CW), lambda i: (jnp.maximum(i - 1, 0), col))

    def nxt(col):
        return pl.BlockSpec((tm, CW), lambda i: (jnp.minimum(i + 1, nb - 1), col))

    def whole(shape):
        return pl.BlockSpec(shape, lambda i: (0,) * len(shape))

    row = jax.ShapeDtypeStruct((1, CW), F32)
    return pl.pallas_call(
        body, name=name, grid=(nb,),
        in_specs=[cur(6), cur(7), cur(8), cur(9), prev(7), prev(8), prev(9), nxt(6),
                  cur(0), cur(1), nxt(0), nxt(1), whole(cw.shape), whole(pwbd.shape), whole(ps.shape)],
        out_specs=[pl.BlockSpec((tm, D), lambda i: (i, 0)), whole((1, CW)), whole((1, CW)), whole((1, CW)),
                   whole((1, PWD)), whole((PWD, PWD))],
        out_shape=[jax.ShapeDtypeStruct((s, D), BF16), row, row, row, row,
                   jax.ShapeDtypeStruct((PWD, PWD), F32)],
        compiler_params=_cp("arbitrary"),
    )(p, p, p, p, p, p, p, p, dmix, dmix, dmix, dmix, cw, pwbd, ps)


def _qkv_bwd(name, p, dq, dkp, dvp, dcp, qg, kg):
    s = p.shape[0]
    tm = 512
    off = PADR // tm

    def body(pq_ref, pk_ref, dq_ref, dk_ref, dv_ref, dcp_ref, qg_ref, kg_ref, dp_ref, dqg_ref, dkg_ref):
        i = pl.program_id(0)
        hm = _head_mean_matrix()

        def nrm_bwd(x, g, dy):
            r = lax.rsqrt(_head_mean(x * x, hm) + EPS)
            xn = x * r
            dxn = dy * g
            dx = r * (dxn - xn * _head_mean(dxn * xn, hm))
            dg = jnp.sum(dy * xn, axis=0, keepdims=True)
            dg = (dg[:, 0:128] + dg[:, 128:256]) + (dg[:, 256:384] + dg[:, 384:512])
            return dx, dg + pltpu.roll(dg, HD, 1)

        dxq, dgq = nrm_bwd(pq_ref[...], qg_ref[...], dq_ref[...] * 0.125)
        dxk, dgk = nrm_bwd(pk_ref[...], kg_ref[...], dk_ref[...])
        dp_ref[:, 0:AW] = dxq.astype(BF16)
        dp_ref[:, AW:2 * AW] = dxk.astype(BF16)
        dp_ref[:, 2 * AW:3 * AW] = dv_ref[...].astype(BF16)
        dp_ref[:, 3 * AW:DIN] = dcp_ref[...]

        @pl.when(i == 0)
        def _():
            dqg_ref[...] = dgq
            dkg_ref[...] = dgk

        @pl.when(i > 0)
        def _():
            dqg_ref[...] += dgq
            dkg_ref[...] += dgk

    gspec = pl.BlockSpec((1, AW), lambda i: (0, 0))
    gout = pl.BlockSpec((1, 128), lambda i: (0, 0))
    return pl.pallas_call(
        body, name=name, grid=(s // tm,),
        in_specs=[pl.BlockSpec((tm, AW), lambda i: (i, 0)), pl.BlockSpec((tm, AW), lambda i: (i, 1)),
                  pl.BlockSpec((tm, AW), lambda i: (i, 0)),
                  pl.BlockSpec((tm, AW), lambda i: (i + off, 0)),
                  pl.BlockSpec((tm, AW), lambda i: (i + off, 0)),
                  pl.BlockSpec((tm, D), lambda i: (i, 0)), gspec, gspec],
        out_specs=[pl.BlockSpec((tm, DIN), lambda i: (i, 0)), gout, gout],
        out_shape=[jax.ShapeDtypeStruct((s, DIN), BF16), jax.ShapeDtypeStruct((1, 128), F32),
                   jax.ShapeDtypeStruct((1, 128), F32)],
        compiler_params=_cp("arbitrary"),
    )(p, p, dq, dkp, dvp, dcp, qg, kg)


def _loss_grad(name, y, t):
    s = y.shape[0]
    tm = 512

    def body(y_ref, t_ref, dy_ref, dyb_ref, l_ref):
        i = pl.program_id(0)
        e = y_ref[...] - t_ref[...]
        dy = e * (1.0 / D)
        dy_ref[...] = dy
        dyb_ref[...] = dy.astype(BF16)
        part = 0.5 * jnp.sum(jnp.mean(e * e, axis=-1, keepdims=True), axis=0, keepdims=True)

        @pl.when(i == 0)
        def _():
            l_ref[...] = part

        @pl.when(i > 0)
        def _():
            l_ref[...] += part

    blk = pl.BlockSpec((tm, D), lambda i: (i, 0))
    return pl.pallas_call(
        body, name=name, grid=(s // tm,),
        in_specs=[blk, blk],
        out_specs=[blk, blk, pl.BlockSpec((1, 1), lambda i: (0, 0))],
        out_shape=[jax.ShapeDtypeStruct((s, D), F32), jax.ShapeDtypeStruct((s, D), BF16),
                   jax.ShapeDtypeStruct((1, 1), F32)],
        compiler_params=_cp("arbitrary"),
    )(y, t)


def _mm_nt_relu(name, dxb, w, l, a):
    s = dxb.shape[0]
    tm = MM_ROWS

    def body(d_ref, w_ref, a_ref, o_ref):
        df = lax.dot_general(d_ref[...], w_ref[...], NT_DIMS, preferred_element_type=F32)
        o_ref[...] = (df * (2.0 * jnp.maximum(a_ref[...].astype(F32), 0.0))).astype(BF16)

    return pl.pallas_call(
        body, name=name, grid=(s // tm,),
        in_specs=[pl.BlockSpec((tm, D), lambda i: (i, 0)),
                  pl.BlockSpec((None, DFF, D), lambda i: (l, 0, 0)),
                  pl.BlockSpec((tm, DFF), lambda i: (i, 0))],
        out_specs=pl.BlockSpec((tm, DFF), lambda i: (i, 0)),
        out_shape=jax.ShapeDtypeStruct((s, DFF), BF16),
        compiler_params=_cp("parallel"),
    )(dxb, w, a)


def _proj_out_bwd(name, dxb, w, l, mix):
    s = dxb.shape[0]
    tm = 512

    def body(d_ref, w_ref, o_ref, do_ref, dot_ref, dcp_ref, dl_ref):
        d = d_ref[...]
        wa, wc = w_ref[0:AW, :], w_ref[AW:D, :]
        do = lax.dot_general(d, wa, NT_DIMS, preferred_element_type=F32)
        do_ref[...] = do.astype(BF16)
        dot_ref[...] = lax.dot_general(wa, d, NT_DIMS, preferred_element_type=F32).astype(BF16)
        dcp_ref[...] = lax.dot_general(d, wc, NT_DIMS, preferred_element_type=F32)
        head = lax.broadcasted_iota(jnp.int32, (AW, 128), 0) // HD
        pick = jnp.where(head == lax.broadcasted_iota(jnp.int32, (AW, 128), 1), 1.0, 0.0).astype(BF16)
        dl_ref[...] = _two_pass_dot(do * o_ref[...].astype(F32), pick)

    return pl.pallas_call(
        body, name=name, grid=(s // tm,),
        in_specs=[pl.BlockSpec((tm, D), lambda i: (i, 0)),
                  pl.BlockSpec((None, D, D), lambda i: (l, 0, 0)),
                  pl.BlockSpec((tm, AW), lambda i: (i, 0))],
        out_specs=[pl.BlockSpec((tm, AW), lambda i: (i, 0)), pl.BlockSpec((AW, tm), lambda i: (0, i)),
                   pl.BlockSpec((tm, D - AW), lambda i: (i, 0)), pl.BlockSpec((tm, 128), lambda i: (i, 0))],
        out_shape=[jax.ShapeDtypeStruct((s, AW), BF16), jax.ShapeDtypeStruct((AW, s), BF16),
                   jax.ShapeDtypeStruct((s, D - AW), F32), jax.ShapeDtypeStruct((s, 128), F32)],
        compiler_params=_cp("parallel"),
    )(dxb, w, mix)


def _mm_nt_normbwd(name, gy, w, l, x, g, dres, dep):
    s, k = gy.shape
    tm = MM_ROWS

    def body(gy_ref, w_ref, x_ref, g_ref, dr_ref, dep_ref, dx_ref, dxb_ref, dg_ref):
        del dep_ref
        i = pl.program_id(0)
        dh = lax.dot_general(gy_ref[...], w_ref[...], NT_DIMS, preferred_element_type=F32)
        xv = x_ref[...]
        r = _inv_rms(xv)
        xn = xv * r
        dxn = dh * g_ref[...]
        dx = r * (dxn - xn * jnp.mean(dxn * xn, axis=-1, keepdims=True)) + dr_ref[...]
        dx_ref[...] = dx
        dxb_ref[...] = dx.astype(BF16)
        part = jnp.sum(dh * xn, axis=0, keepdims=True)

        @pl.when(i == 0)
        def _():
            dg_ref[...] = part

        @pl.when(i > 0)
        def _():
            dg_ref[...] += part

    blk = pl.BlockSpec((tm, D), lambda i: (i, 0))
    vec = pl.BlockSpec((1, D), lambda i: (0, 0))
    return pl.pallas_call(
        body, name=name, grid=(s // tm,),
        in_specs=[pl.BlockSpec((tm, k), lambda i: (i, 0)),
                  pl.BlockSpec((None, D, k), lambda i: (l, 0, 0)), blk, vec, blk, ANY],
        out_specs=[blk, blk, vec],
        out_shape=[jax.ShapeDtypeStruct((s, D), F32), jax.ShapeDtypeStruct((s, D), BF16),
                   jax.ShapeDtypeStruct((1, D), F32)],
        compiler_params=_cp("arbitrary"),
    )(gy, w, x, g, dres, dep)


def _mm_tn(name, a, b, tma, tnb, relu2=False):
    s, m = a.shape
    n = b.shape[1]

    def body(a_ref, b_ref, o_ref):
        av = _relu2(a_ref[...]) if relu2 else a_ref[...]
        o_ref[...] = lax.dot_general(av, b_ref[...], (((0,), (0,)), ((), ())),
                                     preferred_element_type=F32).astype(BF16)

    return pl.pallas_call(
        body, name=name, grid=(m // tma, n // tnb),
        in_specs=[pl.BlockSpec((s, tma), lambda i, j: (0, i)),
                  pl.BlockSpec((s, tnb), lambda i, j: (0, j))],
        out_specs=pl.BlockSpec((tma, tnb), lambda i, j: (i, j)),
        out_shape=jax.ShapeDtypeStruct((m, n), BF16),
        compiler_params=_cp("parallel", "parallel"),
    )(a, b)


def _adamw_math(gv, wv, mv, vv):
    mn = ADAM_B1 * mv + (1.0 - ADAM_B1) * gv
    vn = ADAM_B2 * vv + (1.0 - ADAM_B2) * jnp.square(gv)
    m_hat = mn / (1.0 - ADAM_B1 ** ADAM_STEP)
    v_hat = vn / (1.0 - ADAM_B2 ** ADAM_STEP)
    return gv, -ADAM_LR * (m_hat / (jnp.sqrt(v_hat) + ADAM_EPS) + ADAM_WD * wv), mn, vn


def _adamw(name, g, w, m, v):
    r, c = g.shape
    tm = 256 if r % 256 == 0 else r

    def body(g_ref, w_ref, m_ref, v_ref, go_ref, d_ref, mo_ref, vo_ref):
        go_ref[...], d_ref[...], mo_ref[...], vo_ref[...] = _adamw_math(g_ref[...], w_ref[...], m_ref[...], v_ref[...])

    blk = pl.BlockSpec((tm, c), lambda i: (i, 0))
    return pl.pallas_call(
        body, name=name, grid=(r // tm,),
        in_specs=[blk] * 4, out_specs=[blk] * 4,
        out_shape=[jax.ShapeDtypeStruct((r, c), F32)] * 4,
        compiler_params=_cp("parallel"),
    )(g, w, m, v)


def _place():
    x, y, c = lax.axis_index("x"), lax.axis_index("y"), lax.axis_index("c")
    chips = [(1 - x, y), (x, 1 - y), (1 - x, 1 - y)]
    return x, y, c, chips


BLOCK_AXIS = (2, 1, 2, 1)
LARGE_DIMS = ((D, DIN), (D, D), (D, DFF), (DFF, D))


def _full_shape(t, layers, dtype):
    r, c = LARGE_DIMS[t]
    return jax.ShapeDtypeStruct((layers, r, c), dtype)


def _cast_into_full(name, t, shard, b1, dep):
    _, r, c = shard.shape
    tm = min(512, r)
    if BLOCK_AXIS[t] == 1:
        out_spec = pl.BlockSpec((None, tm, c), lambda l, i, br: (l, br[0] * (r // tm) + i, 0))
    else:
        out_spec = pl.BlockSpec((None, tm, c), lambda l, i, br: (l, i, br[0]))

    def body(b_ref, x_ref, dep_ref, o_ref):
        del b_ref, dep_ref
        o_ref[...] = x_ref[...].astype(BF16)

    return pl.pallas_call(
        body, name=name,
        grid_spec=pltpu.PrefetchScalarGridSpec(
            num_scalar_prefetch=1, grid=(DEPTH, r // tm),
            in_specs=[pl.BlockSpec((None, tm, c), lambda l, i, br: (l, i, 0)), ANY],
            out_specs=out_spec),
        out_shape=_full_shape(t, DEPTH, BF16),
        compiler_params=_cp("parallel", "parallel"),
    )(b1, shard, dep)


HBM = pl.BlockSpec(memory_space=pltpu.HBM)
SEM = pl.BlockSpec(memory_space=pltpu.SEMAPHORE)
DATAFLOW = pltpu.SideEffectType.DATAFLOW_SIDE_EFFECTING


def _half(ref, l, t, b, c):
    r, cols = LARGE_DIMS[t]
    if BLOCK_AXIS[t] == 1:
        n = r // 8
        return ref.at[l, pl.ds(pl.multiple_of(b * (2 * n) + c * n, 16), n), :]
    n, w = r // 2, cols // 4
    return ref.at[l, pl.ds(pl.multiple_of(c * n, 16), n), pl.ds(pl.multiple_of(b * w, 128), w)]


def _gather_start(name, layers, ts, fulls):
    n = len(ts)

    def body(*refs):
        f_refs, sems = refs[n:2 * n], refs[2 * n:2 * n + 2 * len(layers)]
        x, y, c, chips = _place()
        for i, l in enumerate(layers):
            for k, t in enumerate(ts):
                own = _half(f_refs[k], l, t, 2 * x + y, c)
                for j, (cx, cy) in enumerate(chips):
                    pltpu.make_async_remote_copy(src_ref=own, dst_ref=own, send_sem=sems[2 * i].at[3 * t + j],
                                                 recv_sem=sems[2 * i + 1].at[3 * t + j], device_id=(cx, cy, c),
                                                 device_id_type=MESH).start()
        refs[-1][...] = jnp.zeros((8, 128), F32)

    outs = pl.pallas_call(
        body, name=name,
        in_specs=[HBM] * n,
        out_specs=[HBM] * n + [SEM] * (2 * len(layers)) + [pl.BlockSpec(memory_space=pltpu.VMEM)],
        out_shape=[pltpu.HBM(f.shape, f.dtype) for f in fulls]
        + [pltpu.SemaphoreType.DMA((12,))] * (2 * len(layers)) + [jax.ShapeDtypeStruct((8, 128), F32)],
        input_output_aliases={k: k for k in range(n)},
        compiler_params=pltpu.CompilerParams(has_side_effects=DATAFLOW),
    )(*[pltpu.with_memory_space_constraint(f, pltpu.HBM) for f in fulls])
    return outs[0:n], {l: (outs[n + 2 * i], outs[n + 1 + 2 * i]) for i, l in enumerate(layers)}, outs[-1]


def _gather_wait(name, l, ts, fulls, sems, after):
    def body(*refs):
        send_sems, recv_sems, f_refs = refs[4], refs[5], refs[7:11]
        x, y, c, chips = _place()
        for t in ts:
            own = _half(f_refs[t], l, t, 2 * x + y, c)
            for j, (cx, cy) in enumerate(chips):
                landed = _half(f_refs[t], l, t, 2 * cx + cy, c)
                pltpu.make_async_remote_copy(src_ref=own, dst_ref=landed, send_sem=send_sems.at[3 * t + j],
                                             recv_sem=recv_sems.at[3 * t + j], device_id=(cx, cy, c),
                                             device_id_type=MESH).wait()

    return pl.pallas_call(
        body, name=name,
        in_specs=[HBM] * 4 + [SEM, SEM, ANY], out_specs=[HBM] * 4,
        out_shape=[pltpu.HBM(s.shape, s.dtype) for s in (_full_shape(t, DEPTH, BF16) for t in range(4))],
        input_output_aliases={t: t for t in range(4)},
        compiler_params=pltpu.CompilerParams(has_side_effects=DATAFLOW),
    )(*fulls, sems[0], sems[1], after)


def _pass_on(name, l, ts, fulls):
    def body(*refs):
        f_refs, send_sems, recv_sems = refs[4:8], refs[8], refs[9]
        x, y, c, chips = _place()

        def copy(t, j, half):
            cx, cy = chips[j]
            part = _half(f_refs[t], l, t, 2 * cx + cy, half)
            return pltpu.make_async_remote_copy(src_ref=part, dst_ref=part, send_sem=send_sems.at[3 * t + j],
                                                recv_sem=recv_sems.at[3 * t + j], device_id=(x, y, 1 - c),
                                                device_id_type=MESH)

        for t in ts:
            for j in range(3):
                copy(t, j, c).start()
        for t in ts:
            for j in range(3):
                copy(t, j, 1 - c).wait_recv()
                copy(t, j, c).wait_send()

    return pl.pallas_call(
        body, name=name,
        in_specs=[ANY] * 4, out_specs=[ANY] * 4,
        out_shape=[_full_shape(t, DEPTH, BF16) for t in range(4)],
        input_output_aliases={t: t for t in range(4)},
        scratch_shapes=[pltpu.SemaphoreType.DMA((12,)), pltpu.SemaphoreType.DMA((12,))],
    )(*fulls)


def _block2d(ref, t, b):
    r, cols = LARGE_DIMS[t]
    if BLOCK_AXIS[t] == 1:
        return ref.at[pl.ds(pl.multiple_of(b * (r // 4), 16), r // 4), :]
    return ref.at[:, pl.ds(pl.multiple_of(b * (cols // 4), 128), cols // 4)]


def _block_dims(t):
    r, cols = LARGE_DIMS[t]
    return (r // 4, cols) if BLOCK_AXIS[t] == 1 else (r, cols // 4)


def _reduce_copies(ts, g_refs, r_refs, send_sems, recv_sems):
    _, _, c, chips = _place()
    return [pltpu.make_async_remote_copy(src_ref=_block2d(g_refs[i], t, 2 * cx + cy), dst_ref=r_refs[i].at[j],
                                         send_sem=send_sems.at[3 * i + j], recv_sem=recv_sems.at[3 * i + j],
                                         device_id=(cx, cy, c), device_id_type=MESH)
            for i, t in enumerate(ts) for j, (cx, cy) in enumerate(chips)]


def _reduce_start(name, ts, grads):
    n = len(ts)

    def body(*refs):
        for cp in _reduce_copies(ts, refs[n:2 * n], refs[2 * n:3 * n], refs[3 * n], refs[3 * n + 1]):
            cp.start()
        refs[3 * n + 2][...] = jnp.zeros((8, 128), F32)

    outs = pl.pallas_call(
        body, name=name,
        in_specs=[HBM] * n,
        out_specs=[HBM] * (2 * n) + [SEM, SEM, pl.BlockSpec(memory_space=pltpu.VMEM)],
        out_shape=[pltpu.HBM(g.shape, BF16) for g in grads]
        + [pltpu.HBM((3,) + _block_dims(t), BF16) for t in ts]
        + [pltpu.SemaphoreType.DMA((3 * n,)), pltpu.SemaphoreType.DMA((3 * n,)), jax.ShapeDtypeStruct((8, 128), F32)],
        input_output_aliases={i: i for i in range(n)},
        compiler_params=pltpu.CompilerParams(has_side_effects=DATAFLOW),
    )(*[pltpu.with_memory_space_constraint(g, pltpu.HBM) for g in grads])
    return outs[0:n], outs[n:2 * n], (outs[2 * n], outs[2 * n + 1]), outs[2 * n + 2]


def _reduce_wait(name, ts, grads, landing, sems, afters):
    n = len(ts)
    first_out = 2 * n + 2 + len(afters)

    def body(*refs):
        for cp in _reduce_copies(ts, refs[first_out:first_out + n], refs[first_out + n:first_out + 2 * n],
                                 refs[2 * n], refs[2 * n + 1]):
            cp.wait()

    outs = pl.pallas_call(
        body, name=name,
        in_specs=[HBM] * (2 * n) + [SEM, SEM] + [ANY] * len(afters), out_specs=[HBM] * (2 * n),
        out_shape=[pltpu.HBM(g.shape, BF16) for g in grads] + [pltpu.HBM(r.shape, BF16) for r in landing],
        input_output_aliases={i: i for i in range(2 * n)},
        compiler_params=pltpu.CompilerParams(has_side_effects=DATAFLOW),
    )(*grads, *landing, sems[0], sems[1], *afters)
    return outs[0:n], outs[n:2 * n]


def _add4(name, t, own, landed, b1):
    rb, cb = _block_dims(t)
    tm = min(512, rb)
    if BLOCK_AXIS[t] == 1:
        own_spec = pl.BlockSpec((tm, cb), lambda i, br: (br[0] * (rb // tm) + i, 0))
    else:
        own_spec = pl.BlockSpec((tm, cb), lambda i, br: (i, br[0]))

    def body(b_ref, o_ref, r0_ref, r1_ref, r2_ref, s_ref):
        del b_ref
        s_ref[...] = ((o_ref[...].astype(F32) + r0_ref[...].astype(F32))
                      + (r1_ref[...].astype(F32) + r2_ref[...].astype(F32))).astype(BF16)

    def got(j):
        return pl.BlockSpec((None, tm, cb), lambda i, br: (j, i, 0))

    return pl.pallas_call(
        body, name=name,
        grid_spec=pltpu.PrefetchScalarGridSpec(
            num_scalar_prefetch=1, grid=(rb // tm,),
            in_specs=[own_spec, got(0), got(1), got(2)],
            out_specs=pl.BlockSpec((tm, cb), lambda i, br: (i, 0))),
        out_shape=jax.ShapeDtypeStruct((rb, cb), BF16),
        compiler_params=_cp("parallel"),
    )(b1, own, landed, landed, landed)


def _swap_sib(name, sums):
    def body(*refs):
        s_refs, t_refs, send_sems, recv_sems = refs[0:4], refs[4:8], refs[8], refs[9]
        x, y, c, _ = _place()
        cps = [pltpu.make_async_remote_copy(src_ref=s_refs[t], dst_ref=t_refs[t], send_sem=send_sems.at[t],
                                            recv_sem=recv_sems.at[t], device_id=(x, y, 1 - c), device_id_type=MESH)
               for t in range(4)]
        for cp in cps:
            cp.start()
        for cp in cps:
            cp.wait()

    return pl.pallas_call(
        body, name=name,
        in_specs=[ANY] * 4, out_specs=[ANY] * 4,
        out_shape=[jax.ShapeDtypeStruct(s.shape, BF16) for s in sums],
        scratch_shapes=[pltpu.SemaphoreType.DMA((4,)), pltpu.SemaphoreType.DMA((4,))],
    )(*sums)


def _adamw_pair(name, l, s_own, s_sib, w, m, v, outs):
    rb, cb = s_own.shape
    tm = min(512, rb)

    def body(a_ref, b_ref, w_ref, m_ref, v_ref, g0, d0, m0, v0, go_ref, d_ref, mo_ref, vo_ref):
        del g0, d0, m0, v0
        gv = a_ref[...].astype(F32) + b_ref[...].astype(F32)
        go_ref[...], d_ref[...], mo_ref[...], vo_ref[...] = _adamw_math(gv, w_ref[...], m_ref[...], v_ref[...])

    part = pl.BlockSpec((tm, cb), lambda i: (i, 0))
    layer = pl.BlockSpec((None, tm, cb), lambda i: (l, i, 0))
    return pl.pallas_call(
        body, name=name, grid=(rb // tm,),
        in_specs=[part, part, layer, layer, layer] + [ANY] * 4,
        out_specs=[layer] * 4,
        out_shape=[jax.ShapeDtypeStruct((DEPTH, rb, cb), F32)] * 4,
        input_output_aliases={5 + i: i for i in range(4)},
        compiler_params=_cp("parallel"),
    )(s_own, s_sib, w, m, v, *outs)


def _all_gather8(name, v, dep):
    m_per, n = v.shape

    def body(v_ref, dep_ref, out_ref, send_sems, recv_sems, local_sem):
        del dep_ref
        x, y, c, chips = _place()
        me, sib = (x, y, c), (x, y, 1 - c)

        def rows(px, py, pc):
            return out_ref.at[pl.ds((4 * px + 2 * py + pc) * m_per, m_per), :]

        def copy(k, block, to, src=None):
            return pltpu.make_async_remote_copy(
                src_ref=rows(*block) if src is None else src, dst_ref=rows(*block),
                send_sem=send_sems.at[k], recv_sem=recv_sems.at[k], device_id=to, device_id_type=MESH)

        mine = pltpu.make_async_copy(v_ref, rows(*me), local_sem)
        mine.start()
        first = [copy(0, me, sib, src=v_ref)]
        first += [copy(1 + j, me, (*chip, c), src=v_ref) for j, chip in enumerate(chips)]
        for cp in first:
            cp.start()
        passed = [copy(4 + j, (*chip, c), sib) for j, chip in enumerate(chips)]
        for j, chip in enumerate(chips):
            copy(1 + j, (*chip, c), me).wait_recv()
            passed[j].start()
        copy(0, sib, me).wait_recv()
        for j, chip in enumerate(chips):
            copy(4 + j, (*chip, 1 - c), me).wait_recv()
        for cp in first + passed:
            cp.wait_send()
        mine.wait()

    return pl.pallas_call(
        body, name=name,
        out_shape=jax.ShapeDtypeStruct((8 * m_per, n), v.dtype),
        in_specs=[pl.BlockSpec(memory_space=pltpu.VMEM), ANY],
        out_specs=pl.BlockSpec(memory_space=pltpu.VMEM),
        scratch_shapes=[pltpu.SemaphoreType.DMA((7,)), pltpu.SemaphoreType.DMA((7,)), pltpu.SemaphoreType.DMA],
    )(v, dep)


def _sum8(name, g):
    def body(g_ref, o_ref):
        acc = g_ref[0]
        for d in range(1, 8):
            acc = acc + g_ref[d]
        o_ref[...] = acc

    return pl.pallas_call(body, name=name, out_shape=jax.ShapeDtypeStruct(g.shape[1:], F32))(g)


def _pack(parts):
    flat = []
    for a in parts:
        a = a.reshape(-1)
        flat.append(jnp.pad(a, (0, (-a.shape[0]) % 128)))
    cat = jnp.concatenate(flat)
    cat = jnp.pad(cat, (0, (-cat.shape[0]) % 1024))
    return cat.reshape(-1, 128)


def _unpack(packed, shapes):
    flat = packed.reshape(-1)
    out, at = [], 0
    for shp in shapes:
        n = 1
        for d in shp:
            n *= d
        out.append(flat[at:at + n].reshape(shp))
        at += n + (-n) % 128
    return out


def _local_step(x, target, layer_weights, on_grads, small):
    qg_all = jnp.tile(small["q_norm_g"], (1, 8))
    kg_all = jnp.tile(small["k_norm_g"], (1, 8))
    bias_all = _bias_layout(_bias_expand("bias_expand", jnp.pad(small["rel_bias"], ((0, 0), (0, 0), (0, NIDX - 257)))))
    same_group = jnp.eye(4, dtype=F32)[None, :, None, :, None]
    pwbd_all = (small["pool_w"][:, :, :, None, :] * same_group).reshape(DEPTH, PWD, PWD)
    saved = []
    xin = x
    h = _rmsnorm("norm_first", x, small["norm1_g"][0:1])
    for l in range(DEPTH):
        w_in = layer_weights(l, (0,), xin)[0]
        qg, kg = qg_all[l:l + 1], kg_all[l:l + 1]
        cw, pwbd, ps = small["conv_w"][l], pwbd_all[l], small["pool_scale"][l:l + 1]
        p = _mm_nn(f"proj_in_{l}", h, w_in, l, F32)
        q, qt, kp, kt, vp, vt = _qkv(f"qkv_{l}", p, qg, kg)
        o, lse = _attn_fwd(f"attn_fwd_{l}", kp, qt, vt, bias_all, l)
        w_in, w_out, w_1, w_2 = layer_weights(l, (1, 2, 3), o)
        mix = _convpool_fwd(f"convpool_fwd_{l}", p, o, cw, pwbd, ps)
        x1, h2 = _mm_res_norm(f"proj_out_{l}", mix, w_out, l, xin, small["norm2_g"][l:l + 1])
        gnext = small["norm1_g"][(l + 1) % DEPTH][None]
        a, x2, hnext = _mlp_fwd(f"mlp_{l}", h2, w_1, w_2, l, x1, gnext)
        saved.append(dict(xin=xin, h=h, p=p, q=q, qt=qt, kp=kp, kt=kt, vp=vp, mix=mix, x1=x1, h2=h2, a=a, lse=lse,
                          qg=qg, kg=kg, cw=cw, pwbd=pwbd, ps=ps))
        xin, h = x2, hnext

    dx, dxb, loss = _loss_grad("loss_grad", xin, target)
    raw = {k: [None] * DEPTH for k in ("dg1", "dqg", "dkg", "db", "dw0", "dw1", "dw2", "dpw", "dps", "dg2")}
    for l in reversed(range(DEPTH)):
        sv = saved[l]
        da = _mm_nt_relu(f"mlp2_bwd_{l}", dxb, w_2, l, sv["a"])
        g_2 = _mm_tn(f"mlp2_wgrad_{l}", sv["a"], dxb, 512, 1024, relu2=True)
        g_1 = _mm_tn(f"mlp1_wgrad_{l}", sv["h2"], da, 1024, 512)
        dep = on_grads(l, (2, 3), (g_1, g_2))
        dx1, dx1b, dg2 = _mm_nt_normbwd(f"mlp1_bwd_{l}", da, w_1, l, sv["x1"], small["norm2_g"][l:l + 1], dx, dep)
        do, dot, dmix, dl = _proj_out_bwd(f"proj_out_bwd_{l}", dx1b, w_out, l, sv["mix"])
        g_out = _mm_tn(f"proj_out_wgrad_{l}", sv["mix"], dx1b, 512, 1024)
        dcp, dw0, dw1, dw2, dps, dpw = _convpool_bwd(f"convpool_bwd_{l}", sv["p"], dmix, sv["cw"], sv["pwbd"], sv["ps"])
        dq, dkp, dvp, db = _attn_bwd(f"attn_bwd_{l}", sv["q"], sv["qt"], sv["kp"], sv["kt"], sv["vp"], bias_all, l,
                                     do, dot, sv["lse"], _rowsum_layout(dl, x.shape[0] // UNIT))
        dp, dqg, dkg = _qkv_bwd(f"qkv_bwd_{l}", sv["p"], dq, dkp, dvp, dcp, sv["qg"], sv["kg"])
        g_in = _mm_tn(f"proj_in_wgrad_{l}", sv["h"], dp, 1024, 640)
        dep = on_grads(l, (0, 1), (g_in, g_out))
        dx, dxb, dg1 = _mm_nt_normbwd(f"proj_in_bwd_{l}", dp, w_in, l, sv["xin"], small["norm1_g"][l:l + 1], dx1, dep)
        for k, val in dict(dg1=dg1, dqg=dqg, dkg=dkg, db=db, dw0=dw0, dw1=dw1, dw2=dw2, dpw=dpw, dps=dps, dg2=dg2).items():
            raw[k][l] = val
    cat = {k: jnp.concatenate(v, axis=0) for k, v in raw.items() if k not in ("db", "dpw")}
    drb = _bias_reduce("bias_reduce", _bias_unlayout(jnp.stack(raw["db"])))
    dpw = jnp.stack(raw["dpw"])
    gsmall = {
        "norm1_g": cat["dg1"], "q_norm_g": cat["dqg"][:, :HD], "k_norm_g": cat["dkg"][:, :HD],
        "rel_bias": drb[:, :, :257],
        "conv_w": jnp.stack([cat["dw0"], cat["dw1"], cat["dw2"]], axis=1),
        "pool_w": jnp.stack([dpw[:, g * 64:(g + 1) * 64, g * 64:(g + 1) * 64] for g in range(4)], axis=1),
        "pool_scale": cat["dps"], "norm2_g": cat["dg2"],
    }
    return loss, dx, gsmall


SMALL = ("norm1_g", "q_norm_g", "k_norm_g", "rel_bias", "conv_w", "pool_w", "pool_scale", "norm2_g")
LARGE = ("w_in", "w_out", "w_mlp1", "w_mlp2")


def kernel(x, norm1_g, w_in, q_norm_g, k_norm_g, rel_bias, conv_w, pool_w, pool_scale, w_out, norm2_g, w_mlp1, w_mlp2, loss_target, m_norm1_g, m_w_in, m_q_norm_g, m_k_norm_g, m_rel_bias, m_conv_w, m_pool_w, m_pool_scale, m_w_out, m_norm2_g, m_w_mlp1, m_w_mlp2, v_norm1_g, v_w_in, v_q_norm_g, v_k_norm_g, v_rel_bias, v_conv_w, v_pool_w, v_pool_scale, v_w_out, v_norm2_g, v_w_mlp1, v_w_mlp2):
    w = dict(norm1_g=norm1_g, w_in=w_in, q_norm_g=q_norm_g, k_norm_g=k_norm_g, rel_bias=rel_bias, conv_w=conv_w,
             pool_w=pool_w, pool_scale=pool_scale, w_out=w_out, norm2_g=norm2_g, w_mlp1=w_mlp1, w_mlp2=w_mlp2)
    m = dict(norm1_g=m_norm1_g, w_in=m_w_in, q_norm_g=m_q_norm_g, k_norm_g=m_k_norm_g, rel_bias=m_rel_bias,
             conv_w=m_conv_w, pool_w=m_pool_w, pool_scale=m_pool_scale, w_out=m_w_out, norm2_g=m_norm2_g,
             w_mlp1=m_w_mlp1, w_mlp2=m_w_mlp2)
    v = dict(norm1_g=v_norm1_g, w_in=v_w_in, q_norm_g=v_q_norm_g, k_norm_g=v_k_norm_g, rel_bias=v_rel_bias,
             conv_w=v_conv_w, pool_w=v_pool_w, pool_scale=v_pool_scale, w_out=v_w_out, norm2_g=v_norm2_g,
             w_mlp1=v_w_mlp1, w_mlp2=v_w_mlp2)
    ax, ay, ac = lax.axis_index("x"), lax.axis_index("y"), lax.axis_index("c")
    b1 = jnp.reshape(2 * ax + ay, (1,)).astype(jnp.int32)

    cw_rows = _all_gather8("gather_conv_w", jnp.pad(conv_w.reshape(DEPTH * 3, 64), ((0, 4), (0, 64))), b1)
    cw_chips = [cw_rows[(4 * cx + 2 * cy) * 16:(4 * cx + 2 * cy) * 16 + 12, :64] for cx in range(2) for cy in range(2)]
    small = {n: w[n] for n in SMALL}
    small["conv_w"] = jnp.concatenate(cw_chips, axis=1).reshape(DEPTH, 3, CW)

    (w_in_full,), in_sems, in_token = _gather_start(
        "gather_start_in", (0,), (0,), [_cast_into_full("cast_w_in", 0, w["w_in"], b1, cw_rows)])
    others, first_sems, first_token = _gather_start(
        "gather_start_first", (0,), (1, 2, 3),
        [_cast_into_full(f"cast_{LARGE[t]}", t, w[LARGE[t]], b1, in_token) for t in (1, 2, 3)])
    held = [[w_in_full] + list(others)]
    sems = {(0, 0): in_sems[0], (0, 1): first_sems[0]}

    def layer_weights(l, ts, after):
        if l > 0:
            ts = (0, 1, 2, 3) if ts == (0,) else ()
        if ts:
            tag = f"{l}_{ts[0]}"
            first_in = l == 0 and ts == (0,)
            after = first_token if first_in else after
            arrived = _gather_wait(f"gather_wait_{tag}", l, ts, held[0], sems[l, ts[0] if l == 0 else 0], after)
            if first_in:
                arrived, rest_sems, _ = _gather_start("gather_start_rest", tuple(range(1, DEPTH)), (0, 1, 2, 3),
                                                      arrived)
                sems.update({(k, 0): v for k, v in rest_sems.items()})
            held[0] = _pass_on(f"pass_on_{tag}", l, ts, arrived)
        return held[0]

    flights = {}

    def await_flight(l, ts, afters):
        g, landing, sm, _ = flights[l, ts]
        flights[l, ts] = _reduce_wait(f"reduce_wait_{l}_{ts[0]}", ts, g, landing, sm, afters)

    def on_grads(l, ts, grads):
        if ts == (0, 1) and l + 1 < DEPTH:
            await_flight(l + 1, (2, 3), [grads[0]])
            await_flight(l + 1, (0, 1), [grads[0]])
        flights[l, ts] = _reduce_start(f"reduce_start_{l}_{ts[0]}", ts, grads)
        return flights[l, ts][3]

    loss_part, grad_x, gsmall = _local_step(x[0], loss_target[0], layer_weights, on_grads, small)
    loss = lax.psum(loss_part[0, 0], ("x", "y", "c"))
    order = [n for n in SMALL]
    packed = _pack([gsmall[n] for n in order])

    out = {n: [lax.empty(w[n].shape, F32) for _ in range(4)] for n in LARGE}
    for l in reversed(range(DEPTH)):
        if l == 0:
            afters = [grad_x, packed] + [out[n][0] for n in LARGE]
            await_flight(0, (2, 3), afters)
            await_flight(0, (0, 1), afters)
        sums = [None] * 4
        for ts in ((0, 1), (2, 3)):
            g, landing = flights[l, ts]
            for i, t in enumerate(ts):
                sums[t] = _add4(f"add4_{LARGE[t]}_{l}", t, g[i], landing[i], b1)
        theirs = _swap_sib(f"swap_sib_{l}", sums)
        for t, n in enumerate(LARGE):
            out[n] = _adamw_pair(f"adamw_{n}_{l}", l, sums[t], theirs[t], w[n], m[n], v[n], out[n])

    rows = packed.shape[0]
    summed = _sum8("sum_small", _all_gather8("gather_small", packed, out[LARGE[0]][0]).reshape(8, rows, 128))
    gfull = dict(zip(order, _unpack(summed, [gsmall[n].shape for n in order])))
    gfull["conv_w"] = lax.dynamic_slice_in_dim(gfull["conv_w"], (2 * ax + ay) * 64, 64, axis=2)
    res = _adamw("adamw_small", _pack([gfull[n] for n in order]), _pack([w[n] for n in order]),
                 _pack([m[n] for n in order]), _pack([v[n] for n in order]))
    for n, parts in zip(order, zip(*[_unpack(r, [w[k].shape for k in order]) for r in res])):
        out[n] = list(parts)

    names = ("norm1_g", "w_in", "q_norm_g", "k_norm_g", "rel_bias", "conv_w", "pool_w", "pool_scale", "w_out",
             "norm2_g", "w_mlp1", "w_mlp2")
    flat = [loss, grad_x[None]]
    for i in range(4):
        flat += [out[n][i] for n in names]
    return tuple(flat)
```

```python
import functools

import jax
import jax.numpy as jnp
from jax import lax
from jax.experimental import pallas as pl
from jax.experimental.pallas import tpu as pltpu

F32 = jnp.float32
BF16 = jnp.bfloat16

D = 1024
DEPTH = 4
CH = 64
NPREV = 8
KB = (NPREV + 1) * CH
PADR = NPREV * CH
HD = 64
AW = 512
CW = 256
PWD = 256
DIN = 3 * AW + 3 * CW + PWD
DFF = 4 * D
NIDX = 384
EPS = 1e-6
NEG_INF = -1e30

ADAM_LR = 0.001
ADAM_B1 = 0.9
ADAM_B2 = 0.999
ADAM_EPS = 1e-08
ADAM_WD = 0.01
ADAM_STEP = 10

VMEM_LIMIT = 52 * 1024 * 1024
MM_ROWS = 512


def _mm_rows(k, n):
    return 2 * MM_ROWS if k + n <= 4096 else MM_ROWS
MESH = pl.DeviceIdType.MESH
ANY = pl.BlockSpec(memory_space=pl.ANY)


def _cp(*sem):
    return pltpu.CompilerParams(dimension_semantics=sem, vmem_limit_bytes=VMEM_LIMIT)


def _inv_rms(x):
    return lax.rsqrt(jnp.mean(x * x, axis=-1, keepdims=True) + EPS)


def _head_mean_matrix():
    r = lax.broadcasted_iota(jnp.int32, (AW, AW), 0) // HD
    c = lax.broadcasted_iota(jnp.int32, (AW, AW), 1) // HD
    return jnp.where(r == c, 1.0 / HD, 0.0).astype(BF16)


def _two_pass_dot(x, m):
    hi = x.astype(BF16)
    lo = (x - hi.astype(F32)).astype(BF16)
    return (jnp.dot(hi, m, preferred_element_type=F32)
            + jnp.dot(lo, m, preferred_element_type=F32))


def _head_mean(x, hm):
    return _two_pass_dot(x, hm)


def _rmsnorm(name, x, g):
    s = x.shape[0]
    tm = 512

    def body(x_ref, g_ref, h_ref):
        xv = x_ref[...]
        h_ref[...] = (xv * _inv_rms(xv) * g_ref[...]).astype(BF16)

    return pl.pallas_call(
        body, name=name, grid=(s // tm,),
        in_specs=[pl.BlockSpec((tm, D), lambda i: (i, 0)), pl.BlockSpec((1, D), lambda i: (0, 0))],
        out_specs=pl.BlockSpec((tm, D), lambda i: (i, 0)),
        out_shape=jax.ShapeDtypeStruct((s, D), BF16),
        compiler_params=_cp("parallel"),
    )(x, g)


def _relu2(a):
    r = jnp.maximum(a, jnp.zeros_like(a))
    return r * r


def _mm_nn(name, a, w, l, out_dtype):
    s, k = a.shape
    n = w.shape[2]
    tm = _mm_rows(k, n)

    def body(a_ref, w_ref, o_ref):
        o_ref[...] = jnp.dot(a_ref[...], w_ref[...], preferred_element_type=F32).astype(o_ref.dtype)

    return pl.pallas_call(
        body, name=name, grid=(s // tm,),
        in_specs=[pl.BlockSpec((tm, k), lambda i: (i, 0)),
                  pl.BlockSpec((None, k, n), lambda i: (l, 0, 0))],
        out_specs=pl.BlockSpec((tm, n), lambda i: (i, 0)),
        out_shape=jax.ShapeDtypeStruct((s, n), out_dtype),
        compiler_params=_cp("parallel"),
    )(a, w)


def _mm_res_norm(name, a, w, l, res, g):
    s, k = a.shape
    tm = _mm_rows(k, D)

    def body(a_ref, w_ref, r_ref, g_ref, x_ref, h_ref):
        acc = r_ref[...] + jnp.dot(a_ref[...], w_ref[...], preferred_element_type=F32)
        x_ref[...] = acc
        h_ref[...] = (acc * _inv_rms(acc) * g_ref[...]).astype(BF16)

    return pl.pallas_call(
        body, name=name, grid=(s // tm,),
        in_specs=[pl.BlockSpec((tm, k), lambda i: (i, 0)),
                  pl.BlockSpec((None, k, D), lambda i: (l, 0, 0)),
                  pl.BlockSpec((tm, D), lambda i: (i, 0)),
                  pl.BlockSpec((1, D), lambda i: (0, 0))],
        out_specs=[pl.BlockSpec((tm, D), lambda i: (i, 0))] * 2,
        out_shape=[jax.ShapeDtypeStruct((s, D), F32), jax.ShapeDtypeStruct((s, D), BF16)],
        compiler_params=_cp("parallel"),
    )(a, w, res, g)


def _mlp_fwd(name, h2, w1, w2, l, res, g):
    s = h2.shape[0]
    tm = 256

    def body(h_ref, w1_ref, w2_ref, r_ref, g_ref, a_ref, x_ref, hn_ref):
        a = jnp.dot(h_ref[...], w1_ref[...], preferred_element_type=F32).astype(BF16)
        a_ref[...] = a
        acc = r_ref[...] + jnp.dot(_relu2(a), w2_ref[...], preferred_element_type=F32)
        x_ref[...] = acc
        hn_ref[...] = (acc * _inv_rms(acc) * g_ref[...]).astype(BF16)

    once = pl.Buffered(1)
    rows = pl.BlockSpec((tm, D), lambda i: (i, 0))
    return pl.pallas_call(
        body, name=name, grid=(s // tm,),
        in_specs=[rows,
                  pl.BlockSpec((None, D, DFF), lambda i: (l, 0, 0), pipeline_mode=once),
                  pl.BlockSpec((None, DFF, D), lambda i: (l, 0, 0), pipeline_mode=once),
                  rows, pl.BlockSpec((1, D), lambda i: (0, 0))],
        out_specs=[pl.BlockSpec((tm, DFF), lambda i: (i, 0)), rows, rows],
        out_shape=[jax.ShapeDtypeStruct((s, DFF), BF16), jax.ShapeDtypeStruct((s, D), F32),
                   jax.ShapeDtypeStruct((s, D), BF16)],
        compiler_params=_cp("parallel"),
    )(h2, w1, w2, res, g)


def _qkv(name, p, qg, kg):
    s = p.shape[0]
    tm = PADR
    nb = s // tm

    def body(pq_ref, pk_ref, pv_ref, qg_ref, kg_ref, q_ref, qt_ref, k_ref, kt_ref, v_ref, vt_ref):
        t = pl.program_id(0)
        hm = _head_mean_matrix()

        def nrm(x, g):
            return x * lax.rsqrt(_head_mean(x * x, hm) + EPS) * g

        first = t == 0
        qq = nrm(pq_ref[...], qg_ref[...]) * 0.125
        kk = jnp.where(first, 0.0, nrm(pk_ref[...], kg_ref[...]))
        vv = jnp.where(first, 0.0, pv_ref[...])
        q_ref[...] = qq.astype(BF16)
        qt_ref[...] = qq.T.astype(BF16)
        k_ref[...] = kk.astype(BF16)
        kt_ref[...] = kk.T.astype(BF16)
        v_ref[...] = vv.astype(BF16)
        vt_ref[...] = vv.T.astype(BF16)

    def src(col):
        return pl.BlockSpec((tm, AW), lambda t: (jnp.maximum(t - 1, 0), col))

    gspec = pl.BlockSpec((1, AW), lambda t: (0, 0))
    rows = pl.BlockSpec((tm, AW), lambda t: (t, 0))
    cols = pl.BlockSpec((AW, tm), lambda t: (0, t))
    return pl.pallas_call(
        body, name=name, grid=(nb + 1,),
        in_specs=[src(0), src(1), src(2), gspec, gspec],
        out_specs=[pl.BlockSpec((tm, AW), lambda t: (jnp.maximum(t - 1, 0), 0)),
                   pl.BlockSpec((AW, tm), lambda t: (0, jnp.maximum(t - 1, 0))),
                   rows, cols, rows, cols],
        out_shape=[jax.ShapeDtypeStruct((s, AW), BF16), jax.ShapeDtypeStruct((AW, s), BF16),
                   jax.ShapeDtypeStruct((s + PADR, AW), BF16), jax.ShapeDtypeStruct((AW, s + PADR), BF16),
                   jax.ShapeDtypeStruct((s + PADR, AW), BF16), jax.ShapeDtypeStruct((AW, s + PADR), BF16)],
        compiler_params=_cp("arbitrary"),
    )(p, p, p, qg, kg)


NBAND = KB // CH
HIGHEST = lax.Precision.HIGHEST
NT_DIMS = (((1,), (1,)), ((), ()))


def _onehot_table(a):
    m = lax.broadcasted_iota(jnp.int32, (128, NIDX), 0)
    idx = lax.broadcasted_iota(jnp.int32, (128, NIDX), 1)
    rel = jnp.clip(KB - 1 - (CH * a + m), -128, 128) + 128
    return jnp.where(rel == idx, 1.0, 0.0).astype(F32)


def _onehot_diagonal():
    r = lax.broadcasted_iota(jnp.int32, (CH * CH, 128), 0)
    m = lax.broadcasted_iota(jnp.int32, (CH * CH, 128), 1)
    return jnp.where((r % CH) - (r // CH) + (CH - 1) == m, 1.0, 0.0).astype(F32)


def _bias_expand(name, rb):
    def body(rb_ref, o_ref):
        along = [lax.dot_general(rb_ref[...], _onehot_table(a), NT_DIMS, preferred_element_type=F32,
                                 precision=HIGHEST) for a in range(NBAND)]
        o_ref[...] = lax.dot_general(jnp.concatenate(along, axis=0), _onehot_diagonal(), NT_DIMS,
                                     preferred_element_type=F32, precision=HIGHEST)

    return pl.pallas_call(
        body, name=name, grid=(DEPTH,),
        in_specs=[pl.BlockSpec((None, 8, NIDX), lambda l: (l, 0, 0))],
        out_specs=pl.BlockSpec((None, NBAND * 8, CH * CH), lambda l: (l, 0, 0)),
        out_shape=jax.ShapeDtypeStruct((DEPTH, NBAND * 8, CH * CH), F32),
        compiler_params=_cp("parallel"),
    )(rb)


def _bias_reduce(name, db):
    def body(db_ref, o_ref):
        along = jnp.dot(db_ref[...], _onehot_diagonal(), preferred_element_type=F32, precision=HIGHEST)
        acc = jnp.zeros((8, NIDX), F32)
        for a in range(NBAND):
            acc = acc + jnp.dot(along[8 * a:8 * a + 8, :], _onehot_table(a), preferred_element_type=F32,
                                precision=HIGHEST)
        o_ref[...] = acc

    return pl.pallas_call(
        body, name=name, grid=(DEPTH,),
        in_specs=[pl.BlockSpec((None, NBAND * 8, CH * CH), lambda l: (l, 0, 0))],
        out_specs=pl.BlockSpec((None, 8, NIDX), lambda l: (l, 0, 0)),
        out_shape=jax.ShapeDtypeStruct((DEPTH, 8, NIDX), F32),
        compiler_params=_cp("parallel"),
    )(db)


def _bias_layout(flat):
    b = flat.reshape(DEPTH, NBAND, 8, CH, CH).transpose(0, 2, 1, 4, 3).reshape(DEPTH, 4, 2, KB, CH)
    pair = b.transpose(0, 1, 3, 2, 4).reshape(DEPTH, 4, KB, 128)
    first = jnp.pad(pair, ((0, 0), (0, 0), (0, CH), (0, 0)), constant_values=NEG_INF)
    second = jnp.pad(pair, ((0, 0), (0, 0), (CH, 0), (0, 0)), constant_values=NEG_INF)
    return jnp.concatenate([first, second], axis=3)


def _bias_unlayout(dbt):
    b = dbt.reshape(DEPTH, 4, NBAND, CH, 2, CH)
    return b.transpose(0, 2, 1, 4, 5, 3).reshape(DEPTH, NBAND * 8, CH * CH)


UNIT = 2 * CH
BAND2 = KB + CH


def _pair_weights(xt):
    x = xt.astype(F32)
    row = lax.broadcasted_iota(jnp.int32, (128, UNIT), 0)
    low = lax.broadcasted_iota(jnp.int32, (128, UNIT), 1) < HD
    swapped = pltpu.roll(x, HD, 1)
    same = (row < HD) == low
    first = jnp.where(same, jnp.where(low, x, swapped), 0.0)
    second = jnp.where(same, jnp.where(low, swapped, x), 0.0)
    return jnp.concatenate([first, second], axis=1).astype(BF16)


def _pair_rows(x):
    low = lax.broadcasted_iota(jnp.int32, (CH, 128), 1) < HD
    zero = jnp.zeros((CH, 128), x.dtype)
    parts = []
    for c in range(2):
        xc = x[c * CH:(c + 1) * CH, :]
        parts += [jnp.where(low, xc, zero), jnp.where(low, zero, xc)]
    return jnp.concatenate(parts, axis=0)


def _unpair(raw):
    b0, b1 = raw[:, 0:128], raw[:, 128:256]
    row = lax.broadcasted_iota(jnp.int32, (128, 128), 0)
    low = lax.broadcasted_iota(jnp.int32, (128, 128), 1) < HD
    top = jnp.where(low, b0, pltpu.roll(b1, HD, 1))
    bottom = jnp.where(low, pltpu.roll(b0, HD, 1), b1)
    return jnp.where(row < HD, top, bottom).T


def _scores_t(kb, qw, bias2, row0, padded):
    s = jnp.dot(kb, qw, preferred_element_type=F32) + bias2
    if padded:
        s = jnp.where(row0 + lax.broadcasted_iota(jnp.int32, (BAND2, 256), 0) >= PADR, s, NEG_INF)
    return s


def _unit_loops(s, unit):
    lax.fori_loop(0, PADR // UNIT, lambda u, c: unit(u, True, c), 0, unroll=2)
    lax.fori_loop(PADR // UNIT, s // UNIT, lambda u, c: unit(u, False, c), 0, unroll=7)


def _attn_fwd(name, kp, qt, vt, bias2, l):
    s = qt.shape[1]
    nu = s // UNIT

    def body(k_ref, qt_ref, vt_ref, b_ref, o_ref, lse_ref):
        def unit(u, padded, carry):
            r0 = pl.multiple_of(u * UNIT, UNIT)
            sc = _scores_t(k_ref[pl.ds(r0, BAND2), :], _pair_weights(qt_ref[:, pl.ds(r0, UNIT)]), b_ref[...],
                           r0, padded)
            top = jnp.max(sc, axis=0, keepdims=True)
            e = jnp.exp(sc - top)
            total = jnp.sum(e, axis=0, keepdims=True)
            raw = jnp.dot(vt_ref[:, pl.ds(r0, BAND2)], e.astype(BF16), preferred_element_type=F32) * (1.0 / total)
            o_ref[pl.ds(r0, UNIT), :] = _unpair(raw).astype(BF16)
            lse_ref[u] = jnp.broadcast_to(top + jnp.log(total), (8, 256))
            return carry

        _unit_loops(s, unit)

    return pl.pallas_call(
        body, name=name, grid=(AW // 128,),
        in_specs=[pl.BlockSpec((s + PADR, 128), lambda h: (0, h)),
                  pl.BlockSpec((128, s), lambda h: (h, 0)),
                  pl.BlockSpec((128, s + PADR), lambda h: (h, 0)),
                  pl.BlockSpec((None, None, BAND2, 256), lambda h: (l, h, 0, 0))],
        out_specs=[pl.BlockSpec((s, 128), lambda h: (0, h)),
                   pl.BlockSpec((None, nu, 8, 256), lambda h: (h, 0, 0, 0))],
        out_shape=[jax.ShapeDtypeStruct((s, AW), BF16), jax.ShapeDtypeStruct((4, nu, 8, 256), F32)],
        compiler_params=_cp("parallel"),
    )(kp, qt, vt, bias2)


def _attn_bwd(name, q, qt, kp, kt, vp, bias2, l, do, dot, lse, dl):
    s = q.shape[0]
    nu = s // UNIT

    def body(q_ref, qt_ref, k_ref, kt_ref, v_ref, b_ref, do_ref, dot_ref, lse_ref, dl_ref,
             dq_ref, dk_ref, dv_ref, db_ref):
        dk_ref[...] = jnp.zeros_like(dk_ref)
        dv_ref[...] = jnp.zeros_like(dv_ref)
        db_ref[...] = jnp.zeros_like(db_ref)

        def unit(u, padded, carry):
            r0 = pl.multiple_of(u * UNIT, UNIT)
            rows, band = pl.ds(r0, UNIT), pl.ds(r0, BAND2)
            sc = _scores_t(k_ref[band, :], _pair_weights(qt_ref[:, rows]), b_ref[...], r0, padded)
            pt = jnp.exp(sc - lse_ref[u][0:1, :])
            dpt = jnp.dot(v_ref[band, :], _pair_weights(dot_ref[:, rows]), preferred_element_type=F32)
            ds = pt * (dpt - dl_ref[u][0:1, :])
            db_ref[...] += ds[0:KB, 0:128] + ds[CH:BAND2, 128:256]
            dsb = ds.astype(BF16)
            dq_ref[rows, :] = _unpair(jnp.dot(kt_ref[:, band], dsb, preferred_element_type=F32))
            dk_ref[band, :] += jnp.dot(dsb, _pair_rows(q_ref[rows, :]), preferred_element_type=F32)
            dv_ref[band, :] += jnp.dot(pt.astype(BF16), _pair_rows(do_ref[rows, :]), preferred_element_type=F32)
            return carry

        _unit_loops(s, unit)

    row_q = pl.BlockSpec((s, 128), lambda h: (0, h))
    col_q = pl.BlockSpec((128, s), lambda h: (h, 0))
    row_k = pl.BlockSpec((s + PADR, 128), lambda h: (0, h))
    col_k = pl.BlockSpec((128, s + PADR), lambda h: (h, 0))
    stat = pl.BlockSpec((None, nu, 8, 256), lambda h: (h, 0, 0, 0))
    return pl.pallas_call(
        body, name=name, grid=(AW // 128,),
        in_specs=[row_q, col_q, row_k, col_k, row_k,
                  pl.BlockSpec((None, None, BAND2, 256), lambda h: (l, h, 0, 0)), row_q, col_q, stat, stat],
        out_specs=[row_q, row_k, row_k, pl.BlockSpec((None, KB, 128), lambda h: (h, 0, 0))],
        out_shape=[jax.ShapeDtypeStruct((s, AW), F32),
                   jax.ShapeDtypeStruct((s + PADR, AW), F32),
                   jax.ShapeDtypeStruct((s + PADR, AW), F32),
                   jax.ShapeDtypeStruct((4, KB, 128), F32)],
        compiler_params=_cp("parallel"),
    )(q, qt, kp, kt, vp, bias2, do, dot, lse, dl)


def _rowsum_layout(dl, nu):
    d = dl[:, :8].reshape(nu, 2, CH, 4, 2)
    d = d.transpose(3, 0, 1, 4, 2).reshape(4, nu, 1, 256)
    return jnp.broadcast_to(d, (4, nu, 8, 256))


def _rows_before(cur, prev, k):
    row = lax.broadcasted_iota(jnp.int32, cur.shape, 0)
    return jnp.where(row >= k, pltpu.roll(cur, k, 0), pltpu.roll(prev, k, 0))


def _rows_after(cur, nxt, k):
    n = cur.shape[0]
    row = lax.broadcasted_iota(jnp.int32, cur.shape, 0)
    return jnp.where(row < n - k, pltpu.roll(cur, n - k, 0), pltpu.roll(nxt, n - k, 0))


def _pool_window_lanes():
    lg = lax.broadcasted_iota(jnp.int32, (1, PWD), 1) // 64
    return lg, jnp.where(lg == 0, 2.0, jnp.where(lg == 1, 4.0, jnp.where(lg == 2, 8.0, 16.0))).astype(F32)


def _pool_mean_minus_token(u, up, row0):
    lg, wv = _pool_window_lanes()
    sums = []
    c, p = u, up
    for k in (1, 2, 4, 8):
        c2 = c + _rows_before(c, p, k)
        p = p + pltpu.roll(p, k, 0)
        c = c2
        sums.append(c)
    win = jnp.where(lg == 0, sums[0], jnp.where(lg == 1, sums[1], jnp.where(lg == 2, sums[2], sums[3])))
    pos1 = (row0 + lax.broadcasted_iota(jnp.int32, u.shape, 0) + 1).astype(F32)
    cnt = jnp.minimum(pos1, wv)
    return win / cnt - u, cnt


def _conv_taps(z, zp, w0, w1, w2):
    z1 = _rows_before(z, zp, 1)
    z2 = _rows_before(z, zp, 2)
    return (w0 * z2 + w1 * z1) + w2 * z, z1, z2


CP_TM = 512


def _convpool_fwd(name, p, o, cw, pwbd, ps):
    s = p.shape[0]
    tm = CP_TM
    nb = s // tm

    def body(gb_ref, gc_ref, hin_ref, u_ref, gcp_ref, hinp_ref, up_ref, o_ref, cw_ref, pw_ref, ps_ref, mix_ref):
        i = pl.program_id(0)
        has_prev = i > 0
        z = gc_ref[...] * hin_ref[...]
        zp = jnp.where(has_prev, gcp_ref[...] * hinp_ref[...], 0.0)
        y3, _, _ = _conv_taps(z, zp, cw_ref[0:1, :], cw_ref[1:2, :], cw_ref[2:3, :])
        m, _ = _pool_mean_minus_token(u_ref[...], jnp.where(has_prev, up_ref[...], 0.0), i * tm)
        yp = jnp.dot(m.astype(BF16), pw_ref[...].astype(BF16), preferred_element_type=F32) * ps_ref[...]
        mix_ref[:, 0:AW] = o_ref[...]
        mix_ref[:, AW:AW + CW] = (gb_ref[...] * y3).astype(BF16)
        mix_ref[:, AW + CW:D] = yp.astype(BF16)

    def cur(col):
        return pl.BlockSpec((tm, CW), lambda i: (i, col))

    def prev(col):
        return pl.BlockSpec((tm, CW), lambda i: (jnp.maximum(i - 1, 0), col))

    def whole(a):
        return pl.BlockSpec(a.shape, lambda i: (0,) * a.ndim)

    return pl.pallas_call(
        body, name=name, grid=(nb,),
        in_specs=[cur(6), cur(7), cur(8), cur(9), prev(7), prev(8), prev(9),
                  pl.BlockSpec((tm, AW), lambda i: (i, 0)), whole(cw), whole(pwbd), whole(ps)],
        out_specs=pl.BlockSpec((tm, D), lambda i: (i, 0)),
        out_shape=jax.ShapeDtypeStruct((s, D), BF16),
        compiler_params=_cp("parallel"),
    )(p, p, p, p, p, p, p, o, cw, pwbd, ps)


def _convpool_bwd(name, p, dmix, cw, pwbd, ps):
    s = p.shape[0]
    tm = CP_TM
    nb = s // tm

    def body(gb_ref, gc_ref, hin_ref, u_ref, gcp_ref, hinp_ref, up_ref, gbn_ref, dyc_ref, dyp_ref, dycn_ref, dypn_ref,
             cw_ref, pw_ref, ps_ref, dcp_ref, dw0_ref, dw1_ref, dw2_ref, dps_ref, dpw_ref):
        i = pl.program_id(0)
        has_prev = i > 0
        has_next = i < nb - 1
        w0, w1, w2 = cw_ref[0:1, :], cw_ref[1:2, :], cw_ref[2:3, :]
        gb, gc, hin = gb_ref[...], gc_ref[...], hin_ref[...]
        dyc = dyc_ref[...]
        z = gc * hin
        zp = jnp.where(has_prev, gcp_ref[...] * hinp_ref[...], 0.0)
        y3, z1, z2 = _conv_taps(z, zp, w0, w1, w2)
        dy3 = dyc * gb
        dy3n = jnp.where(has_next, dycn_ref[...] * gbn_ref[...], 0.0)
        dz = w2 * dy3 + w1 * _rows_after(dy3, dy3n, 1) + w0 * _rows_after(dy3, dy3n, 2)
        pw = pw_ref[...].astype(BF16)
        psv = ps_ref[...]
        m, cnt = _pool_mean_minus_token(u_ref[...], jnp.where(has_prev, up_ref[...], 0.0), i * tm)
        mb = m.astype(BF16)
        dyp = dyp_ref[...]
        dmp = (dyp * psv).astype(BF16)
        dmpn = jnp.where(has_next, dypn_ref[...] * psv, 0.0).astype(BF16)
        nt = (((1,), (1,)), ((), ()))
        dm = lax.dot_general(dmp, pw, nt, preferred_element_type=F32)
        dmn = lax.dot_general(dmpn, pw, nt, preferred_element_type=F32)
        lg, wv = _pool_window_lanes()
        cc, cn = dm / cnt, dmn / wv
        sums = []
        for k in (1, 2, 4, 8):
            c2 = cc + _rows_after(cc, cn, k)
            cn = cn + pltpu.roll(cn, tm - k, 0)
            cc = c2
            sums.append(cc)
        du = jnp.where(lg == 0, sums[0], jnp.where(lg == 1, sums[1], jnp.where(lg == 2, sums[2], sums[3]))) - dm
        dcp_ref[:, 0:CW] = (dyc * y3).astype(BF16)
        dcp_ref[:, CW:2 * CW] = (dz * hin).astype(BF16)
        dcp_ref[:, 2 * CW:3 * CW] = (dz * gc).astype(BF16)
        dcp_ref[:, 3 * CW:4 * CW] = du.astype(BF16)
        parts = (jnp.sum(dy3 * z2, axis=0, keepdims=True),
                 jnp.sum(dy3 * z1, axis=0, keepdims=True),
                 jnp.sum(dy3 * z, axis=0, keepdims=True),
                 jnp.sum(dyp * jnp.dot(mb, pw, preferred_element_type=F32), axis=0, keepdims=True),
                 lax.dot_general(mb, dmp, (((0,), (0,)), ((), ())), preferred_element_type=F32))
        accs = (dw0_ref, dw1_ref, dw2_ref, dps_ref, dpw_ref)

        @pl.when(i == 0)
        def _():
            for a, v in zip(accs, parts):
                a[...] = v

        @pl.when(i > 0)
        def _():
            for a, v in zip(accs, parts):
                a[...] += v

    def cur(col):
        return pl.BlockSpec((tm, CW), lambda i: (i, col))

    def prev(col):
        return pl.BlockSpec((tm, CW), lambda i: (jnp.maximum(i - 1, 0), col))

    def nxt(col):
        return pl.BlockSpec((tm, CW), lambda i: (jnp.minimum(i + 1, nb - 1), col))

    def whole(shape):
        return pl.BlockSpec(shape, lambda i: (0,) * len(shape))

    row = jax.ShapeDtypeStruct((1, CW), F32)
    return pl.pallas_call(
        body, name=name, grid=(nb,),
        in_specs=[cur(6), cur(7), cur(8), cur(9), prev(7), prev(8), prev(9), nxt(6),
                  cur(0), cur(1), nxt(0), nxt(1), whole(cw.shape), whole(pwbd.shape), whole(ps.shape)],
        out_specs=[pl.BlockSpec((tm, D), lambda i: (i, 0)), whole((1, CW)), whole((1, CW)), whole((1, CW)),
                   whole((1, PWD)), whole((PWD, PWD))],
        out_shape=[jax.ShapeDtypeStruct((s, D), BF16), row, row, row, row,
                   jax.ShapeDtypeStruct((PWD, PWD), F32)],
        compiler_params=_cp("arbitrary"),
    )(p, p, p, p, p, p, p, p, dmix, dmix, dmix, dmix, cw, pwbd, ps)


def _qkv_bwd(name, p, dq, dkp, dvp, dcp, qg, kg):
    s = p.shape[0]
    tm = 512
    off = PADR // tm

    def body(pq_ref, pk_ref, dq_ref, dk_ref, dv_ref, dcp_ref, qg_ref, kg_ref, dp_ref, dqg_ref, dkg_ref):
        i = pl.program_id(0)
        hm = _head_mean_matrix()

        def nrm_bwd(x, g, dy):
            r = lax.rsqrt(_head_mean(x * x, hm) + EPS)
            xn = x * r
            dxn = dy * g
            dx = r * (dxn - xn * _head_mean(dxn * xn, hm))
            dg = jnp.sum(dy * xn, axis=0, keepdims=True)
            dg = (dg[:, 0:128] + dg[:, 128:256]) + (dg[:, 256:384] + dg[:, 384:512])
            return dx, dg + pltpu.roll(dg, HD, 1)

        dxq, dgq = nrm_bwd(pq_ref[...], qg_ref[...], dq_ref[...] * 0.125)
        dxk, dgk = nrm_bwd(pk_ref[...], kg_ref[...], dk_ref[...])
        dp_ref[:, 0:AW] = dxq.astype(BF16)
        dp_ref[:, AW:2 * AW] = dxk.astype(BF16)
        dp_ref[:, 2 * AW:3 * AW] = dv_ref[...].astype(BF16)
        dp_ref[:, 3 * AW:DIN] = dcp_ref[...]

        @pl.when(i == 0)
        def _():
            dqg_ref[...] = dgq
            dkg_ref[...] = dgk

        @pl.when(i > 0)
        def _():
            dqg_ref[...] += dgq
            dkg_ref[...] += dgk

    gspec = pl.BlockSpec((1, AW), lambda i: (0, 0))
    gout = pl.BlockSpec((1, 128), lambda i: (0, 0))
    return pl.pallas_call(
        body, name=name, grid=(s // tm,),
        in_specs=[pl.BlockSpec((tm, AW), lambda i: (i, 0)), pl.BlockSpec((tm, AW), lambda i: (i, 1)),
                  pl.BlockSpec((tm, AW), lambda i: (i, 0)),
                  pl.BlockSpec((tm, AW), lambda i: (i + off, 0)),
                  pl.BlockSpec((tm, AW), lambda i: (i + off, 0)),
                  pl.BlockSpec((tm, D), lambda i: (i, 0)), gspec, gspec],
        out_specs=[pl.BlockSpec((tm, DIN), lambda i: (i, 0)), gout, gout],
        out_shape=[jax.ShapeDtypeStruct((s, DIN), BF16), jax.ShapeDtypeStruct((1, 128), F32),
                   jax.ShapeDtypeStruct((1, 128), F32)],
        compiler_params=_cp("arbitrary"),
    )(p, p, dq, dkp, dvp, dcp, qg, kg)


def _loss_grad(name, y, t):
    s = y.shape[0]
    tm = 512

    def body(y_ref, t_ref, dy_ref, dyb_ref, l_ref):
        i = pl.program_id(0)
        e = y_ref[...] - t_ref[...]
        dy = e * (1.0 / D)
        dy_ref[...] = dy
        dyb_ref[...] = dy.astype(BF16)
        part = 0.5 * jnp.sum(jnp.mean(e * e, axis=-1, keepdims=True), axis=0, keepdims=True)

        @pl.when(i == 0)
        def _():
            l_ref[...] = part

        @pl.when(i > 0)
        def _():
            l_ref[...] += part

    blk = pl.BlockSpec((tm, D), lambda i: (i, 0))
    return pl.pallas_call(
        body, name=name, grid=(s // tm,),
        in_specs=[blk, blk],
        out_specs=[blk, blk, pl.BlockSpec((1, 1), lambda i: (0, 0))],
        out_shape=[jax.ShapeDtypeStruct((s, D), F32), jax.ShapeDtypeStruct((s, D), BF16),
                   jax.ShapeDtypeStruct((1, 1), F32)],
        compiler_params=_cp("arbitrary"),
    )(y, t)


def _mm_nt_relu(name, dxb, w, l, a):
    s = dxb.shape[0]
    tm = MM_ROWS

    def body(d_ref, w_ref, a_ref, o_ref):
        df = lax.dot_general(d_ref[...], w_ref[...], NT_DIMS, preferred_element_type=F32)
        o_ref[...] = (df * (2.0 * jnp.maximum(a_ref[...].astype(F32), 0.0))).astype(BF16)

    return pl.pallas_call(
        body, name=name, grid=(s // tm,),
        in_specs=[pl.BlockSpec((tm, D), lambda i: (i, 0)),
                  pl.BlockSpec((None, DFF, D), lambda i: (l, 0, 0)),
                  pl.BlockSpec((tm, DFF), lambda i: (i, 0))],
        out_specs=pl.BlockSpec((tm, DFF), lambda i: (i, 0)),
        out_shape=jax.ShapeDtypeStruct((s, DFF), BF16),
        compiler_params=_cp("parallel"),
    )(dxb, w, a)


def _proj_out_bwd(name, dxb, w, l, mix):
    s = dxb.shape[0]
    tm = _mm_rows(D, D)

    def body(d_ref, w_ref, o_ref, do_ref, dot_ref, dcp_ref, dl_ref):
        d = d_ref[...]
        wa, wc = w_ref[0:AW, :], w_ref[AW:D, :]
        do = lax.dot_general(d, wa, NT_DIMS, preferred_element_type=F32)
        do_ref[...] = do.astype(BF16)
        dot_ref[...] = lax.dot_general(wa, d, NT_DIMS, preferred_element_type=F32).astype(BF16)
        dcp_ref[...] = lax.dot_general(d, wc, NT_DIMS, preferred_element_type=F32)
        head = lax.broadcasted_iota(jnp.int32, (AW, 128), 0) // HD
        pick = jnp.where(head == lax.broadcasted_iota(jnp.int32, (AW, 128), 1), 1.0, 0.0).astype(BF16)
        dl_ref[...] = _two_pass_dot(do * o_ref[...].astype(F32), pick)

    return pl.pallas_call(
        body, name=name, grid=(s // tm,),
        in_specs=[pl.BlockSpec((tm, D), lambda i: (i, 0)),
                  pl.BlockSpec((None, D, D), lambda i: (l, 0, 0)),
                  pl.BlockSpec((tm, AW), lambda i: (i, 0))],
        out_specs=[pl.BlockSpec((tm, AW), lambda i: (i, 0)), pl.BlockSpec((AW, tm), lambda i: (0, i)),
                   pl.BlockSpec((tm, D - AW), lambda i: (i, 0)), pl.BlockSpec((tm, 128), lambda i: (i, 0))],
        out_shape=[jax.ShapeDtypeStruct((s, AW), BF16), jax.ShapeDtypeStruct((AW, s), BF16),
                   jax.ShapeDtypeStruct((s, D - AW), F32), jax.ShapeDtypeStruct((s, 128), F32)],
        compiler_params=_cp("parallel"),
    )(dxb, w, mix)


def _mm_nt_normbwd(name, gy, w, l, x, g, dres, dep):
    s, k = gy.shape
    tm = MM_ROWS

    def body(gy_ref, w_ref, x_ref, g_ref, dr_ref, dep_ref, dx_ref, dxb_ref, dg_ref):
        del dep_ref
        i = pl.program_id(0)
        dh = lax.dot_general(gy_ref[...], w_ref[...], NT_DIMS, preferred_element_type=F32)
        xv = x_ref[...]
        r = _inv_rms(xv)
        xn = xv * r
        dxn = dh * g_ref[...]
        dx = r * (dxn - xn * jnp.mean(dxn * xn, axis=-1, keepdims=True)) + dr_ref[...]
        dx_ref[...] = dx
        dxb_ref[...] = dx.astype(BF16)
        part = jnp.sum(dh * xn, axis=0, keepdims=True)

        @pl.when(i == 0)
        def _():
            dg_ref[...] = part

        @pl.when(i > 0)
        def _():
            dg_ref[...] += part

    blk = pl.BlockSpec((tm, D), lambda i: (i, 0))
    vec = pl.BlockSpec((1, D), lambda i: (0, 0))
    return pl.pallas_call(
        body, name=name, grid=(s // tm,),
        in_specs=[pl.BlockSpec((tm, k), lambda i: (i, 0)),
                  pl.BlockSpec((None, D, k), lambda i: (l, 0, 0)), blk, vec, blk, ANY],
        out_specs=[blk, blk, vec],
        out_shape=[jax.ShapeDtypeStruct((s, D), F32), jax.ShapeDtypeStruct((s, D), BF16),
                   jax.ShapeDtypeStruct((1, D), F32)],
        compiler_params=_cp("arbitrary"),
    )(gy, w, x, g, dres, dep)


def _mm_tn(name, a, b, tma, tnb, relu2=False):
    s, m = a.shape
    n = b.shape[1]

    def body(a_ref, b_ref, o_ref):
        av = _relu2(a_ref[...]) if relu2 else a_ref[...]
        o_ref[...] = lax.dot_general(av, b_ref[...], (((0,), (0,)), ((), ())),
                                     preferred_element_type=F32).astype(BF16)

    return pl.pallas_call(
        body, name=name, grid=(m // tma, n // tnb),
        in_specs=[pl.BlockSpec((s, tma), lambda i, j: (0, i), pipeline_mode=pl.Buffered(1) if m == tma else None),
                  pl.BlockSpec((s, tnb), lambda i, j: (0, j))],
        out_specs=pl.BlockSpec((tma, tnb), lambda i, j: (i, j)),
        out_shape=jax.ShapeDtypeStruct((m, n), BF16),
        compiler_params=_cp("parallel", "parallel"),
    )(a, b)


def _adamw_math(gv, wv, mv, vv):
    mn = ADAM_B1 * mv + (1.0 - ADAM_B1) * gv
    vn = ADAM_B2 * vv + (1.0 - ADAM_B2) * jnp.square(gv)
    m_hat = mn / (1.0 - ADAM_B1 ** ADAM_STEP)
    v_hat = vn / (1.0 - ADAM_B2 ** ADAM_STEP)
    return gv, -ADAM_LR * (m_hat / (jnp.sqrt(v_hat) + ADAM_EPS) + ADAM_WD * wv), mn, vn


def _adamw(name, g, w, m, v):
    r, c = g.shape
    tm = 256 if r % 256 == 0 else r

    def body(g_ref, w_ref, m_ref, v_ref, go_ref, d_ref, mo_ref, vo_ref):
        go_ref[...], d_ref[...], mo_ref[...], vo_ref[...] = _adamw_math(g_ref[...], w_ref[...], m_ref[...], v_ref[...])

    blk = pl.BlockSpec((tm, c), lambda i: (i, 0))
    return pl.pallas_call(
        body, name=name, grid=(r // tm,),
        in_specs=[blk] * 4, out_specs=[blk] * 4,
        out_shape=[jax.ShapeDtypeStruct((r, c), F32)] * 4,
        compiler_params=_cp("parallel"),
    )(g, w, m, v)


def _place():
    x, y, c = lax.axis_index("x"), lax.axis_index("y"), lax.axis_index("c")
    chips = [(1 - x, y), (x, 1 - y), (1 - x, 1 - y)]
    return x, y, c, chips


BLOCK_AXIS = (2, 1, 2, 1)
LARGE_DIMS = ((D, DIN), (D, D), (D, DFF), (DFF, D))


def _full_shape(t, layers, dtype):
    r, c = LARGE_DIMS[t]
    return jax.ShapeDtypeStruct((layers, r, c), dtype)


def _cast_into_full(name, t, shard, b1, dep):
    _, r, c = shard.shape
    tm = min(512, r)
    if BLOCK_AXIS[t] == 1:
        out_spec = pl.BlockSpec((None, tm, c), lambda l, i, br: (l, br[0] * (r // tm) + i, 0))
    else:
        out_spec = pl.BlockSpec((None, tm, c), lambda l, i, br: (l, i, br[0]))

    def body(b_ref, x_ref, dep_ref, o_ref):
        del b_ref, dep_ref
        o_ref[...] = x_ref[...].astype(BF16)

    return pl.pallas_call(
        body, name=name,
        grid_spec=pltpu.PrefetchScalarGridSpec(
            num_scalar_prefetch=1, grid=(DEPTH, r // tm),
            in_specs=[pl.BlockSpec((None, tm, c), lambda l, i, br: (l, i, 0)), ANY],
            out_specs=out_spec),
        out_shape=_full_shape(t, DEPTH, BF16),
        compiler_params=_cp("parallel", "parallel"),
    )(b1, shard, dep)


HBM = pl.BlockSpec(memory_space=pltpu.HBM)
SEM = pl.BlockSpec(memory_space=pltpu.SEMAPHORE)
DATAFLOW = pltpu.SideEffectType.DATAFLOW_SIDE_EFFECTING


def _half(ref, l, t, b, c):
    r, cols = LARGE_DIMS[t]
    if BLOCK_AXIS[t] == 1:
        n = r // 8
        return ref.at[l, pl.ds(pl.multiple_of(b * (2 * n) + c * n, 16), n), :]
    n, w = r // 2, cols // 4
    return ref.at[l, pl.ds(pl.multiple_of(c * n, 16), n), pl.ds(pl.multiple_of(b * w, 128), w)]


def _gather_start(name, layers, ts, fulls):
    n = len(ts)

    def body(*refs):
        f_refs, sems = refs[n:2 * n], refs[2 * n:2 * n + 2 * len(layers)]
        x, y, c, chips = _place()
        for i, l in enumerate(layers):
            for k, t in enumerate(ts):
                own = _half(f_refs[k], l, t, 2 * x + y, c)
                for j, (cx, cy) in enumerate(chips):
                    pltpu.make_async_remote_copy(src_ref=own, dst_ref=own, send_sem=sems[2 * i].at[3 * t + j],
                                                 recv_sem=sems[2 * i + 1].at[3 * t + j], device_id=(cx, cy, c),
                                                 device_id_type=MESH).start()
        refs[-1][...] = jnp.zeros((8, 128), F32)

    outs = pl.pallas_call(
        body, name=name,
        in_specs=[HBM] * n,
        out_specs=[HBM] * n + [SEM] * (2 * len(layers)) + [pl.BlockSpec(memory_space=pltpu.VMEM)],
        out_shape=[pltpu.HBM(f.shape, f.dtype) for f in fulls]
        + [pltpu.SemaphoreType.DMA((12,))] * (2 * len(layers)) + [jax.ShapeDtypeStruct((8, 128), F32)],
        input_output_aliases={k: k for k in range(n)},
        compiler_params=pltpu.CompilerParams(has_side_effects=DATAFLOW),
    )(*[pltpu.with_memory_space_constraint(f, pltpu.HBM) for f in fulls])
    return outs[0:n], {l: (outs[n + 2 * i], outs[n + 1 + 2 * i]) for i, l in enumerate(layers)}, outs[-1]


def _gather_wait(name, l, ts, fulls, sems, after):
    def body(*refs):
        send_sems, recv_sems, f_refs = refs[4], refs[5], refs[7:11]
        x, y, c, chips = _place()
        for t in ts:
            own = _half(f_refs[t], l, t, 2 * x + y, c)
            for j, (cx, cy) in enumerate(chips):
                landed = _half(f_refs[t], l, t, 2 * cx + cy, c)
                pltpu.make_async_remote_copy(src_ref=own, dst_ref=landed, send_sem=send_sems.at[3 * t + j],
                                             recv_sem=recv_sems.at[3 * t + j], device_id=(cx, cy, c),
                                             device_id_type=MESH).wait()

    return pl.pallas_call(
        body, name=name,
        in_specs=[HBM] * 4 + [SEM, SEM, ANY], out_specs=[HBM] * 4,
        out_shape=[pltpu.HBM(s.shape, s.dtype) for s in (_full_shape(t, DEPTH, BF16) for t in range(4))],
        input_output_aliases={t: t for t in range(4)},
        compiler_params=pltpu.CompilerParams(has_side_effects=DATAFLOW),
    )(*fulls, sems[0], sems[1], after)


def _pass_on(name, l, ts, fulls):
    def body(*refs):
        f_refs, send_sems, recv_sems = refs[4:8], refs[8], refs[9]
        x, y, c, chips = _place()

        def copy(t, j, half):
            cx, cy = chips[j]
            part = _half(f_refs[t], l, t, 2 * cx + cy, half)
            return pltpu.make_async_remote_copy(src_ref=part, dst_ref=part, send_sem=send_sems.at[3 * t + j],
                                                recv_sem=recv_sems.at[3 * t + j], device_id=(x, y, 1 - c),
                                                device_id_type=MESH)

        for t in ts:
            for j in range(3):
                copy(t, j, c).start()
        for t in ts:
            for j in range(3):
                copy(t, j, 1 - c).wait_recv()
                copy(t, j, c).wait_send()

    return pl.pallas_call(
        body, name=name,
        in_specs=[ANY] * 4, out_specs=[ANY] * 4,
        out_shape=[_full_shape(t, DEPTH, BF16) for t in range(4)],
        input_output_aliases={t: t for t in range(4)},
        scratch_shapes=[pltpu.SemaphoreType.DMA((12,)), pltpu.SemaphoreType.DMA((12,))],
    )(*fulls)


def _block2d(ref, t, b):
    r, cols = LARGE_DIMS[t]
    if BLOCK_AXIS[t] == 1:
        return ref.at[pl.ds(pl.multiple_of(b * (r // 4), 16), r // 4), :]
    return ref.at[:, pl.ds(pl.multiple_of(b * (cols // 4), 128), cols // 4)]


def _block_dims(t):
    r, cols = LARGE_DIMS[t]
    return (r // 4, cols) if BLOCK_AXIS[t] == 1 else (r, cols // 4)


def _reduce_copies(ts, g_refs, r_refs, send_sems, recv_sems):
    _, _, c, chips = _place()
    return [pltpu.make_async_remote_copy(src_ref=_block2d(g_refs[i], t, 2 * cx + cy), dst_ref=r_refs[i].at[j],
                                         send_sem=send_sems.at[3 * i + j], recv_sem=recv_sems.at[3 * i + j],
                                         device_id=(cx, cy, c), device_id_type=MESH)
            for i, t in enumerate(ts) for j, (cx, cy) in enumerate(chips)]


def _reduce_start(name, ts, grads):
    n = len(ts)

    def body(*refs):
        for cp in _reduce_copies(ts, refs[n:2 * n], refs[2 * n:3 * n], refs[3 * n], refs[3 * n + 1]):
            cp.start()
        refs[3 * n + 2][...] = jnp.zeros((8, 128), F32)

    outs = pl.pallas_call(
        body, name=name,
        in_specs=[HBM] * n,
        out_specs=[HBM] * (2 * n) + [SEM, SEM, pl.BlockSpec(memory_space=pltpu.VMEM)],
        out_shape=[pltpu.HBM(g.shape, BF16) for g in grads]
        + [pltpu.HBM((3,) + _block_dims(t), BF16) for t in ts]
        + [pltpu.SemaphoreType.DMA((3 * n,)), pltpu.SemaphoreType.DMA((3 * n,)), jax.ShapeDtypeStruct((8, 128), F32)],
        input_output_aliases={i: i for i in range(n)},
        compiler_params=pltpu.CompilerParams(has_side_effects=DATAFLOW),
    )(*[pltpu.with_memory_space_constraint(g, pltpu.HBM) for g in grads])
    return outs[0:n], outs[n:2 * n], (outs[2 * n], outs[2 * n + 1]), outs[2 * n + 2]


def _reduce_wait(name, ts, grads, landing, sems, afters):
    n = len(ts)
    first_out = 2 * n + 2 + len(afters)

    def body(*refs):
        for cp in _reduce_copies(ts, refs[first_out:first_out + n], refs[first_out + n:first_out + 2 * n],
                                 refs[2 * n], refs[2 * n + 1]):
            cp.wait()

    outs = pl.pallas_call(
        body, name=name,
        in_specs=[HBM] * (2 * n) + [SEM, SEM] + [ANY] * len(afters), out_specs=[HBM] * (2 * n),
        out_shape=[pltpu.HBM(g.shape, BF16) for g in grads] + [pltpu.HBM(r.shape, BF16) for r in landing],
        input_output_aliases={i: i for i in range(2 * n)},
        compiler_params=pltpu.CompilerParams(has_side_effects=DATAFLOW),
    )(*grads, *landing, sems[0], sems[1], *afters)
    return outs[0:n], outs[n:2 * n]


def _add4(name, t, own, landed, b1):
    rb, cb = _block_dims(t)
    tm = min(512, rb)
    if BLOCK_AXIS[t] == 1:
        own_spec = pl.BlockSpec((tm, cb), lambda i, br: (br[0] * (rb // tm) + i, 0))
    else:
        own_spec = pl.BlockSpec((tm, cb), lambda i, br: (i, br[0]))

    def body(b_ref, o_ref, r0_ref, r1_ref, r2_ref, s_ref):
        del b_ref
        s_ref[...] = ((o_ref[...].astype(F32) + r0_ref[...].astype(F32))
                      + (r1_ref[...].astype(F32) + r2_ref[...].astype(F32))).astype(BF16)

    def got(j):
        return pl.BlockSpec((None, tm, cb), lambda i, br: (j, i, 0))

    return pl.pallas_call(
        body, name=name,
        grid_spec=pltpu.PrefetchScalarGridSpec(
            num_scalar_prefetch=1, grid=(rb // tm,),
            in_specs=[own_spec, got(0), got(1), got(2)],
            out_specs=pl.BlockSpec((tm, cb), lambda i, br: (i, 0))),
        out_shape=jax.ShapeDtypeStruct((rb, cb), BF16),
        compiler_params=_cp("parallel"),
    )(b1, own, landed, landed, landed)


def _swap_sib(name, sums):
    def body(*refs):
        s_refs, t_refs, send_sems, recv_sems = refs[0:4], refs[4:8], refs[8], refs[9]
        x, y, c, _ = _place()
        cps = [pltpu.make_async_remote_copy(src_ref=s_refs[t], dst_ref=t_refs[t], send_sem=send_sems.at[t],
                                            recv_sem=recv_sems.at[t], device_id=(x, y, 1 - c), device_id_type=MESH)
               for t in range(4)]
        for cp in cps:
            cp.start()
        for cp in cps:
            cp.wait()

    return pl.pallas_call(
        body, name=name,
        in_specs=[ANY] * 4, out_specs=[ANY] * 4,
        out_shape=[jax.ShapeDtypeStruct(s.shape, BF16) for s in sums],
        scratch_shapes=[pltpu.SemaphoreType.DMA((4,)), pltpu.SemaphoreType.DMA((4,))],
    )(*sums)


def _adamw_pair(name, l, s_own, s_sib, w, m, v, outs):
    rb, cb = s_own.shape
    tm = min(512, rb)

    def body(a_ref, b_ref, w_ref, m_ref, v_ref, g0, d0, m0, v0, go_ref, d_ref, mo_ref, vo_ref):
        del g0, d0, m0, v0
        gv = a_ref[...].astype(F32) + b_ref[...].astype(F32)
        go_ref[...], d_ref[...], mo_ref[...], vo_ref[...] = _adamw_math(gv, w_ref[...], m_ref[...], v_ref[...])

    part = pl.BlockSpec((tm, cb), lambda i: (i, 0))
    layer = pl.BlockSpec((None, tm, cb), lambda i: (l, i, 0))
    return pl.pallas_call(
        body, name=name, grid=(rb // tm,),
        in_specs=[part, part, layer, layer, layer] + [ANY] * 4,
        out_specs=[layer] * 4,
        out_shape=[jax.ShapeDtypeStruct((DEPTH, rb, cb), F32)] * 4,
        input_output_aliases={5 + i: i for i in range(4)},
        compiler_params=_cp("parallel"),
    )(s_own, s_sib, w, m, v, *outs)


def _all_gather8(name, v, dep):
    m_per, n = v.shape

    def body(v_ref, dep_ref, out_ref, send_sems, recv_sems, local_sem):
        del dep_ref
        x, y, c, chips = _place()
        me, sib = (x, y, c), (x, y, 1 - c)

        def rows(px, py, pc):
            return out_ref.at[pl.ds((4 * px + 2 * py + pc) * m_per, m_per), :]

        def copy(k, block, to, src=None):
            return pltpu.make_async_remote_copy(
                src_ref=rows(*block) if src is None else src, dst_ref=rows(*block),
                send_sem=send_sems.at[k], recv_sem=recv_sems.at[k], device_id=to, device_id_type=MESH)

        mine = pltpu.make_async_copy(v_ref, rows(*me), local_sem)
        mine.start()
        first = [copy(0, me, sib, src=v_ref)]
        first += [copy(1 + j, me, (*chip, c), src=v_ref) for j, chip in enumerate(chips)]
        for cp in first:
            cp.start()
        passed = [copy(4 + j, (*chip, c), sib) for j, chip in enumerate(chips)]
        for j, chip in enumerate(chips):
            copy(1 + j, (*chip, c), me).wait_recv()
            passed[j].start()
        copy(0, sib, me).wait_recv()
        for j, chip in enumerate(chips):
            copy(4 + j, (*chip, 1 - c), me).wait_recv()
        for cp in first + passed:
            cp.wait_send()
        mine.wait()

    return pl.pallas_call(
        body, name=name,
        out_shape=jax.ShapeDtypeStruct((8 * m_per, n), v.dtype),
        in_specs=[pl.BlockSpec(memory_space=pltpu.VMEM), ANY],
        out_specs=pl.BlockSpec(memory_space=pltpu.VMEM),
        scratch_shapes=[pltpu.SemaphoreType.DMA((7,)), pltpu.SemaphoreType.DMA((7,)), pltpu.SemaphoreType.DMA],
    )(v, dep)


def _sum8(name, g):
    def body(g_ref, o_ref):
        acc = g_ref[0]
        for d in range(1, 8):
            acc = acc + g_ref[d]
        o_ref[...] = acc

    return pl.pallas_call(body, name=name, out_shape=jax.ShapeDtypeStruct(g.shape[1:], F32))(g)


def _pack(parts):
    flat = []
    for a in parts:
        a = a.reshape(-1)
        flat.append(jnp.pad(a, (0, (-a.shape[0]) % 128)))
    cat = jnp.concatenate(flat)
    cat = jnp.pad(cat, (0, (-cat.shape[0]) % 1024))
    return cat.reshape(-1, 128)


def _unpack(packed, shapes):
    flat = packed.reshape(-1)
    out, at = [], 0
    for shp in shapes:
        n = 1
        for d in shp:
            n *= d
        out.append(flat[at:at + n].reshape(shp))
        at += n + (-n) % 128
    return out


def _local_step(x, target, layer_weights, on_grads, small):
    qg_all = jnp.tile(small["q_norm_g"], (1, 8))
    kg_all = jnp.tile(small["k_norm_g"], (1, 8))
    bias_all = _bias_layout(_bias_expand("bias_expand", jnp.pad(small["rel_bias"], ((0, 0), (0, 0), (0, NIDX - 257)))))
    same_group = jnp.eye(4, dtype=F32)[None, :, None, :, None]
    pwbd_all = (small["pool_w"][:, :, :, None, :] * same_group).reshape(DEPTH, PWD, PWD)
    saved = []
    xin = x
    h = _rmsnorm("norm_first", x, small["norm1_g"][0:1])
    for l in range(DEPTH):
        w_in = layer_weights(l, (0,), xin)[0]
        qg, kg = qg_all[l:l + 1], kg_all[l:l + 1]
        cw, pwbd, ps = small["conv_w"][l], pwbd_all[l], small["pool_scale"][l:l + 1]
        p = _mm_nn(f"proj_in_{l}", h, w_in, l, F32)
        q, qt, kp, kt, vp, vt = _qkv(f"qkv_{l}", p, qg, kg)
        o, lse = _attn_fwd(f"attn_fwd_{l}", kp, qt, vt, bias_all, l)
        w_in, w_out, w_1, w_2 = layer_weights(l, (1, 2, 3), o)
        mix = _convpool_fwd(f"convpool_fwd_{l}", p, o, cw, pwbd, ps)
        x1, h2 = _mm_res_norm(f"proj_out_{l}", mix, w_out, l, xin, small["norm2_g"][l:l + 1])
        gnext = small["norm1_g"][(l + 1) % DEPTH][None]
        a, x2, hnext = _mlp_fwd(f"mlp_{l}", h2, w_1, w_2, l, x1, gnext)
        saved.append(dict(xin=xin, h=h, p=p, q=q, qt=qt, kp=kp, kt=kt, vp=vp, mix=mix, x1=x1, h2=h2, a=a, lse=lse,
                          qg=qg, kg=kg, cw=cw, pwbd=pwbd, ps=ps))
        xin, h = x2, hnext

    dx, dxb, loss = _loss_grad("loss_grad", xin, target)
    raw = {k: [None] * DEPTH for k in ("dg1", "dqg", "dkg", "db", "dw0", "dw1", "dw2", "dpw", "dps", "dg2")}
    for l in reversed(range(DEPTH)):
        sv = saved[l]
        da = _mm_nt_relu(f"mlp2_bwd_{l}", dxb, w_2, l, sv["a"])
        g_2 = _mm_tn(f"mlp2_wgrad_{l}", sv["a"], dxb, 512, 1024, relu2=True)
        g_1 = _mm_tn(f"mlp1_wgrad_{l}", sv["h2"], da, 1024, 512)
        dep = on_grads(l, (2, 3), (g_1, g_2))
        dx1, dx1b, dg2 = _mm_nt_normbwd(f"mlp1_bwd_{l}", da, w_1, l, sv["x1"], small["norm2_g"][l:l + 1], dx, dep)
        do, dot, dmix, dl = _proj_out_bwd(f"proj_out_bwd_{l}", dx1b, w_out, l, sv["mix"])
        g_out = _mm_tn(f"proj_out_wgrad_{l}", sv["mix"], dx1b, 512, 1024)
        dcp, dw0, dw1, dw2, dps, dpw = _convpool_bwd(f"convpool_bwd_{l}", sv["p"], dmix, sv["cw"], sv["pwbd"], sv["ps"])
        dq, dkp, dvp, db = _attn_bwd(f"attn_bwd_{l}", sv["q"], sv["qt"], sv["kp"], sv["kt"], sv["vp"], bias_all, l,
                                     do, dot, sv["lse"], _rowsum_layout(dl, x.shape[0] // UNIT))
        dp, dqg, dkg = _qkv_bwd(f"qkv_bwd_{l}", sv["p"], dq, dkp, dvp, dcp, sv["qg"], sv["kg"])
        g_in = _mm_tn(f"proj_in_wgrad_{l}", sv["h"], dp, 1024, 1280)
        dep = on_grads(l, (0, 1), (g_in, g_out))
        dx, dxb, dg1 = _mm_nt_normbwd(f"proj_in_bwd_{l}", dp, w_in, l, sv["xin"], small["norm1_g"][l:l + 1], dx1, dep)
        for k, val in dict(dg1=dg1, dqg=dqg, dkg=dkg, db=db, dw0=dw0, dw1=dw1, dw2=dw2, dpw=dpw, dps=dps, dg2=dg2).items():
            raw[k][l] = val
    cat = {k: jnp.concatenate(v, axis=0) for k, v in raw.items() if k not in ("db", "dpw")}
    drb = _bias_reduce("bias_reduce", _bias_unlayout(jnp.stack(raw["db"])))
    dpw = jnp.stack(raw["dpw"])
    gsmall = {
        "norm1_g": cat["dg1"], "q_norm_g": cat["dqg"][:, :HD], "k_norm_g": cat["dkg"][:, :HD],
        "rel_bias": drb[:, :, :257],
        "conv_w": jnp.stack([cat["dw0"], cat["dw1"], cat["dw2"]], axis=1),
        "pool_w": jnp.stack([dpw[:, g * 64:(g + 1) * 64, g * 64:(g + 1) * 64] for g in range(4)], axis=1),
        "pool_scale": cat["dps"], "norm2_g": cat["dg2"],
    }
    return loss, dx, gsmall


SMALL = ("norm1_g", "q_norm_g", "k_norm_g", "rel_bias", "conv_w", "pool_w", "pool_scale", "norm2_g")
LARGE = ("w_in", "w_out", "w_mlp1", "w_mlp2")


def kernel(x, norm1_g, w_in, q_norm_g, k_norm_g, rel_bias, conv_w, pool_w, pool_scale, w_out, norm2_g, w_mlp1, w_mlp2, loss_target, m_norm1_g, m_w_in, m_q_norm_g, m_k_norm_g, m_rel_bias, m_conv_w, m_pool_w, m_pool_scale, m_w_out, m_norm2_g, m_w_mlp1, m_w_mlp2, v_norm1_g, v_w_in, v_q_norm_g, v_k_norm_g, v_rel_bias, v_conv_w, v_pool_w, v_pool_scale, v_w_out, v_norm2_g, v_w_mlp1, v_w_mlp2):
    w = dict(norm1_g=norm1_g, w_in=w_in, q_norm_g=q_norm_g, k_norm_g=k_norm_g, rel_bias=rel_bias, conv_w=conv_w,
             pool_w=pool_w, pool_scale=pool_scale, w_out=w_out, norm2_g=norm2_g, w_mlp1=w_mlp1, w_mlp2=w_mlp2)
    m = dict(norm1_g=m_norm1_g, w_in=m_w_in, q_norm_g=m_q_norm_g, k_norm_g=m_k_norm_g, rel_bias=m_rel_bias,
             conv_w=m_conv_w, pool_w=m_pool_w, pool_scale=m_pool_scale, w_out=m_w_out, norm2_g=m_norm2_g,
             w_mlp1=m_w_mlp1, w_mlp2=m_w_mlp2)
    v = dict(norm1_g=v_norm1_g, w_in=v_w_in, q_norm_g=v_q_norm_g, k_norm_g=v_k_norm_g, rel_bias=v_rel_bias,
             conv_w=v_conv_w, pool_w=v_pool_w, pool_scale=v_pool_scale, w_out=v_w_out, norm2_g=v_norm2_g,
             w_mlp1=v_w_mlp1, w_mlp2=v_w_mlp2)
    ax, ay, ac = lax.axis_index("x"), lax.axis_index("y"), lax.axis_index("c")
    b1 = jnp.reshape(2 * ax + ay, (1,)).astype(jnp.int32)

    cw_rows = _all_gather8("gather_conv_w", jnp.pad(conv_w.reshape(DEPTH * 3, 64), ((0, 4), (0, 64))), b1)
    cw_chips = [cw_rows[(4 * cx + 2 * cy) * 16:(4 * cx + 2 * cy) * 16 + 12, :64] for cx in range(2) for cy in range(2)]
    small = {n: w[n] for n in SMALL}
    small["conv_w"] = jnp.concatenate(cw_chips, axis=1).reshape(DEPTH, 3, CW)

    (w_in_full,), in_sems, in_token = _gather_start(
        "gather_start_in", (0,), (0,), [_cast_into_full("cast_w_in", 0, w["w_in"], b1, cw_rows)])
    others, first_sems, first_token = _gather_start(
        "gather_start_first", (0,), (1, 2, 3),
        [_cast_into_full(f"cast_{LARGE[t]}", t, w[LARGE[t]], b1, in_token) for t in (1, 2, 3)])
    held = [[w_in_full] + list(others)]
    sems = {(0, 0): in_sems[0], (0, 1): first_sems[0]}

    def layer_weights(l, ts, after):
        if l > 0:
            ts = (0, 1, 2, 3) if ts == (0,) else ()
        if ts:
            tag = f"{l}_{ts[0]}"
            first_in = l == 0 and ts == (0,)
            after = first_token if first_in else after
            arrived = _gather_wait(f"gather_wait_{tag}", l, ts, held[0], sems[l, ts[0] if l == 0 else 0], after)
            if first_in:
                arrived, rest_sems, _ = _gather_start("gather_start_rest", tuple(range(1, DEPTH)), (0, 1, 2, 3),
                                                      arrived)
                sems.update({(k, 0): v for k, v in rest_sems.items()})
            held[0] = _pass_on(f"pass_on_{tag}", l, ts, arrived)
        return held[0]

    flights = {}

    def await_flight(l, ts, afters):
        g, landing, sm, _ = flights[l, ts]
        flights[l, ts] = _reduce_wait(f"reduce_wait_{l}_{ts[0]}", ts, g, landing, sm, afters)

    def on_grads(l, ts, grads):
        if ts == (0, 1) and l + 1 < DEPTH:
            await_flight(l + 1, (2, 3), [grads[0]])
            await_flight(l + 1, (0, 1), [grads[0]])
        flights[l, ts] = _reduce_start(f"reduce_start_{l}_{ts[0]}", ts, grads)
        return flights[l, ts][3]

    loss_part, grad_x, gsmall = _local_step(x[0], loss_target[0], layer_weights, on_grads, small)
    loss = lax.psum(loss_part[0, 0], ("x", "y", "c"))
    order = [n for n in SMALL]
    packed = _pack([gsmall[n] for n in order])

    out = {n: [lax.empty(w[n].shape, F32) for _ in range(4)] for n in LARGE}
    for l in reversed(range(DEPTH)):
        if l == 0:
            afters = [grad_x, packed] + [out[n][0] for n in LARGE]
            await_flight(0, (2, 3), afters)
            await_flight(0, (0, 1), afters)
        sums = [None] * 4
        for ts in ((0, 1), (2, 3)):
            g, landing = flights[l, ts]
            for i, t in enumerate(ts):
                sums[t] = _add4(f"add4_{LARGE[t]}_{l}", t, g[i], landing[i], b1)
        theirs = _swap_sib(f"swap_sib_{l}", sums)
        for t, n in enumerate(LARGE):
            out[n] = _adamw_pair(f"adamw_{n}_{l}", l, sums[t], theirs[t], w[n], m[n], v[n], out[n])

    rows = packed.shape[0]
    summed = _sum8("sum_small", _all_gather8("gather_small", packed, out[LARGE[0]][0]).reshape(8, rows, 128))
    gfull = dict(zip(order, _unpack(summed, [gsmall[n].shape for n in order])))
    gfull["conv_w"] = lax.dynamic_slice_in_dim(gfull["conv_w"], (2 * ax + ay) * 64, 64, axis=2)
    res = _adamw("adamw_small", _pack([gfull[n] for n in order]), _pack([w[n] for n in order]),
                 _pack([m[n] for n in order]), _pack([v[n] for n in order]))
    for n, parts in zip(order, zip(*[_unpack(r, [w[k].shape for k in order]) for r in res])):
        out[n] = list(parts)

    names = ("norm1_g", "w_in", "q_norm_g", "k_norm_g", "rel_bias", "conv_w", "pool_w", "pool_scale", "w_out",
             "norm2_g", "w_mlp1", "w_mlp2")
    flat = [loss, grad_x[None]]
    for i in range(4):
        flat += [out[n][i] for n in names]
    return tuple(flat)
```

```python
import functools

import jax
import jax.numpy as jnp
from jax import lax
from jax.experimental import pallas as pl
from jax.experimental.pallas import tpu as pltpu

F32 = jnp.float32
BF16 = jnp.bfloat16

D = 1024
DEPTH = 4
CH = 64
NPREV = 8
KB = (NPREV + 1) * CH
PADR = NPREV * CH
HD = 64
AW = 512
CW = 256
PWD = 256
DIN = 3 * AW + 3 * CW + PWD
DFF = 4 * D
NIDX = 384
EPS = 1e-6
NEG_INF = -1e30

ADAM_LR = 0.001
ADAM_B1 = 0.9
ADAM_B2 = 0.999
ADAM_EPS = 1e-08
ADAM_WD = 0.01
ADAM_STEP = 10

VMEM_LIMIT = 52 * 1024 * 1024
MM_ROWS = 512


def _mm_rows(k, n):
    return 2 * MM_ROWS if k + n <= 2048 else MM_ROWS


MESH = pl.DeviceIdType.MESH
ANY = pl.BlockSpec(memory_space=pl.ANY)


def _cp(*sem):
    return pltpu.CompilerParams(dimension_semantics=sem, vmem_limit_bytes=VMEM_LIMIT)


def _inv_rms(x):
    return lax.rsqrt(jnp.mean(x * x, axis=-1, keepdims=True) + EPS)


def _head_mean_matrix():
    r = lax.broadcasted_iota(jnp.int32, (AW, AW), 0) // HD
    c = lax.broadcasted_iota(jnp.int32, (AW, AW), 1) // HD
    return jnp.where(r == c, 1.0 / HD, 0.0).astype(BF16)


def _two_pass_dot(x, m):
    hi = x.astype(BF16)
    lo = (x - hi.astype(F32)).astype(BF16)
    return (jnp.dot(hi, m, preferred_element_type=F32)
            + jnp.dot(lo, m, preferred_element_type=F32))


def _head_mean(x, hm):
    return _two_pass_dot(x, hm)


def _rmsnorm(name, x, g):
    s = x.shape[0]
    tm = 512

    def body(x_ref, g_ref, h_ref):
        xv = x_ref[...]
        h_ref[...] = (xv * _inv_rms(xv) * g_ref[...]).astype(BF16)

    return pl.pallas_call(
        body, name=name, grid=(s // tm,),
        in_specs=[pl.BlockSpec((tm, D), lambda i: (i, 0)), pl.BlockSpec((1, D), lambda i: (0, 0))],
        out_specs=pl.BlockSpec((tm, D), lambda i: (i, 0)),
        out_shape=jax.ShapeDtypeStruct((s, D), BF16),
        compiler_params=_cp("parallel"),
    )(x, g)


def _relu2(a):
    r = jnp.maximum(a, jnp.zeros_like(a))
    return r * r


def _mm_nn(name, a, w, l, out_dtype):
    s, k = a.shape
    n = w.shape[2]
    tm = _mm_rows(k, n)

    def body(a_ref, w_ref, o_ref):
        o_ref[...] = jnp.dot(a_ref[...], w_ref[...], preferred_element_type=F32).astype(o_ref.dtype)

    return pl.pallas_call(
        body, name=name, grid=(s // tm,),
        in_specs=[pl.BlockSpec((tm, k), lambda i: (i, 0)),
                  pl.BlockSpec((None, k, n), lambda i: (l, 0, 0))],
        out_specs=pl.BlockSpec((tm, n), lambda i: (i, 0)),
        out_shape=jax.ShapeDtypeStruct((s, n), out_dtype),
        compiler_params=_cp("parallel"),
    )(a, w)


def _mm_res_norm(name, a, w, l, res, g):
    s, k = a.shape
    tm = _mm_rows(k, D)

    def body(a_ref, w_ref, r_ref, g_ref, x_ref, h_ref):
        acc = r_ref[...] + jnp.dot(a_ref[...], w_ref[...], preferred_element_type=F32)
        x_ref[...] = acc
        h_ref[...] = (acc * _inv_rms(acc) * g_ref[...]).astype(BF16)

    return pl.pallas_call(
        body, name=name, grid=(s // tm,),
        in_specs=[pl.BlockSpec((tm, k), lambda i: (i, 0)),
                  pl.BlockSpec((None, k, D), lambda i: (l, 0, 0)),
                  pl.BlockSpec((tm, D), lambda i: (i, 0)),
                  pl.BlockSpec((1, D), lambda i: (0, 0))],
        out_specs=[pl.BlockSpec((tm, D), lambda i: (i, 0))] * 2,
        out_shape=[jax.ShapeDtypeStruct((s, D), F32), jax.ShapeDtypeStruct((s, D), BF16)],
        compiler_params=_cp("parallel"),
    )(a, w, res, g)


def _mlp_fwd(name, h2, w1, w2, l, res, g):
    s = h2.shape[0]
    tm = 256

    def body(h_ref, w1_ref, w2_ref, r_ref, g_ref, a_ref, x_ref, hn_ref):
        a = jnp.dot(h_ref[...], w1_ref[...], preferred_element_type=F32).astype(BF16)
        a_ref[...] = a
        acc = r_ref[...] + jnp.dot(_relu2(a), w2_ref[...], preferred_element_type=F32)
        x_ref[...] = acc
        hn_ref[...] = (acc * _inv_rms(acc) * g_ref[...]).astype(BF16)

    once = pl.Buffered(1)
    rows = pl.BlockSpec((tm, D), lambda i: (i, 0))
    return pl.pallas_call(
        body, name=name, grid=(s // tm,),
        in_specs=[rows,
                  pl.BlockSpec((None, D, DFF), lambda i: (l, 0, 0), pipeline_mode=once),
                  pl.BlockSpec((None, DFF, D), lambda i: (l, 0, 0), pipeline_mode=once),
                  rows, pl.BlockSpec((1, D), lambda i: (0, 0))],
        out_specs=[pl.BlockSpec((tm, DFF), lambda i: (i, 0)), rows, rows],
        out_shape=[jax.ShapeDtypeStruct((s, DFF), BF16), jax.ShapeDtypeStruct((s, D), F32),
                   jax.ShapeDtypeStruct((s, D), BF16)],
        compiler_params=_cp("parallel"),
    )(h2, w1, w2, res, g)


def _qkv(name, p, qg, kg):
    s = p.shape[0]
    tm = PADR
    nb = s // tm

    def body(pq_ref, pk_ref, pv_ref, qg_ref, kg_ref, q_ref, qt_ref, k_ref, kt_ref, v_ref, vt_ref):
        t = pl.program_id(0)
        hm = _head_mean_matrix()

        def nrm(x, g):
            return x * lax.rsqrt(_head_mean(x * x, hm) + EPS) * g

        first = t == 0
        qq = nrm(pq_ref[...], qg_ref[...]) * 0.125
        kk = jnp.where(first, 0.0, nrm(pk_ref[...], kg_ref[...]))
        vv = jnp.where(first, 0.0, pv_ref[...])
        q_ref[...] = qq.astype(BF16)
        qt_ref[...] = qq.T.astype(BF16)
        k_ref[...] = kk.astype(BF16)
        kt_ref[...] = kk.T.astype(BF16)
        v_ref[...] = vv.astype(BF16)
        vt_ref[...] = vv.T.astype(BF16)

    def src(col):
        return pl.BlockSpec((tm, AW), lambda t: (jnp.maximum(t - 1, 0), col))

    gspec = pl.BlockSpec((1, AW), lambda t: (0, 0))
    rows = pl.BlockSpec((tm, AW), lambda t: (t, 0))
    cols = pl.BlockSpec((AW, tm), lambda t: (0, t))
    return pl.pallas_call(
        body, name=name, grid=(nb + 1,),
        in_specs=[src(0), src(1), src(2), gspec, gspec],
        out_specs=[pl.BlockSpec((tm, AW), lambda t: (jnp.maximum(t - 1, 0), 0)),
                   pl.BlockSpec((AW, tm), lambda t: (0, jnp.maximum(t - 1, 0))),
                   rows, cols, rows, cols],
        out_shape=[jax.ShapeDtypeStruct((s, AW), BF16), jax.ShapeDtypeStruct((AW, s), BF16),
                   jax.ShapeDtypeStruct((s + PADR, AW), BF16), jax.ShapeDtypeStruct((AW, s + PADR), BF16),
                   jax.ShapeDtypeStruct((s + PADR, AW), BF16), jax.ShapeDtypeStruct((AW, s + PADR), BF16)],
        compiler_params=_cp("arbitrary"),
    )(p, p, p, qg, kg)


NBAND = KB // CH
HIGHEST = lax.Precision.HIGHEST
NT_DIMS = (((1,), (1,)), ((), ()))


def _onehot_table(a):
    m = lax.broadcasted_iota(jnp.int32, (128, NIDX), 0)
    idx = lax.broadcasted_iota(jnp.int32, (128, NIDX), 1)
    rel = jnp.clip(KB - 1 - (CH * a + m), -128, 128) + 128
    return jnp.where(rel == idx, 1.0, 0.0).astype(F32)


def _onehot_diagonal():
    r = lax.broadcasted_iota(jnp.int32, (CH * CH, 128), 0)
    m = lax.broadcasted_iota(jnp.int32, (CH * CH, 128), 1)
    return jnp.where((r % CH) - (r // CH) + (CH - 1) == m, 1.0, 0.0).astype(F32)


def _bias_expand(name, rb):
    def body(rb_ref, o_ref):
        along = [lax.dot_general(rb_ref[...], _onehot_table(a), NT_DIMS, preferred_element_type=F32,
                                 precision=HIGHEST) for a in range(NBAND)]
        o_ref[...] = lax.dot_general(jnp.concatenate(along, axis=0), _onehot_diagonal(), NT_DIMS,
                                     preferred_element_type=F32, precision=HIGHEST)

    return pl.pallas_call(
        body, name=name, grid=(DEPTH,),
        in_specs=[pl.BlockSpec((None, 8, NIDX), lambda l: (l, 0, 0))],
        out_specs=pl.BlockSpec((None, NBAND * 8, CH * CH), lambda l: (l, 0, 0)),
        out_shape=jax.ShapeDtypeStruct((DEPTH, NBAND * 8, CH * CH), F32),
        compiler_params=_cp("parallel"),
    )(rb)


def _bias_reduce(name, db):
    def body(db_ref, o_ref):
        along = jnp.dot(db_ref[...], _onehot_diagonal(), preferred_element_type=F32, precision=HIGHEST)
        acc = jnp.zeros((8, NIDX), F32)
        for a in range(NBAND):
            acc = acc + jnp.dot(along[8 * a:8 * a + 8, :], _onehot_table(a), preferred_element_type=F32,
                                precision=HIGHEST)
        o_ref[...] = acc

    return pl.pallas_call(
        body, name=name, grid=(DEPTH,),
        in_specs=[pl.BlockSpec((None, NBAND * 8, CH * CH), lambda l: (l, 0, 0))],
        out_specs=pl.BlockSpec((None, 8, NIDX), lambda l: (l, 0, 0)),
        out_shape=jax.ShapeDtypeStruct((DEPTH, 8, NIDX), F32),
        compiler_params=_cp("parallel"),
    )(db)


def _bias_layout(flat):
    b = flat.reshape(DEPTH, NBAND, 8, CH, CH).transpose(0, 2, 1, 4, 3).reshape(DEPTH, 4, 2, KB, CH)
    pair = b.transpose(0, 1, 3, 2, 4).reshape(DEPTH, 4, KB, 128)
    first = jnp.pad(pair, ((0, 0), (0, 0), (0, CH), (0, 0)), constant_values=NEG_INF)
    second = jnp.pad(pair, ((0, 0), (0, 0), (CH, 0), (0, 0)), constant_values=NEG_INF)
    return jnp.concatenate([first, second], axis=3)


def _bias_unlayout(dbt):
    b = dbt.reshape(DEPTH, 4, NBAND, CH, 2, CH)
    return b.transpose(0, 2, 1, 4, 5, 3).reshape(DEPTH, NBAND * 8, CH * CH)


UNIT = 2 * CH
BAND2 = KB + CH


def _pair_weights(xt):
    x = xt.astype(F32)
    row = lax.broadcasted_iota(jnp.int32, (128, UNIT), 0)
    low = lax.broadcasted_iota(jnp.int32, (128, UNIT), 1) < HD
    swapped = pltpu.roll(x, HD, 1)
    same = (row < HD) == low
    first = jnp.where(same, jnp.where(low, x, swapped), 0.0)
    second = jnp.where(same, jnp.where(low, swapped, x), 0.0)
    return jnp.concatenate([first, second], axis=1).astype(BF16)


def _pair_rows(x):
    low = lax.broadcasted_iota(jnp.int32, (CH, 128), 1) < HD
    zero = jnp.zeros((CH, 128), x.dtype)
    parts = []
    for c in range(2):
        xc = x[c * CH:(c + 1) * CH, :]
        parts += [jnp.where(low, xc, zero), jnp.where(low, zero, xc)]
    return jnp.concatenate(parts, axis=0)


def _unpair(raw):
    b0, b1 = raw[:, 0:128], raw[:, 128:256]
    row = lax.broadcasted_iota(jnp.int32, (128, 128), 0)
    low = lax.broadcasted_iota(jnp.int32, (128, 128), 1) < HD
    top = jnp.where(low, b0, pltpu.roll(b1, HD, 1))
    bottom = jnp.where(low, pltpu.roll(b0, HD, 1), b1)
    return jnp.where(row < HD, top, bottom).T


def _scores_t(kb, qw, bias2, row0, padded):
    s = jnp.dot(kb, qw, preferred_element_type=F32) + bias2
    if padded:
        s = jnp.where(row0 + lax.broadcasted_iota(jnp.int32, (BAND2, 256), 0) >= PADR, s, NEG_INF)
    return s


def _unit_loops(s, unit):
    lax.fori_loop(0, PADR // UNIT, lambda u, c: unit(u, True, c), 0, unroll=2)
    lax.fori_loop(PADR // UNIT, s // UNIT, lambda u, c: unit(u, False, c), 0, unroll=7)


def _attn_fwd(name, kp, qt, vt, bias2, l):
    s = qt.shape[1]
    nu = s // UNIT

    def body(k_ref, qt_ref, vt_ref, b_ref, o_ref, lse_ref):
        def unit(u, padded, carry):
            r0 = pl.multiple_of(u * UNIT, UNIT)
            sc = _scores_t(k_ref[pl.ds(r0, BAND2), :], _pair_weights(qt_ref[:, pl.ds(r0, UNIT)]), b_ref[...],
                           r0, padded)
            top = jnp.max(sc, axis=0, keepdims=True)
            e = jnp.exp(sc - top)
            total = jnp.sum(e, axis=0, keepdims=True)
            raw = jnp.dot(vt_ref[:, pl.ds(r0, BAND2)], e.astype(BF16), preferred_element_type=F32) * (1.0 / total)
            o_ref[pl.ds(r0, UNIT), :] = _unpair(raw).astype(BF16)
            lse_ref[u] = jnp.broadcast_to(top + jnp.log(total), (8, 256))
            return carry

        _unit_loops(s, unit)

    return pl.pallas_call(
        body, name=name, grid=(AW // 128,),
        in_specs=[pl.BlockSpec((s + PADR, 128), lambda h: (0, h)),
                  pl.BlockSpec((128, s), lambda h: (h, 0)),
                  pl.BlockSpec((128, s + PADR), lambda h: (h, 0)),
                  pl.BlockSpec((None, None, BAND2, 256), lambda h: (l, h, 0, 0))],
        out_specs=[pl.BlockSpec((s, 128), lambda h: (0, h)),
                   pl.BlockSpec((None, nu, 8, 256), lambda h: (h, 0, 0, 0))],
        out_shape=[jax.ShapeDtypeStruct((s, AW), BF16), jax.ShapeDtypeStruct((4, nu, 8, 256), F32)],
        compiler_params=_cp("parallel"),
    )(kp, qt, vt, bias2)


def _attn_bwd(name, q, qt, kp, kt, vp, bias2, l, do, dot, lse, dl, db_all):
    s = q.shape[0]
    nu = s // UNIT

    def body(q_ref, qt_ref, k_ref, kt_ref, v_ref, b_ref, do_ref, dot_ref, lse_ref, dl_ref, dbin_ref,
             dq_ref, dk_ref, dv_ref, db_ref):
        del dbin_ref
        dk_ref[...] = jnp.zeros_like(dk_ref)
        dv_ref[...] = jnp.zeros_like(dv_ref)
        db_ref[...] = jnp.zeros_like(db_ref)

        def unit(u, padded, carry):
            r0 = pl.multiple_of(u * UNIT, UNIT)
            rows, band = pl.ds(r0, UNIT), pl.ds(r0, BAND2)
            sc = _scores_t(k_ref[band, :], _pair_weights(qt_ref[:, rows]), b_ref[...], r0, padded)
            pt = jnp.exp(sc - lse_ref[u][0:1, :])
            dpt = jnp.dot(v_ref[band, :], _pair_weights(dot_ref[:, rows]), preferred_element_type=F32)
            ds = pt * (dpt - dl_ref[u][0:1, :])
            db_ref[...] += ds[0:KB, 0:128] + ds[CH:BAND2, 128:256]
            dsb = ds.astype(BF16)
            dq_ref[rows, :] = _unpair(jnp.dot(kt_ref[:, band], dsb, preferred_element_type=F32))
            dk_ref[band, :] += jnp.dot(dsb, _pair_rows(q_ref[rows, :]), preferred_element_type=F32)
            dv_ref[band, :] += jnp.dot(pt.astype(BF16), _pair_rows(do_ref[rows, :]), preferred_element_type=F32)
            return carry

        _unit_loops(s, unit)

    row_q = pl.BlockSpec((s, 128), lambda h: (0, h))
    col_q = pl.BlockSpec((128, s), lambda h: (h, 0))
    row_k = pl.BlockSpec((s + PADR, 128), lambda h: (0, h))
    col_k = pl.BlockSpec((128, s + PADR), lambda h: (h, 0))
    stat = pl.BlockSpec((None, nu, 8, 256), lambda h: (h, 0, 0, 0))
    return pl.pallas_call(
        body, name=name, grid=(AW // 128,),
        in_specs=[row_q, col_q, row_k, col_k, row_k,
                  pl.BlockSpec((None, None, BAND2, 256), lambda h: (l, h, 0, 0)), row_q, col_q, stat, stat, ANY],
        out_specs=[row_q, row_k, row_k, pl.BlockSpec((None, None, KB, 128), lambda h: (l, h, 0, 0))],
        out_shape=[jax.ShapeDtypeStruct((s, AW), F32),
                   jax.ShapeDtypeStruct((s + PADR, AW), F32),
                   jax.ShapeDtypeStruct((s + PADR, AW), F32),
                   jax.ShapeDtypeStruct((DEPTH, 4, KB, 128), F32)],
        input_output_aliases={10: 3},
        compiler_params=_cp("parallel"),
    )(q, qt, kp, kt, vp, bias2, do, dot, lse, dl, db_all)


def _rowsum_layout(dl, nu):
    d = dl[:, :8].reshape(nu, 2, CH, 4, 2)
    d = d.transpose(3, 0, 1, 4, 2).reshape(4, nu, 1, 256)
    return jnp.broadcast_to(d, (4, nu, 8, 256))


def _rows_before(cur, prev, k):
    row = lax.broadcasted_iota(jnp.int32, cur.shape, 0)
    return jnp.where(row >= k, pltpu.roll(cur, k, 0), pltpu.roll(prev, k, 0))


def _rows_after(cur, nxt, k):
    n = cur.shape[0]
    row = lax.broadcasted_iota(jnp.int32, cur.shape, 0)
    return jnp.where(row < n - k, pltpu.roll(cur, n - k, 0), pltpu.roll(nxt, n - k, 0))


def _pool_window_lanes():
    lg = lax.broadcasted_iota(jnp.int32, (1, PWD), 1) // 64
    return lg, jnp.where(lg == 0, 2.0, jnp.where(lg == 1, 4.0, jnp.where(lg == 2, 8.0, 16.0))).astype(F32)


def _pool_mean_minus_token(u, up, row0):
    lg, wv = _pool_window_lanes()
    sums = []
    c, p = u, up
    for k in (1, 2, 4, 8):
        c2 = c + _rows_before(c, p, k)
        p = p + pltpu.roll(p, k, 0)
        c = c2
        sums.append(c)
    win = jnp.where(lg == 0, sums[0], jnp.where(lg == 1, sums[1], jnp.where(lg == 2, sums[2], sums[3])))
    pos1 = (row0 + lax.broadcasted_iota(jnp.int32, u.shape, 0) + 1).astype(F32)
    cnt = jnp.minimum(pos1, wv)
    return win / cnt - u, cnt


def _conv_taps(z, zp, w0, w1, w2):
    z1 = _rows_before(z, zp, 1)
    z2 = _rows_before(z, zp, 2)
    return (w0 * z2 + w1 * z1) + w2 * z, z1, z2


CP_TM = 512


def _convpool_fwd(name, p, o, cw, pwbd, ps):
    s = p.shape[0]
    tm = CP_TM
    nb = s // tm

    def body(gb_ref, gc_ref, hin_ref, u_ref, gcp_ref, hinp_ref, up_ref, o_ref, cw_ref, pw_ref, ps_ref, mix_ref):
        i = pl.program_id(0)
        has_prev = i > 0
        z = gc_ref[...] * hin_ref[...]
        zp = jnp.where(has_prev, gcp_ref[...] * hinp_ref[...], 0.0)
        y3, _, _ = _conv_taps(z, zp, cw_ref[0:1, :], cw_ref[1:2, :], cw_ref[2:3, :])
        m, _ = _pool_mean_minus_token(u_ref[...], jnp.where(has_prev, up_ref[...], 0.0), i * tm)
        yp = jnp.dot(m.astype(BF16), pw_ref[...].astype(BF16), preferred_element_type=F32) * ps_ref[...]
        mix_ref[:, 0:AW] = o_ref[...]
        mix_ref[:, AW:AW + CW] = (gb_ref[...] * y3).astype(BF16)
        mix_ref[:, AW + CW:D] = yp.astype(BF16)

    def cur(col):
        return pl.BlockSpec((tm, CW), lambda i: (i, col))

    def prev(col):
        return pl.BlockSpec((tm, CW), lambda i: (jnp.maximum(i - 1, 0), col))

    def whole(a):
        return pl.BlockSpec(a.shape, lambda i: (0,) * a.ndim)

    return pl.pallas_call(
        body, name=name, grid=(nb,),
        in_specs=[cur(6), cur(7), cur(8), cur(9), prev(7), prev(8), prev(9),
                  pl.BlockSpec((tm, AW), lambda i: (i, 0)), whole(cw), whole(pwbd), whole(ps)],
        out_specs=pl.BlockSpec((tm, D), lambda i: (i, 0)),
        out_shape=jax.ShapeDtypeStruct((s, D), BF16),
        compiler_params=_cp("parallel"),
    )(p, p, p, p, p, p, p, o, cw, pwbd, ps)


def _convpool_bwd(name, p, dmix, cw, pwbd, ps):
    s = p.shape[0]
    tm = CP_TM
    nb = s // tm

    def body(gb_ref, gc_ref, hin_ref, u_ref, gcp_ref, hinp_ref, up_ref, gbn_ref, dyc_ref, dyp_ref, dycn_ref, dypn_ref,
             cw_ref, pw_ref, ps_ref, dcp_ref, dw0_ref, dw1_ref, dw2_ref, dps_ref, dpw_ref):
        i = pl.program_id(0)
        has_prev = i > 0
        has_next = i < nb - 1
        w0, w1, w2 = cw_ref[0:1, :], cw_ref[1:2, :], cw_ref[2:3, :]
        gb, gc, hin = gb_ref[...], gc_ref[...], hin_ref[...]
        dyc = dyc_ref[...]
        z = gc * hin
        zp = jnp.where(has_prev, gcp_ref[...] * hinp_ref[...], 0.0)
        y3, z1, z2 = _conv_taps(z, zp, w0, w1, w2)
        dy3 = dyc * gb
        dy3n = jnp.where(has_next, dycn_ref[...] * gbn_ref[...], 0.0)
        dz = w2 * dy3 + w1 * _rows_after(dy3, dy3n, 1) + w0 * _rows_after(dy3, dy3n, 2)
        pw = pw_ref[...].astype(BF16)
        psv = ps_ref[...]
        m, cnt = _pool_mean_minus_token(u_ref[...], jnp.where(has_prev, up_ref[...], 0.0), i * tm)
        mb = m.astype(BF16)
        dyp = dyp_ref[...]
        dmp = (dyp * psv).astype(BF16)
        dmpn = jnp.where(has_next, dypn_ref[...] * psv, 0.0).astype(BF16)
        nt = (((1,), (1,)), ((), ()))
        dm = lax.dot_general(dmp, pw, nt, preferred_element_type=F32)
        dmn = lax.dot_general(dmpn, pw, nt, preferred_element_type=F32)
        lg, wv = _pool_window_lanes()
        cc, cn = dm / cnt, dmn / wv
        sums = []
        for k in (1, 2, 4, 8):
            c2 = cc + _rows_after(cc, cn, k)
            cn = cn + pltpu.roll(cn, tm - k, 0)
            cc = c2
            sums.append(cc)
        du = jnp.where(lg == 0, sums[0], jnp.where(lg == 1, sums[1], jnp.where(lg == 2, sums[2], sums[3]))) - dm
        dcp_ref[:, 0:CW] = (dyc * y3).astype(BF16)
        dcp_ref[:, CW:2 * CW] = (dz * hin).astype(BF16)
        dcp_ref[:, 2 * CW:3 * CW] = (dz * gc).astype(BF16)
        dcp_ref[:, 3 * CW:4 * CW] = du.astype(BF16)
        parts = (jnp.sum(dy3 * z2, axis=0, keepdims=True),
                 jnp.sum(dy3 * z1, axis=0, keepdims=True),
                 jnp.sum(dy3 * z, axis=0, keepdims=True),
                 jnp.sum(dyp * jnp.dot(mb, pw, preferred_element_type=F32), axis=0, keepdims=True),
                 lax.dot_general(mb, dmp, (((0,), (0,)), ((), ())), preferred_element_type=F32))
        accs = (dw0_ref, dw1_ref, dw2_ref, dps_ref, dpw_ref)

        @pl.when(i == 0)
        def _():
            for a, v in zip(accs, parts):
                a[...] = v

        @pl.when(i > 0)
        def _():
            for a, v in zip(accs, parts):
                a[...] += v

    def cur(col):
        return pl.BlockSpec((tm, CW), lambda i: (i, col))

    def prev(col):
        return pl.BlockSpec((tm, CW), lambda i: (jnp.maximum(i - 1, 0), col))

    def nxt(col):
        return pl.BlockSpec((tm, CW), lambda i: (jnp.minimum(i + 1, nb - 1), col))

    def whole(shape):
        return pl.BlockSpec(shape, lambda i: (0,) * len(shape))

    row = jax.ShapeDtypeStruct((1, CW), F32)
    return pl.pallas_call(
        body, name=name, grid=(nb,),
        in_specs=[cur(6), cur(7), cur(8), cur(9), prev(7), prev(8), prev(9), nxt(6),
                  cur(0), cur(1), nxt(0), nxt(1), whole(cw.shape), whole(pwbd.shape), whole(ps.shape)],
        out_specs=[pl.BlockSpec((tm, D), lambda i: (i, 0)), whole((1, CW)), whole((1, CW)), whole((1, CW)),
                   whole((1, PWD)), whole((PWD, PWD))],
        out_shape=[jax.ShapeDtypeStruct((s, D), BF16), row, row, row, row,
                   jax.ShapeDtypeStruct((PWD, PWD), F32)],
        compiler_params=_cp("arbitrary"),
    )(p, p, p, p, p, p, p, p, dmix, dmix, dmix, dmix, cw, pwbd, ps)


def _qkv_bwd(name, p, dq, dkp, dvp, dcp, qg, kg):
    s = p.shape[0]
    tm = 512
    off = PADR // tm

    def body(pq_ref, pk_ref, dq_ref, dk_ref, dv_ref, dcp_ref, qg_ref, kg_ref, dp_ref, dqg_ref, dkg_ref):
        i = pl.program_id(0)
        hm = _head_mean_matrix()

        def nrm_bwd(x, g, dy):
            r = lax.rsqrt(_head_mean(x * x, hm) + EPS)
            xn = x * r
            dxn = dy * g
            dx = r * (dxn - xn * _head_mean(dxn * xn, hm))
            dg = jnp.sum(dy * xn, axis=0, keepdims=True)
            dg = (dg[:, 0:128] + dg[:, 128:256]) + (dg[:, 256:384] + dg[:, 384:512])
            return dx, dg + pltpu.roll(dg, HD, 1)

        dxq, dgq = nrm_bwd(pq_ref[...], qg_ref[...], dq_ref[...] * 0.125)
        dxk, dgk = nrm_bwd(pk_ref[...], kg_ref[...], dk_ref[...])
        dp_ref[:, 0:AW] = dxq.astype(BF16)
        dp_ref[:, AW:2 * AW] = dxk.astype(BF16)
        dp_ref[:, 2 * AW:3 * AW] = dv_ref[...].astype(BF16)
        dp_ref[:, 3 * AW:DIN] = dcp_ref[...]

        @pl.when(i == 0)
        def _():
            dqg_ref[...] = dgq
            dkg_ref[...] = dgk

        @pl.when(i > 0)
        def _():
            dqg_ref[...] += dgq
            dkg_ref[...] += dgk

    gspec = pl.BlockSpec((1, AW), lambda i: (0, 0))
    gout = pl.BlockSpec((1, 128), lambda i: (0, 0))
    return pl.pallas_call(
        body, name=name, grid=(s // tm,),
        in_specs=[pl.BlockSpec((tm, AW), lambda i: (i, 0)), pl.BlockSpec((tm, AW), lambda i: (i, 1)),
                  pl.BlockSpec((tm, AW), lambda i: (i, 0)),
                  pl.BlockSpec((tm, AW), lambda i: (i + off, 0)),
                  pl.BlockSpec((tm, AW), lambda i: (i + off, 0)),
                  pl.BlockSpec((tm, D), lambda i: (i, 0)), gspec, gspec],
        out_specs=[pl.BlockSpec((tm, DIN), lambda i: (i, 0)), gout, gout],
        out_shape=[jax.ShapeDtypeStruct((s, DIN), BF16), jax.ShapeDtypeStruct((1, 128), F32),
                   jax.ShapeDtypeStruct((1, 128), F32)],
        compiler_params=_cp("arbitrary"),
    )(p, p, dq, dkp, dvp, dcp, qg, kg)


def _loss_grad(name, y, t):
    s = y.shape[0]
    tm = 512

    def body(y_ref, t_ref, dy_ref, dyb_ref, l_ref):
        i = pl.program_id(0)
        e = y_ref[...] - t_ref[...]
        dy = e * (1.0 / D)
        dy_ref[...] = dy
        dyb_ref[...] = dy.astype(BF16)
        part = 0.5 * jnp.sum(jnp.mean(e * e, axis=-1, keepdims=True), axis=0, keepdims=True)

        @pl.when(i == 0)
        def _():
            l_ref[...] = part

        @pl.when(i > 0)
        def _():
            l_ref[...] += part

    blk = pl.BlockSpec((tm, D), lambda i: (i, 0))
    return pl.pallas_call(
        body, name=name, grid=(s // tm,),
        in_specs=[blk, blk],
        out_specs=[blk, blk, pl.BlockSpec((1, 1), lambda i: (0, 0))],
        out_shape=[jax.ShapeDtypeStruct((s, D), F32), jax.ShapeDtypeStruct((s, D), BF16),
                   jax.ShapeDtypeStruct((1, 1), F32)],
        compiler_params=_cp("arbitrary"),
    )(y, t)


def _mm_nt_relu(name, dxb, w, l, a):
    s = dxb.shape[0]
    tm = MM_ROWS

    def body(d_ref, w_ref, a_ref, o_ref):
        df = lax.dot_general(d_ref[...], w_ref[...], NT_DIMS, preferred_element_type=F32)
        o_ref[...] = (df * (2.0 * jnp.maximum(a_ref[...].astype(F32), 0.0))).astype(BF16)

    return pl.pallas_call(
        body, name=name, grid=(s // tm,),
        in_specs=[pl.BlockSpec((tm, D), lambda i: (i, 0)),
                  pl.BlockSpec((None, DFF, D), lambda i: (l, 0, 0)),
                  pl.BlockSpec((tm, DFF), lambda i: (i, 0))],
        out_specs=pl.BlockSpec((tm, DFF), lambda i: (i, 0)),
        out_shape=jax.ShapeDtypeStruct((s, DFF), BF16),
        compiler_params=_cp("parallel"),
    )(dxb, w, a)


def _proj_out_bwd(name, dxb, w, l, mix):
    s = dxb.shape[0]
    tm = _mm_rows(D, D)

    def body(d_ref, w_ref, o_ref, do_ref, dot_ref, dcp_ref, dl_ref):
        d = d_ref[...]
        wa, wc = w_ref[0:AW, :], w_ref[AW:D, :]
        do = lax.dot_general(d, wa, NT_DIMS, preferred_element_type=F32)
        do_ref[...] = do.astype(BF16)
        dot_ref[...] = lax.dot_general(wa, d, NT_DIMS, preferred_element_type=F32).astype(BF16)
        dcp_ref[...] = lax.dot_general(d, wc, NT_DIMS, preferred_element_type=F32)
        head = lax.broadcasted_iota(jnp.int32, (AW, 128), 0) // HD
        pick = jnp.where(head == lax.broadcasted_iota(jnp.int32, (AW, 128), 1), 1.0, 0.0).astype(BF16)
        dl_ref[...] = _two_pass_dot(do * o_ref[...].astype(F32), pick)

    return pl.pallas_call(
        body, name=name, grid=(s // tm,),
        in_specs=[pl.BlockSpec((tm, D), lambda i: (i, 0)),
                  pl.BlockSpec((None, D, D), lambda i: (l, 0, 0)),
                  pl.BlockSpec((tm, AW), lambda i: (i, 0))],
        out_specs=[pl.BlockSpec((tm, AW), lambda i: (i, 0)), pl.BlockSpec((AW, tm), lambda i: (0, i)),
                   pl.BlockSpec((tm, D - AW), lambda i: (i, 0)), pl.BlockSpec((tm, 128), lambda i: (i, 0))],
        out_shape=[jax.ShapeDtypeStruct((s, AW), BF16), jax.ShapeDtypeStruct((AW, s), BF16),
                   jax.ShapeDtypeStruct((s, D - AW), F32), jax.ShapeDtypeStruct((s, 128), F32)],
        compiler_params=_cp("parallel"),
    )(dxb, w, mix)


def _mm_nt_normbwd(name, gy, w, l, x, g, dres, dep):
    s, k = gy.shape
    tm = MM_ROWS

    def body(gy_ref, w_ref, x_ref, g_ref, dr_ref, dep_ref, dx_ref, dxb_ref, dg_ref):
        del dep_ref
        i = pl.program_id(0)
        dh = lax.dot_general(gy_ref[...], w_ref[...], NT_DIMS, preferred_element_type=F32)
        xv = x_ref[...]
        r = _inv_rms(xv)
        xn = xv * r
        dxn = dh * g_ref[...]
        dx = r * (dxn - xn * jnp.mean(dxn * xn, axis=-1, keepdims=True)) + dr_ref[...]
        dx_ref[...] = dx
        dxb_ref[...] = dx.astype(BF16)
        part = jnp.sum(dh * xn, axis=0, keepdims=True)

        @pl.when(i == 0)
        def _():
            dg_ref[...] = part

        @pl.when(i > 0)
        def _():
            dg_ref[...] += part

    blk = pl.BlockSpec((tm, D), lambda i: (i, 0))
    vec = pl.BlockSpec((1, D), lambda i: (0, 0))
    return pl.pallas_call(
        body, name=name, grid=(s // tm,),
        in_specs=[pl.BlockSpec((tm, k), lambda i: (i, 0)),
                  pl.BlockSpec((None, D, k), lambda i: (l, 0, 0)), blk, vec, blk, ANY],
        out_specs=[blk, blk, vec],
        out_shape=[jax.ShapeDtypeStruct((s, D), F32), jax.ShapeDtypeStruct((s, D), BF16),
                   jax.ShapeDtypeStruct((1, D), F32)],
        compiler_params=_cp("arbitrary"),
    )(gy, w, x, g, dres, dep)


def _mm_tn(name, a, b, tma, tnb, relu2=False):
    s, m = a.shape
    n = b.shape[1]

    def body(a_ref, b_ref, o_ref):
        av = _relu2(a_ref[...]) if relu2 else a_ref[...]
        o_ref[...] = lax.dot_general(av, b_ref[...], (((0,), (0,)), ((), ())),
                                     preferred_element_type=F32).astype(BF16)

    return pl.pallas_call(
        body, name=name, grid=(m // tma, n // tnb),
        in_specs=[pl.BlockSpec((s, tma), lambda i, j: (0, i), pipeline_mode=pl.Buffered(1) if m == tma else None),
                  pl.BlockSpec((s, tnb), lambda i, j: (0, j))],
        out_specs=pl.BlockSpec((tma, tnb), lambda i, j: (i, j)),
        out_shape=jax.ShapeDtypeStruct((m, n), BF16),
        compiler_params=_cp("parallel", "parallel"),
    )(a, b)


def _adamw_math(gv, wv, mv, vv):
    mn = ADAM_B1 * mv + (1.0 - ADAM_B1) * gv
    vn = ADAM_B2 * vv + (1.0 - ADAM_B2) * jnp.square(gv)
    m_hat = mn / (1.0 - ADAM_B1 ** ADAM_STEP)
    v_hat = vn / (1.0 - ADAM_B2 ** ADAM_STEP)
    return gv, -ADAM_LR * (m_hat / (jnp.sqrt(v_hat) + ADAM_EPS) + ADAM_WD * wv), mn, vn


def _adamw(name, g, w, m, v):
    r, c = g.shape
    tm = 256 if r % 256 == 0 else r

    def body(g_ref, w_ref, m_ref, v_ref, go_ref, d_ref, mo_ref, vo_ref):
        go_ref[...], d_ref[...], mo_ref[...], vo_ref[...] = _adamw_math(g_ref[...], w_ref[...], m_ref[...], v_ref[...])

    blk = pl.BlockSpec((tm, c), lambda i: (i, 0))
    return pl.pallas_call(
        body, name=name, grid=(r // tm,),
        in_specs=[blk] * 4, out_specs=[blk] * 4,
        out_shape=[jax.ShapeDtypeStruct((r, c), F32)] * 4,
        compiler_params=_cp("parallel"),
    )(g, w, m, v)


def _place():
    x, y, c = lax.axis_index("x"), lax.axis_index("y"), lax.axis_index("c")
    chips = [(1 - x, y), (x, 1 - y), (1 - x, 1 - y)]
    return x, y, c, chips


BLOCK_AXIS = (2, 1, 2, 1)
LARGE_DIMS = ((D, DIN), (D, D), (D, DFF), (DFF, D))


def _full_shape(t, layers, dtype):
    r, c = LARGE_DIMS[t]
    return jax.ShapeDtypeStruct((layers, r, c), dtype)


def _cast_into_full(name, t, shard, b1, dep):
    _, r, c = shard.shape
    tm = min(512, r)
    if BLOCK_AXIS[t] == 1:
        out_spec = pl.BlockSpec((None, tm, c), lambda l, i, br: (l, br[0] * (r // tm) + i, 0))
    else:
        out_spec = pl.BlockSpec((None, tm, c), lambda l, i, br: (l, i, br[0]))

    def body(b_ref, x_ref, dep_ref, o_ref):
        del b_ref, dep_ref
        o_ref[...] = x_ref[...].astype(BF16)

    return pl.pallas_call(
        body, name=name,
        grid_spec=pltpu.PrefetchScalarGridSpec(
            num_scalar_prefetch=1, grid=(DEPTH, r // tm),
            in_specs=[pl.BlockSpec((None, tm, c), lambda l, i, br: (l, i, 0)), ANY],
            out_specs=out_spec),
        out_shape=_full_shape(t, DEPTH, BF16),
        compiler_params=_cp("parallel", "parallel"),
    )(b1, shard, dep)


HBM = pl.BlockSpec(memory_space=pltpu.HBM)
SEM = pl.BlockSpec(memory_space=pltpu.SEMAPHORE)
DATAFLOW = pltpu.SideEffectType.DATAFLOW_SIDE_EFFECTING


def _half(ref, l, t, b, c):
    r, cols = LARGE_DIMS[t]
    if BLOCK_AXIS[t] == 1:
        n = r // 8
        return ref.at[l, pl.ds(pl.multiple_of(b * (2 * n) + c * n, 16), n), :]
    n, w = r // 2, cols // 4
    return ref.at[l, pl.ds(pl.multiple_of(c * n, 16), n), pl.ds(pl.multiple_of(b * w, 128), w)]


def _gather_start(name, layers, ts, fulls):
    n = len(ts)

    def body(*refs):
        f_refs, sems = refs[n:2 * n], refs[2 * n:2 * n + 2 * len(layers)]
        x, y, c, chips = _place()
        for i, l in enumerate(layers):
            for k, t in enumerate(ts):
                own = _half(f_refs[k], l, t, 2 * x + y, c)
                for j, (cx, cy) in enumerate(chips):
                    pltpu.make_async_remote_copy(src_ref=own, dst_ref=own, send_sem=sems[2 * i].at[3 * t + j],
                                                 recv_sem=sems[2 * i + 1].at[3 * t + j], device_id=(cx, cy, c),
                                                 device_id_type=MESH).start()
        refs[-1][...] = jnp.zeros((8, 128), F32)

    outs = pl.pallas_call(
        body, name=name,
        in_specs=[HBM] * n,
        out_specs=[HBM] * n + [SEM] * (2 * len(layers)) + [pl.BlockSpec(memory_space=pltpu.VMEM)],
        out_shape=[pltpu.HBM(f.shape, f.dtype) for f in fulls]
        + [pltpu.SemaphoreType.DMA((12,))] * (2 * len(layers)) + [jax.ShapeDtypeStruct((8, 128), F32)],
        input_output_aliases={k: k for k in range(n)},
        compiler_params=pltpu.CompilerParams(has_side_effects=DATAFLOW),
    )(*[pltpu.with_memory_space_constraint(f, pltpu.HBM) for f in fulls])
    return outs[0:n], {l: (outs[n + 2 * i], outs[n + 1 + 2 * i]) for i, l in enumerate(layers)}, outs[-1]


def _gather_wait(name, l, ts, fulls, sems, after):
    def body(*refs):
        send_sems, recv_sems, f_refs = refs[4], refs[5], refs[7:11]
        x, y, c, chips = _place()
        for t in ts:
            own = _half(f_refs[t], l, t, 2 * x + y, c)
            for j, (cx, cy) in enumerate(chips):
                landed = _half(f_refs[t], l, t, 2 * cx + cy, c)
                pltpu.make_async_remote_copy(src_ref=own, dst_ref=landed, send_sem=send_sems.at[3 * t + j],
                                             recv_sem=recv_sems.at[3 * t + j], device_id=(cx, cy, c),
                                             device_id_type=MESH).wait()

    return pl.pallas_call(
        body, name=name,
        in_specs=[HBM] * 4 + [SEM, SEM, ANY], out_specs=[HBM] * 4,
        out_shape=[pltpu.HBM(s.shape, s.dtype) for s in (_full_shape(t, DEPTH, BF16) for t in range(4))],
        input_output_aliases={t: t for t in range(4)},
        compiler_params=pltpu.CompilerParams(has_side_effects=DATAFLOW),
    )(*fulls, sems[0], sems[1], after)


def _pass_on(name, l, ts, fulls):
    def body(*refs):
        f_refs, send_sems, recv_sems = refs[4:8], refs[8], refs[9]
        x, y, c, chips = _place()

        def copy(t, j, half):
            cx, cy = chips[j]
            part = _half(f_refs[t], l, t, 2 * cx + cy, half)
            return pltpu.make_async_remote_copy(src_ref=part, dst_ref=part, send_sem=send_sems.at[3 * t + j],
                                                recv_sem=recv_sems.at[3 * t + j], device_id=(x, y, 1 - c),
                                                device_id_type=MESH)

        for t in ts:
            for j in range(3):
                copy(t, j, c).start()
        for t in ts:
            for j in range(3):
                copy(t, j, 1 - c).wait_recv()
                copy(t, j, c).wait_send()

    return pl.pallas_call(
        body, name=name,
        in_specs=[ANY] * 4, out_specs=[ANY] * 4,
        out_shape=[_full_shape(t, DEPTH, BF16) for t in range(4)],
        input_output_aliases={t: t for t in range(4)},
        scratch_shapes=[pltpu.SemaphoreType.DMA((12,)), pltpu.SemaphoreType.DMA((12,))],
    )(*fulls)


def _block2d(ref, t, b):
    r, cols = LARGE_DIMS[t]
    if BLOCK_AXIS[t] == 1:
        return ref.at[pl.ds(pl.multiple_of(b * (r // 4), 16), r // 4), :]
    return ref.at[:, pl.ds(pl.multiple_of(b * (cols // 4), 128), cols // 4)]


def _block_dims(t):
    r, cols = LARGE_DIMS[t]
    return (r // 4, cols) if BLOCK_AXIS[t] == 1 else (r, cols // 4)


def _reduce_copies(ts, g_refs, r_refs, send_sems, recv_sems):
    _, _, c, chips = _place()
    return [pltpu.make_async_remote_copy(src_ref=_block2d(g_refs[i], t, 2 * cx + cy), dst_ref=r_refs[i].at[j],
                                         send_sem=send_sems.at[3 * i + j], recv_sem=recv_sems.at[3 * i + j],
                                         device_id=(cx, cy, c), device_id_type=MESH)
            for i, t in enumerate(ts) for j, (cx, cy) in enumerate(chips)]


def _reduce_start(name, ts, grads):
    n = len(ts)

    def body(*refs):
        for cp in _reduce_copies(ts, refs[n:2 * n], refs[2 * n:3 * n], refs[3 * n], refs[3 * n + 1]):
            cp.start()
        refs[3 * n + 2][...] = jnp.zeros((8, 128), F32)

    outs = pl.pallas_call(
        body, name=name,
        in_specs=[HBM] * n,
        out_specs=[HBM] * (2 * n) + [SEM, SEM, pl.BlockSpec(memory_space=pltpu.VMEM)],
        out_shape=[pltpu.HBM(g.shape, BF16) for g in grads]
        + [pltpu.HBM((3,) + _block_dims(t), BF16) for t in ts]
        + [pltpu.SemaphoreType.DMA((3 * n,)), pltpu.SemaphoreType.DMA((3 * n,)), jax.ShapeDtypeStruct((8, 128), F32)],
        input_output_aliases={i: i for i in range(n)},
        compiler_params=pltpu.CompilerParams(has_side_effects=DATAFLOW),
    )(*[pltpu.with_memory_space_constraint(g, pltpu.HBM) for g in grads])
    return outs[0:n], outs[n:2 * n], (outs[2 * n], outs[2 * n + 1]), outs[2 * n + 2]


def _reduce_wait(name, ts, grads, landing, sems, afters):
    n = len(ts)
    first_out = 2 * n + 2 + len(afters)

    def body(*refs):
        for cp in _reduce_copies(ts, refs[first_out:first_out + n], refs[first_out + n:first_out + 2 * n],
                                 refs[2 * n], refs[2 * n + 1]):
            cp.wait()

    outs = pl.pallas_call(
        body, name=name,
        in_specs=[HBM] * (2 * n) + [SEM, SEM] + [ANY] * len(afters), out_specs=[HBM] * (2 * n),
        out_shape=[pltpu.HBM(g.shape, BF16) for g in grads] + [pltpu.HBM(r.shape, BF16) for r in landing],
        input_output_aliases={i: i for i in range(2 * n)},
        compiler_params=pltpu.CompilerParams(has_side_effects=DATAFLOW),
    )(*grads, *landing, sems[0], sems[1], *afters)
    return outs[0:n], outs[n:2 * n]


def _add4(name, t, own, landed, b1):
    rb, cb = _block_dims(t)
    tm = min(512, rb)
    if BLOCK_AXIS[t] == 1:
        own_spec = pl.BlockSpec((tm, cb), lambda i, br: (br[0] * (rb // tm) + i, 0))
    else:
        own_spec = pl.BlockSpec((tm, cb), lambda i, br: (i, br[0]))

    def body(b_ref, o_ref, r0_ref, r1_ref, r2_ref, s_ref):
        del b_ref
        s_ref[...] = ((o_ref[...].astype(F32) + r0_ref[...].astype(F32))
                      + (r1_ref[...].astype(F32) + r2_ref[...].astype(F32))).astype(BF16)

    def got(j):
        return pl.BlockSpec((None, tm, cb), lambda i, br: (j, i, 0))

    return pl.pallas_call(
        body, name=name,
        grid_spec=pltpu.PrefetchScalarGridSpec(
            num_scalar_prefetch=1, grid=(rb // tm,),
            in_specs=[own_spec, got(0), got(1), got(2)],
            out_specs=pl.BlockSpec((tm, cb), lambda i, br: (i, 0))),
        out_shape=jax.ShapeDtypeStruct((rb, cb), BF16),
        compiler_params=_cp("parallel"),
    )(b1, own, landed, landed, landed)


def _swap_sib(name, sums):
    def body(*refs):
        s_refs, t_refs, send_sems, recv_sems = refs[0:4], refs[4:8], refs[8], refs[9]
        x, y, c, _ = _place()
        cps = [pltpu.make_async_remote_copy(src_ref=s_refs[t], dst_ref=t_refs[t], send_sem=send_sems.at[t],
                                            recv_sem=recv_sems.at[t], device_id=(x, y, 1 - c), device_id_type=MESH)
               for t in range(4)]
        for cp in cps:
            cp.start()
        for cp in cps:
            cp.wait()

    return pl.pallas_call(
        body, name=name,
        in_specs=[ANY] * 4, out_specs=[ANY] * 4,
        out_shape=[jax.ShapeDtypeStruct(s.shape, BF16) for s in sums],
        scratch_shapes=[pltpu.SemaphoreType.DMA((4,)), pltpu.SemaphoreType.DMA((4,))],
    )(*sums)


def _adamw_pair(name, l, s_own, s_sib, w, m, v, outs):
    rb, cb = s_own.shape
    tm = min(512, rb)

    def body(a_ref, b_ref, w_ref, m_ref, v_ref, g0, d0, m0, v0, go_ref, d_ref, mo_ref, vo_ref):
        del g0, d0, m0, v0
        gv = a_ref[...].astype(F32) + b_ref[...].astype(F32)
        go_ref[...], d_ref[...], mo_ref[...], vo_ref[...] = _adamw_math(gv, w_ref[...], m_ref[...], v_ref[...])

    part = pl.BlockSpec((tm, cb), lambda i: (i, 0))
    layer = pl.BlockSpec((None, tm, cb), lambda i: (l, i, 0))
    return pl.pallas_call(
        body, name=name, grid=(rb // tm,),
        in_specs=[part, part, layer, layer, layer] + [ANY] * 4,
        out_specs=[layer] * 4,
        out_shape=[jax.ShapeDtypeStruct((DEPTH, rb, cb), F32)] * 4,
        input_output_aliases={5 + i: i for i in range(4)},
        compiler_params=_cp("parallel"),
    )(s_own, s_sib, w, m, v, *outs)


def _all_gather8(name, v, dep):
    m_per, n = v.shape

    def body(v_ref, dep_ref, out_ref, send_sems, recv_sems, local_sem):
        del dep_ref
        x, y, c, chips = _place()
        me, sib = (x, y, c), (x, y, 1 - c)

        def rows(px, py, pc):
            return out_ref.at[pl.ds((4 * px + 2 * py + pc) * m_per, m_per), :]

        def copy(k, block, to, src=None):
            return pltpu.make_async_remote_copy(
                src_ref=rows(*block) if src is None else src, dst_ref=rows(*block),
                send_sem=send_sems.at[k], recv_sem=recv_sems.at[k], device_id=to, device_id_type=MESH)

        mine = pltpu.make_async_copy(v_ref, rows(*me), local_sem)
        mine.start()
        first = [copy(0, me, sib, src=v_ref)]
        first += [copy(1 + j, me, (*chip, c), src=v_ref) for j, chip in enumerate(chips)]
        for cp in first:
            cp.start()
        passed = [copy(4 + j, (*chip, c), sib) for j, chip in enumerate(chips)]
        for j, chip in enumerate(chips):
            copy(1 + j, (*chip, c), me).wait_recv()
            passed[j].start()
        copy(0, sib, me).wait_recv()
        for j, chip in enumerate(chips):
            copy(4 + j, (*chip, 1 - c), me).wait_recv()
        for cp in first + passed:
            cp.wait_send()
        mine.wait()

    return pl.pallas_call(
        body, name=name,
        out_shape=jax.ShapeDtypeStruct((8 * m_per, n), v.dtype),
        in_specs=[pl.BlockSpec(memory_space=pltpu.VMEM), ANY],
        out_specs=pl.BlockSpec(memory_space=pltpu.VMEM),
        scratch_shapes=[pltpu.SemaphoreType.DMA((7,)), pltpu.SemaphoreType.DMA((7,)), pltpu.SemaphoreType.DMA],
    )(v, dep)


def _sum8(name, g):
    def body(g_ref, o_ref):
        acc = g_ref[0]
        for d in range(1, 8):
            acc = acc + g_ref[d]
        o_ref[...] = acc

    return pl.pallas_call(body, name=name, out_shape=jax.ShapeDtypeStruct(g.shape[1:], F32))(g)


def _pack(parts):
    flat = []
    for a in parts:
        a = a.reshape(-1)
        flat.append(jnp.pad(a, (0, (-a.shape[0]) % 128)))
    cat = jnp.concatenate(flat)
    cat = jnp.pad(cat, (0, (-cat.shape[0]) % 1024))
    return cat.reshape(-1, 128)


def _unpack(packed, shapes):
    flat = packed.reshape(-1)
    out, at = [], 0
    for shp in shapes:
        n = 1
        for d in shp:
            n *= d
        out.append(flat[at:at + n].reshape(shp))
        at += n + (-n) % 128
    return out


def _local_step(x, target, layer_weights, on_grads, small):
    qg_all = jnp.tile(small["q_norm_g"], (1, 8))
    kg_all = jnp.tile(small["k_norm_g"], (1, 8))
    bias_all = _bias_layout(_bias_expand("bias_expand", jnp.pad(small["rel_bias"], ((0, 0), (0, 0), (0, NIDX - 257)))))
    same_group = jnp.eye(4, dtype=F32)[None, :, None, :, None]
    pwbd_all = (small["pool_w"][:, :, :, None, :] * same_group).reshape(DEPTH, PWD, PWD)
    saved = []
    xin = x
    h = _rmsnorm("norm_first", x, small["norm1_g"][0:1])
    for l in range(DEPTH):
        w_in = layer_weights(l, (0,), xin)[0]
        qg, kg = qg_all[l:l + 1], kg_all[l:l + 1]
        cw, pwbd, ps = small["conv_w"][l], pwbd_all[l], small["pool_scale"][l:l + 1]
        p = _mm_nn(f"proj_in_{l}", h, w_in, l, F32)
        q, qt, kp, kt, vp, vt = _qkv(f"qkv_{l}", p, qg, kg)
        o, lse = _attn_fwd(f"attn_fwd_{l}", kp, qt, vt, bias_all, l)
        w_in, w_out, w_1, w_2 = layer_weights(l, (1, 2, 3), o)
        mix = _convpool_fwd(f"convpool_fwd_{l}", p, o, cw, pwbd, ps)
        x1, h2 = _mm_res_norm(f"proj_out_{l}", mix, w_out, l, xin, small["norm2_g"][l:l + 1])
        gnext = small["norm1_g"][(l + 1) % DEPTH][None]
        a, x2, hnext = _mlp_fwd(f"mlp_{l}", h2, w_1, w_2, l, x1, gnext)
        saved.append(dict(xin=xin, h=h, p=p, q=q, qt=qt, kp=kp, kt=kt, vp=vp, mix=mix, x1=x1, h2=h2, a=a, lse=lse,
                          qg=qg, kg=kg, cw=cw, pwbd=pwbd, ps=ps))
        xin, h = x2, hnext

    dx, dxb, loss = _loss_grad("loss_grad", xin, target)
    raw = {k: [None] * DEPTH for k in ("dg1", "dqg", "dkg", "dw0", "dw1", "dw2", "dpw", "dps", "dg2")}
    db_all = lax.empty((DEPTH, 4, KB, 128), F32)
    for l in reversed(range(DEPTH)):
        sv = saved[l]
        da = _mm_nt_relu(f"mlp2_bwd_{l}", dxb, w_2, l, sv["a"])
        g_2 = _mm_tn(f"mlp2_wgrad_{l}", sv["a"], dxb, 512, 1024, relu2=True)
        g_1 = _mm_tn(f"mlp1_wgrad_{l}", sv["h2"], da, 1024, 512)
        dep = on_grads(l, (2, 3), (g_1, g_2))
        dx1, dx1b, dg2 = _mm_nt_normbwd(f"mlp1_bwd_{l}", da, w_1, l, sv["x1"], small["norm2_g"][l:l + 1], dx, dep)
        do, dot, dmix, dl = _proj_out_bwd(f"proj_out_bwd_{l}", dx1b, w_out, l, sv["mix"])
        g_out = _mm_tn(f"proj_out_wgrad_{l}", sv["mix"], dx1b, 512, 1024)
        dcp, dw0, dw1, dw2, dps, dpw = _convpool_bwd(f"convpool_bwd_{l}", sv["p"], dmix, sv["cw"], sv["pwbd"], sv["ps"])
        dq, dkp, dvp, db_all = _attn_bwd(f"attn_bwd_{l}", sv["q"], sv["qt"], sv["kp"], sv["kt"], sv["vp"], bias_all, l,
                                     do, dot, sv["lse"], _rowsum_layout(dl, x.shape[0] // UNIT), db_all)
        dp, dqg, dkg = _qkv_bwd(f"qkv_bwd_{l}", sv["p"], dq, dkp, dvp, dcp, sv["qg"], sv["kg"])
        g_in = _mm_tn(f"proj_in_wgrad_{l}", sv["h"], dp, 1024, 1280)
        dep = on_grads(l, (0, 1), (g_in, g_out))
        dx, dxb, dg1 = _mm_nt_normbwd(f"proj_in_bwd_{l}", dp, w_in, l, sv["xin"], small["norm1_g"][l:l + 1], dx1, dep)
        for k, val in dict(dg1=dg1, dqg=dqg, dkg=dkg, dw0=dw0, dw1=dw1, dw2=dw2, dpw=dpw, dps=dps, dg2=dg2).items():
            raw[k][l] = val
    cat = {k: jnp.concatenate(v, axis=0) for k, v in raw.items() if k != "dpw"}
    drb = _bias_reduce("bias_reduce", _bias_unlayout(db_all))
    dpw = jnp.stack(raw["dpw"])
    gsmall = {
        "norm1_g": cat["dg1"], "q_norm_g": cat["dqg"][:, :HD], "k_norm_g": cat["dkg"][:, :HD],
        "rel_bias": drb[:, :, :257],
        "conv_w": jnp.stack([cat["dw0"], cat["dw1"], cat["dw2"]], axis=1),
        "pool_w": jnp.stack([dpw[:, g * 64:(g + 1) * 64, g * 64:(g + 1) * 64] for g in range(4)], axis=1),
        "pool_scale": cat["dps"], "norm2_g": cat["dg2"],
    }
    return loss, dx, gsmall


SMALL = ("norm1_g", "q_norm_g", "k_norm_g", "rel_bias", "conv_w", "pool_w", "pool_scale", "norm2_g")
LARGE = ("w_in", "w_out", "w_mlp1", "w_mlp2")


def kernel(x, norm1_g, w_in, q_norm_g, k_norm_g, rel_bias, conv_w, pool_w, pool_scale, w_out, norm2_g, w_mlp1, w_mlp2, loss_target, m_norm1_g, m_w_in, m_q_norm_g, m_k_norm_g, m_rel_bias, m_conv_w, m_pool_w, m_pool_scale, m_w_out, m_norm2_g, m_w_mlp1, m_w_mlp2, v_norm1_g, v_w_in, v_q_norm_g, v_k_norm_g, v_rel_bias, v_conv_w, v_pool_w, v_pool_scale, v_w_out, v_norm2_g, v_w_mlp1, v_w_mlp2):
    w = dict(norm1_g=norm1_g, w_in=w_in, q_norm_g=q_norm_g, k_norm_g=k_norm_g, rel_bias=rel_bias, conv_w=conv_w,
             pool_w=pool_w, pool_scale=pool_scale, w_out=w_out, norm2_g=norm2_g, w_mlp1=w_mlp1, w_mlp2=w_mlp2)
    m = dict(norm1_g=m_norm1_g, w_in=m_w_in, q_norm_g=m_q_norm_g, k_norm_g=m_k_norm_g, rel_bias=m_rel_bias,
             conv_w=m_conv_w, pool_w=m_pool_w, pool_scale=m_pool_scale, w_out=m_w_out, norm2_g=m_norm2_g,
             w_mlp1=m_w_mlp1, w_mlp2=m_w_mlp2)
    v = dict(norm1_g=v_norm1_g, w_in=v_w_in, q_norm_g=v_q_norm_g, k_norm_g=v_k_norm_g, rel_bias=v_rel_bias,
             conv_w=v_conv_w, pool_w=v_pool_w, pool_scale=v_pool_scale, w_out=v_w_out, norm2_g=v_norm2_g,
             w_mlp1=v_w_mlp1, w_mlp2=v_w_mlp2)
    ax, ay, ac = lax.axis_index("x"), lax.axis_index("y"), lax.axis_index("c")
    b1 = jnp.reshape(2 * ax + ay, (1,)).astype(jnp.int32)

    cw_rows = _all_gather8("gather_conv_w", jnp.pad(conv_w.reshape(DEPTH * 3, 64), ((0, 4), (0, 64))), b1)
    cw_chips = [cw_rows[(4 * cx + 2 * cy) * 16:(4 * cx + 2 * cy) * 16 + 12, :64] for cx in range(2) for cy in range(2)]
    small = {n: w[n] for n in SMALL}
    small["conv_w"] = jnp.concatenate(cw_chips, axis=1).reshape(DEPTH, 3, CW)

    (w_in_full,), in_sems, in_token = _gather_start(
        "gather_start_in", (0,), (0,), [_cast_into_full("cast_w_in", 0, w["w_in"], b1, cw_rows)])
    others, first_sems, first_token = _gather_start(
        "gather_start_first", (0,), (1, 2, 3),
        [_cast_into_full(f"cast_{LARGE[t]}", t, w[LARGE[t]], b1, in_token) for t in (1, 2, 3)])
    held = [[w_in_full] + list(others)]
    sems = {(0, 0): in_sems[0], (0, 1): first_sems[0]}

    def layer_weights(l, ts, after):
        if l > 0:
            ts = (0, 1, 2, 3) if ts == (0,) else ()
        if ts:
            tag = f"{l}_{ts[0]}"
            first_in = l == 0 and ts == (0,)
            after = first_token if first_in else after
            arrived = _gather_wait(f"gather_wait_{tag}", l, ts, held[0], sems[l, ts[0] if l == 0 else 0], after)
            if first_in:
                arrived, rest_sems, _ = _gather_start("gather_start_rest", tuple(range(1, DEPTH)), (0, 1, 2, 3),
                                                      arrived)
                sems.update({(k, 0): v for k, v in rest_sems.items()})
            held[0] = _pass_on(f"pass_on_{tag}", l, ts, arrived)
        return held[0]

    flights = {}

    def await_flight(l, ts, afters):
        g, landing, sm, _ = flights[l, ts]
        flights[l, ts] = _reduce_wait(f"reduce_wait_{l}_{ts[0]}", ts, g, landing, sm, afters)

    def on_grads(l, ts, grads):
        if ts == (0, 1) and l + 1 < DEPTH:
            await_flight(l + 1, (2, 3), [grads[0]])
            await_flight(l + 1, (0, 1), [grads[0]])
        flights[l, ts] = _reduce_start(f"reduce_start_{l}_{ts[0]}", ts, grads)
        return flights[l, ts][3]

    loss_part, grad_x, gsmall = _local_step(x[0], loss_target[0], layer_weights, on_grads, small)
    loss = lax.psum(loss_part[0, 0], ("x", "y", "c"))
    order = [n for n in SMALL]
    packed = _pack([gsmall[n] for n in order])

    out = {n: [lax.empty(w[n].shape, F32) for _ in range(4)] for n in LARGE}
    for l in reversed(range(DEPTH)):
        if l == 0:
            afters = [grad_x, packed] + [out[n][0] for n in LARGE]
            await_flight(0, (2, 3), afters)
            await_flight(0, (0, 1), afters)
        sums = [None] * 4
        for ts in ((0, 1), (2, 3)):
            g, landing = flights[l, ts]
            for i, t in enumerate(ts):
                sums[t] = _add4(f"add4_{LARGE[t]}_{l}", t, g[i], landing[i], b1)
        theirs = _swap_sib(f"swap_sib_{l}", sums)
        for t, n in enumerate(LARGE):
            out[n] = _adamw_pair(f"adamw_{n}_{l}", l, sums[t], theirs[t], w[n], m[n], v[n], out[n])

    rows = packed.shape[0]
    summed = _sum8("sum_small", _all_gather8("gather_small", packed, out[LARGE[0]][0]).reshape(8, rows, 128))
    gfull = dict(zip(order, _unpack(summed, [gsmall[n].shape for n in order])))
    gfull["conv_w"] = lax.dynamic_slice_in_dim(gfull["conv_w"], (2 * ax + ay) * 64, 64, axis=2)
    res = _adamw("adamw_small", _pack([gfull[n] for n in order]), _pack([w[n] for n in order]),
                 _pack([m[n] for n in order]), _pack([v[n] for n in order]))
    for n, parts in zip(order, zip(*[_unpack(r, [w[k].shape for k in order]) for r in res])):
        out[n] = list(parts)

    names = ("norm1_g", "w_in", "q_norm_g", "k_norm_g", "rel_bias", "conv_w", "pool_w", "pool_scale", "w_out",
             "norm2_g", "w_mlp1", "w_mlp2")
    flat = [loss, grad_x[None]]
    for i in range(4):
        flat += [out[n][i] for n in names]
    return tuple(flat)
```

```python
import functools

import jax
import jax.numpy as jnp
from jax import lax
from jax.experimental import pallas as pl
from jax.experimental.pallas import tpu as pltpu

F32 = jnp.float32
BF16 = jnp.bfloat16

D = 1024
DEPTH = 4
CH = 64
NPREV = 8
KB = (NPREV + 1) * CH
PADR = NPREV * CH
HD = 64
AW = 512
CW = 256
PWD = 256
DIN = 3 * AW + 3 * CW + PWD
DFF = 4 * D
NIDX = 384
EPS = 1e-6
NEG_INF = -1e30

ADAM_LR = 0.001
ADAM_B1 = 0.9
ADAM_B2 = 0.999
ADAM_EPS = 1e-08
ADAM_WD = 0.01
ADAM_STEP = 10

VMEM_LIMIT = 52 * 1024 * 1024
MM_ROWS = 512


def _mm_rows(k, n):
    return 2 * MM_ROWS if k + n <= 2048 else MM_ROWS


MESH = pl.DeviceIdType.MESH
ANY = pl.BlockSpec(memory_space=pl.ANY)


def _cp(*sem):
    return pltpu.CompilerParams(dimension_semantics=sem, vmem_limit_bytes=VMEM_LIMIT)


def _inv_rms(x):
    return lax.rsqrt(jnp.mean(x * x, axis=-1, keepdims=True) + EPS)


def _head_mean_matrix():
    r = lax.broadcasted_iota(jnp.int32, (AW, AW), 0) // HD
    c = lax.broadcasted_iota(jnp.int32, (AW, AW), 1) // HD
    return jnp.where(r == c, 1.0 / HD, 0.0).astype(BF16)


def _two_pass_dot(x, m):
    hi = x.astype(BF16)
    lo = (x - hi.astype(F32)).astype(BF16)
    return (jnp.dot(hi, m, preferred_element_type=F32)
            + jnp.dot(lo, m, preferred_element_type=F32))


def _head_mean(x, hm):
    return _two_pass_dot(x, hm)


def _rmsnorm(name, x, g):
    s = x.shape[0]
    tm = 512

    def body(x_ref, g_ref, h_ref):
        xv = x_ref[...]
        h_ref[...] = (xv * _inv_rms(xv) * g_ref[...]).astype(BF16)

    return pl.pallas_call(
        body, name=name, grid=(s // tm,),
        in_specs=[pl.BlockSpec((tm, D), lambda i: (i, 0)), pl.BlockSpec((1, D), lambda i: (0, 0))],
        out_specs=pl.BlockSpec((tm, D), lambda i: (i, 0)),
        out_shape=jax.ShapeDtypeStruct((s, D), BF16),
        compiler_params=_cp("parallel"),
    )(x, g)


def _relu2(a):
    r = jnp.maximum(a, jnp.zeros_like(a))
    return r * r


def _mm_nn(name, a, w, l, out_dtype):
    s, k = a.shape
    n = w.shape[2]
    tm = _mm_rows(k, n)

    def body(a_ref, w_ref, o_ref):
        o_ref[...] = jnp.dot(a_ref[...], w_ref[...], preferred_element_type=F32).astype(o_ref.dtype)

    return pl.pallas_call(
        body, name=name, grid=(s // tm,),
        in_specs=[pl.BlockSpec((tm, k), lambda i: (i, 0)),
                  pl.BlockSpec((None, k, n), lambda i: (l, 0, 0))],
        out_specs=pl.BlockSpec((tm, n), lambda i: (i, 0)),
        out_shape=jax.ShapeDtypeStruct((s, n), out_dtype),
        compiler_params=_cp("parallel"),
    )(a, w)


def _mm_res_norm(name, a, w, l, res, g):
    s, k = a.shape
    tm = _mm_rows(k, D)

    def body(a_ref, w_ref, r_ref, g_ref, x_ref, h_ref):
        acc = r_ref[...] + jnp.dot(a_ref[...], w_ref[...], preferred_element_type=F32)
        x_ref[...] = acc
        h_ref[...] = (acc * _inv_rms(acc) * g_ref[...]).astype(BF16)

    return pl.pallas_call(
        body, name=name, grid=(s // tm,),
        in_specs=[pl.BlockSpec((tm, k), lambda i: (i, 0)),
                  pl.BlockSpec((None, k, D), lambda i: (l, 0, 0)),
                  pl.BlockSpec((tm, D), lambda i: (i, 0)),
                  pl.BlockSpec((1, D), lambda i: (0, 0))],
        out_specs=[pl.BlockSpec((tm, D), lambda i: (i, 0))] * 2,
        out_shape=[jax.ShapeDtypeStruct((s, D), F32), jax.ShapeDtypeStruct((s, D), BF16)],
        compiler_params=_cp("parallel"),
    )(a, w, res, g)


def _mlp_fwd(name, h2, w1, w2, l, res, g):
    s = h2.shape[0]
    tm = 256

    def body(h_ref, w1_ref, w2_ref, r_ref, g_ref, a_ref, x_ref, hn_ref):
        a = jnp.dot(h_ref[...], w1_ref[...], preferred_element_type=F32).astype(BF16)
        a_ref[...] = a
        acc = r_ref[...] + jnp.dot(_relu2(a), w2_ref[...], preferred_element_type=F32)
        x_ref[...] = acc
        hn_ref[...] = (acc * _inv_rms(acc) * g_ref[...]).astype(BF16)

    once = pl.Buffered(1)
    rows = pl.BlockSpec((tm, D), lambda i: (i, 0))
    return pl.pallas_call(
        body, name=name, grid=(s // tm,),
        in_specs=[rows,
                  pl.BlockSpec((None, D, DFF), lambda i: (l, 0, 0), pipeline_mode=once),
                  pl.BlockSpec((None, DFF, D), lambda i: (l, 0, 0), pipeline_mode=once),
                  rows, pl.BlockSpec((1, D), lambda i: (0, 0))],
        out_specs=[pl.BlockSpec((tm, DFF), lambda i: (i, 0)), rows, rows],
        out_shape=[jax.ShapeDtypeStruct((s, DFF), BF16), jax.ShapeDtypeStruct((s, D), F32),
                   jax.ShapeDtypeStruct((s, D), BF16)],
        compiler_params=_cp("parallel"),
    )(h2, w1, w2, res, g)


def _qkv(name, p, qg, kg):
    s = p.shape[0]
    tm = PADR
    nb = s // tm

    def body(pq_ref, pk_ref, pv_ref, qg_ref, kg_ref, q_ref, qt_ref, k_ref, kt_ref, v_ref, vt_ref):
        t = pl.program_id(0)
        hm = _head_mean_matrix()

        def nrm(x, g):
            return x * lax.rsqrt(_head_mean(x * x, hm) + EPS) * g

        first = t == 0
        qq = nrm(pq_ref[...], qg_ref[...]) * 0.125
        kk = jnp.where(first, 0.0, nrm(pk_ref[...], kg_ref[...]))
        vv = jnp.where(first, 0.0, pv_ref[...])
        q_ref[...] = qq.astype(BF16)
        qt_ref[...] = qq.T.astype(BF16)
        k_ref[...] = kk.astype(BF16)
        kt_ref[...] = kk.T.astype(BF16)
        v_ref[...] = vv.astype(BF16)
        vt_ref[...] = vv.T.astype(BF16)

    def src(col):
        return pl.BlockSpec((tm, AW), lambda t: (jnp.maximum(t - 1, 0), col))

    gspec = pl.BlockSpec((1, AW), lambda t: (0, 0))
    rows = pl.BlockSpec((tm, AW), lambda t: (t, 0))
    cols = pl.BlockSpec((AW, tm), lambda t: (0, t))
    return pl.pallas_call(
        body, name=name, grid=(nb + 1,),
        in_specs=[src(0), src(1), src(2), gspec, gspec],
        out_specs=[pl.BlockSpec((tm, AW), lambda t: (jnp.maximum(t - 1, 0), 0)),
                   pl.BlockSpec((AW, tm), lambda t: (0, jnp.maximum(t - 1, 0))),
                   rows, cols, rows, cols],
        out_shape=[jax.ShapeDtypeStruct((s, AW), BF16), jax.ShapeDtypeStruct((AW, s), BF16),
                   jax.ShapeDtypeStruct((s + PADR, AW), BF16), jax.ShapeDtypeStruct((AW, s + PADR), BF16),
                   jax.ShapeDtypeStruct((s + PADR, AW), BF16), jax.ShapeDtypeStruct((AW, s + PADR), BF16)],
        compiler_params=_cp("arbitrary"),
    )(p, p, p, qg, kg)


NBAND = KB // CH
HIGHEST = lax.Precision.HIGHEST
NT_DIMS = (((1,), (1,)), ((), ()))


def _onehot_table(a):
    m = lax.broadcasted_iota(jnp.int32, (128, NIDX), 0)
    idx = lax.broadcasted_iota(jnp.int32, (128, NIDX), 1)
    rel = jnp.clip(KB - 1 - (CH * a + m), -128, 128) + 128
    return jnp.where(rel == idx, 1.0, 0.0).astype(F32)


def _onehot_diagonal():
    r = lax.broadcasted_iota(jnp.int32, (CH * CH, 128), 0)
    m = lax.broadcasted_iota(jnp.int32, (CH * CH, 128), 1)
    return jnp.where((r % CH) - (r // CH) + (CH - 1) == m, 1.0, 0.0).astype(F32)


def _bias_expand(name, rb):
    def body(rb_ref, o_ref):
        along = [lax.dot_general(rb_ref[...], _onehot_table(a), NT_DIMS, preferred_element_type=F32,
                                 precision=HIGHEST) for a in range(NBAND)]
        o_ref[...] = lax.dot_general(jnp.concatenate(along, axis=0), _onehot_diagonal(), NT_DIMS,
                                     preferred_element_type=F32, precision=HIGHEST)

    return pl.pallas_call(
        body, name=name, grid=(DEPTH,),
        in_specs=[pl.BlockSpec((None, 8, NIDX), lambda l: (l, 0, 0))],
        out_specs=pl.BlockSpec((None, NBAND * 8, CH * CH), lambda l: (l, 0, 0)),
        out_shape=jax.ShapeDtypeStruct((DEPTH, NBAND * 8, CH * CH), F32),
        compiler_params=_cp("parallel"),
    )(rb)


def _bias_reduce(name, db):
    def body(db_ref, o_ref):
        along = jnp.dot(db_ref[...], _onehot_diagonal(), preferred_element_type=F32, precision=HIGHEST)
        acc = jnp.zeros((8, NIDX), F32)
        for a in range(NBAND):
            acc = acc + jnp.dot(along[8 * a:8 * a + 8, :], _onehot_table(a), preferred_element_type=F32,
                                precision=HIGHEST)
        o_ref[...] = acc

    return pl.pallas_call(
        body, name=name, grid=(DEPTH,),
        in_specs=[pl.BlockSpec((None, NBAND * 8, CH * CH), lambda l: (l, 0, 0))],
        out_specs=pl.BlockSpec((None, 8, NIDX), lambda l: (l, 0, 0)),
        out_shape=jax.ShapeDtypeStruct((DEPTH, 8, NIDX), F32),
        compiler_params=_cp("parallel"),
    )(db)


def _bias_layout(flat):
    b = flat.reshape(DEPTH, NBAND, 8, CH, CH).transpose(0, 2, 1, 4, 3).reshape(DEPTH, 4, 2, KB, CH)
    pair = b.transpose(0, 1, 3, 2, 4).reshape(DEPTH, 4, KB, 128)
    first = jnp.pad(pair, ((0, 0), (0, 0), (0, CH), (0, 0)), constant_values=NEG_INF)
    second = jnp.pad(pair, ((0, 0), (0, 0), (CH, 0), (0, 0)), constant_values=NEG_INF)
    return jnp.concatenate([first, second], axis=3)


def _bias_unlayout(dbt):
    b = dbt.reshape(DEPTH, 4, NBAND, CH, 2, CH)
    return b.transpose(0, 2, 1, 4, 5, 3).reshape(DEPTH, NBAND * 8, CH * CH)


UNIT = 2 * CH
BAND2 = KB + CH


def _pair_weights(xt):
    x = xt.astype(F32)
    row = lax.broadcasted_iota(jnp.int32, (128, UNIT), 0)
    low = lax.broadcasted_iota(jnp.int32, (128, UNIT), 1) < HD
    swapped = pltpu.roll(x, HD, 1)
    same = (row < HD) == low
    first = jnp.where(same, jnp.where(low, x, swapped), 0.0)
    second = jnp.where(same, jnp.where(low, swapped, x), 0.0)
    return jnp.concatenate([first, second], axis=1).astype(BF16)


def _pair_rows(x):
    low = lax.broadcasted_iota(jnp.int32, (CH, 128), 1) < HD
    zero = jnp.zeros((CH, 128), x.dtype)
    parts = []
    for c in range(2):
        xc = x[c * CH:(c + 1) * CH, :]
        parts += [jnp.where(low, xc, zero), jnp.where(low, zero, xc)]
    return jnp.concatenate(parts, axis=0)


def _unpair(raw):
    b0, b1 = raw[:, 0:128], raw[:, 128:256]
    row = lax.broadcasted_iota(jnp.int32, (128, 128), 0)
    low = lax.broadcasted_iota(jnp.int32, (128, 128), 1) < HD
    top = jnp.where(low, b0, pltpu.roll(b1, HD, 1))
    bottom = jnp.where(low, pltpu.roll(b0, HD, 1), b1)
    return jnp.where(row < HD, top, bottom).T


def _scores_t(kb, qw, bias2, row0, padded):
    s = jnp.dot(kb, qw, preferred_element_type=F32) + bias2
    if padded:
        s = jnp.where(row0 + lax.broadcasted_iota(jnp.int32, (BAND2, 256), 0) >= PADR, s, NEG_INF)
    return s


def _unit_loops(s, unit):
    lax.fori_loop(0, PADR // UNIT, lambda u, c: unit(u, True, c), 0, unroll=2)
    lax.fori_loop(PADR // UNIT, s // UNIT, lambda u, c: unit(u, False, c), 0, unroll=7)


def _attn_fwd(name, kp, qt, vt, bias2, l):
    s = qt.shape[1]
    nu = s // UNIT

    def body(k_ref, qt_ref, vt_ref, b_ref, o_ref, lse_ref):
        def unit(u, padded, carry):
            r0 = pl.multiple_of(u * UNIT, UNIT)
            sc = _scores_t(k_ref[pl.ds(r0, BAND2), :], _pair_weights(qt_ref[:, pl.ds(r0, UNIT)]), b_ref[...],
                           r0, padded)
            top = jnp.max(sc, axis=0, keepdims=True)
            e = jnp.exp(sc - top)
            total = jnp.sum(e, axis=0, keepdims=True)
            raw = jnp.dot(vt_ref[:, pl.ds(r0, BAND2)], e.astype(BF16), preferred_element_type=F32) * (1.0 / total)
            o_ref[pl.ds(r0, UNIT), :] = _unpair(raw).astype(BF16)
            lse_ref[u] = jnp.broadcast_to(top + jnp.log(total), (8, 256))
            return carry

        _unit_loops(s, unit)

    return pl.pallas_call(
        body, name=name, grid=(AW // 128,),
        in_specs=[pl.BlockSpec((s + PADR, 128), lambda h: (0, h)),
                  pl.BlockSpec((128, s), lambda h: (h, 0)),
                  pl.BlockSpec((128, s + PADR), lambda h: (h, 0)),
                  pl.BlockSpec((None, None, BAND2, 256), lambda h: (l, h, 0, 0))],
        out_specs=[pl.BlockSpec((s, 128), lambda h: (0, h)),
                   pl.BlockSpec((None, nu, 8, 256), lambda h: (h, 0, 0, 0))],
        out_shape=[jax.ShapeDtypeStruct((s, AW), BF16), jax.ShapeDtypeStruct((4, nu, 8, 256), F32)],
        compiler_params=_cp("parallel"),
    )(kp, qt, vt, bias2)


def _attn_bwd(name, q, qt, kp, kt, vp, bias2, l, do, dot, lse, dl, db_all):
    s = q.shape[0]
    nu = s // UNIT

    def body(q_ref, qt_ref, k_ref, kt_ref, v_ref, b_ref, do_ref, dot_ref, lse_ref, dl_ref, dbin_ref,
             dq_ref, dk_ref, dv_ref, db_ref):
        del dbin_ref
        dk_ref[...] = jnp.zeros_like(dk_ref)
        dv_ref[...] = jnp.zeros_like(dv_ref)
        db_ref[...] = jnp.zeros_like(db_ref)

        def unit(u, padded, carry):
            r0 = pl.multiple_of(u * UNIT, UNIT)
            rows, band = pl.ds(r0, UNIT), pl.ds(r0, BAND2)
            sc = _scores_t(k_ref[band, :], _pair_weights(qt_ref[:, rows]), b_ref[...], r0, padded)
            pt = jnp.exp(sc - lse_ref[u][0:1, :])
            dpt = jnp.dot(v_ref[band, :], _pair_weights(dot_ref[:, rows]), preferred_element_type=F32)
            ds = pt * (dpt - dl_ref[u][0:1, :])
            db_ref[...] += ds[0:KB, 0:128] + ds[CH:BAND2, 128:256]
            dsb = ds.astype(BF16)
            dq_ref[rows, :] = _unpair(jnp.dot(kt_ref[:, band], dsb, preferred_element_type=F32))
            dk_ref[band, :] += jnp.dot(dsb, _pair_rows(q_ref[rows, :]), preferred_element_type=F32)
            dv_ref[band, :] += jnp.dot(pt.astype(BF16), _pair_rows(do_ref[rows, :]), preferred_element_type=F32)
            return carry

        _unit_loops(s, unit)

    row_q = pl.BlockSpec((s, 128), lambda h: (0, h))
    col_q = pl.BlockSpec((128, s), lambda h: (h, 0))
    row_k = pl.BlockSpec((s + PADR, 128), lambda h: (0, h))
    col_k = pl.BlockSpec((128, s + PADR), lambda h: (h, 0))
    stat = pl.BlockSpec((None, nu, 8, 256), lambda h: (h, 0, 0, 0))
    return pl.pallas_call(
        body, name=name, grid=(AW // 128,),
        in_specs=[row_q, col_q, row_k, col_k, row_k,
                  pl.BlockSpec((None, None, BAND2, 256), lambda h: (l, h, 0, 0)), row_q, col_q, stat, stat, ANY],
        out_specs=[row_q, row_k, row_k, pl.BlockSpec((None, None, KB, 128), lambda h: (l, h, 0, 0))],
        out_shape=[jax.ShapeDtypeStruct((s, AW), F32),
                   jax.ShapeDtypeStruct((s + PADR, AW), F32),
                   jax.ShapeDtypeStruct((s + PADR, AW), F32),
                   jax.ShapeDtypeStruct((DEPTH, 4, KB, 128), F32)],
        input_output_aliases={10: 3},
        compiler_params=_cp("parallel"),
    )(q, qt, kp, kt, vp, bias2, do, dot, lse, dl, db_all)


def _rowsum_layout(dl, nu):
    d = dl[:, :8].reshape(nu, 2, CH, 4, 2)
    d = d.transpose(3, 0, 1, 4, 2).reshape(4, nu, 1, 256)
    return jnp.broadcast_to(d, (4, nu, 8, 256))


def _rows_before(cur, prev, k):
    row = lax.broadcasted_iota(jnp.int32, cur.shape, 0)
    return jnp.where(row >= k, pltpu.roll(cur, k, 0), pltpu.roll(prev, k, 0))


def _rows_after(cur, nxt, k):
    n = cur.shape[0]
    row = lax.broadcasted_iota(jnp.int32, cur.shape, 0)
    return jnp.where(row < n - k, pltpu.roll(cur, n - k, 0), pltpu.roll(nxt, n - k, 0))


def _pool_window_lanes():
    lg = lax.broadcasted_iota(jnp.int32, (1, PWD), 1) // 64
    return lg, jnp.where(lg == 0, 2.0, jnp.where(lg == 1, 4.0, jnp.where(lg == 2, 8.0, 16.0))).astype(F32)


def _pool_mean_minus_token(u, up, row0):
    lg, wv = _pool_window_lanes()
    sums = []
    c, p = u, up
    for k in (1, 2, 4, 8):
        c2 = c + _rows_before(c, p, k)
        p = p + pltpu.roll(p, k, 0)
        c = c2
        sums.append(c)
    win = jnp.where(lg == 0, sums[0], jnp.where(lg == 1, sums[1], jnp.where(lg == 2, sums[2], sums[3])))
    pos1 = (row0 + lax.broadcasted_iota(jnp.int32, u.shape, 0) + 1).astype(F32)
    cnt = jnp.minimum(pos1, wv)
    return win / cnt - u, cnt


def _conv_taps(z, zp, w0, w1, w2):
    z1 = _rows_before(z, zp, 1)
    z2 = _rows_before(z, zp, 2)
    return (w0 * z2 + w1 * z1) + w2 * z, z1, z2


CP_TM = 512
HALO = 16


def _halo_before(tm, col):
    return pl.BlockSpec((HALO, CW), lambda i: (jnp.maximum(i * (tm // HALO) - 1, 0), col))


def _halo_after(tm, col, rows):
    return pl.BlockSpec((HALO, CW), lambda i: (jnp.minimum((i + 1) * (tm // HALO), rows // HALO - 1), col))


def _as_block_end(halo, tm):
    return jnp.concatenate([jnp.zeros((tm - HALO, halo.shape[1]), halo.dtype), halo], axis=0)


def _as_block_start(halo, tm):
    return jnp.concatenate([halo, jnp.zeros((tm - HALO, halo.shape[1]), halo.dtype)], axis=0)


def _convpool_fwd(name, p, o, cw, pwbd, ps):
    s = p.shape[0]
    tm = CP_TM
    nb = s // tm

    def body(gb_ref, gc_ref, hin_ref, u_ref, gcp_ref, hinp_ref, up_ref, o_ref, cw_ref, pw_ref, ps_ref, mix_ref):
        i = pl.program_id(0)
        has_prev = i > 0
        z = gc_ref[...] * hin_ref[...]
        zp = _as_block_end(jnp.where(has_prev, gcp_ref[...] * hinp_ref[...], 0.0), tm)
        y3, _, _ = _conv_taps(z, zp, cw_ref[0:1, :], cw_ref[1:2, :], cw_ref[2:3, :])
        m, _ = _pool_mean_minus_token(u_ref[...], _as_block_end(jnp.where(has_prev, up_ref[...], 0.0), tm), i * tm)
        yp = jnp.dot(m.astype(BF16), pw_ref[...].astype(BF16), preferred_element_type=F32) * ps_ref[...]
        mix_ref[:, 0:AW] = o_ref[...]
        mix_ref[:, AW:AW + CW] = (gb_ref[...] * y3).astype(BF16)
        mix_ref[:, AW + CW:D] = yp.astype(BF16)

    def cur(col):
        return pl.BlockSpec((tm, CW), lambda i: (i, col))

    def whole(a):
        return pl.BlockSpec(a.shape, lambda i: (0,) * a.ndim)

    return pl.pallas_call(
        body, name=name, grid=(nb,),
        in_specs=[cur(6), cur(7), cur(8), cur(9), _halo_before(tm, 7), _halo_before(tm, 8), _halo_before(tm, 9),
                  pl.BlockSpec((tm, AW), lambda i: (i, 0)), whole(cw), whole(pwbd), whole(ps)],
        out_specs=pl.BlockSpec((tm, D), lambda i: (i, 0)),
        out_shape=jax.ShapeDtypeStruct((s, D), BF16),
        compiler_params=_cp("parallel"),
    )(p, p, p, p, p, p, p, o, cw, pwbd, ps)


def _convpool_bwd(name, p, dmix, cw, pwbd, ps):
    s = p.shape[0]
    tm = CP_TM
    nb = s // tm

    def body(gb_ref, gc_ref, hin_ref, u_ref, gcp_ref, hinp_ref, up_ref, gbn_ref, dyc_ref, dyp_ref, dycn_ref, dypn_ref,
             cw_ref, pw_ref, ps_ref, dcp_ref, dw0_ref, dw1_ref, dw2_ref, dps_ref, dpw_ref):
        i = pl.program_id(0)
        has_prev = i > 0
        has_next = i < nb - 1
        w0, w1, w2 = cw_ref[0:1, :], cw_ref[1:2, :], cw_ref[2:3, :]
        gb, gc, hin = gb_ref[...], gc_ref[...], hin_ref[...]
        dyc = dyc_ref[...]
        z = gc * hin
        zp = _as_block_end(jnp.where(has_prev, gcp_ref[...] * hinp_ref[...], 0.0), tm)
        y3, z1, z2 = _conv_taps(z, zp, w0, w1, w2)
        dy3 = dyc * gb
        dy3n = _as_block_start(jnp.where(has_next, dycn_ref[...] * gbn_ref[...], 0.0), tm)
        dz = w2 * dy3 + w1 * _rows_after(dy3, dy3n, 1) + w0 * _rows_after(dy3, dy3n, 2)
        pw = pw_ref[...].astype(BF16)
        psv = ps_ref[...]
        m, cnt = _pool_mean_minus_token(u_ref[...], _as_block_end(jnp.where(has_prev, up_ref[...], 0.0), tm), i * tm)
        mb = m.astype(BF16)
        dyp = dyp_ref[...]
        dmp = (dyp * psv).astype(BF16)
        dmpn = jnp.where(has_next, dypn_ref[...] * psv, 0.0).astype(BF16)
        nt = (((1,), (1,)), ((), ()))
        dm = lax.dot_general(dmp, pw, nt, preferred_element_type=F32)
        dmn = lax.dot_general(dmpn, pw, nt, preferred_element_type=F32)
        lg, wv = _pool_window_lanes()
        cc, cn = dm / cnt, _as_block_start(dmn / wv, tm)
        sums = []
        for k in (1, 2, 4, 8):
            c2 = cc + _rows_after(cc, cn, k)
            cn = cn + pltpu.roll(cn, tm - k, 0)
            cc = c2
            sums.append(cc)
        du = jnp.where(lg == 0, sums[0], jnp.where(lg == 1, sums[1], jnp.where(lg == 2, sums[2], sums[3]))) - dm
        dcp_ref[:, 0:CW] = (dyc * y3).astype(BF16)
        dcp_ref[:, CW:2 * CW] = (dz * hin).astype(BF16)
        dcp_ref[:, 2 * CW:3 * CW] = (dz * gc).astype(BF16)
        dcp_ref[:, 3 * CW:4 * CW] = du.astype(BF16)
        parts = (jnp.sum(dy3 * z2, axis=0, keepdims=True),
                 jnp.sum(dy3 * z1, axis=0, keepdims=True),
                 jnp.sum(dy3 * z, axis=0, keepdims=True),
                 jnp.sum(dyp * jnp.dot(mb, pw, preferred_element_type=F32), axis=0, keepdims=True),
                 lax.dot_general(mb, dmp, (((0,), (0,)), ((), ())), preferred_element_type=F32))
        accs = (dw0_ref, dw1_ref, dw2_ref, dps_ref, dpw_ref)

        @pl.when(i == 0)
        def _():
            for a, v in zip(accs, parts):
                a[...] = v

        @pl.when(i > 0)
        def _():
            for a, v in zip(accs, parts):
                a[...] += v

    def cur(col):
        return pl.BlockSpec((tm, CW), lambda i: (i, col))

    def prev(col):
        return _halo_before(tm, col)

    def nxt(col):
        return _halo_after(tm, col, s)

    def whole(shape):
        return pl.BlockSpec(shape, lambda i: (0,) * len(shape))

    row = jax.ShapeDtypeStruct((1, CW), F32)
    return pl.pallas_call(
        body, name=name, grid=(nb,),
        in_specs=[cur(6), cur(7), cur(8), cur(9), prev(7), prev(8), prev(9), nxt(6),
                  cur(0), cur(1), nxt(0), nxt(1), whole(cw.shape), whole(pwbd.shape), whole(ps.shape)],
        out_specs=[pl.BlockSpec((tm, D), lambda i: (i, 0)), whole((1, CW)), whole((1, CW)), whole((1, CW)),
                   whole((1, PWD)), whole((PWD, PWD))],
        out_shape=[jax.ShapeDtypeStruct((s, D), BF16), row, row, row, row,
                   jax.ShapeDtypeStruct((PWD, PWD), F32)],
        compiler_params=_cp("arbitrary"),
    )(p, p, p, p, p, p, p, p, dmix, dmix, dmix, dmix, cw, pwbd, ps)


def _qkv_bwd(name, p, dq, dkp, dvp, dcp, qg, kg):
    s = p.shape[0]
    tm = 512
    off = PADR // tm

    def body(pq_ref, pk_ref, dq_ref, dk_ref, dv_ref, dcp_ref, qg_ref, kg_ref, dp_ref, dqg_ref, dkg_ref):
        i = pl.program_id(0)
        hm = _head_mean_matrix()

        def nrm_bwd(x, g, dy):
            r = lax.rsqrt(_head_mean(x * x, hm) + EPS)
            xn = x * r
            dxn = dy * g
            dx = r * (dxn - xn * _head_mean(dxn * xn, hm))
            dg = jnp.sum(dy * xn, axis=0, keepdims=True)
            dg = (dg[:, 0:128] + dg[:, 128:256]) + (dg[:, 256:384] + dg[:, 384:512])
            return dx, dg + pltpu.roll(dg, HD, 1)

        dxq, dgq = nrm_bwd(pq_ref[...], qg_ref[...], dq_ref[...] * 0.125)
        dxk, dgk = nrm_bwd(pk_ref[...], kg_ref[...], dk_ref[...])
        dp_ref[:, 0:AW] = dxq.astype(BF16)
        dp_ref[:, AW:2 * AW] = dxk.astype(BF16)
        dp_ref[:, 2 * AW:3 * AW] = dv_ref[...].astype(BF16)
        dp_ref[:, 3 * AW:DIN] = dcp_ref[...]

        @pl.when(i == 0)
        def _():
            dqg_ref[...] = dgq
            dkg_ref[...] = dgk

        @pl.when(i > 0)
        def _():
            dqg_ref[...] += dgq
            dkg_ref[...] += dgk

    gspec = pl.BlockSpec((1, AW), lambda i: (0, 0))
    gout = pl.BlockSpec((1, 128), lambda i: (0, 0))
    return pl.pallas_call(
        body, name=name, grid=(s // tm,),
        in_specs=[pl.BlockSpec((tm, AW), lambda i: (i, 0)), pl.BlockSpec((tm, AW), lambda i: (i, 1)),
                  pl.BlockSpec((tm, AW), lambda i: (i, 0)),
                  pl.BlockSpec((tm, AW), lambda i: (i + off, 0)),
                  pl.BlockSpec((tm, AW), lambda i: (i + off, 0)),
                  pl.BlockSpec((tm, D), lambda i: (i, 0)), gspec, gspec],
        out_specs=[pl.BlockSpec((tm, DIN), lambda i: (i, 0)), gout, gout],
        out_shape=[jax.ShapeDtypeStruct((s, DIN), BF16), jax.ShapeDtypeStruct((1, 128), F32),
                   jax.ShapeDtypeStruct((1, 128), F32)],
        compiler_params=_cp("arbitrary"),
    )(p, p, dq, dkp, dvp, dcp, qg, kg)


def _loss_grad(name, y, t):
    s = y.shape[0]
    tm = 512

    def body(y_ref, t_ref, dy_ref, dyb_ref, l_ref):
        i = pl.program_id(0)
        e = y_ref[...] - t_ref[...]
        dy = e * (1.0 / D)
        dy_ref[...] = dy
        dyb_ref[...] = dy.astype(BF16)
        part = 0.5 * jnp.sum(jnp.mean(e * e, axis=-1, keepdims=True), axis=0, keepdims=True)

        @pl.when(i == 0)
        def _():
            l_ref[...] = part

        @pl.when(i > 0)
        def _():
            l_ref[...] += part

    blk = pl.BlockSpec((tm, D), lambda i: (i, 0))
    return pl.pallas_call(
        body, name=name, grid=(s // tm,),
        in_specs=[blk, blk],
        out_specs=[blk, blk, pl.BlockSpec((1, 1), lambda i: (0, 0))],
        out_shape=[jax.ShapeDtypeStruct((s, D), F32), jax.ShapeDtypeStruct((s, D), BF16),
                   jax.ShapeDtypeStruct((1, 1), F32)],
        compiler_params=_cp("arbitrary"),
    )(y, t)


def _mm_nt_relu(name, dxb, w, l, a):
    s = dxb.shape[0]
    tm = MM_ROWS

    def body(d_ref, w_ref, a_ref, o_ref):
        df = lax.dot_general(d_ref[...], w_ref[...], NT_DIMS, preferred_element_type=F32)
        o_ref[...] = (df * (2.0 * jnp.maximum(a_ref[...].astype(F32), 0.0))).astype(BF16)

    return pl.pallas_call(
        body, name=name, grid=(s // tm,),
        in_specs=[pl.BlockSpec((tm, D), lambda i: (i, 0)),
                  pl.BlockSpec((None, DFF, D), lambda i: (l, 0, 0)),
                  pl.BlockSpec((tm, DFF), lambda i: (i, 0))],
        out_specs=pl.BlockSpec((tm, DFF), lambda i: (i, 0)),
        out_shape=jax.ShapeDtypeStruct((s, DFF), BF16),
        compiler_params=_cp("parallel"),
    )(dxb, w, a)


def _proj_out_bwd(name, dxb, w, l, mix):
    s = dxb.shape[0]
    tm = _mm_rows(D, D)

    def body(d_ref, w_ref, o_ref, do_ref, dot_ref, dcp_ref, dl_ref):
        d = d_ref[...]
        wa, wc = w_ref[0:AW, :], w_ref[AW:D, :]
        do = lax.dot_general(d, wa, NT_DIMS, preferred_element_type=F32)
        do_ref[...] = do.astype(BF16)
        dot_ref[...] = lax.dot_general(wa, d, NT_DIMS, preferred_element_type=F32).astype(BF16)
        dcp_ref[...] = lax.dot_general(d, wc, NT_DIMS, preferred_element_type=F32)
        head = lax.broadcasted_iota(jnp.int32, (AW, 128), 0) // HD
        pick = jnp.where(head == lax.broadcasted_iota(jnp.int32, (AW, 128), 1), 1.0, 0.0).astype(BF16)
        dl_ref[...] = _two_pass_dot(do * o_ref[...].astype(F32), pick)

    return pl.pallas_call(
        body, name=name, grid=(s // tm,),
        in_specs=[pl.BlockSpec((tm, D), lambda i: (i, 0)),
                  pl.BlockSpec((None, D, D), lambda i: (l, 0, 0)),
                  pl.BlockSpec((tm, AW), lambda i: (i, 0))],
        out_specs=[pl.BlockSpec((tm, AW), lambda i: (i, 0)), pl.BlockSpec((AW, tm), lambda i: (0, i)),
                   pl.BlockSpec((tm, D - AW), lambda i: (i, 0)), pl.BlockSpec((tm, 128), lambda i: (i, 0))],
        out_shape=[jax.ShapeDtypeStruct((s, AW), BF16), jax.ShapeDtypeStruct((AW, s), BF16),
                   jax.ShapeDtypeStruct((s, D - AW), F32), jax.ShapeDtypeStruct((s, 128), F32)],
        compiler_params=_cp("parallel"),
    )(dxb, w, mix)


def _mm_nt_normbwd(name, gy, w, l, x, g, dres, dep):
    s, k = gy.shape
    tm = MM_ROWS

    def body(gy_ref, w_ref, x_ref, g_ref, dr_ref, dep_ref, dx_ref, dxb_ref, dg_ref):
        del dep_ref
        i = pl.program_id(0)
        dh = lax.dot_general(gy_ref[...], w_ref[...], NT_DIMS, preferred_element_type=F32)
        xv = x_ref[...]
        r = _inv_rms(xv)
        xn = xv * r
        dxn = dh * g_ref[...]
        dx = r * (dxn - xn * jnp.mean(dxn * xn, axis=-1, keepdims=True)) + dr_ref[...]
        dx_ref[...] = dx
        dxb_ref[...] = dx.astype(BF16)
        part = jnp.sum(dh * xn, axis=0, keepdims=True)

        @pl.when(i == 0)
        def _():
            dg_ref[...] = part

        @pl.when(i > 0)
        def _():
            dg_ref[...] += part

    blk = pl.BlockSpec((tm, D), lambda i: (i, 0))
    vec = pl.BlockSpec((1, D), lambda i: (0, 0))
    return pl.pallas_call(
        body, name=name, grid=(s // tm,),
        in_specs=[pl.BlockSpec((tm, k), lambda i: (i, 0)),
                  pl.BlockSpec((None, D, k), lambda i: (l, 0, 0)), blk, vec, blk, ANY],
        out_specs=[blk, blk, vec],
        out_shape=[jax.ShapeDtypeStruct((s, D), F32), jax.ShapeDtypeStruct((s, D), BF16),
                   jax.ShapeDtypeStruct((1, D), F32)],
        compiler_params=_cp("arbitrary"),
    )(gy, w, x, g, dres, dep)


def _mm_tn(name, a, b, tma, tnb, relu2=False):
    s, m = a.shape
    n = b.shape[1]

    def body(a_ref, b_ref, o_ref):
        av = _relu2(a_ref[...]) if relu2 else a_ref[...]
        o_ref[...] = lax.dot_general(av, b_ref[...], (((0,), (0,)), ((), ())),
                                     preferred_element_type=F32).astype(BF16)

    return pl.pallas_call(
        body, name=name, grid=(m // tma, n // tnb),
        in_specs=[pl.BlockSpec((s, tma), lambda i, j: (0, i), pipeline_mode=pl.Buffered(1) if m == tma else None),
                  pl.BlockSpec((s, tnb), lambda i, j: (0, j))],
        out_specs=pl.BlockSpec((tma, tnb), lambda i, j: (i, j)),
        out_shape=jax.ShapeDtypeStruct((m, n), BF16),
        compiler_params=_cp("parallel", "parallel"),
    )(a, b)


def _adamw_math(gv, wv, mv, vv):
    mn = ADAM_B1 * mv + (1.0 - ADAM_B1) * gv
    vn = ADAM_B2 * vv + (1.0 - ADAM_B2) * jnp.square(gv)
    m_hat = mn / (1.0 - ADAM_B1 ** ADAM_STEP)
    v_hat = vn / (1.0 - ADAM_B2 ** ADAM_STEP)
    return gv, -ADAM_LR * (m_hat / (jnp.sqrt(v_hat) + ADAM_EPS) + ADAM_WD * wv), mn, vn


def _adamw(name, g, w, m, v):
    r, c = g.shape
    tm = 256 if r % 256 == 0 else r

    def body(g_ref, w_ref, m_ref, v_ref, go_ref, d_ref, mo_ref, vo_ref):
        go_ref[...], d_ref[...], mo_ref[...], vo_ref[...] = _adamw_math(g_ref[...], w_ref[...], m_ref[...], v_ref[...])

    blk = pl.BlockSpec((tm, c), lambda i: (i, 0))
    return pl.pallas_call(
        body, name=name, grid=(r // tm,),
        in_specs=[blk] * 4, out_specs=[blk] * 4,
        out_shape=[jax.ShapeDtypeStruct((r, c), F32)] * 4,
        compiler_params=_cp("parallel"),
    )(g, w, m, v)


def _place():
    x, y, c = lax.axis_index("x"), lax.axis_index("y"), lax.axis_index("c")
    chips = [(1 - x, y), (x, 1 - y), (1 - x, 1 - y)]
    return x, y, c, chips


BLOCK_AXIS = (2, 1, 2, 1)
LARGE_DIMS = ((D, DIN), (D, D), (D, DFF), (DFF, D))


def _full_shape(t, layers, dtype):
    r, c = LARGE_DIMS[t]
    return jax.ShapeDtypeStruct((layers, r, c), dtype)


def _cast_into_full(name, t, shard, b1, dep):
    _, r, c = shard.shape
    tm = min(512, r)
    if BLOCK_AXIS[t] == 1:
        out_spec = pl.BlockSpec((None, tm, c), lambda l, i, br: (l, br[0] * (r // tm) + i, 0))
    else:
        out_spec = pl.BlockSpec((None, tm, c), lambda l, i, br: (l, i, br[0]))

    def body(b_ref, x_ref, dep_ref, o_ref):
        del b_ref, dep_ref
        o_ref[...] = x_ref[...].astype(BF16)

    return pl.pallas_call(
        body, name=name,
        grid_spec=pltpu.PrefetchScalarGridSpec(
            num_scalar_prefetch=1, grid=(DEPTH, r // tm),
            in_specs=[pl.BlockSpec((None, tm, c), lambda l, i, br: (l, i, 0)), ANY],
            out_specs=out_spec),
        out_shape=_full_shape(t, DEPTH, BF16),
        compiler_params=_cp("parallel", "parallel"),
    )(b1, shard, dep)


HBM = pl.BlockSpec(memory_space=pltpu.HBM)
SEM = pl.BlockSpec(memory_space=pltpu.SEMAPHORE)
DATAFLOW = pltpu.SideEffectType.DATAFLOW_SIDE_EFFECTING


def _half(ref, l, t, b, c):
    r, cols = LARGE_DIMS[t]
    if BLOCK_AXIS[t] == 1:
        n = r // 8
        return ref.at[l, pl.ds(pl.multiple_of(b * (2 * n) + c * n, 16), n), :]
    n, w = r // 2, cols // 4
    return ref.at[l, pl.ds(pl.multiple_of(c * n, 16), n), pl.ds(pl.multiple_of(b * w, 128), w)]


def _gather_start(name, layers, ts, fulls):
    n = len(ts)

    def body(*refs):
        f_refs, sems = refs[n:2 * n], refs[2 * n:2 * n + 2 * len(layers)]
        x, y, c, chips = _place()
        for i, l in enumerate(layers):
            for k, t in enumerate(ts):
                own = _half(f_refs[k], l, t, 2 * x + y, c)
                for j, (cx, cy) in enumerate(chips):
                    pltpu.make_async_remote_copy(src_ref=own, dst_ref=own, send_sem=sems[2 * i].at[3 * t + j],
                                                 recv_sem=sems[2 * i + 1].at[3 * t + j], device_id=(cx, cy, c),
                                                 device_id_type=MESH).start()
        refs[-1][...] = jnp.zeros((8, 128), F32)

    outs = pl.pallas_call(
        body, name=name,
        in_specs=[HBM] * n,
        out_specs=[HBM] * n + [SEM] * (2 * len(layers)) + [pl.BlockSpec(memory_space=pltpu.VMEM)],
        out_shape=[pltpu.HBM(f.shape, f.dtype) for f in fulls]
        + [pltpu.SemaphoreType.DMA((12,))] * (2 * len(layers)) + [jax.ShapeDtypeStruct((8, 128), F32)],
        input_output_aliases={k: k for k in range(n)},
        compiler_params=pltpu.CompilerParams(has_side_effects=DATAFLOW),
    )(*[pltpu.with_memory_space_constraint(f, pltpu.HBM) for f in fulls])
    return outs[0:n], {l: (outs[n + 2 * i], outs[n + 1 + 2 * i]) for i, l in enumerate(layers)}, outs[-1]


def _gather_wait(name, l, ts, fulls, sems, after):
    def body(*refs):
        send_sems, recv_sems, f_refs = refs[4], refs[5], refs[7:11]
        x, y, c, chips = _place()
        for t in ts:
            own = _half(f_refs[t], l, t, 2 * x + y, c)
            for j, (cx, cy) in enumerate(chips):
                landed = _half(f_refs[t], l, t, 2 * cx + cy, c)
                pltpu.make_async_remote_copy(src_ref=own, dst_ref=landed, send_sem=send_sems.at[3 * t + j],
                                             recv_sem=recv_sems.at[3 * t + j], device_id=(cx, cy, c),
                                             device_id_type=MESH).wait()

    return pl.pallas_call(
        body, name=name,
        in_specs=[HBM] * 4 + [SEM, SEM, ANY], out_specs=[HBM] * 4,
        out_shape=[pltpu.HBM(s.shape, s.dtype) for s in (_full_shape(t, DEPTH, BF16) for t in range(4))],
        input_output_aliases={t: t for t in range(4)},
        compiler_params=pltpu.CompilerParams(has_side_effects=DATAFLOW),
    )(*fulls, sems[0], sems[1], after)


def _pass_on(name, l, ts, fulls):
    def body(*refs):
        f_refs, send_sems, recv_sems = refs[4:8], refs[8], refs[9]
        x, y, c, chips = _place()

        def copy(t, j, half):
            cx, cy = chips[j]
            part = _half(f_refs[t], l, t, 2 * cx + cy, half)
            return pltpu.make_async_remote_copy(src_ref=part, dst_ref=part, send_sem=send_sems.at[3 * t + j],
                                                recv_sem=recv_sems.at[3 * t + j], device_id=(x, y, 1 - c),
                                                device_id_type=MESH)

        for t in ts:
            for j in range(3):
                copy(t, j, c).start()
        for t in ts:
            for j in range(3):
                copy(t, j, 1 - c).wait_recv()
                copy(t, j, c).wait_send()

    return pl.pallas_call(
        body, name=name,
        in_specs=[ANY] * 4, out_specs=[ANY] * 4,
        out_shape=[_full_shape(t, DEPTH, BF16) for t in range(4)],
        input_output_aliases={t: t for t in range(4)},
        scratch_shapes=[pltpu.SemaphoreType.DMA((12,)), pltpu.SemaphoreType.DMA((12,))],
    )(*fulls)


def _block2d(ref, t, b):
    r, cols = LARGE_DIMS[t]
    if BLOCK_AXIS[t] == 1:
        return ref.at[pl.ds(pl.multiple_of(b * (r // 4), 16), r // 4), :]
    return ref.at[:, pl.ds(pl.multiple_of(b * (cols // 4), 128), cols // 4)]


def _block_dims(t):
    r, cols = LARGE_DIMS[t]
    return (r // 4, cols) if BLOCK_AXIS[t] == 1 else (r, cols // 4)


def _reduce_copies(ts, g_refs, r_refs, send_sems, recv_sems):
    _, _, c, chips = _place()
    return [pltpu.make_async_remote_copy(src_ref=_block2d(g_refs[i], t, 2 * cx + cy), dst_ref=r_refs[i].at[j],
                                         send_sem=send_sems.at[3 * i + j], recv_sem=recv_sems.at[3 * i + j],
                                         device_id=(cx, cy, c), device_id_type=MESH)
            for i, t in enumerate(ts) for j, (cx, cy) in enumerate(chips)]


def _reduce_start(name, ts, grads):
    n = len(ts)

    def body(*refs):
        for cp in _reduce_copies(ts, refs[n:2 * n], refs[2 * n:3 * n], refs[3 * n], refs[3 * n + 1]):
            cp.start()
        refs[3 * n + 2][...] = jnp.zeros((8, 128), F32)

    outs = pl.pallas_call(
        body, name=name,
        in_specs=[HBM] * n,
        out_specs=[HBM] * (2 * n) + [SEM, SEM, pl.BlockSpec(memory_space=pltpu.VMEM)],
        out_shape=[pltpu.HBM(g.shape, BF16) for g in grads]
        + [pltpu.HBM((3,) + _block_dims(t), BF16) for t in ts]
        + [pltpu.SemaphoreType.DMA((3 * n,)), pltpu.SemaphoreType.DMA((3 * n,)), jax.ShapeDtypeStruct((8, 128), F32)],
        input_output_aliases={i: i for i in range(n)},
        compiler_params=pltpu.CompilerParams(has_side_effects=DATAFLOW),
    )(*[pltpu.with_memory_space_constraint(g, pltpu.HBM) for g in grads])
    return outs[0:n], outs[n:2 * n], (outs[2 * n], outs[2 * n + 1]), outs[2 * n + 2]


def _reduce_wait(name, ts, grads, landing, sems, afters):
    n = len(ts)
    first_out = 2 * n + 2 + len(afters)

    def body(*refs):
        for cp in _reduce_copies(ts, refs[first_out:first_out + n], refs[first_out + n:first_out + 2 * n],
                                 refs[2 * n], refs[2 * n + 1]):
            cp.wait()

    outs = pl.pallas_call(
        body, name=name,
        in_specs=[HBM] * (2 * n) + [SEM, SEM] + [ANY] * len(afters), out_specs=[HBM] * (2 * n),
        out_shape=[pltpu.HBM(g.shape, BF16) for g in grads] + [pltpu.HBM(r.shape, BF16) for r in landing],
        input_output_aliases={i: i for i in range(2 * n)},
        compiler_params=pltpu.CompilerParams(has_side_effects=DATAFLOW),
    )(*grads, *landing, sems[0], sems[1], *afters)
    return outs[0:n], outs[n:2 * n]


def _add4(name, t, own, landed, b1):
    rb, cb = _block_dims(t)
    tm = min(512, rb)
    if BLOCK_AXIS[t] == 1:
        own_spec = pl.BlockSpec((tm, cb), lambda i, br: (br[0] * (rb // tm) + i, 0))
    else:
        own_spec = pl.BlockSpec((tm, cb), lambda i, br: (i, br[0]))

    def body(b_ref, o_ref, r0_ref, r1_ref, r2_ref, s_ref):
        del b_ref
        s_ref[...] = ((o_ref[...].astype(F32) + r0_ref[...].astype(F32))
                      + (r1_ref[...].astype(F32) + r2_ref[...].astype(F32))).astype(BF16)

    def got(j):
        return pl.BlockSpec((None, tm, cb), lambda i, br: (j, i, 0))

    return pl.pallas_call(
        body, name=name,
        grid_spec=pltpu.PrefetchScalarGridSpec(
            num_scalar_prefetch=1, grid=(rb // tm,),
            in_specs=[own_spec, got(0), got(1), got(2)],
            out_specs=pl.BlockSpec((tm, cb), lambda i, br: (i, 0))),
        out_shape=jax.ShapeDtypeStruct((rb, cb), BF16),
        compiler_params=_cp("parallel"),
    )(b1, own, landed, landed, landed)


def _swap_sib(name, sums):
    def body(*refs):
        s_refs, t_refs, send_sems, recv_sems = refs[0:4], refs[4:8], refs[8], refs[9]
        x, y, c, _ = _place()
        cps = [pltpu.make_async_remote_copy(src_ref=s_refs[t], dst_ref=t_refs[t], send_sem=send_sems.at[t],
                                            recv_sem=recv_sems.at[t], device_id=(x, y, 1 - c), device_id_type=MESH)
               for t in range(4)]
        for cp in cps:
            cp.start()
        for cp in cps:
            cp.wait()

    return pl.pallas_call(
        body, name=name,
        in_specs=[ANY] * 4, out_specs=[ANY] * 4,
        out_shape=[jax.ShapeDtypeStruct(s.shape, BF16) for s in sums],
        scratch_shapes=[pltpu.SemaphoreType.DMA((4,)), pltpu.SemaphoreType.DMA((4,))],
    )(*sums)


def _adamw_pair(name, l, s_own, s_sib, w, m, v, outs):
    rb, cb = s_own.shape
    tm = min(512, rb)

    def body(a_ref, b_ref, w_ref, m_ref, v_ref, g0, d0, m0, v0, go_ref, d_ref, mo_ref, vo_ref):
        del g0, d0, m0, v0
        gv = a_ref[...].astype(F32) + b_ref[...].astype(F32)
        go_ref[...], d_ref[...], mo_ref[...], vo_ref[...] = _adamw_math(gv, w_ref[...], m_ref[...], v_ref[...])

    part = pl.BlockSpec((tm, cb), lambda i: (i, 0))
    layer = pl.BlockSpec((None, tm, cb), lambda i: (l, i, 0))
    return pl.pallas_call(
        body, name=name, grid=(rb // tm,),
        in_specs=[part, part, layer, layer, layer] + [ANY] * 4,
        out_specs=[layer] * 4,
        out_shape=[jax.ShapeDtypeStruct((DEPTH, rb, cb), F32)] * 4,
        input_output_aliases={5 + i: i for i in range(4)},
        compiler_params=_cp("parallel"),
    )(s_own, s_sib, w, m, v, *outs)


def _all_gather8(name, v, dep):
    m_per, n = v.shape

    def body(v_ref, dep_ref, out_ref, send_sems, recv_sems, local_sem):
        del dep_ref
        x, y, c, chips = _place()
        me, sib = (x, y, c), (x, y, 1 - c)

        def rows(px, py, pc):
            return out_ref.at[pl.ds((4 * px + 2 * py + pc) * m_per, m_per), :]

        def copy(k, block, to, src=None):
            return pltpu.make_async_remote_copy(
                src_ref=rows(*block) if src is None else src, dst_ref=rows(*block),
                send_sem=send_sems.at[k], recv_sem=recv_sems.at[k], device_id=to, device_id_type=MESH)

        mine = pltpu.make_async_copy(v_ref, rows(*me), local_sem)
        mine.start()
        first = [copy(0, me, sib, src=v_ref)]
        first += [copy(1 + j, me, (*chip, c), src=v_ref) for j, chip in enumerate(chips)]
        for cp in first:
            cp.start()
        passed = [copy(4 + j, (*chip, c), sib) for j, chip in enumerate(chips)]
        for j, chip in enumerate(chips):
            copy(1 + j, (*chip, c), me).wait_recv()
            passed[j].start()
        copy(0, sib, me).wait_recv()
        for j, chip in enumerate(chips):
            copy(4 + j, (*chip, 1 - c), me).wait_recv()
        for cp in first + passed:
            cp.wait_send()
        mine.wait()

    return pl.pallas_call(
        body, name=name,
        out_shape=jax.ShapeDtypeStruct((8 * m_per, n), v.dtype),
        in_specs=[pl.BlockSpec(memory_space=pltpu.VMEM), ANY],
        out_specs=pl.BlockSpec(memory_space=pltpu.VMEM),
        scratch_shapes=[pltpu.SemaphoreType.DMA((7,)), pltpu.SemaphoreType.DMA((7,)), pltpu.SemaphoreType.DMA],
    )(v, dep)


def _sum8(name, g):
    def body(g_ref, o_ref):
        acc = g_ref[0]
        for d in range(1, 8):
            acc = acc + g_ref[d]
        o_ref[...] = acc

    return pl.pallas_call(body, name=name, out_shape=jax.ShapeDtypeStruct(g.shape[1:], F32))(g)


def _pack(parts):
    flat = []
    for a in parts:
        a = a.reshape(-1)
        flat.append(jnp.pad(a, (0, (-a.shape[0]) % 128)))
    cat = jnp.concatenate(flat)
    cat = jnp.pad(cat, (0, (-cat.shape[0]) % 1024))
    return cat.reshape(-1, 128)


def _unpack(packed, shapes):
    flat = packed.reshape(-1)
    out, at = [], 0
    for shp in shapes:
        n = 1
        for d in shp:
            n *= d
        out.append(flat[at:at + n].reshape(shp))
        at += n + (-n) % 128
    return out


def _local_step(x, target, layer_weights, on_grads, small):
    qg_all = jnp.tile(small["q_norm_g"], (1, 8))
    kg_all = jnp.tile(small["k_norm_g"], (1, 8))
    bias_all = _bias_layout(_bias_expand("bias_expand", jnp.pad(small["rel_bias"], ((0, 0), (0, 0), (0, NIDX - 257)))))
    same_group = jnp.eye(4, dtype=F32)[None, :, None, :, None]
    pwbd_all = (small["pool_w"][:, :, :, None, :] * same_group).reshape(DEPTH, PWD, PWD)
    saved = []
    xin = x
    h = _rmsnorm("norm_first", x, small["norm1_g"][0:1])
    for l in range(DEPTH):
        w_in = layer_weights(l, (0,), xin)[0]
        qg, kg = qg_all[l:l + 1], kg_all[l:l + 1]
        cw, pwbd, ps = small["conv_w"][l], pwbd_all[l], small["pool_scale"][l:l + 1]
        p = _mm_nn(f"proj_in_{l}", h, w_in, l, F32)
        q, qt, kp, kt, vp, vt = _qkv(f"qkv_{l}", p, qg, kg)
        o, lse = _attn_fwd(f"attn_fwd_{l}", kp, qt, vt, bias_all, l)
        w_in, w_out, w_1, w_2 = layer_weights(l, (1, 2, 3), o)
        mix = _convpool_fwd(f"convpool_fwd_{l}", p, o, cw, pwbd, ps)
        x1, h2 = _mm_res_norm(f"proj_out_{l}", mix, w_out, l, xin, small["norm2_g"][l:l + 1])
        gnext = small["norm1_g"][(l + 1) % DEPTH][None]
        a, x2, hnext = _mlp_fwd(f"mlp_{l}", h2, w_1, w_2, l, x1, gnext)
        saved.append(dict(xin=xin, h=h, p=p, q=q, qt=qt, kp=kp, kt=kt, vp=vp, mix=mix, x1=x1, h2=h2, a=a, lse=lse,
                          qg=qg, kg=kg, cw=cw, pwbd=pwbd, ps=ps))
        xin, h = x2, hnext

    dx, dxb, loss = _loss_grad("loss_grad", xin, target)
    raw = {k: [None] * DEPTH for k in ("dg1", "dqg", "dkg", "dw0", "dw1", "dw2", "dpw", "dps", "dg2")}
    db_all = lax.empty((DEPTH, 4, KB, 128), F32)
    for l in reversed(range(DEPTH)):
        sv = saved[l]
        da = _mm_nt_relu(f"mlp2_bwd_{l}", dxb, w_2, l, sv["a"])
        g_2 = _mm_tn(f"mlp2_wgrad_{l}", sv["a"], dxb, 512, 1024, relu2=True)
        g_1 = _mm_tn(f"mlp1_wgrad_{l}", sv["h2"], da, 1024, 512)
        dep = on_grads(l, (2, 3), (g_1, g_2))
        dx1, dx1b, dg2 = _mm_nt_normbwd(f"mlp1_bwd_{l}", da, w_1, l, sv["x1"], small["norm2_g"][l:l + 1], dx, dep)
        do, dot, dmix, dl = _proj_out_bwd(f"proj_out_bwd_{l}", dx1b, w_out, l, sv["mix"])
        g_out = _mm_tn(f"proj_out_wgrad_{l}", sv["mix"], dx1b, 512, 1024)
        dcp, dw0, dw1, dw2, dps, dpw = _convpool_bwd(f"convpool_bwd_{l}", sv["p"], dmix, sv["cw"], sv["pwbd"], sv["ps"])
        dq, dkp, dvp, db_all = _attn_bwd(f"attn_bwd_{l}", sv["q"], sv["qt"], sv["kp"], sv["kt"], sv["vp"], bias_all, l,
                                     do, dot, sv["lse"], _rowsum_layout(dl, x.shape[0] // UNIT), db_all)
        dp, dqg, dkg = _qkv_bwd(f"qkv_bwd_{l}", sv["p"], dq, dkp, dvp, dcp, sv["qg"], sv["kg"])
        g_in = _mm_tn(f"proj_in_wgrad_{l}", sv["h"], dp, 1024, 1280)
        dep = on_grads(l, (0, 1), (g_in, g_out))
        dx, dxb, dg1 = _mm_nt_normbwd(f"proj_in_bwd_{l}", dp, w_in, l, sv["xin"], small["norm1_g"][l:l + 1], dx1, dep)
        for k, val in dict(dg1=dg1, dqg=dqg, dkg=dkg, dw0=dw0, dw1=dw1, dw2=dw2, dpw=dpw, dps=dps, dg2=dg2).items():
            raw[k][l] = val
    cat = {k: jnp.concatenate(v, axis=0) for k, v in raw.items() if k != "dpw"}
    drb = _bias_reduce("bias_reduce", _bias_unlayout(db_all))
    dpw = jnp.stack(raw["dpw"])
    gsmall = {
        "norm1_g": cat["dg1"], "q_norm_g": cat["dqg"][:, :HD], "k_norm_g": cat["dkg"][:, :HD],
        "rel_bias": drb[:, :, :257],
        "conv_w": jnp.stack([cat["dw0"], cat["dw1"], cat["dw2"]], axis=1),
        "pool_w": jnp.stack([dpw[:, g * 64:(g + 1) * 64, g * 64:(g + 1) * 64] for g in range(4)], axis=1),
        "pool_scale": cat["dps"], "norm2_g": cat["dg2"],
    }
    return loss, dx, gsmall


SMALL = ("norm1_g", "q_norm_g", "k_norm_g", "rel_bias", "conv_w", "pool_w", "pool_scale", "norm2_g")
LARGE = ("w_in", "w_out", "w_mlp1", "w_mlp2")


def kernel(x, norm1_g, w_in, q_norm_g, k_norm_g, rel_bias, conv_w, pool_w, pool_scale, w_out, norm2_g, w_mlp1, w_mlp2, loss_target, m_norm1_g, m_w_in, m_q_norm_g, m_k_norm_g, m_rel_bias, m_conv_w, m_pool_w, m_pool_scale, m_w_out, m_norm2_g, m_w_mlp1, m_w_mlp2, v_norm1_g, v_w_in, v_q_norm_g, v_k_norm_g, v_rel_bias, v_conv_w, v_pool_w, v_pool_scale, v_w_out, v_norm2_g, v_w_mlp1, v_w_mlp2):
    w = dict(norm1_g=norm1_g, w_in=w_in, q_norm_g=q_norm_g, k_norm_g=k_norm_g, rel_bias=rel_bias, conv_w=conv_w,
             pool_w=pool_w, pool_scale=pool_scale, w_out=w_out, norm2_g=norm2_g, w_mlp1=w_mlp1, w_mlp2=w_mlp2)
    m = dict(norm1_g=m_norm1_g, w_in=m_w_in, q_norm_g=m_q_norm_g, k_norm_g=m_k_norm_g, rel_bias=m_rel_bias,
             conv_w=m_conv_w, pool_w=m_pool_w, pool_scale=m_pool_scale, w_out=m_w_out, norm2_g=m_norm2_g,
             w_mlp1=m_w_mlp1, w_mlp2=m_w_mlp2)
    v = dict(norm1_g=v_norm1_g, w_in=v_w_in, q_norm_g=v_q_norm_g, k_norm_g=v_k_norm_g, rel_bias=v_rel_bias,
             conv_w=v_conv_w, pool_w=v_pool_w, pool_scale=v_pool_scale, w_out=v_w_out, norm2_g=v_norm2_g,
             w_mlp1=v_w_mlp1, w_mlp2=v_w_mlp2)
    ax, ay, ac = lax.axis_index("x"), lax.axis_index("y"), lax.axis_index("c")
    b1 = jnp.reshape(2 * ax + ay, (1,)).astype(jnp.int32)

    cw_rows = _all_gather8("gather_conv_w", jnp.pad(conv_w.reshape(DEPTH * 3, 64), ((0, 4), (0, 64))), b1)
    cw_chips = [cw_rows[(4 * cx + 2 * cy) * 16:(4 * cx + 2 * cy) * 16 + 12, :64] for cx in range(2) for cy in range(2)]
    small = {n: w[n] for n in SMALL}
    small["conv_w"] = jnp.concatenate(cw_chips, axis=1).reshape(DEPTH, 3, CW)

    (w_in_full,), in_sems, in_token = _gather_start(
        "gather_start_in", (0,), (0,), [_cast_into_full("cast_w_in", 0, w["w_in"], b1, cw_rows)])
    others, first_sems, first_token = _gather_start(
        "gather_start_first", (0,), (1, 2, 3),
        [_cast_into_full(f"cast_{LARGE[t]}", t, w[LARGE[t]], b1, in_token) for t in (1, 2, 3)])
    held = [[w_in_full] + list(others)]
    sems = {(0, 0): in_sems[0], (0, 1): first_sems[0]}

    def layer_weights(l, ts, after):
        if l > 0:
            ts = (0, 1, 2, 3) if ts == (0,) else ()
        if ts:
            tag = f"{l}_{ts[0]}"
            first_in = l == 0 and ts == (0,)
            after = first_token if first_in else after
            arrived = _gather_wait(f"gather_wait_{tag}", l, ts, held[0], sems[l, ts[0] if l == 0 else 0], after)
            if first_in:
                arrived, rest_sems, _ = _gather_start("gather_start_rest", tuple(range(1, DEPTH)), (0, 1, 2, 3),
                                                      arrived)
                sems.update({(k, 0): v for k, v in rest_sems.items()})
            held[0] = _pass_on(f"pass_on_{tag}", l, ts, arrived)
        return held[0]

    flights = {}

    def await_flight(l, ts, afters):
        g, landing, sm, _ = flights[l, ts]
        flights[l, ts] = _reduce_wait(f"reduce_wait_{l}_{ts[0]}", ts, g, landing, sm, afters)

    def on_grads(l, ts, grads):
        if ts == (0, 1) and l + 1 < DEPTH:
            await_flight(l + 1, (2, 3), [grads[0]])
            await_flight(l + 1, (0, 1), [grads[0]])
        flights[l, ts] = _reduce_start(f"reduce_start_{l}_{ts[0]}", ts, grads)
        return flights[l, ts][3]

    loss_part, grad_x, gsmall = _local_step(x[0], loss_target[0], layer_weights, on_grads, small)
    loss = lax.psum(loss_part[0, 0], ("x", "y", "c"))
    order = [n for n in SMALL]
    packed = _pack([gsmall[n] for n in order])

    out = {n: [lax.empty(w[n].shape, F32) for _ in range(4)] for n in LARGE}
    for l in reversed(range(DEPTH)):
        if l == 0:
            afters = [grad_x, packed] + [out[n][0] for n in LARGE]
            await_flight(0, (2, 3), afters)
            await_flight(0, (0, 1), afters)
        sums = [None] * 4
        for ts in ((0, 1), (2, 3)):
            g, landing = flights[l, ts]
            for i, t in enumerate(ts):
                sums[t] = _add4(f"add4_{LARGE[t]}_{l}", t, g[i], landing[i], b1)
        theirs = _swap_sib(f"swap_sib_{l}", sums)
        for t, n in enumerate(LARGE):
            out[n] = _adamw_pair(f"adamw_{n}_{l}", l, sums[t], theirs[t], w[n], m[n], v[n], out[n])

    rows = packed.shape[0]
    summed = _sum8("sum_small", _all_gather8("gather_small", packed, out[LARGE[0]][0]).reshape(8, rows, 128))
    gfull = dict(zip(order, _unpack(summed, [gsmall[n].shape for n in order])))
    gfull["conv_w"] = lax.dynamic_slice_in_dim(gfull["conv_w"], (2 * ax + ay) * 64, 64, axis=2)
    res = _adamw("adamw_small", _pack([gfull[n] for n in order]), _pack([w[n] for n in order]),
                 _pack([m[n] for n in order]), _pack([v[n] for n in order]))
    for n, parts in zip(order, zip(*[_unpack(r, [w[k].shape for k in order]) for r in res])):
        out[n] = list(parts)

    names = ("norm1_g", "w_in", "q_norm_g", "k_norm_g", "rel_bias", "conv_w", "pool_w", "pool_scale", "w_out",
             "norm2_g", "w_mlp1", "w_mlp2")
    flat = [loss, grad_x[None]]
    for i in range(4):
        flat += [out[n][i] for n in names]
    return tuple(flat)
```

```python
import functools

import jax
import jax.numpy as jnp
from jax import lax
from jax.experimental import pallas as pl
from jax.experimental.pallas import tpu as pltpu

F32 = jnp.float32
BF16 = jnp.bfloat16

D = 1024
DEPTH = 4
CH = 64
NPREV = 8
KB = (NPREV + 1) * CH
PADR = NPREV * CH
HD = 64
AW = 512
CW = 256
PWD = 256
DIN = 3 * AW + 3 * CW + PWD
DFF = 4 * D
NIDX = 384
EPS = 1e-6
NEG_INF = -1e30

ADAM_LR = 0.001
ADAM_B1 = 0.9
ADAM_B2 = 0.999
ADAM_EPS = 1e-08
ADAM_WD = 0.01
ADAM_STEP = 10

VMEM_LIMIT = 52 * 1024 * 1024
MM_ROWS = 512


def _mm_rows(k, n):
    return 2 * MM_ROWS if k + n <= 2048 else MM_ROWS


MESH = pl.DeviceIdType.MESH
ANY = pl.BlockSpec(memory_space=pl.ANY)


def _cp(*sem):
    return pltpu.CompilerParams(dimension_semantics=sem, vmem_limit_bytes=VMEM_LIMIT)


def _inv_rms(x):
    return lax.rsqrt(jnp.mean(x * x, axis=-1, keepdims=True) + EPS)


def _head_mean_matrix():
    r = lax.broadcasted_iota(jnp.int32, (AW, AW), 0) // HD
    c = lax.broadcasted_iota(jnp.int32, (AW, AW), 1) // HD
    return jnp.where(r == c, 1.0 / HD, 0.0).astype(BF16)


def _two_pass_dot(x, m):
    hi = x.astype(BF16)
    lo = (x - hi.astype(F32)).astype(BF16)
    return (jnp.dot(hi, m, preferred_element_type=F32)
            + jnp.dot(lo, m, preferred_element_type=F32))


def _head_mean(x, hm):
    return _two_pass_dot(x, hm)


def _rmsnorm(name, x, g):
    s = x.shape[0]
    tm = 512

    def body(x_ref, g_ref, h_ref):
        xv = x_ref[...]
        h_ref[...] = (xv * _inv_rms(xv) * g_ref[...]).astype(BF16)

    return pl.pallas_call(
        body, name=name, grid=(s // tm,),
        in_specs=[pl.BlockSpec((tm, D), lambda i: (i, 0)), pl.BlockSpec((1, D), lambda i: (0, 0))],
        out_specs=pl.BlockSpec((tm, D), lambda i: (i, 0)),
        out_shape=jax.ShapeDtypeStruct((s, D), BF16),
        compiler_params=_cp("parallel"),
    )(x, g)


def _relu2(a):
    r = jnp.maximum(a, jnp.zeros_like(a))
    return r * r


def _mm_nn(name, a, w, l, out_dtype):
    s, k = a.shape
    n = w.shape[2]
    tm = _mm_rows(k, n)

    def body(a_ref, w_ref, o_ref):
        o_ref[...] = jnp.dot(a_ref[...], w_ref[...], preferred_element_type=F32).astype(o_ref.dtype)

    return pl.pallas_call(
        body, name=name, grid=(s // tm,),
        in_specs=[pl.BlockSpec((tm, k), lambda i: (i, 0)),
                  pl.BlockSpec((None, k, n), lambda i: (l, 0, 0))],
        out_specs=pl.BlockSpec((tm, n), lambda i: (i, 0)),
        out_shape=jax.ShapeDtypeStruct((s, n), out_dtype),
        compiler_params=_cp("parallel"),
    )(a, w)


def _mm_res_norm(name, a, w, l, res, g):
    s, k = a.shape
    tm = _mm_rows(k, D)

    def body(a_ref, w_ref, r_ref, g_ref, x_ref, h_ref):
        acc = r_ref[...] + jnp.dot(a_ref[...], w_ref[...], preferred_element_type=F32)
        x_ref[...] = acc
        h_ref[...] = (acc * _inv_rms(acc) * g_ref[...]).astype(BF16)

    return pl.pallas_call(
        body, name=name, grid=(s // tm,),
        in_specs=[pl.BlockSpec((tm, k), lambda i: (i, 0)),
                  pl.BlockSpec((None, k, D), lambda i: (l, 0, 0)),
                  pl.BlockSpec((tm, D), lambda i: (i, 0)),
                  pl.BlockSpec((1, D), lambda i: (0, 0))],
        out_specs=[pl.BlockSpec((tm, D), lambda i: (i, 0))] * 2,
        out_shape=[jax.ShapeDtypeStruct((s, D), F32), jax.ShapeDtypeStruct((s, D), BF16)],
        compiler_params=_cp("parallel"),
    )(a, w, res, g)


def _mlp_fwd(name, h2, w1, w2, l, res, g):
    s = h2.shape[0]
    tm = 256

    def body(h_ref, w1_ref, w2_ref, r_ref, g_ref, a_ref, x_ref, hn_ref):
        a = jnp.dot(h_ref[...], w1_ref[...], preferred_element_type=F32).astype(BF16)
        a_ref[...] = a
        acc = r_ref[...] + jnp.dot(_relu2(a), w2_ref[...], preferred_element_type=F32)
        x_ref[...] = acc
        hn_ref[...] = (acc * _inv_rms(acc) * g_ref[...]).astype(BF16)

    once = pl.Buffered(1)
    rows = pl.BlockSpec((tm, D), lambda i: (i, 0))
    return pl.pallas_call(
        body, name=name, grid=(s // tm,),
        in_specs=[rows,
                  pl.BlockSpec((None, D, DFF), lambda i: (l, 0, 0), pipeline_mode=once),
                  pl.BlockSpec((None, DFF, D), lambda i: (l, 0, 0), pipeline_mode=once),
                  rows, pl.BlockSpec((1, D), lambda i: (0, 0))],
        out_specs=[pl.BlockSpec((tm, DFF), lambda i: (i, 0)), rows, rows],
        out_shape=[jax.ShapeDtypeStruct((s, DFF), BF16), jax.ShapeDtypeStruct((s, D), F32),
                   jax.ShapeDtypeStruct((s, D), BF16)],
        compiler_params=_cp("parallel"),
    )(h2, w1, w2, res, g)


def _qkv(name, p, qg, kg):
    s = p.shape[0]
    tm = PADR
    nb = s // tm

    def body(pq_ref, pk_ref, pv_ref, qg_ref, kg_ref, q_ref, qt_ref, k_ref, kt_ref, v_ref, vt_ref):
        t = pl.program_id(0)
        hm = _head_mean_matrix()

        def nrm(x, g):
            return x * lax.rsqrt(_head_mean(x * x, hm) + EPS) * g

        first = t == 0
        qq = nrm(pq_ref[...], qg_ref[...]) * 0.125
        kk = jnp.where(first, 0.0, nrm(pk_ref[...], kg_ref[...]))
        vv = jnp.where(first, 0.0, pv_ref[...])
        q_ref[...] = qq.astype(BF16)
        qt_ref[...] = qq.T.astype(BF16)
        k_ref[...] = kk.astype(BF16)
        kt_ref[...] = kk.T.astype(BF16)
        v_ref[...] = vv.astype(BF16)
        vt_ref[...] = vv.T.astype(BF16)

    def src(col):
        return pl.BlockSpec((tm, AW), lambda t: (jnp.maximum(t - 1, 0), col))

    gspec = pl.BlockSpec((1, AW), lambda t: (0, 0))
    rows = pl.BlockSpec((tm, AW), lambda t: (t, 0))
    cols = pl.BlockSpec((AW, tm), lambda t: (0, t))
    return pl.pallas_call(
        body, name=name, grid=(nb + 1,),
        in_specs=[src(0), src(1), src(2), gspec, gspec],
        out_specs=[pl.BlockSpec((tm, AW), lambda t: (jnp.maximum(t - 1, 0), 0)),
                   pl.BlockSpec((AW, tm), lambda t: (0, jnp.maximum(t - 1, 0))),
                   rows, cols, rows, cols],
        out_shape=[jax.ShapeDtypeStruct((s, AW), BF16), jax.ShapeDtypeStruct((AW, s), BF16),
                   jax.ShapeDtypeStruct((s + PADR, AW), BF16), jax.ShapeDtypeStruct((AW, s + PADR), BF16),
                   jax.ShapeDtypeStruct((s + PADR, AW), BF16), jax.ShapeDtypeStruct((AW, s + PADR), BF16)],
        compiler_params=_cp("arbitrary"),
    )(p, p, p, qg, kg)


NBAND = KB // CH
HIGHEST = lax.Precision.HIGHEST
NT_DIMS = (((1,), (1,)), ((), ()))


def _onehot_table(a):
    m = lax.broadcasted_iota(jnp.int32, (128, NIDX), 0)
    idx = lax.broadcasted_iota(jnp.int32, (128, NIDX), 1)
    rel = jnp.clip(KB - 1 - (CH * a + m), -128, 128) + 128
    return jnp.where(rel == idx, 1.0, 0.0).astype(F32)


def _onehot_diagonal():
    r = lax.broadcasted_iota(jnp.int32, (CH * CH, 128), 0)
    m = lax.broadcasted_iota(jnp.int32, (CH * CH, 128), 1)
    return jnp.where((r % CH) - (r // CH) + (CH - 1) == m, 1.0, 0.0).astype(F32)


def _bias_expand(name, rb):
    def body(rb_ref, o_ref):
        along = [lax.dot_general(rb_ref[...], _onehot_table(a), NT_DIMS, preferred_element_type=F32,
                                 precision=HIGHEST) for a in range(NBAND)]
        o_ref[...] = lax.dot_general(jnp.concatenate(along, axis=0), _onehot_diagonal(), NT_DIMS,
                                     preferred_element_type=F32, precision=HIGHEST)

    return pl.pallas_call(
        body, name=name, grid=(DEPTH,),
        in_specs=[pl.BlockSpec((None, 8, NIDX), lambda l: (l, 0, 0))],
        out_specs=pl.BlockSpec((None, NBAND * 8, CH * CH), lambda l: (l, 0, 0)),
        out_shape=jax.ShapeDtypeStruct((DEPTH, NBAND * 8, CH * CH), F32),
        compiler_params=_cp("parallel"),
    )(rb)


def _bias_reduce(name, db):
    def body(db_ref, o_ref):
        along = jnp.dot(db_ref[...], _onehot_diagonal(), preferred_element_type=F32, precision=HIGHEST)
        acc = jnp.zeros((8, NIDX), F32)
        for a in range(NBAND):
            acc = acc + jnp.dot(along[8 * a:8 * a + 8, :], _onehot_table(a), preferred_element_type=F32,
                                precision=HIGHEST)
        o_ref[...] = acc

    return pl.pallas_call(
        body, name=name, grid=(DEPTH,),
        in_specs=[pl.BlockSpec((None, NBAND * 8, CH * CH), lambda l: (l, 0, 0))],
        out_specs=pl.BlockSpec((None, 8, NIDX), lambda l: (l, 0, 0)),
        out_shape=jax.ShapeDtypeStruct((DEPTH, 8, NIDX), F32),
        compiler_params=_cp("parallel"),
    )(db)


def _bias_layout(flat):
    b = flat.reshape(DEPTH, NBAND, 8, CH, CH).transpose(0, 2, 1, 4, 3).reshape(DEPTH, 4, 2, KB, CH)
    pair = b.transpose(0, 1, 3, 2, 4).reshape(DEPTH, 4, KB, 128)
    first = jnp.pad(pair, ((0, 0), (0, 0), (0, CH), (0, 0)), constant_values=NEG_INF)
    second = jnp.pad(pair, ((0, 0), (0, 0), (CH, 0), (0, 0)), constant_values=NEG_INF)
    return jnp.concatenate([first, second], axis=3)


def _bias_unlayout(dbt):
    b = dbt.reshape(DEPTH, 4, NBAND, CH, 2, CH)
    return b.transpose(0, 2, 1, 4, 5, 3).reshape(DEPTH, NBAND * 8, CH * CH)


UNIT = 2 * CH
BAND2 = KB + CH


def _pair_weights(xt):
    x = xt.astype(F32)
    row = lax.broadcasted_iota(jnp.int32, (128, UNIT), 0)
    low = lax.broadcasted_iota(jnp.int32, (128, UNIT), 1) < HD
    swapped = pltpu.roll(x, HD, 1)
    same = (row < HD) == low
    first = jnp.where(same, jnp.where(low, x, swapped), 0.0)
    second = jnp.where(same, jnp.where(low, swapped, x), 0.0)
    return jnp.concatenate([first, second], axis=1).astype(BF16)


def _pair_rows(x):
    low = lax.broadcasted_iota(jnp.int32, (CH, 128), 1) < HD
    zero = jnp.zeros((CH, 128), x.dtype)
    parts = []
    for c in range(2):
        xc = x[c * CH:(c + 1) * CH, :]
        parts += [jnp.where(low, xc, zero), jnp.where(low, zero, xc)]
    return jnp.concatenate(parts, axis=0)


def _unpair(raw):
    b0, b1 = raw[:, 0:128], raw[:, 128:256]
    row = lax.broadcasted_iota(jnp.int32, (128, 128), 0)
    low = lax.broadcasted_iota(jnp.int32, (128, 128), 1) < HD
    top = jnp.where(low, b0, pltpu.roll(b1, HD, 1))
    bottom = jnp.where(low, pltpu.roll(b0, HD, 1), b1)
    return jnp.where(row < HD, top, bottom).T


def _scores_t(kb, qw, bias2, row0, padded):
    s = jnp.dot(kb, qw, preferred_element_type=F32) + bias2
    if padded:
        s = jnp.where(row0 + lax.broadcasted_iota(jnp.int32, (BAND2, 256), 0) >= PADR, s, NEG_INF)
    return s


def _unit_loops(s, unit):
    lax.fori_loop(0, PADR // UNIT, lambda u, c: unit(u, True, c), 0, unroll=4)
    lax.fori_loop(PADR // UNIT, s // UNIT, lambda u, c: unit(u, False, c), 0, unroll=7)


def _attn_fwd(name, kp, qt, vt, bias2, l):
    s = qt.shape[1]
    nu = s // UNIT

    def body(k_ref, qt_ref, vt_ref, b_ref, o_ref, lse_ref):
        def unit(u, padded, carry):
            r0 = pl.multiple_of(u * UNIT, UNIT)
            sc = _scores_t(k_ref[pl.ds(r0, BAND2), :], _pair_weights(qt_ref[:, pl.ds(r0, UNIT)]), b_ref[...],
                           r0, padded)
            top = jnp.max(sc, axis=0, keepdims=True)
            e = jnp.exp(sc - top)
            total = jnp.sum(e, axis=0, keepdims=True)
            raw = jnp.dot(vt_ref[:, pl.ds(r0, BAND2)], e.astype(BF16), preferred_element_type=F32) * (1.0 / total)
            o_ref[pl.ds(r0, UNIT), :] = _unpair(raw).astype(BF16)
            lse_ref[u] = jnp.broadcast_to(top + jnp.log(total), (8, 256))
            return carry

        _unit_loops(s, unit)

    return pl.pallas_call(
        body, name=name, grid=(AW // 128,),
        in_specs=[pl.BlockSpec((s + PADR, 128), lambda h: (0, h)),
                  pl.BlockSpec((128, s), lambda h: (h, 0)),
                  pl.BlockSpec((128, s + PADR), lambda h: (h, 0)),
                  pl.BlockSpec((None, None, BAND2, 256), lambda h: (l, h, 0, 0))],
        out_specs=[pl.BlockSpec((s, 128), lambda h: (0, h)),
                   pl.BlockSpec((None, nu, 8, 256), lambda h: (h, 0, 0, 0))],
        out_shape=[jax.ShapeDtypeStruct((s, AW), BF16), jax.ShapeDtypeStruct((4, nu, 8, 256), F32)],
        compiler_params=_cp("parallel"),
    )(kp, qt, vt, bias2)


def _attn_bwd(name, q, qt, kp, kt, vp, bias2, l, do, dot, lse, dl, db_all):
    s = q.shape[0]
    nu = s // UNIT

    def body(q_ref, qt_ref, k_ref, kt_ref, v_ref, b_ref, do_ref, dot_ref, lse_ref, dl_ref, dbin_ref,
             dq_ref, dk_ref, dv_ref, db_ref):
        del dbin_ref
        dk_ref[...] = jnp.zeros_like(dk_ref)
        dv_ref[...] = jnp.zeros_like(dv_ref)
        db_ref[...] = jnp.zeros_like(db_ref)

        def unit(u, padded, carry):
            r0 = pl.multiple_of(u * UNIT, UNIT)
            rows, band = pl.ds(r0, UNIT), pl.ds(r0, BAND2)
            sc = _scores_t(k_ref[band, :], _pair_weights(qt_ref[:, rows]), b_ref[...], r0, padded)
            pt = jnp.exp(sc - lse_ref[u][0:1, :])
            dpt = jnp.dot(v_ref[band, :], _pair_weights(dot_ref[:, rows]), preferred_element_type=F32)
            ds = pt * (dpt - dl_ref[u][0:1, :])
            db_ref[...] += ds[0:KB, 0:128] + ds[CH:BAND2, 128:256]
            dsb = ds.astype(BF16)
            dq_ref[rows, :] = _unpair(jnp.dot(kt_ref[:, band], dsb, preferred_element_type=F32))
            dk_ref[band, :] += jnp.dot(dsb, _pair_rows(q_ref[rows, :]), preferred_element_type=F32)
            dv_ref[band, :] += jnp.dot(pt.astype(BF16), _pair_rows(do_ref[rows, :]), preferred_element_type=F32)
            return carry

        _unit_loops(s, unit)

    row_q = pl.BlockSpec((s, 128), lambda h: (0, h))
    col_q = pl.BlockSpec((128, s), lambda h: (h, 0))
    row_k = pl.BlockSpec((s + PADR, 128), lambda h: (0, h))
    col_k = pl.BlockSpec((128, s + PADR), lambda h: (h, 0))
    stat = pl.BlockSpec((None, nu, 8, 256), lambda h: (h, 0, 0, 0))
    return pl.pallas_call(
        body, name=name, grid=(AW // 128,),
        in_specs=[row_q, col_q, row_k, col_k, row_k,
                  pl.BlockSpec((None, None, BAND2, 256), lambda h: (l, h, 0, 0)), row_q, col_q, stat, stat, ANY],
        out_specs=[row_q, row_k, row_k, pl.BlockSpec((None, None, KB, 128), lambda h: (l, h, 0, 0))],
        out_shape=[jax.ShapeDtypeStruct((s, AW), F32),
                   jax.ShapeDtypeStruct((s + PADR, AW), F32),
                   jax.ShapeDtypeStruct((s + PADR, AW), F32),
                   jax.ShapeDtypeStruct((DEPTH, 4, KB, 128), F32)],
        input_output_aliases={10: 3},
        compiler_params=_cp("parallel"),
    )(q, qt, kp, kt, vp, bias2, do, dot, lse, dl, db_all)


def _rowsum_layout(dl, nu):
    d = dl[:, :8].reshape(nu, 2, CH, 4, 2)
    d = d.transpose(3, 0, 1, 4, 2).reshape(4, nu, 1, 256)
    return jnp.broadcast_to(d, (4, nu, 8, 256))


def _rows_before(cur, prev, k):
    row = lax.broadcasted_iota(jnp.int32, cur.shape, 0)
    return jnp.where(row >= k, pltpu.roll(cur, k, 0), pltpu.roll(prev, k, 0))


def _rows_after(cur, nxt, k):
    n = cur.shape[0]
    row = lax.broadcasted_iota(jnp.int32, cur.shape, 0)
    return jnp.where(row < n - k, pltpu.roll(cur, n - k, 0), pltpu.roll(nxt, n - k, 0))


def _pool_window_lanes():
    lg = lax.broadcasted_iota(jnp.int32, (1, PWD), 1) // 64
    return lg, jnp.where(lg == 0, 2.0, jnp.where(lg == 1, 4.0, jnp.where(lg == 2, 8.0, 16.0))).astype(F32)


def _pool_mean_minus_token(u, up, row0):
    lg, wv = _pool_window_lanes()
    sums = []
    c, p = u, up
    for k in (1, 2, 4, 8):
        c2 = c + _rows_before(c, p, k)
        p = p + pltpu.roll(p, k, 0)
        c = c2
        sums.append(c)
    win = jnp.where(lg == 0, sums[0], jnp.where(lg == 1, sums[1], jnp.where(lg == 2, sums[2], sums[3])))
    pos1 = (row0 + lax.broadcasted_iota(jnp.int32, u.shape, 0) + 1).astype(F32)
    cnt = jnp.minimum(pos1, wv)
    return win / cnt - u, cnt


def _conv_taps(z, zp, w0, w1, w2):
    z1 = _rows_before(z, zp, 1)
    z2 = _rows_before(z, zp, 2)
    return (w0 * z2 + w1 * z1) + w2 * z, z1, z2


CP_TM = 1024
HALO = 16


def _halo_before(tm, col):
    return pl.BlockSpec((HALO, CW), lambda i: (jnp.maximum(i * (tm // HALO) - 1, 0), col))


def _halo_after(tm, col, rows):
    return pl.BlockSpec((HALO, CW), lambda i: (jnp.minimum((i + 1) * (tm // HALO), rows // HALO - 1), col))


def _as_block_end(halo, tm):
    return jnp.concatenate([jnp.zeros((tm - HALO, halo.shape[1]), halo.dtype), halo], axis=0)


def _as_block_start(halo, tm):
    return jnp.concatenate([halo, jnp.zeros((tm - HALO, halo.shape[1]), halo.dtype)], axis=0)


def _convpool_fwd(name, p, o, cw, pwbd, ps):
    s = p.shape[0]
    tm = CP_TM
    nb = s // tm

    def body(gb_ref, gc_ref, hin_ref, u_ref, gcp_ref, hinp_ref, up_ref, o_ref, cw_ref, pw_ref, ps_ref, mix_ref):
        i = pl.program_id(0)
        has_prev = i > 0
        z = gc_ref[...] * hin_ref[...]
        zp = _as_block_end(jnp.where(has_prev, gcp_ref[...] * hinp_ref[...], 0.0), tm)
        y3, _, _ = _conv_taps(z, zp, cw_ref[0:1, :], cw_ref[1:2, :], cw_ref[2:3, :])
        m, _ = _pool_mean_minus_token(u_ref[...], _as_block_end(jnp.where(has_prev, up_ref[...], 0.0), tm), i * tm)
        yp = jnp.dot(m.astype(BF16), pw_ref[...].astype(BF16), preferred_element_type=F32) * ps_ref[...]
        mix_ref[:, 0:AW] = o_ref[...]
        mix_ref[:, AW:AW + CW] = (gb_ref[...] * y3).astype(BF16)
        mix_ref[:, AW + CW:D] = yp.astype(BF16)

    def cur(col):
        return pl.BlockSpec((tm, CW), lambda i: (i, col))

    def whole(a):
        return pl.BlockSpec(a.shape, lambda i: (0,) * a.ndim)

    return pl.pallas_call(
        body, name=name, grid=(nb,),
        in_specs=[cur(6), cur(7), cur(8), cur(9), _halo_before(tm, 7), _halo_before(tm, 8), _halo_before(tm, 9),
                  pl.BlockSpec((tm, AW), lambda i: (i, 0)), whole(cw), whole(pwbd), whole(ps)],
        out_specs=pl.BlockSpec((tm, D), lambda i: (i, 0)),
        out_shape=jax.ShapeDtypeStruct((s, D), BF16),
        compiler_params=_cp("parallel"),
    )(p, p, p, p, p, p, p, o, cw, pwbd, ps)


def _convpool_bwd(name, p, dmix, cw, pwbd, ps):
    s = p.shape[0]
    tm = CP_TM
    nb = s // tm

    def body(gb_ref, gc_ref, hin_ref, u_ref, gcp_ref, hinp_ref, up_ref, gbn_ref, dyc_ref, dyp_ref, dycn_ref, dypn_ref,
             cw_ref, pw_ref, ps_ref, dcp_ref, dw0_ref, dw1_ref, dw2_ref, dps_ref, dpw_ref):
        i = pl.program_id(0)
        has_prev = i > 0
        has_next = i < nb - 1
        w0, w1, w2 = cw_ref[0:1, :], cw_ref[1:2, :], cw_ref[2:3, :]
        gb, gc, hin = gb_ref[...], gc_ref[...], hin_ref[...]
        dyc = dyc_ref[...]
        z = gc * hin
        zp = _as_block_end(jnp.where(has_prev, gcp_ref[...] * hinp_ref[...], 0.0), tm)
        y3, z1, z2 = _conv_taps(z, zp, w0, w1, w2)
        dy3 = dyc * gb
        dy3n = _as_block_start(jnp.where(has_next, dycn_ref[...] * gbn_ref[...], 0.0), tm)
        dz = w2 * dy3 + w1 * _rows_after(dy3, dy3n, 1) + w0 * _rows_after(dy3, dy3n, 2)
        pw = pw_ref[...].astype(BF16)
        psv = ps_ref[...]
        m, cnt = _pool_mean_minus_token(u_ref[...], _as_block_end(jnp.where(has_prev, up_ref[...], 0.0), tm), i * tm)
        mb = m.astype(BF16)
        dyp = dyp_ref[...]
        dmp = (dyp * psv).astype(BF16)
        dmpn = jnp.where(has_next, dypn_ref[...] * psv, 0.0).astype(BF16)
        nt = (((1,), (1,)), ((), ()))
        dm = lax.dot_general(dmp, pw, nt, preferred_element_type=F32)
        dmn = lax.dot_general(dmpn, pw, nt, preferred_element_type=F32)
        lg, wv = _pool_window_lanes()
        cc, cn = dm / cnt, _as_block_start(dmn / wv, tm)
        sums = []
        for k in (1, 2, 4, 8):
            c2 = cc + _rows_after(cc, cn, k)
            cn = cn + pltpu.roll(cn, tm - k, 0)
            cc = c2
            sums.append(cc)
        du = jnp.where(lg == 0, sums[0], jnp.where(lg == 1, sums[1], jnp.where(lg == 2, sums[2], sums[3]))) - dm
        dcp_ref[:, 0:CW] = (dyc * y3).astype(BF16)
        dcp_ref[:, CW:2 * CW] = (dz * hin).astype(BF16)
        dcp_ref[:, 2 * CW:3 * CW] = (dz * gc).astype(BF16)
        dcp_ref[:, 3 * CW:4 * CW] = du.astype(BF16)
        parts = (jnp.sum(dy3 * z2, axis=0, keepdims=True),
                 jnp.sum(dy3 * z1, axis=0, keepdims=True),
                 jnp.sum(dy3 * z, axis=0, keepdims=True),
                 jnp.sum(dyp * jnp.dot(mb, pw, preferred_element_type=F32), axis=0, keepdims=True),
                 lax.dot_general(mb, dmp, (((0,), (0,)), ((), ())), preferred_element_type=F32))
        accs = (dw0_ref, dw1_ref, dw2_ref, dps_ref, dpw_ref)

        @pl.when(i == 0)
        def _():
            for a, v in zip(accs, parts):
                a[...] = v

        @pl.when(i > 0)
        def _():
            for a, v in zip(accs, parts):
                a[...] += v

    def cur(col):
        return pl.BlockSpec((tm, CW), lambda i: (i, col))

    def prev(col):
        return _halo_before(tm, col)

    def nxt(col):
        return _halo_after(tm, col, s)

    def whole(shape):
        return pl.BlockSpec(shape, lambda i: (0,) * len(shape))

    row = jax.ShapeDtypeStruct((1, CW), F32)
    return pl.pallas_call(
        body, name=name, grid=(nb,),
        in_specs=[cur(6), cur(7), cur(8), cur(9), prev(7), prev(8), prev(9), nxt(6),
                  cur(0), cur(1), nxt(0), nxt(1), whole(cw.shape), whole(pwbd.shape), whole(ps.shape)],
        out_specs=[pl.BlockSpec((tm, D), lambda i: (i, 0)), whole((1, CW)), whole((1, CW)), whole((1, CW)),
                   whole((1, PWD)), whole((PWD, PWD))],
        out_shape=[jax.ShapeDtypeStruct((s, D), BF16), row, row, row, row,
                   jax.ShapeDtypeStruct((PWD, PWD), F32)],
        compiler_params=_cp("arbitrary"),
    )(p, p, p, p, p, p, p, p, dmix, dmix, dmix, dmix, cw, pwbd, ps)


def _qkv_bwd(name, p, dq, dkp, dvp, dcp, qg, kg):
    s = p.shape[0]
    tm = 512
    off = PADR // tm

    def body(pq_ref, pk_ref, dq_ref, dk_ref, dv_ref, dcp_ref, qg_ref, kg_ref, dp_ref, dqg_ref, dkg_ref):
        i = pl.program_id(0)
        hm = _head_mean_matrix()

        def nrm_bwd(x, g, dy):
            r = lax.rsqrt(_head_mean(x * x, hm) + EPS)
            xn = x * r
            dxn = dy * g
            dx = r * (dxn - xn * _head_mean(dxn * xn, hm))
            dg = jnp.sum(dy * xn, axis=0, keepdims=True)
            dg = (dg[:, 0:128] + dg[:, 128:256]) + (dg[:, 256:384] + dg[:, 384:512])
            return dx, dg + pltpu.roll(dg, HD, 1)

        dxq, dgq = nrm_bwd(pq_ref[...], qg_ref[...], dq_ref[...] * 0.125)
        dxk, dgk = nrm_bwd(pk_ref[...], kg_ref[...], dk_ref[...])
        dp_ref[:, 0:AW] = dxq.astype(BF16)
        dp_ref[:, AW:2 * AW] = dxk.astype(BF16)
        dp_ref[:, 2 * AW:3 * AW] = dv_ref[...].astype(BF16)
        dp_ref[:, 3 * AW:DIN] = dcp_ref[...]

        @pl.when(i == 0)
        def _():
            dqg_ref[...] = dgq
            dkg_ref[...] = dgk

        @pl.when(i > 0)
        def _():
            dqg_ref[...] += dgq
            dkg_ref[...] += dgk

    gspec = pl.BlockSpec((1, AW), lambda i: (0, 0))
    gout = pl.BlockSpec((1, 128), lambda i: (0, 0))
    return pl.pallas_call(
        body, name=name, grid=(s // tm,),
        in_specs=[pl.BlockSpec((tm, AW), lambda i: (i, 0)), pl.BlockSpec((tm, AW), lambda i: (i, 1)),
                  pl.BlockSpec((tm, AW), lambda i: (i, 0)),
                  pl.BlockSpec((tm, AW), lambda i: (i + off, 0)),
                  pl.BlockSpec((tm, AW), lambda i: (i + off, 0)),
                  pl.BlockSpec((tm, D), lambda i: (i, 0)), gspec, gspec],
        out_specs=[pl.BlockSpec((tm, DIN), lambda i: (i, 0)), gout, gout],
        out_shape=[jax.ShapeDtypeStruct((s, DIN), BF16), jax.ShapeDtypeStruct((1, 128), F32),
                   jax.ShapeDtypeStruct((1, 128), F32)],
        compiler_params=_cp("arbitrary"),
    )(p, p, dq, dkp, dvp, dcp, qg, kg)


def _loss_grad(name, y, t):
    s = y.shape[0]
    tm = 512

    def body(y_ref, t_ref, dy_ref, dyb_ref, l_ref):
        i = pl.program_id(0)
        e = y_ref[...] - t_ref[...]
        dy = e * (1.0 / D)
        dy_ref[...] = dy
        dyb_ref[...] = dy.astype(BF16)
        part = 0.5 * jnp.sum(jnp.mean(e * e, axis=-1, keepdims=True), axis=0, keepdims=True)

        @pl.when(i == 0)
        def _():
            l_ref[...] = part

        @pl.when(i > 0)
        def _():
            l_ref[...] += part

    blk = pl.BlockSpec((tm, D), lambda i: (i, 0))
    return pl.pallas_call(
        body, name=name, grid=(s // tm,),
        in_specs=[blk, blk],
        out_specs=[blk, blk, pl.BlockSpec((1, 1), lambda i: (0, 0))],
        out_shape=[jax.ShapeDtypeStruct((s, D), F32), jax.ShapeDtypeStruct((s, D), BF16),
                   jax.ShapeDtypeStruct((1, 1), F32)],
        compiler_params=_cp("arbitrary"),
    )(y, t)


def _mm_nt_relu(name, dxb, w, l, a):
    s = dxb.shape[0]
    tm = MM_ROWS

    def body(d_ref, w_ref, a_ref, o_ref):
        df = lax.dot_general(d_ref[...], w_ref[...], NT_DIMS, preferred_element_type=F32)
        o_ref[...] = (df * (2.0 * jnp.maximum(a_ref[...].astype(F32), 0.0))).astype(BF16)

    return pl.pallas_call(
        body, name=name, grid=(s // tm,),
        in_specs=[pl.BlockSpec((tm, D), lambda i: (i, 0)),
                  pl.BlockSpec((None, DFF, D), lambda i: (l, 0, 0)),
                  pl.BlockSpec((tm, DFF), lambda i: (i, 0))],
        out_specs=pl.BlockSpec((tm, DFF), lambda i: (i, 0)),
        out_shape=jax.ShapeDtypeStruct((s, DFF), BF16),
        compiler_params=_cp("parallel"),
    )(dxb, w, a)


def _proj_out_bwd(name, dxb, w, l, mix):
    s = dxb.shape[0]
    tm = _mm_rows(D, D)

    def body(d_ref, w_ref, o_ref, do_ref, dot_ref, dcp_ref, dl_ref):
        d = d_ref[...]
        wa, wc = w_ref[0:AW, :], w_ref[AW:D, :]
        do = lax.dot_general(d, wa, NT_DIMS, preferred_element_type=F32)
        do_ref[...] = do.astype(BF16)
        dot_ref[...] = lax.dot_general(wa, d, NT_DIMS, preferred_element_type=F32).astype(BF16)
        dcp_ref[...] = lax.dot_general(d, wc, NT_DIMS, preferred_element_type=F32)
        head = lax.broadcasted_iota(jnp.int32, (AW, 128), 0) // HD
        pick = jnp.where(head == lax.broadcasted_iota(jnp.int32, (AW, 128), 1), 1.0, 0.0).astype(BF16)
        dl_ref[...] = _two_pass_dot(do * o_ref[...].astype(F32), pick)

    return pl.pallas_call(
        body, name=name, grid=(s // tm,),
        in_specs=[pl.BlockSpec((tm, D), lambda i: (i, 0)),
                  pl.BlockSpec((None, D, D), lambda i: (l, 0, 0)),
                  pl.BlockSpec((tm, AW), lambda i: (i, 0))],
        out_specs=[pl.BlockSpec((tm, AW), lambda i: (i, 0)), pl.BlockSpec((AW, tm), lambda i: (0, i)),
                   pl.BlockSpec((tm, D - AW), lambda i: (i, 0)), pl.BlockSpec((tm, 128), lambda i: (i, 0))],
        out_shape=[jax.ShapeDtypeStruct((s, AW), BF16), jax.ShapeDtypeStruct((AW, s), BF16),
                   jax.ShapeDtypeStruct((s, D - AW), F32), jax.ShapeDtypeStruct((s, 128), F32)],
        compiler_params=_cp("parallel"),
    )(dxb, w, mix)


def _mm_nt_normbwd(name, gy, w, l, x, g, dres, dep):
    s, k = gy.shape
    tm = MM_ROWS

    def body(gy_ref, w_ref, x_ref, g_ref, dr_ref, dep_ref, dx_ref, dxb_ref, dg_ref):
        del dep_ref
        i = pl.program_id(0)
        dh = lax.dot_general(gy_ref[...], w_ref[...], NT_DIMS, preferred_element_type=F32)
        xv = x_ref[...]
        r = _inv_rms(xv)
        xn = xv * r
        dxn = dh * g_ref[...]
        dx = r * (dxn - xn * jnp.mean(dxn * xn, axis=-1, keepdims=True)) + dr_ref[...]
        dx_ref[...] = dx
        dxb_ref[...] = dx.astype(BF16)
        part = jnp.sum(dh * xn, axis=0, keepdims=True)

        @pl.when(i == 0)
        def _():
            dg_ref[...] = part

        @pl.when(i > 0)
        def _():
            dg_ref[...] += part

    blk = pl.BlockSpec((tm, D), lambda i: (i, 0))
    vec = pl.BlockSpec((1, D), lambda i: (0, 0))
    return pl.pallas_call(
        body, name=name, grid=(s // tm,),
        in_specs=[pl.BlockSpec((tm, k), lambda i: (i, 0)),
                  pl.BlockSpec((None, D, k), lambda i: (l, 0, 0)), blk, vec, blk, ANY],
        out_specs=[blk, blk, vec],
        out_shape=[jax.ShapeDtypeStruct((s, D), F32), jax.ShapeDtypeStruct((s, D), BF16),
                   jax.ShapeDtypeStruct((1, D), F32)],
        compiler_params=_cp("arbitrary"),
    )(gy, w, x, g, dres, dep)


def _mm_tn(name, a, b, tma, tnb, relu2=False):
    s, m = a.shape
    n = b.shape[1]

    def body(a_ref, b_ref, o_ref):
        av = _relu2(a_ref[...]) if relu2 else a_ref[...]
        o_ref[...] = lax.dot_general(av, b_ref[...], (((0,), (0,)), ((), ())),
                                     preferred_element_type=F32).astype(BF16)

    return pl.pallas_call(
        body, name=name, grid=(m // tma, n // tnb),
        in_specs=[pl.BlockSpec((s, tma), lambda i, j: (0, i), pipeline_mode=pl.Buffered(1) if m == tma else None),
                  pl.BlockSpec((s, tnb), lambda i, j: (0, j))],
        out_specs=pl.BlockSpec((tma, tnb), lambda i, j: (i, j)),
        out_shape=jax.ShapeDtypeStruct((m, n), BF16),
        compiler_params=_cp("parallel", "parallel"),
    )(a, b)


def _adamw_math(gv, wv, mv, vv):
    mn = ADAM_B1 * mv + (1.0 - ADAM_B1) * gv
    vn = ADAM_B2 * vv + (1.0 - ADAM_B2) * jnp.square(gv)
    m_hat = mn / (1.0 - ADAM_B1 ** ADAM_STEP)
    v_hat = vn / (1.0 - ADAM_B2 ** ADAM_STEP)
    return gv, -ADAM_LR * (m_hat / (jnp.sqrt(v_hat) + ADAM_EPS) + ADAM_WD * wv), mn, vn


def _adamw(name, g, w, m, v):
    r, c = g.shape
    tm = 256 if r % 256 == 0 else r

    def body(g_ref, w_ref, m_ref, v_ref, go_ref, d_ref, mo_ref, vo_ref):
        go_ref[...], d_ref[...], mo_ref[...], vo_ref[...] = _adamw_math(g_ref[...], w_ref[...], m_ref[...], v_ref[...])

    blk = pl.BlockSpec((tm, c), lambda i: (i, 0))
    return pl.pallas_call(
        body, name=name, grid=(r // tm,),
        in_specs=[blk] * 4, out_specs=[blk] * 4,
        out_shape=[jax.ShapeDtypeStruct((r, c), F32)] * 4,
        compiler_params=_cp("parallel"),
    )(g, w, m, v)


def _place():
    x, y, c = lax.axis_index("x"), lax.axis_index("y"), lax.axis_index("c")
    chips = [(1 - x, y), (x, 1 - y), (1 - x, 1 - y)]
    return x, y, c, chips


BLOCK_AXIS = (2, 1, 2, 1)
LARGE_DIMS = ((D, DIN), (D, D), (D, DFF), (DFF, D))


def _full_shape(t, layers, dtype):
    r, c = LARGE_DIMS[t]
    return jax.ShapeDtypeStruct((layers, r, c), dtype)


def _cast_into_full(name, t, shard, b1, dep):
    _, r, c = shard.shape
    tm = min(512, r)
    if BLOCK_AXIS[t] == 1:
        out_spec = pl.BlockSpec((None, tm, c), lambda l, i, br: (l, br[0] * (r // tm) + i, 0))
    else:
        out_spec = pl.BlockSpec((None, tm, c), lambda l, i, br: (l, i, br[0]))

    def body(b_ref, x_ref, dep_ref, o_ref):
        del b_ref, dep_ref
        o_ref[...] = x_ref[...].astype(BF16)

    return pl.pallas_call(
        body, name=name,
        grid_spec=pltpu.PrefetchScalarGridSpec(
            num_scalar_prefetch=1, grid=(DEPTH, r // tm),
            in_specs=[pl.BlockSpec((None, tm, c), lambda l, i, br: (l, i, 0)), ANY],
            out_specs=out_spec),
        out_shape=_full_shape(t, DEPTH, BF16),
        compiler_params=_cp("parallel", "parallel"),
    )(b1, shard, dep)


HBM = pl.BlockSpec(memory_space=pltpu.HBM)
SEM = pl.BlockSpec(memory_space=pltpu.SEMAPHORE)
DATAFLOW = pltpu.SideEffectType.DATAFLOW_SIDE_EFFECTING


def _half(ref, l, t, b, c):
    r, cols = LARGE_DIMS[t]
    if BLOCK_AXIS[t] == 1:
        n = r // 8
        return ref.at[l, pl.ds(pl.multiple_of(b * (2 * n) + c * n, 16), n), :]
    n, w = r // 2, cols // 4
    return ref.at[l, pl.ds(pl.multiple_of(c * n, 16), n), pl.ds(pl.multiple_of(b * w, 128), w)]


def _gather_start(name, layers, ts, fulls):
    n = len(ts)

    def body(*refs):
        f_refs, sems = refs[n:2 * n], refs[2 * n:2 * n + 2 * len(layers)]
        x, y, c, chips = _place()
        for i, l in enumerate(layers):
            for k, t in enumerate(ts):
                own = _half(f_refs[k], l, t, 2 * x + y, c)
                for j, (cx, cy) in enumerate(chips):
                    pltpu.make_async_remote_copy(src_ref=own, dst_ref=own, send_sem=sems[2 * i].at[3 * t + j],
                                                 recv_sem=sems[2 * i + 1].at[3 * t + j], device_id=(cx, cy, c),
                                                 device_id_type=MESH).start()
        refs[-1][...] = jnp.zeros((8, 128), F32)

    outs = pl.pallas_call(
        body, name=name,
        in_specs=[HBM] * n,
        out_specs=[HBM] * n + [SEM] * (2 * len(layers)) + [pl.BlockSpec(memory_space=pltpu.VMEM)],
        out_shape=[pltpu.HBM(f.shape, f.dtype) for f in fulls]
        + [pltpu.SemaphoreType.DMA((12,))] * (2 * len(layers)) + [jax.ShapeDtypeStruct((8, 128), F32)],
        input_output_aliases={k: k for k in range(n)},
        compiler_params=pltpu.CompilerParams(has_side_effects=DATAFLOW),
    )(*[pltpu.with_memory_space_constraint(f, pltpu.HBM) for f in fulls])
    return outs[0:n], {l: (outs[n + 2 * i], outs[n + 1 + 2 * i]) for i, l in enumerate(layers)}, outs[-1]


def _gather_wait(name, l, ts, fulls, sems, after):
    def body(*refs):
        send_sems, recv_sems, f_refs = refs[4], refs[5], refs[7:11]
        x, y, c, chips = _place()
        for t in ts:
            own = _half(f_refs[t], l, t, 2 * x + y, c)
            for j, (cx, cy) in enumerate(chips):
                landed = _half(f_refs[t], l, t, 2 * cx + cy, c)
                pltpu.make_async_remote_copy(src_ref=own, dst_ref=landed, send_sem=send_sems.at[3 * t + j],
                                             recv_sem=recv_sems.at[3 * t + j], device_id=(cx, cy, c),
                                             device_id_type=MESH).wait()

    return pl.pallas_call(
        body, name=name,
        in_specs=[HBM] * 4 + [SEM, SEM, ANY], out_specs=[HBM] * 4,
        out_shape=[pltpu.HBM(s.shape, s.dtype) for s in (_full_shape(t, DEPTH, BF16) for t in range(4))],
        input_output_aliases={t: t for t in range(4)},
        compiler_params=pltpu.CompilerParams(has_side_effects=DATAFLOW),
    )(*fulls, sems[0], sems[1], after)


def _pass_on(name, l, ts, fulls):
    def body(*refs):
        f_refs, send_sems, recv_sems = refs[4:8], refs[8], refs[9]
        x, y, c, chips = _place()

        def copy(t, j, half):
            cx, cy = chips[j]
            part = _half(f_refs[t], l, t, 2 * cx + cy, half)
            return pltpu.make_async_remote_copy(src_ref=part, dst_ref=part, send_sem=send_sems.at[3 * t + j],
                                                recv_sem=recv_sems.at[3 * t + j], device_id=(x, y, 1 - c),
                                                device_id_type=MESH)

        for t in ts:
            for j in range(3):
                copy(t, j, c).start()
        for t in ts:
            for j in range(3):
                copy(t, j, 1 - c).wait_recv()
                copy(t, j, c).wait_send()

    return pl.pallas_call(
        body, name=name,
        in_specs=[ANY] * 4, out_specs=[ANY] * 4,
        out_shape=[_full_shape(t, DEPTH, BF16) for t in range(4)],
        input_output_aliases={t: t for t in range(4)},
        scratch_shapes=[pltpu.SemaphoreType.DMA((12,)), pltpu.SemaphoreType.DMA((12,))],
    )(*fulls)


def _block2d(ref, t, b):
    r, cols = LARGE_DIMS[t]
    if BLOCK_AXIS[t] == 1:
        return ref.at[pl.ds(pl.multiple_of(b * (r // 4), 16), r // 4), :]
    return ref.at[:, pl.ds(pl.multiple_of(b * (cols // 4), 128), cols // 4)]


def _block_dims(t):
    r, cols = LARGE_DIMS[t]
    return (r // 4, cols) if BLOCK_AXIS[t] == 1 else (r, cols // 4)


def _reduce_copies(ts, g_refs, r_refs, send_sems, recv_sems):
    _, _, c, chips = _place()
    return [pltpu.make_async_remote_copy(src_ref=_block2d(g_refs[i], t, 2 * cx + cy), dst_ref=r_refs[i].at[j],
                                         send_sem=send_sems.at[3 * i + j], recv_sem=recv_sems.at[3 * i + j],
                                         device_id=(cx, cy, c), device_id_type=MESH)
            for i, t in enumerate(ts) for j, (cx, cy) in enumerate(chips)]


def _reduce_start(name, ts, grads):
    n = len(ts)

    def body(*refs):
        for cp in _reduce_copies(ts, refs[n:2 * n], refs[2 * n:3 * n], refs[3 * n], refs[3 * n + 1]):
            cp.start()
        refs[3 * n + 2][...] = jnp.zeros((8, 128), F32)

    outs = pl.pallas_call(
        body, name=name,
        in_specs=[HBM] * n,
        out_specs=[HBM] * (2 * n) + [SEM, SEM, pl.BlockSpec(memory_space=pltpu.VMEM)],
        out_shape=[pltpu.HBM(g.shape, BF16) for g in grads]
        + [pltpu.HBM((3,) + _block_dims(t), BF16) for t in ts]
        + [pltpu.SemaphoreType.DMA((3 * n,)), pltpu.SemaphoreType.DMA((3 * n,)), jax.ShapeDtypeStruct((8, 128), F32)],
        input_output_aliases={i: i for i in range(n)},
        compiler_params=pltpu.CompilerParams(has_side_effects=DATAFLOW),
    )(*[pltpu.with_memory_space_constraint(g, pltpu.HBM) for g in grads])
    return outs[0:n], outs[n:2 * n], (outs[2 * n], outs[2 * n + 1]), outs[2 * n + 2]


def _reduce_wait(name, ts, grads, landing, sems, afters):
    n = len(ts)
    first_out = 2 * n + 2 + len(afters)

    def body(*refs):
        for cp in _reduce_copies(ts, refs[first_out:first_out + n], refs[first_out + n:first_out + 2 * n],
                                 refs[2 * n], refs[2 * n + 1]):
            cp.wait()

    outs = pl.pallas_call(
        body, name=name,
        in_specs=[HBM] * (2 * n) + [SEM, SEM] + [ANY] * len(afters), out_specs=[HBM] * (2 * n),
        out_shape=[pltpu.HBM(g.shape, BF16) for g in grads] + [pltpu.HBM(r.shape, BF16) for r in landing],
        input_output_aliases={i: i for i in range(2 * n)},
        compiler_params=pltpu.CompilerParams(has_side_effects=DATAFLOW),
    )(*grads, *landing, sems[0], sems[1], *afters)
    return outs[0:n], outs[n:2 * n]


def _add4(name, t, own, landed, b1):
    rb, cb = _block_dims(t)
    tm = min(512, rb)
    if BLOCK_AXIS[t] == 1:
        own_spec = pl.BlockSpec((tm, cb), lambda i, br: (br[0] * (rb // tm) + i, 0))
    else:
        own_spec = pl.BlockSpec((tm, cb), lambda i, br: (i, br[0]))

    def body(b_ref, o_ref, r0_ref, r1_ref, r2_ref, s_ref):
        del b_ref
        s_ref[...] = ((o_ref[...].astype(F32) + r0_ref[...].astype(F32))
                      + (r1_ref[...].astype(F32) + r2_ref[...].astype(F32))).astype(BF16)

    def got(j):
        return pl.BlockSpec((None, tm, cb), lambda i, br: (j, i, 0))

    return pl.pallas_call(
        body, name=name,
        grid_spec=pltpu.PrefetchScalarGridSpec(
            num_scalar_prefetch=1, grid=(rb // tm,),
            in_specs=[own_spec, got(0), got(1), got(2)],
            out_specs=pl.BlockSpec((tm, cb), lambda i, br: (i, 0))),
        out_shape=jax.ShapeDtypeStruct((rb, cb), BF16),
        compiler_params=_cp("parallel"),
    )(b1, own, landed, landed, landed)


def _swap_sib(name, sums):
    def body(*refs):
        s_refs, t_refs, send_sems, recv_sems = refs[0:4], refs[4:8], refs[8], refs[9]
        x, y, c, _ = _place()
        cps = [pltpu.make_async_remote_copy(src_ref=s_refs[t], dst_ref=t_refs[t], send_sem=send_sems.at[t],
                                            recv_sem=recv_sems.at[t], device_id=(x, y, 1 - c), device_id_type=MESH)
               for t in range(4)]
        for cp in cps:
            cp.start()
        for cp in cps:
            cp.wait()

    return pl.pallas_call(
        body, name=name,
        in_specs=[ANY] * 4, out_specs=[ANY] * 4,
        out_shape=[jax.ShapeDtypeStruct(s.shape, BF16) for s in sums],
        scratch_shapes=[pltpu.SemaphoreType.DMA((4,)), pltpu.SemaphoreType.DMA((4,))],
    )(*sums)


def _adamw_pair(name, l, s_own, s_sib, w, m, v, outs):
    rb, cb = s_own.shape
    tm = min(512, rb)

    def body(a_ref, b_ref, w_ref, m_ref, v_ref, g0, d0, m0, v0, go_ref, d_ref, mo_ref, vo_ref):
        del g0, d0, m0, v0
        gv = a_ref[...].astype(F32) + b_ref[...].astype(F32)
        go_ref[...], d_ref[...], mo_ref[...], vo_ref[...] = _adamw_math(gv, w_ref[...], m_ref[...], v_ref[...])

    part = pl.BlockSpec((tm, cb), lambda i: (i, 0))
    layer = pl.BlockSpec((None, tm, cb), lambda i: (l, i, 0))
    return pl.pallas_call(
        body, name=name, grid=(rb // tm,),
        in_specs=[part, part, layer, layer, layer] + [ANY] * 4,
        out_specs=[layer] * 4,
        out_shape=[jax.ShapeDtypeStruct((DEPTH, rb, cb), F32)] * 4,
        input_output_aliases={5 + i: i for i in range(4)},
        compiler_params=_cp("parallel"),
    )(s_own, s_sib, w, m, v, *outs)


def _all_gather8(name, v, dep):
    m_per, n = v.shape

    def body(v_ref, dep_ref, out_ref, send_sems, recv_sems, local_sem):
        del dep_ref
        x, y, c, chips = _place()
        me, sib = (x, y, c), (x, y, 1 - c)

        def rows(px, py, pc):
            return out_ref.at[pl.ds((4 * px + 2 * py + pc) * m_per, m_per), :]

        def copy(k, block, to, src=None):
            return pltpu.make_async_remote_copy(
                src_ref=rows(*block) if src is None else src, dst_ref=rows(*block),
                send_sem=send_sems.at[k], recv_sem=recv_sems.at[k], device_id=to, device_id_type=MESH)

        mine = pltpu.make_async_copy(v_ref, rows(*me), local_sem)
        mine.start()
        first = [copy(0, me, sib, src=v_ref)]
        first += [copy(1 + j, me, (*chip, c), src=v_ref) for j, chip in enumerate(chips)]
        for cp in first:
            cp.start()
        passed = [copy(4 + j, (*chip, c), sib) for j, chip in enumerate(chips)]
        for j, chip in enumerate(chips):
            copy(1 + j, (*chip, c), me).wait_recv()
            passed[j].start()
        copy(0, sib, me).wait_recv()
        for j, chip in enumerate(chips):
            copy(4 + j, (*chip, 1 - c), me).wait_recv()
        for cp in first + passed:
            cp.wait_send()
        mine.wait()

    return pl.pallas_call(
        body, name=name,
        out_shape=jax.ShapeDtypeStruct((8 * m_per, n), v.dtype),
        in_specs=[pl.BlockSpec(memory_space=pltpu.VMEM), ANY],
        out_specs=pl.BlockSpec(memory_space=pltpu.VMEM),
        scratch_shapes=[pltpu.SemaphoreType.DMA((7,)), pltpu.SemaphoreType.DMA((7,)), pltpu.SemaphoreType.DMA],
    )(v, dep)


def _sum8(name, g):
    def body(g_ref, o_ref):
        acc = g_ref[0]
        for d in range(1, 8):
            acc = acc + g_ref[d]
        o_ref[...] = acc

    return pl.pallas_call(body, name=name, out_shape=jax.ShapeDtypeStruct(g.shape[1:], F32))(g)


def _pack(parts):
    flat = []
    for a in parts:
        a = a.reshape(-1)
        flat.append(jnp.pad(a, (0, (-a.shape[0]) % 128)))
    cat = jnp.concatenate(flat)
    cat = jnp.pad(cat, (0, (-cat.shape[0]) % 1024))
    return cat.reshape(-1, 128)


def _unpack(packed, shapes):
    flat = packed.reshape(-1)
    out, at = [], 0
    for shp in shapes:
        n = 1
        for d in shp:
            n *= d
        out.append(flat[at:at + n].reshape(shp))
        at += n + (-n) % 128
    return out


def _local_step(x, target, layer_weights, on_grads, small):
    qg_all = jnp.tile(small["q_norm_g"], (1, 8))
    kg_all = jnp.tile(small["k_norm_g"], (1, 8))
    bias_all = _bias_layout(_bias_expand("bias_expand", jnp.pad(small["rel_bias"], ((0, 0), (0, 0), (0, NIDX - 257)))))
    same_group = jnp.eye(4, dtype=F32)[None, :, None, :, None]
    pwbd_all = (small["pool_w"][:, :, :, None, :] * same_group).reshape(DEPTH, PWD, PWD)
    saved = []
    xin = x
    h = _rmsnorm("norm_first", x, small["norm1_g"][0:1])
    for l in range(DEPTH):
        w_in = layer_weights(l, (0,), xin)[0]
        qg, kg = qg_all[l:l + 1], kg_all[l:l + 1]
        cw, pwbd, ps = small["conv_w"][l], pwbd_all[l], small["pool_scale"][l:l + 1]
        p = _mm_nn(f"proj_in_{l}", h, w_in, l, F32)
        q, qt, kp, kt, vp, vt = _qkv(f"qkv_{l}", p, qg, kg)
        o, lse = _attn_fwd(f"attn_fwd_{l}", kp, qt, vt, bias_all, l)
        w_in, w_out, w_1, w_2 = layer_weights(l, (1, 2, 3), o)
        mix = _convpool_fwd(f"convpool_fwd_{l}", p, o, cw, pwbd, ps)
        x1, h2 = _mm_res_norm(f"proj_out_{l}", mix, w_out, l, xin, small["norm2_g"][l:l + 1])
        gnext = small["norm1_g"][(l + 1) % DEPTH][None]
        a, x2, hnext = _mlp_fwd(f"mlp_{l}", h2, w_1, w_2, l, x1, gnext)
        saved.append(dict(xin=xin, h=h, p=p, q=q, qt=qt, kp=kp, kt=kt, vp=vp, mix=mix, x1=x1, h2=h2, a=a, lse=lse,
                          qg=qg, kg=kg, cw=cw, pwbd=pwbd, ps=ps))
        xin, h = x2, hnext

    dx, dxb, loss = _loss_grad("loss_grad", xin, target)
    raw = {k: [None] * DEPTH for k in ("dg1", "dqg", "dkg", "dw0", "dw1", "dw2", "dpw", "dps", "dg2")}
    db_all = lax.empty((DEPTH, 4, KB, 128), F32)
    for l in reversed(range(DEPTH)):
        sv = saved[l]
        da = _mm_nt_relu(f"mlp2_bwd_{l}", dxb, w_2, l, sv["a"])
        g_2 = _mm_tn(f"mlp2_wgrad_{l}", sv["a"], dxb, 512, 1024, relu2=True)
        g_1 = _mm_tn(f"mlp1_wgrad_{l}", sv["h2"], da, 1024, 512)
        dep = on_grads(l, (2, 3), (g_1, g_2))
        dx1, dx1b, dg2 = _mm_nt_normbwd(f"mlp1_bwd_{l}", da, w_1, l, sv["x1"], small["norm2_g"][l:l + 1], dx, dep)
        do, dot, dmix, dl = _proj_out_bwd(f"proj_out_bwd_{l}", dx1b, w_out, l, sv["mix"])
        g_out = _mm_tn(f"proj_out_wgrad_{l}", sv["mix"], dx1b, 512, 1024)
        dcp, dw0, dw1, dw2, dps, dpw = _convpool_bwd(f"convpool_bwd_{l}", sv["p"], dmix, sv["cw"], sv["pwbd"], sv["ps"])
        dq, dkp, dvp, db_all = _attn_bwd(f"attn_bwd_{l}", sv["q"], sv["qt"], sv["kp"], sv["kt"], sv["vp"], bias_all, l,
                                     do, dot, sv["lse"], _rowsum_layout(dl, x.shape[0] // UNIT), db_all)
        dp, dqg, dkg = _qkv_bwd(f"qkv_bwd_{l}", sv["p"], dq, dkp, dvp, dcp, sv["qg"], sv["kg"])
        g_in = _mm_tn(f"proj_in_wgrad_{l}", sv["h"], dp, 1024, 1280)
        dep = on_grads(l, (0, 1), (g_in, g_out))
        dx, dxb, dg1 = _mm_nt_normbwd(f"proj_in_bwd_{l}", dp, w_in, l, sv["xin"], small["norm1_g"][l:l + 1], dx1, dep)
        for k, val in dict(dg1=dg1, dqg=dqg, dkg=dkg, dw0=dw0, dw1=dw1, dw2=dw2, dpw=dpw, dps=dps, dg2=dg2).items():
            raw[k][l] = val
    cat = {k: jnp.concatenate(v, axis=0) for k, v in raw.items() if k != "dpw"}
    drb = _bias_reduce("bias_reduce", _bias_unlayout(db_all))
    dpw = jnp.stack(raw["dpw"])
    gsmall = {
        "norm1_g": cat["dg1"], "q_norm_g": cat["dqg"][:, :HD], "k_norm_g": cat["dkg"][:, :HD],
        "rel_bias": drb[:, :, :257],
        "conv_w": jnp.stack([cat["dw0"], cat["dw1"], cat["dw2"]], axis=1),
        "pool_w": jnp.stack([dpw[:, g * 64:(g + 1) * 64, g * 64:(g + 1) * 64] for g in range(4)], axis=1),
        "pool_scale": cat["dps"], "norm2_g": cat["dg2"],
    }
    return loss, dx, gsmall


SMALL = ("norm1_g", "q_norm_g", "k_norm_g", "rel_bias", "conv_w", "pool_w", "pool_scale", "norm2_g")
LARGE = ("w_in", "w_out", "w_mlp1", "w_mlp2")


def kernel(x, norm1_g, w_in, q_norm_g, k_norm_g, rel_bias, conv_w, pool_w, pool_scale, w_out, norm2_g, w_mlp1, w_mlp2, loss_target, m_norm1_g, m_w_in, m_q_norm_g, m_k_norm_g, m_rel_bias, m_conv_w, m_pool_w, m_pool_scale, m_w_out, m_norm2_g, m_w_mlp1, m_w_mlp2, v_norm1_g, v_w_in, v_q_norm_g, v_k_norm_g, v_rel_bias, v_conv_w, v_pool_w, v_pool_scale, v_w_out, v_norm2_g, v_w_mlp1, v_w_mlp2):
    w = dict(norm1_g=norm1_g, w_in=w_in, q_norm_g=q_norm_g, k_norm_g=k_norm_g, rel_bias=rel_bias, conv_w=conv_w,
             pool_w=pool_w, pool_scale=pool_scale, w_out=w_out, norm2_g=norm2_g, w_mlp1=w_mlp1, w_mlp2=w_mlp2)
    m = dict(norm1_g=m_norm1_g, w_in=m_w_in, q_norm_g=m_q_norm_g, k_norm_g=m_k_norm_g, rel_bias=m_rel_bias,
             conv_w=m_conv_w, pool_w=m_pool_w, pool_scale=m_pool_scale, w_out=m_w_out, norm2_g=m_norm2_g,
             w_mlp1=m_w_mlp1, w_mlp2=m_w_mlp2)
    v = dict(norm1_g=v_norm1_g, w_in=v_w_in, q_norm_g=v_q_norm_g, k_norm_g=v_k_norm_g, rel_bias=v_rel_bias,
             conv_w=v_conv_w, pool_w=v_pool_w, pool_scale=v_pool_scale, w_out=v_w_out, norm2_g=v_norm2_g,
             w_mlp1=v_w_mlp1, w_mlp2=v_w_mlp2)
    ax, ay, ac = lax.axis_index("x"), lax.axis_index("y"), lax.axis_index("c")
    b1 = jnp.reshape(2 * ax + ay, (1,)).astype(jnp.int32)

    cw_rows = _all_gather8("gather_conv_w", jnp.pad(conv_w.reshape(DEPTH * 3, 64), ((0, 4), (0, 64))), b1)
    cw_chips = [cw_rows[(4 * cx + 2 * cy) * 16:(4 * cx + 2 * cy) * 16 + 12, :64] for cx in range(2) for cy in range(2)]
    small = {n: w[n] for n in SMALL}
    small["conv_w"] = jnp.concatenate(cw_chips, axis=1).reshape(DEPTH, 3, CW)

    (w_in_full,), in_sems, in_token = _gather_start(
        "gather_start_in", (0,), (0,), [_cast_into_full("cast_w_in", 0, w["w_in"], b1, cw_rows)])
    others, first_sems, first_token = _gather_start(
        "gather_start_first", (0,), (1, 2, 3),
        [_cast_into_full(f"cast_{LARGE[t]}", t, w[LARGE[t]], b1, in_token) for t in (1, 2, 3)])
    held = [[w_in_full] + list(others)]
    sems = {(0, 0): in_sems[0], (0, 1): first_sems[0]}

    def layer_weights(l, ts, after):
        if l > 0:
            ts = (0, 1, 2, 3) if ts == (0,) else ()
        if ts:
            tag = f"{l}_{ts[0]}"
            first_in = l == 0 and ts == (0,)
            after = first_token if first_in else after
            arrived = _gather_wait(f"gather_wait_{tag}", l, ts, held[0], sems[l, ts[0] if l == 0 else 0], after)
            if first_in:
                arrived, rest_sems, _ = _gather_start("gather_start_rest", tuple(range(1, DEPTH)), (0, 1, 2, 3),
                                                      arrived)
                sems.update({(k, 0): v for k, v in rest_sems.items()})
            held[0] = _pass_on(f"pass_on_{tag}", l, ts, arrived)
        return held[0]

    flights = {}

    def await_flight(l, ts, afters):
        g, landing, sm, _ = flights[l, ts]
        flights[l, ts] = _reduce_wait(f"reduce_wait_{l}_{ts[0]}", ts, g, landing, sm, afters)

    def on_grads(l, ts, grads):
        if ts == (0, 1) and l + 1 < DEPTH:
            await_flight(l + 1, (2, 3), [grads[0]])
            await_flight(l + 1, (0, 1), [grads[0]])
        flights[l, ts] = _reduce_start(f"reduce_start_{l}_{ts[0]}", ts, grads)
        return flights[l, ts][3]

    loss_part, grad_x, gsmall = _local_step(x[0], loss_target[0], layer_weights, on_grads, small)
    loss = lax.psum(loss_part[0, 0], ("x", "y", "c"))
    order = [n for n in SMALL]
    packed = _pack([gsmall[n] for n in order])

    out = {n: [lax.empty(w[n].shape, F32) for _ in range(4)] for n in LARGE}
    for l in reversed(range(DEPTH)):
        if l == 0:
            afters = [grad_x, packed] + [out[n][0] for n in LARGE]
            await_flight(0, (2, 3), afters)
            await_flight(0, (0, 1), afters)
        sums = [None] * 4
        for ts in ((0, 1), (2, 3)):
            g, landing = flights[l, ts]
            for i, t in enumerate(ts):
                sums[t] = _add4(f"add4_{LARGE[t]}_{l}", t, g[i], landing[i], b1)
        theirs = _swap_sib(f"swap_sib_{l}", sums)
        for t, n in enumerate(LARGE):
            out[n] = _adamw_pair(f"adamw_{n}_{l}", l, sums[t], theirs[t], w[n], m[n], v[n], out[n])

    rows = packed.shape[0]
    summed = _sum8("sum_small", _all_gather8("gather_small", packed, out[LARGE[0]][0]).reshape(8, rows, 128))
    gfull = dict(zip(order, _unpack(summed, [gsmall[n].shape for n in order])))
    gfull["conv_w"] = lax.dynamic_slice_in_dim(gfull["conv_w"], (2 * ax + ay) * 64, 64, axis=2)
    res = _adamw("adamw_small", _pack([gfull[n] for n in order]), _pack([w[n] for n in order]),
                 _pack([m[n] for n in order]), _pack([v[n] for n in order]))
    for n, parts in zip(order, zip(*[_unpack(r, [w[k].shape for k in order]) for r in res])):
        out[n] = list(parts)

    names = ("norm1_g", "w_in", "q_norm_g", "k_norm_g", "rel_bias", "conv_w", "pool_w", "pool_scale", "w_out",
             "norm2_g", "w_mlp1", "w_mlp2")
    flat = [loss, grad_x[None]]
    for i in range(4):
        flat += [out[n][i] for n in names]
    return tuple(flat)
```

```python
import jax
import jax.numpy as jnp
from jax import lax
from jax.experimental import pallas as pl
from jax.experimental.pallas import tpu as pltpu

F32 = jnp.float32
BF16 = jnp.bfloat16

D = 1024
DEPTH = 4
CH = 64
NPREV = 8
KB = (NPREV + 1) * CH
PADR = NPREV * CH
HD = 64
AW = 512
CW = 256
PWD = 256
DIN = 3 * AW + 3 * CW + PWD
DFF = 4 * D
NIDX = 384
EPS = 1e-6
NEG_INF = -1e30

ADAM_LR = 0.001
ADAM_B1 = 0.9
ADAM_B2 = 0.999
ADAM_EPS = 1e-08
ADAM_WD = 0.01
ADAM_STEP = 10

VMEM_LIMIT = 52 * 1024 * 1024
MM_ROWS = 512


def _mm_rows(k, n):
    return 2 * MM_ROWS if k + n <= 2048 else MM_ROWS


MESH = pl.DeviceIdType.MESH
ANY = pl.BlockSpec(memory_space=pl.ANY)


def _cp(*sem):
    return pltpu.CompilerParams(dimension_semantics=sem, vmem_limit_bytes=VMEM_LIMIT)


def _inv_rms(x):
    return lax.rsqrt(jnp.mean(x * x, axis=-1, keepdims=True) + EPS)


def _head_mean_matrix():
    r = lax.broadcasted_iota(jnp.int32, (AW, AW), 0) // HD
    c = lax.broadcasted_iota(jnp.int32, (AW, AW), 1) // HD
    return jnp.where(r == c, 1.0 / HD, 0.0).astype(BF16)


def _two_pass_dot(x, m):
    hi = x.astype(BF16)
    lo = (x - hi.astype(F32)).astype(BF16)
    return (jnp.dot(hi, m, preferred_element_type=F32)
            + jnp.dot(lo, m, preferred_element_type=F32))


def _head_mean(x, hm):
    return _two_pass_dot(x, hm)


def _rmsnorm(name, x, g):
    s = x.shape[0]
    tm = 512

    def body(x_ref, g_ref, h_ref):
        xv = x_ref[...]
        h_ref[...] = (xv * _inv_rms(xv) * g_ref[...]).astype(BF16)

    return pl.pallas_call(
        body, name=name, grid=(s // tm,),
        in_specs=[pl.BlockSpec((tm, D), lambda i: (i, 0)), pl.BlockSpec((1, D), lambda i: (0, 0))],
        out_specs=pl.BlockSpec((tm, D), lambda i: (i, 0)),
        out_shape=jax.ShapeDtypeStruct((s, D), BF16),
        compiler_params=_cp("parallel"),
    )(x, g)


def _relu2(a):
    r = jnp.maximum(a, jnp.zeros_like(a))
    return r * r


def _mm_nn(name, a, w, l, out_dtype):
    s, k = a.shape
    n = w.shape[2]
    tm = _mm_rows(k, n)

    def body(a_ref, w_ref, o_ref):
        o_ref[...] = jnp.dot(a_ref[...], w_ref[...], preferred_element_type=F32).astype(o_ref.dtype)

    return pl.pallas_call(
        body, name=name, grid=(s // tm,),
        in_specs=[pl.BlockSpec((tm, k), lambda i: (i, 0)),
                  pl.BlockSpec((None, k, n), lambda i: (l, 0, 0))],
        out_specs=pl.BlockSpec((tm, n), lambda i: (i, 0)),
        out_shape=jax.ShapeDtypeStruct((s, n), out_dtype),
        compiler_params=_cp("parallel"),
    )(a, w)


def _mm_res_norm(name, a, w, l, res, g):
    s, k = a.shape
    tm = _mm_rows(k, D)

    def body(a_ref, w_ref, r_ref, g_ref, x_ref, h_ref):
        acc = r_ref[...] + jnp.dot(a_ref[...], w_ref[...], preferred_element_type=F32)
        x_ref[...] = acc
        h_ref[...] = (acc * _inv_rms(acc) * g_ref[...]).astype(BF16)

    return pl.pallas_call(
        body, name=name, grid=(s // tm,),
        in_specs=[pl.BlockSpec((tm, k), lambda i: (i, 0)),
                  pl.BlockSpec((None, k, D), lambda i: (l, 0, 0)),
                  pl.BlockSpec((tm, D), lambda i: (i, 0)),
                  pl.BlockSpec((1, D), lambda i: (0, 0))],
        out_specs=[pl.BlockSpec((tm, D), lambda i: (i, 0))] * 2,
        out_shape=[jax.ShapeDtypeStruct((s, D), F32), jax.ShapeDtypeStruct((s, D), BF16)],
        compiler_params=_cp("parallel"),
    )(a, w, res, g)


def _mlp_fwd(name, h2, w1, w2, l, res, last):
    s = h2.shape[0]
    tm = 256
    final = last.shape[0] == s

    def body(h_ref, w1_ref, w2_ref, r_ref, last_ref, a_ref, first_ref, second_ref, *loss_ref):
        a = jnp.dot(h_ref[...], w1_ref[...], preferred_element_type=F32).astype(BF16)
        a_ref[...] = a
        acc = r_ref[...] + jnp.dot(_relu2(a), w2_ref[...], preferred_element_type=F32)
        if not final:
            first_ref[...] = acc
            second_ref[...] = (acc * _inv_rms(acc) * last_ref[...]).astype(BF16)
            return
        e = acc - last_ref[...]
        dy = e * (1.0 / D)
        first_ref[...] = dy
        second_ref[...] = dy.astype(BF16)
        part = 0.5 * jnp.sum(jnp.mean(e * e, axis=-1, keepdims=True), axis=0, keepdims=True)
        i = pl.program_id(0)

        @pl.when(i == 0)
        def _():
            loss_ref[0][...] = part

        @pl.when(i > 0)
        def _():
            loss_ref[0][...] += part

    once = pl.Buffered(1)
    rows = pl.BlockSpec((tm, D), lambda i: (i, 0))
    one = pl.BlockSpec((1, 1), lambda i: (0, 0))
    return pl.pallas_call(
        body, name=name, grid=(s // tm,),
        in_specs=[rows,
                  pl.BlockSpec((None, D, DFF), lambda i: (l, 0, 0), pipeline_mode=once),
                  pl.BlockSpec((None, DFF, D), lambda i: (l, 0, 0), pipeline_mode=once),
                  rows, rows if final else pl.BlockSpec((1, D), lambda i: (0, 0))],
        out_specs=[pl.BlockSpec((tm, DFF), lambda i: (i, 0)), rows, rows] + ([one] if final else []),
        out_shape=[jax.ShapeDtypeStruct((s, DFF), BF16), jax.ShapeDtypeStruct((s, D), F32),
                   jax.ShapeDtypeStruct((s, D), BF16)] + ([jax.ShapeDtypeStruct((1, 1), F32)] if final else []),
        compiler_params=_cp("arbitrary" if final else "parallel"),
    )(h2, w1, w2, res, last)


def _qkv(name, p, qg, kg):
    s = p.shape[0]
    tm = PADR
    nb = s // tm

    def body(pq_ref, pk_ref, pv_ref, qg_ref, kg_ref, q_ref, qt_ref, k_ref, kt_ref, v_ref, vt_ref):
        t = pl.program_id(0)
        hm = _head_mean_matrix()

        def nrm(x, g):
            return x * lax.rsqrt(_head_mean(x * x, hm) + EPS) * g

        first = t == 0
        qq = nrm(pq_ref[...], qg_ref[...]) * 0.125
        kk = jnp.where(first, 0.0, nrm(pk_ref[...], kg_ref[...]))
        vv = jnp.where(first, 0.0, pv_ref[...])
        q_ref[...] = qq.astype(BF16)
        qt_ref[...] = qq.T.astype(BF16)
        k_ref[...] = kk.astype(BF16)
        kt_ref[...] = kk.T.astype(BF16)
        v_ref[...] = vv.astype(BF16)
        vt_ref[...] = vv.T.astype(BF16)

    def src(col):
        return pl.BlockSpec((tm, AW), lambda t: (jnp.maximum(t - 1, 0), col))

    gspec = pl.BlockSpec((1, AW), lambda t: (0, 0))
    rows = pl.BlockSpec((tm, AW), lambda t: (t, 0))
    cols = pl.BlockSpec((AW, tm), lambda t: (0, t))
    return pl.pallas_call(
        body, name=name, grid=(nb + 1,),
        in_specs=[src(0), src(1), src(2), gspec, gspec],
        out_specs=[pl.BlockSpec((tm, AW), lambda t: (jnp.maximum(t - 1, 0), 0)),
                   pl.BlockSpec((AW, tm), lambda t: (0, jnp.maximum(t - 1, 0))),
                   rows, cols, rows, cols],
        out_shape=[jax.ShapeDtypeStruct((s, AW), BF16), jax.ShapeDtypeStruct((AW, s), BF16),
                   jax.ShapeDtypeStruct((s + PADR, AW), BF16), jax.ShapeDtypeStruct((AW, s + PADR), BF16),
                   jax.ShapeDtypeStruct((s + PADR, AW), BF16), jax.ShapeDtypeStruct((AW, s + PADR), BF16)],
        compiler_params=_cp("arbitrary"),
    )(p, p, p, qg, kg)


NBAND = KB // CH
HIGHEST = lax.Precision.HIGHEST
NT_DIMS = (((1,), (1,)), ((), ()))


def _onehot_table(a):
    m = lax.broadcasted_iota(jnp.int32, (128, NIDX), 0)
    idx = lax.broadcasted_iota(jnp.int32, (128, NIDX), 1)
    rel = jnp.clip(KB - 1 - (CH * a + m), -128, 128) + 128
    return jnp.where(rel == idx, 1.0, 0.0).astype(F32)


def _onehot_diagonal():
    r = lax.broadcasted_iota(jnp.int32, (CH * CH, 128), 0)
    m = lax.broadcasted_iota(jnp.int32, (CH * CH, 128), 1)
    return jnp.where((r % CH) - (r // CH) + (CH - 1) == m, 1.0, 0.0).astype(F32)


def _bias_expand(name, rb):
    def body(rb_ref, o_ref):
        along = [lax.dot_general(rb_ref[...], _onehot_table(a), NT_DIMS, preferred_element_type=F32,
                                 precision=HIGHEST) for a in range(NBAND)]
        o_ref[...] = lax.dot_general(jnp.concatenate(along, axis=0), _onehot_diagonal(), NT_DIMS,
                                     preferred_element_type=F32, precision=HIGHEST)

    return pl.pallas_call(
        body, name=name, grid=(DEPTH,),
        in_specs=[pl.BlockSpec((None, 8, NIDX), lambda l: (l, 0, 0))],
        out_specs=pl.BlockSpec((None, NBAND * 8, CH * CH), lambda l: (l, 0, 0)),
        out_shape=jax.ShapeDtypeStruct((DEPTH, NBAND * 8, CH * CH), F32),
        compiler_params=_cp("parallel"),
    )(rb)


def _bias_reduce(name, db):
    def body(db_ref, o_ref):
        along = jnp.dot(db_ref[...], _onehot_diagonal(), preferred_element_type=F32, precision=HIGHEST)
        acc = jnp.zeros((8, NIDX), F32)
        for a in range(NBAND):
            acc = acc + jnp.dot(along[8 * a:8 * a + 8, :], _onehot_table(a), preferred_element_type=F32,
                                precision=HIGHEST)
        o_ref[...] = acc

    return pl.pallas_call(
        body, name=name, grid=(DEPTH,),
        in_specs=[pl.BlockSpec((None, NBAND * 8, CH * CH), lambda l: (l, 0, 0))],
        out_specs=pl.BlockSpec((None, 8, NIDX), lambda l: (l, 0, 0)),
        out_shape=jax.ShapeDtypeStruct((DEPTH, 8, NIDX), F32),
        compiler_params=_cp("parallel"),
    )(db)


def _bias_layout(flat):
    b = flat.reshape(DEPTH, NBAND, 8, CH, CH).transpose(0, 2, 1, 4, 3).reshape(DEPTH, 4, 2, KB, CH)
    pair = b.transpose(0, 1, 3, 2, 4).reshape(DEPTH, 4, KB, 128)
    first = jnp.pad(pair, ((0, 0), (0, 0), (0, CH), (0, 0)), constant_values=NEG_INF)
    second = jnp.pad(pair, ((0, 0), (0, 0), (CH, 0), (0, 0)), constant_values=NEG_INF)
    return jnp.concatenate([first, second], axis=3)


def _bias_unlayout(dbt):
    b = dbt.reshape(DEPTH, 4, NBAND, CH, 2, CH)
    return b.transpose(0, 2, 1, 4, 5, 3).reshape(DEPTH, NBAND * 8, CH * CH)


UNIT = 2 * CH
BAND2 = KB + CH


def _pair_weights(xt):
    x = xt.astype(F32)
    row = lax.broadcasted_iota(jnp.int32, (128, UNIT), 0)
    low = lax.broadcasted_iota(jnp.int32, (128, UNIT), 1) < HD
    swapped = pltpu.roll(x, HD, 1)
    same = (row < HD) == low
    first = jnp.where(same, jnp.where(low, x, swapped), 0.0)
    second = jnp.where(same, jnp.where(low, swapped, x), 0.0)
    return jnp.concatenate([first, second], axis=1).astype(BF16)


def _pair_rows(x):
    low = lax.broadcasted_iota(jnp.int32, (CH, 128), 1) < HD
    zero = jnp.zeros((CH, 128), x.dtype)
    parts = []
    for c in range(2):
        xc = x[c * CH:(c + 1) * CH, :]
        parts += [jnp.where(low, xc, zero), jnp.where(low, zero, xc)]
    return jnp.concatenate(parts, axis=0)


def _unpair(raw):
    b0, b1 = raw[:, 0:128], raw[:, 128:256]
    row = lax.broadcasted_iota(jnp.int32, (128, 128), 0)
    low = lax.broadcasted_iota(jnp.int32, (128, 128), 1) < HD
    top = jnp.where(low, b0, pltpu.roll(b1, HD, 1))
    bottom = jnp.where(low, pltpu.roll(b0, HD, 1), b1)
    return jnp.where(row < HD, top, bottom).T


def _scores_t(kb, qw, bias2, row0, padded):
    s = jnp.dot(kb, qw, preferred_element_type=F32) + bias2
    if padded:
        s = jnp.where(row0 + lax.broadcasted_iota(jnp.int32, (BAND2, 256), 0) >= PADR, s, NEG_INF)
    return s


def _unit_loops(s, unit):
    lax.fori_loop(0, PADR // UNIT, lambda u, c: unit(u, True, c), 0, unroll=4)
    lax.fori_loop(PADR // UNIT, s // UNIT, lambda u, c: unit(u, False, c), 0, unroll=7)


def _attn_fwd(name, kp, qt, vt, bias2, l):
    s = qt.shape[1]
    nu = s // UNIT

    def body(k_ref, qt_ref, vt_ref, b_ref, o_ref, lse_ref):
        def unit(u, padded, carry):
            r0 = pl.multiple_of(u * UNIT, UNIT)
            sc = _scores_t(k_ref[pl.ds(r0, BAND2), :], _pair_weights(qt_ref[:, pl.ds(r0, UNIT)]), b_ref[...],
                           r0, padded)
            top = jnp.max(sc, axis=0, keepdims=True)
            e = jnp.exp(sc - top)
            total = jnp.sum(e, axis=0, keepdims=True)
            raw = jnp.dot(vt_ref[:, pl.ds(r0, BAND2)], e.astype(BF16), preferred_element_type=F32) * (1.0 / total)
            o_ref[pl.ds(r0, UNIT), :] = _unpair(raw).astype(BF16)
            lse_ref[u] = jnp.broadcast_to(top + jnp.log(total), (8, 256))
            return carry

        _unit_loops(s, unit)

    return pl.pallas_call(
        body, name=name, grid=(AW // 128,),
        in_specs=[pl.BlockSpec((s + PADR, 128), lambda h: (0, h)),
                  pl.BlockSpec((128, s), lambda h: (h, 0)),
                  pl.BlockSpec((128, s + PADR), lambda h: (h, 0)),
                  pl.BlockSpec((None, None, BAND2, 256), lambda h: (l, h, 0, 0))],
        out_specs=[pl.BlockSpec((s, 128), lambda h: (0, h)),
                   pl.BlockSpec((None, nu, 8, 256), lambda h: (h, 0, 0, 0))],
        out_shape=[jax.ShapeDtypeStruct((s, AW), BF16), jax.ShapeDtypeStruct((4, nu, 8, 256), F32)],
        compiler_params=_cp("parallel"),
    )(kp, qt, vt, bias2)


def _attn_bwd(name, q, qt, kp, kt, vp, bias2, l, do, dot, lse, dl, db_all):
    s = q.shape[0]
    nu = s // UNIT

    def body(q_ref, qt_ref, k_ref, kt_ref, v_ref, b_ref, do_ref, dot_ref, lse_ref, dl_ref, dbin_ref,
             dq_ref, dk_ref, dv_ref, db_ref):
        del dbin_ref
        dk_ref[...] = jnp.zeros_like(dk_ref)
        dv_ref[...] = jnp.zeros_like(dv_ref)
        db_ref[...] = jnp.zeros_like(db_ref)

        def unit(u, padded, carry):
            r0 = pl.multiple_of(u * UNIT, UNIT)
            rows, band = pl.ds(r0, UNIT), pl.ds(r0, BAND2)
            sc = _scores_t(k_ref[band, :], _pair_weights(qt_ref[:, rows]), b_ref[...], r0, padded)
            pt = jnp.exp(sc - lse_ref[u][0:1, :])
            dpt = jnp.dot(v_ref[band, :], _pair_weights(dot_ref[:, rows]), preferred_element_type=F32)
            ds = pt * (dpt - dl_ref[u][0:1, :])
            db_ref[...] += ds[0:KB, 0:128] + ds[CH:BAND2, 128:256]
            dsb = ds.astype(BF16)
            dq_ref[rows, :] = _unpair(jnp.dot(kt_ref[:, band], dsb, preferred_element_type=F32))
            dk_ref[band, :] += jnp.dot(dsb, _pair_rows(q_ref[rows, :]), preferred_element_type=F32)
            dv_ref[band, :] += jnp.dot(pt.astype(BF16), _pair_rows(do_ref[rows, :]), preferred_element_type=F32)
            return carry

        _unit_loops(s, unit)

    row_q = pl.BlockSpec((s, 128), lambda h: (0, h))
    col_q = pl.BlockSpec((128, s), lambda h: (h, 0))
    row_k = pl.BlockSpec((s + PADR, 128), lambda h: (0, h))
    col_k = pl.BlockSpec((128, s + PADR), lambda h: (h, 0))
    stat = pl.BlockSpec((None, nu, 8, 256), lambda h: (h, 0, 0, 0))
    return pl.pallas_call(
        body, name=name, grid=(AW // 128,),
        in_specs=[row_q, col_q, row_k, col_k, row_k,
                  pl.BlockSpec((None, None, BAND2, 256), lambda h: (l, h, 0, 0)), row_q, col_q, stat, stat, ANY],
        out_specs=[row_q, row_k, row_k, pl.BlockSpec((None, None, KB, 128), lambda h: (l, h, 0, 0))],
        out_shape=[jax.ShapeDtypeStruct((s, AW), F32),
                   jax.ShapeDtypeStruct((s + PADR, AW), F32),
                   jax.ShapeDtypeStruct((s + PADR, AW), F32),
                   jax.ShapeDtypeStruct((DEPTH, 4, KB, 128), F32)],
        input_output_aliases={10: 3},
        compiler_params=_cp("parallel"),
    )(q, qt, kp, kt, vp, bias2, do, dot, lse, dl, db_all)


def _rowsum_layout(dl, nu):
    d = dl[:, :8].reshape(nu, 2, CH, 4, 2)
    d = d.transpose(3, 0, 1, 4, 2).reshape(4, nu, 1, 256)
    return jnp.broadcast_to(d, (4, nu, 8, 256))


def _rows_before(cur, prev, k):
    row = lax.broadcasted_iota(jnp.int32, cur.shape, 0)
    return jnp.where(row >= k, pltpu.roll(cur, k, 0), pltpu.roll(prev, k, 0))


def _rows_after(cur, nxt, k):
    n = cur.shape[0]
    row = lax.broadcasted_iota(jnp.int32, cur.shape, 0)
    return jnp.where(row < n - k, pltpu.roll(cur, n - k, 0), pltpu.roll(nxt, n - k, 0))


def _pool_window_lanes():
    lg = lax.broadcasted_iota(jnp.int32, (1, PWD), 1) // 64
    return lg, jnp.where(lg == 0, 2.0, jnp.where(lg == 1, 4.0, jnp.where(lg == 2, 8.0, 16.0))).astype(F32)


def _pool_mean_minus_token(u, up, row0):
    lg, wv = _pool_window_lanes()
    sums = []
    c, p = u, up
    for k in (1, 2, 4, 8):
        c2 = c + _rows_before(c, p, k)
        p = p + pltpu.roll(p, k, 0)
        c = c2
        sums.append(c)
    win = jnp.where(lg == 0, sums[0], jnp.where(lg == 1, sums[1], jnp.where(lg == 2, sums[2], sums[3])))
    pos1 = (row0 + lax.broadcasted_iota(jnp.int32, u.shape, 0) + 1).astype(F32)
    cnt = jnp.minimum(pos1, wv)
    return win / cnt - u, cnt


def _conv_taps(z, zp, w0, w1, w2):
    z1 = _rows_before(z, zp, 1)
    z2 = _rows_before(z, zp, 2)
    return (w0 * z2 + w1 * z1) + w2 * z, z1, z2


CP_TM = 1024
HALO = 16


def _halo_before(tm, col):
    return pl.BlockSpec((HALO, CW), lambda i: (jnp.maximum(i * (tm // HALO) - 1, 0), col))


def _halo_after(tm, col, rows):
    return pl.BlockSpec((HALO, CW), lambda i: (jnp.minimum((i + 1) * (tm // HALO), rows // HALO - 1), col))


def _as_block_end(halo, tm):
    return jnp.concatenate([jnp.zeros((tm - HALO, halo.shape[1]), halo.dtype), halo], axis=0)


def _as_block_start(halo, tm):
    return jnp.concatenate([halo, jnp.zeros((tm - HALO, halo.shape[1]), halo.dtype)], axis=0)


def _convpool_fwd(name, p, o, cw, pwbd, ps):
    s = p.shape[0]
    tm = CP_TM
    nb = s // tm

    def body(gb_ref, gc_ref, hin_ref, u_ref, gcp_ref, hinp_ref, up_ref, o_ref, cw_ref, pw_ref, ps_ref, mix_ref):
        i = pl.program_id(0)
        has_prev = i > 0
        z = gc_ref[...] * hin_ref[...]
        zp = _as_block_end(jnp.where(has_prev, gcp_ref[...] * hinp_ref[...], 0.0), tm)
        y3, _, _ = _conv_taps(z, zp, cw_ref[0:1, :], cw_ref[1:2, :], cw_ref[2:3, :])
        m, _ = _pool_mean_minus_token(u_ref[...], _as_block_end(jnp.where(has_prev, up_ref[...], 0.0), tm), i * tm)
        yp = jnp.dot(m.astype(BF16), pw_ref[...].astype(BF16), preferred_element_type=F32) * ps_ref[...]
        mix_ref[:, 0:AW] = o_ref[...]
        mix_ref[:, AW:AW + CW] = (gb_ref[...] * y3).astype(BF16)
        mix_ref[:, AW + CW:D] = yp.astype(BF16)

    def cur(col):
        return pl.BlockSpec((tm, CW), lambda i: (i, col))

    def whole(a):
        return pl.BlockSpec(a.shape, lambda i: (0,) * a.ndim)

    return pl.pallas_call(
        body, name=name, grid=(nb,),
        in_specs=[cur(6), cur(7), cur(8), cur(9), _halo_before(tm, 7), _halo_before(tm, 8), _halo_before(tm, 9),
                  pl.BlockSpec((tm, AW), lambda i: (i, 0)), whole(cw), whole(pwbd), whole(ps)],
        out_specs=pl.BlockSpec((tm, D), lambda i: (i, 0)),
        out_shape=jax.ShapeDtypeStruct((s, D), BF16),
        compiler_params=_cp("parallel"),
    )(p, p, p, p, p, p, p, o, cw, pwbd, ps)


def _convpool_bwd(name, p, dmix, cw, pwbd, ps):
    s = p.shape[0]
    tm = CP_TM // 2
    nb = s // tm

    def body(gb_ref, gc_ref, hin_ref, u_ref, gcp_ref, hinp_ref, up_ref, gbn_ref, dyc_ref, dyp_ref, dycn_ref, dypn_ref,
             cw_ref, pw_ref, ps_ref, dcp_ref, dw0_ref, dw1_ref, dw2_ref, dps_ref, dpw_ref):
        i = pl.program_id(0)
        has_prev = i > 0
        has_next = i < nb - 1
        w0, w1, w2 = cw_ref[0:1, :], cw_ref[1:2, :], cw_ref[2:3, :]
        gb, gc, hin = gb_ref[...], gc_ref[...], hin_ref[...]
        dyc = dyc_ref[...]
        z = gc * hin
        zp = _as_block_end(jnp.where(has_prev, gcp_ref[...] * hinp_ref[...], 0.0), tm)
        y3, z1, z2 = _conv_taps(z, zp, w0, w1, w2)
        dy3 = dyc * gb
        dy3n = _as_block_start(jnp.where(has_next, dycn_ref[...] * gbn_ref[...], 0.0), tm)
        dz = w2 * dy3 + w1 * _rows_after(dy3, dy3n, 1) + w0 * _rows_after(dy3, dy3n, 2)
        pw = pw_ref[...].astype(BF16)
        psv = ps_ref[...]
        m, cnt = _pool_mean_minus_token(u_ref[...], _as_block_end(jnp.where(has_prev, up_ref[...], 0.0), tm), i * tm)
        mb = m.astype(BF16)
        dyp = dyp_ref[...]
        dmp = (dyp * psv).astype(BF16)
        dmpn = jnp.where(has_next, dypn_ref[...] * psv, 0.0).astype(BF16)
        nt = (((1,), (1,)), ((), ()))
        dm = lax.dot_general(dmp, pw, nt, preferred_element_type=F32)
        dmn = lax.dot_general(dmpn, pw, nt, preferred_element_type=F32)
        lg, wv = _pool_window_lanes()
        cc, cn = dm / cnt, _as_block_start(dmn / wv, tm)
        sums = []
        for k in (1, 2, 4, 8):
            c2 = cc + _rows_after(cc, cn, k)
            cn = cn + pltpu.roll(cn, tm - k, 0)
            cc = c2
            sums.append(cc)
        du = jnp.where(lg == 0, sums[0], jnp.where(lg == 1, sums[1], jnp.where(lg == 2, sums[2], sums[3]))) - dm
        dcp_ref[:, 0:CW] = (dyc * y3).astype(BF16)
        dcp_ref[:, CW:2 * CW] = (dz * hin).astype(BF16)
        dcp_ref[:, 2 * CW:3 * CW] = (dz * gc).astype(BF16)
        dcp_ref[:, 3 * CW:4 * CW] = du.astype(BF16)
        parts = (jnp.sum(dy3 * z2, axis=0, keepdims=True),
                 jnp.sum(dy3 * z1, axis=0, keepdims=True),
                 jnp.sum(dy3 * z, axis=0, keepdims=True),
                 jnp.sum(dyp * jnp.dot(mb, pw, preferred_element_type=F32), axis=0, keepdims=True),
                 lax.dot_general(mb, dmp, (((0,), (0,)), ((), ())), preferred_element_type=F32))
        accs = (dw0_ref, dw1_ref, dw2_ref, dps_ref, dpw_ref)

        @pl.when(i == 0)
        def _():
            for a, v in zip(accs, parts):
                a[...] = v

        @pl.when(i > 0)
        def _():
            for a, v in zip(accs, parts):
                a[...] += v

    def cur(col):
        return pl.BlockSpec((tm, CW), lambda i: (i, col))

    def prev(col):
        return _halo_before(tm, col)

    def nxt(col):
        return _halo_after(tm, col, s)

    def whole(shape):
        return pl.BlockSpec(shape, lambda i: (0,) * len(shape))

    row = jax.ShapeDtypeStruct((1, CW), F32)
    return pl.pallas_call(
        body, name=name, grid=(nb,),
        in_specs=[cur(6), cur(7), cur(8), cur(9), prev(7), prev(8), prev(9), nxt(6),
                  cur(0), cur(1), nxt(0), nxt(1), whole(cw.shape), whole(pwbd.shape), whole(ps.shape)],
        out_specs=[pl.BlockSpec((tm, D), lambda i: (i, 0)), whole((1, CW)), whole((1, CW)), whole((1, CW)),
                   whole((1, PWD)), whole((PWD, PWD))],
        out_shape=[jax.ShapeDtypeStruct((s, D), BF16), row, row, row, row,
                   jax.ShapeDtypeStruct((PWD, PWD), F32)],
        compiler_params=_cp("arbitrary"),
    )(p, p, p, p, p, p, p, p, dmix, dmix, dmix, dmix, cw, pwbd, ps)


def _qkv_bwd(name, p, dq, dkp, dvp, dcp, qg, kg):
    s = p.shape[0]
    tm = 512
    off = PADR // tm

    def body(pq_ref, pk_ref, dq_ref, dk_ref, dv_ref, dcp_ref, qg_ref, kg_ref, dp_ref, dqg_ref, dkg_ref):
        i = pl.program_id(0)
        hm = _head_mean_matrix()

        def nrm_bwd(x, g, dy):
            r = lax.rsqrt(_head_mean(x * x, hm) + EPS)
            xn = x * r
            dxn = dy * g
            dx = r * (dxn - xn * _head_mean(dxn * xn, hm))
            dg = jnp.sum(dy * xn, axis=0, keepdims=True)
            dg = (dg[:, 0:128] + dg[:, 128:256]) + (dg[:, 256:384] + dg[:, 384:512])
            return dx, dg + pltpu.roll(dg, HD, 1)

        dxq, dgq = nrm_bwd(pq_ref[...], qg_ref[...], dq_ref[...] * 0.125)
        dxk, dgk = nrm_bwd(pk_ref[...], kg_ref[...], dk_ref[...])
        dp_ref[:, 0:AW] = dxq.astype(BF16)
        dp_ref[:, AW:2 * AW] = dxk.astype(BF16)
        dp_ref[:, 2 * AW:3 * AW] = dv_ref[...].astype(BF16)
        dp_ref[:, 3 * AW:DIN] = dcp_ref[...]

        @pl.when(i == 0)
        def _():
            dqg_ref[...] = dgq
            dkg_ref[...] = dgk

        @pl.when(i > 0)
        def _():
            dqg_ref[...] += dgq
            dkg_ref[...] += dgk

    gspec = pl.BlockSpec((1, AW), lambda i: (0, 0))
    gout = pl.BlockSpec((1, 128), lambda i: (0, 0))
    return pl.pallas_call(
        body, name=name, grid=(s // tm,),
        in_specs=[pl.BlockSpec((tm, AW), lambda i: (i, 0)), pl.BlockSpec((tm, AW), lambda i: (i, 1)),
                  pl.BlockSpec((tm, AW), lambda i: (i, 0)),
                  pl.BlockSpec((tm, AW), lambda i: (i + off, 0)),
                  pl.BlockSpec((tm, AW), lambda i: (i + off, 0)),
                  pl.BlockSpec((tm, D), lambda i: (i, 0)), gspec, gspec],
        out_specs=[pl.BlockSpec((tm, DIN), lambda i: (i, 0)), gout, gout],
        out_shape=[jax.ShapeDtypeStruct((s, DIN), BF16), jax.ShapeDtypeStruct((1, 128), F32),
                   jax.ShapeDtypeStruct((1, 128), F32)],
        compiler_params=_cp("arbitrary"),
    )(p, p, dq, dkp, dvp, dcp, qg, kg)


def _mm_nt_relu(name, dxb, w, l, a):
    s = dxb.shape[0]
    tm = MM_ROWS

    def body(d_ref, w_ref, a_ref, o_ref):
        df = lax.dot_general(d_ref[...], w_ref[...], NT_DIMS, preferred_element_type=F32)
        o_ref[...] = (df * (2.0 * jnp.maximum(a_ref[...].astype(F32), 0.0))).astype(BF16)

    return pl.pallas_call(
        body, name=name, grid=(s // tm,),
        in_specs=[pl.BlockSpec((tm, D), lambda i: (i, 0)),
                  pl.BlockSpec((None, DFF, D), lambda i: (l, 0, 0)),
                  pl.BlockSpec((tm, DFF), lambda i: (i, 0))],
        out_specs=pl.BlockSpec((tm, DFF), lambda i: (i, 0)),
        out_shape=jax.ShapeDtypeStruct((s, DFF), BF16),
        compiler_params=_cp("parallel"),
    )(dxb, w, a)


def _proj_out_bwd(name, dxb, w, l, mix):
    s = dxb.shape[0]
    tm = _mm_rows(D, D)

    def body(d_ref, w_ref, o_ref, do_ref, dot_ref, dcp_ref, dl_ref):
        d = d_ref[...]
        wa, wc = w_ref[0:AW, :], w_ref[AW:D, :]
        do = lax.dot_general(d, wa, NT_DIMS, preferred_element_type=F32)
        do_ref[...] = do.astype(BF16)
        dot_ref[...] = lax.dot_general(wa, d, NT_DIMS, preferred_element_type=F32).astype(BF16)
        dcp_ref[...] = lax.dot_general(d, wc, NT_DIMS, preferred_element_type=F32)
        head = lax.broadcasted_iota(jnp.int32, (AW, 128), 0) // HD
        pick = jnp.where(head == lax.broadcasted_iota(jnp.int32, (AW, 128), 1), 1.0, 0.0).astype(BF16)
        dl_ref[...] = _two_pass_dot(do * o_ref[...].astype(F32), pick)

    return pl.pallas_call(
        body, name=name, grid=(s // tm,),
        in_specs=[pl.BlockSpec((tm, D), lambda i: (i, 0)),
                  pl.BlockSpec((None, D, D), lambda i: (l, 0, 0)),
                  pl.BlockSpec((tm, AW), lambda i: (i, 0))],
        out_specs=[pl.BlockSpec((tm, AW), lambda i: (i, 0)), pl.BlockSpec((AW, tm), lambda i: (0, i)),
                   pl.BlockSpec((tm, D - AW), lambda i: (i, 0)), pl.BlockSpec((tm, 128), lambda i: (i, 0))],
        out_shape=[jax.ShapeDtypeStruct((s, AW), BF16), jax.ShapeDtypeStruct((AW, s), BF16),
                   jax.ShapeDtypeStruct((s, D - AW), F32), jax.ShapeDtypeStruct((s, 128), F32)],
        compiler_params=_cp("parallel"),
    )(dxb, w, mix)


def _mm_nt_normbwd(name, gy, w, l, x, g, dres, dep):
    s, k = gy.shape
    tm = MM_ROWS

    def body(gy_ref, w_ref, x_ref, g_ref, dr_ref, dep_ref, dx_ref, dxb_ref, dg_ref):
        del dep_ref
        i = pl.program_id(0)
        dh = lax.dot_general(gy_ref[...], w_ref[...], NT_DIMS, preferred_element_type=F32)
        xv = x_ref[...]
        r = _inv_rms(xv)
        xn = xv * r
        dxn = dh * g_ref[...]
        dx = r * (dxn - xn * jnp.mean(dxn * xn, axis=-1, keepdims=True)) + dr_ref[...]
        dx_ref[...] = dx
        dxb_ref[...] = dx.astype(BF16)
        part = jnp.sum(dh * xn, axis=0, keepdims=True)

        @pl.when(i == 0)
        def _():
            dg_ref[...] = part

        @pl.when(i > 0)
        def _():
            dg_ref[...] += part

    blk = pl.BlockSpec((tm, D), lambda i: (i, 0))
    vec = pl.BlockSpec((1, D), lambda i: (0, 0))
    return pl.pallas_call(
        body, name=name, grid=(s // tm,),
        in_specs=[pl.BlockSpec((tm, k), lambda i: (i, 0)),
                  pl.BlockSpec((None, D, k), lambda i: (l, 0, 0)), blk, vec, blk, ANY],
        out_specs=[blk, blk, vec],
        out_shape=[jax.ShapeDtypeStruct((s, D), F32), jax.ShapeDtypeStruct((s, D), BF16),
                   jax.ShapeDtypeStruct((1, D), F32)],
        compiler_params=_cp("arbitrary"),
    )(gy, w, x, g, dres, dep)


def _mm_tn(name, a, b, tma, tnb, relu2=False):
    s, m = a.shape
    n = b.shape[1]

    def body(a_ref, b_ref, o_ref):
        av = _relu2(a_ref[...]) if relu2 else a_ref[...]
        o_ref[...] = lax.dot_general(av, b_ref[...], (((0,), (0,)), ((), ())),
                                     preferred_element_type=F32).astype(BF16)

    return pl.pallas_call(
        body, name=name, grid=(m // tma, n // tnb),
        in_specs=[pl.BlockSpec((s, tma), lambda i, j: (0, i), pipeline_mode=pl.Buffered(1) if m == tma else None),
                  pl.BlockSpec((s, tnb), lambda i, j: (0, j))],
        out_specs=pl.BlockSpec((tma, tnb), lambda i, j: (i, j)),
        out_shape=jax.ShapeDtypeStruct((m, n), BF16),
        compiler_params=_cp("parallel", "parallel"),
    )(a, b)


def _adamw_math(gv, wv, mv, vv):
    mn = ADAM_B1 * mv + (1.0 - ADAM_B1) * gv
    vn = ADAM_B2 * vv + (1.0 - ADAM_B2) * jnp.square(gv)
    m_hat = mn / (1.0 - ADAM_B1 ** ADAM_STEP)
    v_hat = vn / (1.0 - ADAM_B2 ** ADAM_STEP)
    return gv, -ADAM_LR * (m_hat / (jnp.sqrt(v_hat) + ADAM_EPS) + ADAM_WD * wv), mn, vn


def _adamw(name, g, w, m, v):
    r, c = g.shape
    tm = 256 if r % 256 == 0 else r

    def body(g_ref, w_ref, m_ref, v_ref, go_ref, d_ref, mo_ref, vo_ref):
        go_ref[...], d_ref[...], mo_ref[...], vo_ref[...] = _adamw_math(g_ref[...], w_ref[...], m_ref[...], v_ref[...])

    blk = pl.BlockSpec((tm, c), lambda i: (i, 0))
    return pl.pallas_call(
        body, name=name, grid=(r // tm,),
        in_specs=[blk] * 4, out_specs=[blk] * 4,
        out_shape=[jax.ShapeDtypeStruct((r, c), F32)] * 4,
        compiler_params=_cp("parallel"),
    )(g, w, m, v)


def _place():
    x, y, c = lax.axis_index("x"), lax.axis_index("y"), lax.axis_index("c")
    chips = [(1 - x, y), (x, 1 - y), (1 - x, 1 - y)]
    return x, y, c, chips


BLOCK_AXIS = (2, 1, 2, 1)
LARGE_DIMS = ((D, DIN), (D, D), (D, DFF), (DFF, D))


def _full_shape(t, layers, dtype):
    r, c = LARGE_DIMS[t]
    return jax.ShapeDtypeStruct((layers, r, c), dtype)


def _cast_into_full(name, t, shard, b1, dep):
    _, r, c = shard.shape
    tm = min(512, r)
    if BLOCK_AXIS[t] == 1:
        out_spec = pl.BlockSpec((None, tm, c), lambda l, i, br: (l, br[0] * (r // tm) + i, 0))
    else:
        out_spec = pl.BlockSpec((None, tm, c), lambda l, i, br: (l, i, br[0]))

    def body(b_ref, x_ref, dep_ref, o_ref):
        del b_ref, dep_ref
        o_ref[...] = x_ref[...].astype(BF16)

    return pl.pallas_call(
        body, name=name,
        grid_spec=pltpu.PrefetchScalarGridSpec(
            num_scalar_prefetch=1, grid=(DEPTH, r // tm),
            in_specs=[pl.BlockSpec((None, tm, c), lambda l, i, br: (l, i, 0)), ANY],
            out_specs=out_spec),
        out_shape=_full_shape(t, DEPTH, BF16),
        compiler_params=_cp("parallel", "parallel"),
    )(b1, shard, dep)


HBM = pl.BlockSpec(memory_space=pltpu.HBM)
SEM = pl.BlockSpec(memory_space=pltpu.SEMAPHORE)
DATAFLOW = pltpu.SideEffectType.DATAFLOW_SIDE_EFFECTING


def _half(ref, l, t, b, c):
    r, cols = LARGE_DIMS[t]
    if BLOCK_AXIS[t] == 1:
        n = r // 8
        return ref.at[l, pl.ds(pl.multiple_of(b * (2 * n) + c * n, 16), n), :]
    n, w = r // 2, cols // 4
    return ref.at[l, pl.ds(pl.multiple_of(c * n, 16), n), pl.ds(pl.multiple_of(b * w, 128), w)]


def _gather_start(name, layers, ts, fulls):
    n = len(ts)

    def body(*refs):
        f_refs, sems = refs[n:2 * n], refs[2 * n:2 * n + 2 * len(layers)]
        x, y, c, chips = _place()
        for i, l in enumerate(layers):
            for k, t in enumerate(ts):
                own = _half(f_refs[k], l, t, 2 * x + y, c)
                for j, (cx, cy) in enumerate(chips):
                    pltpu.make_async_remote_copy(src_ref=own, dst_ref=own, send_sem=sems[2 * i].at[3 * t + j],
                                                 recv_sem=sems[2 * i + 1].at[3 * t + j], device_id=(cx, cy, c),
                                                 device_id_type=MESH).start()
        refs[-1][...] = jnp.zeros((8, 128), F32)

    outs = pl.pallas_call(
        body, name=name,
        in_specs=[HBM] * n,
        out_specs=[HBM] * n + [SEM] * (2 * len(layers)) + [pl.BlockSpec(memory_space=pltpu.VMEM)],
        out_shape=[pltpu.HBM(f.shape, f.dtype) for f in fulls]
        + [pltpu.SemaphoreType.DMA((12,))] * (2 * len(layers)) + [jax.ShapeDtypeStruct((8, 128), F32)],
        input_output_aliases={k: k for k in range(n)},
        compiler_params=pltpu.CompilerParams(has_side_effects=DATAFLOW),
    )(*[pltpu.with_memory_space_constraint(f, pltpu.HBM) for f in fulls])
    return outs[0:n], {l: (outs[n + 2 * i], outs[n + 1 + 2 * i]) for i, l in enumerate(layers)}, outs[-1]


def _gather_wait(name, l, ts, fulls, sems, after):
    def body(*refs):
        send_sems, recv_sems, f_refs = refs[4], refs[5], refs[7:11]
        x, y, c, chips = _place()
        for t in ts:
            own = _half(f_refs[t], l, t, 2 * x + y, c)
            for j, (cx, cy) in enumerate(chips):
                landed = _half(f_refs[t], l, t, 2 * cx + cy, c)
                pltpu.make_async_remote_copy(src_ref=own, dst_ref=landed, send_sem=send_sems.at[3 * t + j],
                                             recv_sem=recv_sems.at[3 * t + j], device_id=(cx, cy, c),
                                             device_id_type=MESH).wait()

    return pl.pallas_call(
        body, name=name,
        in_specs=[HBM] * 4 + [SEM, SEM, ANY], out_specs=[HBM] * 4,
        out_shape=[pltpu.HBM(s.shape, s.dtype) for s in (_full_shape(t, DEPTH, BF16) for t in range(4))],
        input_output_aliases={t: t for t in range(4)},
        compiler_params=pltpu.CompilerParams(has_side_effects=DATAFLOW),
    )(*fulls, sems[0], sems[1], after)


def _pass_on(name, l, ts, fulls):
    def body(*refs):
        f_refs, send_sems, recv_sems = refs[4:8], refs[8], refs[9]
        x, y, c, chips = _place()

        def copy(t, j, half):
            cx, cy = chips[j]
            part = _half(f_refs[t], l, t, 2 * cx + cy, half)
            return pltpu.make_async_remote_copy(src_ref=part, dst_ref=part, send_sem=send_sems.at[3 * t + j],
                                                recv_sem=recv_sems.at[3 * t + j], device_id=(x, y, 1 - c),
                                                device_id_type=MESH)

        for t in ts:
            for j in range(3):
                copy(t, j, c).start()
        for t in ts:
            for j in range(3):
                copy(t, j, 1 - c).wait_recv()
                copy(t, j, c).wait_send()

    return pl.pallas_call(
        body, name=name,
        in_specs=[ANY] * 4, out_specs=[ANY] * 4,
        out_shape=[_full_shape(t, DEPTH, BF16) for t in range(4)],
        input_output_aliases={t: t for t in range(4)},
        scratch_shapes=[pltpu.SemaphoreType.DMA((12,)), pltpu.SemaphoreType.DMA((12,))],
    )(*fulls)


def _block2d(ref, t, b):
    r, cols = LARGE_DIMS[t]
    if BLOCK_AXIS[t] == 1:
        return ref.at[pl.ds(pl.multiple_of(b * (r // 4), 16), r // 4), :]
    return ref.at[:, pl.ds(pl.multiple_of(b * (cols // 4), 128), cols // 4)]


def _block_dims(t):
    r, cols = LARGE_DIMS[t]
    return (r // 4, cols) if BLOCK_AXIS[t] == 1 else (r, cols // 4)


def _reduce_copies(ts, g_refs, r_refs, send_sems, recv_sems):
    _, _, c, chips = _place()
    return [pltpu.make_async_remote_copy(src_ref=_block2d(g_refs[i], t, 2 * cx + cy), dst_ref=r_refs[i].at[j],
                                         send_sem=send_sems.at[3 * i + j], recv_sem=recv_sems.at[3 * i + j],
                                         device_id=(cx, cy, c), device_id_type=MESH)
            for i, t in enumerate(ts) for j, (cx, cy) in enumerate(chips)]


def _reduce_start(name, ts, grads):
    n = len(ts)

    def body(*refs):
        for cp in _reduce_copies(ts, refs[n:2 * n], refs[2 * n:3 * n], refs[3 * n], refs[3 * n + 1]):
            cp.start()
        refs[3 * n + 2][...] = jnp.zeros((8, 128), F32)

    outs = pl.pallas_call(
        body, name=name,
        in_specs=[HBM] * n,
        out_specs=[HBM] * (2 * n) + [SEM, SEM, pl.BlockSpec(memory_space=pltpu.VMEM)],
        out_shape=[pltpu.HBM(g.shape, BF16) for g in grads]
        + [pltpu.HBM((3,) + _block_dims(t), BF16) for t in ts]
        + [pltpu.SemaphoreType.DMA((3 * n,)), pltpu.SemaphoreType.DMA((3 * n,)), jax.ShapeDtypeStruct((8, 128), F32)],
        input_output_aliases={i: i for i in range(n)},
        compiler_params=pltpu.CompilerParams(has_side_effects=DATAFLOW),
    )(*[pltpu.with_memory_space_constraint(g, pltpu.HBM) for g in grads])
    return outs[0:n], outs[n:2 * n], (outs[2 * n], outs[2 * n + 1]), outs[2 * n + 2]


def _reduce_wait(name, ts, grads, landing, sems, afters):
    n = len(ts)
    first_out = 2 * n + 2 + len(afters)

    def body(*refs):
        for cp in _reduce_copies(ts, refs[first_out:first_out + n], refs[first_out + n:first_out + 2 * n],
                                 refs[2 * n], refs[2 * n + 1]):
            cp.wait()

    outs = pl.pallas_call(
        body, name=name,
        in_specs=[HBM] * (2 * n) + [SEM, SEM] + [ANY] * len(afters), out_specs=[HBM] * (2 * n),
        out_shape=[pltpu.HBM(g.shape, BF16) for g in grads] + [pltpu.HBM(r.shape, BF16) for r in landing],
        input_output_aliases={i: i for i in range(2 * n)},
        compiler_params=pltpu.CompilerParams(has_side_effects=DATAFLOW),
    )(*grads, *landing, sems[0], sems[1], *afters)
    return outs[0:n], outs[n:2 * n]


def _add4(name, t, own, landed, b1):
    rb, cb = _block_dims(t)
    tm = min(512, rb)
    if BLOCK_AXIS[t] == 1:
        own_spec = pl.BlockSpec((tm, cb), lambda i, br: (br[0] * (rb // tm) + i, 0))
    else:
        own_spec = pl.BlockSpec((tm, cb), lambda i, br: (i, br[0]))

    def body(b_ref, o_ref, r0_ref, r1_ref, r2_ref, s_ref):
        del b_ref
        s_ref[...] = ((o_ref[...].astype(F32) + r0_ref[...].astype(F32))
                      + (r1_ref[...].astype(F32) + r2_ref[...].astype(F32))).astype(BF16)

    def got(j):
        return pl.BlockSpec((None, tm, cb), lambda i, br: (j, i, 0))

    return pl.pallas_call(
        body, name=name,
        grid_spec=pltpu.PrefetchScalarGridSpec(
            num_scalar_prefetch=1, grid=(rb // tm,),
            in_specs=[own_spec, got(0), got(1), got(2)],
            out_specs=pl.BlockSpec((tm, cb), lambda i, br: (i, 0))),
        out_shape=jax.ShapeDtypeStruct((rb, cb), BF16),
        compiler_params=_cp("parallel"),
    )(b1, own, landed, landed, landed)


def _swap_sib(name, sums):
    def body(*refs):
        s_refs, t_refs, send_sems, recv_sems = refs[0:4], refs[4:8], refs[8], refs[9]
        x, y, c, _ = _place()
        cps = [pltpu.make_async_remote_copy(src_ref=s_refs[t], dst_ref=t_refs[t], send_sem=send_sems.at[t],
                                            recv_sem=recv_sems.at[t], device_id=(x, y, 1 - c), device_id_type=MESH)
               for t in range(4)]
        for cp in cps:
            cp.start()
        for cp in cps:
            cp.wait()

    return pl.pallas_call(
        body, name=name,
        in_specs=[ANY] * 4, out_specs=[ANY] * 4,
        out_shape=[jax.ShapeDtypeStruct(s.shape, BF16) for s in sums],
        scratch_shapes=[pltpu.SemaphoreType.DMA((4,)), pltpu.SemaphoreType.DMA((4,))],
    )(*sums)


def _adamw_pair(name, l, s_own, s_sib, w, m, v, outs):
    rb, cb = s_own.shape
    tm = min(512, rb)

    def body(a_ref, b_ref, w_ref, m_ref, v_ref, g0, d0, m0, v0, go_ref, d_ref, mo_ref, vo_ref):
        del g0, d0, m0, v0
        gv = a_ref[...].astype(F32) + b_ref[...].astype(F32)
        go_ref[...], d_ref[...], mo_ref[...], vo_ref[...] = _adamw_math(gv, w_ref[...], m_ref[...], v_ref[...])

    part = pl.BlockSpec((tm, cb), lambda i: (i, 0))
    layer = pl.BlockSpec((None, tm, cb), lambda i: (l, i, 0))
    return pl.pallas_call(
        body, name=name, grid=(rb // tm,),
        in_specs=[part, part, layer, layer, layer] + [ANY] * 4,
        out_specs=[layer] * 4,
        out_shape=[jax.ShapeDtypeStruct((DEPTH, rb, cb), F32)] * 4,
        input_output_aliases={5 + i: i for i in range(4)},
        compiler_params=_cp("parallel"),
    )(s_own, s_sib, w, m, v, *outs)


def _all_gather8(name, v, dep):
    m_per, n = v.shape

    def body(v_ref, dep_ref, out_ref, send_sems, recv_sems, local_sem):
        del dep_ref
        x, y, c, chips = _place()
        me, sib = (x, y, c), (x, y, 1 - c)

        def rows(px, py, pc):
            return out_ref.at[pl.ds((4 * px + 2 * py + pc) * m_per, m_per), :]

        def copy(k, block, to, src=None):
            return pltpu.make_async_remote_copy(
                src_ref=rows(*block) if src is None else src, dst_ref=rows(*block),
                send_sem=send_sems.at[k], recv_sem=recv_sems.at[k], device_id=to, device_id_type=MESH)

        mine = pltpu.make_async_copy(v_ref, rows(*me), local_sem)
        mine.start()
        first = [copy(0, me, sib, src=v_ref)]
        first += [copy(1 + j, me, (*chip, c), src=v_ref) for j, chip in enumerate(chips)]
        for cp in first:
            cp.start()
        passed = [copy(4 + j, (*chip, c), sib) for j, chip in enumerate(chips)]
        for j, chip in enumerate(chips):
            copy(1 + j, (*chip, c), me).wait_recv()
            passed[j].start()
        copy(0, sib, me).wait_recv()
        for j, chip in enumerate(chips):
            copy(4 + j, (*chip, 1 - c), me).wait_recv()
        for cp in first + passed:
            cp.wait_send()
        mine.wait()

    return pl.pallas_call(
        body, name=name,
        out_shape=jax.ShapeDtypeStruct((8 * m_per, n), v.dtype),
        in_specs=[pl.BlockSpec(memory_space=pltpu.VMEM), ANY],
        out_specs=pl.BlockSpec(memory_space=pltpu.VMEM),
        scratch_shapes=[pltpu.SemaphoreType.DMA((7,)), pltpu.SemaphoreType.DMA((7,)), pltpu.SemaphoreType.DMA],
    )(v, dep)


def _sum8(name, g):
    def body(g_ref, o_ref):
        acc = g_ref[0]
        for d in range(1, 8):
            acc = acc + g_ref[d]
        o_ref[...] = acc

    return pl.pallas_call(body, name=name, out_shape=jax.ShapeDtypeStruct(g.shape[1:], F32))(g)


def _pack(parts):
    flat = []
    for a in parts:
        a = a.reshape(-1)
        flat.append(jnp.pad(a, (0, (-a.shape[0]) % 128)))
    cat = jnp.concatenate(flat)
    cat = jnp.pad(cat, (0, (-cat.shape[0]) % 1024))
    return cat.reshape(-1, 128)


def _unpack(packed, shapes):
    flat = packed.reshape(-1)
    out, at = [], 0
    for shp in shapes:
        n = 1
        for d in shp:
            n *= d
        out.append(flat[at:at + n].reshape(shp))
        at += n + (-n) % 128
    return out


def _local_step(x, target, layer_weights, on_grads, small):
    qg_all = jnp.tile(small["q_norm_g"], (1, 8))
    kg_all = jnp.tile(small["k_norm_g"], (1, 8))
    bias_all = _bias_layout(_bias_expand("bias_expand", jnp.pad(small["rel_bias"], ((0, 0), (0, 0), (0, NIDX - 257)))))
    same_group = jnp.eye(4, dtype=F32)[None, :, None, :, None]
    pwbd_all = (small["pool_w"][:, :, :, None, :] * same_group).reshape(DEPTH, PWD, PWD)
    saved = []
    xin = x
    h = _rmsnorm("norm_first", x, small["norm1_g"][0:1])
    for l in range(DEPTH):
        w_in = layer_weights(l, (0,), xin)[0]
        qg, kg = qg_all[l:l + 1], kg_all[l:l + 1]
        cw, pwbd, ps = small["conv_w"][l], pwbd_all[l], small["pool_scale"][l:l + 1]
        p = _mm_nn(f"proj_in_{l}", h, w_in, l, F32)
        q, qt, kp, kt, vp, vt = _qkv(f"qkv_{l}", p, qg, kg)
        o, lse = _attn_fwd(f"attn_fwd_{l}", kp, qt, vt, bias_all, l)
        w_in, w_out, w_1, w_2 = layer_weights(l, (1, 2, 3), o)
        mix = _convpool_fwd(f"convpool_fwd_{l}", p, o, cw, pwbd, ps)
        x1, h2 = _mm_res_norm(f"proj_out_{l}", mix, w_out, l, xin, small["norm2_g"][l:l + 1])
        saved.append(dict(xin=xin, h=h, p=p, q=q, qt=qt, kp=kp, kt=kt, vp=vp, mix=mix, x1=x1, h2=h2, lse=lse,
                          qg=qg, kg=kg, cw=cw, pwbd=pwbd, ps=ps))
        if l + 1 < DEPTH:
            saved[l]["a"], xin, h = _mlp_fwd(f"mlp_{l}", h2, w_1, w_2, l, x1, small["norm1_g"][l + 1:l + 2])
        else:
            saved[l]["a"], dx, dxb, loss = _mlp_fwd(f"mlp_{l}", h2, w_1, w_2, l, x1, target)

    raw = {k: [None] * DEPTH for k in ("dg1", "dqg", "dkg", "dw0", "dw1", "dw2", "dpw", "dps", "dg2")}
    db_all = lax.empty((DEPTH, 4, KB, 128), F32)
    for l in reversed(range(DEPTH)):
        sv = saved[l]
        da = _mm_nt_relu(f"mlp2_bwd_{l}", dxb, w_2, l, sv["a"])
        g_2 = _mm_tn(f"mlp2_wgrad_{l}", sv["a"], dxb, 512, 1024, relu2=True)
        g_1 = _mm_tn(f"mlp1_wgrad_{l}", sv["h2"], da, 1024, 512)
        dep = on_grads(l, (2, 3), (g_1, g_2))
        dx1, dx1b, dg2 = _mm_nt_normbwd(f"mlp1_bwd_{l}", da, w_1, l, sv["x1"], small["norm2_g"][l:l + 1], dx, dep)
        do, dot, dmix, dl = _proj_out_bwd(f"proj_out_bwd_{l}", dx1b, w_out, l, sv["mix"])
        g_out = _mm_tn(f"proj_out_wgrad_{l}", sv["mix"], dx1b, 512, 1024)
        dcp, dw0, dw1, dw2, dps, dpw = _convpool_bwd(f"convpool_bwd_{l}", sv["p"], dmix, sv["cw"], sv["pwbd"], sv["ps"])
        dq, dkp, dvp, db_all = _attn_bwd(f"attn_bwd_{l}", sv["q"], sv["qt"], sv["kp"], sv["kt"], sv["vp"], bias_all, l,
                                     do, dot, sv["lse"], _rowsum_layout(dl, x.shape[0] // UNIT), db_all)
        dp, dqg, dkg = _qkv_bwd(f"qkv_bwd_{l}", sv["p"], dq, dkp, dvp, dcp, sv["qg"], sv["kg"])
        g_in = _mm_tn(f"proj_in_wgrad_{l}", sv["h"], dp, 1024, 1280)
        dep = on_grads(l, (0, 1), (g_in, g_out))
        dx, dxb, dg1 = _mm_nt_normbwd(f"proj_in_bwd_{l}", dp, w_in, l, sv["xin"], small["norm1_g"][l:l + 1], dx1, dep)
        for k, val in dict(dg1=dg1, dqg=dqg, dkg=dkg, dw0=dw0, dw1=dw1, dw2=dw2, dpw=dpw, dps=dps, dg2=dg2).items():
            raw[k][l] = val
    cat = {k: jnp.concatenate(v, axis=0) for k, v in raw.items() if k != "dpw"}
    drb = _bias_reduce("bias_reduce", _bias_unlayout(db_all))
    dpw = jnp.stack(raw["dpw"])
    gsmall = {
        "norm1_g": cat["dg1"], "q_norm_g": cat["dqg"][:, :HD], "k_norm_g": cat["dkg"][:, :HD],
        "rel_bias": drb[:, :, :257],
        "conv_w": jnp.stack([cat["dw0"], cat["dw1"], cat["dw2"]], axis=1),
        "pool_w": jnp.stack([dpw[:, g * 64:(g + 1) * 64, g * 64:(g + 1) * 64] for g in range(4)], axis=1),
        "pool_scale": cat["dps"], "norm2_g": cat["dg2"],
    }
    return loss, dx, gsmall


SMALL = ("norm1_g", "q_norm_g", "k_norm_g", "rel_bias", "conv_w", "pool_w", "pool_scale", "norm2_g")
LARGE = ("w_in", "w_out", "w_mlp1", "w_mlp2")


def kernel(x, norm1_g, w_in, q_norm_g, k_norm_g, rel_bias, conv_w, pool_w, pool_scale, w_out, norm2_g, w_mlp1, w_mlp2, loss_target, m_norm1_g, m_w_in, m_q_norm_g, m_k_norm_g, m_rel_bias, m_conv_w, m_pool_w, m_pool_scale, m_w_out, m_norm2_g, m_w_mlp1, m_w_mlp2, v_norm1_g, v_w_in, v_q_norm_g, v_k_norm_g, v_rel_bias, v_conv_w, v_pool_w, v_pool_scale, v_w_out, v_norm2_g, v_w_mlp1, v_w_mlp2):
    w = dict(norm1_g=norm1_g, w_in=w_in, q_norm_g=q_norm_g, k_norm_g=k_norm_g, rel_bias=rel_bias, conv_w=conv_w,
             pool_w=pool_w, pool_scale=pool_scale, w_out=w_out, norm2_g=norm2_g, w_mlp1=w_mlp1, w_mlp2=w_mlp2)
    m = dict(norm1_g=m_norm1_g, w_in=m_w_in, q_norm_g=m_q_norm_g, k_norm_g=m_k_norm_g, rel_bias=m_rel_bias,
             conv_w=m_conv_w, pool_w=m_pool_w, pool_scale=m_pool_scale, w_out=m_w_out, norm2_g=m_norm2_g,
             w_mlp1=m_w_mlp1, w_mlp2=m_w_mlp2)
    v = dict(norm1_g=v_norm1_g, w_in=v_w_in, q_norm_g=v_q_norm_g, k_norm_g=v_k_norm_g, rel_bias=v_rel_bias,
             conv_w=v_conv_w, pool_w=v_pool_w, pool_scale=v_pool_scale, w_out=v_w_out, norm2_g=v_norm2_g,
             w_mlp1=v_w_mlp1, w_mlp2=v_w_mlp2)
    ax, ay, ac = lax.axis_index("x"), lax.axis_index("y"), lax.axis_index("c")
    b1 = jnp.reshape(2 * ax + ay, (1,)).astype(jnp.int32)

    cw_rows = _all_gather8("gather_conv_w", jnp.pad(conv_w.reshape(DEPTH * 3, 64), ((0, 4), (0, 64))), b1)
    cw_chips = [cw_rows[(4 * cx + 2 * cy) * 16:(4 * cx + 2 * cy) * 16 + 12, :64] for cx in range(2) for cy in range(2)]
    small = {n: w[n] for n in SMALL}
    small["conv_w"] = jnp.concatenate(cw_chips, axis=1).reshape(DEPTH, 3, CW)

    (w_in_full,), in_sems, in_token = _gather_start(
        "gather_start_in", (0,), (0,), [_cast_into_full("cast_w_in", 0, w["w_in"], b1, cw_rows)])
    others, first_sems, first_token = _gather_start(
        "gather_start_first", (0,), (1, 2, 3),
        [_cast_into_full(f"cast_{LARGE[t]}", t, w[LARGE[t]], b1, in_token) for t in (1, 2, 3)])
    held = [[w_in_full] + list(others)]
    sems = {(0, 0): in_sems[0], (0, 1): first_sems[0]}

    def layer_weights(l, ts, after):
        if l > 0:
            ts = (0, 1, 2, 3) if ts == (0,) else ()
        if ts:
            tag = f"{l}_{ts[0]}"
            first_in = l == 0 and ts == (0,)
            after = first_token if first_in else after
            arrived = _gather_wait(f"gather_wait_{tag}", l, ts, held[0], sems[l, ts[0] if l == 0 else 0], after)
            if first_in:
                arrived, rest_sems, _ = _gather_start("gather_start_rest", tuple(range(1, DEPTH)), (0, 1, 2, 3),
                                                      arrived)
                sems.update({(k, 0): v for k, v in rest_sems.items()})
            held[0] = _pass_on(f"pass_on_{tag}", l, ts, arrived)
        return held[0]

    flights = {}

    def await_flight(l, ts, afters):
        g, landing, sm, _ = flights[l, ts]
        flights[l, ts] = _reduce_wait(f"reduce_wait_{l}_{ts[0]}", ts, g, landing, sm, afters)

    def on_grads(l, ts, grads):
        if ts == (0, 1) and l + 1 < DEPTH:
            await_flight(l + 1, (2, 3), [grads[0]])
            await_flight(l + 1, (0, 1), [grads[0]])
        flights[l, ts] = _reduce_start(f"reduce_start_{l}_{ts[0]}", ts, grads)
        return flights[l, ts][3]

    loss_part, grad_x, gsmall = _local_step(x[0], loss_target[0], layer_weights, on_grads, small)
    loss = lax.psum(loss_part[0, 0], ("x", "y", "c"))
    order = [n for n in SMALL]
    packed = _pack([gsmall[n] for n in order])

    out = {n: [lax.empty(w[n].shape, F32) for _ in range(4)] for n in LARGE}
    for l in reversed(range(DEPTH)):
        if l == 0:
            afters = [grad_x, packed] + [out[n][0] for n in LARGE]
            await_flight(0, (2, 3), afters)
            await_flight(0, (0, 1), afters)
        sums = [None] * 4
        for ts in ((0, 1), (2, 3)):
            g, landing = flights[l, ts]
            for i, t in enumerate(ts):
                sums[t] = _add4(f"add4_{LARGE[t]}_{l}", t, g[i], landing[i], b1)
        theirs = _swap_sib(f"swap_sib_{l}", sums)
        for t, n in enumerate(LARGE):
            out[n] = _adamw_pair(f"adamw_{n}_{l}", l, sums[t], theirs[t], w[n], m[n], v[n], out[n])

    rows = packed.shape[0]
    summed = _sum8("sum_small", _all_gather8("gather_small", packed, out[LARGE[0]][0]).reshape(8, rows, 128))
    gfull = dict(zip(order, _unpack(summed, [gsmall[n].shape for n in order])))
    gfull["conv_w"] = lax.dynamic_slice_in_dim(gfull["conv_w"], (2 * ax + ay) * 64, 64, axis=2)
    res = _adamw("adamw_small", _pack([gfull[n] for n in order]), _pack([w[n] for n in order]),
                 _pack([m[n] for n in order]), _pack([v[n] for n in order]))
    for n, parts in zip(order, zip(*[_unpack(r, [w[k].shape for k in order]) for r in res])):
        out[n] = list(parts)

    names = ("norm1_g", "w_in", "q_norm_g", "k_norm_g", "rel_bias", "conv_w", "pool_w", "pool_scale", "w_out",
             "norm2_g", "w_mlp1", "w_mlp2")
    flat = [loss, grad_x[None]]
    for i in range(4):
        flat += [out[n][i] for n in names]
    return tuple(flat)
```

```python
import jax
import jax.numpy as jnp
from jax import lax
from jax.experimental import pallas as pl
from jax.experimental.pallas import tpu as pltpu

F32 = jnp.float32
BF16 = jnp.bfloat16

D = 1024
DEPTH = 4
CH = 64
NPREV = 8
KB = (NPREV + 1) * CH
PADR = NPREV * CH
HD = 64
AW = 512
CW = 256
PWD = 256
DIN = 3 * AW + 3 * CW + PWD
DFF = 4 * D
NIDX = 384
EPS = 1e-6
NEG_INF = -1e30

ADAM_LR = 0.001
ADAM_B1 = 0.9
ADAM_B2 = 0.999
ADAM_EPS = 1e-08
ADAM_WD = 0.01
ADAM_STEP = 10

VMEM_LIMIT = 52 * 1024 * 1024
MM_ROWS = 512


def _mm_rows(k, n):
    return 2 * MM_ROWS if k + n <= 2048 else MM_ROWS


MESH = pl.DeviceIdType.MESH
ANY = pl.BlockSpec(memory_space=pl.ANY)


def _cp(*sem):
    return pltpu.CompilerParams(dimension_semantics=sem, vmem_limit_bytes=VMEM_LIMIT)


def _inv_rms(x):
    return lax.rsqrt(jnp.mean(x * x, axis=-1, keepdims=True) + EPS)


def _head_mean_matrix():
    r = lax.broadcasted_iota(jnp.int32, (AW, AW), 0) // HD
    c = lax.broadcasted_iota(jnp.int32, (AW, AW), 1) // HD
    return jnp.where(r == c, 1.0 / HD, 0.0).astype(BF16)


def _two_pass_dot(x, m):
    hi = x.astype(BF16)
    lo = (x - hi.astype(F32)).astype(BF16)
    return (jnp.dot(hi, m, preferred_element_type=F32)
            + jnp.dot(lo, m, preferred_element_type=F32))


def _head_mean(x, hm):
    return _two_pass_dot(x, hm)


def _rmsnorm(name, x, g):
    s = x.shape[0]
    tm = 512

    def body(x_ref, g_ref, h_ref):
        xv = x_ref[...]
        h_ref[...] = (xv * _inv_rms(xv) * g_ref[...]).astype(BF16)

    return pl.pallas_call(
        body, name=name, grid=(s // tm,),
        in_specs=[pl.BlockSpec((tm, D), lambda i: (i, 0)), pl.BlockSpec((1, D), lambda i: (0, 0))],
        out_specs=pl.BlockSpec((tm, D), lambda i: (i, 0)),
        out_shape=jax.ShapeDtypeStruct((s, D), BF16),
        compiler_params=_cp("parallel"),
    )(x, g)


def _relu2(a):
    r = jnp.maximum(a, jnp.zeros_like(a))
    return r * r


def _mm_nn(name, a, w, l, out_dtype):
    s, k = a.shape
    n = w.shape[2]
    tm = _mm_rows(k, n)

    def body(a_ref, w_ref, o_ref):
        o_ref[...] = jnp.dot(a_ref[...], w_ref[...], preferred_element_type=F32).astype(o_ref.dtype)

    return pl.pallas_call(
        body, name=name, grid=(s // tm,),
        in_specs=[pl.BlockSpec((tm, k), lambda i: (i, 0)),
                  pl.BlockSpec((None, k, n), lambda i: (l, 0, 0))],
        out_specs=pl.BlockSpec((tm, n), lambda i: (i, 0)),
        out_shape=jax.ShapeDtypeStruct((s, n), out_dtype),
        compiler_params=_cp("parallel"),
    )(a, w)


def _mm_res_norm(name, a, w, l, res, g):
    s, k = a.shape
    tm = _mm_rows(k, D)

    def body(a_ref, w_ref, r_ref, g_ref, x_ref, h_ref):
        acc = r_ref[...] + jnp.dot(a_ref[...], w_ref[...], preferred_element_type=F32)
        x_ref[...] = acc
        h_ref[...] = (acc * _inv_rms(acc) * g_ref[...]).astype(BF16)

    return pl.pallas_call(
        body, name=name, grid=(s // tm,),
        in_specs=[pl.BlockSpec((tm, k), lambda i: (i, 0)),
                  pl.BlockSpec((None, k, D), lambda i: (l, 0, 0)),
                  pl.BlockSpec((tm, D), lambda i: (i, 0)),
                  pl.BlockSpec((1, D), lambda i: (0, 0))],
        out_specs=[pl.BlockSpec((tm, D), lambda i: (i, 0))] * 2,
        out_shape=[jax.ShapeDtypeStruct((s, D), F32), jax.ShapeDtypeStruct((s, D), BF16)],
        compiler_params=_cp("parallel"),
    )(a, w, res, g)


def _mlp_fwd(name, h2, w1, w2, l, res, last):
    s = h2.shape[0]
    tm = 256
    final = last.shape[0] == s

    def body(h_ref, w1_ref, w2_ref, r_ref, last_ref, a_ref, first_ref, second_ref, *loss_ref):
        a = jnp.dot(h_ref[...], w1_ref[...], preferred_element_type=F32).astype(BF16)
        a_ref[...] = a
        acc = r_ref[...] + jnp.dot(_relu2(a), w2_ref[...], preferred_element_type=F32)
        if not final:
            first_ref[...] = acc
            second_ref[...] = (acc * _inv_rms(acc) * last_ref[...]).astype(BF16)
            return
        e = acc - last_ref[...]
        dy = e * (1.0 / D)
        first_ref[...] = dy
        second_ref[...] = dy.astype(BF16)
        part = 0.5 * jnp.sum(jnp.mean(e * e, axis=-1, keepdims=True), axis=0, keepdims=True)
        i = pl.program_id(0)

        @pl.when(i == 0)
        def _():
            loss_ref[0][...] = part

        @pl.when(i > 0)
        def _():
            loss_ref[0][...] += part

    once = pl.Buffered(1)
    rows = pl.BlockSpec((tm, D), lambda i: (i, 0))
    one = pl.BlockSpec((1, 1), lambda i: (0, 0))
    return pl.pallas_call(
        body, name=name, grid=(s // tm,),
        in_specs=[rows,
                  pl.BlockSpec((None, D, DFF), lambda i: (l, 0, 0), pipeline_mode=once),
                  pl.BlockSpec((None, DFF, D), lambda i: (l, 0, 0), pipeline_mode=once),
                  rows, rows if final else pl.BlockSpec((1, D), lambda i: (0, 0))],
        out_specs=[pl.BlockSpec((tm, DFF), lambda i: (i, 0)), rows, rows] + ([one] if final else []),
        out_shape=[jax.ShapeDtypeStruct((s, DFF), BF16), jax.ShapeDtypeStruct((s, D), F32),
                   jax.ShapeDtypeStruct((s, D), BF16)] + ([jax.ShapeDtypeStruct((1, 1), F32)] if final else []),
        compiler_params=_cp("arbitrary" if final else "parallel"),
    )(h2, w1, w2, res, last)


def _qkv(name, p, qg, kg):
    s = p.shape[0]
    tm = PADR
    nb = s // tm

    def body(pq_ref, pk_ref, pv_ref, qg_ref, kg_ref, q_ref, qt_ref, k_ref, kt_ref, v_ref, vt_ref):
        t = pl.program_id(0)
        hm = _head_mean_matrix()

        def nrm(x, g):
            return x * lax.rsqrt(_head_mean(x * x, hm) + EPS) * g

        first = t == 0
        qq = nrm(pq_ref[...], qg_ref[...]) * 0.125
        kk = jnp.where(first, 0.0, nrm(pk_ref[...], kg_ref[...]))
        vv = jnp.where(first, 0.0, pv_ref[...])
        q_ref[...] = qq.astype(BF16)
        qt_ref[...] = qq.T.astype(BF16)
        k_ref[...] = kk.astype(BF16)
        kt_ref[...] = kk.T.astype(BF16)
        v_ref[...] = vv.astype(BF16)
        vt_ref[...] = vv.T.astype(BF16)

    def src(col):
        return pl.BlockSpec((tm, AW), lambda t: (jnp.maximum(t - 1, 0), col))

    gspec = pl.BlockSpec((1, AW), lambda t: (0, 0))
    rows = pl.BlockSpec((tm, AW), lambda t: (t, 0))
    cols = pl.BlockSpec((AW, tm), lambda t: (0, t))
    return pl.pallas_call(
        body, name=name, grid=(nb + 1,),
        in_specs=[src(0), src(1), src(2), gspec, gspec],
        out_specs=[pl.BlockSpec((tm, AW), lambda t: (jnp.maximum(t - 1, 0), 0)),
                   pl.BlockSpec((AW, tm), lambda t: (0, jnp.maximum(t - 1, 0))),
                   rows, cols, rows, cols],
        out_shape=[jax.ShapeDtypeStruct((s, AW), BF16), jax.ShapeDtypeStruct((AW, s), BF16),
                   jax.ShapeDtypeStruct((s + PADR, AW), BF16), jax.ShapeDtypeStruct((AW, s + PADR), BF16),
                   jax.ShapeDtypeStruct((s + PADR, AW), BF16), jax.ShapeDtypeStruct((AW, s + PADR), BF16)],
        compiler_params=_cp("arbitrary"),
    )(p, p, p, qg, kg)


NBAND = KB // CH
HIGHEST = lax.Precision.HIGHEST
NT_DIMS = (((1,), (1,)), ((), ()))


def _onehot_table(a):
    m = lax.broadcasted_iota(jnp.int32, (128, NIDX), 0)
    idx = lax.broadcasted_iota(jnp.int32, (128, NIDX), 1)
    rel = jnp.clip(KB - 1 - (CH * a + m), -128, 128) + 128
    return jnp.where(rel == idx, 1.0, 0.0).astype(F32)


def _onehot_diagonal():
    r = lax.broadcasted_iota(jnp.int32, (CH * CH, 128), 0)
    m = lax.broadcasted_iota(jnp.int32, (CH * CH, 128), 1)
    return jnp.where((r % CH) - (r // CH) + (CH - 1) == m, 1.0, 0.0).astype(F32)


def _bias_expand(name, rb):
    def body(rb_ref, o_ref):
        along = [lax.dot_general(rb_ref[...], _onehot_table(a), NT_DIMS, preferred_element_type=F32,
                                 precision=HIGHEST) for a in range(NBAND)]
        o_ref[...] = lax.dot_general(jnp.concatenate(along, axis=0), _onehot_diagonal(), NT_DIMS,
                                     preferred_element_type=F32, precision=HIGHEST)

    return pl.pallas_call(
        body, name=name, grid=(DEPTH,),
        in_specs=[pl.BlockSpec((None, 8, NIDX), lambda l: (l, 0, 0))],
        out_specs=pl.BlockSpec((None, NBAND * 8, CH * CH), lambda l: (l, 0, 0)),
        out_shape=jax.ShapeDtypeStruct((DEPTH, NBAND * 8, CH * CH), F32),
        compiler_params=_cp("parallel"),
    )(rb)


def _bias_reduce(name, db):
    def body(db_ref, o_ref):
        along = jnp.dot(db_ref[...], _onehot_diagonal(), preferred_element_type=F32, precision=HIGHEST)
        acc = jnp.zeros((8, NIDX), F32)
        for a in range(NBAND):
            acc = acc + jnp.dot(along[8 * a:8 * a + 8, :], _onehot_table(a), preferred_element_type=F32,
                                precision=HIGHEST)
        o_ref[...] = acc

    return pl.pallas_call(
        body, name=name, grid=(DEPTH,),
        in_specs=[pl.BlockSpec((None, NBAND * 8, CH * CH), lambda l: (l, 0, 0))],
        out_specs=pl.BlockSpec((None, 8, NIDX), lambda l: (l, 0, 0)),
        out_shape=jax.ShapeDtypeStruct((DEPTH, 8, NIDX), F32),
        compiler_params=_cp("parallel"),
    )(db)


def _bias_layout(flat):
    b = flat.reshape(DEPTH, NBAND, 8, CH, CH).transpose(0, 2, 1, 4, 3).reshape(DEPTH, 4, 2, KB, CH)
    pair = b.transpose(0, 1, 3, 2, 4).reshape(DEPTH, 4, KB, 128)
    first = jnp.pad(pair, ((0, 0), (0, 0), (0, CH), (0, 0)), constant_values=NEG_INF)
    second = jnp.pad(pair, ((0, 0), (0, 0), (CH, 0), (0, 0)), constant_values=NEG_INF)
    return jnp.concatenate([first, second], axis=3)


def _bias_unlayout(dbt):
    b = dbt.reshape(DEPTH, 4, NBAND, CH, 2, CH)
    return b.transpose(0, 2, 1, 4, 5, 3).reshape(DEPTH, NBAND * 8, CH * CH)


UNIT = 2 * CH
BAND2 = KB + CH


def _pair_weights(xt):
    x = xt.astype(F32)
    row = lax.broadcasted_iota(jnp.int32, (128, UNIT), 0)
    low = lax.broadcasted_iota(jnp.int32, (128, UNIT), 1) < HD
    swapped = pltpu.roll(x, HD, 1)
    same = (row < HD) == low
    first = jnp.where(same, jnp.where(low, x, swapped), 0.0)
    second = jnp.where(same, jnp.where(low, swapped, x), 0.0)
    return jnp.concatenate([first, second], axis=1).astype(BF16)


def _pair_rows(x):
    low = lax.broadcasted_iota(jnp.int32, (CH, 128), 1) < HD
    zero = jnp.zeros((CH, 128), x.dtype)
    parts = []
    for c in range(2):
        xc = x[c * CH:(c + 1) * CH, :]
        parts += [jnp.where(low, xc, zero), jnp.where(low, zero, xc)]
    return jnp.concatenate(parts, axis=0)


def _unpair(raw):
    b0, b1 = raw[:, 0:128], raw[:, 128:256]
    row = lax.broadcasted_iota(jnp.int32, (128, 128), 0)
    low = lax.broadcasted_iota(jnp.int32, (128, 128), 1) < HD
    top = jnp.where(low, b0, pltpu.roll(b1, HD, 1))
    bottom = jnp.where(low, pltpu.roll(b0, HD, 1), b1)
    return jnp.where(row < HD, top, bottom).T


def _scores_t(kb, qw, bias2, row0, padded):
    s = jnp.dot(kb, qw, preferred_element_type=F32) + bias2
    if padded:
        s = jnp.where(row0 + lax.broadcasted_iota(jnp.int32, (BAND2, 256), 0) >= PADR, s, NEG_INF)
    return s


def _unit_loops(s, unit):
    lax.fori_loop(0, PADR // UNIT, lambda u, c: unit(u, True, c), 0, unroll=4)
    lax.fori_loop(PADR // UNIT, s // UNIT, lambda u, c: unit(u, False, c), 0, unroll=7)


def _attn_fwd(name, kp, qt, vt, bias2, l):
    s = qt.shape[1]
    nu = s // UNIT

    def body(k_ref, qt_ref, vt_ref, b_ref, o_ref, lse_ref):
        def unit(u, padded, carry):
            r0 = pl.multiple_of(u * UNIT, UNIT)
            sc = _scores_t(k_ref[pl.ds(r0, BAND2), :], _pair_weights(qt_ref[:, pl.ds(r0, UNIT)]), b_ref[...],
                           r0, padded)
            top = jnp.max(sc, axis=0, keepdims=True)
            e = jnp.exp(sc - top)
            total = jnp.sum(e, axis=0, keepdims=True)
            raw = jnp.dot(vt_ref[:, pl.ds(r0, BAND2)], e.astype(BF16), preferred_element_type=F32) * (1.0 / total)
            o_ref[pl.ds(r0, UNIT), :] = _unpair(raw).astype(BF16)
            lse_ref[u] = jnp.broadcast_to(top + jnp.log(total), (8, 256))
            return carry

        _unit_loops(s, unit)

    return pl.pallas_call(
        body, name=name, grid=(AW // 128,),
        in_specs=[pl.BlockSpec((s + PADR, 128), lambda h: (0, h)),
                  pl.BlockSpec((128, s), lambda h: (h, 0)),
                  pl.BlockSpec((128, s + PADR), lambda h: (h, 0)),
                  pl.BlockSpec((None, None, BAND2, 256), lambda h: (l, h, 0, 0))],
        out_specs=[pl.BlockSpec((s, 128), lambda h: (0, h)),
                   pl.BlockSpec((None, nu, 8, 256), lambda h: (h, 0, 0, 0))],
        out_shape=[jax.ShapeDtypeStruct((s, AW), BF16), jax.ShapeDtypeStruct((4, nu, 8, 256), F32)],
        compiler_params=_cp("parallel"),
    )(kp, qt, vt, bias2)


def _attn_bwd(name, q, qt, kp, kt, vp, bias2, l, do, dot, lse, dl, db_all):
    s = q.shape[0]
    nu = s // UNIT

    def body(q_ref, qt_ref, k_ref, kt_ref, v_ref, b_ref, do_ref, dot_ref, lse_ref, dl_ref, dbin_ref,
             dq_ref, dk_ref, dv_ref, db_ref):
        del dbin_ref
        dk_ref[...] = jnp.zeros_like(dk_ref)
        dv_ref[...] = jnp.zeros_like(dv_ref)
        db_ref[...] = jnp.zeros_like(db_ref)

        def unit(u, padded, carry):
            r0 = pl.multiple_of(u * UNIT, UNIT)
            rows, band = pl.ds(r0, UNIT), pl.ds(r0, BAND2)
            sc = _scores_t(k_ref[band, :], _pair_weights(qt_ref[:, rows]), b_ref[...], r0, padded)
            pt = jnp.exp(sc - lse_ref[u][0:1, :])
            dpt = jnp.dot(v_ref[band, :], _pair_weights(dot_ref[:, rows]), preferred_element_type=F32)
            ds = pt * (dpt - dl_ref[u][0:1, :])
            db_ref[...] += ds[0:KB, 0:128] + ds[CH:BAND2, 128:256]
            dsb = ds.astype(BF16)
            dq_ref[rows, :] = _unpair(jnp.dot(kt_ref[:, band], dsb, preferred_element_type=F32))
            dk_ref[band, :] += jnp.dot(dsb, _pair_rows(q_ref[rows, :]), preferred_element_type=F32)
            dv_ref[band, :] += jnp.dot(pt.astype(BF16), _pair_rows(do_ref[rows, :]), preferred_element_type=F32)
            return carry

        _unit_loops(s, unit)

    row_q = pl.BlockSpec((s, 128), lambda h: (0, h))
    col_q = pl.BlockSpec((128, s), lambda h: (h, 0))
    row_k = pl.BlockSpec((s + PADR, 128), lambda h: (0, h))
    col_k = pl.BlockSpec((128, s + PADR), lambda h: (h, 0))
    stat = pl.BlockSpec((None, nu, 8, 256), lambda h: (h, 0, 0, 0))
    return pl.pallas_call(
        body, name=name, grid=(AW // 128,),
        in_specs=[row_q, col_q, row_k, col_k, row_k,
                  pl.BlockSpec((None, None, BAND2, 256), lambda h: (l, h, 0, 0)), row_q, col_q, stat, stat, ANY],
        out_specs=[row_q, row_k, row_k, pl.BlockSpec((None, None, KB, 128), lambda h: (l, h, 0, 0))],
        out_shape=[jax.ShapeDtypeStruct((s, AW), F32),
                   jax.ShapeDtypeStruct((s + PADR, AW), F32),
                   jax.ShapeDtypeStruct((s + PADR, AW), F32),
                   jax.ShapeDtypeStruct((DEPTH, 4, KB, 128), F32)],
        input_output_aliases={10: 3},
        compiler_params=_cp("parallel"),
    )(q, qt, kp, kt, vp, bias2, do, dot, lse, dl, db_all)


def _rowsum_layout(dl, nu):
    d = dl[:, :8].reshape(nu, 2, CH, 4, 2)
    d = d.transpose(3, 0, 1, 4, 2).reshape(4, nu, 1, 256)
    return jnp.broadcast_to(d, (4, nu, 8, 256))


def _rows_before(cur, prev, k):
    row = lax.broadcasted_iota(jnp.int32, cur.shape, 0)
    return jnp.where(row >= k, pltpu.roll(cur, k, 0), pltpu.roll(prev, k, 0))


def _rows_after(cur, nxt, k):
    n = cur.shape[0]
    row = lax.broadcasted_iota(jnp.int32, cur.shape, 0)
    return jnp.where(row < n - k, pltpu.roll(cur, n - k, 0), pltpu.roll(nxt, n - k, 0))


def _pool_window_lanes():
    lg = lax.broadcasted_iota(jnp.int32, (1, PWD), 1) // 64
    return lg, jnp.where(lg == 0, 2.0, jnp.where(lg == 1, 4.0, jnp.where(lg == 2, 8.0, 16.0))).astype(F32)


def _pool_mean_minus_token(u, up, row0):
    lg, wv = _pool_window_lanes()
    sums = []
    c, p = u, up
    for k in (1, 2, 4, 8):
        c2 = c + _rows_before(c, p, k)
        p = p + pltpu.roll(p, k, 0)
        c = c2
        sums.append(c)
    win = jnp.where(lg == 0, sums[0], jnp.where(lg == 1, sums[1], jnp.where(lg == 2, sums[2], sums[3])))
    pos1 = (row0 + lax.broadcasted_iota(jnp.int32, u.shape, 0) + 1).astype(F32)
    cnt = jnp.minimum(pos1, wv)
    return win / cnt - u, cnt


def _conv_taps(z, zp, w0, w1, w2):
    z1 = _rows_before(z, zp, 1)
    z2 = _rows_before(z, zp, 2)
    return (w0 * z2 + w1 * z1) + w2 * z, z1, z2


CP_TM = 1024
HALO = 16


def _halo_before(tm, col):
    return pl.BlockSpec((HALO, CW), lambda i: (jnp.maximum(i * (tm // HALO) - 1, 0), col))


def _halo_after(tm, col, rows):
    return pl.BlockSpec((HALO, CW), lambda i: (jnp.minimum((i + 1) * (tm // HALO), rows // HALO - 1), col))


def _as_block_end(halo, tm):
    return jnp.concatenate([jnp.zeros((tm - HALO, halo.shape[1]), halo.dtype), halo], axis=0)


def _as_block_start(halo, tm):
    return jnp.concatenate([halo, jnp.zeros((tm - HALO, halo.shape[1]), halo.dtype)], axis=0)


def _convpool_fwd(name, p, o, cw, pwbd, ps):
    s = p.shape[0]
    tm = CP_TM
    nb = s // tm

    def body(gb_ref, gc_ref, hin_ref, u_ref, gcp_ref, hinp_ref, up_ref, o_ref, cw_ref, pw_ref, ps_ref, mix_ref):
        i = pl.program_id(0)
        has_prev = i > 0
        z = gc_ref[...] * hin_ref[...]
        zp = _as_block_end(jnp.where(has_prev, gcp_ref[...] * hinp_ref[...], 0.0), tm)
        y3, _, _ = _conv_taps(z, zp, cw_ref[0:1, :], cw_ref[1:2, :], cw_ref[2:3, :])
        m, _ = _pool_mean_minus_token(u_ref[...], _as_block_end(jnp.where(has_prev, up_ref[...], 0.0), tm), i * tm)
        yp = jnp.dot(m.astype(BF16), pw_ref[...].astype(BF16), preferred_element_type=F32) * ps_ref[...]
        mix_ref[:, 0:AW] = o_ref[...]
        mix_ref[:, AW:AW + CW] = (gb_ref[...] * y3).astype(BF16)
        mix_ref[:, AW + CW:D] = yp.astype(BF16)

    def cur(col):
        return pl.BlockSpec((tm, CW), lambda i: (i, col))

    def whole(a):
        return pl.BlockSpec(a.shape, lambda i: (0,) * a.ndim)

    return pl.pallas_call(
        body, name=name, grid=(nb,),
        in_specs=[cur(6), cur(7), cur(8), cur(9), _halo_before(tm, 7), _halo_before(tm, 8), _halo_before(tm, 9),
                  pl.BlockSpec((tm, AW), lambda i: (i, 0)), whole(cw), whole(pwbd), whole(ps)],
        out_specs=pl.BlockSpec((tm, D), lambda i: (i, 0)),
        out_shape=jax.ShapeDtypeStruct((s, D), BF16),
        compiler_params=_cp("parallel"),
    )(p, p, p, p, p, p, p, o, cw, pwbd, ps)


def _convpool_bwd(name, p, dmix, cw, pwbd, ps):
    s = p.shape[0]
    tm = CP_TM // 2
    nb = s // tm

    def body(gb_ref, gc_ref, hin_ref, u_ref, gcp_ref, hinp_ref, up_ref, gbn_ref, dyc_ref, dyp_ref, dycn_ref, dypn_ref,
             cw_ref, pw_ref, ps_ref, dcp_ref, dw0_ref, dw1_ref, dw2_ref, dps_ref, dpw_ref):
        i = pl.program_id(0)
        has_prev = i > 0
        has_next = i < nb - 1
        w0, w1, w2 = cw_ref[0:1, :], cw_ref[1:2, :], cw_ref[2:3, :]
        gb, gc, hin = gb_ref[...], gc_ref[...], hin_ref[...]
        dyc = dyc_ref[...]
        z = gc * hin
        zp = _as_block_end(jnp.where(has_prev, gcp_ref[...] * hinp_ref[...], 0.0), tm)
        y3, z1, z2 = _conv_taps(z, zp, w0, w1, w2)
        dy3 = dyc * gb
        dy3n = _as_block_start(jnp.where(has_next, dycn_ref[...] * gbn_ref[...], 0.0), tm)
        dz = w2 * dy3 + w1 * _rows_after(dy3, dy3n, 1) + w0 * _rows_after(dy3, dy3n, 2)
        pw = pw_ref[...].astype(BF16)
        psv = ps_ref[...]
        m, cnt = _pool_mean_minus_token(u_ref[...], _as_block_end(jnp.where(has_prev, up_ref[...], 0.0), tm), i * tm)
        mb = m.astype(BF16)
        dyp = dyp_ref[...]
        dmp = (dyp * psv).astype(BF16)
        dmpn = jnp.where(has_next, dypn_ref[...] * psv, 0.0).astype(BF16)
        nt = (((1,), (1,)), ((), ()))
        dm = lax.dot_general(dmp, pw, nt, preferred_element_type=F32)
        dmn = lax.dot_general(dmpn, pw, nt, preferred_element_type=F32)
        lg, wv = _pool_window_lanes()
        cc, cn = dm / cnt, _as_block_start(dmn / wv, tm)
        sums = []
        for k in (1, 2, 4, 8):
            c2 = cc + _rows_after(cc, cn, k)
            cn = cn + pltpu.roll(cn, tm - k, 0)
            cc = c2
            sums.append(cc)
        du = jnp.where(lg == 0, sums[0], jnp.where(lg == 1, sums[1], jnp.where(lg == 2, sums[2], sums[3]))) - dm
        dcp_ref[:, 0:CW] = (dyc * y3).astype(BF16)
        dcp_ref[:, CW:2 * CW] = (dz * hin).astype(BF16)
        dcp_ref[:, 2 * CW:3 * CW] = (dz * gc).astype(BF16)
        dcp_ref[:, 3 * CW:4 * CW] = du.astype(BF16)
        parts = (jnp.sum(dy3 * z2, axis=0, keepdims=True),
                 jnp.sum(dy3 * z1, axis=0, keepdims=True),
                 jnp.sum(dy3 * z, axis=0, keepdims=True),
                 jnp.sum(dyp * jnp.dot(mb, pw, preferred_element_type=F32), axis=0, keepdims=True),
                 lax.dot_general(mb, dmp, (((0,), (0,)), ((), ())), preferred_element_type=F32))
        accs = (dw0_ref, dw1_ref, dw2_ref, dps_ref, dpw_ref)

        @pl.when(i == 0)
        def _():
            for a, v in zip(accs, parts):
                a[...] = v

        @pl.when(i > 0)
        def _():
            for a, v in zip(accs, parts):
                a[...] += v

    def cur(col):
        return pl.BlockSpec((tm, CW), lambda i: (i, col))

    def prev(col):
        return _halo_before(tm, col)

    def nxt(col):
        return _halo_after(tm, col, s)

    def whole(shape):
        return pl.BlockSpec(shape, lambda i: (0,) * len(shape))

    row = jax.ShapeDtypeStruct((1, CW), F32)
    return pl.pallas_call(
        body, name=name, grid=(nb,),
        in_specs=[cur(6), cur(7), cur(8), cur(9), prev(7), prev(8), prev(9), nxt(6),
                  cur(0), cur(1), nxt(0), nxt(1), whole(cw.shape), whole(pwbd.shape), whole(ps.shape)],
        out_specs=[pl.BlockSpec((tm, D), lambda i: (i, 0)), whole((1, CW)), whole((1, CW)), whole((1, CW)),
                   whole((1, PWD)), whole((PWD, PWD))],
        out_shape=[jax.ShapeDtypeStruct((s, D), BF16), row, row, row, row,
                   jax.ShapeDtypeStruct((PWD, PWD), F32)],
        compiler_params=_cp("arbitrary"),
    )(p, p, p, p, p, p, p, p, dmix, dmix, dmix, dmix, cw, pwbd, ps)


def _qkv_bwd(name, p, dq, dkp, dvp, dcp, qg, kg):
    s = p.shape[0]
    tm = 512
    off = PADR // tm

    def body(pq_ref, pk_ref, dq_ref, dk_ref, dv_ref, dcp_ref, qg_ref, kg_ref, dp_ref, dqg_ref, dkg_ref):
        i = pl.program_id(0)
        hm = _head_mean_matrix()

        def nrm_bwd(x, g, dy):
            r = lax.rsqrt(_head_mean(x * x, hm) + EPS)
            xn = x * r
            dxn = dy * g
            dx = r * (dxn - xn * _head_mean(dxn * xn, hm))
            dg = jnp.sum(dy * xn, axis=0, keepdims=True)
            dg = (dg[:, 0:128] + dg[:, 128:256]) + (dg[:, 256:384] + dg[:, 384:512])
            return dx, dg + pltpu.roll(dg, HD, 1)

        dxq, dgq = nrm_bwd(pq_ref[...], qg_ref[...], dq_ref[...] * 0.125)
        dxk, dgk = nrm_bwd(pk_ref[...], kg_ref[...], dk_ref[...])
        dp_ref[:, 0:AW] = dxq.astype(BF16)
        dp_ref[:, AW:2 * AW] = dxk.astype(BF16)
        dp_ref[:, 2 * AW:3 * AW] = dv_ref[...].astype(BF16)
        dp_ref[:, 3 * AW:DIN] = dcp_ref[...]

        @pl.when(i == 0)
        def _():
            dqg_ref[...] = dgq
            dkg_ref[...] = dgk

        @pl.when(i > 0)
        def _():
            dqg_ref[...] += dgq
            dkg_ref[...] += dgk

    gspec = pl.BlockSpec((1, AW), lambda i: (0, 0))
    gout = pl.BlockSpec((1, 128), lambda i: (0, 0))
    return pl.pallas_call(
        body, name=name, grid=(s // tm,),
        in_specs=[pl.BlockSpec((tm, AW), lambda i: (i, 0)), pl.BlockSpec((tm, AW), lambda i: (i, 1)),
                  pl.BlockSpec((tm, AW), lambda i: (i, 0)),
                  pl.BlockSpec((tm, AW), lambda i: (i + off, 0)),
                  pl.BlockSpec((tm, AW), lambda i: (i + off, 0)),
                  pl.BlockSpec((tm, D), lambda i: (i, 0)), gspec, gspec],
        out_specs=[pl.BlockSpec((tm, DIN), lambda i: (i, 0)), gout, gout],
        out_shape=[jax.ShapeDtypeStruct((s, DIN), BF16), jax.ShapeDtypeStruct((1, 128), F32),
                   jax.ShapeDtypeStruct((1, 128), F32)],
        compiler_params=_cp("arbitrary"),
    )(p, p, dq, dkp, dvp, dcp, qg, kg)


def _mm_nt_relu(name, dxb, w, l, a):
    s = dxb.shape[0]
    tm = MM_ROWS

    def body(d_ref, w_ref, a_ref, o_ref):
        df = lax.dot_general(d_ref[...], w_ref[...], NT_DIMS, preferred_element_type=F32)
        o_ref[...] = (df * (2.0 * jnp.maximum(a_ref[...].astype(F32), 0.0))).astype(BF16)

    return pl.pallas_call(
        body, name=name, grid=(s // tm,),
        in_specs=[pl.BlockSpec((tm, D), lambda i: (i, 0)),
                  pl.BlockSpec((None, DFF, D), lambda i: (l, 0, 0)),
                  pl.BlockSpec((tm, DFF), lambda i: (i, 0))],
        out_specs=pl.BlockSpec((tm, DFF), lambda i: (i, 0)),
        out_shape=jax.ShapeDtypeStruct((s, DFF), BF16),
        compiler_params=_cp("parallel"),
    )(dxb, w, a)


def _proj_out_bwd(name, dxb, w, l, mix):
    s = dxb.shape[0]
    tm = _mm_rows(D, D)

    def body(d_ref, w_ref, o_ref, do_ref, dot_ref, dcp_ref, dl_ref):
        d = d_ref[...]
        wa, wc = w_ref[0:AW, :], w_ref[AW:D, :]
        do = lax.dot_general(d, wa, NT_DIMS, preferred_element_type=F32)
        do_ref[...] = do.astype(BF16)
        dot_ref[...] = lax.dot_general(wa, d, NT_DIMS, preferred_element_type=F32).astype(BF16)
        dcp_ref[...] = lax.dot_general(d, wc, NT_DIMS, preferred_element_type=F32)
        head = lax.broadcasted_iota(jnp.int32, (AW, 128), 0) // HD
        pick = jnp.where(head == lax.broadcasted_iota(jnp.int32, (AW, 128), 1), 1.0, 0.0).astype(BF16)
        dl_ref[...] = _two_pass_dot(do * o_ref[...].astype(F32), pick)

    return pl.pallas_call(
        body, name=name, grid=(s // tm,),
        in_specs=[pl.BlockSpec((tm, D), lambda i: (i, 0)),
                  pl.BlockSpec((None, D, D), lambda i: (l, 0, 0)),
                  pl.BlockSpec((tm, AW), lambda i: (i, 0))],
        out_specs=[pl.BlockSpec((tm, AW), lambda i: (i, 0)), pl.BlockSpec((AW, tm), lambda i: (0, i)),
                   pl.BlockSpec((tm, D - AW), lambda i: (i, 0)), pl.BlockSpec((tm, 128), lambda i: (i, 0))],
        out_shape=[jax.ShapeDtypeStruct((s, AW), BF16), jax.ShapeDtypeStruct((AW, s), BF16),
                   jax.ShapeDtypeStruct((s, D - AW), F32), jax.ShapeDtypeStruct((s, 128), F32)],
        compiler_params=_cp("parallel"),
    )(dxb, w, mix)


def _mm_nt_normbwd(name, gy, w, l, x, g, dres, dep):
    s, k = gy.shape
    tm = MM_ROWS

    def body(gy_ref, w_ref, x_ref, g_ref, dr_ref, dep_ref, dx_ref, dxb_ref, dg_ref):
        del dep_ref
        i = pl.program_id(0)
        dh = lax.dot_general(gy_ref[...], w_ref[...], NT_DIMS, preferred_element_type=F32)
        xv = x_ref[...]
        r = _inv_rms(xv)
        xn = xv * r
        dxn = dh * g_ref[...]
        dx = r * (dxn - xn * jnp.mean(dxn * xn, axis=-1, keepdims=True)) + dr_ref[...]
        dx_ref[...] = dx
        dxb_ref[...] = dx.astype(BF16)
        part = jnp.sum(dh * xn, axis=0, keepdims=True)

        @pl.when(i == 0)
        def _():
            dg_ref[...] = part

        @pl.when(i > 0)
        def _():
            dg_ref[...] += part

    blk = pl.BlockSpec((tm, D), lambda i: (i, 0))
    vec = pl.BlockSpec((1, D), lambda i: (0, 0))
    return pl.pallas_call(
        body, name=name, grid=(s // tm,),
        in_specs=[pl.BlockSpec((tm, k), lambda i: (i, 0)),
                  pl.BlockSpec((None, D, k), lambda i: (l, 0, 0)), blk, vec, blk, ANY],
        out_specs=[blk, blk, vec],
        out_shape=[jax.ShapeDtypeStruct((s, D), F32), jax.ShapeDtypeStruct((s, D), BF16),
                   jax.ShapeDtypeStruct((1, D), F32)],
        compiler_params=_cp("arbitrary"),
    )(gy, w, x, g, dres, dep)


def _mm_tn(name, a, b, tma, tnb, relu2=False):
    s, m = a.shape
    n = b.shape[1]

    def body(a_ref, b_ref, o_ref):
        av = _relu2(a_ref[...]) if relu2 else a_ref[...]
        o_ref[...] = lax.dot_general(av, b_ref[...], (((0,), (0,)), ((), ())),
                                     preferred_element_type=F32).astype(BF16)

    return pl.pallas_call(
        body, name=name, grid=(m // tma, n // tnb),
        in_specs=[pl.BlockSpec((s, tma), lambda i, j: (0, i), pipeline_mode=pl.Buffered(1) if m == tma else None),
                  pl.BlockSpec((s, tnb), lambda i, j: (0, j))],
        out_specs=pl.BlockSpec((tma, tnb), lambda i, j: (i, j)),
        out_shape=jax.ShapeDtypeStruct((m, n), BF16),
        compiler_params=_cp("parallel", "parallel"),
    )(a, b)


def _adamw_math(gv, wv, mv, vv):
    mn = ADAM_B1 * mv + (1.0 - ADAM_B1) * gv
    vn = ADAM_B2 * vv + (1.0 - ADAM_B2) * jnp.square(gv)
    m_hat = mn / (1.0 - ADAM_B1 ** ADAM_STEP)
    v_hat = vn / (1.0 - ADAM_B2 ** ADAM_STEP)
    return gv, -ADAM_LR * (m_hat / (jnp.sqrt(v_hat) + ADAM_EPS) + ADAM_WD * wv), mn, vn


def _adamw(name, g, w, m, v):
    r, c = g.shape
    tm = 256 if r % 256 == 0 else r

    def body(g_ref, w_ref, m_ref, v_ref, go_ref, d_ref, mo_ref, vo_ref):
        go_ref[...], d_ref[...], mo_ref[...], vo_ref[...] = _adamw_math(g_ref[...], w_ref[...], m_ref[...], v_ref[...])

    blk = pl.BlockSpec((tm, c), lambda i: (i, 0))
    return pl.pallas_call(
        body, name=name, grid=(r // tm,),
        in_specs=[blk] * 4, out_specs=[blk] * 4,
        out_shape=[jax.ShapeDtypeStruct((r, c), F32)] * 4,
        compiler_params=_cp("parallel"),
    )(g, w, m, v)


def _place():
    x, y, c = lax.axis_index("x"), lax.axis_index("y"), lax.axis_index("c")
    chips = [(1 - x, y), (x, 1 - y), (1 - x, 1 - y)]
    return x, y, c, chips


BLOCK_AXIS = (2, 1, 2, 1)
LARGE_DIMS = ((D, DIN), (D, D), (D, DFF), (DFF, D))


def _full_shape(t, layers, dtype):
    r, c = LARGE_DIMS[t]
    return jax.ShapeDtypeStruct((layers, r, c), dtype)


def _cast_into_full(name, t, shard, b1, dep):
    _, r, c = shard.shape
    tm = min(512, r)
    if BLOCK_AXIS[t] == 1:
        out_spec = pl.BlockSpec((None, tm, c), lambda l, i, br: (l, br[0] * (r // tm) + i, 0))
    else:
        out_spec = pl.BlockSpec((None, tm, c), lambda l, i, br: (l, i, br[0]))

    def body(b_ref, x_ref, dep_ref, o_ref):
        del b_ref, dep_ref
        o_ref[...] = x_ref[...].astype(BF16)

    return pl.pallas_call(
        body, name=name,
        grid_spec=pltpu.PrefetchScalarGridSpec(
            num_scalar_prefetch=1, grid=(DEPTH, r // tm),
            in_specs=[pl.BlockSpec((None, tm, c), lambda l, i, br: (l, i, 0)), ANY],
            out_specs=out_spec),
        out_shape=_full_shape(t, DEPTH, BF16),
        compiler_params=_cp("parallel", "parallel"),
    )(b1, shard, dep)


HBM = pl.BlockSpec(memory_space=pltpu.HBM)
SEM = pl.BlockSpec(memory_space=pltpu.SEMAPHORE)
DATAFLOW = pltpu.SideEffectType.DATAFLOW_SIDE_EFFECTING


def _half(ref, l, t, b, c):
    r, cols = LARGE_DIMS[t]
    if BLOCK_AXIS[t] == 1:
        n = r // 8
        return ref.at[l, pl.ds(pl.multiple_of(b * (2 * n) + c * n, 16), n), :]
    n, w = r // 2, cols // 4
    return ref.at[l, pl.ds(pl.multiple_of(c * n, 16), n), pl.ds(pl.multiple_of(b * w, 128), w)]


def _gather_start(name, layers, ts, fulls, both=False):
    n = len(ts)

    def body(*refs):
        f_refs, sems = refs[n:2 * n], refs[2 * n:2 * n + 2 * len(layers)]
        x, y, c, chips = _place()
        for i, l in enumerate(layers):
            for k, t in enumerate(ts):
                own = _half(f_refs[k], l, t, 2 * x + y, c)
                for j, (cx, cy) in enumerate(chips):
                    pltpu.make_async_remote_copy(src_ref=own, dst_ref=own, send_sem=sems[2 * i].at[3 * t + j],
                                                 recv_sem=sems[2 * i + 1].at[3 * t + j], device_id=(cx, cy, c),
                                                 device_id_type=MESH).start()
                    if both:
                        pltpu.make_async_remote_copy(src_ref=own, dst_ref=own,
                                                     send_sem=sems[2 * i].at[12 + 3 * t + j],
                                                     recv_sem=sems[2 * i + 1].at[12 + 3 * t + j],
                                                     device_id=(cx, cy, 1 - c), device_id_type=MESH).start()
        refs[-1][...] = jnp.zeros((8, 128), F32)

    outs = pl.pallas_call(
        body, name=name,
        in_specs=[HBM] * n,
        out_specs=[HBM] * n + [SEM] * (2 * len(layers)) + [pl.BlockSpec(memory_space=pltpu.VMEM)],
        out_shape=[pltpu.HBM(f.shape, f.dtype) for f in fulls]
        + [pltpu.SemaphoreType.DMA((24,))] * (2 * len(layers)) + [jax.ShapeDtypeStruct((8, 128), F32)],
        input_output_aliases={k: k for k in range(n)},
        compiler_params=pltpu.CompilerParams(has_side_effects=DATAFLOW),
    )(*[pltpu.with_memory_space_constraint(f, pltpu.HBM) for f in fulls])
    return outs[0:n], {l: (outs[n + 2 * i], outs[n + 1 + 2 * i]) for i, l in enumerate(layers)}, outs[-1]


def _gather_wait(name, l, ts, fulls, sems, after, both=False):
    def body(*refs):
        send_sems, recv_sems, f_refs = refs[4], refs[5], refs[7:11]
        x, y, c, chips = _place()
        for t in ts:
            own = _half(f_refs[t], l, t, 2 * x + y, c)
            for j, (cx, cy) in enumerate(chips):
                landed = _half(f_refs[t], l, t, 2 * cx + cy, c)
                pltpu.make_async_remote_copy(src_ref=own, dst_ref=landed, send_sem=send_sems.at[3 * t + j],
                                             recv_sem=recv_sems.at[3 * t + j], device_id=(cx, cy, c),
                                             device_id_type=MESH).wait()
                if both:
                    crossed = _half(f_refs[t], l, t, 2 * cx + cy, 1 - c)
                    pltpu.make_async_remote_copy(src_ref=own, dst_ref=crossed, send_sem=send_sems.at[12 + 3 * t + j],
                                                 recv_sem=recv_sems.at[12 + 3 * t + j], device_id=(cx, cy, 1 - c),
                                                 device_id_type=MESH).wait()

    return pl.pallas_call(
        body, name=name,
        in_specs=[HBM] * 4 + [SEM, SEM, ANY], out_specs=[HBM] * 4,
        out_shape=[pltpu.HBM(s.shape, s.dtype) for s in (_full_shape(t, DEPTH, BF16) for t in range(4))],
        input_output_aliases={t: t for t in range(4)},
        compiler_params=pltpu.CompilerParams(has_side_effects=DATAFLOW),
    )(*fulls, sems[0], sems[1], after)


def _pass_on(name, l, ts, fulls):
    def body(*refs):
        f_refs, send_sems, recv_sems = refs[4:8], refs[8], refs[9]
        x, y, c, chips = _place()

        def copy(t, j, half):
            cx, cy = chips[j]
            part = _half(f_refs[t], l, t, 2 * cx + cy, half)
            return pltpu.make_async_remote_copy(src_ref=part, dst_ref=part, send_sem=send_sems.at[3 * t + j],
                                                recv_sem=recv_sems.at[3 * t + j], device_id=(x, y, 1 - c),
                                                device_id_type=MESH)

        for t in ts:
            for j in range(3):
                copy(t, j, c).start()
        for t in ts:
            for j in range(3):
                copy(t, j, 1 - c).wait_recv()
                copy(t, j, c).wait_send()

    return pl.pallas_call(
        body, name=name,
        in_specs=[ANY] * 4, out_specs=[ANY] * 4,
        out_shape=[_full_shape(t, DEPTH, BF16) for t in range(4)],
        input_output_aliases={t: t for t in range(4)},
        scratch_shapes=[pltpu.SemaphoreType.DMA((12,)), pltpu.SemaphoreType.DMA((12,))],
    )(*fulls)


def _block2d(ref, t, b):
    r, cols = LARGE_DIMS[t]
    if BLOCK_AXIS[t] == 1:
        return ref.at[pl.ds(pl.multiple_of(b * (r // 4), 16), r // 4), :]
    return ref.at[:, pl.ds(pl.multiple_of(b * (cols // 4), 128), cols // 4)]


def _block_dims(t):
    r, cols = LARGE_DIMS[t]
    return (r // 4, cols) if BLOCK_AXIS[t] == 1 else (r, cols // 4)


def _reduce_copies(ts, g_refs, r_refs, send_sems, recv_sems):
    _, _, c, chips = _place()
    return [pltpu.make_async_remote_copy(src_ref=_block2d(g_refs[i], t, 2 * cx + cy), dst_ref=r_refs[i].at[j],
                                         send_sem=send_sems.at[3 * i + j], recv_sem=recv_sems.at[3 * i + j],
                                         device_id=(cx, cy, c), device_id_type=MESH)
            for i, t in enumerate(ts) for j, (cx, cy) in enumerate(chips)]


def _reduce_start(name, ts, grads):
    n = len(ts)

    def body(*refs):
        for cp in _reduce_copies(ts, refs[n:2 * n], refs[2 * n:3 * n], refs[3 * n], refs[3 * n + 1]):
            cp.start()
        refs[3 * n + 2][...] = jnp.zeros((8, 128), F32)

    outs = pl.pallas_call(
        body, name=name,
        in_specs=[HBM] * n,
        out_specs=[HBM] * (2 * n) + [SEM, SEM, pl.BlockSpec(memory_space=pltpu.VMEM)],
        out_shape=[pltpu.HBM(g.shape, BF16) for g in grads]
        + [pltpu.HBM((3,) + _block_dims(t), BF16) for t in ts]
        + [pltpu.SemaphoreType.DMA((3 * n,)), pltpu.SemaphoreType.DMA((3 * n,)), jax.ShapeDtypeStruct((8, 128), F32)],
        input_output_aliases={i: i for i in range(n)},
        compiler_params=pltpu.CompilerParams(has_side_effects=DATAFLOW),
    )(*[pltpu.with_memory_space_constraint(g, pltpu.HBM) for g in grads])
    return outs[0:n], outs[n:2 * n], (outs[2 * n], outs[2 * n + 1]), outs[2 * n + 2]


def _reduce_wait(name, ts, grads, landing, sems, afters):
    n = len(ts)
    first_out = 2 * n + 2 + len(afters)

    def body(*refs):
        for cp in _reduce_copies(ts, refs[first_out:first_out + n], refs[first_out + n:first_out + 2 * n],
                                 refs[2 * n], refs[2 * n + 1]):
            cp.wait()

    outs = pl.pallas_call(
        body, name=name,
        in_specs=[HBM] * (2 * n) + [SEM, SEM] + [ANY] * len(afters), out_specs=[HBM] * (2 * n),
        out_shape=[pltpu.HBM(g.shape, BF16) for g in grads] + [pltpu.HBM(r.shape, BF16) for r in landing],
        input_output_aliases={i: i for i in range(2 * n)},
        compiler_params=pltpu.CompilerParams(has_side_effects=DATAFLOW),
    )(*grads, *landing, sems[0], sems[1], *afters)
    return outs[0:n], outs[n:2 * n]


def _add4(name, t, own, landed, b1):
    rb, cb = _block_dims(t)
    tm = min(512, rb)
    if BLOCK_AXIS[t] == 1:
        own_spec = pl.BlockSpec((tm, cb), lambda i, br: (br[0] * (rb // tm) + i, 0))
    else:
        own_spec = pl.BlockSpec((tm, cb), lambda i, br: (i, br[0]))

    def body(b_ref, o_ref, r0_ref, r1_ref, r2_ref, s_ref):
        del b_ref
        s_ref[...] = ((o_ref[...].astype(F32) + r0_ref[...].astype(F32))
                      + (r1_ref[...].astype(F32) + r2_ref[...].astype(F32))).astype(BF16)

    def got(j):
        return pl.BlockSpec((None, tm, cb), lambda i, br: (j, i, 0))

    return pl.pallas_call(
        body, name=name,
        grid_spec=pltpu.PrefetchScalarGridSpec(
            num_scalar_prefetch=1, grid=(rb // tm,),
            in_specs=[own_spec, got(0), got(1), got(2)],
            out_specs=pl.BlockSpec((tm, cb), lambda i, br: (i, 0))),
        out_shape=jax.ShapeDtypeStruct((rb, cb), BF16),
        compiler_params=_cp("parallel"),
    )(b1, own, landed, landed, landed)


def _swap_sib(name, sums):
    def body(*refs):
        s_refs, t_refs, send_sems, recv_sems = refs[0:4], refs[4:8], refs[8], refs[9]
        x, y, c, _ = _place()
        cps = [pltpu.make_async_remote_copy(src_ref=s_refs[t], dst_ref=t_refs[t], send_sem=send_sems.at[t],
                                            recv_sem=recv_sems.at[t], device_id=(x, y, 1 - c), device_id_type=MESH)
               for t in range(4)]
        for cp in cps:
            cp.start()
        for cp in cps:
            cp.wait()

    return pl.pallas_call(
        body, name=name,
        in_specs=[ANY] * 4, out_specs=[ANY] * 4,
        out_shape=[jax.ShapeDtypeStruct(s.shape, BF16) for s in sums],
        scratch_shapes=[pltpu.SemaphoreType.DMA((4,)), pltpu.SemaphoreType.DMA((4,))],
    )(*sums)


def _adamw_pair(name, l, s_own, s_sib, w, m, v, outs):
    rb, cb = s_own.shape
    tm = min(512, rb)

    def body(a_ref, b_ref, w_ref, m_ref, v_ref, g0, d0, m0, v0, go_ref, d_ref, mo_ref, vo_ref):
        del g0, d0, m0, v0
        gv = a_ref[...].astype(F32) + b_ref[...].astype(F32)
        go_ref[...], d_ref[...], mo_ref[...], vo_ref[...] = _adamw_math(gv, w_ref[...], m_ref[...], v_ref[...])

    part = pl.BlockSpec((tm, cb), lambda i: (i, 0))
    layer = pl.BlockSpec((None, tm, cb), lambda i: (l, i, 0))
    return pl.pallas_call(
        body, name=name, grid=(rb // tm,),
        in_specs=[part, part, layer, layer, layer] + [ANY] * 4,
        out_specs=[layer] * 4,
        out_shape=[jax.ShapeDtypeStruct((DEPTH, rb, cb), F32)] * 4,
        input_output_aliases={5 + i: i for i in range(4)},
        compiler_params=_cp("parallel"),
    )(s_own, s_sib, w, m, v, *outs)


def _all_gather8(name, v, dep):
    m_per, n = v.shape

    def body(v_ref, dep_ref, out_ref, send_sems, recv_sems, local_sem):
        del dep_ref
        x, y, c, chips = _place()
        me, sib = (x, y, c), (x, y, 1 - c)

        def rows(px, py, pc):
            return out_ref.at[pl.ds((4 * px + 2 * py + pc) * m_per, m_per), :]

        def copy(k, block, to, src=None):
            return pltpu.make_async_remote_copy(
                src_ref=rows(*block) if src is None else src, dst_ref=rows(*block),
                send_sem=send_sems.at[k], recv_sem=recv_sems.at[k], device_id=to, device_id_type=MESH)

        mine = pltpu.make_async_copy(v_ref, rows(*me), local_sem)
        mine.start()
        first = [copy(0, me, sib, src=v_ref)]
        first += [copy(1 + j, me, (*chip, c), src=v_ref) for j, chip in enumerate(chips)]
        for cp in first:
            cp.start()
        passed = [copy(4 + j, (*chip, c), sib) for j, chip in enumerate(chips)]
        for j, chip in enumerate(chips):
            copy(1 + j, (*chip, c), me).wait_recv()
            passed[j].start()
        copy(0, sib, me).wait_recv()
        for j, chip in enumerate(chips):
            copy(4 + j, (*chip, 1 - c), me).wait_recv()
        for cp in first + passed:
            cp.wait_send()
        mine.wait()

    return pl.pallas_call(
        body, name=name,
        out_shape=jax.ShapeDtypeStruct((8 * m_per, n), v.dtype),
        in_specs=[pl.BlockSpec(memory_space=pltpu.VMEM), ANY],
        out_specs=pl.BlockSpec(memory_space=pltpu.VMEM),
        scratch_shapes=[pltpu.SemaphoreType.DMA((7,)), pltpu.SemaphoreType.DMA((7,)), pltpu.SemaphoreType.DMA],
    )(v, dep)


def _sum8(name, g):
    def body(g_ref, o_ref):
        acc = g_ref[0]
        for d in range(1, 8):
            acc = acc + g_ref[d]
        o_ref[...] = acc

    return pl.pallas_call(body, name=name, out_shape=jax.ShapeDtypeStruct(g.shape[1:], F32))(g)


def _pack(parts):
    flat = []
    for a in parts:
        a = a.reshape(-1)
        flat.append(jnp.pad(a, (0, (-a.shape[0]) % 128)))
    cat = jnp.concatenate(flat)
    cat = jnp.pad(cat, (0, (-cat.shape[0]) % 1024))
    return cat.reshape(-1, 128)


def _unpack(packed, shapes):
    flat = packed.reshape(-1)
    out, at = [], 0
    for shp in shapes:
        n = 1
        for d in shp:
            n *= d
        out.append(flat[at:at + n].reshape(shp))
        at += n + (-n) % 128
    return out


def _local_step(x, target, layer_weights, on_grads, small):
    qg_all = jnp.tile(small["q_norm_g"], (1, 8))
    kg_all = jnp.tile(small["k_norm_g"], (1, 8))
    bias_all = _bias_layout(_bias_expand("bias_expand", jnp.pad(small["rel_bias"], ((0, 0), (0, 0), (0, NIDX - 257)))))
    same_group = jnp.eye(4, dtype=F32)[None, :, None, :, None]
    pwbd_all = (small["pool_w"][:, :, :, None, :] * same_group).reshape(DEPTH, PWD, PWD)
    saved = []
    xin = x
    h = _rmsnorm("norm_first", x, small["norm1_g"][0:1])
    for l in range(DEPTH):
        w_in = layer_weights(l, (0,), xin)[0]
        qg, kg = qg_all[l:l + 1], kg_all[l:l + 1]
        cw, pwbd, ps = small["conv_w"][l], pwbd_all[l], small["pool_scale"][l:l + 1]
        p = _mm_nn(f"proj_in_{l}", h, w_in, l, F32)
        q, qt, kp, kt, vp, vt = _qkv(f"qkv_{l}", p, qg, kg)
        o, lse = _attn_fwd(f"attn_fwd_{l}", kp, qt, vt, bias_all, l)
        w_in, w_out, w_1, w_2 = layer_weights(l, (1, 2, 3), o)
        mix = _convpool_fwd(f"convpool_fwd_{l}", p, o, cw, pwbd, ps)
        x1, h2 = _mm_res_norm(f"proj_out_{l}", mix, w_out, l, xin, small["norm2_g"][l:l + 1])
        saved.append(dict(xin=xin, h=h, p=p, q=q, qt=qt, kp=kp, kt=kt, vp=vp, mix=mix, x1=x1, h2=h2, lse=lse,
                          qg=qg, kg=kg, cw=cw, pwbd=pwbd, ps=ps))
        if l + 1 < DEPTH:
            saved[l]["a"], xin, h = _mlp_fwd(f"mlp_{l}", h2, w_1, w_2, l, x1, small["norm1_g"][l + 1:l + 2])
        else:
            saved[l]["a"], dx, dxb, loss = _mlp_fwd(f"mlp_{l}", h2, w_1, w_2, l, x1, target)

    raw = {k: [None] * DEPTH for k in ("dg1", "dqg", "dkg", "dw0", "dw1", "dw2", "dpw", "dps", "dg2")}
    db_all = lax.empty((DEPTH, 4, KB, 128), F32)
    for l in reversed(range(DEPTH)):
        sv = saved[l]
        da = _mm_nt_relu(f"mlp2_bwd_{l}", dxb, w_2, l, sv["a"])
        g_2 = _mm_tn(f"mlp2_wgrad_{l}", sv["a"], dxb, 512, 1024, relu2=True)
        g_1 = _mm_tn(f"mlp1_wgrad_{l}", sv["h2"], da, 1024, 512)
        dep = on_grads(l, (2, 3), (g_1, g_2))
        dx1, dx1b, dg2 = _mm_nt_normbwd(f"mlp1_bwd_{l}", da, w_1, l, sv["x1"], small["norm2_g"][l:l + 1], dx, dep)
        do, dot, dmix, dl = _proj_out_bwd(f"proj_out_bwd_{l}", dx1b, w_out, l, sv["mix"])
        g_out = _mm_tn(f"proj_out_wgrad_{l}", sv["mix"], dx1b, 512, 1024)
        dcp, dw0, dw1, dw2, dps, dpw = _convpool_bwd(f"convpool_bwd_{l}", sv["p"], dmix, sv["cw"], sv["pwbd"], sv["ps"])
        dq, dkp, dvp, db_all = _attn_bwd(f"attn_bwd_{l}", sv["q"], sv["qt"], sv["kp"], sv["kt"], sv["vp"], bias_all, l,
                                     do, dot, sv["lse"], _rowsum_layout(dl, x.shape[0] // UNIT), db_all)
        dp, dqg, dkg = _qkv_bwd(f"qkv_bwd_{l}", sv["p"], dq, dkp, dvp, dcp, sv["qg"], sv["kg"])
        g_in = _mm_tn(f"proj_in_wgrad_{l}", sv["h"], dp, 1024, 1280)
        dep = on_grads(l, (0, 1), (g_in, g_out))
        dx, dxb, dg1 = _mm_nt_normbwd(f"proj_in_bwd_{l}", dp, w_in, l, sv["xin"], small["norm1_g"][l:l + 1], dx1, dep)
        for k, val in dict(dg1=dg1, dqg=dqg, dkg=dkg, dw0=dw0, dw1=dw1, dw2=dw2, dpw=dpw, dps=dps, dg2=dg2).items():
            raw[k][l] = val
    cat = {k: jnp.concatenate(v, axis=0) for k, v in raw.items() if k != "dpw"}
    drb = _bias_reduce("bias_reduce", _bias_unlayout(db_all))
    dpw = jnp.stack(raw["dpw"])
    gsmall = {
        "norm1_g": cat["dg1"], "q_norm_g": cat["dqg"][:, :HD], "k_norm_g": cat["dkg"][:, :HD],
        "rel_bias": drb[:, :, :257],
        "conv_w": jnp.stack([cat["dw0"], cat["dw1"], cat["dw2"]], axis=1),
        "pool_w": jnp.stack([dpw[:, g * 64:(g + 1) * 64, g * 64:(g + 1) * 64] for g in range(4)], axis=1),
        "pool_scale": cat["dps"], "norm2_g": cat["dg2"],
    }
    return loss, dx, gsmall


SMALL = ("norm1_g", "q_norm_g", "k_norm_g", "rel_bias", "conv_w", "pool_w", "pool_scale", "norm2_g")
LARGE = ("w_in", "w_out", "w_mlp1", "w_mlp2")


def kernel(x, norm1_g, w_in, q_norm_g, k_norm_g, rel_bias, conv_w, pool_w, pool_scale, w_out, norm2_g, w_mlp1, w_mlp2, loss_target, m_norm1_g, m_w_in, m_q_norm_g, m_k_norm_g, m_rel_bias, m_conv_w, m_pool_w, m_pool_scale, m_w_out, m_norm2_g, m_w_mlp1, m_w_mlp2, v_norm1_g, v_w_in, v_q_norm_g, v_k_norm_g, v_rel_bias, v_conv_w, v_pool_w, v_pool_scale, v_w_out, v_norm2_g, v_w_mlp1, v_w_mlp2):
    w = dict(norm1_g=norm1_g, w_in=w_in, q_norm_g=q_norm_g, k_norm_g=k_norm_g, rel_bias=rel_bias, conv_w=conv_w,
             pool_w=pool_w, pool_scale=pool_scale, w_out=w_out, norm2_g=norm2_g, w_mlp1=w_mlp1, w_mlp2=w_mlp2)
    m = dict(norm1_g=m_norm1_g, w_in=m_w_in, q_norm_g=m_q_norm_g, k_norm_g=m_k_norm_g, rel_bias=m_rel_bias,
             conv_w=m_conv_w, pool_w=m_pool_w, pool_scale=m_pool_scale, w_out=m_w_out, norm2_g=m_norm2_g,
             w_mlp1=m_w_mlp1, w_mlp2=m_w_mlp2)
    v = dict(norm1_g=v_norm1_g, w_in=v_w_in, q_norm_g=v_q_norm_g, k_norm_g=v_k_norm_g, rel_bias=v_rel_bias,
             conv_w=v_conv_w, pool_w=v_pool_w, pool_scale=v_pool_scale, w_out=v_w_out, norm2_g=v_norm2_g,
             w_mlp1=v_w_mlp1, w_mlp2=v_w_mlp2)
    ax, ay, ac = lax.axis_index("x"), lax.axis_index("y"), lax.axis_index("c")
    b1 = jnp.reshape(2 * ax + ay, (1,)).astype(jnp.int32)

    cw_rows = _all_gather8("gather_conv_w", jnp.pad(conv_w.reshape(DEPTH * 3, 64), ((0, 4), (0, 64))), b1)
    cw_chips = [cw_rows[(4 * cx + 2 * cy) * 16:(4 * cx + 2 * cy) * 16 + 12, :64] for cx in range(2) for cy in range(2)]
    small = {n: w[n] for n in SMALL}
    small["conv_w"] = jnp.concatenate(cw_chips, axis=1).reshape(DEPTH, 3, CW)

    (w_in_full,), in_sems, in_token = _gather_start(
        "gather_start_in", (0,), (0,), [_cast_into_full("cast_w_in", 0, w["w_in"], b1, cw_rows)])
    others, first_sems, first_token = _gather_start(
        "gather_start_first", (0,), (1, 2, 3),
        [_cast_into_full(f"cast_{LARGE[t]}", t, w[LARGE[t]], b1, in_token) for t in (1, 2, 3)])
    held = [[w_in_full] + list(others)]
    sems = {(0, 0): in_sems[0], (0, 1): first_sems[0]}

    def layer_weights(l, ts, after):
        if l > 0:
            ts = (0, 1, 2, 3) if ts == (0,) else ()
        if ts:
            tag = f"{l}_{ts[0]}"
            first_in = l == 0 and ts == (0,)
            after = first_token if first_in else after
            arrived = _gather_wait(f"gather_wait_{tag}", l, ts, held[0], sems[l, ts[0] if l == 0 else 0], after,
                                   both=l > 0)
            if first_in:
                arrived, rest_sems, _ = _gather_start("gather_start_rest", tuple(range(1, DEPTH)), (0, 1, 2, 3),
                                                      arrived, both=True)
                sems.update({(k, 0): v for k, v in rest_sems.items()})
            held[0] = _pass_on(f"pass_on_{tag}", l, ts, arrived) if l == 0 else arrived
        return held[0]

    flights = {}

    def await_flight(l, ts, afters):
        g, landing, sm, _ = flights[l, ts]
        flights[l, ts] = _reduce_wait(f"reduce_wait_{l}_{ts[0]}", ts, g, landing, sm, afters)

    def on_grads(l, ts, grads):
        if ts == (0, 1) and l + 1 < DEPTH:
            await_flight(l + 1, (2, 3), [grads[0]])
            await_flight(l + 1, (0, 1), [grads[0]])
        flights[l, ts] = _reduce_start(f"reduce_start_{l}_{ts[0]}", ts, grads)
        return flights[l, ts][3]

    loss_part, grad_x, gsmall = _local_step(x[0], loss_target[0], layer_weights, on_grads, small)
    loss = lax.psum(loss_part[0, 0], ("x", "y", "c"))
    order = [n for n in SMALL]
    packed = _pack([gsmall[n] for n in order])

    out = {n: [lax.empty(w[n].shape, F32) for _ in range(4)] for n in LARGE}
    for l in reversed(range(DEPTH)):
        if l == 0:
            afters = [grad_x, packed] + [out[n][0] for n in LARGE]
            await_flight(0, (2, 3), afters)
            await_flight(0, (0, 1), afters)
        sums = [None] * 4
        for ts in ((0, 1), (2, 3)):
            g, landing = flights[l, ts]
            for i, t in enumerate(ts):
                sums[t] = _add4(f"add4_{LARGE[t]}_{l}", t, g[i], landing[i], b1)
        theirs = _swap_sib(f"swap_sib_{l}", sums)
        for t, n in enumerate(LARGE):
            out[n] = _adamw_pair(f"adamw_{n}_{l}", l, sums[t], theirs[t], w[n], m[n], v[n], out[n])

    rows = packed.shape[0]
    summed = _sum8("sum_small", _all_gather8("gather_small", packed, out[LARGE[0]][0]).reshape(8, rows, 128))
    gfull = dict(zip(order, _unpack(summed, [gsmall[n].shape for n in order])))
    gfull["conv_w"] = lax.dynamic_slice_in_dim(gfull["conv_w"], (2 * ax + ay) * 64, 64, axis=2)
    res = _adamw("adamw_small", _pack([gfull[n] for n in order]), _pack([w[n] for n in order]),
                 _pack([m[n] for n in order]), _pack([v[n] for n in order]))
    for n, parts in zip(order, zip(*[_unpack(r, [w[k].shape for k in order]) for r in res])):
        out[n] = list(parts)

    names = ("norm1_g", "w_in", "q_norm_g", "k_norm_g", "rel_bias", "conv_w", "pool_w", "pool_scale", "w_out",
             "norm2_g", "w_mlp1", "w_mlp2")
    flat = [loss, grad_x[None]]
    for i in range(4):
        flat += [out[n][i] for n in names]
    return tuple(flat)
```

```python
import jax
import jax.numpy as jnp
from jax import lax
from jax.experimental import pallas as pl
from jax.experimental.pallas import tpu as pltpu

F32 = jnp.float32
BF16 = jnp.bfloat16

D = 1024
DEPTH = 4
CH = 64
NPREV = 8
KB = (NPREV + 1) * CH
PADR = NPREV * CH
HD = 64
AW = 512
CW = 256
PWD = 256
DIN = 3 * AW + 3 * CW + PWD
DFF = 4 * D
NIDX = 384
EPS = 1e-6
NEG_INF = -1e30

ADAM_LR = 0.001
ADAM_B1 = 0.9
ADAM_B2 = 0.999
ADAM_EPS = 1e-08
ADAM_WD = 0.01
ADAM_STEP = 10

VMEM_LIMIT = 52 * 1024 * 1024
MM_ROWS = 512


def _mm_rows(k, n):
    return 2 * MM_ROWS if k + n <= 2048 else MM_ROWS


MESH = pl.DeviceIdType.MESH
ANY = pl.BlockSpec(memory_space=pl.ANY)


def _cp(*sem):
    return pltpu.CompilerParams(dimension_semantics=sem, vmem_limit_bytes=VMEM_LIMIT)


def _inv_rms(x):
    return lax.rsqrt(jnp.mean(x * x, axis=-1, keepdims=True) + EPS)


def _head_mean_matrix():
    r = lax.broadcasted_iota(jnp.int32, (AW, AW), 0) // HD
    c = lax.broadcasted_iota(jnp.int32, (AW, AW), 1) // HD
    return jnp.where(r == c, 1.0 / HD, 0.0).astype(BF16)


def _two_pass_dot(x, m):
    hi = x.astype(BF16)
    lo = (x - hi.astype(F32)).astype(BF16)
    return (jnp.dot(hi, m, preferred_element_type=F32)
            + jnp.dot(lo, m, preferred_element_type=F32))


def _head_mean(x, hm):
    return _two_pass_dot(x, hm)


def _rmsnorm(name, x, g):
    s = x.shape[0]
    tm = 512

    def body(x_ref, g_ref, h_ref):
        xv = x_ref[...]
        h_ref[...] = (xv * _inv_rms(xv) * g_ref[...]).astype(BF16)

    return pl.pallas_call(
        body, name=name, grid=(s // tm,),
        in_specs=[pl.BlockSpec((tm, D), lambda i: (i, 0)), pl.BlockSpec((1, D), lambda i: (0, 0))],
        out_specs=pl.BlockSpec((tm, D), lambda i: (i, 0)),
        out_shape=jax.ShapeDtypeStruct((s, D), BF16),
        compiler_params=_cp("parallel"),
    )(x, g)


def _relu2(a):
    r = jnp.maximum(a, jnp.zeros_like(a))
    return r * r


def _mm_nn(name, a, w, l, out_dtype):
    s, k = a.shape
    n = w.shape[2]
    tm = _mm_rows(k, n)

    def body(a_ref, w_ref, o_ref):
        o_ref[...] = jnp.dot(a_ref[...], w_ref[...], preferred_element_type=F32).astype(o_ref.dtype)

    return pl.pallas_call(
        body, name=name, grid=(s // tm,),
        in_specs=[pl.BlockSpec((tm, k), lambda i: (i, 0)),
                  pl.BlockSpec((None, k, n), lambda i: (l, 0, 0))],
        out_specs=pl.BlockSpec((tm, n), lambda i: (i, 0)),
        out_shape=jax.ShapeDtypeStruct((s, n), out_dtype),
        compiler_params=_cp("parallel"),
    )(a, w)


def _mm_res_norm(name, a, w, l, res, g):
    s, k = a.shape
    tm = _mm_rows(k, D)

    def body(a_ref, w_ref, r_ref, g_ref, x_ref, h_ref):
        acc = r_ref[...] + jnp.dot(a_ref[...], w_ref[...], preferred_element_type=F32)
        x_ref[...] = acc
        h_ref[...] = (acc * _inv_rms(acc) * g_ref[...]).astype(BF16)

    return pl.pallas_call(
        body, name=name, grid=(s // tm,),
        in_specs=[pl.BlockSpec((tm, k), lambda i: (i, 0)),
                  pl.BlockSpec((None, k, D), lambda i: (l, 0, 0)),
                  pl.BlockSpec((tm, D), lambda i: (i, 0)),
                  pl.BlockSpec((1, D), lambda i: (0, 0))],
        out_specs=[pl.BlockSpec((tm, D), lambda i: (i, 0))] * 2,
        out_shape=[jax.ShapeDtypeStruct((s, D), F32), jax.ShapeDtypeStruct((s, D), BF16)],
        compiler_params=_cp("parallel"),
    )(a, w, res, g)


def _mlp_fwd(name, h2, w1, w2, l, res, last):
    s = h2.shape[0]
    tm = 256
    final = last.shape[0] == s

    def body(h_ref, w1_ref, w2_ref, r_ref, last_ref, a_ref, first_ref, second_ref, *loss_ref):
        a = jnp.dot(h_ref[...], w1_ref[...], preferred_element_type=F32).astype(BF16)
        a_ref[...] = a
        acc = r_ref[...] + jnp.dot(_relu2(a), w2_ref[...], preferred_element_type=F32)
        if not final:
            first_ref[...] = acc
            second_ref[...] = (acc * _inv_rms(acc) * last_ref[...]).astype(BF16)
            return
        e = acc - last_ref[...]
        dy = e * (1.0 / D)
        first_ref[...] = dy
        second_ref[...] = dy.astype(BF16)
        part = 0.5 * jnp.sum(jnp.mean(e * e, axis=-1, keepdims=True), axis=0, keepdims=True)
        i = pl.program_id(0)

        @pl.when(i == 0)
        def _():
            loss_ref[0][...] = part

        @pl.when(i > 0)
        def _():
            loss_ref[0][...] += part

    once = pl.Buffered(1)
    rows = pl.BlockSpec((tm, D), lambda i: (i, 0))
    one = pl.BlockSpec((1, 1), lambda i: (0, 0))
    return pl.pallas_call(
        body, name=name, grid=(s // tm,),
        in_specs=[rows,
                  pl.BlockSpec((None, D, DFF), lambda i: (l, 0, 0), pipeline_mode=once),
                  pl.BlockSpec((None, DFF, D), lambda i: (l, 0, 0), pipeline_mode=once),
                  rows, rows if final else pl.BlockSpec((1, D), lambda i: (0, 0))],
        out_specs=[pl.BlockSpec((tm, DFF), lambda i: (i, 0)), rows, rows] + ([one] if final else []),
        out_shape=[jax.ShapeDtypeStruct((s, DFF), BF16), jax.ShapeDtypeStruct((s, D), F32),
                   jax.ShapeDtypeStruct((s, D), BF16)] + ([jax.ShapeDtypeStruct((1, 1), F32)] if final else []),
        compiler_params=_cp("arbitrary" if final else "parallel"),
    )(h2, w1, w2, res, last)


def _qkv(name, p, qg, kg):
    s = p.shape[0]
    tm = PADR
    nb = s // tm

    def body(pq_ref, pk_ref, pv_ref, qg_ref, kg_ref, q_ref, qt_ref, k_ref, kt_ref, v_ref, vt_ref):
        t = pl.program_id(0)
        hm = _head_mean_matrix()

        def nrm(x, g):
            return x * lax.rsqrt(_head_mean(x * x, hm) + EPS) * g

        first = t == 0
        qq = nrm(pq_ref[...], qg_ref[...]) * 0.125
        kk = jnp.where(first, 0.0, nrm(pk_ref[...], kg_ref[...]))
        vv = jnp.where(first, 0.0, pv_ref[...])
        q_ref[...] = qq.astype(BF16)
        qt_ref[...] = qq.T.astype(BF16)
        k_ref[...] = kk.astype(BF16)
        kt_ref[...] = kk.T.astype(BF16)
        v_ref[...] = vv.astype(BF16)
        vt_ref[...] = vv.T.astype(BF16)

    def src(col):
        return pl.BlockSpec((tm, AW), lambda t: (jnp.maximum(t - 1, 0), col))

    gspec = pl.BlockSpec((1, AW), lambda t: (0, 0))
    rows = pl.BlockSpec((tm, AW), lambda t: (t, 0))
    cols = pl.BlockSpec((AW, tm), lambda t: (0, t))
    return pl.pallas_call(
        body, name=name, grid=(nb + 1,),
        in_specs=[src(0), src(1), src(2), gspec, gspec],
        out_specs=[pl.BlockSpec((tm, AW), lambda t: (jnp.maximum(t - 1, 0), 0)),
                   pl.BlockSpec((AW, tm), lambda t: (0, jnp.maximum(t - 1, 0))),
                   rows, cols, rows, cols],
        out_shape=[jax.ShapeDtypeStruct((s, AW), BF16), jax.ShapeDtypeStruct((AW, s), BF16),
                   jax.ShapeDtypeStruct((s + PADR, AW), BF16), jax.ShapeDtypeStruct((AW, s + PADR), BF16),
                   jax.ShapeDtypeStruct((s + PADR, AW), BF16), jax.ShapeDtypeStruct((AW, s + PADR), BF16)],
        compiler_params=_cp("arbitrary"),
    )(p, p, p, qg, kg)


NBAND = KB // CH
HIGHEST = lax.Precision.HIGHEST
NT_DIMS = (((1,), (1,)), ((), ()))


def _onehot_table(a):
    m = lax.broadcasted_iota(jnp.int32, (128, NIDX), 0)
    idx = lax.broadcasted_iota(jnp.int32, (128, NIDX), 1)
    rel = jnp.clip(KB - 1 - (CH * a + m), -128, 128) + 128
    return jnp.where(rel == idx, 1.0, 0.0).astype(F32)


def _onehot_diagonal():
    r = lax.broadcasted_iota(jnp.int32, (CH * CH, 128), 0)
    m = lax.broadcasted_iota(jnp.int32, (CH * CH, 128), 1)
    return jnp.where((r % CH) - (r // CH) + (CH - 1) == m, 1.0, 0.0).astype(F32)


def _bias_expand(name, rb):
    def body(rb_ref, o_ref):
        along = [lax.dot_general(rb_ref[...], _onehot_table(a), NT_DIMS, preferred_element_type=F32,
                                 precision=HIGHEST) for a in range(NBAND)]
        o_ref[...] = lax.dot_general(jnp.concatenate(along, axis=0), _onehot_diagonal(), NT_DIMS,
                                     preferred_element_type=F32, precision=HIGHEST)

    return pl.pallas_call(
        body, name=name, grid=(DEPTH,),
        in_specs=[pl.BlockSpec((None, 8, NIDX), lambda l: (l, 0, 0))],
        out_specs=pl.BlockSpec((None, NBAND * 8, CH * CH), lambda l: (l, 0, 0)),
        out_shape=jax.ShapeDtypeStruct((DEPTH, NBAND * 8, CH * CH), F32),
        compiler_params=_cp("parallel"),
    )(rb)


def _bias_reduce(name, db):
    def body(db_ref, o_ref):
        along = jnp.dot(db_ref[...], _onehot_diagonal(), preferred_element_type=F32, precision=HIGHEST)
        acc = jnp.zeros((8, NIDX), F32)
        for a in range(NBAND):
            acc = acc + jnp.dot(along[8 * a:8 * a + 8, :], _onehot_table(a), preferred_element_type=F32,
                                precision=HIGHEST)
        o_ref[...] = acc

    return pl.pallas_call(
        body, name=name, grid=(DEPTH,),
        in_specs=[pl.BlockSpec((None, NBAND * 8, CH * CH), lambda l: (l, 0, 0))],
        out_specs=pl.BlockSpec((None, 8, NIDX), lambda l: (l, 0, 0)),
        out_shape=jax.ShapeDtypeStruct((DEPTH, 8, NIDX), F32),
        compiler_params=_cp("parallel"),
    )(db)


def _bias_layout(flat):
    b = flat.reshape(DEPTH, NBAND, 8, CH, CH).transpose(0, 2, 1, 4, 3).reshape(DEPTH, 4, 2, KB, CH)
    pair = b.transpose(0, 1, 3, 2, 4).reshape(DEPTH, 4, KB, 128)
    first = jnp.pad(pair, ((0, 0), (0, 0), (0, CH), (0, 0)), constant_values=NEG_INF)
    second = jnp.pad(pair, ((0, 0), (0, 0), (CH, 0), (0, 0)), constant_values=NEG_INF)
    return jnp.concatenate([first, second], axis=3)


def _bias_unlayout(dbt):
    b = dbt.reshape(DEPTH, 4, NBAND, CH, 2, CH)
    return b.transpose(0, 2, 1, 4, 5, 3).reshape(DEPTH, NBAND * 8, CH * CH)


UNIT = 2 * CH
BAND2 = KB + CH


def _pair_weights(xt):
    x = xt.astype(F32)
    row = lax.broadcasted_iota(jnp.int32, (128, UNIT), 0)
    low = lax.broadcasted_iota(jnp.int32, (128, UNIT), 1) < HD
    swapped = pltpu.roll(x, HD, 1)
    same = (row < HD) == low
    first = jnp.where(same, jnp.where(low, x, swapped), 0.0)
    second = jnp.where(same, jnp.where(low, swapped, x), 0.0)
    return jnp.concatenate([first, second], axis=1).astype(BF16)


def _pair_rows(x):
    low = lax.broadcasted_iota(jnp.int32, (CH, 128), 1) < HD
    zero = jnp.zeros((CH, 128), x.dtype)
    parts = []
    for c in range(2):
        xc = x[c * CH:(c + 1) * CH, :]
        parts += [jnp.where(low, xc, zero), jnp.where(low, zero, xc)]
    return jnp.concatenate(parts, axis=0)


def _unpair(raw):
    b0, b1 = raw[:, 0:128], raw[:, 128:256]
    row = lax.broadcasted_iota(jnp.int32, (128, 128), 0)
    low = lax.broadcasted_iota(jnp.int32, (128, 128), 1) < HD
    top = jnp.where(low, b0, pltpu.roll(b1, HD, 1))
    bottom = jnp.where(low, pltpu.roll(b0, HD, 1), b1)
    return jnp.where(row < HD, top, bottom).T


def _scores_t(kb, qw, bias2, row0, padded):
    s = jnp.dot(kb, qw, preferred_element_type=F32) + bias2
    if padded:
        s = jnp.where(row0 + lax.broadcasted_iota(jnp.int32, (BAND2, 256), 0) >= PADR, s, NEG_INF)
    return s


def _unit_loops(s, unit):
    lax.fori_loop(0, PADR // UNIT, lambda u, c: unit(u, True, c), 0, unroll=4)
    lax.fori_loop(PADR // UNIT, s // UNIT, lambda u, c: unit(u, False, c), 0, unroll=7)


def _attn_fwd(name, kp, qt, vt, bias2, l):
    s = qt.shape[1]
    nu = s // UNIT

    def body(k_ref, qt_ref, vt_ref, b_ref, o_ref, lse_ref):
        def unit(u, padded, carry):
            r0 = pl.multiple_of(u * UNIT, UNIT)
            sc = _scores_t(k_ref[pl.ds(r0, BAND2), :], _pair_weights(qt_ref[:, pl.ds(r0, UNIT)]), b_ref[...],
                           r0, padded)
            top = jnp.max(sc, axis=0, keepdims=True)
            e = jnp.exp(sc - top)
            total = jnp.sum(e, axis=0, keepdims=True)
            raw = jnp.dot(vt_ref[:, pl.ds(r0, BAND2)], e.astype(BF16), preferred_element_type=F32) * (1.0 / total)
            o_ref[pl.ds(r0, UNIT), :] = _unpair(raw).astype(BF16)
            lse_ref[u] = jnp.broadcast_to(top + jnp.log(total), (8, 256))
            return carry

        _unit_loops(s, unit)

    return pl.pallas_call(
        body, name=name, grid=(AW // 128,),
        in_specs=[pl.BlockSpec((s + PADR, 128), lambda h: (0, h)),
                  pl.BlockSpec((128, s), lambda h: (h, 0)),
                  pl.BlockSpec((128, s + PADR), lambda h: (h, 0)),
                  pl.BlockSpec((None, None, BAND2, 256), lambda h: (l, h, 0, 0))],
        out_specs=[pl.BlockSpec((s, 128), lambda h: (0, h)),
                   pl.BlockSpec((None, nu, 8, 256), lambda h: (h, 0, 0, 0))],
        out_shape=[jax.ShapeDtypeStruct((s, AW), BF16), jax.ShapeDtypeStruct((4, nu, 8, 256), F32)],
        compiler_params=_cp("parallel"),
    )(kp, qt, vt, bias2)


def _attn_bwd(name, q, qt, kp, kt, vp, bias2, l, do, dot, lse, dl, db_all):
    s = q.shape[0]
    nu = s // UNIT

    def body(q_ref, qt_ref, k_ref, kt_ref, v_ref, b_ref, do_ref, dot_ref, lse_ref, dl_ref, dbin_ref,
             dq_ref, dk_ref, dvb_ref, db_ref, dv_ref):
        del dbin_ref
        dk_ref[...] = jnp.zeros_like(dk_ref)
        dv_ref[...] = jnp.zeros_like(dv_ref)
        db_ref[...] = jnp.zeros_like(db_ref)

        def unit(u, padded, carry):
            r0 = pl.multiple_of(u * UNIT, UNIT)
            rows, band = pl.ds(r0, UNIT), pl.ds(r0, BAND2)
            sc = _scores_t(k_ref[band, :], _pair_weights(qt_ref[:, rows]), b_ref[...], r0, padded)
            pt = jnp.exp(sc - lse_ref[u][0:1, :])
            dpt = jnp.dot(v_ref[band, :], _pair_weights(dot_ref[:, rows]), preferred_element_type=F32)
            ds = pt * (dpt - dl_ref[u][0:1, :])
            db_ref[...] += ds[0:KB, 0:128] + ds[CH:BAND2, 128:256]
            dsb = ds.astype(BF16)
            dq_ref[rows, :] = _unpair(jnp.dot(kt_ref[:, band], dsb, preferred_element_type=F32))
            dk_ref[band, :] += jnp.dot(dsb, _pair_rows(q_ref[rows, :]), preferred_element_type=F32)
            dv_ref[band, :] += jnp.dot(pt.astype(BF16), _pair_rows(do_ref[rows, :]), preferred_element_type=F32)
            return carry

        _unit_loops(s, unit)
        dvb_ref[...] = dv_ref[...].astype(BF16)

    row_q = pl.BlockSpec((s, 128), lambda h: (0, h))
    col_q = pl.BlockSpec((128, s), lambda h: (h, 0))
    row_k = pl.BlockSpec((s + PADR, 128), lambda h: (0, h))
    col_k = pl.BlockSpec((128, s + PADR), lambda h: (h, 0))
    stat = pl.BlockSpec((None, nu, 8, 256), lambda h: (h, 0, 0, 0))
    return pl.pallas_call(
        body, name=name, grid=(AW // 128,),
        in_specs=[row_q, col_q, row_k, col_k, row_k,
                  pl.BlockSpec((None, None, BAND2, 256), lambda h: (l, h, 0, 0)), row_q, col_q, stat, stat, ANY],
        out_specs=[row_q, row_k, row_k, pl.BlockSpec((None, None, KB, 128), lambda h: (l, h, 0, 0))],
        out_shape=[jax.ShapeDtypeStruct((s, AW), F32),
                   jax.ShapeDtypeStruct((s + PADR, AW), F32),
                   jax.ShapeDtypeStruct((s + PADR, AW), BF16),
                   jax.ShapeDtypeStruct((DEPTH, 4, KB, 128), F32)],
        scratch_shapes=[pltpu.VMEM((s + PADR, 128), F32)],
        input_output_aliases={10: 3},
        compiler_params=_cp("parallel"),
    )(q, qt, kp, kt, vp, bias2, do, dot, lse, dl, db_all)


def _rowsum_layout(dl, nu):
    d = dl[:, :8].reshape(nu, 2, CH, 4, 2)
    d = d.transpose(3, 0, 1, 4, 2).reshape(4, nu, 1, 256)
    return jnp.broadcast_to(d, (4, nu, 8, 256))


def _rows_before(cur, prev, k):
    row = lax.broadcasted_iota(jnp.int32, cur.shape, 0)
    return jnp.where(row >= k, pltpu.roll(cur, k, 0), pltpu.roll(prev, k, 0))


def _rows_after(cur, nxt, k):
    n = cur.shape[0]
    row = lax.broadcasted_iota(jnp.int32, cur.shape, 0)
    return jnp.where(row < n - k, pltpu.roll(cur, n - k, 0), pltpu.roll(nxt, n - k, 0))


def _pool_window_lanes():
    lg = lax.broadcasted_iota(jnp.int32, (1, PWD), 1) // 64
    return lg, jnp.where(lg == 0, 2.0, jnp.where(lg == 1, 4.0, jnp.where(lg == 2, 8.0, 16.0))).astype(F32)


def _pool_mean_minus_token(u, up, row0):
    lg, wv = _pool_window_lanes()
    sums = []
    c, p = u, up
    for k in (1, 2, 4, 8):
        c2 = c + _rows_before(c, p, k)
        p = p + pltpu.roll(p, k, 0)
        c = c2
        sums.append(c)
    win = jnp.where(lg == 0, sums[0], jnp.where(lg == 1, sums[1], jnp.where(lg == 2, sums[2], sums[3])))
    pos1 = (row0 + lax.broadcasted_iota(jnp.int32, u.shape, 0) + 1).astype(F32)
    cnt = jnp.minimum(pos1, wv)
    return win / cnt - u, cnt


def _conv_taps(z, zp, w0, w1, w2):
    z1 = _rows_before(z, zp, 1)
    z2 = _rows_before(z, zp, 2)
    return (w0 * z2 + w1 * z1) + w2 * z, z1, z2


CP_TM = 1024
HALO = 16


def _halo_before(tm, col):
    return pl.BlockSpec((HALO, CW), lambda i: (jnp.maximum(i * (tm // HALO) - 1, 0), col))


def _halo_after(tm, col, rows):
    return pl.BlockSpec((HALO, CW), lambda i: (jnp.minimum((i + 1) * (tm // HALO), rows // HALO - 1), col))


def _as_block_end(halo, tm):
    return jnp.concatenate([jnp.zeros((tm - HALO, halo.shape[1]), halo.dtype), halo], axis=0)


def _as_block_start(halo, tm):
    return jnp.concatenate([halo, jnp.zeros((tm - HALO, halo.shape[1]), halo.dtype)], axis=0)


def _convpool_fwd(name, p, o, cw, pwbd, ps):
    s = p.shape[0]
    tm = CP_TM
    nb = s // tm

    def body(gb_ref, gc_ref, hin_ref, u_ref, gcp_ref, hinp_ref, up_ref, o_ref, cw_ref, pw_ref, ps_ref, mix_ref):
        i = pl.program_id(0)
        has_prev = i > 0
        z = gc_ref[...] * hin_ref[...]
        zp = _as_block_end(jnp.where(has_prev, gcp_ref[...] * hinp_ref[...], 0.0), tm)
        y3, _, _ = _conv_taps(z, zp, cw_ref[0:1, :], cw_ref[1:2, :], cw_ref[2:3, :])
        m, _ = _pool_mean_minus_token(u_ref[...], _as_block_end(jnp.where(has_prev, up_ref[...], 0.0), tm), i * tm)
        yp = jnp.dot(m.astype(BF16), pw_ref[...].astype(BF16), preferred_element_type=F32) * ps_ref[...]
        mix_ref[:, 0:AW] = o_ref[...]
        mix_ref[:, AW:AW + CW] = (gb_ref[...] * y3).astype(BF16)
        mix_ref[:, AW + CW:D] = yp.astype(BF16)

    def cur(col):
        return pl.BlockSpec((tm, CW), lambda i: (i, col))

    def whole(a):
        return pl.BlockSpec(a.shape, lambda i: (0,) * a.ndim)

    return pl.pallas_call(
        body, name=name, grid=(nb,),
        in_specs=[cur(6), cur(7), cur(8), cur(9), _halo_before(tm, 7), _halo_before(tm, 8), _halo_before(tm, 9),
                  pl.BlockSpec((tm, AW), lambda i: (i, 0)), whole(cw), whole(pwbd), whole(ps)],
        out_specs=pl.BlockSpec((tm, D), lambda i: (i, 0)),
        out_shape=jax.ShapeDtypeStruct((s, D), BF16),
        compiler_params=_cp("parallel"),
    )(p, p, p, p, p, p, p, o, cw, pwbd, ps)


def _convpool_bwd(name, p, dmix, cw, pwbd, ps):
    s = p.shape[0]
    tm = CP_TM // 2
    nb = s // tm

    def body(gb_ref, gc_ref, hin_ref, u_ref, gcp_ref, hinp_ref, up_ref, gbn_ref, dyc_ref, dyp_ref, dycn_ref, dypn_ref,
             cw_ref, pw_ref, ps_ref, dcp_ref, dw0_ref, dw1_ref, dw2_ref, dps_ref, dpw_ref):
        i = pl.program_id(0)
        has_prev = i > 0
        has_next = i < nb - 1
        w0, w1, w2 = cw_ref[0:1, :], cw_ref[1:2, :], cw_ref[2:3, :]
        gb, gc, hin = gb_ref[...], gc_ref[...], hin_ref[...]
        dyc = dyc_ref[...]
        z = gc * hin
        zp = _as_block_end(jnp.where(has_prev, gcp_ref[...] * hinp_ref[...], 0.0), tm)
        y3, z1, z2 = _conv_taps(z, zp, w0, w1, w2)
        dy3 = dyc * gb
        dy3n = _as_block_start(jnp.where(has_next, dycn_ref[...] * gbn_ref[...], 0.0), tm)
        dz = w2 * dy3 + w1 * _rows_after(dy3, dy3n, 1) + w0 * _rows_after(dy3, dy3n, 2)
        pw = pw_ref[...].astype(BF16)
        psv = ps_ref[...]
        m, cnt = _pool_mean_minus_token(u_ref[...], _as_block_end(jnp.where(has_prev, up_ref[...], 0.0), tm), i * tm)
        mb = m.astype(BF16)
        dyp = dyp_ref[...]
        dmp = (dyp * psv).astype(BF16)
        dmpn = jnp.where(has_next, dypn_ref[...] * psv, 0.0).astype(BF16)
        nt = (((1,), (1,)), ((), ()))
        dm = lax.dot_general(dmp, pw, nt, preferred_element_type=F32)
        dmn = lax.dot_general(dmpn, pw, nt, preferred_element_type=F32)
        lg, wv = _pool_window_lanes()
        cc, cn = dm / cnt, _as_block_start(dmn / wv, tm)
        sums = []
        for k in (1, 2, 4, 8):
            c2 = cc + _rows_after(cc, cn, k)
            cn = cn + pltpu.roll(cn, tm - k, 0)
            cc = c2
            sums.append(cc)
        du = jnp.where(lg == 0, sums[0], jnp.where(lg == 1, sums[1], jnp.where(lg == 2, sums[2], sums[3]))) - dm
        dcp_ref[:, 0:CW] = (dyc * y3).astype(BF16)
        dcp_ref[:, CW:2 * CW] = (dz * hin).astype(BF16)
        dcp_ref[:, 2 * CW:3 * CW] = (dz * gc).astype(BF16)
        dcp_ref[:, 3 * CW:4 * CW] = du.astype(BF16)
        parts = (jnp.sum(dy3 * z2, axis=0, keepdims=True),
                 jnp.sum(dy3 * z1, axis=0, keepdims=True),
                 jnp.sum(dy3 * z, axis=0, keepdims=True),
                 jnp.sum(dyp * jnp.dot(mb, pw, preferred_element_type=F32), axis=0, keepdims=True),
                 lax.dot_general(mb, dmp, (((0,), (0,)), ((), ())), preferred_element_type=F32))
        accs = (dw0_ref, dw1_ref, dw2_ref, dps_ref, dpw_ref)

        @pl.when(i == 0)
        def _():
            for a, v in zip(accs, parts):
                a[...] = v

        @pl.when(i > 0)
        def _():
            for a, v in zip(accs, parts):
                a[...] += v

    def cur(col):
        return pl.BlockSpec((tm, CW), lambda i: (i, col))

    def prev(col):
        return _halo_before(tm, col)

    def nxt(col):
        return _halo_after(tm, col, s)

    def whole(shape):
        return pl.BlockSpec(shape, lambda i: (0,) * len(shape))

    row = jax.ShapeDtypeStruct((1, CW), F32)
    return pl.pallas_call(
        body, name=name, grid=(nb,),
        in_specs=[cur(6), cur(7), cur(8), cur(9), prev(7), prev(8), prev(9), nxt(6),
                  cur(0), cur(1), nxt(0), nxt(1), whole(cw.shape), whole(pwbd.shape), whole(ps.shape)],
        out_specs=[pl.BlockSpec((tm, D), lambda i: (i, 0)), whole((1, CW)), whole((1, CW)), whole((1, CW)),
                   whole((1, PWD)), whole((PWD, PWD))],
        out_shape=[jax.ShapeDtypeStruct((s, D), BF16), row, row, row, row,
                   jax.ShapeDtypeStruct((PWD, PWD), F32)],
        compiler_params=_cp("arbitrary"),
    )(p, p, p, p, p, p, p, p, dmix, dmix, dmix, dmix, cw, pwbd, ps)


def _qkv_bwd(name, p, dq, dkp, dvp, dcp, qg, kg):
    s = p.shape[0]
    tm = 512
    off = PADR // tm

    def body(pq_ref, pk_ref, dq_ref, dk_ref, dv_ref, dcp_ref, qg_ref, kg_ref, dp_ref, dqg_ref, dkg_ref):
        i = pl.program_id(0)
        hm = _head_mean_matrix()

        def nrm_bwd(x, g, dy):
            r = lax.rsqrt(_head_mean(x * x, hm) + EPS)
            xn = x * r
            dxn = dy * g
            dx = r * (dxn - xn * _head_mean(dxn * xn, hm))
            dg = jnp.sum(dy * xn, axis=0, keepdims=True)
            dg = (dg[:, 0:128] + dg[:, 128:256]) + (dg[:, 256:384] + dg[:, 384:512])
            return dx, dg + pltpu.roll(dg, HD, 1)

        dxq, dgq = nrm_bwd(pq_ref[...], qg_ref[...], dq_ref[...] * 0.125)
        dxk, dgk = nrm_bwd(pk_ref[...], kg_ref[...], dk_ref[...])
        dp_ref[:, 0:AW] = dxq.astype(BF16)
        dp_ref[:, AW:2 * AW] = dxk.astype(BF16)
        dp_ref[:, 2 * AW:3 * AW] = dv_ref[...].astype(BF16)
        dp_ref[:, 3 * AW:DIN] = dcp_ref[...]

        @pl.when(i == 0)
        def _():
            dqg_ref[...] = dgq
            dkg_ref[...] = dgk

        @pl.when(i > 0)
        def _():
            dqg_ref[...] += dgq
            dkg_ref[...] += dgk

    gspec = pl.BlockSpec((1, AW), lambda i: (0, 0))
    gout = pl.BlockSpec((1, 128), lambda i: (0, 0))
    return pl.pallas_call(
        body, name=name, grid=(s // tm,),
        in_specs=[pl.BlockSpec((tm, AW), lambda i: (i, 0)), pl.BlockSpec((tm, AW), lambda i: (i, 1)),
                  pl.BlockSpec((tm, AW), lambda i: (i, 0)),
                  pl.BlockSpec((tm, AW), lambda i: (i + off, 0)),
                  pl.BlockSpec((tm, AW), lambda i: (i + off, 0)),
                  pl.BlockSpec((tm, D), lambda i: (i, 0)), gspec, gspec],
        out_specs=[pl.BlockSpec((tm, DIN), lambda i: (i, 0)), gout, gout],
        out_shape=[jax.ShapeDtypeStruct((s, DIN), BF16), jax.ShapeDtypeStruct((1, 128), F32),
                   jax.ShapeDtypeStruct((1, 128), F32)],
        compiler_params=_cp("arbitrary"),
    )(p, p, dq, dkp, dvp, dcp, qg, kg)


def _mm_nt_relu(name, dxb, w, l, a):
    s = dxb.shape[0]
    tm = MM_ROWS

    def body(d_ref, w_ref, a_ref, o_ref):
        df = lax.dot_general(d_ref[...], w_ref[...], NT_DIMS, preferred_element_type=F32)
        o_ref[...] = (df * (2.0 * jnp.maximum(a_ref[...].astype(F32), 0.0))).astype(BF16)

    return pl.pallas_call(
        body, name=name, grid=(s // tm,),
        in_specs=[pl.BlockSpec((tm, D), lambda i: (i, 0)),
                  pl.BlockSpec((None, DFF, D), lambda i: (l, 0, 0)),
                  pl.BlockSpec((tm, DFF), lambda i: (i, 0))],
        out_specs=pl.BlockSpec((tm, DFF), lambda i: (i, 0)),
        out_shape=jax.ShapeDtypeStruct((s, DFF), BF16),
        compiler_params=_cp("parallel"),
    )(dxb, w, a)


def _proj_out_bwd(name, dxb, w, l, mix):
    s = dxb.shape[0]
    tm = _mm_rows(D, D)

    def body(d_ref, w_ref, o_ref, do_ref, dot_ref, dcp_ref, dl_ref):
        d = d_ref[...]
        wa, wc = w_ref[0:AW, :], w_ref[AW:D, :]
        do = lax.dot_general(d, wa, NT_DIMS, preferred_element_type=F32)
        do_ref[...] = do.astype(BF16)
        dot_ref[...] = lax.dot_general(wa, d, NT_DIMS, preferred_element_type=F32).astype(BF16)
        dcp_ref[...] = lax.dot_general(d, wc, NT_DIMS, preferred_element_type=F32)
        head = lax.broadcasted_iota(jnp.int32, (AW, 128), 0) // HD
        pick = jnp.where(head == lax.broadcasted_iota(jnp.int32, (AW, 128), 1), 1.0, 0.0).astype(BF16)
        dl_ref[...] = _two_pass_dot(do * o_ref[...].astype(F32), pick)

    return pl.pallas_call(
        body, name=name, grid=(s // tm,),
        in_specs=[pl.BlockSpec((tm, D), lambda i: (i, 0)),
                  pl.BlockSpec((None, D, D), lambda i: (l, 0, 0)),
                  pl.BlockSpec((tm, AW), lambda i: (i, 0))],
        out_specs=[pl.BlockSpec((tm, AW), lambda i: (i, 0)), pl.BlockSpec((AW, tm), lambda i: (0, i)),
                   pl.BlockSpec((tm, D - AW), lambda i: (i, 0)), pl.BlockSpec((tm, 128), lambda i: (i, 0))],
        out_shape=[jax.ShapeDtypeStruct((s, AW), BF16), jax.ShapeDtypeStruct((AW, s), BF16),
                   jax.ShapeDtypeStruct((s, D - AW), F32), jax.ShapeDtypeStruct((s, 128), F32)],
        compiler_params=_cp("parallel"),
    )(dxb, w, mix)


def _mm_nt_normbwd(name, gy, w, l, x, g, dres, dep):
    s, k = gy.shape
    tm = MM_ROWS

    def body(gy_ref, w_ref, x_ref, g_ref, dr_ref, dep_ref, dx_ref, dxb_ref, dg_ref):
        del dep_ref
        i = pl.program_id(0)
        dh = lax.dot_general(gy_ref[...], w_ref[...], NT_DIMS, preferred_element_type=F32)
        xv = x_ref[...]
        r = _inv_rms(xv)
        xn = xv * r
        dxn = dh * g_ref[...]
        dx = r * (dxn - xn * jnp.mean(dxn * xn, axis=-1, keepdims=True)) + dr_ref[...]
        dx_ref[...] = dx
        dxb_ref[...] = dx.astype(BF16)
        part = jnp.sum(dh * xn, axis=0, keepdims=True)

        @pl.when(i == 0)
        def _():
            dg_ref[...] = part

        @pl.when(i > 0)
        def _():
            dg_ref[...] += part

    blk = pl.BlockSpec((tm, D), lambda i: (i, 0))
    vec = pl.BlockSpec((1, D), lambda i: (0, 0))
    return pl.pallas_call(
        body, name=name, grid=(s // tm,),
        in_specs=[pl.BlockSpec((tm, k), lambda i: (i, 0)),
                  pl.BlockSpec((None, D, k), lambda i: (l, 0, 0)), blk, vec, blk, ANY],
        out_specs=[blk, blk, vec],
        out_shape=[jax.ShapeDtypeStruct((s, D), F32), jax.ShapeDtypeStruct((s, D), BF16),
                   jax.ShapeDtypeStruct((1, D), F32)],
        compiler_params=_cp("arbitrary"),
    )(gy, w, x, g, dres, dep)


def _mm_tn(name, a, b, tma, tnb, relu2=False):
    s, m = a.shape
    n = b.shape[1]

    def body(a_ref, b_ref, o_ref):
        av = _relu2(a_ref[...]) if relu2 else a_ref[...]
        o_ref[...] = lax.dot_general(av, b_ref[...], (((0,), (0,)), ((), ())),
                                     preferred_element_type=F32).astype(BF16)

    return pl.pallas_call(
        body, name=name, grid=(m // tma, n // tnb),
        in_specs=[pl.BlockSpec((s, tma), lambda i, j: (0, i), pipeline_mode=pl.Buffered(1) if m == tma else None),
                  pl.BlockSpec((s, tnb), lambda i, j: (0, j))],
        out_specs=pl.BlockSpec((tma, tnb), lambda i, j: (i, j)),
        out_shape=jax.ShapeDtypeStruct((m, n), BF16),
        compiler_params=_cp("parallel", "parallel"),
    )(a, b)


def _adamw_math(gv, wv, mv, vv):
    mn = ADAM_B1 * mv + (1.0 - ADAM_B1) * gv
    vn = ADAM_B2 * vv + (1.0 - ADAM_B2) * jnp.square(gv)
    m_hat = mn / (1.0 - ADAM_B1 ** ADAM_STEP)
    v_hat = vn / (1.0 - ADAM_B2 ** ADAM_STEP)
    return gv, -ADAM_LR * (m_hat / (jnp.sqrt(v_hat) + ADAM_EPS) + ADAM_WD * wv), mn, vn


def _adamw(name, g, w, m, v):
    r, c = g.shape
    tm = 256 if r % 256 == 0 else r

    def body(g_ref, w_ref, m_ref, v_ref, go_ref, d_ref, mo_ref, vo_ref):
        go_ref[...], d_ref[...], mo_ref[...], vo_ref[...] = _adamw_math(g_ref[...], w_ref[...], m_ref[...], v_ref[...])

    blk = pl.BlockSpec((tm, c), lambda i: (i, 0))
    return pl.pallas_call(
        body, name=name, grid=(r // tm,),
        in_specs=[blk] * 4, out_specs=[blk] * 4,
        out_shape=[jax.ShapeDtypeStruct((r, c), F32)] * 4,
        compiler_params=_cp("parallel"),
    )(g, w, m, v)


def _place():
    x, y, c = lax.axis_index("x"), lax.axis_index("y"), lax.axis_index("c")
    chips = [(1 - x, y), (x, 1 - y), (1 - x, 1 - y)]
    return x, y, c, chips


BLOCK_AXIS = (2, 1, 2, 1)
LARGE_DIMS = ((D, DIN), (D, D), (D, DFF), (DFF, D))


def _full_shape(t, layers, dtype):
    r, c = LARGE_DIMS[t]
    return jax.ShapeDtypeStruct((layers, r, c), dtype)


def _cast_into_full(name, t, shard, b1, dep):
    _, r, c = shard.shape
    tm = min(512, r)
    if BLOCK_AXIS[t] == 1:
        out_spec = pl.BlockSpec((None, tm, c), lambda l, i, br: (l, br[0] * (r // tm) + i, 0))
    else:
        out_spec = pl.BlockSpec((None, tm, c), lambda l, i, br: (l, i, br[0]))

    def body(b_ref, x_ref, dep_ref, o_ref):
        del b_ref, dep_ref
        o_ref[...] = x_ref[...].astype(BF16)

    return pl.pallas_call(
        body, name=name,
        grid_spec=pltpu.PrefetchScalarGridSpec(
            num_scalar_prefetch=1, grid=(DEPTH, r // tm),
            in_specs=[pl.BlockSpec((None, tm, c), lambda l, i, br: (l, i, 0)), ANY],
            out_specs=out_spec),
        out_shape=_full_shape(t, DEPTH, BF16),
        compiler_params=_cp("parallel", "parallel"),
    )(b1, shard, dep)


HBM = pl.BlockSpec(memory_space=pltpu.HBM)
SEM = pl.BlockSpec(memory_space=pltpu.SEMAPHORE)
DATAFLOW = pltpu.SideEffectType.DATAFLOW_SIDE_EFFECTING


def _half(ref, l, t, b, c):
    r, cols = LARGE_DIMS[t]
    if BLOCK_AXIS[t] == 1:
        n = r // 8
        return ref.at[l, pl.ds(pl.multiple_of(b * (2 * n) + c * n, 16), n), :]
    n, w = r // 2, cols // 4
    return ref.at[l, pl.ds(pl.multiple_of(c * n, 16), n), pl.ds(pl.multiple_of(b * w, 128), w)]


def _gather_start(name, layers, ts, fulls, both=False):
    n = len(ts)

    def body(*refs):
        f_refs, sems = refs[n:2 * n], refs[2 * n:2 * n + 2 * len(layers)]
        x, y, c, chips = _place()
        for i, l in enumerate(layers):
            for k, t in enumerate(ts):
                own = _half(f_refs[k], l, t, 2 * x + y, c)
                for j, (cx, cy) in enumerate(chips):
                    pltpu.make_async_remote_copy(src_ref=own, dst_ref=own, send_sem=sems[2 * i].at[3 * t + j],
                                                 recv_sem=sems[2 * i + 1].at[3 * t + j], device_id=(cx, cy, c),
                                                 device_id_type=MESH).start()
                    if both:
                        pltpu.make_async_remote_copy(src_ref=own, dst_ref=own,
                                                     send_sem=sems[2 * i].at[12 + 3 * t + j],
                                                     recv_sem=sems[2 * i + 1].at[12 + 3 * t + j],
                                                     device_id=(cx, cy, 1 - c), device_id_type=MESH).start()
        refs[-1][...] = jnp.zeros((8, 128), F32)

    outs = pl.pallas_call(
        body, name=name,
        in_specs=[HBM] * n,
        out_specs=[HBM] * n + [SEM] * (2 * len(layers)) + [pl.BlockSpec(memory_space=pltpu.VMEM)],
        out_shape=[pltpu.HBM(f.shape, f.dtype) for f in fulls]
        + [pltpu.SemaphoreType.DMA((24,))] * (2 * len(layers)) + [jax.ShapeDtypeStruct((8, 128), F32)],
        input_output_aliases={k: k for k in range(n)},
        compiler_params=pltpu.CompilerParams(has_side_effects=DATAFLOW),
    )(*[pltpu.with_memory_space_constraint(f, pltpu.HBM) for f in fulls])
    return outs[0:n], {l: (outs[n + 2 * i], outs[n + 1 + 2 * i]) for i, l in enumerate(layers)}, outs[-1]


def _gather_wait(name, l, ts, fulls, sems, after, both=False):
    def body(*refs):
        send_sems, recv_sems, f_refs = refs[4], refs[5], refs[7:11]
        x, y, c, chips = _place()
        for t in ts:
            own = _half(f_refs[t], l, t, 2 * x + y, c)
            for j, (cx, cy) in enumerate(chips):
                landed = _half(f_refs[t], l, t, 2 * cx + cy, c)
                pltpu.make_async_remote_copy(src_ref=own, dst_ref=landed, send_sem=send_sems.at[3 * t + j],
                                             recv_sem=recv_sems.at[3 * t + j], device_id=(cx, cy, c),
                                             device_id_type=MESH).wait()
                if both:
                    crossed = _half(f_refs[t], l, t, 2 * cx + cy, 1 - c)
                    pltpu.make_async_remote_copy(src_ref=own, dst_ref=crossed, send_sem=send_sems.at[12 + 3 * t + j],
                                                 recv_sem=recv_sems.at[12 + 3 * t + j], device_id=(cx, cy, 1 - c),
                                                 device_id_type=MESH).wait()

    return pl.pallas_call(
        body, name=name,
        in_specs=[HBM] * 4 + [SEM, SEM, ANY], out_specs=[HBM] * 4,
        out_shape=[pltpu.HBM(s.shape, s.dtype) for s in (_full_shape(t, DEPTH, BF16) for t in range(4))],
        input_output_aliases={t: t for t in range(4)},
        compiler_params=pltpu.CompilerParams(has_side_effects=DATAFLOW),
    )(*fulls, sems[0], sems[1], after)


def _pass_on(name, l, ts, fulls):
    def body(*refs):
        f_refs, send_sems, recv_sems = refs[4:8], refs[8], refs[9]
        x, y, c, chips = _place()

        def copy(t, j, half):
            cx, cy = chips[j]
            part = _half(f_refs[t], l, t, 2 * cx + cy, half)
            return pltpu.make_async_remote_copy(src_ref=part, dst_ref=part, send_sem=send_sems.at[3 * t + j],
                                                recv_sem=recv_sems.at[3 * t + j], device_id=(x, y, 1 - c),
                                                device_id_type=MESH)

        for t in ts:
            for j in range(3):
                copy(t, j, c).start()
        for t in ts:
            for j in range(3):
                copy(t, j, 1 - c).wait_recv()
                copy(t, j, c).wait_send()

    return pl.pallas_call(
        body, name=name,
        in_specs=[ANY] * 4, out_specs=[ANY] * 4,
        out_shape=[_full_shape(t, DEPTH, BF16) for t in range(4)],
        input_output_aliases={t: t for t in range(4)},
        scratch_shapes=[pltpu.SemaphoreType.DMA((12,)), pltpu.SemaphoreType.DMA((12,))],
    )(*fulls)


def _block2d(ref, t, b):
    r, cols = LARGE_DIMS[t]
    if BLOCK_AXIS[t] == 1:
        return ref.at[pl.ds(pl.multiple_of(b * (r // 4), 16), r // 4), :]
    return ref.at[:, pl.ds(pl.multiple_of(b * (cols // 4), 128), cols // 4)]


def _block_dims(t):
    r, cols = LARGE_DIMS[t]
    return (r // 4, cols) if BLOCK_AXIS[t] == 1 else (r, cols // 4)


def _reduce_copies(ts, g_refs, r_refs, send_sems, recv_sems):
    _, _, c, chips = _place()
    return [pltpu.make_async_remote_copy(src_ref=_block2d(g_refs[i], t, 2 * cx + cy), dst_ref=r_refs[i].at[j],
                                         send_sem=send_sems.at[3 * i + j], recv_sem=recv_sems.at[3 * i + j],
                                         device_id=(cx, cy, c), device_id_type=MESH)
            for i, t in enumerate(ts) for j, (cx, cy) in enumerate(chips)]


def _reduce_start(name, ts, grads):
    n = len(ts)

    def body(*refs):
        for cp in _reduce_copies(ts, refs[n:2 * n], refs[2 * n:3 * n], refs[3 * n], refs[3 * n + 1]):
            cp.start()
        refs[3 * n + 2][...] = jnp.zeros((8, 128), F32)

    outs = pl.pallas_call(
        body, name=name,
        in_specs=[HBM] * n,
        out_specs=[HBM] * (2 * n) + [SEM, SEM, pl.BlockSpec(memory_space=pltpu.VMEM)],
        out_shape=[pltpu.HBM(g.shape, BF16) for g in grads]
        + [pltpu.HBM((3,) + _block_dims(t), BF16) for t in ts]
        + [pltpu.SemaphoreType.DMA((3 * n,)), pltpu.SemaphoreType.DMA((3 * n,)), jax.ShapeDtypeStruct((8, 128), F32)],
        input_output_aliases={i: i for i in range(n)},
        compiler_params=pltpu.CompilerParams(has_side_effects=DATAFLOW),
    )(*[pltpu.with_memory_space_constraint(g, pltpu.HBM) for g in grads])
    return outs[0:n], outs[n:2 * n], (outs[2 * n], outs[2 * n + 1]), outs[2 * n + 2]


def _reduce_wait(name, ts, grads, landing, sems, afters):
    n = len(ts)
    first_out = 2 * n + 2 + len(afters)

    def body(*refs):
        for cp in _reduce_copies(ts, refs[first_out:first_out + n], refs[first_out + n:first_out + 2 * n],
                                 refs[2 * n], refs[2 * n + 1]):
            cp.wait()

    outs = pl.pallas_call(
        body, name=name,
        in_specs=[HBM] * (2 * n) + [SEM, SEM] + [ANY] * len(afters), out_specs=[HBM] * (2 * n),
        out_shape=[pltpu.HBM(g.shape, BF16) for g in grads] + [pltpu.HBM(r.shape, BF16) for r in landing],
        input_output_aliases={i: i for i in range(2 * n)},
        compiler_params=pltpu.CompilerParams(has_side_effects=DATAFLOW),
    )(*grads, *landing, sems[0], sems[1], *afters)
    return outs[0:n], outs[n:2 * n]


def _add4(name, t, own, landed, b1):
    rb, cb = _block_dims(t)
    tm = min(512, rb)
    if BLOCK_AXIS[t] == 1:
        own_spec = pl.BlockSpec((tm, cb), lambda i, br: (br[0] * (rb // tm) + i, 0))
    else:
        own_spec = pl.BlockSpec((tm, cb), lambda i, br: (i, br[0]))

    def body(b_ref, o_ref, r0_ref, r1_ref, r2_ref, s_ref):
        del b_ref
        s_ref[...] = ((o_ref[...].astype(F32) + r0_ref[...].astype(F32))
                      + (r1_ref[...].astype(F32) + r2_ref[...].astype(F32))).astype(BF16)

    def got(j):
        return pl.BlockSpec((None, tm, cb), lambda i, br: (j, i, 0))

    return pl.pallas_call(
        body, name=name,
        grid_spec=pltpu.PrefetchScalarGridSpec(
            num_scalar_prefetch=1, grid=(rb // tm,),
            in_specs=[own_spec, got(0), got(1), got(2)],
            out_specs=pl.BlockSpec((tm, cb), lambda i, br: (i, 0))),
        out_shape=jax.ShapeDtypeStruct((rb, cb), BF16),
        compiler_params=_cp("parallel"),
    )(b1, own, landed, landed, landed)


def _swap_sib(name, sums):
    def body(*refs):
        s_refs, t_refs, send_sems, recv_sems = refs[0:4], refs[4:8], refs[8], refs[9]
        x, y, c, _ = _place()
        cps = [pltpu.make_async_remote_copy(src_ref=s_refs[t], dst_ref=t_refs[t], send_sem=send_sems.at[t],
                                            recv_sem=recv_sems.at[t], device_id=(x, y, 1 - c), device_id_type=MESH)
               for t in range(4)]
        for cp in cps:
            cp.start()
        for cp in cps:
            cp.wait()

    return pl.pallas_call(
        body, name=name,
        in_specs=[ANY] * 4, out_specs=[ANY] * 4,
        out_shape=[jax.ShapeDtypeStruct(s.shape, BF16) for s in sums],
        scratch_shapes=[pltpu.SemaphoreType.DMA((4,)), pltpu.SemaphoreType.DMA((4,))],
    )(*sums)


def _adamw_pair(name, l, s_own, s_sib, w, m, v, outs):
    rb, cb = s_own.shape
    tm = min(512, rb)

    def body(a_ref, b_ref, w_ref, m_ref, v_ref, g0, d0, m0, v0, go_ref, d_ref, mo_ref, vo_ref):
        del g0, d0, m0, v0
        gv = a_ref[...].astype(F32) + b_ref[...].astype(F32)
        go_ref[...], d_ref[...], mo_ref[...], vo_ref[...] = _adamw_math(gv, w_ref[...], m_ref[...], v_ref[...])

    part = pl.BlockSpec((tm, cb), lambda i: (i, 0))
    layer = pl.BlockSpec((None, tm, cb), lambda i: (l, i, 0))
    return pl.pallas_call(
        body, name=name, grid=(rb // tm,),
        in_specs=[part, part, layer, layer, layer] + [ANY] * 4,
        out_specs=[layer] * 4,
        out_shape=[jax.ShapeDtypeStruct((DEPTH, rb, cb), F32)] * 4,
        input_output_aliases={5 + i: i for i in range(4)},
        compiler_params=_cp("parallel"),
    )(s_own, s_sib, w, m, v, *outs)


def _all_gather8(name, v, dep):
    m_per, n = v.shape

    def body(v_ref, dep_ref, out_ref, send_sems, recv_sems, local_sem):
        del dep_ref
        x, y, c, chips = _place()
        me, sib = (x, y, c), (x, y, 1 - c)

        def rows(px, py, pc):
            return out_ref.at[pl.ds((4 * px + 2 * py + pc) * m_per, m_per), :]

        def copy(k, block, to, src=None):
            return pltpu.make_async_remote_copy(
                src_ref=rows(*block) if src is None else src, dst_ref=rows(*block),
                send_sem=send_sems.at[k], recv_sem=recv_sems.at[k], device_id=to, device_id_type=MESH)

        mine = pltpu.make_async_copy(v_ref, rows(*me), local_sem)
        mine.start()
        first = [copy(0, me, sib, src=v_ref)]
        first += [copy(1 + j, me, (*chip, c), src=v_ref) for j, chip in enumerate(chips)]
        for cp in first:
            cp.start()
        passed = [copy(4 + j, (*chip, c), sib) for j, chip in enumerate(chips)]
        for j, chip in enumerate(chips):
            copy(1 + j, (*chip, c), me).wait_recv()
            passed[j].start()
        copy(0, sib, me).wait_recv()
        for j, chip in enumerate(chips):
            copy(4 + j, (*chip, 1 - c), me).wait_recv()
        for cp in first + passed:
            cp.wait_send()
        mine.wait()

    return pl.pallas_call(
        body, name=name,
        out_shape=jax.ShapeDtypeStruct((8 * m_per, n), v.dtype),
        in_specs=[pl.BlockSpec(memory_space=pltpu.VMEM), ANY],
        out_specs=pl.BlockSpec(memory_space=pltpu.VMEM),
        scratch_shapes=[pltpu.SemaphoreType.DMA((7,)), pltpu.SemaphoreType.DMA((7,)), pltpu.SemaphoreType.DMA],
    )(v, dep)


def _sum8(name, g):
    def body(g_ref, o_ref):
        acc = g_ref[0]
        for d in range(1, 8):
            acc = acc + g_ref[d]
        o_ref[...] = acc

    return pl.pallas_call(body, name=name, out_shape=jax.ShapeDtypeStruct(g.shape[1:], F32))(g)


def _pack(parts):
    flat = []
    for a in parts:
        a = a.reshape(-1)
        flat.append(jnp.pad(a, (0, (-a.shape[0]) % 128)))
    cat = jnp.concatenate(flat)
    cat = jnp.pad(cat, (0, (-cat.shape[0]) % 1024))
    return cat.reshape(-1, 128)


def _unpack(packed, shapes):
    flat = packed.reshape(-1)
    out, at = [], 0
    for shp in shapes:
        n = 1
        for d in shp:
            n *= d
        out.append(flat[at:at + n].reshape(shp))
        at += n + (-n) % 128
    return out


def _local_step(x, target, layer_weights, on_grads, small):
    qg_all = jnp.tile(small["q_norm_g"], (1, 8))
    kg_all = jnp.tile(small["k_norm_g"], (1, 8))
    bias_all = _bias_layout(_bias_expand("bias_expand", jnp.pad(small["rel_bias"], ((0, 0), (0, 0), (0, NIDX - 257)))))
    same_group = jnp.eye(4, dtype=F32)[None, :, None, :, None]
    pwbd_all = (small["pool_w"][:, :, :, None, :] * same_group).reshape(DEPTH, PWD, PWD)
    saved = []
    xin = x
    h = _rmsnorm("norm_first", x, small["norm1_g"][0:1])
    for l in range(DEPTH):
        w_in = layer_weights(l, (0,), xin)[0]
        qg, kg = qg_all[l:l + 1], kg_all[l:l + 1]
        cw, pwbd, ps = small["conv_w"][l], pwbd_all[l], small["pool_scale"][l:l + 1]
        p = _mm_nn(f"proj_in_{l}", h, w_in, l, F32)
        q, qt, kp, kt, vp, vt = _qkv(f"qkv_{l}", p, qg, kg)
        o, lse = _attn_fwd(f"attn_fwd_{l}", kp, qt, vt, bias_all, l)
        w_in, w_out, w_1, w_2 = layer_weights(l, (1, 2, 3), o)
        mix = _convpool_fwd(f"convpool_fwd_{l}", p, o, cw, pwbd, ps)
        x1, h2 = _mm_res_norm(f"proj_out_{l}", mix, w_out, l, xin, small["norm2_g"][l:l + 1])
        saved.append(dict(xin=xin, h=h, p=p, q=q, qt=qt, kp=kp, kt=kt, vp=vp, mix=mix, x1=x1, h2=h2, lse=lse,
                          qg=qg, kg=kg, cw=cw, pwbd=pwbd, ps=ps))
        if l + 1 < DEPTH:
            saved[l]["a"], xin, h = _mlp_fwd(f"mlp_{l}", h2, w_1, w_2, l, x1, small["norm1_g"][l + 1:l + 2])
        else:
            saved[l]["a"], dx, dxb, loss = _mlp_fwd(f"mlp_{l}", h2, w_1, w_2, l, x1, target)

    raw = {k: [None] * DEPTH for k in ("dg1", "dqg", "dkg", "dw0", "dw1", "dw2", "dpw", "dps", "dg2")}
    db_all = lax.empty((DEPTH, 4, KB, 128), F32)
    for l in reversed(range(DEPTH)):
        sv = saved[l]
        da = _mm_nt_relu(f"mlp2_bwd_{l}", dxb, w_2, l, sv["a"])
        g_2 = _mm_tn(f"mlp2_wgrad_{l}", sv["a"], dxb, 512, 1024, relu2=True)
        g_1 = _mm_tn(f"mlp1_wgrad_{l}", sv["h2"], da, 1024, 512)
        dep = on_grads(l, (2, 3), (g_1, g_2))
        dx1, dx1b, dg2 = _mm_nt_normbwd(f"mlp1_bwd_{l}", da, w_1, l, sv["x1"], small["norm2_g"][l:l + 1], dx, dep)
        do, dot, dmix, dl = _proj_out_bwd(f"proj_out_bwd_{l}", dx1b, w_out, l, sv["mix"])
        g_out = _mm_tn(f"proj_out_wgrad_{l}", sv["mix"], dx1b, 512, 1024)
        dcp, dw0, dw1, dw2, dps, dpw = _convpool_bwd(f"convpool_bwd_{l}", sv["p"], dmix, sv["cw"], sv["pwbd"], sv["ps"])
        dq, dkp, dvp, db_all = _attn_bwd(f"attn_bwd_{l}", sv["q"], sv["qt"], sv["kp"], sv["kt"], sv["vp"], bias_all, l,
                                     do, dot, sv["lse"], _rowsum_layout(dl, x.shape[0] // UNIT), db_all)
        dp, dqg, dkg = _qkv_bwd(f"qkv_bwd_{l}", sv["p"], dq, dkp, dvp, dcp, sv["qg"], sv["kg"])
        g_in = _mm_tn(f"proj_in_wgrad_{l}", sv["h"], dp, 1024, 1280)
        dep = on_grads(l, (0, 1), (g_in, g_out))
        dx, dxb, dg1 = _mm_nt_normbwd(f"proj_in_bwd_{l}", dp, w_in, l, sv["xin"], small["norm1_g"][l:l + 1], dx1, dep)
        for k, val in dict(dg1=dg1, dqg=dqg, dkg=dkg, dw0=dw0, dw1=dw1, dw2=dw2, dpw=dpw, dps=dps, dg2=dg2).items():
            raw[k][l] = val
    cat = {k: jnp.concatenate(v, axis=0) for k, v in raw.items() if k != "dpw"}
    drb = _bias_reduce("bias_reduce", _bias_unlayout(db_all))
    dpw = jnp.stack(raw["dpw"])
    gsmall = {
        "norm1_g": cat["dg1"], "q_norm_g": cat["dqg"][:, :HD], "k_norm_g": cat["dkg"][:, :HD],
        "rel_bias": drb[:, :, :257],
        "conv_w": jnp.stack([cat["dw0"], cat["dw1"], cat["dw2"]], axis=1),
        "pool_w": jnp.stack([dpw[:, g * 64:(g + 1) * 64, g * 64:(g + 1) * 64] for g in range(4)], axis=1),
        "pool_scale": cat["dps"], "norm2_g": cat["dg2"],
    }
    return loss, dx, gsmall


SMALL = ("norm1_g", "q_norm_g", "k_norm_g", "rel_bias", "conv_w", "pool_w", "pool_scale", "norm2_g")
LARGE = ("w_in", "w_out", "w_mlp1", "w_mlp2")


def kernel(x, norm1_g, w_in, q_norm_g, k_norm_g, rel_bias, conv_w, pool_w, pool_scale, w_out, norm2_g, w_mlp1, w_mlp2, loss_target, m_norm1_g, m_w_in, m_q_norm_g, m_k_norm_g, m_rel_bias, m_conv_w, m_pool_w, m_pool_scale, m_w_out, m_norm2_g, m_w_mlp1, m_w_mlp2, v_norm1_g, v_w_in, v_q_norm_g, v_k_norm_g, v_rel_bias, v_conv_w, v_pool_w, v_pool_scale, v_w_out, v_norm2_g, v_w_mlp1, v_w_mlp2):
    w = dict(norm1_g=norm1_g, w_in=w_in, q_norm_g=q_norm_g, k_norm_g=k_norm_g, rel_bias=rel_bias, conv_w=conv_w,
             pool_w=pool_w, pool_scale=pool_scale, w_out=w_out, norm2_g=norm2_g, w_mlp1=w_mlp1, w_mlp2=w_mlp2)
    m = dict(norm1_g=m_norm1_g, w_in=m_w_in, q_norm_g=m_q_norm_g, k_norm_g=m_k_norm_g, rel_bias=m_rel_bias,
             conv_w=m_conv_w, pool_w=m_pool_w, pool_scale=m_pool_scale, w_out=m_w_out, norm2_g=m_norm2_g,
             w_mlp1=m_w_mlp1, w_mlp2=m_w_mlp2)
    v = dict(norm1_g=v_norm1_g, w_in=v_w_in, q_norm_g=v_q_norm_g, k_norm_g=v_k_norm_g, rel_bias=v_rel_bias,
             conv_w=v_conv_w, pool_w=v_pool_w, pool_scale=v_pool_scale, w_out=v_w_out, norm2_g=v_norm2_g,
             w_mlp1=v_w_mlp1, w_mlp2=v_w_mlp2)
    ax, ay, ac = lax.axis_index("x"), lax.axis_index("y"), lax.axis_index("c")
    b1 = jnp.reshape(2 * ax + ay, (1,)).astype(jnp.int32)

    cw_rows = _all_gather8("gather_conv_w", jnp.pad(conv_w.reshape(DEPTH * 3, 64), ((0, 4), (0, 64))), b1)
    cw_chips = [cw_rows[(4 * cx + 2 * cy) * 16:(4 * cx + 2 * cy) * 16 + 12, :64] for cx in range(2) for cy in range(2)]
    small = {n: w[n] for n in SMALL}
    small["conv_w"] = jnp.concatenate(cw_chips, axis=1).reshape(DEPTH, 3, CW)

    (w_in_full,), in_sems, in_token = _gather_start(
        "gather_start_in", (0,), (0,), [_cast_into_full("cast_w_in", 0, w["w_in"], b1, cw_rows)])
    others, first_sems, first_token = _gather_start(
        "gather_start_first", (0,), (1, 2, 3),
        [_cast_into_full(f"cast_{LARGE[t]}", t, w[LARGE[t]], b1, in_token) for t in (1, 2, 3)])
    held = [[w_in_full] + list(others)]
    sems = {(0, 0): in_sems[0], (0, 1): first_sems[0]}

    def layer_weights(l, ts, after):
        if l > 0:
            ts = (0, 1, 2, 3) if ts == (0,) else ()
        if ts:
            tag = f"{l}_{ts[0]}"
            first_in = l == 0 and ts == (0,)
            after = first_token if first_in else after
            arrived = _gather_wait(f"gather_wait_{tag}", l, ts, held[0], sems[l, ts[0] if l == 0 else 0], after,
                                   both=l > 0)
            if first_in:
                arrived, rest_sems, _ = _gather_start("gather_start_rest", tuple(range(1, DEPTH)), (0, 1, 2, 3),
                                                      arrived, both=True)
                sems.update({(k, 0): v for k, v in rest_sems.items()})
            held[0] = _pass_on(f"pass_on_{tag}", l, ts, arrived) if l == 0 else arrived
        return held[0]

    flights = {}

    def await_flight(l, ts, afters):
        g, landing, sm, _ = flights[l, ts]
        flights[l, ts] = _reduce_wait(f"reduce_wait_{l}_{ts[0]}", ts, g, landing, sm, afters)

    def on_grads(l, ts, grads):
        if ts == (0, 1) and l + 1 < DEPTH:
            await_flight(l + 1, (2, 3), [grads[0]])
            await_flight(l + 1, (0, 1), [grads[0]])
        flights[l, ts] = _reduce_start(f"reduce_start_{l}_{ts[0]}", ts, grads)
        return flights[l, ts][3]

    loss_part, grad_x, gsmall = _local_step(x[0], loss_target[0], layer_weights, on_grads, small)
    loss = lax.psum(loss_part[0, 0], ("x", "y", "c"))
    order = [n for n in SMALL]
    packed = _pack([gsmall[n] for n in order])

    out = {n: [lax.empty(w[n].shape, F32) for _ in range(4)] for n in LARGE}
    for l in reversed(range(DEPTH)):
        if l == 0:
            afters = [grad_x, packed] + [out[n][0] for n in LARGE]
            await_flight(0, (2, 3), afters)
            await_flight(0, (0, 1), afters)
        sums = [None] * 4
        for ts in ((0, 1), (2, 3)):
            g, landing = flights[l, ts]
            for i, t in enumerate(ts):
                sums[t] = _add4(f"add4_{LARGE[t]}_{l}", t, g[i], landing[i], b1)
        theirs = _swap_sib(f"swap_sib_{l}", sums)
        for t, n in enumerate(LARGE):
            out[n] = _adamw_pair(f"adamw_{n}_{l}", l, sums[t], theirs[t], w[n], m[n], v[n], out[n])

    rows = packed.shape[0]
    summed = _sum8("sum_small", _all_gather8("gather_small", packed, out[LARGE[0]][0]).reshape(8, rows, 128))
    gfull = dict(zip(order, _unpack(summed, [gsmall[n].shape for n in order])))
    gfull["conv_w"] = lax.dynamic_slice_in_dim(gfull["conv_w"], (2 * ax + ay) * 64, 64, axis=2)
    res = _adamw("adamw_small", _pack([gfull[n] for n in order]), _pack([w[n] for n in order]),
                 _pack([m[n] for n in order]), _pack([v[n] for n in order]))
    for n, parts in zip(order, zip(*[_unpack(r, [w[k].shape for k in order]) for r in res])):
        out[n] = list(parts)

    names = ("norm1_g", "w_in", "q_norm_g", "k_norm_g", "rel_bias", "conv_w", "pool_w", "pool_scale", "w_out",
             "norm2_g", "w_mlp1", "w_mlp2")
    flat = [loss, grad_x[None]]
    for i in range(4):
        flat += [out[n][i] for n in names]
    return tuple(flat)
```

```python
import jax
import jax.numpy as jnp
from jax import lax
from jax.experimental import pallas as pl
from jax.experimental.pallas import tpu as pltpu

F32 = jnp.float32
BF16 = jnp.bfloat16

D = 1024
DEPTH = 4
CH = 64
NPREV = 8
KB = (NPREV + 1) * CH
PADR = NPREV * CH
HD = 64
AW = 512
CW = 256
PWD = 256
DIN = 3 * AW + 3 * CW + PWD
DFF = 4 * D
NIDX = 384
EPS = 1e-6
NEG_INF = -1e30

ADAM_LR = 0.001
ADAM_B1 = 0.9
ADAM_B2 = 0.999
ADAM_EPS = 1e-08
ADAM_WD = 0.01
ADAM_STEP = 10

VMEM_LIMIT = 52 * 1024 * 1024
MM_ROWS = 512


def _mm_rows(k, n):
    return 2 * MM_ROWS if k + n <= 2048 else MM_ROWS


MESH = pl.DeviceIdType.MESH
ANY = pl.BlockSpec(memory_space=pl.ANY)


def _cp(*sem):
    return pltpu.CompilerParams(dimension_semantics=sem, vmem_limit_bytes=VMEM_LIMIT)


def _inv_rms(x):
    return lax.rsqrt(jnp.mean(x * x, axis=-1, keepdims=True) + EPS)


def _head_mean_matrix():
    r = lax.broadcasted_iota(jnp.int32, (AW, AW), 0) // HD
    c = lax.broadcasted_iota(jnp.int32, (AW, AW), 1) // HD
    return jnp.where(r == c, 1.0 / HD, 0.0).astype(BF16)


def _two_pass_dot(x, m):
    hi = x.astype(BF16)
    lo = (x - hi.astype(F32)).astype(BF16)
    return (jnp.dot(hi, m, preferred_element_type=F32)
            + jnp.dot(lo, m, preferred_element_type=F32))


def _head_mean(x, hm):
    return _two_pass_dot(x, hm)


def _rmsnorm(name, x, g):
    s = x.shape[0]
    tm = 512

    def body(x_ref, g_ref, h_ref):
        xv = x_ref[...]
        h_ref[...] = (xv * _inv_rms(xv) * g_ref[...]).astype(BF16)

    return pl.pallas_call(
        body, name=name, grid=(s // tm,),
        in_specs=[pl.BlockSpec((tm, D), lambda i: (i, 0)), pl.BlockSpec((1, D), lambda i: (0, 0))],
        out_specs=pl.BlockSpec((tm, D), lambda i: (i, 0)),
        out_shape=jax.ShapeDtypeStruct((s, D), BF16),
        compiler_params=_cp("parallel"),
    )(x, g)


def _relu2(a):
    r = jnp.maximum(a, jnp.zeros_like(a))
    return r * r


def _mm_nn(name, a, w, l, out_dtype):
    s, k = a.shape
    n = w.shape[2]
    tm = _mm_rows(k, n)

    def body(a_ref, w_ref, o_ref):
        o_ref[...] = jnp.dot(a_ref[...], w_ref[...], preferred_element_type=F32).astype(o_ref.dtype)

    return pl.pallas_call(
        body, name=name, grid=(s // tm,),
        in_specs=[pl.BlockSpec((tm, k), lambda i: (i, 0)),
                  pl.BlockSpec((None, k, n), lambda i: (l, 0, 0))],
        out_specs=pl.BlockSpec((tm, n), lambda i: (i, 0)),
        out_shape=jax.ShapeDtypeStruct((s, n), out_dtype),
        compiler_params=_cp("parallel"),
    )(a, w)


def _mm_res_norm(name, a, w, l, res, g):
    s, k = a.shape
    tm = _mm_rows(k, D)

    def body(a_ref, w_ref, r_ref, g_ref, x_ref, h_ref):
        acc = r_ref[...] + jnp.dot(a_ref[...], w_ref[...], preferred_element_type=F32)
        x_ref[...] = acc
        h_ref[...] = (acc * _inv_rms(acc) * g_ref[...]).astype(BF16)

    return pl.pallas_call(
        body, name=name, grid=(s // tm,),
        in_specs=[pl.BlockSpec((tm, k), lambda i: (i, 0)),
                  pl.BlockSpec((None, k, D), lambda i: (l, 0, 0)),
                  pl.BlockSpec((tm, D), lambda i: (i, 0)),
                  pl.BlockSpec((1, D), lambda i: (0, 0))],
        out_specs=[pl.BlockSpec((tm, D), lambda i: (i, 0))] * 2,
        out_shape=[jax.ShapeDtypeStruct((s, D), F32), jax.ShapeDtypeStruct((s, D), BF16)],
        compiler_params=_cp("parallel"),
    )(a, w, res, g)


def _mlp_fwd(name, h2, w1, w2, l, res, last):
    s = h2.shape[0]
    tm = 256
    final = last.shape[0] == s

    def body(h_ref, w1_ref, w2_ref, r_ref, last_ref, a_ref, first_ref, second_ref, *loss_ref):
        a = jnp.dot(h_ref[...], w1_ref[...], preferred_element_type=F32).astype(BF16)
        a_ref[...] = a
        acc = r_ref[...] + jnp.dot(_relu2(a), w2_ref[...], preferred_element_type=F32)
        if not final:
            first_ref[...] = acc
            second_ref[...] = (acc * _inv_rms(acc) * last_ref[...]).astype(BF16)
            return
        e = acc - last_ref[...]
        dy = e * (1.0 / D)
        first_ref[...] = dy
        second_ref[...] = dy.astype(BF16)
        part = 0.5 * jnp.sum(jnp.mean(e * e, axis=-1, keepdims=True), axis=0, keepdims=True)
        i = pl.program_id(0)

        @pl.when(i == 0)
        def _():
            loss_ref[0][...] = part

        @pl.when(i > 0)
        def _():
            loss_ref[0][...] += part

    once = pl.Buffered(1)
    rows = pl.BlockSpec((tm, D), lambda i: (i, 0))
    one = pl.BlockSpec((1, 1), lambda i: (0, 0))
    return pl.pallas_call(
        body, name=name, grid=(s // tm,),
        in_specs=[rows,
                  pl.BlockSpec((None, D, DFF), lambda i: (l, 0, 0), pipeline_mode=once),
                  pl.BlockSpec((None, DFF, D), lambda i: (l, 0, 0), pipeline_mode=once),
                  rows, rows if final else pl.BlockSpec((1, D), lambda i: (0, 0))],
        out_specs=[pl.BlockSpec((tm, DFF), lambda i: (i, 0)), rows, rows] + ([one] if final else []),
        out_shape=[jax.ShapeDtypeStruct((s, DFF), BF16), jax.ShapeDtypeStruct((s, D), F32),
                   jax.ShapeDtypeStruct((s, D), BF16)] + ([jax.ShapeDtypeStruct((1, 1), F32)] if final else []),
        compiler_params=_cp("arbitrary" if final else "parallel"),
    )(h2, w1, w2, res, last)


def _qkv(name, p, qg, kg):
    s = p.shape[0]
    tm = PADR
    nb = s // tm

    def body(pq_ref, pk_ref, pv_ref, qg_ref, kg_ref, q_ref, qt_ref, k_ref, kt_ref, v_ref, vt_ref):
        t = pl.program_id(0)
        hm = _head_mean_matrix()

        def nrm(x, g):
            return x * lax.rsqrt(_head_mean(x * x, hm) + EPS) * g

        first = t == 0
        qq = nrm(pq_ref[...], qg_ref[...]) * 0.125
        kk = jnp.where(first, 0.0, nrm(pk_ref[...], kg_ref[...]))
        vv = jnp.where(first, 0.0, pv_ref[...])
        q_ref[...] = qq.astype(BF16)
        qt_ref[...] = qq.T.astype(BF16)
        k_ref[...] = kk.astype(BF16)
        kt_ref[...] = kk.T.astype(BF16)
        v_ref[...] = vv.astype(BF16)
        vt_ref[...] = vv.T.astype(BF16)

    def src(col):
        return pl.BlockSpec((tm, AW), lambda t: (jnp.maximum(t - 1, 0), col))

    gspec = pl.BlockSpec((1, AW), lambda t: (0, 0))
    rows = pl.BlockSpec((tm, AW), lambda t: (t, 0))
    cols = pl.BlockSpec((AW, tm), lambda t: (0, t))
    return pl.pallas_call(
        body, name=name, grid=(nb + 1,),
        in_specs=[src(0), src(1), src(2), gspec, gspec],
        out_specs=[pl.BlockSpec((tm, AW), lambda t: (jnp.maximum(t - 1, 0), 0)),
                   pl.BlockSpec((AW, tm), lambda t: (0, jnp.maximum(t - 1, 0))),
                   rows, cols, rows, cols],
        out_shape=[jax.ShapeDtypeStruct((s, AW), BF16), jax.ShapeDtypeStruct((AW, s), BF16),
                   jax.ShapeDtypeStruct((s + PADR, AW), BF16), jax.ShapeDtypeStruct((AW, s + PADR), BF16),
                   jax.ShapeDtypeStruct((s + PADR, AW), BF16), jax.ShapeDtypeStruct((AW, s + PADR), BF16)],
        compiler_params=_cp("arbitrary"),
    )(p, p, p, qg, kg)


NBAND = KB // CH
HIGHEST = lax.Precision.HIGHEST
NT_DIMS = (((1,), (1,)), ((), ()))


def _onehot_table(a):
    m = lax.broadcasted_iota(jnp.int32, (128, NIDX), 0)
    idx = lax.broadcasted_iota(jnp.int32, (128, NIDX), 1)
    rel = jnp.clip(KB - 1 - (CH * a + m), -128, 128) + 128
    return jnp.where(rel == idx, 1.0, 0.0).astype(F32)


def _onehot_diagonal():
    r = lax.broadcasted_iota(jnp.int32, (CH * CH, 128), 0)
    m = lax.broadcasted_iota(jnp.int32, (CH * CH, 128), 1)
    return jnp.where((r % CH) - (r // CH) + (CH - 1) == m, 1.0, 0.0).astype(F32)


def _bias_expand(name, rb):
    def body(rb_ref, o_ref):
        along = [lax.dot_general(rb_ref[...], _onehot_table(a), NT_DIMS, preferred_element_type=F32,
                                 precision=HIGHEST) for a in range(NBAND)]
        o_ref[...] = lax.dot_general(jnp.concatenate(along, axis=0), _onehot_diagonal(), NT_DIMS,
                                     preferred_element_type=F32, precision=HIGHEST)

    return pl.pallas_call(
        body, name=name, grid=(DEPTH,),
        in_specs=[pl.BlockSpec((None, 8, NIDX), lambda l: (l, 0, 0))],
        out_specs=pl.BlockSpec((None, NBAND * 8, CH * CH), lambda l: (l, 0, 0)),
        out_shape=jax.ShapeDtypeStruct((DEPTH, NBAND * 8, CH * CH), F32),
        compiler_params=_cp("parallel"),
    )(rb)


def _bias_reduce(name, db):
    def body(db_ref, o_ref):
        along = jnp.dot(db_ref[...], _onehot_diagonal(), preferred_element_type=F32, precision=HIGHEST)
        acc = jnp.zeros((8, NIDX), F32)
        for a in range(NBAND):
            acc = acc + jnp.dot(along[8 * a:8 * a + 8, :], _onehot_table(a), preferred_element_type=F32,
                                precision=HIGHEST)
        o_ref[...] = acc

    return pl.pallas_call(
        body, name=name, grid=(DEPTH,),
        in_specs=[pl.BlockSpec((None, NBAND * 8, CH * CH), lambda l: (l, 0, 0))],
        out_specs=pl.BlockSpec((None, 8, NIDX), lambda l: (l, 0, 0)),
        out_shape=jax.ShapeDtypeStruct((DEPTH, 8, NIDX), F32),
        compiler_params=_cp("parallel"),
    )(db)


def _bias_layout(flat):
    b = flat.reshape(DEPTH, NBAND, 8, CH, CH).transpose(0, 2, 1, 4, 3).reshape(DEPTH, 4, 2, KB, CH)
    pair = b.transpose(0, 1, 3, 2, 4).reshape(DEPTH, 4, KB, 128)
    first = jnp.pad(pair, ((0, 0), (0, 0), (0, CH), (0, 0)), constant_values=NEG_INF)
    second = jnp.pad(pair, ((0, 0), (0, 0), (CH, 0), (0, 0)), constant_values=NEG_INF)
    return jnp.concatenate([first, second], axis=3)


def _bias_unlayout(dbt):
    b = dbt.reshape(DEPTH, 4, NBAND, CH, 2, CH)
    return b.transpose(0, 2, 1, 4, 5, 3).reshape(DEPTH, NBAND * 8, CH * CH)


UNIT = 2 * CH
BAND2 = KB + CH


def _pair_weights(xt):
    x = xt.astype(F32)
    row = lax.broadcasted_iota(jnp.int32, (128, UNIT), 0)
    low = lax.broadcasted_iota(jnp.int32, (128, UNIT), 1) < HD
    swapped = pltpu.roll(x, HD, 1)
    same = (row < HD) == low
    first = jnp.where(same, jnp.where(low, x, swapped), 0.0)
    second = jnp.where(same, jnp.where(low, swapped, x), 0.0)
    return jnp.concatenate([first, second], axis=1).astype(BF16)


def _pair_rows(x):
    low = lax.broadcasted_iota(jnp.int32, (CH, 128), 1) < HD
    zero = jnp.zeros((CH, 128), x.dtype)
    parts = []
    for c in range(2):
        xc = x[c * CH:(c + 1) * CH, :]
        parts += [jnp.where(low, xc, zero), jnp.where(low, zero, xc)]
    return jnp.concatenate(parts, axis=0)


def _unpair(raw):
    b0, b1 = raw[:, 0:128], raw[:, 128:256]
    row = lax.broadcasted_iota(jnp.int32, (128, 128), 0)
    low = lax.broadcasted_iota(jnp.int32, (128, 128), 1) < HD
    top = jnp.where(low, b0, pltpu.roll(b1, HD, 1))
    bottom = jnp.where(low, pltpu.roll(b0, HD, 1), b1)
    return jnp.where(row < HD, top, bottom).T


def _scores_t(kb, qw, bias2, row0, padded):
    s = jnp.dot(kb, qw, preferred_element_type=F32) + bias2
    if padded:
        s = jnp.where(row0 + lax.broadcasted_iota(jnp.int32, (BAND2, 256), 0) >= PADR, s, NEG_INF)
    return s


def _unit_loops(s, unit):
    lax.fori_loop(0, PADR // UNIT, lambda u, c: unit(u, True, c), 0, unroll=4)
    lax.fori_loop(PADR // UNIT, s // UNIT, lambda u, c: unit(u, False, c), 0, unroll=7)


def _attn_fwd(name, kp, qt, vt, bias2, l):
    s = qt.shape[1]
    nu = s // UNIT

    def body(k_ref, qt_ref, vt_ref, b_ref, o_ref, lse_ref):
        def unit(u, padded, carry):
            r0 = pl.multiple_of(u * UNIT, UNIT)
            sc = _scores_t(k_ref[pl.ds(r0, BAND2), :], _pair_weights(qt_ref[:, pl.ds(r0, UNIT)]), b_ref[...],
                           r0, padded)
            top = jnp.max(sc, axis=0, keepdims=True)
            e = jnp.exp(sc - top)
            total = jnp.sum(e, axis=0, keepdims=True)
            raw = jnp.dot(vt_ref[:, pl.ds(r0, BAND2)], e.astype(BF16), preferred_element_type=F32) * (1.0 / total)
            o_ref[pl.ds(r0, UNIT), :] = _unpair(raw).astype(BF16)
            lse_ref[u] = jnp.broadcast_to(top + jnp.log(total), (8, 256))
            return carry

        _unit_loops(s, unit)

    return pl.pallas_call(
        body, name=name, grid=(AW // 128,),
        in_specs=[pl.BlockSpec((s + PADR, 128), lambda h: (0, h)),
                  pl.BlockSpec((128, s), lambda h: (h, 0)),
                  pl.BlockSpec((128, s + PADR), lambda h: (h, 0)),
                  pl.BlockSpec((None, None, BAND2, 256), lambda h: (l, h, 0, 0))],
        out_specs=[pl.BlockSpec((s, 128), lambda h: (0, h)),
                   pl.BlockSpec((None, nu, 8, 256), lambda h: (h, 0, 0, 0))],
        out_shape=[jax.ShapeDtypeStruct((s, AW), BF16), jax.ShapeDtypeStruct((4, nu, 8, 256), F32)],
        compiler_params=_cp("parallel"),
    )(kp, qt, vt, bias2)


def _attn_bwd(name, q, qt, kp, kt, vp, bias2, l, do, dot, lse, dl, db_all):
    s = q.shape[0]
    nu = s // UNIT

    def body(q_ref, qt_ref, k_ref, kt_ref, v_ref, b_ref, do_ref, dot_ref, lse_ref, dl_ref, dbin_ref,
             dq_ref, dk_ref, dvb_ref, db_ref, dv_ref):
        del dbin_ref
        dk_ref[...] = jnp.zeros_like(dk_ref)
        dv_ref[...] = jnp.zeros_like(dv_ref)
        db_ref[...] = jnp.zeros_like(db_ref)

        def unit(u, padded, carry):
            r0 = pl.multiple_of(u * UNIT, UNIT)
            rows, band = pl.ds(r0, UNIT), pl.ds(r0, BAND2)
            sc = _scores_t(k_ref[band, :], _pair_weights(qt_ref[:, rows]), b_ref[...], r0, padded)
            pt = jnp.exp(sc - lse_ref[u][0:1, :])
            dpt = jnp.dot(v_ref[band, :], _pair_weights(dot_ref[:, rows]), preferred_element_type=F32)
            ds = pt * (dpt - dl_ref[u][0:1, :])
            db_ref[...] += ds[0:KB, 0:128] + ds[CH:BAND2, 128:256]
            dsb = ds.astype(BF16)
            dq_ref[rows, :] = _unpair(jnp.dot(kt_ref[:, band], dsb, preferred_element_type=F32))
            dk_ref[band, :] += jnp.dot(dsb, _pair_rows(q_ref[rows, :]), preferred_element_type=F32)
            dv_ref[band, :] += jnp.dot(pt.astype(BF16), _pair_rows(do_ref[rows, :]), preferred_element_type=F32)
            return carry

        _unit_loops(s, unit)
        dvb_ref[...] = dv_ref[...].astype(BF16)

    row_q = pl.BlockSpec((s, 128), lambda h: (0, h))
    col_q = pl.BlockSpec((128, s), lambda h: (h, 0))
    row_k = pl.BlockSpec((s + PADR, 128), lambda h: (0, h))
    col_k = pl.BlockSpec((128, s + PADR), lambda h: (h, 0))
    stat = pl.BlockSpec((None, nu, 8, 256), lambda h: (h, 0, 0, 0))
    return pl.pallas_call(
        body, name=name, grid=(AW // 128,),
        in_specs=[row_q, col_q, row_k, col_k, row_k,
                  pl.BlockSpec((None, None, BAND2, 256), lambda h: (l, h, 0, 0)), row_q, col_q, stat, stat, ANY],
        out_specs=[row_q, row_k, row_k, pl.BlockSpec((None, None, KB, 128), lambda h: (l, h, 0, 0))],
        out_shape=[jax.ShapeDtypeStruct((s, AW), F32),
                   jax.ShapeDtypeStruct((s + PADR, AW), F32),
                   jax.ShapeDtypeStruct((s + PADR, AW), BF16),
                   jax.ShapeDtypeStruct((DEPTH, 4, KB, 128), F32)],
        scratch_shapes=[pltpu.VMEM((s + PADR, 128), F32)],
        input_output_aliases={10: 3},
        compiler_params=_cp("parallel"),
    )(q, qt, kp, kt, vp, bias2, do, dot, lse, dl, db_all)


def _rowsum_layout(dl, nu):
    d = dl[:, :8].reshape(nu, 2, CH, 4, 2)
    d = d.transpose(3, 0, 1, 4, 2).reshape(4, nu, 1, 256)
    return jnp.broadcast_to(d, (4, nu, 8, 256))


def _rows_before(cur, prev, k):
    row = lax.broadcasted_iota(jnp.int32, cur.shape, 0)
    return jnp.where(row >= k, pltpu.roll(cur, k, 0), pltpu.roll(prev, k, 0))


def _rows_after(cur, nxt, k):
    n = cur.shape[0]
    row = lax.broadcasted_iota(jnp.int32, cur.shape, 0)
    return jnp.where(row < n - k, pltpu.roll(cur, n - k, 0), pltpu.roll(nxt, n - k, 0))


def _pool_window_lanes():
    lg = lax.broadcasted_iota(jnp.int32, (1, PWD), 1) // 64
    return lg, jnp.where(lg == 0, 2.0, jnp.where(lg == 1, 4.0, jnp.where(lg == 2, 8.0, 16.0))).astype(F32)


def _pool_mean_minus_token(u, up, row0):
    lg, wv = _pool_window_lanes()
    sums = []
    c, p = u, up
    for k in (1, 2, 4, 8):
        c2 = c + _rows_before(c, p, k)
        p = p + pltpu.roll(p, k, 0)
        c = c2
        sums.append(c)
    win = jnp.where(lg == 0, sums[0], jnp.where(lg == 1, sums[1], jnp.where(lg == 2, sums[2], sums[3])))
    pos1 = (row0 + lax.broadcasted_iota(jnp.int32, u.shape, 0) + 1).astype(F32)
    cnt = jnp.minimum(pos1, wv)
    return win / cnt - u, cnt


def _conv_taps(z, zp, w0, w1, w2):
    z1 = _rows_before(z, zp, 1)
    z2 = _rows_before(z, zp, 2)
    return (w0 * z2 + w1 * z1) + w2 * z, z1, z2


CP_TM = 1024
HALO = 16


def _halo_before(tm, col):
    return pl.BlockSpec((HALO, CW), lambda i: (jnp.maximum(i * (tm // HALO) - 1, 0), col))


def _halo_after(tm, col, rows):
    return pl.BlockSpec((HALO, CW), lambda i: (jnp.minimum((i + 1) * (tm // HALO), rows // HALO - 1), col))


def _as_block_end(halo, tm):
    return jnp.concatenate([jnp.zeros((tm - HALO, halo.shape[1]), halo.dtype), halo], axis=0)


def _as_block_start(halo, tm):
    return jnp.concatenate([halo, jnp.zeros((tm - HALO, halo.shape[1]), halo.dtype)], axis=0)


def _convpool_fwd(name, p, o, cw, pwbd, ps):
    s = p.shape[0]
    tm = CP_TM
    nb = s // tm

    def body(gb_ref, gc_ref, hin_ref, u_ref, gcp_ref, hinp_ref, up_ref, o_ref, cw_ref, pw_ref, ps_ref, mix_ref):
        i = pl.program_id(0)
        has_prev = i > 0
        z = gc_ref[...] * hin_ref[...]
        zp = _as_block_end(jnp.where(has_prev, gcp_ref[...] * hinp_ref[...], 0.0), tm)
        y3, _, _ = _conv_taps(z, zp, cw_ref[0:1, :], cw_ref[1:2, :], cw_ref[2:3, :])
        m, _ = _pool_mean_minus_token(u_ref[...], _as_block_end(jnp.where(has_prev, up_ref[...], 0.0), tm), i * tm)
        yp = jnp.dot(m.astype(BF16), pw_ref[...].astype(BF16), preferred_element_type=F32) * ps_ref[...]
        mix_ref[:, 0:AW] = o_ref[...]
        mix_ref[:, AW:AW + CW] = (gb_ref[...] * y3).astype(BF16)
        mix_ref[:, AW + CW:D] = yp.astype(BF16)

    def cur(col):
        return pl.BlockSpec((tm, CW), lambda i: (i, col))

    def whole(a):
        return pl.BlockSpec(a.shape, lambda i: (0,) * a.ndim)

    return pl.pallas_call(
        body, name=name, grid=(nb,),
        in_specs=[cur(6), cur(7), cur(8), cur(9), _halo_before(tm, 7), _halo_before(tm, 8), _halo_before(tm, 9),
                  pl.BlockSpec((tm, AW), lambda i: (i, 0)), whole(cw), whole(pwbd), whole(ps)],
        out_specs=pl.BlockSpec((tm, D), lambda i: (i, 0)),
        out_shape=jax.ShapeDtypeStruct((s, D), BF16),
        compiler_params=_cp("parallel"),
    )(p, p, p, p, p, p, p, o, cw, pwbd, ps)


def _convpool_bwd(name, p, dmix, cw, pwbd, ps):
    s = p.shape[0]
    tm = CP_TM // 2
    nb = s // tm

    def body(gb_ref, gc_ref, hin_ref, u_ref, gcp_ref, hinp_ref, up_ref, gbn_ref, dyc_ref, dyp_ref, dycn_ref, dypn_ref,
             cw_ref, pw_ref, ps_ref, dcp_ref, dw0_ref, dw1_ref, dw2_ref, dps_ref, dpw_ref):
        i = pl.program_id(0)
        has_prev = i > 0
        has_next = i < nb - 1
        w0, w1, w2 = cw_ref[0:1, :], cw_ref[1:2, :], cw_ref[2:3, :]
        gb, gc, hin = gb_ref[...], gc_ref[...], hin_ref[...]
        dyc = dyc_ref[...]
        z = gc * hin
        zp = _as_block_end(jnp.where(has_prev, gcp_ref[...] * hinp_ref[...], 0.0), tm)
        y3, z1, z2 = _conv_taps(z, zp, w0, w1, w2)
        dy3 = dyc * gb
        dy3n = _as_block_start(jnp.where(has_next, dycn_ref[...] * gbn_ref[...], 0.0), tm)
        dz = w2 * dy3 + w1 * _rows_after(dy3, dy3n, 1) + w0 * _rows_after(dy3, dy3n, 2)
        pw = pw_ref[...].astype(BF16)
        psv = ps_ref[...]
        m, cnt = _pool_mean_minus_token(u_ref[...], _as_block_end(jnp.where(has_prev, up_ref[...], 0.0), tm), i * tm)
        mb = m.astype(BF16)
        dyp = dyp_ref[...]
        dmp = (dyp * psv).astype(BF16)
        dmpn = jnp.where(has_next, dypn_ref[...] * psv, 0.0).astype(BF16)
        nt = (((1,), (1,)), ((), ()))
        dm = lax.dot_general(dmp, pw, nt, preferred_element_type=F32)
        dmn = lax.dot_general(dmpn, pw, nt, preferred_element_type=F32)
        lg, wv = _pool_window_lanes()
        cc, cn = dm / cnt, _as_block_start(dmn / wv, tm)
        sums = []
        for k in (1, 2, 4, 8):
            c2 = cc + _rows_after(cc, cn, k)
            cn = cn + pltpu.roll(cn, tm - k, 0)
            cc = c2
            sums.append(cc)
        du = jnp.where(lg == 0, sums[0], jnp.where(lg == 1, sums[1], jnp.where(lg == 2, sums[2], sums[3]))) - dm
        dcp_ref[:, 0:CW] = (dyc * y3).astype(BF16)
        dcp_ref[:, CW:2 * CW] = (dz * hin).astype(BF16)
        dcp_ref[:, 2 * CW:3 * CW] = (dz * gc).astype(BF16)
        dcp_ref[:, 3 * CW:4 * CW] = du.astype(BF16)
        parts = (jnp.sum(dy3 * z2, axis=0, keepdims=True),
                 jnp.sum(dy3 * z1, axis=0, keepdims=True),
                 jnp.sum(dy3 * z, axis=0, keepdims=True),
                 jnp.sum(dyp * jnp.dot(mb, pw, preferred_element_type=F32), axis=0, keepdims=True),
                 lax.dot_general(mb, dmp, (((0,), (0,)), ((), ())), preferred_element_type=F32))
        accs = (dw0_ref, dw1_ref, dw2_ref, dps_ref, dpw_ref)

        @pl.when(i == 0)
        def _():
            for a, v in zip(accs, parts):
                a[...] = v

        @pl.when(i > 0)
        def _():
            for a, v in zip(accs, parts):
                a[...] += v

    def cur(col):
        return pl.BlockSpec((tm, CW), lambda i: (i, col))

    def prev(col):
        return _halo_before(tm, col)

    def nxt(col):
        return _halo_after(tm, col, s)

    def whole(shape):
        return pl.BlockSpec(shape, lambda i: (0,) * len(shape))

    row = jax.ShapeDtypeStruct((1, CW), F32)
    return pl.pallas_call(
        body, name=name, grid=(nb,),
        in_specs=[cur(6), cur(7), cur(8), cur(9), prev(7), prev(8), prev(9), nxt(6),
                  cur(0), cur(1), nxt(0), nxt(1), whole(cw.shape), whole(pwbd.shape), whole(ps.shape)],
        out_specs=[pl.BlockSpec((tm, D), lambda i: (i, 0)), whole((1, CW)), whole((1, CW)), whole((1, CW)),
                   whole((1, PWD)), whole((PWD, PWD))],
        out_shape=[jax.ShapeDtypeStruct((s, D), BF16), row, row, row, row,
                   jax.ShapeDtypeStruct((PWD, PWD), F32)],
        compiler_params=_cp("arbitrary"),
    )(p, p, p, p, p, p, p, p, dmix, dmix, dmix, dmix, cw, pwbd, ps)


def _qkv_bwd(name, p, dq, dkp, dvp, dcp, qg, kg):
    s = p.shape[0]
    tm = 512
    off = PADR // tm

    def body(pq_ref, pk_ref, dq_ref, dk_ref, dv_ref, dcp_ref, qg_ref, kg_ref, dp_ref, dqg_ref, dkg_ref):
        i = pl.program_id(0)
        hm = _head_mean_matrix()

        def nrm_bwd(x, g, dy):
            r = lax.rsqrt(_head_mean(x * x, hm) + EPS)
            xn = x * r
            dxn = dy * g
            dx = r * (dxn - xn * _head_mean(dxn * xn, hm))
            dg = jnp.sum(dy * xn, axis=0, keepdims=True)
            dg = (dg[:, 0:128] + dg[:, 128:256]) + (dg[:, 256:384] + dg[:, 384:512])
            return dx, dg + pltpu.roll(dg, HD, 1)

        dxq, dgq = nrm_bwd(pq_ref[...], qg_ref[...], dq_ref[...] * 0.125)
        dxk, dgk = nrm_bwd(pk_ref[...], kg_ref[...], dk_ref[...])
        dp_ref[:, 0:AW] = dxq.astype(BF16)
        dp_ref[:, AW:2 * AW] = dxk.astype(BF16)
        dp_ref[:, 2 * AW:3 * AW] = dv_ref[...].astype(BF16)
        dp_ref[:, 3 * AW:DIN] = dcp_ref[...]

        @pl.when(i == 0)
        def _():
            dqg_ref[...] = dgq
            dkg_ref[...] = dgk

        @pl.when(i > 0)
        def _():
            dqg_ref[...] += dgq
            dkg_ref[...] += dgk

    gspec = pl.BlockSpec((1, AW), lambda i: (0, 0))
    gout = pl.BlockSpec((1, 128), lambda i: (0, 0))
    return pl.pallas_call(
        body, name=name, grid=(s // tm,),
        in_specs=[pl.BlockSpec((tm, AW), lambda i: (i, 0)), pl.BlockSpec((tm, AW), lambda i: (i, 1)),
                  pl.BlockSpec((tm, AW), lambda i: (i, 0)),
                  pl.BlockSpec((tm, AW), lambda i: (i + off, 0)),
                  pl.BlockSpec((tm, AW), lambda i: (i + off, 0)),
                  pl.BlockSpec((tm, D), lambda i: (i, 0)), gspec, gspec],
        out_specs=[pl.BlockSpec((tm, DIN), lambda i: (i, 0)), gout, gout],
        out_shape=[jax.ShapeDtypeStruct((s, DIN), BF16), jax.ShapeDtypeStruct((1, 128), F32),
                   jax.ShapeDtypeStruct((1, 128), F32)],
        compiler_params=_cp("arbitrary"),
    )(p, p, dq, dkp, dvp, dcp, qg, kg)


def _mm_nt_relu(name, dxb, w, l, a):
    s = dxb.shape[0]
    tm = MM_ROWS

    def body(d_ref, w_ref, a_ref, o_ref):
        df = lax.dot_general(d_ref[...], w_ref[...], NT_DIMS, preferred_element_type=F32)
        o_ref[...] = (df * (2.0 * jnp.maximum(a_ref[...].astype(F32), 0.0))).astype(BF16)

    return pl.pallas_call(
        body, name=name, grid=(s // tm,),
        in_specs=[pl.BlockSpec((tm, D), lambda i: (i, 0)),
                  pl.BlockSpec((None, DFF, D), lambda i: (l, 0, 0)),
                  pl.BlockSpec((tm, DFF), lambda i: (i, 0))],
        out_specs=pl.BlockSpec((tm, DFF), lambda i: (i, 0)),
        out_shape=jax.ShapeDtypeStruct((s, DFF), BF16),
        compiler_params=_cp("parallel"),
    )(dxb, w, a)


def _proj_out_bwd(name, dxb, w, l, mix):
    s = dxb.shape[0]
    tm = _mm_rows(D, D)

    def body(d_ref, w_ref, o_ref, do_ref, dot_ref, dcp_ref, dl_ref):
        d = d_ref[...]
        wa, wc = w_ref[0:AW, :], w_ref[AW:D, :]
        do = lax.dot_general(d, wa, NT_DIMS, preferred_element_type=F32)
        do_ref[...] = do.astype(BF16)
        dot_ref[...] = lax.dot_general(wa, d, NT_DIMS, preferred_element_type=F32).astype(BF16)
        dcp_ref[...] = lax.dot_general(d, wc, NT_DIMS, preferred_element_type=F32)
        head = lax.broadcasted_iota(jnp.int32, (AW, 128), 0) // HD
        pick = jnp.where(head == lax.broadcasted_iota(jnp.int32, (AW, 128), 1), 1.0, 0.0).astype(BF16)
        dl_ref[...] = _two_pass_dot(do * o_ref[...].astype(F32), pick)

    return pl.pallas_call(
        body, name=name, grid=(s // tm,),
        in_specs=[pl.BlockSpec((tm, D), lambda i: (i, 0)),
                  pl.BlockSpec((None, D, D), lambda i: (l, 0, 0)),
                  pl.BlockSpec((tm, AW), lambda i: (i, 0))],
        out_specs=[pl.BlockSpec((tm, AW), lambda i: (i, 0)), pl.BlockSpec((AW, tm), lambda i: (0, i)),
                   pl.BlockSpec((tm, D - AW), lambda i: (i, 0)), pl.BlockSpec((tm, 128), lambda i: (i, 0))],
        out_shape=[jax.ShapeDtypeStruct((s, AW), BF16), jax.ShapeDtypeStruct((AW, s), BF16),
                   jax.ShapeDtypeStruct((s, D - AW), F32), jax.ShapeDtypeStruct((s, 128), F32)],
        compiler_params=_cp("parallel"),
    )(dxb, w, mix)


def _mm_nt_normbwd(name, gy, w, l, x, g, dres, dep):
    s, k = gy.shape
    tm = MM_ROWS

    def body(gy_ref, w_ref, x_ref, g_ref, dr_ref, dep_ref, dx_ref, dxb_ref, dg_ref):
        del dep_ref
        i = pl.program_id(0)
        dh = lax.dot_general(gy_ref[...], w_ref[...], NT_DIMS, preferred_element_type=F32)
        xv = x_ref[...]
        r = _inv_rms(xv)
        xn = xv * r
        dxn = dh * g_ref[...]
        dx = r * (dxn - xn * jnp.mean(dxn * xn, axis=-1, keepdims=True)) + dr_ref[...]
        dx_ref[...] = dx
        dxb_ref[...] = dx.astype(BF16)
        part = jnp.sum(dh * xn, axis=0, keepdims=True)

        @pl.when(i == 0)
        def _():
            dg_ref[...] = part

        @pl.when(i > 0)
        def _():
            dg_ref[...] += part

    blk = pl.BlockSpec((tm, D), lambda i: (i, 0))
    vec = pl.BlockSpec((1, D), lambda i: (0, 0))
    return pl.pallas_call(
        body, name=name, grid=(s // tm,),
        in_specs=[pl.BlockSpec((tm, k), lambda i: (i, 0)),
                  pl.BlockSpec((None, D, k), lambda i: (l, 0, 0)), blk, vec, blk, ANY],
        out_specs=[blk, blk, vec],
        out_shape=[jax.ShapeDtypeStruct((s, D), F32), jax.ShapeDtypeStruct((s, D), BF16),
                   jax.ShapeDtypeStruct((1, D), F32)],
        compiler_params=_cp("arbitrary"),
    )(gy, w, x, g, dres, dep)


def _mm_tn(name, a, b, tma, tnb, relu2=False):
    s, m = a.shape
    n = b.shape[1]

    def body(a_ref, b_ref, o_ref):
        av = _relu2(a_ref[...]) if relu2 else a_ref[...]
        o_ref[...] = lax.dot_general(av, b_ref[...], (((0,), (0,)), ((), ())),
                                     preferred_element_type=F32).astype(BF16)

    return pl.pallas_call(
        body, name=name, grid=(m // tma, n // tnb),
        in_specs=[pl.BlockSpec((s, tma), lambda i, j: (0, i), pipeline_mode=pl.Buffered(1) if m == tma else None),
                  pl.BlockSpec((s, tnb), lambda i, j: (0, j))],
        out_specs=pl.BlockSpec((tma, tnb), lambda i, j: (i, j)),
        out_shape=jax.ShapeDtypeStruct((m, n), BF16),
        compiler_params=_cp("parallel", "parallel"),
    )(a, b)


def _adamw_math(gv, wv, mv, vv):
    mn = ADAM_B1 * mv + (1.0 - ADAM_B1) * gv
    vn = ADAM_B2 * vv + (1.0 - ADAM_B2) * jnp.square(gv)
    m_hat = mn / (1.0 - ADAM_B1 ** ADAM_STEP)
    v_hat = vn / (1.0 - ADAM_B2 ** ADAM_STEP)
    return gv, -ADAM_LR * (m_hat / (jnp.sqrt(v_hat) + ADAM_EPS) + ADAM_WD * wv), mn, vn


def _adamw(name, g, w, m, v):
    r, c = g.shape
    tm = 256 if r % 256 == 0 else r

    def body(g_ref, w_ref, m_ref, v_ref, go_ref, d_ref, mo_ref, vo_ref):
        go_ref[...], d_ref[...], mo_ref[...], vo_ref[...] = _adamw_math(g_ref[...], w_ref[...], m_ref[...], v_ref[...])

    blk = pl.BlockSpec((tm, c), lambda i: (i, 0))
    return pl.pallas_call(
        body, name=name, grid=(r // tm,),
        in_specs=[blk] * 4, out_specs=[blk] * 4,
        out_shape=[jax.ShapeDtypeStruct((r, c), F32)] * 4,
        compiler_params=_cp("parallel"),
    )(g, w, m, v)


def _place():
    x, y, c = lax.axis_index("x"), lax.axis_index("y"), lax.axis_index("c")
    chips = [(1 - x, y), (x, 1 - y), (1 - x, 1 - y)]
    return x, y, c, chips


BLOCK_AXIS = (2, 1, 2, 1)
LARGE_DIMS = ((D, DIN), (D, D), (D, DFF), (DFF, D))


def _full_shape(t, layers, dtype):
    r, c = LARGE_DIMS[t]
    return jax.ShapeDtypeStruct((layers, r, c), dtype)


def _cast_into_full(name, t, shard, b1, dep):
    _, r, c = shard.shape
    tm = min(512, r)
    if BLOCK_AXIS[t] == 1:
        out_spec = pl.BlockSpec((None, tm, c), lambda l, i, br: (l, br[0] * (r // tm) + i, 0))
    else:
        out_spec = pl.BlockSpec((None, tm, c), lambda l, i, br: (l, i, br[0]))

    def body(b_ref, x_ref, dep_ref, o_ref):
        del b_ref, dep_ref
        o_ref[...] = x_ref[...].astype(BF16)

    return pl.pallas_call(
        body, name=name,
        grid_spec=pltpu.PrefetchScalarGridSpec(
            num_scalar_prefetch=1, grid=(DEPTH, r // tm),
            in_specs=[pl.BlockSpec((None, tm, c), lambda l, i, br: (l, i, 0)), ANY],
            out_specs=out_spec),
        out_shape=_full_shape(t, DEPTH, BF16),
        compiler_params=_cp("parallel", "parallel"),
    )(b1, shard, dep)


HBM = pl.BlockSpec(memory_space=pltpu.HBM)
SEM = pl.BlockSpec(memory_space=pltpu.SEMAPHORE)
DATAFLOW = pltpu.SideEffectType.DATAFLOW_SIDE_EFFECTING


def _half(ref, l, t, b, c):
    r, cols = LARGE_DIMS[t]
    if BLOCK_AXIS[t] == 1:
        n = r // 8
        return ref.at[l, pl.ds(pl.multiple_of(b * (2 * n) + c * n, 16), n), :]
    n, w = r // 2, cols // 4
    return ref.at[l, pl.ds(pl.multiple_of(c * n, 16), n), pl.ds(pl.multiple_of(b * w, 128), w)]


def _gather_start(name, layers, ts, fulls, both=False):
    n = len(ts)

    def body(*refs):
        f_refs, sems = refs[n:2 * n], refs[2 * n:2 * n + 2 * len(layers)]
        x, y, c, chips = _place()
        for i, l in enumerate(layers):
            for k, t in enumerate(ts):
                own = _half(f_refs[k], l, t, 2 * x + y, c)
                for j, (cx, cy) in enumerate(chips):
                    pltpu.make_async_remote_copy(src_ref=own, dst_ref=own, send_sem=sems[2 * i].at[3 * t + j],
                                                 recv_sem=sems[2 * i + 1].at[3 * t + j], device_id=(cx, cy, c),
                                                 device_id_type=MESH).start()
                    if both:
                        pltpu.make_async_remote_copy(src_ref=own, dst_ref=own,
                                                     send_sem=sems[2 * i].at[12 + 3 * t + j],
                                                     recv_sem=sems[2 * i + 1].at[12 + 3 * t + j],
                                                     device_id=(cx, cy, 1 - c), device_id_type=MESH).start()
        refs[-1][...] = jnp.zeros((8, 128), F32)

    outs = pl.pallas_call(
        body, name=name,
        in_specs=[HBM] * n,
        out_specs=[HBM] * n + [SEM] * (2 * len(layers)) + [pl.BlockSpec(memory_space=pltpu.VMEM)],
        out_shape=[pltpu.HBM(f.shape, f.dtype) for f in fulls]
        + [pltpu.SemaphoreType.DMA((24,))] * (2 * len(layers)) + [jax.ShapeDtypeStruct((8, 128), F32)],
        input_output_aliases={k: k for k in range(n)},
        compiler_params=pltpu.CompilerParams(has_side_effects=DATAFLOW),
    )(*[pltpu.with_memory_space_constraint(f, pltpu.HBM) for f in fulls])
    return outs[0:n], {l: (outs[n + 2 * i], outs[n + 1 + 2 * i]) for i, l in enumerate(layers)}, outs[-1]


def _gather_wait(name, l, ts, fulls, sems, after, both=False):
    def body(*refs):
        send_sems, recv_sems, f_refs = refs[4], refs[5], refs[7:11]
        x, y, c, chips = _place()
        for t in ts:
            own = _half(f_refs[t], l, t, 2 * x + y, c)
            for j, (cx, cy) in enumerate(chips):
                landed = _half(f_refs[t], l, t, 2 * cx + cy, c)
                pltpu.make_async_remote_copy(src_ref=own, dst_ref=landed, send_sem=send_sems.at[3 * t + j],
                                             recv_sem=recv_sems.at[3 * t + j], device_id=(cx, cy, c),
                                             device_id_type=MESH).wait()
                if both:
                    crossed = _half(f_refs[t], l, t, 2 * cx + cy, 1 - c)
                    pltpu.make_async_remote_copy(src_ref=own, dst_ref=crossed, send_sem=send_sems.at[12 + 3 * t + j],
                                                 recv_sem=recv_sems.at[12 + 3 * t + j], device_id=(cx, cy, 1 - c),
                                                 device_id_type=MESH).wait()

    return pl.pallas_call(
        body, name=name,
        in_specs=[HBM] * 4 + [SEM, SEM, ANY], out_specs=[HBM] * 4,
        out_shape=[pltpu.HBM(s.shape, s.dtype) for s in (_full_shape(t, DEPTH, BF16) for t in range(4))],
        input_output_aliases={t: t for t in range(4)},
        compiler_params=pltpu.CompilerParams(has_side_effects=DATAFLOW),
    )(*fulls, sems[0], sems[1], after)


def _pass_on(name, l, ts, fulls):
    def body(*refs):
        f_refs, send_sems, recv_sems = refs[4:8], refs[8], refs[9]
        x, y, c, chips = _place()

        def copy(t, j, half):
            cx, cy = chips[j]
            part = _half(f_refs[t], l, t, 2 * cx + cy, half)
            return pltpu.make_async_remote_copy(src_ref=part, dst_ref=part, send_sem=send_sems.at[3 * t + j],
                                                recv_sem=recv_sems.at[3 * t + j], device_id=(x, y, 1 - c),
                                                device_id_type=MESH)

        for t in ts:
            for j in range(3):
                copy(t, j, c).start()
        for t in ts:
            for j in range(3):
                copy(t, j, 1 - c).wait_recv()
                copy(t, j, c).wait_send()

    return pl.pallas_call(
        body, name=name,
        in_specs=[ANY] * 4, out_specs=[ANY] * 4,
        out_shape=[_full_shape(t, DEPTH, BF16) for t in range(4)],
        input_output_aliases={t: t for t in range(4)},
        scratch_shapes=[pltpu.SemaphoreType.DMA((12,)), pltpu.SemaphoreType.DMA((12,))],
    )(*fulls)


def _block2d(ref, t, b):
    r, cols = LARGE_DIMS[t]
    if BLOCK_AXIS[t] == 1:
        return ref.at[pl.ds(pl.multiple_of(b * (r // 4), 16), r // 4), :]
    return ref.at[:, pl.ds(pl.multiple_of(b * (cols // 4), 128), cols // 4)]


def _block_dims(t):
    r, cols = LARGE_DIMS[t]
    return (r // 4, cols) if BLOCK_AXIS[t] == 1 else (r, cols // 4)


def _reduce_copies(ts, g_refs, r_refs, send_sems, recv_sems):
    _, _, c, chips = _place()
    return [pltpu.make_async_remote_copy(src_ref=_block2d(g_refs[i], t, 2 * cx + cy), dst_ref=r_refs[i].at[j],
                                         send_sem=send_sems.at[3 * i + j], recv_sem=recv_sems.at[3 * i + j],
                                         device_id=(cx, cy, c), device_id_type=MESH)
            for i, t in enumerate(ts) for j, (cx, cy) in enumerate(chips)]


def _reduce_start(name, ts, grads):
    n = len(ts)

    def body(*refs):
        for cp in _reduce_copies(ts, refs[n:2 * n], refs[2 * n:3 * n], refs[3 * n], refs[3 * n + 1]):
            cp.start()
        refs[3 * n + 2][...] = jnp.zeros((8, 128), F32)

    outs = pl.pallas_call(
        body, name=name,
        in_specs=[HBM] * n,
        out_specs=[HBM] * (2 * n) + [SEM, SEM, pl.BlockSpec(memory_space=pltpu.VMEM)],
        out_shape=[pltpu.HBM(g.shape, BF16) for g in grads]
        + [pltpu.HBM((3,) + _block_dims(t), BF16) for t in ts]
        + [pltpu.SemaphoreType.DMA((3 * n,)), pltpu.SemaphoreType.DMA((3 * n,)), jax.ShapeDtypeStruct((8, 128), F32)],
        input_output_aliases={i: i for i in range(n)},
        compiler_params=pltpu.CompilerParams(has_side_effects=DATAFLOW),
    )(*[pltpu.with_memory_space_constraint(g, pltpu.HBM) for g in grads])
    return outs[0:n], outs[n:2 * n], (outs[2 * n], outs[2 * n + 1]), outs[2 * n + 2]


def _reduce_wait(name, ts, grads, landing, sems, afters):
    n = len(ts)
    first_out = 2 * n + 2 + len(afters)

    def body(*refs):
        for cp in _reduce_copies(ts, refs[first_out:first_out + n], refs[first_out + n:first_out + 2 * n],
                                 refs[2 * n], refs[2 * n + 1]):
            cp.wait()

    outs = pl.pallas_call(
        body, name=name,
        in_specs=[HBM] * (2 * n) + [SEM, SEM] + [ANY] * len(afters), out_specs=[HBM] * (2 * n),
        out_shape=[pltpu.HBM(g.shape, BF16) for g in grads] + [pltpu.HBM(r.shape, BF16) for r in landing],
        input_output_aliases={i: i for i in range(2 * n)},
        compiler_params=pltpu.CompilerParams(has_side_effects=DATAFLOW),
    )(*grads, *landing, sems[0], sems[1], *afters)
    return outs[0:n], outs[n:2 * n]


def _add4(name, t, own, landed, b1):
    rb, cb = _block_dims(t)
    tm = min(512, rb)
    if BLOCK_AXIS[t] == 1:
        own_spec = pl.BlockSpec((tm, cb), lambda i, br: (br[0] * (rb // tm) + i, 0))
    else:
        own_spec = pl.BlockSpec((tm, cb), lambda i, br: (i, br[0]))

    def body(b_ref, o_ref, r0_ref, r1_ref, r2_ref, s_ref):
        del b_ref
        s_ref[...] = ((o_ref[...].astype(F32) + r0_ref[...].astype(F32))
                      + (r1_ref[...].astype(F32) + r2_ref[...].astype(F32))).astype(BF16)

    def got(j):
        return pl.BlockSpec((None, tm, cb), lambda i, br: (j, i, 0))

    return pl.pallas_call(
        body, name=name,
        grid_spec=pltpu.PrefetchScalarGridSpec(
            num_scalar_prefetch=1, grid=(rb // tm,),
            in_specs=[own_spec, got(0), got(1), got(2)],
            out_specs=pl.BlockSpec((tm, cb), lambda i, br: (i, 0))),
        out_shape=jax.ShapeDtypeStruct((rb, cb), BF16),
        compiler_params=_cp("parallel"),
    )(b1, own, landed, landed, landed)


def _swap_sib(name, sums):
    def body(*refs):
        s_refs, t_refs, send_sems, recv_sems = refs[0:4], refs[4:8], refs[8], refs[9]
        x, y, c, _ = _place()
        cps = [pltpu.make_async_remote_copy(src_ref=s_refs[t], dst_ref=t_refs[t], send_sem=send_sems.at[t],
                                            recv_sem=recv_sems.at[t], device_id=(x, y, 1 - c), device_id_type=MESH)
               for t in range(4)]
        for cp in cps:
            cp.start()
        for cp in cps:
            cp.wait()

    return pl.pallas_call(
        body, name=name,
        in_specs=[ANY] * 4, out_specs=[ANY] * 4,
        out_shape=[jax.ShapeDtypeStruct(s.shape, BF16) for s in sums],
        scratch_shapes=[pltpu.SemaphoreType.DMA((4,)), pltpu.SemaphoreType.DMA((4,))],
    )(*sums)


def _adamw_pair(name, l, s_own, s_sib, w, m, v, outs):
    rb, cb = s_own.shape
    tm = min(512, rb)

    def body(a_ref, b_ref, w_ref, m_ref, v_ref, g0, d0, m0, v0, go_ref, d_ref, mo_ref, vo_ref):
        del g0, d0, m0, v0
        gv = a_ref[...].astype(F32) + b_ref[...].astype(F32)
        go_ref[...], d_ref[...], mo_ref[...], vo_ref[...] = _adamw_math(gv, w_ref[...], m_ref[...], v_ref[...])

    part = pl.BlockSpec((tm, cb), lambda i: (i, 0))
    layer = pl.BlockSpec((None, tm, cb), lambda i: (l, i, 0))
    return pl.pallas_call(
        body, name=name, grid=(rb // tm,),
        in_specs=[part, part, layer, layer, layer] + [ANY] * 4,
        out_specs=[layer] * 4,
        out_shape=[jax.ShapeDtypeStruct((DEPTH, rb, cb), F32)] * 4,
        input_output_aliases={5 + i: i for i in range(4)},
        compiler_params=_cp("parallel"),
    )(s_own, s_sib, w, m, v, *outs)


def _all_gather8(name, v, dep):
    m_per, n = v.shape

    def body(v_ref, dep_ref, out_ref, send_sems, recv_sems, local_sem):
        del dep_ref
        x, y, c, chips = _place()
        me, sib = (x, y, c), (x, y, 1 - c)

        def rows(px, py, pc):
            return out_ref.at[pl.ds((4 * px + 2 * py + pc) * m_per, m_per), :]

        def copy(k, block, to, src=None):
            return pltpu.make_async_remote_copy(
                src_ref=rows(*block) if src is None else src, dst_ref=rows(*block),
                send_sem=send_sems.at[k], recv_sem=recv_sems.at[k], device_id=to, device_id_type=MESH)

        mine = pltpu.make_async_copy(v_ref, rows(*me), local_sem)
        mine.start()
        first = [copy(0, me, sib, src=v_ref)]
        first += [copy(1 + j, me, (*chip, c), src=v_ref) for j, chip in enumerate(chips)]
        for cp in first:
            cp.start()
        passed = [copy(4 + j, (*chip, c), sib) for j, chip in enumerate(chips)]
        for j, chip in enumerate(chips):
            copy(1 + j, (*chip, c), me).wait_recv()
            passed[j].start()
        copy(0, sib, me).wait_recv()
        for j, chip in enumerate(chips):
            copy(4 + j, (*chip, 1 - c), me).wait_recv()
        for cp in first + passed:
            cp.wait_send()
        mine.wait()

    return pl.pallas_call(
        body, name=name,
        out_shape=jax.ShapeDtypeStruct((8 * m_per, n), v.dtype),
        in_specs=[pl.BlockSpec(memory_space=pltpu.VMEM), ANY],
        out_specs=pl.BlockSpec(memory_space=pltpu.VMEM),
        scratch_shapes=[pltpu.SemaphoreType.DMA((7,)), pltpu.SemaphoreType.DMA((7,)), pltpu.SemaphoreType.DMA],
    )(v, dep)


def _sum8(name, g):
    def body(g_ref, o_ref):
        acc = g_ref[0]
        for d in range(1, 8):
            acc = acc + g_ref[d]
        o_ref[...] = acc

    return pl.pallas_call(body, name=name, out_shape=jax.ShapeDtypeStruct(g.shape[1:], F32))(g)


def _pack(parts):
    flat = []
    for a in parts:
        a = a.reshape(-1)
        flat.append(jnp.pad(a, (0, (-a.shape[0]) % 128)))
    cat = jnp.concatenate(flat)
    cat = jnp.pad(cat, (0, (-cat.shape[0]) % 1024))
    return cat.reshape(-1, 128)


def _unpack(packed, shapes):
    flat = packed.reshape(-1)
    out, at = [], 0
    for shp in shapes:
        n = 1
        for d in shp:
            n *= d
        out.append(flat[at:at + n].reshape(shp))
        at += n + (-n) % 128
    return out


def _local_step(x, target, layer_weights, on_grads, small):
    qg_all = jnp.tile(small["q_norm_g"], (1, 8))
    kg_all = jnp.tile(small["k_norm_g"], (1, 8))
    bias_all = _bias_layout(_bias_expand("bias_expand", jnp.pad(small["rel_bias"], ((0, 0), (0, 0), (0, NIDX - 257)))))
    same_group = jnp.eye(4, dtype=F32)[None, :, None, :, None]
    pwbd_all = (small["pool_w"][:, :, :, None, :] * same_group).reshape(DEPTH, PWD, PWD)
    saved = []
    xin = x
    h = _rmsnorm("norm_first", x, small["norm1_g"][0:1])
    for l in range(DEPTH):
        w_in = layer_weights(l, (0,), xin)[0]
        qg, kg = qg_all[l:l + 1], kg_all[l:l + 1]
        cw, pwbd, ps = small["conv_w"][l], pwbd_all[l], small["pool_scale"][l:l + 1]
        p = _mm_nn(f"proj_in_{l}", h, w_in, l, F32)
        q, qt, kp, kt, vp, vt = _qkv(f"qkv_{l}", p, qg, kg)
        o, lse = _attn_fwd(f"attn_fwd_{l}", kp, qt, vt, bias_all, l)
        w_in, w_out, w_1, w_2 = layer_weights(l, (1, 2, 3), o)
        mix = _convpool_fwd(f"convpool_fwd_{l}", p, o, cw, pwbd, ps)
        x1, h2 = _mm_res_norm(f"proj_out_{l}", mix, w_out, l, xin, small["norm2_g"][l:l + 1])
        saved.append(dict(xin=xin, h=h, p=p, q=q, qt=qt, kp=kp, kt=kt, vp=vp, mix=mix, x1=x1, h2=h2, lse=lse,
                          qg=qg, kg=kg, cw=cw, pwbd=pwbd, ps=ps))
        if l + 1 < DEPTH:
            saved[l]["a"], xin, h = _mlp_fwd(f"mlp_{l}", h2, w_1, w_2, l, x1, small["norm1_g"][l + 1:l + 2])
        else:
            saved[l]["a"], dx, dxb, loss = _mlp_fwd(f"mlp_{l}", h2, w_1, w_2, l, x1, target)

    raw = {k: [None] * DEPTH for k in ("dg1", "dqg", "dkg", "dw0", "dw1", "dw2", "dpw", "dps", "dg2")}
    db_all = lax.empty((DEPTH, 4, KB, 128), F32)
    for l in reversed(range(DEPTH)):
        sv = saved[l]
        da = _mm_nt_relu(f"mlp2_bwd_{l}", dxb, w_2, l, sv["a"])
        g_2 = _mm_tn(f"mlp2_wgrad_{l}", sv["a"], dxb, 512, 1024, relu2=True)
        g_1 = _mm_tn(f"mlp1_wgrad_{l}", sv["h2"], da, 1024, 512)
        dep = on_grads(l, (2, 3), (g_1, g_2))
        dx1, dx1b, dg2 = _mm_nt_normbwd(f"mlp1_bwd_{l}", da, w_1, l, sv["x1"], small["norm2_g"][l:l + 1], dx, dep)
        do, dot, dmix, dl = _proj_out_bwd(f"proj_out_bwd_{l}", dx1b, w_out, l, sv["mix"])
        g_out = _mm_tn(f"proj_out_wgrad_{l}", sv["mix"], dx1b, 512, 1024)
        dcp, dw0, dw1, dw2, dps, dpw = _convpool_bwd(f"convpool_bwd_{l}", sv["p"], dmix, sv["cw"], sv["pwbd"], sv["ps"])
        dq, dkp, dvp, db_all = _attn_bwd(f"attn_bwd_{l}", sv["q"], sv["qt"], sv["kp"], sv["kt"], sv["vp"], bias_all, l,
                                     do, dot, sv["lse"], _rowsum_layout(dl, x.shape[0] // UNIT), db_all)
        dp, dqg, dkg = _qkv_bwd(f"qkv_bwd_{l}", sv["p"], dq, dkp, dvp, dcp, sv["qg"], sv["kg"])
        g_in = _mm_tn(f"proj_in_wgrad_{l}", sv["h"], dp, 1024, 1280)
        dep = on_grads(l, (0, 1), (g_in, g_out))
        dx, dxb, dg1 = _mm_nt_normbwd(f"proj_in_bwd_{l}", dp, w_in, l, sv["xin"], small["norm1_g"][l:l + 1], dx1, dep)
        for k, val in dict(dg1=dg1, dqg=dqg, dkg=dkg, dw0=dw0, dw1=dw1, dw2=dw2, dpw=dpw, dps=dps, dg2=dg2).items():
            raw[k][l] = val
    cat = {k: jnp.concatenate(v, axis=0) for k, v in raw.items() if k != "dpw"}
    drb = _bias_reduce("bias_reduce", _bias_unlayout(db_all))
    dpw = jnp.stack(raw["dpw"])
    gsmall = {
        "norm1_g": cat["dg1"], "q_norm_g": cat["dqg"][:, :HD], "k_norm_g": cat["dkg"][:, :HD],
        "rel_bias": drb[:, :, :257],
        "conv_w": jnp.stack([cat["dw0"], cat["dw1"], cat["dw2"]], axis=1),
        "pool_w": jnp.stack([dpw[:, g * 64:(g + 1) * 64, g * 64:(g + 1) * 64] for g in range(4)], axis=1),
        "pool_scale": cat["dps"], "norm2_g": cat["dg2"],
    }
    return loss, dx, gsmall


SMALL = ("norm1_g", "q_norm_g", "k_norm_g", "rel_bias", "conv_w", "pool_w", "pool_scale", "norm2_g")
LARGE = ("w_in", "w_out", "w_mlp1", "w_mlp2")


def kernel(x, norm1_g, w_in, q_norm_g, k_norm_g, rel_bias, conv_w, pool_w, pool_scale, w_out, norm2_g, w_mlp1, w_mlp2, loss_target, m_norm1_g, m_w_in, m_q_norm_g, m_k_norm_g, m_rel_bias, m_conv_w, m_pool_w, m_pool_scale, m_w_out, m_norm2_g, m_w_mlp1, m_w_mlp2, v_norm1_g, v_w_in, v_q_norm_g, v_k_norm_g, v_rel_bias, v_conv_w, v_pool_w, v_pool_scale, v_w_out, v_norm2_g, v_w_mlp1, v_w_mlp2):
    w = dict(norm1_g=norm1_g, w_in=w_in, q_norm_g=q_norm_g, k_norm_g=k_norm_g, rel_bias=rel_bias, conv_w=conv_w,
             pool_w=pool_w, pool_scale=pool_scale, w_out=w_out, norm2_g=norm2_g, w_mlp1=w_mlp1, w_mlp2=w_mlp2)
    m = dict(norm1_g=m_norm1_g, w_in=m_w_in, q_norm_g=m_q_norm_g, k_norm_g=m_k_norm_g, rel_bias=m_rel_bias,
             conv_w=m_conv_w, pool_w=m_pool_w, pool_scale=m_pool_scale, w_out=m_w_out, norm2_g=m_norm2_g,
             w_mlp1=m_w_mlp1, w_mlp2=m_w_mlp2)
    v = dict(norm1_g=v_norm1_g, w_in=v_w_in, q_norm_g=v_q_norm_g, k_norm_g=v_k_norm_g, rel_bias=v_rel_bias,
             conv_w=v_conv_w, pool_w=v_pool_w, pool_scale=v_pool_scale, w_out=v_w_out, norm2_g=v_norm2_g,
             w_mlp1=v_w_mlp1, w_mlp2=v_w_mlp2)
    ax, ay, ac = lax.axis_index("x"), lax.axis_index("y"), lax.axis_index("c")
    b1 = jnp.reshape(2 * ax + ay, (1,)).astype(jnp.int32)

    cw_rows = _all_gather8("gather_conv_w", jnp.pad(conv_w.reshape(DEPTH * 3, 64), ((0, 4), (0, 64))), b1)
    cw_chips = [cw_rows[(4 * cx + 2 * cy) * 16:(4 * cx + 2 * cy) * 16 + 12, :64] for cx in range(2) for cy in range(2)]
    small = {n: w[n] for n in SMALL}
    small["conv_w"] = jnp.concatenate(cw_chips, axis=1).reshape(DEPTH, 3, CW)

    (w_in_full,), in_sems, in_token = _gather_start(
        "gather_start_in", (0,), (0,), [_cast_into_full("cast_w_in", 0, w["w_in"], b1, cw_rows)])
    others, first_sems, first_token = _gather_start(
        "gather_start_first", (0,), (1, 2, 3),
        [_cast_into_full(f"cast_{LARGE[t]}", t, w[LARGE[t]], b1, in_token) for t in (1, 2, 3)])
    held = [[w_in_full] + list(others)]
    sems = {(0, 0): in_sems[0], (0, 1): first_sems[0]}

    def layer_weights(l, ts, after):
        tag = f"{l}_{ts[0]}"
        first_in = l == 0 and ts == (0,)
        after = first_token if first_in else after
        arrived = _gather_wait(f"gather_wait_{tag}", l, ts, held[0], sems[l, ts[0] if l == 0 else 0], after,
                               both=l > 0)
        if first_in:
            arrived, rest_sems, _ = _gather_start("gather_start_rest", tuple(range(1, DEPTH)), (0, 1, 2, 3),
                                                  arrived, both=True)
            sems.update({(k, 0): v for k, v in rest_sems.items()})
        held[0] = _pass_on(f"pass_on_{tag}", l, ts, arrived) if l == 0 else arrived
        return held[0]

    flights = {}

    def await_flight(l, ts, afters):
        g, landing, sm, _ = flights[l, ts]
        flights[l, ts] = _reduce_wait(f"reduce_wait_{l}_{ts[0]}", ts, g, landing, sm, afters)

    def on_grads(l, ts, grads):
        if ts == (0, 1) and l + 1 < DEPTH:
            await_flight(l + 1, (2, 3), [grads[0]])
            await_flight(l + 1, (0, 1), [grads[0]])
        flights[l, ts] = _reduce_start(f"reduce_start_{l}_{ts[0]}", ts, grads)
        return flights[l, ts][3]

    loss_part, grad_x, gsmall = _local_step(x[0], loss_target[0], layer_weights, on_grads, small)
    loss = lax.psum(loss_part[0, 0], ("x", "y", "c"))
    order = [n for n in SMALL]
    packed = _pack([gsmall[n] for n in order])

    out = {n: [lax.empty(w[n].shape, F32) for _ in range(4)] for n in LARGE}
    for l in reversed(range(DEPTH)):
        if l == 0:
            afters = [grad_x, packed] + [out[n][0] for n in LARGE]
            await_flight(0, (2, 3), afters)
            await_flight(0, (0, 1), afters)
        sums = [None] * 4
        for ts in ((0, 1), (2, 3)):
            g, landing = flights[l, ts]
            for i, t in enumerate(ts):
                sums[t] = _add4(f"add4_{LARGE[t]}_{l}", t, g[i], landing[i], b1)
        theirs = _swap_sib(f"swap_sib_{l}", sums)
        for t, n in enumerate(LARGE):
            out[n] = _adamw_pair(f"adamw_{n}_{l}", l, sums[t], theirs[t], w[n], m[n], v[n], out[n])

    rows = packed.shape[0]
    summed = _sum8("sum_small", _all_gather8("gather_small", packed, out[LARGE[0]][0]).reshape(8, rows, 128))
    gfull = dict(zip(order, _unpack(summed, [gsmall[n].shape for n in order])))
    gfull["conv_w"] = lax.dynamic_slice_in_dim(gfull["conv_w"], (2 * ax + ay) * 64, 64, axis=2)
    res = _adamw("adamw_small", _pack([gfull[n] for n in order]), _pack([w[n] for n in order]),
                 _pack([m[n] for n in order]), _pack([v[n] for n in order]))
    for n, parts in zip(order, zip(*[_unpack(r, [w[k].shape for k in order]) for r in res])):
        out[n] = list(parts)

    names = ("norm1_g", "w_in", "q_norm_g", "k_norm_g", "rel_bias", "conv_w", "pool_w", "pool_scale", "w_out",
             "norm2_g", "w_mlp1", "w_mlp2")
    flat = [loss, grad_x[None]]
    for i in range(4):
        flat += [out[n][i] for n in names]
    return tuple(flat)
```

```python
import jax
import jax.numpy as jnp
from jax import lax
from jax.experimental import pallas as pl
from jax.experimental.pallas import tpu as pltpu

F32 = jnp.float32
BF16 = jnp.bfloat16

D = 1024
DEPTH = 4
CH = 64
NPREV = 8
KB = (NPREV + 1) * CH
PADR = NPREV * CH
HD = 64
AW = 512
CW = 256
PWD = 256
DIN = 3 * AW + 3 * CW + PWD
DFF = 4 * D
NIDX = 384
EPS = 1e-6
NEG_INF = -1e30

ADAM_LR = 0.001
ADAM_B1 = 0.9
ADAM_B2 = 0.999
ADAM_EPS = 1e-08
ADAM_WD = 0.01
ADAM_STEP = 10

VMEM_LIMIT = 52 * 1024 * 1024
MM_ROWS = 512


def _mm_rows(k, n):
    return 2 * MM_ROWS if k + n <= 2048 else MM_ROWS


MESH = pl.DeviceIdType.MESH
ANY = pl.BlockSpec(memory_space=pl.ANY)


def _cp(*sem):
    return pltpu.CompilerParams(dimension_semantics=sem, vmem_limit_bytes=VMEM_LIMIT)


def _inv_rms(x):
    return lax.rsqrt(jnp.mean(x * x, axis=-1, keepdims=True) + EPS)


def _head_mean_matrix():
    r = lax.broadcasted_iota(jnp.int32, (AW, AW), 0) // HD
    c = lax.broadcasted_iota(jnp.int32, (AW, AW), 1) // HD
    return jnp.where(r == c, 1.0 / HD, 0.0).astype(BF16)


def _two_pass_dot(x, m):
    hi = x.astype(BF16)
    lo = (x - hi.astype(F32)).astype(BF16)
    return (jnp.dot(hi, m, preferred_element_type=F32)
            + jnp.dot(lo, m, preferred_element_type=F32))


def _head_mean(x, hm):
    return _two_pass_dot(x, hm)


def _rmsnorm(name, x, g):
    s = x.shape[0]
    tm = 512

    def body(x_ref, g_ref, h_ref):
        xv = x_ref[...]
        h_ref[...] = (xv * _inv_rms(xv) * g_ref[...]).astype(BF16)

    return pl.pallas_call(
        body, name=name, grid=(s // tm,),
        in_specs=[pl.BlockSpec((tm, D), lambda i: (i, 0)), pl.BlockSpec((1, D), lambda i: (0, 0))],
        out_specs=pl.BlockSpec((tm, D), lambda i: (i, 0)),
        out_shape=jax.ShapeDtypeStruct((s, D), BF16),
        compiler_params=_cp("parallel"),
    )(x, g)


def _relu2(a):
    r = jnp.maximum(a, jnp.zeros_like(a))
    return r * r


def _mm_nn(name, a, w, l, out_dtype):
    s, k = a.shape
    n = w.shape[2]
    tm = _mm_rows(k, n)

    def body(a_ref, w_ref, o_ref):
        o_ref[...] = jnp.dot(a_ref[...], w_ref[...], preferred_element_type=F32).astype(o_ref.dtype)

    return pl.pallas_call(
        body, name=name, grid=(s // tm,),
        in_specs=[pl.BlockSpec((tm, k), lambda i: (i, 0)),
                  pl.BlockSpec((None, k, n), lambda i: (l, 0, 0))],
        out_specs=pl.BlockSpec((tm, n), lambda i: (i, 0)),
        out_shape=jax.ShapeDtypeStruct((s, n), out_dtype),
        compiler_params=_cp("parallel"),
    )(a, w)


def _mm_res_norm(name, a, w, l, res, g):
    s, k = a.shape
    tm = _mm_rows(k, D)

    def body(a_ref, w_ref, r_ref, g_ref, x_ref, h_ref):
        acc = r_ref[...] + jnp.dot(a_ref[...], w_ref[...], preferred_element_type=F32)
        x_ref[...] = acc
        h_ref[...] = (acc * _inv_rms(acc) * g_ref[...]).astype(BF16)

    return pl.pallas_call(
        body, name=name, grid=(s // tm,),
        in_specs=[pl.BlockSpec((tm, k), lambda i: (i, 0)),
                  pl.BlockSpec((None, k, D), lambda i: (l, 0, 0)),
                  pl.BlockSpec((tm, D), lambda i: (i, 0)),
                  pl.BlockSpec((1, D), lambda i: (0, 0))],
        out_specs=[pl.BlockSpec((tm, D), lambda i: (i, 0))] * 2,
        out_shape=[jax.ShapeDtypeStruct((s, D), F32), jax.ShapeDtypeStruct((s, D), BF16)],
        compiler_params=_cp("parallel"),
    )(a, w, res, g)


def _mlp_fwd(name, h2, w1, w2, l, res, last):
    s = h2.shape[0]
    tm = 256
    final = last.shape[0] == s

    def body(h_ref, w1_ref, w2_ref, r_ref, last_ref, a_ref, first_ref, second_ref, *loss_ref):
        a = jnp.dot(h_ref[...], w1_ref[...], preferred_element_type=F32).astype(BF16)
        a_ref[...] = a
        acc = r_ref[...] + jnp.dot(_relu2(a), w2_ref[...], preferred_element_type=F32)
        if not final:
            first_ref[...] = acc
            second_ref[...] = (acc * _inv_rms(acc) * last_ref[...]).astype(BF16)
            return
        e = acc - last_ref[...]
        dy = e * (1.0 / D)
        first_ref[...] = dy
        second_ref[...] = dy.astype(BF16)
        part = 0.5 * jnp.sum(jnp.mean(e * e, axis=-1, keepdims=True), axis=0, keepdims=True)
        i = pl.program_id(0)

        @pl.when(i == 0)
        def _():
            loss_ref[0][...] = part

        @pl.when(i > 0)
        def _():
            loss_ref[0][...] += part

    once = pl.Buffered(1)
    rows = pl.BlockSpec((tm, D), lambda i: (i, 0))
    one = pl.BlockSpec((1, 1), lambda i: (0, 0))
    return pl.pallas_call(
        body, name=name, grid=(s // tm,),
        in_specs=[rows,
                  pl.BlockSpec((None, D, DFF), lambda i: (l, 0, 0), pipeline_mode=once),
                  pl.BlockSpec((None, DFF, D), lambda i: (l, 0, 0), pipeline_mode=once),
                  rows, rows if final else pl.BlockSpec((1, D), lambda i: (0, 0))],
        out_specs=[pl.BlockSpec((tm, DFF), lambda i: (i, 0)), rows, rows] + ([one] if final else []),
        out_shape=[jax.ShapeDtypeStruct((s, DFF), BF16), jax.ShapeDtypeStruct((s, D), F32),
                   jax.ShapeDtypeStruct((s, D), BF16)] + ([jax.ShapeDtypeStruct((1, 1), F32)] if final else []),
        compiler_params=_cp("arbitrary" if final else "parallel"),
    )(h2, w1, w2, res, last)


def _qkv(name, p, qg, kg):
    s = p.shape[0]
    tm = PADR
    nb = s // tm

    def body(pq_ref, pk_ref, pv_ref, qg_ref, kg_ref, q_ref, qt_ref, k_ref, kt_ref, v_ref, vt_ref):
        t = pl.program_id(0)
        hm = _head_mean_matrix()

        def nrm(x, g):
            return x * lax.rsqrt(_head_mean(x * x, hm) + EPS) * g

        first = t == 0
        qq = nrm(pq_ref[...], qg_ref[...]) * 0.125
        kk = jnp.where(first, 0.0, nrm(pk_ref[...], kg_ref[...]))
        vv = jnp.where(first, 0.0, pv_ref[...])
        q_ref[...] = qq.astype(BF16)
        qt_ref[...] = qq.T.astype(BF16)
        k_ref[...] = kk.astype(BF16)
        kt_ref[...] = kk.T.astype(BF16)
        v_ref[...] = vv.astype(BF16)
        vt_ref[...] = vv.T.astype(BF16)

    def src(col):
        return pl.BlockSpec((tm, AW), lambda t: (jnp.maximum(t - 1, 0), col))

    gspec = pl.BlockSpec((1, AW), lambda t: (0, 0))
    rows = pl.BlockSpec((tm, AW), lambda t: (t, 0))
    cols = pl.BlockSpec((AW, tm), lambda t: (0, t))
    return pl.pallas_call(
        body, name=name, grid=(nb + 1,),
        in_specs=[src(0), src(1), src(2), gspec, gspec],
        out_specs=[pl.BlockSpec((tm, AW), lambda t: (jnp.maximum(t - 1, 0), 0)),
                   pl.BlockSpec((AW, tm), lambda t: (0, jnp.maximum(t - 1, 0))),
                   rows, cols, rows, cols],
        out_shape=[jax.ShapeDtypeStruct((s, AW), BF16), jax.ShapeDtypeStruct((AW, s), BF16),
                   jax.ShapeDtypeStruct((s + PADR, AW), BF16), jax.ShapeDtypeStruct((AW, s + PADR), BF16),
                   jax.ShapeDtypeStruct((s + PADR, AW), BF16), jax.ShapeDtypeStruct((AW, s + PADR), BF16)],
        compiler_params=_cp("arbitrary"),
    )(p, p, p, qg, kg)


NBAND = KB // CH
HIGHEST = lax.Precision.HIGHEST
NT_DIMS = (((1,), (1,)), ((), ()))


def _onehot_table(a):
    m = lax.broadcasted_iota(jnp.int32, (128, NIDX), 0)
    idx = lax.broadcasted_iota(jnp.int32, (128, NIDX), 1)
    rel = jnp.clip(KB - 1 - (CH * a + m), -128, 128) + 128
    return jnp.where(rel == idx, 1.0, 0.0).astype(F32)


def _onehot_diagonal():
    r = lax.broadcasted_iota(jnp.int32, (CH * CH, 128), 0)
    m = lax.broadcasted_iota(jnp.int32, (CH * CH, 128), 1)
    return jnp.where((r % CH) - (r // CH) + (CH - 1) == m, 1.0, 0.0).astype(F32)


def _bias_expand(name, rb):
    def body(rb_ref, o_ref):
        along = [lax.dot_general(rb_ref[...], _onehot_table(a), NT_DIMS, preferred_element_type=F32,
                                 precision=HIGHEST) for a in range(NBAND)]
        o_ref[...] = lax.dot_general(jnp.concatenate(along, axis=0), _onehot_diagonal(), NT_DIMS,
                                     preferred_element_type=F32, precision=HIGHEST)

    return pl.pallas_call(
        body, name=name, grid=(DEPTH,),
        in_specs=[pl.BlockSpec((None, 8, NIDX), lambda l: (l, 0, 0))],
        out_specs=pl.BlockSpec((None, NBAND * 8, CH * CH), lambda l: (l, 0, 0)),
        out_shape=jax.ShapeDtypeStruct((DEPTH, NBAND * 8, CH * CH), F32),
        compiler_params=_cp("parallel"),
    )(rb)


def _bias_reduce(name, db):
    def body(db_ref, o_ref):
        along = jnp.dot(db_ref[...], _onehot_diagonal(), preferred_element_type=F32, precision=HIGHEST)
        acc = jnp.zeros((8, NIDX), F32)
        for a in range(NBAND):
            acc = acc + jnp.dot(along[8 * a:8 * a + 8, :], _onehot_table(a), preferred_element_type=F32,
                                precision=HIGHEST)
        o_ref[...] = acc

    return pl.pallas_call(
        body, name=name, grid=(DEPTH,),
        in_specs=[pl.BlockSpec((None, NBAND * 8, CH * CH), lambda l: (l, 0, 0))],
        out_specs=pl.BlockSpec((None, 8, NIDX), lambda l: (l, 0, 0)),
        out_shape=jax.ShapeDtypeStruct((DEPTH, 8, NIDX), F32),
        compiler_params=_cp("parallel"),
    )(db)


def _bias_layout(flat):
    b = flat.reshape(DEPTH, NBAND, 8, CH, CH).transpose(0, 2, 1, 4, 3).reshape(DEPTH, 4, 2, KB, CH)
    pair = b.transpose(0, 1, 3, 2, 4).reshape(DEPTH, 4, KB, 128)
    first = jnp.pad(pair, ((0, 0), (0, 0), (0, CH), (0, 0)), constant_values=NEG_INF)
    second = jnp.pad(pair, ((0, 0), (0, 0), (CH, 0), (0, 0)), constant_values=NEG_INF)
    return jnp.concatenate([first, second], axis=3)


def _bias_unlayout(dbt):
    b = dbt.reshape(DEPTH, 4, NBAND, CH, 2, CH)
    return b.transpose(0, 2, 1, 4, 5, 3).reshape(DEPTH, NBAND * 8, CH * CH)


UNIT = 2 * CH
BAND2 = KB + CH


def _pair_weights(xt):
    x = xt.astype(F32)
    row = lax.broadcasted_iota(jnp.int32, (128, UNIT), 0)
    low = lax.broadcasted_iota(jnp.int32, (128, UNIT), 1) < HD
    swapped = pltpu.roll(x, HD, 1)
    same = (row < HD) == low
    first = jnp.where(same, jnp.where(low, x, swapped), 0.0)
    second = jnp.where(same, jnp.where(low, swapped, x), 0.0)
    return jnp.concatenate([first, second], axis=1).astype(BF16)


def _pair_rows(x):
    low = lax.broadcasted_iota(jnp.int32, (CH, 128), 1) < HD
    zero = jnp.zeros((CH, 128), x.dtype)
    parts = []
    for c in range(2):
        xc = x[c * CH:(c + 1) * CH, :]
        parts += [jnp.where(low, xc, zero), jnp.where(low, zero, xc)]
    return jnp.concatenate(parts, axis=0)


def _unpair(raw):
    b0, b1 = raw[:, 0:128], raw[:, 128:256]
    row = lax.broadcasted_iota(jnp.int32, (128, 128), 0)
    low = lax.broadcasted_iota(jnp.int32, (128, 128), 1) < HD
    top = jnp.where(low, b0, pltpu.roll(b1, HD, 1))
    bottom = jnp.where(low, pltpu.roll(b0, HD, 1), b1)
    return jnp.where(row < HD, top, bottom).T


def _scores_t(kb, qw, bias2, row0, padded):
    s = jnp.dot(kb, qw, preferred_element_type=F32) + bias2
    if padded:
        s = jnp.where(row0 + lax.broadcasted_iota(jnp.int32, (BAND2, 256), 0) >= PADR, s, NEG_INF)
    return s


def _unit_loops(s, unit):
    lax.fori_loop(0, PADR // UNIT, lambda u, c: unit(u, True, c), 0, unroll=4)
    lax.fori_loop(PADR // UNIT, s // UNIT, lambda u, c: unit(u, False, c), 0, unroll=7)


def _attn_fwd(name, kp, qt, vt, bias2, l):
    s = qt.shape[1]
    nu = s // UNIT

    def body(k_ref, qt_ref, vt_ref, b_ref, o_ref, lse_ref):
        def unit(u, padded, carry):
            r0 = pl.multiple_of(u * UNIT, UNIT)
            kb, vtb = k_ref[pl.ds(r0, BAND2), :], vt_ref[:, pl.ds(r0, BAND2)]
            qw = _pair_weights(qt_ref[:, pl.ds(r0, UNIT)])
            raws, stats = [], []
            for c in range(2):
                cols = slice(128 * c, 128 * (c + 1))
                sc = jnp.dot(kb, qw[:, cols], preferred_element_type=F32) + b_ref[:, cols]
                if padded:
                    sc = jnp.where(r0 + lax.broadcasted_iota(jnp.int32, (BAND2, 128), 0) >= PADR, sc, NEG_INF)
                top = jnp.max(sc, axis=0, keepdims=True)
                e = jnp.exp(sc - top)
                total = jnp.sum(e, axis=0, keepdims=True)
                raws.append(jnp.dot(vtb, e.astype(BF16), preferred_element_type=F32) * (1.0 / total))
                stats.append(top + jnp.log(total))
            o_ref[pl.ds(r0, UNIT), :] = _unpair(jnp.concatenate(raws, axis=1)).astype(BF16)
            lse_ref[u] = jnp.broadcast_to(jnp.concatenate(stats, axis=1), (8, 256))
            return carry

        _unit_loops(s, unit)

    return pl.pallas_call(
        body, name=name, grid=(AW // 128,),
        in_specs=[pl.BlockSpec((s + PADR, 128), lambda h: (0, h)),
                  pl.BlockSpec((128, s), lambda h: (h, 0)),
                  pl.BlockSpec((128, s + PADR), lambda h: (h, 0)),
                  pl.BlockSpec((None, None, BAND2, 256), lambda h: (l, h, 0, 0))],
        out_specs=[pl.BlockSpec((s, 128), lambda h: (0, h)),
                   pl.BlockSpec((None, nu, 8, 256), lambda h: (h, 0, 0, 0))],
        out_shape=[jax.ShapeDtypeStruct((s, AW), BF16), jax.ShapeDtypeStruct((4, nu, 8, 256), F32)],
        compiler_params=_cp("parallel"),
    )(kp, qt, vt, bias2)


def _attn_bwd(name, q, qt, kp, kt, vp, bias2, l, do, dot, lse, dl, db_all):
    s = q.shape[0]
    nu = s // UNIT

    def body(q_ref, qt_ref, k_ref, kt_ref, v_ref, b_ref, do_ref, dot_ref, lse_ref, dl_ref, dbin_ref,
             dq_ref, dk_ref, dvb_ref, db_ref, dv_ref):
        del dbin_ref
        dk_ref[...] = jnp.zeros_like(dk_ref)
        dv_ref[...] = jnp.zeros_like(dv_ref)
        db_ref[...] = jnp.zeros_like(db_ref)

        def unit(u, padded, carry):
            r0 = pl.multiple_of(u * UNIT, UNIT)
            rows, band = pl.ds(r0, UNIT), pl.ds(r0, BAND2)
            sc = _scores_t(k_ref[band, :], _pair_weights(qt_ref[:, rows]), b_ref[...], r0, padded)
            pt = jnp.exp(sc - lse_ref[u][0:1, :])
            dpt = jnp.dot(v_ref[band, :], _pair_weights(dot_ref[:, rows]), preferred_element_type=F32)
            ds = pt * (dpt - dl_ref[u][0:1, :])
            db_ref[...] += ds[0:KB, 0:128] + ds[CH:BAND2, 128:256]
            dsb = ds.astype(BF16)
            dq_ref[rows, :] = _unpair(jnp.dot(kt_ref[:, band], dsb, preferred_element_type=F32))
            dk_ref[band, :] += jnp.dot(dsb, _pair_rows(q_ref[rows, :]), preferred_element_type=F32)
            dv_ref[band, :] += jnp.dot(pt.astype(BF16), _pair_rows(do_ref[rows, :]), preferred_element_type=F32)
            return carry

        _unit_loops(s, unit)
        dvb_ref[...] = dv_ref[...].astype(BF16)

    row_q = pl.BlockSpec((s, 128), lambda h: (0, h))
    col_q = pl.BlockSpec((128, s), lambda h: (h, 0))
    row_k = pl.BlockSpec((s + PADR, 128), lambda h: (0, h))
    col_k = pl.BlockSpec((128, s + PADR), lambda h: (h, 0))
    stat = pl.BlockSpec((None, nu, 8, 256), lambda h: (h, 0, 0, 0))
    return pl.pallas_call(
        body, name=name, grid=(AW // 128,),
        in_specs=[row_q, col_q, row_k, col_k, row_k,
                  pl.BlockSpec((None, None, BAND2, 256), lambda h: (l, h, 0, 0)), row_q, col_q, stat, stat, ANY],
        out_specs=[row_q, row_k, row_k, pl.BlockSpec((None, None, KB, 128), lambda h: (l, h, 0, 0))],
        out_shape=[jax.ShapeDtypeStruct((s, AW), F32),
                   jax.ShapeDtypeStruct((s + PADR, AW), F32),
                   jax.ShapeDtypeStruct((s + PADR, AW), BF16),
                   jax.ShapeDtypeStruct((DEPTH, 4, KB, 128), F32)],
        scratch_shapes=[pltpu.VMEM((s + PADR, 128), F32)],
        input_output_aliases={10: 3},
        compiler_params=_cp("parallel"),
    )(q, qt, kp, kt, vp, bias2, do, dot, lse, dl, db_all)


def _rowsum_layout(dl, nu):
    d = dl[:, :8].reshape(nu, 2, CH, 4, 2)
    d = d.transpose(3, 0, 1, 4, 2).reshape(4, nu, 1, 256)
    return jnp.broadcast_to(d, (4, nu, 8, 256))


def _rows_before(cur, prev, k):
    row = lax.broadcasted_iota(jnp.int32, cur.shape, 0)
    return jnp.where(row >= k, pltpu.roll(cur, k, 0), pltpu.roll(prev, k, 0))


def _rows_after(cur, nxt, k):
    n = cur.shape[0]
    row = lax.broadcasted_iota(jnp.int32, cur.shape, 0)
    return jnp.where(row < n - k, pltpu.roll(cur, n - k, 0), pltpu.roll(nxt, n - k, 0))


def _pool_window_lanes():
    lg = lax.broadcasted_iota(jnp.int32, (1, PWD), 1) // 64
    return lg, jnp.where(lg == 0, 2.0, jnp.where(lg == 1, 4.0, jnp.where(lg == 2, 8.0, 16.0))).astype(F32)


def _pool_mean_minus_token(u, up, row0):
    lg, wv = _pool_window_lanes()
    sums = []
    c, p = u, up
    for k in (1, 2, 4, 8):
        c2 = c + _rows_before(c, p, k)
        p = p + pltpu.roll(p, k, 0)
        c = c2
        sums.append(c)
    win = jnp.where(lg == 0, sums[0], jnp.where(lg == 1, sums[1], jnp.where(lg == 2, sums[2], sums[3])))
    pos1 = (row0 + lax.broadcasted_iota(jnp.int32, u.shape, 0) + 1).astype(F32)
    cnt = jnp.minimum(pos1, wv)
    return win / cnt - u, cnt


def _conv_taps(z, zp, w0, w1, w2):
    z1 = _rows_before(z, zp, 1)
    z2 = _rows_before(z, zp, 2)
    return (w0 * z2 + w1 * z1) + w2 * z, z1, z2


CP_TM = 1024
HALO = 16


def _halo_before(tm, col):
    return pl.BlockSpec((HALO, CW), lambda i: (jnp.maximum(i * (tm // HALO) - 1, 0), col))


def _halo_after(tm, col, rows):
    return pl.BlockSpec((HALO, CW), lambda i: (jnp.minimum((i + 1) * (tm // HALO), rows // HALO - 1), col))


def _as_block_end(halo, tm):
    return jnp.concatenate([jnp.zeros((tm - HALO, halo.shape[1]), halo.dtype), halo], axis=0)


def _as_block_start(halo, tm):
    return jnp.concatenate([halo, jnp.zeros((tm - HALO, halo.shape[1]), halo.dtype)], axis=0)


def _convpool_fwd(name, p, o, cw, pwbd, ps):
    s = p.shape[0]
    tm = CP_TM
    nb = s // tm

    def body(gb_ref, gc_ref, hin_ref, u_ref, gcp_ref, hinp_ref, up_ref, o_ref, cw_ref, pw_ref, ps_ref, mix_ref):
        i = pl.program_id(0)
        has_prev = i > 0
        z = gc_ref[...] * hin_ref[...]
        zp = _as_block_end(jnp.where(has_prev, gcp_ref[...] * hinp_ref[...], 0.0), tm)
        y3, _, _ = _conv_taps(z, zp, cw_ref[0:1, :], cw_ref[1:2, :], cw_ref[2:3, :])
        m, _ = _pool_mean_minus_token(u_ref[...], _as_block_end(jnp.where(has_prev, up_ref[...], 0.0), tm), i * tm)
        yp = jnp.dot(m.astype(BF16), pw_ref[...].astype(BF16), preferred_element_type=F32) * ps_ref[...]
        mix_ref[:, 0:AW] = o_ref[...]
        mix_ref[:, AW:AW + CW] = (gb_ref[...] * y3).astype(BF16)
        mix_ref[:, AW + CW:D] = yp.astype(BF16)

    def cur(col):
        return pl.BlockSpec((tm, CW), lambda i: (i, col))

    def whole(a):
        return pl.BlockSpec(a.shape, lambda i: (0,) * a.ndim)

    return pl.pallas_call(
        body, name=name, grid=(nb,),
        in_specs=[cur(6), cur(7), cur(8), cur(9), _halo_before(tm, 7), _halo_before(tm, 8), _halo_before(tm, 9),
                  pl.BlockSpec((tm, AW), lambda i: (i, 0)), whole(cw), whole(pwbd), whole(ps)],
        out_specs=pl.BlockSpec((tm, D), lambda i: (i, 0)),
        out_shape=jax.ShapeDtypeStruct((s, D), BF16),
        compiler_params=_cp("parallel"),
    )(p, p, p, p, p, p, p, o, cw, pwbd, ps)


def _convpool_bwd(name, p, dmix, cw, pwbd, ps):
    s = p.shape[0]
    tm = CP_TM // 2
    nb = s // tm

    def body(gb_ref, gc_ref, hin_ref, u_ref, gcp_ref, hinp_ref, up_ref, gbn_ref, dyc_ref, dyp_ref, dycn_ref, dypn_ref,
             cw_ref, pw_ref, ps_ref, dcp_ref, dw0_ref, dw1_ref, dw2_ref, dps_ref, dpw_ref):
        i = pl.program_id(0)
        has_prev = i > 0
        has_next = i < nb - 1
        w0, w1, w2 = cw_ref[0:1, :], cw_ref[1:2, :], cw_ref[2:3, :]
        gb, gc, hin = gb_ref[...], gc_ref[...], hin_ref[...]
        dyc = dyc_ref[...]
        z = gc * hin
        zp = _as_block_end(jnp.where(has_prev, gcp_ref[...] * hinp_ref[...], 0.0), tm)
        y3, z1, z2 = _conv_taps(z, zp, w0, w1, w2)
        dy3 = dyc * gb
        dy3n = _as_block_start(jnp.where(has_next, dycn_ref[...] * gbn_ref[...], 0.0), tm)
        dz = w2 * dy3 + w1 * _rows_after(dy3, dy3n, 1) + w0 * _rows_after(dy3, dy3n, 2)
        pw = pw_ref[...].astype(BF16)
        psv = ps_ref[...]
        m, cnt = _pool_mean_minus_token(u_ref[...], _as_block_end(jnp.where(has_prev, up_ref[...], 0.0), tm), i * tm)
        mb = m.astype(BF16)
        dyp = dyp_ref[...]
        dmp = (dyp * psv).astype(BF16)
        dmpn = jnp.where(has_next, dypn_ref[...] * psv, 0.0).astype(BF16)
        nt = (((1,), (1,)), ((), ()))
        dm = lax.dot_general(dmp, pw, nt, preferred_element_type=F32)
        dmn = lax.dot_general(dmpn, pw, nt, preferred_element_type=F32)
        lg, wv = _pool_window_lanes()
        cc, cn = dm / cnt, _as_block_start(dmn / wv, tm)
        sums = []
        for k in (1, 2, 4, 8):
            c2 = cc + _rows_after(cc, cn, k)
            cn = cn + pltpu.roll(cn, tm - k, 0)
            cc = c2
            sums.append(cc)
        du = jnp.where(lg == 0, sums[0], jnp.where(lg == 1, sums[1], jnp.where(lg == 2, sums[2], sums[3]))) - dm
        dcp_ref[:, 0:CW] = (dyc * y3).astype(BF16)
        dcp_ref[:, CW:2 * CW] = (dz * hin).astype(BF16)
        dcp_ref[:, 2 * CW:3 * CW] = (dz * gc).astype(BF16)
        dcp_ref[:, 3 * CW:4 * CW] = du.astype(BF16)
        parts = (jnp.sum(dy3 * z2, axis=0, keepdims=True),
                 jnp.sum(dy3 * z1, axis=0, keepdims=True),
                 jnp.sum(dy3 * z, axis=0, keepdims=True),
                 jnp.sum(dyp * jnp.dot(mb, pw, preferred_element_type=F32), axis=0, keepdims=True),
                 lax.dot_general(mb, dmp, (((0,), (0,)), ((), ())), preferred_element_type=F32))
        accs = (dw0_ref, dw1_ref, dw2_ref, dps_ref, dpw_ref)

        @pl.when(i == 0)
        def _():
            for a, v in zip(accs, parts):
                a[...] = v

        @pl.when(i > 0)
        def _():
            for a, v in zip(accs, parts):
                a[...] += v

    def cur(col):
        return pl.BlockSpec((tm, CW), lambda i: (i, col))

    def prev(col):
        return _halo_before(tm, col)

    def nxt(col):
        return _halo_after(tm, col, s)

    def whole(shape):
        return pl.BlockSpec(shape, lambda i: (0,) * len(shape))

    row = jax.ShapeDtypeStruct((1, CW), F32)
    return pl.pallas_call(
        body, name=name, grid=(nb,),
        in_specs=[cur(6), cur(7), cur(8), cur(9), prev(7), prev(8), prev(9), nxt(6),
                  cur(0), cur(1), nxt(0), nxt(1), whole(cw.shape), whole(pwbd.shape), whole(ps.shape)],
        out_specs=[pl.BlockSpec((tm, D), lambda i: (i, 0)), whole((1, CW)), whole((1, CW)), whole((1, CW)),
                   whole((1, PWD)), whole((PWD, PWD))],
        out_shape=[jax.ShapeDtypeStruct((s, D), BF16), row, row, row, row,
                   jax.ShapeDtypeStruct((PWD, PWD), F32)],
        compiler_params=_cp("arbitrary"),
    )(p, p, p, p, p, p, p, p, dmix, dmix, dmix, dmix, cw, pwbd, ps)


def _qkv_bwd(name, p, dq, dkp, dvp, dcp, qg, kg):
    s = p.shape[0]
    tm = 512
    off = PADR // tm

    def body(pq_ref, pk_ref, dq_ref, dk_ref, dv_ref, dcp_ref, qg_ref, kg_ref, dp_ref, dqg_ref, dkg_ref):
        i = pl.program_id(0)
        hm = _head_mean_matrix()

        def nrm_bwd(x, g, dy):
            r = lax.rsqrt(_head_mean(x * x, hm) + EPS)
            xn = x * r
            dxn = dy * g
            dx = r * (dxn - xn * _head_mean(dxn * xn, hm))
            dg = jnp.sum(dy * xn, axis=0, keepdims=True)
            dg = (dg[:, 0:128] + dg[:, 128:256]) + (dg[:, 256:384] + dg[:, 384:512])
            return dx, dg + pltpu.roll(dg, HD, 1)

        dxq, dgq = nrm_bwd(pq_ref[...], qg_ref[...], dq_ref[...] * 0.125)
        dxk, dgk = nrm_bwd(pk_ref[...], kg_ref[...], dk_ref[...])
        dp_ref[:, 0:AW] = dxq.astype(BF16)
        dp_ref[:, AW:2 * AW] = dxk.astype(BF16)
        dp_ref[:, 2 * AW:3 * AW] = dv_ref[...].astype(BF16)
        dp_ref[:, 3 * AW:DIN] = dcp_ref[...]

        @pl.when(i == 0)
        def _():
            dqg_ref[...] = dgq
            dkg_ref[...] = dgk

        @pl.when(i > 0)
        def _():
            dqg_ref[...] += dgq
            dkg_ref[...] += dgk

    gspec = pl.BlockSpec((1, AW), lambda i: (0, 0))
    gout = pl.BlockSpec((1, 128), lambda i: (0, 0))
    return pl.pallas_call(
        body, name=name, grid=(s // tm,),
        in_specs=[pl.BlockSpec((tm, AW), lambda i: (i, 0)), pl.BlockSpec((tm, AW), lambda i: (i, 1)),
                  pl.BlockSpec((tm, AW), lambda i: (i, 0)),
                  pl.BlockSpec((tm, AW), lambda i: (i + off, 0)),
                  pl.BlockSpec((tm, AW), lambda i: (i + off, 0)),
                  pl.BlockSpec((tm, D), lambda i: (i, 0)), gspec, gspec],
        out_specs=[pl.BlockSpec((tm, DIN), lambda i: (i, 0)), gout, gout],
        out_shape=[jax.ShapeDtypeStruct((s, DIN), BF16), jax.ShapeDtypeStruct((1, 128), F32),
                   jax.ShapeDtypeStruct((1, 128), F32)],
        compiler_params=_cp("arbitrary"),
    )(p, p, dq, dkp, dvp, dcp, qg, kg)


def _mm_nt_relu(name, dxb, w, l, a):
    s = dxb.shape[0]
    tm = MM_ROWS

    def body(d_ref, w_ref, a_ref, o_ref):
        df = lax.dot_general(d_ref[...], w_ref[...], NT_DIMS, preferred_element_type=F32)
        o_ref[...] = (df * (2.0 * jnp.maximum(a_ref[...].astype(F32), 0.0))).astype(BF16)

    return pl.pallas_call(
        body, name=name, grid=(s // tm,),
        in_specs=[pl.BlockSpec((tm, D), lambda i: (i, 0)),
                  pl.BlockSpec((None, DFF, D), lambda i: (l, 0, 0)),
                  pl.BlockSpec((tm, DFF), lambda i: (i, 0))],
        out_specs=pl.BlockSpec((tm, DFF), lambda i: (i, 0)),
        out_shape=jax.ShapeDtypeStruct((s, DFF), BF16),
        compiler_params=_cp("parallel"),
    )(dxb, w, a)


def _proj_out_bwd(name, dxb, w, l, mix):
    s = dxb.shape[0]
    tm = _mm_rows(D, D)

    def body(d_ref, w_ref, o_ref, do_ref, dot_ref, dcp_ref, dl_ref):
        d = d_ref[...]
        wa, wc = w_ref[0:AW, :], w_ref[AW:D, :]
        do = lax.dot_general(d, wa, NT_DIMS, preferred_element_type=F32)
        do_ref[...] = do.astype(BF16)
        dot_ref[...] = lax.dot_general(wa, d, NT_DIMS, preferred_element_type=F32).astype(BF16)
        dcp_ref[...] = lax.dot_general(d, wc, NT_DIMS, preferred_element_type=F32)
        head = lax.broadcasted_iota(jnp.int32, (AW, 128), 0) // HD
        pick = jnp.where(head == lax.broadcasted_iota(jnp.int32, (AW, 128), 1), 1.0, 0.0).astype(BF16)
        dl_ref[...] = _two_pass_dot(do * o_ref[...].astype(F32), pick)

    return pl.pallas_call(
        body, name=name, grid=(s // tm,),
        in_specs=[pl.BlockSpec((tm, D), lambda i: (i, 0)),
                  pl.BlockSpec((None, D, D), lambda i: (l, 0, 0)),
                  pl.BlockSpec((tm, AW), lambda i: (i, 0))],
        out_specs=[pl.BlockSpec((tm, AW), lambda i: (i, 0)), pl.BlockSpec((AW, tm), lambda i: (0, i)),
                   pl.BlockSpec((tm, D - AW), lambda i: (i, 0)), pl.BlockSpec((tm, 128), lambda i: (i, 0))],
        out_shape=[jax.ShapeDtypeStruct((s, AW), BF16), jax.ShapeDtypeStruct((AW, s), BF16),
                   jax.ShapeDtypeStruct((s, D - AW), F32), jax.ShapeDtypeStruct((s, 128), F32)],
        compiler_params=_cp("parallel"),
    )(dxb, w, mix)


def _mm_nt_normbwd(name, gy, w, l, x, g, dres, dep):
    s, k = gy.shape
    tm = MM_ROWS

    def body(gy_ref, w_ref, x_ref, g_ref, dr_ref, dep_ref, dx_ref, dxb_ref, dg_ref):
        del dep_ref
        i = pl.program_id(0)
        dh = lax.dot_general(gy_ref[...], w_ref[...], NT_DIMS, preferred_element_type=F32)
        xv = x_ref[...]
        r = _inv_rms(xv)
        xn = xv * r
        dxn = dh * g_ref[...]
        dx = r * (dxn - xn * jnp.mean(dxn * xn, axis=-1, keepdims=True)) + dr_ref[...]
        dx_ref[...] = dx
        dxb_ref[...] = dx.astype(BF16)
        part = jnp.sum(dh * xn, axis=0, keepdims=True)

        @pl.when(i == 0)
        def _():
            dg_ref[...] = part

        @pl.when(i > 0)
        def _():
            dg_ref[...] += part

    blk = pl.BlockSpec((tm, D), lambda i: (i, 0))
    vec = pl.BlockSpec((1, D), lambda i: (0, 0))
    return pl.pallas_call(
        body, name=name, grid=(s // tm,),
        in_specs=[pl.BlockSpec((tm, k), lambda i: (i, 0)),
                  pl.BlockSpec((None, D, k), lambda i: (l, 0, 0)), blk, vec, blk, ANY],
        out_specs=[blk, blk, vec],
        out_shape=[jax.ShapeDtypeStruct((s, D), F32), jax.ShapeDtypeStruct((s, D), BF16),
                   jax.ShapeDtypeStruct((1, D), F32)],
        compiler_params=_cp("arbitrary"),
    )(gy, w, x, g, dres, dep)


def _mm_tn(name, a, b, tma, tnb, relu2=False):
    s, m = a.shape
    n = b.shape[1]

    def body(a_ref, b_ref, o_ref):
        av = _relu2(a_ref[...]) if relu2 else a_ref[...]
        o_ref[...] = lax.dot_general(av, b_ref[...], (((0,), (0,)), ((), ())),
                                     preferred_element_type=F32).astype(BF16)

    return pl.pallas_call(
        body, name=name, grid=(m // tma, n // tnb),
        in_specs=[pl.BlockSpec((s, tma), lambda i, j: (0, i), pipeline_mode=pl.Buffered(1) if m == tma else None),
                  pl.BlockSpec((s, tnb), lambda i, j: (0, j))],
        out_specs=pl.BlockSpec((tma, tnb), lambda i, j: (i, j)),
        out_shape=jax.ShapeDtypeStruct((m, n), BF16),
        compiler_params=_cp("parallel", "parallel"),
    )(a, b)


def _adamw_math(gv, wv, mv, vv):
    mn = ADAM_B1 * mv + (1.0 - ADAM_B1) * gv
    vn = ADAM_B2 * vv + (1.0 - ADAM_B2) * jnp.square(gv)
    m_hat = mn / (1.0 - ADAM_B1 ** ADAM_STEP)
    v_hat = vn / (1.0 - ADAM_B2 ** ADAM_STEP)
    return gv, -ADAM_LR * (m_hat / (jnp.sqrt(v_hat) + ADAM_EPS) + ADAM_WD * wv), mn, vn


def _adamw(name, g, w, m, v):
    r, c = g.shape
    tm = 256 if r % 256 == 0 else r

    def body(g_ref, w_ref, m_ref, v_ref, go_ref, d_ref, mo_ref, vo_ref):
        go_ref[...], d_ref[...], mo_ref[...], vo_ref[...] = _adamw_math(g_ref[...], w_ref[...], m_ref[...], v_ref[...])

    blk = pl.BlockSpec((tm, c), lambda i: (i, 0))
    return pl.pallas_call(
        body, name=name, grid=(r // tm,),
        in_specs=[blk] * 4, out_specs=[blk] * 4,
        out_shape=[jax.ShapeDtypeStruct((r, c), F32)] * 4,
        compiler_params=_cp("parallel"),
    )(g, w, m, v)


def _place():
    x, y, c = lax.axis_index("x"), lax.axis_index("y"), lax.axis_index("c")
    chips = [(1 - x, y), (x, 1 - y), (1 - x, 1 - y)]
    return x, y, c, chips


BLOCK_AXIS = (2, 1, 2, 1)
LARGE_DIMS = ((D, DIN), (D, D), (D, DFF), (DFF, D))


def _full_shape(t, layers, dtype):
    r, c = LARGE_DIMS[t]
    return jax.ShapeDtypeStruct((layers, r, c), dtype)


def _cast_into_full(name, t, shard, b1, dep):
    _, r, c = shard.shape
    tm = min(512, r)
    if BLOCK_AXIS[t] == 1:
        out_spec = pl.BlockSpec((None, tm, c), lambda l, i, br: (l, br[0] * (r // tm) + i, 0))
    else:
        out_spec = pl.BlockSpec((None, tm, c), lambda l, i, br: (l, i, br[0]))

    def body(b_ref, x_ref, dep_ref, o_ref):
        del b_ref, dep_ref
        o_ref[...] = x_ref[...].astype(BF16)

    return pl.pallas_call(
        body, name=name,
        grid_spec=pltpu.PrefetchScalarGridSpec(
            num_scalar_prefetch=1, grid=(DEPTH, r // tm),
            in_specs=[pl.BlockSpec((None, tm, c), lambda l, i, br: (l, i, 0)), ANY],
            out_specs=out_spec),
        out_shape=_full_shape(t, DEPTH, BF16),
        compiler_params=_cp("parallel", "parallel"),
    )(b1, shard, dep)


HBM = pl.BlockSpec(memory_space=pltpu.HBM)
SEM = pl.BlockSpec(memory_space=pltpu.SEMAPHORE)
DATAFLOW = pltpu.SideEffectType.DATAFLOW_SIDE_EFFECTING


def _half(ref, l, t, b, c):
    r, cols = LARGE_DIMS[t]
    if BLOCK_AXIS[t] == 1:
        n = r // 8
        return ref.at[l, pl.ds(pl.multiple_of(b * (2 * n) + c * n, 16), n), :]
    n, w = r // 2, cols // 4
    return ref.at[l, pl.ds(pl.multiple_of(c * n, 16), n), pl.ds(pl.multiple_of(b * w, 128), w)]


def _gather_start(name, layers, ts, fulls, both=False):
    n = len(ts)

    def body(*refs):
        f_refs, sems = refs[n:2 * n], refs[2 * n:2 * n + 2 * len(layers)]
        x, y, c, chips = _place()
        for i, l in enumerate(layers):
            for k, t in enumerate(ts):
                own = _half(f_refs[k], l, t, 2 * x + y, c)
                for j, (cx, cy) in enumerate(chips):
                    pltpu.make_async_remote_copy(src_ref=own, dst_ref=own, send_sem=sems[2 * i].at[3 * t + j],
                                                 recv_sem=sems[2 * i + 1].at[3 * t + j], device_id=(cx, cy, c),
                                                 device_id_type=MESH).start()
                    if both:
                        pltpu.make_async_remote_copy(src_ref=own, dst_ref=own,
                                                     send_sem=sems[2 * i].at[12 + 3 * t + j],
                                                     recv_sem=sems[2 * i + 1].at[12 + 3 * t + j],
                                                     device_id=(cx, cy, 1 - c), device_id_type=MESH).start()
        refs[-1][...] = jnp.zeros((8, 128), F32)

    outs = pl.pallas_call(
        body, name=name,
        in_specs=[HBM] * n,
        out_specs=[HBM] * n + [SEM] * (2 * len(layers)) + [pl.BlockSpec(memory_space=pltpu.VMEM)],
        out_shape=[pltpu.HBM(f.shape, f.dtype) for f in fulls]
        + [pltpu.SemaphoreType.DMA((24,))] * (2 * len(layers)) + [jax.ShapeDtypeStruct((8, 128), F32)],
        input_output_aliases={k: k for k in range(n)},
        compiler_params=pltpu.CompilerParams(has_side_effects=DATAFLOW),
    )(*[pltpu.with_memory_space_constraint(f, pltpu.HBM) for f in fulls])
    return outs[0:n], {l: (outs[n + 2 * i], outs[n + 1 + 2 * i]) for i, l in enumerate(layers)}, outs[-1]


def _gather_wait(name, l, ts, fulls, sems, after, both=False):
    def body(*refs):
        send_sems, recv_sems, f_refs = refs[4], refs[5], refs[7:11]
        x, y, c, chips = _place()
        for t in ts:
            own = _half(f_refs[t], l, t, 2 * x + y, c)
            for j, (cx, cy) in enumerate(chips):
                landed = _half(f_refs[t], l, t, 2 * cx + cy, c)
                pltpu.make_async_remote_copy(src_ref=own, dst_ref=landed, send_sem=send_sems.at[3 * t + j],
                                             recv_sem=recv_sems.at[3 * t + j], device_id=(cx, cy, c),
                                             device_id_type=MESH).wait()
                if both:
                    crossed = _half(f_refs[t], l, t, 2 * cx + cy, 1 - c)
                    pltpu.make_async_remote_copy(src_ref=own, dst_ref=crossed, send_sem=send_sems.at[12 + 3 * t + j],
                                                 recv_sem=recv_sems.at[12 + 3 * t + j], device_id=(cx, cy, 1 - c),
                                                 device_id_type=MESH).wait()

    return pl.pallas_call(
        body, name=name,
        in_specs=[HBM] * 4 + [SEM, SEM, ANY], out_specs=[HBM] * 4,
        out_shape=[pltpu.HBM(s.shape, s.dtype) for s in (_full_shape(t, DEPTH, BF16) for t in range(4))],
        input_output_aliases={t: t for t in range(4)},
        compiler_params=pltpu.CompilerParams(has_side_effects=DATAFLOW),
    )(*fulls, sems[0], sems[1], after)


def _pass_on(name, l, ts, fulls):
    def body(*refs):
        f_refs, send_sems, recv_sems = refs[4:8], refs[8], refs[9]
        x, y, c, chips = _place()

        def copy(t, j, half):
            cx, cy = chips[j]
            part = _half(f_refs[t], l, t, 2 * cx + cy, half)
            return pltpu.make_async_remote_copy(src_ref=part, dst_ref=part, send_sem=send_sems.at[3 * t + j],
                                                recv_sem=recv_sems.at[3 * t + j], device_id=(x, y, 1 - c),
                                                device_id_type=MESH)

        for t in ts:
            for j in range(3):
                copy(t, j, c).start()
        for t in ts:
            for j in range(3):
                copy(t, j, 1 - c).wait_recv()
                copy(t, j, c).wait_send()

    return pl.pallas_call(
        body, name=name,
        in_specs=[ANY] * 4, out_specs=[ANY] * 4,
        out_shape=[_full_shape(t, DEPTH, BF16) for t in range(4)],
        input_output_aliases={t: t for t in range(4)},
        scratch_shapes=[pltpu.SemaphoreType.DMA((12,)), pltpu.SemaphoreType.DMA((12,))],
    )(*fulls)


def _block2d(ref, t, b):
    r, cols = LARGE_DIMS[t]
    if BLOCK_AXIS[t] == 1:
        return ref.at[pl.ds(pl.multiple_of(b * (r // 4), 16), r // 4), :]
    return ref.at[:, pl.ds(pl.multiple_of(b * (cols // 4), 128), cols // 4)]


def _block_dims(t):
    r, cols = LARGE_DIMS[t]
    return (r // 4, cols) if BLOCK_AXIS[t] == 1 else (r, cols // 4)


def _reduce_copies(ts, g_refs, r_refs, send_sems, recv_sems):
    _, _, c, chips = _place()
    return [pltpu.make_async_remote_copy(src_ref=_block2d(g_refs[i], t, 2 * cx + cy), dst_ref=r_refs[i].at[j],
                                         send_sem=send_sems.at[3 * i + j], recv_sem=recv_sems.at[3 * i + j],
                                         device_id=(cx, cy, c), device_id_type=MESH)
            for i, t in enumerate(ts) for j, (cx, cy) in enumerate(chips)]


def _reduce_start(name, ts, grads):
    n = len(ts)

    def body(*refs):
        for cp in _reduce_copies(ts, refs[n:2 * n], refs[2 * n:3 * n], refs[3 * n], refs[3 * n + 1]):
            cp.start()
        refs[3 * n + 2][...] = jnp.zeros((8, 128), F32)

    outs = pl.pallas_call(
        body, name=name,
        in_specs=[HBM] * n,
        out_specs=[HBM] * (2 * n) + [SEM, SEM, pl.BlockSpec(memory_space=pltpu.VMEM)],
        out_shape=[pltpu.HBM(g.shape, BF16) for g in grads]
        + [pltpu.HBM((3,) + _block_dims(t), BF16) for t in ts]
        + [pltpu.SemaphoreType.DMA((3 * n,)), pltpu.SemaphoreType.DMA((3 * n,)), jax.ShapeDtypeStruct((8, 128), F32)],
        input_output_aliases={i: i for i in range(n)},
        compiler_params=pltpu.CompilerParams(has_side_effects=DATAFLOW),
    )(*[pltpu.with_memory_space_constraint(g, pltpu.HBM) for g in grads])
    return outs[0:n], outs[n:2 * n], (outs[2 * n], outs[2 * n + 1]), outs[2 * n + 2]


def _reduce_wait(name, ts, grads, landing, sems, afters):
    n = len(ts)
    first_out = 2 * n + 2 + len(afters)

    def body(*refs):
        for cp in _reduce_copies(ts, refs[first_out:first_out + n], refs[first_out + n:first_out + 2 * n],
                                 refs[2 * n], refs[2 * n + 1]):
            cp.wait()

    outs = pl.pallas_call(
        body, name=name,
        in_specs=[HBM] * (2 * n) + [SEM, SEM] + [ANY] * len(afters), out_specs=[HBM] * (2 * n),
        out_shape=[pltpu.HBM(g.shape, BF16) for g in grads] + [pltpu.HBM(r.shape, BF16) for r in landing],
        input_output_aliases={i: i for i in range(2 * n)},
        compiler_params=pltpu.CompilerParams(has_side_effects=DATAFLOW),
    )(*grads, *landing, sems[0], sems[1], *afters)
    return outs[0:n], outs[n:2 * n]


def _add4(name, t, own, landed, b1):
    rb, cb = _block_dims(t)
    tm = min(512, rb)
    if BLOCK_AXIS[t] == 1:
        own_spec = pl.BlockSpec((tm, cb), lambda i, br: (br[0] * (rb // tm) + i, 0))
    else:
        own_spec = pl.BlockSpec((tm, cb), lambda i, br: (i, br[0]))

    def body(b_ref, o_ref, r0_ref, r1_ref, r2_ref, s_ref):
        del b_ref
        s_ref[...] = ((o_ref[...].astype(F32) + r0_ref[...].astype(F32))
                      + (r1_ref[...].astype(F32) + r2_ref[...].astype(F32))).astype(BF16)

    def got(j):
        return pl.BlockSpec((None, tm, cb), lambda i, br: (j, i, 0))

    return pl.pallas_call(
        body, name=name,
        grid_spec=pltpu.PrefetchScalarGridSpec(
            num_scalar_prefetch=1, grid=(rb // tm,),
            in_specs=[own_spec, got(0), got(1), got(2)],
            out_specs=pl.BlockSpec((tm, cb), lambda i, br: (i, 0))),
        out_shape=jax.ShapeDtypeStruct((rb, cb), BF16),
        compiler_params=_cp("parallel"),
    )(b1, own, landed, landed, landed)


def _swap_sib(name, sums):
    def body(*refs):
        s_refs, t_refs, send_sems, recv_sems = refs[0:4], refs[4:8], refs[8], refs[9]
        x, y, c, _ = _place()
        cps = [pltpu.make_async_remote_copy(src_ref=s_refs[t], dst_ref=t_refs[t], send_sem=send_sems.at[t],
                                            recv_sem=recv_sems.at[t], device_id=(x, y, 1 - c), device_id_type=MESH)
               for t in range(4)]
        for cp in cps:
            cp.start()
        for cp in cps:
            cp.wait()

    return pl.pallas_call(
        body, name=name,
        in_specs=[ANY] * 4, out_specs=[ANY] * 4,
        out_shape=[jax.ShapeDtypeStruct(s.shape, BF16) for s in sums],
        scratch_shapes=[pltpu.SemaphoreType.DMA((4,)), pltpu.SemaphoreType.DMA((4,))],
    )(*sums)


def _adamw_pair(name, l, s_own, s_sib, w, m, v, outs):
    rb, cb = s_own.shape
    tm = min(512, rb)

    def body(a_ref, b_ref, w_ref, m_ref, v_ref, g0, d0, m0, v0, go_ref, d_ref, mo_ref, vo_ref):
        del g0, d0, m0, v0
        gv = a_ref[...].astype(F32) + b_ref[...].astype(F32)
        go_ref[...], d_ref[...], mo_ref[...], vo_ref[...] = _adamw_math(gv, w_ref[...], m_ref[...], v_ref[...])

    part = pl.BlockSpec((tm, cb), lambda i: (i, 0))
    layer = pl.BlockSpec((None, tm, cb), lambda i: (l, i, 0))
    return pl.pallas_call(
        body, name=name, grid=(rb // tm,),
        in_specs=[part, part, layer, layer, layer] + [ANY] * 4,
        out_specs=[layer] * 4,
        out_shape=[jax.ShapeDtypeStruct((DEPTH, rb, cb), F32)] * 4,
        input_output_aliases={5 + i: i for i in range(4)},
        compiler_params=_cp("parallel"),
    )(s_own, s_sib, w, m, v, *outs)


def _all_gather8(name, v, dep):
    m_per, n = v.shape

    def body(v_ref, dep_ref, out_ref, send_sems, recv_sems, local_sem):
        del dep_ref
        x, y, c, chips = _place()
        me, sib = (x, y, c), (x, y, 1 - c)

        def rows(px, py, pc):
            return out_ref.at[pl.ds((4 * px + 2 * py + pc) * m_per, m_per), :]

        def copy(k, block, to, src=None):
            return pltpu.make_async_remote_copy(
                src_ref=rows(*block) if src is None else src, dst_ref=rows(*block),
                send_sem=send_sems.at[k], recv_sem=recv_sems.at[k], device_id=to, device_id_type=MESH)

        mine = pltpu.make_async_copy(v_ref, rows(*me), local_sem)
        mine.start()
        first = [copy(0, me, sib, src=v_ref)]
        first += [copy(1 + j, me, (*chip, c), src=v_ref) for j, chip in enumerate(chips)]
        for cp in first:
            cp.start()
        passed = [copy(4 + j, (*chip, c), sib) for j, chip in enumerate(chips)]
        for j, chip in enumerate(chips):
            copy(1 + j, (*chip, c), me).wait_recv()
            passed[j].start()
        copy(0, sib, me).wait_recv()
        for j, chip in enumerate(chips):
            copy(4 + j, (*chip, 1 - c), me).wait_recv()
        for cp in first + passed:
            cp.wait_send()
        mine.wait()

    return pl.pallas_call(
        body, name=name,
        out_shape=jax.ShapeDtypeStruct((8 * m_per, n), v.dtype),
        in_specs=[pl.BlockSpec(memory_space=pltpu.VMEM), ANY],
        out_specs=pl.BlockSpec(memory_space=pltpu.VMEM),
        scratch_shapes=[pltpu.SemaphoreType.DMA((7,)), pltpu.SemaphoreType.DMA((7,)), pltpu.SemaphoreType.DMA],
    )(v, dep)


def _sum8(name, g):
    def body(g_ref, o_ref):
        acc = g_ref[0]
        for d in range(1, 8):
            acc = acc + g_ref[d]
        o_ref[...] = acc

    return pl.pallas_call(body, name=name, out_shape=jax.ShapeDtypeStruct(g.shape[1:], F32))(g)


def _pack(parts):
    flat = []
    for a in parts:
        a = a.reshape(-1)
        flat.append(jnp.pad(a, (0, (-a.shape[0]) % 128)))
    cat = jnp.concatenate(flat)
    cat = jnp.pad(cat, (0, (-cat.shape[0]) % 1024))
    return cat.reshape(-1, 128)


def _unpack(packed, shapes):
    flat = packed.reshape(-1)
    out, at = [], 0
    for shp in shapes:
        n = 1
        for d in shp:
            n *= d
        out.append(flat[at:at + n].reshape(shp))
        at += n + (-n) % 128
    return out


def _local_step(x, target, layer_weights, on_grads, small):
    qg_all = jnp.tile(small["q_norm_g"], (1, 8))
    kg_all = jnp.tile(small["k_norm_g"], (1, 8))
    bias_all = _bias_layout(_bias_expand("bias_expand", jnp.pad(small["rel_bias"], ((0, 0), (0, 0), (0, NIDX - 257)))))
    same_group = jnp.eye(4, dtype=F32)[None, :, None, :, None]
    pwbd_all = (small["pool_w"][:, :, :, None, :] * same_group).reshape(DEPTH, PWD, PWD)
    saved = []
    xin = x
    h = _rmsnorm("norm_first", x, small["norm1_g"][0:1])
    for l in range(DEPTH):
        w_in = layer_weights(l, (0,), xin)[0]
        qg, kg = qg_all[l:l + 1], kg_all[l:l + 1]
        cw, pwbd, ps = small["conv_w"][l], pwbd_all[l], small["pool_scale"][l:l + 1]
        p = _mm_nn(f"proj_in_{l}", h, w_in, l, F32)
        q, qt, kp, kt, vp, vt = _qkv(f"qkv_{l}", p, qg, kg)
        o, lse = _attn_fwd(f"attn_fwd_{l}", kp, qt, vt, bias_all, l)
        w_in, w_out, w_1, w_2 = layer_weights(l, (1, 2, 3), o)
        mix = _convpool_fwd(f"convpool_fwd_{l}", p, o, cw, pwbd, ps)
        x1, h2 = _mm_res_norm(f"proj_out_{l}", mix, w_out, l, xin, small["norm2_g"][l:l + 1])
        saved.append(dict(xin=xin, h=h, p=p, q=q, qt=qt, kp=kp, kt=kt, vp=vp, mix=mix, x1=x1, h2=h2, lse=lse,
                          qg=qg, kg=kg, cw=cw, pwbd=pwbd, ps=ps))
        if l + 1 < DEPTH:
            saved[l]["a"], xin, h = _mlp_fwd(f"mlp_{l}", h2, w_1, w_2, l, x1, small["norm1_g"][l + 1:l + 2])
        else:
            saved[l]["a"], dx, dxb, loss = _mlp_fwd(f"mlp_{l}", h2, w_1, w_2, l, x1, target)

    raw = {k: [None] * DEPTH for k in ("dg1", "dqg", "dkg", "dw0", "dw1", "dw2", "dpw", "dps", "dg2")}
    db_all = lax.empty((DEPTH, 4, KB, 128), F32)
    for l in reversed(range(DEPTH)):
        sv = saved[l]
        da = _mm_nt_relu(f"mlp2_bwd_{l}", dxb, w_2, l, sv["a"])
        g_2 = _mm_tn(f"mlp2_wgrad_{l}", sv["a"], dxb, 512, 1024, relu2=True)
        g_1 = _mm_tn(f"mlp1_wgrad_{l}", sv["h2"], da, 1024, 512)
        dep = on_grads(l, (2, 3), (g_1, g_2))
        dx1, dx1b, dg2 = _mm_nt_normbwd(f"mlp1_bwd_{l}", da, w_1, l, sv["x1"], small["norm2_g"][l:l + 1], dx, dep)
        do, dot, dmix, dl = _proj_out_bwd(f"proj_out_bwd_{l}", dx1b, w_out, l, sv["mix"])
        g_out = _mm_tn(f"proj_out_wgrad_{l}", sv["mix"], dx1b, 512, 1024)
        dcp, dw0, dw1, dw2, dps, dpw = _convpool_bwd(f"convpool_bwd_{l}", sv["p"], dmix, sv["cw"], sv["pwbd"], sv["ps"])
        dq, dkp, dvp, db_all = _attn_bwd(f"attn_bwd_{l}", sv["q"], sv["qt"], sv["kp"], sv["kt"], sv["vp"], bias_all, l,
                                     do, dot, sv["lse"], _rowsum_layout(dl, x.shape[0] // UNIT), db_all)
        dp, dqg, dkg = _qkv_bwd(f"qkv_bwd_{l}", sv["p"], dq, dkp, dvp, dcp, sv["qg"], sv["kg"])
        g_in = _mm_tn(f"proj_in_wgrad_{l}", sv["h"], dp, 1024, 1280)
        dep = on_grads(l, (0, 1), (g_in, g_out))
        dx, dxb, dg1 = _mm_nt_normbwd(f"proj_in_bwd_{l}", dp, w_in, l, sv["xin"], small["norm1_g"][l:l + 1], dx1, dep)
        for k, val in dict(dg1=dg1, dqg=dqg, dkg=dkg, dw0=dw0, dw1=dw1, dw2=dw2, dpw=dpw, dps=dps, dg2=dg2).items():
            raw[k][l] = val
    cat = {k: jnp.concatenate(v, axis=0) for k, v in raw.items() if k != "dpw"}
    drb = _bias_reduce("bias_reduce", _bias_unlayout(db_all))
    dpw = jnp.stack(raw["dpw"])
    gsmall = {
        "norm1_g": cat["dg1"], "q_norm_g": cat["dqg"][:, :HD], "k_norm_g": cat["dkg"][:, :HD],
        "rel_bias": drb[:, :, :257],
        "conv_w": jnp.stack([cat["dw0"], cat["dw1"], cat["dw2"]], axis=1),
        "pool_w": jnp.stack([dpw[:, g * 64:(g + 1) * 64, g * 64:(g + 1) * 64] for g in range(4)], axis=1),
        "pool_scale": cat["dps"], "norm2_g": cat["dg2"],
    }
    return loss, dx, gsmall


SMALL = ("norm1_g", "q_norm_g", "k_norm_g", "rel_bias", "conv_w", "pool_w", "pool_scale", "norm2_g")
LARGE = ("w_in", "w_out", "w_mlp1", "w_mlp2")


def kernel(x, norm1_g, w_in, q_norm_g, k_norm_g, rel_bias, conv_w, pool_w, pool_scale, w_out, norm2_g, w_mlp1, w_mlp2, loss_target, m_norm1_g, m_w_in, m_q_norm_g, m_k_norm_g, m_rel_bias, m_conv_w, m_pool_w, m_pool_scale, m_w_out, m_norm2_g, m_w_mlp1, m_w_mlp2, v_norm1_g, v_w_in, v_q_norm_g, v_k_norm_g, v_rel_bias, v_conv_w, v_pool_w, v_pool_scale, v_w_out, v_norm2_g, v_w_mlp1, v_w_mlp2):
    w = dict(norm1_g=norm1_g, w_in=w_in, q_norm_g=q_norm_g, k_norm_g=k_norm_g, rel_bias=rel_bias, conv_w=conv_w,
             pool_w=pool_w, pool_scale=pool_scale, w_out=w_out, norm2_g=norm2_g, w_mlp1=w_mlp1, w_mlp2=w_mlp2)
    m = dict(norm1_g=m_norm1_g, w_in=m_w_in, q_norm_g=m_q_norm_g, k_norm_g=m_k_norm_g, rel_bias=m_rel_bias,
             conv_w=m_conv_w, pool_w=m_pool_w, pool_scale=m_pool_scale, w_out=m_w_out, norm2_g=m_norm2_g,
             w_mlp1=m_w_mlp1, w_mlp2=m_w_mlp2)
    v = dict(norm1_g=v_norm1_g, w_in=v_w_in, q_norm_g=v_q_norm_g, k_norm_g=v_k_norm_g, rel_bias=v_rel_bias,
             conv_w=v_conv_w, pool_w=v_pool_w, pool_scale=v_pool_scale, w_out=v_w_out, norm2_g=v_norm2_g,
             w_mlp1=v_w_mlp1, w_mlp2=v_w_mlp2)
    ax, ay, ac = lax.axis_index("x"), lax.axis_index("y"), lax.axis_index("c")
    b1 = jnp.reshape(2 * ax + ay, (1,)).astype(jnp.int32)

    cw_rows = _all_gather8("gather_conv_w", jnp.pad(conv_w.reshape(DEPTH * 3, 64), ((0, 4), (0, 64))), b1)
    cw_chips = [cw_rows[(4 * cx + 2 * cy) * 16:(4 * cx + 2 * cy) * 16 + 12, :64] for cx in range(2) for cy in range(2)]
    small = {n: w[n] for n in SMALL}
    small["conv_w"] = jnp.concatenate(cw_chips, axis=1).reshape(DEPTH, 3, CW)

    (w_in_full,), in_sems, in_token = _gather_start(
        "gather_start_in", (0,), (0,), [_cast_into_full("cast_w_in", 0, w["w_in"], b1, cw_rows)])
    others, first_sems, first_token = _gather_start(
        "gather_start_first", (0,), (1, 2, 3),
        [_cast_into_full(f"cast_{LARGE[t]}", t, w[LARGE[t]], b1, in_token) for t in (1, 2, 3)])
    held = [[w_in_full] + list(others)]
    sems = {(0, 0): in_sems[0], (0, 1): first_sems[0]}

    def layer_weights(l, ts, after):
        tag = f"{l}_{ts[0]}"
        first_in = l == 0 and ts == (0,)
        after = first_token if first_in else after
        arrived = _gather_wait(f"gather_wait_{tag}", l, ts, held[0], sems[l, ts[0] if l == 0 else 0], after,
                               both=l > 0)
        if first_in:
            arrived, rest_sems, _ = _gather_start("gather_start_rest", tuple(range(1, DEPTH)), (0, 1, 2, 3),
                                                  arrived, both=True)
            sems.update({(k, 0): v for k, v in rest_sems.items()})
        held[0] = _pass_on(f"pass_on_{tag}", l, ts, arrived) if l == 0 else arrived
        return held[0]

    flights = {}

    def await_flight(l, ts, afters):
        g, landing, sm, _ = flights[l, ts]
        flights[l, ts] = _reduce_wait(f"reduce_wait_{l}_{ts[0]}", ts, g, landing, sm, afters)

    def on_grads(l, ts, grads):
        if ts == (0, 1) and l + 1 < DEPTH:
            await_flight(l + 1, (2, 3), [grads[0]])
            await_flight(l + 1, (0, 1), [grads[0]])
        flights[l, ts] = _reduce_start(f"reduce_start_{l}_{ts[0]}", ts, grads)
        return flights[l, ts][3]

    loss_part, grad_x, gsmall = _local_step(x[0], loss_target[0], layer_weights, on_grads, small)
    loss = lax.psum(loss_part[0, 0], ("x", "y", "c"))
    order = [n for n in SMALL]
    packed = _pack([gsmall[n] for n in order])

    out = {n: [lax.empty(w[n].shape, F32) for _ in range(4)] for n in LARGE}
    for l in reversed(range(DEPTH)):
        if l == 0:
            afters = [grad_x, packed] + [out[n][0] for n in LARGE]
            await_flight(0, (2, 3), afters)
            await_flight(0, (0, 1), afters)
        sums = [None] * 4
        for ts in ((0, 1), (2, 3)):
            g, landing = flights[l, ts]
            for i, t in enumerate(ts):
                sums[t] = _add4(f"add4_{LARGE[t]}_{l}", t, g[i], landing[i], b1)
        theirs = _swap_sib(f"swap_sib_{l}", sums)
        for t, n in enumerate(LARGE):
            out[n] = _adamw_pair(f"adamw_{n}_{l}", l, sums[t], theirs[t], w[n], m[n], v[n], out[n])

    rows = packed.shape[0]
    summed = _sum8("sum_small", _all_gather8("gather_small", packed, out[LARGE[0]][0]).reshape(8, rows, 128))
    gfull = dict(zip(order, _unpack(summed, [gsmall[n].shape for n in order])))
    gfull["conv_w"] = lax.dynamic_slice_in_dim(gfull["conv_w"], (2 * ax + ay) * 64, 64, axis=2)
    res = _adamw("adamw_small", _pack([gfull[n] for n in order]), _pack([w[n] for n in order]),
                 _pack([m[n] for n in order]), _pack([v[n] for n in order]))
    for n, parts in zip(order, zip(*[_unpack(r, [w[k].shape for k in order]) for r in res])):
        out[n] = list(parts)

    names = ("norm1_g", "w_in", "q_norm_g", "k_norm_g", "rel_bias", "conv_w", "pool_w", "pool_scale", "w_out",
             "norm2_g", "w_mlp1", "w_mlp2")
    flat = [loss, grad_x[None]]
    for i in range(4):
        flat += [out[n][i] for n in names]
    return tuple(flat)
```

```python
import jax
import jax.numpy as jnp
from jax import lax
from jax.experimental import pallas as pl
from jax.experimental.pallas import tpu as pltpu

F32 = jnp.float32
BF16 = jnp.bfloat16

D = 1024
DEPTH = 4
CH = 64
NPREV = 8
KB = (NPREV + 1) * CH
PADR = NPREV * CH
HD = 64
AW = 512
CW = 256
PWD = 256
DIN = 3 * AW + 3 * CW + PWD
DFF = 4 * D
NIDX = 384
EPS = 1e-6
NEG_INF = -1e30

ADAM_LR = 0.001
ADAM_B1 = 0.9
ADAM_B2 = 0.999
ADAM_EPS = 1e-08
ADAM_WD = 0.01
ADAM_STEP = 10

VMEM_LIMIT = 52 * 1024 * 1024
MM_ROWS = 512


def _mm_rows(k, n):
    return 2 * MM_ROWS if k + n <= 2048 else MM_ROWS


MESH = pl.DeviceIdType.MESH
ANY = pl.BlockSpec(memory_space=pl.ANY)


def _cp(*sem):
    return pltpu.CompilerParams(dimension_semantics=sem, vmem_limit_bytes=VMEM_LIMIT)


def _inv_rms(x):
    return lax.rsqrt(jnp.mean(x * x, axis=-1, keepdims=True) + EPS)


def _head_mean_matrix():
    r = lax.broadcasted_iota(jnp.int32, (AW, AW), 0) // HD
    c = lax.broadcasted_iota(jnp.int32, (AW, AW), 1) // HD
    return jnp.where(r == c, 1.0 / HD, 0.0).astype(BF16)


def _two_pass_dot(x, m):
    hi = x.astype(BF16)
    lo = (x - hi.astype(F32)).astype(BF16)
    return (jnp.dot(hi, m, preferred_element_type=F32)
            + jnp.dot(lo, m, preferred_element_type=F32))


def _head_mean(x, hm):
    return _two_pass_dot(x, hm)


def _rmsnorm(name, x, g):
    s = x.shape[0]
    tm = 512

    def body(x_ref, g_ref, h_ref):
        xv = x_ref[...]
        h_ref[...] = (xv * _inv_rms(xv) * g_ref[...]).astype(BF16)

    return pl.pallas_call(
        body, name=name, grid=(s // tm,),
        in_specs=[pl.BlockSpec((tm, D), lambda i: (i, 0)), pl.BlockSpec((1, D), lambda i: (0, 0))],
        out_specs=pl.BlockSpec((tm, D), lambda i: (i, 0)),
        out_shape=jax.ShapeDtypeStruct((s, D), BF16),
        compiler_params=_cp("parallel"),
    )(x, g)


def _relu2(a):
    r = jnp.maximum(a, jnp.zeros_like(a))
    return r * r


def _mm_nn(name, a, w, l, out_dtype):
    s, k = a.shape
    n = w.shape[2]
    tm = _mm_rows(k, n)

    def body(a_ref, w_ref, o_ref):
        o_ref[...] = jnp.dot(a_ref[...], w_ref[...], preferred_element_type=F32).astype(o_ref.dtype)

    return pl.pallas_call(
        body, name=name, grid=(s // tm,),
        in_specs=[pl.BlockSpec((tm, k), lambda i: (i, 0)),
                  pl.BlockSpec((None, k, n), lambda i: (l, 0, 0))],
        out_specs=pl.BlockSpec((tm, n), lambda i: (i, 0)),
        out_shape=jax.ShapeDtypeStruct((s, n), out_dtype),
        compiler_params=_cp("parallel"),
    )(a, w)


def _mm_res_norm(name, a, w, l, res, g):
    s, k = a.shape
    tm = _mm_rows(k, D)

    def body(a_ref, w_ref, r_ref, g_ref, x_ref, h_ref):
        acc = r_ref[...] + jnp.dot(a_ref[...], w_ref[...], preferred_element_type=F32)
        x_ref[...] = acc
        h_ref[...] = (acc * _inv_rms(acc) * g_ref[...]).astype(BF16)

    return pl.pallas_call(
        body, name=name, grid=(s // tm,),
        in_specs=[pl.BlockSpec((tm, k), lambda i: (i, 0)),
                  pl.BlockSpec((None, k, D), lambda i: (l, 0, 0)),
                  pl.BlockSpec((tm, D), lambda i: (i, 0)),
                  pl.BlockSpec((1, D), lambda i: (0, 0))],
        out_specs=[pl.BlockSpec((tm, D), lambda i: (i, 0))] * 2,
        out_shape=[jax.ShapeDtypeStruct((s, D), F32), jax.ShapeDtypeStruct((s, D), BF16)],
        compiler_params=_cp("parallel"),
    )(a, w, res, g)


def _mlp_fwd(name, h2, w1, w2, l, res, last):
    s = h2.shape[0]
    tm = 256
    final = last.shape[0] == s

    def body(h_ref, w1_ref, w2_ref, r_ref, last_ref, a_ref, first_ref, second_ref, *loss_ref):
        a = jnp.dot(h_ref[...], w1_ref[...], preferred_element_type=F32).astype(BF16)
        a_ref[...] = a
        acc = r_ref[...] + jnp.dot(_relu2(a), w2_ref[...], preferred_element_type=F32)
        if not final:
            first_ref[...] = acc
            second_ref[...] = (acc * _inv_rms(acc) * last_ref[...]).astype(BF16)
            return
        e = acc - last_ref[...]
        dy = e * (1.0 / D)
        first_ref[...] = dy
        second_ref[...] = dy.astype(BF16)
        part = 0.5 * jnp.sum(jnp.mean(e * e, axis=-1, keepdims=True), axis=0, keepdims=True)
        i = pl.program_id(0)

        @pl.when(i == 0)
        def _():
            loss_ref[0][...] = part

        @pl.when(i > 0)
        def _():
            loss_ref[0][...] += part

    once = pl.Buffered(1)
    rows = pl.BlockSpec((tm, D), lambda i: (i, 0))
    one = pl.BlockSpec((1, 1), lambda i: (0, 0))
    return pl.pallas_call(
        body, name=name, grid=(s // tm,),
        in_specs=[rows,
                  pl.BlockSpec((None, D, DFF), lambda i: (l, 0, 0), pipeline_mode=once),
                  pl.BlockSpec((None, DFF, D), lambda i: (l, 0, 0), pipeline_mode=once),
                  rows, rows if final else pl.BlockSpec((1, D), lambda i: (0, 0))],
        out_specs=[pl.BlockSpec((tm, DFF), lambda i: (i, 0)), rows, rows] + ([one] if final else []),
        out_shape=[jax.ShapeDtypeStruct((s, DFF), BF16), jax.ShapeDtypeStruct((s, D), F32),
                   jax.ShapeDtypeStruct((s, D), BF16)] + ([jax.ShapeDtypeStruct((1, 1), F32)] if final else []),
        compiler_params=_cp("arbitrary" if final else "parallel"),
    )(h2, w1, w2, res, last)


def _qkv(name, p, qg, kg):
    s = p.shape[0]
    tm = PADR
    nb = s // tm

    def body(pq_ref, pk_ref, pv_ref, qg_ref, kg_ref, q_ref, qt_ref, k_ref, kt_ref, v_ref, vt_ref):
        t = pl.program_id(0)
        hm = _head_mean_matrix()

        def nrm(x, g):
            return x * lax.rsqrt(_head_mean(x * x, hm) + EPS) * g

        first = t == 0
        qq = nrm(pq_ref[...], qg_ref[...]) * 0.125
        kk = jnp.where(first, 0.0, nrm(pk_ref[...], kg_ref[...]))
        vv = jnp.where(first, 0.0, pv_ref[...])
        q_ref[...] = qq.astype(BF16)
        qt_ref[...] = qq.T.astype(BF16)
        k_ref[...] = kk.astype(BF16)
        kt_ref[...] = kk.T.astype(BF16)
        v_ref[...] = vv.astype(BF16)
        vt_ref[...] = vv.T.astype(BF16)

    def src(col):
        return pl.BlockSpec((tm, AW), lambda t: (jnp.maximum(t - 1, 0), col))

    gspec = pl.BlockSpec((1, AW), lambda t: (0, 0))
    rows = pl.BlockSpec((tm, AW), lambda t: (t, 0))
    cols = pl.BlockSpec((AW, tm), lambda t: (0, t))
    return pl.pallas_call(
        body, name=name, grid=(nb + 1,),
        in_specs=[src(0), src(1), src(2), gspec, gspec],
        out_specs=[pl.BlockSpec((tm, AW), lambda t: (jnp.maximum(t - 1, 0), 0)),
                   pl.BlockSpec((AW, tm), lambda t: (0, jnp.maximum(t - 1, 0))),
                   rows, cols, rows, cols],
        out_shape=[jax.ShapeDtypeStruct((s, AW), BF16), jax.ShapeDtypeStruct((AW, s), BF16),
                   jax.ShapeDtypeStruct((s + PADR, AW), BF16), jax.ShapeDtypeStruct((AW, s + PADR), BF16),
                   jax.ShapeDtypeStruct((s + PADR, AW), BF16), jax.ShapeDtypeStruct((AW, s + PADR), BF16)],
        compiler_params=_cp("arbitrary"),
    )(p, p, p, qg, kg)


NBAND = KB // CH
HIGHEST = lax.Precision.HIGHEST
NT_DIMS = (((1,), (1,)), ((), ()))


def _onehot_table(a):
    m = lax.broadcasted_iota(jnp.int32, (128, NIDX), 0)
    idx = lax.broadcasted_iota(jnp.int32, (128, NIDX), 1)
    rel = jnp.clip(KB - 1 - (CH * a + m), -128, 128) + 128
    return jnp.where(rel == idx, 1.0, 0.0).astype(F32)


def _onehot_diagonal():
    r = lax.broadcasted_iota(jnp.int32, (CH * CH, 128), 0)
    m = lax.broadcasted_iota(jnp.int32, (CH * CH, 128), 1)
    return jnp.where((r % CH) - (r // CH) + (CH - 1) == m, 1.0, 0.0).astype(F32)


def _bias_expand(name, rb):
    def body(rb_ref, o_ref):
        along = [lax.dot_general(rb_ref[...], _onehot_table(a), NT_DIMS, preferred_element_type=F32,
                                 precision=HIGHEST) for a in range(NBAND)]
        o_ref[...] = lax.dot_general(jnp.concatenate(along, axis=0), _onehot_diagonal(), NT_DIMS,
                                     preferred_element_type=F32, precision=HIGHEST)

    return pl.pallas_call(
        body, name=name, grid=(DEPTH,),
        in_specs=[pl.BlockSpec((None, 8, NIDX), lambda l: (l, 0, 0))],
        out_specs=pl.BlockSpec((None, NBAND * 8, CH * CH), lambda l: (l, 0, 0)),
        out_shape=jax.ShapeDtypeStruct((DEPTH, NBAND * 8, CH * CH), F32),
        compiler_params=_cp("parallel"),
    )(rb)


def _bias_reduce(name, db):
    def body(db_ref, o_ref):
        along = jnp.dot(db_ref[...], _onehot_diagonal(), preferred_element_type=F32, precision=HIGHEST)
        acc = jnp.zeros((8, NIDX), F32)
        for a in range(NBAND):
            acc = acc + jnp.dot(along[8 * a:8 * a + 8, :], _onehot_table(a), preferred_element_type=F32,
                                precision=HIGHEST)
        o_ref[...] = acc

    return pl.pallas_call(
        body, name=name, grid=(DEPTH,),
        in_specs=[pl.BlockSpec((None, NBAND * 8, CH * CH), lambda l: (l, 0, 0))],
        out_specs=pl.BlockSpec((None, 8, NIDX), lambda l: (l, 0, 0)),
        out_shape=jax.ShapeDtypeStruct((DEPTH, 8, NIDX), F32),
        compiler_params=_cp("parallel"),
    )(db)


def _bias_layout(flat):
    b = flat.reshape(DEPTH, NBAND, 8, CH, CH).transpose(0, 2, 1, 4, 3).reshape(DEPTH, 4, 2, KB, CH)
    pair = b.transpose(0, 1, 3, 2, 4).reshape(DEPTH, 4, KB, 128)
    first = jnp.pad(pair, ((0, 0), (0, 0), (0, CH), (0, 0)), constant_values=NEG_INF)
    second = jnp.pad(pair, ((0, 0), (0, 0), (CH, 0), (0, 0)), constant_values=NEG_INF)
    return jnp.concatenate([first, second], axis=3)


def _bias_unlayout(dbt):
    b = dbt.reshape(DEPTH, 4, NBAND, CH, 2, CH)
    return b.transpose(0, 2, 1, 4, 5, 3).reshape(DEPTH, NBAND * 8, CH * CH)


UNIT = 2 * CH
BAND2 = KB + CH


def _pair_weights(xt):
    x = xt.astype(F32)
    row = lax.broadcasted_iota(jnp.int32, (128, UNIT), 0)
    low = lax.broadcasted_iota(jnp.int32, (128, UNIT), 1) < HD
    swapped = pltpu.roll(x, HD, 1)
    same = (row < HD) == low
    first = jnp.where(same, jnp.where(low, x, swapped), 0.0)
    second = jnp.where(same, jnp.where(low, swapped, x), 0.0)
    return jnp.concatenate([first, second], axis=1).astype(BF16)


def _pair_rows(x):
    low = lax.broadcasted_iota(jnp.int32, (CH, 128), 1) < HD
    zero = jnp.zeros((CH, 128), x.dtype)
    parts = []
    for c in range(2):
        xc = x[c * CH:(c + 1) * CH, :]
        parts += [jnp.where(low, xc, zero), jnp.where(low, zero, xc)]
    return jnp.concatenate(parts, axis=0)


def _unpair(raw):
    b0, b1 = raw[:, 0:128], raw[:, 128:256]
    row = lax.broadcasted_iota(jnp.int32, (128, 128), 0)
    low = lax.broadcasted_iota(jnp.int32, (128, 128), 1) < HD
    top = jnp.where(low, b0, pltpu.roll(b1, HD, 1))
    bottom = jnp.where(low, pltpu.roll(b0, HD, 1), b1)
    return jnp.where(row < HD, top, bottom).T


def _scores_t(kb, qw, bias2, row0, padded):
    s = jnp.dot(kb, qw, preferred_element_type=F32) + bias2
    if padded:
        s = jnp.where(row0 + lax.broadcasted_iota(jnp.int32, (BAND2, 256), 0) >= PADR, s, NEG_INF)
    return s


def _unit_loops(s, unit):
    lax.fori_loop(0, PADR // UNIT, lambda u, c: unit(u, True, c), 0, unroll=4)
    lax.fori_loop(PADR // UNIT, s // UNIT, lambda u, c: unit(u, False, c), 0, unroll=7)


def _attn_fwd(name, kp, qt, vt, bias2, l):
    s = qt.shape[1]
    nu = s // UNIT

    def body(k_ref, qt_ref, vt_ref, b_ref, o_ref, lse_ref):
        def unit(u, padded, carry):
            r0 = pl.multiple_of(u * UNIT, UNIT)
            kb, vtb = k_ref[pl.ds(r0, BAND2), :], vt_ref[:, pl.ds(r0, BAND2)]
            qw = _pair_weights(qt_ref[:, pl.ds(r0, UNIT)])
            raws, stats = [], []
            for c in range(2):
                cols = slice(128 * c, 128 * (c + 1))
                sc = jnp.dot(kb, qw[:, cols], preferred_element_type=F32) + b_ref[:, cols]
                if padded:
                    sc = jnp.where(r0 + lax.broadcasted_iota(jnp.int32, (BAND2, 128), 0) >= PADR, sc, NEG_INF)
                top = jnp.max(sc, axis=0, keepdims=True)
                e = jnp.exp(sc - top)
                total = jnp.sum(e, axis=0, keepdims=True)
                raws.append(jnp.dot(vtb, e.astype(BF16), preferred_element_type=F32) * (1.0 / total))
                stats.append(top + jnp.log(total))
            o_ref[pl.ds(r0, UNIT), :] = _unpair(jnp.concatenate(raws, axis=1)).astype(BF16)
            lse_ref[u] = jnp.broadcast_to(jnp.concatenate(stats, axis=1), (8, 256))
            return carry

        _unit_loops(s, unit)

    return pl.pallas_call(
        body, name=name, grid=(AW // 128,),
        in_specs=[pl.BlockSpec((s + PADR, 128), lambda h: (0, h)),
                  pl.BlockSpec((128, s), lambda h: (h, 0)),
                  pl.BlockSpec((128, s + PADR), lambda h: (h, 0)),
                  pl.BlockSpec((None, None, BAND2, 256), lambda h: (l, h, 0, 0))],
        out_specs=[pl.BlockSpec((s, 128), lambda h: (0, h)),
                   pl.BlockSpec((None, nu, 8, 256), lambda h: (h, 0, 0, 0))],
        out_shape=[jax.ShapeDtypeStruct((s, AW), BF16), jax.ShapeDtypeStruct((4, nu, 8, 256), F32)],
        compiler_params=_cp("parallel"),
    )(kp, qt, vt, bias2)


def _attn_bwd(name, q, qt, kp, kt, vp, bias2, l, do, dot, lse, dl, db_all):
    s = q.shape[0]
    nu = s // UNIT

    def body(q_ref, qt_ref, k_ref, kt_ref, v_ref, b_ref, do_ref, dot_ref, lse_ref, dl_ref, dbin_ref,
             dq_ref, dk_ref, dvb_ref, db_ref, dv_ref):
        del dbin_ref
        dk_ref[...] = jnp.zeros_like(dk_ref)
        dv_ref[...] = jnp.zeros_like(dv_ref)
        db_ref[...] = jnp.zeros_like(db_ref)

        def unit(u, padded, carry):
            r0 = pl.multiple_of(u * UNIT, UNIT)
            rows, band = pl.ds(r0, UNIT), pl.ds(r0, BAND2)
            kb, vb = k_ref[band, :], v_ref[band, :]
            qw, dow = _pair_weights(qt_ref[:, rows]), _pair_weights(dot_ref[:, rows])
            lse, dl = lse_ref[u][0:1, :], dl_ref[u][0:1, :]
            pts, dss = [], []
            for c in range(2):
                cols = slice(128 * c, 128 * (c + 1))
                sc = jnp.dot(kb, qw[:, cols], preferred_element_type=F32) + b_ref[:, cols]
                if padded:
                    sc = jnp.where(r0 + lax.broadcasted_iota(jnp.int32, (BAND2, 128), 0) >= PADR, sc, NEG_INF)
                pt = jnp.exp(sc - lse[:, cols])
                dpt = jnp.dot(vb, dow[:, cols], preferred_element_type=F32)
                pts.append(pt.astype(BF16))
                dss.append(pt * (dpt - dl[:, cols]))
            db_ref[...] += dss[0][0:KB, :] + dss[1][CH:BAND2, :]
            dsb = jnp.concatenate([d.astype(BF16) for d in dss], axis=1)
            dq_ref[rows, :] = _unpair(jnp.dot(kt_ref[:, band], dsb, preferred_element_type=F32))
            dk_ref[band, :] += jnp.dot(dsb, _pair_rows(q_ref[rows, :]), preferred_element_type=F32)
            dv_ref[band, :] += jnp.dot(jnp.concatenate(pts, axis=1), _pair_rows(do_ref[rows, :]),
                                       preferred_element_type=F32)
            return carry

        _unit_loops(s, unit)
        dvb_ref[...] = dv_ref[...].astype(BF16)

    row_q = pl.BlockSpec((s, 128), lambda h: (0, h))
    col_q = pl.BlockSpec((128, s), lambda h: (h, 0))
    row_k = pl.BlockSpec((s + PADR, 128), lambda h: (0, h))
    col_k = pl.BlockSpec((128, s + PADR), lambda h: (h, 0))
    stat = pl.BlockSpec((None, nu, 8, 256), lambda h: (h, 0, 0, 0))
    return pl.pallas_call(
        body, name=name, grid=(AW // 128,),
        in_specs=[row_q, col_q, row_k, col_k, row_k,
                  pl.BlockSpec((None, None, BAND2, 256), lambda h: (l, h, 0, 0)), row_q, col_q, stat, stat, ANY],
        out_specs=[row_q, row_k, row_k, pl.BlockSpec((None, None, KB, 128), lambda h: (l, h, 0, 0))],
        out_shape=[jax.ShapeDtypeStruct((s, AW), F32),
                   jax.ShapeDtypeStruct((s + PADR, AW), F32),
                   jax.ShapeDtypeStruct((s + PADR, AW), BF16),
                   jax.ShapeDtypeStruct((DEPTH, 4, KB, 128), F32)],
        scratch_shapes=[pltpu.VMEM((s + PADR, 128), F32)],
        input_output_aliases={10: 3},
        compiler_params=_cp("parallel"),
    )(q, qt, kp, kt, vp, bias2, do, dot, lse, dl, db_all)


def _rowsum_layout(dl, nu):
    d = dl[:, :8].reshape(nu, 2, CH, 4, 2)
    d = d.transpose(3, 0, 1, 4, 2).reshape(4, nu, 1, 256)
    return jnp.broadcast_to(d, (4, nu, 8, 256))


def _rows_before(cur, prev, k):
    row = lax.broadcasted_iota(jnp.int32, cur.shape, 0)
    return jnp.where(row >= k, pltpu.roll(cur, k, 0), pltpu.roll(prev, k, 0))


def _rows_after(cur, nxt, k):
    n = cur.shape[0]
    row = lax.broadcasted_iota(jnp.int32, cur.shape, 0)
    return jnp.where(row < n - k, pltpu.roll(cur, n - k, 0), pltpu.roll(nxt, n - k, 0))


def _pool_window_lanes():
    lg = lax.broadcasted_iota(jnp.int32, (1, PWD), 1) // 64
    return lg, jnp.where(lg == 0, 2.0, jnp.where(lg == 1, 4.0, jnp.where(lg == 2, 8.0, 16.0))).astype(F32)


def _pool_mean_minus_token(u, up, row0):
    lg, wv = _pool_window_lanes()
    sums = []
    c, p = u, up
    for k in (1, 2, 4, 8):
        c2 = c + _rows_before(c, p, k)
        p = p + pltpu.roll(p, k, 0)
        c = c2
        sums.append(c)
    win = jnp.where(lg == 0, sums[0], jnp.where(lg == 1, sums[1], jnp.where(lg == 2, sums[2], sums[3])))
    pos1 = (row0 + lax.broadcasted_iota(jnp.int32, u.shape, 0) + 1).astype(F32)
    cnt = jnp.minimum(pos1, wv)
    return win / cnt - u, cnt


def _conv_taps(z, zp, w0, w1, w2):
    z1 = _rows_before(z, zp, 1)
    z2 = _rows_before(z, zp, 2)
    return (w0 * z2 + w1 * z1) + w2 * z, z1, z2


CP_TM = 1024
HALO = 16


def _halo_before(tm, col):
    return pl.BlockSpec((HALO, CW), lambda i: (jnp.maximum(i * (tm // HALO) - 1, 0), col))


def _halo_after(tm, col, rows):
    return pl.BlockSpec((HALO, CW), lambda i: (jnp.minimum((i + 1) * (tm // HALO), rows // HALO - 1), col))


def _as_block_end(halo, tm):
    return jnp.concatenate([jnp.zeros((tm - HALO, halo.shape[1]), halo.dtype), halo], axis=0)


def _as_block_start(halo, tm):
    return jnp.concatenate([halo, jnp.zeros((tm - HALO, halo.shape[1]), halo.dtype)], axis=0)


def _convpool_fwd(name, p, o, cw, pwbd, ps):
    s = p.shape[0]
    tm = CP_TM
    nb = s // tm

    def body(gb_ref, gc_ref, hin_ref, u_ref, gcp_ref, hinp_ref, up_ref, o_ref, cw_ref, pw_ref, ps_ref, mix_ref):
        i = pl.program_id(0)
        has_prev = i > 0
        z = gc_ref[...] * hin_ref[...]
        zp = _as_block_end(jnp.where(has_prev, gcp_ref[...] * hinp_ref[...], 0.0), tm)
        y3, _, _ = _conv_taps(z, zp, cw_ref[0:1, :], cw_ref[1:2, :], cw_ref[2:3, :])
        m, _ = _pool_mean_minus_token(u_ref[...], _as_block_end(jnp.where(has_prev, up_ref[...], 0.0), tm), i * tm)
        yp = jnp.dot(m.astype(BF16), pw_ref[...].astype(BF16), preferred_element_type=F32) * ps_ref[...]
        mix_ref[:, 0:AW] = o_ref[...]
        mix_ref[:, AW:AW + CW] = (gb_ref[...] * y3).astype(BF16)
        mix_ref[:, AW + CW:D] = yp.astype(BF16)

    def cur(col):
        return pl.BlockSpec((tm, CW), lambda i: (i, col))

    def whole(a):
        return pl.BlockSpec(a.shape, lambda i: (0,) * a.ndim)

    return pl.pallas_call(
        body, name=name, grid=(nb,),
        in_specs=[cur(6), cur(7), cur(8), cur(9), _halo_before(tm, 7), _halo_before(tm, 8), _halo_before(tm, 9),
                  pl.BlockSpec((tm, AW), lambda i: (i, 0)), whole(cw), whole(pwbd), whole(ps)],
        out_specs=pl.BlockSpec((tm, D), lambda i: (i, 0)),
        out_shape=jax.ShapeDtypeStruct((s, D), BF16),
        compiler_params=_cp("parallel"),
    )(p, p, p, p, p, p, p, o, cw, pwbd, ps)


def _convpool_bwd(name, p, dmix, cw, pwbd, ps):
    s = p.shape[0]
    tm = CP_TM // 2
    nb = s // tm

    def body(gb_ref, gc_ref, hin_ref, u_ref, gcp_ref, hinp_ref, up_ref, gbn_ref, dyc_ref, dyp_ref, dycn_ref, dypn_ref,
             cw_ref, pw_ref, ps_ref, dcp_ref, dw0_ref, dw1_ref, dw2_ref, dps_ref, dpw_ref):
        i = pl.program_id(0)
        has_prev = i > 0
        has_next = i < nb - 1
        w0, w1, w2 = cw_ref[0:1, :], cw_ref[1:2, :], cw_ref[2:3, :]
        gb, gc, hin = gb_ref[...], gc_ref[...], hin_ref[...]
        dyc = dyc_ref[...]
        z = gc * hin
        zp = _as_block_end(jnp.where(has_prev, gcp_ref[...] * hinp_ref[...], 0.0), tm)
        y3, z1, z2 = _conv_taps(z, zp, w0, w1, w2)
        dy3 = dyc * gb
        dy3n = _as_block_start(jnp.where(has_next, dycn_ref[...] * gbn_ref[...], 0.0), tm)
        dz = w2 * dy3 + w1 * _rows_after(dy3, dy3n, 1) + w0 * _rows_after(dy3, dy3n, 2)
        pw = pw_ref[...].astype(BF16)
        psv = ps_ref[...]
        m, cnt = _pool_mean_minus_token(u_ref[...], _as_block_end(jnp.where(has_prev, up_ref[...], 0.0), tm), i * tm)
        mb = m.astype(BF16)
        dyp = dyp_ref[...]
        dmp = (dyp * psv).astype(BF16)
        dmpn = jnp.where(has_next, dypn_ref[...] * psv, 0.0).astype(BF16)
        nt = (((1,), (1,)), ((), ()))
        dm = lax.dot_general(dmp, pw, nt, preferred_element_type=F32)
        dmn = lax.dot_general(dmpn, pw, nt, preferred_element_type=F32)
        lg, wv = _pool_window_lanes()
        cc, cn = dm / cnt, _as_block_start(dmn / wv, tm)
        sums = []
        for k in (1, 2, 4, 8):
            c2 = cc + _rows_after(cc, cn, k)
            cn = cn + pltpu.roll(cn, tm - k, 0)
            cc = c2
            sums.append(cc)
        du = jnp.where(lg == 0, sums[0], jnp.where(lg == 1, sums[1], jnp.where(lg == 2, sums[2], sums[3]))) - dm
        dcp_ref[:, 0:CW] = (dyc * y3).astype(BF16)
        dcp_ref[:, CW:2 * CW] = (dz * hin).astype(BF16)
        dcp_ref[:, 2 * CW:3 * CW] = (dz * gc).astype(BF16)
        dcp_ref[:, 3 * CW:4 * CW] = du.astype(BF16)
        parts = (jnp.sum(dy3 * z2, axis=0, keepdims=True),
                 jnp.sum(dy3 * z1, axis=0, keepdims=True),
                 jnp.sum(dy3 * z, axis=0, keepdims=True),
                 jnp.sum(dyp * jnp.dot(mb, pw, preferred_element_type=F32), axis=0, keepdims=True),
                 lax.dot_general(mb, dmp, (((0,), (0,)), ((), ())), preferred_element_type=F32))
        accs = (dw0_ref, dw1_ref, dw2_ref, dps_ref, dpw_ref)

        @pl.when(i == 0)
        def _():
            for a, v in zip(accs, parts):
                a[...] = v

        @pl.when(i > 0)
        def _():
            for a, v in zip(accs, parts):
                a[...] += v

    def cur(col):
        return pl.BlockSpec((tm, CW), lambda i: (i, col))

    def prev(col):
        return _halo_before(tm, col)

    def nxt(col):
        return _halo_after(tm, col, s)

    def whole(shape):
        return pl.BlockSpec(shape, lambda i: (0,) * len(shape))

    row = jax.ShapeDtypeStruct((1, CW), F32)
    return pl.pallas_call(
        body, name=name, grid=(nb,),
        in_specs=[cur(6), cur(7), cur(8), cur(9), prev(7), prev(8), prev(9), nxt(6),
                  cur(0), cur(1), nxt(0), nxt(1), whole(cw.shape), whole(pwbd.shape), whole(ps.shape)],
        out_specs=[pl.BlockSpec((tm, D), lambda i: (i, 0)), whole((1, CW)), whole((1, CW)), whole((1, CW)),
                   whole((1, PWD)), whole((PWD, PWD))],
        out_shape=[jax.ShapeDtypeStruct((s, D), BF16), row, row, row, row,
                   jax.ShapeDtypeStruct((PWD, PWD), F32)],
        compiler_params=_cp("arbitrary"),
    )(p, p, p, p, p, p, p, p, dmix, dmix, dmix, dmix, cw, pwbd, ps)


def _qkv_bwd(name, p, dq, dkp, dvp, dcp, qg, kg):
    s = p.shape[0]
    tm = 512
    off = PADR // tm

    def body(pq_ref, pk_ref, dq_ref, dk_ref, dv_ref, dcp_ref, qg_ref, kg_ref, dp_ref, dqg_ref, dkg_ref):
        i = pl.program_id(0)
        hm = _head_mean_matrix()

        def nrm_bwd(x, g, dy):
            r = lax.rsqrt(_head_mean(x * x, hm) + EPS)
            xn = x * r
            dxn = dy * g
            dx = r * (dxn - xn * _head_mean(dxn * xn, hm))
            dg = jnp.sum(dy * xn, axis=0, keepdims=True)
            dg = (dg[:, 0:128] + dg[:, 128:256]) + (dg[:, 256:384] + dg[:, 384:512])
            return dx, dg + pltpu.roll(dg, HD, 1)

        dxq, dgq = nrm_bwd(pq_ref[...], qg_ref[...], dq_ref[...] * 0.125)
        dxk, dgk = nrm_bwd(pk_ref[...], kg_ref[...], dk_ref[...])
        dp_ref[:, 0:AW] = dxq.astype(BF16)
        dp_ref[:, AW:2 * AW] = dxk.astype(BF16)
        dp_ref[:, 2 * AW:3 * AW] = dv_ref[...].astype(BF16)
        dp_ref[:, 3 * AW:DIN] = dcp_ref[...]

        @pl.when(i == 0)
        def _():
            dqg_ref[...] = dgq
            dkg_ref[...] = dgk

        @pl.when(i > 0)
        def _():
            dqg_ref[...] += dgq
            dkg_ref[...] += dgk

    gspec = pl.BlockSpec((1, AW), lambda i: (0, 0))
    gout = pl.BlockSpec((1, 128), lambda i: (0, 0))
    return pl.pallas_call(
        body, name=name, grid=(s // tm,),
        in_specs=[pl.BlockSpec((tm, AW), lambda i: (i, 0)), pl.BlockSpec((tm, AW), lambda i: (i, 1)),
                  pl.BlockSpec((tm, AW), lambda i: (i, 0)),
                  pl.BlockSpec((tm, AW), lambda i: (i + off, 0)),
                  pl.BlockSpec((tm, AW), lambda i: (i + off, 0)),
                  pl.BlockSpec((tm, D), lambda i: (i, 0)), gspec, gspec],
        out_specs=[pl.BlockSpec((tm, DIN), lambda i: (i, 0)), gout, gout],
        out_shape=[jax.ShapeDtypeStruct((s, DIN), BF16), jax.ShapeDtypeStruct((1, 128), F32),
                   jax.ShapeDtypeStruct((1, 128), F32)],
        compiler_params=_cp("arbitrary"),
    )(p, p, dq, dkp, dvp, dcp, qg, kg)


def _mm_nt_relu(name, dxb, w, l, a):
    s = dxb.shape[0]
    tm = MM_ROWS

    def body(d_ref, w_ref, a_ref, o_ref):
        df = lax.dot_general(d_ref[...], w_ref[...], NT_DIMS, preferred_element_type=F32)
        o_ref[...] = (df * (2.0 * jnp.maximum(a_ref[...].astype(F32), 0.0))).astype(BF16)

    return pl.pallas_call(
        body, name=name, grid=(s // tm,),
        in_specs=[pl.BlockSpec((tm, D), lambda i: (i, 0)),
                  pl.BlockSpec((None, DFF, D), lambda i: (l, 0, 0)),
                  pl.BlockSpec((tm, DFF), lambda i: (i, 0))],
        out_specs=pl.BlockSpec((tm, DFF), lambda i: (i, 0)),
        out_shape=jax.ShapeDtypeStruct((s, DFF), BF16),
        compiler_params=_cp("parallel"),
    )(dxb, w, a)


def _proj_out_bwd(name, dxb, w, l, mix):
    s = dxb.shape[0]
    tm = _mm_rows(D, D)

    def body(d_ref, w_ref, o_ref, do_ref, dot_ref, dcp_ref, dl_ref):
        d = d_ref[...]
        wa, wc = w_ref[0:AW, :], w_ref[AW:D, :]
        do = lax.dot_general(d, wa, NT_DIMS, preferred_element_type=F32)
        do_ref[...] = do.astype(BF16)
        dot_ref[...] = lax.dot_general(wa, d, NT_DIMS, preferred_element_type=F32).astype(BF16)
        dcp_ref[...] = lax.dot_general(d, wc, NT_DIMS, preferred_element_type=F32)
        head = lax.broadcasted_iota(jnp.int32, (AW, 128), 0) // HD
        pick = jnp.where(head == lax.broadcasted_iota(jnp.int32, (AW, 128), 1), 1.0, 0.0).astype(BF16)
        dl_ref[...] = _two_pass_dot(do * o_ref[...].astype(F32), pick)

    return pl.pallas_call(
        body, name=name, grid=(s // tm,),
        in_specs=[pl.BlockSpec((tm, D), lambda i: (i, 0)),
                  pl.BlockSpec((None, D, D), lambda i: (l, 0, 0)),
                  pl.BlockSpec((tm, AW), lambda i: (i, 0))],
        out_specs=[pl.BlockSpec((tm, AW), lambda i: (i, 0)), pl.BlockSpec((AW, tm), lambda i: (0, i)),
                   pl.BlockSpec((tm, D - AW), lambda i: (i, 0)), pl.BlockSpec((tm, 128), lambda i: (i, 0))],
        out_shape=[jax.ShapeDtypeStruct((s, AW), BF16), jax.ShapeDtypeStruct((AW, s), BF16),
                   jax.ShapeDtypeStruct((s, D - AW), F32), jax.ShapeDtypeStruct((s, 128), F32)],
        compiler_params=_cp("parallel"),
    )(dxb, w, mix)


def _mm_nt_normbwd(name, gy, w, l, x, g, dres, dep):
    s, k = gy.shape
    tm = MM_ROWS

    def body(gy_ref, w_ref, x_ref, g_ref, dr_ref, dep_ref, dx_ref, dxb_ref, dg_ref):
        del dep_ref
        i = pl.program_id(0)
        dh = lax.dot_general(gy_ref[...], w_ref[...], NT_DIMS, preferred_element_type=F32)
        xv = x_ref[...]
        r = _inv_rms(xv)
        xn = xv * r
        dxn = dh * g_ref[...]
        dx = r * (dxn - xn * jnp.mean(dxn * xn, axis=-1, keepdims=True)) + dr_ref[...]
        dx_ref[...] = dx
        dxb_ref[...] = dx.astype(BF16)
        part = jnp.sum(dh * xn, axis=0, keepdims=True)

        @pl.when(i == 0)
        def _():
            dg_ref[...] = part

        @pl.when(i > 0)
        def _():
            dg_ref[...] += part

    blk = pl.BlockSpec((tm, D), lambda i: (i, 0))
    vec = pl.BlockSpec((1, D), lambda i: (0, 0))
    return pl.pallas_call(
        body, name=name, grid=(s // tm,),
        in_specs=[pl.BlockSpec((tm, k), lambda i: (i, 0)),
                  pl.BlockSpec((None, D, k), lambda i: (l, 0, 0)), blk, vec, blk, ANY],
        out_specs=[blk, blk, vec],
        out_shape=[jax.ShapeDtypeStruct((s, D), F32), jax.ShapeDtypeStruct((s, D), BF16),
                   jax.ShapeDtypeStruct((1, D), F32)],
        compiler_params=_cp("arbitrary"),
    )(gy, w, x, g, dres, dep)


def _mm_tn(name, a, b, tma, tnb, relu2=False):
    s, m = a.shape
    n = b.shape[1]

    def body(a_ref, b_ref, o_ref):
        av = _relu2(a_ref[...]) if relu2 else a_ref[...]
        o_ref[...] = lax.dot_general(av, b_ref[...], (((0,), (0,)), ((), ())),
                                     preferred_element_type=F32).astype(BF16)

    return pl.pallas_call(
        body, name=name, grid=(m // tma, n // tnb),
        in_specs=[pl.BlockSpec((s, tma), lambda i, j: (0, i), pipeline_mode=pl.Buffered(1) if m == tma else None),
                  pl.BlockSpec((s, tnb), lambda i, j: (0, j))],
        out_specs=pl.BlockSpec((tma, tnb), lambda i, j: (i, j)),
        out_shape=jax.ShapeDtypeStruct((m, n), BF16),
        compiler_params=_cp("parallel", "parallel"),
    )(a, b)


def _adamw_math(gv, wv, mv, vv):
    mn = ADAM_B1 * mv + (1.0 - ADAM_B1) * gv
    vn = ADAM_B2 * vv + (1.0 - ADAM_B2) * jnp.square(gv)
    m_hat = mn / (1.0 - ADAM_B1 ** ADAM_STEP)
    v_hat = vn / (1.0 - ADAM_B2 ** ADAM_STEP)
    return gv, -ADAM_LR * (m_hat / (jnp.sqrt(v_hat) + ADAM_EPS) + ADAM_WD * wv), mn, vn


def _adamw(name, g, w, m, v):
    r, c = g.shape
    tm = 256 if r % 256 == 0 else r

    def body(g_ref, w_ref, m_ref, v_ref, go_ref, d_ref, mo_ref, vo_ref):
        go_ref[...], d_ref[...], mo_ref[...], vo_ref[...] = _adamw_math(g_ref[...], w_ref[...], m_ref[...], v_ref[...])

    blk = pl.BlockSpec((tm, c), lambda i: (i, 0))
    return pl.pallas_call(
        body, name=name, grid=(r // tm,),
        in_specs=[blk] * 4, out_specs=[blk] * 4,
        out_shape=[jax.ShapeDtypeStruct((r, c), F32)] * 4,
        compiler_params=_cp("parallel"),
    )(g, w, m, v)


def _place():
    x, y, c = lax.axis_index("x"), lax.axis_index("y"), lax.axis_index("c")
    chips = [(1 - x, y), (x, 1 - y), (1 - x, 1 - y)]
    return x, y, c, chips


BLOCK_AXIS = (2, 1, 2, 1)
LARGE_DIMS = ((D, DIN), (D, D), (D, DFF), (DFF, D))


def _full_shape(t, layers, dtype):
    r, c = LARGE_DIMS[t]
    return jax.ShapeDtypeStruct((layers, r, c), dtype)


def _cast_into_full(name, t, shard, b1, dep):
    _, r, c = shard.shape
    tm = min(512, r)
    if BLOCK_AXIS[t] == 1:
        out_spec = pl.BlockSpec((None, tm, c), lambda l, i, br: (l, br[0] * (r // tm) + i, 0))
    else:
        out_spec = pl.BlockSpec((None, tm, c), lambda l, i, br: (l, i, br[0]))

    def body(b_ref, x_ref, dep_ref, o_ref):
        del b_ref, dep_ref
        o_ref[...] = x_ref[...].astype(BF16)

    return pl.pallas_call(
        body, name=name,
        grid_spec=pltpu.PrefetchScalarGridSpec(
            num_scalar_prefetch=1, grid=(DEPTH, r // tm),
            in_specs=[pl.BlockSpec((None, tm, c), lambda l, i, br: (l, i, 0)), ANY],
            out_specs=out_spec),
        out_shape=_full_shape(t, DEPTH, BF16),
        compiler_params=_cp("parallel", "parallel"),
    )(b1, shard, dep)


HBM = pl.BlockSpec(memory_space=pltpu.HBM)
SEM = pl.BlockSpec(memory_space=pltpu.SEMAPHORE)
DATAFLOW = pltpu.SideEffectType.DATAFLOW_SIDE_EFFECTING


def _half(ref, l, t, b, c):
    r, cols = LARGE_DIMS[t]
    if BLOCK_AXIS[t] == 1:
        n = r // 8
        return ref.at[l, pl.ds(pl.multiple_of(b * (2 * n) + c * n, 16), n), :]
    n, w = r // 2, cols // 4
    return ref.at[l, pl.ds(pl.multiple_of(c * n, 16), n), pl.ds(pl.multiple_of(b * w, 128), w)]


def _gather_start(name, layers, ts, fulls, both=False):
    n = len(ts)

    def body(*refs):
        f_refs, sems = refs[n:2 * n], refs[2 * n:2 * n + 2 * len(layers)]
        x, y, c, chips = _place()
        for i, l in enumerate(layers):
            for k, t in enumerate(ts):
                own = _half(f_refs[k], l, t, 2 * x + y, c)
                for j, (cx, cy) in enumerate(chips):
                    pltpu.make_async_remote_copy(src_ref=own, dst_ref=own, send_sem=sems[2 * i].at[3 * t + j],
                                                 recv_sem=sems[2 * i + 1].at[3 * t + j], device_id=(cx, cy, c),
                                                 device_id_type=MESH).start()
                    if both:
                        pltpu.make_async_remote_copy(src_ref=own, dst_ref=own,
                                                     send_sem=sems[2 * i].at[12 + 3 * t + j],
                                                     recv_sem=sems[2 * i + 1].at[12 + 3 * t + j],
                                                     device_id=(cx, cy, 1 - c), device_id_type=MESH).start()
        refs[-1][...] = jnp.zeros((8, 128), F32)

    outs = pl.pallas_call(
        body, name=name,
        in_specs=[HBM] * n,
        out_specs=[HBM] * n + [SEM] * (2 * len(layers)) + [pl.BlockSpec(memory_space=pltpu.VMEM)],
        out_shape=[pltpu.HBM(f.shape, f.dtype) for f in fulls]
        + [pltpu.SemaphoreType.DMA((24,))] * (2 * len(layers)) + [jax.ShapeDtypeStruct((8, 128), F32)],
        input_output_aliases={k: k for k in range(n)},
        compiler_params=pltpu.CompilerParams(has_side_effects=DATAFLOW),
    )(*[pltpu.with_memory_space_constraint(f, pltpu.HBM) for f in fulls])
    return outs[0:n], {l: (outs[n + 2 * i], outs[n + 1 + 2 * i]) for i, l in enumerate(layers)}, outs[-1]


def _gather_wait(name, l, ts, fulls, sems, after, both=False):
    def body(*refs):
        send_sems, recv_sems, f_refs = refs[4], refs[5], refs[7:11]
        x, y, c, chips = _place()
        for t in ts:
            own = _half(f_refs[t], l, t, 2 * x + y, c)
            for j, (cx, cy) in enumerate(chips):
                landed = _half(f_refs[t], l, t, 2 * cx + cy, c)
                pltpu.make_async_remote_copy(src_ref=own, dst_ref=landed, send_sem=send_sems.at[3 * t + j],
                                             recv_sem=recv_sems.at[3 * t + j], device_id=(cx, cy, c),
                                             device_id_type=MESH).wait()
                if both:
                    crossed = _half(f_refs[t], l, t, 2 * cx + cy, 1 - c)
                    pltpu.make_async_remote_copy(src_ref=own, dst_ref=crossed, send_sem=send_sems.at[12 + 3 * t + j],
                                                 recv_sem=recv_sems.at[12 + 3 * t + j], device_id=(cx, cy, 1 - c),
                                                 device_id_type=MESH).wait()

    return pl.pallas_call(
        body, name=name,
        in_specs=[HBM] * 4 + [SEM, SEM, ANY], out_specs=[HBM] * 4,
        out_shape=[pltpu.HBM(s.shape, s.dtype) for s in (_full_shape(t, DEPTH, BF16) for t in range(4))],
        input_output_aliases={t: t for t in range(4)},
        compiler_params=pltpu.CompilerParams(has_side_effects=DATAFLOW),
    )(*fulls, sems[0], sems[1], after)


def _pass_on(name, l, ts, fulls):
    def body(*refs):
        f_refs, send_sems, recv_sems = refs[4:8], refs[8], refs[9]
        x, y, c, chips = _place()

        def copy(t, j, half):
            cx, cy = chips[j]
            part = _half(f_refs[t], l, t, 2 * cx + cy, half)
            return pltpu.make_async_remote_copy(src_ref=part, dst_ref=part, send_sem=send_sems.at[3 * t + j],
                                                recv_sem=recv_sems.at[3 * t + j], device_id=(x, y, 1 - c),
                                                device_id_type=MESH)

        for t in ts:
            for j in range(3):
                copy(t, j, c).start()
        for t in ts:
            for j in range(3):
                copy(t, j, 1 - c).wait_recv()
                copy(t, j, c).wait_send()

    return pl.pallas_call(
        body, name=name,
        in_specs=[ANY] * 4, out_specs=[ANY] * 4,
        out_shape=[_full_shape(t, DEPTH, BF16) for t in range(4)],
        input_output_aliases={t: t for t in range(4)},
        scratch_shapes=[pltpu.SemaphoreType.DMA((12,)), pltpu.SemaphoreType.DMA((12,))],
    )(*fulls)


def _block2d(ref, t, b):
    r, cols = LARGE_DIMS[t]
    if BLOCK_AXIS[t] == 1:
        return ref.at[pl.ds(pl.multiple_of(b * (r // 4), 16), r // 4), :]
    return ref.at[:, pl.ds(pl.multiple_of(b * (cols // 4), 128), cols // 4)]


def _block_dims(t):
    r, cols = LARGE_DIMS[t]
    return (r // 4, cols) if BLOCK_AXIS[t] == 1 else (r, cols // 4)


def _reduce_copies(ts, g_refs, r_refs, send_sems, recv_sems):
    _, _, c, chips = _place()
    return [pltpu.make_async_remote_copy(src_ref=_block2d(g_refs[i], t, 2 * cx + cy), dst_ref=r_refs[i].at[j],
                                         send_sem=send_sems.at[3 * i + j], recv_sem=recv_sems.at[3 * i + j],
                                         device_id=(cx, cy, c), device_id_type=MESH)
            for i, t in enumerate(ts) for j, (cx, cy) in enumerate(chips)]


def _reduce_start(name, ts, grads):
    n = len(ts)

    def body(*refs):
        for cp in _reduce_copies(ts, refs[n:2 * n], refs[2 * n:3 * n], refs[3 * n], refs[3 * n + 1]):
            cp.start()
        refs[3 * n + 2][...] = jnp.zeros((8, 128), F32)

    outs = pl.pallas_call(
        body, name=name,
        in_specs=[HBM] * n,
        out_specs=[HBM] * (2 * n) + [SEM, SEM, pl.BlockSpec(memory_space=pltpu.VMEM)],
        out_shape=[pltpu.HBM(g.shape, BF16) for g in grads]
        + [pltpu.HBM((3,) + _block_dims(t), BF16) for t in ts]
        + [pltpu.SemaphoreType.DMA((3 * n,)), pltpu.SemaphoreType.DMA((3 * n,)), jax.ShapeDtypeStruct((8, 128), F32)],
        input_output_aliases={i: i for i in range(n)},
        compiler_params=pltpu.CompilerParams(has_side_effects=DATAFLOW),
    )(*[pltpu.with_memory_space_constraint(g, pltpu.HBM) for g in grads])
    return outs[0:n], outs[n:2 * n], (outs[2 * n], outs[2 * n + 1]), outs[2 * n + 2]


def _reduce_wait(name, ts, grads, landing, sems, afters):
    n = len(ts)
    first_out = 2 * n + 2 + len(afters)

    def body(*refs):
        for cp in _reduce_copies(ts, refs[first_out:first_out + n], refs[first_out + n:first_out + 2 * n],
                                 refs[2 * n], refs[2 * n + 1]):
            cp.wait()

    outs = pl.pallas_call(
        body, name=name,
        in_specs=[HBM] * (2 * n) + [SEM, SEM] + [ANY] * len(afters), out_specs=[HBM] * (2 * n),
        out_shape=[pltpu.HBM(g.shape, BF16) for g in grads] + [pltpu.HBM(r.shape, BF16) for r in landing],
        input_output_aliases={i: i for i in range(2 * n)},
        compiler_params=pltpu.CompilerParams(has_side_effects=DATAFLOW),
    )(*grads, *landing, sems[0], sems[1], *afters)
    return outs[0:n], outs[n:2 * n]


def _add4(name, t, own, landed, b1):
    rb, cb = _block_dims(t)
    tm = min(512, rb)
    if BLOCK_AXIS[t] == 1:
        own_spec = pl.BlockSpec((tm, cb), lambda i, br: (br[0] * (rb // tm) + i, 0))
    else:
        own_spec = pl.BlockSpec((tm, cb), lambda i, br: (i, br[0]))

    def body(b_ref, o_ref, r0_ref, r1_ref, r2_ref, s_ref):
        del b_ref
        s_ref[...] = ((o_ref[...].astype(F32) + r0_ref[...].astype(F32))
                      + (r1_ref[...].astype(F32) + r2_ref[...].astype(F32))).astype(BF16)

    def got(j):
        return pl.BlockSpec((None, tm, cb), lambda i, br: (j, i, 0))

    return pl.pallas_call(
        body, name=name,
        grid_spec=pltpu.PrefetchScalarGridSpec(
            num_scalar_prefetch=1, grid=(rb // tm,),
            in_specs=[own_spec, got(0), got(1), got(2)],
            out_specs=pl.BlockSpec((tm, cb), lambda i, br: (i, 0))),
        out_shape=jax.ShapeDtypeStruct((rb, cb), BF16),
        compiler_params=_cp("parallel"),
    )(b1, own, landed, landed, landed)


def _swap_sib(name, sums):
    def body(*refs):
        s_refs, t_refs, send_sems, recv_sems = refs[0:4], refs[4:8], refs[8], refs[9]
        x, y, c, _ = _place()
        cps = [pltpu.make_async_remote_copy(src_ref=s_refs[t], dst_ref=t_refs[t], send_sem=send_sems.at[t],
                                            recv_sem=recv_sems.at[t], device_id=(x, y, 1 - c), device_id_type=MESH)
               for t in range(4)]
        for cp in cps:
            cp.start()
        for cp in cps:
            cp.wait()

    return pl.pallas_call(
        body, name=name,
        in_specs=[ANY] * 4, out_specs=[ANY] * 4,
        out_shape=[jax.ShapeDtypeStruct(s.shape, BF16) for s in sums],
        scratch_shapes=[pltpu.SemaphoreType.DMA((4,)), pltpu.SemaphoreType.DMA((4,))],
    )(*sums)


def _adamw_pair(name, l, s_own, s_sib, w, m, v, outs):
    rb, cb = s_own.shape
    tm = min(512, rb)

    def body(a_ref, b_ref, w_ref, m_ref, v_ref, g0, d0, m0, v0, go_ref, d_ref, mo_ref, vo_ref):
        del g0, d0, m0, v0
        gv = a_ref[...].astype(F32) + b_ref[...].astype(F32)
        go_ref[...], d_ref[...], mo_ref[...], vo_ref[...] = _adamw_math(gv, w_ref[...], m_ref[...], v_ref[...])

    part = pl.BlockSpec((tm, cb), lambda i: (i, 0))
    layer = pl.BlockSpec((None, tm, cb), lambda i: (l, i, 0))
    return pl.pallas_call(
        body, name=name, grid=(rb // tm,),
        in_specs=[part, part, layer, layer, layer] + [ANY] * 4,
        out_specs=[layer] * 4,
        out_shape=[jax.ShapeDtypeStruct((DEPTH, rb, cb), F32)] * 4,
        input_output_aliases={5 + i: i for i in range(4)},
        compiler_params=_cp("parallel"),
    )(s_own, s_sib, w, m, v, *outs)


def _all_gather8(name, v, dep):
    m_per, n = v.shape

    def body(v_ref, dep_ref, out_ref, send_sems, recv_sems, local_sem):
        del dep_ref
        x, y, c, chips = _place()
        me, sib = (x, y, c), (x, y, 1 - c)

        def rows(px, py, pc):
            return out_ref.at[pl.ds((4 * px + 2 * py + pc) * m_per, m_per), :]

        def copy(k, block, to, src=None):
            return pltpu.make_async_remote_copy(
                src_ref=rows(*block) if src is None else src, dst_ref=rows(*block),
                send_sem=send_sems.at[k], recv_sem=recv_sems.at[k], device_id=to, device_id_type=MESH)

        mine = pltpu.make_async_copy(v_ref, rows(*me), local_sem)
        mine.start()
        first = [copy(0, me, sib, src=v_ref)]
        first += [copy(1 + j, me, (*chip, c), src=v_ref) for j, chip in enumerate(chips)]
        for cp in first:
            cp.start()
        passed = [copy(4 + j, (*chip, c), sib) for j, chip in enumerate(chips)]
        for j, chip in enumerate(chips):
            copy(1 + j, (*chip, c), me).wait_recv()
            passed[j].start()
        copy(0, sib, me).wait_recv()
        for j, chip in enumerate(chips):
            copy(4 + j, (*chip, 1 - c), me).wait_recv()
        for cp in first + passed:
            cp.wait_send()
        mine.wait()

    return pl.pallas_call(
        body, name=name,
        out_shape=jax.ShapeDtypeStruct((8 * m_per, n), v.dtype),
        in_specs=[pl.BlockSpec(memory_space=pltpu.VMEM), ANY],
        out_specs=pl.BlockSpec(memory_space=pltpu.VMEM),
        scratch_shapes=[pltpu.SemaphoreType.DMA((7,)), pltpu.SemaphoreType.DMA((7,)), pltpu.SemaphoreType.DMA],
    )(v, dep)


def _sum8(name, g):
    def body(g_ref, o_ref):
        acc = g_ref[0]
        for d in range(1, 8):
            acc = acc + g_ref[d]
        o_ref[...] = acc

    return pl.pallas_call(body, name=name, out_shape=jax.ShapeDtypeStruct(g.shape[1:], F32))(g)


def _pack(parts):
    flat = []
    for a in parts:
        a = a.reshape(-1)
        flat.append(jnp.pad(a, (0, (-a.shape[0]) % 128)))
    cat = jnp.concatenate(flat)
    cat = jnp.pad(cat, (0, (-cat.shape[0]) % 1024))
    return cat.reshape(-1, 128)


def _unpack(packed, shapes):
    flat = packed.reshape(-1)
    out, at = [], 0
    for shp in shapes:
        n = 1
        for d in shp:
            n *= d
        out.append(flat[at:at + n].reshape(shp))
        at += n + (-n) % 128
    return out


def _local_step(x, target, layer_weights, on_grads, small):
    qg_all = jnp.tile(small["q_norm_g"], (1, 8))
    kg_all = jnp.tile(small["k_norm_g"], (1, 8))
    bias_all = _bias_layout(_bias_expand("bias_expand", jnp.pad(small["rel_bias"], ((0, 0), (0, 0), (0, NIDX - 257)))))
    same_group = jnp.eye(4, dtype=F32)[None, :, None, :, None]
    pwbd_all = (small["pool_w"][:, :, :, None, :] * same_group).reshape(DEPTH, PWD, PWD)
    saved = []
    xin = x
    h = _rmsnorm("norm_first", x, small["norm1_g"][0:1])
    for l in range(DEPTH):
        w_in = layer_weights(l, (0,), xin)[0]
        qg, kg = qg_all[l:l + 1], kg_all[l:l + 1]
        cw, pwbd, ps = small["conv_w"][l], pwbd_all[l], small["pool_scale"][l:l + 1]
        p = _mm_nn(f"proj_in_{l}", h, w_in, l, F32)
        q, qt, kp, kt, vp, vt = _qkv(f"qkv_{l}", p, qg, kg)
        o, lse = _attn_fwd(f"attn_fwd_{l}", kp, qt, vt, bias_all, l)
        w_in, w_out, w_1, w_2 = layer_weights(l, (1, 2, 3), o)
        mix = _convpool_fwd(f"convpool_fwd_{l}", p, o, cw, pwbd, ps)
        x1, h2 = _mm_res_norm(f"proj_out_{l}", mix, w_out, l, xin, small["norm2_g"][l:l + 1])
        saved.append(dict(xin=xin, h=h, p=p, q=q, qt=qt, kp=kp, kt=kt, vp=vp, mix=mix, x1=x1, h2=h2, lse=lse,
                          qg=qg, kg=kg, cw=cw, pwbd=pwbd, ps=ps))
        if l + 1 < DEPTH:
            saved[l]["a"], xin, h = _mlp_fwd(f"mlp_{l}", h2, w_1, w_2, l, x1, small["norm1_g"][l + 1:l + 2])
        else:
            saved[l]["a"], dx, dxb, loss = _mlp_fwd(f"mlp_{l}", h2, w_1, w_2, l, x1, target)

    raw = {k: [None] * DEPTH for k in ("dg1", "dqg", "dkg", "dw0", "dw1", "dw2", "dpw", "dps", "dg2")}
    db_all = lax.empty((DEPTH, 4, KB, 128), F32)
    for l in reversed(range(DEPTH)):
        sv = saved[l]
        da = _mm_nt_relu(f"mlp2_bwd_{l}", dxb, w_2, l, sv["a"])
        g_2 = _mm_tn(f"mlp2_wgrad_{l}", sv["a"], dxb, 512, 1024, relu2=True)
        g_1 = _mm_tn(f"mlp1_wgrad_{l}", sv["h2"], da, 1024, 512)
        dep = on_grads(l, (2, 3), (g_1, g_2))
        dx1, dx1b, dg2 = _mm_nt_normbwd(f"mlp1_bwd_{l}", da, w_1, l, sv["x1"], small["norm2_g"][l:l + 1], dx, dep)
        do, dot, dmix, dl = _proj_out_bwd(f"proj_out_bwd_{l}", dx1b, w_out, l, sv["mix"])
        g_out = _mm_tn(f"proj_out_wgrad_{l}", sv["mix"], dx1b, 512, 1024)
        dcp, dw0, dw1, dw2, dps, dpw = _convpool_bwd(f"convpool_bwd_{l}", sv["p"], dmix, sv["cw"], sv["pwbd"], sv["ps"])
        dq, dkp, dvp, db_all = _attn_bwd(f"attn_bwd_{l}", sv["q"], sv["qt"], sv["kp"], sv["kt"], sv["vp"], bias_all, l,
                                     do, dot, sv["lse"], _rowsum_layout(dl, x.shape[0] // UNIT), db_all)
        dp, dqg, dkg = _qkv_bwd(f"qkv_bwd_{l}", sv["p"], dq, dkp, dvp, dcp, sv["qg"], sv["kg"])
        g_in = _mm_tn(f"proj_in_wgrad_{l}", sv["h"], dp, 1024, 1280)
        dep = on_grads(l, (0, 1), (g_in, g_out))
        dx, dxb, dg1 = _mm_nt_normbwd(f"proj_in_bwd_{l}", dp, w_in, l, sv["xin"], small["norm1_g"][l:l + 1], dx1, dep)
        for k, val in dict(dg1=dg1, dqg=dqg, dkg=dkg, dw0=dw0, dw1=dw1, dw2=dw2, dpw=dpw, dps=dps, dg2=dg2).items():
            raw[k][l] = val
    cat = {k: jnp.concatenate(v, axis=0) for k, v in raw.items() if k != "dpw"}
    drb = _bias_reduce("bias_reduce", _bias_unlayout(db_all))
    dpw = jnp.stack(raw["dpw"])
    gsmall = {
        "norm1_g": cat["dg1"], "q_norm_g": cat["dqg"][:, :HD], "k_norm_g": cat["dkg"][:, :HD],
        "rel_bias": drb[:, :, :257],
        "conv_w": jnp.stack([cat["dw0"], cat["dw1"], cat["dw2"]], axis=1),
        "pool_w": jnp.stack([dpw[:, g * 64:(g + 1) * 64, g * 64:(g + 1) * 64] for g in range(4)], axis=1),
        "pool_scale": cat["dps"], "norm2_g": cat["dg2"],
    }
    return loss, dx, gsmall


SMALL = ("norm1_g", "q_norm_g", "k_norm_g", "rel_bias", "conv_w", "pool_w", "pool_scale", "norm2_g")
LARGE = ("w_in", "w_out", "w_mlp1", "w_mlp2")


def kernel(x, norm1_g, w_in, q_norm_g, k_norm_g, rel_bias, conv_w, pool_w, pool_scale, w_out, norm2_g, w_mlp1, w_mlp2, loss_target, m_norm1_g, m_w_in, m_q_norm_g, m_k_norm_g, m_rel_bias, m_conv_w, m_pool_w, m_pool_scale, m_w_out, m_norm2_g, m_w_mlp1, m_w_mlp2, v_norm1_g, v_w_in, v_q_norm_g, v_k_norm_g, v_rel_bias, v_conv_w, v_pool_w, v_pool_scale, v_w_out, v_norm2_g, v_w_mlp1, v_w_mlp2):
    w = dict(norm1_g=norm1_g, w_in=w_in, q_norm_g=q_norm_g, k_norm_g=k_norm_g, rel_bias=rel_bias, conv_w=conv_w,
             pool_w=pool_w, pool_scale=pool_scale, w_out=w_out, norm2_g=norm2_g, w_mlp1=w_mlp1, w_mlp2=w_mlp2)
    m = dict(norm1_g=m_norm1_g, w_in=m_w_in, q_norm_g=m_q_norm_g, k_norm_g=m_k_norm_g, rel_bias=m_rel_bias,
             conv_w=m_conv_w, pool_w=m_pool_w, pool_scale=m_pool_scale, w_out=m_w_out, norm2_g=m_norm2_g,
             w_mlp1=m_w_mlp1, w_mlp2=m_w_mlp2)
    v = dict(norm1_g=v_norm1_g, w_in=v_w_in, q_norm_g=v_q_norm_g, k_norm_g=v_k_norm_g, rel_bias=v_rel_bias,
             conv_w=v_conv_w, pool_w=v_pool_w, pool_scale=v_pool_scale, w_out=v_w_out, norm2_g=v_norm2_g,
             w_mlp1=v_w_mlp1, w_mlp2=v_w_mlp2)
    ax, ay, ac = lax.axis_index("x"), lax.axis_index("y"), lax.axis_index("c")
    b1 = jnp.reshape(2 * ax + ay, (1,)).astype(jnp.int32)

    cw_rows = _all_gather8("gather_conv_w", jnp.pad(conv_w.reshape(DEPTH * 3, 64), ((0, 4), (0, 64))), b1)
    cw_chips = [cw_rows[(4 * cx + 2 * cy) * 16:(4 * cx + 2 * cy) * 16 + 12, :64] for cx in range(2) for cy in range(2)]
    small = {n: w[n] for n in SMALL}
    small["conv_w"] = jnp.concatenate(cw_chips, axis=1).reshape(DEPTH, 3, CW)

    (w_in_full,), in_sems, in_token = _gather_start(
        "gather_start_in", (0,), (0,), [_cast_into_full("cast_w_in", 0, w["w_in"], b1, cw_rows)])
    others, first_sems, first_token = _gather_start(
        "gather_start_first", (0,), (1, 2, 3),
        [_cast_into_full(f"cast_{LARGE[t]}", t, w[LARGE[t]], b1, in_token) for t in (1, 2, 3)])
    held = [[w_in_full] + list(others)]
    sems = {(0, 0): in_sems[0], (0, 1): first_sems[0]}

    def layer_weights(l, ts, after):
        tag = f"{l}_{ts[0]}"
        first_in = l == 0 and ts == (0,)
        after = first_token if first_in else after
        arrived = _gather_wait(f"gather_wait_{tag}", l, ts, held[0], sems[l, ts[0] if l == 0 else 0], after,
                               both=l > 0)
        if first_in:
            arrived, rest_sems, _ = _gather_start("gather_start_rest", tuple(range(1, DEPTH)), (0, 1, 2, 3),
                                                  arrived, both=True)
            sems.update({(k, 0): v for k, v in rest_sems.items()})
        held[0] = _pass_on(f"pass_on_{tag}", l, ts, arrived) if l == 0 else arrived
        return held[0]

    flights = {}

    def await_flight(l, ts, afters):
        g, landing, sm, _ = flights[l, ts]
        flights[l, ts] = _reduce_wait(f"reduce_wait_{l}_{ts[0]}", ts, g, landing, sm, afters)

    def on_grads(l, ts, grads):
        if ts == (0, 1) and l + 1 < DEPTH:
            await_flight(l + 1, (2, 3), [grads[0]])
            await_flight(l + 1, (0, 1), [grads[0]])
        flights[l, ts] = _reduce_start(f"reduce_start_{l}_{ts[0]}", ts, grads)
        return flights[l, ts][3]

    loss_part, grad_x, gsmall = _local_step(x[0], loss_target[0], layer_weights, on_grads, small)
    loss = lax.psum(loss_part[0, 0], ("x", "y", "c"))
    order = [n for n in SMALL]
    packed = _pack([gsmall[n] for n in order])

    out = {n: [lax.empty(w[n].shape, F32) for _ in range(4)] for n in LARGE}
    for l in reversed(range(DEPTH)):
        if l == 0:
            afters = [grad_x, packed] + [out[n][0] for n in LARGE]
            await_flight(0, (2, 3), afters)
            await_flight(0, (0, 1), afters)
        sums = [None] * 4
        for ts in ((0, 1), (2, 3)):
            g, landing = flights[l, ts]
            for i, t in enumerate(ts):
                sums[t] = _add4(f"add4_{LARGE[t]}_{l}", t, g[i], landing[i], b1)
        theirs = _swap_sib(f"swap_sib_{l}", sums)
        for t, n in enumerate(LARGE):
            out[n] = _adamw_pair(f"adamw_{n}_{l}", l, sums[t], theirs[t], w[n], m[n], v[n], out[n])

    rows = packed.shape[0]
    summed = _sum8("sum_small", _all_gather8("gather_small", packed, out[LARGE[0]][0]).reshape(8, rows, 128))
    gfull = dict(zip(order, _unpack(summed, [gsmall[n].shape for n in order])))
    gfull["conv_w"] = lax.dynamic_slice_in_dim(gfull["conv_w"], (2 * ax + ay) * 64, 64, axis=2)
    res = _adamw("adamw_small", _pack([gfull[n] for n in order]), _pack([w[n] for n in order]),
                 _pack([m[n] for n in order]), _pack([v[n] for n in order]))
    for n, parts in zip(order, zip(*[_unpack(r, [w[k].shape for k in order]) for r in res])):
        out[n] = list(parts)

    names = ("norm1_g", "w_in", "q_norm_g", "k_norm_g", "rel_bias", "conv_w", "pool_w", "pool_scale", "w_out",
             "norm2_g", "w_mlp1", "w_mlp2")
    flat = [loss, grad_x[None]]
    for i in range(4):
        flat += [out[n][i] for n in names]
    return tuple(flat)
```
